```python
import jax, jax.numpy as jnp
from jax import lax
import numpy as np

D_MODEL = 1024
BATCH = 8
SEQ = 4096
DEPTH = 1

GRID_W = 64
ATTN_HEADS = 8
ATTN_KV_HEADS = 2
ATTN_HEAD_DIM = 64
RET_HEADS = 4
RET_KEY_DIM = 64
RET_VALUE_DIM = 128
Q_BLOCK = 128
RET_CHUNK = 128
ROPE_THETA = 10000.0
EPS = 1e-6

ATTN_WIDTH = ATTN_HEADS * ATTN_HEAD_DIM
ATTN_KV_WIDTH = ATTN_KV_HEADS * ATTN_HEAD_DIM
RET_QK_WIDTH = RET_HEADS * RET_KEY_DIM
RET_V_WIDTH = RET_HEADS * RET_VALUE_DIM
MERGE_WIDTH = 2 * D_MODEL
IN_SIZES = (ATTN_WIDTH, ATTN_KV_WIDTH, ATTN_KV_WIDTH, ATTN_WIDTH,
            RET_QK_WIDTH, RET_QK_WIDTH, RET_V_WIDTH, RET_V_WIDTH, MERGE_WIDTH)
IN_WIDTH = int(sum(IN_SIZES))
IN_SPLITS = tuple(int(s) for s in np.cumsum(IN_SIZES)[:-1])

kernel_name = "hybrid_gqa_axialrope_bidir_retention_gated_merge"


def rmsnorm(x, g):
    xf = x.astype(jnp.float32)
    xf = xf * lax.rsqrt(jnp.mean(xf * xf, axis=-1, keepdims=True) + EPS)
    return xf.astype(x.dtype) * g


def axial_rope_tables(seq_len, head_dim):
    rows = seq_len // GRID_W
    r, cidx = jnp.meshgrid(jnp.arange(rows), jnp.arange(GRID_W), indexing='ij')
    row = r.reshape(-1).astype(jnp.float32)
    col = cidx.reshape(-1).astype(jnp.float32)
    half = head_dim // 2
    inv_freq = ROPE_THETA ** (-jnp.arange(0, half, 2, dtype=jnp.float32) / half)
    ang_r = row[:, None] * inv_freq[None, :]
    ang_c = col[:, None] * inv_freq[None, :]
    return (jnp.cos(ang_r), jnp.sin(ang_r), jnp.cos(ang_c), jnp.sin(ang_c))


def _rotate(xp, cos, sin):
    x1, x2 = jnp.split(xp, 2, axis=-1)
    cos = cos[None, :, None, :].astype(xp.dtype)
    sin = sin[None, :, None, :].astype(xp.dtype)
    return jnp.concatenate([x1 * cos - x2 * sin, x2 * cos + x1 * sin], axis=-1)


def apply_axial_rope(x, tables):
    cos_r, sin_r, cos_c, sin_c = tables
    xr, xc = jnp.split(x, 2, axis=-1)
    return jnp.concatenate([_rotate(xr, cos_r, sin_r), _rotate(xc, cos_c, sin_c)], axis=-1)


def gqa_attention(q, k, v, qn_g, kn_g, tables):
    B, S = q.shape[0], q.shape[1]
    G = ATTN_HEADS // ATTN_KV_HEADS
    q = apply_axial_rope(rmsnorm(q, qn_g), tables) * (ATTN_HEAD_DIM ** -0.5)
    k = apply_axial_rope(rmsnorm(k, kn_g), tables)
    nblk = S // Q_BLOCK
    qb = q.reshape(B, nblk, Q_BLOCK, ATTN_KV_HEADS, G, ATTN_HEAD_DIM).transpose(1, 0, 2, 3, 4, 5)

    def block(qi):
        s = jnp.einsum('bqkgd,bskd->bkgqs', qi, k).astype(jnp.float32)
        p = jax.nn.softmax(s, axis=-1).astype(v.dtype)
        return jnp.einsum('bkgqs,bskd->bqkgd', p, v)

    o = lax.map(block, qb)
    return o.transpose(1, 0, 2, 3, 4, 5).reshape(B, S, ATTN_WIDTH)


def retention_one_direction(q, k, v, log_gamma, strict):
    B, S, H, dk = q.shape
    dv = v.shape[-1]
    C = RET_CHUNK
    N = S // C
    dt = q.dtype
    qc = q.reshape(B, N, C, H, dk)
    kc = k.reshape(B, N, C, H, dk)
    vc = v.reshape(B, N, C, H, dv)
    idx = jnp.arange(C, dtype=jnp.float32)
    diff = idx[:, None] - idx[None, :]
    mask = (diff > 0) if strict else (diff >= 0)
    decay_intra = jnp.where(mask[None], jnp.exp(log_gamma[:, None, None] * jnp.maximum(diff, 0.0)[None]), 0.0)
    scores = jnp.einsum('bnihd,bnjhd->bnhij', qc, kc) * decay_intra.astype(dt)[None, None]
    o_intra = jnp.einsum('bnhij,bnjhe->bnihe', scores, vc)
    k_dec = jnp.exp(log_gamma[None, :] * (C - 1 - idx)[:, None]).astype(dt)
    kv = jnp.einsum('bnjhd,bnjhe->nbhde', kc * k_dec[:, :, None], vc)
    chunk_decay = jnp.exp(log_gamma * C).astype(kv.dtype)[None, :, None, None]

    def step(R, kv_n):
        return chunk_decay * R + kv_n, R

    _, R_prev = lax.scan(step, jnp.zeros_like(kv[0]), kv)
    q_dec = jnp.exp(log_gamma[None, :] * (idx + 1.0)[:, None]).astype(dt)
    o_inter = jnp.einsum('bnihd,nbhde->bnihe', qc * q_dec[:, :, None], R_prev)
    return (o_intra + o_inter).reshape(B, S, H, dv)


def bidirectional_retention(q, k, v, w_dec_f, w_dec_b, gn_g, tables):
    B, S = q.shape[0], q.shape[1]
    q = apply_axial_rope(q, tables)
    k = apply_axial_rope(k, tables) * (RET_KEY_DIM ** -0.5)
    lg_f = jax.nn.log_sigmoid(w_dec_f.astype(jnp.float32))
    lg_b = jax.nn.log_sigmoid(w_dec_b.astype(jnp.float32))
    o_f = retention_one_direction(q, k, v, lg_f, False)
    o_b = jnp.flip(retention_one_direction(jnp.flip(q, 1), jnp.flip(k, 1), jnp.flip(v, 1), lg_b, True), 1)
    o = (o_f + o_b).astype(jnp.float32)
    mu = jnp.mean(o, axis=-1, keepdims=True)
    var = jnp.mean(jnp.square(o - mu), axis=-1, keepdims=True)
    o = ((o - mu) * lax.rsqrt(var + EPS)).astype(v.dtype)
    return o.reshape(B, S, RET_V_WIDTH) * gn_g


def _fwd_setup_inputs(seed: int = 0) -> dict:
    key = jax.random.key(seed)
    ks = jax.random.split(key, 20)
    f32 = jnp.float32

    def w(k, shape, fan_in):
        return jax.random.normal(k, shape, f32) * (fan_in ** -0.5)

    def gain(k, shape):
        return 1.0 + 0.05 * jax.random.normal(k, shape, f32)

    base = jnp.log(2.0 ** (5.0 + jnp.arange(RET_HEADS, dtype=f32)) - 1.0)
    return {
        "x": jax.random.normal(ks[0], (BATCH, SEQ, D_MODEL), f32),
        "c": jax.random.normal(ks[1], (BATCH, D_MODEL), f32),
        "w_ada": w(ks[2], (DEPTH, D_MODEL, 3 * D_MODEL), D_MODEL) * 0.5,
        "b_ada": 0.02 * jax.random.normal(ks[3], (DEPTH, 3 * D_MODEL), f32),
        "g_pre": gain(ks[4], (DEPTH, D_MODEL)),
        "w_in": w(ks[5], (DEPTH, D_MODEL, IN_WIDTH), D_MODEL),
        "qn_g": gain(ks[6], (DEPTH, ATTN_HEAD_DIM)),
        "kn_g": gain(ks[7], (DEPTH, ATTN_HEAD_DIM)),
        "w_dec_f": base[None] + 0.1 * jax.random.normal(ks[8], (DEPTH, RET_HEADS), f32),
        "w_dec_b": base[None] + 0.1 * jax.random.normal(ks[9], (DEPTH, RET_HEADS), f32),
        "gn_g": gain(ks[10], (DEPTH, RET_V_WIDTH)),
        "w_pa": w(ks[11], (DEPTH, ATTN_WIDTH, D_MODEL), ATTN_WIDTH),
        "w_pr": w(ks[12], (DEPTH, RET_V_WIDTH, D_MODEL), RET_V_WIDTH),
        "w_out": w(ks[13], (DEPTH, D_MODEL, D_MODEL), D_MODEL),
        "g_post": gain(ks[14], (DEPTH, D_MODEL)),
    }


def _fwd_reference(x, c, w_ada, b_ada, g_pre, w_in, qn_g, kn_g, w_dec_f, w_dec_b, gn_g,
              w_pa, w_pr, w_out, g_post):
    B, S, D = x.shape
    tables = axial_rope_tables(S, ATTN_HEAD_DIM)
    c_act = jax.nn.silu(c)
    for l in range(DEPTH):
        mod = c_act @ w_ada[l] + b_ada[l]
        shift, scale, gate = jnp.split(mod, 3, axis=-1)
        h = rmsnorm(x, g_pre[l]) * (1.0 + scale[:, None, :]) + shift[:, None, :]
        p = h @ w_in[l]
        qa, ka, va, za, qr, kr, vr, zr, gl = jnp.split(p, IN_SPLITS, axis=-1)
        ya = gqa_attention(qa.reshape(B, S, ATTN_HEADS, ATTN_HEAD_DIM),
                           ka.reshape(B, S, ATTN_KV_HEADS, ATTN_HEAD_DIM),
                           va.reshape(B, S, ATTN_KV_HEADS, ATTN_HEAD_DIM),
                           qn_g[l], kn_g[l], tables) * jax.nn.silu(za)
        yr = bidirectional_retention(qr.reshape(B, S, RET_HEADS, RET_KEY_DIM),
                                     kr.reshape(B, S, RET_HEADS, RET_KEY_DIM),
                                     vr.reshape(B, S, RET_HEADS, RET_VALUE_DIM),
                                     w_dec_f[l], w_dec_b[l], gn_g[l], tables) * jax.nn.silu(zr)
        g_att, g_ret = jnp.split(jax.nn.sigmoid(gl), 2, axis=-1)
        merged = g_att * (ya @ w_pa[l]) + g_ret * (yr @ w_pr[l])
        y = rmsnorm(merged @ w_out[l], g_post[l])
        x = x + gate[:, None, :] * y
    return x


import jax as _jax
import jax.numpy as _jnp

TWIN_FORMAT = 'train_step'
FWD_PARAMS = ['x', 'c', 'w_ada', 'b_ada', 'g_pre', 'w_in', 'qn_g', 'kn_g', 'w_dec_f', 'w_dec_b', 'gn_g', 'w_pa', 'w_pr', 'w_out', 'g_post']
TWIN_WEIGHTS = ['w_ada', 'b_ada', 'g_pre', 'w_in', 'qn_g', 'kn_g', 'w_dec_f', 'w_dec_b', 'gn_g', 'w_pa', 'w_pr', 'w_out', 'g_post']
TWIN_DIFF_INPUT = 'x'
TWIN_INPUTS = ['x', 'c', 'w_ada', 'b_ada', 'g_pre', 'w_in', 'qn_g', 'kn_g', 'w_dec_f', 'w_dec_b', 'gn_g', 'w_pa', 'w_pr', 'w_out', 'g_post', 'loss_target', 'm_w_ada', 'm_b_ada', 'm_g_pre', 'm_w_in', 'm_qn_g', 'm_kn_g', 'm_w_dec_f', 'm_w_dec_b', 'm_gn_g', 'm_w_pa', 'm_w_pr', 'm_w_out', 'm_g_post', 'v_w_ada', 'v_b_ada', 'v_g_pre', 'v_w_in', 'v_qn_g', 'v_kn_g', 'v_w_dec_f', 'v_w_dec_b', 'v_gn_g', 'v_w_pa', 'v_w_pr', 'v_w_out', 'v_g_post']
TWIN_OUTPUTS = ['loss', 'grad_x', 'grad_w_ada', 'grad_b_ada', 'grad_g_pre', 'grad_w_in', 'grad_qn_g', 'grad_kn_g', 'grad_w_dec_f', 'grad_w_dec_b', 'grad_gn_g', 'grad_w_pa', 'grad_w_pr', 'grad_w_out', 'grad_g_post', 'delta_w_ada', 'delta_b_ada', 'delta_g_pre', 'delta_w_in', 'delta_qn_g', 'delta_kn_g', 'delta_w_dec_f', 'delta_w_dec_b', 'delta_gn_g', 'delta_w_pa', 'delta_w_pr', 'delta_w_out', 'delta_g_post', 'new_m_w_ada', 'new_m_b_ada', 'new_m_g_pre', 'new_m_w_in', 'new_m_qn_g', 'new_m_kn_g', 'new_m_w_dec_f', 'new_m_w_dec_b', 'new_m_gn_g', 'new_m_w_pa', 'new_m_w_pr', 'new_m_w_out', 'new_m_g_post', 'new_v_w_ada', 'new_v_b_ada', 'new_v_g_pre', 'new_v_w_in', 'new_v_qn_g', 'new_v_kn_g', 'new_v_w_dec_f', 'new_v_w_dec_b', 'new_v_gn_g', 'new_v_w_pa', 'new_v_w_pr', 'new_v_w_out', 'new_v_g_post']
TWIN_LEAF_KINDS = {'loss': 'loss', 'grad_x': 'grad_x', 'grad_w_ada': 'grad_w', 'grad_b_ada': 'grad_w', 'grad_g_pre': 'grad_w', 'grad_w_in': 'grad_w', 'grad_qn_g': 'grad_w', 'grad_kn_g': 'grad_w', 'grad_w_dec_f': 'grad_w', 'grad_w_dec_b': 'grad_w', 'grad_gn_g': 'grad_w', 'grad_w_pa': 'grad_w', 'grad_w_pr': 'grad_w', 'grad_w_out': 'grad_w', 'grad_g_post': 'grad_w', 'delta_w_ada': 'delta_w', 'delta_b_ada': 'delta_w', 'delta_g_pre': 'delta_w', 'delta_w_in': 'delta_w', 'delta_qn_g': 'delta_w', 'delta_kn_g': 'delta_w', 'delta_w_dec_f': 'delta_w', 'delta_w_dec_b': 'delta_w', 'delta_gn_g': 'delta_w', 'delta_w_pa': 'delta_w', 'delta_w_pr': 'delta_w', 'delta_w_out': 'delta_w', 'delta_g_post': 'delta_w', 'new_m_w_ada': 'new_m', 'new_m_b_ada': 'new_m', 'new_m_g_pre': 'new_m', 'new_m_w_in': 'new_m', 'new_m_qn_g': 'new_m', 'new_m_kn_g': 'new_m', 'new_m_w_dec_f': 'new_m', 'new_m_w_dec_b': 'new_m', 'new_m_gn_g': 'new_m', 'new_m_w_pa': 'new_m', 'new_m_w_pr': 'new_m', 'new_m_w_out': 'new_m', 'new_m_g_post': 'new_m', 'new_v_w_ada': 'new_v', 'new_v_b_ada': 'new_v', 'new_v_g_pre': 'new_v', 'new_v_w_in': 'new_v', 'new_v_qn_g': 'new_v', 'new_v_kn_g': 'new_v', 'new_v_w_dec_f': 'new_v', 'new_v_w_dec_b': 'new_v', 'new_v_gn_g': 'new_v', 'new_v_w_pa': 'new_v', 'new_v_w_pr': 'new_v', 'new_v_w_out': 'new_v', 'new_v_g_post': 'new_v'}


def _forward(args):
    return _fwd_reference(*[args[k] for k in FWD_PARAMS])


def _output_shape():
    out = _jax.eval_shape(lambda: _forward(_fwd_setup_inputs(0)))
    return out.shape, out.dtype

N_MICROBATCH = 1
ADAM_LR = 0.001
ADAM_B1 = 0.9
ADAM_B2 = 0.999
ADAM_EPS = 1e-08
ADAM_WD = 0.01
ADAM_STEP = 10
PER_EXAMPLE_BATCH_AXIS = {'x': 0, 'c': 0, 'loss_target': 0}
SHARED_INPUTS = []
_WEIGHT_DTYPES = {'w_ada': _jnp.float32, 'b_ada': _jnp.float32, 'g_pre': _jnp.float32, 'w_in': _jnp.float32, 'qn_g': _jnp.float32, 'kn_g': _jnp.float32, 'w_dec_f': _jnp.float32, 'w_dec_b': _jnp.float32, 'gn_g': _jnp.float32, 'w_pa': _jnp.float32, 'w_pr': _jnp.float32, 'w_out': _jnp.float32, 'g_post': _jnp.float32}
MOMENT_SCALE = {'w_ada': 1.691048e+00, 'b_ada': 3.154428e+00, 'g_pre': 1.715540e-01, 'w_in': 9.065108e-02, 'qn_g': 3.154976e-02, 'kn_g': 3.117440e-02, 'w_dec_f': 1.835307e-01, 'w_dec_b': 1.354543e+00, 'gn_g': 1.849300e-01, 'w_pa': 3.725098e-02, 'w_pr': 1.066410e-01, 'w_out': 1.201527e-01, 'g_post': 3.714420e+00}


def _to_microbatches(a, axis):
    t = _jnp.moveaxis(a, axis, 0)
    t = t.reshape((N_MICROBATCH, t.shape[0] // N_MICROBATCH) + t.shape[1:])
    return _jnp.moveaxis(t, 1, axis + 1)


def setup_inputs(seed: int = 0) -> dict:
    inp = _fwd_setup_inputs(seed)
    key = _jax.random.fold_in(_jax.random.key(seed), 7919)
    shape, _ = _output_shape()
    out = dict(inp)
    out["loss_target"] = _jax.random.normal(_jax.random.fold_in(key, 0), shape, _jnp.float32)
    for i, name in enumerate(TWIN_WEIGHTS):
        w = inp[name].astype(_jnp.float32)
        if MOMENT_SCALE is None:
            s = _jnp.sqrt(_jnp.mean(_jnp.square(w)) + 1e-30)
        else:
            s = MOMENT_SCALE[name]
        km, kv = _jax.random.split(_jax.random.fold_in(key, i + 1))
        out[name] = w
        out["m_" + name] = s * _jax.random.normal(km, w.shape, _jnp.float32)
        out["v_" + name] = (s * s) * _jax.random.uniform(kv, w.shape, _jnp.float32, 0.5, 1.5)
    if N_MICROBATCH > 1:
        for name, axis in PER_EXAMPLE_BATCH_AXIS.items():
            out[name] = _to_microbatches(out[name], axis)
    return {'x': out['x'], 'c': out['c'], 'w_ada': out['w_ada'], 'b_ada': out['b_ada'], 'g_pre': out['g_pre'], 'w_in': out['w_in'], 'qn_g': out['qn_g'], 'kn_g': out['kn_g'], 'w_dec_f': out['w_dec_f'], 'w_dec_b': out['w_dec_b'], 'gn_g': out['gn_g'], 'w_pa': out['w_pa'], 'w_pr': out['w_pr'], 'w_out': out['w_out'], 'g_post': out['g_post'], 'loss_target': out['loss_target'], 'm_w_ada': out['m_w_ada'], 'm_b_ada': out['m_b_ada'], 'm_g_pre': out['m_g_pre'], 'm_w_in': out['m_w_in'], 'm_qn_g': out['m_qn_g'], 'm_kn_g': out['m_kn_g'], 'm_w_dec_f': out['m_w_dec_f'], 'm_w_dec_b': out['m_w_dec_b'], 'm_gn_g': out['m_gn_g'], 'm_w_pa': out['m_w_pa'], 'm_w_pr': out['m_w_pr'], 'm_w_out': out['m_w_out'], 'm_g_post': out['m_g_post'], 'v_w_ada': out['v_w_ada'], 'v_b_ada': out['v_b_ada'], 'v_g_pre': out['v_g_pre'], 'v_w_in': out['v_w_in'], 'v_qn_g': out['v_qn_g'], 'v_kn_g': out['v_kn_g'], 'v_w_dec_f': out['v_w_dec_f'], 'v_w_dec_b': out['v_w_dec_b'], 'v_gn_g': out['v_gn_g'], 'v_w_pa': out['v_w_pa'], 'v_w_pr': out['v_w_pr'], 'v_w_out': out['v_w_out'], 'v_g_post': out['v_g_post']}


def _loss(weights, diff, rest, loss_target):
    with _jax.named_scope("forward"):
        args = {**rest, TWIN_DIFF_INPUT: diff, **{k: w.astype(_WEIGHT_DTYPES[k]) for k, w in weights.items()}}
        y = _forward(args)
    with _jax.named_scope("loss_head"):
        err = _jnp.square(y.astype(_jnp.float32) - loss_target)
        return 0.5 * _jnp.sum(_jnp.mean(err, axis=-1)) if err.ndim else 0.5 * err


def _adamw(w, g, m, v):
    m = ADAM_B1 * m + (1.0 - ADAM_B1) * g
    v = ADAM_B2 * v + (1.0 - ADAM_B2) * _jnp.square(g)
    m_hat = m / (1.0 - ADAM_B1 ** ADAM_STEP)
    v_hat = v / (1.0 - ADAM_B2 ** ADAM_STEP)
    delta = -ADAM_LR * (m_hat / (_jnp.sqrt(v_hat) + ADAM_EPS) + ADAM_WD * w)
    return delta, m, v


def reference(x, c, w_ada, b_ada, g_pre, w_in, qn_g, kn_g, w_dec_f, w_dec_b, gn_g, w_pa, w_pr, w_out, g_post, loss_target, m_w_ada, m_b_ada, m_g_pre, m_w_in, m_qn_g, m_kn_g, m_w_dec_f, m_w_dec_b, m_gn_g, m_w_pa, m_w_pr, m_w_out, m_g_post, v_w_ada, v_b_ada, v_g_pre, v_w_in, v_qn_g, v_kn_g, v_w_dec_f, v_w_dec_b, v_gn_g, v_w_pa, v_w_pr, v_w_out, v_g_post):
    given = dict(x=x, c=c, w_ada=w_ada, b_ada=b_ada, g_pre=g_pre, w_in=w_in, qn_g=qn_g, kn_g=kn_g, w_dec_f=w_dec_f, w_dec_b=w_dec_b, gn_g=gn_g, w_pa=w_pa, w_pr=w_pr, w_out=w_out, g_post=g_post, loss_target=loss_target, m_w_ada=m_w_ada, m_b_ada=m_b_ada, m_g_pre=m_g_pre, m_w_in=m_w_in, m_qn_g=m_qn_g, m_kn_g=m_kn_g, m_w_dec_f=m_w_dec_f, m_w_dec_b=m_w_dec_b, m_gn_g=m_gn_g, m_w_pa=m_w_pa, m_w_pr=m_w_pr, m_w_out=m_w_out, m_g_post=m_g_post, v_w_ada=v_w_ada, v_b_ada=v_b_ada, v_g_pre=v_g_pre, v_w_in=v_w_in, v_qn_g=v_qn_g, v_kn_g=v_kn_g, v_w_dec_f=v_w_dec_f, v_w_dec_b=v_w_dec_b, v_gn_g=v_gn_g, v_w_pa=v_w_pa, v_w_pr=v_w_pr, v_w_out=v_w_out, v_g_post=v_g_post)
    weights = {n: given[n] for n in TWIN_WEIGHTS}
    shared = {n: given[n] for n in SHARED_INPUTS}
    per_example = {n: given[n] for n in ['x', 'c']}
    grad_fn = _jax.value_and_grad(_loss, argnums=(0, 1))

    def one_microbatch(ex, loss_target):
        ex = dict(ex)
        diff = ex.pop(TWIN_DIFF_INPUT)
        return grad_fn(weights, diff, {**shared, **ex}, loss_target)

    if N_MICROBATCH == 1:
        loss, (grad_w, grad_x) = one_microbatch(per_example, given["loss_target"])
    else:
        def body(carry, xs):
            loss_sum, grad_sum = carry
            l_k, (gw_k, gx_k) = one_microbatch(xs[0], xs[1])
            with _jax.named_scope("update"):
                return (loss_sum + l_k, _jax.tree.map(_jnp.add, grad_sum, gw_k)), gx_k

        init = (_jnp.zeros((), _jnp.float32), _jax.tree.map(_jnp.zeros_like, weights))
        (loss, grad_w), grad_x = _jax.lax.scan(body, init, (per_example, given["loss_target"]))
    with _jax.named_scope("update"):
        delta_w, new_m, new_v = {}, {}, {}
        for n in TWIN_WEIGHTS:
            delta_w[n], new_m[n], new_v[n] = _adamw(weights[n], grad_w[n], given["m_" + n], given["v_" + n])
    return (loss, grad_x, *[grad_w[n] for n in TWIN_WEIGHTS], *[delta_w[n] for n in TWIN_WEIGHTS],
            *[new_m[n] for n in TWIN_WEIGHTS], *[new_v[n] for n in TWIN_WEIGHTS])
```

```python
import functools

import jax
import jax.numpy as jnp
from jax import lax
from jax.experimental import pallas as pl
from jax.experimental.pallas import tpu as pltpu

F32 = jnp.float32
BF16 = jnp.bfloat16
MXU_DTYPE = jnp.bfloat16

D_MODEL = 1024
GRID_W = 64
HEAD_DIM = 64
ROPE_THETA = 10000.0
EPS = 1e-6
RET_HEADS = 4
RET_CHUNK = 256
IN_WIDTH = 4864
QA, KA, VA, ZA, QR, KR, VR, ZR, GL = 0, 512, 640, 768, 1280, 1536, 1792, 2304, 2816

ADAM_LR, ADAM_B1, ADAM_B2, ADAM_EPS, ADAM_WD, ADAM_STEP = 0.001, 0.9, 0.999, 1e-08, 0.01, 10

VMEM_LIMIT = 56 * 1024 * 1024
MESH = pl.DeviceIdType.MESH
HIGHEST = lax.Precision.HIGHEST


def _cp(*sem, **kw):
    if sem:
        kw["dimension_semantics"] = sem
    return pltpu.CompilerParams(vmem_limit_bytes=VMEM_LIMIT, **kw)


def _nn(a, b, **kw):
    return lax.dot_general(a, b, (((1,), (0,)), ((), ())), preferred_element_type=F32, **kw)


def _nt(a, b):
    return lax.dot_general(a, b, (((1,), (1,)), ((), ())), preferred_element_type=F32)


def _tn(a, b):
    return lax.dot_general(a, b, (((0,), (0,)), ((), ())), preferred_element_type=F32)


def _mx(x):
    return x.astype(MXU_DTYPE)


def _lane(shape):
    return lax.broadcasted_iota(jnp.int32, shape, 1)


def _sigmoid(z):
    return 1.0 / (1.0 + jnp.exp(-z))


def _rowsum128(x):
    s = jnp.sum(x, axis=0, keepdims=True)
    out = s[:, 0:128]
    for k in range(1, x.shape[1] // 128):
        out = out + s[:, 128 * k:128 * (k + 1)]
    return out


def _swap16(x):
    return jnp.where((_lane(x.shape) & 16) == 0, pltpu.roll(x, 112, 1), pltpu.roll(x, 16, 1))


def _rope(x, cos, sin):
    return x * cos + _swap16(x) * sin


def _rope_t(d, cos, sin):
    return d * cos + _swap16(d * sin)


def _blockdiag64():
    r = lax.broadcasted_iota(jnp.int32, (128, 128), 0) // 64
    c = lax.broadcasted_iota(jnp.int32, (128, 128), 1) // 64
    return (r == c).astype(BF16)


def _seg64(x, m):
    hi = x.astype(BF16)
    lo = (x - hi.astype(F32)).astype(BF16)
    return _nn(hi, m) + _nn(lo, m)


def _rope_tables(seq):
    t = jnp.arange(seq)
    row = (t // GRID_W).astype(F32)
    col = (t % GRID_W).astype(F32)
    half = HEAD_DIM // 2
    inv_freq = ROPE_THETA ** (-jnp.arange(0, half, 2, dtype=F32) / half)
    ar = row[:, None] * inv_freq[None, :]
    ac = col[:, None] * inv_freq[None, :]
    cr, sr, cc, sc = jnp.cos(ar), jnp.sin(ar), jnp.cos(ac), jnp.sin(ac)
    cos64 = jnp.concatenate([cr, cr, cc, cc], axis=1)
    sin64 = jnp.concatenate([-sr, sr, -sc, sc], axis=1)
    return jnp.tile(cos64, (1, 2)), jnp.tile(sin64, (1, 2))


def _prenorm(x, mod, g_pre):
    seq = x.shape[0]
    bs = 512

    def body(x_ref, mod_ref, g_ref, h_ref):
        xv = x_ref[...]
        r = lax.rsqrt(jnp.mean(xv * xv, axis=-1, keepdims=True) + EPS)
        shift = mod_ref[:, 0:D_MODEL]
        scale = mod_ref[:, D_MODEL:2 * D_MODEL]
        h_ref[...] = ((xv * r) * g_ref[...] * (1.0 + scale) + shift).astype(h_ref.dtype)

    return pl.pallas_call(
        body, grid=(seq // bs,), name="prenorm",
        in_specs=[pl.BlockSpec((bs, D_MODEL), lambda i: (i, 0)), pl.BlockSpec((1, 3 * D_MODEL), lambda i: (0, 0)),
                  pl.BlockSpec((1, D_MODEL), lambda i: (0, 0))],
        out_specs=pl.BlockSpec((bs, D_MODEL), lambda i: (i, 0)),
        out_shape=jax.ShapeDtypeStruct((seq, D_MODEL), MXU_DTYPE), compiler_params=_cp("parallel"))(x, mod, g_pre)


def _prenorm_bwd(x, dh, dout, mod, g_pre):
    seq = x.shape[0]
    bs = 512

    def body(x_ref, dh_ref, dout_ref, mod_ref, g_ref, gx_ref, dshift_ref, dscale_ref, dg_ref):
        @pl.when(pl.program_id(0) == 0)
        def _():
            dshift_ref[...] = jnp.zeros_like(dshift_ref)
            dscale_ref[...] = jnp.zeros_like(dscale_ref)
            dg_ref[...] = jnp.zeros_like(dg_ref)

        xv = x_ref[...]
        dh = dh_ref[...]
        g = g_ref[...]
        r = lax.rsqrt(jnp.mean(xv * xv, axis=-1, keepdims=True) + EPS)
        xn = xv * r
        scale = mod_ref[:, D_MODEL:2 * D_MODEL]
        dshift_ref[...] += jnp.sum(dh, axis=0, keepdims=True)
        dscale_ref[...] += jnp.sum(dh * (xn * g), axis=0, keepdims=True)
        da = dh * (1.0 + scale)
        dg_ref[...] += jnp.sum(da * xn, axis=0, keepdims=True)
        dxn = da * g
        dx = r * (dxn - xn * jnp.mean(dxn * xn, axis=-1, keepdims=True))
        gx_ref[...] = dout_ref[...] + dx

    row = pl.BlockSpec((bs, D_MODEL), lambda i: (i, 0))
    vec = pl.BlockSpec((1, D_MODEL), lambda i: (0, 0))
    return pl.pallas_call(
        body, grid=(seq // bs,), name="prenorm_bwd",
        in_specs=[row, row, row, pl.BlockSpec((1, 3 * D_MODEL), lambda i: (0, 0)), vec],
        out_specs=[row, vec, vec, vec],
        out_shape=[jax.ShapeDtypeStruct((seq, D_MODEL), F32)] + [jax.ShapeDtypeStruct((1, D_MODEL), F32)] * 3,
        compiler_params=_cp("arbitrary"))(x, dh, dout, mod, g_pre)


def _matmul(a, b, *, ta, tb, tm, tn, out_dtype, name):
    kdim = a.shape[0] if ta else a.shape[1]
    m = a.shape[1] if ta else a.shape[0]
    n = b.shape[0] if tb else b.shape[1]
    assert m % tm == 0 and n % tn == 0

    def body(a_ref, b_ref, o_ref):
        dims = (((0 if ta else 1,), (1 if tb else 0,)), ((), ()))
        o_ref[...] = lax.dot_general(a_ref[...], b_ref[...], dims, preferred_element_type=F32).astype(o_ref.dtype)

    a_spec = pl.BlockSpec((kdim, tm), lambda j, i: (0, i)) if ta else pl.BlockSpec((tm, kdim), lambda j, i: (i, 0))
    b_spec = pl.BlockSpec((tn, kdim), lambda j, i: (j, 0)) if tb else pl.BlockSpec((kdim, tn), lambda j, i: (0, j))
    return pl.pallas_call(
        body, grid=(n // tn, m // tm), name=name, in_specs=[a_spec, b_spec],
        out_specs=pl.BlockSpec((tm, tn), lambda j, i: (i, j)),
        out_shape=jax.ShapeDtypeStruct((m, n), out_dtype), compiler_params=_cp("parallel", "parallel"))(a, b)


def _prep(p, cos, sin, qg2, kg2):
    seq = p.shape[0]
    bs = 512

    def body(p_ref, cos_ref, sin_ref, qg_ref, kg_ref, qt_ref, kd_ref, vd_ref, rq_ref, rk_ref, rv_ref):
        m = _blockdiag64()
        cs, sn = cos_ref[...], sin_ref[...]
        lo = _lane((bs, 128)) < 64
        for pr in range(4):
            q = p_ref[:, QA + 128 * pr:QA + 128 * (pr + 1)]
            r = lax.rsqrt(_seg64(q * q, m) * (1.0 / 64) + EPS)
            qt_ref[:, 128 * pr:128 * (pr + 1)] = (_rope(q * r * qg_ref[...], cs, sn) * 0.125).astype(qt_ref.dtype)
        k = p_ref[:, KA:KA + 128]
        r = lax.rsqrt(_seg64(k * k, m) * (1.0 / 64) + EPS)
        kr = _rope(k * r * kg_ref[...], cs, sn)
        ksw = pltpu.roll(kr, 64, 1)
        kd_ref[0] = jnp.where(lo, kr, ksw).astype(kd_ref.dtype)
        kd_ref[1] = jnp.where(lo, ksw, kr).astype(kd_ref.dtype)
        v = p_ref[:, VA:VA + 128]
        vsw = pltpu.roll(v, 64, 1)
        vd_ref[0] = jnp.where(lo, v, vsw).astype(vd_ref.dtype)
        vd_ref[1] = jnp.where(lo, vsw, v).astype(vd_ref.dtype)
        for pr in range(2):
            sl = slice(128 * pr, 128 * (pr + 1))
            rq_ref[:, sl] = _rope(p_ref[:, QR + 128 * pr:QR + 128 * (pr + 1)], cs, sn).astype(rq_ref.dtype)
            rk_ref[:, sl] = (_rope(p_ref[:, KR + 128 * pr:KR + 128 * (pr + 1)], cs, sn) * 0.125).astype(rk_ref.dtype)
        rv_ref[...] = p_ref[:, VR:VR + 512].astype(rv_ref.dtype)

    tab = pl.BlockSpec((bs, 128), lambda i: (i, 0))
    gsp = pl.BlockSpec((1, 128), lambda i: (0, 0))
    dup = pl.BlockSpec((2, bs, 128), lambda i: (0, i, 0))
    return pl.pallas_call(
        body, grid=(seq // bs,), name="mixer_prep",
        in_specs=[pl.BlockSpec((bs, ZR), lambda i: (i, 0)), tab, tab, gsp, gsp],
        out_specs=[pl.BlockSpec((bs, 512), lambda i: (i, 0)), dup, dup, pl.BlockSpec((bs, 256), lambda i: (i, 0)),
                   pl.BlockSpec((bs, 256), lambda i: (i, 0)), pl.BlockSpec((bs, 512), lambda i: (i, 0))],
        out_shape=[jax.ShapeDtypeStruct((seq, 512), MXU_DTYPE), jax.ShapeDtypeStruct((2, seq, 128), MXU_DTYPE),
                   jax.ShapeDtypeStruct((2, seq, 128), MXU_DTYPE), jax.ShapeDtypeStruct((seq, 256), MXU_DTYPE),
                   jax.ShapeDtypeStruct((seq, 256), MXU_DTYPE), jax.ShapeDtypeStruct((seq, 512), MXU_DTYPE)],
        compiler_params=_cp("parallel"))(p, cos, sin, qg2, kg2)


def _halves(kd):
    lo = _lane(kd.shape) < 64
    z = jnp.zeros_like(kd)
    return jnp.concatenate([jnp.where(lo, kd, z), jnp.where(lo, z, kd)], axis=0)


def _attn_fwd(qt, kd, vd):
    seq = qt.shape[0]
    bq = bk = 512
    nk = seq // bk

    def body(q_ref, k_ref, v_ref, o_ref, lse_ref, m_scr, l_scr, acc):
        j, n = pl.program_id(1), pl.program_id(2)

        @pl.when(n == 0)
        def _():
            m_scr[...] = jnp.full_like(m_scr, -jnp.inf)
            l_scr[...] = jnp.zeros_like(l_scr)
            acc[...] = jnp.zeros_like(acc)

        k2, v2 = _halves(k_ref[...]), _halves(v_ref[...])
        lo = _lane((bq, 128)) < 64
        for pr in range(2):
            s2 = _nt(q_ref[:, 128 * pr:128 * (pr + 1)], k2)
            ps, alphas = [], []
            for e in range(2):
                g = 2 * pr + e
                s = s2[:, e * bk:(e + 1) * bk]
                m_prev = m_scr[g]
                m_next = jnp.maximum(m_prev, jnp.max(s, axis=1, keepdims=True))
                alpha = jnp.exp(m_prev - m_next)
                pe = jnp.exp(s - jnp.tile(m_next, (1, bk // 128)))
                l_scr[g] = alpha * l_scr[g] + jnp.sum(pe, axis=1, keepdims=True)
                m_scr[g] = m_next
                ps.append(_mx(pe))
                alphas.append(alpha)
            o2 = _nn(jnp.concatenate(ps, axis=1), v2)
            sl = slice(128 * pr, 128 * (pr + 1))
            acc[:, sl] = acc[:, sl] * jnp.where(lo, alphas[0], alphas[1]) + o2

        @pl.when(n == nk - 1)
        def _():
            for pr in range(2):
                sl = slice(128 * pr, 128 * (pr + 1))
                o_ref[:, sl] = acc[:, sl] / jnp.where(lo, l_scr[2 * pr], l_scr[2 * pr + 1])
            for g in range(4):
                lse = jnp.transpose(m_scr[g] + jnp.log(l_scr[g]))
                lse_ref[pl.ds(4 * j + g, 1), :] = lse[0:1, :]

    return pl.pallas_call(
        body, grid=(seq // bq, 2, nk), name="attn_fwd",
        in_specs=[pl.BlockSpec((bq, 256), lambda i, j, n: (i, j)), pl.BlockSpec((None, bk, 128), lambda i, j, n: (j, n, 0)),
                  pl.BlockSpec((None, bk, 128), lambda i, j, n: (j, n, 0))],
        out_specs=[pl.BlockSpec((bq, 256), lambda i, j, n: (i, j)), pl.BlockSpec((8, bq), lambda i, j, n: (0, i))],
        out_shape=[jax.ShapeDtypeStruct((seq, 512), F32), jax.ShapeDtypeStruct((8, seq), F32)],
        scratch_shapes=[pltpu.VMEM((4, bq, 128), F32), pltpu.VMEM((4, bq, 128), F32), pltpu.VMEM((bq, 256), F32)],
        compiler_params=_cp("parallel", "arbitrary", "arbitrary"))(qt, kd, vd)


def _attn_bwd(qt, kd, vd, do, lse, delta):
    seq = qt.shape[0]
    bq = bk = 512
    nq, nk = seq // bq, seq // bk

    def body(q_ref, do_ref, k_ref, v_ref, lse_ref, del_ref, dq_ref, dk_ref, dv_ref):
        j, i, n = pl.program_id(0), pl.program_id(1), pl.program_id(2)

        @pl.when(n == 0)
        def _():
            dq_ref[...] = jnp.zeros_like(dq_ref)

        @pl.when((i == 0) & (n == 0))
        def _():
            dk_ref[...] = jnp.zeros_like(dk_ref)
            dv_ref[...] = jnp.zeros_like(dv_ref)

        k2, v2 = _halves(k_ref[...]), _halves(v_ref[...])
        lo = _lane((bk, 128)) < 64
        rows = pl.ds(pl.multiple_of(n * bk, bk), bk)
        for pr in range(2):
            sl = slice(128 * pr, 128 * (pr + 1))
            qp, dop = q_ref[:, sl], do_ref[:, sl]
            s_t = _nt(k2, qp)
            dp_t = _nt(v2, dop)
            ls = [lse_ref[pl.ds(4 * j + 2 * pr + e, 1), :] for e in range(2)]
            dl = [del_ref[pl.ds(4 * j + 2 * pr + e, 1), :] for e in range(2)]
            big = lambda r: jnp.concatenate([jnp.broadcast_to(r[0], (bk, bq)), jnp.broadcast_to(r[1], (bk, bq))], axis=0)
            p_t = jnp.exp(s_t - big(ls))
            ds_t = _mx(p_t * (dp_t - big(dl)))
            rv = _nn(_mx(p_t), dop)
            rk = _nn(ds_t, qp)
            dv_ref[rows, :] += jnp.where(lo, rv[:bk], rv[bk:])
            dk_ref[rows, :] += jnp.where(lo, rk[:bk], rk[bk:])
            dq_ref[:, sl] += _tn(ds_t, k2)

    return pl.pallas_call(
        body, grid=(2, nq, nk), name="attn_bwd",
        in_specs=[pl.BlockSpec((bq, 256), lambda j, i, n: (i, j)), pl.BlockSpec((bq, 256), lambda j, i, n: (i, j)),
                  pl.BlockSpec((None, bk, 128), lambda j, i, n: (j, n, 0)), pl.BlockSpec((None, bk, 128), lambda j, i, n: (j, n, 0)),
                  pl.BlockSpec((8, bq), lambda j, i, n: (0, i)), pl.BlockSpec((8, bq), lambda j, i, n: (0, i))],
        out_specs=[pl.BlockSpec((bq, 256), lambda j, i, n: (i, j)), pl.BlockSpec((None, seq, 128), lambda j, i, n: (j, 0, 0)),
                   pl.BlockSpec((None, seq, 128), lambda j, i, n: (j, 0, 0))],
        out_shape=[jax.ShapeDtypeStruct((seq, 512), F32), jax.ShapeDtypeStruct((2, seq, 128), F32),
                   jax.ShapeDtypeStruct((2, seq, 128), F32)],
        compiler_params=_cp("arbitrary", "arbitrary", "arbitrary"))(qt, do, kd, vd, lse, delta)


def _ret_tables(lg_f, lg_b, c):
    idx = jnp.arange(c, dtype=F32)
    diff = idx[:, None] - idx[None, :]
    a, b = lg_f[:, None, None], lg_b[:, None, None]
    low, up = (diff >= 0)[None], (diff < 0)[None]
    dmat = jnp.where(low, jnp.exp(a * jnp.maximum(diff, 0.0)[None]), jnp.exp(b * jnp.maximum(-diff, 0.0)[None]))
    dda = jnp.where(low, diff[None] * jnp.exp(a * jnp.maximum(diff, 0.0)[None]), 0.0)
    ddb = jnp.where(up, -diff[None] * jnp.exp(b * jnp.maximum(-diff, 0.0)[None]), 0.0)
    rep = lambda w: jnp.broadcast_to(w[:, :, None], (RET_HEADS, c, 128))
    wqf = rep(jnp.exp(lg_f[:, None] * (idx + 1.0)[None]))
    wqb = rep(jnp.exp(lg_b[:, None] * (c - idx)[None]))
    wkf = rep(jnp.exp(lg_f[:, None] * (c - 1.0 - idx)[None]))
    wkb = rep(jnp.exp(lg_b[:, None] * idx[None]))
    cdf, cdb = jnp.exp(lg_f * c), jnp.exp(lg_b * c)
    dec_fb = jnp.broadcast_to(jnp.concatenate([cdf, cdb])[:, None], (8, 128))
    dec_bf = jnp.broadcast_to(jnp.concatenate([cdb, cdf])[:, None], (8, 128))
    return dict(dmat=dmat, dda=dda, ddb=ddb, wqf=wqf, wqb=wqb, wkf=wkf, wkb=wkb, dec_fb=dec_fb, dec_bf=dec_bf)


def _ret_states(xk, yv, w_fwd, w_rev, dec, name):
    seq = xk.shape[0]
    c = RET_CHUNK
    nc = seq // c

    def body(xf_ref, yf_ref, xr_ref, yr_ref, wf_ref, wr_ref, dec_ref, sf_ref, sr_ref, st_f, st_r):
        @pl.when(pl.program_id(0) == 0)
        def _():
            st_f[...] = jnp.zeros_like(st_f)
            st_r[...] = jnp.zeros_like(st_r)

        lane = _lane((c, 128))
        for h in range(RET_HEADS):
            pr, e = h // 2, h % 2
            half = (lane < 64) if e == 0 else (lane >= 64)
            for x_ref, y_ref, w_ref, s_ref, st, drow in ((xf_ref, yf_ref, wf_ref, sf_ref, st_f, h),
                                                        (xr_ref, yr_ref, wr_ref, sr_ref, st_r, 4 + h)):
                s_ref[h] = st[h].astype(s_ref.dtype)
                xm = jnp.where(half, x_ref[:, 128 * pr:128 * (pr + 1)].astype(F32) * w_ref[h], 0.0)
                st[h] = st[h] * dec_ref[drow:drow + 1, :] + _tn(_mx(xm), _mx(y_ref[:, 128 * h:128 * (h + 1)]))

    xs = lambda f: pl.BlockSpec((c, 256), f)
    ys = lambda f: pl.BlockSpec((c, 512), f)
    fwd, rev = (lambda n: (n, 0)), (lambda n: (nc - 1 - n, 0))
    wsp = pl.BlockSpec((RET_HEADS, c, 128), lambda n: (0, 0, 0))
    return pl.pallas_call(
        body, grid=(nc,), name=name,
        in_specs=[xs(fwd), ys(fwd), xs(rev), ys(rev), wsp, wsp, pl.BlockSpec((8, 128), lambda n: (0, 0))],
        out_specs=[pl.BlockSpec((None, RET_HEADS, 128, 128), lambda n: (n, 0, 0, 0)),
                   pl.BlockSpec((None, RET_HEADS, 128, 128), lambda n: (nc - 1 - n, 0, 0, 0))],
        out_shape=[jax.ShapeDtypeStruct((nc, RET_HEADS, 128, 128), MXU_DTYPE)] * 2,
        scratch_shapes=[pltpu.VMEM((RET_HEADS, 128, 128), F32), pltpu.VMEM((RET_HEADS, 128, 128), F32)],
        compiler_params=_cp("arbitrary"))(xk, yv, xk, yv, w_fwd, w_rev, dec)


def _ret_fwd(rq, rk, rv, rf, rb, tb):
    seq = rq.shape[0]
    c = RET_CHUNK

    def body(q_ref, k_ref, v_ref, rf_ref, rb_ref, d_ref, wqf_ref, wqb_ref, o_ref):
        lane = _lane((c, 128))
        for h in range(RET_HEADS):
            pr, e = h // 2, h % 2
            half = (lane < 64) if e == 0 else (lane >= 64)
            sl = slice(128 * pr, 128 * (pr + 1))
            qp = q_ref[:, sl]
            km = jnp.where(half, k_ref[:, sl], jnp.zeros_like(k_ref[:, sl]))
            sd = _mx(_nt(qp, km) * d_ref[h])
            qf = qp.astype(F32)
            o = _nn(sd, v_ref[:, 128 * h:128 * (h + 1)])
            o = o + _nn(_mx(qf * wqf_ref[h]), rf_ref[h]) + _nn(_mx(qf * wqb_ref[h]), rb_ref[h])
            o_ref[:, 128 * h:128 * (h + 1)] = o

    st = pl.BlockSpec((None, RET_HEADS, 128, 128), lambda n: (n, 0, 0, 0))
    wsp = pl.BlockSpec((RET_HEADS, c, 128), lambda n: (0, 0, 0))
    return pl.pallas_call(
        body, grid=(seq // c,), name="ret_fwd",
        in_specs=[pl.BlockSpec((c, 256), lambda n: (n, 0)), pl.BlockSpec((c, 256), lambda n: (n, 0)),
                  pl.BlockSpec((c, 512), lambda n: (n, 0)), st, st, pl.BlockSpec((RET_HEADS, c, c), lambda n: (0, 0, 0)), wsp, wsp],
        out_specs=pl.BlockSpec((c, 512), lambda n: (n, 0)),
        out_shape=jax.ShapeDtypeStruct((seq, 512), F32), compiler_params=_cp("parallel"))(
            rq, rk, rv, rf, rb, tb["dmat"], tb["wqf"], tb["wqb"])


def _ret_bwd(rq, rk, rv, do, rf, rb, hf, hb, tb):
    seq = rq.shape[0]
    c = RET_CHUNK

    def body(q_ref, k_ref, v_ref, do_ref, rf_ref, rb_ref, hf_ref, hb_ref, d_ref, dda_ref, ddb_ref,
             wqf_ref, wqb_ref, wkf_ref, wkb_ref, cdec_ref, dq_ref, dk_ref, dv_ref, dlg_ref):
        @pl.when(pl.program_id(0) == 0)
        def _():
            dlg_ref[...] = jnp.zeros_like(dlg_ref)

        lane = _lane((c, 128))
        pos = lax.broadcasted_iota(jnp.int32, (c, 128), 0).astype(F32)
        for pr in range(2):
            sl = slice(128 * pr, 128 * (pr + 1))
            qp, kp = q_ref[:, sl], k_ref[:, sl]
            qf, kf = qp.astype(F32), kp.astype(F32)
            dq_acc = jnp.zeros((c, 128), F32)
            dk_acc = jnp.zeros((c, 128), F32)
            for e in range(2):
                h = 2 * pr + e
                half = (lane < 64) if e == 0 else (lane >= 64)
                hs = slice(128 * h, 128 * (h + 1))
                km = jnp.where(half, kp, jnp.zeros_like(kp))
                kmf = km.astype(F32)
                vh, doh = v_ref[:, hs], do_ref[:, hs]
                rfh, rbh, hfh, hbh = rf_ref[h], rb_ref[h], hf_ref[h], hb_ref[h]
                s = _nt(qp, km)
                dpm = _nt(doh, vh)
                ds_b = _mx(dpm * d_ref[h])
                sd_b = _mx(s * d_ref[h])
                dq_if = wqf_ref[h] * _nt(doh, rfh)
                dq_ib = wqb_ref[h] * _nt(doh, rbh)
                dq_acc = dq_acc + _nn(ds_b, km) + dq_if + dq_ib
                dk_sf = wkf_ref[h] * _nt(vh, hfh)
                dk_sb = wkb_ref[h] * _nt(vh, hbh)
                dk_acc = dk_acc + jnp.where(half, _tn(ds_b, qp), 0.0) + dk_sf + dk_sb
                dv = _tn(sd_b, doh) + _nn(_mx(kmf * wkf_ref[h]), hfh) + _nn(_mx(kmf * wkb_ref[h]), hbh)
                dv_ref[:, hs] = dv.astype(dv_ref.dtype)
                sdp = s * dpm
                hr_f = hfh.astype(F32) * rfh.astype(F32)
                hr_b = hbh.astype(F32) * rbh.astype(F32)
                da = (_rowsum128(sdp * dda_ref[h]) + _rowsum128((pos + 1.0) * qf * dq_if)
                      + _rowsum128((c - 1.0 - pos) * kf * dk_sf) + cdec_ref[h:h + 1, :] * _rowsum128(hr_f))
                db = (_rowsum128(sdp * ddb_ref[h]) + _rowsum128((c - pos) * qf * dq_ib)
                      + _rowsum128(pos * kf * dk_sb) + cdec_ref[4 + h:5 + h, :] * _rowsum128(hr_b))
                dlg_ref[h:h + 1, :] += da
                dlg_ref[4 + h:5 + h, :] += db
            dq_ref[:, sl] = dq_acc
            dk_ref[:, sl] = dk_acc

    st = pl.BlockSpec((None, RET_HEADS, 128, 128), lambda n: (n, 0, 0, 0))
    wsp = pl.BlockSpec((RET_HEADS, c, 128), lambda n: (0, 0, 0))
    dsp = pl.BlockSpec((RET_HEADS, c, c), lambda n: (0, 0, 0))
    x256 = pl.BlockSpec((c, 256), lambda n: (n, 0))
    x512 = pl.BlockSpec((c, 512), lambda n: (n, 0))
    cdec = tb["dec_fb"] * float(c)
    return pl.pallas_call(
        body, grid=(seq // c,), name="ret_bwd",
        in_specs=[x256, x256, x512, x512, st, st, st, st, dsp, dsp, dsp, wsp, wsp, wsp, wsp,
                  pl.BlockSpec((8, 128), lambda n: (0, 0))],
        out_specs=[x256, x256, x512, pl.BlockSpec((8, 128), lambda n: (0, 0))],
        out_shape=[jax.ShapeDtypeStruct((seq, 256), F32), jax.ShapeDtypeStruct((seq, 256), F32),
                   jax.ShapeDtypeStruct((seq, 512), MXU_DTYPE), jax.ShapeDtypeStruct((8, 128), F32)],
        compiler_params=_cp("arbitrary"))(
            rq, rk, rv, do, rf, rb, hf, hb, tb["dmat"], tb["dda"], tb["ddb"], tb["wqf"], tb["wqb"], tb["wkf"], tb["wkb"], cdec)


def _tail(x, tgt, o_att, o_ret, p, mod, g_post, gn_g, wpt, wout):
    seq = x.shape[0]
    bs = 256
    nb = seq // bs

    def body(x_ref, t_ref, oa_ref, or_ref, za_ref, zr_ref, gl_ref, mod_ref, gp_ref, gn_ref, wpt_ref, wout_ref,
             dout_ref, dza_ref, dzr_ref, dgl_ref, doa_ref, dor_ref, del_ref, gout_ref, gpt_ref,
             dgate_ref, dgpost_ref, dgn_ref, loss_ref, gout_acc, gpt_acc):
        i = pl.program_id(0)

        @pl.when(i == 0)
        def _():
            gout_acc[...] = jnp.zeros_like(gout_acc)
            gpt_acc[...] = jnp.zeros_like(gpt_acc)
            dgate_ref[...] = jnp.zeros_like(dgate_ref)
            dgpost_ref[...] = jnp.zeros_like(dgpost_ref)
            dgn_ref[...] = jnp.zeros_like(dgn_ref)
            loss_ref[...] = jnp.zeros_like(loss_ref)

        oa, za, zr = oa_ref[...], za_ref[...], zr_ref[...]
        sga, sgr = _sigmoid(za), _sigmoid(zr)
        sza, szr = za * sga, zr * sgr
        ya_b = _mx(oa * sza)
        ons, rstds = [], []
        for h in range(RET_HEADS):
            oh = or_ref[:, 128 * h:128 * (h + 1)]
            xc = oh - jnp.mean(oh, axis=-1, keepdims=True)
            rstd = lax.rsqrt(jnp.mean(xc * xc, axis=-1, keepdims=True) + EPS)
            ons.append(xc * rstd)
            rstds.append(rstd)
        on = jnp.concatenate(ons, axis=1)
        yn = on * gn_ref[...]
        yr_b = _mx(yn * szr)
        wpa_t, wpr_t = wpt_ref[:, 0:512], wpt_ref[:, 512:1024]
        a_att = _nt(ya_b, wpa_t)
        a_ret = _nt(yr_b, wpr_t)
        gts = _sigmoid(gl_ref[...])
        g_att, g_ret = gts[:, 0:D_MODEL], gts[:, D_MODEL:2 * D_MODEL]
        merged_b = _mx(g_att * a_att + g_ret * a_ret)
        u = _nn(merged_b, wout_ref[...])
        r = lax.rsqrt(jnp.mean(u * u, axis=-1, keepdims=True) + EPS)
        un = u * r
        gpost = gp_ref[...]
        y = un * gpost
        gate = mod_ref[:, 2 * D_MODEL:3 * D_MODEL]
        err = x_ref[...] + gate * y - t_ref[...]
        loss_ref[...] += (0.5 / D_MODEL) * _rowsum128(err * err)
        dout = err * (1.0 / D_MODEL)
        dout_ref[...] = dout
        dgate_ref[...] += jnp.sum(dout * y, axis=0, keepdims=True)
        dy = dout * gate
        dgpost_ref[...] += jnp.sum(dy * un, axis=0, keepdims=True)
        dun = dy * gpost
        du_b = _mx(r * (dun - un * jnp.mean(dun * un, axis=-1, keepdims=True)))
        dmerged = _nt(du_b, wout_ref[...])
        gout_acc[...] += _tn(merged_b, du_b)
        d_att, d_ret = dmerged * g_att, dmerged * g_ret
        dgl_ref[:, 0:D_MODEL] = (d_att * a_att * (1.0 - g_att)).astype(dgl_ref.dtype)
        dgl_ref[:, D_MODEL:2 * D_MODEL] = (d_ret * a_ret * (1.0 - g_ret)).astype(dgl_ref.dtype)
        d_att_b, d_ret_b = _mx(d_att), _mx(d_ret)
        dya = _nn(d_att_b, wpa_t)
        dyr = _nn(d_ret_b, wpr_t)
        gpt_acc[:, 0:512] += _tn(d_att_b, ya_b)
        gpt_acc[:, 512:1024] += _tn(d_ret_b, yr_b)
        dza_ref[...] = (dya * oa * (sga * (1.0 + za * (1.0 - sga)))).astype(dza_ref.dtype)
        doa = dya * sza
        doa_ref[...] = doa.astype(doa_ref.dtype)
        dt = jnp.transpose(doa * oa)
        del_ref[...] = jnp.sum(dt.reshape(8, HEAD_DIM, bs), axis=1)
        dzr_ref[...] = (dyr * yn * (sgr * (1.0 + zr * (1.0 - sgr)))).astype(dzr_ref.dtype)
        dyn = dyr * szr
        dgn_ref[...] += jnp.sum(dyn * on, axis=0, keepdims=True)
        don = dyn * gn_ref[...]
        for h in range(RET_HEADS):
            hs = slice(128 * h, 128 * (h + 1))
            dh, oh = don[:, hs], ons[h]
            doh = rstds[h] * (dh - jnp.mean(dh, axis=-1, keepdims=True) - oh * jnp.mean(dh * oh, axis=-1, keepdims=True))
            dor_ref[:, hs] = doh.astype(dor_ref.dtype)

        @pl.when(i == nb - 1)
        def _():
            gout_ref[...] = gout_acc[...].astype(gout_ref.dtype)
            gpt_ref[...] = gpt_acc[...].astype(gpt_ref.dtype)

    row = lambda w: pl.BlockSpec((bs, w), lambda i: (i, 0))
    el = lambda w, off: pl.BlockSpec((pl.Element(bs), pl.Element(w)), lambda i: (i * bs, off))
    vec = lambda w: pl.BlockSpec((1, w), lambda i: (0, 0))
    full = pl.BlockSpec((D_MODEL, D_MODEL), lambda i: (0, 0))
    sds = jax.ShapeDtypeStruct
    return pl.pallas_call(
        body, grid=(nb,), name="tail",
        in_specs=[row(D_MODEL), row(D_MODEL), row(512), row(512), el(512, ZA), el(512, ZR), el(2 * D_MODEL, GL),
                  vec(3 * D_MODEL), vec(D_MODEL), vec(512), full, full],
        out_specs=[row(D_MODEL), row(512), row(512), row(2 * D_MODEL), row(512), row(512),
                   pl.BlockSpec((8, bs), lambda i: (0, i)), full, full, vec(D_MODEL), vec(D_MODEL), vec(512), vec(128)],
        out_shape=[sds((seq, D_MODEL), F32), sds((seq, 512), MXU_DTYPE), sds((seq, 512), MXU_DTYPE),
                   sds((seq, 2 * D_MODEL), MXU_DTYPE), sds((seq, 512), MXU_DTYPE), sds((seq, 512), MXU_DTYPE),
                   sds((8, seq), F32), sds((D_MODEL, D_MODEL), MXU_DTYPE), sds((D_MODEL, D_MODEL), MXU_DTYPE),
                   sds((1, D_MODEL), F32), sds((1, D_MODEL), F32), sds((1, 512), F32), sds((1, 128), F32)],
        scratch_shapes=[pltpu.VMEM((D_MODEL, D_MODEL), F32), pltpu.VMEM((D_MODEL, D_MODEL), F32)],
        compiler_params=_cp("arbitrary"))(x, tgt, o_att, o_ret, p, p, p, mod, g_post, gn_g, wpt, wout)


def _assemble(p, cos, sin, qg2, kg2, dqt, dkp, dvp, dza, drq, drk, drv, dzr, dgl):
    seq = p.shape[0]
    bs = 512

    def body(p_ref, cos_ref, sin_ref, qg_ref, kg_ref, dqt_ref, dkp_ref, dvp_ref, dza_ref, drq_ref, drk_ref, drv_ref,
             dzr_ref, dgl_ref, dp_ref, dqg_ref, dkg_ref):
        @pl.when(pl.program_id(0) == 0)
        def _():
            dqg_ref[...] = jnp.zeros_like(dqg_ref)
            dkg_ref[...] = jnp.zeros_like(dkg_ref)

        m = _blockdiag64()
        cs, sn = cos_ref[...], sin_ref[...]
        lo = _lane((bs, 128)) < 64

        def norm_bwd(raw, dn, g, dg_ref):
            r = lax.rsqrt(_seg64(raw * raw, m) * (1.0 / 64) + EPS)
            xn = raw * r
            dg_ref[...] += jnp.sum(dn * xn, axis=0, keepdims=True)
            dxn = dn * g
            return r * (dxn - xn * (_seg64(dxn * xn, m) * (1.0 / 64)))

        for pr in range(4):
            sl = slice(128 * pr, 128 * (pr + 1))
            dn = _rope_t(dqt_ref[:, sl] * 0.125, cs, sn)
            dp_ref[:, QA + 128 * pr:QA + 128 * (pr + 1)] = norm_bwd(p_ref[:, sl], dn, qg_ref[...], dqg_ref).astype(dp_ref.dtype)
        fold = lambda a: a + pltpu.roll(a, 64, 1)
        dk = jnp.where(lo, fold(dkp_ref[0]), fold(dkp_ref[1]))
        dp_ref[:, KA:KA + 128] = norm_bwd(p_ref[:, KA:KA + 128], _rope_t(dk, cs, sn), kg_ref[...], dkg_ref).astype(dp_ref.dtype)
        dp_ref[:, VA:VA + 128] = jnp.where(lo, fold(dvp_ref[0]), fold(dvp_ref[1])).astype(dp_ref.dtype)
        dp_ref[:, ZA:ZA + 512] = dza_ref[...]
        for pr in range(2):
            sl = slice(128 * pr, 128 * (pr + 1))
            dp_ref[:, QR + 128 * pr:QR + 128 * (pr + 1)] = _rope_t(drq_ref[:, sl], cs, sn).astype(dp_ref.dtype)
            dp_ref[:, KR + 128 * pr:KR + 128 * (pr + 1)] = _rope_t(drk_ref[:, sl] * 0.125, cs, sn).astype(dp_ref.dtype)
        dp_ref[:, VR:VR + 512] = drv_ref[...]
        dp_ref[:, ZR:ZR + 512] = dzr_ref[...]
        dp_ref[:, GL:GL + 2 * D_MODEL] = dgl_ref[...]

    row = lambda w: pl.BlockSpec((bs, w), lambda i: (i, 0))
    gsp = pl.BlockSpec((1, 128), lambda i: (0, 0))
    dup = pl.BlockSpec((2, bs, 128), lambda i: (0, i, 0))
    return pl.pallas_call(
        body, grid=(seq // bs,), name="assemble_dp",
        in_specs=[row(768), row(128), row(128), gsp, gsp, row(512), dup, dup, row(512), row(256), row(256), row(512),
                  row(512), row(2 * D_MODEL)],
        out_specs=[row(IN_WIDTH), gsp, gsp],
        out_shape=[jax.ShapeDtypeStruct((seq, IN_WIDTH), MXU_DTYPE), jax.ShapeDtypeStruct((1, 128), F32),
                   jax.ShapeDtypeStruct((1, 128), F32)],
        compiler_params=_cp("arbitrary"))(p, cos, sin, qg2, kg2, dqt, dkp, dvp, dza, drq, drk, drv, dzr, dgl)


def _local_step(x, tgt, mod, g_pre, qn_g, kn_g, w_dec_f, w_dec_b, gn_g, g_post, win_t, wpt, wout):
    seq = x.shape[0]
    cos, sin = _rope_tables(seq)
    qg2, kg2 = jnp.tile(qn_g, (1, 2)), jnp.tile(kn_g, (1, 2))
    lg_f = jax.nn.log_sigmoid(w_dec_f[0])
    lg_b = jax.nn.log_sigmoid(w_dec_b[0])
    tb = _ret_tables(lg_f, lg_b, RET_CHUNK)

    h = _prenorm(x, mod, g_pre)
    p = _matmul(h, win_t, ta=False, tb=True, tm=512, tn=IN_WIDTH // 2, out_dtype=F32, name="in_proj")
    qt, kd, vd, rq, rk, rv = _prep(p, cos, sin, qg2, kg2)
    o_att, lse = _attn_fwd(qt, kd, vd)
    rf, rb = _ret_states(rk, rv, tb["wkf"], tb["wkb"], tb["dec_fb"], "ret_states_fwd")
    o_ret = _ret_fwd(rq, rk, rv, rf, rb, tb)
    (dout, dza, dzr, dgl, do_att, do_ret, delta, g_out, g_pt, dgate, dgpost, dgn, loss_l) = _tail(
        x, tgt, o_att, o_ret, p, mod, g_post, gn_g, wpt, wout)
    dqt, dkp, dvp = _attn_bwd(qt, kd, vd, do_att, lse, delta)
    hb, hf = _ret_states(rq, do_ret, tb["wqb"], tb["wqf"], tb["dec_bf"], "ret_states_bwd")
    drq, drk, drv, dlg = _ret_bwd(rq, rk, rv, do_ret, rf, rb, hf, hb, tb)
    dp, dqg, dkg = _assemble(p, cos, sin, qg2, kg2, dqt, dkp, dvp, dza, drq, drk, drv, dzr, dgl)
    dh = _matmul(dp, win_t, ta=False, tb=False, tm=512, tn=D_MODEL, out_dtype=F32, name="in_proj_dx")
    g_in_t = _matmul(dp, h, ta=True, tb=False, tm=256, tn=D_MODEL, out_dtype=MXU_DTYPE, name="in_proj_dw")
    grad_x, dshift, dscale, dgpre = _prenorm_bwd(x, dh, dout, mod, g_pre)

    dlg = jnp.sum(dlg, axis=1)
    small = dict(
        dmod=jnp.concatenate([dshift, dscale, dgate], axis=1),
        g_pre=dgpre, g_post=dgpost, gn_g=dgn,
        qn_g=dqg[:, :64] + dqg[:, 64:], kn_g=dkg[:, :64] + dkg[:, 64:],
        w_dec_f=(dlg[0:4] * jax.nn.sigmoid(-w_dec_f[0]))[None], w_dec_b=(dlg[4:8] * jax.nn.sigmoid(-w_dec_b[0]))[None],
        loss=jnp.sum(loss_l, axis=1, keepdims=True))
    return grad_x, g_in_t, g_pt, g_out, small


N_DEV = 8
N_CHIP = 4
HBM_SPEC = pl.BlockSpec(memory_space=pl.ANY)
VMEM_SPEC = pl.BlockSpec(memory_space=pltpu.VMEM)
SHARD_ROWS = (IN_WIDTH // N_CHIP, D_MODEL // N_CHIP, D_MODEL // N_CHIP)


def _coords():
    return lax.axis_index("x"), lax.axis_index("y"), lax.axis_index("c")


def _peer(k):
    x, y, c = _coords()
    return (1 - x if k & 4 else x, 1 - y if k & 2 else y, 1 - c if k & 1 else c)


def _dev_index(p):
    return 4 * p[0] + 2 * p[1] + p[2]


def _rows(ref, start, size):
    return ref.at[pl.ds(pl.multiple_of(start, 16), size), :]


def _remote(src, dst, ssem, rsem, dev):
    return pltpu.make_async_remote_copy(src_ref=src, dst_ref=dst, send_sem=ssem, recv_sem=rsem, device_id=dev,
                                        device_id_type=MESH)


def _mod_exchange(c, w_ada_s, b4):
    ncol = w_ada_s.shape[1]

    def body(c_ref, w_ref, b_ref, cs_ref, mod_ref, modsh, modall, ssem, rsem):
        me = _dev_index(_coords())
        chip = me // 2
        cs_ref[me] = c_ref[...]
        sends = []
        for k in range(1, N_DEV):
            cp = _remote(c_ref, cs_ref.at[me], ssem.at[k - 1], rsem.at[k - 1], _peer(k))
            cp.start()
            sends.append(cp)
        for k in range(1, N_DEV):
            slot = cs_ref.at[_dev_index(_peer(k))]
            _remote(slot, slot, ssem.at[k - 1], rsem.at[k - 1], _peer(k)).wait_recv()
        cs = jnp.concatenate([cs_ref[d] for d in range(N_DEV)], axis=0)
        ms = _nn(cs * _sigmoid(cs), w_ref[...], precision=HIGHEST) + b_ref[pl.ds(chip, 1), :]
        modsh[...] = ms
        modall[chip] = ms
        for k2 in range(1, N_CHIP):
            cp = _remote(modsh, modall.at[chip], ssem.at[6 + k2], rsem.at[6 + k2], _peer(2 * k2))
            cp.start()
            sends.append(cp)
        for k2 in range(1, N_CHIP):
            slot = modall.at[_dev_index(_peer(2 * k2)) // 2]
            _remote(slot, slot, ssem.at[6 + k2], rsem.at[6 + k2], _peer(2 * k2)).wait_recv()
        for jj in range(N_CHIP):
            mod_ref[:, ncol * jj:ncol * (jj + 1)] = modall[jj, pl.ds(me, 1), :]
        for cp in sends:
            cp.wait_send()

    return pl.pallas_call(
        body, name="mod_exchange", in_specs=[VMEM_SPEC] * 3, out_specs=[VMEM_SPEC] * 2,
        out_shape=[jax.ShapeDtypeStruct((N_DEV, 1, D_MODEL), F32), jax.ShapeDtypeStruct((1, N_CHIP * ncol), F32)],
        scratch_shapes=[pltpu.VMEM((N_DEV, ncol), F32), pltpu.VMEM((N_CHIP, N_DEV, ncol), F32),
                        pltpu.SemaphoreType.DMA((10,)), pltpu.SemaphoreType.DMA((10,))],
        compiler_params=_cp())(c, w_ada_s, b4)


def _weight_gather(shards):
    nt = len(shards)

    def body(*refs):
        srcs, outs = refs[:nt], refs[nt:2 * nt]
        lsem, ssem, rsem = refs[2 * nt:]
        x, y, c = _coords()
        chip = 2 * x + y
        sib = (x, y, 1 - c)
        half = lambda t, ch, cc: _rows(outs[t], ch * SHARD_ROWS[t] + cc * (SHARD_ROWS[t] // 2), SHARD_ROWS[t] // 2)
        local = [pltpu.make_async_copy(srcs[t], _rows(outs[t], chip * SHARD_ROWS[t], SHARD_ROWS[t]), lsem.at[t]) for t in range(nt)]
        for cp in local:
            cp.start()
        sends = []
        for k2 in range(1, N_CHIP):
            for t in range(nt):
                i = (k2 - 1) * nt + t
                cp = _remote(_rows(srcs[t], c * (SHARD_ROWS[t] // 2), SHARD_ROWS[t] // 2), half(t, chip, c),
                             ssem.at[i], rsem.at[i], _peer(2 * k2))
                cp.start()
                sends.append(cp)
        for k2 in range(1, N_CHIP):
            pchip = _dev_index(_peer(2 * k2)) // 2
            for t in range(nt):
                i = (k2 - 1) * nt + t
                got = half(t, pchip, c)
                _remote(got, got, ssem.at[i], rsem.at[i], _peer(2 * k2)).wait_recv()
                cp = _remote(got, got, ssem.at[3 * nt + i], rsem.at[3 * nt + i], sib)
                cp.start()
                sends.append(cp)
        for k2 in range(1, N_CHIP):
            pchip = _dev_index(_peer(2 * k2)) // 2
            for t in range(nt):
                i = (k2 - 1) * nt + t
                got = half(t, pchip, 1 - c)
                _remote(got, got, ssem.at[3 * nt + i], rsem.at[3 * nt + i], sib).wait_recv()
        for cp in sends:
            cp.wait_send()
        for cp in local:
            cp.wait()

    return pl.pallas_call(
        body, name="weight_gather", in_specs=[HBM_SPEC] * nt, out_specs=[HBM_SPEC] * nt,
        out_shape=[jax.ShapeDtypeStruct((N_CHIP * s.shape[0], s.shape[1]), s.dtype) for s in shards],
        scratch_shapes=[pltpu.SemaphoreType.DMA((nt,)), pltpu.SemaphoreType.DMA((6 * nt,)), pltpu.SemaphoreType.DMA((6 * nt,))],
        compiler_params=_cp())(*shards)


def _grad_scatter(grads):
    nt = len(grads)

    def body(*refs):
        srcs, outs = refs[:nt], refs[nt:2 * nt]
        lsem, ssem, rsem = refs[2 * nt:]
        me = _dev_index(_coords())
        mine = lambda t, dev: _rows(srcs[t], (dev // 2) * SHARD_ROWS[t] + (dev % 2) * (SHARD_ROWS[t] // 2), SHARD_ROWS[t] // 2)
        local = [pltpu.make_async_copy(mine(t, me), outs[t].at[me], lsem.at[t]) for t in range(nt)]
        for cp in local:
            cp.start()
        sends = []
        for k in range(1, N_DEV):
            for t in range(nt):
                i = (k - 1) * nt + t
                cp = _remote(mine(t, _dev_index(_peer(k))), outs[t].at[me], ssem.at[i], rsem.at[i], _peer(k))
                cp.start()
                sends.append(cp)
        for k in range(1, N_DEV):
            for t in range(nt):
                i = (k - 1) * nt + t
                slot = outs[t].at[_dev_index(_peer(k))]
                _remote(slot, slot, ssem.at[i], rsem.at[i], _peer(k)).wait_recv()
        for cp in sends:
            cp.wait_send()
        for cp in local:
            cp.wait()

    return pl.pallas_call(
        body, name="grad_scatter", in_specs=[HBM_SPEC] * nt, out_specs=[HBM_SPEC] * nt,
        out_shape=[jax.ShapeDtypeStruct((N_DEV, SHARD_ROWS[t] // 2, g.shape[1]), g.dtype) for t, g in enumerate(grads)],
        scratch_shapes=[pltpu.SemaphoreType.DMA((nt,)), pltpu.SemaphoreType.DMA((7 * nt,)), pltpu.SemaphoreType.DMA((7 * nt,))],
        compiler_params=_cp())(*grads)


def _sum_slots(slots):
    nt = len(slots)
    steps = 2

    def body(*refs):
        for r_ref, o_ref in zip(refs[:nt], refs[nt:]):
            acc = r_ref[0].astype(F32)
            for d in range(1, N_DEV):
                acc = acc + r_ref[d].astype(F32)
            o_ref[...] = acc

    return pl.pallas_call(
        body, grid=(steps,), name="grad_sum",
        in_specs=[pl.BlockSpec((N_DEV, s.shape[1] // steps, s.shape[2]), lambda i: (0, i, 0)) for s in slots],
        out_specs=[pl.BlockSpec((s.shape[1] // steps, s.shape[2]), lambda i: (i, 0)) for s in slots],
        out_shape=[jax.ShapeDtypeStruct(s.shape[1:], F32) for s in slots], compiler_params=_cp("parallel"))(*slots)


def _sibling_exchange(halves):
    nt = len(halves)

    def body(*refs):
        srcs, outs = refs[:nt], refs[nt:2 * nt]
        lsem, ssem, rsem = refs[2 * nt:]
        x, y, c = _coords()
        sib = (x, y, 1 - c)
        place = lambda t, cc: _rows(outs[t], cc * (SHARD_ROWS[t] // 2), SHARD_ROWS[t] // 2)
        local = [pltpu.make_async_copy(srcs[t], place(t, c), lsem.at[t]) for t in range(nt)]
        sends = [_remote(srcs[t], place(t, c), ssem.at[t], rsem.at[t], sib) for t in range(nt)]
        for cp in local + sends:
            cp.start()
        for t in range(nt):
            _remote(place(t, 1 - c), place(t, 1 - c), ssem.at[t], rsem.at[t], sib).wait_recv()
        for cp in sends:
            cp.wait_send()
        for cp in local:
            cp.wait()

    return pl.pallas_call(
        body, name="grad_sibling_exchange", in_specs=[HBM_SPEC] * nt, out_specs=[HBM_SPEC] * nt,
        out_shape=[jax.ShapeDtypeStruct((2 * h.shape[0], h.shape[1]), h.dtype) for h in halves],
        scratch_shapes=[pltpu.SemaphoreType.DMA((nt,)), pltpu.SemaphoreType.DMA((nt,)), pltpu.SemaphoreType.DMA((nt,))],
        compiler_params=_cp())(*halves)


def _adam_math(w, g, m, v):
    m = ADAM_B1 * m + (1.0 - ADAM_B1) * g
    v = ADAM_B2 * v + (1.0 - ADAM_B2) * (g * g)
    m_hat = m / (1.0 - ADAM_B1 ** ADAM_STEP)
    v_hat = v / (1.0 - ADAM_B2 ** ADAM_STEP)
    return -ADAM_LR * (m_hat / (jnp.sqrt(v_hat) + ADAM_EPS) + ADAM_WD * w), m, v


def _adamw(w, g, m, v, rb, name):
    rows, cols = w.shape

    def body(w_ref, g_ref, m_ref, v_ref, d_ref, nm_ref, nv_ref):
        d_ref[...], nm_ref[...], nv_ref[...] = _adam_math(w_ref[...], g_ref[...], m_ref[...], v_ref[...])

    spec = pl.BlockSpec((rb, cols), lambda i: (i, 0))
    return pl.pallas_call(body, grid=(rows // rb,), name=name, in_specs=[spec] * 4, out_specs=[spec] * 3,
                          out_shape=[jax.ShapeDtypeStruct(w.shape, F32)] * 3, compiler_params=_cp("parallel"))(w, g, m, v)


PACK = (("b_ada", 3 * D_MODEL), ("g_pre", D_MODEL), ("g_post", D_MODEL), ("gn_g", 512), ("qn_g", 64), ("kn_g", 64),
        ("w_dec_f", 4), ("w_dec_b", 4), ("loss", 1))
PACK_WIDTH = 6144


def _pack(parts):
    used = sum(n for _, n in PACK)
    cols = [parts[name].reshape(1, n).astype(F32) if name in parts else jnp.zeros((1, n), F32) for name, n in PACK]
    return jnp.concatenate(cols + [jnp.zeros((1, PACK_WIDTH - used), F32)], axis=1)


def _unpack(row):
    out, off = {}, 0
    for name, n in PACK:
        out[name] = row[:, off:off + n]
        off += n
    return out


def _small_reduce(vec, wvec, mvec, vvec, cs):
    ncol = 3 * D_MODEL // N_CHIP

    def body(vec_ref, w_ref, m_ref, v_ref, cs_ref, gsum_ref, d_ref, nm_ref, nv_ref, gwa_ref, gat, ssem, rsem):
        me = _dev_index(_coords())
        chip = me // 2
        gat[me] = vec_ref[...]
        sends = []
        for k in range(1, N_DEV):
            cp = _remote(vec_ref, gat.at[me], ssem.at[k - 1], rsem.at[k - 1], _peer(k))
            cp.start()
            sends.append(cp)
        for k in range(1, N_DEV):
            slot = gat.at[_dev_index(_peer(k))]
            _remote(slot, slot, ssem.at[k - 1], rsem.at[k - 1], _peer(k)).wait_recv()
        rows = [gat[d] for d in range(N_DEV)]
        tot = rows[0]
        for d in range(1, N_DEV):
            tot = tot + rows[d]
        gsum_ref[...] = tot
        d_ref[...], nm_ref[...], nv_ref[...] = _adam_math(w_ref[...], tot, m_ref[...], v_ref[...])
        cs = cs_ref[...]
        ca_t = jnp.transpose(jnp.concatenate([cs * _sigmoid(cs), jnp.zeros((128 - N_DEV, D_MODEL), F32)], axis=0))
        dm = jnp.concatenate(rows + [jnp.zeros((128 - N_DEV, PACK_WIDTH), F32)], axis=0)
        for jj in range(N_CHIP):
            @pl.when(chip == jj)
            def _():
                gwa_ref[...] = _nn(ca_t, dm[:, ncol * jj:ncol * (jj + 1)], precision=HIGHEST)
        for cp in sends:
            cp.wait_send()

    row = jax.ShapeDtypeStruct((1, PACK_WIDTH), F32)
    return pl.pallas_call(
        body, name="small_reduce", in_specs=[VMEM_SPEC] * 5, out_specs=[VMEM_SPEC] * 5,
        out_shape=[row, row, row, row, jax.ShapeDtypeStruct((D_MODEL, ncol), F32)],
        scratch_shapes=[pltpu.VMEM((N_DEV, 1, PACK_WIDTH), F32), pltpu.SemaphoreType.DMA((7,)), pltpu.SemaphoreType.DMA((7,))],
        compiler_params=_cp())(vec, wvec, mvec, vvec, cs)


def kernel(x, c, w_ada, b_ada, g_pre, w_in, qn_g, kn_g, w_dec_f, w_dec_b, gn_g, w_pa, w_pr, w_out, g_post, loss_target, m_w_ada, m_b_ada, m_g_pre, m_w_in, m_qn_g, m_kn_g, m_w_dec_f, m_w_dec_b, m_gn_g, m_w_pa, m_w_pr, m_w_out, m_g_post, v_w_ada, v_b_ada, v_g_pre, v_w_in, v_qn_g, v_kn_g, v_w_dec_f, v_w_dec_b, v_gn_g, v_w_pa, v_w_pr, v_w_out, v_g_post):
    shards = (w_in[0].T.astype(MXU_DTYPE), jnp.concatenate([w_pa[0].T, w_pr[0].T], axis=1).astype(MXU_DTYPE),
              w_out[0].astype(MXU_DTYPE))
    win_t, wpt, wout = _weight_gather(shards)
    cs, mod = _mod_exchange(c, w_ada[0], b_ada.reshape(N_CHIP, 3 * D_MODEL // N_CHIP))

    grad_x, g_in_t, g_pt, g_out, small = _local_step(x[0], loss_target[0], mod, g_pre, qn_g, kn_g, w_dec_f, w_dec_b,
                                                     gn_g, g_post, win_t, wpt, wout)

    full = _sibling_exchange(_sum_slots(_grad_scatter((g_in_t, g_pt, g_out))))
    big = {"w_in": (w_in, m_w_in, v_w_in, full[0].T), "w_pa": (w_pa, m_w_pa, v_w_pa, full[1][:, :512].T),
           "w_pr": (w_pr, m_w_pr, v_w_pr, full[1][:, 512:].T), "w_out": (w_out, m_w_out, v_w_out, full[2])}

    small = dict(small)
    small["b_ada"] = small.pop("dmod")
    given = dict(b_ada=(b_ada, m_b_ada, v_b_ada), g_pre=(g_pre, m_g_pre, v_g_pre), g_post=(g_post, m_g_post, v_g_post),
                 gn_g=(gn_g, m_gn_g, v_gn_g), qn_g=(qn_g, m_qn_g, v_qn_g), kn_g=(kn_g, m_kn_g, v_kn_g),
                 w_dec_f=(w_dec_f, m_w_dec_f, v_w_dec_f), w_dec_b=(w_dec_b, m_w_dec_b, v_w_dec_b))
    packs = [_pack({n: t[i] for n, t in given.items()}) for i in range(3)]
    gsum, dvec, nmvec, nvvec, g_w_ada = _small_reduce(_pack(small), *packs, cs.reshape(N_DEV, D_MODEL))
    big["w_ada"] = (w_ada, m_w_ada, v_w_ada, g_w_ada)

    grads, deltas, new_m, new_v = {}, {}, {}, {}
    for name, (w, m, v, g) in big.items():
        d, nm, nv = _adamw(w[0], g, m[0], v[0], 256, "adamw_" + name)
        grads[name], deltas[name], new_m[name], new_v[name] = g[None], d[None], nm[None], nv[None]
    for dst, row in ((grads, gsum), (deltas, dvec), (new_m, nmvec), (new_v, nvvec)):
        dst.update({n: a for n, a in _unpack(row).items() if n != "loss"})
    order = ("w_ada", "b_ada", "g_pre", "w_in", "qn_g", "kn_g", "w_dec_f", "w_dec_b", "gn_g", "w_pa", "w_pr", "w_out", "g_post")
    loss = _unpack(gsum)["loss"][0, 0]
    return (loss, grad_x[None], *[grads[n] for n in order], *[deltas[n] for n in order],
            *[new_m[n] for n in order], *[new_v[n] for n in order])
```

```python
import functools

import jax
import jax.numpy as jnp
from jax import lax
from jax.experimental import pallas as pl
from jax.experimental.pallas import tpu as pltpu

F32 = jnp.float32
BF16 = jnp.bfloat16
MXU_DTYPE = jnp.bfloat16

D_MODEL = 1024
GRID_W = 64
HEAD_DIM = 64
ROPE_THETA = 10000.0
EPS = 1e-6
RET_HEADS = 4
RET_CHUNK = 256
IN_WIDTH = 4864
QA, KA, VA, ZA, QR, KR, VR, ZR, GL = 0, 512, 640, 768, 1280, 1536, 1792, 2304, 2816

ADAM_LR, ADAM_B1, ADAM_B2, ADAM_EPS, ADAM_WD, ADAM_STEP = 0.001, 0.9, 0.999, 1e-08, 0.01, 10

VMEM_LIMIT = 56 * 1024 * 1024
MESH = pl.DeviceIdType.MESH
HIGHEST = lax.Precision.HIGHEST
LOG2E = 1.4426950408889634
LN2 = 0.6931471805599453


def _cp(*sem, **kw):
    if sem:
        kw["dimension_semantics"] = sem
    return pltpu.CompilerParams(vmem_limit_bytes=VMEM_LIMIT, **kw)


def _nn(a, b, **kw):
    return lax.dot_general(a, b, (((1,), (0,)), ((), ())), preferred_element_type=F32, **kw)


def _nt(a, b):
    return lax.dot_general(a, b, (((1,), (1,)), ((), ())), preferred_element_type=F32)


def _tn(a, b):
    return lax.dot_general(a, b, (((0,), (0,)), ((), ())), preferred_element_type=F32)


def _mx(x):
    return x.astype(MXU_DTYPE)


def _lane(shape):
    return lax.broadcasted_iota(jnp.int32, shape, 1)


def _sigmoid(z):
    return 1.0 / (1.0 + jnp.exp(-z))


def _rowsum128(x):
    s = jnp.sum(x, axis=0, keepdims=True)
    out = s[:, 0:128]
    for k in range(1, x.shape[1] // 128):
        out = out + s[:, 128 * k:128 * (k + 1)]
    return out


def _swap16(x):
    return jnp.where((_lane(x.shape) & 16) == 0, pltpu.roll(x, 112, 1), pltpu.roll(x, 16, 1))


def _rope(x, cos, sin):
    return x * cos + _swap16(x) * sin


def _rope_t(d, cos, sin):
    return d * cos + _swap16(d * sin)


def _blockdiag64():
    r = lax.broadcasted_iota(jnp.int32, (128, 128), 0) // 64
    c = lax.broadcasted_iota(jnp.int32, (128, 128), 1) // 64
    return (r == c).astype(BF16)


def _seg64(x, m):
    hi = x.astype(BF16)
    lo = (x - hi.astype(F32)).astype(BF16)
    return _nn(hi, m) + _nn(lo, m)


def _rope_tables(seq):
    t = jnp.arange(seq)
    row = (t // GRID_W).astype(F32)
    col = (t % GRID_W).astype(F32)
    half = HEAD_DIM // 2
    inv_freq = ROPE_THETA ** (-jnp.arange(0, half, 2, dtype=F32) / half)
    ar = row[:, None] * inv_freq[None, :]
    ac = col[:, None] * inv_freq[None, :]
    cr, sr, cc, sc = jnp.cos(ar), jnp.sin(ar), jnp.cos(ac), jnp.sin(ac)
    cos64 = jnp.concatenate([cr, cr, cc, cc], axis=1)
    sin64 = jnp.concatenate([-sr, sr, -sc, sc], axis=1)
    return jnp.tile(cos64, (1, 2)), jnp.tile(sin64, (1, 2))


def _prenorm(x, mod, g_pre):
    seq = x.shape[0]
    bs = 512

    def body(x_ref, mod_ref, g_ref, h_ref):
        xv = x_ref[...]
        r = lax.rsqrt(jnp.mean(xv * xv, axis=-1, keepdims=True) + EPS)
        shift = mod_ref[:, 0:D_MODEL]
        scale = mod_ref[:, D_MODEL:2 * D_MODEL]
        h_ref[...] = ((xv * r) * g_ref[...] * (1.0 + scale) + shift).astype(h_ref.dtype)

    return pl.pallas_call(
        body, grid=(seq // bs,), name="prenorm",
        in_specs=[pl.BlockSpec((bs, D_MODEL), lambda i: (i, 0)), pl.BlockSpec((1, 3 * D_MODEL), lambda i: (0, 0)),
                  pl.BlockSpec((1, D_MODEL), lambda i: (0, 0))],
        out_specs=pl.BlockSpec((bs, D_MODEL), lambda i: (i, 0)),
        out_shape=jax.ShapeDtypeStruct((seq, D_MODEL), MXU_DTYPE), compiler_params=_cp("parallel"))(x, mod, g_pre)


def _prenorm_bwd(x, dh, dout, mod, g_pre):
    seq = x.shape[0]
    bs = 512

    def body(x_ref, dh_ref, dout_ref, mod_ref, g_ref, gx_ref, dshift_ref, dscale_ref, dg_ref):
        @pl.when(pl.program_id(0) == 0)
        def _():
            dshift_ref[...] = jnp.zeros_like(dshift_ref)
            dscale_ref[...] = jnp.zeros_like(dscale_ref)
            dg_ref[...] = jnp.zeros_like(dg_ref)

        xv = x_ref[...]
        dh = dh_ref[...]
        g = g_ref[...]
        r = lax.rsqrt(jnp.mean(xv * xv, axis=-1, keepdims=True) + EPS)
        xn = xv * r
        scale = mod_ref[:, D_MODEL:2 * D_MODEL]
        dshift_ref[...] += jnp.sum(dh, axis=0, keepdims=True)
        dscale_ref[...] += jnp.sum(dh * (xn * g), axis=0, keepdims=True)
        da = dh * (1.0 + scale)
        dg_ref[...] += jnp.sum(da * xn, axis=0, keepdims=True)
        dxn = da * g
        dx = r * (dxn - xn * jnp.mean(dxn * xn, axis=-1, keepdims=True))
        gx_ref[...] = dout_ref[...] + dx

    row = pl.BlockSpec((bs, D_MODEL), lambda i: (i, 0))
    vec = pl.BlockSpec((1, D_MODEL), lambda i: (0, 0))
    return pl.pallas_call(
        body, grid=(seq // bs,), name="prenorm_bwd",
        in_specs=[row, row, row, pl.BlockSpec((1, 3 * D_MODEL), lambda i: (0, 0)), vec],
        out_specs=[row, vec, vec, vec],
        out_shape=[jax.ShapeDtypeStruct((seq, D_MODEL), F32)] + [jax.ShapeDtypeStruct((1, D_MODEL), F32)] * 3,
        compiler_params=_cp("arbitrary"))(x, dh, dout, mod, g_pre)


def _matmul(a, b, *, ta, tb, tm, tn, out_dtype, name):
    kdim = a.shape[0] if ta else a.shape[1]
    m = a.shape[1] if ta else a.shape[0]
    n = b.shape[0] if tb else b.shape[1]
    assert m % tm == 0 and n % tn == 0

    def body(a_ref, b_ref, o_ref):
        dims = (((0 if ta else 1,), (1 if tb else 0,)), ((), ()))
        o_ref[...] = lax.dot_general(a_ref[...], b_ref[...], dims, preferred_element_type=F32).astype(o_ref.dtype)

    a_spec = pl.BlockSpec((kdim, tm), lambda j, i: (0, i)) if ta else pl.BlockSpec((tm, kdim), lambda j, i: (i, 0))
    b_spec = pl.BlockSpec((tn, kdim), lambda j, i: (j, 0)) if tb else pl.BlockSpec((kdim, tn), lambda j, i: (0, j))
    return pl.pallas_call(
        body, grid=(n // tn, m // tm), name=name, in_specs=[a_spec, b_spec],
        out_specs=pl.BlockSpec((tm, tn), lambda j, i: (i, j)),
        out_shape=jax.ShapeDtypeStruct((m, n), out_dtype), compiler_params=_cp("parallel", "parallel"))(a, b)


def _prep(p, cos, sin, qg2, kg2):
    seq = p.shape[0]
    bs = 512

    def body(p_ref, cos_ref, sin_ref, qg_ref, kg_ref, qt_ref, kd_ref, vd_ref, rq_ref, rk_ref, rv_ref):
        m = _blockdiag64()
        cs, sn = cos_ref[...], sin_ref[...]
        lo = _lane((bs, 128)) < 64
        for pr in range(4):
            q = p_ref[:, QA + 128 * pr:QA + 128 * (pr + 1)]
            r = lax.rsqrt(_seg64(q * q, m) * (1.0 / 64) + EPS)
            qt_ref[:, 128 * pr:128 * (pr + 1)] = (_rope(q * r * qg_ref[...], cs, sn) * (0.125 * LOG2E)).astype(qt_ref.dtype)
        k = p_ref[:, KA:KA + 128]
        r = lax.rsqrt(_seg64(k * k, m) * (1.0 / 64) + EPS)
        kr = _rope(k * r * kg_ref[...], cs, sn)
        ksw = pltpu.roll(kr, 64, 1)
        kd_ref[0] = jnp.where(lo, kr, ksw).astype(kd_ref.dtype)
        kd_ref[1] = jnp.where(lo, ksw, kr).astype(kd_ref.dtype)
        v = p_ref[:, VA:VA + 128]
        vsw = pltpu.roll(v, 64, 1)
        vd_ref[0] = jnp.where(lo, v, vsw).astype(vd_ref.dtype)
        vd_ref[1] = jnp.where(lo, vsw, v).astype(vd_ref.dtype)
        for pr in range(2):
            sl = slice(128 * pr, 128 * (pr + 1))
            rq_ref[:, sl] = _rope(p_ref[:, QR + 128 * pr:QR + 128 * (pr + 1)], cs, sn).astype(rq_ref.dtype)
            rk_ref[:, sl] = (_rope(p_ref[:, KR + 128 * pr:KR + 128 * (pr + 1)], cs, sn) * 0.125).astype(rk_ref.dtype)
        rv_ref[...] = p_ref[:, VR:VR + 512].astype(rv_ref.dtype)

    tab = pl.BlockSpec((bs, 128), lambda i: (i, 0))
    gsp = pl.BlockSpec((1, 128), lambda i: (0, 0))
    dup = pl.BlockSpec((2, bs, 128), lambda i: (0, i, 0))
    return pl.pallas_call(
        body, grid=(seq // bs,), name="mixer_prep",
        in_specs=[pl.BlockSpec((bs, ZR), lambda i: (i, 0)), tab, tab, gsp, gsp],
        out_specs=[pl.BlockSpec((bs, 512), lambda i: (i, 0)), dup, dup, pl.BlockSpec((bs, 256), lambda i: (i, 0)),
                   pl.BlockSpec((bs, 256), lambda i: (i, 0)), pl.BlockSpec((bs, 512), lambda i: (i, 0))],
        out_shape=[jax.ShapeDtypeStruct((seq, 512), MXU_DTYPE), jax.ShapeDtypeStruct((2, seq, 128), MXU_DTYPE),
                   jax.ShapeDtypeStruct((2, seq, 128), MXU_DTYPE), jax.ShapeDtypeStruct((seq, 256), MXU_DTYPE),
                   jax.ShapeDtypeStruct((seq, 256), MXU_DTYPE), jax.ShapeDtypeStruct((seq, 512), MXU_DTYPE)],
        compiler_params=_cp("parallel"))(p, cos, sin, qg2, kg2)


def _halves(kd):
    lo = _lane(kd.shape) < 64
    z = jnp.zeros_like(kd)
    return jnp.concatenate([jnp.where(lo, kd, z), jnp.where(lo, z, kd)], axis=0)


def _attn_fwd(qt, kd, vd):
    seq = qt.shape[0]
    bq = bk = 512
    nk = seq // bk

    def body(q_ref, k_ref, v_ref, o_ref, lse_ref, m_scr, l_scr, acc):
        j, n = pl.program_id(1), pl.program_id(2)

        @pl.when(n == 0)
        def _():
            m_scr[...] = jnp.full_like(m_scr, -jnp.inf)
            l_scr[...] = jnp.zeros_like(l_scr)
            acc[...] = jnp.zeros_like(acc)

        k2, v2 = _halves(k_ref[...]), _halves(v_ref[...])
        lo = _lane((bq, 128)) < 64
        s2s = [_nt(q_ref[:, 128 * pr:128 * (pr + 1)], k2) for pr in range(2)]
        for pr in range(2):
            s2 = s2s[pr]
            ps, alphas = [], []
            for e in range(2):
                g = 2 * pr + e
                s = s2[:, e * bk:(e + 1) * bk]
                m_prev = m_scr[g]
                m_next = jnp.maximum(m_prev, jnp.max(s, axis=1, keepdims=True))
                alpha = jnp.exp2(m_prev - m_next)
                pe = jnp.exp2(s - jnp.tile(m_next, (1, bk // 128)))
                l_scr[g] = alpha * l_scr[g] + jnp.sum(pe, axis=1, keepdims=True)
                m_scr[g] = m_next
                ps.append(_mx(pe))
                alphas.append(alpha)
            o2 = _nn(jnp.concatenate(ps, axis=1), v2)
            sl = slice(128 * pr, 128 * (pr + 1))
            acc[:, sl] = acc[:, sl] * jnp.where(lo, alphas[0], alphas[1]) + o2

        @pl.when(n == nk - 1)
        def _():
            for pr in range(2):
                sl = slice(128 * pr, 128 * (pr + 1))
                o_ref[:, sl] = acc[:, sl] / jnp.where(lo, l_scr[2 * pr], l_scr[2 * pr + 1])
            for g in range(4):
                lse = jnp.transpose(m_scr[g] + jnp.log2(l_scr[g]))
                lse_ref[pl.ds(4 * j + g, 1), :] = lse[0:1, :]

    return pl.pallas_call(
        body, grid=(seq // bq, 2, nk), name="attn_fwd",
        in_specs=[pl.BlockSpec((bq, 256), lambda i, j, n: (i, j)), pl.BlockSpec((None, bk, 128), lambda i, j, n: (j, n, 0)),
                  pl.BlockSpec((None, bk, 128), lambda i, j, n: (j, n, 0))],
        out_specs=[pl.BlockSpec((bq, 256), lambda i, j, n: (i, j)), pl.BlockSpec((8, bq), lambda i, j, n: (0, i))],
        out_shape=[jax.ShapeDtypeStruct((seq, 512), F32), jax.ShapeDtypeStruct((8, seq), F32)],
        scratch_shapes=[pltpu.VMEM((4, bq, 128), F32), pltpu.VMEM((4, bq, 128), F32), pltpu.VMEM((bq, 256), F32)],
        compiler_params=_cp("parallel", "arbitrary", "arbitrary"))(qt, kd, vd)


def _attn_bwd(qt, kd, vd, do, lse, delta):
    seq = qt.shape[0]
    bq = bk = 512
    nq, nk = seq // bq, seq // bk

    def body(q_ref, do_ref, k_ref, v_ref, lse_ref, del_ref, dq_ref, dk_ref, dv_ref):
        j, i, n = pl.program_id(0), pl.program_id(1), pl.program_id(2)

        @pl.when(n == 0)
        def _():
            dq_ref[...] = jnp.zeros_like(dq_ref)

        @pl.when((i == 0) & (n == 0))
        def _():
            dk_ref[...] = jnp.zeros_like(dk_ref)
            dv_ref[...] = jnp.zeros_like(dv_ref)

        k2, v2 = _halves(k_ref[...]), _halves(v_ref[...])
        lo = _lane((bk, 128)) < 64
        rows = pl.ds(pl.multiple_of(n * bk, bk), bk)
        sls = [slice(128 * pr, 128 * (pr + 1)) for pr in range(2)]
        s_ts = [_nt(k2, q_ref[:, sl]) for sl in sls]
        dp_ts = [_nt(v2, do_ref[:, sl]) for sl in sls]
        big = lambda r: jnp.concatenate([jnp.broadcast_to(r[0], (bk, bq)), jnp.broadcast_to(r[1], (bk, bq))], axis=0)
        for pr in range(2):
            sl = sls[pr]
            qp, dop = q_ref[:, sl], do_ref[:, sl]
            ls = [lse_ref[pl.ds(4 * j + 2 * pr + e, 1), :] for e in range(2)]
            dl = [del_ref[pl.ds(4 * j + 2 * pr + e, 1), :] for e in range(2)]
            p_t = jnp.exp2(s_ts[pr] - big(ls))
            ds_t = _mx(p_t * (dp_ts[pr] - big(dl)))
            rv = _nn(_mx(p_t), dop)
            rk = _nn(ds_t, qp) * LN2
            dv_ref[rows, :] += jnp.where(lo, rv[:bk], rv[bk:])
            dk_ref[rows, :] += jnp.where(lo, rk[:bk], rk[bk:])
            dq_ref[:, sl] += _tn(ds_t, k2)

    return pl.pallas_call(
        body, grid=(2, nq, nk), name="attn_bwd",
        in_specs=[pl.BlockSpec((bq, 256), lambda j, i, n: (i, j)), pl.BlockSpec((bq, 256), lambda j, i, n: (i, j)),
                  pl.BlockSpec((None, bk, 128), lambda j, i, n: (j, n, 0)), pl.BlockSpec((None, bk, 128), lambda j, i, n: (j, n, 0)),
                  pl.BlockSpec((8, bq), lambda j, i, n: (0, i)), pl.BlockSpec((8, bq), lambda j, i, n: (0, i))],
        out_specs=[pl.BlockSpec((bq, 256), lambda j, i, n: (i, j)), pl.BlockSpec((None, seq, 128), lambda j, i, n: (j, 0, 0)),
                   pl.BlockSpec((None, seq, 128), lambda j, i, n: (j, 0, 0))],
        out_shape=[jax.ShapeDtypeStruct((seq, 512), F32), jax.ShapeDtypeStruct((2, seq, 128), F32),
                   jax.ShapeDtypeStruct((2, seq, 128), F32)],
        compiler_params=_cp("arbitrary", "arbitrary", "arbitrary"))(qt, do, kd, vd, lse, delta)


def _ret_tables(lg_f, lg_b, c):
    idx = jnp.arange(c, dtype=F32)
    diff = idx[:, None] - idx[None, :]
    a, b = lg_f[:, None, None], lg_b[:, None, None]
    low, up = (diff >= 0)[None], (diff < 0)[None]
    dmat = jnp.where(low, jnp.exp(a * jnp.maximum(diff, 0.0)[None]), jnp.exp(b * jnp.maximum(-diff, 0.0)[None]))
    dda = jnp.where(low, diff[None] * jnp.exp(a * jnp.maximum(diff, 0.0)[None]), 0.0)
    ddb = jnp.where(up, -diff[None] * jnp.exp(b * jnp.maximum(-diff, 0.0)[None]), 0.0)
    rep = lambda w: jnp.broadcast_to(w[:, :, None], (RET_HEADS, c, 128))
    wqf = rep(jnp.exp(lg_f[:, None] * (idx + 1.0)[None]))
    wqb = rep(jnp.exp(lg_b[:, None] * (c - idx)[None]))
    wkf = rep(jnp.exp(lg_f[:, None] * (c - 1.0 - idx)[None]))
    wkb = rep(jnp.exp(lg_b[:, None] * idx[None]))
    cdf, cdb = jnp.exp(lg_f * c), jnp.exp(lg_b * c)
    dec_fb = jnp.broadcast_to(jnp.concatenate([cdf, cdb])[:, None], (8, 128))
    dec_bf = jnp.broadcast_to(jnp.concatenate([cdb, cdf])[:, None], (8, 128))
    return dict(dmat=dmat, dda=dda, ddb=ddb, wqf=wqf, wqb=wqb, wkf=wkf, wkb=wkb, dec_fb=dec_fb, dec_bf=dec_bf)


def _ret_states(xk, yv, w_fwd, w_rev, dec, name):
    seq = xk.shape[0]
    c = RET_CHUNK
    nc = seq // c

    def body(xf_ref, yf_ref, xr_ref, yr_ref, wf_ref, wr_ref, dec_ref, sf_ref, sr_ref, st_f, st_r):
        @pl.when(pl.program_id(0) == 0)
        def _():
            st_f[...] = jnp.zeros_like(st_f)
            st_r[...] = jnp.zeros_like(st_r)

        lane = _lane((c, 128))
        for h in range(RET_HEADS):
            pr, e = h // 2, h % 2
            half = (lane < 64) if e == 0 else (lane >= 64)
            for x_ref, y_ref, w_ref, s_ref, st, drow in ((xf_ref, yf_ref, wf_ref, sf_ref, st_f, h),
                                                        (xr_ref, yr_ref, wr_ref, sr_ref, st_r, 4 + h)):
                s_ref[h] = st[h].astype(s_ref.dtype)
                xm = jnp.where(half, x_ref[:, 128 * pr:128 * (pr + 1)].astype(F32) * w_ref[h], 0.0)
                st[h] = st[h] * dec_ref[drow:drow + 1, :] + _tn(_mx(xm), _mx(y_ref[:, 128 * h:128 * (h + 1)]))

    xs = lambda f: pl.BlockSpec((c, 256), f)
    ys = lambda f: pl.BlockSpec((c, 512), f)
    fwd, rev = (lambda n: (n, 0)), (lambda n: (nc - 1 - n, 0))
    wsp = pl.BlockSpec((RET_HEADS, c, 128), lambda n: (0, 0, 0))
    return pl.pallas_call(
        body, grid=(nc,), name=name,
        in_specs=[xs(fwd), ys(fwd), xs(rev), ys(rev), wsp, wsp, pl.BlockSpec((8, 128), lambda n: (0, 0))],
        out_specs=[pl.BlockSpec((None, RET_HEADS, 128, 128), lambda n: (n, 0, 0, 0)),
                   pl.BlockSpec((None, RET_HEADS, 128, 128), lambda n: (nc - 1 - n, 0, 0, 0))],
        out_shape=[jax.ShapeDtypeStruct((nc, RET_HEADS, 128, 128), MXU_DTYPE)] * 2,
        scratch_shapes=[pltpu.VMEM((RET_HEADS, 128, 128), F32), pltpu.VMEM((RET_HEADS, 128, 128), F32)],
        compiler_params=_cp("arbitrary"))(xk, yv, xk, yv, w_fwd, w_rev, dec)


def _ret_fwd(rq, rk, rv, rf, rb, tb):
    seq = rq.shape[0]
    c = RET_CHUNK

    def body(q_ref, k_ref, v_ref, rf_ref, rb_ref, d_ref, wqf_ref, wqb_ref, o_ref):
        lane = _lane((c, 128))
        for h in range(RET_HEADS):
            pr, e = h // 2, h % 2
            half = (lane < 64) if e == 0 else (lane >= 64)
            sl = slice(128 * pr, 128 * (pr + 1))
            qp = q_ref[:, sl]
            km = jnp.where(half, k_ref[:, sl], jnp.zeros_like(k_ref[:, sl]))
            sd = _mx(_nt(qp, km) * d_ref[h])
            qf = qp.astype(F32)
            o = _nn(sd, v_ref[:, 128 * h:128 * (h + 1)])
            o = o + _nn(_mx(qf * wqf_ref[h]), rf_ref[h]) + _nn(_mx(qf * wqb_ref[h]), rb_ref[h])
            o_ref[:, 128 * h:128 * (h + 1)] = o

    st = pl.BlockSpec((None, RET_HEADS, 128, 128), lambda n: (n, 0, 0, 0))
    wsp = pl.BlockSpec((RET_HEADS, c, 128), lambda n: (0, 0, 0))
    return pl.pallas_call(
        body, grid=(seq // c,), name="ret_fwd",
        in_specs=[pl.BlockSpec((c, 256), lambda n: (n, 0)), pl.BlockSpec((c, 256), lambda n: (n, 0)),
                  pl.BlockSpec((c, 512), lambda n: (n, 0)), st, st, pl.BlockSpec((RET_HEADS, c, c), lambda n: (0, 0, 0)), wsp, wsp],
        out_specs=pl.BlockSpec((c, 512), lambda n: (n, 0)),
        out_shape=jax.ShapeDtypeStruct((seq, 512), F32), compiler_params=_cp("parallel"))(
            rq, rk, rv, rf, rb, tb["dmat"], tb["wqf"], tb["wqb"])


def _ret_bwd(rq, rk, rv, do, rf, rb, hf, hb, tb):
    seq = rq.shape[0]
    c = RET_CHUNK

    def body(q_ref, k_ref, v_ref, do_ref, rf_ref, rb_ref, hf_ref, hb_ref, d_ref, dda_ref, ddb_ref,
             wqf_ref, wqb_ref, wkf_ref, wkb_ref, cdec_ref, dq_ref, dk_ref, dv_ref, dlg_ref):
        @pl.when(pl.program_id(0) == 0)
        def _():
            dlg_ref[...] = jnp.zeros_like(dlg_ref)

        lane = _lane((c, 128))
        pos = lax.broadcasted_iota(jnp.int32, (c, 128), 0).astype(F32)
        for pr in range(2):
            sl = slice(128 * pr, 128 * (pr + 1))
            qp, kp = q_ref[:, sl], k_ref[:, sl]
            qf, kf = qp.astype(F32), kp.astype(F32)
            dq_acc = jnp.zeros((c, 128), F32)
            dk_acc = jnp.zeros((c, 128), F32)
            for e in range(2):
                h = 2 * pr + e
                half = (lane < 64) if e == 0 else (lane >= 64)
                hs = slice(128 * h, 128 * (h + 1))
                km = jnp.where(half, kp, jnp.zeros_like(kp))
                kmf = km.astype(F32)
                vh, doh = v_ref[:, hs], do_ref[:, hs]
                rfh, rbh, hfh, hbh = rf_ref[h], rb_ref[h], hf_ref[h], hb_ref[h]
                s = _nt(qp, km)
                dpm = _nt(doh, vh)
                ds_b = _mx(dpm * d_ref[h])
                sd_b = _mx(s * d_ref[h])
                dq_if = wqf_ref[h] * _nt(doh, rfh)
                dq_ib = wqb_ref[h] * _nt(doh, rbh)
                dq_acc = dq_acc + _nn(ds_b, km) + dq_if + dq_ib
                dk_sf = wkf_ref[h] * _nt(vh, hfh)
                dk_sb = wkb_ref[h] * _nt(vh, hbh)
                dk_acc = dk_acc + jnp.where(half, _tn(ds_b, qp), 0.0) + dk_sf + dk_sb
                dv = _tn(sd_b, doh) + _nn(_mx(kmf * wkf_ref[h]), hfh) + _nn(_mx(kmf * wkb_ref[h]), hbh)
                dv_ref[:, hs] = dv.astype(dv_ref.dtype)
                sdp = s * dpm
                hr_f = hfh.astype(F32) * rfh.astype(F32)
                hr_b = hbh.astype(F32) * rbh.astype(F32)
                da = (_rowsum128(sdp * dda_ref[h]) + _rowsum128((pos + 1.0) * qf * dq_if)
                      + _rowsum128((c - 1.0 - pos) * kf * dk_sf) + cdec_ref[h:h + 1, :] * _rowsum128(hr_f))
                db = (_rowsum128(sdp * ddb_ref[h]) + _rowsum128((c - pos) * qf * dq_ib)
                      + _rowsum128(pos * kf * dk_sb) + cdec_ref[4 + h:5 + h, :] * _rowsum128(hr_b))
                dlg_ref[h:h + 1, :] += da
                dlg_ref[4 + h:5 + h, :] += db
            dq_ref[:, sl] = dq_acc
            dk_ref[:, sl] = dk_acc

    st = pl.BlockSpec((None, RET_HEADS, 128, 128), lambda n: (n, 0, 0, 0))
    wsp = pl.BlockSpec((RET_HEADS, c, 128), lambda n: (0, 0, 0))
    dsp = pl.BlockSpec((RET_HEADS, c, c), lambda n: (0, 0, 0))
    x256 = pl.BlockSpec((c, 256), lambda n: (n, 0))
    x512 = pl.BlockSpec((c, 512), lambda n: (n, 0))
    cdec = tb["dec_fb"] * float(c)
    return pl.pallas_call(
        body, grid=(seq // c,), name="ret_bwd",
        in_specs=[x256, x256, x512, x512, st, st, st, st, dsp, dsp, dsp, wsp, wsp, wsp, wsp,
                  pl.BlockSpec((8, 128), lambda n: (0, 0))],
        out_specs=[x256, x256, x512, pl.BlockSpec((8, 128), lambda n: (0, 0))],
        out_shape=[jax.ShapeDtypeStruct((seq, 256), F32), jax.ShapeDtypeStruct((seq, 256), F32),
                   jax.ShapeDtypeStruct((seq, 512), MXU_DTYPE), jax.ShapeDtypeStruct((8, 128), F32)],
        compiler_params=_cp("arbitrary"))(
            rq, rk, rv, do, rf, rb, hf, hb, tb["dmat"], tb["dda"], tb["ddb"], tb["wqf"], tb["wqb"], tb["wkf"], tb["wkb"], cdec)


def _tail(x, tgt, o_att, o_ret, p, mod, g_post, gn_g, wpt, wout):
    seq = x.shape[0]
    bs = 256
    nb = seq // bs

    def body(x_ref, t_ref, oa_ref, or_ref, za_ref, zr_ref, gl_ref, mod_ref, gp_ref, gn_ref, wpt_ref, wout_ref,
             dout_ref, dza_ref, dzr_ref, dgl_ref, doa_ref, dor_ref, del_ref, gout_ref, gpt_ref,
             dgate_ref, dgpost_ref, dgn_ref, loss_ref, gout_acc, gpt_acc):
        i = pl.program_id(0)

        @pl.when(i == 0)
        def _():
            gout_acc[...] = jnp.zeros_like(gout_acc)
            gpt_acc[...] = jnp.zeros_like(gpt_acc)
            dgate_ref[...] = jnp.zeros_like(dgate_ref)
            dgpost_ref[...] = jnp.zeros_like(dgpost_ref)
            dgn_ref[...] = jnp.zeros_like(dgn_ref)
            loss_ref[...] = jnp.zeros_like(loss_ref)

        oa, za, zr = oa_ref[...], za_ref[...], zr_ref[...]
        sga, sgr = _sigmoid(za), _sigmoid(zr)
        sza, szr = za * sga, zr * sgr
        ya_b = _mx(oa * sza)
        ons, rstds = [], []
        for h in range(RET_HEADS):
            oh = or_ref[:, 128 * h:128 * (h + 1)]
            xc = oh - jnp.mean(oh, axis=-1, keepdims=True)
            rstd = lax.rsqrt(jnp.mean(xc * xc, axis=-1, keepdims=True) + EPS)
            ons.append(xc * rstd)
            rstds.append(rstd)
        on = jnp.concatenate(ons, axis=1)
        yn = on * gn_ref[...]
        yr_b = _mx(yn * szr)
        wpa_t, wpr_t = wpt_ref[:, 0:512], wpt_ref[:, 512:1024]
        a_att = _nt(ya_b, wpa_t)
        a_ret = _nt(yr_b, wpr_t)
        gts = _sigmoid(gl_ref[...])
        g_att, g_ret = gts[:, 0:D_MODEL], gts[:, D_MODEL:2 * D_MODEL]
        merged_b = _mx(g_att * a_att + g_ret * a_ret)
        u = _nn(merged_b, wout_ref[...])
        r = lax.rsqrt(jnp.mean(u * u, axis=-1, keepdims=True) + EPS)
        un = u * r
        gpost = gp_ref[...]
        y = un * gpost
        gate = mod_ref[:, 2 * D_MODEL:3 * D_MODEL]
        err = x_ref[...] + gate * y - t_ref[...]
        loss_ref[...] += (0.5 / D_MODEL) * _rowsum128(err * err)
        dout = err * (1.0 / D_MODEL)
        dout_ref[...] = dout
        dgate_ref[...] += jnp.sum(dout * y, axis=0, keepdims=True)
        dy = dout * gate
        dgpost_ref[...] += jnp.sum(dy * un, axis=0, keepdims=True)
        dun = dy * gpost
        du_b = _mx(r * (dun - un * jnp.mean(dun * un, axis=-1, keepdims=True)))
        dmerged = _nt(du_b, wout_ref[...])
        gout_acc[...] += _tn(merged_b, du_b)
        d_att, d_ret = dmerged * g_att, dmerged * g_ret
        dgl_ref[:, 0:D_MODEL] = (d_att * a_att * (1.0 - g_att)).astype(dgl_ref.dtype)
        dgl_ref[:, D_MODEL:2 * D_MODEL] = (d_ret * a_ret * (1.0 - g_ret)).astype(dgl_ref.dtype)
        d_att_b, d_ret_b = _mx(d_att), _mx(d_ret)
        dya = _nn(d_att_b, wpa_t)
        dyr = _nn(d_ret_b, wpr_t)
        gpt_acc[:, 0:512] += _tn(d_att_b, ya_b)
        gpt_acc[:, 512:1024] += _tn(d_ret_b, yr_b)
        dza_ref[...] = (dya * oa * (sga * (1.0 + za * (1.0 - sga)))).astype(dza_ref.dtype)
        doa = dya * sza
        doa_ref[...] = doa.astype(doa_ref.dtype)
        dt = jnp.transpose(doa * oa)
        del_ref[...] = jnp.sum(dt.reshape(8, HEAD_DIM, bs), axis=1)
        dzr_ref[...] = (dyr * yn * (sgr * (1.0 + zr * (1.0 - sgr)))).astype(dzr_ref.dtype)
        dyn = dyr * szr
        dgn_ref[...] += jnp.sum(dyn * on, axis=0, keepdims=True)
        don = dyn * gn_ref[...]
        for h in range(RET_HEADS):
            hs = slice(128 * h, 128 * (h + 1))
            dh, oh = don[:, hs], ons[h]
            doh = rstds[h] * (dh - jnp.mean(dh, axis=-1, keepdims=True) - oh * jnp.mean(dh * oh, axis=-1, keepdims=True))
            dor_ref[:, hs] = doh.astype(dor_ref.dtype)

        @pl.when(i == nb - 1)
        def _():
            gout_ref[...] = gout_acc[...].astype(gout_ref.dtype)
            gpt_ref[...] = gpt_acc[...].astype(gpt_ref.dtype)

    row = lambda w: pl.BlockSpec((bs, w), lambda i: (i, 0))
    el = lambda w, off: pl.BlockSpec((pl.Element(bs), pl.Element(w)), lambda i: (i * bs, off))
    vec = lambda w: pl.BlockSpec((1, w), lambda i: (0, 0))
    full = pl.BlockSpec((D_MODEL, D_MODEL), lambda i: (0, 0))
    sds = jax.ShapeDtypeStruct
    return pl.pallas_call(
        body, grid=(nb,), name="tail",
        in_specs=[row(D_MODEL), row(D_MODEL), row(512), row(512), el(512, ZA), el(512, ZR), el(2 * D_MODEL, GL),
                  vec(3 * D_MODEL), vec(D_MODEL), vec(512), full, full],
        out_specs=[row(D_MODEL), row(512), row(512), row(2 * D_MODEL), row(512), row(512),
                   pl.BlockSpec((8, bs), lambda i: (0, i)), full, full, vec(D_MODEL), vec(D_MODEL), vec(512), vec(128)],
        out_shape=[sds((seq, D_MODEL), F32), sds((seq, 512), MXU_DTYPE), sds((seq, 512), MXU_DTYPE),
                   sds((seq, 2 * D_MODEL), MXU_DTYPE), sds((seq, 512), MXU_DTYPE), sds((seq, 512), MXU_DTYPE),
                   sds((8, seq), F32), sds((D_MODEL, D_MODEL), MXU_DTYPE), sds((D_MODEL, D_MODEL), MXU_DTYPE),
                   sds((1, D_MODEL), F32), sds((1, D_MODEL), F32), sds((1, 512), F32), sds((1, 128), F32)],
        scratch_shapes=[pltpu.VMEM((D_MODEL, D_MODEL), F32), pltpu.VMEM((D_MODEL, D_MODEL), F32)],
        compiler_params=_cp("arbitrary"))(x, tgt, o_att, o_ret, p, p, p, mod, g_post, gn_g, wpt, wout)


def _assemble(p, cos, sin, qg2, kg2, dqt, dkp, dvp, dza, drq, drk, drv, dzr, dgl):
    seq = p.shape[0]
    bs = 512

    def body(p_ref, cos_ref, sin_ref, qg_ref, kg_ref, dqt_ref, dkp_ref, dvp_ref, dza_ref, drq_ref, drk_ref, drv_ref,
             dzr_ref, dgl_ref, dp_ref, dqg_ref, dkg_ref):
        @pl.when(pl.program_id(0) == 0)
        def _():
            dqg_ref[...] = jnp.zeros_like(dqg_ref)
            dkg_ref[...] = jnp.zeros_like(dkg_ref)

        m = _blockdiag64()
        cs, sn = cos_ref[...], sin_ref[...]
        lo = _lane((bs, 128)) < 64

        def norm_bwd(raw, dn, g, dg_ref):
            r = lax.rsqrt(_seg64(raw * raw, m) * (1.0 / 64) + EPS)
            xn = raw * r
            dg_ref[...] += jnp.sum(dn * xn, axis=0, keepdims=True)
            dxn = dn * g
            return r * (dxn - xn * (_seg64(dxn * xn, m) * (1.0 / 64)))

        for pr in range(4):
            sl = slice(128 * pr, 128 * (pr + 1))
            dn = _rope_t(dqt_ref[:, sl] * 0.125, cs, sn)
            dp_ref[:, QA + 128 * pr:QA + 128 * (pr + 1)] = norm_bwd(p_ref[:, sl], dn, qg_ref[...], dqg_ref).astype(dp_ref.dtype)
        fold = lambda a: a + pltpu.roll(a, 64, 1)
        dk = jnp.where(lo, fold(dkp_ref[0]), fold(dkp_ref[1]))
        dp_ref[:, KA:KA + 128] = norm_bwd(p_ref[:, KA:KA + 128], _rope_t(dk, cs, sn), kg_ref[...], dkg_ref).astype(dp_ref.dtype)
        dp_ref[:, VA:VA + 128] = jnp.where(lo, fold(dvp_ref[0]), fold(dvp_ref[1])).astype(dp_ref.dtype)
        dp_ref[:, ZA:ZA + 512] = dza_ref[...]
        for pr in range(2):
            sl = slice(128 * pr, 128 * (pr + 1))
            dp_ref[:, QR + 128 * pr:QR + 128 * (pr + 1)] = _rope_t(drq_ref[:, sl], cs, sn).astype(dp_ref.dtype)
            dp_ref[:, KR + 128 * pr:KR + 128 * (pr + 1)] = _rope_t(drk_ref[:, sl] * 0.125, cs, sn).astype(dp_ref.dtype)
        dp_ref[:, VR:VR + 512] = drv_ref[...]
        dp_ref[:, ZR:ZR + 512] = dzr_ref[...]
        dp_ref[:, GL:GL + 2 * D_MODEL] = dgl_ref[...]

    row = lambda w: pl.BlockSpec((bs, w), lambda i: (i, 0))
    gsp = pl.BlockSpec((1, 128), lambda i: (0, 0))
    dup = pl.BlockSpec((2, bs, 128), lambda i: (0, i, 0))
    return pl.pallas_call(
        body, grid=(seq // bs,), name="assemble_dp",
        in_specs=[row(768), row(128), row(128), gsp, gsp, row(512), dup, dup, row(512), row(256), row(256), row(512),
                  row(512), row(2 * D_MODEL)],
        out_specs=[row(IN_WIDTH), gsp, gsp],
        out_shape=[jax.ShapeDtypeStruct((seq, IN_WIDTH), MXU_DTYPE), jax.ShapeDtypeStruct((1, 128), F32),
                   jax.ShapeDtypeStruct((1, 128), F32)],
        compiler_params=_cp("arbitrary"))(p, cos, sin, qg2, kg2, dqt, dkp, dvp, dza, drq, drk, drv, dzr, dgl)


def _local_step(x, tgt, mod, g_pre, qn_g, kn_g, w_dec_f, w_dec_b, gn_g, g_post, win_t, wpt, wout):
    seq = x.shape[0]
    cos, sin = _rope_tables(seq)
    qg2, kg2 = jnp.tile(qn_g, (1, 2)), jnp.tile(kn_g, (1, 2))
    lg_f = jax.nn.log_sigmoid(w_dec_f[0])
    lg_b = jax.nn.log_sigmoid(w_dec_b[0])
    tb = _ret_tables(lg_f, lg_b, RET_CHUNK)

    h = _prenorm(x, mod, g_pre)
    p = _matmul(h, win_t, ta=False, tb=True, tm=512, tn=IN_WIDTH // 2, out_dtype=F32, name="in_proj")
    qt, kd, vd, rq, rk, rv = _prep(p, cos, sin, qg2, kg2)
    o_att, lse = _attn_fwd(qt, kd, vd)
    rf, rb = _ret_states(rk, rv, tb["wkf"], tb["wkb"], tb["dec_fb"], "ret_states_fwd")
    o_ret = _ret_fwd(rq, rk, rv, rf, rb, tb)
    (dout, dza, dzr, dgl, do_att, do_ret, delta, g_out, g_pt, dgate, dgpost, dgn, loss_l) = _tail(
        x, tgt, o_att, o_ret, p, mod, g_post, gn_g, wpt, wout)
    dqt, dkp, dvp = _attn_bwd(qt, kd, vd, do_att, lse, delta)
    hb, hf = _ret_states(rq, do_ret, tb["wqb"], tb["wqf"], tb["dec_bf"], "ret_states_bwd")
    drq, drk, drv, dlg = _ret_bwd(rq, rk, rv, do_ret, rf, rb, hf, hb, tb)
    dp, dqg, dkg = _assemble(p, cos, sin, qg2, kg2, dqt, dkp, dvp, dza, drq, drk, drv, dzr, dgl)
    dh = _matmul(dp, win_t, ta=False, tb=False, tm=512, tn=D_MODEL, out_dtype=F32, name="in_proj_dx")
    g_in_t = _matmul(dp, h, ta=True, tb=False, tm=256, tn=D_MODEL, out_dtype=MXU_DTYPE, name="in_proj_dw")
    grad_x, dshift, dscale, dgpre = _prenorm_bwd(x, dh, dout, mod, g_pre)

    dlg = jnp.sum(dlg, axis=1)
    small = dict(
        dmod=jnp.concatenate([dshift, dscale, dgate], axis=1),
        g_pre=dgpre, g_post=dgpost, gn_g=dgn,
        qn_g=dqg[:, :64] + dqg[:, 64:], kn_g=dkg[:, :64] + dkg[:, 64:],
        w_dec_f=(dlg[0:4] * jax.nn.sigmoid(-w_dec_f[0]))[None], w_dec_b=(dlg[4:8] * jax.nn.sigmoid(-w_dec_b[0]))[None],
        loss=jnp.sum(loss_l, axis=1, keepdims=True))
    return grad_x, g_in_t, g_pt, g_out, small


N_DEV = 8
N_CHIP = 4
HBM_SPEC = pl.BlockSpec(memory_space=pl.ANY)
VMEM_SPEC = pl.BlockSpec(memory_space=pltpu.VMEM)
SHARD_ROWS = (IN_WIDTH // N_CHIP, D_MODEL // N_CHIP, D_MODEL // N_CHIP)


def _coords():
    return lax.axis_index("x"), lax.axis_index("y"), lax.axis_index("c")


def _peer(k):
    x, y, c = _coords()
    return (1 - x if k & 4 else x, 1 - y if k & 2 else y, 1 - c if k & 1 else c)


def _dev_index(p):
    return 4 * p[0] + 2 * p[1] + p[2]


def _rows(ref, start, size):
    return ref.at[pl.ds(pl.multiple_of(start, 16), size), :]


def _remote(src, dst, ssem, rsem, dev):
    return pltpu.make_async_remote_copy(src_ref=src, dst_ref=dst, send_sem=ssem, recv_sem=rsem, device_id=dev,
                                        device_id_type=MESH)


def _mod_exchange(c, w_ada_s, b4):
    ncol = w_ada_s.shape[1]

    def body(c_ref, w_ref, b_ref, cs_ref, mod_ref, modsh, modall, ssem, rsem):
        me = _dev_index(_coords())
        chip = me // 2
        cs_ref[me] = c_ref[...]
        sends = []
        for k in range(1, N_DEV):
            cp = _remote(c_ref, cs_ref.at[me], ssem.at[k - 1], rsem.at[k - 1], _peer(k))
            cp.start()
            sends.append(cp)
        for k in range(1, N_DEV):
            slot = cs_ref.at[_dev_index(_peer(k))]
            _remote(slot, slot, ssem.at[k - 1], rsem.at[k - 1], _peer(k)).wait_recv()
        cs = jnp.concatenate([cs_ref[d] for d in range(N_DEV)], axis=0)
        ms = _nn(cs * _sigmoid(cs), w_ref[...], precision=HIGHEST) + b_ref[pl.ds(chip, 1), :]
        modsh[...] = ms
        modall[chip] = ms
        for k2 in range(1, N_CHIP):
            cp = _remote(modsh, modall.at[chip], ssem.at[6 + k2], rsem.at[6 + k2], _peer(2 * k2))
            cp.start()
            sends.append(cp)
        for k2 in range(1, N_CHIP):
            slot = modall.at[_dev_index(_peer(2 * k2)) // 2]
            _remote(slot, slot, ssem.at[6 + k2], rsem.at[6 + k2], _peer(2 * k2)).wait_recv()
        for jj in range(N_CHIP):
            mod_ref[:, ncol * jj:ncol * (jj + 1)] = modall[jj, pl.ds(me, 1), :]
        for cp in sends:
            cp.wait_send()

    return pl.pallas_call(
        body, name="mod_exchange", in_specs=[VMEM_SPEC] * 3, out_specs=[VMEM_SPEC] * 2,
        out_shape=[jax.ShapeDtypeStruct((N_DEV, 1, D_MODEL), F32), jax.ShapeDtypeStruct((1, N_CHIP * ncol), F32)],
        scratch_shapes=[pltpu.VMEM((N_DEV, ncol), F32), pltpu.VMEM((N_CHIP, N_DEV, ncol), F32),
                        pltpu.SemaphoreType.DMA((10,)), pltpu.SemaphoreType.DMA((10,))],
        compiler_params=_cp())(c, w_ada_s, b4)


GATHER_CHUNK = 32


def _chunks(nt, chunk):
    out = []
    for t in range(nt):
        half = SHARD_ROWS[t] // 2
        step = chunk if half % chunk == 0 else half
        out += [(t, off, step) for off in range(0, half, step)]
    return out


def _weight_gather(shards):
    nt = len(shards)
    pieces = _chunks(nt, GATHER_CHUNK)
    npc = len(pieces)

    def body(*refs):
        srcs, outs = refs[:nt], refs[nt:2 * nt]
        lsem, ssem, rsem = refs[2 * nt:]
        x, y, c = _coords()
        chip = 2 * x + y
        sib = (x, y, 1 - c)

        def place(t, ch, cc, off, n):
            return _rows(outs[t], ch * SHARD_ROWS[t] + cc * (SHARD_ROWS[t] // 2) + off, n)

        local = [pltpu.make_async_copy(srcs[t], _rows(outs[t], chip * SHARD_ROWS[t], SHARD_ROWS[t]), lsem.at[t]) for t in range(nt)]
        for cp in local:
            cp.start()
        sends = []
        for k2 in range(1, N_CHIP):
            for q, (t, off, n) in enumerate(pieces):
                i = (k2 - 1) * npc + q
                cp = _remote(_rows(srcs[t], c * (SHARD_ROWS[t] // 2) + off, n), place(t, chip, c, off, n),
                             ssem.at[i], rsem.at[i], _peer(2 * k2))
                cp.start()
                sends.append(cp)
        for k2 in range(1, N_CHIP):
            pchip = _dev_index(_peer(2 * k2)) // 2
            for q, (t, off, n) in enumerate(pieces):
                i = (k2 - 1) * npc + q
                got = place(t, pchip, c, off, n)
                _remote(got, got, ssem.at[i], rsem.at[i], _peer(2 * k2)).wait_recv()
                cp = _remote(got, got, ssem.at[3 * npc + i], rsem.at[3 * npc + i], sib)
                cp.start()
                sends.append(cp)
        for k2 in range(1, N_CHIP):
            pchip = _dev_index(_peer(2 * k2)) // 2
            for q, (t, off, n) in enumerate(pieces):
                i = (k2 - 1) * npc + q
                got = place(t, pchip, 1 - c, off, n)
                _remote(got, got, ssem.at[3 * npc + i], rsem.at[3 * npc + i], sib).wait_recv()
        for cp in sends:
            cp.wait_send()
        for cp in local:
            cp.wait()

    return pl.pallas_call(
        body, name="weight_gather", in_specs=[HBM_SPEC] * nt, out_specs=[HBM_SPEC] * nt,
        out_shape=[jax.ShapeDtypeStruct((N_CHIP * s.shape[0], s.shape[1]), s.dtype) for s in shards],
        scratch_shapes=[pltpu.SemaphoreType.DMA((nt,)), pltpu.SemaphoreType.DMA((6 * npc,)), pltpu.SemaphoreType.DMA((6 * npc,))],
        compiler_params=_cp())(*shards)


def _grad_scatter(grads):
    nt = len(grads)

    def body(*refs):
        srcs, outs = refs[:nt], refs[nt:2 * nt]
        lsem, ssem, rsem = refs[2 * nt:]
        me = _dev_index(_coords())
        mine = lambda t, dev: _rows(srcs[t], (dev // 2) * SHARD_ROWS[t] + (dev % 2) * (SHARD_ROWS[t] // 2), SHARD_ROWS[t] // 2)
        local = [pltpu.make_async_copy(mine(t, me), outs[t].at[me], lsem.at[t]) for t in range(nt)]
        for cp in local:
            cp.start()
        sends = []
        for k in range(1, N_DEV):
            for t in range(nt):
                i = (k - 1) * nt + t
                cp = _remote(mine(t, _dev_index(_peer(k))), outs[t].at[me], ssem.at[i], rsem.at[i], _peer(k))
                cp.start()
                sends.append(cp)
        for k in range(1, N_DEV):
            for t in range(nt):
                i = (k - 1) * nt + t
                slot = outs[t].at[_dev_index(_peer(k))]
                _remote(slot, slot, ssem.at[i], rsem.at[i], _peer(k)).wait_recv()
        for cp in sends:
            cp.wait_send()
        for cp in local:
            cp.wait()

    return pl.pallas_call(
        body, name="grad_scatter", in_specs=[HBM_SPEC] * nt, out_specs=[HBM_SPEC] * nt,
        out_shape=[jax.ShapeDtypeStruct((N_DEV, SHARD_ROWS[t] // 2, g.shape[1]), g.dtype) for t, g in enumerate(grads)],
        scratch_shapes=[pltpu.SemaphoreType.DMA((nt,)), pltpu.SemaphoreType.DMA((7 * nt,)), pltpu.SemaphoreType.DMA((7 * nt,))],
        compiler_params=_cp())(*grads)


def _sum_slots(slots):
    nt = len(slots)
    steps = 2

    def body(*refs):
        for r_ref, o_ref in zip(refs[:nt], refs[nt:]):
            acc = r_ref[0].astype(F32)
            for d in range(1, N_DEV):
                acc = acc + r_ref[d].astype(F32)
            o_ref[...] = acc

    return pl.pallas_call(
        body, grid=(steps,), name="grad_sum",
        in_specs=[pl.BlockSpec((N_DEV, s.shape[1] // steps, s.shape[2]), lambda i: (0, i, 0)) for s in slots],
        out_specs=[pl.BlockSpec((s.shape[1] // steps, s.shape[2]), lambda i: (i, 0)) for s in slots],
        out_shape=[jax.ShapeDtypeStruct(s.shape[1:], F32) for s in slots], compiler_params=_cp("parallel"))(*slots)


def _sibling_exchange(halves):
    nt = len(halves)
    pieces = _chunks(nt, GATHER_CHUNK)
    npc = len(pieces)

    def body(*refs):
        srcs, outs = refs[:nt], refs[nt:2 * nt]
        lsem, ssem, rsem = refs[2 * nt:]
        x, y, c = _coords()
        sib = (x, y, 1 - c)
        place = lambda t, cc, off, n: _rows(outs[t], cc * (SHARD_ROWS[t] // 2) + off, n)
        local = [pltpu.make_async_copy(srcs[t], place(t, c, 0, SHARD_ROWS[t] // 2), lsem.at[t]) for t in range(nt)]
        sends = [_remote(srcs[t].at[pl.ds(off, n), :], place(t, c, off, n), ssem.at[q], rsem.at[q], sib)
                 for q, (t, off, n) in enumerate(pieces)]
        for cp in local + sends:
            cp.start()
        for q, (t, off, n) in enumerate(pieces):
            got = place(t, 1 - c, off, n)
            _remote(got, got, ssem.at[q], rsem.at[q], sib).wait_recv()
        for cp in sends:
            cp.wait_send()
        for cp in local:
            cp.wait()

    return pl.pallas_call(
        body, name="grad_sibling_exchange", in_specs=[HBM_SPEC] * nt, out_specs=[HBM_SPEC] * nt,
        out_shape=[jax.ShapeDtypeStruct((2 * h.shape[0], h.shape[1]), h.dtype) for h in halves],
        scratch_shapes=[pltpu.SemaphoreType.DMA((nt,)), pltpu.SemaphoreType.DMA((npc,)), pltpu.SemaphoreType.DMA((npc,))],
        compiler_params=_cp())(*halves)


def _adam_math(w, g, m, v):
    m = ADAM_B1 * m + (1.0 - ADAM_B1) * g
    v = ADAM_B2 * v + (1.0 - ADAM_B2) * (g * g)
    m_hat = m / (1.0 - ADAM_B1 ** ADAM_STEP)
    v_hat = v / (1.0 - ADAM_B2 ** ADAM_STEP)
    return -ADAM_LR * (m_hat / (jnp.sqrt(v_hat) + ADAM_EPS) + ADAM_WD * w), m, v


def _adamw(w, g, m, v, rb, name):
    rows, cols = w.shape

    def body(w_ref, g_ref, m_ref, v_ref, d_ref, nm_ref, nv_ref):
        d_ref[...], nm_ref[...], nv_ref[...] = _adam_math(w_ref[...], g_ref[...], m_ref[...], v_ref[...])

    spec = pl.BlockSpec((rb, cols), lambda i: (i, 0))
    return pl.pallas_call(body, grid=(rows // rb,), name=name, in_specs=[spec] * 4, out_specs=[spec] * 3,
                          out_shape=[jax.ShapeDtypeStruct(w.shape, F32)] * 3, compiler_params=_cp("parallel"))(w, g, m, v)


PACK = (("b_ada", 3 * D_MODEL), ("g_pre", D_MODEL), ("g_post", D_MODEL), ("gn_g", 512), ("qn_g", 64), ("kn_g", 64),
        ("w_dec_f", 4), ("w_dec_b", 4), ("loss", 1))
PACK_WIDTH = 6144


def _pack(parts):
    used = sum(n for _, n in PACK)
    cols = [parts[name].reshape(1, n).astype(F32) if name in parts else jnp.zeros((1, n), F32) for name, n in PACK]
    return jnp.concatenate(cols + [jnp.zeros((1, PACK_WIDTH - used), F32)], axis=1)


def _unpack(row):
    out, off = {}, 0
    for name, n in PACK:
        out[name] = row[:, off:off + n]
        off += n
    return out


def _small_reduce(vec, wvec, mvec, vvec, cs):
    ncol = 3 * D_MODEL // N_CHIP

    def body(vec_ref, w_ref, m_ref, v_ref, cs_ref, gsum_ref, d_ref, nm_ref, nv_ref, gwa_ref, gat, ssem, rsem):
        me = _dev_index(_coords())
        chip = me // 2
        gat[me] = vec_ref[...]
        sends = []
        for k in range(1, N_DEV):
            cp = _remote(vec_ref, gat.at[me], ssem.at[k - 1], rsem.at[k - 1], _peer(k))
            cp.start()
            sends.append(cp)
        for k in range(1, N_DEV):
            slot = gat.at[_dev_index(_peer(k))]
            _remote(slot, slot, ssem.at[k - 1], rsem.at[k - 1], _peer(k)).wait_recv()
        rows = [gat[d] for d in range(N_DEV)]
        tot = rows[0]
        for d in range(1, N_DEV):
            tot = tot + rows[d]
        gsum_ref[...] = tot
        d_ref[...], nm_ref[...], nv_ref[...] = _adam_math(w_ref[...], tot, m_ref[...], v_ref[...])
        cs = cs_ref[...]
        ca_t = jnp.transpose(jnp.concatenate([cs * _sigmoid(cs), jnp.zeros((128 - N_DEV, D_MODEL), F32)], axis=0))
        dm = jnp.concatenate(rows + [jnp.zeros((128 - N_DEV, PACK_WIDTH), F32)], axis=0)
        for jj in range(N_CHIP):
            @pl.when(chip == jj)
            def _():
                gwa_ref[...] = _nn(ca_t, dm[:, ncol * jj:ncol * (jj + 1)], precision=HIGHEST)
        for cp in sends:
            cp.wait_send()

    row = jax.ShapeDtypeStruct((1, PACK_WIDTH), F32)
    return pl.pallas_call(
        body, name="small_reduce", in_specs=[VMEM_SPEC] * 5, out_specs=[VMEM_SPEC] * 5,
        out_shape=[row, row, row, row, jax.ShapeDtypeStruct((D_MODEL, ncol), F32)],
        scratch_shapes=[pltpu.VMEM((N_DEV, 1, PACK_WIDTH), F32), pltpu.SemaphoreType.DMA((7,)), pltpu.SemaphoreType.DMA((7,))],
        compiler_params=_cp())(vec, wvec, mvec, vvec, cs)


def kernel(x, c, w_ada, b_ada, g_pre, w_in, qn_g, kn_g, w_dec_f, w_dec_b, gn_g, w_pa, w_pr, w_out, g_post, loss_target, m_w_ada, m_b_ada, m_g_pre, m_w_in, m_qn_g, m_kn_g, m_w_dec_f, m_w_dec_b, m_gn_g, m_w_pa, m_w_pr, m_w_out, m_g_post, v_w_ada, v_b_ada, v_g_pre, v_w_in, v_qn_g, v_kn_g, v_w_dec_f, v_w_dec_b, v_gn_g, v_w_pa, v_w_pr, v_w_out, v_g_post):
    shards = (w_in[0].T.astype(MXU_DTYPE), jnp.concatenate([w_pa[0].T, w_pr[0].T], axis=1).astype(MXU_DTYPE),
              w_out[0].astype(MXU_DTYPE))
    win_t, wpt, wout = _weight_gather(shards)
    cs, mod = _mod_exchange(c, w_ada[0], b_ada.reshape(N_CHIP, 3 * D_MODEL // N_CHIP))

    grad_x, g_in_t, g_pt, g_out, small = _local_step(x[0], loss_target[0], mod, g_pre, qn_g, kn_g, w_dec_f, w_dec_b,
                                                     gn_g, g_post, win_t, wpt, wout)

    full = _sibling_exchange(_sum_slots(_grad_scatter((g_in_t, g_pt, g_out))))
    big = {"w_in": (w_in, m_w_in, v_w_in, full[0].T), "w_pa": (w_pa, m_w_pa, v_w_pa, full[1][:, :512].T),
           "w_pr": (w_pr, m_w_pr, v_w_pr, full[1][:, 512:].T), "w_out": (w_out, m_w_out, v_w_out, full[2])}

    small = dict(small)
    small["b_ada"] = small.pop("dmod")
    given = dict(b_ada=(b_ada, m_b_ada, v_b_ada), g_pre=(g_pre, m_g_pre, v_g_pre), g_post=(g_post, m_g_post, v_g_post),
                 gn_g=(gn_g, m_gn_g, v_gn_g), qn_g=(qn_g, m_qn_g, v_qn_g), kn_g=(kn_g, m_kn_g, v_kn_g),
                 w_dec_f=(w_dec_f, m_w_dec_f, v_w_dec_f), w_dec_b=(w_dec_b, m_w_dec_b, v_w_dec_b))
    packs = [_pack({n: t[i] for n, t in given.items()}) for i in range(3)]
    gsum, dvec, nmvec, nvvec, g_w_ada = _small_reduce(_pack(small), *packs, cs.reshape(N_DEV, D_MODEL))
    big["w_ada"] = (w_ada, m_w_ada, v_w_ada, g_w_ada)

    grads, deltas, new_m, new_v = {}, {}, {}, {}
    for name, (w, m, v, g) in big.items():
        d, nm, nv = _adamw(w[0], g, m[0], v[0], 256, "adamw_" + name)
        grads[name], deltas[name], new_m[name], new_v[name] = g[None], d[None], nm[None], nv[None]
    for dst, row in ((grads, gsum), (deltas, dvec), (new_m, nmvec), (new_v, nvvec)):
        dst.update({n: a for n, a in _unpack(row).items() if n != "loss"})
    order = ("w_ada", "b_ada", "g_pre", "w_in", "qn_g", "kn_g", "w_dec_f", "w_dec_b", "gn_g", "w_pa", "w_pr", "w_out", "g_post")
    loss = _unpack(gsum)["loss"][0, 0]
    return (loss, grad_x[None], *[grads[n] for n in order], *[deltas[n] for n in order],
            *[new_m[n] for n in order], *[new_v[n] for n in order])
```

```python
import functools

import jax
import jax.numpy as jnp
from jax import lax
from jax.experimental import pallas as pl
from jax.experimental.pallas import tpu as pltpu

F32 = jnp.float32
BF16 = jnp.bfloat16
MXU_DTYPE = jnp.bfloat16

D_MODEL = 1024
GRID_W = 64
HEAD_DIM = 64
ROPE_THETA = 10000.0
EPS = 1e-6
RET_HEADS = 4
RET_CHUNK = 256
IN_WIDTH = 4864
QA, KA, VA, ZA, QR, KR, VR, ZR, GL = 0, 512, 640, 768, 1280, 1536, 1792, 2304, 2816

ADAM_LR, ADAM_B1, ADAM_B2, ADAM_EPS, ADAM_WD, ADAM_STEP = 0.001, 0.9, 0.999, 1e-08, 0.01, 10

VMEM_LIMIT = 56 * 1024 * 1024
MESH = pl.DeviceIdType.MESH
HIGHEST = lax.Precision.HIGHEST
LOG2E = 1.4426950408889634
LN2 = 0.6931471805599453


def _cp(*sem, **kw):
    if sem:
        kw["dimension_semantics"] = sem
    return pltpu.CompilerParams(vmem_limit_bytes=VMEM_LIMIT, **kw)


def _nn(a, b, **kw):
    return lax.dot_general(a, b, (((1,), (0,)), ((), ())), preferred_element_type=F32, **kw)


def _nt(a, b):
    return lax.dot_general(a, b, (((1,), (1,)), ((), ())), preferred_element_type=F32)


def _tn(a, b):
    return lax.dot_general(a, b, (((0,), (0,)), ((), ())), preferred_element_type=F32)


def _mx(x):
    return x.astype(MXU_DTYPE)


def _lane(shape):
    return lax.broadcasted_iota(jnp.int32, shape, 1)


def _sigmoid(z):
    return 1.0 / (1.0 + jnp.exp(-z))


def _rowsum128(x):
    s = jnp.sum(x, axis=0, keepdims=True)
    out = s[:, 0:128]
    for k in range(1, x.shape[1] // 128):
        out = out + s[:, 128 * k:128 * (k + 1)]
    return out


def _swap16(x):
    return jnp.where((_lane(x.shape) & 16) == 0, pltpu.roll(x, 112, 1), pltpu.roll(x, 16, 1))


def _rope(x, cos, sin):
    return x * cos + _swap16(x) * sin


def _rope_t(d, cos, sin):
    return d * cos + _swap16(d * sin)


def _blockdiag64():
    r = lax.broadcasted_iota(jnp.int32, (128, 128), 0) // 64
    c = lax.broadcasted_iota(jnp.int32, (128, 128), 1) // 64
    return (r == c).astype(BF16)


def _seg64(x, m):
    hi = x.astype(BF16)
    lo = (x - hi.astype(F32)).astype(BF16)
    return _nn(hi, m) + _nn(lo, m)


def _rope_tables(seq):
    t = jnp.arange(seq)
    row = (t // GRID_W).astype(F32)
    col = (t % GRID_W).astype(F32)
    half = HEAD_DIM // 2
    inv_freq = ROPE_THETA ** (-jnp.arange(0, half, 2, dtype=F32) / half)
    ar = row[:, None] * inv_freq[None, :]
    ac = col[:, None] * inv_freq[None, :]
    cr, sr, cc, sc = jnp.cos(ar), jnp.sin(ar), jnp.cos(ac), jnp.sin(ac)
    cos64 = jnp.concatenate([cr, cr, cc, cc], axis=1)
    sin64 = jnp.concatenate([-sr, sr, -sc, sc], axis=1)
    return jnp.tile(cos64, (1, 2)), jnp.tile(sin64, (1, 2))


def _prenorm(x, mod, g_pre):
    seq = x.shape[0]
    bs = 512

    def body(x_ref, mod_ref, g_ref, h_ref):
        xv = x_ref[...]
        r = lax.rsqrt(jnp.mean(xv * xv, axis=-1, keepdims=True) + EPS)
        shift = mod_ref[:, 0:D_MODEL]
        scale = mod_ref[:, D_MODEL:2 * D_MODEL]
        h_ref[...] = ((xv * r) * g_ref[...] * (1.0 + scale) + shift).astype(h_ref.dtype)

    return pl.pallas_call(
        body, grid=(seq // bs,), name="prenorm",
        in_specs=[pl.BlockSpec((bs, D_MODEL), lambda i: (i, 0)), pl.BlockSpec((1, 3 * D_MODEL), lambda i: (0, 0)),
                  pl.BlockSpec((1, D_MODEL), lambda i: (0, 0))],
        out_specs=pl.BlockSpec((bs, D_MODEL), lambda i: (i, 0)),
        out_shape=jax.ShapeDtypeStruct((seq, D_MODEL), MXU_DTYPE), compiler_params=_cp("parallel"))(x, mod, g_pre)


def _prenorm_bwd(x, dh, dout, mod, g_pre):
    seq = x.shape[0]
    bs = 512

    def body(x_ref, dh_ref, dout_ref, mod_ref, g_ref, gx_ref, dshift_ref, dscale_ref, dg_ref):
        @pl.when(pl.program_id(0) == 0)
        def _():
            dshift_ref[...] = jnp.zeros_like(dshift_ref)
            dscale_ref[...] = jnp.zeros_like(dscale_ref)
            dg_ref[...] = jnp.zeros_like(dg_ref)

        xv = x_ref[...]
        dh = dh_ref[...]
        g = g_ref[...]
        r = lax.rsqrt(jnp.mean(xv * xv, axis=-1, keepdims=True) + EPS)
        xn = xv * r
        scale = mod_ref[:, D_MODEL:2 * D_MODEL]
        dshift_ref[...] += jnp.sum(dh, axis=0, keepdims=True)
        dscale_ref[...] += jnp.sum(dh * (xn * g), axis=0, keepdims=True)
        da = dh * (1.0 + scale)
        dg_ref[...] += jnp.sum(da * xn, axis=0, keepdims=True)
        dxn = da * g
        dx = r * (dxn - xn * jnp.mean(dxn * xn, axis=-1, keepdims=True))
        gx_ref[...] = dout_ref[...] + dx

    row = pl.BlockSpec((bs, D_MODEL), lambda i: (i, 0))
    vec = pl.BlockSpec((1, D_MODEL), lambda i: (0, 0))
    return pl.pallas_call(
        body, grid=(seq // bs,), name="prenorm_bwd",
        in_specs=[row, row, row, pl.BlockSpec((1, 3 * D_MODEL), lambda i: (0, 0)), vec],
        out_specs=[row, vec, vec, vec],
        out_shape=[jax.ShapeDtypeStruct((seq, D_MODEL), F32)] + [jax.ShapeDtypeStruct((1, D_MODEL), F32)] * 3,
        compiler_params=_cp("arbitrary"))(x, dh, dout, mod, g_pre)


def _matmul(a, b, *, ta, tb, tm, tn, out_dtype, name):
    kdim = a.shape[0] if ta else a.shape[1]
    m = a.shape[1] if ta else a.shape[0]
    n = b.shape[0] if tb else b.shape[1]
    assert m % tm == 0 and n % tn == 0

    def body(a_ref, b_ref, o_ref):
        dims = (((0 if ta else 1,), (1 if tb else 0,)), ((), ()))
        o_ref[...] = lax.dot_general(a_ref[...], b_ref[...], dims, preferred_element_type=F32).astype(o_ref.dtype)

    a_spec = pl.BlockSpec((kdim, tm), lambda j, i: (0, i)) if ta else pl.BlockSpec((tm, kdim), lambda j, i: (i, 0))
    b_spec = pl.BlockSpec((tn, kdim), lambda j, i: (j, 0)) if tb else pl.BlockSpec((kdim, tn), lambda j, i: (0, j))
    return pl.pallas_call(
        body, grid=(n // tn, m // tm), name=name, in_specs=[a_spec, b_spec],
        out_specs=pl.BlockSpec((tm, tn), lambda j, i: (i, j)),
        out_shape=jax.ShapeDtypeStruct((m, n), out_dtype), compiler_params=_cp("parallel", "parallel"))(a, b)


def _prep(p, cos, sin, qg2, kg2):
    seq = p.shape[0]
    bs = 512

    def body(p_ref, cos_ref, sin_ref, qg_ref, kg_ref, qt_ref, kd_ref, vd_ref, rq_ref, rk_ref, rv_ref):
        m = _blockdiag64()
        cs, sn = cos_ref[...], sin_ref[...]
        lo = _lane((bs, 128)) < 64
        for pr in range(4):
            q = p_ref[:, QA + 128 * pr:QA + 128 * (pr + 1)]
            r = lax.rsqrt(_seg64(q * q, m) * (1.0 / 64) + EPS)
            qt_ref[:, 128 * pr:128 * (pr + 1)] = (_rope(q * r * qg_ref[...], cs, sn) * (0.125 * LOG2E)).astype(qt_ref.dtype)
        k = p_ref[:, KA:KA + 128]
        r = lax.rsqrt(_seg64(k * k, m) * (1.0 / 64) + EPS)
        kr = _rope(k * r * kg_ref[...], cs, sn)
        ksw = pltpu.roll(kr, 64, 1)
        kd_ref[0] = jnp.where(lo, kr, ksw).astype(kd_ref.dtype)
        kd_ref[1] = jnp.where(lo, ksw, kr).astype(kd_ref.dtype)
        v = p_ref[:, VA:VA + 128]
        vsw = pltpu.roll(v, 64, 1)
        vd_ref[0] = jnp.where(lo, v, vsw).astype(vd_ref.dtype)
        vd_ref[1] = jnp.where(lo, vsw, v).astype(vd_ref.dtype)
        for pr in range(2):
            sl = slice(128 * pr, 128 * (pr + 1))
            rq_ref[:, sl] = _rope(p_ref[:, QR + 128 * pr:QR + 128 * (pr + 1)], cs, sn).astype(rq_ref.dtype)
            rk_ref[:, sl] = (_rope(p_ref[:, KR + 128 * pr:KR + 128 * (pr + 1)], cs, sn) * 0.125).astype(rk_ref.dtype)
        rv_ref[...] = p_ref[:, VR:VR + 512].astype(rv_ref.dtype)

    tab = pl.BlockSpec((bs, 128), lambda i: (i, 0))
    gsp = pl.BlockSpec((1, 128), lambda i: (0, 0))
    dup = pl.BlockSpec((2, bs, 128), lambda i: (0, i, 0))
    return pl.pallas_call(
        body, grid=(seq // bs,), name="mixer_prep",
        in_specs=[pl.BlockSpec((bs, ZR), lambda i: (i, 0)), tab, tab, gsp, gsp],
        out_specs=[pl.BlockSpec((bs, 512), lambda i: (i, 0)), dup, dup, pl.BlockSpec((bs, 256), lambda i: (i, 0)),
                   pl.BlockSpec((bs, 256), lambda i: (i, 0)), pl.BlockSpec((bs, 512), lambda i: (i, 0))],
        out_shape=[jax.ShapeDtypeStruct((seq, 512), MXU_DTYPE), jax.ShapeDtypeStruct((2, seq, 128), MXU_DTYPE),
                   jax.ShapeDtypeStruct((2, seq, 128), MXU_DTYPE), jax.ShapeDtypeStruct((seq, 256), MXU_DTYPE),
                   jax.ShapeDtypeStruct((seq, 256), MXU_DTYPE), jax.ShapeDtypeStruct((seq, 512), MXU_DTYPE)],
        compiler_params=_cp("parallel"))(p, cos, sin, qg2, kg2)


def _halves(kd):
    lo = _lane(kd.shape) < 64
    z = jnp.zeros_like(kd)
    return jnp.concatenate([jnp.where(lo, kd, z), jnp.where(lo, z, kd)], axis=0)


def _attn_fwd(qt, kd, vd):
    seq = qt.shape[0]
    bq = bk = 512
    nk = seq // bk

    def body(q_ref, k_ref, v_ref, o_ref, lse_ref, m_scr, l_scr, acc):
        j, n = pl.program_id(1), pl.program_id(2)

        @pl.when(n == 0)
        def _():
            m_scr[...] = jnp.full_like(m_scr, -jnp.inf)
            l_scr[...] = jnp.zeros_like(l_scr)
            acc[...] = jnp.zeros_like(acc)

        k2, v2 = _halves(k_ref[...]), _halves(v_ref[...])
        lo = _lane((bq, 128)) < 64
        s2s = [_nt(q_ref[:, 128 * pr:128 * (pr + 1)], k2) for pr in range(2)]
        for pr in range(2):
            s2 = s2s[pr]
            ps, alphas = [], []
            for e in range(2):
                g = 2 * pr + e
                s = s2[:, e * bk:(e + 1) * bk]
                m_prev = m_scr[g]
                m_next = jnp.maximum(m_prev, jnp.max(s, axis=1, keepdims=True))
                alpha = jnp.exp2(m_prev - m_next)
                pe = jnp.exp2(s - jnp.tile(m_next, (1, bk // 128)))
                l_scr[g] = alpha * l_scr[g] + jnp.sum(pe, axis=1, keepdims=True)
                m_scr[g] = m_next
                ps.append(_mx(pe))
                alphas.append(alpha)
            o2 = _nn(jnp.concatenate(ps, axis=1), v2)
            sl = slice(128 * pr, 128 * (pr + 1))
            acc[:, sl] = acc[:, sl] * jnp.where(lo, alphas[0], alphas[1]) + o2

        @pl.when(n == nk - 1)
        def _():
            for pr in range(2):
                sl = slice(128 * pr, 128 * (pr + 1))
                o_ref[:, sl] = acc[:, sl] / jnp.where(lo, l_scr[2 * pr], l_scr[2 * pr + 1])
            for g in range(4):
                lse = jnp.transpose(m_scr[g] + jnp.log2(l_scr[g]))
                lse_ref[pl.ds(4 * j + g, 1), :] = lse[0:1, :]

    return pl.pallas_call(
        body, grid=(seq // bq, 2, nk), name="attn_fwd",
        in_specs=[pl.BlockSpec((bq, 256), lambda i, j, n: (i, j)), pl.BlockSpec((None, bk, 128), lambda i, j, n: (j, n, 0)),
                  pl.BlockSpec((None, bk, 128), lambda i, j, n: (j, n, 0))],
        out_specs=[pl.BlockSpec((bq, 256), lambda i, j, n: (i, j)), pl.BlockSpec((8, bq), lambda i, j, n: (0, i))],
        out_shape=[jax.ShapeDtypeStruct((seq, 512), F32), jax.ShapeDtypeStruct((8, seq), F32)],
        scratch_shapes=[pltpu.VMEM((4, bq, 128), F32), pltpu.VMEM((4, bq, 128), F32), pltpu.VMEM((bq, 256), F32)],
        compiler_params=_cp("parallel", "arbitrary", "arbitrary"))(qt, kd, vd)


def _attn_bwd(qt, kd, vd, do, lse, delta):
    seq = qt.shape[0]
    bq = bk = 512
    nq, nk = seq // bq, seq // bk

    def body(q_ref, do_ref, k_ref, v_ref, lse_ref, del_ref, dq_ref, dk_ref, dv_ref):
        j, i, n = pl.program_id(0), pl.program_id(1), pl.program_id(2)

        @pl.when(n == 0)
        def _():
            dq_ref[...] = jnp.zeros_like(dq_ref)

        @pl.when((i == 0) & (n == 0))
        def _():
            dk_ref[...] = jnp.zeros_like(dk_ref)
            dv_ref[...] = jnp.zeros_like(dv_ref)

        k2, v2 = _halves(k_ref[...]), _halves(v_ref[...])
        lo = _lane((bk, 128)) < 64
        rows = pl.ds(pl.multiple_of(n * bk, bk), bk)
        sls = [slice(128 * pr, 128 * (pr + 1)) for pr in range(2)]
        s_ts = [_nt(k2, q_ref[:, sl]) for sl in sls]
        dp_ts = [_nt(v2, do_ref[:, sl]) for sl in sls]
        big = lambda r: jnp.concatenate([jnp.broadcast_to(r[0], (bk, bq)), jnp.broadcast_to(r[1], (bk, bq))], axis=0)
        for pr in range(2):
            sl = sls[pr]
            qp, dop = q_ref[:, sl], do_ref[:, sl]
            ls = [lse_ref[pl.ds(4 * j + 2 * pr + e, 1), :] for e in range(2)]
            dl = [del_ref[pl.ds(4 * j + 2 * pr + e, 1), :] for e in range(2)]
            p_t = jnp.exp2(s_ts[pr] - big(ls))
            ds_t = _mx(p_t * (dp_ts[pr] - big(dl)))
            rv = _nn(_mx(p_t), dop)
            rk = _nn(ds_t, qp) * LN2
            dv_ref[rows, :] += jnp.where(lo, rv[:bk], rv[bk:])
            dk_ref[rows, :] += jnp.where(lo, rk[:bk], rk[bk:])
            dq_ref[:, sl] += _tn(ds_t, k2)

    return pl.pallas_call(
        body, grid=(2, nq, nk), name="attn_bwd",
        in_specs=[pl.BlockSpec((bq, 256), lambda j, i, n: (i, j)), pl.BlockSpec((bq, 256), lambda j, i, n: (i, j)),
                  pl.BlockSpec((None, bk, 128), lambda j, i, n: (j, n, 0)), pl.BlockSpec((None, bk, 128), lambda j, i, n: (j, n, 0)),
                  pl.BlockSpec((8, bq), lambda j, i, n: (0, i)), pl.BlockSpec((8, bq), lambda j, i, n: (0, i))],
        out_specs=[pl.BlockSpec((bq, 256), lambda j, i, n: (i, j)), pl.BlockSpec((None, seq, 128), lambda j, i, n: (j, 0, 0)),
                   pl.BlockSpec((None, seq, 128), lambda j, i, n: (j, 0, 0))],
        out_shape=[jax.ShapeDtypeStruct((seq, 512), F32), jax.ShapeDtypeStruct((2, seq, 128), F32),
                   jax.ShapeDtypeStruct((2, seq, 128), F32)],
        compiler_params=_cp("arbitrary", "arbitrary", "arbitrary"))(qt, do, kd, vd, lse, delta)


def _ret_tables(lg_f, lg_b, c):
    idx = jnp.arange(c, dtype=F32)
    diff = idx[:, None] - idx[None, :]
    a, b = lg_f[:, None, None], lg_b[:, None, None]
    low, up = (diff >= 0)[None], (diff < 0)[None]
    dmat = jnp.where(low, jnp.exp(a * jnp.maximum(diff, 0.0)[None]), jnp.exp(b * jnp.maximum(-diff, 0.0)[None]))
    dda = jnp.where(low, diff[None] * jnp.exp(a * jnp.maximum(diff, 0.0)[None]), 0.0)
    ddb = jnp.where(up, -diff[None] * jnp.exp(b * jnp.maximum(-diff, 0.0)[None]), 0.0)
    rep = lambda w: jnp.broadcast_to(w[:, :, None], (RET_HEADS, c, 128))
    wqf = rep(jnp.exp(lg_f[:, None] * (idx + 1.0)[None]))
    wqb = rep(jnp.exp(lg_b[:, None] * (c - idx)[None]))
    wkf = rep(jnp.exp(lg_f[:, None] * (c - 1.0 - idx)[None]))
    wkb = rep(jnp.exp(lg_b[:, None] * idx[None]))
    cdf, cdb = jnp.exp(lg_f * c), jnp.exp(lg_b * c)
    dec_fb = jnp.broadcast_to(jnp.concatenate([cdf, cdb])[:, None], (8, 128))
    dec_bf = jnp.broadcast_to(jnp.concatenate([cdb, cdf])[:, None], (8, 128))
    return dict(dmat=dmat, dda=dda, ddb=ddb, wqf=wqf, wqb=wqb, wkf=wkf, wkb=wkb, dec_fb=dec_fb, dec_bf=dec_bf)


def _ret_states(xk, yv, w_fwd, w_rev, dec, name):
    seq = xk.shape[0]
    c = RET_CHUNK
    nc = seq // c

    def body(xf_ref, yf_ref, xr_ref, yr_ref, wf_ref, wr_ref, dec_ref, sf_ref, sr_ref, st_f, st_r):
        @pl.when(pl.program_id(0) == 0)
        def _():
            st_f[...] = jnp.zeros_like(st_f)
            st_r[...] = jnp.zeros_like(st_r)

        lane = _lane((c, 128))
        for h in range(RET_HEADS):
            pr, e = h // 2, h % 2
            half = (lane < 64) if e == 0 else (lane >= 64)
            for x_ref, y_ref, w_ref, s_ref, st, drow in ((xf_ref, yf_ref, wf_ref, sf_ref, st_f, h),
                                                        (xr_ref, yr_ref, wr_ref, sr_ref, st_r, 4 + h)):
                s_ref[h] = st[h].astype(s_ref.dtype)
                xm = jnp.where(half, x_ref[:, 128 * pr:128 * (pr + 1)].astype(F32) * w_ref[h], 0.0)
                st[h] = st[h] * dec_ref[drow:drow + 1, :] + _tn(_mx(xm), _mx(y_ref[:, 128 * h:128 * (h + 1)]))

    xs = lambda f: pl.BlockSpec((c, 256), f)
    ys = lambda f: pl.BlockSpec((c, 512), f)
    fwd, rev = (lambda n: (n, 0)), (lambda n: (nc - 1 - n, 0))
    wsp = pl.BlockSpec((RET_HEADS, c, 128), lambda n: (0, 0, 0))
    return pl.pallas_call(
        body, grid=(nc,), name=name,
        in_specs=[xs(fwd), ys(fwd), xs(rev), ys(rev), wsp, wsp, pl.BlockSpec((8, 128), lambda n: (0, 0))],
        out_specs=[pl.BlockSpec((None, RET_HEADS, 128, 128), lambda n: (n, 0, 0, 0)),
                   pl.BlockSpec((None, RET_HEADS, 128, 128), lambda n: (nc - 1 - n, 0, 0, 0))],
        out_shape=[jax.ShapeDtypeStruct((nc, RET_HEADS, 128, 128), MXU_DTYPE)] * 2,
        scratch_shapes=[pltpu.VMEM((RET_HEADS, 128, 128), F32), pltpu.VMEM((RET_HEADS, 128, 128), F32)],
        compiler_params=_cp("arbitrary"))(xk, yv, xk, yv, w_fwd, w_rev, dec)


def _ret_fwd(rq, rk, rv, rf, rb, tb):
    seq = rq.shape[0]
    c = RET_CHUNK

    def body(q_ref, k_ref, v_ref, rf_ref, rb_ref, d_ref, wqf_ref, wqb_ref, o_ref):
        lane = _lane((c, 128))
        for h in range(RET_HEADS):
            pr, e = h // 2, h % 2
            half = (lane < 64) if e == 0 else (lane >= 64)
            sl = slice(128 * pr, 128 * (pr + 1))
            qp = q_ref[:, sl]
            km = jnp.where(half, k_ref[:, sl], jnp.zeros_like(k_ref[:, sl]))
            sd = _mx(_nt(qp, km) * d_ref[h])
            qf = qp.astype(F32)
            o = _nn(sd, v_ref[:, 128 * h:128 * (h + 1)])
            o = o + _nn(_mx(qf * wqf_ref[h]), rf_ref[h]) + _nn(_mx(qf * wqb_ref[h]), rb_ref[h])
            o_ref[:, 128 * h:128 * (h + 1)] = o

    st = pl.BlockSpec((None, RET_HEADS, 128, 128), lambda n: (n, 0, 0, 0))
    wsp = pl.BlockSpec((RET_HEADS, c, 128), lambda n: (0, 0, 0))
    return pl.pallas_call(
        body, grid=(seq // c,), name="ret_fwd",
        in_specs=[pl.BlockSpec((c, 256), lambda n: (n, 0)), pl.BlockSpec((c, 256), lambda n: (n, 0)),
                  pl.BlockSpec((c, 512), lambda n: (n, 0)), st, st, pl.BlockSpec((RET_HEADS, c, c), lambda n: (0, 0, 0)), wsp, wsp],
        out_specs=pl.BlockSpec((c, 512), lambda n: (n, 0)),
        out_shape=jax.ShapeDtypeStruct((seq, 512), F32), compiler_params=_cp("parallel"))(
            rq, rk, rv, rf, rb, tb["dmat"], tb["wqf"], tb["wqb"])


def _ret_bwd(rq, rk, rv, do, rf, rb, hf, hb, tb):
    seq = rq.shape[0]
    c = RET_CHUNK

    def body(q_ref, k_ref, v_ref, do_ref, rf_ref, rb_ref, hf_ref, hb_ref, d_ref, dda_ref, ddb_ref,
             wqf_ref, wqb_ref, wkf_ref, wkb_ref, cdec_ref, dq_ref, dk_ref, dv_ref, dlg_ref):
        @pl.when(pl.program_id(0) == 0)
        def _():
            dlg_ref[...] = jnp.zeros_like(dlg_ref)

        lane = _lane((c, 128))
        pos = lax.broadcasted_iota(jnp.int32, (c, 128), 0).astype(F32)
        for pr in range(2):
            sl = slice(128 * pr, 128 * (pr + 1))
            qp, kp = q_ref[:, sl], k_ref[:, sl]
            qf, kf = qp.astype(F32), kp.astype(F32)
            dq_acc = jnp.zeros((c, 128), F32)
            dk_acc = jnp.zeros((c, 128), F32)
            for e in range(2):
                h = 2 * pr + e
                half = (lane < 64) if e == 0 else (lane >= 64)
                hs = slice(128 * h, 128 * (h + 1))
                km = jnp.where(half, kp, jnp.zeros_like(kp))
                kmf = km.astype(F32)
                vh, doh = v_ref[:, hs], do_ref[:, hs]
                rfh, rbh, hfh, hbh = rf_ref[h], rb_ref[h], hf_ref[h], hb_ref[h]
                s = _nt(qp, km)
                dpm = _nt(doh, vh)
                ds_b = _mx(dpm * d_ref[h])
                sd_b = _mx(s * d_ref[h])
                dq_if = wqf_ref[h] * _nt(doh, rfh)
                dq_ib = wqb_ref[h] * _nt(doh, rbh)
                dq_acc = dq_acc + _nn(ds_b, km) + dq_if + dq_ib
                dk_sf = wkf_ref[h] * _nt(vh, hfh)
                dk_sb = wkb_ref[h] * _nt(vh, hbh)
                dk_acc = dk_acc + jnp.where(half, _tn(ds_b, qp), 0.0) + dk_sf + dk_sb
                dv = _tn(sd_b, doh) + _nn(_mx(kmf * wkf_ref[h]), hfh) + _nn(_mx(kmf * wkb_ref[h]), hbh)
                dv_ref[:, hs] = dv.astype(dv_ref.dtype)
                sdp = s * dpm
                hr_f = hfh.astype(F32) * rfh.astype(F32)
                hr_b = hbh.astype(F32) * rbh.astype(F32)
                da = (_rowsum128(sdp * dda_ref[h]) + _rowsum128((pos + 1.0) * qf * dq_if)
                      + _rowsum128((c - 1.0 - pos) * kf * dk_sf) + cdec_ref[h:h + 1, :] * _rowsum128(hr_f))
                db = (_rowsum128(sdp * ddb_ref[h]) + _rowsum128((c - pos) * qf * dq_ib)
                      + _rowsum128(pos * kf * dk_sb) + cdec_ref[4 + h:5 + h, :] * _rowsum128(hr_b))
                dlg_ref[h:h + 1, :] += da
                dlg_ref[4 + h:5 + h, :] += db
            dq_ref[:, sl] = dq_acc
            dk_ref[:, sl] = dk_acc

    st = pl.BlockSpec((None, RET_HEADS, 128, 128), lambda n: (n, 0, 0, 0))
    wsp = pl.BlockSpec((RET_HEADS, c, 128), lambda n: (0, 0, 0))
    dsp = pl.BlockSpec((RET_HEADS, c, c), lambda n: (0, 0, 0))
    x256 = pl.BlockSpec((c, 256), lambda n: (n, 0))
    x512 = pl.BlockSpec((c, 512), lambda n: (n, 0))
    cdec = tb["dec_fb"] * float(c)
    return pl.pallas_call(
        body, grid=(seq // c,), name="ret_bwd",
        in_specs=[x256, x256, x512, x512, st, st, st, st, dsp, dsp, dsp, wsp, wsp, wsp, wsp,
                  pl.BlockSpec((8, 128), lambda n: (0, 0))],
        out_specs=[x256, x256, x512, pl.BlockSpec((8, 128), lambda n: (0, 0))],
        out_shape=[jax.ShapeDtypeStruct((seq, 256), F32), jax.ShapeDtypeStruct((seq, 256), F32),
                   jax.ShapeDtypeStruct((seq, 512), MXU_DTYPE), jax.ShapeDtypeStruct((8, 128), F32)],
        compiler_params=_cp("arbitrary"))(
            rq, rk, rv, do, rf, rb, hf, hb, tb["dmat"], tb["dda"], tb["ddb"], tb["wqf"], tb["wqb"], tb["wkf"], tb["wkb"], cdec)


def _tail(x, tgt, o_att, o_ret, p, mod, g_post, gn_g, wpt, wout):
    seq = x.shape[0]
    bs = 256
    nb = seq // bs

    def body(x_ref, t_ref, oa_ref, or_ref, za_ref, zr_ref, gl_ref, mod_ref, gp_ref, gn_ref, wpt_ref, wout_ref,
             dout_ref, dza_ref, dzr_ref, dgl_ref, doa_ref, dor_ref, del_ref, gout_ref, gpt_ref,
             dgate_ref, dgpost_ref, dgn_ref, loss_ref, gout_acc, gpt_acc):
        i = pl.program_id(0)

        @pl.when(i == 0)
        def _():
            gout_acc[...] = jnp.zeros_like(gout_acc)
            gpt_acc[...] = jnp.zeros_like(gpt_acc)
            dgate_ref[...] = jnp.zeros_like(dgate_ref)
            dgpost_ref[...] = jnp.zeros_like(dgpost_ref)
            dgn_ref[...] = jnp.zeros_like(dgn_ref)
            loss_ref[...] = jnp.zeros_like(loss_ref)

        oa, za, zr = oa_ref[...], za_ref[...], zr_ref[...]
        sga, sgr = _sigmoid(za), _sigmoid(zr)
        sza, szr = za * sga, zr * sgr
        ya_b = _mx(oa * sza)
        ons, rstds = [], []
        for h in range(RET_HEADS):
            oh = or_ref[:, 128 * h:128 * (h + 1)]
            xc = oh - jnp.mean(oh, axis=-1, keepdims=True)
            rstd = lax.rsqrt(jnp.mean(xc * xc, axis=-1, keepdims=True) + EPS)
            ons.append(xc * rstd)
            rstds.append(rstd)
        on = jnp.concatenate(ons, axis=1)
        yn = on * gn_ref[...]
        yr_b = _mx(yn * szr)
        wpa_t, wpr_t = wpt_ref[:, 0:512], wpt_ref[:, 512:1024]
        a_att = _nt(ya_b, wpa_t)
        a_ret = _nt(yr_b, wpr_t)
        gts = _sigmoid(gl_ref[...])
        g_att, g_ret = gts[:, 0:D_MODEL], gts[:, D_MODEL:2 * D_MODEL]
        merged_b = _mx(g_att * a_att + g_ret * a_ret)
        u = _nn(merged_b, wout_ref[...])
        r = lax.rsqrt(jnp.mean(u * u, axis=-1, keepdims=True) + EPS)
        un = u * r
        gpost = gp_ref[...]
        y = un * gpost
        gate = mod_ref[:, 2 * D_MODEL:3 * D_MODEL]
        err = x_ref[...] + gate * y - t_ref[...]
        loss_ref[...] += (0.5 / D_MODEL) * _rowsum128(err * err)
        dout = err * (1.0 / D_MODEL)
        dout_ref[...] = dout
        dgate_ref[...] += jnp.sum(dout * y, axis=0, keepdims=True)
        dy = dout * gate
        dgpost_ref[...] += jnp.sum(dy * un, axis=0, keepdims=True)
        dun = dy * gpost
        du_b = _mx(r * (dun - un * jnp.mean(dun * un, axis=-1, keepdims=True)))
        dmerged = _nt(du_b, wout_ref[...])
        gout_acc[...] += _tn(merged_b, du_b)
        d_att, d_ret = dmerged * g_att, dmerged * g_ret
        dgl_ref[:, 0:D_MODEL] = (d_att * a_att * (1.0 - g_att)).astype(dgl_ref.dtype)
        dgl_ref[:, D_MODEL:2 * D_MODEL] = (d_ret * a_ret * (1.0 - g_ret)).astype(dgl_ref.dtype)
        d_att_b, d_ret_b = _mx(d_att), _mx(d_ret)
        dya = _nn(d_att_b, wpa_t)
        dyr = _nn(d_ret_b, wpr_t)
        gpt_acc[:, 0:512] += _tn(d_att_b, ya_b)
        gpt_acc[:, 512:1024] += _tn(d_ret_b, yr_b)
        dza_ref[...] = (dya * oa * (sga * (1.0 + za * (1.0 - sga)))).astype(dza_ref.dtype)
        doa = dya * sza
        doa_ref[...] = doa.astype(doa_ref.dtype)
        dt = jnp.transpose(doa * oa)
        del_ref[...] = jnp.sum(dt.reshape(8, HEAD_DIM, bs), axis=1)
        dzr_ref[...] = (dyr * yn * (sgr * (1.0 + zr * (1.0 - sgr)))).astype(dzr_ref.dtype)
        dyn = dyr * szr
        dgn_ref[...] += jnp.sum(dyn * on, axis=0, keepdims=True)
        don = dyn * gn_ref[...]
        for h in range(RET_HEADS):
            hs = slice(128 * h, 128 * (h + 1))
            dh, oh = don[:, hs], ons[h]
            doh = rstds[h] * (dh - jnp.mean(dh, axis=-1, keepdims=True) - oh * jnp.mean(dh * oh, axis=-1, keepdims=True))
            dor_ref[:, hs] = doh.astype(dor_ref.dtype)

        @pl.when(i == nb - 1)
        def _():
            gout_ref[...] = gout_acc[...].astype(gout_ref.dtype)
            gpt_ref[...] = gpt_acc[...].astype(gpt_ref.dtype)

    row = lambda w: pl.BlockSpec((bs, w), lambda i: (i, 0))
    el = lambda w, off: pl.BlockSpec((pl.Element(bs), pl.Element(w)), lambda i: (i * bs, off))
    vec = lambda w: pl.BlockSpec((1, w), lambda i: (0, 0))
    full = pl.BlockSpec((D_MODEL, D_MODEL), lambda i: (0, 0))
    sds = jax.ShapeDtypeStruct
    return pl.pallas_call(
        body, grid=(nb,), name="tail",
        in_specs=[row(D_MODEL), row(D_MODEL), row(512), row(512), el(512, ZA), el(512, ZR), el(2 * D_MODEL, GL),
                  vec(3 * D_MODEL), vec(D_MODEL), vec(512), full, full],
        out_specs=[row(D_MODEL), row(512), row(512), row(2 * D_MODEL), row(512), row(512),
                   pl.BlockSpec((8, bs), lambda i: (0, i)), full, full, vec(D_MODEL), vec(D_MODEL), vec(512), vec(128)],
        out_shape=[sds((seq, D_MODEL), F32), sds((seq, 512), MXU_DTYPE), sds((seq, 512), MXU_DTYPE),
                   sds((seq, 2 * D_MODEL), MXU_DTYPE), sds((seq, 512), MXU_DTYPE), sds((seq, 512), MXU_DTYPE),
                   sds((8, seq), F32), sds((D_MODEL, D_MODEL), MXU_DTYPE), sds((D_MODEL, D_MODEL), MXU_DTYPE),
                   sds((1, D_MODEL), F32), sds((1, D_MODEL), F32), sds((1, 512), F32), sds((1, 128), F32)],
        scratch_shapes=[pltpu.VMEM((D_MODEL, D_MODEL), F32), pltpu.VMEM((D_MODEL, D_MODEL), F32)],
        compiler_params=_cp("arbitrary"))(x, tgt, o_att, o_ret, p, p, p, mod, g_post, gn_g, wpt, wout)


def _assemble(p, cos, sin, qg2, kg2, dqt, dkp, dvp, dza, drq, drk, drv, dzr, dgl):
    seq = p.shape[0]
    bs = 512

    def body(p_ref, cos_ref, sin_ref, qg_ref, kg_ref, dqt_ref, dkp_ref, dvp_ref, dza_ref, drq_ref, drk_ref, drv_ref,
             dzr_ref, dgl_ref, dp_ref, dqg_ref, dkg_ref):
        @pl.when(pl.program_id(0) == 0)
        def _():
            dqg_ref[...] = jnp.zeros_like(dqg_ref)
            dkg_ref[...] = jnp.zeros_like(dkg_ref)

        m = _blockdiag64()
        cs, sn = cos_ref[...], sin_ref[...]
        lo = _lane((bs, 128)) < 64

        def norm_bwd(raw, dn, g, dg_ref):
            r = lax.rsqrt(_seg64(raw * raw, m) * (1.0 / 64) + EPS)
            xn = raw * r
            dg_ref[...] += jnp.sum(dn * xn, axis=0, keepdims=True)
            dxn = dn * g
            return r * (dxn - xn * (_seg64(dxn * xn, m) * (1.0 / 64)))

        for pr in range(4):
            sl = slice(128 * pr, 128 * (pr + 1))
            dn = _rope_t(dqt_ref[:, sl] * 0.125, cs, sn)
            dp_ref[:, QA + 128 * pr:QA + 128 * (pr + 1)] = norm_bwd(p_ref[:, sl], dn, qg_ref[...], dqg_ref).astype(dp_ref.dtype)
        fold = lambda a: a + pltpu.roll(a, 64, 1)
        dk = jnp.where(lo, fold(dkp_ref[0]), fold(dkp_ref[1]))
        dp_ref[:, KA:KA + 128] = norm_bwd(p_ref[:, KA:KA + 128], _rope_t(dk, cs, sn), kg_ref[...], dkg_ref).astype(dp_ref.dtype)
        dp_ref[:, VA:VA + 128] = jnp.where(lo, fold(dvp_ref[0]), fold(dvp_ref[1])).astype(dp_ref.dtype)
        dp_ref[:, ZA:ZA + 512] = dza_ref[...]
        for pr in range(2):
            sl = slice(128 * pr, 128 * (pr + 1))
            dp_ref[:, QR + 128 * pr:QR + 128 * (pr + 1)] = _rope_t(drq_ref[:, sl], cs, sn).astype(dp_ref.dtype)
            dp_ref[:, KR + 128 * pr:KR + 128 * (pr + 1)] = _rope_t(drk_ref[:, sl] * 0.125, cs, sn).astype(dp_ref.dtype)
        dp_ref[:, VR:VR + 512] = drv_ref[...]
        dp_ref[:, ZR:ZR + 512] = dzr_ref[...]
        dp_ref[:, GL:GL + 2 * D_MODEL] = dgl_ref[...]

    row = lambda w: pl.BlockSpec((bs, w), lambda i: (i, 0))
    gsp = pl.BlockSpec((1, 128), lambda i: (0, 0))
    dup = pl.BlockSpec((2, bs, 128), lambda i: (0, i, 0))
    return pl.pallas_call(
        body, grid=(seq // bs,), name="assemble_dp",
        in_specs=[row(768), row(128), row(128), gsp, gsp, row(512), dup, dup, row(512), row(256), row(256), row(512),
                  row(512), row(2 * D_MODEL)],
        out_specs=[row(IN_WIDTH), gsp, gsp],
        out_shape=[jax.ShapeDtypeStruct((seq, IN_WIDTH), MXU_DTYPE), jax.ShapeDtypeStruct((1, 128), F32),
                   jax.ShapeDtypeStruct((1, 128), F32)],
        compiler_params=_cp("arbitrary"))(p, cos, sin, qg2, kg2, dqt, dkp, dvp, dza, drq, drk, drv, dzr, dgl)


def _local_step(x, tgt, mod, g_pre, qn_g, kn_g, w_dec_f, w_dec_b, gn_g, g_post, win_t, wpt, wout):
    seq = x.shape[0]
    cos, sin = _rope_tables(seq)
    qg2, kg2 = jnp.tile(qn_g, (1, 2)), jnp.tile(kn_g, (1, 2))
    lg_f = jax.nn.log_sigmoid(w_dec_f[0])
    lg_b = jax.nn.log_sigmoid(w_dec_b[0])
    tb = _ret_tables(lg_f, lg_b, RET_CHUNK)

    h = _prenorm(x, mod, g_pre)
    p = _matmul(h, win_t, ta=False, tb=True, tm=512, tn=IN_WIDTH // 2, out_dtype=F32, name="in_proj")
    qt, kd, vd, rq, rk, rv = _prep(p, cos, sin, qg2, kg2)
    o_att, lse = _attn_fwd(qt, kd, vd)
    rf, rb = _ret_states(rk, rv, tb["wkf"], tb["wkb"], tb["dec_fb"], "ret_states_fwd")
    o_ret = _ret_fwd(rq, rk, rv, rf, rb, tb)
    (dout, dza, dzr, dgl, do_att, do_ret, delta, g_out, g_pt, dgate, dgpost, dgn, loss_l) = _tail(
        x, tgt, o_att, o_ret, p, mod, g_post, gn_g, wpt, wout)
    dqt, dkp, dvp = _attn_bwd(qt, kd, vd, do_att, lse, delta)
    hb, hf = _ret_states(rq, do_ret, tb["wqb"], tb["wqf"], tb["dec_bf"], "ret_states_bwd")
    drq, drk, drv, dlg = _ret_bwd(rq, rk, rv, do_ret, rf, rb, hf, hb, tb)
    dp, dqg, dkg = _assemble(p, cos, sin, qg2, kg2, dqt, dkp, dvp, dza, drq, drk, drv, dzr, dgl)
    dh = _matmul(dp, win_t, ta=False, tb=False, tm=512, tn=D_MODEL, out_dtype=F32, name="in_proj_dx")
    g_in_t = _matmul(dp, h, ta=True, tb=False, tm=256, tn=D_MODEL, out_dtype=MXU_DTYPE, name="in_proj_dw")
    grad_x, dshift, dscale, dgpre = _prenorm_bwd(x, dh, dout, mod, g_pre)

    dlg = jnp.sum(dlg, axis=1)
    small = dict(
        dmod=jnp.concatenate([dshift, dscale, dgate], axis=1),
        g_pre=dgpre, g_post=dgpost, gn_g=dgn,
        qn_g=dqg[:, :64] + dqg[:, 64:], kn_g=dkg[:, :64] + dkg[:, 64:],
        w_dec_f=(dlg[0:4] * jax.nn.sigmoid(-w_dec_f[0]))[None], w_dec_b=(dlg[4:8] * jax.nn.sigmoid(-w_dec_b[0]))[None],
        loss=jnp.sum(loss_l, axis=1, keepdims=True))
    return grad_x, g_in_t, g_pt, g_out, small


N_DEV = 8
N_CHIP = 4
HBM_SPEC = pl.BlockSpec(memory_space=pl.ANY)
VMEM_SPEC = pl.BlockSpec(memory_space=pltpu.VMEM)
SHARD_ROWS = (IN_WIDTH // N_CHIP, D_MODEL // N_CHIP, D_MODEL // N_CHIP)


def _coords():
    return lax.axis_index("x"), lax.axis_index("y"), lax.axis_index("c")


def _peer(k):
    x, y, c = _coords()
    return (1 - x if k & 4 else x, 1 - y if k & 2 else y, 1 - c if k & 1 else c)


def _dev_index(p):
    return 4 * p[0] + 2 * p[1] + p[2]


def _rows(ref, start, size):
    return ref.at[pl.ds(pl.multiple_of(start, 16), size), :]


def _remote(src, dst, ssem, rsem, dev):
    return pltpu.make_async_remote_copy(src_ref=src, dst_ref=dst, send_sem=ssem, recv_sem=rsem, device_id=dev,
                                        device_id_type=MESH)


def _mod_exchange(c, w_ada_s, b4):
    ncol = w_ada_s.shape[1]

    def body(c_ref, w_ref, b_ref, cs_ref, mod_ref, modsh, modall, ssem, rsem):
        me = _dev_index(_coords())
        chip = me // 2
        cs_ref[me] = c_ref[...]
        sends = []
        for k in range(1, N_DEV):
            cp = _remote(c_ref, cs_ref.at[me], ssem.at[k - 1], rsem.at[k - 1], _peer(k))
            cp.start()
            sends.append(cp)
        for k in range(1, N_DEV):
            slot = cs_ref.at[_dev_index(_peer(k))]
            _remote(slot, slot, ssem.at[k - 1], rsem.at[k - 1], _peer(k)).wait_recv()
        cs = jnp.concatenate([cs_ref[d] for d in range(N_DEV)], axis=0)
        ms = _nn(cs * _sigmoid(cs), w_ref[...], precision=HIGHEST) + b_ref[pl.ds(chip, 1), :]
        modsh[...] = ms
        modall[chip] = ms
        for k2 in range(1, N_CHIP):
            cp = _remote(modsh, modall.at[chip], ssem.at[6 + k2], rsem.at[6 + k2], _peer(2 * k2))
            cp.start()
            sends.append(cp)
        for k2 in range(1, N_CHIP):
            slot = modall.at[_dev_index(_peer(2 * k2)) // 2]
            _remote(slot, slot, ssem.at[6 + k2], rsem.at[6 + k2], _peer(2 * k2)).wait_recv()
        for jj in range(N_CHIP):
            mod_ref[:, ncol * jj:ncol * (jj + 1)] = modall[jj, pl.ds(me, 1), :]
        for cp in sends:
            cp.wait_send()

    return pl.pallas_call(
        body, name="mod_exchange", in_specs=[VMEM_SPEC] * 3, out_specs=[VMEM_SPEC] * 2,
        out_shape=[jax.ShapeDtypeStruct((N_DEV, 1, D_MODEL), F32), jax.ShapeDtypeStruct((1, N_CHIP * ncol), F32)],
        scratch_shapes=[pltpu.VMEM((N_DEV, ncol), F32), pltpu.VMEM((N_CHIP, N_DEV, ncol), F32),
                        pltpu.SemaphoreType.DMA((10,)), pltpu.SemaphoreType.DMA((10,))],
        compiler_params=_cp())(c, w_ada_s, b4)


GATHER_CHUNK = 32


def _chunks(nt, chunk):
    out = []
    for t in range(nt):
        half = SHARD_ROWS[t] // 2
        step = chunk if half % chunk == 0 else half
        out += [(t, off, step) for off in range(0, half, step)]
    return out


def _weight_gather(shards):
    nt = len(shards)
    pieces = _chunks(nt, GATHER_CHUNK)
    npc = len(pieces)

    def body(*refs):
        srcs, outs = refs[:nt], refs[nt:2 * nt]
        lsem, ssem, rsem = refs[2 * nt:]
        x, y, c = _coords()
        chip = 2 * x + y
        sib = (x, y, 1 - c)

        def place(t, ch, cc, off, n):
            return _rows(outs[t], ch * SHARD_ROWS[t] + cc * (SHARD_ROWS[t] // 2) + off, n)

        local = [pltpu.make_async_copy(srcs[t].at[pl.ds(cc * (SHARD_ROWS[t] // 2) + off, n), :], place(t, chip, cc, off, n),
                                       lsem.at[cc * npc + q]) for cc in range(2) for q, (t, off, n) in enumerate(pieces)]
        for cp in local:
            cp.start()
        sends = []
        for k2 in range(1, N_CHIP):
            for q, (t, off, n) in enumerate(pieces):
                i = (k2 - 1) * npc + q
                cp = _remote(_rows(srcs[t], c * (SHARD_ROWS[t] // 2) + off, n), place(t, chip, c, off, n),
                             ssem.at[i], rsem.at[i], _peer(2 * k2))
                cp.start()
                sends.append(cp)
        for k2 in range(1, N_CHIP):
            pchip = _dev_index(_peer(2 * k2)) // 2
            for q, (t, off, n) in enumerate(pieces):
                i = (k2 - 1) * npc + q
                got = place(t, pchip, c, off, n)
                _remote(got, got, ssem.at[i], rsem.at[i], _peer(2 * k2)).wait_recv()
                cp = _remote(got, got, ssem.at[3 * npc + i], rsem.at[3 * npc + i], sib)
                cp.start()
                sends.append(cp)
        for k2 in range(1, N_CHIP):
            pchip = _dev_index(_peer(2 * k2)) // 2
            for q, (t, off, n) in enumerate(pieces):
                i = (k2 - 1) * npc + q
                got = place(t, pchip, 1 - c, off, n)
                _remote(got, got, ssem.at[3 * npc + i], rsem.at[3 * npc + i], sib).wait_recv()
        for cp in sends:
            cp.wait_send()
        for cp in local:
            cp.wait()

    return pl.pallas_call(
        body, name="weight_gather", in_specs=[HBM_SPEC] * nt, out_specs=[HBM_SPEC] * nt,
        out_shape=[jax.ShapeDtypeStruct((N_CHIP * s.shape[0], s.shape[1]), s.dtype) for s in shards],
        scratch_shapes=[pltpu.SemaphoreType.DMA((2 * npc,)), pltpu.SemaphoreType.DMA((6 * npc,)), pltpu.SemaphoreType.DMA((6 * npc,))],
        compiler_params=_cp())(*shards)


def _grad_scatter(grads):
    nt = len(grads)
    pieces = _chunks(nt, GATHER_CHUNK)

    def body(*refs):
        srcs, outs = refs[:nt], refs[nt:2 * nt]
        lsem, ssem, rsem = refs[2 * nt:]
        me = _dev_index(_coords())
        start = lambda t, dev: (dev // 2) * SHARD_ROWS[t] + (dev % 2) * (SHARD_ROWS[t] // 2)
        mine = lambda t, dev: _rows(srcs[t], start(t, dev), SHARD_ROWS[t] // 2)
        local = [pltpu.make_async_copy(_rows(srcs[t], start(t, me) + off, n), outs[t].at[me, pl.ds(off, n), :], lsem.at[q])
                 for q, (t, off, n) in enumerate(pieces)]
        for cp in local:
            cp.start()
        sends = []
        for k in range(1, N_DEV):
            for t in range(nt):
                i = (k - 1) * nt + t
                cp = _remote(mine(t, _dev_index(_peer(k))), outs[t].at[me], ssem.at[i], rsem.at[i], _peer(k))
                cp.start()
                sends.append(cp)
        for k in range(1, N_DEV):
            for t in range(nt):
                i = (k - 1) * nt + t
                slot = outs[t].at[_dev_index(_peer(k))]
                _remote(slot, slot, ssem.at[i], rsem.at[i], _peer(k)).wait_recv()
        for cp in sends:
            cp.wait_send()
        for cp in local:
            cp.wait()

    return pl.pallas_call(
        body, name="grad_scatter", in_specs=[HBM_SPEC] * nt, out_specs=[HBM_SPEC] * nt,
        out_shape=[jax.ShapeDtypeStruct((N_DEV, SHARD_ROWS[t] // 2, g.shape[1]), g.dtype) for t, g in enumerate(grads)],
        scratch_shapes=[pltpu.SemaphoreType.DMA((len(pieces),)), pltpu.SemaphoreType.DMA((7 * nt,)), pltpu.SemaphoreType.DMA((7 * nt,))],
        compiler_params=_cp())(*grads)


def _sum_slots(slots):
    nt = len(slots)
    steps = 2

    def body(*refs):
        for r_ref, o_ref in zip(refs[:nt], refs[nt:]):
            acc = r_ref[0].astype(F32)
            for d in range(1, N_DEV):
                acc = acc + r_ref[d].astype(F32)
            o_ref[...] = acc

    return pl.pallas_call(
        body, grid=(steps,), name="grad_sum",
        in_specs=[pl.BlockSpec((N_DEV, s.shape[1] // steps, s.shape[2]), lambda i: (0, i, 0)) for s in slots],
        out_specs=[pl.BlockSpec((s.shape[1] // steps, s.shape[2]), lambda i: (i, 0)) for s in slots],
        out_shape=[jax.ShapeDtypeStruct(s.shape[1:], F32) for s in slots], compiler_params=_cp("parallel"))(*slots)


def _sibling_exchange(halves):
    nt = len(halves)
    pieces = _chunks(nt, GATHER_CHUNK)
    npc = len(pieces)

    def body(*refs):
        srcs, outs = refs[:nt], refs[nt:2 * nt]
        lsem, ssem, rsem = refs[2 * nt:]
        x, y, c = _coords()
        sib = (x, y, 1 - c)
        place = lambda t, cc, off, n: _rows(outs[t], cc * (SHARD_ROWS[t] // 2) + off, n)
        local = [pltpu.make_async_copy(srcs[t].at[pl.ds(off, n), :], place(t, c, off, n), lsem.at[q])
                 for q, (t, off, n) in enumerate(pieces)]
        sends = [_remote(srcs[t].at[pl.ds(off, n), :], place(t, c, off, n), ssem.at[q], rsem.at[q], sib)
                 for q, (t, off, n) in enumerate(pieces)]
        for cp in local + sends:
            cp.start()
        for q, (t, off, n) in enumerate(pieces):
            got = place(t, 1 - c, off, n)
            _remote(got, got, ssem.at[q], rsem.at[q], sib).wait_recv()
        for cp in sends:
            cp.wait_send()
        for cp in local:
            cp.wait()

    return pl.pallas_call(
        body, name="grad_sibling_exchange", in_specs=[HBM_SPEC] * nt, out_specs=[HBM_SPEC] * nt,
        out_shape=[jax.ShapeDtypeStruct((2 * h.shape[0], h.shape[1]), h.dtype) for h in halves],
        scratch_shapes=[pltpu.SemaphoreType.DMA((npc,)), pltpu.SemaphoreType.DMA((npc,)), pltpu.SemaphoreType.DMA((npc,))],
        compiler_params=_cp())(*halves)


def _adam_math(w, g, m, v):
    m = ADAM_B1 * m + (1.0 - ADAM_B1) * g
    v = ADAM_B2 * v + (1.0 - ADAM_B2) * (g * g)
    m_hat = m / (1.0 - ADAM_B1 ** ADAM_STEP)
    v_hat = v / (1.0 - ADAM_B2 ** ADAM_STEP)
    return -ADAM_LR * (m_hat / (jnp.sqrt(v_hat) + ADAM_EPS) + ADAM_WD * w), m, v


def _adamw(w, g, m, v, rb, name):
    rows, cols = w.shape

    def body(w_ref, g_ref, m_ref, v_ref, d_ref, nm_ref, nv_ref):
        d_ref[...], nm_ref[...], nv_ref[...] = _adam_math(w_ref[...], g_ref[...], m_ref[...], v_ref[...])

    spec = pl.BlockSpec((rb, cols), lambda i: (i, 0))
    return pl.pallas_call(body, grid=(rows // rb,), name=name, in_specs=[spec] * 4, out_specs=[spec] * 3,
                          out_shape=[jax.ShapeDtypeStruct(w.shape, F32)] * 3, compiler_params=_cp("parallel"))(w, g, m, v)


PACK = (("b_ada", 3 * D_MODEL), ("g_pre", D_MODEL), ("g_post", D_MODEL), ("gn_g", 512), ("qn_g", 64), ("kn_g", 64),
        ("w_dec_f", 4), ("w_dec_b", 4), ("loss", 1))
PACK_WIDTH = 6144


def _pack(parts):
    used = sum(n for _, n in PACK)
    cols = [parts[name].reshape(1, n).astype(F32) if name in parts else jnp.zeros((1, n), F32) for name, n in PACK]
    return jnp.concatenate(cols + [jnp.zeros((1, PACK_WIDTH - used), F32)], axis=1)


def _unpack(row):
    out, off = {}, 0
    for name, n in PACK:
        out[name] = row[:, off:off + n]
        off += n
    return out


def _small_reduce(vec, wvec, mvec, vvec, cs):
    ncol = 3 * D_MODEL // N_CHIP

    def body(vec_ref, w_ref, m_ref, v_ref, cs_ref, gsum_ref, d_ref, nm_ref, nv_ref, gwa_ref, gat, ssem, rsem):
        me = _dev_index(_coords())
        chip = me // 2
        gat[me] = vec_ref[...]
        sends = []
        for k in range(1, N_DEV):
            cp = _remote(vec_ref, gat.at[me], ssem.at[k - 1], rsem.at[k - 1], _peer(k))
            cp.start()
            sends.append(cp)
        for k in range(1, N_DEV):
            slot = gat.at[_dev_index(_peer(k))]
            _remote(slot, slot, ssem.at[k - 1], rsem.at[k - 1], _peer(k)).wait_recv()
        rows = [gat[d] for d in range(N_DEV)]
        tot = rows[0]
        for d in range(1, N_DEV):
            tot = tot + rows[d]
        gsum_ref[...] = tot
        d_ref[...], nm_ref[...], nv_ref[...] = _adam_math(w_ref[...], tot, m_ref[...], v_ref[...])
        cs = cs_ref[...]
        ca_t = jnp.transpose(jnp.concatenate([cs * _sigmoid(cs), jnp.zeros((128 - N_DEV, D_MODEL), F32)], axis=0))
        dm = jnp.concatenate(rows + [jnp.zeros((128 - N_DEV, PACK_WIDTH), F32)], axis=0)
        for jj in range(N_CHIP):
            @pl.when(chip == jj)
            def _():
                gwa_ref[...] = _nn(ca_t, dm[:, ncol * jj:ncol * (jj + 1)], precision=HIGHEST)
        for cp in sends:
            cp.wait_send()

    row = jax.ShapeDtypeStruct((1, PACK_WIDTH), F32)
    return pl.pallas_call(
        body, name="small_reduce", in_specs=[VMEM_SPEC] * 5, out_specs=[VMEM_SPEC] * 5,
        out_shape=[row, row, row, row, jax.ShapeDtypeStruct((D_MODEL, ncol), F32)],
        scratch_shapes=[pltpu.VMEM((N_DEV, 1, PACK_WIDTH), F32), pltpu.SemaphoreType.DMA((7,)), pltpu.SemaphoreType.DMA((7,))],
        compiler_params=_cp())(vec, wvec, mvec, vvec, cs)


def kernel(x, c, w_ada, b_ada, g_pre, w_in, qn_g, kn_g, w_dec_f, w_dec_b, gn_g, w_pa, w_pr, w_out, g_post, loss_target, m_w_ada, m_b_ada, m_g_pre, m_w_in, m_qn_g, m_kn_g, m_w_dec_f, m_w_dec_b, m_gn_g, m_w_pa, m_w_pr, m_w_out, m_g_post, v_w_ada, v_b_ada, v_g_pre, v_w_in, v_qn_g, v_kn_g, v_w_dec_f, v_w_dec_b, v_gn_g, v_w_pa, v_w_pr, v_w_out, v_g_post):
    shards = (w_in[0].T.astype(MXU_DTYPE), jnp.concatenate([w_pa[0].T, w_pr[0].T], axis=1).astype(MXU_DTYPE),
              w_out[0].astype(MXU_DTYPE))
    win_t, wpt, wout = _weight_gather(shards)
    cs, mod = _mod_exchange(c, w_ada[0], b_ada.reshape(N_CHIP, 3 * D_MODEL // N_CHIP))

    grad_x, g_in_t, g_pt, g_out, small = _local_step(x[0], loss_target[0], mod, g_pre, qn_g, kn_g, w_dec_f, w_dec_b,
                                                     gn_g, g_post, win_t, wpt, wout)

    full = _sibling_exchange(_sum_slots(_grad_scatter((g_in_t, g_pt, g_out))))
    big = {"w_in": (w_in, m_w_in, v_w_in, full[0].T), "w_pa": (w_pa, m_w_pa, v_w_pa, full[1][:, :512].T),
           "w_pr": (w_pr, m_w_pr, v_w_pr, full[1][:, 512:].T), "w_out": (w_out, m_w_out, v_w_out, full[2])}

    small = dict(small)
    small["b_ada"] = small.pop("dmod")
    given = dict(b_ada=(b_ada, m_b_ada, v_b_ada), g_pre=(g_pre, m_g_pre, v_g_pre), g_post=(g_post, m_g_post, v_g_post),
                 gn_g=(gn_g, m_gn_g, v_gn_g), qn_g=(qn_g, m_qn_g, v_qn_g), kn_g=(kn_g, m_kn_g, v_kn_g),
                 w_dec_f=(w_dec_f, m_w_dec_f, v_w_dec_f), w_dec_b=(w_dec_b, m_w_dec_b, v_w_dec_b))
    packs = [_pack({n: t[i] for n, t in given.items()}) for i in range(3)]
    gsum, dvec, nmvec, nvvec, g_w_ada = _small_reduce(_pack(small), *packs, cs.reshape(N_DEV, D_MODEL))
    big["w_ada"] = (w_ada, m_w_ada, v_w_ada, g_w_ada)

    grads, deltas, new_m, new_v = {}, {}, {}, {}
    for name, (w, m, v, g) in big.items():
        d, nm, nv = _adamw(w[0], g, m[0], v[0], 256, "adamw_" + name)
        grads[name], deltas[name], new_m[name], new_v[name] = g[None], d[None], nm[None], nv[None]
    for dst, row in ((grads, gsum), (deltas, dvec), (new_m, nmvec), (new_v, nvvec)):
        dst.update({n: a for n, a in _unpack(row).items() if n != "loss"})
    order = ("w_ada", "b_ada", "g_pre", "w_in", "qn_g", "kn_g", "w_dec_f", "w_dec_b", "gn_g", "w_pa", "w_pr", "w_out", "g_post")
    loss = _unpack(gsum)["loss"][0, 0]
    return (loss, grad_x[None], *[grads[n] for n in order], *[deltas[n] for n in order],
            *[new_m[n] for n in order], *[new_v[n] for n in order])
```

```python
import functools

import jax
import jax.numpy as jnp
from jax import lax
from jax.experimental import pallas as pl
from jax.experimental.pallas import tpu as pltpu

F32 = jnp.float32
BF16 = jnp.bfloat16
MXU_DTYPE = jnp.bfloat16

D_MODEL = 1024
GRID_W = 64
HEAD_DIM = 64
ROPE_THETA = 10000.0
EPS = 1e-6
RET_HEADS = 4
RET_CHUNK = 256
IN_WIDTH = 4864
QA, KA, VA, ZA, QR, KR, VR, ZR, GL = 0, 512, 640, 768, 1280, 1536, 1792, 2304, 2816

ADAM_LR, ADAM_B1, ADAM_B2, ADAM_EPS, ADAM_WD, ADAM_STEP = 0.001, 0.9, 0.999, 1e-08, 0.01, 10

VMEM_LIMIT = 56 * 1024 * 1024
MESH = pl.DeviceIdType.MESH
HIGHEST = lax.Precision.HIGHEST
LOG2E = 1.4426950408889634
LN2 = 0.6931471805599453


def _cp(*sem, **kw):
    if sem:
        kw["dimension_semantics"] = sem
    return pltpu.CompilerParams(vmem_limit_bytes=VMEM_LIMIT, **kw)


def _nn(a, b, **kw):
    return lax.dot_general(a, b, (((1,), (0,)), ((), ())), preferred_element_type=F32, **kw)


def _nt(a, b):
    return lax.dot_general(a, b, (((1,), (1,)), ((), ())), preferred_element_type=F32)


def _tn(a, b):
    return lax.dot_general(a, b, (((0,), (0,)), ((), ())), preferred_element_type=F32)


def _mx(x):
    return x.astype(MXU_DTYPE)


def _lane(shape):
    return lax.broadcasted_iota(jnp.int32, shape, 1)


def _sigmoid(z):
    return 1.0 / (1.0 + jnp.exp(-z))


def _rowsum128(x):
    s = jnp.sum(x, axis=0, keepdims=True)
    out = s[:, 0:128]
    for k in range(1, x.shape[1] // 128):
        out = out + s[:, 128 * k:128 * (k + 1)]
    return out


def _swap16(x):
    return jnp.where((_lane(x.shape) & 16) == 0, pltpu.roll(x, 112, 1), pltpu.roll(x, 16, 1))


def _rope(x, cos, sin):
    return x * cos + _swap16(x) * sin


def _rope_t(d, cos, sin):
    return d * cos + _swap16(d * sin)


def _blockdiag64():
    r = lax.broadcasted_iota(jnp.int32, (128, 128), 0) // 64
    c = lax.broadcasted_iota(jnp.int32, (128, 128), 1) // 64
    return (r == c).astype(BF16)


def _seg64(x, m):
    hi = x.astype(BF16)
    lo = (x - hi.astype(F32)).astype(BF16)
    return _nn(hi, m) + _nn(lo, m)


def _rope_tables(seq):
    t = jnp.arange(seq)
    row = (t // GRID_W).astype(F32)
    col = (t % GRID_W).astype(F32)
    half = HEAD_DIM // 2
    inv_freq = ROPE_THETA ** (-jnp.arange(0, half, 2, dtype=F32) / half)
    ar = row[:, None] * inv_freq[None, :]
    ac = col[:, None] * inv_freq[None, :]
    cr, sr, cc, sc = jnp.cos(ar), jnp.sin(ar), jnp.cos(ac), jnp.sin(ac)
    cos64 = jnp.concatenate([cr, cr, cc, cc], axis=1)
    sin64 = jnp.concatenate([-sr, sr, -sc, sc], axis=1)
    return jnp.tile(cos64, (1, 2)), jnp.tile(sin64, (1, 2))


def _prenorm(x, mod, g_pre):
    seq = x.shape[0]
    bs = 512

    def body(x_ref, mod_ref, g_ref, h_ref):
        xv = x_ref[...]
        r = lax.rsqrt(jnp.mean(xv * xv, axis=-1, keepdims=True) + EPS)
        shift = mod_ref[:, 0:D_MODEL]
        scale = mod_ref[:, D_MODEL:2 * D_MODEL]
        h_ref[...] = ((xv * r) * g_ref[...] * (1.0 + scale) + shift).astype(h_ref.dtype)

    return pl.pallas_call(
        body, grid=(seq // bs,), name="prenorm",
        in_specs=[pl.BlockSpec((bs, D_MODEL), lambda i: (i, 0)), pl.BlockSpec((1, 3 * D_MODEL), lambda i: (0, 0)),
                  pl.BlockSpec((1, D_MODEL), lambda i: (0, 0))],
        out_specs=pl.BlockSpec((bs, D_MODEL), lambda i: (i, 0)),
        out_shape=jax.ShapeDtypeStruct((seq, D_MODEL), MXU_DTYPE), compiler_params=_cp("parallel"))(x, mod, g_pre)


def _prenorm_bwd(x, dh, dout, mod, g_pre):
    seq = x.shape[0]
    bs = 512

    def body(x_ref, dh_ref, dout_ref, mod_ref, g_ref, gx_ref, dshift_ref, dscale_ref, dg_ref):
        @pl.when(pl.program_id(0) == 0)
        def _():
            dshift_ref[...] = jnp.zeros_like(dshift_ref)
            dscale_ref[...] = jnp.zeros_like(dscale_ref)
            dg_ref[...] = jnp.zeros_like(dg_ref)

        xv = x_ref[...]
        dh = dh_ref[...]
        g = g_ref[...]
        r = lax.rsqrt(jnp.mean(xv * xv, axis=-1, keepdims=True) + EPS)
        xn = xv * r
        scale = mod_ref[:, D_MODEL:2 * D_MODEL]
        dshift_ref[...] += jnp.sum(dh, axis=0, keepdims=True)
        dscale_ref[...] += jnp.sum(dh * (xn * g), axis=0, keepdims=True)
        da = dh * (1.0 + scale)
        dg_ref[...] += jnp.sum(da * xn, axis=0, keepdims=True)
        dxn = da * g
        dx = r * (dxn - xn * jnp.mean(dxn * xn, axis=-1, keepdims=True))
        gx_ref[...] = dout_ref[...] + dx

    row = pl.BlockSpec((bs, D_MODEL), lambda i: (i, 0))
    vec = pl.BlockSpec((1, D_MODEL), lambda i: (0, 0))
    return pl.pallas_call(
        body, grid=(seq // bs,), name="prenorm_bwd",
        in_specs=[row, row, row, pl.BlockSpec((1, 3 * D_MODEL), lambda i: (0, 0)), vec],
        out_specs=[row, vec, vec, vec],
        out_shape=[jax.ShapeDtypeStruct((seq, D_MODEL), F32)] + [jax.ShapeDtypeStruct((1, D_MODEL), F32)] * 3,
        compiler_params=_cp("arbitrary"))(x, dh, dout, mod, g_pre)


def _matmul(a, b, *, ta, tb, tm, tn, out_dtype, name):
    kdim = a.shape[0] if ta else a.shape[1]
    m = a.shape[1] if ta else a.shape[0]
    n = b.shape[0] if tb else b.shape[1]
    assert m % tm == 0 and n % tn == 0

    def body(a_ref, b_ref, o_ref):
        dims = (((0 if ta else 1,), (1 if tb else 0,)), ((), ()))
        o_ref[...] = lax.dot_general(a_ref[...], b_ref[...], dims, preferred_element_type=F32).astype(o_ref.dtype)

    a_spec = pl.BlockSpec((kdim, tm), lambda j, i: (0, i)) if ta else pl.BlockSpec((tm, kdim), lambda j, i: (i, 0))
    b_spec = pl.BlockSpec((tn, kdim), lambda j, i: (j, 0)) if tb else pl.BlockSpec((kdim, tn), lambda j, i: (0, j))
    return pl.pallas_call(
        body, grid=(n // tn, m // tm), name=name, in_specs=[a_spec, b_spec],
        out_specs=pl.BlockSpec((tm, tn), lambda j, i: (i, j)),
        out_shape=jax.ShapeDtypeStruct((m, n), out_dtype), compiler_params=_cp("parallel", "parallel"))(a, b)


def _prep(p, cos, sin, qg2, kg2):
    seq = p.shape[0]
    bs = 512

    def body(p_ref, cos_ref, sin_ref, qg_ref, kg_ref, qt_ref, kd_ref, vd_ref, rq_ref, rk_ref, rv_ref):
        m = _blockdiag64()
        cs, sn = cos_ref[...], sin_ref[...]
        lo = _lane((bs, 128)) < 64
        for pr in range(4):
            q = p_ref[:, QA + 128 * pr:QA + 128 * (pr + 1)]
            r = lax.rsqrt(_seg64(q * q, m) * (1.0 / 64) + EPS)
            qt_ref[:, 128 * pr:128 * (pr + 1)] = (_rope(q * r * qg_ref[...], cs, sn) * (0.125 * LOG2E)).astype(qt_ref.dtype)
        k = p_ref[:, KA:KA + 128]
        r = lax.rsqrt(_seg64(k * k, m) * (1.0 / 64) + EPS)
        kr = _rope(k * r * kg_ref[...], cs, sn)
        ksw = pltpu.roll(kr, 64, 1)
        kd_ref[0] = jnp.where(lo, kr, ksw).astype(kd_ref.dtype)
        kd_ref[1] = jnp.where(lo, ksw, kr).astype(kd_ref.dtype)
        v = p_ref[:, VA:VA + 128]
        vsw = pltpu.roll(v, 64, 1)
        vd_ref[0] = jnp.where(lo, v, vsw).astype(vd_ref.dtype)
        vd_ref[1] = jnp.where(lo, vsw, v).astype(vd_ref.dtype)
        for pr in range(2):
            sl = slice(128 * pr, 128 * (pr + 1))
            rq_ref[:, sl] = _rope(p_ref[:, QR + 128 * pr:QR + 128 * (pr + 1)], cs, sn).astype(rq_ref.dtype)
            rk_ref[:, sl] = (_rope(p_ref[:, KR + 128 * pr:KR + 128 * (pr + 1)], cs, sn) * 0.125).astype(rk_ref.dtype)
        rv_ref[...] = p_ref[:, VR:VR + 512].astype(rv_ref.dtype)

    tab = pl.BlockSpec((bs, 128), lambda i: (i, 0))
    gsp = pl.BlockSpec((1, 128), lambda i: (0, 0))
    dup = pl.BlockSpec((2, bs, 128), lambda i: (0, i, 0))
    return pl.pallas_call(
        body, grid=(seq // bs,), name="mixer_prep",
        in_specs=[pl.BlockSpec((bs, ZR), lambda i: (i, 0)), tab, tab, gsp, gsp],
        out_specs=[pl.BlockSpec((bs, 512), lambda i: (i, 0)), dup, dup, pl.BlockSpec((bs, 256), lambda i: (i, 0)),
                   pl.BlockSpec((bs, 256), lambda i: (i, 0)), pl.BlockSpec((bs, 512), lambda i: (i, 0))],
        out_shape=[jax.ShapeDtypeStruct((seq, 512), MXU_DTYPE), jax.ShapeDtypeStruct((2, seq, 128), MXU_DTYPE),
                   jax.ShapeDtypeStruct((2, seq, 128), MXU_DTYPE), jax.ShapeDtypeStruct((seq, 256), MXU_DTYPE),
                   jax.ShapeDtypeStruct((seq, 256), MXU_DTYPE), jax.ShapeDtypeStruct((seq, 512), MXU_DTYPE)],
        compiler_params=_cp("parallel"))(p, cos, sin, qg2, kg2)


def _halves(kd):
    lo = _lane(kd.shape) < 64
    z = jnp.zeros_like(kd)
    return jnp.concatenate([jnp.where(lo, kd, z), jnp.where(lo, z, kd)], axis=0)


def _attn_fwd(qt, kd, vd):
    seq = qt.shape[0]
    bq = bk = 512
    nk = seq // bk

    def body(q_ref, k_ref, v_ref, o_ref, lse_ref, m_scr, l_scr, acc):
        j, n = pl.program_id(1), pl.program_id(2)

        @pl.when(n == 0)
        def _():
            m_scr[...] = jnp.full_like(m_scr, -jnp.inf)
            l_scr[...] = jnp.zeros_like(l_scr)
            acc[...] = jnp.zeros_like(acc)

        k2, v2 = _halves(k_ref[...]), _halves(v_ref[...])
        lo = _lane((bq, 128)) < 64
        s2s = [_nt(q_ref[:, 128 * pr:128 * (pr + 1)], k2) for pr in range(2)]
        for pr in range(2):
            s2 = s2s[pr]
            ps, alphas = [], []
            for e in range(2):
                g = 2 * pr + e
                s = s2[:, e * bk:(e + 1) * bk]
                m_prev = m_scr[g]
                m_next = jnp.maximum(m_prev, jnp.max(s, axis=1, keepdims=True))
                alpha = jnp.exp2(m_prev - m_next)
                pe = jnp.exp2(s - jnp.tile(m_next, (1, bk // 128)))
                l_scr[g] = alpha * l_scr[g] + jnp.sum(pe, axis=1, keepdims=True)
                m_scr[g] = m_next
                ps.append(_mx(pe))
                alphas.append(alpha)
            o2 = _nn(jnp.concatenate(ps, axis=1), v2)
            sl = slice(128 * pr, 128 * (pr + 1))
            acc[:, sl] = acc[:, sl] * jnp.where(lo, alphas[0], alphas[1]) + o2

        @pl.when(n == nk - 1)
        def _():
            for pr in range(2):
                sl = slice(128 * pr, 128 * (pr + 1))
                o_ref[:, sl] = acc[:, sl] / jnp.where(lo, l_scr[2 * pr], l_scr[2 * pr + 1])
            for g in range(4):
                lse = jnp.transpose(m_scr[g] + jnp.log2(l_scr[g]))
                lse_ref[pl.ds(4 * j + g, 1), :] = lse[0:1, :]

    return pl.pallas_call(
        body, grid=(seq // bq, 2, nk), name="attn_fwd",
        in_specs=[pl.BlockSpec((bq, 256), lambda i, j, n: (i, j)), pl.BlockSpec((None, bk, 128), lambda i, j, n: (j, n, 0)),
                  pl.BlockSpec((None, bk, 128), lambda i, j, n: (j, n, 0))],
        out_specs=[pl.BlockSpec((bq, 256), lambda i, j, n: (i, j)), pl.BlockSpec((8, bq), lambda i, j, n: (0, i))],
        out_shape=[jax.ShapeDtypeStruct((seq, 512), F32), jax.ShapeDtypeStruct((8, seq), F32)],
        scratch_shapes=[pltpu.VMEM((4, bq, 128), F32), pltpu.VMEM((4, bq, 128), F32), pltpu.VMEM((bq, 256), F32)],
        compiler_params=_cp("parallel", "arbitrary", "arbitrary"))(qt, kd, vd)


def _attn_bwd(qt, kd, vd, do, lse, delta):
    seq = qt.shape[0]
    bq = bk = 512
    nq, nk = seq // bq, seq // bk

    def body(q_ref, do_ref, k_ref, v_ref, lse_ref, del_ref, dq_ref, dk_ref, dv_ref):
        j, i, n = pl.program_id(0), pl.program_id(1), pl.program_id(2)

        @pl.when(n == 0)
        def _():
            dq_ref[...] = jnp.zeros_like(dq_ref)

        @pl.when((i == 0) & (n == 0))
        def _():
            dk_ref[...] = jnp.zeros_like(dk_ref)
            dv_ref[...] = jnp.zeros_like(dv_ref)

        k2, v2 = _halves(k_ref[...]), _halves(v_ref[...])
        lo = _lane((bk, 128)) < 64
        rows = pl.ds(pl.multiple_of(n * bk, bk), bk)
        sls = [slice(128 * pr, 128 * (pr + 1)) for pr in range(2)]
        s_ts = [_nt(k2, q_ref[:, sl]) for sl in sls]
        dp_ts = [_nt(v2, do_ref[:, sl]) for sl in sls]
        big = lambda r: jnp.concatenate([jnp.broadcast_to(r[0], (bk, bq)), jnp.broadcast_to(r[1], (bk, bq))], axis=0)
        for pr in range(2):
            sl = sls[pr]
            qp, dop = q_ref[:, sl], do_ref[:, sl]
            ls = [lse_ref[pl.ds(4 * j + 2 * pr + e, 1), :] for e in range(2)]
            dl = [del_ref[pl.ds(4 * j + 2 * pr + e, 1), :] for e in range(2)]
            p_t = jnp.exp2(s_ts[pr] - big(ls))
            ds_t = _mx(p_t * (dp_ts[pr] - big(dl)))
            rv = _nn(_mx(p_t), dop)
            rk = _nn(ds_t, qp) * LN2
            dv_ref[rows, :] += jnp.where(lo, rv[:bk], rv[bk:])
            dk_ref[rows, :] += jnp.where(lo, rk[:bk], rk[bk:])
            dq_ref[:, sl] += _tn(ds_t, k2)

    return pl.pallas_call(
        body, grid=(2, nq, nk), name="attn_bwd",
        in_specs=[pl.BlockSpec((bq, 256), lambda j, i, n: (i, j)), pl.BlockSpec((bq, 256), lambda j, i, n: (i, j)),
                  pl.BlockSpec((None, bk, 128), lambda j, i, n: (j, n, 0)), pl.BlockSpec((None, bk, 128), lambda j, i, n: (j, n, 0)),
                  pl.BlockSpec((8, bq), lambda j, i, n: (0, i)), pl.BlockSpec((8, bq), lambda j, i, n: (0, i))],
        out_specs=[pl.BlockSpec((bq, 256), lambda j, i, n: (i, j)), pl.BlockSpec((None, seq, 128), lambda j, i, n: (j, 0, 0)),
                   pl.BlockSpec((None, seq, 128), lambda j, i, n: (j, 0, 0))],
        out_shape=[jax.ShapeDtypeStruct((seq, 512), F32), jax.ShapeDtypeStruct((2, seq, 128), F32),
                   jax.ShapeDtypeStruct((2, seq, 128), F32)],
        compiler_params=_cp("arbitrary", "arbitrary", "arbitrary"))(qt, do, kd, vd, lse, delta)


def _ret_tables(lg_f, lg_b, c):
    idx = jnp.arange(c, dtype=F32)
    diff = idx[:, None] - idx[None, :]
    a, b = lg_f[:, None, None], lg_b[:, None, None]
    low, up = (diff >= 0)[None], (diff < 0)[None]
    dmat = jnp.where(low, jnp.exp(a * jnp.maximum(diff, 0.0)[None]), jnp.exp(b * jnp.maximum(-diff, 0.0)[None]))
    dda = jnp.where(low, diff[None] * jnp.exp(a * jnp.maximum(diff, 0.0)[None]), 0.0)
    ddb = jnp.where(up, -diff[None] * jnp.exp(b * jnp.maximum(-diff, 0.0)[None]), 0.0)
    rep = lambda w: jnp.broadcast_to(w[:, :, None], (RET_HEADS, c, 128))
    wqf = rep(jnp.exp(lg_f[:, None] * (idx + 1.0)[None]))
    wqb = rep(jnp.exp(lg_b[:, None] * (c - idx)[None]))
    wkf = rep(jnp.exp(lg_f[:, None] * (c - 1.0 - idx)[None]))
    wkb = rep(jnp.exp(lg_b[:, None] * idx[None]))
    cdf, cdb = jnp.exp(lg_f * c), jnp.exp(lg_b * c)
    dec_fb = jnp.broadcast_to(jnp.concatenate([cdf, cdb])[:, None], (8, 128))
    dec_bf = jnp.broadcast_to(jnp.concatenate([cdb, cdf])[:, None], (8, 128))
    return dict(dmat=dmat, dda=dda, ddb=ddb, wqf=wqf, wqb=wqb, wkf=wkf, wkb=wkb, dec_fb=dec_fb, dec_bf=dec_bf)


def _ret_states(xk, yv, w_fwd, w_rev, dec, name):
    seq = xk.shape[0]
    c = RET_CHUNK
    nc = seq // c

    def body(xf_ref, yf_ref, xr_ref, yr_ref, wf_ref, wr_ref, dec_ref, sf_ref, sr_ref, st_f, st_r):
        @pl.when(pl.program_id(0) == 0)
        def _():
            st_f[...] = jnp.zeros_like(st_f)
            st_r[...] = jnp.zeros_like(st_r)

        lane = _lane((c, 128))
        for h in range(RET_HEADS):
            pr, e = h // 2, h % 2
            half = (lane < 64) if e == 0 else (lane >= 64)
            for x_ref, y_ref, w_ref, s_ref, st, drow in ((xf_ref, yf_ref, wf_ref, sf_ref, st_f, h),
                                                        (xr_ref, yr_ref, wr_ref, sr_ref, st_r, 4 + h)):
                s_ref[h] = st[h].astype(s_ref.dtype)
                xm = jnp.where(half, x_ref[:, 128 * pr:128 * (pr + 1)].astype(F32) * w_ref[h], 0.0)
                st[h] = st[h] * dec_ref[drow:drow + 1, :] + _tn(_mx(xm), _mx(y_ref[:, 128 * h:128 * (h + 1)]))

    xs = lambda f: pl.BlockSpec((c, 256), f)
    ys = lambda f: pl.BlockSpec((c, 512), f)
    fwd, rev = (lambda n: (n, 0)), (lambda n: (nc - 1 - n, 0))
    wsp = pl.BlockSpec((RET_HEADS, c, 128), lambda n: (0, 0, 0))
    return pl.pallas_call(
        body, grid=(nc,), name=name,
        in_specs=[xs(fwd), ys(fwd), xs(rev), ys(rev), wsp, wsp, pl.BlockSpec((8, 128), lambda n: (0, 0))],
        out_specs=[pl.BlockSpec((None, RET_HEADS, 128, 128), lambda n: (n, 0, 0, 0)),
                   pl.BlockSpec((None, RET_HEADS, 128, 128), lambda n: (nc - 1 - n, 0, 0, 0))],
        out_shape=[jax.ShapeDtypeStruct((nc, RET_HEADS, 128, 128), MXU_DTYPE)] * 2,
        scratch_shapes=[pltpu.VMEM((RET_HEADS, 128, 128), F32), pltpu.VMEM((RET_HEADS, 128, 128), F32)],
        compiler_params=_cp("arbitrary"))(xk, yv, xk, yv, w_fwd, w_rev, dec)


def _ret_fwd(rq, rk, rv, rf, rb, tb):
    seq = rq.shape[0]
    c = RET_CHUNK

    def body(q_ref, k_ref, v_ref, rf_ref, rb_ref, d_ref, wqf_ref, wqb_ref, o_ref):
        lane = _lane((c, 128))
        for h in range(RET_HEADS):
            pr, e = h // 2, h % 2
            half = (lane < 64) if e == 0 else (lane >= 64)
            sl = slice(128 * pr, 128 * (pr + 1))
            qp = q_ref[:, sl]
            km = jnp.where(half, k_ref[:, sl], jnp.zeros_like(k_ref[:, sl]))
            sd = _mx(_nt(qp, km) * d_ref[h])
            qf = qp.astype(F32)
            o = _nn(sd, v_ref[:, 128 * h:128 * (h + 1)])
            o = o + _nn(_mx(qf * wqf_ref[h]), rf_ref[h]) + _nn(_mx(qf * wqb_ref[h]), rb_ref[h])
            o_ref[:, 128 * h:128 * (h + 1)] = o

    st = pl.BlockSpec((None, RET_HEADS, 128, 128), lambda n: (n, 0, 0, 0))
    wsp = pl.BlockSpec((RET_HEADS, c, 128), lambda n: (0, 0, 0))
    return pl.pallas_call(
        body, grid=(seq // c,), name="ret_fwd",
        in_specs=[pl.BlockSpec((c, 256), lambda n: (n, 0)), pl.BlockSpec((c, 256), lambda n: (n, 0)),
                  pl.BlockSpec((c, 512), lambda n: (n, 0)), st, st, pl.BlockSpec((RET_HEADS, c, c), lambda n: (0, 0, 0)), wsp, wsp],
        out_specs=pl.BlockSpec((c, 512), lambda n: (n, 0)),
        out_shape=jax.ShapeDtypeStruct((seq, 512), F32), compiler_params=_cp("parallel"))(
            rq, rk, rv, rf, rb, tb["dmat"], tb["wqf"], tb["wqb"])


def _ret_bwd(rq, rk, rv, do, rf, rb, hf, hb, tb):
    seq = rq.shape[0]
    c = RET_CHUNK

    def body(q_ref, k_ref, v_ref, do_ref, rf_ref, rb_ref, hf_ref, hb_ref, d_ref, dda_ref, ddb_ref,
             wqf_ref, wqb_ref, wkf_ref, wkb_ref, cdec_ref, dq_ref, dk_ref, dv_ref, dlg_ref):
        @pl.when(pl.program_id(0) == 0)
        def _():
            dlg_ref[...] = jnp.zeros_like(dlg_ref)

        lane = _lane((c, 128))
        pos = lax.broadcasted_iota(jnp.int32, (c, 128), 0).astype(F32)
        for pr in range(2):
            sl = slice(128 * pr, 128 * (pr + 1))
            qp, kp = q_ref[:, sl], k_ref[:, sl]
            qf, kf = qp.astype(F32), kp.astype(F32)
            dq_acc = jnp.zeros((c, 128), F32)
            dk_acc = jnp.zeros((c, 128), F32)
            for e in range(2):
                h = 2 * pr + e
                half = (lane < 64) if e == 0 else (lane >= 64)
                hs = slice(128 * h, 128 * (h + 1))
                km = jnp.where(half, kp, jnp.zeros_like(kp))
                kmf = km.astype(F32)
                vh, doh = v_ref[:, hs], do_ref[:, hs]
                rfh, rbh, hfh, hbh = rf_ref[h], rb_ref[h], hf_ref[h], hb_ref[h]
                s = _nt(qp, km)
                dpm = _nt(doh, vh)
                ds_b = _mx(dpm * d_ref[h])
                sd_b = _mx(s * d_ref[h])
                dq_if = wqf_ref[h] * _nt(doh, rfh)
                dq_ib = wqb_ref[h] * _nt(doh, rbh)
                dq_acc = dq_acc + _nn(ds_b, km) + dq_if + dq_ib
                dk_sf = wkf_ref[h] * _nt(vh, hfh)
                dk_sb = wkb_ref[h] * _nt(vh, hbh)
                dk_acc = dk_acc + jnp.where(half, _tn(ds_b, qp), 0.0) + dk_sf + dk_sb
                dv = _tn(sd_b, doh) + _nn(_mx(kmf * wkf_ref[h]), hfh) + _nn(_mx(kmf * wkb_ref[h]), hbh)
                dv_ref[:, hs] = dv.astype(dv_ref.dtype)
                sdp = s * dpm
                hr_f = hfh.astype(F32) * rfh.astype(F32)
                hr_b = hbh.astype(F32) * rbh.astype(F32)
                da = (_rowsum128(sdp * dda_ref[h]) + _rowsum128((pos + 1.0) * qf * dq_if)
                      + _rowsum128((c - 1.0 - pos) * kf * dk_sf) + cdec_ref[h:h + 1, :] * _rowsum128(hr_f))
                db = (_rowsum128(sdp * ddb_ref[h]) + _rowsum128((c - pos) * qf * dq_ib)
                      + _rowsum128(pos * kf * dk_sb) + cdec_ref[4 + h:5 + h, :] * _rowsum128(hr_b))
                dlg_ref[h:h + 1, :] += da
                dlg_ref[4 + h:5 + h, :] += db
            dq_ref[:, sl] = dq_acc
            dk_ref[:, sl] = dk_acc

    st = pl.BlockSpec((None, RET_HEADS, 128, 128), lambda n: (n, 0, 0, 0))
    wsp = pl.BlockSpec((RET_HEADS, c, 128), lambda n: (0, 0, 0))
    dsp = pl.BlockSpec((RET_HEADS, c, c), lambda n: (0, 0, 0))
    x256 = pl.BlockSpec((c, 256), lambda n: (n, 0))
    x512 = pl.BlockSpec((c, 512), lambda n: (n, 0))
    cdec = tb["dec_fb"] * float(c)
    return pl.pallas_call(
        body, grid=(seq // c,), name="ret_bwd",
        in_specs=[x256, x256, x512, x512, st, st, st, st, dsp, dsp, dsp, wsp, wsp, wsp, wsp,
                  pl.BlockSpec((8, 128), lambda n: (0, 0))],
        out_specs=[x256, x256, x512, pl.BlockSpec((8, 128), lambda n: (0, 0))],
        out_shape=[jax.ShapeDtypeStruct((seq, 256), F32), jax.ShapeDtypeStruct((seq, 256), F32),
                   jax.ShapeDtypeStruct((seq, 512), MXU_DTYPE), jax.ShapeDtypeStruct((8, 128), F32)],
        compiler_params=_cp("arbitrary"))(
            rq, rk, rv, do, rf, rb, hf, hb, tb["dmat"], tb["dda"], tb["ddb"], tb["wqf"], tb["wqb"], tb["wkf"], tb["wkb"], cdec)


def _tail(x, tgt, o_att, o_ret, p, mod, g_post, gn_g, wpt, wout):
    seq = x.shape[0]
    bs = 256
    nb = seq // bs

    def body(x_ref, t_ref, oa_ref, or_ref, za_ref, zr_ref, gl_ref, mod_ref, gp_ref, gn_ref, wpt_ref, wout_ref,
             dout_ref, dza_ref, dzr_ref, dgl_ref, doa_ref, dor_ref, del_ref, gout_ref, gpt_ref,
             dgate_ref, dgpost_ref, dgn_ref, loss_ref, gout_acc, gpt_acc):
        i = pl.program_id(0)

        @pl.when(i == 0)
        def _():
            gout_acc[...] = jnp.zeros_like(gout_acc)
            gpt_acc[...] = jnp.zeros_like(gpt_acc)
            dgate_ref[...] = jnp.zeros_like(dgate_ref)
            dgpost_ref[...] = jnp.zeros_like(dgpost_ref)
            dgn_ref[...] = jnp.zeros_like(dgn_ref)
            loss_ref[...] = jnp.zeros_like(loss_ref)

        oa, za, zr = oa_ref[...], za_ref[...], zr_ref[...]
        sga, sgr = _sigmoid(za), _sigmoid(zr)
        sza, szr = za * sga, zr * sgr
        ya_b = _mx(oa * sza)
        ons, rstds = [], []
        for h in range(RET_HEADS):
            oh = or_ref[:, 128 * h:128 * (h + 1)]
            xc = oh - jnp.mean(oh, axis=-1, keepdims=True)
            rstd = lax.rsqrt(jnp.mean(xc * xc, axis=-1, keepdims=True) + EPS)
            ons.append(xc * rstd)
            rstds.append(rstd)
        on = jnp.concatenate(ons, axis=1)
        yn = on * gn_ref[...]
        yr_b = _mx(yn * szr)
        wpa_t, wpr_t = wpt_ref[:, 0:512], wpt_ref[:, 512:1024]
        a_att = _nt(ya_b, wpa_t)
        a_ret = _nt(yr_b, wpr_t)
        gts = _sigmoid(gl_ref[...])
        g_att, g_ret = gts[:, 0:D_MODEL], gts[:, D_MODEL:2 * D_MODEL]
        merged_b = _mx(g_att * a_att + g_ret * a_ret)
        u = _nn(merged_b, wout_ref[...])
        r = lax.rsqrt(jnp.mean(u * u, axis=-1, keepdims=True) + EPS)
        un = u * r
        gpost = gp_ref[...]
        y = un * gpost
        gate = mod_ref[:, 2 * D_MODEL:3 * D_MODEL]
        err = x_ref[...] + gate * y - t_ref[...]
        loss_ref[...] += (0.5 / D_MODEL) * _rowsum128(err * err)
        dout = err * (1.0 / D_MODEL)
        dout_ref[...] = dout
        dgate_ref[...] += jnp.sum(dout * y, axis=0, keepdims=True)
        dy = dout * gate
        dgpost_ref[...] += jnp.sum(dy * un, axis=0, keepdims=True)
        dun = dy * gpost
        du_b = _mx(r * (dun - un * jnp.mean(dun * un, axis=-1, keepdims=True)))
        dmerged = _nt(du_b, wout_ref[...])
        gout_acc[...] += _tn(merged_b, du_b)
        d_att, d_ret = dmerged * g_att, dmerged * g_ret
        dgl_ref[:, 0:D_MODEL] = (d_att * a_att * (1.0 - g_att)).astype(dgl_ref.dtype)
        dgl_ref[:, D_MODEL:2 * D_MODEL] = (d_ret * a_ret * (1.0 - g_ret)).astype(dgl_ref.dtype)
        d_att_b, d_ret_b = _mx(d_att), _mx(d_ret)
        dya = _nn(d_att_b, wpa_t)
        dyr = _nn(d_ret_b, wpr_t)
        gpt_acc[:, 0:512] += _tn(d_att_b, ya_b)
        gpt_acc[:, 512:1024] += _tn(d_ret_b, yr_b)
        dza_ref[...] = (dya * oa * (sga * (1.0 + za * (1.0 - sga)))).astype(dza_ref.dtype)
        doa = dya * sza
        doa_ref[...] = doa.astype(doa_ref.dtype)
        dt = jnp.transpose(doa * oa)
        del_ref[...] = jnp.sum(dt.reshape(8, HEAD_DIM, bs), axis=1)
        dzr_ref[...] = (dyr * yn * (sgr * (1.0 + zr * (1.0 - sgr)))).astype(dzr_ref.dtype)
        dyn = dyr * szr
        dgn_ref[...] += jnp.sum(dyn * on, axis=0, keepdims=True)
        don = dyn * gn_ref[...]
        for h in range(RET_HEADS):
            hs = slice(128 * h, 128 * (h + 1))
            dh, oh = don[:, hs], ons[h]
            doh = rstds[h] * (dh - jnp.mean(dh, axis=-1, keepdims=True) - oh * jnp.mean(dh * oh, axis=-1, keepdims=True))
            dor_ref[:, hs] = doh.astype(dor_ref.dtype)

        @pl.when(i == nb - 1)
        def _():
            gout_ref[...] = gout_acc[...].astype(gout_ref.dtype)
            gpt_ref[...] = gpt_acc[...].astype(gpt_ref.dtype)

    row = lambda w: pl.BlockSpec((bs, w), lambda i: (i, 0))
    el = lambda w, off: pl.BlockSpec((pl.Element(bs), pl.Element(w)), lambda i: (i * bs, off))
    vec = lambda w: pl.BlockSpec((1, w), lambda i: (0, 0))
    full = pl.BlockSpec((D_MODEL, D_MODEL), lambda i: (0, 0))
    sds = jax.ShapeDtypeStruct
    return pl.pallas_call(
        body, grid=(nb,), name="tail",
        in_specs=[row(D_MODEL), row(D_MODEL), row(512), row(512), el(512, ZA), el(512, ZR), el(2 * D_MODEL, GL),
                  vec(3 * D_MODEL), vec(D_MODEL), vec(512), full, full],
        out_specs=[row(D_MODEL), row(512), row(512), row(2 * D_MODEL), row(512), row(512),
                   pl.BlockSpec((8, bs), lambda i: (0, i)), full, full, vec(D_MODEL), vec(D_MODEL), vec(512), vec(128)],
        out_shape=[sds((seq, D_MODEL), F32), sds((seq, 512), MXU_DTYPE), sds((seq, 512), MXU_DTYPE),
                   sds((seq, 2 * D_MODEL), MXU_DTYPE), sds((seq, 512), MXU_DTYPE), sds((seq, 512), MXU_DTYPE),
                   sds((8, seq), F32), sds((D_MODEL, D_MODEL), MXU_DTYPE), sds((D_MODEL, D_MODEL), MXU_DTYPE),
                   sds((1, D_MODEL), F32), sds((1, D_MODEL), F32), sds((1, 512), F32), sds((1, 128), F32)],
        scratch_shapes=[pltpu.VMEM((D_MODEL, D_MODEL), F32), pltpu.VMEM((D_MODEL, D_MODEL), F32)],
        compiler_params=_cp("arbitrary"))(x, tgt, o_att, o_ret, p, p, p, mod, g_post, gn_g, wpt, wout)


def _assemble(p, cos, sin, qg2, kg2, dqt, dkp, dvp, dza, drq, drk, drv, dzr, dgl):
    seq = p.shape[0]
    bs = 512

    def body(p_ref, cos_ref, sin_ref, qg_ref, kg_ref, dqt_ref, dkp_ref, dvp_ref, dza_ref, drq_ref, drk_ref, drv_ref,
             dzr_ref, dgl_ref, dp_ref, dqg_ref, dkg_ref):
        @pl.when(pl.program_id(0) == 0)
        def _():
            dqg_ref[...] = jnp.zeros_like(dqg_ref)
            dkg_ref[...] = jnp.zeros_like(dkg_ref)

        m = _blockdiag64()
        cs, sn = cos_ref[...], sin_ref[...]
        lo = _lane((bs, 128)) < 64

        def norm_bwd(raw, dn, g, dg_ref):
            r = lax.rsqrt(_seg64(raw * raw, m) * (1.0 / 64) + EPS)
            xn = raw * r
            dg_ref[...] += jnp.sum(dn * xn, axis=0, keepdims=True)
            dxn = dn * g
            return r * (dxn - xn * (_seg64(dxn * xn, m) * (1.0 / 64)))

        for pr in range(4):
            sl = slice(128 * pr, 128 * (pr + 1))
            dn = _rope_t(dqt_ref[:, sl] * 0.125, cs, sn)
            dp_ref[:, QA + 128 * pr:QA + 128 * (pr + 1)] = norm_bwd(p_ref[:, sl], dn, qg_ref[...], dqg_ref).astype(dp_ref.dtype)
        fold = lambda a: a + pltpu.roll(a, 64, 1)
        dk = jnp.where(lo, fold(dkp_ref[0]), fold(dkp_ref[1]))
        dp_ref[:, KA:KA + 128] = norm_bwd(p_ref[:, KA:KA + 128], _rope_t(dk, cs, sn), kg_ref[...], dkg_ref).astype(dp_ref.dtype)
        dp_ref[:, VA:VA + 128] = jnp.where(lo, fold(dvp_ref[0]), fold(dvp_ref[1])).astype(dp_ref.dtype)
        dp_ref[:, ZA:ZA + 512] = dza_ref[...]
        for pr in range(2):
            sl = slice(128 * pr, 128 * (pr + 1))
            dp_ref[:, QR + 128 * pr:QR + 128 * (pr + 1)] = _rope_t(drq_ref[:, sl], cs, sn).astype(dp_ref.dtype)
            dp_ref[:, KR + 128 * pr:KR + 128 * (pr + 1)] = _rope_t(drk_ref[:, sl] * 0.125, cs, sn).astype(dp_ref.dtype)
        dp_ref[:, VR:VR + 512] = drv_ref[...]
        dp_ref[:, ZR:ZR + 512] = dzr_ref[...]
        dp_ref[:, GL:GL + 2 * D_MODEL] = dgl_ref[...]

    row = lambda w: pl.BlockSpec((bs, w), lambda i: (i, 0))
    gsp = pl.BlockSpec((1, 128), lambda i: (0, 0))
    dup = pl.BlockSpec((2, bs, 128), lambda i: (0, i, 0))
    return pl.pallas_call(
        body, grid=(seq // bs,), name="assemble_dp",
        in_specs=[row(768), row(128), row(128), gsp, gsp, row(512), dup, dup, row(512), row(256), row(256), row(512),
                  row(512), row(2 * D_MODEL)],
        out_specs=[row(IN_WIDTH), gsp, gsp],
        out_shape=[jax.ShapeDtypeStruct((seq, IN_WIDTH), MXU_DTYPE), jax.ShapeDtypeStruct((1, 128), F32),
                   jax.ShapeDtypeStruct((1, 128), F32)],
        compiler_params=_cp("arbitrary"))(p, cos, sin, qg2, kg2, dqt, dkp, dvp, dza, drq, drk, drv, dzr, dgl)


def _local_step(x, tgt, mod, g_pre, qn_g, kn_g, w_dec_f, w_dec_b, gn_g, g_post, win_t, wpt, wout):
    seq = x.shape[0]
    cos, sin = _rope_tables(seq)
    qg2, kg2 = jnp.tile(qn_g, (1, 2)), jnp.tile(kn_g, (1, 2))
    lg_f = jax.nn.log_sigmoid(w_dec_f[0])
    lg_b = jax.nn.log_sigmoid(w_dec_b[0])
    tb = _ret_tables(lg_f, lg_b, RET_CHUNK)

    h = _prenorm(x, mod, g_pre)
    p = _matmul(h, win_t, ta=False, tb=True, tm=512, tn=IN_WIDTH // 2, out_dtype=F32, name="in_proj")
    qt, kd, vd, rq, rk, rv = _prep(p, cos, sin, qg2, kg2)
    o_att, lse = _attn_fwd(qt, kd, vd)
    rf, rb = _ret_states(rk, rv, tb["wkf"], tb["wkb"], tb["dec_fb"], "ret_states_fwd")
    o_ret = _ret_fwd(rq, rk, rv, rf, rb, tb)
    (dout, dza, dzr, dgl, do_att, do_ret, delta, g_out, g_pt, dgate, dgpost, dgn, loss_l) = _tail(
        x, tgt, o_att, o_ret, p, mod, g_post, gn_g, wpt, wout)
    dqt, dkp, dvp = _attn_bwd(qt, kd, vd, do_att, lse, delta)
    hb, hf = _ret_states(rq, do_ret, tb["wqb"], tb["wqf"], tb["dec_bf"], "ret_states_bwd")
    drq, drk, drv, dlg = _ret_bwd(rq, rk, rv, do_ret, rf, rb, hf, hb, tb)
    dp, dqg, dkg = _assemble(p, cos, sin, qg2, kg2, dqt, dkp, dvp, dza, drq, drk, drv, dzr, dgl)
    dh = _matmul(dp, win_t, ta=False, tb=False, tm=512, tn=D_MODEL, out_dtype=F32, name="in_proj_dx")
    g_in_t = _matmul(dp, h, ta=True, tb=False, tm=256, tn=D_MODEL, out_dtype=MXU_DTYPE, name="in_proj_dw")
    grad_x, dshift, dscale, dgpre = _prenorm_bwd(x, dh, dout, mod, g_pre)

    dlg = jnp.sum(dlg, axis=1)
    small = dict(
        dmod=jnp.concatenate([dshift, dscale, dgate], axis=1),
        g_pre=dgpre, g_post=dgpost, gn_g=dgn,
        qn_g=dqg[:, :64] + dqg[:, 64:], kn_g=dkg[:, :64] + dkg[:, 64:],
        w_dec_f=(dlg[0:4] * jax.nn.sigmoid(-w_dec_f[0]))[None], w_dec_b=(dlg[4:8] * jax.nn.sigmoid(-w_dec_b[0]))[None],
        loss=jnp.sum(loss_l, axis=1, keepdims=True))
    return grad_x, g_in_t, g_pt, g_out, small


N_DEV = 8
N_CHIP = 4
HBM_SPEC = pl.BlockSpec(memory_space=pl.ANY)
VMEM_SPEC = pl.BlockSpec(memory_space=pltpu.VMEM)
SHARD_ROWS = (IN_WIDTH // N_CHIP, D_MODEL // N_CHIP, D_MODEL // N_CHIP)


def _coords():
    return lax.axis_index("x"), lax.axis_index("y"), lax.axis_index("c")


def _peer(k):
    x, y, c = _coords()
    return (1 - x if k & 4 else x, 1 - y if k & 2 else y, 1 - c if k & 1 else c)


def _dev_index(p):
    return 4 * p[0] + 2 * p[1] + p[2]


def _rows(ref, start, size):
    return ref.at[pl.ds(pl.multiple_of(start, 16), size), :]


def _remote(src, dst, ssem, rsem, dev):
    return pltpu.make_async_remote_copy(src_ref=src, dst_ref=dst, send_sem=ssem, recv_sem=rsem, device_id=dev,
                                        device_id_type=MESH)


def _mod_exchange(c, w_ada_s, b4):
    ncol = w_ada_s.shape[1]

    def body(c_ref, w_ref, b_ref, cs_ref, mod_ref, modsh, modall, ssem, rsem):
        me = _dev_index(_coords())
        chip = me // 2
        cs_ref[me] = c_ref[...]
        sends = []
        for k in range(1, N_DEV):
            cp = _remote(c_ref, cs_ref.at[me], ssem.at[k - 1], rsem.at[k - 1], _peer(k))
            cp.start()
            sends.append(cp)
        for k in range(1, N_DEV):
            slot = cs_ref.at[_dev_index(_peer(k))]
            _remote(slot, slot, ssem.at[k - 1], rsem.at[k - 1], _peer(k)).wait_recv()
        cs = jnp.concatenate([cs_ref[d] for d in range(N_DEV)], axis=0)
        ms = _nn(cs * _sigmoid(cs), w_ref[...], precision=HIGHEST) + b_ref[pl.ds(chip, 1), :]
        modsh[...] = ms
        modall[chip] = ms
        for k2 in range(1, N_CHIP):
            cp = _remote(modsh, modall.at[chip], ssem.at[6 + k2], rsem.at[6 + k2], _peer(2 * k2))
            cp.start()
            sends.append(cp)
        for k2 in range(1, N_CHIP):
            slot = modall.at[_dev_index(_peer(2 * k2)) // 2]
            _remote(slot, slot, ssem.at[6 + k2], rsem.at[6 + k2], _peer(2 * k2)).wait_recv()
        for jj in range(N_CHIP):
            mod_ref[:, ncol * jj:ncol * (jj + 1)] = modall[jj, pl.ds(me, 1), :]
        for cp in sends:
            cp.wait_send()

    return pl.pallas_call(
        body, name="mod_exchange", in_specs=[VMEM_SPEC] * 3, out_specs=[VMEM_SPEC] * 2,
        out_shape=[jax.ShapeDtypeStruct((N_DEV, 1, D_MODEL), F32), jax.ShapeDtypeStruct((1, N_CHIP * ncol), F32)],
        scratch_shapes=[pltpu.VMEM((N_DEV, ncol), F32), pltpu.VMEM((N_CHIP, N_DEV, ncol), F32),
                        pltpu.SemaphoreType.DMA((10,)), pltpu.SemaphoreType.DMA((10,))],
        compiler_params=_cp())(c, w_ada_s, b4)


GATHER_CHUNK = 32


def _chunks(nt, chunk):
    out = []
    for t in range(nt):
        half = SHARD_ROWS[t] // 2
        step = chunk if half % chunk == 0 else half
        out += [(t, off, step) for off in range(0, half, step)]
    return out


def _weight_gather(shards):
    nt = len(shards)
    pieces = _chunks(nt, GATHER_CHUNK)
    npc = len(pieces)

    def body(*refs):
        srcs, outs = refs[:nt], refs[nt:2 * nt]
        lsem, ssem, rsem = refs[2 * nt:]
        x, y, c = _coords()
        chip = 2 * x + y
        sib = (x, y, 1 - c)

        def place(t, ch, cc, off, n):
            return _rows(outs[t], ch * SHARD_ROWS[t] + cc * (SHARD_ROWS[t] // 2) + off, n)

        local = [pltpu.make_async_copy(srcs[t], _rows(outs[t], chip * SHARD_ROWS[t], SHARD_ROWS[t]), lsem.at[t]) for t in range(nt)]
        for cp in local:
            cp.start()
        sends = []
        for k2 in range(1, N_CHIP):
            for q, (t, off, n) in enumerate(pieces):
                i = (k2 - 1) * npc + q
                cp = _remote(_rows(srcs[t], c * (SHARD_ROWS[t] // 2) + off, n), place(t, chip, c, off, n),
                             ssem.at[i], rsem.at[i], _peer(2 * k2))
                cp.start()
                sends.append(cp)
        for k2 in range(1, N_CHIP):
            pchip = _dev_index(_peer(2 * k2)) // 2
            for q, (t, off, n) in enumerate(pieces):
                i = (k2 - 1) * npc + q
                got = place(t, pchip, c, off, n)
                _remote(got, got, ssem.at[i], rsem.at[i], _peer(2 * k2)).wait_recv()
                cp = _remote(got, got, ssem.at[3 * npc + i], rsem.at[3 * npc + i], sib)
                cp.start()
                sends.append(cp)
        for k2 in range(1, N_CHIP):
            pchip = _dev_index(_peer(2 * k2)) // 2
            for q, (t, off, n) in enumerate(pieces):
                i = (k2 - 1) * npc + q
                got = place(t, pchip, 1 - c, off, n)
                _remote(got, got, ssem.at[3 * npc + i], rsem.at[3 * npc + i], sib).wait_recv()
        for cp in sends:
            cp.wait_send()
        for cp in local:
            cp.wait()

    return pl.pallas_call(
        body, name="weight_gather", in_specs=[VMEM_SPEC] * nt, out_specs=[HBM_SPEC] * nt,
        out_shape=[jax.ShapeDtypeStruct((N_CHIP * s.shape[0], s.shape[1]), s.dtype) for s in shards],
        scratch_shapes=[pltpu.SemaphoreType.DMA((nt,)), pltpu.SemaphoreType.DMA((6 * npc,)), pltpu.SemaphoreType.DMA((6 * npc,))],
        compiler_params=_cp())(*shards)


def _grad_reduce(grads):
    nt = len(grads)
    pieces = _chunks(nt, GATHER_CHUNK)
    npc = len(pieces)

    def body(*refs):
        srcs, outs = refs[:nt], refs[nt:2 * nt]
        slots, sums = refs[2 * nt:3 * nt], refs[3 * nt:4 * nt]
        lsem, ssem, rsem, osem, xsem, ysem = refs[4 * nt:]
        x, y, c = _coords()
        me = _dev_index((x, y, c))
        sib = (x, y, 1 - c)
        mine = lambda t, dev: _rows(srcs[t], (dev // 2) * SHARD_ROWS[t] + (dev % 2) * (SHARD_ROWS[t] // 2), SHARD_ROWS[t] // 2)
        local = [pltpu.make_async_copy(mine(t, me), slots[t].at[me], lsem.at[t]) for t in range(nt)]
        for cp in local:
            cp.start()
        sends = []
        for k in range(1, N_DEV):
            for t in range(nt):
                i = (k - 1) * nt + t
                cp = _remote(mine(t, _dev_index(_peer(k))), slots[t].at[me], ssem.at[i], rsem.at[i], _peer(k))
                cp.start()
                sends.append(cp)
        for k in range(1, N_DEV):
            for t in range(nt):
                i = (k - 1) * nt + t
                slot = slots[t].at[_dev_index(_peer(k))]
                _remote(slot, slot, ssem.at[i], rsem.at[i], _peer(k)).wait_recv()
        for cp in local:
            cp.wait()
        place = lambda t, cc, off, n: _rows(outs[t], cc * (SHARD_ROWS[t] // 2) + off, n)
        stores = []
        for q, (t, off, n) in enumerate(pieces):
            r = pl.ds(off, n)
            acc = slots[t][0, r, :].astype(F32)
            for d in range(1, N_DEV):
                acc = acc + slots[t][d, r, :].astype(F32)
            sums[t][r, :] = acc
            keep = pltpu.make_async_copy(sums[t].at[r, :], place(t, c, off, n), osem.at[q])
            give = _remote(sums[t].at[r, :], place(t, c, off, n), xsem.at[q], ysem.at[q], sib)
            keep.start()
            give.start()
            stores.append(keep)
            sends.append(give)
        for q, (t, off, n) in enumerate(pieces):
            got = place(t, 1 - c, off, n)
            _remote(got, got, xsem.at[q], ysem.at[q], sib).wait_recv()
        for cp in sends:
            cp.wait_send()
        for cp in stores:
            cp.wait()

    dma = pltpu.SemaphoreType.DMA
    return pl.pallas_call(
        body, name="grad_reduce", in_specs=[HBM_SPEC] * nt, out_specs=[HBM_SPEC] * nt,
        out_shape=[jax.ShapeDtypeStruct((SHARD_ROWS[t], g.shape[1]), F32) for t, g in enumerate(grads)],
        scratch_shapes=([pltpu.VMEM((N_DEV, SHARD_ROWS[t] // 2, g.shape[1]), g.dtype) for t, g in enumerate(grads)]
                        + [pltpu.VMEM((SHARD_ROWS[t] // 2, g.shape[1]), F32) for t, g in enumerate(grads)]
                        + [dma((nt,)), dma((7 * nt,)), dma((7 * nt,)), dma((npc,)), dma((npc,)), dma((npc,))]),
        compiler_params=_cp())(*grads)


def _adam_math(w, g, m, v):
    m = ADAM_B1 * m + (1.0 - ADAM_B1) * g
    v = ADAM_B2 * v + (1.0 - ADAM_B2) * (g * g)
    m_hat = m / (1.0 - ADAM_B1 ** ADAM_STEP)
    v_hat = v / (1.0 - ADAM_B2 ** ADAM_STEP)
    return -ADAM_LR * (m_hat / (jnp.sqrt(v_hat) + ADAM_EPS) + ADAM_WD * w), m, v


def _adamw(w, g, m, v, rb, name):
    rows, cols = w.shape

    def body(w_ref, g_ref, m_ref, v_ref, d_ref, nm_ref, nv_ref):
        d_ref[...], nm_ref[...], nv_ref[...] = _adam_math(w_ref[...], g_ref[...], m_ref[...], v_ref[...])

    spec = pl.BlockSpec((rb, cols), lambda i: (i, 0))
    return pl.pallas_call(body, grid=(rows // rb,), name=name, in_specs=[spec] * 4, out_specs=[spec] * 3,
                          out_shape=[jax.ShapeDtypeStruct(w.shape, F32)] * 3, compiler_params=_cp("parallel"))(w, g, m, v)


PACK = (("b_ada", 3 * D_MODEL), ("g_pre", D_MODEL), ("g_post", D_MODEL), ("gn_g", 512), ("qn_g", 64), ("kn_g", 64),
        ("w_dec_f", 4), ("w_dec_b", 4), ("loss", 1))
PACK_WIDTH = 6144


def _pack(parts):
    used = sum(n for _, n in PACK)
    cols = [parts[name].reshape(1, n).astype(F32) if name in parts else jnp.zeros((1, n), F32) for name, n in PACK]
    return jnp.concatenate(cols + [jnp.zeros((1, PACK_WIDTH - used), F32)], axis=1)


def _unpack(row):
    out, off = {}, 0
    for name, n in PACK:
        out[name] = row[:, off:off + n]
        off += n
    return out


def _small_reduce(vec, wvec, mvec, vvec, cs):
    ncol = 3 * D_MODEL // N_CHIP

    def body(vec_ref, w_ref, m_ref, v_ref, cs_ref, gsum_ref, d_ref, nm_ref, nv_ref, gwa_ref, gat, ssem, rsem):
        me = _dev_index(_coords())
        chip = me // 2
        gat[me] = vec_ref[...]
        sends = []
        for k in range(1, N_DEV):
            cp = _remote(vec_ref, gat.at[me], ssem.at[k - 1], rsem.at[k - 1], _peer(k))
            cp.start()
            sends.append(cp)
        for k in range(1, N_DEV):
            slot = gat.at[_dev_index(_peer(k))]
            _remote(slot, slot, ssem.at[k - 1], rsem.at[k - 1], _peer(k)).wait_recv()
        rows = [gat[d] for d in range(N_DEV)]
        tot = rows[0]
        for d in range(1, N_DEV):
            tot = tot + rows[d]
        gsum_ref[...] = tot
        d_ref[...], nm_ref[...], nv_ref[...] = _adam_math(w_ref[...], tot, m_ref[...], v_ref[...])
        cs = cs_ref[...]
        ca_t = jnp.transpose(jnp.concatenate([cs * _sigmoid(cs), jnp.zeros((128 - N_DEV, D_MODEL), F32)], axis=0))
        dm = jnp.concatenate(rows + [jnp.zeros((128 - N_DEV, PACK_WIDTH), F32)], axis=0)
        for jj in range(N_CHIP):
            @pl.when(chip == jj)
            def _():
                gwa_ref[...] = _nn(ca_t, dm[:, ncol * jj:ncol * (jj + 1)], precision=HIGHEST)
        for cp in sends:
            cp.wait_send()

    row = jax.ShapeDtypeStruct((1, PACK_WIDTH), F32)
    return pl.pallas_call(
        body, name="small_reduce", in_specs=[VMEM_SPEC] * 5, out_specs=[VMEM_SPEC] * 5,
        out_shape=[row, row, row, row, jax.ShapeDtypeStruct((D_MODEL, ncol), F32)],
        scratch_shapes=[pltpu.VMEM((N_DEV, 1, PACK_WIDTH), F32), pltpu.SemaphoreType.DMA((7,)), pltpu.SemaphoreType.DMA((7,))],
        compiler_params=_cp())(vec, wvec, mvec, vvec, cs)


def kernel(x, c, w_ada, b_ada, g_pre, w_in, qn_g, kn_g, w_dec_f, w_dec_b, gn_g, w_pa, w_pr, w_out, g_post, loss_target, m_w_ada, m_b_ada, m_g_pre, m_w_in, m_qn_g, m_kn_g, m_w_dec_f, m_w_dec_b, m_gn_g, m_w_pa, m_w_pr, m_w_out, m_g_post, v_w_ada, v_b_ada, v_g_pre, v_w_in, v_qn_g, v_kn_g, v_w_dec_f, v_w_dec_b, v_gn_g, v_w_pa, v_w_pr, v_w_out, v_g_post):
    shards = (w_in[0].T.astype(MXU_DTYPE), jnp.concatenate([w_pa[0].T, w_pr[0].T], axis=1).astype(MXU_DTYPE),
              w_out[0].astype(MXU_DTYPE))
    win_t, wpt, wout = _weight_gather(shards)
    cs, mod = _mod_exchange(c, w_ada[0], b_ada.reshape(N_CHIP, 3 * D_MODEL // N_CHIP))

    grad_x, g_in_t, g_pt, g_out, small = _local_step(x[0], loss_target[0], mod, g_pre, qn_g, kn_g, w_dec_f, w_dec_b,
                                                     gn_g, g_post, win_t, wpt, wout)

    full = _grad_reduce((g_in_t, g_pt, g_out))
    big = {"w_in": (w_in, m_w_in, v_w_in, full[0].T), "w_pa": (w_pa, m_w_pa, v_w_pa, full[1][:, :512].T),
           "w_pr": (w_pr, m_w_pr, v_w_pr, full[1][:, 512:].T), "w_out": (w_out, m_w_out, v_w_out, full[2])}

    small = dict(small)
    small["b_ada"] = small.pop("dmod")
    given = dict(b_ada=(b_ada, m_b_ada, v_b_ada), g_pre=(g_pre, m_g_pre, v_g_pre), g_post=(g_post, m_g_post, v_g_post),
                 gn_g=(gn_g, m_gn_g, v_gn_g), qn_g=(qn_g, m_qn_g, v_qn_g), kn_g=(kn_g, m_kn_g, v_kn_g),
                 w_dec_f=(w_dec_f, m_w_dec_f, v_w_dec_f), w_dec_b=(w_dec_b, m_w_dec_b, v_w_dec_b))
    packs = [_pack({n: t[i] for n, t in given.items()}) for i in range(3)]
    gsum, dvec, nmvec, nvvec, g_w_ada = _small_reduce(_pack(small), *packs, cs.reshape(N_DEV, D_MODEL))
    big["w_ada"] = (w_ada, m_w_ada, v_w_ada, g_w_ada)

    grads, deltas, new_m, new_v = {}, {}, {}, {}
    for name, (w, m, v, g) in big.items():
        d, nm, nv = _adamw(w[0], g, m[0], v[0], 256, "adamw_" + name)
        grads[name], deltas[name], new_m[name], new_v[name] = g[None], d[None], nm[None], nv[None]
    for dst, row in ((grads, gsum), (deltas, dvec), (new_m, nmvec), (new_v, nvvec)):
        dst.update({n: a for n, a in _unpack(row).items() if n != "loss"})
    order = ("w_ada", "b_ada", "g_pre", "w_in", "qn_g", "kn_g", "w_dec_f", "w_dec_b", "gn_g", "w_pa", "w_pr", "w_out", "g_post")
    loss = _unpack(gsum)["loss"][0, 0]
    return (loss, grad_x[None], *[grads[n] for n in order], *[deltas[n] for n in order],
            *[new_m[n] for n in order], *[new_v[n] for n in order])
```

```python
import functools

import jax
import jax.numpy as jnp
from jax import lax
from jax.experimental import pallas as pl
from jax.experimental.pallas import tpu as pltpu

F32 = jnp.float32
BF16 = jnp.bfloat16
MXU_DTYPE = jnp.bfloat16

D_MODEL = 1024
GRID_W = 64
HEAD_DIM = 64
ROPE_THETA = 10000.0
EPS = 1e-6
RET_HEADS = 4
RET_CHUNK = 256
IN_WIDTH = 4864
QA, KA, VA, ZA, QR, KR, VR, ZR, GL = 0, 512, 640, 768, 1280, 1536, 1792, 2304, 2816

ADAM_LR, ADAM_B1, ADAM_B2, ADAM_EPS, ADAM_WD, ADAM_STEP = 0.001, 0.9, 0.999, 1e-08, 0.01, 10

VMEM_LIMIT = 56 * 1024 * 1024
MESH = pl.DeviceIdType.MESH
HIGHEST = lax.Precision.HIGHEST
LOG2E = 1.4426950408889634
LN2 = 0.6931471805599453


def _cp(*sem, **kw):
    if sem:
        kw["dimension_semantics"] = sem
    return pltpu.CompilerParams(vmem_limit_bytes=VMEM_LIMIT, **kw)


def _nn(a, b, **kw):
    return lax.dot_general(a, b, (((1,), (0,)), ((), ())), preferred_element_type=F32, **kw)


def _nt(a, b):
    return lax.dot_general(a, b, (((1,), (1,)), ((), ())), preferred_element_type=F32)


def _tn(a, b):
    return lax.dot_general(a, b, (((0,), (0,)), ((), ())), preferred_element_type=F32)


def _mx(x):
    return x.astype(MXU_DTYPE)


def _lane(shape):
    return lax.broadcasted_iota(jnp.int32, shape, 1)


def _sigmoid(z):
    return 1.0 / (1.0 + jnp.exp(-z))


def _rowsum128(x):
    s = jnp.sum(x, axis=0, keepdims=True)
    out = s[:, 0:128]
    for k in range(1, x.shape[1] // 128):
        out = out + s[:, 128 * k:128 * (k + 1)]
    return out


def _swap16(x):
    return jnp.where((_lane(x.shape) & 16) == 0, pltpu.roll(x, 112, 1), pltpu.roll(x, 16, 1))


def _rope(x, cos, sin):
    return x * cos + _swap16(x) * sin


def _rope_t(d, cos, sin):
    return d * cos + _swap16(d * sin)


def _blockdiag64():
    r = lax.broadcasted_iota(jnp.int32, (128, 128), 0) // 64
    c = lax.broadcasted_iota(jnp.int32, (128, 128), 1) // 64
    return (r == c).astype(BF16)


def _seg64(x, m):
    hi = x.astype(BF16)
    lo = (x - hi.astype(F32)).astype(BF16)
    return _nn(hi, m) + _nn(lo, m)


def _rope_tables(seq):
    t = jnp.arange(seq)
    row = (t // GRID_W).astype(F32)
    col = (t % GRID_W).astype(F32)
    half = HEAD_DIM // 2
    inv_freq = ROPE_THETA ** (-jnp.arange(0, half, 2, dtype=F32) / half)
    ar = row[:, None] * inv_freq[None, :]
    ac = col[:, None] * inv_freq[None, :]
    cr, sr, cc, sc = jnp.cos(ar), jnp.sin(ar), jnp.cos(ac), jnp.sin(ac)
    cos64 = jnp.concatenate([cr, cr, cc, cc], axis=1)
    sin64 = jnp.concatenate([-sr, sr, -sc, sc], axis=1)
    return jnp.tile(cos64, (1, 2)), jnp.tile(sin64, (1, 2))


def _prenorm(x, mod, g_pre):
    seq = x.shape[0]
    bs = 512

    def body(x_ref, mod_ref, g_ref, h_ref):
        xv = x_ref[...]
        r = lax.rsqrt(jnp.mean(xv * xv, axis=-1, keepdims=True) + EPS)
        shift = mod_ref[:, 0:D_MODEL]
        scale = mod_ref[:, D_MODEL:2 * D_MODEL]
        h_ref[...] = ((xv * r) * g_ref[...] * (1.0 + scale) + shift).astype(h_ref.dtype)

    return pl.pallas_call(
        body, grid=(seq // bs,), name="prenorm",
        in_specs=[pl.BlockSpec((bs, D_MODEL), lambda i: (i, 0)), pl.BlockSpec((1, 3 * D_MODEL), lambda i: (0, 0)),
                  pl.BlockSpec((1, D_MODEL), lambda i: (0, 0))],
        out_specs=pl.BlockSpec((bs, D_MODEL), lambda i: (i, 0)),
        out_shape=jax.ShapeDtypeStruct((seq, D_MODEL), MXU_DTYPE), compiler_params=_cp("parallel"))(x, mod, g_pre)


def _prenorm_bwd(x, dh, dout, mod, g_pre):
    seq = x.shape[0]
    bs = 512

    def body(x_ref, dh_ref, dout_ref, mod_ref, g_ref, gx_ref, dshift_ref, dscale_ref, dg_ref):
        @pl.when(pl.program_id(0) == 0)
        def _():
            dshift_ref[...] = jnp.zeros_like(dshift_ref)
            dscale_ref[...] = jnp.zeros_like(dscale_ref)
            dg_ref[...] = jnp.zeros_like(dg_ref)

        xv = x_ref[...]
        dh = dh_ref[...]
        g = g_ref[...]
        r = lax.rsqrt(jnp.mean(xv * xv, axis=-1, keepdims=True) + EPS)
        xn = xv * r
        scale = mod_ref[:, D_MODEL:2 * D_MODEL]
        dshift_ref[...] += jnp.sum(dh, axis=0, keepdims=True)
        dscale_ref[...] += jnp.sum(dh * (xn * g), axis=0, keepdims=True)
        da = dh * (1.0 + scale)
        dg_ref[...] += jnp.sum(da * xn, axis=0, keepdims=True)
        dxn = da * g
        dx = r * (dxn - xn * jnp.mean(dxn * xn, axis=-1, keepdims=True))
        gx_ref[...] = dout_ref[...] + dx

    row = pl.BlockSpec((bs, D_MODEL), lambda i: (i, 0))
    vec = pl.BlockSpec((1, D_MODEL), lambda i: (0, 0))
    return pl.pallas_call(
        body, grid=(seq // bs,), name="prenorm_bwd",
        in_specs=[row, row, row, pl.BlockSpec((1, 3 * D_MODEL), lambda i: (0, 0)), vec],
        out_specs=[row, vec, vec, vec],
        out_shape=[jax.ShapeDtypeStruct((seq, D_MODEL), F32)] + [jax.ShapeDtypeStruct((1, D_MODEL), F32)] * 3,
        compiler_params=_cp("arbitrary"))(x, dh, dout, mod, g_pre)


def _dep_args(dep, grid_rank):
    if dep is None:
        return [], []
    return [dep], [pl.BlockSpec(dep.shape, lambda *_: (0,) * dep.ndim)]


def _matmul(a, b, *, ta, tb, tm, tn, out_dtype, name, dep=None):
    kdim = a.shape[0] if ta else a.shape[1]
    m = a.shape[1] if ta else a.shape[0]
    n = b.shape[0] if tb else b.shape[1]
    assert m % tm == 0 and n % tn == 0
    deps, dep_specs = _dep_args(dep, 2)

    def body(a_ref, b_ref, *rest):
        o_ref = rest[-1]
        dims = (((0 if ta else 1,), (1 if tb else 0,)), ((), ()))
        o_ref[...] = lax.dot_general(a_ref[...], b_ref[...], dims, preferred_element_type=F32).astype(o_ref.dtype)

    a_spec = pl.BlockSpec((kdim, tm), lambda j, i: (0, i)) if ta else pl.BlockSpec((tm, kdim), lambda j, i: (i, 0))
    b_spec = pl.BlockSpec((tn, kdim), lambda j, i: (j, 0)) if tb else pl.BlockSpec((kdim, tn), lambda j, i: (0, j))
    return pl.pallas_call(
        body, grid=(n // tn, m // tm), name=name, in_specs=[a_spec, b_spec] + dep_specs,
        out_specs=pl.BlockSpec((tm, tn), lambda j, i: (i, j)),
        out_shape=jax.ShapeDtypeStruct((m, n), out_dtype), compiler_params=_cp("parallel", "parallel"))(a, b, *deps)


def _prep(p, cos, sin, qg2, kg2):
    seq = p.shape[0]
    bs = 512

    def body(p_ref, cos_ref, sin_ref, qg_ref, kg_ref, qt_ref, kd_ref, vd_ref, rq_ref, rk_ref, rv_ref):
        m = _blockdiag64()
        cs, sn = cos_ref[...], sin_ref[...]
        lo = _lane((bs, 128)) < 64
        for pr in range(4):
            q = p_ref[:, QA + 128 * pr:QA + 128 * (pr + 1)]
            r = lax.rsqrt(_seg64(q * q, m) * (1.0 / 64) + EPS)
            qt_ref[:, 128 * pr:128 * (pr + 1)] = (_rope(q * r * qg_ref[...], cs, sn) * (0.125 * LOG2E)).astype(qt_ref.dtype)
        k = p_ref[:, KA:KA + 128]
        r = lax.rsqrt(_seg64(k * k, m) * (1.0 / 64) + EPS)
        kr = _rope(k * r * kg_ref[...], cs, sn)
        ksw = pltpu.roll(kr, 64, 1)
        kd_ref[0] = jnp.where(lo, kr, ksw).astype(kd_ref.dtype)
        kd_ref[1] = jnp.where(lo, ksw, kr).astype(kd_ref.dtype)
        v = p_ref[:, VA:VA + 128]
        vsw = pltpu.roll(v, 64, 1)
        vd_ref[0] = jnp.where(lo, v, vsw).astype(vd_ref.dtype)
        vd_ref[1] = jnp.where(lo, vsw, v).astype(vd_ref.dtype)
        for pr in range(2):
            sl = slice(128 * pr, 128 * (pr + 1))
            rq_ref[:, sl] = _rope(p_ref[:, QR + 128 * pr:QR + 128 * (pr + 1)], cs, sn).astype(rq_ref.dtype)
            rk_ref[:, sl] = (_rope(p_ref[:, KR + 128 * pr:KR + 128 * (pr + 1)], cs, sn) * 0.125).astype(rk_ref.dtype)
        rv_ref[...] = p_ref[:, VR:VR + 512].astype(rv_ref.dtype)

    tab = pl.BlockSpec((bs, 128), lambda i: (i, 0))
    gsp = pl.BlockSpec((1, 128), lambda i: (0, 0))
    dup = pl.BlockSpec((2, bs, 128), lambda i: (0, i, 0))
    return pl.pallas_call(
        body, grid=(seq // bs,), name="mixer_prep",
        in_specs=[pl.BlockSpec((bs, ZR), lambda i: (i, 0)), tab, tab, gsp, gsp],
        out_specs=[pl.BlockSpec((bs, 512), lambda i: (i, 0)), dup, dup, pl.BlockSpec((bs, 256), lambda i: (i, 0)),
                   pl.BlockSpec((bs, 256), lambda i: (i, 0)), pl.BlockSpec((bs, 512), lambda i: (i, 0))],
        out_shape=[jax.ShapeDtypeStruct((seq, 512), MXU_DTYPE), jax.ShapeDtypeStruct((2, seq, 128), MXU_DTYPE),
                   jax.ShapeDtypeStruct((2, seq, 128), MXU_DTYPE), jax.ShapeDtypeStruct((seq, 256), MXU_DTYPE),
                   jax.ShapeDtypeStruct((seq, 256), MXU_DTYPE), jax.ShapeDtypeStruct((seq, 512), MXU_DTYPE)],
        compiler_params=_cp("parallel"))(p, cos, sin, qg2, kg2)


def _halves(kd):
    lo = _lane(kd.shape) < 64
    z = jnp.zeros_like(kd)
    return jnp.concatenate([jnp.where(lo, kd, z), jnp.where(lo, z, kd)], axis=0)


def _attn_fwd(qt, kd, vd):
    seq = qt.shape[0]
    bq = bk = 512
    nk = seq // bk

    def body(q_ref, k_ref, v_ref, o_ref, lse_ref, m_scr, l_scr, acc):
        j, n = pl.program_id(1), pl.program_id(2)

        @pl.when(n == 0)
        def _():
            m_scr[...] = jnp.full_like(m_scr, -jnp.inf)
            l_scr[...] = jnp.zeros_like(l_scr)
            acc[...] = jnp.zeros_like(acc)

        k2, v2 = _halves(k_ref[...]), _halves(v_ref[...])
        lo = _lane((bq, 128)) < 64
        s2s = [_nt(q_ref[:, 128 * pr:128 * (pr + 1)], k2) for pr in range(2)]
        for pr in range(2):
            s2 = s2s[pr]
            ps, alphas = [], []
            for e in range(2):
                g = 2 * pr + e
                s = s2[:, e * bk:(e + 1) * bk]
                m_prev = m_scr[g]
                m_next = jnp.maximum(m_prev, jnp.max(s, axis=1, keepdims=True))
                alpha = jnp.exp2(m_prev - m_next)
                pe = jnp.exp2(s - jnp.tile(m_next, (1, bk // 128)))
                l_scr[g] = alpha * l_scr[g] + jnp.sum(pe, axis=1, keepdims=True)
                m_scr[g] = m_next
                ps.append(_mx(pe))
                alphas.append(alpha)
            o2 = _nn(jnp.concatenate(ps, axis=1), v2)
            sl = slice(128 * pr, 128 * (pr + 1))
            acc[:, sl] = acc[:, sl] * jnp.where(lo, alphas[0], alphas[1]) + o2

        @pl.when(n == nk - 1)
        def _():
            for pr in range(2):
                sl = slice(128 * pr, 128 * (pr + 1))
                o_ref[:, sl] = acc[:, sl] / jnp.where(lo, l_scr[2 * pr], l_scr[2 * pr + 1])
            for g in range(4):
                lse = jnp.transpose(m_scr[g] + jnp.log2(l_scr[g]))
                lse_ref[pl.ds(4 * j + g, 1), :] = lse[0:1, :]

    return pl.pallas_call(
        body, grid=(seq // bq, 2, nk), name="attn_fwd",
        in_specs=[pl.BlockSpec((bq, 256), lambda i, j, n: (i, j)), pl.BlockSpec((None, bk, 128), lambda i, j, n: (j, n, 0)),
                  pl.BlockSpec((None, bk, 128), lambda i, j, n: (j, n, 0))],
        out_specs=[pl.BlockSpec((bq, 256), lambda i, j, n: (i, j)), pl.BlockSpec((8, bq), lambda i, j, n: (0, i))],
        out_shape=[jax.ShapeDtypeStruct((seq, 512), F32), jax.ShapeDtypeStruct((8, seq), F32)],
        scratch_shapes=[pltpu.VMEM((4, bq, 128), F32), pltpu.VMEM((4, bq, 128), F32), pltpu.VMEM((bq, 256), F32)],
        compiler_params=_cp("parallel", "arbitrary", "arbitrary"))(qt, kd, vd)


def _attn_bwd(qt, kd, vd, do, lse, delta, dep=None):
    seq = qt.shape[0]
    bq = bk = 512
    nq, nk = seq // bq, seq // bk
    deps, dep_specs = _dep_args(dep, 3)

    def body(q_ref, do_ref, k_ref, v_ref, lse_ref, del_ref, *rest):
        dq_ref, dk_ref, dv_ref = rest[-3:]
        j, i, n = pl.program_id(0), pl.program_id(1), pl.program_id(2)

        @pl.when(n == 0)
        def _():
            dq_ref[...] = jnp.zeros_like(dq_ref)

        @pl.when((i == 0) & (n == 0))
        def _():
            dk_ref[...] = jnp.zeros_like(dk_ref)
            dv_ref[...] = jnp.zeros_like(dv_ref)

        k2, v2 = _halves(k_ref[...]), _halves(v_ref[...])
        lo = _lane((bk, 128)) < 64
        rows = pl.ds(pl.multiple_of(n * bk, bk), bk)
        sls = [slice(128 * pr, 128 * (pr + 1)) for pr in range(2)]
        s_ts = [_nt(k2, q_ref[:, sl]) for sl in sls]
        dp_ts = [_nt(v2, do_ref[:, sl]) for sl in sls]
        big = lambda r: jnp.concatenate([jnp.broadcast_to(r[0], (bk, bq)), jnp.broadcast_to(r[1], (bk, bq))], axis=0)
        for pr in range(2):
            sl = sls[pr]
            qp, dop = q_ref[:, sl], do_ref[:, sl]
            ls = [lse_ref[pl.ds(4 * j + 2 * pr + e, 1), :] for e in range(2)]
            dl = [del_ref[pl.ds(4 * j + 2 * pr + e, 1), :] for e in range(2)]
            p_t = jnp.exp2(s_ts[pr] - big(ls))
            ds_t = _mx(p_t * (dp_ts[pr] - big(dl)))
            rv = _nn(_mx(p_t), dop)
            rk = _nn(ds_t, qp) * LN2
            dv_ref[rows, :] += jnp.where(lo, rv[:bk], rv[bk:])
            dk_ref[rows, :] += jnp.where(lo, rk[:bk], rk[bk:])
            dq_ref[:, sl] += _tn(ds_t, k2)

    return pl.pallas_call(
        body, grid=(2, nq, nk), name="attn_bwd",
        in_specs=[pl.BlockSpec((bq, 256), lambda j, i, n: (i, j)), pl.BlockSpec((bq, 256), lambda j, i, n: (i, j)),
                  pl.BlockSpec((None, bk, 128), lambda j, i, n: (j, n, 0)), pl.BlockSpec((None, bk, 128), lambda j, i, n: (j, n, 0)),
                  pl.BlockSpec((8, bq), lambda j, i, n: (0, i)), pl.BlockSpec((8, bq), lambda j, i, n: (0, i))] + dep_specs,
        out_specs=[pl.BlockSpec((bq, 256), lambda j, i, n: (i, j)), pl.BlockSpec((None, seq, 128), lambda j, i, n: (j, 0, 0)),
                   pl.BlockSpec((None, seq, 128), lambda j, i, n: (j, 0, 0))],
        out_shape=[jax.ShapeDtypeStruct((seq, 512), F32), jax.ShapeDtypeStruct((2, seq, 128), F32),
                   jax.ShapeDtypeStruct((2, seq, 128), F32)],
        compiler_params=_cp("arbitrary", "arbitrary", "arbitrary"))(qt, do, kd, vd, lse, delta, *deps)


def _ret_tables(lg_f, lg_b, c):
    idx = jnp.arange(c, dtype=F32)
    diff = idx[:, None] - idx[None, :]
    a, b = lg_f[:, None, None], lg_b[:, None, None]
    low, up = (diff >= 0)[None], (diff < 0)[None]
    dmat = jnp.where(low, jnp.exp(a * jnp.maximum(diff, 0.0)[None]), jnp.exp(b * jnp.maximum(-diff, 0.0)[None]))
    dda = jnp.where(low, diff[None] * jnp.exp(a * jnp.maximum(diff, 0.0)[None]), 0.0)
    ddb = jnp.where(up, -diff[None] * jnp.exp(b * jnp.maximum(-diff, 0.0)[None]), 0.0)
    rep = lambda w: jnp.broadcast_to(w[:, :, None], (RET_HEADS, c, 128))
    wqf = rep(jnp.exp(lg_f[:, None] * (idx + 1.0)[None]))
    wqb = rep(jnp.exp(lg_b[:, None] * (c - idx)[None]))
    wkf = rep(jnp.exp(lg_f[:, None] * (c - 1.0 - idx)[None]))
    wkb = rep(jnp.exp(lg_b[:, None] * idx[None]))
    cdf, cdb = jnp.exp(lg_f * c), jnp.exp(lg_b * c)
    dec_fb = jnp.broadcast_to(jnp.concatenate([cdf, cdb])[:, None], (8, 128))
    dec_bf = jnp.broadcast_to(jnp.concatenate([cdb, cdf])[:, None], (8, 128))
    return dict(dmat=dmat, dda=dda, ddb=ddb, wqf=wqf, wqb=wqb, wkf=wkf, wkb=wkb, dec_fb=dec_fb, dec_bf=dec_bf)


def _ret_states(xk, yv, w_fwd, w_rev, dec, name):
    seq = xk.shape[0]
    c = RET_CHUNK
    nc = seq // c

    def body(xf_ref, yf_ref, xr_ref, yr_ref, wf_ref, wr_ref, dec_ref, sf_ref, sr_ref, st_f, st_r):
        @pl.when(pl.program_id(0) == 0)
        def _():
            st_f[...] = jnp.zeros_like(st_f)
            st_r[...] = jnp.zeros_like(st_r)

        lane = _lane((c, 128))
        for h in range(RET_HEADS):
            pr, e = h // 2, h % 2
            half = (lane < 64) if e == 0 else (lane >= 64)
            for x_ref, y_ref, w_ref, s_ref, st, drow in ((xf_ref, yf_ref, wf_ref, sf_ref, st_f, h),
                                                        (xr_ref, yr_ref, wr_ref, sr_ref, st_r, 4 + h)):
                s_ref[h] = st[h].astype(s_ref.dtype)
                xm = jnp.where(half, x_ref[:, 128 * pr:128 * (pr + 1)].astype(F32) * w_ref[h], 0.0)
                st[h] = st[h] * dec_ref[drow:drow + 1, :] + _tn(_mx(xm), _mx(y_ref[:, 128 * h:128 * (h + 1)]))

    xs = lambda f: pl.BlockSpec((c, 256), f)
    ys = lambda f: pl.BlockSpec((c, 512), f)
    fwd, rev = (lambda n: (n, 0)), (lambda n: (nc - 1 - n, 0))
    wsp = pl.BlockSpec((RET_HEADS, c, 128), lambda n: (0, 0, 0))
    return pl.pallas_call(
        body, grid=(nc,), name=name,
        in_specs=[xs(fwd), ys(fwd), xs(rev), ys(rev), wsp, wsp, pl.BlockSpec((8, 128), lambda n: (0, 0))],
        out_specs=[pl.BlockSpec((None, RET_HEADS, 128, 128), lambda n: (n, 0, 0, 0)),
                   pl.BlockSpec((None, RET_HEADS, 128, 128), lambda n: (nc - 1 - n, 0, 0, 0))],
        out_shape=[jax.ShapeDtypeStruct((nc, RET_HEADS, 128, 128), MXU_DTYPE)] * 2,
        scratch_shapes=[pltpu.VMEM((RET_HEADS, 128, 128), F32), pltpu.VMEM((RET_HEADS, 128, 128), F32)],
        compiler_params=_cp("arbitrary"))(xk, yv, xk, yv, w_fwd, w_rev, dec)


def _ret_fwd(rq, rk, rv, rf, rb, tb):
    seq = rq.shape[0]
    c = RET_CHUNK

    def body(q_ref, k_ref, v_ref, rf_ref, rb_ref, d_ref, wqf_ref, wqb_ref, o_ref):
        lane = _lane((c, 128))
        for h in range(RET_HEADS):
            pr, e = h // 2, h % 2
            half = (lane < 64) if e == 0 else (lane >= 64)
            sl = slice(128 * pr, 128 * (pr + 1))
            qp = q_ref[:, sl]
            km = jnp.where(half, k_ref[:, sl], jnp.zeros_like(k_ref[:, sl]))
            sd = _mx(_nt(qp, km) * d_ref[h])
            qf = qp.astype(F32)
            o = _nn(sd, v_ref[:, 128 * h:128 * (h + 1)])
            o = o + _nn(_mx(qf * wqf_ref[h]), rf_ref[h]) + _nn(_mx(qf * wqb_ref[h]), rb_ref[h])
            o_ref[:, 128 * h:128 * (h + 1)] = o

    st = pl.BlockSpec((None, RET_HEADS, 128, 128), lambda n: (n, 0, 0, 0))
    wsp = pl.BlockSpec((RET_HEADS, c, 128), lambda n: (0, 0, 0))
    return pl.pallas_call(
        body, grid=(seq // c,), name="ret_fwd",
        in_specs=[pl.BlockSpec((c, 256), lambda n: (n, 0)), pl.BlockSpec((c, 256), lambda n: (n, 0)),
                  pl.BlockSpec((c, 512), lambda n: (n, 0)), st, st, pl.BlockSpec((RET_HEADS, c, c), lambda n: (0, 0, 0)), wsp, wsp],
        out_specs=pl.BlockSpec((c, 512), lambda n: (n, 0)),
        out_shape=jax.ShapeDtypeStruct((seq, 512), F32), compiler_params=_cp("parallel"))(
            rq, rk, rv, rf, rb, tb["dmat"], tb["wqf"], tb["wqb"])


def _ret_bwd(rq, rk, rv, do, rf, rb, hf, hb, tb):
    seq = rq.shape[0]
    c = RET_CHUNK

    def body(q_ref, k_ref, v_ref, do_ref, rf_ref, rb_ref, hf_ref, hb_ref, d_ref, dda_ref, ddb_ref,
             wqf_ref, wqb_ref, wkf_ref, wkb_ref, cdec_ref, dq_ref, dk_ref, dv_ref, dlg_ref):
        @pl.when(pl.program_id(0) == 0)
        def _():
            dlg_ref[...] = jnp.zeros_like(dlg_ref)

        lane = _lane((c, 128))
        pos = lax.broadcasted_iota(jnp.int32, (c, 128), 0).astype(F32)
        for pr in range(2):
            sl = slice(128 * pr, 128 * (pr + 1))
            qp, kp = q_ref[:, sl], k_ref[:, sl]
            qf, kf = qp.astype(F32), kp.astype(F32)
            dq_acc = jnp.zeros((c, 128), F32)
            dk_acc = jnp.zeros((c, 128), F32)
            for e in range(2):
                h = 2 * pr + e
                half = (lane < 64) if e == 0 else (lane >= 64)
                hs = slice(128 * h, 128 * (h + 1))
                km = jnp.where(half, kp, jnp.zeros_like(kp))
                kmf = km.astype(F32)
                vh, doh = v_ref[:, hs], do_ref[:, hs]
                rfh, rbh, hfh, hbh = rf_ref[h], rb_ref[h], hf_ref[h], hb_ref[h]
                s = _nt(qp, km)
                dpm = _nt(doh, vh)
                ds_b = _mx(dpm * d_ref[h])
                sd_b = _mx(s * d_ref[h])
                dq_if = wqf_ref[h] * _nt(doh, rfh)
                dq_ib = wqb_ref[h] * _nt(doh, rbh)
                dq_acc = dq_acc + _nn(ds_b, km) + dq_if + dq_ib
                dk_sf = wkf_ref[h] * _nt(vh, hfh)
                dk_sb = wkb_ref[h] * _nt(vh, hbh)
                dk_acc = dk_acc + jnp.where(half, _tn(ds_b, qp), 0.0) + dk_sf + dk_sb
                dv = _tn(sd_b, doh) + _nn(_mx(kmf * wkf_ref[h]), hfh) + _nn(_mx(kmf * wkb_ref[h]), hbh)
                dv_ref[:, hs] = dv.astype(dv_ref.dtype)
                sdp = s * dpm
                hr_f = hfh.astype(F32) * rfh.astype(F32)
                hr_b = hbh.astype(F32) * rbh.astype(F32)
                da = (_rowsum128(sdp * dda_ref[h]) + _rowsum128((pos + 1.0) * qf * dq_if)
                      + _rowsum128((c - 1.0 - pos) * kf * dk_sf) + cdec_ref[h:h + 1, :] * _rowsum128(hr_f))
                db = (_rowsum128(sdp * ddb_ref[h]) + _rowsum128((c - pos) * qf * dq_ib)
                      + _rowsum128(pos * kf * dk_sb) + cdec_ref[4 + h:5 + h, :] * _rowsum128(hr_b))
                dlg_ref[h:h + 1, :] += da
                dlg_ref[4 + h:5 + h, :] += db
            dq_ref[:, sl] = dq_acc
            dk_ref[:, sl] = dk_acc

    st = pl.BlockSpec((None, RET_HEADS, 128, 128), lambda n: (n, 0, 0, 0))
    wsp = pl.BlockSpec((RET_HEADS, c, 128), lambda n: (0, 0, 0))
    dsp = pl.BlockSpec((RET_HEADS, c, c), lambda n: (0, 0, 0))
    x256 = pl.BlockSpec((c, 256), lambda n: (n, 0))
    x512 = pl.BlockSpec((c, 512), lambda n: (n, 0))
    cdec = tb["dec_fb"] * float(c)
    return pl.pallas_call(
        body, grid=(seq // c,), name="ret_bwd",
        in_specs=[x256, x256, x512, x512, st, st, st, st, dsp, dsp, dsp, wsp, wsp, wsp, wsp,
                  pl.BlockSpec((8, 128), lambda n: (0, 0))],
        out_specs=[x256, x256, x512, pl.BlockSpec((8, 128), lambda n: (0, 0))],
        out_shape=[jax.ShapeDtypeStruct((seq, 256), F32), jax.ShapeDtypeStruct((seq, 256), F32),
                   jax.ShapeDtypeStruct((seq, 512), MXU_DTYPE), jax.ShapeDtypeStruct((8, 128), F32)],
        compiler_params=_cp("arbitrary"))(
            rq, rk, rv, do, rf, rb, hf, hb, tb["dmat"], tb["dda"], tb["ddb"], tb["wqf"], tb["wqb"], tb["wkf"], tb["wkb"], cdec)


def _tail(x, tgt, o_att, o_ret, p, mod, g_post, gn_g, wpt, wout):
    seq = x.shape[0]
    bs = 256
    nb = seq // bs

    def body(x_ref, t_ref, oa_ref, or_ref, za_ref, zr_ref, gl_ref, mod_ref, gp_ref, gn_ref, wpt_ref, wout_ref,
             dout_ref, dza_ref, dzr_ref, dgl_ref, doa_ref, dor_ref, del_ref, gout_ref, gpt_ref,
             dgate_ref, dgpost_ref, dgn_ref, loss_ref, gout_acc, gpt_acc):
        i = pl.program_id(0)

        @pl.when(i == 0)
        def _():
            gout_acc[...] = jnp.zeros_like(gout_acc)
            gpt_acc[...] = jnp.zeros_like(gpt_acc)
            dgate_ref[...] = jnp.zeros_like(dgate_ref)
            dgpost_ref[...] = jnp.zeros_like(dgpost_ref)
            dgn_ref[...] = jnp.zeros_like(dgn_ref)
            loss_ref[...] = jnp.zeros_like(loss_ref)

        oa, za, zr = oa_ref[...], za_ref[...], zr_ref[...]
        sga, sgr = _sigmoid(za), _sigmoid(zr)
        sza, szr = za * sga, zr * sgr
        ya_b = _mx(oa * sza)
        ons, rstds = [], []
        for h in range(RET_HEADS):
            oh = or_ref[:, 128 * h:128 * (h + 1)]
            xc = oh - jnp.mean(oh, axis=-1, keepdims=True)
            rstd = lax.rsqrt(jnp.mean(xc * xc, axis=-1, keepdims=True) + EPS)
            ons.append(xc * rstd)
            rstds.append(rstd)
        on = jnp.concatenate(ons, axis=1)
        yn = on * gn_ref[...]
        yr_b = _mx(yn * szr)
        wpa_t, wpr_t = wpt_ref[:, 0:512], wpt_ref[:, 512:1024]
        a_att = _nt(ya_b, wpa_t)
        a_ret = _nt(yr_b, wpr_t)
        gts = _sigmoid(gl_ref[...])
        g_att, g_ret = gts[:, 0:D_MODEL], gts[:, D_MODEL:2 * D_MODEL]
        merged_b = _mx(g_att * a_att + g_ret * a_ret)
        u = _nn(merged_b, wout_ref[...])
        r = lax.rsqrt(jnp.mean(u * u, axis=-1, keepdims=True) + EPS)
        un = u * r
        gpost = gp_ref[...]
        y = un * gpost
        gate = mod_ref[:, 2 * D_MODEL:3 * D_MODEL]
        err = x_ref[...] + gate * y - t_ref[...]
        loss_ref[...] += (0.5 / D_MODEL) * _rowsum128(err * err)
        dout = err * (1.0 / D_MODEL)
        dout_ref[...] = dout
        dgate_ref[...] += jnp.sum(dout * y, axis=0, keepdims=True)
        dy = dout * gate
        dgpost_ref[...] += jnp.sum(dy * un, axis=0, keepdims=True)
        dun = dy * gpost
        du_b = _mx(r * (dun - un * jnp.mean(dun * un, axis=-1, keepdims=True)))
        dmerged = _nt(du_b, wout_ref[...])
        gout_acc[...] += _tn(merged_b, du_b)
        d_att, d_ret = dmerged * g_att, dmerged * g_ret
        dgl_ref[:, 0:D_MODEL] = (d_att * a_att * (1.0 - g_att)).astype(dgl_ref.dtype)
        dgl_ref[:, D_MODEL:2 * D_MODEL] = (d_ret * a_ret * (1.0 - g_ret)).astype(dgl_ref.dtype)
        d_att_b, d_ret_b = _mx(d_att), _mx(d_ret)
        dya = _nn(d_att_b, wpa_t)
        dyr = _nn(d_ret_b, wpr_t)
        gpt_acc[:, 0:512] += _tn(d_att_b, ya_b)
        gpt_acc[:, 512:1024] += _tn(d_ret_b, yr_b)
        dza_ref[...] = (dya * oa * (sga * (1.0 + za * (1.0 - sga)))).astype(dza_ref.dtype)
        doa = dya * sza
        doa_ref[...] = doa.astype(doa_ref.dtype)
        dt = jnp.transpose(doa * oa)
        del_ref[...] = jnp.sum(dt.reshape(8, HEAD_DIM, bs), axis=1)
        dzr_ref[...] = (dyr * yn * (sgr * (1.0 + zr * (1.0 - sgr)))).astype(dzr_ref.dtype)
        dyn = dyr * szr
        dgn_ref[...] += jnp.sum(dyn * on, axis=0, keepdims=True)
        don = dyn * gn_ref[...]
        for h in range(RET_HEADS):
            hs = slice(128 * h, 128 * (h + 1))
            dh, oh = don[:, hs], ons[h]
            doh = rstds[h] * (dh - jnp.mean(dh, axis=-1, keepdims=True) - oh * jnp.mean(dh * oh, axis=-1, keepdims=True))
            dor_ref[:, hs] = doh.astype(dor_ref.dtype)

        @pl.when(i == nb - 1)
        def _():
            gout_ref[...] = gout_acc[...].astype(gout_ref.dtype)
            gpt_ref[...] = gpt_acc[...].astype(gpt_ref.dtype)

    row = lambda w: pl.BlockSpec((bs, w), lambda i: (i, 0))
    el = lambda w, off: pl.BlockSpec((pl.Element(bs), pl.Element(w)), lambda i: (i * bs, off))
    vec = lambda w: pl.BlockSpec((1, w), lambda i: (0, 0))
    full = pl.BlockSpec((D_MODEL, D_MODEL), lambda i: (0, 0))
    sds = jax.ShapeDtypeStruct
    return pl.pallas_call(
        body, grid=(nb,), name="tail",
        in_specs=[row(D_MODEL), row(D_MODEL), row(512), row(512), el(512, ZA), el(512, ZR), el(2 * D_MODEL, GL),
                  vec(3 * D_MODEL), vec(D_MODEL), vec(512), full, full],
        out_specs=[row(D_MODEL), row(512), row(512), row(2 * D_MODEL), row(512), row(512),
                   pl.BlockSpec((8, bs), lambda i: (0, i)), full, full, vec(D_MODEL), vec(D_MODEL), vec(512), vec(128)],
        out_shape=[sds((seq, D_MODEL), F32), sds((seq, 512), MXU_DTYPE), sds((seq, 512), MXU_DTYPE),
                   sds((seq, 2 * D_MODEL), MXU_DTYPE), sds((seq, 512), MXU_DTYPE), sds((seq, 512), MXU_DTYPE),
                   sds((8, seq), F32), sds((D_MODEL, D_MODEL), MXU_DTYPE), sds((D_MODEL, D_MODEL), MXU_DTYPE),
                   sds((1, D_MODEL), F32), sds((1, D_MODEL), F32), sds((1, 512), F32), sds((1, 128), F32)],
        scratch_shapes=[pltpu.VMEM((D_MODEL, D_MODEL), F32), pltpu.VMEM((D_MODEL, D_MODEL), F32)],
        compiler_params=_cp("arbitrary"))(x, tgt, o_att, o_ret, p, p, p, mod, g_post, gn_g, wpt, wout)


def _assemble(p, cos, sin, qg2, kg2, dqt, dkp, dvp, dza, drq, drk, drv, dzr, dgl):
    seq = p.shape[0]
    bs = 512

    def body(p_ref, cos_ref, sin_ref, qg_ref, kg_ref, dqt_ref, dkp_ref, dvp_ref, dza_ref, drq_ref, drk_ref, drv_ref,
             dzr_ref, dgl_ref, dp_ref, dqg_ref, dkg_ref):
        @pl.when(pl.program_id(0) == 0)
        def _():
            dqg_ref[...] = jnp.zeros_like(dqg_ref)
            dkg_ref[...] = jnp.zeros_like(dkg_ref)

        m = _blockdiag64()
        cs, sn = cos_ref[...], sin_ref[...]
        lo = _lane((bs, 128)) < 64

        def norm_bwd(raw, dn, g, dg_ref):
            r = lax.rsqrt(_seg64(raw * raw, m) * (1.0 / 64) + EPS)
            xn = raw * r
            dg_ref[...] += jnp.sum(dn * xn, axis=0, keepdims=True)
            dxn = dn * g
            return r * (dxn - xn * (_seg64(dxn * xn, m) * (1.0 / 64)))

        for pr in range(4):
            sl = slice(128 * pr, 128 * (pr + 1))
            dn = _rope_t(dqt_ref[:, sl] * 0.125, cs, sn)
            dp_ref[:, QA + 128 * pr:QA + 128 * (pr + 1)] = norm_bwd(p_ref[:, sl], dn, qg_ref[...], dqg_ref).astype(dp_ref.dtype)
        fold = lambda a: a + pltpu.roll(a, 64, 1)
        dk = jnp.where(lo, fold(dkp_ref[0]), fold(dkp_ref[1]))
        dp_ref[:, KA:KA + 128] = norm_bwd(p_ref[:, KA:KA + 128], _rope_t(dk, cs, sn), kg_ref[...], dkg_ref).astype(dp_ref.dtype)
        dp_ref[:, VA:VA + 128] = jnp.where(lo, fold(dvp_ref[0]), fold(dvp_ref[1])).astype(dp_ref.dtype)
        dp_ref[:, ZA:ZA + 512] = dza_ref[...]
        for pr in range(2):
            sl = slice(128 * pr, 128 * (pr + 1))
            dp_ref[:, QR + 128 * pr:QR + 128 * (pr + 1)] = _rope_t(drq_ref[:, sl], cs, sn).astype(dp_ref.dtype)
            dp_ref[:, KR + 128 * pr:KR + 128 * (pr + 1)] = _rope_t(drk_ref[:, sl] * 0.125, cs, sn).astype(dp_ref.dtype)
        dp_ref[:, VR:VR + 512] = drv_ref[...]
        dp_ref[:, ZR:ZR + 512] = dzr_ref[...]
        dp_ref[:, GL:GL + 2 * D_MODEL] = dgl_ref[...]

    row = lambda w: pl.BlockSpec((bs, w), lambda i: (i, 0))
    gsp = pl.BlockSpec((1, 128), lambda i: (0, 0))
    dup = pl.BlockSpec((2, bs, 128), lambda i: (0, i, 0))
    return pl.pallas_call(
        body, grid=(seq // bs,), name="assemble_dp",
        in_specs=[row(768), row(128), row(128), gsp, gsp, row(512), dup, dup, row(512), row(256), row(256), row(512),
                  row(512), row(2 * D_MODEL)],
        out_specs=[row(IN_WIDTH), gsp, gsp],
        out_shape=[jax.ShapeDtypeStruct((seq, IN_WIDTH), MXU_DTYPE), jax.ShapeDtypeStruct((1, 128), F32),
                   jax.ShapeDtypeStruct((1, 128), F32)],
        compiler_params=_cp("arbitrary"))(p, cos, sin, qg2, kg2, dqt, dkp, dvp, dza, drq, drk, drv, dzr, dgl)


def _local_step(x, tgt, mod, g_pre, qn_g, kn_g, w_dec_f, w_dec_b, gn_g, g_post, win_t, wpt, wout, send=None):
    send = send or (lambda name, arrays: None)
    seq = x.shape[0]
    cos, sin = _rope_tables(seq)
    qg2, kg2 = jnp.tile(qn_g, (1, 2)), jnp.tile(kn_g, (1, 2))
    lg_f = jax.nn.log_sigmoid(w_dec_f[0])
    lg_b = jax.nn.log_sigmoid(w_dec_b[0])
    tb = _ret_tables(lg_f, lg_b, RET_CHUNK)

    h = _prenorm(x, mod, g_pre)
    p = _matmul(h, win_t, ta=False, tb=True, tm=512, tn=IN_WIDTH // 2, out_dtype=F32, name="in_proj")
    qt, kd, vd, rq, rk, rv = _prep(p, cos, sin, qg2, kg2)
    o_att, lse = _attn_fwd(qt, kd, vd)
    rf, rb = _ret_states(rk, rv, tb["wkf"], tb["wkb"], tb["dec_fb"], "ret_states_fwd")
    o_ret = _ret_fwd(rq, rk, rv, rf, rb, tb)
    (dout, dza, dzr, dgl, do_att, do_ret, delta, g_out, g_pt, dgate, dgpost, dgn, loss_l) = _tail(
        x, tgt, o_att, o_ret, p, mod, g_post, gn_g, wpt, wout)
    dep = send("early", (g_pt, g_out))
    dqt, dkp, dvp = _attn_bwd(qt, kd, vd, do_att, lse, delta, dep=dep)
    hb, hf = _ret_states(rq, do_ret, tb["wqb"], tb["wqf"], tb["dec_bf"], "ret_states_bwd")
    drq, drk, drv, dlg = _ret_bwd(rq, rk, rv, do_ret, rf, rb, hf, hb, tb)
    dp, dqg, dkg = _assemble(p, cos, sin, qg2, kg2, dqt, dkp, dvp, dza, drq, drk, drv, dzr, dgl)
    g_in_t = _matmul(dp, h, ta=True, tb=False, tm=256, tn=D_MODEL, out_dtype=MXU_DTYPE, name="in_proj_dw")
    dep = send("late", (g_in_t,))
    dh = _matmul(dp, win_t, ta=False, tb=False, tm=512, tn=D_MODEL, out_dtype=F32, name="in_proj_dx", dep=dep)
    grad_x, dshift, dscale, dgpre = _prenorm_bwd(x, dh, dout, mod, g_pre)

    dlg = jnp.sum(dlg, axis=1)
    small = dict(
        dmod=jnp.concatenate([dshift, dscale, dgate], axis=1),
        g_pre=dgpre, g_post=dgpost, gn_g=dgn,
        qn_g=dqg[:, :64] + dqg[:, 64:], kn_g=dkg[:, :64] + dkg[:, 64:],
        w_dec_f=(dlg[0:4] * jax.nn.sigmoid(-w_dec_f[0]))[None], w_dec_b=(dlg[4:8] * jax.nn.sigmoid(-w_dec_b[0]))[None],
        loss=jnp.sum(loss_l, axis=1, keepdims=True))
    return grad_x, g_in_t, g_pt, g_out, small


N_DEV = 8
N_CHIP = 4
HBM_SPEC = pl.BlockSpec(memory_space=pl.ANY)
VMEM_SPEC = pl.BlockSpec(memory_space=pltpu.VMEM)
SHARD_ROWS = (IN_WIDTH // N_CHIP, D_MODEL // N_CHIP, D_MODEL // N_CHIP)


def _coords():
    return lax.axis_index("x"), lax.axis_index("y"), lax.axis_index("c")


def _peer(k):
    x, y, c = _coords()
    return (1 - x if k & 4 else x, 1 - y if k & 2 else y, 1 - c if k & 1 else c)


def _dev_index(p):
    return 4 * p[0] + 2 * p[1] + p[2]


def _rows(ref, start, size):
    return ref.at[pl.ds(pl.multiple_of(start, 16), size), :]


def _remote(src, dst, ssem, rsem, dev):
    return pltpu.make_async_remote_copy(src_ref=src, dst_ref=dst, send_sem=ssem, recv_sem=rsem, device_id=dev,
                                        device_id_type=MESH)


def _mod_exchange(c, w_ada_s, b4):
    ncol = w_ada_s.shape[1]

    def body(c_ref, w_ref, b_ref, cs_ref, mod_ref, modsh, modall, ssem, rsem):
        me = _dev_index(_coords())
        chip = me // 2
        cs_ref[me] = c_ref[...]
        sends = []
        for k in range(1, N_DEV):
            cp = _remote(c_ref, cs_ref.at[me], ssem.at[k - 1], rsem.at[k - 1], _peer(k))
            cp.start()
            sends.append(cp)
        for k in range(1, N_DEV):
            slot = cs_ref.at[_dev_index(_peer(k))]
            _remote(slot, slot, ssem.at[k - 1], rsem.at[k - 1], _peer(k)).wait_recv()
        cs = jnp.concatenate([cs_ref[d] for d in range(N_DEV)], axis=0)
        ms = _nn(cs * _sigmoid(cs), w_ref[...], precision=HIGHEST) + b_ref[pl.ds(chip, 1), :]
        modsh[...] = ms
        modall[chip] = ms
        for k2 in range(1, N_CHIP):
            cp = _remote(modsh, modall.at[chip], ssem.at[6 + k2], rsem.at[6 + k2], _peer(2 * k2))
            cp.start()
            sends.append(cp)
        for k2 in range(1, N_CHIP):
            slot = modall.at[_dev_index(_peer(2 * k2)) // 2]
            _remote(slot, slot, ssem.at[6 + k2], rsem.at[6 + k2], _peer(2 * k2)).wait_recv()
        for jj in range(N_CHIP):
            mod_ref[:, ncol * jj:ncol * (jj + 1)] = modall[jj, pl.ds(me, 1), :]
        for cp in sends:
            cp.wait_send()

    return pl.pallas_call(
        body, name="mod_exchange", in_specs=[VMEM_SPEC] * 3, out_specs=[VMEM_SPEC] * 2,
        out_shape=[jax.ShapeDtypeStruct((N_DEV, 1, D_MODEL), F32), jax.ShapeDtypeStruct((1, N_CHIP * ncol), F32)],
        scratch_shapes=[pltpu.VMEM((N_DEV, ncol), F32), pltpu.VMEM((N_CHIP, N_DEV, ncol), F32),
                        pltpu.SemaphoreType.DMA((10,)), pltpu.SemaphoreType.DMA((10,))],
        compiler_params=_cp())(c, w_ada_s, b4)


GATHER_CHUNK = 32


def _chunks(nt, chunk):
    out = []
    for t in range(nt):
        half = SHARD_ROWS[t] // 2
        step = chunk if half % chunk == 0 else half
        out += [(t, off, step) for off in range(0, half, step)]
    return out


def _weight_gather(shards):
    nt = len(shards)
    pieces = _chunks(nt, GATHER_CHUNK)
    npc = len(pieces)

    def body(*refs):
        srcs, outs = refs[:nt], refs[nt:2 * nt]
        lsem, ssem, rsem = refs[2 * nt:]
        x, y, c = _coords()
        chip = 2 * x + y
        sib = (x, y, 1 - c)

        def place(t, ch, cc, off, n):
            return _rows(outs[t], ch * SHARD_ROWS[t] + cc * (SHARD_ROWS[t] // 2) + off, n)

        local = [pltpu.make_async_copy(srcs[t], _rows(outs[t], chip * SHARD_ROWS[t], SHARD_ROWS[t]), lsem.at[t]) for t in range(nt)]
        for cp in local:
            cp.start()
        sends = []
        for k2 in range(1, N_CHIP):
            for q, (t, off, n) in enumerate(pieces):
                i = (k2 - 1) * npc + q
                cp = _remote(_rows(srcs[t], c * (SHARD_ROWS[t] // 2) + off, n), place(t, chip, c, off, n),
                             ssem.at[i], rsem.at[i], _peer(2 * k2))
                cp.start()
                sends.append(cp)
        for k2 in range(1, N_CHIP):
            pchip = _dev_index(_peer(2 * k2)) // 2
            for q, (t, off, n) in enumerate(pieces):
                i = (k2 - 1) * npc + q
                got = place(t, pchip, c, off, n)
                _remote(got, got, ssem.at[i], rsem.at[i], _peer(2 * k2)).wait_recv()
                cp = _remote(got, got, ssem.at[3 * npc + i], rsem.at[3 * npc + i], sib)
                cp.start()
                sends.append(cp)
        for k2 in range(1, N_CHIP):
            pchip = _dev_index(_peer(2 * k2)) // 2
            for q, (t, off, n) in enumerate(pieces):
                i = (k2 - 1) * npc + q
                got = place(t, pchip, 1 - c, off, n)
                _remote(got, got, ssem.at[3 * npc + i], rsem.at[3 * npc + i], sib).wait_recv()
        for cp in sends:
            cp.wait_send()
        for cp in local:
            cp.wait()

    return pl.pallas_call(
        body, name="weight_gather", in_specs=[VMEM_SPEC] * nt, out_specs=[HBM_SPEC] * nt,
        out_shape=[jax.ShapeDtypeStruct((N_CHIP * s.shape[0], s.shape[1]), s.dtype) for s in shards],
        scratch_shapes=[pltpu.SemaphoreType.DMA((nt,)), pltpu.SemaphoreType.DMA((6 * npc,)), pltpu.SemaphoreType.DMA((6 * npc,))],
        compiler_params=_cp())(*shards)


SEM_SPEC = pl.BlockSpec(memory_space=pltpu.SEMAPHORE)
HBM_ONLY = pl.BlockSpec(memory_space=pltpu.HBM)
DATAFLOW = pltpu.SideEffectType.DATAFLOW_SIDE_EFFECTING


def _reduced_by(ref, t, dev):
    return _rows(ref, (dev // 2) * SHARD_ROWS[t] + (dev % 2) * (SHARD_ROWS[t] // 2), SHARD_ROWS[t] // 2)


def _scatter_start(grads, kinds, name):
    n = len(grads)

    def body(*refs):
        srcs, lands = refs[:n], refs[n:2 * n]
        ssem, rsem = refs[2 * n], refs[2 * n + 1]
        token = refs[-1]
        me = _dev_index(_coords())
        for k in range(1, N_DEV):
            for i, t in enumerate(kinds):
                q = (k - 1) * n + i
                _remote(_reduced_by(srcs[i], t, _dev_index(_peer(k))), lands[i].at[me], ssem.at[q], rsem.at[q], _peer(k)).start()
        token[...] = jnp.zeros_like(token)

    zones = [lax.empty((N_DEV, SHARD_ROWS[t] // 2, g.shape[1]), g.dtype) for g, t in zip(grads, kinds)]
    hbm = lambda a: pltpu.with_memory_space_constraint(a, pltpu.HBM)
    dma = pltpu.SemaphoreType.DMA
    outs = pl.pallas_call(
        body, name=name, in_specs=[HBM_ONLY] * (2 * n), out_specs=[SEM_SPEC, SEM_SPEC] + [HBM_ONLY] * (2 * n) + [VMEM_SPEC],
        out_shape=[dma((7 * n,)), dma((7 * n,))] + [pltpu.HBM(a.shape, a.dtype) for a in list(grads) + zones]
        + [jax.ShapeDtypeStruct((8, 128), F32)],
        input_output_aliases={i: 2 + i for i in range(2 * n)},
        compiler_params=pltpu.CompilerParams(has_side_effects=DATAFLOW))(*[hbm(a) for a in list(grads) + zones])
    return outs[0], outs[1], outs[2:2 + n], outs[2 + n:2 + 2 * n], outs[-1]


def _scatter_wait(started, kinds, after, name):
    ssem, rsem, grads, zones, _ = started
    n = len(grads)

    def body(*refs):
        srcs, lands = refs[:n], refs[n:2 * n]
        ssem_ref, rsem_ref = refs[2 * n], refs[2 * n + 1]
        for k in range(1, N_DEV):
            peer = _dev_index(_peer(k))
            for i, t in enumerate(kinds):
                q = (k - 1) * n + i
                cp = _remote(_reduced_by(srcs[i], t, peer), lands[i].at[peer], ssem_ref.at[q], rsem_ref.at[q], _peer(k))
                cp.wait_send()
                cp.wait_recv()

    outs = pl.pallas_call(
        body, name=name, in_specs=[HBM_ONLY] * (2 * n) + [SEM_SPEC, SEM_SPEC, HBM_SPEC], out_specs=[HBM_ONLY] * (2 * n),
        out_shape=[pltpu.HBM(a.shape, a.dtype) for a in list(grads) + list(zones)],
        input_output_aliases={i: i for i in range(2 * n)},
        compiler_params=pltpu.CompilerParams(has_side_effects=DATAFLOW))(*grads, *zones, ssem, rsem, after)
    return outs[:n], outs[n:]


def _grad_finish(grads, zones):
    nt = len(grads)
    pieces = _chunks(nt, GATHER_CHUNK)
    npc = len(pieces)

    def body(*refs):
        srcs, lands, outs = refs[:nt], refs[nt:2 * nt], refs[2 * nt:3 * nt]
        slots, sums = refs[3 * nt:4 * nt], refs[4 * nt:5 * nt]
        lsem, osem, xsem, ysem = refs[5 * nt:]
        x, y, c = _coords()
        me = _dev_index((x, y, c))
        sib = (x, y, 1 - c)
        loads = [pltpu.make_async_copy(_reduced_by(srcs[t], t, me), slots[t].at[me], lsem.at[t]) for t in range(nt)]
        for k in range(1, N_DEV):
            peer = _dev_index(_peer(k))
            loads += [pltpu.make_async_copy(lands[t].at[peer], slots[t].at[peer], lsem.at[k * nt + t]) for t in range(nt)]
        for cp in loads:
            cp.start()
        for cp in loads:
            cp.wait()
        sends = []
        place = lambda t, cc, off, n: _rows(outs[t], cc * (SHARD_ROWS[t] // 2) + off, n)
        stores = []
        for q, (t, off, n) in enumerate(pieces):
            r = pl.ds(off, n)
            acc = slots[t][0, r, :].astype(F32)
            for d in range(1, N_DEV):
                acc = acc + slots[t][d, r, :].astype(F32)
            sums[t][r, :] = acc
            keep = pltpu.make_async_copy(sums[t].at[r, :], place(t, c, off, n), osem.at[q])
            give = _remote(sums[t].at[r, :], place(t, c, off, n), xsem.at[q], ysem.at[q], sib)
            keep.start()
            give.start()
            stores.append(keep)
            sends.append(give)
        for q, (t, off, n) in enumerate(pieces):
            got = place(t, 1 - c, off, n)
            _remote(got, got, xsem.at[q], ysem.at[q], sib).wait_recv()
        for cp in sends:
            cp.wait_send()
        for cp in stores:
            cp.wait()

    dma = pltpu.SemaphoreType.DMA
    return pl.pallas_call(
        body, name="grad_finish", in_specs=[HBM_SPEC] * (2 * nt), out_specs=[HBM_SPEC] * nt,
        out_shape=[jax.ShapeDtypeStruct((SHARD_ROWS[t], g.shape[1]), F32) for t, g in enumerate(grads)],
        scratch_shapes=([pltpu.VMEM((N_DEV, SHARD_ROWS[t] // 2, g.shape[1]), g.dtype) for t, g in enumerate(grads)]
                        + [pltpu.VMEM((SHARD_ROWS[t] // 2, g.shape[1]), F32) for t, g in enumerate(grads)]
                        + [dma((N_DEV * nt,)), dma((npc,)), dma((npc,)), dma((npc,))]),
        compiler_params=_cp())(*grads, *zones)


def _adam_math(w, g, m, v):
    m = ADAM_B1 * m + (1.0 - ADAM_B1) * g
    v = ADAM_B2 * v + (1.0 - ADAM_B2) * (g * g)
    m_hat = m / (1.0 - ADAM_B1 ** ADAM_STEP)
    v_hat = v / (1.0 - ADAM_B2 ** ADAM_STEP)
    return -ADAM_LR * (m_hat / (jnp.sqrt(v_hat) + ADAM_EPS) + ADAM_WD * w), m, v


def _adamw(w, g, m, v, rb, name):
    rows, cols = w.shape

    def body(w_ref, g_ref, m_ref, v_ref, d_ref, nm_ref, nv_ref):
        d_ref[...], nm_ref[...], nv_ref[...] = _adam_math(w_ref[...], g_ref[...], m_ref[...], v_ref[...])

    spec = pl.BlockSpec((rb, cols), lambda i: (i, 0))
    return pl.pallas_call(body, grid=(rows // rb,), name=name, in_specs=[spec] * 4, out_specs=[spec] * 3,
                          out_shape=[jax.ShapeDtypeStruct(w.shape, F32)] * 3, compiler_params=_cp("parallel"))(w, g, m, v)


PACK = (("b_ada", 3 * D_MODEL), ("g_pre", D_MODEL), ("g_post", D_MODEL), ("gn_g", 512), ("qn_g", 64), ("kn_g", 64),
        ("w_dec_f", 4), ("w_dec_b", 4), ("loss", 1))
PACK_WIDTH = 6144


def _pack(parts):
    used = sum(n for _, n in PACK)
    cols = [parts[name].reshape(1, n).astype(F32) if name in parts else jnp.zeros((1, n), F32) for name, n in PACK]
    return jnp.concatenate(cols + [jnp.zeros((1, PACK_WIDTH - used), F32)], axis=1)


def _unpack(row):
    out, off = {}, 0
    for name, n in PACK:
        out[name] = row[:, off:off + n]
        off += n
    return out


def _small_reduce(vec, wvec, mvec, vvec, cs):
    ncol = 3 * D_MODEL // N_CHIP

    def body(vec_ref, w_ref, m_ref, v_ref, cs_ref, gsum_ref, d_ref, nm_ref, nv_ref, gwa_ref, gat, ssem, rsem):
        me = _dev_index(_coords())
        chip = me // 2
        gat[me] = vec_ref[...]
        sends = []
        for k in range(1, N_DEV):
            cp = _remote(vec_ref, gat.at[me], ssem.at[k - 1], rsem.at[k - 1], _peer(k))
            cp.start()
            sends.append(cp)
        for k in range(1, N_DEV):
            slot = gat.at[_dev_index(_peer(k))]
            _remote(slot, slot, ssem.at[k - 1], rsem.at[k - 1], _peer(k)).wait_recv()
        rows = [gat[d] for d in range(N_DEV)]
        tot = rows[0]
        for d in range(1, N_DEV):
            tot = tot + rows[d]
        gsum_ref[...] = tot
        d_ref[...], nm_ref[...], nv_ref[...] = _adam_math(w_ref[...], tot, m_ref[...], v_ref[...])
        cs = cs_ref[...]
        ca_t = jnp.transpose(jnp.concatenate([cs * _sigmoid(cs), jnp.zeros((128 - N_DEV, D_MODEL), F32)], axis=0))
        dm = jnp.concatenate(rows + [jnp.zeros((128 - N_DEV, PACK_WIDTH), F32)], axis=0)
        for jj in range(N_CHIP):
            @pl.when(chip == jj)
            def _():
                gwa_ref[...] = _nn(ca_t, dm[:, ncol * jj:ncol * (jj + 1)], precision=HIGHEST)
        for cp in sends:
            cp.wait_send()

    row = jax.ShapeDtypeStruct((1, PACK_WIDTH), F32)
    return pl.pallas_call(
        body, name="small_reduce", in_specs=[VMEM_SPEC] * 5, out_specs=[VMEM_SPEC] * 5,
        out_shape=[row, row, row, row, jax.ShapeDtypeStruct((D_MODEL, ncol), F32)],
        scratch_shapes=[pltpu.VMEM((N_DEV, 1, PACK_WIDTH), F32), pltpu.SemaphoreType.DMA((7,)), pltpu.SemaphoreType.DMA((7,))],
        compiler_params=_cp())(vec, wvec, mvec, vvec, cs)


def kernel(x, c, w_ada, b_ada, g_pre, w_in, qn_g, kn_g, w_dec_f, w_dec_b, gn_g, w_pa, w_pr, w_out, g_post, loss_target, m_w_ada, m_b_ada, m_g_pre, m_w_in, m_qn_g, m_kn_g, m_w_dec_f, m_w_dec_b, m_gn_g, m_w_pa, m_w_pr, m_w_out, m_g_post, v_w_ada, v_b_ada, v_g_pre, v_w_in, v_qn_g, v_kn_g, v_w_dec_f, v_w_dec_b, v_gn_g, v_w_pa, v_w_pr, v_w_out, v_g_post):
    shards = (w_in[0].T.astype(MXU_DTYPE), jnp.concatenate([w_pa[0].T, w_pr[0].T], axis=1).astype(MXU_DTYPE),
              w_out[0].astype(MXU_DTYPE))
    win_t, wpt, wout = _weight_gather(shards)
    cs, mod = _mod_exchange(c, w_ada[0], b_ada.reshape(N_CHIP, 3 * D_MODEL // N_CHIP))

    kinds = {"early": (1, 2), "late": (0,)}
    started = {}

    def send(name, arrays):
        started[name] = _scatter_start(arrays, kinds[name], "grad_scatter_" + name)
        return started[name][-1]

    grad_x, _, _, _, small = _local_step(x[0], loss_target[0], mod, g_pre, qn_g, kn_g, w_dec_f, w_dec_b,
                                         gn_g, g_post, win_t, wpt, wout, send=send)
    (g_pt, g_out), (z_pt, z_out) = _scatter_wait(started["early"], kinds["early"], grad_x, "grad_scatter_early_wait")
    (g_in_t,), (z_in,) = _scatter_wait(started["late"], kinds["late"], grad_x, "grad_scatter_late_wait")
    full = _grad_finish((g_in_t, g_pt, g_out), (z_in, z_pt, z_out))
    big = {"w_in": (w_in, m_w_in, v_w_in, full[0].T), "w_pa": (w_pa, m_w_pa, v_w_pa, full[1][:, :512].T),
           "w_pr": (w_pr, m_w_pr, v_w_pr, full[1][:, 512:].T), "w_out": (w_out, m_w_out, v_w_out, full[2])}

    small = dict(small)
    small["b_ada"] = small.pop("dmod")
    given = dict(b_ada=(b_ada, m_b_ada, v_b_ada), g_pre=(g_pre, m_g_pre, v_g_pre), g_post=(g_post, m_g_post, v_g_post),
                 gn_g=(gn_g, m_gn_g, v_gn_g), qn_g=(qn_g, m_qn_g, v_qn_g), kn_g=(kn_g, m_kn_g, v_kn_g),
                 w_dec_f=(w_dec_f, m_w_dec_f, v_w_dec_f), w_dec_b=(w_dec_b, m_w_dec_b, v_w_dec_b))
    packs = [_pack({n: t[i] for n, t in given.items()}) for i in range(3)]
    gsum, dvec, nmvec, nvvec, g_w_ada = _small_reduce(_pack(small), *packs, cs.reshape(N_DEV, D_MODEL))
    big["w_ada"] = (w_ada, m_w_ada, v_w_ada, g_w_ada)

    grads, deltas, new_m, new_v = {}, {}, {}, {}
    for name, (w, m, v, g) in big.items():
        d, nm, nv = _adamw(w[0], g, m[0], v[0], 256, "adamw_" + name)
        grads[name], deltas[name], new_m[name], new_v[name] = g[None], d[None], nm[None], nv[None]
    for dst, row in ((grads, gsum), (deltas, dvec), (new_m, nmvec), (new_v, nvvec)):
        dst.update({n: a for n, a in _unpack(row).items() if n != "loss"})
    order = ("w_ada", "b_ada", "g_pre", "w_in", "qn_g", "kn_g", "w_dec_f", "w_dec_b", "gn_g", "w_pa", "w_pr", "w_out", "g_post")
    loss = _unpack(gsum)["loss"][0, 0]
    return (loss, grad_x[None], *[grads[n] for n in order], *[deltas[n] for n in order],
            *[new_m[n] for n in order], *[new_v[n] for n in order])
```

```python
import functools

import jax
import jax.numpy as jnp
from jax import lax
from jax.experimental import pallas as pl
from jax.experimental.pallas import tpu as pltpu

F32 = jnp.float32
BF16 = jnp.bfloat16
MXU_DTYPE = jnp.bfloat16

D_MODEL = 1024
GRID_W = 64
HEAD_DIM = 64
ROPE_THETA = 10000.0
EPS = 1e-6
RET_HEADS = 4
RET_CHUNK = 256
IN_WIDTH = 4864
QA, KA, VA, ZA, QR, KR, VR, ZR, GL = 0, 512, 640, 768, 1280, 1536, 1792, 2304, 2816

ADAM_LR, ADAM_B1, ADAM_B2, ADAM_EPS, ADAM_WD, ADAM_STEP = 0.001, 0.9, 0.999, 1e-08, 0.01, 10

VMEM_LIMIT = 56 * 1024 * 1024
MESH = pl.DeviceIdType.MESH
HIGHEST = lax.Precision.HIGHEST
LOG2E = 1.4426950408889634
LN2 = 0.6931471805599453


def _cp(*sem, **kw):
    if sem:
        kw["dimension_semantics"] = sem
    return pltpu.CompilerParams(vmem_limit_bytes=VMEM_LIMIT, **kw)


def _nn(a, b, **kw):
    return lax.dot_general(a, b, (((1,), (0,)), ((), ())), preferred_element_type=F32, **kw)


def _nt(a, b):
    return lax.dot_general(a, b, (((1,), (1,)), ((), ())), preferred_element_type=F32)


def _tn(a, b):
    return lax.dot_general(a, b, (((0,), (0,)), ((), ())), preferred_element_type=F32)


def _mx(x):
    return x.astype(MXU_DTYPE)


def _lane(shape):
    return lax.broadcasted_iota(jnp.int32, shape, 1)


def _sigmoid(z):
    return 1.0 / (1.0 + jnp.exp(-z))


def _rowsum128(x):
    s = jnp.sum(x, axis=0, keepdims=True)
    out = s[:, 0:128]
    for k in range(1, x.shape[1] // 128):
        out = out + s[:, 128 * k:128 * (k + 1)]
    return out


def _swap16(x):
    return jnp.where((_lane(x.shape) & 16) == 0, pltpu.roll(x, 112, 1), pltpu.roll(x, 16, 1))


def _rope(x, cos, sin):
    return x * cos + _swap16(x) * sin


def _rope_t(d, cos, sin):
    return d * cos + _swap16(d * sin)


def _blockdiag64():
    r = lax.broadcasted_iota(jnp.int32, (128, 128), 0) // 64
    c = lax.broadcasted_iota(jnp.int32, (128, 128), 1) // 64
    return (r == c).astype(BF16)


def _seg64(x, m):
    hi = x.astype(BF16)
    lo = (x - hi.astype(F32)).astype(BF16)
    return _nn(hi, m) + _nn(lo, m)


def _rope_tables(seq):
    t = jnp.arange(seq)
    row = (t // GRID_W).astype(F32)
    col = (t % GRID_W).astype(F32)
    half = HEAD_DIM // 2
    inv_freq = ROPE_THETA ** (-jnp.arange(0, half, 2, dtype=F32) / half)
    ar = row[:, None] * inv_freq[None, :]
    ac = col[:, None] * inv_freq[None, :]
    cr, sr, cc, sc = jnp.cos(ar), jnp.sin(ar), jnp.cos(ac), jnp.sin(ac)
    cos64 = jnp.concatenate([cr, cr, cc, cc], axis=1)
    sin64 = jnp.concatenate([-sr, sr, -sc, sc], axis=1)
    return jnp.tile(cos64, (1, 2)), jnp.tile(sin64, (1, 2))


def _prenorm(x, mod, g_pre):
    seq = x.shape[0]
    bs = 512

    def body(x_ref, mod_ref, g_ref, h_ref):
        xv = x_ref[...]
        r = lax.rsqrt(jnp.mean(xv * xv, axis=-1, keepdims=True) + EPS)
        shift = mod_ref[:, 0:D_MODEL]
        scale = mod_ref[:, D_MODEL:2 * D_MODEL]
        h_ref[...] = ((xv * r) * g_ref[...] * (1.0 + scale) + shift).astype(h_ref.dtype)

    return pl.pallas_call(
        body, grid=(seq // bs,), name="prenorm",
        in_specs=[pl.BlockSpec((bs, D_MODEL), lambda i: (i, 0)), pl.BlockSpec((1, 3 * D_MODEL), lambda i: (0, 0)),
                  pl.BlockSpec((1, D_MODEL), lambda i: (0, 0))],
        out_specs=pl.BlockSpec((bs, D_MODEL), lambda i: (i, 0)),
        out_shape=jax.ShapeDtypeStruct((seq, D_MODEL), MXU_DTYPE), compiler_params=_cp("parallel"))(x, mod, g_pre)


def _prenorm_bwd(x, dh, dout, mod, g_pre):
    seq = x.shape[0]
    bs = 512

    def body(x_ref, dh_ref, dout_ref, mod_ref, g_ref, gx_ref, dshift_ref, dscale_ref, dg_ref):
        @pl.when(pl.program_id(0) == 0)
        def _():
            dshift_ref[...] = jnp.zeros_like(dshift_ref)
            dscale_ref[...] = jnp.zeros_like(dscale_ref)
            dg_ref[...] = jnp.zeros_like(dg_ref)

        xv = x_ref[...]
        dh = dh_ref[...]
        g = g_ref[...]
        r = lax.rsqrt(jnp.mean(xv * xv, axis=-1, keepdims=True) + EPS)
        xn = xv * r
        scale = mod_ref[:, D_MODEL:2 * D_MODEL]
        dshift_ref[...] += jnp.sum(dh, axis=0, keepdims=True)
        dscale_ref[...] += jnp.sum(dh * (xn * g), axis=0, keepdims=True)
        da = dh * (1.0 + scale)
        dg_ref[...] += jnp.sum(da * xn, axis=0, keepdims=True)
        dxn = da * g
        dx = r * (dxn - xn * jnp.mean(dxn * xn, axis=-1, keepdims=True))
        gx_ref[...] = dout_ref[...] + dx

    row = pl.BlockSpec((bs, D_MODEL), lambda i: (i, 0))
    vec = pl.BlockSpec((1, D_MODEL), lambda i: (0, 0))
    return pl.pallas_call(
        body, grid=(seq // bs,), name="prenorm_bwd",
        in_specs=[row, row, row, pl.BlockSpec((1, 3 * D_MODEL), lambda i: (0, 0)), vec],
        out_specs=[row, vec, vec, vec],
        out_shape=[jax.ShapeDtypeStruct((seq, D_MODEL), F32)] + [jax.ShapeDtypeStruct((1, D_MODEL), F32)] * 3,
        compiler_params=_cp("arbitrary"))(x, dh, dout, mod, g_pre)


def _dep_args(dep, grid_rank):
    if dep is None:
        return [], []
    return [dep], [pl.BlockSpec(dep.shape, lambda *_: (0,) * dep.ndim)]


def _matmul(a, b, *, ta, tb, tm, tn, out_dtype, name, dep=None):
    kdim = a.shape[0] if ta else a.shape[1]
    m = a.shape[1] if ta else a.shape[0]
    n = b.shape[0] if tb else b.shape[1]
    assert m % tm == 0 and n % tn == 0
    deps, dep_specs = _dep_args(dep, 2)

    def body(a_ref, b_ref, *rest):
        o_ref = rest[-1]
        dims = (((0 if ta else 1,), (1 if tb else 0,)), ((), ()))
        o_ref[...] = lax.dot_general(a_ref[...], b_ref[...], dims, preferred_element_type=F32).astype(o_ref.dtype)

    a_spec = pl.BlockSpec((kdim, tm), lambda j, i: (0, i)) if ta else pl.BlockSpec((tm, kdim), lambda j, i: (i, 0))
    b_spec = pl.BlockSpec((tn, kdim), lambda j, i: (j, 0)) if tb else pl.BlockSpec((kdim, tn), lambda j, i: (0, j))
    return pl.pallas_call(
        body, grid=(n // tn, m // tm), name=name, in_specs=[a_spec, b_spec] + dep_specs,
        out_specs=pl.BlockSpec((tm, tn), lambda j, i: (i, j)),
        out_shape=jax.ShapeDtypeStruct((m, n), out_dtype), compiler_params=_cp("parallel", "parallel"))(a, b, *deps)


def _prep(p, cos, sin, qg2, kg2):
    seq = p.shape[0]
    bs = 512

    def body(p_ref, cos_ref, sin_ref, qg_ref, kg_ref, qt_ref, kd_ref, vd_ref, rq_ref, rk_ref, rv_ref):
        m = _blockdiag64()
        cs, sn = cos_ref[...], sin_ref[...]
        lo = _lane((bs, 128)) < 64
        for pr in range(4):
            q = p_ref[:, QA + 128 * pr:QA + 128 * (pr + 1)]
            r = lax.rsqrt(_seg64(q * q, m) * (1.0 / 64) + EPS)
            qt_ref[:, 128 * pr:128 * (pr + 1)] = (_rope(q * r * qg_ref[...], cs, sn) * (0.125 * LOG2E)).astype(qt_ref.dtype)
        k = p_ref[:, KA:KA + 128]
        r = lax.rsqrt(_seg64(k * k, m) * (1.0 / 64) + EPS)
        kr = _rope(k * r * kg_ref[...], cs, sn)
        ksw = pltpu.roll(kr, 64, 1)
        kd_ref[0] = jnp.where(lo, kr, ksw).astype(kd_ref.dtype)
        kd_ref[1] = jnp.where(lo, ksw, kr).astype(kd_ref.dtype)
        v = p_ref[:, VA:VA + 128]
        vsw = pltpu.roll(v, 64, 1)
        vd_ref[0] = jnp.where(lo, v, vsw).astype(vd_ref.dtype)
        vd_ref[1] = jnp.where(lo, vsw, v).astype(vd_ref.dtype)
        for pr in range(2):
            sl = slice(128 * pr, 128 * (pr + 1))
            rq_ref[:, sl] = _rope(p_ref[:, QR + 128 * pr:QR + 128 * (pr + 1)], cs, sn).astype(rq_ref.dtype)
            rk_ref[:, sl] = (_rope(p_ref[:, KR + 128 * pr:KR + 128 * (pr + 1)], cs, sn) * 0.125).astype(rk_ref.dtype)
        rv_ref[...] = p_ref[:, VR:VR + 512].astype(rv_ref.dtype)

    tab = pl.BlockSpec((bs, 128), lambda i: (i, 0))
    gsp = pl.BlockSpec((1, 128), lambda i: (0, 0))
    dup = pl.BlockSpec((2, bs, 128), lambda i: (0, i, 0))
    return pl.pallas_call(
        body, grid=(seq // bs,), name="mixer_prep",
        in_specs=[pl.BlockSpec((bs, ZR), lambda i: (i, 0)), tab, tab, gsp, gsp],
        out_specs=[pl.BlockSpec((bs, 512), lambda i: (i, 0)), dup, dup, pl.BlockSpec((bs, 256), lambda i: (i, 0)),
                   pl.BlockSpec((bs, 256), lambda i: (i, 0)), pl.BlockSpec((bs, 512), lambda i: (i, 0))],
        out_shape=[jax.ShapeDtypeStruct((seq, 512), MXU_DTYPE), jax.ShapeDtypeStruct((2, seq, 128), MXU_DTYPE),
                   jax.ShapeDtypeStruct((2, seq, 128), MXU_DTYPE), jax.ShapeDtypeStruct((seq, 256), MXU_DTYPE),
                   jax.ShapeDtypeStruct((seq, 256), MXU_DTYPE), jax.ShapeDtypeStruct((seq, 512), MXU_DTYPE)],
        compiler_params=_cp("parallel"))(p, cos, sin, qg2, kg2)


def _halves(kd):
    lo = _lane(kd.shape) < 64
    z = jnp.zeros_like(kd)
    return jnp.concatenate([jnp.where(lo, kd, z), jnp.where(lo, z, kd)], axis=0)


def _attn_fwd(qt, kd, vd):
    seq = qt.shape[0]
    bq = bk = 512
    nk = seq // bk

    def body(q_ref, k_ref, v_ref, o_ref, lse_ref, m_scr, l_scr, acc):
        j = pl.program_id(1)
        m_scr[...] = jnp.full_like(m_scr, -jnp.inf)
        l_scr[...] = jnp.zeros_like(l_scr)
        acc[...] = jnp.zeros_like(acc)
        lo = _lane((bq, 128)) < 64

        def kv_block(n, carry):
            rows = pl.ds(pl.multiple_of(n * bk, bk), bk)
            k2, v2 = _halves(k_ref[rows, :]), _halves(v_ref[rows, :])
            for pr in range(2):
                s2 = _nt(q_ref[:, 128 * pr:128 * (pr + 1)], k2)
                ps, alphas = [], []
                for e in range(2):
                    g = 2 * pr + e
                    s = s2[:, e * bk:(e + 1) * bk]
                    m_prev = m_scr[g]
                    m_next = jnp.maximum(m_prev, jnp.max(s, axis=1, keepdims=True))
                    alpha = jnp.exp2(m_prev - m_next)
                    pe = jnp.exp2(s - jnp.tile(m_next, (1, bk // 128)))
                    l_scr[g] = alpha * l_scr[g] + jnp.sum(pe, axis=1, keepdims=True)
                    m_scr[g] = m_next
                    ps.append(_mx(pe))
                    alphas.append(alpha)
                o2 = _nn(jnp.concatenate(ps, axis=1), v2)
                sl = slice(128 * pr, 128 * (pr + 1))
                acc[:, sl] = acc[:, sl] * jnp.where(lo, alphas[0], alphas[1]) + o2
            return carry

        lax.fori_loop(0, nk, kv_block, 0)
        for pr in range(2):
            sl = slice(128 * pr, 128 * (pr + 1))
            o_ref[:, sl] = acc[:, sl] / jnp.where(lo, l_scr[2 * pr], l_scr[2 * pr + 1])
        for g in range(4):
            lse = jnp.transpose(m_scr[g] + jnp.log2(l_scr[g]))
            lse_ref[pl.ds(4 * j + g, 1), :] = lse[0:1, :]

    return pl.pallas_call(
        body, grid=(seq // bq, 2), name="attn_fwd",
        in_specs=[pl.BlockSpec((bq, 256), lambda i, j: (i, j)), pl.BlockSpec((None, seq, 128), lambda i, j: (j, 0, 0)),
                  pl.BlockSpec((None, seq, 128), lambda i, j: (j, 0, 0))],
        out_specs=[pl.BlockSpec((bq, 256), lambda i, j: (i, j)), pl.BlockSpec((8, bq), lambda i, j: (0, i))],
        out_shape=[jax.ShapeDtypeStruct((seq, 512), F32), jax.ShapeDtypeStruct((8, seq), F32)],
        scratch_shapes=[pltpu.VMEM((4, bq, 128), F32), pltpu.VMEM((4, bq, 128), F32), pltpu.VMEM((bq, 256), F32)],
        compiler_params=_cp("parallel", "arbitrary"))(qt, kd, vd)


def _attn_bwd(qt, kd, vd, do, lse, delta, dep=None):
    seq = qt.shape[0]
    bq = bk = 512
    nq, nk = seq // bq, seq // bk
    deps, dep_specs = _dep_args(dep, 3)

    def body(q_ref, do_ref, k_ref, v_ref, lse_ref, del_ref, *rest):
        dq_ref, dk_ref, dv_ref = rest[-3:]
        j, i = pl.program_id(0), pl.program_id(1)
        dq_ref[...] = jnp.zeros_like(dq_ref)

        @pl.when(i == 0)
        def _():
            dk_ref[...] = jnp.zeros_like(dk_ref)
            dv_ref[...] = jnp.zeros_like(dv_ref)

        lo = _lane((bk, 128)) < 64
        big = lambda r: jnp.concatenate([jnp.broadcast_to(r[0], (bk, bq)), jnp.broadcast_to(r[1], (bk, bq))], axis=0)

        def kv_block(n, carry):
            rows = pl.ds(pl.multiple_of(n * bk, bk), bk)
            k2, v2 = _halves(k_ref[rows, :]), _halves(v_ref[rows, :])
            for pr in range(2):
                sl = slice(128 * pr, 128 * (pr + 1))
                qp, dop = q_ref[:, sl], do_ref[:, sl]
                s_t = _nt(k2, qp)
                dp_t = _nt(v2, dop)
                ls = [lse_ref[pl.ds(4 * j + 2 * pr + e, 1), :] for e in range(2)]
                dl = [del_ref[pl.ds(4 * j + 2 * pr + e, 1), :] for e in range(2)]
                p_t = jnp.exp2(s_t - big(ls))
                ds_t = _mx(p_t * (dp_t - big(dl)))
                rv = _nn(_mx(p_t), dop)
                rk = _nn(ds_t, qp) * LN2
                dv_ref[rows, :] += jnp.where(lo, rv[:bk], rv[bk:])
                dk_ref[rows, :] += jnp.where(lo, rk[:bk], rk[bk:])
                dq_ref[:, sl] += _tn(ds_t, k2)
            return carry

        lax.fori_loop(0, nk, kv_block, 0)

    return pl.pallas_call(
        body, grid=(2, nq), name="attn_bwd",
        in_specs=[pl.BlockSpec((bq, 256), lambda j, i: (i, j)), pl.BlockSpec((bq, 256), lambda j, i: (i, j)),
                  pl.BlockSpec((None, seq, 128), lambda j, i: (j, 0, 0)), pl.BlockSpec((None, seq, 128), lambda j, i: (j, 0, 0)),
                  pl.BlockSpec((8, bq), lambda j, i: (0, i)), pl.BlockSpec((8, bq), lambda j, i: (0, i))] + dep_specs,
        out_specs=[pl.BlockSpec((bq, 256), lambda j, i: (i, j)), pl.BlockSpec((None, seq, 128), lambda j, i: (j, 0, 0)),
                   pl.BlockSpec((None, seq, 128), lambda j, i: (j, 0, 0))],
        out_shape=[jax.ShapeDtypeStruct((seq, 512), F32), jax.ShapeDtypeStruct((2, seq, 128), F32),
                   jax.ShapeDtypeStruct((2, seq, 128), F32)],
        compiler_params=_cp("arbitrary", "arbitrary"))(qt, do, kd, vd, lse, delta, *deps)


def _ret_tables(lg_f, lg_b, c):
    idx = jnp.arange(c, dtype=F32)
    diff = idx[:, None] - idx[None, :]
    a, b = lg_f[:, None, None], lg_b[:, None, None]
    low, up = (diff >= 0)[None], (diff < 0)[None]
    dmat = jnp.where(low, jnp.exp(a * jnp.maximum(diff, 0.0)[None]), jnp.exp(b * jnp.maximum(-diff, 0.0)[None]))
    dda = jnp.where(low, diff[None] * jnp.exp(a * jnp.maximum(diff, 0.0)[None]), 0.0)
    ddb = jnp.where(up, -diff[None] * jnp.exp(b * jnp.maximum(-diff, 0.0)[None]), 0.0)
    rep = lambda w: jnp.broadcast_to(w[:, :, None], (RET_HEADS, c, 128))
    wqf = rep(jnp.exp(lg_f[:, None] * (idx + 1.0)[None]))
    wqb = rep(jnp.exp(lg_b[:, None] * (c - idx)[None]))
    wkf = rep(jnp.exp(lg_f[:, None] * (c - 1.0 - idx)[None]))
    wkb = rep(jnp.exp(lg_b[:, None] * idx[None]))
    cdf, cdb = jnp.exp(lg_f * c), jnp.exp(lg_b * c)
    dec_fb = jnp.broadcast_to(jnp.concatenate([cdf, cdb])[:, None], (8, 128))
    dec_bf = jnp.broadcast_to(jnp.concatenate([cdb, cdf])[:, None], (8, 128))
    return dict(dmat=dmat, dda=dda, ddb=ddb, wqf=wqf, wqb=wqb, wkf=wkf, wkb=wkb, dec_fb=dec_fb, dec_bf=dec_bf)


def _ret_states(xk, yv, w_fwd, w_rev, dec, name):
    seq = xk.shape[0]
    c = RET_CHUNK
    nc = seq // c

    def body(xf_ref, yf_ref, xr_ref, yr_ref, wf_ref, wr_ref, dec_ref, sf_ref, sr_ref, st_f, st_r):
        @pl.when(pl.program_id(0) == 0)
        def _():
            st_f[...] = jnp.zeros_like(st_f)
            st_r[...] = jnp.zeros_like(st_r)

        lane = _lane((c, 128))
        for h in range(RET_HEADS):
            pr, e = h // 2, h % 2
            half = (lane < 64) if e == 0 else (lane >= 64)
            for x_ref, y_ref, w_ref, s_ref, st, drow in ((xf_ref, yf_ref, wf_ref, sf_ref, st_f, h),
                                                        (xr_ref, yr_ref, wr_ref, sr_ref, st_r, 4 + h)):
                s_ref[h] = st[h].astype(s_ref.dtype)
                xm = jnp.where(half, x_ref[:, 128 * pr:128 * (pr + 1)].astype(F32) * w_ref[h], 0.0)
                st[h] = st[h] * dec_ref[drow:drow + 1, :] + _tn(_mx(xm), _mx(y_ref[:, 128 * h:128 * (h + 1)]))

    xs = lambda f: pl.BlockSpec((c, 256), f)
    ys = lambda f: pl.BlockSpec((c, 512), f)
    fwd, rev = (lambda n: (n, 0)), (lambda n: (nc - 1 - n, 0))
    wsp = pl.BlockSpec((RET_HEADS, c, 128), lambda n: (0, 0, 0))
    return pl.pallas_call(
        body, grid=(nc,), name=name,
        in_specs=[xs(fwd), ys(fwd), xs(rev), ys(rev), wsp, wsp, pl.BlockSpec((8, 128), lambda n: (0, 0))],
        out_specs=[pl.BlockSpec((None, RET_HEADS, 128, 128), lambda n: (n, 0, 0, 0)),
                   pl.BlockSpec((None, RET_HEADS, 128, 128), lambda n: (nc - 1 - n, 0, 0, 0))],
        out_shape=[jax.ShapeDtypeStruct((nc, RET_HEADS, 128, 128), MXU_DTYPE)] * 2,
        scratch_shapes=[pltpu.VMEM((RET_HEADS, 128, 128), F32), pltpu.VMEM((RET_HEADS, 128, 128), F32)],
        compiler_params=_cp("arbitrary"))(xk, yv, xk, yv, w_fwd, w_rev, dec)


def _ret_fwd(rq, rk, rv, rf, rb, tb):
    seq = rq.shape[0]
    c = RET_CHUNK

    def body(q_ref, k_ref, v_ref, rf_ref, rb_ref, d_ref, wqf_ref, wqb_ref, o_ref):
        lane = _lane((c, 128))
        for h in range(RET_HEADS):
            pr, e = h // 2, h % 2
            half = (lane < 64) if e == 0 else (lane >= 64)
            sl = slice(128 * pr, 128 * (pr + 1))
            qp = q_ref[:, sl]
            km = jnp.where(half, k_ref[:, sl], jnp.zeros_like(k_ref[:, sl]))
            sd = _mx(_nt(qp, km) * d_ref[h])
            qf = qp.astype(F32)
            o = _nn(sd, v_ref[:, 128 * h:128 * (h + 1)])
            o = o + _nn(_mx(qf * wqf_ref[h]), rf_ref[h]) + _nn(_mx(qf * wqb_ref[h]), rb_ref[h])
            o_ref[:, 128 * h:128 * (h + 1)] = o

    st = pl.BlockSpec((None, RET_HEADS, 128, 128), lambda n: (n, 0, 0, 0))
    wsp = pl.BlockSpec((RET_HEADS, c, 128), lambda n: (0, 0, 0))
    return pl.pallas_call(
        body, grid=(seq // c,), name="ret_fwd",
        in_specs=[pl.BlockSpec((c, 256), lambda n: (n, 0)), pl.BlockSpec((c, 256), lambda n: (n, 0)),
                  pl.BlockSpec((c, 512), lambda n: (n, 0)), st, st, pl.BlockSpec((RET_HEADS, c, c), lambda n: (0, 0, 0)), wsp, wsp],
        out_specs=pl.BlockSpec((c, 512), lambda n: (n, 0)),
        out_shape=jax.ShapeDtypeStruct((seq, 512), F32), compiler_params=_cp("parallel"))(
            rq, rk, rv, rf, rb, tb["dmat"], tb["wqf"], tb["wqb"])


def _ret_bwd(rq, rk, rv, do, rf, rb, hf, hb, tb):
    seq = rq.shape[0]
    c = RET_CHUNK

    def body(q_ref, k_ref, v_ref, do_ref, rf_ref, rb_ref, hf_ref, hb_ref, d_ref, dda_ref, ddb_ref,
             wqf_ref, wqb_ref, wkf_ref, wkb_ref, cdec_ref, dq_ref, dk_ref, dv_ref, dlg_ref):
        @pl.when(pl.program_id(0) == 0)
        def _():
            dlg_ref[...] = jnp.zeros_like(dlg_ref)

        lane = _lane((c, 128))
        pos = lax.broadcasted_iota(jnp.int32, (c, 128), 0).astype(F32)
        for pr in range(2):
            sl = slice(128 * pr, 128 * (pr + 1))
            qp, kp = q_ref[:, sl], k_ref[:, sl]
            qf, kf = qp.astype(F32), kp.astype(F32)
            dq_acc = jnp.zeros((c, 128), F32)
            dk_acc = jnp.zeros((c, 128), F32)
            for e in range(2):
                h = 2 * pr + e
                half = (lane < 64) if e == 0 else (lane >= 64)
                hs = slice(128 * h, 128 * (h + 1))
                km = jnp.where(half, kp, jnp.zeros_like(kp))
                kmf = km.astype(F32)
                vh, doh = v_ref[:, hs], do_ref[:, hs]
                rfh, rbh, hfh, hbh = rf_ref[h], rb_ref[h], hf_ref[h], hb_ref[h]
                s = _nt(qp, km)
                dpm = _nt(doh, vh)
                ds_b = _mx(dpm * d_ref[h])
                sd_b = _mx(s * d_ref[h])
                dq_if = wqf_ref[h] * _nt(doh, rfh)
                dq_ib = wqb_ref[h] * _nt(doh, rbh)
                dq_acc = dq_acc + _nn(ds_b, km) + dq_if + dq_ib
                dk_sf = wkf_ref[h] * _nt(vh, hfh)
                dk_sb = wkb_ref[h] * _nt(vh, hbh)
                dk_acc = dk_acc + jnp.where(half, _tn(ds_b, qp), 0.0) + dk_sf + dk_sb
                dv = _tn(sd_b, doh) + _nn(_mx(kmf * wkf_ref[h]), hfh) + _nn(_mx(kmf * wkb_ref[h]), hbh)
                dv_ref[:, hs] = dv.astype(dv_ref.dtype)
                sdp = s * dpm
                hr_f = hfh.astype(F32) * rfh.astype(F32)
                hr_b = hbh.astype(F32) * rbh.astype(F32)
                da = (_rowsum128(sdp * dda_ref[h]) + _rowsum128((pos + 1.0) * qf * dq_if)
                      + _rowsum128((c - 1.0 - pos) * kf * dk_sf) + cdec_ref[h:h + 1, :] * _rowsum128(hr_f))
                db = (_rowsum128(sdp * ddb_ref[h]) + _rowsum128((c - pos) * qf * dq_ib)
                      + _rowsum128(pos * kf * dk_sb) + cdec_ref[4 + h:5 + h, :] * _rowsum128(hr_b))
                dlg_ref[h:h + 1, :] += da
                dlg_ref[4 + h:5 + h, :] += db
            dq_ref[:, sl] = dq_acc
            dk_ref[:, sl] = dk_acc

    st = pl.BlockSpec((None, RET_HEADS, 128, 128), lambda n: (n, 0, 0, 0))
    wsp = pl.BlockSpec((RET_HEADS, c, 128), lambda n: (0, 0, 0))
    dsp = pl.BlockSpec((RET_HEADS, c, c), lambda n: (0, 0, 0))
    x256 = pl.BlockSpec((c, 256), lambda n: (n, 0))
    x512 = pl.BlockSpec((c, 512), lambda n: (n, 0))
    cdec = tb["dec_fb"] * float(c)
    return pl.pallas_call(
        body, grid=(seq // c,), name="ret_bwd",
        in_specs=[x256, x256, x512, x512, st, st, st, st, dsp, dsp, dsp, wsp, wsp, wsp, wsp,
                  pl.BlockSpec((8, 128), lambda n: (0, 0))],
        out_specs=[x256, x256, x512, pl.BlockSpec((8, 128), lambda n: (0, 0))],
        out_shape=[jax.ShapeDtypeStruct((seq, 256), F32), jax.ShapeDtypeStruct((seq, 256), F32),
                   jax.ShapeDtypeStruct((seq, 512), MXU_DTYPE), jax.ShapeDtypeStruct((8, 128), F32)],
        compiler_params=_cp("arbitrary"))(
            rq, rk, rv, do, rf, rb, hf, hb, tb["dmat"], tb["dda"], tb["ddb"], tb["wqf"], tb["wqb"], tb["wkf"], tb["wkb"], cdec)


def _tail(x, tgt, o_att, o_ret, p, mod, g_post, gn_g, wpt, wout):
    seq = x.shape[0]
    bs = 256
    nb = seq // bs

    def body(x_ref, t_ref, oa_ref, or_ref, za_ref, zr_ref, gl_ref, mod_ref, gp_ref, gn_ref, wpt_ref, wout_ref,
             dout_ref, dza_ref, dzr_ref, dgl_ref, doa_ref, dor_ref, del_ref, gout_ref, gpt_ref,
             dgate_ref, dgpost_ref, dgn_ref, loss_ref, gout_acc, gpt_acc):
        i = pl.program_id(0)

        @pl.when(i == 0)
        def _():
            gout_acc[...] = jnp.zeros_like(gout_acc)
            gpt_acc[...] = jnp.zeros_like(gpt_acc)
            dgate_ref[...] = jnp.zeros_like(dgate_ref)
            dgpost_ref[...] = jnp.zeros_like(dgpost_ref)
            dgn_ref[...] = jnp.zeros_like(dgn_ref)
            loss_ref[...] = jnp.zeros_like(loss_ref)

        oa, za, zr = oa_ref[...], za_ref[...], zr_ref[...]
        sga, sgr = _sigmoid(za), _sigmoid(zr)
        sza, szr = za * sga, zr * sgr
        ya_b = _mx(oa * sza)
        ons, rstds = [], []
        for h in range(RET_HEADS):
            oh = or_ref[:, 128 * h:128 * (h + 1)]
            xc = oh - jnp.mean(oh, axis=-1, keepdims=True)
            rstd = lax.rsqrt(jnp.mean(xc * xc, axis=-1, keepdims=True) + EPS)
            ons.append(xc * rstd)
            rstds.append(rstd)
        on = jnp.concatenate(ons, axis=1)
        yn = on * gn_ref[...]
        yr_b = _mx(yn * szr)
        wpa_t, wpr_t = wpt_ref[:, 0:512], wpt_ref[:, 512:1024]
        a_att = _nt(ya_b, wpa_t)
        a_ret = _nt(yr_b, wpr_t)
        gts = _sigmoid(gl_ref[...])
        g_att, g_ret = gts[:, 0:D_MODEL], gts[:, D_MODEL:2 * D_MODEL]
        merged_b = _mx(g_att * a_att + g_ret * a_ret)
        u = _nn(merged_b, wout_ref[...])
        r = lax.rsqrt(jnp.mean(u * u, axis=-1, keepdims=True) + EPS)
        un = u * r
        gpost = gp_ref[...]
        y = un * gpost
        gate = mod_ref[:, 2 * D_MODEL:3 * D_MODEL]
        err = x_ref[...] + gate * y - t_ref[...]
        loss_ref[...] += (0.5 / D_MODEL) * _rowsum128(err * err)
        dout = err * (1.0 / D_MODEL)
        dout_ref[...] = dout
        dgate_ref[...] += jnp.sum(dout * y, axis=0, keepdims=True)
        dy = dout * gate
        dgpost_ref[...] += jnp.sum(dy * un, axis=0, keepdims=True)
        dun = dy * gpost
        du_b = _mx(r * (dun - un * jnp.mean(dun * un, axis=-1, keepdims=True)))
        dmerged = _nt(du_b, wout_ref[...])
        gout_acc[...] += _tn(merged_b, du_b)
        d_att, d_ret = dmerged * g_att, dmerged * g_ret
        dgl_ref[:, 0:D_MODEL] = (d_att * a_att * (1.0 - g_att)).astype(dgl_ref.dtype)
        dgl_ref[:, D_MODEL:2 * D_MODEL] = (d_ret * a_ret * (1.0 - g_ret)).astype(dgl_ref.dtype)
        d_att_b, d_ret_b = _mx(d_att), _mx(d_ret)
        dya = _nn(d_att_b, wpa_t)
        dyr = _nn(d_ret_b, wpr_t)
        gpt_acc[:, 0:512] += _tn(d_att_b, ya_b)
        gpt_acc[:, 512:1024] += _tn(d_ret_b, yr_b)
        dza_ref[...] = (dya * oa * (sga * (1.0 + za * (1.0 - sga)))).astype(dza_ref.dtype)
        doa = dya * sza
        doa_ref[...] = doa.astype(doa_ref.dtype)
        dt = jnp.transpose(doa * oa)
        del_ref[...] = jnp.sum(dt.reshape(8, HEAD_DIM, bs), axis=1)
        dzr_ref[...] = (dyr * yn * (sgr * (1.0 + zr * (1.0 - sgr)))).astype(dzr_ref.dtype)
        dyn = dyr * szr
        dgn_ref[...] += jnp.sum(dyn * on, axis=0, keepdims=True)
        don = dyn * gn_ref[...]
        for h in range(RET_HEADS):
            hs = slice(128 * h, 128 * (h + 1))
            dh, oh = don[:, hs], ons[h]
            doh = rstds[h] * (dh - jnp.mean(dh, axis=-1, keepdims=True) - oh * jnp.mean(dh * oh, axis=-1, keepdims=True))
            dor_ref[:, hs] = doh.astype(dor_ref.dtype)

        @pl.when(i == nb - 1)
        def _():
            gout_ref[...] = gout_acc[...].astype(gout_ref.dtype)
            gpt_ref[...] = gpt_acc[...].astype(gpt_ref.dtype)

    row = lambda w: pl.BlockSpec((bs, w), lambda i: (i, 0))
    el = lambda w, off: pl.BlockSpec((pl.Element(bs), pl.Element(w)), lambda i: (i * bs, off))
    vec = lambda w: pl.BlockSpec((1, w), lambda i: (0, 0))
    full = pl.BlockSpec((D_MODEL, D_MODEL), lambda i: (0, 0))
    sds = jax.ShapeDtypeStruct
    return pl.pallas_call(
        body, grid=(nb,), name="tail",
        in_specs=[row(D_MODEL), row(D_MODEL), row(512), row(512), el(512, ZA), el(512, ZR), el(2 * D_MODEL, GL),
                  vec(3 * D_MODEL), vec(D_MODEL), vec(512), full, full],
        out_specs=[row(D_MODEL), row(512), row(512), row(2 * D_MODEL), row(512), row(512),
                   pl.BlockSpec((8, bs), lambda i: (0, i)), full, full, vec(D_MODEL), vec(D_MODEL), vec(512), vec(128)],
        out_shape=[sds((seq, D_MODEL), F32), sds((seq, 512), MXU_DTYPE), sds((seq, 512), MXU_DTYPE),
                   sds((seq, 2 * D_MODEL), MXU_DTYPE), sds((seq, 512), MXU_DTYPE), sds((seq, 512), MXU_DTYPE),
                   sds((8, seq), F32), sds((D_MODEL, D_MODEL), MXU_DTYPE), sds((D_MODEL, D_MODEL), MXU_DTYPE),
                   sds((1, D_MODEL), F32), sds((1, D_MODEL), F32), sds((1, 512), F32), sds((1, 128), F32)],
        scratch_shapes=[pltpu.VMEM((D_MODEL, D_MODEL), F32), pltpu.VMEM((D_MODEL, D_MODEL), F32)],
        compiler_params=_cp("arbitrary"))(x, tgt, o_att, o_ret, p, p, p, mod, g_post, gn_g, wpt, wout)


def _assemble(p, cos, sin, qg2, kg2, dqt, dkp, dvp, dza, drq, drk, drv, dzr, dgl):
    seq = p.shape[0]
    bs = 512

    def body(p_ref, cos_ref, sin_ref, qg_ref, kg_ref, dqt_ref, dkp_ref, dvp_ref, dza_ref, drq_ref, drk_ref, drv_ref,
             dzr_ref, dgl_ref, dp_ref, dqg_ref, dkg_ref):
        @pl.when(pl.program_id(0) == 0)
        def _():
            dqg_ref[...] = jnp.zeros_like(dqg_ref)
            dkg_ref[...] = jnp.zeros_like(dkg_ref)

        m = _blockdiag64()
        cs, sn = cos_ref[...], sin_ref[...]
        lo = _lane((bs, 128)) < 64

        def norm_bwd(raw, dn, g, dg_ref):
            r = lax.rsqrt(_seg64(raw * raw, m) * (1.0 / 64) + EPS)
            xn = raw * r
            dg_ref[...] += jnp.sum(dn * xn, axis=0, keepdims=True)
            dxn = dn * g
            return r * (dxn - xn * (_seg64(dxn * xn, m) * (1.0 / 64)))

        for pr in range(4):
            sl = slice(128 * pr, 128 * (pr + 1))
            dn = _rope_t(dqt_ref[:, sl] * 0.125, cs, sn)
            dp_ref[:, QA + 128 * pr:QA + 128 * (pr + 1)] = norm_bwd(p_ref[:, sl], dn, qg_ref[...], dqg_ref).astype(dp_ref.dtype)
        fold = lambda a: a + pltpu.roll(a, 64, 1)
        dk = jnp.where(lo, fold(dkp_ref[0]), fold(dkp_ref[1]))
        dp_ref[:, KA:KA + 128] = norm_bwd(p_ref[:, KA:KA + 128], _rope_t(dk, cs, sn), kg_ref[...], dkg_ref).astype(dp_ref.dtype)
        dp_ref[:, VA:VA + 128] = jnp.where(lo, fold(dvp_ref[0]), fold(dvp_ref[1])).astype(dp_ref.dtype)
        dp_ref[:, ZA:ZA + 512] = dza_ref[...]
        for pr in range(2):
            sl = slice(128 * pr, 128 * (pr + 1))
            dp_ref[:, QR + 128 * pr:QR + 128 * (pr + 1)] = _rope_t(drq_ref[:, sl], cs, sn).astype(dp_ref.dtype)
            dp_ref[:, KR + 128 * pr:KR + 128 * (pr + 1)] = _rope_t(drk_ref[:, sl] * 0.125, cs, sn).astype(dp_ref.dtype)
        dp_ref[:, VR:VR + 512] = drv_ref[...]
        dp_ref[:, ZR:ZR + 512] = dzr_ref[...]
        dp_ref[:, GL:GL + 2 * D_MODEL] = dgl_ref[...]

    row = lambda w: pl.BlockSpec((bs, w), lambda i: (i, 0))
    gsp = pl.BlockSpec((1, 128), lambda i: (0, 0))
    dup = pl.BlockSpec((2, bs, 128), lambda i: (0, i, 0))
    return pl.pallas_call(
        body, grid=(seq // bs,), name="assemble_dp",
        in_specs=[row(768), row(128), row(128), gsp, gsp, row(512), dup, dup, row(512), row(256), row(256), row(512),
                  row(512), row(2 * D_MODEL)],
        out_specs=[row(IN_WIDTH), gsp, gsp],
        out_shape=[jax.ShapeDtypeStruct((seq, IN_WIDTH), MXU_DTYPE), jax.ShapeDtypeStruct((1, 128), F32),
                   jax.ShapeDtypeStruct((1, 128), F32)],
        compiler_params=_cp("arbitrary"))(p, cos, sin, qg2, kg2, dqt, dkp, dvp, dza, drq, drk, drv, dzr, dgl)


def _local_step(x, tgt, mod, g_pre, qn_g, kn_g, w_dec_f, w_dec_b, gn_g, g_post, win_t, wpt, wout, send=None):
    send = send or (lambda name, arrays: None)
    seq = x.shape[0]
    cos, sin = _rope_tables(seq)
    qg2, kg2 = jnp.tile(qn_g, (1, 2)), jnp.tile(kn_g, (1, 2))
    lg_f = jax.nn.log_sigmoid(w_dec_f[0])
    lg_b = jax.nn.log_sigmoid(w_dec_b[0])
    tb = _ret_tables(lg_f, lg_b, RET_CHUNK)

    h = _prenorm(x, mod, g_pre)
    p = _matmul(h, win_t, ta=False, tb=True, tm=512, tn=IN_WIDTH // 2, out_dtype=F32, name="in_proj")
    qt, kd, vd, rq, rk, rv = _prep(p, cos, sin, qg2, kg2)
    o_att, lse = _attn_fwd(qt, kd, vd)
    rf, rb = _ret_states(rk, rv, tb["wkf"], tb["wkb"], tb["dec_fb"], "ret_states_fwd")
    o_ret = _ret_fwd(rq, rk, rv, rf, rb, tb)
    (dout, dza, dzr, dgl, do_att, do_ret, delta, g_out, g_pt, dgate, dgpost, dgn, loss_l) = _tail(
        x, tgt, o_att, o_ret, p, mod, g_post, gn_g, wpt, wout)
    dep = send("early", (g_pt, g_out))
    dqt, dkp, dvp = _attn_bwd(qt, kd, vd, do_att, lse, delta, dep=dep)
    hb, hf = _ret_states(rq, do_ret, tb["wqb"], tb["wqf"], tb["dec_bf"], "ret_states_bwd")
    drq, drk, drv, dlg = _ret_bwd(rq, rk, rv, do_ret, rf, rb, hf, hb, tb)
    dp, dqg, dkg = _assemble(p, cos, sin, qg2, kg2, dqt, dkp, dvp, dza, drq, drk, drv, dzr, dgl)
    g_in_t = _matmul(dp, h, ta=True, tb=False, tm=256, tn=D_MODEL, out_dtype=MXU_DTYPE, name="in_proj_dw")
    dep = send("late", (g_in_t,))
    dh = _matmul(dp, win_t, ta=False, tb=False, tm=512, tn=D_MODEL, out_dtype=F32, name="in_proj_dx", dep=dep)
    grad_x, dshift, dscale, dgpre = _prenorm_bwd(x, dh, dout, mod, g_pre)

    dlg = jnp.sum(dlg, axis=1)
    small = dict(
        dmod=jnp.concatenate([dshift, dscale, dgate], axis=1),
        g_pre=dgpre, g_post=dgpost, gn_g=dgn,
        qn_g=dqg[:, :64] + dqg[:, 64:], kn_g=dkg[:, :64] + dkg[:, 64:],
        w_dec_f=(dlg[0:4] * jax.nn.sigmoid(-w_dec_f[0]))[None], w_dec_b=(dlg[4:8] * jax.nn.sigmoid(-w_dec_b[0]))[None],
        loss=jnp.sum(loss_l, axis=1, keepdims=True))
    return grad_x, g_in_t, g_pt, g_out, small


N_DEV = 8
N_CHIP = 4
HBM_SPEC = pl.BlockSpec(memory_space=pl.ANY)
VMEM_SPEC = pl.BlockSpec(memory_space=pltpu.VMEM)
SHARD_ROWS = (IN_WIDTH // N_CHIP, D_MODEL // N_CHIP, D_MODEL // N_CHIP)


def _coords():
    return lax.axis_index("x"), lax.axis_index("y"), lax.axis_index("c")


def _peer(k):
    x, y, c = _coords()
    return (1 - x if k & 4 else x, 1 - y if k & 2 else y, 1 - c if k & 1 else c)


def _dev_index(p):
    return 4 * p[0] + 2 * p[1] + p[2]


def _rows(ref, start, size):
    return ref.at[pl.ds(pl.multiple_of(start, 16), size), :]


def _remote(src, dst, ssem, rsem, dev):
    return pltpu.make_async_remote_copy(src_ref=src, dst_ref=dst, send_sem=ssem, recv_sem=rsem, device_id=dev,
                                        device_id_type=MESH)


def _mod_exchange(c, w_ada_s, b4):
    ncol = w_ada_s.shape[1]

    def body(c_ref, w_ref, b_ref, cs_ref, mod_ref, modsh, modall, ssem, rsem):
        me = _dev_index(_coords())
        chip = me // 2
        cs_ref[me] = c_ref[...]
        sends = []
        for k in range(1, N_DEV):
            cp = _remote(c_ref, cs_ref.at[me], ssem.at[k - 1], rsem.at[k - 1], _peer(k))
            cp.start()
            sends.append(cp)
        for k in range(1, N_DEV):
            slot = cs_ref.at[_dev_index(_peer(k))]
            _remote(slot, slot, ssem.at[k - 1], rsem.at[k - 1], _peer(k)).wait_recv()
        cs = jnp.concatenate([cs_ref[d] for d in range(N_DEV)], axis=0)
        ms = _nn(cs * _sigmoid(cs), w_ref[...], precision=HIGHEST) + b_ref[pl.ds(chip, 1), :]
        modsh[...] = ms
        modall[chip] = ms
        for k2 in range(1, N_CHIP):
            cp = _remote(modsh, modall.at[chip], ssem.at[6 + k2], rsem.at[6 + k2], _peer(2 * k2))
            cp.start()
            sends.append(cp)
        for k2 in range(1, N_CHIP):
            slot = modall.at[_dev_index(_peer(2 * k2)) // 2]
            _remote(slot, slot, ssem.at[6 + k2], rsem.at[6 + k2], _peer(2 * k2)).wait_recv()
        for jj in range(N_CHIP):
            mod_ref[:, ncol * jj:ncol * (jj + 1)] = modall[jj, pl.ds(me, 1), :]
        for cp in sends:
            cp.wait_send()

    return pl.pallas_call(
        body, name="mod_exchange", in_specs=[VMEM_SPEC] * 3, out_specs=[VMEM_SPEC] * 2,
        out_shape=[jax.ShapeDtypeStruct((N_DEV, 1, D_MODEL), F32), jax.ShapeDtypeStruct((1, N_CHIP * ncol), F32)],
        scratch_shapes=[pltpu.VMEM((N_DEV, ncol), F32), pltpu.VMEM((N_CHIP, N_DEV, ncol), F32),
                        pltpu.SemaphoreType.DMA((10,)), pltpu.SemaphoreType.DMA((10,))],
        compiler_params=_cp())(c, w_ada_s, b4)


GATHER_CHUNK = 32


def _chunks(nt, chunk):
    out = []
    for t in range(nt):
        half = SHARD_ROWS[t] // 2
        step = chunk if half % chunk == 0 else half
        out += [(t, off, step) for off in range(0, half, step)]
    return out


def _weight_gather(shards):
    nt = len(shards)
    pieces = _chunks(nt, GATHER_CHUNK)
    npc = len(pieces)

    def body(*refs):
        srcs, outs = refs[:nt], refs[nt:2 * nt]
        lsem, ssem, rsem = refs[2 * nt:]
        x, y, c = _coords()
        chip = 2 * x + y
        sib = (x, y, 1 - c)

        def place(t, ch, cc, off, n):
            return _rows(outs[t], ch * SHARD_ROWS[t] + cc * (SHARD_ROWS[t] // 2) + off, n)

        local = [pltpu.make_async_copy(srcs[t], _rows(outs[t], chip * SHARD_ROWS[t], SHARD_ROWS[t]), lsem.at[t]) for t in range(nt)]
        for cp in local:
            cp.start()
        sends = []
        for k2 in range(1, N_CHIP):
            for q, (t, off, n) in enumerate(pieces):
                i = (k2 - 1) * npc + q
                cp = _remote(_rows(srcs[t], c * (SHARD_ROWS[t] // 2) + off, n), place(t, chip, c, off, n),
                             ssem.at[i], rsem.at[i], _peer(2 * k2))
                cp.start()
                sends.append(cp)
        for k2 in range(1, N_CHIP):
            pchip = _dev_index(_peer(2 * k2)) // 2
            for q, (t, off, n) in enumerate(pieces):
                i = (k2 - 1) * npc + q
                got = place(t, pchip, c, off, n)
                _remote(got, got, ssem.at[i], rsem.at[i], _peer(2 * k2)).wait_recv()
                cp = _remote(got, got, ssem.at[3 * npc + i], rsem.at[3 * npc + i], sib)
                cp.start()
                sends.append(cp)
        for k2 in range(1, N_CHIP):
            pchip = _dev_index(_peer(2 * k2)) // 2
            for q, (t, off, n) in enumerate(pieces):
                i = (k2 - 1) * npc + q
                got = place(t, pchip, 1 - c, off, n)
                _remote(got, got, ssem.at[3 * npc + i], rsem.at[3 * npc + i], sib).wait_recv()
        for cp in sends:
            cp.wait_send()
        for cp in local:
            cp.wait()

    return pl.pallas_call(
        body, name="weight_gather", in_specs=[VMEM_SPEC] * nt, out_specs=[HBM_SPEC] * nt,
        out_shape=[jax.ShapeDtypeStruct((N_CHIP * s.shape[0], s.shape[1]), s.dtype) for s in shards],
        scratch_shapes=[pltpu.SemaphoreType.DMA((nt,)), pltpu.SemaphoreType.DMA((6 * npc,)), pltpu.SemaphoreType.DMA((6 * npc,))],
        compiler_params=_cp())(*shards)


SEM_SPEC = pl.BlockSpec(memory_space=pltpu.SEMAPHORE)
HBM_ONLY = pl.BlockSpec(memory_space=pltpu.HBM)
DATAFLOW = pltpu.SideEffectType.DATAFLOW_SIDE_EFFECTING


def _reduced_by(ref, t, dev):
    return _rows(ref, (dev // 2) * SHARD_ROWS[t] + (dev % 2) * (SHARD_ROWS[t] // 2), SHARD_ROWS[t] // 2)


def _scatter_start(grads, kinds, name):
    n = len(grads)

    def body(*refs):
        srcs, lands = refs[:n], refs[n:2 * n]
        ssem, rsem = refs[2 * n], refs[2 * n + 1]
        token = refs[-1]
        me = _dev_index(_coords())
        for k in range(1, N_DEV):
            for i, t in enumerate(kinds):
                q = (k - 1) * n + i
                _remote(_reduced_by(srcs[i], t, _dev_index(_peer(k))), lands[i].at[me], ssem.at[q], rsem.at[q], _peer(k)).start()
        token[...] = jnp.zeros_like(token)

    zones = [lax.empty((N_DEV, SHARD_ROWS[t] // 2, g.shape[1]), g.dtype) for g, t in zip(grads, kinds)]
    hbm = lambda a: pltpu.with_memory_space_constraint(a, pltpu.HBM)
    dma = pltpu.SemaphoreType.DMA
    outs = pl.pallas_call(
        body, name=name, in_specs=[HBM_ONLY] * (2 * n), out_specs=[SEM_SPEC, SEM_SPEC] + [HBM_ONLY] * (2 * n) + [VMEM_SPEC],
        out_shape=[dma((7 * n,)), dma((7 * n,))] + [pltpu.HBM(a.shape, a.dtype) for a in list(grads) + zones]
        + [jax.ShapeDtypeStruct((8, 128), F32)],
        input_output_aliases={i: 2 + i for i in range(2 * n)},
        compiler_params=pltpu.CompilerParams(has_side_effects=DATAFLOW))(*[hbm(a) for a in list(grads) + zones])
    return outs[0], outs[1], outs[2:2 + n], outs[2 + n:2 + 2 * n], outs[-1]


def _scatter_wait(started, kinds, after, name):
    ssem, rsem, grads, zones, _ = started
    n = len(grads)

    def body(*refs):
        srcs, lands = refs[:n], refs[n:2 * n]
        ssem_ref, rsem_ref = refs[2 * n], refs[2 * n + 1]
        for k in range(1, N_DEV):
            peer = _dev_index(_peer(k))
            for i, t in enumerate(kinds):
                q = (k - 1) * n + i
                cp = _remote(_reduced_by(srcs[i], t, peer), lands[i].at[peer], ssem_ref.at[q], rsem_ref.at[q], _peer(k))
                cp.wait_send()
                cp.wait_recv()

    outs = pl.pallas_call(
        body, name=name, in_specs=[HBM_ONLY] * (2 * n) + [SEM_SPEC, SEM_SPEC, HBM_SPEC], out_specs=[HBM_ONLY] * (2 * n),
        out_shape=[pltpu.HBM(a.shape, a.dtype) for a in list(grads) + list(zones)],
        input_output_aliases={i: i for i in range(2 * n)},
        compiler_params=pltpu.CompilerParams(has_side_effects=DATAFLOW))(*grads, *zones, ssem, rsem, after)
    return outs[:n], outs[n:]


def _grad_finish(grads, zones):
    nt = len(grads)
    pieces = _chunks(nt, GATHER_CHUNK)
    npc = len(pieces)

    def body(*refs):
        srcs, lands, outs = refs[:nt], refs[nt:2 * nt], refs[2 * nt:3 * nt]
        slots, sums = refs[3 * nt:4 * nt], refs[4 * nt:5 * nt]
        lsem, osem, xsem, ysem = refs[5 * nt:]
        x, y, c = _coords()
        me = _dev_index((x, y, c))
        sib = (x, y, 1 - c)
        loads = [pltpu.make_async_copy(_reduced_by(srcs[t], t, me), slots[t].at[me], lsem.at[t]) for t in range(nt)]
        for k in range(1, N_DEV):
            peer = _dev_index(_peer(k))
            loads += [pltpu.make_async_copy(lands[t].at[peer], slots[t].at[peer], lsem.at[k * nt + t]) for t in range(nt)]
        for cp in loads:
            cp.start()
        for cp in loads:
            cp.wait()
        sends = []
        place = lambda t, cc, off, n: _rows(outs[t], cc * (SHARD_ROWS[t] // 2) + off, n)
        stores = []
        for q, (t, off, n) in enumerate(pieces):
            r = pl.ds(off, n)
            acc = slots[t][0, r, :].astype(F32)
            for d in range(1, N_DEV):
                acc = acc + slots[t][d, r, :].astype(F32)
            sums[t][r, :] = acc
            keep = pltpu.make_async_copy(sums[t].at[r, :], place(t, c, off, n), osem.at[q])
            give = _remote(sums[t].at[r, :], place(t, c, off, n), xsem.at[q], ysem.at[q], sib)
            keep.start()
            give.start()
            stores.append(keep)
            sends.append(give)
        for q, (t, off, n) in enumerate(pieces):
            got = place(t, 1 - c, off, n)
            _remote(got, got, xsem.at[q], ysem.at[q], sib).wait_recv()
        for cp in sends:
            cp.wait_send()
        for cp in stores:
            cp.wait()

    dma = pltpu.SemaphoreType.DMA
    return pl.pallas_call(
        body, name="grad_finish", in_specs=[HBM_SPEC] * (2 * nt), out_specs=[HBM_SPEC] * nt,
        out_shape=[jax.ShapeDtypeStruct((SHARD_ROWS[t], g.shape[1]), F32) for t, g in enumerate(grads)],
        scratch_shapes=([pltpu.VMEM((N_DEV, SHARD_ROWS[t] // 2, g.shape[1]), g.dtype) for t, g in enumerate(grads)]
                        + [pltpu.VMEM((SHARD_ROWS[t] // 2, g.shape[1]), F32) for t, g in enumerate(grads)]
                        + [dma((N_DEV * nt,)), dma((npc,)), dma((npc,)), dma((npc,))]),
        compiler_params=_cp())(*grads, *zones)


def _adam_math(w, g, m, v):
    m = ADAM_B1 * m + (1.0 - ADAM_B1) * g
    v = ADAM_B2 * v + (1.0 - ADAM_B2) * (g * g)
    m_hat = m / (1.0 - ADAM_B1 ** ADAM_STEP)
    v_hat = v / (1.0 - ADAM_B2 ** ADAM_STEP)
    return -ADAM_LR * (m_hat / (jnp.sqrt(v_hat) + ADAM_EPS) + ADAM_WD * w), m, v


def _adamw(w, g, m, v, rb, name):
    rows, cols = w.shape

    def body(w_ref, g_ref, m_ref, v_ref, d_ref, nm_ref, nv_ref):
        d_ref[...], nm_ref[...], nv_ref[...] = _adam_math(w_ref[...], g_ref[...], m_ref[...], v_ref[...])

    spec = pl.BlockSpec((rb, cols), lambda i: (i, 0))
    return pl.pallas_call(body, grid=(rows // rb,), name=name, in_specs=[spec] * 4, out_specs=[spec] * 3,
                          out_shape=[jax.ShapeDtypeStruct(w.shape, F32)] * 3, compiler_params=_cp("parallel"))(w, g, m, v)


PACK = (("b_ada", 3 * D_MODEL), ("g_pre", D_MODEL), ("g_post", D_MODEL), ("gn_g", 512), ("qn_g", 64), ("kn_g", 64),
        ("w_dec_f", 4), ("w_dec_b", 4), ("loss", 1))
PACK_WIDTH = 6144


def _pack(parts):
    used = sum(n for _, n in PACK)
    cols = [parts[name].reshape(1, n).astype(F32) if name in parts else jnp.zeros((1, n), F32) for name, n in PACK]
    return jnp.concatenate(cols + [jnp.zeros((1, PACK_WIDTH - used), F32)], axis=1)


def _unpack(row):
    out, off = {}, 0
    for name, n in PACK:
        out[name] = row[:, off:off + n]
        off += n
    return out


def _small_reduce(vec, wvec, mvec, vvec, cs):
    ncol = 3 * D_MODEL // N_CHIP

    def body(vec_ref, w_ref, m_ref, v_ref, cs_ref, gsum_ref, d_ref, nm_ref, nv_ref, gwa_ref, gat, ssem, rsem):
        me = _dev_index(_coords())
        chip = me // 2
        gat[me] = vec_ref[...]
        sends = []
        for k in range(1, N_DEV):
            cp = _remote(vec_ref, gat.at[me], ssem.at[k - 1], rsem.at[k - 1], _peer(k))
            cp.start()
            sends.append(cp)
        for k in range(1, N_DEV):
            slot = gat.at[_dev_index(_peer(k))]
            _remote(slot, slot, ssem.at[k - 1], rsem.at[k - 1], _peer(k)).wait_recv()
        rows = [gat[d] for d in range(N_DEV)]
        tot = rows[0]
        for d in range(1, N_DEV):
            tot = tot + rows[d]
        gsum_ref[...] = tot
        d_ref[...], nm_ref[...], nv_ref[...] = _adam_math(w_ref[...], tot, m_ref[...], v_ref[...])
        cs = cs_ref[...]
        ca_t = jnp.transpose(jnp.concatenate([cs * _sigmoid(cs), jnp.zeros((128 - N_DEV, D_MODEL), F32)], axis=0))
        dm = jnp.concatenate(rows + [jnp.zeros((128 - N_DEV, PACK_WIDTH), F32)], axis=0)
        for jj in range(N_CHIP):
            @pl.when(chip == jj)
            def _():
                gwa_ref[...] = _nn(ca_t, dm[:, ncol * jj:ncol * (jj + 1)], precision=HIGHEST)
        for cp in sends:
            cp.wait_send()

    row = jax.ShapeDtypeStruct((1, PACK_WIDTH), F32)
    return pl.pallas_call(
        body, name="small_reduce", in_specs=[VMEM_SPEC] * 5, out_specs=[VMEM_SPEC] * 5,
        out_shape=[row, row, row, row, jax.ShapeDtypeStruct((D_MODEL, ncol), F32)],
        scratch_shapes=[pltpu.VMEM((N_DEV, 1, PACK_WIDTH), F32), pltpu.SemaphoreType.DMA((7,)), pltpu.SemaphoreType.DMA((7,))],
        compiler_params=_cp())(vec, wvec, mvec, vvec, cs)


def kernel(x, c, w_ada, b_ada, g_pre, w_in, qn_g, kn_g, w_dec_f, w_dec_b, gn_g, w_pa, w_pr, w_out, g_post, loss_target, m_w_ada, m_b_ada, m_g_pre, m_w_in, m_qn_g, m_kn_g, m_w_dec_f, m_w_dec_b, m_gn_g, m_w_pa, m_w_pr, m_w_out, m_g_post, v_w_ada, v_b_ada, v_g_pre, v_w_in, v_qn_g, v_kn_g, v_w_dec_f, v_w_dec_b, v_gn_g, v_w_pa, v_w_pr, v_w_out, v_g_post):
    shards = (w_in[0].T.astype(MXU_DTYPE), jnp.concatenate([w_pa[0].T, w_pr[0].T], axis=1).astype(MXU_DTYPE),
              w_out[0].astype(MXU_DTYPE))
    win_t, wpt, wout = _weight_gather(shards)
    cs, mod = _mod_exchange(c, w_ada[0], b_ada.reshape(N_CHIP, 3 * D_MODEL // N_CHIP))

    kinds = {"early": (1, 2), "late": (0,)}
    started = {}

    def send(name, arrays):
        started[name] = _scatter_start(arrays, kinds[name], "grad_scatter_" + name)
        return started[name][-1]

    grad_x, _, _, _, small = _local_step(x[0], loss_target[0], mod, g_pre, qn_g, kn_g, w_dec_f, w_dec_b,
                                         gn_g, g_post, win_t, wpt, wout, send=send)
    (g_pt, g_out), (z_pt, z_out) = _scatter_wait(started["early"], kinds["early"], grad_x, "grad_scatter_early_wait")
    (g_in_t,), (z_in,) = _scatter_wait(started["late"], kinds["late"], grad_x, "grad_scatter_late_wait")
    full = _grad_finish((g_in_t, g_pt, g_out), (z_in, z_pt, z_out))
    big = {"w_in": (w_in, m_w_in, v_w_in, full[0].T), "w_pa": (w_pa, m_w_pa, v_w_pa, full[1][:, :512].T),
           "w_pr": (w_pr, m_w_pr, v_w_pr, full[1][:, 512:].T), "w_out": (w_out, m_w_out, v_w_out, full[2])}

    small = dict(small)
    small["b_ada"] = small.pop("dmod")
    given = dict(b_ada=(b_ada, m_b_ada, v_b_ada), g_pre=(g_pre, m_g_pre, v_g_pre), g_post=(g_post, m_g_post, v_g_post),
                 gn_g=(gn_g, m_gn_g, v_gn_g), qn_g=(qn_g, m_qn_g, v_qn_g), kn_g=(kn_g, m_kn_g, v_kn_g),
                 w_dec_f=(w_dec_f, m_w_dec_f, v_w_dec_f), w_dec_b=(w_dec_b, m_w_dec_b, v_w_dec_b))
    packs = [_pack({n: t[i] for n, t in given.items()}) for i in range(3)]
    gsum, dvec, nmvec, nvvec, g_w_ada = _small_reduce(_pack(small), *packs, cs.reshape(N_DEV, D_MODEL))
    big["w_ada"] = (w_ada, m_w_ada, v_w_ada, g_w_ada)

    grads, deltas, new_m, new_v = {}, {}, {}, {}
    for name, (w, m, v, g) in big.items():
        d, nm, nv = _adamw(w[0], g, m[0], v[0], 256, "adamw_" + name)
        grads[name], deltas[name], new_m[name], new_v[name] = g[None], d[None], nm[None], nv[None]
    for dst, row in ((grads, gsum), (deltas, dvec), (new_m, nmvec), (new_v, nvvec)):
        dst.update({n: a for n, a in _unpack(row).items() if n != "loss"})
    order = ("w_ada", "b_ada", "g_pre", "w_in", "qn_g", "kn_g", "w_dec_f", "w_dec_b", "gn_g", "w_pa", "w_pr", "w_out", "g_post")
    loss = _unpack(gsum)["loss"][0, 0]
    return (loss, grad_x[None], *[grads[n] for n in order], *[deltas[n] for n in order],
            *[new_m[n] for n in order], *[new_v[n] for n in order])
```

```python
import functools

import jax
import jax.numpy as jnp
from jax import lax
from jax.experimental import pallas as pl
from jax.experimental.pallas import tpu as pltpu

F32 = jnp.float32
BF16 = jnp.bfloat16
MXU_DTYPE = jnp.bfloat16

D_MODEL = 1024
GRID_W = 64
HEAD_DIM = 64
ROPE_THETA = 10000.0
EPS = 1e-6
RET_HEADS = 4
RET_CHUNK = 256
IN_WIDTH = 4864
QA, KA, VA, ZA, QR, KR, VR, ZR, GL = 0, 512, 640, 768, 1280, 1536, 1792, 2304, 2816

ADAM_LR, ADAM_B1, ADAM_B2, ADAM_EPS, ADAM_WD, ADAM_STEP = 0.001, 0.9, 0.999, 1e-08, 0.01, 10

VMEM_LIMIT = 56 * 1024 * 1024
MESH = pl.DeviceIdType.MESH
HIGHEST = lax.Precision.HIGHEST
LOG2E = 1.4426950408889634
LN2 = 0.6931471805599453


def _cp(*sem, **kw):
    if sem:
        kw["dimension_semantics"] = sem
    return pltpu.CompilerParams(vmem_limit_bytes=VMEM_LIMIT, **kw)


def _nn(a, b, **kw):
    return lax.dot_general(a, b, (((1,), (0,)), ((), ())), preferred_element_type=F32, **kw)


def _nt(a, b):
    return lax.dot_general(a, b, (((1,), (1,)), ((), ())), preferred_element_type=F32)


def _tn(a, b):
    return lax.dot_general(a, b, (((0,), (0,)), ((), ())), preferred_element_type=F32)


def _mx(x):
    return x.astype(MXU_DTYPE)


def _lane(shape):
    return lax.broadcasted_iota(jnp.int32, shape, 1)


def _sigmoid(z):
    return 1.0 / (1.0 + jnp.exp(-z))


def _rowsum128(x):
    s = jnp.sum(x, axis=0, keepdims=True)
    out = s[:, 0:128]
    for k in range(1, x.shape[1] // 128):
        out = out + s[:, 128 * k:128 * (k + 1)]
    return out


def _swap16(x):
    return jnp.where((_lane(x.shape) & 16) == 0, pltpu.roll(x, 112, 1), pltpu.roll(x, 16, 1))


def _rope(x, cos, sin):
    return x * cos + _swap16(x) * sin


def _rope_t(d, cos, sin):
    return d * cos + _swap16(d * sin)


def _blockdiag64():
    r = lax.broadcasted_iota(jnp.int32, (128, 128), 0) // 64
    c = lax.broadcasted_iota(jnp.int32, (128, 128), 1) // 64
    return (r == c).astype(BF16)


def _seg64(x, m):
    hi = x.astype(BF16)
    lo = (x - hi.astype(F32)).astype(BF16)
    return _nn(hi, m) + _nn(lo, m)


def _rope_tables(seq):
    t = jnp.arange(seq)
    row = (t // GRID_W).astype(F32)
    col = (t % GRID_W).astype(F32)
    half = HEAD_DIM // 2
    inv_freq = ROPE_THETA ** (-jnp.arange(0, half, 2, dtype=F32) / half)
    ar = row[:, None] * inv_freq[None, :]
    ac = col[:, None] * inv_freq[None, :]
    cr, sr, cc, sc = jnp.cos(ar), jnp.sin(ar), jnp.cos(ac), jnp.sin(ac)
    cos64 = jnp.concatenate([cr, cr, cc, cc], axis=1)
    sin64 = jnp.concatenate([-sr, sr, -sc, sc], axis=1)
    return jnp.tile(cos64, (1, 2)), jnp.tile(sin64, (1, 2))


def _prenorm(x, mod, g_pre):
    seq = x.shape[0]
    bs = 512

    def body(x_ref, mod_ref, g_ref, h_ref):
        xv = x_ref[...]
        r = lax.rsqrt(jnp.mean(xv * xv, axis=-1, keepdims=True) + EPS)
        shift = mod_ref[:, 0:D_MODEL]
        scale = mod_ref[:, D_MODEL:2 * D_MODEL]
        h_ref[...] = ((xv * r) * g_ref[...] * (1.0 + scale) + shift).astype(h_ref.dtype)

    return pl.pallas_call(
        body, grid=(seq // bs,), name="prenorm",
        in_specs=[pl.BlockSpec((bs, D_MODEL), lambda i: (i, 0)), pl.BlockSpec((1, 3 * D_MODEL), lambda i: (0, 0)),
                  pl.BlockSpec((1, D_MODEL), lambda i: (0, 0))],
        out_specs=pl.BlockSpec((bs, D_MODEL), lambda i: (i, 0)),
        out_shape=jax.ShapeDtypeStruct((seq, D_MODEL), MXU_DTYPE), compiler_params=_cp("parallel"))(x, mod, g_pre)


def _prenorm_bwd(x, dh, dout, mod, g_pre):
    seq = x.shape[0]
    bs = 512

    def body(x_ref, dh_ref, dout_ref, mod_ref, g_ref, gx_ref, dshift_ref, dscale_ref, dg_ref):
        @pl.when(pl.program_id(0) == 0)
        def _():
            dshift_ref[...] = jnp.zeros_like(dshift_ref)
            dscale_ref[...] = jnp.zeros_like(dscale_ref)
            dg_ref[...] = jnp.zeros_like(dg_ref)

        xv = x_ref[...]
        dh = dh_ref[...]
        g = g_ref[...]
        r = lax.rsqrt(jnp.mean(xv * xv, axis=-1, keepdims=True) + EPS)
        xn = xv * r
        scale = mod_ref[:, D_MODEL:2 * D_MODEL]
        dshift_ref[...] += jnp.sum(dh, axis=0, keepdims=True)
        dscale_ref[...] += jnp.sum(dh * (xn * g), axis=0, keepdims=True)
        da = dh * (1.0 + scale)
        dg_ref[...] += jnp.sum(da * xn, axis=0, keepdims=True)
        dxn = da * g
        dx = r * (dxn - xn * jnp.mean(dxn * xn, axis=-1, keepdims=True))
        gx_ref[...] = dout_ref[...] + dx

    row = pl.BlockSpec((bs, D_MODEL), lambda i: (i, 0))
    vec = pl.BlockSpec((1, D_MODEL), lambda i: (0, 0))
    return pl.pallas_call(
        body, grid=(seq // bs,), name="prenorm_bwd",
        in_specs=[row, row, row, pl.BlockSpec((1, 3 * D_MODEL), lambda i: (0, 0)), vec],
        out_specs=[row, vec, vec, vec],
        out_shape=[jax.ShapeDtypeStruct((seq, D_MODEL), F32)] + [jax.ShapeDtypeStruct((1, D_MODEL), F32)] * 3,
        compiler_params=_cp("arbitrary"))(x, dh, dout, mod, g_pre)


def _dep_args(dep, grid_rank):
    if dep is None:
        return [], []
    return [dep], [pl.BlockSpec(dep.shape, lambda *_: (0,) * dep.ndim)]


def _matmul(a, b, *, ta, tb, tm, tn, out_dtype, name, dep=None):
    kdim = a.shape[0] if ta else a.shape[1]
    m = a.shape[1] if ta else a.shape[0]
    n = b.shape[0] if tb else b.shape[1]
    assert m % tm == 0 and n % tn == 0
    deps, dep_specs = _dep_args(dep, 2)

    def body(a_ref, b_ref, *rest):
        o_ref = rest[-1]
        dims = (((0 if ta else 1,), (1 if tb else 0,)), ((), ()))
        o_ref[...] = lax.dot_general(a_ref[...], b_ref[...], dims, preferred_element_type=F32).astype(o_ref.dtype)

    a_spec = pl.BlockSpec((kdim, tm), lambda j, i: (0, i)) if ta else pl.BlockSpec((tm, kdim), lambda j, i: (i, 0))
    b_spec = pl.BlockSpec((tn, kdim), lambda j, i: (j, 0)) if tb else pl.BlockSpec((kdim, tn), lambda j, i: (0, j))
    return pl.pallas_call(
        body, grid=(n // tn, m // tm), name=name, in_specs=[a_spec, b_spec] + dep_specs,
        out_specs=pl.BlockSpec((tm, tn), lambda j, i: (i, j)),
        out_shape=jax.ShapeDtypeStruct((m, n), out_dtype), compiler_params=_cp("parallel", "parallel"))(a, b, *deps)


def _prep(p, cos, sin, qg2, kg2):
    seq = p.shape[0]
    bs = 512

    def body(p_ref, cos_ref, sin_ref, qg_ref, kg_ref, qt_ref, kd_ref, vd_ref, rq_ref, rk_ref, rv_ref):
        m = _blockdiag64()
        cs, sn = cos_ref[...], sin_ref[...]
        lo = _lane((bs, 128)) < 64
        for pr in range(4):
            q = p_ref[:, QA + 128 * pr:QA + 128 * (pr + 1)]
            r = lax.rsqrt(_seg64(q * q, m) * (1.0 / 64) + EPS)
            qt_ref[:, 128 * pr:128 * (pr + 1)] = (_rope(q * r * qg_ref[...], cs, sn) * (0.125 * LOG2E)).astype(qt_ref.dtype)
        k = p_ref[:, KA:KA + 128]
        r = lax.rsqrt(_seg64(k * k, m) * (1.0 / 64) + EPS)
        kr = _rope(k * r * kg_ref[...], cs, sn)
        ksw = pltpu.roll(kr, 64, 1)
        kd_ref[0] = jnp.where(lo, kr, ksw).astype(kd_ref.dtype)
        kd_ref[1] = jnp.where(lo, ksw, kr).astype(kd_ref.dtype)
        v = p_ref[:, VA:VA + 128]
        vsw = pltpu.roll(v, 64, 1)
        vd_ref[0] = jnp.where(lo, v, vsw).astype(vd_ref.dtype)
        vd_ref[1] = jnp.where(lo, vsw, v).astype(vd_ref.dtype)
        for pr in range(2):
            sl = slice(128 * pr, 128 * (pr + 1))
            rq_ref[:, sl] = _rope(p_ref[:, QR + 128 * pr:QR + 128 * (pr + 1)], cs, sn).astype(rq_ref.dtype)
            rk_ref[:, sl] = (_rope(p_ref[:, KR + 128 * pr:KR + 128 * (pr + 1)], cs, sn) * 0.125).astype(rk_ref.dtype)
        rv_ref[...] = p_ref[:, VR:VR + 512].astype(rv_ref.dtype)

    tab = pl.BlockSpec((bs, 128), lambda i: (i, 0))
    gsp = pl.BlockSpec((1, 128), lambda i: (0, 0))
    dup = pl.BlockSpec((2, bs, 128), lambda i: (0, i, 0))
    return pl.pallas_call(
        body, grid=(seq // bs,), name="mixer_prep",
        in_specs=[pl.BlockSpec((bs, ZR), lambda i: (i, 0)), tab, tab, gsp, gsp],
        out_specs=[pl.BlockSpec((bs, 512), lambda i: (i, 0)), dup, dup, pl.BlockSpec((bs, 256), lambda i: (i, 0)),
                   pl.BlockSpec((bs, 256), lambda i: (i, 0)), pl.BlockSpec((bs, 512), lambda i: (i, 0))],
        out_shape=[jax.ShapeDtypeStruct((seq, 512), MXU_DTYPE), jax.ShapeDtypeStruct((2, seq, 128), MXU_DTYPE),
                   jax.ShapeDtypeStruct((2, seq, 128), MXU_DTYPE), jax.ShapeDtypeStruct((seq, 256), MXU_DTYPE),
                   jax.ShapeDtypeStruct((seq, 256), MXU_DTYPE), jax.ShapeDtypeStruct((seq, 512), MXU_DTYPE)],
        compiler_params=_cp("parallel"))(p, cos, sin, qg2, kg2)


def _halves(kd):
    lo = _lane(kd.shape) < 64
    z = jnp.zeros_like(kd)
    return jnp.concatenate([jnp.where(lo, kd, z), jnp.where(lo, z, kd)], axis=0)


def _attn_fwd(qt, kd, vd):
    seq = qt.shape[0]
    bq = bk = 512
    nk = seq // bk

    def body(q_ref, k_ref, v_ref, o_ref, lse_ref, m_scr, l_scr, acc):
        j = pl.program_id(1)
        m_scr[...] = jnp.full_like(m_scr, -jnp.inf)
        l_scr[...] = jnp.zeros_like(l_scr)
        acc[...] = jnp.zeros_like(acc)
        lo = _lane((bq, 128)) < 64

        def kv_block(n, carry):
            rows = pl.ds(pl.multiple_of(n * bk, bk), bk)
            k2, v2 = _halves(k_ref[rows, :]), _halves(v_ref[rows, :])
            for pr in range(2):
                s2 = _nt(q_ref[:, 128 * pr:128 * (pr + 1)], k2)
                ps, alphas = [], []
                for e in range(2):
                    g = 2 * pr + e
                    s = s2[:, e * bk:(e + 1) * bk]
                    m_prev = m_scr[g]
                    m_next = jnp.maximum(m_prev, jnp.max(s, axis=1, keepdims=True))
                    alpha = jnp.exp2(m_prev - m_next)
                    pe = jnp.exp2(s - jnp.tile(m_next, (1, bk // 128)))
                    l_scr[g] = alpha * l_scr[g] + jnp.sum(pe, axis=1, keepdims=True)
                    m_scr[g] = m_next
                    ps.append(_mx(pe))
                    alphas.append(alpha)
                o2 = _nn(jnp.concatenate(ps, axis=1), v2)
                sl = slice(128 * pr, 128 * (pr + 1))
                acc[:, sl] = acc[:, sl] * jnp.where(lo, alphas[0], alphas[1]) + o2
            return carry

        lax.fori_loop(0, nk, kv_block, 0)
        for pr in range(2):
            sl = slice(128 * pr, 128 * (pr + 1))
            o_ref[:, sl] = acc[:, sl] / jnp.where(lo, l_scr[2 * pr], l_scr[2 * pr + 1])
        for g in range(4):
            lse = jnp.transpose(m_scr[g] + jnp.log2(l_scr[g]))
            lse_ref[pl.ds(4 * j + g, 1), :] = lse[0:1, :]

    return pl.pallas_call(
        body, grid=(seq // bq, 2), name="attn_fwd",
        in_specs=[pl.BlockSpec((bq, 256), lambda i, j: (i, j)), pl.BlockSpec((None, seq, 128), lambda i, j: (j, 0, 0)),
                  pl.BlockSpec((None, seq, 128), lambda i, j: (j, 0, 0))],
        out_specs=[pl.BlockSpec((bq, 256), lambda i, j: (i, j)), pl.BlockSpec((8, bq), lambda i, j: (0, i))],
        out_shape=[jax.ShapeDtypeStruct((seq, 512), F32), jax.ShapeDtypeStruct((8, seq), F32)],
        scratch_shapes=[pltpu.VMEM((4, bq, 128), F32), pltpu.VMEM((4, bq, 128), F32), pltpu.VMEM((bq, 256), F32)],
        compiler_params=_cp("parallel", "arbitrary"))(qt, kd, vd)


def _attn_bwd(qt, kd, vd, do, lse, delta, dep=None):
    seq = qt.shape[0]
    bq = bk = 512
    nq, nk = seq // bq, seq // bk
    deps, dep_specs = _dep_args(dep, 3)

    def body(q_ref, do_ref, k_ref, v_ref, lse_ref, del_ref, *rest):
        dq_ref, dk_ref, dv_ref = rest[-3:]
        j, i = pl.program_id(0), pl.program_id(1)
        dq_ref[...] = jnp.zeros_like(dq_ref)

        @pl.when(i == 0)
        def _():
            dk_ref[...] = jnp.zeros_like(dk_ref)
            dv_ref[...] = jnp.zeros_like(dv_ref)

        lo = _lane((bk, 128)) < 64
        big = lambda r: jnp.concatenate([jnp.broadcast_to(r[0], (bk, bq)), jnp.broadcast_to(r[1], (bk, bq))], axis=0)

        def kv_block(n, carry):
            rows = pl.ds(pl.multiple_of(n * bk, bk), bk)
            k2, v2 = _halves(k_ref[rows, :]), _halves(v_ref[rows, :])
            for pr in range(2):
                sl = slice(128 * pr, 128 * (pr + 1))
                qp, dop = q_ref[:, sl], do_ref[:, sl]
                s_t = _nt(k2, qp)
                dp_t = _nt(v2, dop)
                ls = [lse_ref[pl.ds(4 * j + 2 * pr + e, 1), :] for e in range(2)]
                dl = [del_ref[pl.ds(4 * j + 2 * pr + e, 1), :] for e in range(2)]
                p_t = jnp.exp2(s_t - big(ls))
                ds_t = _mx(p_t * (dp_t - big(dl)))
                rv = _nn(_mx(p_t), dop)
                rk = _nn(ds_t, qp) * LN2
                dv_ref[rows, :] += jnp.where(lo, rv[:bk], rv[bk:])
                dk_ref[rows, :] += jnp.where(lo, rk[:bk], rk[bk:])
                dq_ref[:, sl] += _tn(ds_t, k2)
            return carry

        lax.fori_loop(0, nk, kv_block, 0)

    return pl.pallas_call(
        body, grid=(2, nq), name="attn_bwd",
        in_specs=[pl.BlockSpec((bq, 256), lambda j, i: (i, j)), pl.BlockSpec((bq, 256), lambda j, i: (i, j)),
                  pl.BlockSpec((None, seq, 128), lambda j, i: (j, 0, 0)), pl.BlockSpec((None, seq, 128), lambda j, i: (j, 0, 0)),
                  pl.BlockSpec((8, bq), lambda j, i: (0, i)), pl.BlockSpec((8, bq), lambda j, i: (0, i))] + dep_specs,
        out_specs=[pl.BlockSpec((bq, 256), lambda j, i: (i, j)), pl.BlockSpec((None, seq, 128), lambda j, i: (j, 0, 0)),
                   pl.BlockSpec((None, seq, 128), lambda j, i: (j, 0, 0))],
        out_shape=[jax.ShapeDtypeStruct((seq, 512), F32), jax.ShapeDtypeStruct((2, seq, 128), F32),
                   jax.ShapeDtypeStruct((2, seq, 128), F32)],
        compiler_params=_cp("arbitrary", "arbitrary"))(qt, do, kd, vd, lse, delta, *deps)


def _ret_tables(lg_f, lg_b, c):
    idx = jnp.arange(c, dtype=F32)
    diff = idx[:, None] - idx[None, :]
    a, b = lg_f[:, None, None], lg_b[:, None, None]
    low, up = (diff >= 0)[None], (diff < 0)[None]
    dmat = jnp.where(low, jnp.exp(a * jnp.maximum(diff, 0.0)[None]), jnp.exp(b * jnp.maximum(-diff, 0.0)[None]))
    dda = jnp.where(low, diff[None] * jnp.exp(a * jnp.maximum(diff, 0.0)[None]), 0.0)
    ddb = jnp.where(up, -diff[None] * jnp.exp(b * jnp.maximum(-diff, 0.0)[None]), 0.0)
    rep = lambda w: jnp.broadcast_to(w[:, :, None], (RET_HEADS, c, 128))
    wqf = rep(jnp.exp(lg_f[:, None] * (idx + 1.0)[None]))
    wqb = rep(jnp.exp(lg_b[:, None] * (c - idx)[None]))
    wkf = rep(jnp.exp(lg_f[:, None] * (c - 1.0 - idx)[None]))
    wkb = rep(jnp.exp(lg_b[:, None] * idx[None]))
    cdf, cdb = jnp.exp(lg_f * c), jnp.exp(lg_b * c)
    dec_fb = jnp.broadcast_to(jnp.concatenate([cdf, cdb])[:, None], (8, 128))
    dec_bf = jnp.broadcast_to(jnp.concatenate([cdb, cdf])[:, None], (8, 128))
    return dict(dmat=dmat, dda=dda, ddb=ddb, wqf=wqf, wqb=wqb, wkf=wkf, wkb=wkb, dec_fb=dec_fb, dec_bf=dec_bf)


def _ret_states(xk, yv, w_fwd, w_rev, dec, name):
    seq = xk.shape[0]
    c = RET_CHUNK
    nc = seq // c

    def body(xf_ref, yf_ref, xr_ref, yr_ref, wf_ref, wr_ref, dec_ref, sf_ref, sr_ref, st_f, st_r):
        @pl.when(pl.program_id(0) == 0)
        def _():
            st_f[...] = jnp.zeros_like(st_f)
            st_r[...] = jnp.zeros_like(st_r)

        lane = _lane((c, 128))
        for h in range(RET_HEADS):
            pr, e = h // 2, h % 2
            half = (lane < 64) if e == 0 else (lane >= 64)
            for x_ref, y_ref, w_ref, s_ref, st, drow in ((xf_ref, yf_ref, wf_ref, sf_ref, st_f, h),
                                                        (xr_ref, yr_ref, wr_ref, sr_ref, st_r, 4 + h)):
                s_ref[h] = st[h].astype(s_ref.dtype)
                xm = jnp.where(half, x_ref[:, 128 * pr:128 * (pr + 1)].astype(F32) * w_ref[h], 0.0)
                st[h] = st[h] * dec_ref[drow:drow + 1, :] + _tn(_mx(xm), _mx(y_ref[:, 128 * h:128 * (h + 1)]))

    xs = lambda f: pl.BlockSpec((c, 256), f)
    ys = lambda f: pl.BlockSpec((c, 512), f)
    fwd, rev = (lambda n: (n, 0)), (lambda n: (nc - 1 - n, 0))
    wsp = pl.BlockSpec((RET_HEADS, c, 128), lambda n: (0, 0, 0))
    return pl.pallas_call(
        body, grid=(nc,), name=name,
        in_specs=[xs(fwd), ys(fwd), xs(rev), ys(rev), wsp, wsp, pl.BlockSpec((8, 128), lambda n: (0, 0))],
        out_specs=[pl.BlockSpec((None, RET_HEADS, 128, 128), lambda n: (n, 0, 0, 0)),
                   pl.BlockSpec((None, RET_HEADS, 128, 128), lambda n: (nc - 1 - n, 0, 0, 0))],
        out_shape=[jax.ShapeDtypeStruct((nc, RET_HEADS, 128, 128), MXU_DTYPE)] * 2,
        scratch_shapes=[pltpu.VMEM((RET_HEADS, 128, 128), F32), pltpu.VMEM((RET_HEADS, 128, 128), F32)],
        compiler_params=_cp("arbitrary"))(xk, yv, xk, yv, w_fwd, w_rev, dec)


def _ret_fwd(rq, rk, rv, rf, rb, tb):
    seq = rq.shape[0]
    c = RET_CHUNK

    def body(q_ref, k_ref, v_ref, rf_ref, rb_ref, d_ref, wqf_ref, wqb_ref, o_ref):
        lane = _lane((c, 128))
        for h in range(RET_HEADS):
            pr, e = h // 2, h % 2
            half = (lane < 64) if e == 0 else (lane >= 64)
            sl = slice(128 * pr, 128 * (pr + 1))
            qp = q_ref[:, sl]
            km = jnp.where(half, k_ref[:, sl], jnp.zeros_like(k_ref[:, sl]))
            sd = _mx(_nt(qp, km) * d_ref[h])
            qf = qp.astype(F32)
            o = _nn(sd, v_ref[:, 128 * h:128 * (h + 1)])
            o = o + _nn(_mx(qf * wqf_ref[h]), rf_ref[h]) + _nn(_mx(qf * wqb_ref[h]), rb_ref[h])
            o_ref[:, 128 * h:128 * (h + 1)] = o

    st = pl.BlockSpec((None, RET_HEADS, 128, 128), lambda n: (n, 0, 0, 0))
    wsp = pl.BlockSpec((RET_HEADS, c, 128), lambda n: (0, 0, 0))
    return pl.pallas_call(
        body, grid=(seq // c,), name="ret_fwd",
        in_specs=[pl.BlockSpec((c, 256), lambda n: (n, 0)), pl.BlockSpec((c, 256), lambda n: (n, 0)),
                  pl.BlockSpec((c, 512), lambda n: (n, 0)), st, st, pl.BlockSpec((RET_HEADS, c, c), lambda n: (0, 0, 0)), wsp, wsp],
        out_specs=pl.BlockSpec((c, 512), lambda n: (n, 0)),
        out_shape=jax.ShapeDtypeStruct((seq, 512), F32), compiler_params=_cp("parallel"))(
            rq, rk, rv, rf, rb, tb["dmat"], tb["wqf"], tb["wqb"])


def _ret_bwd(rq, rk, rv, do, rf, rb, hf, hb, tb):
    seq = rq.shape[0]
    c = RET_CHUNK

    def body(q_ref, k_ref, v_ref, do_ref, rf_ref, rb_ref, hf_ref, hb_ref, d_ref, dda_ref, ddb_ref,
             wqf_ref, wqb_ref, wkf_ref, wkb_ref, cdec_ref, dq_ref, dk_ref, dv_ref, dlg_ref):
        @pl.when(pl.program_id(0) == 0)
        def _():
            dlg_ref[...] = jnp.zeros_like(dlg_ref)

        lane = _lane((c, 128))
        pos = lax.broadcasted_iota(jnp.int32, (c, 128), 0).astype(F32)
        for pr in range(2):
            sl = slice(128 * pr, 128 * (pr + 1))
            qp, kp = q_ref[:, sl], k_ref[:, sl]
            qf, kf = qp.astype(F32), kp.astype(F32)
            dq_acc = jnp.zeros((c, 128), F32)
            dk_acc = jnp.zeros((c, 128), F32)
            for e in range(2):
                h = 2 * pr + e
                half = (lane < 64) if e == 0 else (lane >= 64)
                hs = slice(128 * h, 128 * (h + 1))
                km = jnp.where(half, kp, jnp.zeros_like(kp))
                kmf = km.astype(F32)
                vh, doh = v_ref[:, hs], do_ref[:, hs]
                rfh, rbh, hfh, hbh = rf_ref[h], rb_ref[h], hf_ref[h], hb_ref[h]
                s = _nt(qp, km)
                dpm = _nt(doh, vh)
                ds_b = _mx(dpm * d_ref[h])
                sd_b = _mx(s * d_ref[h])
                dq_if = wqf_ref[h] * _nt(doh, rfh)
                dq_ib = wqb_ref[h] * _nt(doh, rbh)
                dq_acc = dq_acc + _nn(ds_b, km) + dq_if + dq_ib
                dk_sf = wkf_ref[h] * _nt(vh, hfh)
                dk_sb = wkb_ref[h] * _nt(vh, hbh)
                dk_acc = dk_acc + jnp.where(half, _tn(ds_b, qp), 0.0) + dk_sf + dk_sb
                dv = _tn(sd_b, doh) + _nn(_mx(kmf * wkf_ref[h]), hfh) + _nn(_mx(kmf * wkb_ref[h]), hbh)
                dv_ref[:, hs] = dv.astype(dv_ref.dtype)
                sdp = s * dpm
                hr_f = hfh.astype(F32) * rfh.astype(F32)
                hr_b = hbh.astype(F32) * rbh.astype(F32)
                da = (_rowsum128(sdp * dda_ref[h]) + _rowsum128((pos + 1.0) * qf * dq_if)
                      + _rowsum128((c - 1.0 - pos) * kf * dk_sf) + cdec_ref[h:h + 1, :] * _rowsum128(hr_f))
                db = (_rowsum128(sdp * ddb_ref[h]) + _rowsum128((c - pos) * qf * dq_ib)
                      + _rowsum128(pos * kf * dk_sb) + cdec_ref[4 + h:5 + h, :] * _rowsum128(hr_b))
                dlg_ref[h:h + 1, :] += da
                dlg_ref[4 + h:5 + h, :] += db
            dq_ref[:, sl] = dq_acc
            dk_ref[:, sl] = dk_acc

    st = pl.BlockSpec((None, RET_HEADS, 128, 128), lambda n: (n, 0, 0, 0))
    wsp = pl.BlockSpec((RET_HEADS, c, 128), lambda n: (0, 0, 0))
    dsp = pl.BlockSpec((RET_HEADS, c, c), lambda n: (0, 0, 0))
    x256 = pl.BlockSpec((c, 256), lambda n: (n, 0))
    x512 = pl.BlockSpec((c, 512), lambda n: (n, 0))
    cdec = tb["dec_fb"] * float(c)
    return pl.pallas_call(
        body, grid=(seq // c,), name="ret_bwd",
        in_specs=[x256, x256, x512, x512, st, st, st, st, dsp, dsp, dsp, wsp, wsp, wsp, wsp,
                  pl.BlockSpec((8, 128), lambda n: (0, 0))],
        out_specs=[x256, x256, x512, pl.BlockSpec((8, 128), lambda n: (0, 0))],
        out_shape=[jax.ShapeDtypeStruct((seq, 256), F32), jax.ShapeDtypeStruct((seq, 256), F32),
                   jax.ShapeDtypeStruct((seq, 512), MXU_DTYPE), jax.ShapeDtypeStruct((8, 128), F32)],
        compiler_params=_cp("arbitrary"))(
            rq, rk, rv, do, rf, rb, hf, hb, tb["dmat"], tb["dda"], tb["ddb"], tb["wqf"], tb["wqb"], tb["wkf"], tb["wkb"], cdec)


def _tail(x, tgt, o_att, o_ret, p, mod, g_post, gn_g, wpt, wout):
    seq = x.shape[0]
    bs = 256
    nb = seq // bs

    def body(x_ref, t_ref, oa_ref, or_ref, za_ref, zr_ref, gl_ref, mod_ref, gp_ref, gn_ref, wpt_ref, wout_ref,
             dout_ref, dza_ref, dzr_ref, dgl_ref, doa_ref, dor_ref, del_ref, gout_ref, gpt_ref,
             dgate_ref, dgpost_ref, dgn_ref, loss_ref, gout_acc, gpt_acc):
        i = pl.program_id(0)

        @pl.when(i == 0)
        def _():
            gout_acc[...] = jnp.zeros_like(gout_acc)
            gpt_acc[...] = jnp.zeros_like(gpt_acc)
            dgate_ref[...] = jnp.zeros_like(dgate_ref)
            dgpost_ref[...] = jnp.zeros_like(dgpost_ref)
            dgn_ref[...] = jnp.zeros_like(dgn_ref)
            loss_ref[...] = jnp.zeros_like(loss_ref)

        oa, za, zr = oa_ref[...], za_ref[...], zr_ref[...]
        sga, sgr = _sigmoid(za), _sigmoid(zr)
        sza, szr = za * sga, zr * sgr
        ya_b = _mx(oa * sza)
        ons, rstds = [], []
        for h in range(RET_HEADS):
            oh = or_ref[:, 128 * h:128 * (h + 1)]
            xc = oh - jnp.mean(oh, axis=-1, keepdims=True)
            rstd = lax.rsqrt(jnp.mean(xc * xc, axis=-1, keepdims=True) + EPS)
            ons.append(xc * rstd)
            rstds.append(rstd)
        on = jnp.concatenate(ons, axis=1)
        yn = on * gn_ref[...]
        yr_b = _mx(yn * szr)
        wpa_t, wpr_t = wpt_ref[:, 0:512], wpt_ref[:, 512:1024]
        a_att = _nt(ya_b, wpa_t)
        a_ret = _nt(yr_b, wpr_t)
        gts = _sigmoid(gl_ref[...])
        g_att, g_ret = gts[:, 0:D_MODEL], gts[:, D_MODEL:2 * D_MODEL]
        merged_b = _mx(g_att * a_att + g_ret * a_ret)
        u = _nn(merged_b, wout_ref[...])
        r = lax.rsqrt(jnp.mean(u * u, axis=-1, keepdims=True) + EPS)
        un = u * r
        gpost = gp_ref[...]
        y = un * gpost
        gate = mod_ref[:, 2 * D_MODEL:3 * D_MODEL]
        err = x_ref[...] + gate * y - t_ref[...]
        loss_ref[...] += (0.5 / D_MODEL) * _rowsum128(err * err)
        dout = err * (1.0 / D_MODEL)
        dout_ref[...] = dout
        dgate_ref[...] += jnp.sum(dout * y, axis=0, keepdims=True)
        dy = dout * gate
        dgpost_ref[...] += jnp.sum(dy * un, axis=0, keepdims=True)
        dun = dy * gpost
        du_b = _mx(r * (dun - un * jnp.mean(dun * un, axis=-1, keepdims=True)))
        dmerged = _nt(du_b, wout_ref[...])
        gout_acc[...] += _tn(merged_b, du_b)
        d_att, d_ret = dmerged * g_att, dmerged * g_ret
        dgl_ref[:, 0:D_MODEL] = (d_att * a_att * (1.0 - g_att)).astype(dgl_ref.dtype)
        dgl_ref[:, D_MODEL:2 * D_MODEL] = (d_ret * a_ret * (1.0 - g_ret)).astype(dgl_ref.dtype)
        d_att_b, d_ret_b = _mx(d_att), _mx(d_ret)
        dya = _nn(d_att_b, wpa_t)
        dyr = _nn(d_ret_b, wpr_t)
        gpt_acc[:, 0:512] += _tn(d_att_b, ya_b)
        gpt_acc[:, 512:1024] += _tn(d_ret_b, yr_b)
        dza_ref[...] = (dya * oa * (sga * (1.0 + za * (1.0 - sga)))).astype(dza_ref.dtype)
        doa = dya * sza
        doa_ref[...] = doa.astype(doa_ref.dtype)
        dt = jnp.transpose(doa * oa)
        del_ref[...] = jnp.sum(dt.reshape(8, HEAD_DIM, bs), axis=1)
        dzr_ref[...] = (dyr * yn * (sgr * (1.0 + zr * (1.0 - sgr)))).astype(dzr_ref.dtype)
        dyn = dyr * szr
        dgn_ref[...] += jnp.sum(dyn * on, axis=0, keepdims=True)
        don = dyn * gn_ref[...]
        for h in range(RET_HEADS):
            hs = slice(128 * h, 128 * (h + 1))
            dh, oh = don[:, hs], ons[h]
            doh = rstds[h] * (dh - jnp.mean(dh, axis=-1, keepdims=True) - oh * jnp.mean(dh * oh, axis=-1, keepdims=True))
            dor_ref[:, hs] = doh.astype(dor_ref.dtype)

        @pl.when(i == nb - 1)
        def _():
            gout_ref[...] = gout_acc[...].astype(gout_ref.dtype)
            gpt_ref[...] = gpt_acc[...].astype(gpt_ref.dtype)

    row = lambda w: pl.BlockSpec((bs, w), lambda i: (i, 0))
    el = lambda w, off: pl.BlockSpec((pl.Element(bs), pl.Element(w)), lambda i: (i * bs, off))
    vec = lambda w: pl.BlockSpec((1, w), lambda i: (0, 0))
    full = pl.BlockSpec((D_MODEL, D_MODEL), lambda i: (0, 0))
    sds = jax.ShapeDtypeStruct
    return pl.pallas_call(
        body, grid=(nb,), name="tail",
        in_specs=[row(D_MODEL), row(D_MODEL), row(512), row(512), el(512, ZA), el(512, ZR), el(2 * D_MODEL, GL),
                  vec(3 * D_MODEL), vec(D_MODEL), vec(512), full, full],
        out_specs=[row(D_MODEL), row(512), row(512), row(2 * D_MODEL), row(512), row(512),
                   pl.BlockSpec((8, bs), lambda i: (0, i)), full, full, vec(D_MODEL), vec(D_MODEL), vec(512), vec(128)],
        out_shape=[sds((seq, D_MODEL), F32), sds((seq, 512), MXU_DTYPE), sds((seq, 512), MXU_DTYPE),
                   sds((seq, 2 * D_MODEL), MXU_DTYPE), sds((seq, 512), MXU_DTYPE), sds((seq, 512), MXU_DTYPE),
                   sds((8, seq), F32), sds((D_MODEL, D_MODEL), MXU_DTYPE), sds((D_MODEL, D_MODEL), MXU_DTYPE),
                   sds((1, D_MODEL), F32), sds((1, D_MODEL), F32), sds((1, 512), F32), sds((1, 128), F32)],
        scratch_shapes=[pltpu.VMEM((D_MODEL, D_MODEL), F32), pltpu.VMEM((D_MODEL, D_MODEL), F32)],
        compiler_params=_cp("arbitrary"))(x, tgt, o_att, o_ret, p, p, p, mod, g_post, gn_g, wpt, wout)


def _assemble(p, cos, sin, qg2, kg2, dqt, dkp, dvp, dza, drq, drk, drv, dzr, dgl):
    seq = p.shape[0]
    bs = 512

    def body(p_ref, cos_ref, sin_ref, qg_ref, kg_ref, dqt_ref, dkp_ref, dvp_ref, dza_ref, drq_ref, drk_ref, drv_ref,
             dzr_ref, dgl_ref, dp_ref, dqg_ref, dkg_ref):
        @pl.when(pl.program_id(0) == 0)
        def _():
            dqg_ref[...] = jnp.zeros_like(dqg_ref)
            dkg_ref[...] = jnp.zeros_like(dkg_ref)

        m = _blockdiag64()
        cs, sn = cos_ref[...], sin_ref[...]
        lo = _lane((bs, 128)) < 64

        def norm_bwd(raw, dn, g, dg_ref):
            r = lax.rsqrt(_seg64(raw * raw, m) * (1.0 / 64) + EPS)
            xn = raw * r
            dg_ref[...] += jnp.sum(dn * xn, axis=0, keepdims=True)
            dxn = dn * g
            return r * (dxn - xn * (_seg64(dxn * xn, m) * (1.0 / 64)))

        for pr in range(4):
            sl = slice(128 * pr, 128 * (pr + 1))
            dn = _rope_t(dqt_ref[:, sl] * 0.125, cs, sn)
            dp_ref[:, QA + 128 * pr:QA + 128 * (pr + 1)] = norm_bwd(p_ref[:, sl], dn, qg_ref[...], dqg_ref).astype(dp_ref.dtype)
        fold = lambda a: a + pltpu.roll(a, 64, 1)
        dk = jnp.where(lo, fold(dkp_ref[0]), fold(dkp_ref[1]))
        dp_ref[:, KA:KA + 128] = norm_bwd(p_ref[:, KA:KA + 128], _rope_t(dk, cs, sn), kg_ref[...], dkg_ref).astype(dp_ref.dtype)
        dp_ref[:, VA:VA + 128] = jnp.where(lo, fold(dvp_ref[0]), fold(dvp_ref[1])).astype(dp_ref.dtype)
        dp_ref[:, ZA:ZA + 512] = dza_ref[...]
        for pr in range(2):
            sl = slice(128 * pr, 128 * (pr + 1))
            dp_ref[:, QR + 128 * pr:QR + 128 * (pr + 1)] = _rope_t(drq_ref[:, sl], cs, sn).astype(dp_ref.dtype)
            dp_ref[:, KR + 128 * pr:KR + 128 * (pr + 1)] = _rope_t(drk_ref[:, sl] * 0.125, cs, sn).astype(dp_ref.dtype)
        dp_ref[:, VR:VR + 512] = drv_ref[...]
        dp_ref[:, ZR:ZR + 512] = dzr_ref[...]
        dp_ref[:, GL:GL + 2 * D_MODEL] = dgl_ref[...]

    row = lambda w: pl.BlockSpec((bs, w), lambda i: (i, 0))
    gsp = pl.BlockSpec((1, 128), lambda i: (0, 0))
    dup = pl.BlockSpec((2, bs, 128), lambda i: (0, i, 0))
    return pl.pallas_call(
        body, grid=(seq // bs,), name="assemble_dp",
        in_specs=[row(768), row(128), row(128), gsp, gsp, row(512), dup, dup, row(512), row(256), row(256), row(512),
                  row(512), row(2 * D_MODEL)],
        out_specs=[row(IN_WIDTH), gsp, gsp],
        out_shape=[jax.ShapeDtypeStruct((seq, IN_WIDTH), MXU_DTYPE), jax.ShapeDtypeStruct((1, 128), F32),
                   jax.ShapeDtypeStruct((1, 128), F32)],
        compiler_params=_cp("arbitrary"))(p, cos, sin, qg2, kg2, dqt, dkp, dvp, dza, drq, drk, drv, dzr, dgl)


def _local_step(x, tgt, mod, g_pre, qn_g, kn_g, w_dec_f, w_dec_b, gn_g, g_post, win_t, wpt, wout, send=None):
    send = send or (lambda name, arrays: None)
    seq = x.shape[0]
    cos, sin = _rope_tables(seq)
    qg2, kg2 = jnp.tile(qn_g, (1, 2)), jnp.tile(kn_g, (1, 2))
    lg_f = jax.nn.log_sigmoid(w_dec_f[0])
    lg_b = jax.nn.log_sigmoid(w_dec_b[0])
    tb = _ret_tables(lg_f, lg_b, RET_CHUNK)

    h = _prenorm(x, mod, g_pre)
    p = _matmul(h, win_t, ta=False, tb=True, tm=512, tn=IN_WIDTH // 2, out_dtype=F32, name="in_proj")
    qt, kd, vd, rq, rk, rv = _prep(p, cos, sin, qg2, kg2)
    o_att, lse = _attn_fwd(qt, kd, vd)
    rf, rb = _ret_states(rk, rv, tb["wkf"], tb["wkb"], tb["dec_fb"], "ret_states_fwd")
    o_ret = _ret_fwd(rq, rk, rv, rf, rb, tb)
    (dout, dza, dzr, dgl, do_att, do_ret, delta, g_out, g_pt, dgate, dgpost, dgn, loss_l) = _tail(
        x, tgt, o_att, o_ret, p, mod, g_post, gn_g, wpt, wout)
    dep = send("early", (g_pt, g_out))
    dqt, dkp, dvp = _attn_bwd(qt, kd, vd, do_att, lse, delta, dep=dep)
    hb, hf = _ret_states(rq, do_ret, tb["wqb"], tb["wqf"], tb["dec_bf"], "ret_states_bwd")
    drq, drk, drv, dlg = _ret_bwd(rq, rk, rv, do_ret, rf, rb, hf, hb, tb)
    dp, dqg, dkg = _assemble(p, cos, sin, qg2, kg2, dqt, dkp, dvp, dza, drq, drk, drv, dzr, dgl)
    g_in_t = _matmul(dp, h, ta=True, tb=False, tm=256, tn=D_MODEL, out_dtype=MXU_DTYPE, name="in_proj_dw")
    dep = send("late", (g_in_t,))
    dh = _matmul(dp, win_t, ta=False, tb=False, tm=512, tn=D_MODEL, out_dtype=F32, name="in_proj_dx", dep=dep)
    grad_x, dshift, dscale, dgpre = _prenorm_bwd(x, dh, dout, mod, g_pre)

    dlg = jnp.sum(dlg, axis=1)
    small = dict(
        dmod=jnp.concatenate([dshift, dscale, dgate], axis=1),
        g_pre=dgpre, g_post=dgpost, gn_g=dgn,
        qn_g=dqg[:, :64] + dqg[:, 64:], kn_g=dkg[:, :64] + dkg[:, 64:],
        w_dec_f=(dlg[0:4] * jax.nn.sigmoid(-w_dec_f[0]))[None], w_dec_b=(dlg[4:8] * jax.nn.sigmoid(-w_dec_b[0]))[None],
        loss=jnp.sum(loss_l, axis=1, keepdims=True))
    return grad_x, g_in_t, g_pt, g_out, small


N_DEV = 8
N_CHIP = 4
HBM_SPEC = pl.BlockSpec(memory_space=pl.ANY)
VMEM_SPEC = pl.BlockSpec(memory_space=pltpu.VMEM)
SHARD_ROWS = (IN_WIDTH // N_CHIP, D_MODEL // N_CHIP, D_MODEL // N_CHIP)


def _coords():
    return lax.axis_index("x"), lax.axis_index("y"), lax.axis_index("c")


def _peer(k):
    x, y, c = _coords()
    return (1 - x if k & 4 else x, 1 - y if k & 2 else y, 1 - c if k & 1 else c)


def _dev_index(p):
    return 4 * p[0] + 2 * p[1] + p[2]


def _rows(ref, start, size):
    return ref.at[pl.ds(pl.multiple_of(start, 16), size), :]


def _remote(src, dst, ssem, rsem, dev):
    return pltpu.make_async_remote_copy(src_ref=src, dst_ref=dst, send_sem=ssem, recv_sem=rsem, device_id=dev,
                                        device_id_type=MESH)


def _mod_exchange(c, w_ada_s, b4):
    ncol = w_ada_s.shape[1]

    def body(c_ref, w_ref, b_ref, cs_ref, mod_ref, modsh, modall, ssem, rsem):
        me = _dev_index(_coords())
        chip = me // 2
        cs_ref[me] = c_ref[...]
        sends = []
        for k in range(1, N_DEV):
            cp = _remote(c_ref, cs_ref.at[me], ssem.at[k - 1], rsem.at[k - 1], _peer(k))
            cp.start()
            sends.append(cp)
        for k in range(1, N_DEV):
            slot = cs_ref.at[_dev_index(_peer(k))]
            _remote(slot, slot, ssem.at[k - 1], rsem.at[k - 1], _peer(k)).wait_recv()
        cs = jnp.concatenate([cs_ref[d] for d in range(N_DEV)], axis=0)
        ms = _nn(cs * _sigmoid(cs), w_ref[...], precision=HIGHEST) + b_ref[pl.ds(chip, 1), :]
        modsh[...] = ms
        modall[chip] = ms
        for k2 in range(1, N_CHIP):
            cp = _remote(modsh, modall.at[chip], ssem.at[6 + k2], rsem.at[6 + k2], _peer(2 * k2))
            cp.start()
            sends.append(cp)
        for k2 in range(1, N_CHIP):
            slot = modall.at[_dev_index(_peer(2 * k2)) // 2]
            _remote(slot, slot, ssem.at[6 + k2], rsem.at[6 + k2], _peer(2 * k2)).wait_recv()
        for jj in range(N_CHIP):
            mod_ref[:, ncol * jj:ncol * (jj + 1)] = modall[jj, pl.ds(me, 1), :]
        for cp in sends:
            cp.wait_send()

    return pl.pallas_call(
        body, name="mod_exchange", in_specs=[VMEM_SPEC] * 3, out_specs=[VMEM_SPEC] * 2,
        out_shape=[jax.ShapeDtypeStruct((N_DEV, 1, D_MODEL), F32), jax.ShapeDtypeStruct((1, N_CHIP * ncol), F32)],
        scratch_shapes=[pltpu.VMEM((N_DEV, ncol), F32), pltpu.VMEM((N_CHIP, N_DEV, ncol), F32),
                        pltpu.SemaphoreType.DMA((10,)), pltpu.SemaphoreType.DMA((10,))],
        compiler_params=_cp())(c, w_ada_s, b4)


GATHER_CHUNK = 32


def _chunks(nt, chunk):
    out = []
    for t in range(nt):
        half = SHARD_ROWS[t] // 2
        step = chunk if half % chunk == 0 else half
        out += [(t, off, step) for off in range(0, half, step)]
    return out


def _weight_gather(shards):
    nt = len(shards)
    pieces = _chunks(nt, GATHER_CHUNK)
    npc = len(pieces)

    def body(*refs):
        srcs, outs = refs[:nt], refs[nt:2 * nt]
        lsem, ssem, rsem = refs[2 * nt:]
        x, y, c = _coords()
        chip = 2 * x + y
        sib = (x, y, 1 - c)

        def place(t, ch, cc, off, n):
            return _rows(outs[t], ch * SHARD_ROWS[t] + cc * (SHARD_ROWS[t] // 2) + off, n)

        local = [pltpu.make_async_copy(srcs[t].at[pl.ds(0, SHARD_ROWS[t]), :], _rows(outs[t], chip * SHARD_ROWS[t], SHARD_ROWS[t]),
                                       lsem.at[t]) for t in range(nt)]
        for cp in local:
            cp.start()
        sends = []
        for k2 in range(1, N_CHIP):
            for q, (t, off, n) in enumerate(pieces):
                i = (k2 - 1) * npc + q
                cp = _remote(_rows(srcs[t], c * (SHARD_ROWS[t] // 2) + off, n), place(t, chip, c, off, n),
                             ssem.at[i], rsem.at[i], _peer(2 * k2))
                cp.start()
                sends.append(cp)
        for k2 in range(1, N_CHIP):
            pchip = _dev_index(_peer(2 * k2)) // 2
            for q, (t, off, n) in enumerate(pieces):
                i = (k2 - 1) * npc + q
                got = place(t, pchip, c, off, n)
                _remote(got, got, ssem.at[i], rsem.at[i], _peer(2 * k2)).wait_recv()
                cp = _remote(got, got, ssem.at[3 * npc + i], rsem.at[3 * npc + i], sib)
                cp.start()
                sends.append(cp)
        for k2 in range(1, N_CHIP):
            pchip = _dev_index(_peer(2 * k2)) // 2
            for q, (t, off, n) in enumerate(pieces):
                i = (k2 - 1) * npc + q
                got = place(t, pchip, 1 - c, off, n)
                _remote(got, got, ssem.at[3 * npc + i], rsem.at[3 * npc + i], sib).wait_recv()
        for cp in sends:
            cp.wait_send()
        for cp in local:
            cp.wait()

    return pl.pallas_call(
        body, name="weight_gather", in_specs=[VMEM_SPEC] * nt, out_specs=[HBM_SPEC] * nt,
        out_shape=[jax.ShapeDtypeStruct((N_CHIP * SHARD_ROWS[t], s.shape[1]), s.dtype) for t, s in enumerate(shards)],
        scratch_shapes=[pltpu.SemaphoreType.DMA((nt,)), pltpu.SemaphoreType.DMA((6 * npc,)), pltpu.SemaphoreType.DMA((6 * npc,))],
        compiler_params=_cp())(*shards)


SEM_SPEC = pl.BlockSpec(memory_space=pltpu.SEMAPHORE)
HBM_ONLY = pl.BlockSpec(memory_space=pltpu.HBM)
DATAFLOW = pltpu.SideEffectType.DATAFLOW_SIDE_EFFECTING


def _reduced_by(ref, t, dev):
    return _rows(ref, (dev // 2) * SHARD_ROWS[t] + (dev % 2) * (SHARD_ROWS[t] // 2), SHARD_ROWS[t] // 2)


def _scatter_start(grads, kinds, name):
    n = len(grads)

    def body(*refs):
        srcs, lands = refs[:n], refs[n:2 * n]
        ssem, rsem = refs[2 * n], refs[2 * n + 1]
        token = refs[-1]
        me = _dev_index(_coords())
        for k in range(1, N_DEV):
            for i, t in enumerate(kinds):
                q = (k - 1) * n + i
                _remote(_reduced_by(srcs[i], t, _dev_index(_peer(k))), lands[i].at[me], ssem.at[q], rsem.at[q], _peer(k)).start()
        token[...] = jnp.zeros_like(token)

    zones = [lax.empty((N_DEV, SHARD_ROWS[t] // 2, g.shape[1]), g.dtype) for g, t in zip(grads, kinds)]
    hbm = lambda a: pltpu.with_memory_space_constraint(a, pltpu.HBM)
    dma = pltpu.SemaphoreType.DMA
    outs = pl.pallas_call(
        body, name=name, in_specs=[HBM_ONLY] * (2 * n), out_specs=[SEM_SPEC, SEM_SPEC] + [HBM_ONLY] * (2 * n) + [VMEM_SPEC],
        out_shape=[dma((7 * n,)), dma((7 * n,))] + [pltpu.HBM(a.shape, a.dtype) for a in list(grads) + zones]
        + [jax.ShapeDtypeStruct((8, 128), F32)],
        input_output_aliases={i: 2 + i for i in range(2 * n)},
        compiler_params=pltpu.CompilerParams(has_side_effects=DATAFLOW))(*[hbm(a) for a in list(grads) + zones])
    return outs[0], outs[1], outs[2:2 + n], outs[2 + n:2 + 2 * n], outs[-1]


def _scatter_wait(started, kinds, after, name):
    ssem, rsem, grads, zones, _ = started
    n = len(grads)

    def body(*refs):
        srcs, lands = refs[:n], refs[n:2 * n]
        ssem_ref, rsem_ref = refs[2 * n], refs[2 * n + 1]
        for k in range(1, N_DEV):
            peer = _dev_index(_peer(k))
            for i, t in enumerate(kinds):
                q = (k - 1) * n + i
                cp = _remote(_reduced_by(srcs[i], t, peer), lands[i].at[peer], ssem_ref.at[q], rsem_ref.at[q], _peer(k))
                cp.wait_send()
                cp.wait_recv()

    outs = pl.pallas_call(
        body, name=name, in_specs=[HBM_ONLY] * (2 * n) + [SEM_SPEC, SEM_SPEC, HBM_SPEC], out_specs=[HBM_ONLY] * (2 * n),
        out_shape=[pltpu.HBM(a.shape, a.dtype) for a in list(grads) + list(zones)],
        input_output_aliases={i: i for i in range(2 * n)},
        compiler_params=pltpu.CompilerParams(has_side_effects=DATAFLOW))(*grads, *zones, ssem, rsem, after)
    return outs[:n], outs[n:]


def _grad_finish(grads, zones):
    nt = len(grads)
    pieces = _chunks(nt, GATHER_CHUNK)
    npc = len(pieces)

    def body(*refs):
        srcs, lands, outs = refs[:nt], refs[nt:2 * nt], refs[2 * nt:3 * nt]
        slots, sums = refs[3 * nt:4 * nt], refs[4 * nt:5 * nt]
        lsem, osem, xsem, ysem, zeros, zsem = refs[5 * nt:]
        x, y, c = _coords()
        me = _dev_index((x, y, c))
        sib = (x, y, 1 - c)
        loads = [pltpu.make_async_copy(_reduced_by(srcs[t], t, me), slots[t].at[me], lsem.at[t]) for t in range(nt)]
        for k in range(1, N_DEV):
            peer = _dev_index(_peer(k))
            loads += [pltpu.make_async_copy(lands[t].at[peer], slots[t].at[peer], lsem.at[k * nt + t]) for t in range(nt)]
        for cp in loads:
            cp.start()
        for cp in loads:
            cp.wait()
        sends = []
        place = lambda t, cc, off, n: _rows(outs[t], cc * (SHARD_ROWS[t] // 2) + off, n)
        stores = []
        for q, (t, off, n) in enumerate(pieces):
            r = pl.ds(off, n)
            acc = slots[t][0, r, :].astype(F32)
            for d in range(1, N_DEV):
                acc = acc + slots[t][d, r, :].astype(F32)
            sums[t][r, :] = acc
            keep = pltpu.make_async_copy(sums[t].at[r, :], place(t, c, off, n), osem.at[q])
            give = _remote(sums[t].at[r, :], place(t, c, off, n), xsem.at[q], ysem.at[q], sib)
            keep.start()
            give.start()
            stores.append(keep)
            sends.append(give)
        for t in range(nt):
            if out_rows[t] > SHARD_ROWS[t]:
                zeros[...] = jnp.zeros_like(zeros)
                fill = pltpu.make_async_copy(zeros.at[pl.ds(0, out_rows[t] - SHARD_ROWS[t]), :],
                                             outs[t].at[pl.ds(SHARD_ROWS[t], out_rows[t] - SHARD_ROWS[t]), :], zsem.at[0])
                fill.start()
                stores.append(fill)
        for q, (t, off, n) in enumerate(pieces):
            got = place(t, 1 - c, off, n)
            _remote(got, got, xsem.at[q], ysem.at[q], sib).wait_recv()
        for cp in sends:
            cp.wait_send()
        for cp in stores:
            cp.wait()

    dma = pltpu.SemaphoreType.DMA
    out_rows = [-(-SHARD_ROWS[t] // 128) * 128 for t in range(nt)]
    assert sum(r > s for r, s in zip(out_rows, SHARD_ROWS)) <= 1
    return pl.pallas_call(
        body, name="grad_finish", in_specs=[HBM_SPEC] * (2 * nt), out_specs=[HBM_SPEC] * nt,
        out_shape=[jax.ShapeDtypeStruct((out_rows[t], g.shape[1]), F32) for t, g in enumerate(grads)],
        scratch_shapes=([pltpu.VMEM((N_DEV, SHARD_ROWS[t] // 2, g.shape[1]), g.dtype) for t, g in enumerate(grads)]
                        + [pltpu.VMEM((SHARD_ROWS[t] // 2, g.shape[1]), F32) for t, g in enumerate(grads)]
                        + [dma((N_DEV * nt,)), dma((npc,)), dma((npc,)), dma((npc,)), pltpu.VMEM((128, grads[0].shape[1]), F32), dma((1,))]),
        compiler_params=_cp())(*grads, *zones)


def _adam_math(w, g, m, v):
    m = ADAM_B1 * m + (1.0 - ADAM_B1) * g
    v = ADAM_B2 * v + (1.0 - ADAM_B2) * (g * g)
    m_hat = m / (1.0 - ADAM_B1 ** ADAM_STEP)
    v_hat = v / (1.0 - ADAM_B2 ** ADAM_STEP)
    return -ADAM_LR * (m_hat / (jnp.sqrt(v_hat) + ADAM_EPS) + ADAM_WD * w), m, v


def _adamw(w, g, m, v, rb, name):
    rows, cols = w.shape

    def body(w_ref, g_ref, m_ref, v_ref, d_ref, nm_ref, nv_ref):
        d_ref[...], nm_ref[...], nv_ref[...] = _adam_math(w_ref[...], g_ref[...], m_ref[...], v_ref[...])

    spec = pl.BlockSpec((rb, cols), lambda i: (i, 0))
    return pl.pallas_call(body, grid=(rows // rb,), name=name, in_specs=[spec] * 4, out_specs=[spec] * 3,
                          out_shape=[jax.ShapeDtypeStruct(w.shape, F32)] * 3, compiler_params=_cp("parallel"))(w, g, m, v)


def _transpose_cast(w):
    rows, cols = w.shape
    nb = pl.cdiv(cols, 128)

    def body(x_ref, o_ref):
        col = pl.program_id(0) * 128 + _lane(x_ref.shape)
        o_ref[...] = jnp.transpose(jnp.where(col < cols, x_ref[...], 0.0)).astype(o_ref.dtype)

    return pl.pallas_call(body, grid=(nb,), name="w_in_transpose", in_specs=[pl.BlockSpec((rows, 128), lambda i: (0, i))],
                          out_specs=pl.BlockSpec((128, rows), lambda i: (i, 0)),
                          out_shape=jax.ShapeDtypeStruct((nb * 128, rows), MXU_DTYPE), compiler_params=_cp("parallel"))(w)


def _adamw_t(w, g_t, m, v, name):
    rows, cols = w.shape
    nb = pl.cdiv(cols, 128)
    assert g_t.shape == (nb * 128, rows)

    def body(w_ref, g_ref, m_ref, v_ref, go_ref, d_ref, nm_ref, nv_ref):
        g = jnp.transpose(g_ref[...])
        go_ref[...] = g
        d_ref[...], nm_ref[...], nv_ref[...] = _adam_math(w_ref[...], g, m_ref[...], v_ref[...])

    col = pl.BlockSpec((rows, 128), lambda i: (0, i))
    return pl.pallas_call(body, grid=(nb,), name=name, in_specs=[col, pl.BlockSpec((128, rows), lambda i: (i, 0)), col, col],
                          out_specs=[col] * 4, out_shape=[jax.ShapeDtypeStruct(w.shape, F32)] * 4,
                          compiler_params=_cp("parallel"))(w, g_t, m, v)


PACK = (("b_ada", 3 * D_MODEL), ("g_pre", D_MODEL), ("g_post", D_MODEL), ("gn_g", 512), ("qn_g", 64), ("kn_g", 64),
        ("w_dec_f", 4), ("w_dec_b", 4), ("loss", 1))
PACK_OFFSETS = {}
PACK_WIDTH = 0
for _name, _n in PACK:
    PACK_OFFSETS[_name] = PACK_WIDTH
    PACK_WIDTH += -(-_n // 128) * 128
SMALL_PARAMS = tuple(name for name, _ in PACK if name != "loss")


def _pack(parts):
    cols = []
    for name, n in PACK:
        pad = -(-n // 128) * 128 - n
        cols.append(parts[name].reshape(1, n).astype(F32))
        if pad:
            cols.append(jnp.zeros((1, pad), F32))
    return jnp.concatenate(cols, axis=1)


def _small_reduce(vec, cs, given):
    ncol = 3 * D_MODEL // N_CHIP
    sizes = dict(PACK)
    np_ = len(SMALL_PARAMS)

    def body(*refs):
        vec_ref, cs_ref = refs[:2]
        wmv = refs[2:2 + 3 * np_]
        outs = refs[2 + 3 * np_:2 + 7 * np_]
        loss_ref, gwa_ref, gat, ssem, rsem = refs[2 + 7 * np_:]
        me = _dev_index(_coords())
        chip = me // 2
        gat[me] = vec_ref[...]
        sends = []
        for k in range(1, N_DEV):
            cp = _remote(vec_ref, gat.at[me], ssem.at[k - 1], rsem.at[k - 1], _peer(k))
            cp.start()
            sends.append(cp)
        for k in range(1, N_DEV):
            slot = gat.at[_dev_index(_peer(k))]
            _remote(slot, slot, ssem.at[k - 1], rsem.at[k - 1], _peer(k)).wait_recv()
        rows = [gat[d] for d in range(N_DEV)]
        tot = rows[0]
        for d in range(1, N_DEV):
            tot = tot + rows[d]
        for i, name in enumerate(SMALL_PARAMS):
            off, n = PACK_OFFSETS[name], sizes[name]
            g = tot[:, off:off + n]
            w_ref, m_ref, v_ref = wmv[3 * i:3 * i + 3]
            outs[i][...] = g
            outs[np_ + i][...], outs[2 * np_ + i][...], outs[3 * np_ + i][...] = _adam_math(w_ref[...], g, m_ref[...], v_ref[...])
        loss_ref[...] = tot[:, PACK_OFFSETS["loss"]:PACK_OFFSETS["loss"] + 1]
        cs = cs_ref[...]
        ca_t = jnp.transpose(jnp.concatenate([cs * _sigmoid(cs), jnp.zeros((128 - N_DEV, D_MODEL), F32)], axis=0))
        dm = jnp.concatenate(rows + [jnp.zeros((128 - N_DEV, PACK_WIDTH), F32)], axis=0)
        for jj in range(N_CHIP):
            @pl.when(chip == jj)
            def _():
                gwa_ref[...] = _nn(ca_t, dm[:, ncol * jj:ncol * (jj + 1)], precision=HIGHEST)
        for cp in sends:
            cp.wait_send()

    flat = [a for name in SMALL_PARAMS for a in given[name]]
    shapes = [jax.ShapeDtypeStruct((1, sizes[name]), F32) for name in SMALL_PARAMS]
    res = pl.pallas_call(
        body, name="small_reduce", in_specs=[VMEM_SPEC] * (2 + 3 * np_), out_specs=[VMEM_SPEC] * (4 * np_ + 2),
        out_shape=shapes * 4 + [jax.ShapeDtypeStruct((1, 1), F32), jax.ShapeDtypeStruct((D_MODEL, ncol), F32)],
        scratch_shapes=[pltpu.VMEM((N_DEV, 1, PACK_WIDTH), F32), pltpu.SemaphoreType.DMA((7,)), pltpu.SemaphoreType.DMA((7,))],
        compiler_params=_cp())(vec, cs, *flat)
    parts = [dict(zip(SMALL_PARAMS, res[k * np_:(k + 1) * np_])) for k in range(4)]
    return parts, res[4 * np_], res[4 * np_ + 1]


def kernel(x, c, w_ada, b_ada, g_pre, w_in, qn_g, kn_g, w_dec_f, w_dec_b, gn_g, w_pa, w_pr, w_out, g_post, loss_target, m_w_ada, m_b_ada, m_g_pre, m_w_in, m_qn_g, m_kn_g, m_w_dec_f, m_w_dec_b, m_gn_g, m_w_pa, m_w_pr, m_w_out, m_g_post, v_w_ada, v_b_ada, v_g_pre, v_w_in, v_qn_g, v_kn_g, v_w_dec_f, v_w_dec_b, v_gn_g, v_w_pa, v_w_pr, v_w_out, v_g_post):
    shards = (_transpose_cast(w_in[0]), jnp.concatenate([w_pa[0].T, w_pr[0].T], axis=1).astype(MXU_DTYPE),
              w_out[0].astype(MXU_DTYPE))
    win_t, wpt, wout = _weight_gather(shards)
    cs, mod = _mod_exchange(c, w_ada[0], b_ada.reshape(N_CHIP, 3 * D_MODEL // N_CHIP))

    kinds = {"early": (1, 2), "late": (0,)}
    started = {}

    def send(name, arrays):
        started[name] = _scatter_start(arrays, kinds[name], "grad_scatter_" + name)
        return started[name][-1]

    grad_x, _, _, _, small = _local_step(x[0], loss_target[0], mod, g_pre, qn_g, kn_g, w_dec_f, w_dec_b,
                                         gn_g, g_post, win_t, wpt, wout, send=send)
    small = dict(small)
    small["b_ada"] = small.pop("dmod")
    given = dict(b_ada=(b_ada, m_b_ada, v_b_ada), g_pre=(g_pre, m_g_pre, v_g_pre), g_post=(g_post, m_g_post, v_g_post),
                 gn_g=(gn_g, m_gn_g, v_gn_g), qn_g=(qn_g, m_qn_g, v_qn_g), kn_g=(kn_g, m_kn_g, v_kn_g),
                 w_dec_f=(w_dec_f, m_w_dec_f, v_w_dec_f), w_dec_b=(w_dec_b, m_w_dec_b, v_w_dec_b))
    (grads, deltas, new_m, new_v), loss, g_w_ada = _small_reduce(_pack(small), cs.reshape(N_DEV, D_MODEL), given)
    d, nm, nv = _adamw(w_ada[0], g_w_ada, m_w_ada[0], v_w_ada[0], 256, "adamw_w_ada")
    grads["w_ada"], deltas["w_ada"], new_m["w_ada"], new_v["w_ada"] = g_w_ada[None], d[None], nm[None], nv[None]

    (g_pt, g_out), (z_pt, z_out) = _scatter_wait(started["early"], kinds["early"], d, "grad_scatter_early_wait")
    (g_in_t,), (z_in,) = _scatter_wait(started["late"], kinds["late"], d, "grad_scatter_late_wait")
    full = _grad_finish((g_in_t, g_pt, g_out), (z_in, z_pt, z_out))
    g, d, nm, nv = _adamw_t(w_in[0], full[0], m_w_in[0], v_w_in[0], "adamw_w_in")
    grads["w_in"], deltas["w_in"], new_m["w_in"], new_v["w_in"] = g[None], d[None], nm[None], nv[None]
    for name, (w, m, v, g) in {"w_pa": (w_pa, m_w_pa, v_w_pa, full[1][:, :512].T), "w_pr": (w_pr, m_w_pr, v_w_pr, full[1][:, 512:].T),
                               "w_out": (w_out, m_w_out, v_w_out, full[2])}.items():
        d, nm, nv = _adamw(w[0], g, m[0], v[0], 256, "adamw_" + name)
        grads[name], deltas[name], new_m[name], new_v[name] = g[None], d[None], nm[None], nv[None]
    order = ("w_ada", "b_ada", "g_pre", "w_in", "qn_g", "kn_g", "w_dec_f", "w_dec_b", "gn_g", "w_pa", "w_pr", "w_out", "g_post")
    return (loss[0, 0], grad_x[None], *[grads[n] for n in order], *[deltas[n] for n in order],
            *[new_m[n] for n in order], *[new_v[n] for n in order])
```

```python
import functools

import jax
import jax.numpy as jnp
import numpy as np
from jax import lax
from jax.experimental import pallas as pl
from jax.experimental.pallas import tpu as pltpu

F32 = jnp.float32
BF16 = jnp.bfloat16
MXU_DTYPE = jnp.bfloat16

D_MODEL = 1024
GRID_W = 64
HEAD_DIM = 64
ROPE_THETA = 10000.0
EPS = 1e-6
RET_HEADS = 4
RET_CHUNK = 256
IN_WIDTH = 4864
QA, KA, VA, ZA, QR, KR, VR, ZR, GL = 0, 512, 640, 768, 1280, 1536, 1792, 2304, 2816

ADAM_LR, ADAM_B1, ADAM_B2, ADAM_EPS, ADAM_WD, ADAM_STEP = 0.001, 0.9, 0.999, 1e-08, 0.01, 10

VMEM_LIMIT = 56 * 1024 * 1024
MESH = pl.DeviceIdType.MESH
HIGHEST = lax.Precision.HIGHEST
LOG2E = 1.4426950408889634
LN2 = 0.6931471805599453


def _cp(*sem, **kw):
    if sem:
        kw["dimension_semantics"] = sem
    return pltpu.CompilerParams(vmem_limit_bytes=VMEM_LIMIT, **kw)


def _nn(a, b, **kw):
    return lax.dot_general(a, b, (((1,), (0,)), ((), ())), preferred_element_type=F32, **kw)


def _nt(a, b):
    return lax.dot_general(a, b, (((1,), (1,)), ((), ())), preferred_element_type=F32)


def _tn(a, b):
    return lax.dot_general(a, b, (((0,), (0,)), ((), ())), preferred_element_type=F32)


def _mx(x):
    return x.astype(MXU_DTYPE)


def _lane(shape):
    return lax.broadcasted_iota(jnp.int32, shape, 1)


def _sigmoid(z):
    return 1.0 / (1.0 + jnp.exp(-z))


def _rowsum128(x):
    s = jnp.sum(x, axis=0, keepdims=True)
    out = s[:, 0:128]
    for k in range(1, x.shape[1] // 128):
        out = out + s[:, 128 * k:128 * (k + 1)]
    return out


def _swap16(x):
    return jnp.where((_lane(x.shape) & 16) == 0, pltpu.roll(x, 112, 1), pltpu.roll(x, 16, 1))


def _rope(x, cos, sin):
    return x * cos + _swap16(x) * sin


def _rope_t(d, cos, sin):
    return d * cos + _swap16(d * sin)


def _blockdiag64():
    r = lax.broadcasted_iota(jnp.int32, (128, 128), 0) // 64
    c = lax.broadcasted_iota(jnp.int32, (128, 128), 1) // 64
    return (r == c).astype(BF16)


def _seg64(x, m):
    hi = x.astype(BF16)
    lo = (x - hi.astype(F32)).astype(BF16)
    return _nn(hi, m) + _nn(lo, m)


def _rope_tables(seq):
    t = np.arange(seq)
    row = (t // GRID_W).astype(np.float32)
    col = (t % GRID_W).astype(np.float32)
    half = HEAD_DIM // 2
    inv_freq = (np.float32(ROPE_THETA) ** (-np.arange(0, half, 2, dtype=np.float32) / np.float32(half))).astype(np.float32)
    ar = (row[:, None] * inv_freq[None, :]).astype(np.float32).astype(np.float64)
    ac = (col[:, None] * inv_freq[None, :]).astype(np.float32).astype(np.float64)
    cr, sr, cc, sc = np.cos(ar), np.sin(ar), np.cos(ac), np.sin(ac)
    cos64 = np.concatenate([cr, cr, cc, cc], axis=1)
    sin64 = np.concatenate([-sr, sr, -sc, sc], axis=1)
    return jnp.asarray(np.tile(cos64, (1, 2)), F32), jnp.asarray(np.tile(sin64, (1, 2)), F32)


def _prenorm(x, mod, g_pre):
    seq = x.shape[0]
    bs = 512

    def body(x_ref, mod_ref, g_ref, h_ref):
        xv = x_ref[...]
        r = lax.rsqrt(jnp.mean(xv * xv, axis=-1, keepdims=True) + EPS)
        shift = mod_ref[:, 0:D_MODEL]
        scale = mod_ref[:, D_MODEL:2 * D_MODEL]
        h_ref[...] = ((xv * r) * g_ref[...] * (1.0 + scale) + shift).astype(h_ref.dtype)

    return pl.pallas_call(
        body, grid=(seq // bs,), name="prenorm",
        in_specs=[pl.BlockSpec((bs, D_MODEL), lambda i: (i, 0)), pl.BlockSpec((1, 3 * D_MODEL), lambda i: (0, 0)),
                  pl.BlockSpec((1, D_MODEL), lambda i: (0, 0))],
        out_specs=pl.BlockSpec((bs, D_MODEL), lambda i: (i, 0)),
        out_shape=jax.ShapeDtypeStruct((seq, D_MODEL), MXU_DTYPE), compiler_params=_cp("parallel"))(x, mod, g_pre)


def _prenorm_bwd(x, dh, dout, mod, g_pre):
    seq = x.shape[0]
    bs = 512

    def body(x_ref, dh_ref, dout_ref, mod_ref, g_ref, gx_ref, dshift_ref, dscale_ref, dg_ref):
        @pl.when(pl.program_id(0) == 0)
        def _():
            dshift_ref[...] = jnp.zeros_like(dshift_ref)
            dscale_ref[...] = jnp.zeros_like(dscale_ref)
            dg_ref[...] = jnp.zeros_like(dg_ref)

        xv = x_ref[...]
        dh = dh_ref[...]
        g = g_ref[...]
        r = lax.rsqrt(jnp.mean(xv * xv, axis=-1, keepdims=True) + EPS)
        xn = xv * r
        scale = mod_ref[:, D_MODEL:2 * D_MODEL]
        dshift_ref[...] += jnp.sum(dh, axis=0, keepdims=True)
        dscale_ref[...] += jnp.sum(dh * (xn * g), axis=0, keepdims=True)
        da = dh * (1.0 + scale)
        dg_ref[...] += jnp.sum(da * xn, axis=0, keepdims=True)
        dxn = da * g
        dx = r * (dxn - xn * jnp.mean(dxn * xn, axis=-1, keepdims=True))
        gx_ref[...] = dout_ref[...] + dx

    row = pl.BlockSpec((bs, D_MODEL), lambda i: (i, 0))
    vec = pl.BlockSpec((1, D_MODEL), lambda i: (0, 0))
    return pl.pallas_call(
        body, grid=(seq // bs,), name="prenorm_bwd",
        in_specs=[row, row, row, pl.BlockSpec((1, 3 * D_MODEL), lambda i: (0, 0)), vec],
        out_specs=[row, vec, vec, vec],
        out_shape=[jax.ShapeDtypeStruct((seq, D_MODEL), F32)] + [jax.ShapeDtypeStruct((1, D_MODEL), F32)] * 3,
        compiler_params=_cp("arbitrary"))(x, dh, dout, mod, g_pre)


def _dep_args(dep, grid_rank):
    if dep is None:
        return [], []
    return [dep], [pl.BlockSpec(dep.shape, lambda *_: (0,) * dep.ndim)]


def _matmul(a, b, *, ta, tb, tm, tn, out_dtype, name, dep=None):
    kdim = a.shape[0] if ta else a.shape[1]
    m = a.shape[1] if ta else a.shape[0]
    n = b.shape[0] if tb else b.shape[1]
    assert m % tm == 0 and n % tn == 0
    deps, dep_specs = _dep_args(dep, 2)

    def body(a_ref, b_ref, *rest):
        o_ref = rest[-1]
        dims = (((0 if ta else 1,), (1 if tb else 0,)), ((), ()))
        o_ref[...] = lax.dot_general(a_ref[...], b_ref[...], dims, preferred_element_type=F32).astype(o_ref.dtype)

    a_spec = pl.BlockSpec((kdim, tm), lambda j, i: (0, i)) if ta else pl.BlockSpec((tm, kdim), lambda j, i: (i, 0))
    b_spec = pl.BlockSpec((tn, kdim), lambda j, i: (j, 0)) if tb else pl.BlockSpec((kdim, tn), lambda j, i: (0, j))
    return pl.pallas_call(
        body, grid=(n // tn, m // tm), name=name, in_specs=[a_spec, b_spec] + dep_specs,
        out_specs=pl.BlockSpec((tm, tn), lambda j, i: (i, j)),
        out_shape=jax.ShapeDtypeStruct((m, n), out_dtype), compiler_params=_cp("parallel", "parallel"))(a, b, *deps)


def _prep(p, cos, sin, qg2, kg2):
    seq = p.shape[0]
    bs = 512

    def body(p_ref, cos_ref, sin_ref, qg_ref, kg_ref, qt_ref, kd_ref, vd_ref, rq_ref, rk_ref, rv_ref):
        m = _blockdiag64()
        cs, sn = cos_ref[...], sin_ref[...]
        lo = _lane((bs, 128)) < 64
        for pr in range(4):
            q = p_ref[:, QA + 128 * pr:QA + 128 * (pr + 1)]
            r = lax.rsqrt(_seg64(q * q, m) * (1.0 / 64) + EPS)
            qt_ref[:, 128 * pr:128 * (pr + 1)] = (_rope(q * r * qg_ref[...], cs, sn) * (0.125 * LOG2E)).astype(qt_ref.dtype)
        k = p_ref[:, KA:KA + 128]
        r = lax.rsqrt(_seg64(k * k, m) * (1.0 / 64) + EPS)
        kr = _rope(k * r * kg_ref[...], cs, sn)
        ksw = pltpu.roll(kr, 64, 1)
        kd_ref[0] = jnp.where(lo, kr, ksw).astype(kd_ref.dtype)
        kd_ref[1] = jnp.where(lo, ksw, kr).astype(kd_ref.dtype)
        v = p_ref[:, VA:VA + 128]
        vsw = pltpu.roll(v, 64, 1)
        vd_ref[0] = jnp.where(lo, v, vsw).astype(vd_ref.dtype)
        vd_ref[1] = jnp.where(lo, vsw, v).astype(vd_ref.dtype)
        for pr in range(2):
            sl = slice(128 * pr, 128 * (pr + 1))
            rq_ref[:, sl] = _rope(p_ref[:, QR + 128 * pr:QR + 128 * (pr + 1)], cs, sn).astype(rq_ref.dtype)
            rk_ref[:, sl] = (_rope(p_ref[:, KR + 128 * pr:KR + 128 * (pr + 1)], cs, sn) * 0.125).astype(rk_ref.dtype)
        rv_ref[...] = p_ref[:, VR:VR + 512].astype(rv_ref.dtype)

    tab = pl.BlockSpec((bs, 128), lambda i: (i, 0))
    gsp = pl.BlockSpec((1, 128), lambda i: (0, 0))
    dup = pl.BlockSpec((2, bs, 128), lambda i: (0, i, 0))
    return pl.pallas_call(
        body, grid=(seq // bs,), name="mixer_prep",
        in_specs=[pl.BlockSpec((bs, ZR), lambda i: (i, 0)), tab, tab, gsp, gsp],
        out_specs=[pl.BlockSpec((bs, 512), lambda i: (i, 0)), dup, dup, pl.BlockSpec((bs, 256), lambda i: (i, 0)),
                   pl.BlockSpec((bs, 256), lambda i: (i, 0)), pl.BlockSpec((bs, 512), lambda i: (i, 0))],
        out_shape=[jax.ShapeDtypeStruct((seq, 512), MXU_DTYPE), jax.ShapeDtypeStruct((2, seq, 128), MXU_DTYPE),
                   jax.ShapeDtypeStruct((2, seq, 128), MXU_DTYPE), jax.ShapeDtypeStruct((seq, 256), MXU_DTYPE),
                   jax.ShapeDtypeStruct((seq, 256), MXU_DTYPE), jax.ShapeDtypeStruct((seq, 512), MXU_DTYPE)],
        compiler_params=_cp("parallel"))(p, cos, sin, qg2, kg2)


def _halves(kd):
    lo = _lane(kd.shape) < 64
    z = jnp.zeros_like(kd)
    return jnp.concatenate([jnp.where(lo, kd, z), jnp.where(lo, z, kd)], axis=0)


def _attn_fwd(qt, kd, vd):
    seq = qt.shape[0]
    bq = bk = 512
    nk = seq // bk

    def body(q_ref, k_ref, v_ref, o_ref, lse_ref, m_scr, l_scr, acc):
        j = pl.program_id(1)
        m_scr[...] = jnp.full_like(m_scr, -jnp.inf)
        l_scr[...] = jnp.zeros_like(l_scr)
        acc[...] = jnp.zeros_like(acc)
        lo = _lane((bq, 128)) < 64

        def kv_block(n, carry):
            rows = pl.ds(pl.multiple_of(n * bk, bk), bk)
            k2, v2 = _halves(k_ref[rows, :]), _halves(v_ref[rows, :])
            for pr in range(2):
                s2 = _nt(q_ref[:, 128 * pr:128 * (pr + 1)], k2)
                ps, alphas = [], []
                for e in range(2):
                    g = 2 * pr + e
                    s = s2[:, e * bk:(e + 1) * bk]
                    m_prev = m_scr[g]
                    m_next = jnp.maximum(m_prev, jnp.max(s, axis=1, keepdims=True))
                    alpha = jnp.exp2(m_prev - m_next)
                    pe = jnp.exp2(s - jnp.tile(m_next, (1, bk // 128)))
                    l_scr[g] = alpha * l_scr[g] + jnp.sum(pe, axis=1, keepdims=True)
                    m_scr[g] = m_next
                    ps.append(_mx(pe))
                    alphas.append(alpha)
                o2 = _nn(jnp.concatenate(ps, axis=1), v2)
                sl = slice(128 * pr, 128 * (pr + 1))
                acc[:, sl] = acc[:, sl] * jnp.where(lo, alphas[0], alphas[1]) + o2
            return carry

        lax.fori_loop(0, nk, kv_block, 0)
        for pr in range(2):
            sl = slice(128 * pr, 128 * (pr + 1))
            o_ref[:, sl] = acc[:, sl] / jnp.where(lo, l_scr[2 * pr], l_scr[2 * pr + 1])
        for g in range(4):
            lse = jnp.transpose(m_scr[g] + jnp.log2(l_scr[g]))
            lse_ref[pl.ds(4 * j + g, 1), :] = lse[0:1, :]

    return pl.pallas_call(
        body, grid=(seq // bq, 2), name="attn_fwd",
        in_specs=[pl.BlockSpec((bq, 256), lambda i, j: (i, j)), pl.BlockSpec((None, seq, 128), lambda i, j: (j, 0, 0)),
                  pl.BlockSpec((None, seq, 128), lambda i, j: (j, 0, 0))],
        out_specs=[pl.BlockSpec((bq, 256), lambda i, j: (i, j)), pl.BlockSpec((8, bq), lambda i, j: (0, i))],
        out_shape=[jax.ShapeDtypeStruct((seq, 512), F32), jax.ShapeDtypeStruct((8, seq), F32)],
        scratch_shapes=[pltpu.VMEM((4, bq, 128), F32), pltpu.VMEM((4, bq, 128), F32), pltpu.VMEM((bq, 256), F32)],
        compiler_params=_cp("parallel", "arbitrary"))(qt, kd, vd)


def _attn_bwd(qt, kd, vd, do, lse, delta, dep=None):
    seq = qt.shape[0]
    bq = bk = 512
    nq, nk = seq // bq, seq // bk
    deps, dep_specs = _dep_args(dep, 3)

    def body(q_ref, do_ref, k_ref, v_ref, lse_ref, del_ref, *rest):
        dq_ref, dk_ref, dv_ref = rest[-3:]
        j, i = pl.program_id(0), pl.program_id(1)
        dq_ref[...] = jnp.zeros_like(dq_ref)

        @pl.when(i == 0)
        def _():
            dk_ref[...] = jnp.zeros_like(dk_ref)
            dv_ref[...] = jnp.zeros_like(dv_ref)

        lo = _lane((bk, 128)) < 64
        big = lambda r: jnp.concatenate([jnp.broadcast_to(r[0], (bk, bq)), jnp.broadcast_to(r[1], (bk, bq))], axis=0)

        def kv_block(n, carry):
            rows = pl.ds(pl.multiple_of(n * bk, bk), bk)
            k2, v2 = _halves(k_ref[rows, :]), _halves(v_ref[rows, :])
            for pr in range(2):
                sl = slice(128 * pr, 128 * (pr + 1))
                qp, dop = q_ref[:, sl], do_ref[:, sl]
                s_t = _nt(k2, qp)
                dp_t = _nt(v2, dop)
                ls = [lse_ref[pl.ds(4 * j + 2 * pr + e, 1), :] for e in range(2)]
                dl = [del_ref[pl.ds(4 * j + 2 * pr + e, 1), :] for e in range(2)]
                p_t = jnp.exp2(s_t - big(ls))
                ds_t = _mx(p_t * (dp_t - big(dl)))
                rv = _nn(_mx(p_t), dop)
                rk = _nn(ds_t, qp) * LN2
                dv_ref[rows, :] += jnp.where(lo, rv[:bk], rv[bk:])
                dk_ref[rows, :] += jnp.where(lo, rk[:bk], rk[bk:])
                dq_ref[:, sl] += _tn(ds_t, k2)
            return carry

        lax.fori_loop(0, nk, kv_block, 0)

    return pl.pallas_call(
        body, grid=(2, nq), name="attn_bwd",
        in_specs=[pl.BlockSpec((bq, 256), lambda j, i: (i, j)), pl.BlockSpec((bq, 256), lambda j, i: (i, j)),
                  pl.BlockSpec((None, seq, 128), lambda j, i: (j, 0, 0)), pl.BlockSpec((None, seq, 128), lambda j, i: (j, 0, 0)),
                  pl.BlockSpec((8, bq), lambda j, i: (0, i)), pl.BlockSpec((8, bq), lambda j, i: (0, i))] + dep_specs,
        out_specs=[pl.BlockSpec((bq, 256), lambda j, i: (i, j)), pl.BlockSpec((None, seq, 128), lambda j, i: (j, 0, 0)),
                   pl.BlockSpec((None, seq, 128), lambda j, i: (j, 0, 0))],
        out_shape=[jax.ShapeDtypeStruct((seq, 512), F32), jax.ShapeDtypeStruct((2, seq, 128), F32),
                   jax.ShapeDtypeStruct((2, seq, 128), F32)],
        compiler_params=_cp("arbitrary", "arbitrary"))(qt, do, kd, vd, lse, delta, *deps)


def _ret_tables(lg_f, lg_b, c):
    idx = jnp.arange(c, dtype=F32)
    diff = idx[:, None] - idx[None, :]
    a, b = lg_f[:, None, None], lg_b[:, None, None]
    low, up = (diff >= 0)[None], (diff < 0)[None]
    dmat = jnp.where(low, jnp.exp(a * jnp.maximum(diff, 0.0)[None]), jnp.exp(b * jnp.maximum(-diff, 0.0)[None]))
    dda = jnp.where(low, diff[None] * jnp.exp(a * jnp.maximum(diff, 0.0)[None]), 0.0)
    ddb = jnp.where(up, -diff[None] * jnp.exp(b * jnp.maximum(-diff, 0.0)[None]), 0.0)
    rep = lambda w: jnp.broadcast_to(w[:, :, None], (RET_HEADS, c, 128))
    wqf = rep(jnp.exp(lg_f[:, None] * (idx + 1.0)[None]))
    wqb = rep(jnp.exp(lg_b[:, None] * (c - idx)[None]))
    wkf = rep(jnp.exp(lg_f[:, None] * (c - 1.0 - idx)[None]))
    wkb = rep(jnp.exp(lg_b[:, None] * idx[None]))
    cdf, cdb = jnp.exp(lg_f * c), jnp.exp(lg_b * c)
    dec_fb = jnp.broadcast_to(jnp.concatenate([cdf, cdb])[:, None], (8, 128))
    dec_bf = jnp.broadcast_to(jnp.concatenate([cdb, cdf])[:, None], (8, 128))
    return dict(dmat=dmat, dda=dda, ddb=ddb, wqf=wqf, wqb=wqb, wkf=wkf, wkb=wkb, dec_fb=dec_fb, dec_bf=dec_bf)


def _ret_states(xk, yv, w_fwd, w_rev, dec, name):
    seq = xk.shape[0]
    c = RET_CHUNK
    nc = seq // c

    def body(xf_ref, yf_ref, xr_ref, yr_ref, wf_ref, wr_ref, dec_ref, sf_ref, sr_ref, st_f, st_r):
        @pl.when(pl.program_id(0) == 0)
        def _():
            st_f[...] = jnp.zeros_like(st_f)
            st_r[...] = jnp.zeros_like(st_r)

        lane = _lane((c, 128))
        for h in range(RET_HEADS):
            pr, e = h // 2, h % 2
            half = (lane < 64) if e == 0 else (lane >= 64)
            for x_ref, y_ref, w_ref, s_ref, st, drow in ((xf_ref, yf_ref, wf_ref, sf_ref, st_f, h),
                                                        (xr_ref, yr_ref, wr_ref, sr_ref, st_r, 4 + h)):
                s_ref[h] = st[h].astype(s_ref.dtype)
                xm = jnp.where(half, x_ref[:, 128 * pr:128 * (pr + 1)].astype(F32) * w_ref[h], 0.0)
                st[h] = st[h] * dec_ref[drow:drow + 1, :] + _tn(_mx(xm), _mx(y_ref[:, 128 * h:128 * (h + 1)]))

    xs = lambda f: pl.BlockSpec((c, 256), f)
    ys = lambda f: pl.BlockSpec((c, 512), f)
    fwd, rev = (lambda n: (n, 0)), (lambda n: (nc - 1 - n, 0))
    wsp = pl.BlockSpec((RET_HEADS, c, 128), lambda n: (0, 0, 0))
    return pl.pallas_call(
        body, grid=(nc,), name=name,
        in_specs=[xs(fwd), ys(fwd), xs(rev), ys(rev), wsp, wsp, pl.BlockSpec((8, 128), lambda n: (0, 0))],
        out_specs=[pl.BlockSpec((None, RET_HEADS, 128, 128), lambda n: (n, 0, 0, 0)),
                   pl.BlockSpec((None, RET_HEADS, 128, 128), lambda n: (nc - 1 - n, 0, 0, 0))],
        out_shape=[jax.ShapeDtypeStruct((nc, RET_HEADS, 128, 128), MXU_DTYPE)] * 2,
        scratch_shapes=[pltpu.VMEM((RET_HEADS, 128, 128), F32), pltpu.VMEM((RET_HEADS, 128, 128), F32)],
        compiler_params=_cp("arbitrary"))(xk, yv, xk, yv, w_fwd, w_rev, dec)


def _ret_fwd(rq, rk, rv, rf, rb, tb):
    seq = rq.shape[0]
    c = RET_CHUNK

    def body(q_ref, k_ref, v_ref, rf_ref, rb_ref, d_ref, wqf_ref, wqb_ref, o_ref):
        lane = _lane((c, 128))
        for h in range(RET_HEADS):
            pr, e = h // 2, h % 2
            half = (lane < 64) if e == 0 else (lane >= 64)
            sl = slice(128 * pr, 128 * (pr + 1))
            qp = q_ref[:, sl]
            km = jnp.where(half, k_ref[:, sl], jnp.zeros_like(k_ref[:, sl]))
            sd = _mx(_nt(qp, km) * d_ref[h])
            qf = qp.astype(F32)
            o = _nn(sd, v_ref[:, 128 * h:128 * (h + 1)])
            o = o + _nn(_mx(qf * wqf_ref[h]), rf_ref[h]) + _nn(_mx(qf * wqb_ref[h]), rb_ref[h])
            o_ref[:, 128 * h:128 * (h + 1)] = o

    st = pl.BlockSpec((None, RET_HEADS, 128, 128), lambda n: (n, 0, 0, 0))
    wsp = pl.BlockSpec((RET_HEADS, c, 128), lambda n: (0, 0, 0))
    return pl.pallas_call(
        body, grid=(seq // c,), name="ret_fwd",
        in_specs=[pl.BlockSpec((c, 256), lambda n: (n, 0)), pl.BlockSpec((c, 256), lambda n: (n, 0)),
                  pl.BlockSpec((c, 512), lambda n: (n, 0)), st, st, pl.BlockSpec((RET_HEADS, c, c), lambda n: (0, 0, 0)), wsp, wsp],
        out_specs=pl.BlockSpec((c, 512), lambda n: (n, 0)),
        out_shape=jax.ShapeDtypeStruct((seq, 512), F32), compiler_params=_cp("parallel"))(
            rq, rk, rv, rf, rb, tb["dmat"], tb["wqf"], tb["wqb"])


def _ret_bwd(rq, rk, rv, do, rf, rb, hf, hb, tb):
    seq = rq.shape[0]
    c = RET_CHUNK

    def body(q_ref, k_ref, v_ref, do_ref, rf_ref, rb_ref, hf_ref, hb_ref, d_ref, dda_ref, ddb_ref,
             wqf_ref, wqb_ref, wkf_ref, wkb_ref, cdec_ref, dq_ref, dk_ref, dv_ref, dlg_ref):
        @pl.when(pl.program_id(0) == 0)
        def _():
            dlg_ref[...] = jnp.zeros_like(dlg_ref)

        lane = _lane((c, 128))
        pos = lax.broadcasted_iota(jnp.int32, (c, 128), 0).astype(F32)
        for pr in range(2):
            sl = slice(128 * pr, 128 * (pr + 1))
            qp, kp = q_ref[:, sl], k_ref[:, sl]
            qf, kf = qp.astype(F32), kp.astype(F32)
            dq_acc = jnp.zeros((c, 128), F32)
            dk_acc = jnp.zeros((c, 128), F32)
            for e in range(2):
                h = 2 * pr + e
                half = (lane < 64) if e == 0 else (lane >= 64)
                hs = slice(128 * h, 128 * (h + 1))
                km = jnp.where(half, kp, jnp.zeros_like(kp))
                kmf = km.astype(F32)
                vh, doh = v_ref[:, hs], do_ref[:, hs]
                rfh, rbh, hfh, hbh = rf_ref[h], rb_ref[h], hf_ref[h], hb_ref[h]
                s = _nt(qp, km)
                dpm = _nt(doh, vh)
                ds_b = _mx(dpm * d_ref[h])
                sd_b = _mx(s * d_ref[h])
                dq_if = wqf_ref[h] * _nt(doh, rfh)
                dq_ib = wqb_ref[h] * _nt(doh, rbh)
                dq_acc = dq_acc + _nn(ds_b, km) + dq_if + dq_ib
                dk_sf = wkf_ref[h] * _nt(vh, hfh)
                dk_sb = wkb_ref[h] * _nt(vh, hbh)
                dk_acc = dk_acc + jnp.where(half, _tn(ds_b, qp), 0.0) + dk_sf + dk_sb
                dv = _tn(sd_b, doh) + _nn(_mx(kmf * wkf_ref[h]), hfh) + _nn(_mx(kmf * wkb_ref[h]), hbh)
                dv_ref[:, hs] = dv.astype(dv_ref.dtype)
                sdp = s * dpm
                hr_f = hfh.astype(F32) * rfh.astype(F32)
                hr_b = hbh.astype(F32) * rbh.astype(F32)
                da = (_rowsum128(sdp * dda_ref[h]) + _rowsum128((pos + 1.0) * qf * dq_if)
                      + _rowsum128((c - 1.0 - pos) * kf * dk_sf) + cdec_ref[h:h + 1, :] * _rowsum128(hr_f))
                db = (_rowsum128(sdp * ddb_ref[h]) + _rowsum128((c - pos) * qf * dq_ib)
                      + _rowsum128(pos * kf * dk_sb) + cdec_ref[4 + h:5 + h, :] * _rowsum128(hr_b))
                dlg_ref[h:h + 1, :] += da
                dlg_ref[4 + h:5 + h, :] += db
            dq_ref[:, sl] = dq_acc
            dk_ref[:, sl] = dk_acc

    st = pl.BlockSpec((None, RET_HEADS, 128, 128), lambda n: (n, 0, 0, 0))
    wsp = pl.BlockSpec((RET_HEADS, c, 128), lambda n: (0, 0, 0))
    dsp = pl.BlockSpec((RET_HEADS, c, c), lambda n: (0, 0, 0))
    x256 = pl.BlockSpec((c, 256), lambda n: (n, 0))
    x512 = pl.BlockSpec((c, 512), lambda n: (n, 0))
    cdec = tb["dec_fb"] * float(c)
    return pl.pallas_call(
        body, grid=(seq // c,), name="ret_bwd",
        in_specs=[x256, x256, x512, x512, st, st, st, st, dsp, dsp, dsp, wsp, wsp, wsp, wsp,
                  pl.BlockSpec((8, 128), lambda n: (0, 0))],
        out_specs=[x256, x256, x512, pl.BlockSpec((8, 128), lambda n: (0, 0))],
        out_shape=[jax.ShapeDtypeStruct((seq, 256), F32), jax.ShapeDtypeStruct((seq, 256), F32),
                   jax.ShapeDtypeStruct((seq, 512), MXU_DTYPE), jax.ShapeDtypeStruct((8, 128), F32)],
        compiler_params=_cp("arbitrary"))(
            rq, rk, rv, do, rf, rb, hf, hb, tb["dmat"], tb["dda"], tb["ddb"], tb["wqf"], tb["wqb"], tb["wkf"], tb["wkb"], cdec)


def _tail(x, tgt, o_att, o_ret, p, mod, g_post, gn_g, wpt, wout):
    seq = x.shape[0]
    bs = 256
    nb = seq // bs

    def body(x_ref, t_ref, oa_ref, or_ref, za_ref, zr_ref, gl_ref, mod_ref, gp_ref, gn_ref, wpt_ref, wout_ref,
             dout_ref, dza_ref, dzr_ref, dgl_ref, doa_ref, dor_ref, del_ref, gout_ref, gpt_ref,
             dgate_ref, dgpost_ref, dgn_ref, loss_ref, gout_acc, gpt_acc):
        i = pl.program_id(0)

        @pl.when(i == 0)
        def _():
            gout_acc[...] = jnp.zeros_like(gout_acc)
            gpt_acc[...] = jnp.zeros_like(gpt_acc)
            dgate_ref[...] = jnp.zeros_like(dgate_ref)
            dgpost_ref[...] = jnp.zeros_like(dgpost_ref)
            dgn_ref[...] = jnp.zeros_like(dgn_ref)
            loss_ref[...] = jnp.zeros_like(loss_ref)

        oa, za, zr = oa_ref[...], za_ref[...], zr_ref[...]
        sga, sgr = _sigmoid(za), _sigmoid(zr)
        sza, szr = za * sga, zr * sgr
        ya_b = _mx(oa * sza)
        ons, rstds = [], []
        for h in range(RET_HEADS):
            oh = or_ref[:, 128 * h:128 * (h + 1)]
            xc = oh - jnp.mean(oh, axis=-1, keepdims=True)
            rstd = lax.rsqrt(jnp.mean(xc * xc, axis=-1, keepdims=True) + EPS)
            ons.append(xc * rstd)
            rstds.append(rstd)
        on = jnp.concatenate(ons, axis=1)
        yn = on * gn_ref[...]
        yr_b = _mx(yn * szr)
        wpa_t, wpr_t = wpt_ref[:, 0:512], wpt_ref[:, 512:1024]
        a_att = _nt(ya_b, wpa_t)
        a_ret = _nt(yr_b, wpr_t)
        gts = _sigmoid(gl_ref[...])
        g_att, g_ret = gts[:, 0:D_MODEL], gts[:, D_MODEL:2 * D_MODEL]
        merged_b = _mx(g_att * a_att + g_ret * a_ret)
        u = _nn(merged_b, wout_ref[...])
        r = lax.rsqrt(jnp.mean(u * u, axis=-1, keepdims=True) + EPS)
        un = u * r
        gpost = gp_ref[...]
        y = un * gpost
        gate = mod_ref[:, 2 * D_MODEL:3 * D_MODEL]
        err = x_ref[...] + gate * y - t_ref[...]
        loss_ref[...] += (0.5 / D_MODEL) * _rowsum128(err * err)
        dout = err * (1.0 / D_MODEL)
        dout_ref[...] = dout
        dgate_ref[...] += jnp.sum(dout * y, axis=0, keepdims=True)
        dy = dout * gate
        dgpost_ref[...] += jnp.sum(dy * un, axis=0, keepdims=True)
        dun = dy * gpost
        du_b = _mx(r * (dun - un * jnp.mean(dun * un, axis=-1, keepdims=True)))
        dmerged = _nt(du_b, wout_ref[...])
        gout_acc[...] += _tn(merged_b, du_b)
        d_att, d_ret = dmerged * g_att, dmerged * g_ret
        dgl_ref[:, 0:D_MODEL] = (d_att * a_att * (1.0 - g_att)).astype(dgl_ref.dtype)
        dgl_ref[:, D_MODEL:2 * D_MODEL] = (d_ret * a_ret * (1.0 - g_ret)).astype(dgl_ref.dtype)
        d_att_b, d_ret_b = _mx(d_att), _mx(d_ret)
        dya = _nn(d_att_b, wpa_t)
        dyr = _nn(d_ret_b, wpr_t)
        gpt_acc[:, 0:512] += _tn(d_att_b, ya_b)
        gpt_acc[:, 512:1024] += _tn(d_ret_b, yr_b)
        dza_ref[...] = (dya * oa * (sga * (1.0 + za * (1.0 - sga)))).astype(dza_ref.dtype)
        doa = dya * sza
        doa_ref[...] = doa.astype(doa_ref.dtype)
        dt = jnp.transpose(doa * oa)
        del_ref[...] = jnp.sum(dt.reshape(8, HEAD_DIM, bs), axis=1)
        dzr_ref[...] = (dyr * yn * (sgr * (1.0 + zr * (1.0 - sgr)))).astype(dzr_ref.dtype)
        dyn = dyr * szr
        dgn_ref[...] += jnp.sum(dyn * on, axis=0, keepdims=True)
        don = dyn * gn_ref[...]
        for h in range(RET_HEADS):
            hs = slice(128 * h, 128 * (h + 1))
            dh, oh = don[:, hs], ons[h]
            doh = rstds[h] * (dh - jnp.mean(dh, axis=-1, keepdims=True) - oh * jnp.mean(dh * oh, axis=-1, keepdims=True))
            dor_ref[:, hs] = doh.astype(dor_ref.dtype)

        @pl.when(i == nb - 1)
        def _():
            gout_ref[...] = gout_acc[...].astype(gout_ref.dtype)
            gpt_ref[...] = gpt_acc[...].astype(gpt_ref.dtype)

    row = lambda w: pl.BlockSpec((bs, w), lambda i: (i, 0))
    el = lambda w, off: pl.BlockSpec((pl.Element(bs), pl.Element(w)), lambda i: (i * bs, off))
    vec = lambda w: pl.BlockSpec((1, w), lambda i: (0, 0))
    full = pl.BlockSpec((D_MODEL, D_MODEL), lambda i: (0, 0))
    sds = jax.ShapeDtypeStruct
    return pl.pallas_call(
        body, grid=(nb,), name="tail",
        in_specs=[row(D_MODEL), row(D_MODEL), row(512), row(512), el(512, ZA), el(512, ZR), el(2 * D_MODEL, GL),
                  vec(3 * D_MODEL), vec(D_MODEL), vec(512), full, full],
        out_specs=[row(D_MODEL), row(512), row(512), row(2 * D_MODEL), row(512), row(512),
                   pl.BlockSpec((8, bs), lambda i: (0, i)), full, full, vec(D_MODEL), vec(D_MODEL), vec(512), vec(128)],
        out_shape=[sds((seq, D_MODEL), F32), sds((seq, 512), MXU_DTYPE), sds((seq, 512), MXU_DTYPE),
                   sds((seq, 2 * D_MODEL), MXU_DTYPE), sds((seq, 512), MXU_DTYPE), sds((seq, 512), MXU_DTYPE),
                   sds((8, seq), F32), sds((D_MODEL, D_MODEL), MXU_DTYPE), sds((D_MODEL, D_MODEL), MXU_DTYPE),
                   sds((1, D_MODEL), F32), sds((1, D_MODEL), F32), sds((1, 512), F32), sds((1, 128), F32)],
        scratch_shapes=[pltpu.VMEM((D_MODEL, D_MODEL), F32), pltpu.VMEM((D_MODEL, D_MODEL), F32)],
        compiler_params=_cp("arbitrary"))(x, tgt, o_att, o_ret, p, p, p, mod, g_post, gn_g, wpt, wout)


def _assemble(p, cos, sin, qg2, kg2, dqt, dkp, dvp, dza, drq, drk, drv, dzr, dgl):
    seq = p.shape[0]
    bs = 512

    def body(p_ref, cos_ref, sin_ref, qg_ref, kg_ref, dqt_ref, dkp_ref, dvp_ref, dza_ref, drq_ref, drk_ref, drv_ref,
             dzr_ref, dgl_ref, dp_ref, dqg_ref, dkg_ref):
        @pl.when(pl.program_id(0) == 0)
        def _():
            dqg_ref[...] = jnp.zeros_like(dqg_ref)
            dkg_ref[...] = jnp.zeros_like(dkg_ref)

        m = _blockdiag64()
        cs, sn = cos_ref[...], sin_ref[...]
        lo = _lane((bs, 128)) < 64

        def norm_bwd(raw, dn, g, dg_ref):
            r = lax.rsqrt(_seg64(raw * raw, m) * (1.0 / 64) + EPS)
            xn = raw * r
            dg_ref[...] += jnp.sum(dn * xn, axis=0, keepdims=True)
            dxn = dn * g
            return r * (dxn - xn * (_seg64(dxn * xn, m) * (1.0 / 64)))

        for pr in range(4):
            sl = slice(128 * pr, 128 * (pr + 1))
            dn = _rope_t(dqt_ref[:, sl] * 0.125, cs, sn)
            dp_ref[:, QA + 128 * pr:QA + 128 * (pr + 1)] = norm_bwd(p_ref[:, sl], dn, qg_ref[...], dqg_ref).astype(dp_ref.dtype)
        fold = lambda a: a + pltpu.roll(a, 64, 1)
        dk = jnp.where(lo, fold(dkp_ref[0]), fold(dkp_ref[1]))
        dp_ref[:, KA:KA + 128] = norm_bwd(p_ref[:, KA:KA + 128], _rope_t(dk, cs, sn), kg_ref[...], dkg_ref).astype(dp_ref.dtype)
        dp_ref[:, VA:VA + 128] = jnp.where(lo, fold(dvp_ref[0]), fold(dvp_ref[1])).astype(dp_ref.dtype)
        dp_ref[:, ZA:ZA + 512] = dza_ref[...]
        for pr in range(2):
            sl = slice(128 * pr, 128 * (pr + 1))
            dp_ref[:, QR + 128 * pr:QR + 128 * (pr + 1)] = _rope_t(drq_ref[:, sl], cs, sn).astype(dp_ref.dtype)
            dp_ref[:, KR + 128 * pr:KR + 128 * (pr + 1)] = _rope_t(drk_ref[:, sl] * 0.125, cs, sn).astype(dp_ref.dtype)
        dp_ref[:, VR:VR + 512] = drv_ref[...]
        dp_ref[:, ZR:ZR + 512] = dzr_ref[...]
        dp_ref[:, GL:GL + 2 * D_MODEL] = dgl_ref[...]

    row = lambda w: pl.BlockSpec((bs, w), lambda i: (i, 0))
    gsp = pl.BlockSpec((1, 128), lambda i: (0, 0))
    dup = pl.BlockSpec((2, bs, 128), lambda i: (0, i, 0))
    return pl.pallas_call(
        body, grid=(seq // bs,), name="assemble_dp",
        in_specs=[row(768), row(128), row(128), gsp, gsp, row(512), dup, dup, row(512), row(256), row(256), row(512),
                  row(512), row(2 * D_MODEL)],
        out_specs=[row(IN_WIDTH), gsp, gsp],
        out_shape=[jax.ShapeDtypeStruct((seq, IN_WIDTH), MXU_DTYPE), jax.ShapeDtypeStruct((1, 128), F32),
                   jax.ShapeDtypeStruct((1, 128), F32)],
        compiler_params=_cp("arbitrary"))(p, cos, sin, qg2, kg2, dqt, dkp, dvp, dza, drq, drk, drv, dzr, dgl)


def _local_step(x, tgt, mod, g_pre, qn_g, kn_g, w_dec_f, w_dec_b, gn_g, g_post, win_t, wpt, wout, send=None):
    send = send or (lambda name, arrays: None)
    seq = x.shape[0]
    cos, sin = _rope_tables(seq)
    qg2, kg2 = jnp.tile(qn_g, (1, 2)), jnp.tile(kn_g, (1, 2))
    lg_f = jax.nn.log_sigmoid(w_dec_f[0])
    lg_b = jax.nn.log_sigmoid(w_dec_b[0])
    tb = _ret_tables(lg_f, lg_b, RET_CHUNK)

    h = _prenorm(x, mod, g_pre)
    p = _matmul(h, win_t, ta=False, tb=True, tm=512, tn=IN_WIDTH // 2, out_dtype=F32, name="in_proj")
    qt, kd, vd, rq, rk, rv = _prep(p, cos, sin, qg2, kg2)
    o_att, lse = _attn_fwd(qt, kd, vd)
    rf, rb = _ret_states(rk, rv, tb["wkf"], tb["wkb"], tb["dec_fb"], "ret_states_fwd")
    o_ret = _ret_fwd(rq, rk, rv, rf, rb, tb)
    (dout, dza, dzr, dgl, do_att, do_ret, delta, g_out, g_pt, dgate, dgpost, dgn, loss_l) = _tail(
        x, tgt, o_att, o_ret, p, mod, g_post, gn_g, wpt, wout)
    dep = send("early", (g_pt, g_out))
    dqt, dkp, dvp = _attn_bwd(qt, kd, vd, do_att, lse, delta, dep=dep)
    hb, hf = _ret_states(rq, do_ret, tb["wqb"], tb["wqf"], tb["dec_bf"], "ret_states_bwd")
    drq, drk, drv, dlg = _ret_bwd(rq, rk, rv, do_ret, rf, rb, hf, hb, tb)
    dp, dqg, dkg = _assemble(p, cos, sin, qg2, kg2, dqt, dkp, dvp, dza, drq, drk, drv, dzr, dgl)
    g_in_t = _matmul(dp, h, ta=True, tb=False, tm=256, tn=D_MODEL, out_dtype=MXU_DTYPE, name="in_proj_dw")
    dep = send("late", (g_in_t,))
    dh = _matmul(dp, win_t, ta=False, tb=False, tm=512, tn=D_MODEL, out_dtype=F32, name="in_proj_dx", dep=dep)
    grad_x, dshift, dscale, dgpre = _prenorm_bwd(x, dh, dout, mod, g_pre)

    dlg = jnp.sum(dlg, axis=1)
    small = dict(
        dmod=jnp.concatenate([dshift, dscale, dgate], axis=1),
        g_pre=dgpre, g_post=dgpost, gn_g=dgn,
        qn_g=dqg[:, :64] + dqg[:, 64:], kn_g=dkg[:, :64] + dkg[:, 64:],
        w_dec_f=(dlg[0:4] * jax.nn.sigmoid(-w_dec_f[0]))[None], w_dec_b=(dlg[4:8] * jax.nn.sigmoid(-w_dec_b[0]))[None],
        loss=jnp.sum(loss_l, axis=1, keepdims=True))
    return grad_x, g_in_t, g_pt, g_out, small


N_DEV = 8
N_CHIP = 4
HBM_SPEC = pl.BlockSpec(memory_space=pl.ANY)
VMEM_SPEC = pl.BlockSpec(memory_space=pltpu.VMEM)
SHARD_ROWS = (IN_WIDTH // N_CHIP, D_MODEL // N_CHIP, D_MODEL // N_CHIP)


def _coords():
    return lax.axis_index("x"), lax.axis_index("y"), lax.axis_index("c")


def _peer(k):
    x, y, c = _coords()
    return (1 - x if k & 4 else x, 1 - y if k & 2 else y, 1 - c if k & 1 else c)


def _dev_index(p):
    return 4 * p[0] + 2 * p[1] + p[2]


def _rows(ref, start, size):
    return ref.at[pl.ds(pl.multiple_of(start, 16), size), :]


def _remote(src, dst, ssem, rsem, dev):
    return pltpu.make_async_remote_copy(src_ref=src, dst_ref=dst, send_sem=ssem, recv_sem=rsem, device_id=dev,
                                        device_id_type=MESH)


def _mod_exchange(c, w_ada_s, b4):
    ncol = w_ada_s.shape[1]

    def body(c_ref, w_ref, b_ref, cs_ref, mod_ref, modsh, modall, ssem, rsem):
        me = _dev_index(_coords())
        chip = me // 2
        cs_ref[me] = c_ref[...]
        sends = []
        for k in range(1, N_DEV):
            cp = _remote(c_ref, cs_ref.at[me], ssem.at[k - 1], rsem.at[k - 1], _peer(k))
            cp.start()
            sends.append(cp)
        for k in range(1, N_DEV):
            slot = cs_ref.at[_dev_index(_peer(k))]
            _remote(slot, slot, ssem.at[k - 1], rsem.at[k - 1], _peer(k)).wait_recv()
        cs = jnp.concatenate([cs_ref[d] for d in range(N_DEV)], axis=0)
        ms = _nn(cs * _sigmoid(cs), w_ref[...], precision=HIGHEST) + b_ref[pl.ds(chip, 1), :]
        modsh[...] = ms
        modall[chip] = ms
        for k2 in range(1, N_CHIP):
            cp = _remote(modsh, modall.at[chip], ssem.at[6 + k2], rsem.at[6 + k2], _peer(2 * k2))
            cp.start()
            sends.append(cp)
        for k2 in range(1, N_CHIP):
            slot = modall.at[_dev_index(_peer(2 * k2)) // 2]
            _remote(slot, slot, ssem.at[6 + k2], rsem.at[6 + k2], _peer(2 * k2)).wait_recv()
        for jj in range(N_CHIP):
            mod_ref[:, ncol * jj:ncol * (jj + 1)] = modall[jj, pl.ds(me, 1), :]
        for cp in sends:
            cp.wait_send()

    return pl.pallas_call(
        body, name="mod_exchange", in_specs=[VMEM_SPEC] * 3, out_specs=[VMEM_SPEC] * 2,
        out_shape=[jax.ShapeDtypeStruct((N_DEV, 1, D_MODEL), F32), jax.ShapeDtypeStruct((1, N_CHIP * ncol), F32)],
        scratch_shapes=[pltpu.VMEM((N_DEV, ncol), F32), pltpu.VMEM((N_CHIP, N_DEV, ncol), F32),
                        pltpu.SemaphoreType.DMA((10,)), pltpu.SemaphoreType.DMA((10,))],
        compiler_params=_cp())(c, w_ada_s, b4)


GATHER_CHUNK = 32


def _chunks(nt, chunk):
    out = []
    for t in range(nt):
        half = SHARD_ROWS[t] // 2
        step = chunk if half % chunk == 0 else half
        out += [(t, off, step) for off in range(0, half, step)]
    return out


def _weight_gather(shards):
    nt = len(shards)
    pieces = _chunks(nt, GATHER_CHUNK)
    npc = len(pieces)

    def body(*refs):
        srcs, outs = refs[:nt], refs[nt:2 * nt]
        lsem, ssem, rsem = refs[2 * nt:]
        x, y, c = _coords()
        chip = 2 * x + y
        sib = (x, y, 1 - c)

        def place(t, ch, cc, off, n):
            return _rows(outs[t], ch * SHARD_ROWS[t] + cc * (SHARD_ROWS[t] // 2) + off, n)

        local = [pltpu.make_async_copy(srcs[t].at[pl.ds(0, SHARD_ROWS[t]), :], _rows(outs[t], chip * SHARD_ROWS[t], SHARD_ROWS[t]),
                                       lsem.at[t]) for t in range(nt)]
        for cp in local:
            cp.start()
        sends = []
        for k2 in range(1, N_CHIP):
            for q, (t, off, n) in enumerate(pieces):
                i = (k2 - 1) * npc + q
                cp = _remote(_rows(srcs[t], c * (SHARD_ROWS[t] // 2) + off, n), place(t, chip, c, off, n),
                             ssem.at[i], rsem.at[i], _peer(2 * k2))
                cp.start()
                sends.append(cp)
        for k2 in range(1, N_CHIP):
            pchip = _dev_index(_peer(2 * k2)) // 2
            for q, (t, off, n) in enumerate(pieces):
                i = (k2 - 1) * npc + q
                got = place(t, pchip, c, off, n)
                _remote(got, got, ssem.at[i], rsem.at[i], _peer(2 * k2)).wait_recv()
                cp = _remote(got, got, ssem.at[3 * npc + i], rsem.at[3 * npc + i], sib)
                cp.start()
                sends.append(cp)
        for k2 in range(1, N_CHIP):
            pchip = _dev_index(_peer(2 * k2)) // 2
            for q, (t, off, n) in enumerate(pieces):
                i = (k2 - 1) * npc + q
                got = place(t, pchip, 1 - c, off, n)
                _remote(got, got, ssem.at[3 * npc + i], rsem.at[3 * npc + i], sib).wait_recv()
        for cp in sends:
            cp.wait_send()
        for cp in local:
            cp.wait()

    return pl.pallas_call(
        body, name="weight_gather", in_specs=[VMEM_SPEC] * nt, out_specs=[HBM_SPEC] * nt,
        out_shape=[jax.ShapeDtypeStruct((N_CHIP * SHARD_ROWS[t], s.shape[1]), s.dtype) for t, s in enumerate(shards)],
        scratch_shapes=[pltpu.SemaphoreType.DMA((nt,)), pltpu.SemaphoreType.DMA((6 * npc,)), pltpu.SemaphoreType.DMA((6 * npc,))],
        compiler_params=_cp())(*shards)


SEM_SPEC = pl.BlockSpec(memory_space=pltpu.SEMAPHORE)
HBM_ONLY = pl.BlockSpec(memory_space=pltpu.HBM)
DATAFLOW = pltpu.SideEffectType.DATAFLOW_SIDE_EFFECTING


def _reduced_by(ref, t, dev):
    return _rows(ref, (dev // 2) * SHARD_ROWS[t] + (dev % 2) * (SHARD_ROWS[t] // 2), SHARD_ROWS[t] // 2)


def _scatter_start(grads, kinds, name):
    n = len(grads)

    def body(*refs):
        srcs, lands = refs[:n], refs[n:2 * n]
        ssem, rsem = refs[2 * n], refs[2 * n + 1]
        token = refs[-1]
        me = _dev_index(_coords())
        for k in range(1, N_DEV):
            for i, t in enumerate(kinds):
                q = (k - 1) * n + i
                _remote(_reduced_by(srcs[i], t, _dev_index(_peer(k))), lands[i].at[me], ssem.at[q], rsem.at[q], _peer(k)).start()
        token[...] = jnp.zeros_like(token)

    zones = [lax.empty((N_DEV, SHARD_ROWS[t] // 2, g.shape[1]), g.dtype) for g, t in zip(grads, kinds)]
    hbm = lambda a: pltpu.with_memory_space_constraint(a, pltpu.HBM)
    dma = pltpu.SemaphoreType.DMA
    outs = pl.pallas_call(
        body, name=name, in_specs=[HBM_ONLY] * (2 * n), out_specs=[SEM_SPEC, SEM_SPEC] + [HBM_ONLY] * (2 * n) + [VMEM_SPEC],
        out_shape=[dma((7 * n,)), dma((7 * n,))] + [pltpu.HBM(a.shape, a.dtype) for a in list(grads) + zones]
        + [jax.ShapeDtypeStruct((8, 128), F32)],
        input_output_aliases={i: 2 + i for i in range(2 * n)},
        compiler_params=pltpu.CompilerParams(has_side_effects=DATAFLOW))(*[hbm(a) for a in list(grads) + zones])
    return outs[0], outs[1], outs[2:2 + n], outs[2 + n:2 + 2 * n], outs[-1]


def _scatter_wait(started, kinds, after, name):
    ssem, rsem, grads, zones, _ = started
    n = len(grads)

    def body(*refs):
        srcs, lands = refs[:n], refs[n:2 * n]
        ssem_ref, rsem_ref = refs[2 * n], refs[2 * n + 1]
        for k in range(1, N_DEV):
            peer = _dev_index(_peer(k))
            for i, t in enumerate(kinds):
                q = (k - 1) * n + i
                cp = _remote(_reduced_by(srcs[i], t, peer), lands[i].at[peer], ssem_ref.at[q], rsem_ref.at[q], _peer(k))
                cp.wait_send()
                cp.wait_recv()

    outs = pl.pallas_call(
        body, name=name, in_specs=[HBM_ONLY] * (2 * n) + [SEM_SPEC, SEM_SPEC, HBM_SPEC], out_specs=[HBM_ONLY] * (2 * n),
        out_shape=[pltpu.HBM(a.shape, a.dtype) for a in list(grads) + list(zones)],
        input_output_aliases={i: i for i in range(2 * n)},
        compiler_params=pltpu.CompilerParams(has_side_effects=DATAFLOW))(*grads, *zones, ssem, rsem, after)
    return outs[:n], outs[n:]


def _grad_finish(grads, zones):
    nt = len(grads)
    pieces = _chunks(nt, GATHER_CHUNK)
    npc = len(pieces)

    def body(*refs):
        srcs, lands, outs = refs[:nt], refs[nt:2 * nt], refs[2 * nt:3 * nt]
        slots, sums = refs[3 * nt:4 * nt], refs[4 * nt:5 * nt]
        lsem, osem, xsem, ysem = refs[5 * nt:]
        x, y, c = _coords()
        me = _dev_index((x, y, c))
        sib = (x, y, 1 - c)
        loads = [pltpu.make_async_copy(_reduced_by(srcs[t], t, me), slots[t].at[me], lsem.at[t]) for t in range(nt)]
        for k in range(1, N_DEV):
            peer = _dev_index(_peer(k))
            loads += [pltpu.make_async_copy(lands[t].at[peer], slots[t].at[peer], lsem.at[k * nt + t]) for t in range(nt)]
        for cp in loads:
            cp.start()
        for cp in loads:
            cp.wait()
        sends = []
        place = lambda t, cc, off, n: _rows(outs[t], cc * (SHARD_ROWS[t] // 2) + off, n)
        stores = []
        for q, (t, off, n) in enumerate(pieces):
            r = pl.ds(off, n)
            acc = slots[t][0, r, :].astype(F32)
            for d in range(1, N_DEV):
                acc = acc + slots[t][d, r, :].astype(F32)
            sums[t][r, :] = acc
            keep = pltpu.make_async_copy(sums[t].at[r, :], place(t, c, off, n), osem.at[q])
            give = _remote(sums[t].at[r, :], place(t, c, off, n), xsem.at[q], ysem.at[q], sib)
            keep.start()
            give.start()
            stores.append(keep)
            sends.append(give)
        for q, (t, off, n) in enumerate(pieces):
            got = place(t, 1 - c, off, n)
            _remote(got, got, xsem.at[q], ysem.at[q], sib).wait_recv()
        for cp in sends:
            cp.wait_send()
        for cp in stores:
            cp.wait()

    dma = pltpu.SemaphoreType.DMA
    return pl.pallas_call(
        body, name="grad_finish", in_specs=[HBM_SPEC] * (2 * nt), out_specs=[HBM_SPEC] * nt,
        out_shape=[jax.ShapeDtypeStruct((SHARD_ROWS[t], g.shape[1]), F32) for t, g in enumerate(grads)],
        scratch_shapes=([pltpu.VMEM((N_DEV, SHARD_ROWS[t] // 2, g.shape[1]), g.dtype) for t, g in enumerate(grads)]
                        + [pltpu.VMEM((SHARD_ROWS[t] // 2, g.shape[1]), F32) for t, g in enumerate(grads)]
                        + [dma((N_DEV * nt,)), dma((npc,)), dma((npc,)), dma((npc,))]),
        compiler_params=_cp())(*grads, *zones)


def _adam_math(w, g, m, v):
    m = ADAM_B1 * m + (1.0 - ADAM_B1) * g
    v = ADAM_B2 * v + (1.0 - ADAM_B2) * (g * g)
    m_hat = m / (1.0 - ADAM_B1 ** ADAM_STEP)
    v_hat = v / (1.0 - ADAM_B2 ** ADAM_STEP)
    return -ADAM_LR * (m_hat / (jnp.sqrt(v_hat) + ADAM_EPS) + ADAM_WD * w), m, v


def _adamw(w, g, m, v, rb, name):
    rows, cols = w.shape

    def body(w_ref, g_ref, m_ref, v_ref, d_ref, nm_ref, nv_ref):
        d_ref[...], nm_ref[...], nv_ref[...] = _adam_math(w_ref[...], g_ref[...], m_ref[...], v_ref[...])

    spec = pl.BlockSpec((rb, cols), lambda i: (i, 0))
    return pl.pallas_call(body, grid=(rows // rb,), name=name, in_specs=[spec] * 4, out_specs=[spec] * 3,
                          out_shape=[jax.ShapeDtypeStruct(w.shape, F32)] * 3, compiler_params=_cp("parallel"))(w, g, m, v)


PACK = (("b_ada", 3 * D_MODEL), ("g_pre", D_MODEL), ("g_post", D_MODEL), ("gn_g", 512), ("qn_g", 64), ("kn_g", 64),
        ("w_dec_f", 4), ("w_dec_b", 4), ("loss", 1))
PACK_OFFSETS = {}
PACK_WIDTH = 0
for _name, _n in PACK:
    PACK_OFFSETS[_name] = PACK_WIDTH
    PACK_WIDTH += -(-_n // 128) * 128
SMALL_PARAMS = tuple(name for name, _ in PACK if name != "loss")


def _pack(parts):
    cols = []
    for name, n in PACK:
        pad = -(-n // 128) * 128 - n
        cols.append(parts[name].reshape(1, n).astype(F32))
        if pad:
            cols.append(jnp.zeros((1, pad), F32))
    return jnp.concatenate(cols, axis=1)


def _small_reduce(vec, cs):
    ncol = 3 * D_MODEL // N_CHIP

    def body(vec_ref, cs_ref, tot_ref, gwa_ref, gat, ssem, rsem):
        me = _dev_index(_coords())
        chip = me // 2
        gat[me] = vec_ref[...]
        sends = []
        for k in range(1, N_DEV):
            cp = _remote(vec_ref, gat.at[me], ssem.at[k - 1], rsem.at[k - 1], _peer(k))
            cp.start()
            sends.append(cp)
        for k in range(1, N_DEV):
            slot = gat.at[_dev_index(_peer(k))]
            _remote(slot, slot, ssem.at[k - 1], rsem.at[k - 1], _peer(k)).wait_recv()
        rows = [gat[d] for d in range(N_DEV)]
        tot = rows[0]
        for d in range(1, N_DEV):
            tot = tot + rows[d]
        tot_ref[...] = tot
        cs = cs_ref[...]
        ca_t = jnp.transpose(jnp.concatenate([cs * _sigmoid(cs), jnp.zeros((128 - N_DEV, D_MODEL), F32)], axis=0))
        dm = jnp.concatenate(rows + [jnp.zeros((128 - N_DEV, PACK_WIDTH), F32)], axis=0)
        for jj in range(N_CHIP):
            @pl.when(chip == jj)
            def _():
                gwa_ref[...] = _nn(ca_t, dm[:, ncol * jj:ncol * (jj + 1)], precision=HIGHEST)
        for cp in sends:
            cp.wait_send()

    return pl.pallas_call(
        body, name="small_reduce", in_specs=[VMEM_SPEC] * 2, out_specs=[VMEM_SPEC] * 2,
        out_shape=[jax.ShapeDtypeStruct((1, PACK_WIDTH), F32), jax.ShapeDtypeStruct((D_MODEL, ncol), F32)],
        scratch_shapes=[pltpu.VMEM((N_DEV, 1, PACK_WIDTH), F32), pltpu.SemaphoreType.DMA((7,)), pltpu.SemaphoreType.DMA((7,))],
        compiler_params=_cp())(vec, cs)


def _small_adamw(tot, given):
    sizes = dict(PACK)
    np_ = len(SMALL_PARAMS)

    def body(*refs):
        tot_ref = refs[0]
        wmv = refs[1:1 + 3 * np_]
        outs = refs[1 + 3 * np_:1 + 7 * np_]
        loss_ref = refs[-1]
        for i, name in enumerate(SMALL_PARAMS):
            off, n = PACK_OFFSETS[name], sizes[name]
            g = tot_ref[:, off:off + n]
            w_ref, m_ref, v_ref = wmv[3 * i:3 * i + 3]
            outs[i][...] = g
            outs[np_ + i][...], outs[2 * np_ + i][...], outs[3 * np_ + i][...] = _adam_math(w_ref[...], g, m_ref[...], v_ref[...])
        loss_ref[...] = tot_ref[:, PACK_OFFSETS["loss"]:PACK_OFFSETS["loss"] + 1]

    flat = [a for name in SMALL_PARAMS for a in given[name]]
    shapes = [jax.ShapeDtypeStruct((1, sizes[name]), F32) for name in SMALL_PARAMS]
    res = pl.pallas_call(body, name="small_adamw", in_specs=[VMEM_SPEC] * (1 + 3 * np_), out_specs=[VMEM_SPEC] * (4 * np_ + 1),
                         out_shape=shapes * 4 + [jax.ShapeDtypeStruct((1, 1), F32)], compiler_params=_cp())(tot, *flat)
    return [dict(zip(SMALL_PARAMS, res[k * np_:(k + 1) * np_])) for k in range(4)], res[-1]


def kernel(x, c, w_ada, b_ada, g_pre, w_in, qn_g, kn_g, w_dec_f, w_dec_b, gn_g, w_pa, w_pr, w_out, g_post, loss_target, m_w_ada, m_b_ada, m_g_pre, m_w_in, m_qn_g, m_kn_g, m_w_dec_f, m_w_dec_b, m_gn_g, m_w_pa, m_w_pr, m_w_out, m_g_post, v_w_ada, v_b_ada, v_g_pre, v_w_in, v_qn_g, v_kn_g, v_w_dec_f, v_w_dec_b, v_gn_g, v_w_pa, v_w_pr, v_w_out, v_g_post):
    shards = (w_in[0].T.astype(MXU_DTYPE), jnp.concatenate([w_pa[0].T, w_pr[0].T], axis=1).astype(MXU_DTYPE),
              w_out[0].astype(MXU_DTYPE))
    win_t, wpt, wout = _weight_gather(shards)
    cs, mod = _mod_exchange(c, w_ada[0], b_ada.reshape(N_CHIP, 3 * D_MODEL // N_CHIP))

    kinds = {"early": (1, 2), "late": (0,)}
    started = {}

    def send(name, arrays):
        started[name] = _scatter_start(arrays, kinds[name], "grad_scatter_" + name)
        return started[name][-1]

    grad_x, _, _, _, small = _local_step(x[0], loss_target[0], mod, g_pre, qn_g, kn_g, w_dec_f, w_dec_b,
                                         gn_g, g_post, win_t, wpt, wout, send=send)
    small = dict(small)
    small["b_ada"] = small.pop("dmod")
    given = dict(b_ada=(b_ada, m_b_ada, v_b_ada), g_pre=(g_pre, m_g_pre, v_g_pre), g_post=(g_post, m_g_post, v_g_post),
                 gn_g=(gn_g, m_gn_g, v_gn_g), qn_g=(qn_g, m_qn_g, v_qn_g), kn_g=(kn_g, m_kn_g, v_kn_g),
                 w_dec_f=(w_dec_f, m_w_dec_f, v_w_dec_f), w_dec_b=(w_dec_b, m_w_dec_b, v_w_dec_b))
    tot, g_w_ada = _small_reduce(_pack(small), cs.reshape(N_DEV, D_MODEL))
    (grads, deltas, new_m, new_v), loss = _small_adamw(tot, given)
    d, nm, nv = _adamw(w_ada[0], g_w_ada, m_w_ada[0], v_w_ada[0], 256, "adamw_w_ada")
    grads["w_ada"], deltas["w_ada"], new_m["w_ada"], new_v["w_ada"] = g_w_ada[None], d[None], nm[None], nv[None]

    (g_pt, g_out), (z_pt, z_out) = _scatter_wait(started["early"], kinds["early"], d, "grad_scatter_early_wait")
    (g_in_t,), (z_in,) = _scatter_wait(started["late"], kinds["late"], d, "grad_scatter_late_wait")
    full = _grad_finish((g_in_t, g_pt, g_out), (z_in, z_pt, z_out))
    tr = lambda a: a[0].T
    for name, (w, m, v, g, back) in {"w_in": (tr(w_in), tr(m_w_in), tr(v_w_in), full[0], lambda a: a.T[None]),
                                     "w_pa": (w_pa[0], m_w_pa[0], v_w_pa[0], full[1][:, :512].T, lambda a: a[None]),
                                     "w_pr": (w_pr[0], m_w_pr[0], v_w_pr[0], full[1][:, 512:].T, lambda a: a[None]),
                                     "w_out": (w_out[0], m_w_out[0], v_w_out[0], full[2], lambda a: a[None])}.items():
        d, nm, nv = _adamw(w, g, m, v, w.shape[0] // 4, "adamw_" + name)
        grads[name], deltas[name], new_m[name], new_v[name] = back(g), back(d), back(nm), back(nv)
    order = ("w_ada", "b_ada", "g_pre", "w_in", "qn_g", "kn_g", "w_dec_f", "w_dec_b", "gn_g", "w_pa", "w_pr", "w_out", "g_post")
    return (loss[0, 0], grad_x[None], *[grads[n] for n in order], *[deltas[n] for n in order],
            *[new_m[n] for n in order], *[new_v[n] for n in order])
```

```python
import functools

import jax
import jax.numpy as jnp
import numpy as np
from jax import lax
from jax.experimental import pallas as pl
from jax.experimental.pallas import tpu as pltpu

F32 = jnp.float32
BF16 = jnp.bfloat16
MXU_DTYPE = jnp.bfloat16

D_MODEL = 1024
GRID_W = 64
HEAD_DIM = 64
ROPE_THETA = 10000.0
EPS = 1e-6
RET_HEADS = 4
RET_CHUNK = 256
IN_WIDTH = 4864
QA, KA, VA, ZA, QR, KR, VR, ZR, GL = 0, 512, 640, 768, 1280, 1536, 1792, 2304, 2816

ADAM_LR, ADAM_B1, ADAM_B2, ADAM_EPS, ADAM_WD, ADAM_STEP = 0.001, 0.9, 0.999, 1e-08, 0.01, 10

VMEM_LIMIT = 56 * 1024 * 1024
MESH = pl.DeviceIdType.MESH
HIGHEST = lax.Precision.HIGHEST
LOG2E = 1.4426950408889634
LN2 = 0.6931471805599453


def _cp(*sem, **kw):
    if sem:
        kw["dimension_semantics"] = sem
    return pltpu.CompilerParams(vmem_limit_bytes=VMEM_LIMIT, **kw)


def _nn(a, b, **kw):
    return lax.dot_general(a, b, (((1,), (0,)), ((), ())), preferred_element_type=F32, **kw)


def _nt(a, b):
    return lax.dot_general(a, b, (((1,), (1,)), ((), ())), preferred_element_type=F32)


def _tn(a, b):
    return lax.dot_general(a, b, (((0,), (0,)), ((), ())), preferred_element_type=F32)


def _mx(x):
    return x.astype(MXU_DTYPE)


def _lane(shape):
    return lax.broadcasted_iota(jnp.int32, shape, 1)


def _sigmoid(z):
    return 1.0 / (1.0 + jnp.exp(-z))


def _rowsum128(x):
    s = jnp.sum(x, axis=0, keepdims=True)
    out = s[:, 0:128]
    for k in range(1, x.shape[1] // 128):
        out = out + s[:, 128 * k:128 * (k + 1)]
    return out


def _swap16(x):
    return jnp.where((_lane(x.shape) & 16) == 0, pltpu.roll(x, 112, 1), pltpu.roll(x, 16, 1))


def _rope(x, cos, sin):
    return x * cos + _swap16(x) * sin


def _rope_t(d, cos, sin):
    return d * cos + _swap16(d * sin)


def _blockdiag64():
    r = lax.broadcasted_iota(jnp.int32, (128, 128), 0) // 64
    c = lax.broadcasted_iota(jnp.int32, (128, 128), 1) // 64
    return (r == c).astype(BF16)


def _seg64(x, m):
    hi = x.astype(BF16)
    lo = (x - hi.astype(F32)).astype(BF16)
    return _nn(hi, m) + _nn(lo, m)


def _rope_tables(seq):
    t = np.arange(seq)
    row = (t // GRID_W).astype(np.float32)
    col = (t % GRID_W).astype(np.float32)
    half = HEAD_DIM // 2
    inv_freq = (np.float32(ROPE_THETA) ** (-np.arange(0, half, 2, dtype=np.float32) / np.float32(half))).astype(np.float32)
    ar = (row[:, None] * inv_freq[None, :]).astype(np.float32).astype(np.float64)
    ac = (col[:, None] * inv_freq[None, :]).astype(np.float32).astype(np.float64)
    cr, sr, cc, sc = np.cos(ar), np.sin(ar), np.cos(ac), np.sin(ac)
    cos64 = np.concatenate([cr, cr, cc, cc], axis=1)
    sin64 = np.concatenate([-sr, sr, -sc, sc], axis=1)
    return jnp.asarray(np.tile(cos64, (1, 2)), F32), jnp.asarray(np.tile(sin64, (1, 2)), F32)


def _prenorm(x, mod, g_pre):
    seq = x.shape[0]
    bs = 512

    def body(x_ref, mod_ref, g_ref, h_ref):
        xv = x_ref[...]
        r = lax.rsqrt(jnp.mean(xv * xv, axis=-1, keepdims=True) + EPS)
        shift = mod_ref[:, 0:D_MODEL]
        scale = mod_ref[:, D_MODEL:2 * D_MODEL]
        h_ref[...] = ((xv * r) * g_ref[...] * (1.0 + scale) + shift).astype(h_ref.dtype)

    return pl.pallas_call(
        body, grid=(seq // bs,), name="prenorm",
        in_specs=[pl.BlockSpec((bs, D_MODEL), lambda i: (i, 0)), pl.BlockSpec((1, 3 * D_MODEL), lambda i: (0, 0)),
                  pl.BlockSpec((1, D_MODEL), lambda i: (0, 0))],
        out_specs=pl.BlockSpec((bs, D_MODEL), lambda i: (i, 0)),
        out_shape=jax.ShapeDtypeStruct((seq, D_MODEL), MXU_DTYPE), compiler_params=_cp("parallel"))(x, mod, g_pre)


def _prenorm_bwd(x, dh, dout, mod, g_pre):
    seq = x.shape[0]
    bs = 512

    def body(x_ref, dh_ref, dout_ref, mod_ref, g_ref, gx_ref, dshift_ref, dscale_ref, dg_ref):
        @pl.when(pl.program_id(0) == 0)
        def _():
            dshift_ref[...] = jnp.zeros_like(dshift_ref)
            dscale_ref[...] = jnp.zeros_like(dscale_ref)
            dg_ref[...] = jnp.zeros_like(dg_ref)

        xv = x_ref[...]
        dh = dh_ref[...]
        g = g_ref[...]
        r = lax.rsqrt(jnp.mean(xv * xv, axis=-1, keepdims=True) + EPS)
        xn = xv * r
        scale = mod_ref[:, D_MODEL:2 * D_MODEL]
        dshift_ref[...] += jnp.sum(dh, axis=0, keepdims=True)
        dscale_ref[...] += jnp.sum(dh * (xn * g), axis=0, keepdims=True)
        da = dh * (1.0 + scale)
        dg_ref[...] += jnp.sum(da * xn, axis=0, keepdims=True)
        dxn = da * g
        dx = r * (dxn - xn * jnp.mean(dxn * xn, axis=-1, keepdims=True))
        gx_ref[...] = dout_ref[...] + dx

    row = pl.BlockSpec((bs, D_MODEL), lambda i: (i, 0))
    vec = pl.BlockSpec((1, D_MODEL), lambda i: (0, 0))
    return pl.pallas_call(
        body, grid=(seq // bs,), name="prenorm_bwd",
        in_specs=[row, row, row, pl.BlockSpec((1, 3 * D_MODEL), lambda i: (0, 0)), vec],
        out_specs=[row, vec, vec, vec],
        out_shape=[jax.ShapeDtypeStruct((seq, D_MODEL), F32)] + [jax.ShapeDtypeStruct((1, D_MODEL), F32)] * 3,
        compiler_params=_cp("arbitrary"))(x, dh, dout, mod, g_pre)


def _dep_args(dep, grid_rank):
    if dep is None:
        return [], []
    return [dep], [pl.BlockSpec(dep.shape, lambda *_: (0,) * dep.ndim)]


def _matmul(a, b, *, ta, tb, tm, tn, out_dtype, name, dep=None):
    kdim = a.shape[0] if ta else a.shape[1]
    m = a.shape[1] if ta else a.shape[0]
    n = b.shape[0] if tb else b.shape[1]
    assert m % tm == 0 and n % tn == 0
    deps, dep_specs = _dep_args(dep, 2)

    def body(a_ref, b_ref, *rest):
        o_ref = rest[-1]
        dims = (((0 if ta else 1,), (1 if tb else 0,)), ((), ()))
        o_ref[...] = lax.dot_general(a_ref[...], b_ref[...], dims, preferred_element_type=F32).astype(o_ref.dtype)

    a_spec = pl.BlockSpec((kdim, tm), lambda j, i: (0, i)) if ta else pl.BlockSpec((tm, kdim), lambda j, i: (i, 0))
    b_spec = pl.BlockSpec((tn, kdim), lambda j, i: (j, 0)) if tb else pl.BlockSpec((kdim, tn), lambda j, i: (0, j))
    return pl.pallas_call(
        body, grid=(n // tn, m // tm), name=name, in_specs=[a_spec, b_spec] + dep_specs,
        out_specs=pl.BlockSpec((tm, tn), lambda j, i: (i, j)),
        out_shape=jax.ShapeDtypeStruct((m, n), out_dtype), compiler_params=_cp("parallel", "parallel"))(a, b, *deps)


def _prep(p, cos, sin, qg2, kg2):
    seq = p.shape[0]
    bs = 512

    def body(p_ref, cos_ref, sin_ref, qg_ref, kg_ref, qt_ref, kd_ref, vd_ref, rq_ref, rk_ref, rv_ref):
        m = _blockdiag64()
        cs, sn = cos_ref[...], sin_ref[...]
        lo = _lane((bs, 128)) < 64
        for pr in range(4):
            q = p_ref[:, QA + 128 * pr:QA + 128 * (pr + 1)]
            r = lax.rsqrt(_seg64(q * q, m) * (1.0 / 64) + EPS)
            qt_ref[:, 128 * pr:128 * (pr + 1)] = (_rope(q * r * qg_ref[...], cs, sn) * (0.125 * LOG2E)).astype(qt_ref.dtype)
        k = p_ref[:, KA:KA + 128]
        r = lax.rsqrt(_seg64(k * k, m) * (1.0 / 64) + EPS)
        kr = _rope(k * r * kg_ref[...], cs, sn)
        ksw = pltpu.roll(kr, 64, 1)
        kd_ref[0] = jnp.where(lo, kr, ksw).astype(kd_ref.dtype)
        kd_ref[1] = jnp.where(lo, ksw, kr).astype(kd_ref.dtype)
        v = p_ref[:, VA:VA + 128]
        vsw = pltpu.roll(v, 64, 1)
        vd_ref[0] = jnp.where(lo, v, vsw).astype(vd_ref.dtype)
        vd_ref[1] = jnp.where(lo, vsw, v).astype(vd_ref.dtype)
        for pr in range(2):
            sl = slice(128 * pr, 128 * (pr + 1))
            rq_ref[:, sl] = _rope(p_ref[:, QR + 128 * pr:QR + 128 * (pr + 1)], cs, sn).astype(rq_ref.dtype)
            rk_ref[:, sl] = (_rope(p_ref[:, KR + 128 * pr:KR + 128 * (pr + 1)], cs, sn) * 0.125).astype(rk_ref.dtype)
        rv_ref[...] = p_ref[:, VR:VR + 512].astype(rv_ref.dtype)

    tab = pl.BlockSpec((bs, 128), lambda i: (i, 0))
    gsp = pl.BlockSpec((1, 128), lambda i: (0, 0))
    dup = pl.BlockSpec((2, bs, 128), lambda i: (0, i, 0))
    return pl.pallas_call(
        body, grid=(seq // bs,), name="mixer_prep",
        in_specs=[pl.BlockSpec((bs, ZR), lambda i: (i, 0)), tab, tab, gsp, gsp],
        out_specs=[pl.BlockSpec((bs, 512), lambda i: (i, 0)), dup, dup, pl.BlockSpec((bs, 256), lambda i: (i, 0)),
                   pl.BlockSpec((bs, 256), lambda i: (i, 0)), pl.BlockSpec((bs, 512), lambda i: (i, 0))],
        out_shape=[jax.ShapeDtypeStruct((seq, 512), MXU_DTYPE), jax.ShapeDtypeStruct((2, seq, 128), MXU_DTYPE),
                   jax.ShapeDtypeStruct((2, seq, 128), MXU_DTYPE), jax.ShapeDtypeStruct((seq, 256), MXU_DTYPE),
                   jax.ShapeDtypeStruct((seq, 256), MXU_DTYPE), jax.ShapeDtypeStruct((seq, 512), MXU_DTYPE)],
        compiler_params=_cp("parallel"))(p, cos, sin, qg2, kg2)


def _halves(kd):
    lo = _lane(kd.shape) < 64
    z = jnp.zeros_like(kd)
    return jnp.concatenate([jnp.where(lo, kd, z), jnp.where(lo, z, kd)], axis=0)


def _attn_fwd(qt, kd, vd):
    seq = qt.shape[0]
    bq = bk = 512
    nk = seq // bk

    def body(q_ref, k_ref, v_ref, o_ref, lse_ref, m_scr, l_scr, acc):
        j = pl.program_id(1)
        m_scr[...] = jnp.full_like(m_scr, -jnp.inf)
        l_scr[...] = jnp.zeros_like(l_scr)
        acc[...] = jnp.zeros_like(acc)
        lo = _lane((bq, 128)) < 64

        def kv_block(n, carry):
            rows = pl.ds(pl.multiple_of(n * bk, bk), bk)
            k2, v2 = _halves(k_ref[rows, :]), _halves(v_ref[rows, :])
            for pr in range(2):
                s2 = _nt(q_ref[:, 128 * pr:128 * (pr + 1)], k2)
                ps, alphas = [], []
                for e in range(2):
                    g = 2 * pr + e
                    s = s2[:, e * bk:(e + 1) * bk]
                    m_prev = m_scr[g]
                    m_next = jnp.maximum(m_prev, jnp.max(s, axis=1, keepdims=True))
                    alpha = jnp.exp2(m_prev - m_next)
                    pe = jnp.exp2(s - jnp.tile(m_next, (1, bk // 128)))
                    l_scr[g] = alpha * l_scr[g] + jnp.sum(pe, axis=1, keepdims=True)
                    m_scr[g] = m_next
                    ps.append(_mx(pe))
                    alphas.append(alpha)
                o2 = _nn(jnp.concatenate(ps, axis=1), v2)
                sl = slice(128 * pr, 128 * (pr + 1))
                acc[:, sl] = acc[:, sl] * jnp.where(lo, alphas[0], alphas[1]) + o2
            return carry

        lax.fori_loop(0, nk, kv_block, 0)
        for pr in range(2):
            sl = slice(128 * pr, 128 * (pr + 1))
            o_ref[:, sl] = acc[:, sl] / jnp.where(lo, l_scr[2 * pr], l_scr[2 * pr + 1])
        for g in range(4):
            lse = jnp.transpose(m_scr[g] + jnp.log2(l_scr[g]))
            lse_ref[pl.ds(4 * j + g, 1), :] = lse[0:1, :]

    return pl.pallas_call(
        body, grid=(seq // bq, 2), name="attn_fwd",
        in_specs=[pl.BlockSpec((bq, 256), lambda i, j: (i, j)), pl.BlockSpec((None, seq, 128), lambda i, j: (j, 0, 0)),
                  pl.BlockSpec((None, seq, 128), lambda i, j: (j, 0, 0))],
        out_specs=[pl.BlockSpec((bq, 256), lambda i, j: (i, j)), pl.BlockSpec((8, bq), lambda i, j: (0, i))],
        out_shape=[jax.ShapeDtypeStruct((seq, 512), F32), jax.ShapeDtypeStruct((8, seq), F32)],
        scratch_shapes=[pltpu.VMEM((4, bq, 128), F32), pltpu.VMEM((4, bq, 128), F32), pltpu.VMEM((bq, 256), F32)],
        compiler_params=_cp("parallel", "arbitrary"))(qt, kd, vd)


def _attn_bwd(qt, kd, vd, do, lse, delta, dep=None):
    seq = qt.shape[0]
    bq = bk = 512
    nq, nk = seq // bq, seq // bk
    deps, dep_specs = _dep_args(dep, 3)

    def body(q_ref, do_ref, k_ref, v_ref, lse_ref, del_ref, *rest):
        dq_ref, dk_ref, dv_ref = rest[-3:]
        j, i = pl.program_id(0), pl.program_id(1)
        dq_ref[...] = jnp.zeros_like(dq_ref)

        @pl.when(i == 0)
        def _():
            dk_ref[...] = jnp.zeros_like(dk_ref)
            dv_ref[...] = jnp.zeros_like(dv_ref)

        lo = _lane((bk, 128)) < 64
        big = lambda r: jnp.concatenate([jnp.broadcast_to(r[0], (bk, bq)), jnp.broadcast_to(r[1], (bk, bq))], axis=0)

        def kv_block(n, carry):
            rows = pl.ds(pl.multiple_of(n * bk, bk), bk)
            k2, v2 = _halves(k_ref[rows, :]), _halves(v_ref[rows, :])
            for pr in range(2):
                sl = slice(128 * pr, 128 * (pr + 1))
                qp, dop = q_ref[:, sl], do_ref[:, sl]
                s_t = _nt(k2, qp)
                dp_t = _nt(v2, dop)
                ls = [lse_ref[pl.ds(4 * j + 2 * pr + e, 1), :] for e in range(2)]
                dl = [del_ref[pl.ds(4 * j + 2 * pr + e, 1), :] for e in range(2)]
                p_t = jnp.exp2(s_t - big(ls))
                ds_t = _mx(p_t * (dp_t - big(dl)))
                rv = _nn(_mx(p_t), dop)
                rk = _nn(ds_t, qp) * LN2
                dv_ref[rows, :] += jnp.where(lo, rv[:bk], rv[bk:])
                dk_ref[rows, :] += jnp.where(lo, rk[:bk], rk[bk:])
                dq_ref[:, sl] += _tn(ds_t, k2)
            return carry

        lax.fori_loop(0, nk, kv_block, 0)

    return pl.pallas_call(
        body, grid=(2, nq), name="attn_bwd",
        in_specs=[pl.BlockSpec((bq, 256), lambda j, i: (i, j)), pl.BlockSpec((bq, 256), lambda j, i: (i, j)),
                  pl.BlockSpec((None, seq, 128), lambda j, i: (j, 0, 0)), pl.BlockSpec((None, seq, 128), lambda j, i: (j, 0, 0)),
                  pl.BlockSpec((8, bq), lambda j, i: (0, i)), pl.BlockSpec((8, bq), lambda j, i: (0, i))] + dep_specs,
        out_specs=[pl.BlockSpec((bq, 256), lambda j, i: (i, j)), pl.BlockSpec((None, seq, 128), lambda j, i: (j, 0, 0)),
                   pl.BlockSpec((None, seq, 128), lambda j, i: (j, 0, 0))],
        out_shape=[jax.ShapeDtypeStruct((seq, 512), F32), jax.ShapeDtypeStruct((2, seq, 128), F32),
                   jax.ShapeDtypeStruct((2, seq, 128), F32)],
        compiler_params=_cp("arbitrary", "arbitrary"))(qt, do, kd, vd, lse, delta, *deps)


def _ret_tables(lg_f, lg_b, c):
    idx = jnp.arange(c, dtype=F32)
    diff = idx[:, None] - idx[None, :]
    a, b = lg_f[:, None, None], lg_b[:, None, None]
    low, up = (diff >= 0)[None], (diff < 0)[None]
    dmat = jnp.where(low, jnp.exp(a * jnp.maximum(diff, 0.0)[None]), jnp.exp(b * jnp.maximum(-diff, 0.0)[None]))
    dda = jnp.where(low, diff[None] * jnp.exp(a * jnp.maximum(diff, 0.0)[None]), 0.0)
    ddb = jnp.where(up, -diff[None] * jnp.exp(b * jnp.maximum(-diff, 0.0)[None]), 0.0)
    rep = lambda w: jnp.broadcast_to(w[:, :, None], (RET_HEADS, c, 128))
    wqf = rep(jnp.exp(lg_f[:, None] * (idx + 1.0)[None]))
    wqb = rep(jnp.exp(lg_b[:, None] * (c - idx)[None]))
    wkf = rep(jnp.exp(lg_f[:, None] * (c - 1.0 - idx)[None]))
    wkb = rep(jnp.exp(lg_b[:, None] * idx[None]))
    cdf, cdb = jnp.exp(lg_f * c), jnp.exp(lg_b * c)
    dec_fb = jnp.broadcast_to(jnp.concatenate([cdf, cdb])[:, None], (8, 128))
    dec_bf = jnp.broadcast_to(jnp.concatenate([cdb, cdf])[:, None], (8, 128))
    return dict(dmat=dmat, dda=dda, ddb=ddb, wqf=wqf, wqb=wqb, wkf=wkf, wkb=wkb, dec_fb=dec_fb, dec_bf=dec_bf)


def _ret_states(xk, yv, w_fwd, w_rev, dec, name):
    seq = xk.shape[0]
    c = RET_CHUNK
    nc = seq // c

    def body(xf_ref, yf_ref, xr_ref, yr_ref, wf_ref, wr_ref, dec_ref, sf_ref, sr_ref, st_f, st_r):
        @pl.when(pl.program_id(0) == 0)
        def _():
            st_f[...] = jnp.zeros_like(st_f)
            st_r[...] = jnp.zeros_like(st_r)

        lane = _lane((c, 128))
        for h in range(RET_HEADS):
            pr, e = h // 2, h % 2
            half = (lane < 64) if e == 0 else (lane >= 64)
            for x_ref, y_ref, w_ref, s_ref, st, drow in ((xf_ref, yf_ref, wf_ref, sf_ref, st_f, h),
                                                        (xr_ref, yr_ref, wr_ref, sr_ref, st_r, 4 + h)):
                s_ref[h] = st[h].astype(s_ref.dtype)
                xm = jnp.where(half, x_ref[:, 128 * pr:128 * (pr + 1)].astype(F32) * w_ref[h], 0.0)
                st[h] = st[h] * dec_ref[drow:drow + 1, :] + _tn(_mx(xm), _mx(y_ref[:, 128 * h:128 * (h + 1)]))

    xs = lambda f: pl.BlockSpec((c, 256), f)
    ys = lambda f: pl.BlockSpec((c, 512), f)
    fwd, rev = (lambda n: (n, 0)), (lambda n: (nc - 1 - n, 0))
    wsp = pl.BlockSpec((RET_HEADS, c, 128), lambda n: (0, 0, 0))
    return pl.pallas_call(
        body, grid=(nc,), name=name,
        in_specs=[xs(fwd), ys(fwd), xs(rev), ys(rev), wsp, wsp, pl.BlockSpec((8, 128), lambda n: (0, 0))],
        out_specs=[pl.BlockSpec((None, RET_HEADS, 128, 128), lambda n: (n, 0, 0, 0)),
                   pl.BlockSpec((None, RET_HEADS, 128, 128), lambda n: (nc - 1 - n, 0, 0, 0))],
        out_shape=[jax.ShapeDtypeStruct((nc, RET_HEADS, 128, 128), MXU_DTYPE)] * 2,
        scratch_shapes=[pltpu.VMEM((RET_HEADS, 128, 128), F32), pltpu.VMEM((RET_HEADS, 128, 128), F32)],
        compiler_params=_cp("arbitrary"))(xk, yv, xk, yv, w_fwd, w_rev, dec)


def _ret_fwd(rq, rk, rv, rf, rb, tb):
    seq = rq.shape[0]
    c = RET_CHUNK

    def body(q_ref, k_ref, v_ref, rf_ref, rb_ref, d_ref, wqf_ref, wqb_ref, o_ref):
        lane = _lane((c, 128))
        for h in range(RET_HEADS):
            pr, e = h // 2, h % 2
            half = (lane < 64) if e == 0 else (lane >= 64)
            sl = slice(128 * pr, 128 * (pr + 1))
            qp = q_ref[:, sl]
            km = jnp.where(half, k_ref[:, sl], jnp.zeros_like(k_ref[:, sl]))
            sd = _mx(_nt(qp, km) * d_ref[h])
            qf = qp.astype(F32)
            o = _nn(sd, v_ref[:, 128 * h:128 * (h + 1)])
            o = o + _nn(_mx(qf * wqf_ref[h]), rf_ref[h]) + _nn(_mx(qf * wqb_ref[h]), rb_ref[h])
            o_ref[:, 128 * h:128 * (h + 1)] = o

    st = pl.BlockSpec((None, RET_HEADS, 128, 128), lambda n: (n, 0, 0, 0))
    wsp = pl.BlockSpec((RET_HEADS, c, 128), lambda n: (0, 0, 0))
    return pl.pallas_call(
        body, grid=(seq // c,), name="ret_fwd",
        in_specs=[pl.BlockSpec((c, 256), lambda n: (n, 0)), pl.BlockSpec((c, 256), lambda n: (n, 0)),
                  pl.BlockSpec((c, 512), lambda n: (n, 0)), st, st, pl.BlockSpec((RET_HEADS, c, c), lambda n: (0, 0, 0)), wsp, wsp],
        out_specs=pl.BlockSpec((c, 512), lambda n: (n, 0)),
        out_shape=jax.ShapeDtypeStruct((seq, 512), F32), compiler_params=_cp("parallel"))(
            rq, rk, rv, rf, rb, tb["dmat"], tb["wqf"], tb["wqb"])


def _ret_bwd(rq, rk, rv, do, rf, rb, hf, hb, tb):
    seq = rq.shape[0]
    c = RET_CHUNK

    def body(q_ref, k_ref, v_ref, do_ref, rf_ref, rb_ref, hf_ref, hb_ref, d_ref, dda_ref, ddb_ref,
             wqf_ref, wqb_ref, wkf_ref, wkb_ref, cdec_ref, dq_ref, dk_ref, dv_ref, dlg_ref):
        @pl.when(pl.program_id(0) == 0)
        def _():
            dlg_ref[...] = jnp.zeros_like(dlg_ref)

        lane = _lane((c, 128))
        pos = lax.broadcasted_iota(jnp.int32, (c, 128), 0).astype(F32)
        for pr in range(2):
            sl = slice(128 * pr, 128 * (pr + 1))
            qp, kp = q_ref[:, sl], k_ref[:, sl]
            qf, kf = qp.astype(F32), kp.astype(F32)
            dq_acc = jnp.zeros((c, 128), F32)
            dk_acc = jnp.zeros((c, 128), F32)
            for e in range(2):
                h = 2 * pr + e
                half = (lane < 64) if e == 0 else (lane >= 64)
                hs = slice(128 * h, 128 * (h + 1))
                km = jnp.where(half, kp, jnp.zeros_like(kp))
                kmf = km.astype(F32)
                vh, doh = v_ref[:, hs], do_ref[:, hs]
                rfh, rbh, hfh, hbh = rf_ref[h], rb_ref[h], hf_ref[h], hb_ref[h]
                s = _nt(qp, km)
                dpm = _nt(doh, vh)
                ds_b = _mx(dpm * d_ref[h])
                sd_b = _mx(s * d_ref[h])
                dq_if = wqf_ref[h] * _nt(doh, rfh)
                dq_ib = wqb_ref[h] * _nt(doh, rbh)
                dq_acc = dq_acc + _nn(ds_b, km) + dq_if + dq_ib
                dk_sf = wkf_ref[h] * _nt(vh, hfh)
                dk_sb = wkb_ref[h] * _nt(vh, hbh)
                dk_acc = dk_acc + jnp.where(half, _tn(ds_b, qp), 0.0) + dk_sf + dk_sb
                dv = _tn(sd_b, doh) + _nn(_mx(kmf * wkf_ref[h]), hfh) + _nn(_mx(kmf * wkb_ref[h]), hbh)
                dv_ref[:, hs] = dv.astype(dv_ref.dtype)
                sdp = s * dpm
                hr_f = hfh.astype(F32) * rfh.astype(F32)
                hr_b = hbh.astype(F32) * rbh.astype(F32)
                da = (_rowsum128(sdp * dda_ref[h]) + _rowsum128((pos + 1.0) * qf * dq_if)
                      + _rowsum128((c - 1.0 - pos) * kf * dk_sf) + cdec_ref[h:h + 1, :] * _rowsum128(hr_f))
                db = (_rowsum128(sdp * ddb_ref[h]) + _rowsum128((c - pos) * qf * dq_ib)
                      + _rowsum128(pos * kf * dk_sb) + cdec_ref[4 + h:5 + h, :] * _rowsum128(hr_b))
                dlg_ref[h:h + 1, :] += da
                dlg_ref[4 + h:5 + h, :] += db
            dq_ref[:, sl] = dq_acc
            dk_ref[:, sl] = dk_acc

    st = pl.BlockSpec((None, RET_HEADS, 128, 128), lambda n: (n, 0, 0, 0))
    wsp = pl.BlockSpec((RET_HEADS, c, 128), lambda n: (0, 0, 0))
    dsp = pl.BlockSpec((RET_HEADS, c, c), lambda n: (0, 0, 0))
    x256 = pl.BlockSpec((c, 256), lambda n: (n, 0))
    x512 = pl.BlockSpec((c, 512), lambda n: (n, 0))
    cdec = tb["dec_fb"] * float(c)
    return pl.pallas_call(
        body, grid=(seq // c,), name="ret_bwd",
        in_specs=[x256, x256, x512, x512, st, st, st, st, dsp, dsp, dsp, wsp, wsp, wsp, wsp,
                  pl.BlockSpec((8, 128), lambda n: (0, 0))],
        out_specs=[x256, x256, x512, pl.BlockSpec((8, 128), lambda n: (0, 0))],
        out_shape=[jax.ShapeDtypeStruct((seq, 256), F32), jax.ShapeDtypeStruct((seq, 256), F32),
                   jax.ShapeDtypeStruct((seq, 512), MXU_DTYPE), jax.ShapeDtypeStruct((8, 128), F32)],
        compiler_params=_cp("arbitrary"))(
            rq, rk, rv, do, rf, rb, hf, hb, tb["dmat"], tb["dda"], tb["ddb"], tb["wqf"], tb["wqb"], tb["wkf"], tb["wkb"], cdec)


def _tail(x, tgt, o_att, o_ret, p, mod, g_post, gn_g, wpt, wout):
    seq = x.shape[0]
    bs = 256
    nb = seq // bs

    def body(x_ref, t_ref, oa_ref, or_ref, za_ref, zr_ref, gl_ref, mod_ref, gp_ref, gn_ref, wpt_ref, wout_ref,
             dout_ref, dza_ref, dzr_ref, dgl_ref, doa_ref, dor_ref, del_ref, gout_ref, gpt_ref,
             dgate_ref, dgpost_ref, dgn_ref, loss_ref, gout_acc, gpt_acc):
        i = pl.program_id(0)

        @pl.when(i == 0)
        def _():
            gout_acc[...] = jnp.zeros_like(gout_acc)
            gpt_acc[...] = jnp.zeros_like(gpt_acc)
            dgate_ref[...] = jnp.zeros_like(dgate_ref)
            dgpost_ref[...] = jnp.zeros_like(dgpost_ref)
            dgn_ref[...] = jnp.zeros_like(dgn_ref)
            loss_ref[...] = jnp.zeros_like(loss_ref)

        oa, za, zr = oa_ref[...], za_ref[...], zr_ref[...]
        sga, sgr = _sigmoid(za), _sigmoid(zr)
        sza, szr = za * sga, zr * sgr
        ya_b = _mx(oa * sza)
        ons, rstds = [], []
        for h in range(RET_HEADS):
            oh = or_ref[:, 128 * h:128 * (h + 1)]
            xc = oh - jnp.mean(oh, axis=-1, keepdims=True)
            rstd = lax.rsqrt(jnp.mean(xc * xc, axis=-1, keepdims=True) + EPS)
            ons.append(xc * rstd)
            rstds.append(rstd)
        on = jnp.concatenate(ons, axis=1)
        yn = on * gn_ref[...]
        yr_b = _mx(yn * szr)
        wpa_t, wpr_t = wpt_ref[:, 0:512], wpt_ref[:, 512:1024]
        a_att = _nt(ya_b, wpa_t)
        a_ret = _nt(yr_b, wpr_t)
        gts = _sigmoid(gl_ref[...])
        g_att, g_ret = gts[:, 0:D_MODEL], gts[:, D_MODEL:2 * D_MODEL]
        merged_b = _mx(g_att * a_att + g_ret * a_ret)
        u = _nn(merged_b, wout_ref[...])
        r = lax.rsqrt(jnp.mean(u * u, axis=-1, keepdims=True) + EPS)
        un = u * r
        gpost = gp_ref[...]
        y = un * gpost
        gate = mod_ref[:, 2 * D_MODEL:3 * D_MODEL]
        err = x_ref[...] + gate * y - t_ref[...]
        loss_ref[...] += (0.5 / D_MODEL) * _rowsum128(err * err)
        dout = err * (1.0 / D_MODEL)
        dout_ref[...] = dout
        dgate_ref[...] += jnp.sum(dout * y, axis=0, keepdims=True)
        dy = dout * gate
        dgpost_ref[...] += jnp.sum(dy * un, axis=0, keepdims=True)
        dun = dy * gpost
        du_b = _mx(r * (dun - un * jnp.mean(dun * un, axis=-1, keepdims=True)))
        dmerged = _nt(du_b, wout_ref[...])
        gout_acc[...] += _tn(merged_b, du_b)
        d_att, d_ret = dmerged * g_att, dmerged * g_ret
        dgl_ref[:, 0:D_MODEL] = (d_att * a_att * (1.0 - g_att)).astype(dgl_ref.dtype)
        dgl_ref[:, D_MODEL:2 * D_MODEL] = (d_ret * a_ret * (1.0 - g_ret)).astype(dgl_ref.dtype)
        d_att_b, d_ret_b = _mx(d_att), _mx(d_ret)
        dya = _nn(d_att_b, wpa_t)
        dyr = _nn(d_ret_b, wpr_t)
        gpt_acc[:, 0:512] += _tn(d_att_b, ya_b)
        gpt_acc[:, 512:1024] += _tn(d_ret_b, yr_b)
        dza_ref[...] = (dya * oa * (sga * (1.0 + za * (1.0 - sga)))).astype(dza_ref.dtype)
        doa = dya * sza
        doa_ref[...] = doa.astype(doa_ref.dtype)
        dt = jnp.transpose(doa * oa)
        del_ref[...] = jnp.sum(dt.reshape(8, HEAD_DIM, bs), axis=1)
        dzr_ref[...] = (dyr * yn * (sgr * (1.0 + zr * (1.0 - sgr)))).astype(dzr_ref.dtype)
        dyn = dyr * szr
        dgn_ref[...] += jnp.sum(dyn * on, axis=0, keepdims=True)
        don = dyn * gn_ref[...]
        for h in range(RET_HEADS):
            hs = slice(128 * h, 128 * (h + 1))
            dh, oh = don[:, hs], ons[h]
            doh = rstds[h] * (dh - jnp.mean(dh, axis=-1, keepdims=True) - oh * jnp.mean(dh * oh, axis=-1, keepdims=True))
            dor_ref[:, hs] = doh.astype(dor_ref.dtype)

        @pl.when(i == nb - 1)
        def _():
            gout_ref[...] = gout_acc[...].astype(gout_ref.dtype)
            gpt_ref[...] = gpt_acc[...].astype(gpt_ref.dtype)

    row = lambda w: pl.BlockSpec((bs, w), lambda i: (i, 0))
    el = lambda w, off: pl.BlockSpec((pl.Element(bs), pl.Element(w)), lambda i: (i * bs, off))
    vec = lambda w: pl.BlockSpec((1, w), lambda i: (0, 0))
    full = pl.BlockSpec((D_MODEL, D_MODEL), lambda i: (0, 0))
    sds = jax.ShapeDtypeStruct
    return pl.pallas_call(
        body, grid=(nb,), name="tail",
        in_specs=[row(D_MODEL), row(D_MODEL), row(512), row(512), el(512, ZA), el(512, ZR), el(2 * D_MODEL, GL),
                  vec(3 * D_MODEL), vec(D_MODEL), vec(512), full, full],
        out_specs=[row(D_MODEL), row(512), row(512), row(2 * D_MODEL), row(512), row(512),
                   pl.BlockSpec((8, bs), lambda i: (0, i)), full, full, vec(D_MODEL), vec(D_MODEL), vec(512), vec(128)],
        out_shape=[sds((seq, D_MODEL), F32), sds((seq, 512), MXU_DTYPE), sds((seq, 512), MXU_DTYPE),
                   sds((seq, 2 * D_MODEL), MXU_DTYPE), sds((seq, 512), MXU_DTYPE), sds((seq, 512), MXU_DTYPE),
                   sds((8, seq), F32), sds((D_MODEL, D_MODEL), MXU_DTYPE), sds((D_MODEL, D_MODEL), MXU_DTYPE),
                   sds((1, D_MODEL), F32), sds((1, D_MODEL), F32), sds((1, 512), F32), sds((1, 128), F32)],
        scratch_shapes=[pltpu.VMEM((D_MODEL, D_MODEL), F32), pltpu.VMEM((D_MODEL, D_MODEL), F32)],
        compiler_params=_cp("arbitrary"))(x, tgt, o_att, o_ret, p, p, p, mod, g_post, gn_g, wpt, wout)


def _assemble(p, cos, sin, qg2, kg2, dqt, dkp, dvp, dza, drq, drk, drv, dzr, dgl):
    seq = p.shape[0]
    bs = 512

    def body(p_ref, cos_ref, sin_ref, qg_ref, kg_ref, dqt_ref, dkp_ref, dvp_ref, dza_ref, drq_ref, drk_ref, drv_ref,
             dzr_ref, dgl_ref, dp_ref, dqg_ref, dkg_ref):
        @pl.when(pl.program_id(0) == 0)
        def _():
            dqg_ref[...] = jnp.zeros_like(dqg_ref)
            dkg_ref[...] = jnp.zeros_like(dkg_ref)

        m = _blockdiag64()
        cs, sn = cos_ref[...], sin_ref[...]
        lo = _lane((bs, 128)) < 64

        def norm_bwd(raw, dn, g, dg_ref):
            r = lax.rsqrt(_seg64(raw * raw, m) * (1.0 / 64) + EPS)
            xn = raw * r
            dg_ref[...] += jnp.sum(dn * xn, axis=0, keepdims=True)
            dxn = dn * g
            return r * (dxn - xn * (_seg64(dxn * xn, m) * (1.0 / 64)))

        for pr in range(4):
            sl = slice(128 * pr, 128 * (pr + 1))
            dn = _rope_t(dqt_ref[:, sl] * 0.125, cs, sn)
            dp_ref[:, QA + 128 * pr:QA + 128 * (pr + 1)] = norm_bwd(p_ref[:, sl], dn, qg_ref[...], dqg_ref).astype(dp_ref.dtype)
        fold = lambda a: a + pltpu.roll(a, 64, 1)
        dk = jnp.where(lo, fold(dkp_ref[0]), fold(dkp_ref[1]))
        dp_ref[:, KA:KA + 128] = norm_bwd(p_ref[:, KA:KA + 128], _rope_t(dk, cs, sn), kg_ref[...], dkg_ref).astype(dp_ref.dtype)
        dp_ref[:, VA:VA + 128] = jnp.where(lo, fold(dvp_ref[0]), fold(dvp_ref[1])).astype(dp_ref.dtype)
        dp_ref[:, ZA:ZA + 512] = dza_ref[...]
        for pr in range(2):
            sl = slice(128 * pr, 128 * (pr + 1))
            dp_ref[:, QR + 128 * pr:QR + 128 * (pr + 1)] = _rope_t(drq_ref[:, sl], cs, sn).astype(dp_ref.dtype)
            dp_ref[:, KR + 128 * pr:KR + 128 * (pr + 1)] = _rope_t(drk_ref[:, sl] * 0.125, cs, sn).astype(dp_ref.dtype)
        dp_ref[:, VR:VR + 512] = drv_ref[...]
        dp_ref[:, ZR:ZR + 512] = dzr_ref[...]
        dp_ref[:, GL:GL + 2 * D_MODEL] = dgl_ref[...]

    row = lambda w: pl.BlockSpec((bs, w), lambda i: (i, 0))
    gsp = pl.BlockSpec((1, 128), lambda i: (0, 0))
    dup = pl.BlockSpec((2, bs, 128), lambda i: (0, i, 0))
    return pl.pallas_call(
        body, grid=(seq // bs,), name="assemble_dp",
        in_specs=[row(768), row(128), row(128), gsp, gsp, row(512), dup, dup, row(512), row(256), row(256), row(512),
                  row(512), row(2 * D_MODEL)],
        out_specs=[row(IN_WIDTH), gsp, gsp],
        out_shape=[jax.ShapeDtypeStruct((seq, IN_WIDTH), MXU_DTYPE), jax.ShapeDtypeStruct((1, 128), F32),
                   jax.ShapeDtypeStruct((1, 128), F32)],
        compiler_params=_cp("arbitrary"))(p, cos, sin, qg2, kg2, dqt, dkp, dvp, dza, drq, drk, drv, dzr, dgl)


def _local_step(x, tgt, mod, g_pre, qn_g, kn_g, w_dec_f, w_dec_b, gn_g, g_post, win_t, wpt, wout, send=None):
    send = send or (lambda name, arrays: None)
    seq = x.shape[0]
    cos, sin = _rope_tables(seq)
    qg2, kg2 = jnp.tile(qn_g, (1, 2)), jnp.tile(kn_g, (1, 2))
    lg_f = jax.nn.log_sigmoid(w_dec_f[0])
    lg_b = jax.nn.log_sigmoid(w_dec_b[0])
    tb = _ret_tables(lg_f, lg_b, RET_CHUNK)

    h = _prenorm(x, mod, g_pre)
    p = _matmul(h, win_t, ta=False, tb=True, tm=512, tn=IN_WIDTH // 2, out_dtype=F32, name="in_proj")
    qt, kd, vd, rq, rk, rv = _prep(p, cos, sin, qg2, kg2)
    o_att, lse = _attn_fwd(qt, kd, vd)
    rf, rb = _ret_states(rk, rv, tb["wkf"], tb["wkb"], tb["dec_fb"], "ret_states_fwd")
    o_ret = _ret_fwd(rq, rk, rv, rf, rb, tb)
    (dout, dza, dzr, dgl, do_att, do_ret, delta, g_out, g_pt, dgate, dgpost, dgn, loss_l) = _tail(
        x, tgt, o_att, o_ret, p, mod, g_post, gn_g, wpt, wout)
    dep = send("early", (g_pt, g_out))
    dqt, dkp, dvp = _attn_bwd(qt, kd, vd, do_att, lse, delta, dep=dep)
    hb, hf = _ret_states(rq, do_ret, tb["wqb"], tb["wqf"], tb["dec_bf"], "ret_states_bwd")
    drq, drk, drv, dlg = _ret_bwd(rq, rk, rv, do_ret, rf, rb, hf, hb, tb)
    dp, dqg, dkg = _assemble(p, cos, sin, qg2, kg2, dqt, dkp, dvp, dza, drq, drk, drv, dzr, dgl)
    g_in_t = _matmul(dp, h, ta=True, tb=False, tm=256, tn=D_MODEL, out_dtype=MXU_DTYPE, name="in_proj_dw")
    dep = send("late", (g_in_t,))
    dh = _matmul(dp, win_t, ta=False, tb=False, tm=512, tn=D_MODEL, out_dtype=F32, name="in_proj_dx", dep=dep)
    grad_x, dshift, dscale, dgpre = _prenorm_bwd(x, dh, dout, mod, g_pre)

    dlg = jnp.sum(dlg, axis=1)
    small = dict(
        dmod=jnp.concatenate([dshift, dscale, dgate], axis=1),
        g_pre=dgpre, g_post=dgpost, gn_g=dgn,
        qn_g=dqg[:, :64] + dqg[:, 64:], kn_g=dkg[:, :64] + dkg[:, 64:],
        w_dec_f=(dlg[0:4] * jax.nn.sigmoid(-w_dec_f[0]))[None], w_dec_b=(dlg[4:8] * jax.nn.sigmoid(-w_dec_b[0]))[None],
        loss=jnp.sum(loss_l, axis=1, keepdims=True))
    return grad_x, g_in_t, g_pt, g_out, small


N_DEV = 8
N_CHIP = 4
HBM_SPEC = pl.BlockSpec(memory_space=pl.ANY)
VMEM_SPEC = pl.BlockSpec(memory_space=pltpu.VMEM)
SHARD_ROWS = (IN_WIDTH // N_CHIP, D_MODEL // N_CHIP, D_MODEL // N_CHIP)


def _coords():
    return lax.axis_index("x"), lax.axis_index("y"), lax.axis_index("c")


def _peer(k):
    x, y, c = _coords()
    return (1 - x if k & 4 else x, 1 - y if k & 2 else y, 1 - c if k & 1 else c)


def _dev_index(p):
    return 4 * p[0] + 2 * p[1] + p[2]


def _rows(ref, start, size):
    return ref.at[pl.ds(pl.multiple_of(start, 16), size), :]


def _remote(src, dst, ssem, rsem, dev):
    return pltpu.make_async_remote_copy(src_ref=src, dst_ref=dst, send_sem=ssem, recv_sem=rsem, device_id=dev,
                                        device_id_type=MESH)


def _mod_exchange(c, w_ada_s, b4):
    ncol = w_ada_s.shape[1]

    def body(c_ref, w_ref, b_ref, cs_ref, mod_ref, modsh, modall, ssem, rsem):
        me = _dev_index(_coords())
        chip = me // 2
        cs_ref[me] = c_ref[...]
        sends = []
        for k in range(1, N_DEV):
            cp = _remote(c_ref, cs_ref.at[me], ssem.at[k - 1], rsem.at[k - 1], _peer(k))
            cp.start()
            sends.append(cp)
        for k in range(1, N_DEV):
            slot = cs_ref.at[_dev_index(_peer(k))]
            _remote(slot, slot, ssem.at[k - 1], rsem.at[k - 1], _peer(k)).wait_recv()
        cs = jnp.concatenate([cs_ref[d] for d in range(N_DEV)], axis=0)
        ms = _nn(cs * _sigmoid(cs), w_ref[...], precision=HIGHEST) + b_ref[pl.ds(chip, 1), :]
        modsh[...] = ms
        modall[chip] = ms
        for k2 in range(1, N_CHIP):
            cp = _remote(modsh, modall.at[chip], ssem.at[6 + k2], rsem.at[6 + k2], _peer(2 * k2))
            cp.start()
            sends.append(cp)
        for k2 in range(1, N_CHIP):
            slot = modall.at[_dev_index(_peer(2 * k2)) // 2]
            _remote(slot, slot, ssem.at[6 + k2], rsem.at[6 + k2], _peer(2 * k2)).wait_recv()
        for jj in range(N_CHIP):
            mod_ref[:, ncol * jj:ncol * (jj + 1)] = modall[jj, pl.ds(me, 1), :]
        for cp in sends:
            cp.wait_send()

    return pl.pallas_call(
        body, name="mod_exchange", in_specs=[VMEM_SPEC] * 3, out_specs=[VMEM_SPEC] * 2,
        out_shape=[jax.ShapeDtypeStruct((N_DEV, 1, D_MODEL), F32), jax.ShapeDtypeStruct((1, N_CHIP * ncol), F32)],
        scratch_shapes=[pltpu.VMEM((N_DEV, ncol), F32), pltpu.VMEM((N_CHIP, N_DEV, ncol), F32),
                        pltpu.SemaphoreType.DMA((10,)), pltpu.SemaphoreType.DMA((10,))],
        compiler_params=_cp())(c, w_ada_s, b4)


GATHER_CHUNK = 32


def _chunks(kinds, chunk):
    out = []
    for t, kind in enumerate(kinds):
        half = SHARD_ROWS[kind] // 2
        step = chunk if half % chunk == 0 else half
        out += [(t, off, step) for off in range(0, half, step)]
    return out


def _weight_gather(shards):
    nt = len(shards)
    pieces = _chunks(tuple(range(nt)), GATHER_CHUNK)
    npc = len(pieces)

    def body(*refs):
        srcs, outs = refs[:nt], refs[nt:2 * nt]
        lsem, ssem, rsem = refs[2 * nt:]
        x, y, c = _coords()
        chip = 2 * x + y
        sib = (x, y, 1 - c)

        def place(t, ch, cc, off, n):
            return _rows(outs[t], ch * SHARD_ROWS[t] + cc * (SHARD_ROWS[t] // 2) + off, n)

        local = [pltpu.make_async_copy(srcs[t].at[pl.ds(0, SHARD_ROWS[t]), :], _rows(outs[t], chip * SHARD_ROWS[t], SHARD_ROWS[t]),
                                       lsem.at[t]) for t in range(nt)]
        for cp in local:
            cp.start()
        sends = []
        for k2 in range(1, N_CHIP):
            for q, (t, off, n) in enumerate(pieces):
                i = (k2 - 1) * npc + q
                cp = _remote(_rows(srcs[t], c * (SHARD_ROWS[t] // 2) + off, n), place(t, chip, c, off, n),
                             ssem.at[i], rsem.at[i], _peer(2 * k2))
                cp.start()
                sends.append(cp)
        for k2 in range(1, N_CHIP):
            pchip = _dev_index(_peer(2 * k2)) // 2
            for q, (t, off, n) in enumerate(pieces):
                i = (k2 - 1) * npc + q
                got = place(t, pchip, c, off, n)
                _remote(got, got, ssem.at[i], rsem.at[i], _peer(2 * k2)).wait_recv()
                cp = _remote(got, got, ssem.at[3 * npc + i], rsem.at[3 * npc + i], sib)
                cp.start()
                sends.append(cp)
        for k2 in range(1, N_CHIP):
            pchip = _dev_index(_peer(2 * k2)) // 2
            for q, (t, off, n) in enumerate(pieces):
                i = (k2 - 1) * npc + q
                got = place(t, pchip, 1 - c, off, n)
                _remote(got, got, ssem.at[3 * npc + i], rsem.at[3 * npc + i], sib).wait_recv()
        for cp in sends:
            cp.wait_send()
        for cp in local:
            cp.wait()

    return pl.pallas_call(
        body, name="weight_gather", in_specs=[VMEM_SPEC] * nt, out_specs=[HBM_SPEC] * nt,
        out_shape=[jax.ShapeDtypeStruct((N_CHIP * SHARD_ROWS[t], s.shape[1]), s.dtype) for t, s in enumerate(shards)],
        scratch_shapes=[pltpu.SemaphoreType.DMA((nt,)), pltpu.SemaphoreType.DMA((6 * npc,)), pltpu.SemaphoreType.DMA((6 * npc,))],
        compiler_params=_cp())(*shards)


SEM_SPEC = pl.BlockSpec(memory_space=pltpu.SEMAPHORE)
HBM_ONLY = pl.BlockSpec(memory_space=pltpu.HBM)
DATAFLOW = pltpu.SideEffectType.DATAFLOW_SIDE_EFFECTING


def _reduced_by(ref, t, dev):
    return _rows(ref, (dev // 2) * SHARD_ROWS[t] + (dev % 2) * (SHARD_ROWS[t] // 2), SHARD_ROWS[t] // 2)


def _scatter_start(grads, kinds, name):
    n = len(grads)

    def body(*refs):
        srcs, lands = refs[:n], refs[n:2 * n]
        ssem, rsem = refs[2 * n], refs[2 * n + 1]
        token = refs[-1]
        me = _dev_index(_coords())
        for k in range(1, N_DEV):
            for i, t in enumerate(kinds):
                q = (k - 1) * n + i
                _remote(_reduced_by(srcs[i], t, _dev_index(_peer(k))), lands[i].at[me], ssem.at[q], rsem.at[q], _peer(k)).start()
        token[...] = jnp.zeros_like(token)

    zones = [lax.empty((N_DEV, SHARD_ROWS[t] // 2, g.shape[1]), g.dtype) for g, t in zip(grads, kinds)]
    hbm = lambda a: pltpu.with_memory_space_constraint(a, pltpu.HBM)
    dma = pltpu.SemaphoreType.DMA
    outs = pl.pallas_call(
        body, name=name, in_specs=[HBM_ONLY] * (2 * n), out_specs=[SEM_SPEC, SEM_SPEC] + [HBM_ONLY] * (2 * n) + [VMEM_SPEC],
        out_shape=[dma((7 * n,)), dma((7 * n,))] + [pltpu.HBM(a.shape, a.dtype) for a in list(grads) + zones]
        + [jax.ShapeDtypeStruct((8, 128), F32)],
        input_output_aliases={i: 2 + i for i in range(2 * n)},
        compiler_params=pltpu.CompilerParams(has_side_effects=DATAFLOW))(*[hbm(a) for a in list(grads) + zones])
    return outs[0], outs[1], outs[2:2 + n], outs[2 + n:2 + 2 * n], outs[-1]


def _scatter_wait(started, kinds, after, name):
    ssem, rsem, grads, zones, _ = started
    n = len(grads)

    def body(*refs):
        srcs, lands = refs[:n], refs[n:2 * n]
        ssem_ref, rsem_ref = refs[2 * n], refs[2 * n + 1]
        for k in range(1, N_DEV):
            peer = _dev_index(_peer(k))
            for i, t in enumerate(kinds):
                q = (k - 1) * n + i
                cp = _remote(_reduced_by(srcs[i], t, peer), lands[i].at[peer], ssem_ref.at[q], rsem_ref.at[q], _peer(k))
                cp.wait_send()
                cp.wait_recv()

    outs = pl.pallas_call(
        body, name=name, in_specs=[HBM_ONLY] * (2 * n) + [SEM_SPEC, SEM_SPEC, HBM_SPEC], out_specs=[HBM_ONLY] * (2 * n),
        out_shape=[pltpu.HBM(a.shape, a.dtype) for a in list(grads) + list(zones)],
        input_output_aliases={i: i for i in range(2 * n)},
        compiler_params=pltpu.CompilerParams(has_side_effects=DATAFLOW))(*grads, *zones, ssem, rsem, after)
    return outs[:n], outs[n:]


def _grad_finish(grads, zones, kinds, name):
    nt = len(grads)
    pieces = _chunks(kinds, GATHER_CHUNK)
    npc = len(pieces)
    shard = [SHARD_ROWS[k] for k in kinds]

    def body(*refs):
        srcs, lands, outs = refs[:nt], refs[nt:2 * nt], refs[2 * nt:3 * nt]
        slots, sums = refs[3 * nt:4 * nt], refs[4 * nt:5 * nt]
        lsem, osem, xsem, ysem = refs[5 * nt:]
        x, y, c = _coords()
        me = _dev_index((x, y, c))
        sib = (x, y, 1 - c)
        loads = [pltpu.make_async_copy(_reduced_by(srcs[t], kinds[t], me), slots[t].at[me], lsem.at[t]) for t in range(nt)]
        for k in range(1, N_DEV):
            peer = _dev_index(_peer(k))
            loads += [pltpu.make_async_copy(lands[t].at[peer], slots[t].at[peer], lsem.at[k * nt + t]) for t in range(nt)]
        for cp in loads:
            cp.start()
        for cp in loads:
            cp.wait()
        sends = []
        place = lambda t, cc, off, n: _rows(outs[t], cc * (shard[t] // 2) + off, n)
        stores = []
        for q, (t, off, n) in enumerate(pieces):
            r = pl.ds(off, n)
            acc = slots[t][0, r, :].astype(F32)
            for d in range(1, N_DEV):
                acc = acc + slots[t][d, r, :].astype(F32)
            sums[t][r, :] = acc
            keep = pltpu.make_async_copy(sums[t].at[r, :], place(t, c, off, n), osem.at[q])
            give = _remote(sums[t].at[r, :], place(t, c, off, n), xsem.at[q], ysem.at[q], sib)
            keep.start()
            give.start()
            stores.append(keep)
            sends.append(give)
        for q, (t, off, n) in enumerate(pieces):
            got = place(t, 1 - c, off, n)
            _remote(got, got, xsem.at[q], ysem.at[q], sib).wait_recv()
        for cp in sends:
            cp.wait_send()
        for cp in stores:
            cp.wait()

    dma = pltpu.SemaphoreType.DMA
    return pl.pallas_call(
        body, name=name, in_specs=[HBM_SPEC] * (2 * nt), out_specs=[HBM_SPEC] * nt,
        out_shape=[jax.ShapeDtypeStruct((shard[t], g.shape[1]), F32) for t, g in enumerate(grads)],
        scratch_shapes=([pltpu.VMEM((N_DEV, shard[t] // 2, g.shape[1]), g.dtype) for t, g in enumerate(grads)]
                        + [pltpu.VMEM((shard[t] // 2, g.shape[1]), F32) for t, g in enumerate(grads)]
                        + [dma((N_DEV * nt,)), dma((npc,)), dma((npc,)), dma((npc,))]),
        compiler_params=_cp())(*grads, *zones)


def _adam_math(w, g, m, v):
    m = ADAM_B1 * m + (1.0 - ADAM_B1) * g
    v = ADAM_B2 * v + (1.0 - ADAM_B2) * (g * g)
    m_hat = m / (1.0 - ADAM_B1 ** ADAM_STEP)
    v_hat = v / (1.0 - ADAM_B2 ** ADAM_STEP)
    return -ADAM_LR * (m_hat / (jnp.sqrt(v_hat) + ADAM_EPS) + ADAM_WD * w), m, v


def _adamw(w, g, m, v, rb, name):
    rows, cols = w.shape

    def body(w_ref, g_ref, m_ref, v_ref, d_ref, nm_ref, nv_ref):
        d_ref[...], nm_ref[...], nv_ref[...] = _adam_math(w_ref[...], g_ref[...], m_ref[...], v_ref[...])

    spec = pl.BlockSpec((rb, cols), lambda i: (i, 0))
    return pl.pallas_call(body, grid=(rows // rb,), name=name, in_specs=[spec] * 4, out_specs=[spec] * 3,
                          out_shape=[jax.ShapeDtypeStruct(w.shape, F32)] * 3, compiler_params=_cp("parallel"))(w, g, m, v)


PACK = (("b_ada", 3 * D_MODEL), ("g_pre", D_MODEL), ("g_post", D_MODEL), ("gn_g", 512), ("qn_g", 64), ("kn_g", 64),
        ("w_dec_f", 4), ("w_dec_b", 4), ("loss", 1))
PACK_OFFSETS = {}
PACK_WIDTH = 0
for _name, _n in PACK:
    PACK_OFFSETS[_name] = PACK_WIDTH
    PACK_WIDTH += -(-_n // 128) * 128
SMALL_PARAMS = tuple(name for name, _ in PACK if name != "loss")


def _pack(parts):
    cols = []
    for name, n in PACK:
        pad = -(-n // 128) * 128 - n
        cols.append(parts[name].reshape(1, n).astype(F32))
        if pad:
            cols.append(jnp.zeros((1, pad), F32))
    return jnp.concatenate(cols, axis=1)


def _small_reduce(vec, cs, dep):
    ncol = 3 * D_MODEL // N_CHIP

    def body(vec_ref, cs_ref, dep_ref, tot_ref, gwa_ref, gat, ssem, rsem):
        me = _dev_index(_coords())
        chip = me // 2
        gat[me] = vec_ref[...]
        sends = []
        for k in range(1, N_DEV):
            cp = _remote(vec_ref, gat.at[me], ssem.at[k - 1], rsem.at[k - 1], _peer(k))
            cp.start()
            sends.append(cp)
        for k in range(1, N_DEV):
            slot = gat.at[_dev_index(_peer(k))]
            _remote(slot, slot, ssem.at[k - 1], rsem.at[k - 1], _peer(k)).wait_recv()
        rows = [gat[d] for d in range(N_DEV)]
        tot = rows[0]
        for d in range(1, N_DEV):
            tot = tot + rows[d]
        tot_ref[...] = tot
        cs = cs_ref[...]
        ca_t = jnp.transpose(jnp.concatenate([cs * _sigmoid(cs), jnp.zeros((128 - N_DEV, D_MODEL), F32)], axis=0))
        dm = jnp.concatenate(rows + [jnp.zeros((128 - N_DEV, PACK_WIDTH), F32)], axis=0)
        for jj in range(N_CHIP):
            @pl.when(chip == jj)
            def _():
                gwa_ref[...] = _nn(ca_t, dm[:, ncol * jj:ncol * (jj + 1)], precision=HIGHEST)
        for cp in sends:
            cp.wait_send()

    return pl.pallas_call(
        body, name="small_reduce", in_specs=[VMEM_SPEC, VMEM_SPEC, HBM_SPEC], out_specs=[VMEM_SPEC] * 2,
        out_shape=[jax.ShapeDtypeStruct((1, PACK_WIDTH), F32), jax.ShapeDtypeStruct((D_MODEL, ncol), F32)],
        scratch_shapes=[pltpu.VMEM((N_DEV, 1, PACK_WIDTH), F32), pltpu.SemaphoreType.DMA((7,)), pltpu.SemaphoreType.DMA((7,))],
        compiler_params=_cp())(vec, cs, dep)


def _small_adamw(tot, given):
    sizes = dict(PACK)
    np_ = len(SMALL_PARAMS)

    def body(*refs):
        tot_ref = refs[0]
        wmv = refs[1:1 + 3 * np_]
        outs = refs[1 + 3 * np_:1 + 7 * np_]
        loss_ref = refs[-1]
        for i, name in enumerate(SMALL_PARAMS):
            off, n = PACK_OFFSETS[name], sizes[name]
            g = tot_ref[:, off:off + n]
            w_ref, m_ref, v_ref = wmv[3 * i:3 * i + 3]
            outs[i][...] = g
            outs[np_ + i][...], outs[2 * np_ + i][...], outs[3 * np_ + i][...] = _adam_math(w_ref[...], g, m_ref[...], v_ref[...])
        loss_ref[...] = tot_ref[:, PACK_OFFSETS["loss"]:PACK_OFFSETS["loss"] + 1]

    flat = [a for name in SMALL_PARAMS for a in given[name]]
    shapes = [jax.ShapeDtypeStruct((1, sizes[name]), F32) for name in SMALL_PARAMS]
    res = pl.pallas_call(body, name="small_adamw", in_specs=[VMEM_SPEC] * (1 + 3 * np_), out_specs=[VMEM_SPEC] * (4 * np_ + 1),
                         out_shape=shapes * 4 + [jax.ShapeDtypeStruct((1, 1), F32)], compiler_params=_cp())(tot, *flat)
    return [dict(zip(SMALL_PARAMS, res[k * np_:(k + 1) * np_])) for k in range(4)], res[-1]


def kernel(x, c, w_ada, b_ada, g_pre, w_in, qn_g, kn_g, w_dec_f, w_dec_b, gn_g, w_pa, w_pr, w_out, g_post, loss_target, m_w_ada, m_b_ada, m_g_pre, m_w_in, m_qn_g, m_kn_g, m_w_dec_f, m_w_dec_b, m_gn_g, m_w_pa, m_w_pr, m_w_out, m_g_post, v_w_ada, v_b_ada, v_g_pre, v_w_in, v_qn_g, v_kn_g, v_w_dec_f, v_w_dec_b, v_gn_g, v_w_pa, v_w_pr, v_w_out, v_g_post):
    shards = (w_in[0].T.astype(MXU_DTYPE), jnp.concatenate([w_pa[0].T, w_pr[0].T], axis=1).astype(MXU_DTYPE),
              w_out[0].astype(MXU_DTYPE))
    win_t, wpt, wout = _weight_gather(shards)
    cs, mod = _mod_exchange(c, w_ada[0], b_ada.reshape(N_CHIP, 3 * D_MODEL // N_CHIP))

    kinds = {"early": (1, 2), "late": (0,)}
    started = {}

    def send(name, arrays):
        started[name] = _scatter_start(arrays, kinds[name], "grad_scatter_" + name)
        return started[name][-1]

    grad_x, _, _, _, small = _local_step(x[0], loss_target[0], mod, g_pre, qn_g, kn_g, w_dec_f, w_dec_b,
                                         gn_g, g_post, win_t, wpt, wout, send=send)
    small = dict(small)
    small["b_ada"] = small.pop("dmod")
    given = dict(b_ada=(b_ada, m_b_ada, v_b_ada), g_pre=(g_pre, m_g_pre, v_g_pre), g_post=(g_post, m_g_post, v_g_post),
                 gn_g=(gn_g, m_gn_g, v_gn_g), qn_g=(qn_g, m_qn_g, v_qn_g), kn_g=(kn_g, m_kn_g, v_kn_g),
                 w_dec_f=(w_dec_f, m_w_dec_f, v_w_dec_f), w_dec_b=(w_dec_b, m_w_dec_b, v_w_dec_b))
    grads, deltas, new_m, new_v = {}, {}, {}, {}

    def update(name, w, g, m, v, back):
        d, nm, nv = _adamw(w, g, m, v, w.shape[0] // 4, "adamw_" + name)
        grads[name], deltas[name], new_m[name], new_v[name] = back(g), back(d), back(nm), back(nv)
        return d

    lead = lambda a: a[None]
    (g_pt, g_out), (z_pt, z_out) = _scatter_wait(started["early"], kinds["early"], grad_x, "grad_scatter_early_wait")
    r_pt, r_out = _grad_finish((g_pt, g_out), (z_pt, z_out), kinds["early"], "grad_finish_early")
    update("w_pa", w_pa[0], r_pt[:, :512].T, m_w_pa[0], v_w_pa[0], lead)
    update("w_pr", w_pr[0], r_pt[:, 512:].T, m_w_pr[0], v_w_pr[0], lead)
    last = update("w_out", w_out[0], r_out, m_w_out[0], v_w_out[0], lead)
    tot, g_w_ada = _small_reduce(_pack(small), cs.reshape(N_DEV, D_MODEL), last)
    small_parts, loss = _small_adamw(tot, given)
    for dst, part in zip((grads, deltas, new_m, new_v), small_parts):
        dst.update(part)
    last = update("w_ada", w_ada[0], g_w_ada, m_w_ada[0], v_w_ada[0], lead)

    (g_in_t,), (z_in,) = _scatter_wait(started["late"], kinds["late"], last, "grad_scatter_late_wait")
    (r_in,) = _grad_finish((g_in_t,), (z_in,), kinds["late"], "grad_finish_late")
    tr = lambda a: a[0].T
    update("w_in", tr(w_in), r_in, tr(m_w_in), tr(v_w_in), lambda a: a.T[None])
    order = ("w_ada", "b_ada", "g_pre", "w_in", "qn_g", "kn_g", "w_dec_f", "w_dec_b", "gn_g", "w_pa", "w_pr", "w_out", "g_post")
    return (loss[0, 0], grad_x[None], *[grads[n] for n in order], *[deltas[n] for n in order],
            *[new_m[n] for n in order], *[new_v[n] for n in order])
```

```python
import functools

import jax
import jax.numpy as jnp
import numpy as np
from jax import lax
from jax.experimental import pallas as pl
from jax.experimental.pallas import tpu as pltpu

F32 = jnp.float32
BF16 = jnp.bfloat16
MXU_DTYPE = jnp.bfloat16

D_MODEL = 1024
GRID_W = 64
HEAD_DIM = 64
ROPE_THETA = 10000.0
EPS = 1e-6
RET_HEADS = 4
RET_CHUNK = 256
IN_WIDTH = 4864
QA, KA, VA, ZA, QR, KR, VR, ZR, GL = 0, 512, 640, 768, 1280, 1536, 1792, 2304, 2816
ATTN_COLS = ZA
ZA_B, QR_B, KR_B, VR_B, ZR_B, GL_B = (c - ATTN_COLS for c in (ZA, QR, KR, VR, ZR, GL))

ADAM_LR, ADAM_B1, ADAM_B2, ADAM_EPS, ADAM_WD, ADAM_STEP = 0.001, 0.9, 0.999, 1e-08, 0.01, 10

VMEM_LIMIT = 56 * 1024 * 1024
MESH = pl.DeviceIdType.MESH
HIGHEST = lax.Precision.HIGHEST
LOG2E = 1.4426950408889634
LN2 = 0.6931471805599453


def _cp(*sem, **kw):
    if sem:
        kw["dimension_semantics"] = sem
    return pltpu.CompilerParams(vmem_limit_bytes=VMEM_LIMIT, **kw)


def _nn(a, b, **kw):
    return lax.dot_general(a, b, (((1,), (0,)), ((), ())), preferred_element_type=F32, **kw)


def _nt(a, b):
    return lax.dot_general(a, b, (((1,), (1,)), ((), ())), preferred_element_type=F32)


def _tn(a, b):
    return lax.dot_general(a, b, (((0,), (0,)), ((), ())), preferred_element_type=F32)


def _mx(x):
    return x.astype(MXU_DTYPE)


def _lane(shape):
    return lax.broadcasted_iota(jnp.int32, shape, 1)


def _sigmoid(z):
    return 1.0 / (1.0 + jnp.exp(-z))


def _rowsum128(x):
    s = jnp.sum(x, axis=0, keepdims=True)
    out = s[:, 0:128]
    for k in range(1, x.shape[1] // 128):
        out = out + s[:, 128 * k:128 * (k + 1)]
    return out


def _swap16(x):
    return jnp.where((_lane(x.shape) & 16) == 0, pltpu.roll(x, 112, 1), pltpu.roll(x, 16, 1))


def _rope(x, cos, sin):
    return x * cos + _swap16(x) * sin


def _rope_t(d, cos, sin):
    return d * cos + _swap16(d * sin)


def _blockdiag64():
    r = lax.broadcasted_iota(jnp.int32, (128, 128), 0) // 64
    c = lax.broadcasted_iota(jnp.int32, (128, 128), 1) // 64
    return (r == c).astype(BF16)


def _seg64(x, m):
    hi = x.astype(BF16)
    lo = (x - hi.astype(F32)).astype(BF16)
    return _nn(hi, m) + _nn(lo, m)


def _rope_tables(seq):
    t = np.arange(seq)
    row = (t // GRID_W).astype(np.float32)
    col = (t % GRID_W).astype(np.float32)
    half = HEAD_DIM // 2
    inv_freq = (np.float32(ROPE_THETA) ** (-np.arange(0, half, 2, dtype=np.float32) / np.float32(half))).astype(np.float32)
    ar = (row[:, None] * inv_freq[None, :]).astype(np.float32).astype(np.float64)
    ac = (col[:, None] * inv_freq[None, :]).astype(np.float32).astype(np.float64)
    cr, sr, cc, sc = np.cos(ar), np.sin(ar), np.cos(ac), np.sin(ac)
    cos64 = np.concatenate([cr, cr, cc, cc], axis=1)
    sin64 = np.concatenate([-sr, sr, -sc, sc], axis=1)
    return jnp.asarray(np.tile(cos64, (1, 2)), F32), jnp.asarray(np.tile(sin64, (1, 2)), F32)


def _prenorm(x, mod, g_pre):
    seq = x.shape[0]
    bs = 512

    def body(x_ref, mod_ref, g_ref, h_ref):
        xv = x_ref[...]
        r = lax.rsqrt(jnp.mean(xv * xv, axis=-1, keepdims=True) + EPS)
        shift = mod_ref[:, 0:D_MODEL]
        scale = mod_ref[:, D_MODEL:2 * D_MODEL]
        h_ref[...] = ((xv * r) * g_ref[...] * (1.0 + scale) + shift).astype(h_ref.dtype)

    return pl.pallas_call(
        body, grid=(seq // bs,), name="prenorm",
        in_specs=[pl.BlockSpec((bs, D_MODEL), lambda i: (i, 0)), pl.BlockSpec((1, 3 * D_MODEL), lambda i: (0, 0)),
                  pl.BlockSpec((1, D_MODEL), lambda i: (0, 0))],
        out_specs=pl.BlockSpec((bs, D_MODEL), lambda i: (i, 0)),
        out_shape=jax.ShapeDtypeStruct((seq, D_MODEL), MXU_DTYPE), compiler_params=_cp("parallel"))(x, mod, g_pre)


def _prenorm_bwd(x, dh, dout, mod, g_pre):
    seq = x.shape[0]
    bs = 512

    def body(x_ref, dh_ref, dout_ref, mod_ref, g_ref, gx_ref, dshift_ref, dscale_ref, dg_ref):
        @pl.when(pl.program_id(0) == 0)
        def _():
            dshift_ref[...] = jnp.zeros_like(dshift_ref)
            dscale_ref[...] = jnp.zeros_like(dscale_ref)
            dg_ref[...] = jnp.zeros_like(dg_ref)

        xv = x_ref[...]
        dh = dh_ref[...]
        g = g_ref[...]
        r = lax.rsqrt(jnp.mean(xv * xv, axis=-1, keepdims=True) + EPS)
        xn = xv * r
        scale = mod_ref[:, D_MODEL:2 * D_MODEL]
        dshift_ref[...] += jnp.sum(dh, axis=0, keepdims=True)
        dscale_ref[...] += jnp.sum(dh * (xn * g), axis=0, keepdims=True)
        da = dh * (1.0 + scale)
        dg_ref[...] += jnp.sum(da * xn, axis=0, keepdims=True)
        dxn = da * g
        dx = r * (dxn - xn * jnp.mean(dxn * xn, axis=-1, keepdims=True))
        gx_ref[...] = dout_ref[...] + dx

    row = pl.BlockSpec((bs, D_MODEL), lambda i: (i, 0))
    vec = pl.BlockSpec((1, D_MODEL), lambda i: (0, 0))
    return pl.pallas_call(
        body, grid=(seq // bs,), name="prenorm_bwd",
        in_specs=[row, row, row, pl.BlockSpec((1, 3 * D_MODEL), lambda i: (0, 0)), vec],
        out_specs=[row, vec, vec, vec],
        out_shape=[jax.ShapeDtypeStruct((seq, D_MODEL), F32)] + [jax.ShapeDtypeStruct((1, D_MODEL), F32)] * 3,
        compiler_params=_cp("arbitrary"))(x, dh, dout, mod, g_pre)


def _dep_args(dep, grid_rank):
    if dep is None:
        return [], []
    return [dep], [pl.BlockSpec(dep.shape, lambda *_: (0,) * dep.ndim)]


def _matmul(a, b, *, ta, tb, tm, tn, out_dtype, name, dep=None, b_rows=None):
    kdim = a.shape[0] if ta else a.shape[1]
    m = a.shape[1] if ta else a.shape[0]
    n = b.shape[0] if tb else b.shape[1]
    if b_rows is not None:
        assert tb
        n = b_rows[1]
    assert m % tm == 0 and n % tn == 0
    deps, dep_specs = _dep_args(dep, 2)

    def body(a_ref, b_ref, *rest):
        o_ref = rest[-1]
        dims = (((0 if ta else 1,), (1 if tb else 0,)), ((), ()))
        o_ref[...] = lax.dot_general(a_ref[...], b_ref[...], dims, preferred_element_type=F32).astype(o_ref.dtype)

    a_spec = pl.BlockSpec((kdim, tm), lambda j, i: (0, i)) if ta else pl.BlockSpec((tm, kdim), lambda j, i: (i, 0))
    b_spec = pl.BlockSpec((tn, kdim), lambda j, i: (j, 0)) if tb else pl.BlockSpec((kdim, tn), lambda j, i: (0, j))
    if b_rows is not None:
        assert b_rows[0] % 128 == 0 and tn % 128 == 0
        b_spec = pl.BlockSpec((pl.Element(tn), pl.Element(kdim)), lambda j, i: (pl.multiple_of(b_rows[0] + j * tn, 128), 0))
    return pl.pallas_call(
        body, grid=(n // tn, m // tm), name=name, in_specs=[a_spec, b_spec] + dep_specs,
        out_specs=pl.BlockSpec((tm, tn), lambda j, i: (i, j)),
        out_shape=jax.ShapeDtypeStruct((m, n), out_dtype), compiler_params=_cp("parallel", "parallel"))(a, b, *deps)


def _prep_attn(p_a, cos, sin, qg2, kg2):
    seq = p_a.shape[0]
    bs = 512

    def body(p_ref, cos_ref, sin_ref, qg_ref, kg_ref, qt_ref, kd_ref, vd_ref):
        m = _blockdiag64()
        cs, sn = cos_ref[...], sin_ref[...]
        lo = _lane((bs, 128)) < 64
        for pr in range(4):
            q = p_ref[:, QA + 128 * pr:QA + 128 * (pr + 1)]
            r = lax.rsqrt(_seg64(q * q, m) * (1.0 / 64) + EPS)
            qt_ref[:, 128 * pr:128 * (pr + 1)] = (_rope(q * r * qg_ref[...], cs, sn) * (0.125 * LOG2E)).astype(qt_ref.dtype)
        k = p_ref[:, KA:KA + 128]
        r = lax.rsqrt(_seg64(k * k, m) * (1.0 / 64) + EPS)
        kr = _rope(k * r * kg_ref[...], cs, sn)
        ksw = pltpu.roll(kr, 64, 1)
        kd_ref[0] = jnp.where(lo, kr, ksw).astype(kd_ref.dtype)
        kd_ref[1] = jnp.where(lo, ksw, kr).astype(kd_ref.dtype)
        v = p_ref[:, VA:VA + 128]
        vsw = pltpu.roll(v, 64, 1)
        vd_ref[0] = jnp.where(lo, v, vsw).astype(vd_ref.dtype)
        vd_ref[1] = jnp.where(lo, vsw, v).astype(vd_ref.dtype)

    tab = pl.BlockSpec((bs, 128), lambda i: (i, 0))
    gsp = pl.BlockSpec((1, 128), lambda i: (0, 0))
    dup = pl.BlockSpec((2, bs, 128), lambda i: (0, i, 0))
    return pl.pallas_call(
        body, grid=(seq // bs,), name="attn_prep",
        in_specs=[pl.BlockSpec((bs, ATTN_COLS), lambda i: (i, 0)), tab, tab, gsp, gsp],
        out_specs=[pl.BlockSpec((bs, 512), lambda i: (i, 0)), dup, dup],
        out_shape=[jax.ShapeDtypeStruct((seq, 512), MXU_DTYPE), jax.ShapeDtypeStruct((2, seq, 128), MXU_DTYPE),
                   jax.ShapeDtypeStruct((2, seq, 128), MXU_DTYPE)],
        compiler_params=_cp("parallel"))(p_a, cos, sin, qg2, kg2)


def _prep_ret(p_b, cos, sin):
    seq = p_b.shape[0]
    bs = 512

    def body(p_ref, cos_ref, sin_ref, rq_ref, rk_ref, rv_ref):
        cs, sn = cos_ref[...], sin_ref[...]
        for pr in range(2):
            sl = slice(128 * pr, 128 * (pr + 1))
            rq_ref[:, sl] = _rope(p_ref[:, QR_B + 128 * pr:QR_B + 128 * (pr + 1)], cs, sn).astype(rq_ref.dtype)
            rk_ref[:, sl] = (_rope(p_ref[:, KR_B + 128 * pr:KR_B + 128 * (pr + 1)], cs, sn) * 0.125).astype(rk_ref.dtype)
        rv_ref[...] = p_ref[:, VR_B:VR_B + 512].astype(rv_ref.dtype)

    tab = pl.BlockSpec((bs, 128), lambda i: (i, 0))
    return pl.pallas_call(
        body, grid=(seq // bs,), name="ret_prep",
        in_specs=[pl.BlockSpec((bs, ZR_B), lambda i: (i, 0)), tab, tab],
        out_specs=[pl.BlockSpec((bs, 256), lambda i: (i, 0)), pl.BlockSpec((bs, 256), lambda i: (i, 0)),
                   pl.BlockSpec((bs, 512), lambda i: (i, 0))],
        out_shape=[jax.ShapeDtypeStruct((seq, 256), MXU_DTYPE), jax.ShapeDtypeStruct((seq, 256), MXU_DTYPE),
                   jax.ShapeDtypeStruct((seq, 512), MXU_DTYPE)],
        compiler_params=_cp("parallel"))(p_b, cos, sin)


def _halves(kd):
    lo = _lane(kd.shape) < 64
    z = jnp.zeros_like(kd)
    return jnp.concatenate([jnp.where(lo, kd, z), jnp.where(lo, z, kd)], axis=0)


def _attn_fwd(qt, kd, vd, dep=None):
    seq = qt.shape[0]
    bq = bk = 512
    nk = seq // bk
    deps, dep_specs = _dep_args(dep, 2)

    def body(q_ref, k_ref, v_ref, *rest):
        o_ref, lse_ref, m_scr, l_scr, acc = rest[-5:]
        j = pl.program_id(1)
        m_scr[...] = jnp.full_like(m_scr, -jnp.inf)
        l_scr[...] = jnp.zeros_like(l_scr)
        acc[...] = jnp.zeros_like(acc)
        lo = _lane((bq, 128)) < 64

        def kv_block(n, carry):
            rows = pl.ds(pl.multiple_of(n * bk, bk), bk)
            k2, v2 = _halves(k_ref[rows, :]), _halves(v_ref[rows, :])
            for pr in range(2):
                s2 = _nt(q_ref[:, 128 * pr:128 * (pr + 1)], k2)
                ps, alphas = [], []
                for e in range(2):
                    g = 2 * pr + e
                    s = s2[:, e * bk:(e + 1) * bk]
                    m_prev = m_scr[g]
                    m_next = jnp.maximum(m_prev, jnp.max(s, axis=1, keepdims=True))
                    alpha = jnp.exp2(m_prev - m_next)
                    pe = jnp.exp2(s - jnp.tile(m_next, (1, bk // 128)))
                    l_scr[g] = alpha * l_scr[g] + jnp.sum(pe, axis=1, keepdims=True)
                    m_scr[g] = m_next
                    ps.append(_mx(pe))
                    alphas.append(alpha)
                o2 = _nn(jnp.concatenate(ps, axis=1), v2)
                sl = slice(128 * pr, 128 * (pr + 1))
                acc[:, sl] = acc[:, sl] * jnp.where(lo, alphas[0], alphas[1]) + o2
            return carry

        lax.fori_loop(0, nk, kv_block, 0)
        for pr in range(2):
            sl = slice(128 * pr, 128 * (pr + 1))
            o_ref[:, sl] = acc[:, sl] / jnp.where(lo, l_scr[2 * pr], l_scr[2 * pr + 1])
        for g in range(4):
            lse = jnp.transpose(m_scr[g] + jnp.log2(l_scr[g]))
            lse_ref[pl.ds(4 * j + g, 1), :] = lse[0:1, :]

    return pl.pallas_call(
        body, grid=(seq // bq, 2), name="attn_fwd",
        in_specs=[pl.BlockSpec((bq, 256), lambda i, j: (i, j)), pl.BlockSpec((None, seq, 128), lambda i, j: (j, 0, 0)),
                  pl.BlockSpec((None, seq, 128), lambda i, j: (j, 0, 0))] + dep_specs,
        out_specs=[pl.BlockSpec((bq, 256), lambda i, j: (i, j)), pl.BlockSpec((8, bq), lambda i, j: (0, i))],
        out_shape=[jax.ShapeDtypeStruct((seq, 512), F32), jax.ShapeDtypeStruct((8, seq), F32)],
        scratch_shapes=[pltpu.VMEM((4, bq, 128), F32), pltpu.VMEM((4, bq, 128), F32), pltpu.VMEM((bq, 256), F32)],
        compiler_params=_cp("parallel", "arbitrary"))(qt, kd, vd, *deps)


def _attn_bwd(qt, kd, vd, do, lse, delta, dep=None):
    seq = qt.shape[0]
    bq = bk = 512
    nq, nk = seq // bq, seq // bk
    deps, dep_specs = _dep_args(dep, 3)

    def body(q_ref, do_ref, k_ref, v_ref, lse_ref, del_ref, *rest):
        dq_ref, dk_ref, dv_ref = rest[-3:]
        j, i = pl.program_id(0), pl.program_id(1)
        dq_ref[...] = jnp.zeros_like(dq_ref)

        @pl.when(i == 0)
        def _():
            dk_ref[...] = jnp.zeros_like(dk_ref)
            dv_ref[...] = jnp.zeros_like(dv_ref)

        lo = _lane((bk, 128)) < 64
        big = lambda r: jnp.concatenate([jnp.broadcast_to(r[0], (bk, bq)), jnp.broadcast_to(r[1], (bk, bq))], axis=0)

        def kv_block(n, carry):
            rows = pl.ds(pl.multiple_of(n * bk, bk), bk)
            k2, v2 = _halves(k_ref[rows, :]), _halves(v_ref[rows, :])
            for pr in range(2):
                sl = slice(128 * pr, 128 * (pr + 1))
                qp, dop = q_ref[:, sl], do_ref[:, sl]
                s_t = _nt(k2, qp)
                dp_t = _nt(v2, dop)
                ls = [lse_ref[pl.ds(4 * j + 2 * pr + e, 1), :] for e in range(2)]
                dl = [del_ref[pl.ds(4 * j + 2 * pr + e, 1), :] for e in range(2)]
                p_t = jnp.exp2(s_t - big(ls))
                ds_t = _mx(p_t * (dp_t - big(dl)))
                rv = _nn(_mx(p_t), dop)
                rk = _nn(ds_t, qp) * LN2
                dv_ref[rows, :] += jnp.where(lo, rv[:bk], rv[bk:])
                dk_ref[rows, :] += jnp.where(lo, rk[:bk], rk[bk:])
                dq_ref[:, sl] += _tn(ds_t, k2)
            return carry

        lax.fori_loop(0, nk, kv_block, 0)

    return pl.pallas_call(
        body, grid=(2, nq), name="attn_bwd",
        in_specs=[pl.BlockSpec((bq, 256), lambda j, i: (i, j)), pl.BlockSpec((bq, 256), lambda j, i: (i, j)),
                  pl.BlockSpec((None, seq, 128), lambda j, i: (j, 0, 0)), pl.BlockSpec((None, seq, 128), lambda j, i: (j, 0, 0)),
                  pl.BlockSpec((8, bq), lambda j, i: (0, i)), pl.BlockSpec((8, bq), lambda j, i: (0, i))] + dep_specs,
        out_specs=[pl.BlockSpec((bq, 256), lambda j, i: (i, j)), pl.BlockSpec((None, seq, 128), lambda j, i: (j, 0, 0)),
                   pl.BlockSpec((None, seq, 128), lambda j, i: (j, 0, 0))],
        out_shape=[jax.ShapeDtypeStruct((seq, 512), F32), jax.ShapeDtypeStruct((2, seq, 128), F32),
                   jax.ShapeDtypeStruct((2, seq, 128), F32)],
        compiler_params=_cp("arbitrary", "arbitrary"))(qt, do, kd, vd, lse, delta, *deps)


def _ret_tables(lg_f, lg_b, c):
    idx = jnp.arange(c, dtype=F32)
    diff = idx[:, None] - idx[None, :]
    a, b = lg_f[:, None, None], lg_b[:, None, None]
    low, up = (diff >= 0)[None], (diff < 0)[None]
    dmat = jnp.where(low, jnp.exp(a * jnp.maximum(diff, 0.0)[None]), jnp.exp(b * jnp.maximum(-diff, 0.0)[None]))
    dda = jnp.where(low, diff[None] * jnp.exp(a * jnp.maximum(diff, 0.0)[None]), 0.0)
    ddb = jnp.where(up, -diff[None] * jnp.exp(b * jnp.maximum(-diff, 0.0)[None]), 0.0)
    rep = lambda w: jnp.broadcast_to(w[:, :, None], (RET_HEADS, c, 128))
    wqf = rep(jnp.exp(lg_f[:, None] * (idx + 1.0)[None]))
    wqb = rep(jnp.exp(lg_b[:, None] * (c - idx)[None]))
    wkf = rep(jnp.exp(lg_f[:, None] * (c - 1.0 - idx)[None]))
    wkb = rep(jnp.exp(lg_b[:, None] * idx[None]))
    cdf, cdb = jnp.exp(lg_f * c), jnp.exp(lg_b * c)
    dec_fb = jnp.broadcast_to(jnp.concatenate([cdf, cdb])[:, None], (8, 128))
    dec_bf = jnp.broadcast_to(jnp.concatenate([cdb, cdf])[:, None], (8, 128))
    return dict(dmat=dmat, dda=dda, ddb=ddb, wqf=wqf, wqb=wqb, wkf=wkf, wkb=wkb, dec_fb=dec_fb, dec_bf=dec_bf)


def _ret_states(xk, yv, w_fwd, w_rev, dec, name):
    seq = xk.shape[0]
    c = RET_CHUNK
    nc = seq // c

    def body(xf_ref, yf_ref, xr_ref, yr_ref, wf_ref, wr_ref, dec_ref, sf_ref, sr_ref, st_f, st_r):
        @pl.when(pl.program_id(0) == 0)
        def _():
            st_f[...] = jnp.zeros_like(st_f)
            st_r[...] = jnp.zeros_like(st_r)

        lane = _lane((c, 128))
        for h in range(RET_HEADS):
            pr, e = h // 2, h % 2
            half = (lane < 64) if e == 0 else (lane >= 64)
            for x_ref, y_ref, w_ref, s_ref, st, drow in ((xf_ref, yf_ref, wf_ref, sf_ref, st_f, h),
                                                        (xr_ref, yr_ref, wr_ref, sr_ref, st_r, 4 + h)):
                s_ref[h] = st[h].astype(s_ref.dtype)
                xm = jnp.where(half, x_ref[:, 128 * pr:128 * (pr + 1)].astype(F32) * w_ref[h], 0.0)
                st[h] = st[h] * dec_ref[drow:drow + 1, :] + _tn(_mx(xm), _mx(y_ref[:, 128 * h:128 * (h + 1)]))

    xs = lambda f: pl.BlockSpec((c, 256), f)
    ys = lambda f: pl.BlockSpec((c, 512), f)
    fwd, rev = (lambda n: (n, 0)), (lambda n: (nc - 1 - n, 0))
    wsp = pl.BlockSpec((RET_HEADS, c, 128), lambda n: (0, 0, 0))
    return pl.pallas_call(
        body, grid=(nc,), name=name,
        in_specs=[xs(fwd), ys(fwd), xs(rev), ys(rev), wsp, wsp, pl.BlockSpec((8, 128), lambda n: (0, 0))],
        out_specs=[pl.BlockSpec((None, RET_HEADS, 128, 128), lambda n: (n, 0, 0, 0)),
                   pl.BlockSpec((None, RET_HEADS, 128, 128), lambda n: (nc - 1 - n, 0, 0, 0))],
        out_shape=[jax.ShapeDtypeStruct((nc, RET_HEADS, 128, 128), MXU_DTYPE)] * 2,
        scratch_shapes=[pltpu.VMEM((RET_HEADS, 128, 128), F32), pltpu.VMEM((RET_HEADS, 128, 128), F32)],
        compiler_params=_cp("arbitrary"))(xk, yv, xk, yv, w_fwd, w_rev, dec)


def _ret_fwd(rq, rk, rv, rf, rb, tb):
    seq = rq.shape[0]
    c = RET_CHUNK

    def body(q_ref, k_ref, v_ref, rf_ref, rb_ref, d_ref, wqf_ref, wqb_ref, o_ref):
        lane = _lane((c, 128))
        for h in range(RET_HEADS):
            pr, e = h // 2, h % 2
            half = (lane < 64) if e == 0 else (lane >= 64)
            sl = slice(128 * pr, 128 * (pr + 1))
            qp = q_ref[:, sl]
            km = jnp.where(half, k_ref[:, sl], jnp.zeros_like(k_ref[:, sl]))
            sd = _mx(_nt(qp, km) * d_ref[h])
            qf = qp.astype(F32)
            o = _nn(sd, v_ref[:, 128 * h:128 * (h + 1)])
            o = o + _nn(_mx(qf * wqf_ref[h]), rf_ref[h]) + _nn(_mx(qf * wqb_ref[h]), rb_ref[h])
            o_ref[:, 128 * h:128 * (h + 1)] = o

    st = pl.BlockSpec((None, RET_HEADS, 128, 128), lambda n: (n, 0, 0, 0))
    wsp = pl.BlockSpec((RET_HEADS, c, 128), lambda n: (0, 0, 0))
    return pl.pallas_call(
        body, grid=(seq // c,), name="ret_fwd",
        in_specs=[pl.BlockSpec((c, 256), lambda n: (n, 0)), pl.BlockSpec((c, 256), lambda n: (n, 0)),
                  pl.BlockSpec((c, 512), lambda n: (n, 0)), st, st, pl.BlockSpec((RET_HEADS, c, c), lambda n: (0, 0, 0)), wsp, wsp],
        out_specs=pl.BlockSpec((c, 512), lambda n: (n, 0)),
        out_shape=jax.ShapeDtypeStruct((seq, 512), F32), compiler_params=_cp("parallel"))(
            rq, rk, rv, rf, rb, tb["dmat"], tb["wqf"], tb["wqb"])


def _ret_bwd(rq, rk, rv, do, rf, rb, hf, hb, tb):
    seq = rq.shape[0]
    c = RET_CHUNK

    def body(q_ref, k_ref, v_ref, do_ref, rf_ref, rb_ref, hf_ref, hb_ref, d_ref, dda_ref, ddb_ref,
             wqf_ref, wqb_ref, wkf_ref, wkb_ref, cdec_ref, dq_ref, dk_ref, dv_ref, dlg_ref):
        @pl.when(pl.program_id(0) == 0)
        def _():
            dlg_ref[...] = jnp.zeros_like(dlg_ref)

        lane = _lane((c, 128))
        pos = lax.broadcasted_iota(jnp.int32, (c, 128), 0).astype(F32)
        for pr in range(2):
            sl = slice(128 * pr, 128 * (pr + 1))
            qp, kp = q_ref[:, sl], k_ref[:, sl]
            qf, kf = qp.astype(F32), kp.astype(F32)
            dq_acc = jnp.zeros((c, 128), F32)
            dk_acc = jnp.zeros((c, 128), F32)
            for e in range(2):
                h = 2 * pr + e
                half = (lane < 64) if e == 0 else (lane >= 64)
                hs = slice(128 * h, 128 * (h + 1))
                km = jnp.where(half, kp, jnp.zeros_like(kp))
                kmf = km.astype(F32)
                vh, doh = v_ref[:, hs], do_ref[:, hs]
                rfh, rbh, hfh, hbh = rf_ref[h], rb_ref[h], hf_ref[h], hb_ref[h]
                s = _nt(qp, km)
                dpm = _nt(doh, vh)
                ds_b = _mx(dpm * d_ref[h])
                sd_b = _mx(s * d_ref[h])
                dq_if = wqf_ref[h] * _nt(doh, rfh)
                dq_ib = wqb_ref[h] * _nt(doh, rbh)
                dq_acc = dq_acc + _nn(ds_b, km) + dq_if + dq_ib
                dk_sf = wkf_ref[h] * _nt(vh, hfh)
                dk_sb = wkb_ref[h] * _nt(vh, hbh)
                dk_acc = dk_acc + jnp.where(half, _tn(ds_b, qp), 0.0) + dk_sf + dk_sb
                dv = _tn(sd_b, doh) + _nn(_mx(kmf * wkf_ref[h]), hfh) + _nn(_mx(kmf * wkb_ref[h]), hbh)
                dv_ref[:, hs] = dv.astype(dv_ref.dtype)
                sdp = s * dpm
                hr_f = hfh.astype(F32) * rfh.astype(F32)
                hr_b = hbh.astype(F32) * rbh.astype(F32)
                da = (_rowsum128(sdp * dda_ref[h]) + _rowsum128((pos + 1.0) * qf * dq_if)
                      + _rowsum128((c - 1.0 - pos) * kf * dk_sf) + cdec_ref[h:h + 1, :] * _rowsum128(hr_f))
                db = (_rowsum128(sdp * ddb_ref[h]) + _rowsum128((c - pos) * qf * dq_ib)
                      + _rowsum128(pos * kf * dk_sb) + cdec_ref[4 + h:5 + h, :] * _rowsum128(hr_b))
                dlg_ref[h:h + 1, :] += da
                dlg_ref[4 + h:5 + h, :] += db
            dq_ref[:, sl] = dq_acc
            dk_ref[:, sl] = dk_acc

    st = pl.BlockSpec((None, RET_HEADS, 128, 128), lambda n: (n, 0, 0, 0))
    wsp = pl.BlockSpec((RET_HEADS, c, 128), lambda n: (0, 0, 0))
    dsp = pl.BlockSpec((RET_HEADS, c, c), lambda n: (0, 0, 0))
    x256 = pl.BlockSpec((c, 256), lambda n: (n, 0))
    x512 = pl.BlockSpec((c, 512), lambda n: (n, 0))
    cdec = tb["dec_fb"] * float(c)
    return pl.pallas_call(
        body, grid=(seq // c,), name="ret_bwd",
        in_specs=[x256, x256, x512, x512, st, st, st, st, dsp, dsp, dsp, wsp, wsp, wsp, wsp,
                  pl.BlockSpec((8, 128), lambda n: (0, 0))],
        out_specs=[x256, x256, x512, pl.BlockSpec((8, 128), lambda n: (0, 0))],
        out_shape=[jax.ShapeDtypeStruct((seq, 256), F32), jax.ShapeDtypeStruct((seq, 256), F32),
                   jax.ShapeDtypeStruct((seq, 512), MXU_DTYPE), jax.ShapeDtypeStruct((8, 128), F32)],
        compiler_params=_cp("arbitrary"))(
            rq, rk, rv, do, rf, rb, hf, hb, tb["dmat"], tb["dda"], tb["ddb"], tb["wqf"], tb["wqb"], tb["wkf"], tb["wkb"], cdec)


def _tail(x, tgt, o_att, o_ret, p, mod, g_post, gn_g, wpt, wout):
    seq = x.shape[0]
    bs = 256
    nb = seq // bs

    def body(x_ref, t_ref, oa_ref, or_ref, za_ref, zr_ref, gl_ref, mod_ref, gp_ref, gn_ref, wpt_ref, wout_ref,
             dout_ref, dza_ref, dzr_ref, dgl_ref, doa_ref, dor_ref, del_ref, gout_ref, gpt_ref,
             dgate_ref, dgpost_ref, dgn_ref, loss_ref, gout_acc, gpt_acc):
        i = pl.program_id(0)

        @pl.when(i == 0)
        def _():
            gout_acc[...] = jnp.zeros_like(gout_acc)
            gpt_acc[...] = jnp.zeros_like(gpt_acc)
            dgate_ref[...] = jnp.zeros_like(dgate_ref)
            dgpost_ref[...] = jnp.zeros_like(dgpost_ref)
            dgn_ref[...] = jnp.zeros_like(dgn_ref)
            loss_ref[...] = jnp.zeros_like(loss_ref)

        oa, za, zr = oa_ref[...], za_ref[...], zr_ref[...]
        sga, sgr = _sigmoid(za), _sigmoid(zr)
        sza, szr = za * sga, zr * sgr
        ya_b = _mx(oa * sza)
        ons, rstds = [], []
        for h in range(RET_HEADS):
            oh = or_ref[:, 128 * h:128 * (h + 1)]
            xc = oh - jnp.mean(oh, axis=-1, keepdims=True)
            rstd = lax.rsqrt(jnp.mean(xc * xc, axis=-1, keepdims=True) + EPS)
            ons.append(xc * rstd)
            rstds.append(rstd)
        on = jnp.concatenate(ons, axis=1)
        yn = on * gn_ref[...]
        yr_b = _mx(yn * szr)
        wpa_t, wpr_t = wpt_ref[:, 0:512], wpt_ref[:, 512:1024]
        a_att = _nt(ya_b, wpa_t)
        a_ret = _nt(yr_b, wpr_t)
        gts = _sigmoid(gl_ref[...])
        g_att, g_ret = gts[:, 0:D_MODEL], gts[:, D_MODEL:2 * D_MODEL]
        merged_b = _mx(g_att * a_att + g_ret * a_ret)
        u = _nn(merged_b, wout_ref[...])
        r = lax.rsqrt(jnp.mean(u * u, axis=-1, keepdims=True) + EPS)
        un = u * r
        gpost = gp_ref[...]
        y = un * gpost
        gate = mod_ref[:, 2 * D_MODEL:3 * D_MODEL]
        err = x_ref[...] + gate * y - t_ref[...]
        loss_ref[...] += (0.5 / D_MODEL) * _rowsum128(err * err)
        dout = err * (1.0 / D_MODEL)
        dout_ref[...] = dout
        dgate_ref[...] += jnp.sum(dout * y, axis=0, keepdims=True)
        dy = dout * gate
        dgpost_ref[...] += jnp.sum(dy * un, axis=0, keepdims=True)
        dun = dy * gpost
        du_b = _mx(r * (dun - un * jnp.mean(dun * un, axis=-1, keepdims=True)))
        dmerged = _nt(du_b, wout_ref[...])
        gout_acc[...] += _tn(merged_b, du_b)
        d_att, d_ret = dmerged * g_att, dmerged * g_ret
        dgl_ref[:, 0:D_MODEL] = (d_att * a_att * (1.0 - g_att)).astype(dgl_ref.dtype)
        dgl_ref[:, D_MODEL:2 * D_MODEL] = (d_ret * a_ret * (1.0 - g_ret)).astype(dgl_ref.dtype)
        d_att_b, d_ret_b = _mx(d_att), _mx(d_ret)
        dya = _nn(d_att_b, wpa_t)
        dyr = _nn(d_ret_b, wpr_t)
        gpt_acc[:, 0:512] += _tn(d_att_b, ya_b)
        gpt_acc[:, 512:1024] += _tn(d_ret_b, yr_b)
        dza_ref[...] = (dya * oa * (sga * (1.0 + za * (1.0 - sga)))).astype(dza_ref.dtype)
        doa = dya * sza
        doa_ref[...] = doa.astype(doa_ref.dtype)
        dt = jnp.transpose(doa * oa)
        del_ref[...] = jnp.sum(dt.reshape(8, HEAD_DIM, bs), axis=1)
        dzr_ref[...] = (dyr * yn * (sgr * (1.0 + zr * (1.0 - sgr)))).astype(dzr_ref.dtype)
        dyn = dyr * szr
        dgn_ref[...] += jnp.sum(dyn * on, axis=0, keepdims=True)
        don = dyn * gn_ref[...]
        for h in range(RET_HEADS):
            hs = slice(128 * h, 128 * (h + 1))
            dh, oh = don[:, hs], ons[h]
            doh = rstds[h] * (dh - jnp.mean(dh, axis=-1, keepdims=True) - oh * jnp.mean(dh * oh, axis=-1, keepdims=True))
            dor_ref[:, hs] = doh.astype(dor_ref.dtype)

        @pl.when(i == nb - 1)
        def _():
            gout_ref[...] = gout_acc[...].astype(gout_ref.dtype)
            gpt_ref[...] = gpt_acc[...].astype(gpt_ref.dtype)

    row = lambda w: pl.BlockSpec((bs, w), lambda i: (i, 0))
    el = lambda w, off: pl.BlockSpec((pl.Element(bs), pl.Element(w)), lambda i: (i * bs, off))
    vec = lambda w: pl.BlockSpec((1, w), lambda i: (0, 0))
    full = pl.BlockSpec((D_MODEL, D_MODEL), lambda i: (0, 0))
    sds = jax.ShapeDtypeStruct
    return pl.pallas_call(
        body, grid=(nb,), name="tail",
        in_specs=[row(D_MODEL), row(D_MODEL), row(512), row(512), el(512, ZA_B), el(512, ZR_B), el(2 * D_MODEL, GL_B),
                  vec(3 * D_MODEL), vec(D_MODEL), vec(512), full, full],
        out_specs=[row(D_MODEL), row(512), row(512), row(2 * D_MODEL), row(512), row(512),
                   pl.BlockSpec((8, bs), lambda i: (0, i)), full, full, vec(D_MODEL), vec(D_MODEL), vec(512), vec(128)],
        out_shape=[sds((seq, D_MODEL), F32), sds((seq, 512), MXU_DTYPE), sds((seq, 512), MXU_DTYPE),
                   sds((seq, 2 * D_MODEL), MXU_DTYPE), sds((seq, 512), MXU_DTYPE), sds((seq, 512), MXU_DTYPE),
                   sds((8, seq), F32), sds((D_MODEL, D_MODEL), MXU_DTYPE), sds((D_MODEL, D_MODEL), MXU_DTYPE),
                   sds((1, D_MODEL), F32), sds((1, D_MODEL), F32), sds((1, 512), F32), sds((1, 128), F32)],
        scratch_shapes=[pltpu.VMEM((D_MODEL, D_MODEL), F32), pltpu.VMEM((D_MODEL, D_MODEL), F32)],
        compiler_params=_cp("arbitrary"))(x, tgt, o_att, o_ret, p, p, p, mod, g_post, gn_g, wpt, wout)


def _assemble(p, cos, sin, qg2, kg2, dqt, dkp, dvp, dza, drq, drk, drv, dzr, dgl):
    seq = p.shape[0]
    bs = 512

    def body(p_ref, cos_ref, sin_ref, qg_ref, kg_ref, dqt_ref, dkp_ref, dvp_ref, dza_ref, drq_ref, drk_ref, drv_ref,
             dzr_ref, dgl_ref, dp_ref, dqg_ref, dkg_ref):
        @pl.when(pl.program_id(0) == 0)
        def _():
            dqg_ref[...] = jnp.zeros_like(dqg_ref)
            dkg_ref[...] = jnp.zeros_like(dkg_ref)

        m = _blockdiag64()
        cs, sn = cos_ref[...], sin_ref[...]
        lo = _lane((bs, 128)) < 64

        def norm_bwd(raw, dn, g, dg_ref):
            r = lax.rsqrt(_seg64(raw * raw, m) * (1.0 / 64) + EPS)
            xn = raw * r
            dg_ref[...] += jnp.sum(dn * xn, axis=0, keepdims=True)
            dxn = dn * g
            return r * (dxn - xn * (_seg64(dxn * xn, m) * (1.0 / 64)))

        for pr in range(4):
            sl = slice(128 * pr, 128 * (pr + 1))
            dn = _rope_t(dqt_ref[:, sl] * 0.125, cs, sn)
            dp_ref[:, QA + 128 * pr:QA + 128 * (pr + 1)] = norm_bwd(p_ref[:, sl], dn, qg_ref[...], dqg_ref).astype(dp_ref.dtype)
        fold = lambda a: a + pltpu.roll(a, 64, 1)
        dk = jnp.where(lo, fold(dkp_ref[0]), fold(dkp_ref[1]))
        dp_ref[:, KA:KA + 128] = norm_bwd(p_ref[:, KA:KA + 128], _rope_t(dk, cs, sn), kg_ref[...], dkg_ref).astype(dp_ref.dtype)
        dp_ref[:, VA:VA + 128] = jnp.where(lo, fold(dvp_ref[0]), fold(dvp_ref[1])).astype(dp_ref.dtype)
        dp_ref[:, ZA:ZA + 512] = dza_ref[...]
        for pr in range(2):
            sl = slice(128 * pr, 128 * (pr + 1))
            dp_ref[:, QR + 128 * pr:QR + 128 * (pr + 1)] = _rope_t(drq_ref[:, sl], cs, sn).astype(dp_ref.dtype)
            dp_ref[:, KR + 128 * pr:KR + 128 * (pr + 1)] = _rope_t(drk_ref[:, sl] * 0.125, cs, sn).astype(dp_ref.dtype)
        dp_ref[:, VR:VR + 512] = drv_ref[...]
        dp_ref[:, ZR:ZR + 512] = dzr_ref[...]
        dp_ref[:, GL:GL + 2 * D_MODEL] = dgl_ref[...]

    row = lambda w: pl.BlockSpec((bs, w), lambda i: (i, 0))
    gsp = pl.BlockSpec((1, 128), lambda i: (0, 0))
    dup = pl.BlockSpec((2, bs, 128), lambda i: (0, i, 0))
    return pl.pallas_call(
        body, grid=(seq // bs,), name="assemble_dp",
        in_specs=[row(768), row(128), row(128), gsp, gsp, row(512), dup, dup, row(512), row(256), row(256), row(512),
                  row(512), row(2 * D_MODEL)],
        out_specs=[row(IN_WIDTH), gsp, gsp],
        out_shape=[jax.ShapeDtypeStruct((seq, IN_WIDTH), MXU_DTYPE), jax.ShapeDtypeStruct((1, 128), F32),
                   jax.ShapeDtypeStruct((1, 128), F32)],
        compiler_params=_cp("arbitrary"))(p, cos, sin, qg2, kg2, dqt, dkp, dvp, dza, drq, drk, drv, dzr, dgl)


def _local_step(x, tgt, mod, g_pre, qn_g, kn_g, w_dec_f, w_dec_b, gn_g, g_post, w_attn, rest_start, rest_wait, send=None):
    send = send or (lambda name, arrays: None)
    seq = x.shape[0]
    cos, sin = _rope_tables(seq)
    qg2, kg2 = jnp.tile(qn_g, (1, 2)), jnp.tile(kn_g, (1, 2))
    lg_f = jax.nn.log_sigmoid(w_dec_f[0])
    lg_b = jax.nn.log_sigmoid(w_dec_b[0])
    tb = _ret_tables(lg_f, lg_b, RET_CHUNK)

    h = _prenorm(x, mod, g_pre)
    p_a = _matmul(h, w_attn, ta=False, tb=True, tm=512, tn=ATTN_COLS, out_dtype=F32, name="in_proj_attn")
    qt, kd, vd = _prep_attn(p_a, cos, sin, qg2, kg2)
    o_att, lse = _attn_fwd(qt, kd, vd, dep=rest_start())
    win_t, wpt, wout = rest_wait(o_att)
    p_b = _matmul(h, win_t, ta=False, tb=True, tm=512, tn=(IN_WIDTH - ATTN_COLS) // 2, out_dtype=F32, name="in_proj_rest",
                  b_rows=(ATTN_COLS, IN_WIDTH - ATTN_COLS))
    rq, rk, rv = _prep_ret(p_b, cos, sin)
    rf, rb = _ret_states(rk, rv, tb["wkf"], tb["wkb"], tb["dec_fb"], "ret_states_fwd")
    o_ret = _ret_fwd(rq, rk, rv, rf, rb, tb)
    (dout, dza, dzr, dgl, do_att, do_ret, delta, g_out, g_pt, dgate, dgpost, dgn, loss_l) = _tail(
        x, tgt, o_att, o_ret, p_b, mod, g_post, gn_g, wpt, wout)
    dep = send("early", (g_pt, g_out))
    dqt, dkp, dvp = _attn_bwd(qt, kd, vd, do_att, lse, delta, dep=dep)
    hb, hf = _ret_states(rq, do_ret, tb["wqb"], tb["wqf"], tb["dec_bf"], "ret_states_bwd")
    drq, drk, drv, dlg = _ret_bwd(rq, rk, rv, do_ret, rf, rb, hf, hb, tb)
    dp, dqg, dkg = _assemble(p_a, cos, sin, qg2, kg2, dqt, dkp, dvp, dza, drq, drk, drv, dzr, dgl)
    g_in_t = _matmul(dp, h, ta=True, tb=False, tm=256, tn=D_MODEL, out_dtype=MXU_DTYPE, name="in_proj_dw")
    dep = send("late", (g_in_t,))
    dh = _matmul(dp, win_t, ta=False, tb=False, tm=512, tn=D_MODEL, out_dtype=F32, name="in_proj_dx", dep=dep)
    grad_x, dshift, dscale, dgpre = _prenorm_bwd(x, dh, dout, mod, g_pre)

    dlg = jnp.sum(dlg, axis=1)
    small = dict(
        dmod=jnp.concatenate([dshift, dscale, dgate], axis=1),
        g_pre=dgpre, g_post=dgpost, gn_g=dgn,
        qn_g=dqg[:, :64] + dqg[:, 64:], kn_g=dkg[:, :64] + dkg[:, 64:],
        w_dec_f=(dlg[0:4] * jax.nn.sigmoid(-w_dec_f[0]))[None], w_dec_b=(dlg[4:8] * jax.nn.sigmoid(-w_dec_b[0]))[None],
        loss=jnp.sum(loss_l, axis=1, keepdims=True))
    return grad_x, g_in_t, g_pt, g_out, small


N_DEV = 8
N_CHIP = 4
HBM_SPEC = pl.BlockSpec(memory_space=pl.ANY)
VMEM_SPEC = pl.BlockSpec(memory_space=pltpu.VMEM)
SHARD_ROWS = (IN_WIDTH // N_CHIP, D_MODEL // N_CHIP, D_MODEL // N_CHIP)


def _coords():
    return lax.axis_index("x"), lax.axis_index("y"), lax.axis_index("c")


def _peer(k):
    x, y, c = _coords()
    return (1 - x if k & 4 else x, 1 - y if k & 2 else y, 1 - c if k & 1 else c)


def _dev_index(p):
    return 4 * p[0] + 2 * p[1] + p[2]


def _rows(ref, start, size):
    return ref.at[pl.ds(pl.multiple_of(start, 16), size), :]


def _remote(src, dst, ssem, rsem, dev):
    return pltpu.make_async_remote_copy(src_ref=src, dst_ref=dst, send_sem=ssem, recv_sem=rsem, device_id=dev,
                                        device_id_type=MESH)


def _mod_exchange(c, w_ada_s, b4):
    ncol = w_ada_s.shape[1]

    def body(c_ref, w_ref, b_ref, cs_ref, mod_ref, modsh, modall, ssem, rsem):
        me = _dev_index(_coords())
        chip = me // 2
        cs_ref[me] = c_ref[...]
        sends = []
        for k in range(1, N_DEV):
            cp = _remote(c_ref, cs_ref.at[me], ssem.at[k - 1], rsem.at[k - 1], _peer(k))
            cp.start()
            sends.append(cp)
        for k in range(1, N_DEV):
            slot = cs_ref.at[_dev_index(_peer(k))]
            _remote(slot, slot, ssem.at[k - 1], rsem.at[k - 1], _peer(k)).wait_recv()
        cs = jnp.concatenate([cs_ref[d] for d in range(N_DEV)], axis=0)
        ms = _nn(cs * _sigmoid(cs), w_ref[...], precision=HIGHEST) + b_ref[pl.ds(chip, 1), :]
        modsh[...] = ms
        modall[chip] = ms
        for k2 in range(1, N_CHIP):
            cp = _remote(modsh, modall.at[chip], ssem.at[6 + k2], rsem.at[6 + k2], _peer(2 * k2))
            cp.start()
            sends.append(cp)
        for k2 in range(1, N_CHIP):
            slot = modall.at[_dev_index(_peer(2 * k2)) // 2]
            _remote(slot, slot, ssem.at[6 + k2], rsem.at[6 + k2], _peer(2 * k2)).wait_recv()
        for jj in range(N_CHIP):
            mod_ref[:, ncol * jj:ncol * (jj + 1)] = modall[jj, pl.ds(me, 1), :]
        for cp in sends:
            cp.wait_send()

    return pl.pallas_call(
        body, name="mod_exchange", in_specs=[VMEM_SPEC] * 3, out_specs=[VMEM_SPEC] * 2,
        out_shape=[jax.ShapeDtypeStruct((N_DEV, 1, D_MODEL), F32), jax.ShapeDtypeStruct((1, N_CHIP * ncol), F32)],
        scratch_shapes=[pltpu.VMEM((N_DEV, ncol), F32), pltpu.VMEM((N_CHIP, N_DEV, ncol), F32),
                        pltpu.SemaphoreType.DMA((10,)), pltpu.SemaphoreType.DMA((10,))],
        compiler_params=_cp())(c, w_ada_s, b4)


GATHER_CHUNK = 32


def _chunks(kinds, chunk):
    out = []
    for t, kind in enumerate(kinds):
        half = SHARD_ROWS[kind] // 2
        step = chunk if half % chunk == 0 else half
        out += [(t, off, step) for off in range(0, half, step)]
    return out


ATTN_CHUNK = 96


def _gather_attn(win_s):
    half = ATTN_COLS // 2
    offs = list(range(0, half, ATTN_CHUNK))
    nq = len(offs)
    rows = lambda ref, cc, off: ref.at[pl.ds(pl.multiple_of(cc * half + off, 16), ATTN_CHUNK), :]

    def body(src, out, lsem, ssem, rsem, fsem, gsem):
        x, y, c = _coords()
        chip = 2 * x + y
        sib = (x, y, 1 - c)

        @pl.when(chip == 0)
        def _():
            own = pltpu.make_async_copy(src.at[pl.ds(0, ATTN_COLS), :], out, lsem.at[0])
            own.start()
            sends = []
            for k2 in range(1, N_CHIP):
                for q, off in enumerate(offs):
                    cp = _remote(rows(src, c, off), rows(out, c, off), ssem.at[(k2 - 1) * nq + q], rsem.at[q], (k2 // 2, k2 % 2, c))
                    cp.start()
                    sends.append(cp)
            for cp in sends:
                cp.wait_send()
            own.wait()

        @pl.when(chip != 0)
        def _():
            passed = []
            for q, off in enumerate(offs):
                got = rows(out, c, off)
                _remote(got, got, ssem.at[q], rsem.at[q], (0, 0, c)).wait_recv()
                cp = _remote(got, got, fsem.at[q], gsem.at[q], sib)
                cp.start()
                passed.append(cp)
            for q, off in enumerate(offs):
                got = rows(out, 1 - c, off)
                _remote(got, got, fsem.at[q], gsem.at[q], sib).wait_recv()
            for cp in passed:
                cp.wait_send()

    dma = pltpu.SemaphoreType.DMA
    return pl.pallas_call(
        body, name="gather_attn", in_specs=[VMEM_SPEC], out_specs=HBM_SPEC,
        out_shape=jax.ShapeDtypeStruct((ATTN_COLS, win_s.shape[1]), win_s.dtype),
        scratch_shapes=[dma((1,)), dma((3 * nq,)), dma((nq,)), dma((nq,)), dma((nq,))],
        compiler_params=_cp())(win_s)


SEM_SPEC = pl.BlockSpec(memory_space=pltpu.SEMAPHORE)
HBM_ONLY = pl.BlockSpec(memory_space=pltpu.HBM)
DATAFLOW = pltpu.SideEffectType.DATAFLOW_SIDE_EFFECTING


def _place_own(shards):
    nt = len(shards)

    def body(*refs):
        srcs, outs, lsem = refs[:nt], refs[nt:2 * nt], refs[2 * nt]
        x, y, _ = _coords()
        chip = 2 * x + y
        local = [pltpu.make_async_copy(srcs[t], _rows(outs[t], chip * SHARD_ROWS[t], SHARD_ROWS[t]), lsem.at[t]) for t in range(nt)]
        for cp in local:
            cp.start()
        for cp in local:
            cp.wait()

    return pl.pallas_call(
        body, name="place_own_shard", in_specs=[VMEM_SPEC] * nt, out_specs=[HBM_SPEC] * nt,
        out_shape=[jax.ShapeDtypeStruct((N_CHIP * SHARD_ROWS[t], s.shape[1]), s.dtype) for t, s in enumerate(shards)],
        scratch_shapes=[pltpu.SemaphoreType.DMA((nt,))], compiler_params=_cp())(*shards)


def _gather_rest_start(shards, zones):
    n = len(shards)

    def body(*refs):
        srcs, lands = refs[:n], refs[n:2 * n]
        ssem, rsem = refs[2 * n], refs[2 * n + 1]
        token = refs[-1]
        x, y, _ = _coords()
        chip = 2 * x + y
        for k2 in range(1, N_CHIP):
            for t in range(n):
                q = (k2 - 1) * n + t
                _remote(srcs[t], _rows(lands[t], chip * SHARD_ROWS[t], SHARD_ROWS[t]), ssem.at[q], rsem.at[q], _peer(2 * k2)).start()
        token[...] = jnp.zeros_like(token)

    hbm = lambda a: pltpu.with_memory_space_constraint(a, pltpu.HBM)
    dma = pltpu.SemaphoreType.DMA
    outs = pl.pallas_call(
        body, name="gather_rest", in_specs=[HBM_ONLY] * (2 * n), out_specs=[SEM_SPEC, SEM_SPEC] + [HBM_ONLY] * (2 * n) + [VMEM_SPEC],
        out_shape=[dma((3 * n,)), dma((3 * n,))] + [pltpu.HBM(a.shape, a.dtype) for a in list(shards) + list(zones)]
        + [jax.ShapeDtypeStruct((8, 128), F32)],
        input_output_aliases={i: 2 + i for i in range(2 * n)},
        compiler_params=pltpu.CompilerParams(has_side_effects=DATAFLOW))(*[hbm(a) for a in list(shards) + list(zones)])
    return outs[0], outs[1], outs[2:2 + n], outs[2 + n:2 + 2 * n], outs[-1]


def _gather_rest_wait(started, after):
    ssem, rsem, shards, zones, _ = started
    n = len(shards)

    def body(*refs):
        srcs, lands = refs[:n], refs[n:2 * n]
        ssem_ref, rsem_ref = refs[2 * n], refs[2 * n + 1]
        for k2 in range(1, N_CHIP):
            pchip = _dev_index(_peer(2 * k2)) // 2
            for t in range(n):
                q = (k2 - 1) * n + t
                cp = _remote(srcs[t], _rows(lands[t], pchip * SHARD_ROWS[t], SHARD_ROWS[t]), ssem_ref.at[q], rsem_ref.at[q], _peer(2 * k2))
                cp.wait_send()
                cp.wait_recv()

    outs = pl.pallas_call(
        body, name="gather_rest_wait", in_specs=[HBM_ONLY] * (2 * n) + [SEM_SPEC, SEM_SPEC, HBM_SPEC], out_specs=[HBM_ONLY] * (2 * n),
        out_shape=[pltpu.HBM(a.shape, a.dtype) for a in list(shards) + list(zones)],
        input_output_aliases={i: i for i in range(2 * n)},
        compiler_params=pltpu.CompilerParams(has_side_effects=DATAFLOW))(*shards, *zones, ssem, rsem, after)
    return outs[n:]


def _reduced_by(ref, t, dev):
    return _rows(ref, (dev // 2) * SHARD_ROWS[t] + (dev % 2) * (SHARD_ROWS[t] // 2), SHARD_ROWS[t] // 2)


def _scatter_start(grads, kinds, name):
    n = len(grads)

    def body(*refs):
        srcs, lands = refs[:n], refs[n:2 * n]
        ssem, rsem = refs[2 * n], refs[2 * n + 1]
        token = refs[-1]
        me = _dev_index(_coords())
        for k in range(1, N_DEV):
            for i, t in enumerate(kinds):
                q = (k - 1) * n + i
                _remote(_reduced_by(srcs[i], t, _dev_index(_peer(k))), lands[i].at[me], ssem.at[q], rsem.at[q], _peer(k)).start()
        token[...] = jnp.zeros_like(token)

    zones = [lax.empty((N_DEV, SHARD_ROWS[t] // 2, g.shape[1]), g.dtype) for g, t in zip(grads, kinds)]
    hbm = lambda a: pltpu.with_memory_space_constraint(a, pltpu.HBM)
    dma = pltpu.SemaphoreType.DMA
    outs = pl.pallas_call(
        body, name=name, in_specs=[HBM_ONLY] * (2 * n), out_specs=[SEM_SPEC, SEM_SPEC] + [HBM_ONLY] * (2 * n) + [VMEM_SPEC],
        out_shape=[dma((7 * n,)), dma((7 * n,))] + [pltpu.HBM(a.shape, a.dtype) for a in list(grads) + zones]
        + [jax.ShapeDtypeStruct((8, 128), F32)],
        input_output_aliases={i: 2 + i for i in range(2 * n)},
        compiler_params=pltpu.CompilerParams(has_side_effects=DATAFLOW))(*[hbm(a) for a in list(grads) + zones])
    return outs[0], outs[1], outs[2:2 + n], outs[2 + n:2 + 2 * n], outs[-1]


def _scatter_wait(started, kinds, after, name):
    ssem, rsem, grads, zones, _ = started
    n = len(grads)

    def body(*refs):
        srcs, lands = refs[:n], refs[n:2 * n]
        ssem_ref, rsem_ref = refs[2 * n], refs[2 * n + 1]
        for k in range(1, N_DEV):
            peer = _dev_index(_peer(k))
            for i, t in enumerate(kinds):
                q = (k - 1) * n + i
                cp = _remote(_reduced_by(srcs[i], t, peer), lands[i].at[peer], ssem_ref.at[q], rsem_ref.at[q], _peer(k))
                cp.wait_send()
                cp.wait_recv()

    outs = pl.pallas_call(
        body, name=name, in_specs=[HBM_ONLY] * (2 * n) + [SEM_SPEC, SEM_SPEC, HBM_SPEC], out_specs=[HBM_ONLY] * (2 * n),
        out_shape=[pltpu.HBM(a.shape, a.dtype) for a in list(grads) + list(zones)],
        input_output_aliases={i: i for i in range(2 * n)},
        compiler_params=pltpu.CompilerParams(has_side_effects=DATAFLOW))(*grads, *zones, ssem, rsem, after)
    return outs[:n], outs[n:]


def _grad_finish(grads, zones, kinds, name):
    nt = len(grads)
    pieces = _chunks(kinds, GATHER_CHUNK)
    npc = len(pieces)
    shard = [SHARD_ROWS[k] for k in kinds]

    def body(*refs):
        srcs, lands, outs = refs[:nt], refs[nt:2 * nt], refs[2 * nt:3 * nt]
        slots, sums = refs[3 * nt:4 * nt], refs[4 * nt:5 * nt]
        lsem, osem, xsem, ysem = refs[5 * nt:]
        x, y, c = _coords()
        me = _dev_index((x, y, c))
        sib = (x, y, 1 - c)
        loads = [pltpu.make_async_copy(_reduced_by(srcs[t], kinds[t], me), slots[t].at[me], lsem.at[t]) for t in range(nt)]
        for k in range(1, N_DEV):
            peer = _dev_index(_peer(k))
            loads += [pltpu.make_async_copy(lands[t].at[peer], slots[t].at[peer], lsem.at[k * nt + t]) for t in range(nt)]
        for cp in loads:
            cp.start()
        for cp in loads:
            cp.wait()
        sends = []
        place = lambda t, cc, off, n: _rows(outs[t], cc * (shard[t] // 2) + off, n)
        stores = []
        for q, (t, off, n) in enumerate(pieces):
            r = pl.ds(off, n)
            acc = slots[t][0, r, :].astype(F32)
            for d in range(1, N_DEV):
                acc = acc + slots[t][d, r, :].astype(F32)
            sums[t][r, :] = acc
            keep = pltpu.make_async_copy(sums[t].at[r, :], place(t, c, off, n), osem.at[q])
            give = _remote(sums[t].at[r, :], place(t, c, off, n), xsem.at[q], ysem.at[q], sib)
            keep.start()
            give.start()
            stores.append(keep)
            sends.append(give)
        for q, (t, off, n) in enumerate(pieces):
            got = place(t, 1 - c, off, n)
            _remote(got, got, xsem.at[q], ysem.at[q], sib).wait_recv()
        for cp in sends:
            cp.wait_send()
        for cp in stores:
            cp.wait()

    dma = pltpu.SemaphoreType.DMA
    return pl.pallas_call(
        body, name=name, in_specs=[HBM_SPEC] * (2 * nt), out_specs=[HBM_SPEC] * nt,
        out_shape=[jax.ShapeDtypeStruct((shard[t], g.shape[1]), F32) for t, g in enumerate(grads)],
        scratch_shapes=([pltpu.VMEM((N_DEV, shard[t] // 2, g.shape[1]), g.dtype) for t, g in enumerate(grads)]
                        + [pltpu.VMEM((shard[t] // 2, g.shape[1]), F32) for t, g in enumerate(grads)]
                        + [dma((N_DEV * nt,)), dma((npc,)), dma((npc,)), dma((npc,))]),
        compiler_params=_cp())(*grads, *zones)


def _adam_math(w, g, m, v):
    m = ADAM_B1 * m + (1.0 - ADAM_B1) * g
    v = ADAM_B2 * v + (1.0 - ADAM_B2) * (g * g)
    m_hat = m / (1.0 - ADAM_B1 ** ADAM_STEP)
    v_hat = v / (1.0 - ADAM_B2 ** ADAM_STEP)
    return -ADAM_LR * (m_hat / (jnp.sqrt(v_hat) + ADAM_EPS) + ADAM_WD * w), m, v


def _adamw(w, g, m, v, rb, name):
    rows, cols = w.shape

    def body(w_ref, g_ref, m_ref, v_ref, d_ref, nm_ref, nv_ref):
        d_ref[...], nm_ref[...], nv_ref[...] = _adam_math(w_ref[...], g_ref[...], m_ref[...], v_ref[...])

    spec = pl.BlockSpec((rb, cols), lambda i: (i, 0))
    return pl.pallas_call(body, grid=(rows // rb,), name=name, in_specs=[spec] * 4, out_specs=[spec] * 3,
                          out_shape=[jax.ShapeDtypeStruct(w.shape, F32)] * 3, compiler_params=_cp("parallel"))(w, g, m, v)


PACK = (("b_ada", 3 * D_MODEL), ("g_pre", D_MODEL), ("g_post", D_MODEL), ("gn_g", 512), ("qn_g", 64), ("kn_g", 64),
        ("w_dec_f", 4), ("w_dec_b", 4), ("loss", 1))
PACK_OFFSETS = {}
PACK_WIDTH = 0
for _name, _n in PACK:
    PACK_OFFSETS[_name] = PACK_WIDTH
    PACK_WIDTH += -(-_n // 128) * 128
SMALL_PARAMS = tuple(name for name, _ in PACK if name != "loss")


def _pack(parts):
    cols = []
    for name, n in PACK:
        pad = -(-n // 128) * 128 - n
        cols.append(parts[name].reshape(1, n).astype(F32))
        if pad:
            cols.append(jnp.zeros((1, pad), F32))
    return jnp.concatenate(cols, axis=1)


def _small_reduce(vec, cs, dep):
    ncol = 3 * D_MODEL // N_CHIP

    def body(vec_ref, cs_ref, dep_ref, tot_ref, gwa_ref, gat, ssem, rsem):
        me = _dev_index(_coords())
        chip = me // 2
        gat[me] = vec_ref[...]
        sends = []
        for k in range(1, N_DEV):
            cp = _remote(vec_ref, gat.at[me], ssem.at[k - 1], rsem.at[k - 1], _peer(k))
            cp.start()
            sends.append(cp)
        for k in range(1, N_DEV):
            slot = gat.at[_dev_index(_peer(k))]
            _remote(slot, slot, ssem.at[k - 1], rsem.at[k - 1], _peer(k)).wait_recv()
        rows = [gat[d] for d in range(N_DEV)]
        tot = rows[0]
        for d in range(1, N_DEV):
            tot = tot + rows[d]
        tot_ref[...] = tot
        cs = cs_ref[...]
        ca_t = jnp.transpose(jnp.concatenate([cs * _sigmoid(cs), jnp.zeros((128 - N_DEV, D_MODEL), F32)], axis=0))
        dm = jnp.concatenate(rows + [jnp.zeros((128 - N_DEV, PACK_WIDTH), F32)], axis=0)
        for jj in range(N_CHIP):
            @pl.when(chip == jj)
            def _():
                gwa_ref[...] = _nn(ca_t, dm[:, ncol * jj:ncol * (jj + 1)], precision=HIGHEST)
        for cp in sends:
            cp.wait_send()

    return pl.pallas_call(
        body, name="small_reduce", in_specs=[VMEM_SPEC, VMEM_SPEC, HBM_SPEC], out_specs=[VMEM_SPEC] * 2,
        out_shape=[jax.ShapeDtypeStruct((1, PACK_WIDTH), F32), jax.ShapeDtypeStruct((D_MODEL, ncol), F32)],
        scratch_shapes=[pltpu.VMEM((N_DEV, 1, PACK_WIDTH), F32), pltpu.SemaphoreType.DMA((7,)), pltpu.SemaphoreType.DMA((7,))],
        compiler_params=_cp())(vec, cs, dep)


def _small_adamw(tot, given):
    sizes = dict(PACK)
    np_ = len(SMALL_PARAMS)

    def body(*refs):
        tot_ref = refs[0]
        wmv = refs[1:1 + 3 * np_]
        outs = refs[1 + 3 * np_:1 + 7 * np_]
        loss_ref = refs[-1]
        for i, name in enumerate(SMALL_PARAMS):
            off, n = PACK_OFFSETS[name], sizes[name]
            g = tot_ref[:, off:off + n]
            w_ref, m_ref, v_ref = wmv[3 * i:3 * i + 3]
            outs[i][...] = g
            outs[np_ + i][...], outs[2 * np_ + i][...], outs[3 * np_ + i][...] = _adam_math(w_ref[...], g, m_ref[...], v_ref[...])
        loss_ref[...] = tot_ref[:, PACK_OFFSETS["loss"]:PACK_OFFSETS["loss"] + 1]

    flat = [a for name in SMALL_PARAMS for a in given[name]]
    shapes = [jax.ShapeDtypeStruct((1, sizes[name]), F32) for name in SMALL_PARAMS]
    res = pl.pallas_call(body, name="small_adamw", in_specs=[VMEM_SPEC] * (1 + 3 * np_), out_specs=[VMEM_SPEC] * (4 * np_ + 1),
                         out_shape=shapes * 4 + [jax.ShapeDtypeStruct((1, 1), F32)], compiler_params=_cp())(tot, *flat)
    return [dict(zip(SMALL_PARAMS, res[k * np_:(k + 1) * np_])) for k in range(4)], res[-1]


def kernel(x, c, w_ada, b_ada, g_pre, w_in, qn_g, kn_g, w_dec_f, w_dec_b, gn_g, w_pa, w_pr, w_out, g_post, loss_target, m_w_ada, m_b_ada, m_g_pre, m_w_in, m_qn_g, m_kn_g, m_w_dec_f, m_w_dec_b, m_gn_g, m_w_pa, m_w_pr, m_w_out, m_g_post, v_w_ada, v_b_ada, v_g_pre, v_w_in, v_qn_g, v_kn_g, v_w_dec_f, v_w_dec_b, v_gn_g, v_w_pa, v_w_pr, v_w_out, v_g_post):
    shards = (w_in[0].T.astype(MXU_DTYPE), jnp.concatenate([w_pa[0].T, w_pr[0].T], axis=1).astype(MXU_DTYPE),
              w_out[0].astype(MXU_DTYPE))
    cs, mod = _mod_exchange(c, w_ada[0], b_ada.reshape(N_CHIP, 3 * D_MODEL // N_CHIP))
    w_attn = _gather_attn(shards[0])
    started = {}

    def rest_start():
        started["rest"] = _gather_rest_start(shards, _place_own(shards))
        return started["rest"][-1]

    def rest_wait(after):
        return _gather_rest_wait(started["rest"], after)

    kinds = {"early": (1, 2), "late": (0,)}

    def send(name, arrays):
        started[name] = _scatter_start(arrays, kinds[name], "grad_scatter_" + name)
        return started[name][-1]

    grad_x, _, _, _, small = _local_step(x[0], loss_target[0], mod, g_pre, qn_g, kn_g, w_dec_f, w_dec_b,
                                         gn_g, g_post, w_attn, rest_start, rest_wait, send=send)
    small = dict(small)
    small["b_ada"] = small.pop("dmod")
    given = dict(b_ada=(b_ada, m_b_ada, v_b_ada), g_pre=(g_pre, m_g_pre, v_g_pre), g_post=(g_post, m_g_post, v_g_post),
                 gn_g=(gn_g, m_gn_g, v_gn_g), qn_g=(qn_g, m_qn_g, v_qn_g), kn_g=(kn_g, m_kn_g, v_kn_g),
                 w_dec_f=(w_dec_f, m_w_dec_f, v_w_dec_f), w_dec_b=(w_dec_b, m_w_dec_b, v_w_dec_b))
    grads, deltas, new_m, new_v = {}, {}, {}, {}

    def update(name, w, g, m, v, back):
        d, nm, nv = _adamw(w, g, m, v, w.shape[0] // 4, "adamw_" + name)
        grads[name], deltas[name], new_m[name], new_v[name] = back(g), back(d), back(nm), back(nv)
        return d

    lead = lambda a: a[None]
    (g_pt, g_out), (z_pt, z_out) = _scatter_wait(started["early"], kinds["early"], grad_x, "grad_scatter_early_wait")
    r_pt, r_out = _grad_finish((g_pt, g_out), (z_pt, z_out), kinds["early"], "grad_finish_early")
    update("w_pa", w_pa[0], r_pt[:, :512].T, m_w_pa[0], v_w_pa[0], lead)
    update("w_pr", w_pr[0], r_pt[:, 512:].T, m_w_pr[0], v_w_pr[0], lead)
    last = update("w_out", w_out[0], r_out, m_w_out[0], v_w_out[0], lead)
    tot, g_w_ada = _small_reduce(_pack(small), cs.reshape(N_DEV, D_MODEL), last)
    small_parts, loss = _small_adamw(tot, given)
    for dst, part in zip((grads, deltas, new_m, new_v), small_parts):
        dst.update(part)
    last = update("w_ada", w_ada[0], g_w_ada, m_w_ada[0], v_w_ada[0], lead)

    (g_in_t,), (z_in,) = _scatter_wait(started["late"], kinds["late"], last, "grad_scatter_late_wait")
    (r_in,) = _grad_finish((g_in_t,), (z_in,), kinds["late"], "grad_finish_late")
    tr = lambda a: a[0].T
    update("w_in", tr(w_in), r_in, tr(m_w_in), tr(v_w_in), lambda a: a.T[None])
    order = ("w_ada", "b_ada", "g_pre", "w_in", "qn_g", "kn_g", "w_dec_f", "w_dec_b", "gn_g", "w_pa", "w_pr", "w_out", "g_post")
    return (loss[0, 0], grad_x[None], *[grads[n] for n in order], *[deltas[n] for n in order],
            *[new_m[n] for n in order], *[new_v[n] for n in order])
```

```python
import functools

import jax
import jax.numpy as jnp
import numpy as np
from jax import lax
from jax.experimental import pallas as pl
from jax.experimental.pallas import tpu as pltpu

F32 = jnp.float32
BF16 = jnp.bfloat16
MXU_DTYPE = jnp.bfloat16

D_MODEL = 1024
GRID_W = 64
HEAD_DIM = 64
ROPE_THETA = 10000.0
EPS = 1e-6
RET_HEADS = 4
RET_CHUNK = 256
IN_WIDTH = 4864
QA, KA, VA, ZA, QR, KR, VR, ZR, GL = 0, 512, 640, 768, 1280, 1536, 1792, 2304, 2816
ATTN_COLS = ZA
ZA_B, QR_B, KR_B, VR_B, ZR_B, GL_B = (c - ATTN_COLS for c in (ZA, QR, KR, VR, ZR, GL))

ADAM_LR, ADAM_B1, ADAM_B2, ADAM_EPS, ADAM_WD, ADAM_STEP = 0.001, 0.9, 0.999, 1e-08, 0.01, 10

VMEM_LIMIT = 56 * 1024 * 1024
MESH = pl.DeviceIdType.MESH
HIGHEST = lax.Precision.HIGHEST
LOG2E = 1.4426950408889634
LN2 = 0.6931471805599453


def _cp(*sem, **kw):
    if sem:
        kw["dimension_semantics"] = sem
    return pltpu.CompilerParams(vmem_limit_bytes=VMEM_LIMIT, **kw)


def _nn(a, b, **kw):
    return lax.dot_general(a, b, (((1,), (0,)), ((), ())), preferred_element_type=F32, **kw)


def _nt(a, b):
    return lax.dot_general(a, b, (((1,), (1,)), ((), ())), preferred_element_type=F32)


def _tn(a, b):
    return lax.dot_general(a, b, (((0,), (0,)), ((), ())), preferred_element_type=F32)


def _mx(x):
    return x.astype(MXU_DTYPE)


def _lane(shape):
    return lax.broadcasted_iota(jnp.int32, shape, 1)


def _sigmoid(z):
    return 1.0 / (1.0 + jnp.exp(-z))


def _rowsum128(x):
    s = jnp.sum(x, axis=0, keepdims=True)
    out = s[:, 0:128]
    for k in range(1, x.shape[1] // 128):
        out = out + s[:, 128 * k:128 * (k + 1)]
    return out


def _swap16(x):
    return jnp.where((_lane(x.shape) & 16) == 0, pltpu.roll(x, 112, 1), pltpu.roll(x, 16, 1))


def _rope(x, cos, sin):
    return x * cos + _swap16(x) * sin


def _rope_t(d, cos, sin):
    return d * cos + _swap16(d * sin)


def _blockdiag64():
    r = lax.broadcasted_iota(jnp.int32, (128, 128), 0) // 64
    c = lax.broadcasted_iota(jnp.int32, (128, 128), 1) // 64
    return (r == c).astype(BF16)


def _seg64(x, m):
    hi = x.astype(BF16)
    lo = (x - hi.astype(F32)).astype(BF16)
    return _nn(hi, m) + _nn(lo, m)


def _rope_tables(seq):
    t = np.arange(seq)
    row = (t // GRID_W).astype(np.float32)
    col = (t % GRID_W).astype(np.float32)
    half = HEAD_DIM // 2
    inv_freq = (np.float32(ROPE_THETA) ** (-np.arange(0, half, 2, dtype=np.float32) / np.float32(half))).astype(np.float32)
    ar = (row[:, None] * inv_freq[None, :]).astype(np.float32).astype(np.float64)
    ac = (col[:, None] * inv_freq[None, :]).astype(np.float32).astype(np.float64)
    cr, sr, cc, sc = np.cos(ar), np.sin(ar), np.cos(ac), np.sin(ac)
    cos64 = np.concatenate([cr, cr, cc, cc], axis=1)
    sin64 = np.concatenate([-sr, sr, -sc, sc], axis=1)
    return jnp.asarray(np.tile(cos64, (1, 2)), F32), jnp.asarray(np.tile(sin64, (1, 2)), F32)


def _prenorm(x, mod, g_pre):
    seq = x.shape[0]
    bs = 512

    def body(x_ref, mod_ref, g_ref, h_ref):
        xv = x_ref[...]
        r = lax.rsqrt(jnp.mean(xv * xv, axis=-1, keepdims=True) + EPS)
        shift = mod_ref[:, 0:D_MODEL]
        scale = mod_ref[:, D_MODEL:2 * D_MODEL]
        h_ref[...] = ((xv * r) * g_ref[...] * (1.0 + scale) + shift).astype(h_ref.dtype)

    return pl.pallas_call(
        body, grid=(seq // bs,), name="prenorm",
        in_specs=[pl.BlockSpec((bs, D_MODEL), lambda i: (i, 0)), pl.BlockSpec((1, 3 * D_MODEL), lambda i: (0, 0)),
                  pl.BlockSpec((1, D_MODEL), lambda i: (0, 0))],
        out_specs=pl.BlockSpec((bs, D_MODEL), lambda i: (i, 0)),
        out_shape=jax.ShapeDtypeStruct((seq, D_MODEL), MXU_DTYPE), compiler_params=_cp("parallel"))(x, mod, g_pre)


def _prenorm_bwd(x, dh, dout, mod, g_pre):
    seq = x.shape[0]
    bs = 512

    def body(x_ref, dh_ref, dout_ref, mod_ref, g_ref, gx_ref, dshift_ref, dscale_ref, dg_ref):
        @pl.when(pl.program_id(0) == 0)
        def _():
            dshift_ref[...] = jnp.zeros_like(dshift_ref)
            dscale_ref[...] = jnp.zeros_like(dscale_ref)
            dg_ref[...] = jnp.zeros_like(dg_ref)

        xv = x_ref[...]
        dh = dh_ref[...]
        g = g_ref[...]
        r = lax.rsqrt(jnp.mean(xv * xv, axis=-1, keepdims=True) + EPS)
        xn = xv * r
        scale = mod_ref[:, D_MODEL:2 * D_MODEL]
        dshift_ref[...] += jnp.sum(dh, axis=0, keepdims=True)
        dscale_ref[...] += jnp.sum(dh * (xn * g), axis=0, keepdims=True)
        da = dh * (1.0 + scale)
        dg_ref[...] += jnp.sum(da * xn, axis=0, keepdims=True)
        dxn = da * g
        dx = r * (dxn - xn * jnp.mean(dxn * xn, axis=-1, keepdims=True))
        gx_ref[...] = dout_ref[...] + dx

    row = pl.BlockSpec((bs, D_MODEL), lambda i: (i, 0))
    vec = pl.BlockSpec((1, D_MODEL), lambda i: (0, 0))
    return pl.pallas_call(
        body, grid=(seq // bs,), name="prenorm_bwd",
        in_specs=[row, row, row, pl.BlockSpec((1, 3 * D_MODEL), lambda i: (0, 0)), vec],
        out_specs=[row, vec, vec, vec],
        out_shape=[jax.ShapeDtypeStruct((seq, D_MODEL), F32)] + [jax.ShapeDtypeStruct((1, D_MODEL), F32)] * 3,
        compiler_params=_cp("arbitrary"))(x, dh, dout, mod, g_pre)


def _dep_args(dep, grid_rank):
    if dep is None:
        return [], []
    return [dep], [pl.BlockSpec(dep.shape, lambda *_: (0,) * dep.ndim)]


def _matmul(a, b, *, ta, tb, tm, tn, out_dtype, name, dep=None, b_rows=None):
    kdim = a.shape[0] if ta else a.shape[1]
    m = a.shape[1] if ta else a.shape[0]
    n = b.shape[0] if tb else b.shape[1]
    if b_rows is not None:
        assert tb
        n = b_rows[1]
    assert m % tm == 0 and n % tn == 0
    deps, dep_specs = _dep_args(dep, 2)

    def body(a_ref, b_ref, *rest):
        o_ref = rest[-1]
        dims = (((0 if ta else 1,), (1 if tb else 0,)), ((), ()))
        o_ref[...] = lax.dot_general(a_ref[...], b_ref[...], dims, preferred_element_type=F32).astype(o_ref.dtype)

    a_spec = pl.BlockSpec((kdim, tm), lambda j, i: (0, i)) if ta else pl.BlockSpec((tm, kdim), lambda j, i: (i, 0))
    b_spec = pl.BlockSpec((tn, kdim), lambda j, i: (j, 0)) if tb else pl.BlockSpec((kdim, tn), lambda j, i: (0, j))
    if b_rows is not None:
        assert b_rows[0] % 128 == 0 and tn % 128 == 0
        b_spec = pl.BlockSpec((pl.Element(tn), pl.Element(kdim)), lambda j, i: (pl.multiple_of(b_rows[0] + j * tn, 128), 0))
    return pl.pallas_call(
        body, grid=(n // tn, m // tm), name=name, in_specs=[a_spec, b_spec] + dep_specs,
        out_specs=pl.BlockSpec((tm, tn), lambda j, i: (i, j)),
        out_shape=jax.ShapeDtypeStruct((m, n), out_dtype), compiler_params=_cp("parallel", "parallel"))(a, b, *deps)


def _prep_attn(p_a, cos, sin, qg2, kg2):
    seq = p_a.shape[0]
    bs = 512

    def body(p_ref, cos_ref, sin_ref, qg_ref, kg_ref, qt_ref, kd_ref, vd_ref):
        m = _blockdiag64()
        cs, sn = cos_ref[...], sin_ref[...]
        lo = _lane((bs, 128)) < 64
        for pr in range(4):
            q = p_ref[:, QA + 128 * pr:QA + 128 * (pr + 1)]
            r = lax.rsqrt(_seg64(q * q, m) * (1.0 / 64) + EPS)
            qt_ref[:, 128 * pr:128 * (pr + 1)] = (_rope(q * r * qg_ref[...], cs, sn) * (0.125 * LOG2E)).astype(qt_ref.dtype)
        k = p_ref[:, KA:KA + 128]
        r = lax.rsqrt(_seg64(k * k, m) * (1.0 / 64) + EPS)
        kr = _rope(k * r * kg_ref[...], cs, sn)
        ksw = pltpu.roll(kr, 64, 1)
        kd_ref[0] = jnp.where(lo, kr, ksw).astype(kd_ref.dtype)
        kd_ref[1] = jnp.where(lo, ksw, kr).astype(kd_ref.dtype)
        v = p_ref[:, VA:VA + 128]
        vsw = pltpu.roll(v, 64, 1)
        vd_ref[0] = jnp.where(lo, v, vsw).astype(vd_ref.dtype)
        vd_ref[1] = jnp.where(lo, vsw, v).astype(vd_ref.dtype)

    tab = pl.BlockSpec((bs, 128), lambda i: (i, 0))
    gsp = pl.BlockSpec((1, 128), lambda i: (0, 0))
    dup = pl.BlockSpec((2, bs, 128), lambda i: (0, i, 0))
    return pl.pallas_call(
        body, grid=(seq // bs,), name="attn_prep",
        in_specs=[pl.BlockSpec((bs, ATTN_COLS), lambda i: (i, 0)), tab, tab, gsp, gsp],
        out_specs=[pl.BlockSpec((bs, 512), lambda i: (i, 0)), dup, dup],
        out_shape=[jax.ShapeDtypeStruct((seq, 512), MXU_DTYPE), jax.ShapeDtypeStruct((2, seq, 128), MXU_DTYPE),
                   jax.ShapeDtypeStruct((2, seq, 128), MXU_DTYPE)],
        compiler_params=_cp("parallel"))(p_a, cos, sin, qg2, kg2)


def _prep_ret(p_b, cos, sin):
    seq = p_b.shape[0]
    bs = 512

    def body(p_ref, cos_ref, sin_ref, rq_ref, rk_ref, rv_ref):
        cs, sn = cos_ref[...], sin_ref[...]
        for pr in range(2):
            sl = slice(128 * pr, 128 * (pr + 1))
            rq_ref[:, sl] = _rope(p_ref[:, QR_B + 128 * pr:QR_B + 128 * (pr + 1)], cs, sn).astype(rq_ref.dtype)
            rk_ref[:, sl] = (_rope(p_ref[:, KR_B + 128 * pr:KR_B + 128 * (pr + 1)], cs, sn) * 0.125).astype(rk_ref.dtype)
        rv_ref[...] = p_ref[:, VR_B:VR_B + 512].astype(rv_ref.dtype)

    tab = pl.BlockSpec((bs, 128), lambda i: (i, 0))
    return pl.pallas_call(
        body, grid=(seq // bs,), name="ret_prep",
        in_specs=[pl.BlockSpec((bs, ZR_B), lambda i: (i, 0)), tab, tab],
        out_specs=[pl.BlockSpec((bs, 256), lambda i: (i, 0)), pl.BlockSpec((bs, 256), lambda i: (i, 0)),
                   pl.BlockSpec((bs, 512), lambda i: (i, 0))],
        out_shape=[jax.ShapeDtypeStruct((seq, 256), MXU_DTYPE), jax.ShapeDtypeStruct((seq, 256), MXU_DTYPE),
                   jax.ShapeDtypeStruct((seq, 512), MXU_DTYPE)],
        compiler_params=_cp("parallel"))(p_b, cos, sin)


def _halves(kd):
    lo = _lane(kd.shape) < 64
    z = jnp.zeros_like(kd)
    return jnp.concatenate([jnp.where(lo, kd, z), jnp.where(lo, z, kd)], axis=0)


def _attn_fwd(qt, kd, vd, dep=None):
    seq = qt.shape[0]
    bq = bk = 512
    nk = seq // bk
    deps, dep_specs = _dep_args(dep, 2)

    def body(q_ref, k_ref, v_ref, *rest):
        o_ref, lse_ref, m_scr, l_scr, acc = rest[-5:]
        j = pl.program_id(1)
        m_scr[...] = jnp.full_like(m_scr, -jnp.inf)
        l_scr[...] = jnp.zeros_like(l_scr)
        acc[...] = jnp.zeros_like(acc)
        lo = _lane((bq, 128)) < 64

        def kv_block(n, carry):
            rows = pl.ds(pl.multiple_of(n * bk, bk), bk)
            k2, v2 = _halves(k_ref[rows, :]), _halves(v_ref[rows, :])
            for pr in range(2):
                s2 = _nt(q_ref[:, 128 * pr:128 * (pr + 1)], k2)
                ps, alphas = [], []
                for e in range(2):
                    g = 2 * pr + e
                    s = s2[:, e * bk:(e + 1) * bk]
                    m_prev = m_scr[g]
                    m_next = jnp.maximum(m_prev, jnp.max(s, axis=1, keepdims=True))
                    alpha = jnp.exp2(m_prev - m_next)
                    pe = jnp.exp2(s - jnp.tile(m_next, (1, bk // 128)))
                    l_scr[g] = alpha * l_scr[g] + jnp.sum(pe, axis=1, keepdims=True)
                    m_scr[g] = m_next
                    ps.append(_mx(pe))
                    alphas.append(alpha)
                o2 = _nn(jnp.concatenate(ps, axis=1), v2)
                sl = slice(128 * pr, 128 * (pr + 1))
                acc[:, sl] = acc[:, sl] * jnp.where(lo, alphas[0], alphas[1]) + o2
            return carry

        lax.fori_loop(0, nk, kv_block, 0)
        for pr in range(2):
            sl = slice(128 * pr, 128 * (pr + 1))
            o_ref[:, sl] = acc[:, sl] / jnp.where(lo, l_scr[2 * pr], l_scr[2 * pr + 1])
        for g in range(4):
            lse = jnp.transpose(m_scr[g] + jnp.log2(l_scr[g]))
            lse_ref[pl.ds(4 * j + g, 1), :] = lse[0:1, :]

    return pl.pallas_call(
        body, grid=(seq // bq, 2), name="attn_fwd",
        in_specs=[pl.BlockSpec((bq, 256), lambda i, j: (i, j)), pl.BlockSpec((None, seq, 128), lambda i, j: (j, 0, 0)),
                  pl.BlockSpec((None, seq, 128), lambda i, j: (j, 0, 0))] + dep_specs,
        out_specs=[pl.BlockSpec((bq, 256), lambda i, j: (i, j)), pl.BlockSpec((8, bq), lambda i, j: (0, i))],
        out_shape=[jax.ShapeDtypeStruct((seq, 512), F32), jax.ShapeDtypeStruct((8, seq), F32)],
        scratch_shapes=[pltpu.VMEM((4, bq, 128), F32), pltpu.VMEM((4, bq, 128), F32), pltpu.VMEM((bq, 256), F32)],
        compiler_params=_cp("parallel", "arbitrary"))(qt, kd, vd, *deps)


def _attn_bwd(qt, kd, vd, do, lse, delta, dep=None):
    seq = qt.shape[0]
    bq = bk = 512
    nq, nk = seq // bq, seq // bk
    deps, dep_specs = _dep_args(dep, 3)

    def body(q_ref, do_ref, k_ref, v_ref, lse_ref, del_ref, *rest):
        dq_ref, dk_ref, dv_ref = rest[-3:]
        j, i = pl.program_id(0), pl.program_id(1)
        dq_ref[...] = jnp.zeros_like(dq_ref)

        @pl.when(i == 0)
        def _():
            dk_ref[...] = jnp.zeros_like(dk_ref)
            dv_ref[...] = jnp.zeros_like(dv_ref)

        lo = _lane((bk, 128)) < 64
        big = lambda r: jnp.concatenate([jnp.broadcast_to(r[0], (bk, bq)), jnp.broadcast_to(r[1], (bk, bq))], axis=0)

        def kv_block(n, carry):
            rows = pl.ds(pl.multiple_of(n * bk, bk), bk)
            k2, v2 = _halves(k_ref[rows, :]), _halves(v_ref[rows, :])
            for pr in range(2):
                sl = slice(128 * pr, 128 * (pr + 1))
                qp, dop = q_ref[:, sl], do_ref[:, sl]
                s_t = _nt(k2, qp)
                dp_t = _nt(v2, dop)
                ls = [lse_ref[pl.ds(4 * j + 2 * pr + e, 1), :] for e in range(2)]
                dl = [del_ref[pl.ds(4 * j + 2 * pr + e, 1), :] for e in range(2)]
                p_t = jnp.exp2(s_t - big(ls))
                ds_t = _mx(p_t * (dp_t - big(dl)))
                rv = _nn(_mx(p_t), dop)
                rk = _nn(ds_t, qp) * LN2
                dv_ref[rows, :] += jnp.where(lo, rv[:bk], rv[bk:])
                dk_ref[rows, :] += jnp.where(lo, rk[:bk], rk[bk:])
                dq_ref[:, sl] += _tn(ds_t, k2)
            return carry

        lax.fori_loop(0, nk, kv_block, 0)

    return pl.pallas_call(
        body, grid=(2, nq), name="attn_bwd",
        in_specs=[pl.BlockSpec((bq, 256), lambda j, i: (i, j)), pl.BlockSpec((bq, 256), lambda j, i: (i, j)),
                  pl.BlockSpec((None, seq, 128), lambda j, i: (j, 0, 0)), pl.BlockSpec((None, seq, 128), lambda j, i: (j, 0, 0)),
                  pl.BlockSpec((8, bq), lambda j, i: (0, i)), pl.BlockSpec((8, bq), lambda j, i: (0, i))] + dep_specs,
        out_specs=[pl.BlockSpec((bq, 256), lambda j, i: (i, j)), pl.BlockSpec((None, seq, 128), lambda j, i: (j, 0, 0)),
                   pl.BlockSpec((None, seq, 128), lambda j, i: (j, 0, 0))],
        out_shape=[jax.ShapeDtypeStruct((seq, 512), F32), jax.ShapeDtypeStruct((2, seq, 128), F32),
                   jax.ShapeDtypeStruct((2, seq, 128), F32)],
        compiler_params=_cp("arbitrary", "arbitrary"))(qt, do, kd, vd, lse, delta, *deps)


def _ret_tables(lg_f, lg_b, c):
    idx = jnp.arange(c, dtype=F32)
    diff = idx[:, None] - idx[None, :]
    a, b = lg_f[:, None, None], lg_b[:, None, None]
    low, up = (diff >= 0)[None], (diff < 0)[None]
    dmat = jnp.where(low, jnp.exp(a * jnp.maximum(diff, 0.0)[None]), jnp.exp(b * jnp.maximum(-diff, 0.0)[None]))
    dda = jnp.where(low, diff[None] * jnp.exp(a * jnp.maximum(diff, 0.0)[None]), 0.0)
    ddb = jnp.where(up, -diff[None] * jnp.exp(b * jnp.maximum(-diff, 0.0)[None]), 0.0)
    rep = lambda w: jnp.broadcast_to(w[:, :, None], (RET_HEADS, c, 128))
    wqf = rep(jnp.exp(lg_f[:, None] * (idx + 1.0)[None]))
    wqb = rep(jnp.exp(lg_b[:, None] * (c - idx)[None]))
    wkf = rep(jnp.exp(lg_f[:, None] * (c - 1.0 - idx)[None]))
    wkb = rep(jnp.exp(lg_b[:, None] * idx[None]))
    cdf, cdb = jnp.exp(lg_f * c), jnp.exp(lg_b * c)
    dec_fb = jnp.broadcast_to(jnp.concatenate([cdf, cdb])[:, None], (8, 128))
    dec_bf = jnp.broadcast_to(jnp.concatenate([cdb, cdf])[:, None], (8, 128))
    return dict(dmat=dmat, dda=dda, ddb=ddb, wqf=wqf, wqb=wqb, wkf=wkf, wkb=wkb, dec_fb=dec_fb, dec_bf=dec_bf)


def _ret_states(xk, yv, w_fwd, w_rev, dec, name):
    seq = xk.shape[0]
    c = RET_CHUNK
    nc = seq // c

    def body(xf_ref, yf_ref, xr_ref, yr_ref, wf_ref, wr_ref, dec_ref, sf_ref, sr_ref, st_f, st_r):
        @pl.when(pl.program_id(0) == 0)
        def _():
            st_f[...] = jnp.zeros_like(st_f)
            st_r[...] = jnp.zeros_like(st_r)

        lane = _lane((c, 128))
        for h in range(RET_HEADS):
            pr, e = h // 2, h % 2
            half = (lane < 64) if e == 0 else (lane >= 64)
            for x_ref, y_ref, w_ref, s_ref, st, drow in ((xf_ref, yf_ref, wf_ref, sf_ref, st_f, h),
                                                        (xr_ref, yr_ref, wr_ref, sr_ref, st_r, 4 + h)):
                s_ref[h] = st[h].astype(s_ref.dtype)
                xm = jnp.where(half, x_ref[:, 128 * pr:128 * (pr + 1)].astype(F32) * w_ref[h], 0.0)
                st[h] = st[h] * dec_ref[drow:drow + 1, :] + _tn(_mx(xm), _mx(y_ref[:, 128 * h:128 * (h + 1)]))

    xs = lambda f: pl.BlockSpec((c, 256), f)
    ys = lambda f: pl.BlockSpec((c, 512), f)
    fwd, rev = (lambda n: (n, 0)), (lambda n: (nc - 1 - n, 0))
    wsp = pl.BlockSpec((RET_HEADS, c, 128), lambda n: (0, 0, 0))
    return pl.pallas_call(
        body, grid=(nc,), name=name,
        in_specs=[xs(fwd), ys(fwd), xs(rev), ys(rev), wsp, wsp, pl.BlockSpec((8, 128), lambda n: (0, 0))],
        out_specs=[pl.BlockSpec((None, RET_HEADS, 128, 128), lambda n: (n, 0, 0, 0)),
                   pl.BlockSpec((None, RET_HEADS, 128, 128), lambda n: (nc - 1 - n, 0, 0, 0))],
        out_shape=[jax.ShapeDtypeStruct((nc, RET_HEADS, 128, 128), MXU_DTYPE)] * 2,
        scratch_shapes=[pltpu.VMEM((RET_HEADS, 128, 128), F32), pltpu.VMEM((RET_HEADS, 128, 128), F32)],
        compiler_params=_cp("arbitrary"))(xk, yv, xk, yv, w_fwd, w_rev, dec)


def _ret_fwd(rq, rk, rv, rf, rb, tb):
    seq = rq.shape[0]
    c = RET_CHUNK

    def body(q_ref, k_ref, v_ref, rf_ref, rb_ref, d_ref, wqf_ref, wqb_ref, o_ref):
        lane = _lane((c, 128))
        for h in range(RET_HEADS):
            pr, e = h // 2, h % 2
            half = (lane < 64) if e == 0 else (lane >= 64)
            sl = slice(128 * pr, 128 * (pr + 1))
            qp = q_ref[:, sl]
            km = jnp.where(half, k_ref[:, sl], jnp.zeros_like(k_ref[:, sl]))
            sd = _mx(_nt(qp, km) * d_ref[h])
            qf = qp.astype(F32)
            o = _nn(sd, v_ref[:, 128 * h:128 * (h + 1)])
            o = o + _nn(_mx(qf * wqf_ref[h]), rf_ref[h]) + _nn(_mx(qf * wqb_ref[h]), rb_ref[h])
            o_ref[:, 128 * h:128 * (h + 1)] = o

    st = pl.BlockSpec((None, RET_HEADS, 128, 128), lambda n: (n, 0, 0, 0))
    wsp = pl.BlockSpec((RET_HEADS, c, 128), lambda n: (0, 0, 0))
    return pl.pallas_call(
        body, grid=(seq // c,), name="ret_fwd",
        in_specs=[pl.BlockSpec((c, 256), lambda n: (n, 0)), pl.BlockSpec((c, 256), lambda n: (n, 0)),
                  pl.BlockSpec((c, 512), lambda n: (n, 0)), st, st, pl.BlockSpec((RET_HEADS, c, c), lambda n: (0, 0, 0)), wsp, wsp],
        out_specs=pl.BlockSpec((c, 512), lambda n: (n, 0)),
        out_shape=jax.ShapeDtypeStruct((seq, 512), F32), compiler_params=_cp("parallel"))(
            rq, rk, rv, rf, rb, tb["dmat"], tb["wqf"], tb["wqb"])


def _ret_bwd(rq, rk, rv, do, rf, rb, hf, hb, tb):
    seq = rq.shape[0]
    c = RET_CHUNK

    def body(q_ref, k_ref, v_ref, do_ref, rf_ref, rb_ref, hf_ref, hb_ref, d_ref, dda_ref, ddb_ref,
             wqf_ref, wqb_ref, wkf_ref, wkb_ref, cdec_ref, dq_ref, dk_ref, dv_ref, dlg_ref):
        @pl.when(pl.program_id(0) == 0)
        def _():
            dlg_ref[...] = jnp.zeros_like(dlg_ref)

        lane = _lane((c, 128))
        pos = lax.broadcasted_iota(jnp.int32, (c, 128), 0).astype(F32)
        for pr in range(2):
            sl = slice(128 * pr, 128 * (pr + 1))
            qp, kp = q_ref[:, sl], k_ref[:, sl]
            qf, kf = qp.astype(F32), kp.astype(F32)
            dq_acc = jnp.zeros((c, 128), F32)
            dk_acc = jnp.zeros((c, 128), F32)
            for e in range(2):
                h = 2 * pr + e
                half = (lane < 64) if e == 0 else (lane >= 64)
                hs = slice(128 * h, 128 * (h + 1))
                km = jnp.where(half, kp, jnp.zeros_like(kp))
                kmf = km.astype(F32)
                vh, doh = v_ref[:, hs], do_ref[:, hs]
                rfh, rbh, hfh, hbh = rf_ref[h], rb_ref[h], hf_ref[h], hb_ref[h]
                s = _nt(qp, km)
                dpm = _nt(doh, vh)
                ds_b = _mx(dpm * d_ref[h])
                sd_b = _mx(s * d_ref[h])
                dq_if = wqf_ref[h] * _nt(doh, rfh)
                dq_ib = wqb_ref[h] * _nt(doh, rbh)
                dq_acc = dq_acc + _nn(ds_b, km) + dq_if + dq_ib
                dk_sf = wkf_ref[h] * _nt(vh, hfh)
                dk_sb = wkb_ref[h] * _nt(vh, hbh)
                dk_acc = dk_acc + jnp.where(half, _tn(ds_b, qp), 0.0) + dk_sf + dk_sb
                dv = _tn(sd_b, doh) + _nn(_mx(kmf * wkf_ref[h]), hfh) + _nn(_mx(kmf * wkb_ref[h]), hbh)
                dv_ref[:, hs] = dv.astype(dv_ref.dtype)
                sdp = s * dpm
                hr_f = hfh.astype(F32) * rfh.astype(F32)
                hr_b = hbh.astype(F32) * rbh.astype(F32)
                da = (_rowsum128(sdp * dda_ref[h]) + _rowsum128((pos + 1.0) * qf * dq_if)
                      + _rowsum128((c - 1.0 - pos) * kf * dk_sf) + cdec_ref[h:h + 1, :] * _rowsum128(hr_f))
                db = (_rowsum128(sdp * ddb_ref[h]) + _rowsum128((c - pos) * qf * dq_ib)
                      + _rowsum128(pos * kf * dk_sb) + cdec_ref[4 + h:5 + h, :] * _rowsum128(hr_b))
                dlg_ref[h:h + 1, :] += da
                dlg_ref[4 + h:5 + h, :] += db
            dq_ref[:, sl] = dq_acc
            dk_ref[:, sl] = dk_acc

    st = pl.BlockSpec((None, RET_HEADS, 128, 128), lambda n: (n, 0, 0, 0))
    wsp = pl.BlockSpec((RET_HEADS, c, 128), lambda n: (0, 0, 0))
    dsp = pl.BlockSpec((RET_HEADS, c, c), lambda n: (0, 0, 0))
    x256 = pl.BlockSpec((c, 256), lambda n: (n, 0))
    x512 = pl.BlockSpec((c, 512), lambda n: (n, 0))
    cdec = tb["dec_fb"] * float(c)
    return pl.pallas_call(
        body, grid=(seq // c,), name="ret_bwd",
        in_specs=[x256, x256, x512, x512, st, st, st, st, dsp, dsp, dsp, wsp, wsp, wsp, wsp,
                  pl.BlockSpec((8, 128), lambda n: (0, 0))],
        out_specs=[x256, x256, x512, pl.BlockSpec((8, 128), lambda n: (0, 0))],
        out_shape=[jax.ShapeDtypeStruct((seq, 256), F32), jax.ShapeDtypeStruct((seq, 256), F32),
                   jax.ShapeDtypeStruct((seq, 512), MXU_DTYPE), jax.ShapeDtypeStruct((8, 128), F32)],
        compiler_params=_cp("arbitrary"))(
            rq, rk, rv, do, rf, rb, hf, hb, tb["dmat"], tb["dda"], tb["ddb"], tb["wqf"], tb["wqb"], tb["wkf"], tb["wkb"], cdec)


def _tail(x, tgt, o_att, o_ret, p, mod, g_post, gn_g, wpt, wout):
    seq = x.shape[0]
    bs = 256
    nb = seq // bs

    def body(x_ref, t_ref, oa_ref, or_ref, za_ref, zr_ref, gl_ref, mod_ref, gp_ref, gn_ref, wpt_ref, wout_ref,
             dout_ref, dza_ref, dzr_ref, dgl_ref, doa_ref, dor_ref, del_ref, gout_ref, gpt_ref,
             dgate_ref, dgpost_ref, dgn_ref, loss_ref, gout_acc, gpt_acc):
        i = pl.program_id(0)

        @pl.when(i == 0)
        def _():
            gout_acc[...] = jnp.zeros_like(gout_acc)
            gpt_acc[...] = jnp.zeros_like(gpt_acc)
            dgate_ref[...] = jnp.zeros_like(dgate_ref)
            dgpost_ref[...] = jnp.zeros_like(dgpost_ref)
            dgn_ref[...] = jnp.zeros_like(dgn_ref)
            loss_ref[...] = jnp.zeros_like(loss_ref)

        oa, za, zr = oa_ref[...], za_ref[...], zr_ref[...]
        sga, sgr = _sigmoid(za), _sigmoid(zr)
        sza, szr = za * sga, zr * sgr
        ya_b = _mx(oa * sza)
        ons, rstds = [], []
        for h in range(RET_HEADS):
            oh = or_ref[:, 128 * h:128 * (h + 1)]
            xc = oh - jnp.mean(oh, axis=-1, keepdims=True)
            rstd = lax.rsqrt(jnp.mean(xc * xc, axis=-1, keepdims=True) + EPS)
            ons.append(xc * rstd)
            rstds.append(rstd)
        on = jnp.concatenate(ons, axis=1)
        yn = on * gn_ref[...]
        yr_b = _mx(yn * szr)
        wpa_t, wpr_t = wpt_ref[:, 0:512], wpt_ref[:, 512:1024]
        a_att = _nt(ya_b, wpa_t)
        a_ret = _nt(yr_b, wpr_t)
        gts = _sigmoid(gl_ref[...])
        g_att, g_ret = gts[:, 0:D_MODEL], gts[:, D_MODEL:2 * D_MODEL]
        merged_b = _mx(g_att * a_att + g_ret * a_ret)
        u = _nn(merged_b, wout_ref[...])
        r = lax.rsqrt(jnp.mean(u * u, axis=-1, keepdims=True) + EPS)
        un = u * r
        gpost = gp_ref[...]
        y = un * gpost
        gate = mod_ref[:, 2 * D_MODEL:3 * D_MODEL]
        err = x_ref[...] + gate * y - t_ref[...]
        loss_ref[...] += (0.5 / D_MODEL) * _rowsum128(err * err)
        dout = err * (1.0 / D_MODEL)
        dout_ref[...] = dout
        dgate_ref[...] += jnp.sum(dout * y, axis=0, keepdims=True)
        dy = dout * gate
        dgpost_ref[...] += jnp.sum(dy * un, axis=0, keepdims=True)
        dun = dy * gpost
        du_b = _mx(r * (dun - un * jnp.mean(dun * un, axis=-1, keepdims=True)))
        dmerged = _nt(du_b, wout_ref[...])
        gout_acc[...] += _tn(merged_b, du_b)
        d_att, d_ret = dmerged * g_att, dmerged * g_ret
        dgl_ref[:, 0:D_MODEL] = (d_att * a_att * (1.0 - g_att)).astype(dgl_ref.dtype)
        dgl_ref[:, D_MODEL:2 * D_MODEL] = (d_ret * a_ret * (1.0 - g_ret)).astype(dgl_ref.dtype)
        d_att_b, d_ret_b = _mx(d_att), _mx(d_ret)
        dya = _nn(d_att_b, wpa_t)
        dyr = _nn(d_ret_b, wpr_t)
        gpt_acc[:, 0:512] += _tn(d_att_b, ya_b)
        gpt_acc[:, 512:1024] += _tn(d_ret_b, yr_b)
        dza_ref[...] = (dya * oa * (sga * (1.0 + za * (1.0 - sga)))).astype(dza_ref.dtype)
        doa = dya * sza
        doa_ref[...] = doa.astype(doa_ref.dtype)
        dt = jnp.transpose(doa * oa)
        del_ref[...] = jnp.sum(dt.reshape(8, HEAD_DIM, bs), axis=1)
        dzr_ref[...] = (dyr * yn * (sgr * (1.0 + zr * (1.0 - sgr)))).astype(dzr_ref.dtype)
        dyn = dyr * szr
        dgn_ref[...] += jnp.sum(dyn * on, axis=0, keepdims=True)
        don = dyn * gn_ref[...]
        for h in range(RET_HEADS):
            hs = slice(128 * h, 128 * (h + 1))
            dh, oh = don[:, hs], ons[h]
            doh = rstds[h] * (dh - jnp.mean(dh, axis=-1, keepdims=True) - oh * jnp.mean(dh * oh, axis=-1, keepdims=True))
            dor_ref[:, hs] = doh.astype(dor_ref.dtype)

        @pl.when(i == nb - 1)
        def _():
            gout_ref[...] = gout_acc[...].astype(gout_ref.dtype)
            gpt_ref[...] = gpt_acc[...].astype(gpt_ref.dtype)

    row = lambda w: pl.BlockSpec((bs, w), lambda i: (i, 0))
    el = lambda w, off: pl.BlockSpec((pl.Element(bs), pl.Element(w)), lambda i: (i * bs, off))
    vec = lambda w: pl.BlockSpec((1, w), lambda i: (0, 0))
    full = pl.BlockSpec((D_MODEL, D_MODEL), lambda i: (0, 0))
    sds = jax.ShapeDtypeStruct
    return pl.pallas_call(
        body, grid=(nb,), name="tail",
        in_specs=[row(D_MODEL), row(D_MODEL), row(512), row(512), el(512, ZA_B), el(512, ZR_B), el(2 * D_MODEL, GL_B),
                  vec(3 * D_MODEL), vec(D_MODEL), vec(512), full, full],
        out_specs=[row(D_MODEL), row(512), row(512), row(2 * D_MODEL), row(512), row(512),
                   pl.BlockSpec((8, bs), lambda i: (0, i)), full, full, vec(D_MODEL), vec(D_MODEL), vec(512), vec(128)],
        out_shape=[sds((seq, D_MODEL), F32), sds((seq, 512), MXU_DTYPE), sds((seq, 512), MXU_DTYPE),
                   sds((seq, 2 * D_MODEL), MXU_DTYPE), sds((seq, 512), MXU_DTYPE), sds((seq, 512), MXU_DTYPE),
                   sds((8, seq), F32), sds((D_MODEL, D_MODEL), MXU_DTYPE), sds((D_MODEL, D_MODEL), MXU_DTYPE),
                   sds((1, D_MODEL), F32), sds((1, D_MODEL), F32), sds((1, 512), F32), sds((1, 128), F32)],
        scratch_shapes=[pltpu.VMEM((D_MODEL, D_MODEL), F32), pltpu.VMEM((D_MODEL, D_MODEL), F32)],
        compiler_params=_cp("arbitrary"))(x, tgt, o_att, o_ret, p, p, p, mod, g_post, gn_g, wpt, wout)


def _assemble(p, cos, sin, qg2, kg2, dqt, dkp, dvp, dza, drq, drk, drv, dzr, dgl):
    seq = p.shape[0]
    bs = 512

    def body(p_ref, cos_ref, sin_ref, qg_ref, kg_ref, dqt_ref, dkp_ref, dvp_ref, dza_ref, drq_ref, drk_ref, drv_ref,
             dzr_ref, dgl_ref, dp_ref, dqg_ref, dkg_ref):
        @pl.when(pl.program_id(0) == 0)
        def _():
            dqg_ref[...] = jnp.zeros_like(dqg_ref)
            dkg_ref[...] = jnp.zeros_like(dkg_ref)

        m = _blockdiag64()
        cs, sn = cos_ref[...], sin_ref[...]
        lo = _lane((bs, 128)) < 64

        def norm_bwd(raw, dn, g, dg_ref):
            r = lax.rsqrt(_seg64(raw * raw, m) * (1.0 / 64) + EPS)
            xn = raw * r
            dg_ref[...] += jnp.sum(dn * xn, axis=0, keepdims=True)
            dxn = dn * g
            return r * (dxn - xn * (_seg64(dxn * xn, m) * (1.0 / 64)))

        for pr in range(4):
            sl = slice(128 * pr, 128 * (pr + 1))
            dn = _rope_t(dqt_ref[:, sl] * 0.125, cs, sn)
            dp_ref[:, QA + 128 * pr:QA + 128 * (pr + 1)] = norm_bwd(p_ref[:, sl], dn, qg_ref[...], dqg_ref).astype(dp_ref.dtype)
        fold = lambda a: a + pltpu.roll(a, 64, 1)
        dk = jnp.where(lo, fold(dkp_ref[0]), fold(dkp_ref[1]))
        dp_ref[:, KA:KA + 128] = norm_bwd(p_ref[:, KA:KA + 128], _rope_t(dk, cs, sn), kg_ref[...], dkg_ref).astype(dp_ref.dtype)
        dp_ref[:, VA:VA + 128] = jnp.where(lo, fold(dvp_ref[0]), fold(dvp_ref[1])).astype(dp_ref.dtype)
        dp_ref[:, ZA:ZA + 512] = dza_ref[...]
        for pr in range(2):
            sl = slice(128 * pr, 128 * (pr + 1))
            dp_ref[:, QR + 128 * pr:QR + 128 * (pr + 1)] = _rope_t(drq_ref[:, sl], cs, sn).astype(dp_ref.dtype)
            dp_ref[:, KR + 128 * pr:KR + 128 * (pr + 1)] = _rope_t(drk_ref[:, sl] * 0.125, cs, sn).astype(dp_ref.dtype)
        dp_ref[:, VR:VR + 512] = drv_ref[...]
        dp_ref[:, ZR:ZR + 512] = dzr_ref[...]
        dp_ref[:, GL:GL + 2 * D_MODEL] = dgl_ref[...]

    row = lambda w: pl.BlockSpec((bs, w), lambda i: (i, 0))
    gsp = pl.BlockSpec((1, 128), lambda i: (0, 0))
    dup = pl.BlockSpec((2, bs, 128), lambda i: (0, i, 0))
    return pl.pallas_call(
        body, grid=(seq // bs,), name="assemble_dp",
        in_specs=[row(768), row(128), row(128), gsp, gsp, row(512), dup, dup, row(512), row(256), row(256), row(512),
                  row(512), row(2 * D_MODEL)],
        out_specs=[row(IN_WIDTH), gsp, gsp],
        out_shape=[jax.ShapeDtypeStruct((seq, IN_WIDTH), MXU_DTYPE), jax.ShapeDtypeStruct((1, 128), F32),
                   jax.ShapeDtypeStruct((1, 128), F32)],
        compiler_params=_cp("arbitrary"))(p, cos, sin, qg2, kg2, dqt, dkp, dvp, dza, drq, drk, drv, dzr, dgl)


def _local_step(x, tgt, mod, g_pre, qn_g, kn_g, w_dec_f, w_dec_b, gn_g, g_post, w_attn, rest_start, rest_wait, send=None):
    send = send or (lambda name, arrays: None)
    seq = x.shape[0]
    cos, sin = _rope_tables(seq)
    qg2, kg2 = jnp.tile(qn_g, (1, 2)), jnp.tile(kn_g, (1, 2))
    lg_f = jax.nn.log_sigmoid(w_dec_f[0])
    lg_b = jax.nn.log_sigmoid(w_dec_b[0])
    tb = _ret_tables(lg_f, lg_b, RET_CHUNK)

    h = _prenorm(x, mod, g_pre)
    p_a = _matmul(h, w_attn, ta=False, tb=True, tm=512, tn=ATTN_COLS, out_dtype=F32, name="in_proj_attn")
    qt, kd, vd = _prep_attn(p_a, cos, sin, qg2, kg2)
    o_att, lse = _attn_fwd(qt, kd, vd, dep=rest_start(qt))
    win_t, wpt, wout = rest_wait(o_att)
    p_b = _matmul(h, win_t, ta=False, tb=True, tm=512, tn=(IN_WIDTH - ATTN_COLS) // 2, out_dtype=F32, name="in_proj_rest",
                  b_rows=(ATTN_COLS, IN_WIDTH - ATTN_COLS))
    rq, rk, rv = _prep_ret(p_b, cos, sin)
    rf, rb = _ret_states(rk, rv, tb["wkf"], tb["wkb"], tb["dec_fb"], "ret_states_fwd")
    o_ret = _ret_fwd(rq, rk, rv, rf, rb, tb)
    (dout, dza, dzr, dgl, do_att, do_ret, delta, g_out, g_pt, dgate, dgpost, dgn, loss_l) = _tail(
        x, tgt, o_att, o_ret, p_b, mod, g_post, gn_g, wpt, wout)
    dep = send("early", (g_pt, g_out))
    dqt, dkp, dvp = _attn_bwd(qt, kd, vd, do_att, lse, delta, dep=dep)
    hb, hf = _ret_states(rq, do_ret, tb["wqb"], tb["wqf"], tb["dec_bf"], "ret_states_bwd")
    drq, drk, drv, dlg = _ret_bwd(rq, rk, rv, do_ret, rf, rb, hf, hb, tb)
    dp, dqg, dkg = _assemble(p_a, cos, sin, qg2, kg2, dqt, dkp, dvp, dza, drq, drk, drv, dzr, dgl)
    g_in_t = _matmul(dp, h, ta=True, tb=False, tm=256, tn=D_MODEL, out_dtype=MXU_DTYPE, name="in_proj_dw")
    dep = send("late", (g_in_t,))
    dh = _matmul(dp, win_t, ta=False, tb=False, tm=512, tn=D_MODEL, out_dtype=F32, name="in_proj_dx", dep=dep)
    grad_x, dshift, dscale, dgpre = _prenorm_bwd(x, dh, dout, mod, g_pre)

    dlg = jnp.sum(dlg, axis=1)
    small = dict(
        dmod=jnp.concatenate([dshift, dscale, dgate], axis=1),
        g_pre=dgpre, g_post=dgpost, gn_g=dgn,
        qn_g=dqg[:, :64] + dqg[:, 64:], kn_g=dkg[:, :64] + dkg[:, 64:],
        w_dec_f=(dlg[0:4] * jax.nn.sigmoid(-w_dec_f[0]))[None], w_dec_b=(dlg[4:8] * jax.nn.sigmoid(-w_dec_b[0]))[None],
        loss=jnp.sum(loss_l, axis=1, keepdims=True))
    return grad_x, g_in_t, g_pt, g_out, small


N_DEV = 8
N_CHIP = 4
HBM_SPEC = pl.BlockSpec(memory_space=pl.ANY)
VMEM_SPEC = pl.BlockSpec(memory_space=pltpu.VMEM)
SHARD_ROWS = (IN_WIDTH // N_CHIP, D_MODEL // N_CHIP, D_MODEL // N_CHIP)


def _coords():
    return lax.axis_index("x"), lax.axis_index("y"), lax.axis_index("c")


def _peer(k):
    x, y, c = _coords()
    return (1 - x if k & 4 else x, 1 - y if k & 2 else y, 1 - c if k & 1 else c)


def _dev_index(p):
    return 4 * p[0] + 2 * p[1] + p[2]


def _rows(ref, start, size):
    return ref.at[pl.ds(pl.multiple_of(start, 16), size), :]


def _remote(src, dst, ssem, rsem, dev):
    return pltpu.make_async_remote_copy(src_ref=src, dst_ref=dst, send_sem=ssem, recv_sem=rsem, device_id=dev,
                                        device_id_type=MESH)


def _mod_exchange(c, w_ada_s, b4):
    ncol = w_ada_s.shape[1]

    def body(c_ref, w_ref, b_ref, cs_ref, mod_ref, modsh, modall, ssem, rsem):
        me = _dev_index(_coords())
        chip = me // 2
        cs_ref[me] = c_ref[...]
        sends = []
        for k in range(1, N_DEV):
            cp = _remote(c_ref, cs_ref.at[me], ssem.at[k - 1], rsem.at[k - 1], _peer(k))
            cp.start()
            sends.append(cp)
        for k in range(1, N_DEV):
            slot = cs_ref.at[_dev_index(_peer(k))]
            _remote(slot, slot, ssem.at[k - 1], rsem.at[k - 1], _peer(k)).wait_recv()
        cs = jnp.concatenate([cs_ref[d] for d in range(N_DEV)], axis=0)
        ms = _nn(cs * _sigmoid(cs), w_ref[...], precision=HIGHEST) + b_ref[pl.ds(chip, 1), :]
        modsh[...] = ms
        modall[chip] = ms
        for k2 in range(1, N_CHIP):
            cp = _remote(modsh, modall.at[chip], ssem.at[6 + k2], rsem.at[6 + k2], _peer(2 * k2))
            cp.start()
            sends.append(cp)
        for k2 in range(1, N_CHIP):
            slot = modall.at[_dev_index(_peer(2 * k2)) // 2]
            _remote(slot, slot, ssem.at[6 + k2], rsem.at[6 + k2], _peer(2 * k2)).wait_recv()
        for jj in range(N_CHIP):
            mod_ref[:, ncol * jj:ncol * (jj + 1)] = modall[jj, pl.ds(me, 1), :]
        for cp in sends:
            cp.wait_send()

    return pl.pallas_call(
        body, name="mod_exchange", in_specs=[VMEM_SPEC] * 3, out_specs=[VMEM_SPEC] * 2,
        out_shape=[jax.ShapeDtypeStruct((N_DEV, 1, D_MODEL), F32), jax.ShapeDtypeStruct((1, N_CHIP * ncol), F32)],
        scratch_shapes=[pltpu.VMEM((N_DEV, ncol), F32), pltpu.VMEM((N_CHIP, N_DEV, ncol), F32),
                        pltpu.SemaphoreType.DMA((10,)), pltpu.SemaphoreType.DMA((10,))],
        compiler_params=_cp())(c, w_ada_s, b4)


GATHER_CHUNK = 32


def _chunks(kinds, chunk):
    out = []
    for t, kind in enumerate(kinds):
        half = SHARD_ROWS[kind] // 2
        step = chunk if half % chunk == 0 else half
        out += [(t, off, step) for off in range(0, half, step)]
    return out


ATTN_CHUNK = 96


def _gather_attn(win_s):
    half = ATTN_COLS // 2
    offs = list(range(0, half, ATTN_CHUNK))
    nq = len(offs)
    rows = lambda ref, cc, off: ref.at[pl.ds(pl.multiple_of(cc * half + off, 16), ATTN_CHUNK), :]

    def body(src, out, lsem, ssem, rsem, fsem, gsem):
        x, y, c = _coords()
        chip = 2 * x + y
        sib = (x, y, 1 - c)

        @pl.when(chip == 0)
        def _():
            own = pltpu.make_async_copy(src.at[pl.ds(0, ATTN_COLS), :], out, lsem.at[0])
            own.start()
            sends = []
            for k2 in range(1, N_CHIP):
                for q, off in enumerate(offs):
                    cp = _remote(rows(src, c, off), rows(out, c, off), ssem.at[(k2 - 1) * nq + q], rsem.at[q], (k2 // 2, k2 % 2, c))
                    cp.start()
                    sends.append(cp)
            for cp in sends:
                cp.wait_send()
            own.wait()

        @pl.when(chip != 0)
        def _():
            passed = []
            for q, off in enumerate(offs):
                got = rows(out, c, off)
                _remote(got, got, ssem.at[q], rsem.at[q], (0, 0, c)).wait_recv()
                cp = _remote(got, got, fsem.at[q], gsem.at[q], sib)
                cp.start()
                passed.append(cp)
            for q, off in enumerate(offs):
                got = rows(out, 1 - c, off)
                _remote(got, got, fsem.at[q], gsem.at[q], sib).wait_recv()
            for cp in passed:
                cp.wait_send()

    dma = pltpu.SemaphoreType.DMA
    return pl.pallas_call(
        body, name="gather_attn", in_specs=[VMEM_SPEC], out_specs=HBM_SPEC,
        out_shape=jax.ShapeDtypeStruct((ATTN_COLS, win_s.shape[1]), win_s.dtype),
        scratch_shapes=[dma((1,)), dma((3 * nq,)), dma((nq,)), dma((nq,)), dma((nq,))],
        compiler_params=_cp())(win_s)


SEM_SPEC = pl.BlockSpec(memory_space=pltpu.SEMAPHORE)
HBM_ONLY = pl.BlockSpec(memory_space=pltpu.HBM)
DATAFLOW = pltpu.SideEffectType.DATAFLOW_SIDE_EFFECTING


def _place_own(shards):
    nt = len(shards)

    def body(*refs):
        srcs, outs, lsem = refs[:nt], refs[nt:2 * nt], refs[2 * nt]
        x, y, _ = _coords()
        chip = 2 * x + y
        local = [pltpu.make_async_copy(srcs[t], _rows(outs[t], chip * SHARD_ROWS[t], SHARD_ROWS[t]), lsem.at[t]) for t in range(nt)]
        for cp in local:
            cp.start()
        for cp in local:
            cp.wait()

    return pl.pallas_call(
        body, name="place_own_shard", in_specs=[VMEM_SPEC] * nt, out_specs=[HBM_SPEC] * nt,
        out_shape=[jax.ShapeDtypeStruct((N_CHIP * SHARD_ROWS[t], s.shape[1]), s.dtype) for t, s in enumerate(shards)],
        scratch_shapes=[pltpu.SemaphoreType.DMA((nt,))], compiler_params=_cp())(*shards)


def _gather_rest_start(shards, zones, dep):
    n = len(shards)

    def body(*refs):
        srcs, lands = refs[:n], refs[n:2 * n]
        ssem, rsem = refs[2 * n + 1], refs[2 * n + 2]
        token = refs[-1]
        x, y, _ = _coords()
        chip = 2 * x + y
        for k2 in range(1, N_CHIP):
            for t in range(n):
                q = (k2 - 1) * n + t
                _remote(srcs[t], _rows(lands[t], chip * SHARD_ROWS[t], SHARD_ROWS[t]), ssem.at[q], rsem.at[q], _peer(2 * k2)).start()
        token[...] = jnp.zeros_like(token)

    hbm = lambda a: pltpu.with_memory_space_constraint(a, pltpu.HBM)
    dma = pltpu.SemaphoreType.DMA
    outs = pl.pallas_call(
        body, name="gather_rest", in_specs=[HBM_ONLY] * (2 * n) + [HBM_SPEC],
        out_specs=[SEM_SPEC, SEM_SPEC] + [HBM_ONLY] * (2 * n) + [VMEM_SPEC],
        out_shape=[dma((3 * n,)), dma((3 * n,))] + [pltpu.HBM(a.shape, a.dtype) for a in list(shards) + list(zones)]
        + [jax.ShapeDtypeStruct((8, 128), F32)],
        input_output_aliases={i: 2 + i for i in range(2 * n)},
        compiler_params=pltpu.CompilerParams(has_side_effects=DATAFLOW))(*[hbm(a) for a in list(shards) + list(zones)], dep)
    return outs[0], outs[1], outs[2:2 + n], outs[2 + n:2 + 2 * n], outs[-1]


def _gather_rest_wait(started, after):
    ssem, rsem, shards, zones, _ = started
    n = len(shards)

    def body(*refs):
        srcs, lands = refs[:n], refs[n:2 * n]
        ssem_ref, rsem_ref = refs[2 * n], refs[2 * n + 1]
        for k2 in range(1, N_CHIP):
            pchip = _dev_index(_peer(2 * k2)) // 2
            for t in range(n):
                q = (k2 - 1) * n + t
                cp = _remote(srcs[t], _rows(lands[t], pchip * SHARD_ROWS[t], SHARD_ROWS[t]), ssem_ref.at[q], rsem_ref.at[q], _peer(2 * k2))
                cp.wait_send()
                cp.wait_recv()

    outs = pl.pallas_call(
        body, name="gather_rest_wait", in_specs=[HBM_ONLY] * (2 * n) + [SEM_SPEC, SEM_SPEC, HBM_SPEC], out_specs=[HBM_ONLY] * (2 * n),
        out_shape=[pltpu.HBM(a.shape, a.dtype) for a in list(shards) + list(zones)],
        input_output_aliases={i: i for i in range(2 * n)},
        compiler_params=pltpu.CompilerParams(has_side_effects=DATAFLOW))(*shards, *zones, ssem, rsem, after)
    return outs[n:]


def _reduced_by(ref, t, dev):
    return _rows(ref, (dev // 2) * SHARD_ROWS[t] + (dev % 2) * (SHARD_ROWS[t] // 2), SHARD_ROWS[t] // 2)


def _scatter_start(grads, kinds, name):
    n = len(grads)

    def body(*refs):
        srcs, lands = refs[:n], refs[n:2 * n]
        ssem, rsem = refs[2 * n], refs[2 * n + 1]
        token = refs[-1]
        me = _dev_index(_coords())
        for k in range(1, N_DEV):
            for i, t in enumerate(kinds):
                q = (k - 1) * n + i
                _remote(_reduced_by(srcs[i], t, _dev_index(_peer(k))), lands[i].at[me], ssem.at[q], rsem.at[q], _peer(k)).start()
        token[...] = jnp.zeros_like(token)

    zones = [lax.empty((N_DEV, SHARD_ROWS[t] // 2, g.shape[1]), g.dtype) for g, t in zip(grads, kinds)]
    hbm = lambda a: pltpu.with_memory_space_constraint(a, pltpu.HBM)
    dma = pltpu.SemaphoreType.DMA
    outs = pl.pallas_call(
        body, name=name, in_specs=[HBM_ONLY] * (2 * n), out_specs=[SEM_SPEC, SEM_SPEC] + [HBM_ONLY] * (2 * n) + [VMEM_SPEC],
        out_shape=[dma((7 * n,)), dma((7 * n,))] + [pltpu.HBM(a.shape, a.dtype) for a in list(grads) + zones]
        + [jax.ShapeDtypeStruct((8, 128), F32)],
        input_output_aliases={i: 2 + i for i in range(2 * n)},
        compiler_params=pltpu.CompilerParams(has_side_effects=DATAFLOW))(*[hbm(a) for a in list(grads) + zones])
    return outs[0], outs[1], outs[2:2 + n], outs[2 + n:2 + 2 * n], outs[-1]


def _scatter_wait(started, kinds, after, name):
    ssem, rsem, grads, zones, _ = started
    n = len(grads)

    def body(*refs):
        srcs, lands = refs[:n], refs[n:2 * n]
        ssem_ref, rsem_ref = refs[2 * n], refs[2 * n + 1]
        for k in range(1, N_DEV):
            peer = _dev_index(_peer(k))
            for i, t in enumerate(kinds):
                q = (k - 1) * n + i
                cp = _remote(_reduced_by(srcs[i], t, peer), lands[i].at[peer], ssem_ref.at[q], rsem_ref.at[q], _peer(k))
                cp.wait_send()
                cp.wait_recv()

    outs = pl.pallas_call(
        body, name=name, in_specs=[HBM_ONLY] * (2 * n) + [SEM_SPEC, SEM_SPEC, HBM_SPEC], out_specs=[HBM_ONLY] * (2 * n),
        out_shape=[pltpu.HBM(a.shape, a.dtype) for a in list(grads) + list(zones)],
        input_output_aliases={i: i for i in range(2 * n)},
        compiler_params=pltpu.CompilerParams(has_side_effects=DATAFLOW))(*grads, *zones, ssem, rsem, after)
    return outs[:n], outs[n:]


def _grad_finish(grads, zones, kinds, name):
    nt = len(grads)
    pieces = _chunks(kinds, GATHER_CHUNK)
    npc = len(pieces)
    shard = [SHARD_ROWS[k] for k in kinds]

    def body(*refs):
        srcs, lands, outs = refs[:nt], refs[nt:2 * nt], refs[2 * nt:3 * nt]
        slots, sums = refs[3 * nt:4 * nt], refs[4 * nt:5 * nt]
        lsem, osem, xsem, ysem = refs[5 * nt:]
        x, y, c = _coords()
        me = _dev_index((x, y, c))
        sib = (x, y, 1 - c)
        loads = [pltpu.make_async_copy(_reduced_by(srcs[t], kinds[t], me), slots[t].at[me], lsem.at[t]) for t in range(nt)]
        for k in range(1, N_DEV):
            peer = _dev_index(_peer(k))
            loads += [pltpu.make_async_copy(lands[t].at[peer], slots[t].at[peer], lsem.at[k * nt + t]) for t in range(nt)]
        for cp in loads:
            cp.start()
        for cp in loads:
            cp.wait()
        sends = []
        place = lambda t, cc, off, n: _rows(outs[t], cc * (shard[t] // 2) + off, n)
        stores = []
        for q, (t, off, n) in enumerate(pieces):
            r = pl.ds(off, n)
            acc = slots[t][0, r, :].astype(F32)
            for d in range(1, N_DEV):
                acc = acc + slots[t][d, r, :].astype(F32)
            sums[t][r, :] = acc
            keep = pltpu.make_async_copy(sums[t].at[r, :], place(t, c, off, n), osem.at[q])
            give = _remote(sums[t].at[r, :], place(t, c, off, n), xsem.at[q], ysem.at[q], sib)
            keep.start()
            give.start()
            stores.append(keep)
            sends.append(give)
        for q, (t, off, n) in enumerate(pieces):
            got = place(t, 1 - c, off, n)
            _remote(got, got, xsem.at[q], ysem.at[q], sib).wait_recv()
        for cp in sends:
            cp.wait_send()
        for cp in stores:
            cp.wait()

    dma = pltpu.SemaphoreType.DMA
    return pl.pallas_call(
        body, name=name, in_specs=[HBM_SPEC] * (2 * nt), out_specs=[HBM_SPEC] * nt,
        out_shape=[jax.ShapeDtypeStruct((shard[t], g.shape[1]), F32) for t, g in enumerate(grads)],
        scratch_shapes=([pltpu.VMEM((N_DEV, shard[t] // 2, g.shape[1]), g.dtype) for t, g in enumerate(grads)]
                        + [pltpu.VMEM((shard[t] // 2, g.shape[1]), F32) for t, g in enumerate(grads)]
                        + [dma((N_DEV * nt,)), dma((npc,)), dma((npc,)), dma((npc,))]),
        compiler_params=_cp())(*grads, *zones)


def _adam_math(w, g, m, v):
    m = ADAM_B1 * m + (1.0 - ADAM_B1) * g
    v = ADAM_B2 * v + (1.0 - ADAM_B2) * (g * g)
    m_hat = m / (1.0 - ADAM_B1 ** ADAM_STEP)
    v_hat = v / (1.0 - ADAM_B2 ** ADAM_STEP)
    return -ADAM_LR * (m_hat / (jnp.sqrt(v_hat) + ADAM_EPS) + ADAM_WD * w), m, v


def _adamw(w, g, m, v, rb, name):
    rows, cols = w.shape

    def body(w_ref, g_ref, m_ref, v_ref, d_ref, nm_ref, nv_ref):
        d_ref[...], nm_ref[...], nv_ref[...] = _adam_math(w_ref[...], g_ref[...], m_ref[...], v_ref[...])

    spec = pl.BlockSpec((rb, cols), lambda i: (i, 0))
    return pl.pallas_call(body, grid=(rows // rb,), name=name, in_specs=[spec] * 4, out_specs=[spec] * 3,
                          out_shape=[jax.ShapeDtypeStruct(w.shape, F32)] * 3, compiler_params=_cp("parallel"))(w, g, m, v)


PACK = (("b_ada", 3 * D_MODEL), ("g_pre", D_MODEL), ("g_post", D_MODEL), ("gn_g", 512), ("qn_g", 64), ("kn_g", 64),
        ("w_dec_f", 4), ("w_dec_b", 4), ("loss", 1))
PACK_OFFSETS = {}
PACK_WIDTH = 0
for _name, _n in PACK:
    PACK_OFFSETS[_name] = PACK_WIDTH
    PACK_WIDTH += -(-_n // 128) * 128
SMALL_PARAMS = tuple(name for name, _ in PACK if name != "loss")


def _pack(parts):
    cols = []
    for name, n in PACK:
        pad = -(-n // 128) * 128 - n
        cols.append(parts[name].reshape(1, n).astype(F32))
        if pad:
            cols.append(jnp.zeros((1, pad), F32))
    return jnp.concatenate(cols, axis=1)


def _small_reduce(vec, cs, dep):
    ncol = 3 * D_MODEL // N_CHIP

    def body(vec_ref, cs_ref, dep_ref, tot_ref, gwa_ref, gat, ssem, rsem):
        me = _dev_index(_coords())
        chip = me // 2
        gat[me] = vec_ref[...]
        sends = []
        for k in range(1, N_DEV):
            cp = _remote(vec_ref, gat.at[me], ssem.at[k - 1], rsem.at[k - 1], _peer(k))
            cp.start()
            sends.append(cp)
        for k in range(1, N_DEV):
            slot = gat.at[_dev_index(_peer(k))]
            _remote(slot, slot, ssem.at[k - 1], rsem.at[k - 1], _peer(k)).wait_recv()
        rows = [gat[d] for d in range(N_DEV)]
        tot = rows[0]
        for d in range(1, N_DEV):
            tot = tot + rows[d]
        tot_ref[...] = tot
        cs = cs_ref[...]
        ca_t = jnp.transpose(jnp.concatenate([cs * _sigmoid(cs), jnp.zeros((128 - N_DEV, D_MODEL), F32)], axis=0))
        dm = jnp.concatenate(rows + [jnp.zeros((128 - N_DEV, PACK_WIDTH), F32)], axis=0)
        for jj in range(N_CHIP):
            @pl.when(chip == jj)
            def _():
                gwa_ref[...] = _nn(ca_t, dm[:, ncol * jj:ncol * (jj + 1)], precision=HIGHEST)
        for cp in sends:
            cp.wait_send()

    return pl.pallas_call(
        body, name="small_reduce", in_specs=[VMEM_SPEC, VMEM_SPEC, HBM_SPEC], out_specs=[VMEM_SPEC] * 2,
        out_shape=[jax.ShapeDtypeStruct((1, PACK_WIDTH), F32), jax.ShapeDtypeStruct((D_MODEL, ncol), F32)],
        scratch_shapes=[pltpu.VMEM((N_DEV, 1, PACK_WIDTH), F32), pltpu.SemaphoreType.DMA((7,)), pltpu.SemaphoreType.DMA((7,))],
        compiler_params=_cp())(vec, cs, dep)


def _small_adamw(tot, given):
    sizes = dict(PACK)
    np_ = len(SMALL_PARAMS)

    def body(*refs):
        tot_ref = refs[0]
        wmv = refs[1:1 + 3 * np_]
        outs = refs[1 + 3 * np_:1 + 7 * np_]
        loss_ref = refs[-1]
        for i, name in enumerate(SMALL_PARAMS):
            off, n = PACK_OFFSETS[name], sizes[name]
            g = tot_ref[:, off:off + n]
            w_ref, m_ref, v_ref = wmv[3 * i:3 * i + 3]
            outs[i][...] = g
            outs[np_ + i][...], outs[2 * np_ + i][...], outs[3 * np_ + i][...] = _adam_math(w_ref[...], g, m_ref[...], v_ref[...])
        loss_ref[...] = tot_ref[:, PACK_OFFSETS["loss"]:PACK_OFFSETS["loss"] + 1]

    flat = [a for name in SMALL_PARAMS for a in given[name]]
    shapes = [jax.ShapeDtypeStruct((1, sizes[name]), F32) for name in SMALL_PARAMS]
    res = pl.pallas_call(body, name="small_adamw", in_specs=[VMEM_SPEC] * (1 + 3 * np_), out_specs=[VMEM_SPEC] * (4 * np_ + 1),
                         out_shape=shapes * 4 + [jax.ShapeDtypeStruct((1, 1), F32)], compiler_params=_cp())(tot, *flat)
    return [dict(zip(SMALL_PARAMS, res[k * np_:(k + 1) * np_])) for k in range(4)], res[-1]


def kernel(x, c, w_ada, b_ada, g_pre, w_in, qn_g, kn_g, w_dec_f, w_dec_b, gn_g, w_pa, w_pr, w_out, g_post, loss_target, m_w_ada, m_b_ada, m_g_pre, m_w_in, m_qn_g, m_kn_g, m_w_dec_f, m_w_dec_b, m_gn_g, m_w_pa, m_w_pr, m_w_out, m_g_post, v_w_ada, v_b_ada, v_g_pre, v_w_in, v_qn_g, v_kn_g, v_w_dec_f, v_w_dec_b, v_gn_g, v_w_pa, v_w_pr, v_w_out, v_g_post):
    shards = (w_in[0].T.astype(MXU_DTYPE), jnp.concatenate([w_pa[0].T, w_pr[0].T], axis=1).astype(MXU_DTYPE),
              w_out[0].astype(MXU_DTYPE))
    cs, mod = _mod_exchange(c, w_ada[0], b_ada.reshape(N_CHIP, 3 * D_MODEL // N_CHIP))
    w_attn = _gather_attn(shards[0])
    started = {}

    def rest_start(dep):
        started["rest"] = _gather_rest_start(shards, _place_own(shards), dep)
        return started["rest"][-1]

    def rest_wait(after):
        return _gather_rest_wait(started["rest"], after)

    kinds = {"early": (1, 2), "late": (0,)}

    def send(name, arrays):
        started[name] = _scatter_start(arrays, kinds[name], "grad_scatter_" + name)
        return started[name][-1]

    grad_x, _, _, _, small = _local_step(x[0], loss_target[0], mod, g_pre, qn_g, kn_g, w_dec_f, w_dec_b,
                                         gn_g, g_post, w_attn, rest_start, rest_wait, send=send)
    small = dict(small)
    small["b_ada"] = small.pop("dmod")
    given = dict(b_ada=(b_ada, m_b_ada, v_b_ada), g_pre=(g_pre, m_g_pre, v_g_pre), g_post=(g_post, m_g_post, v_g_post),
                 gn_g=(gn_g, m_gn_g, v_gn_g), qn_g=(qn_g, m_qn_g, v_qn_g), kn_g=(kn_g, m_kn_g, v_kn_g),
                 w_dec_f=(w_dec_f, m_w_dec_f, v_w_dec_f), w_dec_b=(w_dec_b, m_w_dec_b, v_w_dec_b))
    grads, deltas, new_m, new_v = {}, {}, {}, {}

    def update(name, w, g, m, v, back):
        d, nm, nv = _adamw(w, g, m, v, w.shape[0] // 4, "adamw_" + name)
        grads[name], deltas[name], new_m[name], new_v[name] = back(g), back(d), back(nm), back(nv)
        return d

    lead = lambda a: a[None]
    (g_pt, g_out), (z_pt, z_out) = _scatter_wait(started["early"], kinds["early"], grad_x, "grad_scatter_early_wait")
    r_pt, r_out = _grad_finish((g_pt, g_out), (z_pt, z_out), kinds["early"], "grad_finish_early")
    update("w_pa", w_pa[0], r_pt[:, :512].T, m_w_pa[0], v_w_pa[0], lead)
    update("w_pr", w_pr[0], r_pt[:, 512:].T, m_w_pr[0], v_w_pr[0], lead)
    last = update("w_out", w_out[0], r_out, m_w_out[0], v_w_out[0], lead)
    tot, g_w_ada = _small_reduce(_pack(small), cs.reshape(N_DEV, D_MODEL), last)
    small_parts, loss = _small_adamw(tot, given)
    for dst, part in zip((grads, deltas, new_m, new_v), small_parts):
        dst.update(part)
    last = update("w_ada", w_ada[0], g_w_ada, m_w_ada[0], v_w_ada[0], lead)

    (g_in_t,), (z_in,) = _scatter_wait(started["late"], kinds["late"], last, "grad_scatter_late_wait")
    (r_in,) = _grad_finish((g_in_t,), (z_in,), kinds["late"], "grad_finish_late")
    tr = lambda a: a[0].T
    update("w_in", tr(w_in), r_in, tr(m_w_in), tr(v_w_in), lambda a: a.T[None])
    order = ("w_ada", "b_ada", "g_pre", "w_in", "qn_g", "kn_g", "w_dec_f", "w_dec_b", "gn_g", "w_pa", "w_pr", "w_out", "g_post")
    return (loss[0, 0], grad_x[None], *[grads[n] for n in order], *[deltas[n] for n in order],
            *[new_m[n] for n in order], *[new_v[n] for n in order])
```

```python
import functools

import jax
import jax.numpy as jnp
import numpy as np
from jax import lax
from jax.experimental import pallas as pl
from jax.experimental.pallas import tpu as pltpu

F32 = jnp.float32
BF16 = jnp.bfloat16
MXU_DTYPE = jnp.bfloat16

D_MODEL = 1024
GRID_W = 64
HEAD_DIM = 64
ROPE_THETA = 10000.0
EPS = 1e-6
RET_HEADS = 4
RET_CHUNK = 256
IN_WIDTH = 4864
QA, KA, VA, ZA, QR, KR, VR, ZR, GL = 0, 512, 640, 768, 1280, 1536, 1792, 2304, 2816
ATTN_COLS = ZA
ZA_B, QR_B, KR_B, VR_B, ZR_B, GL_B = (c - ATTN_COLS for c in (ZA, QR, KR, VR, ZR, GL))

ADAM_LR, ADAM_B1, ADAM_B2, ADAM_EPS, ADAM_WD, ADAM_STEP = 0.001, 0.9, 0.999, 1e-08, 0.01, 10

VMEM_LIMIT = 56 * 1024 * 1024
MESH = pl.DeviceIdType.MESH
HIGHEST = lax.Precision.HIGHEST
LOG2E = 1.4426950408889634
LN2 = 0.6931471805599453


def _cp(*sem, **kw):
    if sem:
        kw["dimension_semantics"] = sem
    return pltpu.CompilerParams(vmem_limit_bytes=VMEM_LIMIT, **kw)


def _nn(a, b, **kw):
    return lax.dot_general(a, b, (((1,), (0,)), ((), ())), preferred_element_type=F32, **kw)


def _nt(a, b):
    return lax.dot_general(a, b, (((1,), (1,)), ((), ())), preferred_element_type=F32)


def _tn(a, b):
    return lax.dot_general(a, b, (((0,), (0,)), ((), ())), preferred_element_type=F32)


def _mx(x):
    return x.astype(MXU_DTYPE)


def _lane(shape):
    return lax.broadcasted_iota(jnp.int32, shape, 1)


def _sigmoid(z):
    return 1.0 / (1.0 + jnp.exp(-z))


def _rowsum128(x):
    s = jnp.sum(x, axis=0, keepdims=True)
    out = s[:, 0:128]
    for k in range(1, x.shape[1] // 128):
        out = out + s[:, 128 * k:128 * (k + 1)]
    return out


def _swap16(x):
    return jnp.where((_lane(x.shape) & 16) == 0, pltpu.roll(x, 112, 1), pltpu.roll(x, 16, 1))


def _rope(x, cos, sin):
    return x * cos + _swap16(x) * sin


def _rope_t(d, cos, sin):
    return d * cos + _swap16(d * sin)


def _blockdiag64():
    r = lax.broadcasted_iota(jnp.int32, (128, 128), 0) // 64
    c = lax.broadcasted_iota(jnp.int32, (128, 128), 1) // 64
    return (r == c).astype(BF16)


def _seg64(x, m):
    hi = x.astype(BF16)
    lo = (x - hi.astype(F32)).astype(BF16)
    return _nn(hi, m) + _nn(lo, m)


def _rope_tables(seq):
    t = np.arange(seq)
    row = (t // GRID_W).astype(np.float32)
    col = (t % GRID_W).astype(np.float32)
    half = HEAD_DIM // 2
    inv_freq = (np.float32(ROPE_THETA) ** (-np.arange(0, half, 2, dtype=np.float32) / np.float32(half))).astype(np.float32)
    ar = (row[:, None] * inv_freq[None, :]).astype(np.float32).astype(np.float64)
    ac = (col[:, None] * inv_freq[None, :]).astype(np.float32).astype(np.float64)
    cr, sr, cc, sc = np.cos(ar), np.sin(ar), np.cos(ac), np.sin(ac)
    cos64 = np.concatenate([cr, cr, cc, cc], axis=1)
    sin64 = np.concatenate([-sr, sr, -sc, sc], axis=1)
    return jnp.asarray(np.tile(cos64, (1, 2)), F32), jnp.asarray(np.tile(sin64, (1, 2)), F32)


def _prenorm(x, mod, g_pre):
    seq = x.shape[0]
    bs = 512

    def body(x_ref, mod_ref, g_ref, h_ref):
        xv = x_ref[...]
        r = lax.rsqrt(jnp.mean(xv * xv, axis=-1, keepdims=True) + EPS)
        shift = mod_ref[:, 0:D_MODEL]
        scale = mod_ref[:, D_MODEL:2 * D_MODEL]
        h_ref[...] = ((xv * r) * g_ref[...] * (1.0 + scale) + shift).astype(h_ref.dtype)

    return pl.pallas_call(
        body, grid=(seq // bs,), name="prenorm",
        in_specs=[pl.BlockSpec((bs, D_MODEL), lambda i: (i, 0)), pl.BlockSpec((1, 3 * D_MODEL), lambda i: (0, 0)),
                  pl.BlockSpec((1, D_MODEL), lambda i: (0, 0))],
        out_specs=pl.BlockSpec((bs, D_MODEL), lambda i: (i, 0)),
        out_shape=jax.ShapeDtypeStruct((seq, D_MODEL), MXU_DTYPE), compiler_params=_cp("parallel"))(x, mod, g_pre)


def _prenorm_bwd(x, dh, dout, mod, g_pre):
    seq = x.shape[0]
    bs = 512

    def body(x_ref, dh_ref, dout_ref, mod_ref, g_ref, gx_ref, dshift_ref, dscale_ref, dg_ref):
        @pl.when(pl.program_id(0) == 0)
        def _():
            dshift_ref[...] = jnp.zeros_like(dshift_ref)
            dscale_ref[...] = jnp.zeros_like(dscale_ref)
            dg_ref[...] = jnp.zeros_like(dg_ref)

        xv = x_ref[...]
        dh = dh_ref[...]
        g = g_ref[...]
        r = lax.rsqrt(jnp.mean(xv * xv, axis=-1, keepdims=True) + EPS)
        xn = xv * r
        scale = mod_ref[:, D_MODEL:2 * D_MODEL]
        dshift_ref[...] += jnp.sum(dh, axis=0, keepdims=True)
        dscale_ref[...] += jnp.sum(dh * (xn * g), axis=0, keepdims=True)
        da = dh * (1.0 + scale)
        dg_ref[...] += jnp.sum(da * xn, axis=0, keepdims=True)
        dxn = da * g
        dx = r * (dxn - xn * jnp.mean(dxn * xn, axis=-1, keepdims=True))
        gx_ref[...] = dout_ref[...] + dx

    row = pl.BlockSpec((bs, D_MODEL), lambda i: (i, 0))
    vec = pl.BlockSpec((1, D_MODEL), lambda i: (0, 0))
    return pl.pallas_call(
        body, grid=(seq // bs,), name="prenorm_bwd",
        in_specs=[row, row, row, pl.BlockSpec((1, 3 * D_MODEL), lambda i: (0, 0)), vec],
        out_specs=[row, vec, vec, vec],
        out_shape=[jax.ShapeDtypeStruct((seq, D_MODEL), F32)] + [jax.ShapeDtypeStruct((1, D_MODEL), F32)] * 3,
        compiler_params=_cp("arbitrary"))(x, dh, dout, mod, g_pre)


def _dep_args(dep, grid_rank):
    if dep is None:
        return [], []
    return [dep], [pl.BlockSpec(dep.shape, lambda *_: (0,) * dep.ndim)]


def _matmul(a, b, *, ta, tb, tm, tn, out_dtype, name, dep=None, b_rows=None):
    kdim = a.shape[0] if ta else a.shape[1]
    m = a.shape[1] if ta else a.shape[0]
    n = b.shape[0] if tb else b.shape[1]
    if b_rows is not None:
        assert tb
        n = b_rows[1]
    assert m % tm == 0 and n % tn == 0
    deps, dep_specs = _dep_args(dep, 2)

    def body(a_ref, b_ref, *rest):
        o_ref = rest[-1]
        dims = (((0 if ta else 1,), (1 if tb else 0,)), ((), ()))
        o_ref[...] = lax.dot_general(a_ref[...], b_ref[...], dims, preferred_element_type=F32).astype(o_ref.dtype)

    a_spec = pl.BlockSpec((kdim, tm), lambda j, i: (0, i)) if ta else pl.BlockSpec((tm, kdim), lambda j, i: (i, 0))
    b_spec = pl.BlockSpec((tn, kdim), lambda j, i: (j, 0)) if tb else pl.BlockSpec((kdim, tn), lambda j, i: (0, j))
    if b_rows is not None:
        assert b_rows[0] % 128 == 0 and tn % 128 == 0
        b_spec = pl.BlockSpec((pl.Element(tn), pl.Element(kdim)), lambda j, i: (pl.multiple_of(b_rows[0] + j * tn, 128), 0))
    return pl.pallas_call(
        body, grid=(n // tn, m // tm), name=name, in_specs=[a_spec, b_spec] + dep_specs,
        out_specs=pl.BlockSpec((tm, tn), lambda j, i: (i, j)),
        out_shape=jax.ShapeDtypeStruct((m, n), out_dtype), compiler_params=_cp("parallel", "parallel"))(a, b, *deps)


def _prep_attn(p_a, cos, sin, qg2, kg2):
    seq = p_a.shape[0]
    bs = 512

    def body(p_ref, cos_ref, sin_ref, qg_ref, kg_ref, qt_ref, kd_ref, vd_ref):
        m = _blockdiag64()
        cs, sn = cos_ref[...], sin_ref[...]
        lo = _lane((bs, 128)) < 64
        for pr in range(4):
            q = p_ref[:, QA + 128 * pr:QA + 128 * (pr + 1)]
            r = lax.rsqrt(_seg64(q * q, m) * (1.0 / 64) + EPS)
            qt_ref[:, 128 * pr:128 * (pr + 1)] = (_rope(q * r * qg_ref[...], cs, sn) * (0.125 * LOG2E)).astype(qt_ref.dtype)
        k = p_ref[:, KA:KA + 128]
        r = lax.rsqrt(_seg64(k * k, m) * (1.0 / 64) + EPS)
        kr = _rope(k * r * kg_ref[...], cs, sn)
        ksw = pltpu.roll(kr, 64, 1)
        kd_ref[0] = jnp.where(lo, kr, ksw).astype(kd_ref.dtype)
        kd_ref[1] = jnp.where(lo, ksw, kr).astype(kd_ref.dtype)
        v = p_ref[:, VA:VA + 128]
        vsw = pltpu.roll(v, 64, 1)
        vd_ref[0] = jnp.where(lo, v, vsw).astype(vd_ref.dtype)
        vd_ref[1] = jnp.where(lo, vsw, v).astype(vd_ref.dtype)

    tab = pl.BlockSpec((bs, 128), lambda i: (i, 0))
    gsp = pl.BlockSpec((1, 128), lambda i: (0, 0))
    dup = pl.BlockSpec((2, bs, 128), lambda i: (0, i, 0))
    return pl.pallas_call(
        body, grid=(seq // bs,), name="attn_prep",
        in_specs=[pl.BlockSpec((bs, ATTN_COLS), lambda i: (i, 0)), tab, tab, gsp, gsp],
        out_specs=[pl.BlockSpec((bs, 512), lambda i: (i, 0)), dup, dup],
        out_shape=[jax.ShapeDtypeStruct((seq, 512), MXU_DTYPE), jax.ShapeDtypeStruct((2, seq, 128), MXU_DTYPE),
                   jax.ShapeDtypeStruct((2, seq, 128), MXU_DTYPE)],
        compiler_params=_cp("parallel"))(p_a, cos, sin, qg2, kg2)


def _prep_ret(p_b, cos, sin):
    seq = p_b.shape[0]
    bs = 512

    def body(p_ref, cos_ref, sin_ref, rq_ref, rk_ref, rv_ref):
        cs, sn = cos_ref[...], sin_ref[...]
        for pr in range(2):
            sl = slice(128 * pr, 128 * (pr + 1))
            rq_ref[:, sl] = _rope(p_ref[:, QR_B + 128 * pr:QR_B + 128 * (pr + 1)], cs, sn).astype(rq_ref.dtype)
            rk_ref[:, sl] = (_rope(p_ref[:, KR_B + 128 * pr:KR_B + 128 * (pr + 1)], cs, sn) * 0.125).astype(rk_ref.dtype)
        rv_ref[...] = p_ref[:, VR_B:VR_B + 512].astype(rv_ref.dtype)

    tab = pl.BlockSpec((bs, 128), lambda i: (i, 0))
    return pl.pallas_call(
        body, grid=(seq // bs,), name="ret_prep",
        in_specs=[pl.BlockSpec((bs, ZR_B), lambda i: (i, 0)), tab, tab],
        out_specs=[pl.BlockSpec((bs, 256), lambda i: (i, 0)), pl.BlockSpec((bs, 256), lambda i: (i, 0)),
                   pl.BlockSpec((bs, 512), lambda i: (i, 0))],
        out_shape=[jax.ShapeDtypeStruct((seq, 256), MXU_DTYPE), jax.ShapeDtypeStruct((seq, 256), MXU_DTYPE),
                   jax.ShapeDtypeStruct((seq, 512), MXU_DTYPE)],
        compiler_params=_cp("parallel"))(p_b, cos, sin)


def _halves(kd):
    lo = _lane(kd.shape) < 64
    z = jnp.zeros_like(kd)
    return jnp.concatenate([jnp.where(lo, kd, z), jnp.where(lo, z, kd)], axis=0)


def _attn_fwd(qt, kd, vd, dep=None):
    seq = qt.shape[0]
    bq, bk = 512, min(1024, seq)
    nk = seq // bk
    deps, dep_specs = _dep_args(dep, 2)

    def body(q_ref, k_ref, v_ref, *rest):
        o_ref, lse_ref, m_scr, l_scr, acc = rest[-5:]
        j = pl.program_id(1)
        m_scr[...] = jnp.full_like(m_scr, -jnp.inf)
        l_scr[...] = jnp.zeros_like(l_scr)
        acc[...] = jnp.zeros_like(acc)
        lo = _lane((bq, 128)) < 64

        def kv_block(n, carry):
            rows = pl.ds(pl.multiple_of(n * bk, bk), bk)
            k2, v2 = _halves(k_ref[rows, :]), _halves(v_ref[rows, :])
            for pr in range(2):
                s2 = _nt(q_ref[:, 128 * pr:128 * (pr + 1)], k2)
                ps, alphas = [], []
                for e in range(2):
                    g = 2 * pr + e
                    s = s2[:, e * bk:(e + 1) * bk]
                    m_prev = m_scr[g]
                    m_next = jnp.maximum(m_prev, jnp.max(s, axis=1, keepdims=True))
                    alpha = jnp.exp2(m_prev - m_next)
                    pe = jnp.exp2(s - jnp.tile(m_next, (1, bk // 128)))
                    l_scr[g] = alpha * l_scr[g] + jnp.sum(pe, axis=1, keepdims=True)
                    m_scr[g] = m_next
                    ps.append(_mx(pe))
                    alphas.append(alpha)
                o2 = _nn(jnp.concatenate(ps, axis=1), v2)
                sl = slice(128 * pr, 128 * (pr + 1))
                acc[:, sl] = acc[:, sl] * jnp.where(lo, alphas[0], alphas[1]) + o2
            return carry

        lax.fori_loop(0, nk, kv_block, 0)
        for pr in range(2):
            sl = slice(128 * pr, 128 * (pr + 1))
            o_ref[:, sl] = acc[:, sl] / jnp.where(lo, l_scr[2 * pr], l_scr[2 * pr + 1])
        for g in range(4):
            lse = jnp.transpose(m_scr[g] + jnp.log2(l_scr[g]))
            lse_ref[pl.ds(4 * j + g, 1), :] = lse[0:1, :]

    return pl.pallas_call(
        body, grid=(seq // bq, 2), name="attn_fwd",
        in_specs=[pl.BlockSpec((bq, 256), lambda i, j: (i, j)), pl.BlockSpec((None, seq, 128), lambda i, j: (j, 0, 0)),
                  pl.BlockSpec((None, seq, 128), lambda i, j: (j, 0, 0))] + dep_specs,
        out_specs=[pl.BlockSpec((bq, 256), lambda i, j: (i, j)), pl.BlockSpec((8, bq), lambda i, j: (0, i))],
        out_shape=[jax.ShapeDtypeStruct((seq, 512), F32), jax.ShapeDtypeStruct((8, seq), F32)],
        scratch_shapes=[pltpu.VMEM((4, bq, 128), F32), pltpu.VMEM((4, bq, 128), F32), pltpu.VMEM((bq, 256), F32)],
        compiler_params=_cp("parallel", "arbitrary"))(qt, kd, vd, *deps)


def _attn_bwd(qt, kd, vd, do, lse, delta, dep=None):
    seq = qt.shape[0]
    bq, bk = min(1024, seq), 512
    nq, nk = seq // bq, seq // bk
    deps, dep_specs = _dep_args(dep, 3)

    def body(q_ref, do_ref, k_ref, v_ref, lse_ref, del_ref, *rest):
        dq_ref, dk_ref, dv_ref = rest[-3:]
        j, i = pl.program_id(0), pl.program_id(1)
        dq_ref[...] = jnp.zeros_like(dq_ref)

        @pl.when(i == 0)
        def _():
            dk_ref[...] = jnp.zeros_like(dk_ref)
            dv_ref[...] = jnp.zeros_like(dv_ref)

        lo = _lane((bk, 128)) < 64
        big = lambda r: jnp.concatenate([jnp.broadcast_to(r[0], (bk, bq)), jnp.broadcast_to(r[1], (bk, bq))], axis=0)

        def kv_block(n, carry):
            rows = pl.ds(pl.multiple_of(n * bk, bk), bk)
            k2, v2 = _halves(k_ref[rows, :]), _halves(v_ref[rows, :])
            for pr in range(2):
                sl = slice(128 * pr, 128 * (pr + 1))
                qp, dop = q_ref[:, sl], do_ref[:, sl]
                s_t = _nt(k2, qp)
                dp_t = _nt(v2, dop)
                ls = [lse_ref[pl.ds(4 * j + 2 * pr + e, 1), :] for e in range(2)]
                dl = [del_ref[pl.ds(4 * j + 2 * pr + e, 1), :] for e in range(2)]
                p_t = jnp.exp2(s_t - big(ls))
                ds_t = _mx(p_t * (dp_t - big(dl)))
                rv = _nn(_mx(p_t), dop)
                rk = _nn(ds_t, qp) * LN2
                dv_ref[rows, :] += jnp.where(lo, rv[:bk], rv[bk:])
                dk_ref[rows, :] += jnp.where(lo, rk[:bk], rk[bk:])
                dq_ref[:, sl] += _tn(ds_t, k2)
            return carry

        lax.fori_loop(0, nk, kv_block, 0)

    return pl.pallas_call(
        body, grid=(2, nq), name="attn_bwd",
        in_specs=[pl.BlockSpec((bq, 256), lambda j, i: (i, j)), pl.BlockSpec((bq, 256), lambda j, i: (i, j)),
                  pl.BlockSpec((None, seq, 128), lambda j, i: (j, 0, 0)), pl.BlockSpec((None, seq, 128), lambda j, i: (j, 0, 0)),
                  pl.BlockSpec((8, bq), lambda j, i: (0, i)), pl.BlockSpec((8, bq), lambda j, i: (0, i))] + dep_specs,
        out_specs=[pl.BlockSpec((bq, 256), lambda j, i: (i, j)), pl.BlockSpec((None, seq, 128), lambda j, i: (j, 0, 0)),
                   pl.BlockSpec((None, seq, 128), lambda j, i: (j, 0, 0))],
        out_shape=[jax.ShapeDtypeStruct((seq, 512), F32), jax.ShapeDtypeStruct((2, seq, 128), F32),
                   jax.ShapeDtypeStruct((2, seq, 128), F32)],
        compiler_params=_cp("arbitrary", "arbitrary"))(qt, do, kd, vd, lse, delta, *deps)


def _ret_tables(lg_f, lg_b, c):
    idx = jnp.arange(c, dtype=F32)
    diff = idx[:, None] - idx[None, :]
    a, b = lg_f[:, None, None], lg_b[:, None, None]
    low, up = (diff >= 0)[None], (diff < 0)[None]
    dmat = jnp.where(low, jnp.exp(a * jnp.maximum(diff, 0.0)[None]), jnp.exp(b * jnp.maximum(-diff, 0.0)[None]))
    dda = jnp.where(low, diff[None] * jnp.exp(a * jnp.maximum(diff, 0.0)[None]), 0.0)
    ddb = jnp.where(up, -diff[None] * jnp.exp(b * jnp.maximum(-diff, 0.0)[None]), 0.0)
    rep = lambda w: jnp.broadcast_to(w[:, :, None], (RET_HEADS, c, 128))
    wqf = rep(jnp.exp(lg_f[:, None] * (idx + 1.0)[None]))
    wqb = rep(jnp.exp(lg_b[:, None] * (c - idx)[None]))
    wkf = rep(jnp.exp(lg_f[:, None] * (c - 1.0 - idx)[None]))
    wkb = rep(jnp.exp(lg_b[:, None] * idx[None]))
    cdf, cdb = jnp.exp(lg_f * c), jnp.exp(lg_b * c)
    dec_fb = jnp.broadcast_to(jnp.concatenate([cdf, cdb])[:, None], (8, 128))
    dec_bf = jnp.broadcast_to(jnp.concatenate([cdb, cdf])[:, None], (8, 128))
    return dict(dmat=dmat, dda=dda, ddb=ddb, wqf=wqf, wqb=wqb, wkf=wkf, wkb=wkb, dec_fb=dec_fb, dec_bf=dec_bf)


def _ret_states(xk, yv, w_fwd, w_rev, dec, name):
    seq = xk.shape[0]
    c = RET_CHUNK
    nc = seq // c

    def body(xf_ref, yf_ref, xr_ref, yr_ref, wf_ref, wr_ref, dec_ref, sf_ref, sr_ref, st_f, st_r):
        @pl.when(pl.program_id(0) == 0)
        def _():
            st_f[...] = jnp.zeros_like(st_f)
            st_r[...] = jnp.zeros_like(st_r)

        lane = _lane((c, 128))
        for h in range(RET_HEADS):
            pr, e = h // 2, h % 2
            half = (lane < 64) if e == 0 else (lane >= 64)
            for x_ref, y_ref, w_ref, s_ref, st, drow in ((xf_ref, yf_ref, wf_ref, sf_ref, st_f, h),
                                                        (xr_ref, yr_ref, wr_ref, sr_ref, st_r, 4 + h)):
                s_ref[h] = st[h].astype(s_ref.dtype)
                xm = jnp.where(half, x_ref[:, 128 * pr:128 * (pr + 1)].astype(F32) * w_ref[h], 0.0)
                st[h] = st[h] * dec_ref[drow:drow + 1, :] + _tn(_mx(xm), _mx(y_ref[:, 128 * h:128 * (h + 1)]))

    xs = lambda f: pl.BlockSpec((c, 256), f)
    ys = lambda f: pl.BlockSpec((c, 512), f)
    fwd, rev = (lambda n: (n, 0)), (lambda n: (nc - 1 - n, 0))
    wsp = pl.BlockSpec((RET_HEADS, c, 128), lambda n: (0, 0, 0))
    return pl.pallas_call(
        body, grid=(nc,), name=name,
        in_specs=[xs(fwd), ys(fwd), xs(rev), ys(rev), wsp, wsp, pl.BlockSpec((8, 128), lambda n: (0, 0))],
        out_specs=[pl.BlockSpec((None, RET_HEADS, 128, 128), lambda n: (n, 0, 0, 0)),
                   pl.BlockSpec((None, RET_HEADS, 128, 128), lambda n: (nc - 1 - n, 0, 0, 0))],
        out_shape=[jax.ShapeDtypeStruct((nc, RET_HEADS, 128, 128), MXU_DTYPE)] * 2,
        scratch_shapes=[pltpu.VMEM((RET_HEADS, 128, 128), F32), pltpu.VMEM((RET_HEADS, 128, 128), F32)],
        compiler_params=_cp("arbitrary"))(xk, yv, xk, yv, w_fwd, w_rev, dec)


def _ret_fwd(rq, rk, rv, rf, rb, tb):
    seq = rq.shape[0]
    c = RET_CHUNK

    def body(q_ref, k_ref, v_ref, rf_ref, rb_ref, d_ref, wqf_ref, wqb_ref, o_ref):
        lane = _lane((c, 128))
        for h in range(RET_HEADS):
            pr, e = h // 2, h % 2
            half = (lane < 64) if e == 0 else (lane >= 64)
            sl = slice(128 * pr, 128 * (pr + 1))
            qp = q_ref[:, sl]
            km = jnp.where(half, k_ref[:, sl], jnp.zeros_like(k_ref[:, sl]))
            sd = _mx(_nt(qp, km) * d_ref[h])
            qf = qp.astype(F32)
            o = _nn(sd, v_ref[:, 128 * h:128 * (h + 1)])
            o = o + _nn(_mx(qf * wqf_ref[h]), rf_ref[h]) + _nn(_mx(qf * wqb_ref[h]), rb_ref[h])
            o_ref[:, 128 * h:128 * (h + 1)] = o

    st = pl.BlockSpec((None, RET_HEADS, 128, 128), lambda n: (n, 0, 0, 0))
    wsp = pl.BlockSpec((RET_HEADS, c, 128), lambda n: (0, 0, 0))
    return pl.pallas_call(
        body, grid=(seq // c,), name="ret_fwd",
        in_specs=[pl.BlockSpec((c, 256), lambda n: (n, 0)), pl.BlockSpec((c, 256), lambda n: (n, 0)),
                  pl.BlockSpec((c, 512), lambda n: (n, 0)), st, st, pl.BlockSpec((RET_HEADS, c, c), lambda n: (0, 0, 0)), wsp, wsp],
        out_specs=pl.BlockSpec((c, 512), lambda n: (n, 0)),
        out_shape=jax.ShapeDtypeStruct((seq, 512), F32), compiler_params=_cp("parallel"))(
            rq, rk, rv, rf, rb, tb["dmat"], tb["wqf"], tb["wqb"])


def _ret_bwd(rq, rk, rv, do, rf, rb, hf, hb, tb):
    seq = rq.shape[0]
    c = RET_CHUNK

    def body(q_ref, k_ref, v_ref, do_ref, rf_ref, rb_ref, hf_ref, hb_ref, d_ref, dda_ref, ddb_ref,
             wqf_ref, wqb_ref, wkf_ref, wkb_ref, cdec_ref, dq_ref, dk_ref, dv_ref, dlg_ref):
        @pl.when(pl.program_id(0) == 0)
        def _():
            dlg_ref[...] = jnp.zeros_like(dlg_ref)

        lane = _lane((c, 128))
        pos = lax.broadcasted_iota(jnp.int32, (c, 128), 0).astype(F32)
        for pr in range(2):
            sl = slice(128 * pr, 128 * (pr + 1))
            qp, kp = q_ref[:, sl], k_ref[:, sl]
            qf, kf = qp.astype(F32), kp.astype(F32)
            dq_acc = jnp.zeros((c, 128), F32)
            dk_acc = jnp.zeros((c, 128), F32)
            for e in range(2):
                h = 2 * pr + e
                half = (lane < 64) if e == 0 else (lane >= 64)
                hs = slice(128 * h, 128 * (h + 1))
                km = jnp.where(half, kp, jnp.zeros_like(kp))
                kmf = km.astype(F32)
                vh, doh = v_ref[:, hs], do_ref[:, hs]
                rfh, rbh, hfh, hbh = rf_ref[h], rb_ref[h], hf_ref[h], hb_ref[h]
                s = _nt(qp, km)
                dpm = _nt(doh, vh)
                ds_b = _mx(dpm * d_ref[h])
                sd_b = _mx(s * d_ref[h])
                dq_if = wqf_ref[h] * _nt(doh, rfh)
                dq_ib = wqb_ref[h] * _nt(doh, rbh)
                dq_acc = dq_acc + _nn(ds_b, km) + dq_if + dq_ib
                dk_sf = wkf_ref[h] * _nt(vh, hfh)
                dk_sb = wkb_ref[h] * _nt(vh, hbh)
                dk_acc = dk_acc + jnp.where(half, _tn(ds_b, qp), 0.0) + dk_sf + dk_sb
                dv = _tn(sd_b, doh) + _nn(_mx(kmf * wkf_ref[h]), hfh) + _nn(_mx(kmf * wkb_ref[h]), hbh)
                dv_ref[:, hs] = dv.astype(dv_ref.dtype)
                sdp = s * dpm
                hr_f = hfh.astype(F32) * rfh.astype(F32)
                hr_b = hbh.astype(F32) * rbh.astype(F32)
                da = (_rowsum128(sdp * dda_ref[h]) + _rowsum128((pos + 1.0) * qf * dq_if)
                      + _rowsum128((c - 1.0 - pos) * kf * dk_sf) + cdec_ref[h:h + 1, :] * _rowsum128(hr_f))
                db = (_rowsum128(sdp * ddb_ref[h]) + _rowsum128((c - pos) * qf * dq_ib)
                      + _rowsum128(pos * kf * dk_sb) + cdec_ref[4 + h:5 + h, :] * _rowsum128(hr_b))
                dlg_ref[h:h + 1, :] += da
                dlg_ref[4 + h:5 + h, :] += db
            dq_ref[:, sl] = dq_acc
            dk_ref[:, sl] = dk_acc

    st = pl.BlockSpec((None, RET_HEADS, 128, 128), lambda n: (n, 0, 0, 0))
    wsp = pl.BlockSpec((RET_HEADS, c, 128), lambda n: (0, 0, 0))
    dsp = pl.BlockSpec((RET_HEADS, c, c), lambda n: (0, 0, 0))
    x256 = pl.BlockSpec((c, 256), lambda n: (n, 0))
    x512 = pl.BlockSpec((c, 512), lambda n: (n, 0))
    cdec = tb["dec_fb"] * float(c)
    return pl.pallas_call(
        body, grid=(seq // c,), name="ret_bwd",
        in_specs=[x256, x256, x512, x512, st, st, st, st, dsp, dsp, dsp, wsp, wsp, wsp, wsp,
                  pl.BlockSpec((8, 128), lambda n: (0, 0))],
        out_specs=[x256, x256, x512, pl.BlockSpec((8, 128), lambda n: (0, 0))],
        out_shape=[jax.ShapeDtypeStruct((seq, 256), F32), jax.ShapeDtypeStruct((seq, 256), F32),
                   jax.ShapeDtypeStruct((seq, 512), MXU_DTYPE), jax.ShapeDtypeStruct((8, 128), F32)],
        compiler_params=_cp("arbitrary"))(
            rq, rk, rv, do, rf, rb, hf, hb, tb["dmat"], tb["dda"], tb["ddb"], tb["wqf"], tb["wqb"], tb["wkf"], tb["wkb"], cdec)


def _tail(x, tgt, o_att, o_ret, p, mod, g_post, gn_g, wpt, wout):
    seq = x.shape[0]
    bs = 256
    nb = seq // bs

    def body(x_ref, t_ref, oa_ref, or_ref, za_ref, zr_ref, gl_ref, mod_ref, gp_ref, gn_ref, wpt_ref, wout_ref,
             dout_ref, dza_ref, dzr_ref, dgl_ref, doa_ref, dor_ref, del_ref, gout_ref, gpt_ref,
             dgate_ref, dgpost_ref, dgn_ref, loss_ref, gout_acc, gpt_acc):
        i = pl.program_id(0)

        @pl.when(i == 0)
        def _():
            gout_acc[...] = jnp.zeros_like(gout_acc)
            gpt_acc[...] = jnp.zeros_like(gpt_acc)
            dgate_ref[...] = jnp.zeros_like(dgate_ref)
            dgpost_ref[...] = jnp.zeros_like(dgpost_ref)
            dgn_ref[...] = jnp.zeros_like(dgn_ref)
            loss_ref[...] = jnp.zeros_like(loss_ref)

        oa, za, zr = oa_ref[...], za_ref[...], zr_ref[...]
        sga, sgr = _sigmoid(za), _sigmoid(zr)
        sza, szr = za * sga, zr * sgr
        ya_b = _mx(oa * sza)
        ons, rstds = [], []
        for h in range(RET_HEADS):
            oh = or_ref[:, 128 * h:128 * (h + 1)]
            xc = oh - jnp.mean(oh, axis=-1, keepdims=True)
            rstd = lax.rsqrt(jnp.mean(xc * xc, axis=-1, keepdims=True) + EPS)
            ons.append(xc * rstd)
            rstds.append(rstd)
        on = jnp.concatenate(ons, axis=1)
        yn = on * gn_ref[...]
        yr_b = _mx(yn * szr)
        wpa_t, wpr_t = wpt_ref[:, 0:512], wpt_ref[:, 512:1024]
        a_att = _nt(ya_b, wpa_t)
        a_ret = _nt(yr_b, wpr_t)
        gts = _sigmoid(gl_ref[...])
        g_att, g_ret = gts[:, 0:D_MODEL], gts[:, D_MODEL:2 * D_MODEL]
        merged_b = _mx(g_att * a_att + g_ret * a_ret)
        u = _nn(merged_b, wout_ref[...])
        r = lax.rsqrt(jnp.mean(u * u, axis=-1, keepdims=True) + EPS)
        un = u * r
        gpost = gp_ref[...]
        y = un * gpost
        gate = mod_ref[:, 2 * D_MODEL:3 * D_MODEL]
        err = x_ref[...] + gate * y - t_ref[...]
        loss_ref[...] += (0.5 / D_MODEL) * _rowsum128(err * err)
        dout = err * (1.0 / D_MODEL)
        dout_ref[...] = dout
        dgate_ref[...] += jnp.sum(dout * y, axis=0, keepdims=True)
        dy = dout * gate
        dgpost_ref[...] += jnp.sum(dy * un, axis=0, keepdims=True)
        dun = dy * gpost
        du_b = _mx(r * (dun - un * jnp.mean(dun * un, axis=-1, keepdims=True)))
        dmerged = _nt(du_b, wout_ref[...])
        gout_acc[...] += _tn(merged_b, du_b)
        d_att, d_ret = dmerged * g_att, dmerged * g_ret
        dgl_ref[:, 0:D_MODEL] = (d_att * a_att * (1.0 - g_att)).astype(dgl_ref.dtype)
        dgl_ref[:, D_MODEL:2 * D_MODEL] = (d_ret * a_ret * (1.0 - g_ret)).astype(dgl_ref.dtype)
        d_att_b, d_ret_b = _mx(d_att), _mx(d_ret)
        dya = _nn(d_att_b, wpa_t)
        dyr = _nn(d_ret_b, wpr_t)
        gpt_acc[:, 0:512] += _tn(d_att_b, ya_b)
        gpt_acc[:, 512:1024] += _tn(d_ret_b, yr_b)
        dza_ref[...] = (dya * oa * (sga * (1.0 + za * (1.0 - sga)))).astype(dza_ref.dtype)
        doa = dya * sza
        doa_ref[...] = doa.astype(doa_ref.dtype)
        dt = jnp.transpose(doa * oa)
        del_ref[...] = jnp.sum(dt.reshape(8, HEAD_DIM, bs), axis=1)
        dzr_ref[...] = (dyr * yn * (sgr * (1.0 + zr * (1.0 - sgr)))).astype(dzr_ref.dtype)
        dyn = dyr * szr
        dgn_ref[...] += jnp.sum(dyn * on, axis=0, keepdims=True)
        don = dyn * gn_ref[...]
        for h in range(RET_HEADS):
            hs = slice(128 * h, 128 * (h + 1))
            dh, oh = don[:, hs], ons[h]
            doh = rstds[h] * (dh - jnp.mean(dh, axis=-1, keepdims=True) - oh * jnp.mean(dh * oh, axis=-1, keepdims=True))
            dor_ref[:, hs] = doh.astype(dor_ref.dtype)

        @pl.when(i == nb - 1)
        def _():
            gout_ref[...] = gout_acc[...].astype(gout_ref.dtype)
            gpt_ref[...] = gpt_acc[...].astype(gpt_ref.dtype)

    row = lambda w: pl.BlockSpec((bs, w), lambda i: (i, 0))
    el = lambda w, off: pl.BlockSpec((pl.Element(bs), pl.Element(w)), lambda i: (i * bs, off))
    vec = lambda w: pl.BlockSpec((1, w), lambda i: (0, 0))
    full = pl.BlockSpec((D_MODEL, D_MODEL), lambda i: (0, 0))
    sds = jax.ShapeDtypeStruct
    return pl.pallas_call(
        body, grid=(nb,), name="tail",
        in_specs=[row(D_MODEL), row(D_MODEL), row(512), row(512), el(512, ZA_B), el(512, ZR_B), el(2 * D_MODEL, GL_B),
                  vec(3 * D_MODEL), vec(D_MODEL), vec(512), full, full],
        out_specs=[row(D_MODEL), row(512), row(512), row(2 * D_MODEL), row(512), row(512),
                   pl.BlockSpec((8, bs), lambda i: (0, i)), full, full, vec(D_MODEL), vec(D_MODEL), vec(512), vec(128)],
        out_shape=[sds((seq, D_MODEL), F32), sds((seq, 512), MXU_DTYPE), sds((seq, 512), MXU_DTYPE),
                   sds((seq, 2 * D_MODEL), MXU_DTYPE), sds((seq, 512), MXU_DTYPE), sds((seq, 512), MXU_DTYPE),
                   sds((8, seq), F32), sds((D_MODEL, D_MODEL), MXU_DTYPE), sds((D_MODEL, D_MODEL), MXU_DTYPE),
                   sds((1, D_MODEL), F32), sds((1, D_MODEL), F32), sds((1, 512), F32), sds((1, 128), F32)],
        scratch_shapes=[pltpu.VMEM((D_MODEL, D_MODEL), F32), pltpu.VMEM((D_MODEL, D_MODEL), F32)],
        compiler_params=_cp("arbitrary"))(x, tgt, o_att, o_ret, p, p, p, mod, g_post, gn_g, wpt, wout)


def _assemble(p, cos, sin, qg2, kg2, dqt, dkp, dvp, dza, drq, drk, drv, dzr, dgl):
    seq = p.shape[0]
    bs = 512

    def body(p_ref, cos_ref, sin_ref, qg_ref, kg_ref, dqt_ref, dkp_ref, dvp_ref, dza_ref, drq_ref, drk_ref, drv_ref,
             dzr_ref, dgl_ref, dp_ref, dqg_ref, dkg_ref):
        @pl.when(pl.program_id(0) == 0)
        def _():
            dqg_ref[...] = jnp.zeros_like(dqg_ref)
            dkg_ref[...] = jnp.zeros_like(dkg_ref)

        m = _blockdiag64()
        cs, sn = cos_ref[...], sin_ref[...]
        lo = _lane((bs, 128)) < 64

        def norm_bwd(raw, dn, g, dg_ref):
            r = lax.rsqrt(_seg64(raw * raw, m) * (1.0 / 64) + EPS)
            xn = raw * r
            dg_ref[...] += jnp.sum(dn * xn, axis=0, keepdims=True)
            dxn = dn * g
            return r * (dxn - xn * (_seg64(dxn * xn, m) * (1.0 / 64)))

        for pr in range(4):
            sl = slice(128 * pr, 128 * (pr + 1))
            dn = _rope_t(dqt_ref[:, sl] * 0.125, cs, sn)
            dp_ref[:, QA + 128 * pr:QA + 128 * (pr + 1)] = norm_bwd(p_ref[:, sl], dn, qg_ref[...], dqg_ref).astype(dp_ref.dtype)
        fold = lambda a: a + pltpu.roll(a, 64, 1)
        dk = jnp.where(lo, fold(dkp_ref[0]), fold(dkp_ref[1]))
        dp_ref[:, KA:KA + 128] = norm_bwd(p_ref[:, KA:KA + 128], _rope_t(dk, cs, sn), kg_ref[...], dkg_ref).astype(dp_ref.dtype)
        dp_ref[:, VA:VA + 128] = jnp.where(lo, fold(dvp_ref[0]), fold(dvp_ref[1])).astype(dp_ref.dtype)
        dp_ref[:, ZA:ZA + 512] = dza_ref[...]
        for pr in range(2):
            sl = slice(128 * pr, 128 * (pr + 1))
            dp_ref[:, QR + 128 * pr:QR + 128 * (pr + 1)] = _rope_t(drq_ref[:, sl], cs, sn).astype(dp_ref.dtype)
            dp_ref[:, KR + 128 * pr:KR + 128 * (pr + 1)] = _rope_t(drk_ref[:, sl] * 0.125, cs, sn).astype(dp_ref.dtype)
        dp_ref[:, VR:VR + 512] = drv_ref[...]
        dp_ref[:, ZR:ZR + 512] = dzr_ref[...]
        dp_ref[:, GL:GL + 2 * D_MODEL] = dgl_ref[...]

    row = lambda w: pl.BlockSpec((bs, w), lambda i: (i, 0))
    gsp = pl.BlockSpec((1, 128), lambda i: (0, 0))
    dup = pl.BlockSpec((2, bs, 128), lambda i: (0, i, 0))
    return pl.pallas_call(
        body, grid=(seq // bs,), name="assemble_dp",
        in_specs=[row(768), row(128), row(128), gsp, gsp, row(512), dup, dup, row(512), row(256), row(256), row(512),
                  row(512), row(2 * D_MODEL)],
        out_specs=[row(IN_WIDTH), gsp, gsp],
        out_shape=[jax.ShapeDtypeStruct((seq, IN_WIDTH), MXU_DTYPE), jax.ShapeDtypeStruct((1, 128), F32),
                   jax.ShapeDtypeStruct((1, 128), F32)],
        compiler_params=_cp("arbitrary"))(p, cos, sin, qg2, kg2, dqt, dkp, dvp, dza, drq, drk, drv, dzr, dgl)


def _local_step(x, tgt, mod, g_pre, qn_g, kn_g, w_dec_f, w_dec_b, gn_g, g_post, w_attn, rest_start, rest_wait, send=None):
    send = send or (lambda name, arrays: None)
    seq = x.shape[0]
    cos, sin = _rope_tables(seq)
    qg2, kg2 = jnp.tile(qn_g, (1, 2)), jnp.tile(kn_g, (1, 2))
    lg_f = jax.nn.log_sigmoid(w_dec_f[0])
    lg_b = jax.nn.log_sigmoid(w_dec_b[0])
    tb = _ret_tables(lg_f, lg_b, RET_CHUNK)

    h = _prenorm(x, mod, g_pre)
    p_a = _matmul(h, w_attn, ta=False, tb=True, tm=512, tn=ATTN_COLS, out_dtype=F32, name="in_proj_attn")
    qt, kd, vd = _prep_attn(p_a, cos, sin, qg2, kg2)
    o_att, lse = _attn_fwd(qt, kd, vd, dep=rest_start(qt))
    win_t, wpt, wout = rest_wait(o_att)
    p_b = _matmul(h, win_t, ta=False, tb=True, tm=512, tn=(IN_WIDTH - ATTN_COLS) // 2, out_dtype=F32, name="in_proj_rest",
                  b_rows=(ATTN_COLS, IN_WIDTH - ATTN_COLS))
    rq, rk, rv = _prep_ret(p_b, cos, sin)
    rf, rb = _ret_states(rk, rv, tb["wkf"], tb["wkb"], tb["dec_fb"], "ret_states_fwd")
    o_ret = _ret_fwd(rq, rk, rv, rf, rb, tb)
    (dout, dza, dzr, dgl, do_att, do_ret, delta, g_out, g_pt, dgate, dgpost, dgn, loss_l) = _tail(
        x, tgt, o_att, o_ret, p_b, mod, g_post, gn_g, wpt, wout)
    dep = send("early", (g_pt, g_out))
    dqt, dkp, dvp = _attn_bwd(qt, kd, vd, do_att, lse, delta, dep=dep)
    hb, hf = _ret_states(rq, do_ret, tb["wqb"], tb["wqf"], tb["dec_bf"], "ret_states_bwd")
    drq, drk, drv, dlg = _ret_bwd(rq, rk, rv, do_ret, rf, rb, hf, hb, tb)
    dp, dqg, dkg = _assemble(p_a, cos, sin, qg2, kg2, dqt, dkp, dvp, dza, drq, drk, drv, dzr, dgl)
    g_in_t = _matmul(dp, h, ta=True, tb=False, tm=256, tn=D_MODEL, out_dtype=MXU_DTYPE, name="in_proj_dw")
    dep = send("late", (g_in_t,))
    dh = _matmul(dp, win_t, ta=False, tb=False, tm=512, tn=D_MODEL, out_dtype=F32, name="in_proj_dx", dep=dep)
    grad_x, dshift, dscale, dgpre = _prenorm_bwd(x, dh, dout, mod, g_pre)

    dlg = jnp.sum(dlg, axis=1)
    small = dict(
        dmod=jnp.concatenate([dshift, dscale, dgate], axis=1),
        g_pre=dgpre, g_post=dgpost, gn_g=dgn,
        qn_g=dqg[:, :64] + dqg[:, 64:], kn_g=dkg[:, :64] + dkg[:, 64:],
        w_dec_f=(dlg[0:4] * jax.nn.sigmoid(-w_dec_f[0]))[None], w_dec_b=(dlg[4:8] * jax.nn.sigmoid(-w_dec_b[0]))[None],
        loss=jnp.sum(loss_l, axis=1, keepdims=True))
    return grad_x, g_in_t, g_pt, g_out, small


N_DEV = 8
N_CHIP = 4
HBM_SPEC = pl.BlockSpec(memory_space=pl.ANY)
VMEM_SPEC = pl.BlockSpec(memory_space=pltpu.VMEM)
SHARD_ROWS = (IN_WIDTH // N_CHIP, D_MODEL // N_CHIP, D_MODEL // N_CHIP)


def _coords():
    return lax.axis_index("x"), lax.axis_index("y"), lax.axis_index("c")


def _peer(k):
    x, y, c = _coords()
    return (1 - x if k & 4 else x, 1 - y if k & 2 else y, 1 - c if k & 1 else c)


def _dev_index(p):
    return 4 * p[0] + 2 * p[1] + p[2]


def _rows(ref, start, size):
    return ref.at[pl.ds(pl.multiple_of(start, 16), size), :]


def _remote(src, dst, ssem, rsem, dev):
    return pltpu.make_async_remote_copy(src_ref=src, dst_ref=dst, send_sem=ssem, recv_sem=rsem, device_id=dev,
                                        device_id_type=MESH)


def _mod_exchange(c, w_ada_s, b4):
    ncol = w_ada_s.shape[1]

    def body(c_ref, w_ref, b_ref, cs_ref, mod_ref, modsh, modall, ssem, rsem):
        me = _dev_index(_coords())
        chip = me // 2
        cs_ref[me] = c_ref[...]
        sends = []
        for k in range(1, N_DEV):
            cp = _remote(c_ref, cs_ref.at[me], ssem.at[k - 1], rsem.at[k - 1], _peer(k))
            cp.start()
            sends.append(cp)
        for k in range(1, N_DEV):
            slot = cs_ref.at[_dev_index(_peer(k))]
            _remote(slot, slot, ssem.at[k - 1], rsem.at[k - 1], _peer(k)).wait_recv()
        cs = jnp.concatenate([cs_ref[d] for d in range(N_DEV)], axis=0)
        ms = _nn(cs * _sigmoid(cs), w_ref[...], precision=HIGHEST) + b_ref[pl.ds(chip, 1), :]
        modsh[...] = ms
        modall[chip] = ms
        for k2 in range(1, N_CHIP):
            cp = _remote(modsh, modall.at[chip], ssem.at[6 + k2], rsem.at[6 + k2], _peer(2 * k2))
            cp.start()
            sends.append(cp)
        for k2 in range(1, N_CHIP):
            slot = modall.at[_dev_index(_peer(2 * k2)) // 2]
            _remote(slot, slot, ssem.at[6 + k2], rsem.at[6 + k2], _peer(2 * k2)).wait_recv()
        for jj in range(N_CHIP):
            mod_ref[:, ncol * jj:ncol * (jj + 1)] = modall[jj, pl.ds(me, 1), :]
        for cp in sends:
            cp.wait_send()

    return pl.pallas_call(
        body, name="mod_exchange", in_specs=[VMEM_SPEC] * 3, out_specs=[VMEM_SPEC] * 2,
        out_shape=[jax.ShapeDtypeStruct((N_DEV, 1, D_MODEL), F32), jax.ShapeDtypeStruct((1, N_CHIP * ncol), F32)],
        scratch_shapes=[pltpu.VMEM((N_DEV, ncol), F32), pltpu.VMEM((N_CHIP, N_DEV, ncol), F32),
                        pltpu.SemaphoreType.DMA((10,)), pltpu.SemaphoreType.DMA((10,))],
        compiler_params=_cp())(c, w_ada_s, b4)


GATHER_CHUNK = 32


def _chunks(kinds, chunk):
    out = []
    for t, kind in enumerate(kinds):
        half = SHARD_ROWS[kind] // 2
        step = chunk if half % chunk == 0 else half
        out += [(t, off, step) for off in range(0, half, step)]
    return out


ATTN_CHUNK = 96


def _gather_attn(win_s):
    half = ATTN_COLS // 2
    offs = list(range(0, half, ATTN_CHUNK))
    nq = len(offs)
    rows = lambda ref, cc, off: ref.at[pl.ds(pl.multiple_of(cc * half + off, 16), ATTN_CHUNK), :]

    def body(src, out, lsem, ssem, rsem, fsem, gsem):
        x, y, c = _coords()
        chip = 2 * x + y
        sib = (x, y, 1 - c)

        @pl.when(chip == 0)
        def _():
            own = pltpu.make_async_copy(src.at[pl.ds(0, ATTN_COLS), :], out, lsem.at[0])
            own.start()
            sends = []
            for k2 in range(1, N_CHIP):
                for q, off in enumerate(offs):
                    cp = _remote(rows(src, c, off), rows(out, c, off), ssem.at[(k2 - 1) * nq + q], rsem.at[q], (k2 // 2, k2 % 2, c))
                    cp.start()
                    sends.append(cp)
            for cp in sends:
                cp.wait_send()
            own.wait()

        @pl.when(chip != 0)
        def _():
            passed = []
            for q, off in enumerate(offs):
                got = rows(out, c, off)
                _remote(got, got, ssem.at[q], rsem.at[q], (0, 0, c)).wait_recv()
                cp = _remote(got, got, fsem.at[q], gsem.at[q], sib)
                cp.start()
                passed.append(cp)
            for q, off in enumerate(offs):
                got = rows(out, 1 - c, off)
                _remote(got, got, fsem.at[q], gsem.at[q], sib).wait_recv()
            for cp in passed:
                cp.wait_send()

    dma = pltpu.SemaphoreType.DMA
    return pl.pallas_call(
        body, name="gather_attn", in_specs=[VMEM_SPEC], out_specs=HBM_SPEC,
        out_shape=jax.ShapeDtypeStruct((ATTN_COLS, win_s.shape[1]), win_s.dtype),
        scratch_shapes=[dma((1,)), dma((3 * nq,)), dma((nq,)), dma((nq,)), dma((nq,))],
        compiler_params=_cp())(win_s)


SEM_SPEC = pl.BlockSpec(memory_space=pltpu.SEMAPHORE)
HBM_ONLY = pl.BlockSpec(memory_space=pltpu.HBM)
DATAFLOW = pltpu.SideEffectType.DATAFLOW_SIDE_EFFECTING


def _place_own(shards):
    nt = len(shards)

    def body(*refs):
        srcs, outs, lsem = refs[:nt], refs[nt:2 * nt], refs[2 * nt]
        x, y, _ = _coords()
        chip = 2 * x + y
        local = [pltpu.make_async_copy(srcs[t], _rows(outs[t], chip * SHARD_ROWS[t], SHARD_ROWS[t]), lsem.at[t]) for t in range(nt)]
        for cp in local:
            cp.start()
        for cp in local:
            cp.wait()

    return pl.pallas_call(
        body, name="place_own_shard", in_specs=[VMEM_SPEC] * nt, out_specs=[HBM_SPEC] * nt,
        out_shape=[jax.ShapeDtypeStruct((N_CHIP * SHARD_ROWS[t], s.shape[1]), s.dtype) for t, s in enumerate(shards)],
        scratch_shapes=[pltpu.SemaphoreType.DMA((nt,))], compiler_params=_cp())(*shards)


def _gather_rest_start(shards, zones, dep):
    n = len(shards)

    def body(*refs):
        srcs, lands = refs[:n], refs[n:2 * n]
        ssem, rsem = refs[2 * n + 1], refs[2 * n + 2]
        token = refs[-1]
        x, y, _ = _coords()
        chip = 2 * x + y
        for k2 in range(1, N_CHIP):
            for t in range(n):
                q = (k2 - 1) * n + t
                _remote(srcs[t], _rows(lands[t], chip * SHARD_ROWS[t], SHARD_ROWS[t]), ssem.at[q], rsem.at[q], _peer(2 * k2)).start()
        token[...] = jnp.zeros_like(token)

    hbm = lambda a: pltpu.with_memory_space_constraint(a, pltpu.HBM)
    dma = pltpu.SemaphoreType.DMA
    outs = pl.pallas_call(
        body, name="gather_rest", in_specs=[HBM_ONLY] * (2 * n) + [HBM_SPEC],
        out_specs=[SEM_SPEC, SEM_SPEC] + [HBM_ONLY] * (2 * n) + [VMEM_SPEC],
        out_shape=[dma((3 * n,)), dma((3 * n,))] + [pltpu.HBM(a.shape, a.dtype) for a in list(shards) + list(zones)]
        + [jax.ShapeDtypeStruct((8, 128), F32)],
        input_output_aliases={i: 2 + i for i in range(2 * n)},
        compiler_params=pltpu.CompilerParams(has_side_effects=DATAFLOW))(*[hbm(a) for a in list(shards) + list(zones)], dep)
    return outs[0], outs[1], outs[2:2 + n], outs[2 + n:2 + 2 * n], outs[-1]


def _gather_rest_wait(started, after):
    ssem, rsem, shards, zones, _ = started
    n = len(shards)

    def body(*refs):
        srcs, lands = refs[:n], refs[n:2 * n]
        ssem_ref, rsem_ref = refs[2 * n], refs[2 * n + 1]
        for k2 in range(1, N_CHIP):
            pchip = _dev_index(_peer(2 * k2)) // 2
            for t in range(n):
                q = (k2 - 1) * n + t
                cp = _remote(srcs[t], _rows(lands[t], pchip * SHARD_ROWS[t], SHARD_ROWS[t]), ssem_ref.at[q], rsem_ref.at[q], _peer(2 * k2))
                cp.wait_send()
                cp.wait_recv()

    outs = pl.pallas_call(
        body, name="gather_rest_wait", in_specs=[HBM_ONLY] * (2 * n) + [SEM_SPEC, SEM_SPEC, HBM_SPEC], out_specs=[HBM_ONLY] * (2 * n),
        out_shape=[pltpu.HBM(a.shape, a.dtype) for a in list(shards) + list(zones)],
        input_output_aliases={i: i for i in range(2 * n)},
        compiler_params=pltpu.CompilerParams(has_side_effects=DATAFLOW))(*shards, *zones, ssem, rsem, after)
    return outs[n:]


def _reduced_by(ref, t, dev):
    return _rows(ref, (dev // 2) * SHARD_ROWS[t] + (dev % 2) * (SHARD_ROWS[t] // 2), SHARD_ROWS[t] // 2)


def _scatter_start(grads, kinds, name):
    n = len(grads)

    def body(*refs):
        srcs, lands = refs[:n], refs[n:2 * n]
        ssem, rsem = refs[2 * n], refs[2 * n + 1]
        token = refs[-1]
        me = _dev_index(_coords())
        for k in range(1, N_DEV):
            for i, t in enumerate(kinds):
                q = (k - 1) * n + i
                _remote(_reduced_by(srcs[i], t, _dev_index(_peer(k))), lands[i].at[me], ssem.at[q], rsem.at[q], _peer(k)).start()
        token[...] = jnp.zeros_like(token)

    zones = [lax.empty((N_DEV, SHARD_ROWS[t] // 2, g.shape[1]), g.dtype) for g, t in zip(grads, kinds)]
    hbm = lambda a: pltpu.with_memory_space_constraint(a, pltpu.HBM)
    dma = pltpu.SemaphoreType.DMA
    outs = pl.pallas_call(
        body, name=name, in_specs=[HBM_ONLY] * (2 * n), out_specs=[SEM_SPEC, SEM_SPEC] + [HBM_ONLY] * (2 * n) + [VMEM_SPEC],
        out_shape=[dma((7 * n,)), dma((7 * n,))] + [pltpu.HBM(a.shape, a.dtype) for a in list(grads) + zones]
        + [jax.ShapeDtypeStruct((8, 128), F32)],
        input_output_aliases={i: 2 + i for i in range(2 * n)},
        compiler_params=pltpu.CompilerParams(has_side_effects=DATAFLOW))(*[hbm(a) for a in list(grads) + zones])
    return outs[0], outs[1], outs[2:2 + n], outs[2 + n:2 + 2 * n], outs[-1]


def _scatter_wait(started, kinds, after, name):
    ssem, rsem, grads, zones, _ = started
    n = len(grads)

    def body(*refs):
        srcs, lands = refs[:n], refs[n:2 * n]
        ssem_ref, rsem_ref = refs[2 * n], refs[2 * n + 1]
        for k in range(1, N_DEV):
            peer = _dev_index(_peer(k))
            for i, t in enumerate(kinds):
                q = (k - 1) * n + i
                cp = _remote(_reduced_by(srcs[i], t, peer), lands[i].at[peer], ssem_ref.at[q], rsem_ref.at[q], _peer(k))
                cp.wait_send()
                cp.wait_recv()

    outs = pl.pallas_call(
        body, name=name, in_specs=[HBM_ONLY] * (2 * n) + [SEM_SPEC, SEM_SPEC, HBM_SPEC], out_specs=[HBM_ONLY] * (2 * n),
        out_shape=[pltpu.HBM(a.shape, a.dtype) for a in list(grads) + list(zones)],
        input_output_aliases={i: i for i in range(2 * n)},
        compiler_params=pltpu.CompilerParams(has_side_effects=DATAFLOW))(*grads, *zones, ssem, rsem, after)
    return outs[:n], outs[n:]


def _grad_finish(grads, zones, kinds, name):
    nt = len(grads)
    pieces = _chunks(kinds, GATHER_CHUNK)
    npc = len(pieces)
    shard = [SHARD_ROWS[k] for k in kinds]

    def body(*refs):
        srcs, lands, outs = refs[:nt], refs[nt:2 * nt], refs[2 * nt:3 * nt]
        slots, sums = refs[3 * nt:4 * nt], refs[4 * nt:5 * nt]
        lsem, osem, xsem, ysem = refs[5 * nt:]
        x, y, c = _coords()
        me = _dev_index((x, y, c))
        sib = (x, y, 1 - c)
        loads = [pltpu.make_async_copy(_reduced_by(srcs[t], kinds[t], me), slots[t].at[me], lsem.at[t]) for t in range(nt)]
        for k in range(1, N_DEV):
            peer = _dev_index(_peer(k))
            loads += [pltpu.make_async_copy(lands[t].at[peer], slots[t].at[peer], lsem.at[k * nt + t]) for t in range(nt)]
        for cp in loads:
            cp.start()
        for cp in loads:
            cp.wait()
        sends = []
        place = lambda t, cc, off, n: _rows(outs[t], cc * (shard[t] // 2) + off, n)
        stores = []
        for q, (t, off, n) in enumerate(pieces):
            r = pl.ds(off, n)
            acc = slots[t][0, r, :].astype(F32)
            for d in range(1, N_DEV):
                acc = acc + slots[t][d, r, :].astype(F32)
            sums[t][r, :] = acc
            keep = pltpu.make_async_copy(sums[t].at[r, :], place(t, c, off, n), osem.at[q])
            give = _remote(sums[t].at[r, :], place(t, c, off, n), xsem.at[q], ysem.at[q], sib)
            keep.start()
            give.start()
            stores.append(keep)
            sends.append(give)
        for q, (t, off, n) in enumerate(pieces):
            got = place(t, 1 - c, off, n)
            _remote(got, got, xsem.at[q], ysem.at[q], sib).wait_recv()
        for cp in sends:
            cp.wait_send()
        for cp in stores:
            cp.wait()

    dma = pltpu.SemaphoreType.DMA
    return pl.pallas_call(
        body, name=name, in_specs=[HBM_SPEC] * (2 * nt), out_specs=[HBM_SPEC] * nt,
        out_shape=[jax.ShapeDtypeStruct((shard[t], g.shape[1]), F32) for t, g in enumerate(grads)],
        scratch_shapes=([pltpu.VMEM((N_DEV, shard[t] // 2, g.shape[1]), g.dtype) for t, g in enumerate(grads)]
                        + [pltpu.VMEM((shard[t] // 2, g.shape[1]), F32) for t, g in enumerate(grads)]
                        + [dma((N_DEV * nt,)), dma((npc,)), dma((npc,)), dma((npc,))]),
        compiler_params=_cp())(*grads, *zones)


def _adam_math(w, g, m, v):
    m = ADAM_B1 * m + (1.0 - ADAM_B1) * g
    v = ADAM_B2 * v + (1.0 - ADAM_B2) * (g * g)
    m_hat = m / (1.0 - ADAM_B1 ** ADAM_STEP)
    v_hat = v / (1.0 - ADAM_B2 ** ADAM_STEP)
    return -ADAM_LR * (m_hat / (jnp.sqrt(v_hat) + ADAM_EPS) + ADAM_WD * w), m, v


def _adamw(w, g, m, v, rb, name):
    rows, cols = w.shape

    def body(w_ref, g_ref, m_ref, v_ref, d_ref, nm_ref, nv_ref):
        d_ref[...], nm_ref[...], nv_ref[...] = _adam_math(w_ref[...], g_ref[...], m_ref[...], v_ref[...])

    spec = pl.BlockSpec((rb, cols), lambda i: (i, 0))
    return pl.pallas_call(body, grid=(rows // rb,), name=name, in_specs=[spec] * 4, out_specs=[spec] * 3,
                          out_shape=[jax.ShapeDtypeStruct(w.shape, F32)] * 3, compiler_params=_cp("parallel"))(w, g, m, v)


PACK = (("b_ada", 3 * D_MODEL), ("g_pre", D_MODEL), ("g_post", D_MODEL), ("gn_g", 512), ("qn_g", 64), ("kn_g", 64),
        ("w_dec_f", 4), ("w_dec_b", 4), ("loss", 1))
PACK_OFFSETS = {}
PACK_WIDTH = 0
for _name, _n in PACK:
    PACK_OFFSETS[_name] = PACK_WIDTH
    PACK_WIDTH += -(-_n // 128) * 128
SMALL_PARAMS = tuple(name for name, _ in PACK if name != "loss")


def _pack(parts):
    cols = []
    for name, n in PACK:
        pad = -(-n // 128) * 128 - n
        cols.append(parts[name].reshape(1, n).astype(F32))
        if pad:
            cols.append(jnp.zeros((1, pad), F32))
    return jnp.concatenate(cols, axis=1)


def _small_reduce(vec, cs, dep):
    ncol = 3 * D_MODEL // N_CHIP

    def body(vec_ref, cs_ref, dep_ref, tot_ref, gwa_ref, gat, ssem, rsem):
        me = _dev_index(_coords())
        chip = me // 2
        gat[me] = vec_ref[...]
        sends = []
        for k in range(1, N_DEV):
            cp = _remote(vec_ref, gat.at[me], ssem.at[k - 1], rsem.at[k - 1], _peer(k))
            cp.start()
            sends.append(cp)
        for k in range(1, N_DEV):
            slot = gat.at[_dev_index(_peer(k))]
            _remote(slot, slot, ssem.at[k - 1], rsem.at[k - 1], _peer(k)).wait_recv()
        rows = [gat[d] for d in range(N_DEV)]
        tot = rows[0]
        for d in range(1, N_DEV):
            tot = tot + rows[d]
        tot_ref[...] = tot
        cs = cs_ref[...]
        ca_t = jnp.transpose(jnp.concatenate([cs * _sigmoid(cs), jnp.zeros((128 - N_DEV, D_MODEL), F32)], axis=0))
        dm = jnp.concatenate(rows + [jnp.zeros((128 - N_DEV, PACK_WIDTH), F32)], axis=0)
        for jj in range(N_CHIP):
            @pl.when(chip == jj)
            def _():
                gwa_ref[...] = _nn(ca_t, dm[:, ncol * jj:ncol * (jj + 1)], precision=HIGHEST)
        for cp in sends:
            cp.wait_send()

    return pl.pallas_call(
        body, name="small_reduce", in_specs=[VMEM_SPEC, VMEM_SPEC, HBM_SPEC], out_specs=[VMEM_SPEC] * 2,
        out_shape=[jax.ShapeDtypeStruct((1, PACK_WIDTH), F32), jax.ShapeDtypeStruct((D_MODEL, ncol), F32)],
        scratch_shapes=[pltpu.VMEM((N_DEV, 1, PACK_WIDTH), F32), pltpu.SemaphoreType.DMA((7,)), pltpu.SemaphoreType.DMA((7,))],
        compiler_params=_cp())(vec, cs, dep)


def _small_adamw(tot, given):
    sizes = dict(PACK)
    np_ = len(SMALL_PARAMS)

    def body(*refs):
        tot_ref = refs[0]
        wmv = refs[1:1 + 3 * np_]
        outs = refs[1 + 3 * np_:1 + 7 * np_]
        loss_ref = refs[-1]
        for i, name in enumerate(SMALL_PARAMS):
            off, n = PACK_OFFSETS[name], sizes[name]
            g = tot_ref[:, off:off + n]
            w_ref, m_ref, v_ref = wmv[3 * i:3 * i + 3]
            outs[i][...] = g
            outs[np_ + i][...], outs[2 * np_ + i][...], outs[3 * np_ + i][...] = _adam_math(w_ref[...], g, m_ref[...], v_ref[...])
        loss_ref[...] = tot_ref[:, PACK_OFFSETS["loss"]:PACK_OFFSETS["loss"] + 1]

    flat = [a for name in SMALL_PARAMS for a in given[name]]
    shapes = [jax.ShapeDtypeStruct((1, sizes[name]), F32) for name in SMALL_PARAMS]
    res = pl.pallas_call(body, name="small_adamw", in_specs=[VMEM_SPEC] * (1 + 3 * np_), out_specs=[VMEM_SPEC] * (4 * np_ + 1),
                         out_shape=shapes * 4 + [jax.ShapeDtypeStruct((1, 1), F32)], compiler_params=_cp())(tot, *flat)
    return [dict(zip(SMALL_PARAMS, res[k * np_:(k + 1) * np_])) for k in range(4)], res[-1]


def kernel(x, c, w_ada, b_ada, g_pre, w_in, qn_g, kn_g, w_dec_f, w_dec_b, gn_g, w_pa, w_pr, w_out, g_post, loss_target, m_w_ada, m_b_ada, m_g_pre, m_w_in, m_qn_g, m_kn_g, m_w_dec_f, m_w_dec_b, m_gn_g, m_w_pa, m_w_pr, m_w_out, m_g_post, v_w_ada, v_b_ada, v_g_pre, v_w_in, v_qn_g, v_kn_g, v_w_dec_f, v_w_dec_b, v_gn_g, v_w_pa, v_w_pr, v_w_out, v_g_post):
    shards = (w_in[0].T.astype(MXU_DTYPE), jnp.concatenate([w_pa[0].T, w_pr[0].T], axis=1).astype(MXU_DTYPE),
              w_out[0].astype(MXU_DTYPE))
    cs, mod = _mod_exchange(c, w_ada[0], b_ada.reshape(N_CHIP, 3 * D_MODEL // N_CHIP))
    w_attn = _gather_attn(shards[0])
    started = {}

    def rest_start(dep):
        started["rest"] = _gather_rest_start(shards, _place_own(shards), dep)
        return started["rest"][-1]

    def rest_wait(after):
        return _gather_rest_wait(started["rest"], after)

    kinds = {"early": (1, 2), "late": (0,)}

    def send(name, arrays):
        started[name] = _scatter_start(arrays, kinds[name], "grad_scatter_" + name)
        return started[name][-1]

    grad_x, _, _, _, small = _local_step(x[0], loss_target[0], mod, g_pre, qn_g, kn_g, w_dec_f, w_dec_b,
                                         gn_g, g_post, w_attn, rest_start, rest_wait, send=send)
    small = dict(small)
    small["b_ada"] = small.pop("dmod")
    given = dict(b_ada=(b_ada, m_b_ada, v_b_ada), g_pre=(g_pre, m_g_pre, v_g_pre), g_post=(g_post, m_g_post, v_g_post),
                 gn_g=(gn_g, m_gn_g, v_gn_g), qn_g=(qn_g, m_qn_g, v_qn_g), kn_g=(kn_g, m_kn_g, v_kn_g),
                 w_dec_f=(w_dec_f, m_w_dec_f, v_w_dec_f), w_dec_b=(w_dec_b, m_w_dec_b, v_w_dec_b))
    grads, deltas, new_m, new_v = {}, {}, {}, {}

    def update(name, w, g, m, v, back):
        d, nm, nv = _adamw(w, g, m, v, w.shape[0] // 4, "adamw_" + name)
        grads[name], deltas[name], new_m[name], new_v[name] = back(g), back(d), back(nm), back(nv)
        return d

    lead = lambda a: a[None]
    (g_pt, g_out), (z_pt, z_out) = _scatter_wait(started["early"], kinds["early"], grad_x, "grad_scatter_early_wait")
    r_pt, r_out = _grad_finish((g_pt, g_out), (z_pt, z_out), kinds["early"], "grad_finish_early")
    update("w_pa", w_pa[0], r_pt[:, :512].T, m_w_pa[0], v_w_pa[0], lead)
    update("w_pr", w_pr[0], r_pt[:, 512:].T, m_w_pr[0], v_w_pr[0], lead)
    last = update("w_out", w_out[0], r_out, m_w_out[0], v_w_out[0], lead)
    tot, g_w_ada = _small_reduce(_pack(small), cs.reshape(N_DEV, D_MODEL), last)
    small_parts, loss = _small_adamw(tot, given)
    for dst, part in zip((grads, deltas, new_m, new_v), small_parts):
        dst.update(part)
    last = update("w_ada", w_ada[0], g_w_ada, m_w_ada[0], v_w_ada[0], lead)

    (g_in_t,), (z_in,) = _scatter_wait(started["late"], kinds["late"], last, "grad_scatter_late_wait")
    (r_in,) = _grad_finish((g_in_t,), (z_in,), kinds["late"], "grad_finish_late")
    tr = lambda a: a[0].T
    update("w_in", tr(w_in), r_in, tr(m_w_in), tr(v_w_in), lambda a: a.T[None])
    order = ("w_ada", "b_ada", "g_pre", "w_in", "qn_g", "kn_g", "w_dec_f", "w_dec_b", "gn_g", "w_pa", "w_pr", "w_out", "g_post")
    return (loss[0, 0], grad_x[None], *[grads[n] for n in order], *[deltas[n] for n in order],
            *[new_m[n] for n in order], *[new_v[n] for n in order])
```

```python
import functools

import jax
import jax.numpy as jnp
import numpy as np
from jax import lax
from jax.experimental import pallas as pl
from jax.experimental.pallas import tpu as pltpu

F32 = jnp.float32
BF16 = jnp.bfloat16
MXU_DTYPE = jnp.bfloat16

D_MODEL = 1024
GRID_W = 64
HEAD_DIM = 64
ROPE_THETA = 10000.0
EPS = 1e-6
RET_HEADS = 4
RET_CHUNK = 256
IN_WIDTH = 4864
QA, KA, VA, ZA, QR, KR, VR, ZR, GL = 0, 512, 640, 768, 1280, 1536, 1792, 2304, 2816
ATTN_COLS = ZA
ZA_B, QR_B, KR_B, VR_B, ZR_B, GL_B = (c - ATTN_COLS for c in (ZA, QR, KR, VR, ZR, GL))

ADAM_LR, ADAM_B1, ADAM_B2, ADAM_EPS, ADAM_WD, ADAM_STEP = 0.001, 0.9, 0.999, 1e-08, 0.01, 10

VMEM_LIMIT = 56 * 1024 * 1024
MESH = pl.DeviceIdType.MESH
HIGHEST = lax.Precision.HIGHEST
LOG2E = 1.4426950408889634
LN2 = 0.6931471805599453


def _cp(*sem, **kw):
    if sem:
        kw["dimension_semantics"] = sem
    return pltpu.CompilerParams(vmem_limit_bytes=VMEM_LIMIT, **kw)


def _nn(a, b, **kw):
    return lax.dot_general(a, b, (((1,), (0,)), ((), ())), preferred_element_type=F32, **kw)


def _nt(a, b):
    return lax.dot_general(a, b, (((1,), (1,)), ((), ())), preferred_element_type=F32)


def _tn(a, b):
    return lax.dot_general(a, b, (((0,), (0,)), ((), ())), preferred_element_type=F32)


def _mx(x):
    return x.astype(MXU_DTYPE)


def _lane(shape):
    return lax.broadcasted_iota(jnp.int32, shape, 1)


def _sigmoid(z):
    return 1.0 / (1.0 + jnp.exp(-z))


def _rowsum128(x):
    s = jnp.sum(x, axis=0, keepdims=True)
    out = s[:, 0:128]
    for k in range(1, x.shape[1] // 128):
        out = out + s[:, 128 * k:128 * (k + 1)]
    return out


def _swap16(x):
    return jnp.where((_lane(x.shape) & 16) == 0, pltpu.roll(x, 112, 1), pltpu.roll(x, 16, 1))


def _rope(x, cos, sin):
    return x * cos + _swap16(x) * sin


def _rope_t(d, cos, sin):
    return d * cos + _swap16(d * sin)


def _blockdiag64():
    r = lax.broadcasted_iota(jnp.int32, (128, 128), 0) // 64
    c = lax.broadcasted_iota(jnp.int32, (128, 128), 1) // 64
    return (r == c).astype(BF16)


def _seg64(x, m):
    hi = x.astype(BF16)
    lo = (x - hi.astype(F32)).astype(BF16)
    return _nn(hi, m) + _nn(lo, m)


def _rope_tables(seq):
    t = np.arange(seq)
    row = (t // GRID_W).astype(np.float32)
    col = (t % GRID_W).astype(np.float32)
    half = HEAD_DIM // 2
    inv_freq = (np.float32(ROPE_THETA) ** (-np.arange(0, half, 2, dtype=np.float32) / np.float32(half))).astype(np.float32)
    ar = (row[:, None] * inv_freq[None, :]).astype(np.float32).astype(np.float64)
    ac = (col[:, None] * inv_freq[None, :]).astype(np.float32).astype(np.float64)
    cr, sr, cc, sc = np.cos(ar), np.sin(ar), np.cos(ac), np.sin(ac)
    cos64 = np.concatenate([cr, cr, cc, cc], axis=1)
    sin64 = np.concatenate([-sr, sr, -sc, sc], axis=1)
    return jnp.asarray(np.tile(cos64, (1, 2)), F32), jnp.asarray(np.tile(sin64, (1, 2)), F32)


def _prenorm(x, mod, g_pre):
    seq = x.shape[0]
    bs = 512

    def body(x_ref, mod_ref, g_ref, h_ref):
        xv = x_ref[...]
        r = lax.rsqrt(jnp.mean(xv * xv, axis=-1, keepdims=True) + EPS)
        shift = mod_ref[:, 0:D_MODEL]
        scale = mod_ref[:, D_MODEL:2 * D_MODEL]
        h_ref[...] = ((xv * r) * g_ref[...] * (1.0 + scale) + shift).astype(h_ref.dtype)

    return pl.pallas_call(
        body, grid=(seq // bs,), name="prenorm",
        in_specs=[pl.BlockSpec((bs, D_MODEL), lambda i: (i, 0)), pl.BlockSpec((1, 3 * D_MODEL), lambda i: (0, 0)),
                  pl.BlockSpec((1, D_MODEL), lambda i: (0, 0))],
        out_specs=pl.BlockSpec((bs, D_MODEL), lambda i: (i, 0)),
        out_shape=jax.ShapeDtypeStruct((seq, D_MODEL), MXU_DTYPE), compiler_params=_cp("parallel"))(x, mod, g_pre)


def _prenorm_bwd(x, dh, dout, mod, g_pre):
    seq = x.shape[0]
    bs = 512

    def body(x_ref, dh_ref, dout_ref, mod_ref, g_ref, gx_ref, dshift_ref, dscale_ref, dg_ref):
        @pl.when(pl.program_id(0) == 0)
        def _():
            dshift_ref[...] = jnp.zeros_like(dshift_ref)
            dscale_ref[...] = jnp.zeros_like(dscale_ref)
            dg_ref[...] = jnp.zeros_like(dg_ref)

        xv = x_ref[...]
        dh = dh_ref[...]
        g = g_ref[...]
        r = lax.rsqrt(jnp.mean(xv * xv, axis=-1, keepdims=True) + EPS)
        xn = xv * r
        scale = mod_ref[:, D_MODEL:2 * D_MODEL]
        dshift_ref[...] += jnp.sum(dh, axis=0, keepdims=True)
        dscale_ref[...] += jnp.sum(dh * (xn * g), axis=0, keepdims=True)
        da = dh * (1.0 + scale)
        dg_ref[...] += jnp.sum(da * xn, axis=0, keepdims=True)
        dxn = da * g
        dx = r * (dxn - xn * jnp.mean(dxn * xn, axis=-1, keepdims=True))
        gx_ref[...] = dout_ref[...] + dx

    row = pl.BlockSpec((bs, D_MODEL), lambda i: (i, 0))
    vec = pl.BlockSpec((1, D_MODEL), lambda i: (0, 0))
    return pl.pallas_call(
        body, grid=(seq // bs,), name="prenorm_bwd",
        in_specs=[row, row, row, pl.BlockSpec((1, 3 * D_MODEL), lambda i: (0, 0)), vec],
        out_specs=[row, vec, vec, vec],
        out_shape=[jax.ShapeDtypeStruct((seq, D_MODEL), F32)] + [jax.ShapeDtypeStruct((1, D_MODEL), F32)] * 3,
        compiler_params=_cp("arbitrary"))(x, dh, dout, mod, g_pre)


def _dep_args(dep, grid_rank):
    if dep is None:
        return [], []
    return [dep], [pl.BlockSpec(dep.shape, lambda *_: (0,) * dep.ndim)]


def _matmul(a, b, *, ta, tb, tm, tn, out_dtype, name, dep=None, b_rows=None):
    kdim = a.shape[0] if ta else a.shape[1]
    m = a.shape[1] if ta else a.shape[0]
    n = b.shape[0] if tb else b.shape[1]
    if b_rows is not None:
        assert tb
        n = b_rows[1]
    assert m % tm == 0 and n % tn == 0
    deps, dep_specs = _dep_args(dep, 2)

    def body(a_ref, b_ref, *rest):
        o_ref = rest[-1]
        dims = (((0 if ta else 1,), (1 if tb else 0,)), ((), ()))
        o_ref[...] = lax.dot_general(a_ref[...], b_ref[...], dims, preferred_element_type=F32).astype(o_ref.dtype)

    a_spec = pl.BlockSpec((kdim, tm), lambda j, i: (0, i)) if ta else pl.BlockSpec((tm, kdim), lambda j, i: (i, 0))
    b_spec = pl.BlockSpec((tn, kdim), lambda j, i: (j, 0)) if tb else pl.BlockSpec((kdim, tn), lambda j, i: (0, j))
    if b_rows is not None:
        assert b_rows[0] % 128 == 0 and tn % 128 == 0
        b_spec = pl.BlockSpec((pl.Element(tn), pl.Element(kdim)), lambda j, i: (pl.multiple_of(b_rows[0] + j * tn, 128), 0))
    return pl.pallas_call(
        body, grid=(n // tn, m // tm), name=name, in_specs=[a_spec, b_spec] + dep_specs,
        out_specs=pl.BlockSpec((tm, tn), lambda j, i: (i, j)),
        out_shape=jax.ShapeDtypeStruct((m, n), out_dtype), compiler_params=_cp("parallel", "parallel"))(a, b, *deps)


def _prep_attn(p_a, cos, sin, qg2, kg2):
    seq = p_a.shape[0]
    bs = 512

    def body(p_ref, cos_ref, sin_ref, qg_ref, kg_ref, qt_ref, kd_ref, vd_ref):
        m = _blockdiag64()
        cs, sn = cos_ref[...], sin_ref[...]
        lo = _lane((bs, 128)) < 64
        for pr in range(4):
            q = p_ref[:, QA + 128 * pr:QA + 128 * (pr + 1)]
            r = lax.rsqrt(_seg64(q * q, m) * (1.0 / 64) + EPS)
            qt_ref[:, 128 * pr:128 * (pr + 1)] = (_rope(q * r * qg_ref[...], cs, sn) * (0.125 * LOG2E)).astype(qt_ref.dtype)
        k = p_ref[:, KA:KA + 128]
        r = lax.rsqrt(_seg64(k * k, m) * (1.0 / 64) + EPS)
        kr = _rope(k * r * kg_ref[...], cs, sn)
        ksw = pltpu.roll(kr, 64, 1)
        kd_ref[0] = jnp.where(lo, kr, ksw).astype(kd_ref.dtype)
        kd_ref[1] = jnp.where(lo, ksw, kr).astype(kd_ref.dtype)
        v = p_ref[:, VA:VA + 128]
        vsw = pltpu.roll(v, 64, 1)
        vd_ref[0] = jnp.where(lo, v, vsw).astype(vd_ref.dtype)
        vd_ref[1] = jnp.where(lo, vsw, v).astype(vd_ref.dtype)

    tab = pl.BlockSpec((bs, 128), lambda i: (i, 0))
    gsp = pl.BlockSpec((1, 128), lambda i: (0, 0))
    dup = pl.BlockSpec((2, bs, 128), lambda i: (0, i, 0))
    return pl.pallas_call(
        body, grid=(seq // bs,), name="attn_prep",
        in_specs=[pl.BlockSpec((bs, ATTN_COLS), lambda i: (i, 0)), tab, tab, gsp, gsp],
        out_specs=[pl.BlockSpec((bs, 512), lambda i: (i, 0)), dup, dup],
        out_shape=[jax.ShapeDtypeStruct((seq, 512), MXU_DTYPE), jax.ShapeDtypeStruct((2, seq, 128), MXU_DTYPE),
                   jax.ShapeDtypeStruct((2, seq, 128), MXU_DTYPE)],
        compiler_params=_cp("parallel"))(p_a, cos, sin, qg2, kg2)


def _prep_ret(p_b, cos, sin):
    seq = p_b.shape[0]
    bs = 512

    def body(p_ref, cos_ref, sin_ref, rq_ref, rk_ref, rv_ref):
        cs, sn = cos_ref[...], sin_ref[...]
        for pr in range(2):
            sl = slice(128 * pr, 128 * (pr + 1))
            rq_ref[:, sl] = _rope(p_ref[:, QR_B + 128 * pr:QR_B + 128 * (pr + 1)], cs, sn).astype(rq_ref.dtype)
            rk_ref[:, sl] = (_rope(p_ref[:, KR_B + 128 * pr:KR_B + 128 * (pr + 1)], cs, sn) * 0.125).astype(rk_ref.dtype)
        rv_ref[...] = p_ref[:, VR_B:VR_B + 512].astype(rv_ref.dtype)

    tab = pl.BlockSpec((bs, 128), lambda i: (i, 0))
    return pl.pallas_call(
        body, grid=(seq // bs,), name="ret_prep",
        in_specs=[pl.BlockSpec((bs, ZR_B), lambda i: (i, 0)), tab, tab],
        out_specs=[pl.BlockSpec((bs, 256), lambda i: (i, 0)), pl.BlockSpec((bs, 256), lambda i: (i, 0)),
                   pl.BlockSpec((bs, 512), lambda i: (i, 0))],
        out_shape=[jax.ShapeDtypeStruct((seq, 256), MXU_DTYPE), jax.ShapeDtypeStruct((seq, 256), MXU_DTYPE),
                   jax.ShapeDtypeStruct((seq, 512), MXU_DTYPE)],
        compiler_params=_cp("parallel"))(p_b, cos, sin)


def _halves(kd):
    lo = _lane(kd.shape) < 64
    z = jnp.zeros_like(kd)
    return jnp.concatenate([jnp.where(lo, kd, z), jnp.where(lo, z, kd)], axis=0)


def _attn_fwd(qt, kd, vd, dep=None):
    seq = qt.shape[0]
    bq, bk = min(1024, seq), min(1024, seq)
    nk = seq // bk
    deps, dep_specs = _dep_args(dep, 2)

    def body(q_ref, k_ref, v_ref, *rest):
        o_ref, lse_ref, m_scr, l_scr, acc = rest[-5:]
        j = pl.program_id(1)
        m_scr[...] = jnp.full_like(m_scr, -jnp.inf)
        l_scr[...] = jnp.zeros_like(l_scr)
        acc[...] = jnp.zeros_like(acc)
        lo = _lane((bq, 128)) < 64

        def kv_block(n, carry):
            rows = pl.ds(pl.multiple_of(n * bk, bk), bk)
            k2, v2 = _halves(k_ref[rows, :]), _halves(v_ref[rows, :])
            for pr in range(2):
                s2 = _nt(q_ref[:, 128 * pr:128 * (pr + 1)], k2)
                ps, alphas = [], []
                for e in range(2):
                    g = 2 * pr + e
                    s = s2[:, e * bk:(e + 1) * bk]
                    m_prev = m_scr[g]
                    m_next = jnp.maximum(m_prev, jnp.max(s, axis=1, keepdims=True))
                    alpha = jnp.exp2(m_prev - m_next)
                    pe = jnp.exp2(s - jnp.tile(m_next, (1, bk // 128)))
                    l_scr[g] = alpha * l_scr[g] + jnp.sum(pe, axis=1, keepdims=True)
                    m_scr[g] = m_next
                    ps.append(_mx(pe))
                    alphas.append(alpha)
                o2 = _nn(jnp.concatenate(ps, axis=1), v2)
                sl = slice(128 * pr, 128 * (pr + 1))
                acc[:, sl] = acc[:, sl] * jnp.where(lo, alphas[0], alphas[1]) + o2
            return carry

        lax.fori_loop(0, nk, kv_block, 0)
        for pr in range(2):
            sl = slice(128 * pr, 128 * (pr + 1))
            o_ref[:, sl] = acc[:, sl] / jnp.where(lo, l_scr[2 * pr], l_scr[2 * pr + 1])
        for g in range(4):
            lse = jnp.transpose(m_scr[g] + jnp.log2(l_scr[g]))
            lse_ref[pl.ds(4 * j + g, 1), :] = lse[0:1, :]

    return pl.pallas_call(
        body, grid=(seq // bq, 2), name="attn_fwd",
        in_specs=[pl.BlockSpec((bq, 256), lambda i, j: (i, j)), pl.BlockSpec((None, seq, 128), lambda i, j: (j, 0, 0)),
                  pl.BlockSpec((None, seq, 128), lambda i, j: (j, 0, 0))] + dep_specs,
        out_specs=[pl.BlockSpec((bq, 256), lambda i, j: (i, j)), pl.BlockSpec((8, bq), lambda i, j: (0, i))],
        out_shape=[jax.ShapeDtypeStruct((seq, 512), F32), jax.ShapeDtypeStruct((8, seq), F32)],
        scratch_shapes=[pltpu.VMEM((4, bq, 128), F32), pltpu.VMEM((4, bq, 128), F32), pltpu.VMEM((bq, 256), F32)],
        compiler_params=_cp("parallel", "arbitrary"))(qt, kd, vd, *deps)


def _attn_bwd(qt, kd, vd, do, lse, delta, dep=None):
    seq = qt.shape[0]
    bq, bk = min(1024, seq), min(1024, seq)
    nq, nk = seq // bq, seq // bk
    deps, dep_specs = _dep_args(dep, 3)

    def body(q_ref, do_ref, k_ref, v_ref, lse_ref, del_ref, *rest):
        dq_ref, dk_ref, dv_ref = rest[-3:]
        j, i = pl.program_id(0), pl.program_id(1)
        dq_ref[...] = jnp.zeros_like(dq_ref)

        @pl.when(i == 0)
        def _():
            dk_ref[...] = jnp.zeros_like(dk_ref)
            dv_ref[...] = jnp.zeros_like(dv_ref)

        lo = _lane((bk, 128)) < 64
        big = lambda r: jnp.concatenate([jnp.broadcast_to(r[0], (bk, bq)), jnp.broadcast_to(r[1], (bk, bq))], axis=0)

        def kv_block(n, carry):
            rows = pl.ds(pl.multiple_of(n * bk, bk), bk)
            k2, v2 = _halves(k_ref[rows, :]), _halves(v_ref[rows, :])
            for pr in range(2):
                sl = slice(128 * pr, 128 * (pr + 1))
                qp, dop = q_ref[:, sl], do_ref[:, sl]
                s_t = _nt(k2, qp)
                dp_t = _nt(v2, dop)
                ls = [lse_ref[pl.ds(4 * j + 2 * pr + e, 1), :] for e in range(2)]
                dl = [del_ref[pl.ds(4 * j + 2 * pr + e, 1), :] for e in range(2)]
                p_t = jnp.exp2(s_t - big(ls))
                ds_t = _mx(p_t * (dp_t - big(dl)))
                rv = _nn(_mx(p_t), dop)
                rk = _nn(ds_t, qp) * LN2
                dv_ref[rows, :] += jnp.where(lo, rv[:bk], rv[bk:])
                dk_ref[rows, :] += jnp.where(lo, rk[:bk], rk[bk:])
                dq_ref[:, sl] += _tn(ds_t, k2)
            return carry

        lax.fori_loop(0, nk, kv_block, 0)

    return pl.pallas_call(
        body, grid=(2, nq), name="attn_bwd",
        in_specs=[pl.BlockSpec((bq, 256), lambda j, i: (i, j)), pl.BlockSpec((bq, 256), lambda j, i: (i, j)),
                  pl.BlockSpec((None, seq, 128), lambda j, i: (j, 0, 0)), pl.BlockSpec((None, seq, 128), lambda j, i: (j, 0, 0)),
                  pl.BlockSpec((8, bq), lambda j, i: (0, i)), pl.BlockSpec((8, bq), lambda j, i: (0, i))] + dep_specs,
        out_specs=[pl.BlockSpec((bq, 256), lambda j, i: (i, j)), pl.BlockSpec((None, seq, 128), lambda j, i: (j, 0, 0)),
                   pl.BlockSpec((None, seq, 128), lambda j, i: (j, 0, 0))],
        out_shape=[jax.ShapeDtypeStruct((seq, 512), F32), jax.ShapeDtypeStruct((2, seq, 128), F32),
                   jax.ShapeDtypeStruct((2, seq, 128), F32)],
        compiler_params=_cp("arbitrary", "arbitrary"))(qt, do, kd, vd, lse, delta, *deps)


def _ret_tables(lg_f, lg_b, c):
    idx = jnp.arange(c, dtype=F32)
    diff = idx[:, None] - idx[None, :]
    a, b = lg_f[:, None, None], lg_b[:, None, None]
    low, up = (diff >= 0)[None], (diff < 0)[None]
    dmat = jnp.where(low, jnp.exp(a * jnp.maximum(diff, 0.0)[None]), jnp.exp(b * jnp.maximum(-diff, 0.0)[None]))
    dda = jnp.where(low, diff[None] * jnp.exp(a * jnp.maximum(diff, 0.0)[None]), 0.0)
    ddb = jnp.where(up, -diff[None] * jnp.exp(b * jnp.maximum(-diff, 0.0)[None]), 0.0)
    rep = lambda w: jnp.broadcast_to(w[:, :, None], (RET_HEADS, c, 128))
    wqf = rep(jnp.exp(lg_f[:, None] * (idx + 1.0)[None]))
    wqb = rep(jnp.exp(lg_b[:, None] * (c - idx)[None]))
    wkf = rep(jnp.exp(lg_f[:, None] * (c - 1.0 - idx)[None]))
    wkb = rep(jnp.exp(lg_b[:, None] * idx[None]))
    cdf, cdb = jnp.exp(lg_f * c), jnp.exp(lg_b * c)
    dec_fb = jnp.broadcast_to(jnp.concatenate([cdf, cdb])[:, None], (8, 128))
    dec_bf = jnp.broadcast_to(jnp.concatenate([cdb, cdf])[:, None], (8, 128))
    return dict(dmat=dmat, dda=dda, ddb=ddb, wqf=wqf, wqb=wqb, wkf=wkf, wkb=wkb, dec_fb=dec_fb, dec_bf=dec_bf)


def _ret_states(xk, yv, w_fwd, w_rev, dec, name):
    seq = xk.shape[0]
    c = RET_CHUNK
    nc = seq // c

    def body(xf_ref, yf_ref, xr_ref, yr_ref, wf_ref, wr_ref, dec_ref, sf_ref, sr_ref, st_f, st_r):
        @pl.when(pl.program_id(0) == 0)
        def _():
            st_f[...] = jnp.zeros_like(st_f)
            st_r[...] = jnp.zeros_like(st_r)

        lane = _lane((c, 128))
        for h in range(RET_HEADS):
            pr, e = h // 2, h % 2
            half = (lane < 64) if e == 0 else (lane >= 64)
            for x_ref, y_ref, w_ref, s_ref, st, drow in ((xf_ref, yf_ref, wf_ref, sf_ref, st_f, h),
                                                        (xr_ref, yr_ref, wr_ref, sr_ref, st_r, 4 + h)):
                s_ref[h] = st[h].astype(s_ref.dtype)
                xm = jnp.where(half, x_ref[:, 128 * pr:128 * (pr + 1)].astype(F32) * w_ref[h], 0.0)
                st[h] = st[h] * dec_ref[drow:drow + 1, :] + _tn(_mx(xm), _mx(y_ref[:, 128 * h:128 * (h + 1)]))

    xs = lambda f: pl.BlockSpec((c, 256), f)
    ys = lambda f: pl.BlockSpec((c, 512), f)
    fwd, rev = (lambda n: (n, 0)), (lambda n: (nc - 1 - n, 0))
    wsp = pl.BlockSpec((RET_HEADS, c, 128), lambda n: (0, 0, 0))
    return pl.pallas_call(
        body, grid=(nc,), name=name,
        in_specs=[xs(fwd), ys(fwd), xs(rev), ys(rev), wsp, wsp, pl.BlockSpec((8, 128), lambda n: (0, 0))],
        out_specs=[pl.BlockSpec((None, RET_HEADS, 128, 128), lambda n: (n, 0, 0, 0)),
                   pl.BlockSpec((None, RET_HEADS, 128, 128), lambda n: (nc - 1 - n, 0, 0, 0))],
        out_shape=[jax.ShapeDtypeStruct((nc, RET_HEADS, 128, 128), MXU_DTYPE)] * 2,
        scratch_shapes=[pltpu.VMEM((RET_HEADS, 128, 128), F32), pltpu.VMEM((RET_HEADS, 128, 128), F32)],
        compiler_params=_cp("arbitrary"))(xk, yv, xk, yv, w_fwd, w_rev, dec)


def _ret_fwd(rq, rk, rv, rf, rb, tb):
    seq = rq.shape[0]
    c = RET_CHUNK

    def body(q_ref, k_ref, v_ref, rf_ref, rb_ref, d_ref, wqf_ref, wqb_ref, o_ref):
        lane = _lane((c, 128))
        for h in range(RET_HEADS):
            pr, e = h // 2, h % 2
            half = (lane < 64) if e == 0 else (lane >= 64)
            sl = slice(128 * pr, 128 * (pr + 1))
            qp = q_ref[:, sl]
            km = jnp.where(half, k_ref[:, sl], jnp.zeros_like(k_ref[:, sl]))
            sd = _mx(_nt(qp, km) * d_ref[h])
            qf = qp.astype(F32)
            o = _nn(sd, v_ref[:, 128 * h:128 * (h + 1)])
            o = o + _nn(_mx(qf * wqf_ref[h]), rf_ref[h]) + _nn(_mx(qf * wqb_ref[h]), rb_ref[h])
            o_ref[:, 128 * h:128 * (h + 1)] = o

    st = pl.BlockSpec((None, RET_HEADS, 128, 128), lambda n: (n, 0, 0, 0))
    wsp = pl.BlockSpec((RET_HEADS, c, 128), lambda n: (0, 0, 0))
    return pl.pallas_call(
        body, grid=(seq // c,), name="ret_fwd",
        in_specs=[pl.BlockSpec((c, 256), lambda n: (n, 0)), pl.BlockSpec((c, 256), lambda n: (n, 0)),
                  pl.BlockSpec((c, 512), lambda n: (n, 0)), st, st, pl.BlockSpec((RET_HEADS, c, c), lambda n: (0, 0, 0)), wsp, wsp],
        out_specs=pl.BlockSpec((c, 512), lambda n: (n, 0)),
        out_shape=jax.ShapeDtypeStruct((seq, 512), F32), compiler_params=_cp("parallel"))(
            rq, rk, rv, rf, rb, tb["dmat"], tb["wqf"], tb["wqb"])


def _ret_bwd(rq, rk, rv, do, rf, rb, hf, hb, tb):
    seq = rq.shape[0]
    c = RET_CHUNK

    def body(q_ref, k_ref, v_ref, do_ref, rf_ref, rb_ref, hf_ref, hb_ref, d_ref, dda_ref, ddb_ref,
             wqf_ref, wqb_ref, wkf_ref, wkb_ref, cdec_ref, dq_ref, dk_ref, dv_ref, dlg_ref):
        @pl.when(pl.program_id(0) == 0)
        def _():
            dlg_ref[...] = jnp.zeros_like(dlg_ref)

        lane = _lane((c, 128))
        pos = lax.broadcasted_iota(jnp.int32, (c, 128), 0).astype(F32)
        for pr in range(2):
            sl = slice(128 * pr, 128 * (pr + 1))
            qp, kp = q_ref[:, sl], k_ref[:, sl]
            qf, kf = qp.astype(F32), kp.astype(F32)
            dq_acc = jnp.zeros((c, 128), F32)
            dk_acc = jnp.zeros((c, 128), F32)
            for e in range(2):
                h = 2 * pr + e
                half = (lane < 64) if e == 0 else (lane >= 64)
                hs = slice(128 * h, 128 * (h + 1))
                km = jnp.where(half, kp, jnp.zeros_like(kp))
                kmf = km.astype(F32)
                vh, doh = v_ref[:, hs], do_ref[:, hs]
                rfh, rbh, hfh, hbh = rf_ref[h], rb_ref[h], hf_ref[h], hb_ref[h]
                s = _nt(qp, km)
                dpm = _nt(doh, vh)
                ds_b = _mx(dpm * d_ref[h])
                sd_b = _mx(s * d_ref[h])
                dq_if = wqf_ref[h] * _nt(doh, rfh)
                dq_ib = wqb_ref[h] * _nt(doh, rbh)
                dq_acc = dq_acc + _nn(ds_b, km) + dq_if + dq_ib
                dk_sf = wkf_ref[h] * _nt(vh, hfh)
                dk_sb = wkb_ref[h] * _nt(vh, hbh)
                dk_acc = dk_acc + jnp.where(half, _tn(ds_b, qp), 0.0) + dk_sf + dk_sb
                dv = _tn(sd_b, doh) + _nn(_mx(kmf * wkf_ref[h]), hfh) + _nn(_mx(kmf * wkb_ref[h]), hbh)
                dv_ref[:, hs] = dv.astype(dv_ref.dtype)
                sdp = s * dpm
                hr_f = hfh.astype(F32) * rfh.astype(F32)
                hr_b = hbh.astype(F32) * rbh.astype(F32)
                da = (_rowsum128(sdp * dda_ref[h]) + _rowsum128((pos + 1.0) * qf * dq_if)
                      + _rowsum128((c - 1.0 - pos) * kf * dk_sf) + cdec_ref[h:h + 1, :] * _rowsum128(hr_f))
                db = (_rowsum128(sdp * ddb_ref[h]) + _rowsum128((c - pos) * qf * dq_ib)
                      + _rowsum128(pos * kf * dk_sb) + cdec_ref[4 + h:5 + h, :] * _rowsum128(hr_b))
                dlg_ref[h:h + 1, :] += da
                dlg_ref[4 + h:5 + h, :] += db
            dq_ref[:, sl] = dq_acc
            dk_ref[:, sl] = dk_acc

    st = pl.BlockSpec((None, RET_HEADS, 128, 128), lambda n: (n, 0, 0, 0))
    wsp = pl.BlockSpec((RET_HEADS, c, 128), lambda n: (0, 0, 0))
    dsp = pl.BlockSpec((RET_HEADS, c, c), lambda n: (0, 0, 0))
    x256 = pl.BlockSpec((c, 256), lambda n: (n, 0))
    x512 = pl.BlockSpec((c, 512), lambda n: (n, 0))
    cdec = tb["dec_fb"] * float(c)
    return pl.pallas_call(
        body, grid=(seq // c,), name="ret_bwd",
        in_specs=[x256, x256, x512, x512, st, st, st, st, dsp, dsp, dsp, wsp, wsp, wsp, wsp,
                  pl.BlockSpec((8, 128), lambda n: (0, 0))],
        out_specs=[x256, x256, x512, pl.BlockSpec((8, 128), lambda n: (0, 0))],
        out_shape=[jax.ShapeDtypeStruct((seq, 256), F32), jax.ShapeDtypeStruct((seq, 256), F32),
                   jax.ShapeDtypeStruct((seq, 512), MXU_DTYPE), jax.ShapeDtypeStruct((8, 128), F32)],
        compiler_params=_cp("arbitrary"))(
            rq, rk, rv, do, rf, rb, hf, hb, tb["dmat"], tb["dda"], tb["ddb"], tb["wqf"], tb["wqb"], tb["wkf"], tb["wkb"], cdec)


def _tail(x, tgt, o_att, o_ret, p, mod, g_post, gn_g, wpt, wout):
    seq = x.shape[0]
    bs = 256
    nb = seq // bs

    def body(x_ref, t_ref, oa_ref, or_ref, za_ref, zr_ref, gl_ref, mod_ref, gp_ref, gn_ref, wpt_ref, wout_ref,
             dout_ref, dza_ref, dzr_ref, dgl_ref, doa_ref, dor_ref, del_ref, gout_ref, gpt_ref,
             dgate_ref, dgpost_ref, dgn_ref, loss_ref, gout_acc, gpt_acc):
        i = pl.program_id(0)

        @pl.when(i == 0)
        def _():
            gout_acc[...] = jnp.zeros_like(gout_acc)
            gpt_acc[...] = jnp.zeros_like(gpt_acc)
            dgate_ref[...] = jnp.zeros_like(dgate_ref)
            dgpost_ref[...] = jnp.zeros_like(dgpost_ref)
            dgn_ref[...] = jnp.zeros_like(dgn_ref)
            loss_ref[...] = jnp.zeros_like(loss_ref)

        oa, za, zr = oa_ref[...], za_ref[...], zr_ref[...]
        sga, sgr = _sigmoid(za), _sigmoid(zr)
        sza, szr = za * sga, zr * sgr
        ya_b = _mx(oa * sza)
        ons, rstds = [], []
        for h in range(RET_HEADS):
            oh = or_ref[:, 128 * h:128 * (h + 1)]
            xc = oh - jnp.mean(oh, axis=-1, keepdims=True)
            rstd = lax.rsqrt(jnp.mean(xc * xc, axis=-1, keepdims=True) + EPS)
            ons.append(xc * rstd)
            rstds.append(rstd)
        on = jnp.concatenate(ons, axis=1)
        yn = on * gn_ref[...]
        yr_b = _mx(yn * szr)
        wpa_t, wpr_t = wpt_ref[:, 0:512], wpt_ref[:, 512:1024]
        a_att = _nt(ya_b, wpa_t)
        a_ret = _nt(yr_b, wpr_t)
        gts = _sigmoid(gl_ref[...])
        g_att, g_ret = gts[:, 0:D_MODEL], gts[:, D_MODEL:2 * D_MODEL]
        merged_b = _mx(g_att * a_att + g_ret * a_ret)
        u = _nn(merged_b, wout_ref[...])
        r = lax.rsqrt(jnp.mean(u * u, axis=-1, keepdims=True) + EPS)
        un = u * r
        gpost = gp_ref[...]
        y = un * gpost
        gate = mod_ref[:, 2 * D_MODEL:3 * D_MODEL]
        err = x_ref[...] + gate * y - t_ref[...]
        loss_ref[...] += (0.5 / D_MODEL) * _rowsum128(err * err)
        dout = err * (1.0 / D_MODEL)
        dout_ref[...] = dout
        dgate_ref[...] += jnp.sum(dout * y, axis=0, keepdims=True)
        dy = dout * gate
        dgpost_ref[...] += jnp.sum(dy * un, axis=0, keepdims=True)
        dun = dy * gpost
        du_b = _mx(r * (dun - un * jnp.mean(dun * un, axis=-1, keepdims=True)))
        dmerged = _nt(du_b, wout_ref[...])
        gout_acc[...] += _tn(merged_b, du_b)
        d_att, d_ret = dmerged * g_att, dmerged * g_ret
        dgl_ref[:, 0:D_MODEL] = (d_att * a_att * (1.0 - g_att)).astype(dgl_ref.dtype)
        dgl_ref[:, D_MODEL:2 * D_MODEL] = (d_ret * a_ret * (1.0 - g_ret)).astype(dgl_ref.dtype)
        d_att_b, d_ret_b = _mx(d_att), _mx(d_ret)
        dya = _nn(d_att_b, wpa_t)
        dyr = _nn(d_ret_b, wpr_t)
        gpt_acc[:, 0:512] += _tn(d_att_b, ya_b)
        gpt_acc[:, 512:1024] += _tn(d_ret_b, yr_b)
        dza_ref[...] = (dya * oa * (sga * (1.0 + za * (1.0 - sga)))).astype(dza_ref.dtype)
        doa = dya * sza
        doa_ref[...] = doa.astype(doa_ref.dtype)
        dt = jnp.transpose(doa * oa)
        del_ref[...] = jnp.sum(dt.reshape(8, HEAD_DIM, bs), axis=1)
        dzr_ref[...] = (dyr * yn * (sgr * (1.0 + zr * (1.0 - sgr)))).astype(dzr_ref.dtype)
        dyn = dyr * szr
        dgn_ref[...] += jnp.sum(dyn * on, axis=0, keepdims=True)
        don = dyn * gn_ref[...]
        for h in range(RET_HEADS):
            hs = slice(128 * h, 128 * (h + 1))
            dh, oh = don[:, hs], ons[h]
            doh = rstds[h] * (dh - jnp.mean(dh, axis=-1, keepdims=True) - oh * jnp.mean(dh * oh, axis=-1, keepdims=True))
            dor_ref[:, hs] = doh.astype(dor_ref.dtype)

        @pl.when(i == nb - 1)
        def _():
            gout_ref[...] = gout_acc[...].astype(gout_ref.dtype)
            gpt_ref[...] = gpt_acc[...].astype(gpt_ref.dtype)

    row = lambda w: pl.BlockSpec((bs, w), lambda i: (i, 0))
    el = lambda w, off: pl.BlockSpec((pl.Element(bs), pl.Element(w)), lambda i: (i * bs, off))
    vec = lambda w: pl.BlockSpec((1, w), lambda i: (0, 0))
    full = pl.BlockSpec((D_MODEL, D_MODEL), lambda i: (0, 0))
    sds = jax.ShapeDtypeStruct
    return pl.pallas_call(
        body, grid=(nb,), name="tail",
        in_specs=[row(D_MODEL), row(D_MODEL), row(512), row(512), el(512, ZA_B), el(512, ZR_B), el(2 * D_MODEL, GL_B),
                  vec(3 * D_MODEL), vec(D_MODEL), vec(512), full, full],
        out_specs=[row(D_MODEL), row(512), row(512), row(2 * D_MODEL), row(512), row(512),
                   pl.BlockSpec((8, bs), lambda i: (0, i)), full, full, vec(D_MODEL), vec(D_MODEL), vec(512), vec(128)],
        out_shape=[sds((seq, D_MODEL), F32), sds((seq, 512), MXU_DTYPE), sds((seq, 512), MXU_DTYPE),
                   sds((seq, 2 * D_MODEL), MXU_DTYPE), sds((seq, 512), MXU_DTYPE), sds((seq, 512), MXU_DTYPE),
                   sds((8, seq), F32), sds((D_MODEL, D_MODEL), MXU_DTYPE), sds((D_MODEL, D_MODEL), MXU_DTYPE),
                   sds((1, D_MODEL), F32), sds((1, D_MODEL), F32), sds((1, 512), F32), sds((1, 128), F32)],
        scratch_shapes=[pltpu.VMEM((D_MODEL, D_MODEL), F32), pltpu.VMEM((D_MODEL, D_MODEL), F32)],
        compiler_params=_cp("arbitrary"))(x, tgt, o_att, o_ret, p, p, p, mod, g_post, gn_g, wpt, wout)


def _assemble(p, cos, sin, qg2, kg2, dqt, dkp, dvp, dza, drq, drk, drv, dzr, dgl):
    seq = p.shape[0]
    bs = 512

    def body(p_ref, cos_ref, sin_ref, qg_ref, kg_ref, dqt_ref, dkp_ref, dvp_ref, dza_ref, drq_ref, drk_ref, drv_ref,
             dzr_ref, dgl_ref, dp_ref, dqg_ref, dkg_ref):
        @pl.when(pl.program_id(0) == 0)
        def _():
            dqg_ref[...] = jnp.zeros_like(dqg_ref)
            dkg_ref[...] = jnp.zeros_like(dkg_ref)

        m = _blockdiag64()
        cs, sn = cos_ref[...], sin_ref[...]
        lo = _lane((bs, 128)) < 64

        def norm_bwd(raw, dn, g, dg_ref):
            r = lax.rsqrt(_seg64(raw * raw, m) * (1.0 / 64) + EPS)
            xn = raw * r
            dg_ref[...] += jnp.sum(dn * xn, axis=0, keepdims=True)
            dxn = dn * g
            return r * (dxn - xn * (_seg64(dxn * xn, m) * (1.0 / 64)))

        for pr in range(4):
            sl = slice(128 * pr, 128 * (pr + 1))
            dn = _rope_t(dqt_ref[:, sl] * 0.125, cs, sn)
            dp_ref[:, QA + 128 * pr:QA + 128 * (pr + 1)] = norm_bwd(p_ref[:, sl], dn, qg_ref[...], dqg_ref).astype(dp_ref.dtype)
        fold = lambda a: a + pltpu.roll(a, 64, 1)
        dk = jnp.where(lo, fold(dkp_ref[0]), fold(dkp_ref[1]))
        dp_ref[:, KA:KA + 128] = norm_bwd(p_ref[:, KA:KA + 128], _rope_t(dk, cs, sn), kg_ref[...], dkg_ref).astype(dp_ref.dtype)
        dp_ref[:, VA:VA + 128] = jnp.where(lo, fold(dvp_ref[0]), fold(dvp_ref[1])).astype(dp_ref.dtype)
        dp_ref[:, ZA:ZA + 512] = dza_ref[...]
        for pr in range(2):
            sl = slice(128 * pr, 128 * (pr + 1))
            dp_ref[:, QR + 128 * pr:QR + 128 * (pr + 1)] = _rope_t(drq_ref[:, sl], cs, sn).astype(dp_ref.dtype)
            dp_ref[:, KR + 128 * pr:KR + 128 * (pr + 1)] = _rope_t(drk_ref[:, sl] * 0.125, cs, sn).astype(dp_ref.dtype)
        dp_ref[:, VR:VR + 512] = drv_ref[...]
        dp_ref[:, ZR:ZR + 512] = dzr_ref[...]
        dp_ref[:, GL:GL + 2 * D_MODEL] = dgl_ref[...]

    row = lambda w: pl.BlockSpec((bs, w), lambda i: (i, 0))
    gsp = pl.BlockSpec((1, 128), lambda i: (0, 0))
    dup = pl.BlockSpec((2, bs, 128), lambda i: (0, i, 0))
    return pl.pallas_call(
        body, grid=(seq // bs,), name="assemble_dp",
        in_specs=[row(768), row(128), row(128), gsp, gsp, row(512), dup, dup, row(512), row(256), row(256), row(512),
                  row(512), row(2 * D_MODEL)],
        out_specs=[row(IN_WIDTH), gsp, gsp],
        out_shape=[jax.ShapeDtypeStruct((seq, IN_WIDTH), MXU_DTYPE), jax.ShapeDtypeStruct((1, 128), F32),
                   jax.ShapeDtypeStruct((1, 128), F32)],
        compiler_params=_cp("arbitrary"))(p, cos, sin, qg2, kg2, dqt, dkp, dvp, dza, drq, drk, drv, dzr, dgl)


def _local_step(x, tgt, mod, g_pre, qn_g, kn_g, w_dec_f, w_dec_b, gn_g, g_post, w_attn, rest_start, rest_wait, send=None):
    send = send or (lambda name, arrays: None)
    seq = x.shape[0]
    cos, sin = _rope_tables(seq)
    qg2, kg2 = jnp.tile(qn_g, (1, 2)), jnp.tile(kn_g, (1, 2))
    lg_f = jax.nn.log_sigmoid(w_dec_f[0])
    lg_b = jax.nn.log_sigmoid(w_dec_b[0])
    tb = _ret_tables(lg_f, lg_b, RET_CHUNK)

    h = _prenorm(x, mod, g_pre)
    p_a = _matmul(h, w_attn, ta=False, tb=True, tm=min(1024, seq), tn=ATTN_COLS, out_dtype=F32, name="in_proj_attn")
    qt, kd, vd = _prep_attn(p_a, cos, sin, qg2, kg2)
    o_att, lse = _attn_fwd(qt, kd, vd, dep=rest_start(qt))
    win_t, wpt, wout = rest_wait(o_att)
    p_b = _matmul(h, win_t, ta=False, tb=True, tm=min(1024, seq), tn=(IN_WIDTH - ATTN_COLS) // 2, out_dtype=F32, name="in_proj_rest",
                  b_rows=(ATTN_COLS, IN_WIDTH - ATTN_COLS))
    rq, rk, rv = _prep_ret(p_b, cos, sin)
    rf, rb = _ret_states(rk, rv, tb["wkf"], tb["wkb"], tb["dec_fb"], "ret_states_fwd")
    o_ret = _ret_fwd(rq, rk, rv, rf, rb, tb)
    (dout, dza, dzr, dgl, do_att, do_ret, delta, g_out, g_pt, dgate, dgpost, dgn, loss_l) = _tail(
        x, tgt, o_att, o_ret, p_b, mod, g_post, gn_g, wpt, wout)
    dep = send("early", (g_pt, g_out))
    dqt, dkp, dvp = _attn_bwd(qt, kd, vd, do_att, lse, delta, dep=dep)
    hb, hf = _ret_states(rq, do_ret, tb["wqb"], tb["wqf"], tb["dec_bf"], "ret_states_bwd")
    drq, drk, drv, dlg = _ret_bwd(rq, rk, rv, do_ret, rf, rb, hf, hb, tb)
    dp, dqg, dkg = _assemble(p_a, cos, sin, qg2, kg2, dqt, dkp, dvp, dza, drq, drk, drv, dzr, dgl)
    g_in_t = _matmul(dp, h, ta=True, tb=False, tm=256, tn=D_MODEL, out_dtype=MXU_DTYPE, name="in_proj_dw")
    dep = send("late", (g_in_t,))
    dh = _matmul(dp, win_t, ta=False, tb=False, tm=512, tn=D_MODEL, out_dtype=F32, name="in_proj_dx", dep=dep)
    grad_x, dshift, dscale, dgpre = _prenorm_bwd(x, dh, dout, mod, g_pre)

    dlg = jnp.sum(dlg, axis=1)
    small = dict(
        dmod=jnp.concatenate([dshift, dscale, dgate], axis=1),
        g_pre=dgpre, g_post=dgpost, gn_g=dgn,
        qn_g=dqg[:, :64] + dqg[:, 64:], kn_g=dkg[:, :64] + dkg[:, 64:],
        w_dec_f=(dlg[0:4] * jax.nn.sigmoid(-w_dec_f[0]))[None], w_dec_b=(dlg[4:8] * jax.nn.sigmoid(-w_dec_b[0]))[None],
        loss=jnp.sum(loss_l, axis=1, keepdims=True))
    return grad_x, g_in_t, g_pt, g_out, small


N_DEV = 8
N_CHIP = 4
HBM_SPEC = pl.BlockSpec(memory_space=pl.ANY)
VMEM_SPEC = pl.BlockSpec(memory_space=pltpu.VMEM)
SHARD_ROWS = (IN_WIDTH // N_CHIP, D_MODEL // N_CHIP, D_MODEL // N_CHIP)


def _coords():
    return lax.axis_index("x"), lax.axis_index("y"), lax.axis_index("c")


def _peer(k):
    x, y, c = _coords()
    return (1 - x if k & 4 else x, 1 - y if k & 2 else y, 1 - c if k & 1 else c)


def _dev_index(p):
    return 4 * p[0] + 2 * p[1] + p[2]


def _rows(ref, start, size):
    return ref.at[pl.ds(pl.multiple_of(start, 16), size), :]


def _remote(src, dst, ssem, rsem, dev):
    return pltpu.make_async_remote_copy(src_ref=src, dst_ref=dst, send_sem=ssem, recv_sem=rsem, device_id=dev,
                                        device_id_type=MESH)


ATTN_CHUNK = 96


def _mod_and_attn_rows(c, w_ada_s, b4, win_s):
    ncol = w_ada_s.shape[1]
    half = ATTN_COLS // 2
    offs = list(range(0, half, ATTN_CHUNK))
    nq = len(offs)
    rows = lambda ref, cc, off: ref.at[pl.ds(pl.multiple_of(cc * half + off, 16), ATTN_CHUNK), :]

    def body(c_ref, w_ref, b_ref, src, cs_ref, mod_ref, out, modsh, modall, ssem, rsem, lsem, asem, bsem, fsem, gsem):
        x, y, cc = _coords()
        me = _dev_index((x, y, cc))
        chip = me // 2
        sib = (x, y, 1 - cc)
        cs_ref[me] = c_ref[...]
        sends = []
        for k in range(1, N_DEV):
            cp = _remote(c_ref, cs_ref.at[me], ssem.at[k - 1], rsem.at[k - 1], _peer(k))
            cp.start()
            sends.append(cp)
        own = pltpu.make_async_copy(src.at[pl.ds(0, ATTN_COLS), :], out, lsem.at[0])
        bulk = [_remote(rows(src, cc, off), rows(out, cc, off), asem.at[(k2 - 1) * nq + q], bsem.at[q], (k2 // 2, k2 % 2, cc))
                for k2 in range(1, N_CHIP) for q, off in enumerate(offs)]

        @pl.when(chip == 0)
        def _():
            own.start()
            for cp in bulk:
                cp.start()

        for k in range(1, N_DEV):
            slot = cs_ref.at[_dev_index(_peer(k))]
            _remote(slot, slot, ssem.at[k - 1], rsem.at[k - 1], _peer(k)).wait_recv()
        cs = jnp.concatenate([cs_ref[d] for d in range(N_DEV)], axis=0)
        ms = _nn(cs * _sigmoid(cs), w_ref[...], precision=HIGHEST) + b_ref[pl.ds(chip, 1), :]
        modsh[...] = ms
        modall[chip] = ms
        for k2 in range(1, N_CHIP):
            cp = _remote(modsh, modall.at[chip], ssem.at[6 + k2], rsem.at[6 + k2], _peer(2 * k2))
            cp.start()
            sends.append(cp)

        @pl.when(chip != 0)
        def _():
            passed = []
            for q, off in enumerate(offs):
                got = rows(out, cc, off)
                _remote(got, got, asem.at[q], bsem.at[q], (0, 0, cc)).wait_recv()
                cp = _remote(got, got, fsem.at[q], gsem.at[q], sib)
                cp.start()
                passed.append(cp)
            for q, off in enumerate(offs):
                got = rows(out, 1 - cc, off)
                _remote(got, got, fsem.at[q], gsem.at[q], sib).wait_recv()
            for cp in passed:
                cp.wait_send()

        for k2 in range(1, N_CHIP):
            slot = modall.at[_dev_index(_peer(2 * k2)) // 2]
            _remote(slot, slot, ssem.at[6 + k2], rsem.at[6 + k2], _peer(2 * k2)).wait_recv()
        for jj in range(N_CHIP):
            mod_ref[:, ncol * jj:ncol * (jj + 1)] = modall[jj, pl.ds(me, 1), :]
        for cp in sends:
            cp.wait_send()

        @pl.when(chip == 0)
        def _():
            for cp in bulk:
                cp.wait_send()
            own.wait()

    dma = pltpu.SemaphoreType.DMA
    return pl.pallas_call(
        body, name="mod_and_attn_rows", in_specs=[VMEM_SPEC] * 4, out_specs=[VMEM_SPEC, VMEM_SPEC, HBM_SPEC],
        out_shape=[jax.ShapeDtypeStruct((N_DEV, 1, D_MODEL), F32), jax.ShapeDtypeStruct((1, N_CHIP * ncol), F32),
                   jax.ShapeDtypeStruct((ATTN_COLS, win_s.shape[1]), win_s.dtype)],
        scratch_shapes=[pltpu.VMEM((N_DEV, ncol), F32), pltpu.VMEM((N_CHIP, N_DEV, ncol), F32), dma((10,)), dma((10,)),
                        dma((1,)), dma((3 * nq,)), dma((nq,)), dma((nq,)), dma((nq,))],
        compiler_params=_cp())(c, w_ada_s, b4, win_s)


GATHER_CHUNK = 32


def _chunks(kinds, chunk):
    out = []
    for t, kind in enumerate(kinds):
        half = SHARD_ROWS[kind] // 2
        step = chunk if half % chunk == 0 else half
        out += [(t, off, step) for off in range(0, half, step)]
    return out


SEM_SPEC = pl.BlockSpec(memory_space=pltpu.SEMAPHORE)
HBM_ONLY = pl.BlockSpec(memory_space=pltpu.HBM)
DATAFLOW = pltpu.SideEffectType.DATAFLOW_SIDE_EFFECTING


def _place_own(shards):
    nt = len(shards)

    def body(*refs):
        srcs, outs, lsem = refs[:nt], refs[nt:2 * nt], refs[2 * nt]
        x, y, _ = _coords()
        chip = 2 * x + y
        local = [pltpu.make_async_copy(srcs[t], _rows(outs[t], chip * SHARD_ROWS[t], SHARD_ROWS[t]), lsem.at[t]) for t in range(nt)]
        for cp in local:
            cp.start()
        for cp in local:
            cp.wait()

    return pl.pallas_call(
        body, name="place_own_shard", in_specs=[VMEM_SPEC] * nt, out_specs=[HBM_SPEC] * nt,
        out_shape=[jax.ShapeDtypeStruct((N_CHIP * SHARD_ROWS[t], s.shape[1]), s.dtype) for t, s in enumerate(shards)],
        scratch_shapes=[pltpu.SemaphoreType.DMA((nt,))], compiler_params=_cp())(*shards)


def _gather_rest_start(shards, zones, dep):
    n = len(shards)

    def body(*refs):
        srcs, lands = refs[:n], refs[n:2 * n]
        ssem, rsem = refs[2 * n + 1], refs[2 * n + 2]
        token = refs[-1]
        x, y, _ = _coords()
        chip = 2 * x + y
        for k2 in range(1, N_CHIP):
            for t in range(n):
                q = (k2 - 1) * n + t
                _remote(srcs[t], _rows(lands[t], chip * SHARD_ROWS[t], SHARD_ROWS[t]), ssem.at[q], rsem.at[q], _peer(2 * k2)).start()
        token[...] = jnp.zeros_like(token)

    hbm = lambda a: pltpu.with_memory_space_constraint(a, pltpu.HBM)
    dma = pltpu.SemaphoreType.DMA
    outs = pl.pallas_call(
        body, name="gather_rest", in_specs=[HBM_ONLY] * (2 * n) + [HBM_SPEC],
        out_specs=[SEM_SPEC, SEM_SPEC] + [HBM_ONLY] * (2 * n) + [VMEM_SPEC],
        out_shape=[dma((3 * n,)), dma((3 * n,))] + [pltpu.HBM(a.shape, a.dtype) for a in list(shards) + list(zones)]
        + [jax.ShapeDtypeStruct((8, 128), F32)],
        input_output_aliases={i: 2 + i for i in range(2 * n)},
        compiler_params=pltpu.CompilerParams(has_side_effects=DATAFLOW))(*[hbm(a) for a in list(shards) + list(zones)], dep)
    return outs[0], outs[1], outs[2:2 + n], outs[2 + n:2 + 2 * n], outs[-1]


def _gather_rest_wait(started, after):
    ssem, rsem, shards, zones, _ = started
    n = len(shards)

    def body(*refs):
        srcs, lands = refs[:n], refs[n:2 * n]
        ssem_ref, rsem_ref = refs[2 * n], refs[2 * n + 1]
        for k2 in range(1, N_CHIP):
            pchip = _dev_index(_peer(2 * k2)) // 2
            for t in range(n):
                q = (k2 - 1) * n + t
                cp = _remote(srcs[t], _rows(lands[t], pchip * SHARD_ROWS[t], SHARD_ROWS[t]), ssem_ref.at[q], rsem_ref.at[q], _peer(2 * k2))
                cp.wait_send()
                cp.wait_recv()

    outs = pl.pallas_call(
        body, name="gather_rest_wait", in_specs=[HBM_ONLY] * (2 * n) + [SEM_SPEC, SEM_SPEC, HBM_SPEC], out_specs=[HBM_ONLY] * (2 * n),
        out_shape=[pltpu.HBM(a.shape, a.dtype) for a in list(shards) + list(zones)],
        input_output_aliases={i: i for i in range(2 * n)},
        compiler_params=pltpu.CompilerParams(has_side_effects=DATAFLOW))(*shards, *zones, ssem, rsem, after)
    return outs[n:]


def _reduced_by(ref, t, dev):
    return _rows(ref, (dev // 2) * SHARD_ROWS[t] + (dev % 2) * (SHARD_ROWS[t] // 2), SHARD_ROWS[t] // 2)


def _scatter_start(grads, kinds, name):
    n = len(grads)

    def body(*refs):
        srcs, lands = refs[:n], refs[n:2 * n]
        ssem, rsem = refs[2 * n], refs[2 * n + 1]
        token = refs[-1]
        me = _dev_index(_coords())
        for k in range(1, N_DEV):
            for i, t in enumerate(kinds):
                q = (k - 1) * n + i
                _remote(_reduced_by(srcs[i], t, _dev_index(_peer(k))), lands[i].at[me], ssem.at[q], rsem.at[q], _peer(k)).start()
        token[...] = jnp.zeros_like(token)

    zones = [lax.empty((N_DEV, SHARD_ROWS[t] // 2, g.shape[1]), g.dtype) for g, t in zip(grads, kinds)]
    hbm = lambda a: pltpu.with_memory_space_constraint(a, pltpu.HBM)
    dma = pltpu.SemaphoreType.DMA
    outs = pl.pallas_call(
        body, name=name, in_specs=[HBM_ONLY] * (2 * n), out_specs=[SEM_SPEC, SEM_SPEC] + [HBM_ONLY] * (2 * n) + [VMEM_SPEC],
        out_shape=[dma((7 * n,)), dma((7 * n,))] + [pltpu.HBM(a.shape, a.dtype) for a in list(grads) + zones]
        + [jax.ShapeDtypeStruct((8, 128), F32)],
        input_output_aliases={i: 2 + i for i in range(2 * n)},
        compiler_params=pltpu.CompilerParams(has_side_effects=DATAFLOW))(*[hbm(a) for a in list(grads) + zones])
    return outs[0], outs[1], outs[2:2 + n], outs[2 + n:2 + 2 * n], outs[-1]


def _scatter_wait(started, kinds, after, name):
    ssem, rsem, grads, zones, _ = started
    n = len(grads)

    def body(*refs):
        srcs, lands = refs[:n], refs[n:2 * n]
        ssem_ref, rsem_ref = refs[2 * n], refs[2 * n + 1]
        for k in range(1, N_DEV):
            peer = _dev_index(_peer(k))
            for i, t in enumerate(kinds):
                q = (k - 1) * n + i
                cp = _remote(_reduced_by(srcs[i], t, peer), lands[i].at[peer], ssem_ref.at[q], rsem_ref.at[q], _peer(k))
                cp.wait_send()
                cp.wait_recv()

    outs = pl.pallas_call(
        body, name=name, in_specs=[HBM_ONLY] * (2 * n) + [SEM_SPEC, SEM_SPEC, HBM_SPEC], out_specs=[HBM_ONLY] * (2 * n),
        out_shape=[pltpu.HBM(a.shape, a.dtype) for a in list(grads) + list(zones)],
        input_output_aliases={i: i for i in range(2 * n)},
        compiler_params=pltpu.CompilerParams(has_side_effects=DATAFLOW))(*grads, *zones, ssem, rsem, after)
    return outs[:n], outs[n:]


def _grad_finish(grads, zones, kinds, name):
    nt = len(grads)
    pieces = _chunks(kinds, GATHER_CHUNK)
    npc = len(pieces)
    shard = [SHARD_ROWS[k] for k in kinds]

    def body(*refs):
        srcs, lands, outs = refs[:nt], refs[nt:2 * nt], refs[2 * nt:3 * nt]
        slots, sums = refs[3 * nt:4 * nt], refs[4 * nt:5 * nt]
        lsem, osem, xsem, ysem = refs[5 * nt:]
        x, y, c = _coords()
        me = _dev_index((x, y, c))
        sib = (x, y, 1 - c)
        loads = [pltpu.make_async_copy(_reduced_by(srcs[t], kinds[t], me), slots[t].at[me], lsem.at[t]) for t in range(nt)]
        for k in range(1, N_DEV):
            peer = _dev_index(_peer(k))
            loads += [pltpu.make_async_copy(lands[t].at[peer], slots[t].at[peer], lsem.at[k * nt + t]) for t in range(nt)]
        for cp in loads:
            cp.start()
        for cp in loads:
            cp.wait()
        sends = []
        place = lambda t, cc, off, n: _rows(outs[t], cc * (shard[t] // 2) + off, n)
        stores = []
        for q, (t, off, n) in enumerate(pieces):
            r = pl.ds(off, n)
            acc = slots[t][0, r, :].astype(F32)
            for d in range(1, N_DEV):
                acc = acc + slots[t][d, r, :].astype(F32)
            sums[t][r, :] = acc
            keep = pltpu.make_async_copy(sums[t].at[r, :], place(t, c, off, n), osem.at[q])
            give = _remote(sums[t].at[r, :], place(t, c, off, n), xsem.at[q], ysem.at[q], sib)
            keep.start()
            give.start()
            stores.append(keep)
            sends.append(give)
        for q, (t, off, n) in enumerate(pieces):
            got = place(t, 1 - c, off, n)
            _remote(got, got, xsem.at[q], ysem.at[q], sib).wait_recv()
        for cp in sends:
            cp.wait_send()
        for cp in stores:
            cp.wait()

    dma = pltpu.SemaphoreType.DMA
    return pl.pallas_call(
        body, name=name, in_specs=[HBM_SPEC] * (2 * nt), out_specs=[HBM_SPEC] * nt,
        out_shape=[jax.ShapeDtypeStruct((shard[t], g.shape[1]), F32) for t, g in enumerate(grads)],
        scratch_shapes=([pltpu.VMEM((N_DEV, shard[t] // 2, g.shape[1]), g.dtype) for t, g in enumerate(grads)]
                        + [pltpu.VMEM((shard[t] // 2, g.shape[1]), F32) for t, g in enumerate(grads)]
                        + [dma((N_DEV * nt,)), dma((npc,)), dma((npc,)), dma((npc,))]),
        compiler_params=_cp())(*grads, *zones)


def _adam_math(w, g, m, v):
    m = ADAM_B1 * m + (1.0 - ADAM_B1) * g
    v = ADAM_B2 * v + (1.0 - ADAM_B2) * (g * g)
    m_hat = m / (1.0 - ADAM_B1 ** ADAM_STEP)
    v_hat = v / (1.0 - ADAM_B2 ** ADAM_STEP)
    return -ADAM_LR * (m_hat / (jnp.sqrt(v_hat) + ADAM_EPS) + ADAM_WD * w), m, v


def _adamw(w, g, m, v, rb, name):
    rows, cols = w.shape

    def body(w_ref, g_ref, m_ref, v_ref, d_ref, nm_ref, nv_ref):
        d_ref[...], nm_ref[...], nv_ref[...] = _adam_math(w_ref[...], g_ref[...], m_ref[...], v_ref[...])

    spec = pl.BlockSpec((rb, cols), lambda i: (i, 0))
    return pl.pallas_call(body, grid=(rows // rb,), name=name, in_specs=[spec] * 4, out_specs=[spec] * 3,
                          out_shape=[jax.ShapeDtypeStruct(w.shape, F32)] * 3, compiler_params=_cp("parallel"))(w, g, m, v)


PACK = (("b_ada", 3 * D_MODEL), ("g_pre", D_MODEL), ("g_post", D_MODEL), ("gn_g", 512), ("qn_g", 64), ("kn_g", 64),
        ("w_dec_f", 4), ("w_dec_b", 4), ("loss", 1))
PACK_OFFSETS = {}
PACK_WIDTH = 0
for _name, _n in PACK:
    PACK_OFFSETS[_name] = PACK_WIDTH
    PACK_WIDTH += -(-_n // 128) * 128
SMALL_PARAMS = tuple(name for name, _ in PACK if name != "loss")


def _pack(parts):
    cols = []
    for name, n in PACK:
        pad = -(-n // 128) * 128 - n
        cols.append(parts[name].reshape(1, n).astype(F32))
        if pad:
            cols.append(jnp.zeros((1, pad), F32))
    return jnp.concatenate(cols, axis=1)


def _small_reduce(vec, cs, dep):
    ncol = 3 * D_MODEL // N_CHIP

    def body(vec_ref, cs_ref, dep_ref, tot_ref, gwa_ref, gat, ssem, rsem):
        me = _dev_index(_coords())
        chip = me // 2
        gat[me] = vec_ref[...]
        sends = []
        for k in range(1, N_DEV):
            cp = _remote(vec_ref, gat.at[me], ssem.at[k - 1], rsem.at[k - 1], _peer(k))
            cp.start()
            sends.append(cp)
        for k in range(1, N_DEV):
            slot = gat.at[_dev_index(_peer(k))]
            _remote(slot, slot, ssem.at[k - 1], rsem.at[k - 1], _peer(k)).wait_recv()
        rows = [gat[d] for d in range(N_DEV)]
        tot = rows[0]
        for d in range(1, N_DEV):
            tot = tot + rows[d]
        tot_ref[...] = tot
        cs = cs_ref[...]
        ca_t = jnp.transpose(jnp.concatenate([cs * _sigmoid(cs), jnp.zeros((128 - N_DEV, D_MODEL), F32)], axis=0))
        dm = jnp.concatenate(rows + [jnp.zeros((128 - N_DEV, PACK_WIDTH), F32)], axis=0)
        for jj in range(N_CHIP):
            @pl.when(chip == jj)
            def _():
                gwa_ref[...] = _nn(ca_t, dm[:, ncol * jj:ncol * (jj + 1)], precision=HIGHEST)
        for cp in sends:
            cp.wait_send()

    return pl.pallas_call(
        body, name="small_reduce", in_specs=[VMEM_SPEC, VMEM_SPEC, HBM_SPEC], out_specs=[VMEM_SPEC] * 2,
        out_shape=[jax.ShapeDtypeStruct((1, PACK_WIDTH), F32), jax.ShapeDtypeStruct((D_MODEL, ncol), F32)],
        scratch_shapes=[pltpu.VMEM((N_DEV, 1, PACK_WIDTH), F32), pltpu.SemaphoreType.DMA((7,)), pltpu.SemaphoreType.DMA((7,))],
        compiler_params=_cp())(vec, cs, dep)


def _small_adamw(tot, given):
    sizes = dict(PACK)
    np_ = len(SMALL_PARAMS)

    def body(*refs):
        tot_ref = refs[0]
        wmv = refs[1:1 + 3 * np_]
        outs = refs[1 + 3 * np_:1 + 7 * np_]
        loss_ref = refs[-1]
        for i, name in enumerate(SMALL_PARAMS):
            off, n = PACK_OFFSETS[name], sizes[name]
            g = tot_ref[:, off:off + n]
            w_ref, m_ref, v_ref = wmv[3 * i:3 * i + 3]
            outs[i][...] = g
            outs[np_ + i][...], outs[2 * np_ + i][...], outs[3 * np_ + i][...] = _adam_math(w_ref[...], g, m_ref[...], v_ref[...])
        loss_ref[...] = tot_ref[:, PACK_OFFSETS["loss"]:PACK_OFFSETS["loss"] + 1]

    flat = [a for name in SMALL_PARAMS for a in given[name]]
    shapes = [jax.ShapeDtypeStruct((1, sizes[name]), F32) for name in SMALL_PARAMS]
    res = pl.pallas_call(body, name="small_adamw", in_specs=[VMEM_SPEC] * (1 + 3 * np_), out_specs=[VMEM_SPEC] * (4 * np_ + 1),
                         out_shape=shapes * 4 + [jax.ShapeDtypeStruct((1, 1), F32)], compiler_params=_cp())(tot, *flat)
    return [dict(zip(SMALL_PARAMS, res[k * np_:(k + 1) * np_])) for k in range(4)], res[-1]


def kernel(x, c, w_ada, b_ada, g_pre, w_in, qn_g, kn_g, w_dec_f, w_dec_b, gn_g, w_pa, w_pr, w_out, g_post, loss_target, m_w_ada, m_b_ada, m_g_pre, m_w_in, m_qn_g, m_kn_g, m_w_dec_f, m_w_dec_b, m_gn_g, m_w_pa, m_w_pr, m_w_out, m_g_post, v_w_ada, v_b_ada, v_g_pre, v_w_in, v_qn_g, v_kn_g, v_w_dec_f, v_w_dec_b, v_gn_g, v_w_pa, v_w_pr, v_w_out, v_g_post):
    shards = (w_in[0].T.astype(MXU_DTYPE), jnp.concatenate([w_pa[0].T, w_pr[0].T], axis=1).astype(MXU_DTYPE),
              w_out[0].astype(MXU_DTYPE))
    cs, mod, w_attn = _mod_and_attn_rows(c, w_ada[0], b_ada.reshape(N_CHIP, 3 * D_MODEL // N_CHIP), shards[0])
    started = {}

    def rest_start(dep):
        started["rest"] = _gather_rest_start(shards, _place_own(shards), dep)
        return started["rest"][-1]

    def rest_wait(after):
        return _gather_rest_wait(started["rest"], after)

    kinds = {"early": (1, 2), "late": (0,)}

    def send(name, arrays):
        started[name] = _scatter_start(arrays, kinds[name], "grad_scatter_" + name)
        return started[name][-1]

    grad_x, _, _, _, small = _local_step(x[0], loss_target[0], mod, g_pre, qn_g, kn_g, w_dec_f, w_dec_b,
                                         gn_g, g_post, w_attn, rest_start, rest_wait, send=send)
    small = dict(small)
    small["b_ada"] = small.pop("dmod")
    given = dict(b_ada=(b_ada, m_b_ada, v_b_ada), g_pre=(g_pre, m_g_pre, v_g_pre), g_post=(g_post, m_g_post, v_g_post),
                 gn_g=(gn_g, m_gn_g, v_gn_g), qn_g=(qn_g, m_qn_g, v_qn_g), kn_g=(kn_g, m_kn_g, v_kn_g),
                 w_dec_f=(w_dec_f, m_w_dec_f, v_w_dec_f), w_dec_b=(w_dec_b, m_w_dec_b, v_w_dec_b))
    grads, deltas, new_m, new_v = {}, {}, {}, {}

    def update(name, w, g, m, v, back):
        d, nm, nv = _adamw(w, g, m, v, w.shape[0] // 4, "adamw_" + name)
        grads[name], deltas[name], new_m[name], new_v[name] = back(g), back(d), back(nm), back(nv)
        return d

    lead = lambda a: a[None]
    (g_pt, g_out), (z_pt, z_out) = _scatter_wait(started["early"], kinds["early"], grad_x, "grad_scatter_early_wait")
    r_pt, r_out = _grad_finish((g_pt, g_out), (z_pt, z_out), kinds["early"], "grad_finish_early")
    update("w_pa", w_pa[0], r_pt[:, :512].T, m_w_pa[0], v_w_pa[0], lead)
    update("w_pr", w_pr[0], r_pt[:, 512:].T, m_w_pr[0], v_w_pr[0], lead)
    last = update("w_out", w_out[0], r_out, m_w_out[0], v_w_out[0], lead)
    tot, g_w_ada = _small_reduce(_pack(small), cs.reshape(N_DEV, D_MODEL), last)
    small_parts, loss = _small_adamw(tot, given)
    for dst, part in zip((grads, deltas, new_m, new_v), small_parts):
        dst.update(part)
    last = update("w_ada", w_ada[0], g_w_ada, m_w_ada[0], v_w_ada[0], lead)

    (g_in_t,), (z_in,) = _scatter_wait(started["late"], kinds["late"], last, "grad_scatter_late_wait")
    (r_in,) = _grad_finish((g_in_t,), (z_in,), kinds["late"], "grad_finish_late")
    tr = lambda a: a[0].T
    update("w_in", tr(w_in), r_in, tr(m_w_in), tr(v_w_in), lambda a: a.T[None])
    order = ("w_ada", "b_ada", "g_pre", "w_in", "qn_g", "kn_g", "w_dec_f", "w_dec_b", "gn_g", "w_pa", "w_pr", "w_out", "g_post")
    return (loss[0, 0], grad_x[None], *[grads[n] for n in order], *[deltas[n] for n in order],
            *[new_m[n] for n in order], *[new_v[n] for n in order])
```

```python
import functools

import jax
import jax.numpy as jnp
import numpy as np
from jax import lax
from jax.experimental import pallas as pl
from jax.experimental.pallas import tpu as pltpu

F32 = jnp.float32
BF16 = jnp.bfloat16
MXU_DTYPE = jnp.bfloat16

D_MODEL = 1024
GRID_W = 64
HEAD_DIM = 64
ROPE_THETA = 10000.0
EPS = 1e-6
RET_HEADS = 4
RET_CHUNK = 256
IN_WIDTH = 4864
QA, KA, VA, ZA, QR, KR, VR, ZR, GL = 0, 512, 640, 768, 1280, 1536, 1792, 2304, 2816
ATTN_COLS = ZA
ZA_B, QR_B, KR_B, VR_B, ZR_B, GL_B = (c - ATTN_COLS for c in (ZA, QR, KR, VR, ZR, GL))

ADAM_LR, ADAM_B1, ADAM_B2, ADAM_EPS, ADAM_WD, ADAM_STEP = 0.001, 0.9, 0.999, 1e-08, 0.01, 10

VMEM_LIMIT = 56 * 1024 * 1024
MESH = pl.DeviceIdType.MESH
HIGHEST = lax.Precision.HIGHEST
LOG2E = 1.4426950408889634
LN2 = 0.6931471805599453


def _cp(*sem, **kw):
    if sem:
        kw["dimension_semantics"] = sem
    return pltpu.CompilerParams(vmem_limit_bytes=VMEM_LIMIT, **kw)


def _nn(a, b, **kw):
    return lax.dot_general(a, b, (((1,), (0,)), ((), ())), preferred_element_type=F32, **kw)


def _nt(a, b):
    return lax.dot_general(a, b, (((1,), (1,)), ((), ())), preferred_element_type=F32)


def _tn(a, b):
    return lax.dot_general(a, b, (((0,), (0,)), ((), ())), preferred_element_type=F32)


def _mx(x):
    return x.astype(MXU_DTYPE)


def _lane(shape):
    return lax.broadcasted_iota(jnp.int32, shape, 1)


def _sigmoid(z):
    return 1.0 / (1.0 + jnp.exp(-z))


def _rowsum128(x):
    s = jnp.sum(x, axis=0, keepdims=True)
    out = s[:, 0:128]
    for k in range(1, x.shape[1] // 128):
        out = out + s[:, 128 * k:128 * (k + 1)]
    return out


def _swap16(x):
    return jnp.where((_lane(x.shape) & 16) == 0, pltpu.roll(x, 112, 1), pltpu.roll(x, 16, 1))


def _rope(x, cos, sin):
    return x * cos + _swap16(x) * sin


def _rope_t(d, cos, sin):
    return d * cos + _swap16(d * sin)


def _blockdiag64():
    r = lax.broadcasted_iota(jnp.int32, (128, 128), 0) // 64
    c = lax.broadcasted_iota(jnp.int32, (128, 128), 1) // 64
    return (r == c).astype(BF16)


def _seg64(x, m):
    hi = x.astype(BF16)
    lo = (x - hi.astype(F32)).astype(BF16)
    return _nn(hi, m) + _nn(lo, m)


def _rope_tables(seq):
    t = np.arange(seq)
    row = (t // GRID_W).astype(np.float32)
    col = (t % GRID_W).astype(np.float32)
    half = HEAD_DIM // 2
    inv_freq = (np.float32(ROPE_THETA) ** (-np.arange(0, half, 2, dtype=np.float32) / np.float32(half))).astype(np.float32)
    ar = (row[:, None] * inv_freq[None, :]).astype(np.float32).astype(np.float64)
    ac = (col[:, None] * inv_freq[None, :]).astype(np.float32).astype(np.float64)
    cr, sr, cc, sc = np.cos(ar), np.sin(ar), np.cos(ac), np.sin(ac)
    cos64 = np.concatenate([cr, cr, cc, cc], axis=1)
    sin64 = np.concatenate([-sr, sr, -sc, sc], axis=1)
    return jnp.asarray(np.tile(cos64, (1, 2)), F32), jnp.asarray(np.tile(sin64, (1, 2)), F32)


def _attn_inputs(x, mod, g_pre, w_attn, cos, sin, qg2, kg2):
    seq = x.shape[0]
    bs = 512

    def body(x_ref, mod_ref, g_ref, w_ref, cos_ref, sin_ref, qg_ref, kg_ref, h_ref, pa_ref, qt_ref, kd_ref, vd_ref):
        xv = x_ref[...]
        r = lax.rsqrt(jnp.mean(xv * xv, axis=-1, keepdims=True) + EPS)
        shift = mod_ref[:, 0:D_MODEL]
        scale = mod_ref[:, D_MODEL:2 * D_MODEL]
        hb = ((xv * r) * g_ref[...] * (1.0 + scale) + shift).astype(h_ref.dtype)
        h_ref[...] = hb
        p = _nt(hb, w_ref[...])
        pa_ref[...] = p
        m = _blockdiag64()
        cs, sn = cos_ref[...], sin_ref[...]
        lo = _lane((bs, 128)) < 64
        for pr in range(4):
            q = p[:, QA + 128 * pr:QA + 128 * (pr + 1)]
            r = lax.rsqrt(_seg64(q * q, m) * (1.0 / 64) + EPS)
            qt_ref[:, 128 * pr:128 * (pr + 1)] = (_rope(q * r * qg_ref[...], cs, sn) * (0.125 * LOG2E)).astype(qt_ref.dtype)
        k = p[:, KA:KA + 128]
        r = lax.rsqrt(_seg64(k * k, m) * (1.0 / 64) + EPS)
        kr = _rope(k * r * kg_ref[...], cs, sn)
        ksw = pltpu.roll(kr, 64, 1)
        kd_ref[0] = jnp.where(lo, kr, ksw).astype(kd_ref.dtype)
        kd_ref[1] = jnp.where(lo, ksw, kr).astype(kd_ref.dtype)
        v = p[:, VA:VA + 128]
        vsw = pltpu.roll(v, 64, 1)
        vd_ref[0] = jnp.where(lo, v, vsw).astype(vd_ref.dtype)
        vd_ref[1] = jnp.where(lo, vsw, v).astype(vd_ref.dtype)

    row = lambda w: pl.BlockSpec((bs, w), lambda i: (i, 0))
    fix = lambda a, b: pl.BlockSpec((a, b), lambda i: (0, 0))
    dup = pl.BlockSpec((2, bs, 128), lambda i: (0, i, 0))
    sds = jax.ShapeDtypeStruct
    return pl.pallas_call(
        body, grid=(seq // bs,), name="attn_inputs",
        in_specs=[row(D_MODEL), fix(1, 3 * D_MODEL), fix(1, D_MODEL), fix(ATTN_COLS, D_MODEL), row(128), row(128), fix(1, 128), fix(1, 128)],
        out_specs=[row(D_MODEL), row(ATTN_COLS), row(512), dup, dup],
        out_shape=[sds((seq, D_MODEL), MXU_DTYPE), sds((seq, ATTN_COLS), F32), sds((seq, 512), MXU_DTYPE),
                   sds((2, seq, 128), MXU_DTYPE), sds((2, seq, 128), MXU_DTYPE)],
        compiler_params=_cp("parallel"))(x, mod, g_pre, w_attn, cos, sin, qg2, kg2)


def _in_proj_dx(x, dp, win_t, dout, mod, g_pre, dep=None):
    seq = x.shape[0]
    bs = 512
    deps, dep_specs = _dep_args(dep, 1)

    def body(x_ref, dp_ref, w_ref, dout_ref, mod_ref, g_ref, *rest):
        gx_ref, dshift_ref, dscale_ref, dg_ref = rest[-4:]

        @pl.when(pl.program_id(0) == 0)
        def _():
            dshift_ref[...] = jnp.zeros_like(dshift_ref)
            dscale_ref[...] = jnp.zeros_like(dscale_ref)
            dg_ref[...] = jnp.zeros_like(dg_ref)

        xv = x_ref[...]
        dh = _nn(dp_ref[...], w_ref[...])
        g = g_ref[...]
        r = lax.rsqrt(jnp.mean(xv * xv, axis=-1, keepdims=True) + EPS)
        xn = xv * r
        scale = mod_ref[:, D_MODEL:2 * D_MODEL]
        dshift_ref[...] += jnp.sum(dh, axis=0, keepdims=True)
        dscale_ref[...] += jnp.sum(dh * (xn * g), axis=0, keepdims=True)
        da = dh * (1.0 + scale)
        dg_ref[...] += jnp.sum(da * xn, axis=0, keepdims=True)
        dxn = da * g
        dx = r * (dxn - xn * jnp.mean(dxn * xn, axis=-1, keepdims=True))
        gx_ref[...] = dout_ref[...] + dx

    row = pl.BlockSpec((bs, D_MODEL), lambda i: (i, 0))
    vec = pl.BlockSpec((1, D_MODEL), lambda i: (0, 0))
    return pl.pallas_call(
        body, grid=(seq // bs,), name="in_proj_dx",
        in_specs=[row, pl.BlockSpec((bs, IN_WIDTH), lambda i: (i, 0)), pl.BlockSpec((IN_WIDTH, D_MODEL), lambda i: (0, 0)), row,
                  pl.BlockSpec((1, 3 * D_MODEL), lambda i: (0, 0)), vec] + dep_specs,
        out_specs=[row, vec, vec, vec],
        out_shape=[jax.ShapeDtypeStruct((seq, D_MODEL), F32)] + [jax.ShapeDtypeStruct((1, D_MODEL), F32)] * 3,
        compiler_params=_cp("arbitrary"))(x, dp, win_t, dout, mod, g_pre, *deps)


def _dep_args(dep, grid_rank):
    if dep is None:
        return [], []
    return [dep], [pl.BlockSpec(dep.shape, lambda *_: (0,) * dep.ndim)]


def _matmul(a, b, *, ta, tb, tm, tn, out_dtype, name, dep=None):
    kdim = a.shape[0] if ta else a.shape[1]
    m = a.shape[1] if ta else a.shape[0]
    n = b.shape[0] if tb else b.shape[1]
    assert m % tm == 0 and n % tn == 0
    deps, dep_specs = _dep_args(dep, 2)

    def body(a_ref, b_ref, *rest):
        o_ref = rest[-1]
        dims = (((0 if ta else 1,), (1 if tb else 0,)), ((), ()))
        o_ref[...] = lax.dot_general(a_ref[...], b_ref[...], dims, preferred_element_type=F32).astype(o_ref.dtype)

    a_spec = pl.BlockSpec((kdim, tm), lambda j, i: (0, i)) if ta else pl.BlockSpec((tm, kdim), lambda j, i: (i, 0))
    b_spec = pl.BlockSpec((tn, kdim), lambda j, i: (j, 0)) if tb else pl.BlockSpec((kdim, tn), lambda j, i: (0, j))
    return pl.pallas_call(
        body, grid=(n // tn, m // tm), name=name, in_specs=[a_spec, b_spec] + dep_specs,
        out_specs=pl.BlockSpec((tm, tn), lambda j, i: (i, j)),
        out_shape=jax.ShapeDtypeStruct((m, n), out_dtype), compiler_params=_cp("parallel", "parallel"))(a, b, *deps)


def _in_proj_rest(h, win_t, cos, sin):
    seq = h.shape[0]
    tm = min(1024, seq)
    ncols = IN_WIDTH - ATTN_COLS
    tn = ncols // 2
    assert ZR_B <= tn and ATTN_COLS % 128 == 0 and tn % 128 == 0

    def body(h_ref, w_ref, cos_ref, sin_ref, p_ref, rq_ref, rk_ref, rv_ref):
        p = _nt(h_ref[...], w_ref[...])
        p_ref[...] = p

        @pl.when(pl.program_id(1) == 0)
        def _():
            cs, sn = cos_ref[...], sin_ref[...]
            for pr in range(2):
                sl = slice(128 * pr, 128 * (pr + 1))
                rq_ref[:, sl] = _rope(p[:, QR_B + 128 * pr:QR_B + 128 * (pr + 1)], cs, sn).astype(rq_ref.dtype)
                rk_ref[:, sl] = (_rope(p[:, KR_B + 128 * pr:KR_B + 128 * (pr + 1)], cs, sn) * 0.125).astype(rk_ref.dtype)
            rv_ref[...] = p[:, VR_B:VR_B + 512].astype(rv_ref.dtype)

    row = lambda w: pl.BlockSpec((tm, w), lambda i, j: (i, 0))
    sds = jax.ShapeDtypeStruct
    return pl.pallas_call(
        body, grid=(seq // tm, 2), name="in_proj_rest",
        in_specs=[row(D_MODEL), pl.BlockSpec((pl.Element(tn), pl.Element(D_MODEL)),
                                             lambda i, j: (pl.multiple_of(ATTN_COLS + j * tn, 128), 0)), row(128), row(128)],
        out_specs=[pl.BlockSpec((tm, tn), lambda i, j: (i, j)), row(256), row(256), row(512)],
        out_shape=[sds((seq, ncols), F32), sds((seq, 256), MXU_DTYPE), sds((seq, 256), MXU_DTYPE), sds((seq, 512), MXU_DTYPE)],
        compiler_params=_cp("parallel", "arbitrary"))(h, win_t, cos, sin)


def _halves(kd):
    lo = _lane(kd.shape) < 64
    z = jnp.zeros_like(kd)
    return jnp.concatenate([jnp.where(lo, kd, z), jnp.where(lo, z, kd)], axis=0)


def _attn_fwd(qt, kd, vd, dep=None):
    seq = qt.shape[0]
    bq, bk = min(1024, seq), min(1024, seq)
    nk = seq // bk
    deps, dep_specs = _dep_args(dep, 2)

    def body(q_ref, k_ref, v_ref, *rest):
        o_ref, lse_ref, m_scr, l_scr, acc = rest[-5:]
        j = pl.program_id(1)
        m_scr[...] = jnp.full_like(m_scr, -jnp.inf)
        l_scr[...] = jnp.zeros_like(l_scr)
        acc[...] = jnp.zeros_like(acc)
        lo = _lane((bq, 128)) < 64

        def kv_block(n, carry):
            rows = pl.ds(pl.multiple_of(n * bk, bk), bk)
            k2, v2 = _halves(k_ref[rows, :]), _halves(v_ref[rows, :])
            for pr in range(2):
                s2 = _nt(q_ref[:, 128 * pr:128 * (pr + 1)], k2)
                ps, alphas = [], []
                for e in range(2):
                    g = 2 * pr + e
                    s = s2[:, e * bk:(e + 1) * bk]
                    m_prev = m_scr[g]
                    m_next = jnp.maximum(m_prev, jnp.max(s, axis=1, keepdims=True))
                    alpha = jnp.exp2(m_prev - m_next)
                    pe = jnp.exp2(s - jnp.tile(m_next, (1, bk // 128)))
                    l_scr[g] = alpha * l_scr[g] + jnp.sum(pe, axis=1, keepdims=True)
                    m_scr[g] = m_next
                    ps.append(_mx(pe))
                    alphas.append(alpha)
                o2 = _nn(jnp.concatenate(ps, axis=1), v2)
                sl = slice(128 * pr, 128 * (pr + 1))
                acc[:, sl] = acc[:, sl] * jnp.where(lo, alphas[0], alphas[1]) + o2
            return carry

        lax.fori_loop(0, nk, kv_block, 0)
        for pr in range(2):
            sl = slice(128 * pr, 128 * (pr + 1))
            o_ref[:, sl] = acc[:, sl] / jnp.where(lo, l_scr[2 * pr], l_scr[2 * pr + 1])
        for g in range(4):
            lse = jnp.transpose(m_scr[g] + jnp.log2(l_scr[g]))
            lse_ref[pl.ds(4 * j + g, 1), :] = lse[0:1, :]

    return pl.pallas_call(
        body, grid=(seq // bq, 2), name="attn_fwd",
        in_specs=[pl.BlockSpec((bq, 256), lambda i, j: (i, j)), pl.BlockSpec((None, seq, 128), lambda i, j: (j, 0, 0)),
                  pl.BlockSpec((None, seq, 128), lambda i, j: (j, 0, 0))] + dep_specs,
        out_specs=[pl.BlockSpec((bq, 256), lambda i, j: (i, j)), pl.BlockSpec((8, bq), lambda i, j: (0, i))],
        out_shape=[jax.ShapeDtypeStruct((seq, 512), F32), jax.ShapeDtypeStruct((8, seq), F32)],
        scratch_shapes=[pltpu.VMEM((4, bq, 128), F32), pltpu.VMEM((4, bq, 128), F32), pltpu.VMEM((bq, 256), F32)],
        compiler_params=_cp("parallel", "arbitrary"))(qt, kd, vd, *deps)


def _attn_bwd(qt, kd, vd, do, lse, delta, dep=None):
    seq = qt.shape[0]
    bq, bk = min(1024, seq), min(1024, seq)
    nq, nk = seq // bq, seq // bk
    deps, dep_specs = _dep_args(dep, 3)

    def body(q_ref, do_ref, k_ref, v_ref, lse_ref, del_ref, *rest):
        dq_ref, dk_ref, dv_ref = rest[-3:]
        j, i = pl.program_id(0), pl.program_id(1)
        dq_ref[...] = jnp.zeros_like(dq_ref)

        @pl.when(i == 0)
        def _():
            dk_ref[...] = jnp.zeros_like(dk_ref)
            dv_ref[...] = jnp.zeros_like(dv_ref)

        lo = _lane((bk, 128)) < 64
        big = lambda r: jnp.concatenate([jnp.broadcast_to(r[0], (bk, bq)), jnp.broadcast_to(r[1], (bk, bq))], axis=0)

        def kv_block(n, carry):
            rows = pl.ds(pl.multiple_of(n * bk, bk), bk)
            k2, v2 = _halves(k_ref[rows, :]), _halves(v_ref[rows, :])
            for pr in range(2):
                sl = slice(128 * pr, 128 * (pr + 1))
                qp, dop = q_ref[:, sl], do_ref[:, sl]
                s_t = _nt(k2, qp)
                dp_t = _nt(v2, dop)
                ls = [lse_ref[pl.ds(4 * j + 2 * pr + e, 1), :] for e in range(2)]
                dl = [del_ref[pl.ds(4 * j + 2 * pr + e, 1), :] for e in range(2)]
                p_t = jnp.exp2(s_t - big(ls))
                ds_t = _mx(p_t * (dp_t - big(dl)))
                rv = _nn(_mx(p_t), dop)
                rk = _nn(ds_t, qp) * LN2
                dv_ref[rows, :] += jnp.where(lo, rv[:bk], rv[bk:])
                dk_ref[rows, :] += jnp.where(lo, rk[:bk], rk[bk:])
                dq_ref[:, sl] += _tn(ds_t, k2)
            return carry

        lax.fori_loop(0, nk, kv_block, 0)

    return pl.pallas_call(
        body, grid=(2, nq), name="attn_bwd",
        in_specs=[pl.BlockSpec((bq, 256), lambda j, i: (i, j)), pl.BlockSpec((bq, 256), lambda j, i: (i, j)),
                  pl.BlockSpec((None, seq, 128), lambda j, i: (j, 0, 0)), pl.BlockSpec((None, seq, 128), lambda j, i: (j, 0, 0)),
                  pl.BlockSpec((8, bq), lambda j, i: (0, i)), pl.BlockSpec((8, bq), lambda j, i: (0, i))] + dep_specs,
        out_specs=[pl.BlockSpec((bq, 256), lambda j, i: (i, j)), pl.BlockSpec((None, seq, 128), lambda j, i: (j, 0, 0)),
                   pl.BlockSpec((None, seq, 128), lambda j, i: (j, 0, 0))],
        out_shape=[jax.ShapeDtypeStruct((seq, 512), F32), jax.ShapeDtypeStruct((2, seq, 128), F32),
                   jax.ShapeDtypeStruct((2, seq, 128), F32)],
        compiler_params=_cp("arbitrary", "arbitrary"))(qt, do, kd, vd, lse, delta, *deps)


def _ret_tables(lg_f, lg_b, c):
    idx = jnp.arange(c, dtype=F32)
    diff = idx[:, None] - idx[None, :]
    a, b = lg_f[:, None, None], lg_b[:, None, None]
    low, up = (diff >= 0)[None], (diff < 0)[None]
    dmat = jnp.where(low, jnp.exp(a * jnp.maximum(diff, 0.0)[None]), jnp.exp(b * jnp.maximum(-diff, 0.0)[None]))
    dda = jnp.where(low, diff[None] * jnp.exp(a * jnp.maximum(diff, 0.0)[None]), 0.0)
    ddb = jnp.where(up, -diff[None] * jnp.exp(b * jnp.maximum(-diff, 0.0)[None]), 0.0)
    rep = lambda w: jnp.broadcast_to(w[:, :, None], (RET_HEADS, c, 128))
    wqf = rep(jnp.exp(lg_f[:, None] * (idx + 1.0)[None]))
    wqb = rep(jnp.exp(lg_b[:, None] * (c - idx)[None]))
    wkf = rep(jnp.exp(lg_f[:, None] * (c - 1.0 - idx)[None]))
    wkb = rep(jnp.exp(lg_b[:, None] * idx[None]))
    cdf, cdb = jnp.exp(lg_f * c), jnp.exp(lg_b * c)
    dec_fb = jnp.broadcast_to(jnp.concatenate([cdf, cdb])[:, None], (8, 128))
    dec_bf = jnp.broadcast_to(jnp.concatenate([cdb, cdf])[:, None], (8, 128))
    return dict(dmat=dmat, dda=dda, ddb=ddb, wqf=wqf, wqb=wqb, wkf=wkf, wkb=wkb, dec_fb=dec_fb, dec_bf=dec_bf)


def _ret_states(xk, yv, w_fwd, w_rev, dec, name):
    seq = xk.shape[0]
    c = RET_CHUNK
    nc = seq // c

    def body(xf_ref, yf_ref, xr_ref, yr_ref, wf_ref, wr_ref, dec_ref, sf_ref, sr_ref, st_f, st_r):
        @pl.when(pl.program_id(0) == 0)
        def _():
            st_f[...] = jnp.zeros_like(st_f)
            st_r[...] = jnp.zeros_like(st_r)

        lane = _lane((c, 128))
        for h in range(RET_HEADS):
            pr, e = h // 2, h % 2
            half = (lane < 64) if e == 0 else (lane >= 64)
            for x_ref, y_ref, w_ref, s_ref, st, drow in ((xf_ref, yf_ref, wf_ref, sf_ref, st_f, h),
                                                        (xr_ref, yr_ref, wr_ref, sr_ref, st_r, 4 + h)):
                s_ref[h] = st[h].astype(s_ref.dtype)
                xm = jnp.where(half, x_ref[:, 128 * pr:128 * (pr + 1)].astype(F32) * w_ref[h], 0.0)
                st[h] = st[h] * dec_ref[drow:drow + 1, :] + _tn(_mx(xm), _mx(y_ref[:, 128 * h:128 * (h + 1)]))

    xs = lambda f: pl.BlockSpec((c, 256), f)
    ys = lambda f: pl.BlockSpec((c, 512), f)
    fwd, rev = (lambda n: (n, 0)), (lambda n: (nc - 1 - n, 0))
    wsp = pl.BlockSpec((RET_HEADS, c, 128), lambda n: (0, 0, 0))
    return pl.pallas_call(
        body, grid=(nc,), name=name,
        in_specs=[xs(fwd), ys(fwd), xs(rev), ys(rev), wsp, wsp, pl.BlockSpec((8, 128), lambda n: (0, 0))],
        out_specs=[pl.BlockSpec((None, RET_HEADS, 128, 128), lambda n: (n, 0, 0, 0)),
                   pl.BlockSpec((None, RET_HEADS, 128, 128), lambda n: (nc - 1 - n, 0, 0, 0))],
        out_shape=[jax.ShapeDtypeStruct((nc, RET_HEADS, 128, 128), MXU_DTYPE)] * 2,
        scratch_shapes=[pltpu.VMEM((RET_HEADS, 128, 128), F32), pltpu.VMEM((RET_HEADS, 128, 128), F32)],
        compiler_params=_cp("arbitrary"))(xk, yv, xk, yv, w_fwd, w_rev, dec)


def _ret_fwd(rq, rk, rv, rf, rb, tb):
    seq = rq.shape[0]
    c = RET_CHUNK

    def body(q_ref, k_ref, v_ref, rf_ref, rb_ref, d_ref, wqf_ref, wqb_ref, o_ref):
        lane = _lane((c, 128))
        for h in range(RET_HEADS):
            pr, e = h // 2, h % 2
            half = (lane < 64) if e == 0 else (lane >= 64)
            sl = slice(128 * pr, 128 * (pr + 1))
            qp = q_ref[:, sl]
            km = jnp.where(half, k_ref[:, sl], jnp.zeros_like(k_ref[:, sl]))
            sd = _mx(_nt(qp, km) * d_ref[h])
            qf = qp.astype(F32)
            o = _nn(sd, v_ref[:, 128 * h:128 * (h + 1)])
            o = o + _nn(_mx(qf * wqf_ref[h]), rf_ref[h]) + _nn(_mx(qf * wqb_ref[h]), rb_ref[h])
            o_ref[:, 128 * h:128 * (h + 1)] = o

    st = pl.BlockSpec((None, RET_HEADS, 128, 128), lambda n: (n, 0, 0, 0))
    wsp = pl.BlockSpec((RET_HEADS, c, 128), lambda n: (0, 0, 0))
    return pl.pallas_call(
        body, grid=(seq // c,), name="ret_fwd",
        in_specs=[pl.BlockSpec((c, 256), lambda n: (n, 0)), pl.BlockSpec((c, 256), lambda n: (n, 0)),
                  pl.BlockSpec((c, 512), lambda n: (n, 0)), st, st, pl.BlockSpec((RET_HEADS, c, c), lambda n: (0, 0, 0)), wsp, wsp],
        out_specs=pl.BlockSpec((c, 512), lambda n: (n, 0)),
        out_shape=jax.ShapeDtypeStruct((seq, 512), F32), compiler_params=_cp("parallel"))(
            rq, rk, rv, rf, rb, tb["dmat"], tb["wqf"], tb["wqb"])


def _ret_bwd(rq, rk, rv, do, rf, rb, hf, hb, tb):
    seq = rq.shape[0]
    c = RET_CHUNK

    def body(q_ref, k_ref, v_ref, do_ref, rf_ref, rb_ref, hf_ref, hb_ref, d_ref, dda_ref, ddb_ref,
             wqf_ref, wqb_ref, wkf_ref, wkb_ref, cdec_ref, dq_ref, dk_ref, dv_ref, dlg_ref):
        @pl.when(pl.program_id(0) == 0)
        def _():
            dlg_ref[...] = jnp.zeros_like(dlg_ref)

        lane = _lane((c, 128))
        pos = lax.broadcasted_iota(jnp.int32, (c, 128), 0).astype(F32)
        for pr in range(2):
            sl = slice(128 * pr, 128 * (pr + 1))
            qp, kp = q_ref[:, sl], k_ref[:, sl]
            qf, kf = qp.astype(F32), kp.astype(F32)
            dq_acc = jnp.zeros((c, 128), F32)
            dk_acc = jnp.zeros((c, 128), F32)
            for e in range(2):
                h = 2 * pr + e
                half = (lane < 64) if e == 0 else (lane >= 64)
                hs = slice(128 * h, 128 * (h + 1))
                km = jnp.where(half, kp, jnp.zeros_like(kp))
                kmf = km.astype(F32)
                vh, doh = v_ref[:, hs], do_ref[:, hs]
                rfh, rbh, hfh, hbh = rf_ref[h], rb_ref[h], hf_ref[h], hb_ref[h]
                s = _nt(qp, km)
                dpm = _nt(doh, vh)
                ds_b = _mx(dpm * d_ref[h])
                sd_b = _mx(s * d_ref[h])
                dq_if = wqf_ref[h] * _nt(doh, rfh)
                dq_ib = wqb_ref[h] * _nt(doh, rbh)
                dq_acc = dq_acc + _nn(ds_b, km) + dq_if + dq_ib
                dk_sf = wkf_ref[h] * _nt(vh, hfh)
                dk_sb = wkb_ref[h] * _nt(vh, hbh)
                dk_acc = dk_acc + jnp.where(half, _tn(ds_b, qp), 0.0) + dk_sf + dk_sb
                dv = _tn(sd_b, doh) + _nn(_mx(kmf * wkf_ref[h]), hfh) + _nn(_mx(kmf * wkb_ref[h]), hbh)
                dv_ref[:, hs] = dv.astype(dv_ref.dtype)
                sdp = s * dpm
                hr_f = hfh.astype(F32) * rfh.astype(F32)
                hr_b = hbh.astype(F32) * rbh.astype(F32)
                da = (_rowsum128(sdp * dda_ref[h]) + _rowsum128((pos + 1.0) * qf * dq_if)
                      + _rowsum128((c - 1.0 - pos) * kf * dk_sf) + cdec_ref[h:h + 1, :] * _rowsum128(hr_f))
                db = (_rowsum128(sdp * ddb_ref[h]) + _rowsum128((c - pos) * qf * dq_ib)
                      + _rowsum128(pos * kf * dk_sb) + cdec_ref[4 + h:5 + h, :] * _rowsum128(hr_b))
                dlg_ref[h:h + 1, :] += da
                dlg_ref[4 + h:5 + h, :] += db
            dq_ref[:, sl] = dq_acc
            dk_ref[:, sl] = dk_acc

    st = pl.BlockSpec((None, RET_HEADS, 128, 128), lambda n: (n, 0, 0, 0))
    wsp = pl.BlockSpec((RET_HEADS, c, 128), lambda n: (0, 0, 0))
    dsp = pl.BlockSpec((RET_HEADS, c, c), lambda n: (0, 0, 0))
    x256 = pl.BlockSpec((c, 256), lambda n: (n, 0))
    x512 = pl.BlockSpec((c, 512), lambda n: (n, 0))
    cdec = tb["dec_fb"] * float(c)
    return pl.pallas_call(
        body, grid=(seq // c,), name="ret_bwd",
        in_specs=[x256, x256, x512, x512, st, st, st, st, dsp, dsp, dsp, wsp, wsp, wsp, wsp,
                  pl.BlockSpec((8, 128), lambda n: (0, 0))],
        out_specs=[x256, x256, x512, pl.BlockSpec((8, 128), lambda n: (0, 0))],
        out_shape=[jax.ShapeDtypeStruct((seq, 256), F32), jax.ShapeDtypeStruct((seq, 256), F32),
                   jax.ShapeDtypeStruct((seq, 512), MXU_DTYPE), jax.ShapeDtypeStruct((8, 128), F32)],
        compiler_params=_cp("arbitrary"))(
            rq, rk, rv, do, rf, rb, hf, hb, tb["dmat"], tb["dda"], tb["ddb"], tb["wqf"], tb["wqb"], tb["wkf"], tb["wkb"], cdec)


def _tail(x, tgt, o_att, o_ret, p, mod, g_post, gn_g, wpt, wout):
    seq = x.shape[0]
    bs = 256
    nb = seq // bs

    def body(x_ref, t_ref, oa_ref, or_ref, za_ref, zr_ref, gl_ref, mod_ref, gp_ref, gn_ref, wpt_ref, wout_ref,
             dout_ref, dza_ref, dzr_ref, dgl_ref, doa_ref, dor_ref, del_ref, gout_ref, gpt_ref,
             dgate_ref, dgpost_ref, dgn_ref, loss_ref, gout_acc, gpt_acc):
        i = pl.program_id(0)

        @pl.when(i == 0)
        def _():
            gout_acc[...] = jnp.zeros_like(gout_acc)
            gpt_acc[...] = jnp.zeros_like(gpt_acc)
            dgate_ref[...] = jnp.zeros_like(dgate_ref)
            dgpost_ref[...] = jnp.zeros_like(dgpost_ref)
            dgn_ref[...] = jnp.zeros_like(dgn_ref)
            loss_ref[...] = jnp.zeros_like(loss_ref)

        oa, za, zr = oa_ref[...], za_ref[...], zr_ref[...]
        sga, sgr = _sigmoid(za), _sigmoid(zr)
        sza, szr = za * sga, zr * sgr
        ya_b = _mx(oa * sza)
        ons, rstds = [], []
        for h in range(RET_HEADS):
            oh = or_ref[:, 128 * h:128 * (h + 1)]
            xc = oh - jnp.mean(oh, axis=-1, keepdims=True)
            rstd = lax.rsqrt(jnp.mean(xc * xc, axis=-1, keepdims=True) + EPS)
            ons.append(xc * rstd)
            rstds.append(rstd)
        on = jnp.concatenate(ons, axis=1)
        yn = on * gn_ref[...]
        yr_b = _mx(yn * szr)
        wpa_t, wpr_t = wpt_ref[:, 0:512], wpt_ref[:, 512:1024]
        a_att = _nt(ya_b, wpa_t)
        a_ret = _nt(yr_b, wpr_t)
        gts = _sigmoid(gl_ref[...])
        g_att, g_ret = gts[:, 0:D_MODEL], gts[:, D_MODEL:2 * D_MODEL]
        merged_b = _mx(g_att * a_att + g_ret * a_ret)
        u = _nn(merged_b, wout_ref[...])
        r = lax.rsqrt(jnp.mean(u * u, axis=-1, keepdims=True) + EPS)
        un = u * r
        gpost = gp_ref[...]
        y = un * gpost
        gate = mod_ref[:, 2 * D_MODEL:3 * D_MODEL]
        err = x_ref[...] + gate * y - t_ref[...]
        loss_ref[...] += (0.5 / D_MODEL) * _rowsum128(err * err)
        dout = err * (1.0 / D_MODEL)
        dout_ref[...] = dout
        dgate_ref[...] += jnp.sum(dout * y, axis=0, keepdims=True)
        dy = dout * gate
        dgpost_ref[...] += jnp.sum(dy * un, axis=0, keepdims=True)
        dun = dy * gpost
        du_b = _mx(r * (dun - un * jnp.mean(dun * un, axis=-1, keepdims=True)))
        dmerged = _nt(du_b, wout_ref[...])
        gout_acc[...] += _tn(merged_b, du_b)
        d_att, d_ret = dmerged * g_att, dmerged * g_ret
        dgl_ref[:, 0:D_MODEL] = (d_att * a_att * (1.0 - g_att)).astype(dgl_ref.dtype)
        dgl_ref[:, D_MODEL:2 * D_MODEL] = (d_ret * a_ret * (1.0 - g_ret)).astype(dgl_ref.dtype)
        d_att_b, d_ret_b = _mx(d_att), _mx(d_ret)
        dya = _nn(d_att_b, wpa_t)
        dyr = _nn(d_ret_b, wpr_t)
        gpt_acc[:, 0:512] += _tn(d_att_b, ya_b)
        gpt_acc[:, 512:1024] += _tn(d_ret_b, yr_b)
        dza_ref[...] = (dya * oa * (sga * (1.0 + za * (1.0 - sga)))).astype(dza_ref.dtype)
        doa = dya * sza
        doa_ref[...] = doa.astype(doa_ref.dtype)
        dt = jnp.transpose(doa * oa)
        del_ref[...] = jnp.sum(dt.reshape(8, HEAD_DIM, bs), axis=1)
        dzr_ref[...] = (dyr * yn * (sgr * (1.0 + zr * (1.0 - sgr)))).astype(dzr_ref.dtype)
        dyn = dyr * szr
        dgn_ref[...] += jnp.sum(dyn * on, axis=0, keepdims=True)
        don = dyn * gn_ref[...]
        for h in range(RET_HEADS):
            hs = slice(128 * h, 128 * (h + 1))
            dh, oh = don[:, hs], ons[h]
            doh = rstds[h] * (dh - jnp.mean(dh, axis=-1, keepdims=True) - oh * jnp.mean(dh * oh, axis=-1, keepdims=True))
            dor_ref[:, hs] = doh.astype(dor_ref.dtype)

        @pl.when(i == nb - 1)
        def _():
            gout_ref[...] = gout_acc[...].astype(gout_ref.dtype)
            gpt_ref[...] = gpt_acc[...].astype(gpt_ref.dtype)

    row = lambda w: pl.BlockSpec((bs, w), lambda i: (i, 0))
    el = lambda w, off: pl.BlockSpec((pl.Element(bs), pl.Element(w)), lambda i: (i * bs, off))
    vec = lambda w: pl.BlockSpec((1, w), lambda i: (0, 0))
    full = pl.BlockSpec((D_MODEL, D_MODEL), lambda i: (0, 0))
    sds = jax.ShapeDtypeStruct
    return pl.pallas_call(
        body, grid=(nb,), name="tail",
        in_specs=[row(D_MODEL), row(D_MODEL), row(512), row(512), el(512, ZA_B), el(512, ZR_B), el(2 * D_MODEL, GL_B),
                  vec(3 * D_MODEL), vec(D_MODEL), vec(512), full, full],
        out_specs=[row(D_MODEL), row(512), row(512), row(2 * D_MODEL), row(512), row(512),
                   pl.BlockSpec((8, bs), lambda i: (0, i)), full, full, vec(D_MODEL), vec(D_MODEL), vec(512), vec(128)],
        out_shape=[sds((seq, D_MODEL), F32), sds((seq, 512), MXU_DTYPE), sds((seq, 512), MXU_DTYPE),
                   sds((seq, 2 * D_MODEL), MXU_DTYPE), sds((seq, 512), MXU_DTYPE), sds((seq, 512), MXU_DTYPE),
                   sds((8, seq), F32), sds((D_MODEL, D_MODEL), MXU_DTYPE), sds((D_MODEL, D_MODEL), MXU_DTYPE),
                   sds((1, D_MODEL), F32), sds((1, D_MODEL), F32), sds((1, 512), F32), sds((1, 128), F32)],
        scratch_shapes=[pltpu.VMEM((D_MODEL, D_MODEL), F32), pltpu.VMEM((D_MODEL, D_MODEL), F32)],
        compiler_params=_cp("arbitrary"))(x, tgt, o_att, o_ret, p, p, p, mod, g_post, gn_g, wpt, wout)


def _assemble(p, cos, sin, qg2, kg2, dqt, dkp, dvp, dza, drq, drk, drv, dzr, dgl):
    seq = p.shape[0]
    bs = 512

    def body(p_ref, cos_ref, sin_ref, qg_ref, kg_ref, dqt_ref, dkp_ref, dvp_ref, dza_ref, drq_ref, drk_ref, drv_ref,
             dzr_ref, dgl_ref, dp_ref, dqg_ref, dkg_ref):
        @pl.when(pl.program_id(0) == 0)
        def _():
            dqg_ref[...] = jnp.zeros_like(dqg_ref)
            dkg_ref[...] = jnp.zeros_like(dkg_ref)

        m = _blockdiag64()
        cs, sn = cos_ref[...], sin_ref[...]
        lo = _lane((bs, 128)) < 64

        def norm_bwd(raw, dn, g, dg_ref):
            r = lax.rsqrt(_seg64(raw * raw, m) * (1.0 / 64) + EPS)
            xn = raw * r
            dg_ref[...] += jnp.sum(dn * xn, axis=0, keepdims=True)
            dxn = dn * g
            return r * (dxn - xn * (_seg64(dxn * xn, m) * (1.0 / 64)))

        for pr in range(4):
            sl = slice(128 * pr, 128 * (pr + 1))
            dn = _rope_t(dqt_ref[:, sl] * 0.125, cs, sn)
            dp_ref[:, QA + 128 * pr:QA + 128 * (pr + 1)] = norm_bwd(p_ref[:, sl], dn, qg_ref[...], dqg_ref).astype(dp_ref.dtype)
        fold = lambda a: a + pltpu.roll(a, 64, 1)
        dk = jnp.where(lo, fold(dkp_ref[0]), fold(dkp_ref[1]))
        dp_ref[:, KA:KA + 128] = norm_bwd(p_ref[:, KA:KA + 128], _rope_t(dk, cs, sn), kg_ref[...], dkg_ref).astype(dp_ref.dtype)
        dp_ref[:, VA:VA + 128] = jnp.where(lo, fold(dvp_ref[0]), fold(dvp_ref[1])).astype(dp_ref.dtype)
        dp_ref[:, ZA:ZA + 512] = dza_ref[...]
        for pr in range(2):
            sl = slice(128 * pr, 128 * (pr + 1))
            dp_ref[:, QR + 128 * pr:QR + 128 * (pr + 1)] = _rope_t(drq_ref[:, sl], cs, sn).astype(dp_ref.dtype)
            dp_ref[:, KR + 128 * pr:KR + 128 * (pr + 1)] = _rope_t(drk_ref[:, sl] * 0.125, cs, sn).astype(dp_ref.dtype)
        dp_ref[:, VR:VR + 512] = drv_ref[...]
        dp_ref[:, ZR:ZR + 512] = dzr_ref[...]
        dp_ref[:, GL:GL + 2 * D_MODEL] = dgl_ref[...]

    row = lambda w: pl.BlockSpec((bs, w), lambda i: (i, 0))
    gsp = pl.BlockSpec((1, 128), lambda i: (0, 0))
    dup = pl.BlockSpec((2, bs, 128), lambda i: (0, i, 0))
    return pl.pallas_call(
        body, grid=(seq // bs,), name="assemble_dp",
        in_specs=[row(768), row(128), row(128), gsp, gsp, row(512), dup, dup, row(512), row(256), row(256), row(512),
                  row(512), row(2 * D_MODEL)],
        out_specs=[row(IN_WIDTH), gsp, gsp],
        out_shape=[jax.ShapeDtypeStruct((seq, IN_WIDTH), MXU_DTYPE), jax.ShapeDtypeStruct((1, 128), F32),
                   jax.ShapeDtypeStruct((1, 128), F32)],
        compiler_params=_cp("arbitrary"))(p, cos, sin, qg2, kg2, dqt, dkp, dvp, dza, drq, drk, drv, dzr, dgl)


def _local_step(x, tgt, mod, g_pre, qn_g, kn_g, w_dec_f, w_dec_b, gn_g, g_post, w_attn, rest_start, rest_wait, send=None):
    send = send or (lambda name, arrays: None)
    seq = x.shape[0]
    cos, sin = _rope_tables(seq)
    qg2, kg2 = jnp.tile(qn_g, (1, 2)), jnp.tile(kn_g, (1, 2))
    lg_f = jax.nn.log_sigmoid(w_dec_f[0])
    lg_b = jax.nn.log_sigmoid(w_dec_b[0])
    tb = _ret_tables(lg_f, lg_b, RET_CHUNK)

    h, p_a, qt, kd, vd = _attn_inputs(x, mod, g_pre, w_attn, cos, sin, qg2, kg2)
    o_att, lse = _attn_fwd(qt, kd, vd, dep=rest_start(qt))
    win_t, wpt, wout = rest_wait(o_att)
    p_b, rq, rk, rv = _in_proj_rest(h, win_t, cos, sin)
    rf, rb = _ret_states(rk, rv, tb["wkf"], tb["wkb"], tb["dec_fb"], "ret_states_fwd")
    o_ret = _ret_fwd(rq, rk, rv, rf, rb, tb)
    (dout, dza, dzr, dgl, do_att, do_ret, delta, g_out, g_pt, dgate, dgpost, dgn, loss_l) = _tail(
        x, tgt, o_att, o_ret, p_b, mod, g_post, gn_g, wpt, wout)
    dep = send("early", (g_pt, g_out))
    dqt, dkp, dvp = _attn_bwd(qt, kd, vd, do_att, lse, delta, dep=dep)
    hb, hf = _ret_states(rq, do_ret, tb["wqb"], tb["wqf"], tb["dec_bf"], "ret_states_bwd")
    drq, drk, drv, dlg = _ret_bwd(rq, rk, rv, do_ret, rf, rb, hf, hb, tb)
    dp, dqg, dkg = _assemble(p_a, cos, sin, qg2, kg2, dqt, dkp, dvp, dza, drq, drk, drv, dzr, dgl)
    g_in_t = _matmul(dp, h, ta=True, tb=False, tm=256, tn=D_MODEL, out_dtype=MXU_DTYPE, name="in_proj_dw")
    dep = send("late", (g_in_t,))
    grad_x, dshift, dscale, dgpre = _in_proj_dx(x, dp, win_t, dout, mod, g_pre, dep=dep)

    dlg = jnp.sum(dlg, axis=1)
    small = dict(
        dmod=jnp.concatenate([dshift, dscale, dgate], axis=1),
        g_pre=dgpre, g_post=dgpost, gn_g=dgn,
        qn_g=dqg[:, :64] + dqg[:, 64:], kn_g=dkg[:, :64] + dkg[:, 64:],
        w_dec_f=(dlg[0:4] * jax.nn.sigmoid(-w_dec_f[0]))[None], w_dec_b=(dlg[4:8] * jax.nn.sigmoid(-w_dec_b[0]))[None],
        loss=jnp.sum(loss_l, axis=1, keepdims=True))
    return grad_x, g_in_t, g_pt, g_out, small


N_DEV = 8
N_CHIP = 4
HBM_SPEC = pl.BlockSpec(memory_space=pl.ANY)
VMEM_SPEC = pl.BlockSpec(memory_space=pltpu.VMEM)
SHARD_ROWS = (IN_WIDTH // N_CHIP, D_MODEL // N_CHIP, D_MODEL // N_CHIP)


def _coords():
    return lax.axis_index("x"), lax.axis_index("y"), lax.axis_index("c")


def _peer(k):
    x, y, c = _coords()
    return (1 - x if k & 4 else x, 1 - y if k & 2 else y, 1 - c if k & 1 else c)


def _dev_index(p):
    return 4 * p[0] + 2 * p[1] + p[2]


def _rows(ref, start, size):
    return ref.at[pl.ds(pl.multiple_of(start, 16), size), :]


def _remote(src, dst, ssem, rsem, dev):
    return pltpu.make_async_remote_copy(src_ref=src, dst_ref=dst, send_sem=ssem, recv_sem=rsem, device_id=dev,
                                        device_id_type=MESH)


ATTN_CHUNK = 96


def _mod_and_attn_rows(c, w_ada_s, b4, win_s):
    ncol = w_ada_s.shape[1]
    half = ATTN_COLS // 2
    offs = list(range(0, half, ATTN_CHUNK))
    nq = len(offs)
    rows = lambda ref, cc, off: ref.at[pl.ds(pl.multiple_of(cc * half + off, 16), ATTN_CHUNK), :]

    def body(c_ref, w_ref, b_ref, src, cs_ref, mod_ref, out, modsh, modall, ssem, rsem, lsem, asem, bsem, fsem, gsem):
        x, y, cc = _coords()
        me = _dev_index((x, y, cc))
        chip = me // 2
        sib = (x, y, 1 - cc)
        cs_ref[me] = c_ref[...]
        sends = []
        for k in range(1, N_DEV):
            cp = _remote(c_ref, cs_ref.at[me], ssem.at[k - 1], rsem.at[k - 1], _peer(k))
            cp.start()
            sends.append(cp)
        own = pltpu.make_async_copy(src.at[pl.ds(0, ATTN_COLS), :], out, lsem.at[0])
        bulk = [_remote(rows(src, cc, off), rows(out, cc, off), asem.at[(k2 - 1) * nq + q], bsem.at[q], (k2 // 2, k2 % 2, cc))
                for k2 in range(1, N_CHIP) for q, off in enumerate(offs)]

        @pl.when(chip == 0)
        def _():
            own.start()
            for cp in bulk:
                cp.start()

        for k in range(1, N_DEV):
            slot = cs_ref.at[_dev_index(_peer(k))]
            _remote(slot, slot, ssem.at[k - 1], rsem.at[k - 1], _peer(k)).wait_recv()
        cs = jnp.concatenate([cs_ref[d] for d in range(N_DEV)], axis=0)
        ms = _nn(cs * _sigmoid(cs), w_ref[...], precision=HIGHEST) + b_ref[pl.ds(chip, 1), :]
        modsh[...] = ms
        modall[chip] = ms
        for k2 in range(1, N_CHIP):
            cp = _remote(modsh, modall.at[chip], ssem.at[6 + k2], rsem.at[6 + k2], _peer(2 * k2))
            cp.start()
            sends.append(cp)

        @pl.when(chip != 0)
        def _():
            passed = []
            for q, off in enumerate(offs):
                got = rows(out, cc, off)
                _remote(got, got, asem.at[q], bsem.at[q], (0, 0, cc)).wait_recv()
                cp = _remote(got, got, fsem.at[q], gsem.at[q], sib)
                cp.start()
                passed.append(cp)
            for q, off in enumerate(offs):
                got = rows(out, 1 - cc, off)
                _remote(got, got, fsem.at[q], gsem.at[q], sib).wait_recv()
            for cp in passed:
                cp.wait_send()

        for k2 in range(1, N_CHIP):
            slot = modall.at[_dev_index(_peer(2 * k2)) // 2]
            _remote(slot, slot, ssem.at[6 + k2], rsem.at[6 + k2], _peer(2 * k2)).wait_recv()
        for jj in range(N_CHIP):
            mod_ref[:, ncol * jj:ncol * (jj + 1)] = modall[jj, pl.ds(me, 1), :]
        for cp in sends:
            cp.wait_send()

        @pl.when(chip == 0)
        def _():
            for cp in bulk:
                cp.wait_send()
            own.wait()

    dma = pltpu.SemaphoreType.DMA
    return pl.pallas_call(
        body, name="mod_and_attn_rows", in_specs=[VMEM_SPEC] * 4, out_specs=[VMEM_SPEC, VMEM_SPEC, HBM_SPEC],
        out_shape=[jax.ShapeDtypeStruct((N_DEV, 1, D_MODEL), F32), jax.ShapeDtypeStruct((1, N_CHIP * ncol), F32),
                   jax.ShapeDtypeStruct((ATTN_COLS, win_s.shape[1]), win_s.dtype)],
        scratch_shapes=[pltpu.VMEM((N_DEV, ncol), F32), pltpu.VMEM((N_CHIP, N_DEV, ncol), F32), dma((10,)), dma((10,)),
                        dma((1,)), dma((3 * nq,)), dma((nq,)), dma((nq,)), dma((nq,))],
        compiler_params=_cp())(c, w_ada_s, b4, win_s)


GATHER_CHUNK = 32


def _chunks(kinds, chunk):
    out = []
    for t, kind in enumerate(kinds):
        half = SHARD_ROWS[kind] // 2
        step = chunk if half % chunk == 0 else half
        out += [(t, off, step) for off in range(0, half, step)]
    return out


SEM_SPEC = pl.BlockSpec(memory_space=pltpu.SEMAPHORE)
HBM_ONLY = pl.BlockSpec(memory_space=pltpu.HBM)
DATAFLOW = pltpu.SideEffectType.DATAFLOW_SIDE_EFFECTING


def _place_own(shards):
    nt = len(shards)

    def body(*refs):
        srcs, outs, lsem = refs[:nt], refs[nt:2 * nt], refs[2 * nt]
        x, y, _ = _coords()
        chip = 2 * x + y
        local = [pltpu.make_async_copy(srcs[t], _rows(outs[t], chip * SHARD_ROWS[t], SHARD_ROWS[t]), lsem.at[t]) for t in range(nt)]
        for cp in local:
            cp.start()
        for cp in local:
            cp.wait()

    return pl.pallas_call(
        body, name="place_own_shard", in_specs=[VMEM_SPEC] * nt, out_specs=[HBM_SPEC] * nt,
        out_shape=[jax.ShapeDtypeStruct((N_CHIP * SHARD_ROWS[t], s.shape[1]), s.dtype) for t, s in enumerate(shards)],
        scratch_shapes=[pltpu.SemaphoreType.DMA((nt,))], compiler_params=_cp())(*shards)


def _gather_rest_start(shards, zones, dep):
    n = len(shards)

    def body(*refs):
        srcs, lands = refs[:n], refs[n:2 * n]
        ssem, rsem = refs[2 * n + 1], refs[2 * n + 2]
        token = refs[-1]
        x, y, _ = _coords()
        chip = 2 * x + y
        for k2 in range(1, N_CHIP):
            for t in range(n):
                q = (k2 - 1) * n + t
                _remote(srcs[t], _rows(lands[t], chip * SHARD_ROWS[t], SHARD_ROWS[t]), ssem.at[q], rsem.at[q], _peer(2 * k2)).start()
        token[...] = jnp.zeros_like(token)

    hbm = lambda a: pltpu.with_memory_space_constraint(a, pltpu.HBM)
    dma = pltpu.SemaphoreType.DMA
    outs = pl.pallas_call(
        body, name="gather_rest", in_specs=[HBM_ONLY] * (2 * n) + [HBM_SPEC],
        out_specs=[SEM_SPEC, SEM_SPEC] + [HBM_ONLY] * (2 * n) + [VMEM_SPEC],
        out_shape=[dma((3 * n,)), dma((3 * n,))] + [pltpu.HBM(a.shape, a.dtype) for a in list(shards) + list(zones)]
        + [jax.ShapeDtypeStruct((8, 128), F32)],
        input_output_aliases={i: 2 + i for i in range(2 * n)},
        compiler_params=pltpu.CompilerParams(has_side_effects=DATAFLOW))(*[hbm(a) for a in list(shards) + list(zones)], dep)
    return outs[0], outs[1], outs[2:2 + n], outs[2 + n:2 + 2 * n], outs[-1]


def _gather_rest_wait(started, after):
    ssem, rsem, shards, zones, _ = started
    n = len(shards)

    def body(*refs):
        srcs, lands = refs[:n], refs[n:2 * n]
        ssem_ref, rsem_ref = refs[2 * n], refs[2 * n + 1]
        for k2 in range(1, N_CHIP):
            pchip = _dev_index(_peer(2 * k2)) // 2
            for t in range(n):
                q = (k2 - 1) * n + t
                cp = _remote(srcs[t], _rows(lands[t], pchip * SHARD_ROWS[t], SHARD_ROWS[t]), ssem_ref.at[q], rsem_ref.at[q], _peer(2 * k2))
                cp.wait_send()
                cp.wait_recv()

    outs = pl.pallas_call(
        body, name="gather_rest_wait", in_specs=[HBM_ONLY] * (2 * n) + [SEM_SPEC, SEM_SPEC, HBM_SPEC], out_specs=[HBM_ONLY] * (2 * n),
        out_shape=[pltpu.HBM(a.shape, a.dtype) for a in list(shards) + list(zones)],
        input_output_aliases={i: i for i in range(2 * n)},
        compiler_params=pltpu.CompilerParams(has_side_effects=DATAFLOW))(*shards, *zones, ssem, rsem, after)
    return outs[n:]


def _reduced_by(ref, t, dev):
    return _rows(ref, (dev // 2) * SHARD_ROWS[t] + (dev % 2) * (SHARD_ROWS[t] // 2), SHARD_ROWS[t] // 2)


def _scatter_start(grads, kinds, name):
    n = len(grads)

    def body(*refs):
        srcs, lands = refs[:n], refs[n:2 * n]
        ssem, rsem = refs[2 * n], refs[2 * n + 1]
        token = refs[-1]
        me = _dev_index(_coords())
        for k in range(1, N_DEV):
            for i, t in enumerate(kinds):
                q = (k - 1) * n + i
                _remote(_reduced_by(srcs[i], t, _dev_index(_peer(k))), lands[i].at[me], ssem.at[q], rsem.at[q], _peer(k)).start()
        token[...] = jnp.zeros_like(token)

    zones = [lax.empty((N_DEV, SHARD_ROWS[t] // 2, g.shape[1]), g.dtype) for g, t in zip(grads, kinds)]
    hbm = lambda a: pltpu.with_memory_space_constraint(a, pltpu.HBM)
    dma = pltpu.SemaphoreType.DMA
    outs = pl.pallas_call(
        body, name=name, in_specs=[HBM_ONLY] * (2 * n), out_specs=[SEM_SPEC, SEM_SPEC] + [HBM_ONLY] * (2 * n) + [VMEM_SPEC],
        out_shape=[dma((7 * n,)), dma((7 * n,))] + [pltpu.HBM(a.shape, a.dtype) for a in list(grads) + zones]
        + [jax.ShapeDtypeStruct((8, 128), F32)],
        input_output_aliases={i: 2 + i for i in range(2 * n)},
        compiler_params=pltpu.CompilerParams(has_side_effects=DATAFLOW))(*[hbm(a) for a in list(grads) + zones])
    return outs[0], outs[1], outs[2:2 + n], outs[2 + n:2 + 2 * n], outs[-1]


def _scatter_wait(started, kinds, after, name):
    ssem, rsem, grads, zones, _ = started
    n = len(grads)

    def body(*refs):
        srcs, lands = refs[:n], refs[n:2 * n]
        ssem_ref, rsem_ref = refs[2 * n], refs[2 * n + 1]
        for k in range(1, N_DEV):
            peer = _dev_index(_peer(k))
            for i, t in enumerate(kinds):
                q = (k - 1) * n + i
                cp = _remote(_reduced_by(srcs[i], t, peer), lands[i].at[peer], ssem_ref.at[q], rsem_ref.at[q], _peer(k))
                cp.wait_send()
                cp.wait_recv()

    outs = pl.pallas_call(
        body, name=name, in_specs=[HBM_ONLY] * (2 * n) + [SEM_SPEC, SEM_SPEC, HBM_SPEC], out_specs=[HBM_ONLY] * (2 * n),
        out_shape=[pltpu.HBM(a.shape, a.dtype) for a in list(grads) + list(zones)],
        input_output_aliases={i: i for i in range(2 * n)},
        compiler_params=pltpu.CompilerParams(has_side_effects=DATAFLOW))(*grads, *zones, ssem, rsem, after)
    return outs[:n], outs[n:]


def _grad_finish(grads, zones, kinds, name):
    nt = len(grads)
    pieces = _chunks(kinds, GATHER_CHUNK)
    npc = len(pieces)
    shard = [SHARD_ROWS[k] for k in kinds]

    def body(*refs):
        srcs, lands, outs = refs[:nt], refs[nt:2 * nt], refs[2 * nt:3 * nt]
        slots, sums = refs[3 * nt:4 * nt], refs[4 * nt:5 * nt]
        lsem, osem, xsem, ysem = refs[5 * nt:]
        x, y, c = _coords()
        me = _dev_index((x, y, c))
        sib = (x, y, 1 - c)
        loads = [pltpu.make_async_copy(_reduced_by(srcs[t], kinds[t], me), slots[t].at[me], lsem.at[t]) for t in range(nt)]
        for k in range(1, N_DEV):
            peer = _dev_index(_peer(k))
            loads += [pltpu.make_async_copy(lands[t].at[peer], slots[t].at[peer], lsem.at[k * nt + t]) for t in range(nt)]
        for cp in loads:
            cp.start()
        for cp in loads:
            cp.wait()
        sends = []
        place = lambda t, cc, off, n: _rows(outs[t], cc * (shard[t] // 2) + off, n)
        stores = []
        for q, (t, off, n) in enumerate(pieces):
            r = pl.ds(off, n)
            acc = slots[t][0, r, :].astype(F32)
            for d in range(1, N_DEV):
                acc = acc + slots[t][d, r, :].astype(F32)
            sums[t][r, :] = acc
            keep = pltpu.make_async_copy(sums[t].at[r, :], place(t, c, off, n), osem.at[q])
            give = _remote(sums[t].at[r, :], place(t, c, off, n), xsem.at[q], ysem.at[q], sib)
            keep.start()
            give.start()
            stores.append(keep)
            sends.append(give)
        for q, (t, off, n) in enumerate(pieces):
            got = place(t, 1 - c, off, n)
            _remote(got, got, xsem.at[q], ysem.at[q], sib).wait_recv()
        for cp in sends:
            cp.wait_send()
        for cp in stores:
            cp.wait()

    dma = pltpu.SemaphoreType.DMA
    return pl.pallas_call(
        body, name=name, in_specs=[HBM_SPEC] * (2 * nt), out_specs=[HBM_SPEC] * nt,
        out_shape=[jax.ShapeDtypeStruct((shard[t], g.shape[1]), F32) for t, g in enumerate(grads)],
        scratch_shapes=([pltpu.VMEM((N_DEV, shard[t] // 2, g.shape[1]), g.dtype) for t, g in enumerate(grads)]
                        + [pltpu.VMEM((shard[t] // 2, g.shape[1]), F32) for t, g in enumerate(grads)]
                        + [dma((N_DEV * nt,)), dma((npc,)), dma((npc,)), dma((npc,))]),
        compiler_params=_cp())(*grads, *zones)


def _adam_math(w, g, m, v):
    m = ADAM_B1 * m + (1.0 - ADAM_B1) * g
    v = ADAM_B2 * v + (1.0 - ADAM_B2) * (g * g)
    m_hat = m / (1.0 - ADAM_B1 ** ADAM_STEP)
    v_hat = v / (1.0 - ADAM_B2 ** ADAM_STEP)
    return -ADAM_LR * (m_hat / (jnp.sqrt(v_hat) + ADAM_EPS) + ADAM_WD * w), m, v


def _adamw(w, g, m, v, rb, name):
    rows, cols = w.shape

    def body(w_ref, g_ref, m_ref, v_ref, d_ref, nm_ref, nv_ref):
        d_ref[...], nm_ref[...], nv_ref[...] = _adam_math(w_ref[...], g_ref[...], m_ref[...], v_ref[...])

    spec = pl.BlockSpec((rb, cols), lambda i: (i, 0))
    return pl.pallas_call(body, grid=(rows // rb,), name=name, in_specs=[spec] * 4, out_specs=[spec] * 3,
                          out_shape=[jax.ShapeDtypeStruct(w.shape, F32)] * 3, compiler_params=_cp("parallel"))(w, g, m, v)


PACK = (("b_ada", 3 * D_MODEL), ("g_pre", D_MODEL), ("g_post", D_MODEL), ("gn_g", 512), ("qn_g", 64), ("kn_g", 64),
        ("w_dec_f", 4), ("w_dec_b", 4), ("loss", 1))
PACK_OFFSETS = {}
PACK_WIDTH = 0
for _name, _n in PACK:
    PACK_OFFSETS[_name] = PACK_WIDTH
    PACK_WIDTH += -(-_n // 128) * 128
SMALL_PARAMS = tuple(name for name, _ in PACK if name != "loss")


def _pack(parts):
    cols = []
    for name, n in PACK:
        pad = -(-n // 128) * 128 - n
        cols.append(parts[name].reshape(1, n).astype(F32))
        if pad:
            cols.append(jnp.zeros((1, pad), F32))
    return jnp.concatenate(cols, axis=1)


def _small_reduce(vec, cs, dep):
    ncol = 3 * D_MODEL // N_CHIP

    def body(vec_ref, cs_ref, dep_ref, tot_ref, gwa_ref, gat, ssem, rsem):
        me = _dev_index(_coords())
        chip = me // 2
        gat[me] = vec_ref[...]
        sends = []
        for k in range(1, N_DEV):
            cp = _remote(vec_ref, gat.at[me], ssem.at[k - 1], rsem.at[k - 1], _peer(k))
            cp.start()
            sends.append(cp)
        for k in range(1, N_DEV):
            slot = gat.at[_dev_index(_peer(k))]
            _remote(slot, slot, ssem.at[k - 1], rsem.at[k - 1], _peer(k)).wait_recv()
        rows = [gat[d] for d in range(N_DEV)]
        tot = rows[0]
        for d in range(1, N_DEV):
            tot = tot + rows[d]
        tot_ref[...] = tot
        cs = cs_ref[...]
        ca_t = jnp.transpose(jnp.concatenate([cs * _sigmoid(cs), jnp.zeros((128 - N_DEV, D_MODEL), F32)], axis=0))
        dm = jnp.concatenate(rows + [jnp.zeros((128 - N_DEV, PACK_WIDTH), F32)], axis=0)
        for jj in range(N_CHIP):
            @pl.when(chip == jj)
            def _():
                gwa_ref[...] = _nn(ca_t, dm[:, ncol * jj:ncol * (jj + 1)], precision=HIGHEST)
        for cp in sends:
            cp.wait_send()

    return pl.pallas_call(
        body, name="small_reduce", in_specs=[VMEM_SPEC, VMEM_SPEC, HBM_SPEC], out_specs=[VMEM_SPEC] * 2,
        out_shape=[jax.ShapeDtypeStruct((1, PACK_WIDTH), F32), jax.ShapeDtypeStruct((D_MODEL, ncol), F32)],
        scratch_shapes=[pltpu.VMEM((N_DEV, 1, PACK_WIDTH), F32), pltpu.SemaphoreType.DMA((7,)), pltpu.SemaphoreType.DMA((7,))],
        compiler_params=_cp())(vec, cs, dep)


def _small_adamw(tot, given):
    sizes = dict(PACK)
    np_ = len(SMALL_PARAMS)

    def body(*refs):
        tot_ref = refs[0]
        wmv = refs[1:1 + 3 * np_]
        outs = refs[1 + 3 * np_:1 + 7 * np_]
        loss_ref = refs[-1]
        for i, name in enumerate(SMALL_PARAMS):
            off, n = PACK_OFFSETS[name], sizes[name]
            g = tot_ref[:, off:off + n]
            w_ref, m_ref, v_ref = wmv[3 * i:3 * i + 3]
            outs[i][...] = g
            outs[np_ + i][...], outs[2 * np_ + i][...], outs[3 * np_ + i][...] = _adam_math(w_ref[...], g, m_ref[...], v_ref[...])
        loss_ref[...] = tot_ref[:, PACK_OFFSETS["loss"]:PACK_OFFSETS["loss"] + 1]

    flat = [a for name in SMALL_PARAMS for a in given[name]]
    shapes = [jax.ShapeDtypeStruct((1, sizes[name]), F32) for name in SMALL_PARAMS]
    res = pl.pallas_call(body, name="small_adamw", in_specs=[VMEM_SPEC] * (1 + 3 * np_), out_specs=[VMEM_SPEC] * (4 * np_ + 1),
                         out_shape=shapes * 4 + [jax.ShapeDtypeStruct((1, 1), F32)], compiler_params=_cp())(tot, *flat)
    return [dict(zip(SMALL_PARAMS, res[k * np_:(k + 1) * np_])) for k in range(4)], res[-1]


def kernel(x, c, w_ada, b_ada, g_pre, w_in, qn_g, kn_g, w_dec_f, w_dec_b, gn_g, w_pa, w_pr, w_out, g_post, loss_target, m_w_ada, m_b_ada, m_g_pre, m_w_in, m_qn_g, m_kn_g, m_w_dec_f, m_w_dec_b, m_gn_g, m_w_pa, m_w_pr, m_w_out, m_g_post, v_w_ada, v_b_ada, v_g_pre, v_w_in, v_qn_g, v_kn_g, v_w_dec_f, v_w_dec_b, v_gn_g, v_w_pa, v_w_pr, v_w_out, v_g_post):
    shards = (w_in[0].T.astype(MXU_DTYPE), jnp.concatenate([w_pa[0].T, w_pr[0].T], axis=1).astype(MXU_DTYPE),
              w_out[0].astype(MXU_DTYPE))
    cs, mod, w_attn = _mod_and_attn_rows(c, w_ada[0], b_ada.reshape(N_CHIP, 3 * D_MODEL // N_CHIP), shards[0])
    started = {}

    def rest_start(dep):
        started["rest"] = _gather_rest_start(shards, _place_own(shards), dep)
        return started["rest"][-1]

    def rest_wait(after):
        return _gather_rest_wait(started["rest"], after)

    kinds = {"early": (1, 2), "late": (0,)}

    def send(name, arrays):
        started[name] = _scatter_start(arrays, kinds[name], "grad_scatter_" + name)
        return started[name][-1]

    grad_x, _, _, _, small = _local_step(x[0], loss_target[0], mod, g_pre, qn_g, kn_g, w_dec_f, w_dec_b,
                                         gn_g, g_post, w_attn, rest_start, rest_wait, send=send)
    small = dict(small)
    small["b_ada"] = small.pop("dmod")
    given = dict(b_ada=(b_ada, m_b_ada, v_b_ada), g_pre=(g_pre, m_g_pre, v_g_pre), g_post=(g_post, m_g_post, v_g_post),
                 gn_g=(gn_g, m_gn_g, v_gn_g), qn_g=(qn_g, m_qn_g, v_qn_g), kn_g=(kn_g, m_kn_g, v_kn_g),
                 w_dec_f=(w_dec_f, m_w_dec_f, v_w_dec_f), w_dec_b=(w_dec_b, m_w_dec_b, v_w_dec_b))
    grads, deltas, new_m, new_v = {}, {}, {}, {}

    def update(name, w, g, m, v, back):
        d, nm, nv = _adamw(w, g, m, v, w.shape[0] // 4, "adamw_" + name)
        grads[name], deltas[name], new_m[name], new_v[name] = back(g), back(d), back(nm), back(nv)
        return d

    lead = lambda a: a[None]
    (g_pt, g_out), (z_pt, z_out) = _scatter_wait(started["early"], kinds["early"], grad_x, "grad_scatter_early_wait")
    r_pt, r_out = _grad_finish((g_pt, g_out), (z_pt, z_out), kinds["early"], "grad_finish_early")
    update("w_pa", w_pa[0], r_pt[:, :512].T, m_w_pa[0], v_w_pa[0], lead)
    update("w_pr", w_pr[0], r_pt[:, 512:].T, m_w_pr[0], v_w_pr[0], lead)
    last = update("w_out", w_out[0], r_out, m_w_out[0], v_w_out[0], lead)
    tot, g_w_ada = _small_reduce(_pack(small), cs.reshape(N_DEV, D_MODEL), last)
    small_parts, loss = _small_adamw(tot, given)
    for dst, part in zip((grads, deltas, new_m, new_v), small_parts):
        dst.update(part)
    last = update("w_ada", w_ada[0], g_w_ada, m_w_ada[0], v_w_ada[0], lead)

    (g_in_t,), (z_in,) = _scatter_wait(started["late"], kinds["late"], last, "grad_scatter_late_wait")
    (r_in,) = _grad_finish((g_in_t,), (z_in,), kinds["late"], "grad_finish_late")
    tr = lambda a: a[0].T
    update("w_in", tr(w_in), r_in, tr(m_w_in), tr(v_w_in), lambda a: a.T[None])
    order = ("w_ada", "b_ada", "g_pre", "w_in", "qn_g", "kn_g", "w_dec_f", "w_dec_b", "gn_g", "w_pa", "w_pr", "w_out", "g_post")
    return (loss[0, 0], grad_x[None], *[grads[n] for n in order], *[deltas[n] for n in order],
            *[new_m[n] for n in order], *[new_v[n] for n in order])
```

```python
import functools

import jax
import jax.numpy as jnp
import numpy as np
from jax import lax
from jax.experimental import pallas as pl
from jax.experimental.pallas import tpu as pltpu

F32 = jnp.float32
BF16 = jnp.bfloat16
MXU_DTYPE = jnp.bfloat16

D_MODEL = 1024
GRID_W = 64
HEAD_DIM = 64
ROPE_THETA = 10000.0
EPS = 1e-6
RET_HEADS = 4
RET_CHUNK = 256
IN_WIDTH = 4864
QA, KA, VA, ZA, QR, KR, VR, ZR, GL = 0, 512, 640, 768, 1280, 1536, 1792, 2304, 2816
ATTN_COLS = ZA
ZA_B, QR_B, KR_B, VR_B, ZR_B, GL_B = (c - ATTN_COLS for c in (ZA, QR, KR, VR, ZR, GL))

ADAM_LR, ADAM_B1, ADAM_B2, ADAM_EPS, ADAM_WD, ADAM_STEP = 0.001, 0.9, 0.999, 1e-08, 0.01, 10

VMEM_LIMIT = 56 * 1024 * 1024
MESH = pl.DeviceIdType.MESH
HIGHEST = lax.Precision.HIGHEST
LOG2E = 1.4426950408889634
LN2 = 0.6931471805599453


def _cp(*sem, **kw):
    if sem:
        kw["dimension_semantics"] = sem
    return pltpu.CompilerParams(vmem_limit_bytes=VMEM_LIMIT, **kw)


def _nn(a, b, **kw):
    return lax.dot_general(a, b, (((1,), (0,)), ((), ())), preferred_element_type=F32, **kw)


def _nt(a, b):
    return lax.dot_general(a, b, (((1,), (1,)), ((), ())), preferred_element_type=F32)


def _tn(a, b):
    return lax.dot_general(a, b, (((0,), (0,)), ((), ())), preferred_element_type=F32)


def _mx(x):
    return x.astype(MXU_DTYPE)


def _lane(shape):
    return lax.broadcasted_iota(jnp.int32, shape, 1)


def _sigmoid(z):
    return 1.0 / (1.0 + jnp.exp(-z))


def _rowsum128(x):
    s = jnp.sum(x, axis=0, keepdims=True)
    out = s[:, 0:128]
    for k in range(1, x.shape[1] // 128):
        out = out + s[:, 128 * k:128 * (k + 1)]
    return out


def _swap16(x):
    return jnp.where((_lane(x.shape) & 16) == 0, pltpu.roll(x, 112, 1), pltpu.roll(x, 16, 1))


def _rope(x, cos, sin):
    return x * cos + _swap16(x) * sin


def _rope_t(d, cos, sin):
    return d * cos + _swap16(d * sin)


def _blockdiag64():
    r = lax.broadcasted_iota(jnp.int32, (128, 128), 0) // 64
    c = lax.broadcasted_iota(jnp.int32, (128, 128), 1) // 64
    return (r == c).astype(BF16)


def _seg64(x, m):
    hi = x.astype(BF16)
    lo = (x - hi.astype(F32)).astype(BF16)
    return _nn(hi, m) + _nn(lo, m)


def _rope_tables(seq):
    t = np.arange(seq)
    row = (t // GRID_W).astype(np.float32)
    col = (t % GRID_W).astype(np.float32)
    half = HEAD_DIM // 2
    inv_freq = (np.float32(ROPE_THETA) ** (-np.arange(0, half, 2, dtype=np.float32) / np.float32(half))).astype(np.float32)
    ar = (row[:, None] * inv_freq[None, :]).astype(np.float32).astype(np.float64)
    ac = (col[:, None] * inv_freq[None, :]).astype(np.float32).astype(np.float64)
    cr, sr, cc, sc = np.cos(ar), np.sin(ar), np.cos(ac), np.sin(ac)
    cos64 = np.concatenate([cr, cr, cc, cc], axis=1)
    sin64 = np.concatenate([-sr, sr, -sc, sc], axis=1)
    return jnp.asarray(np.tile(cos64, (1, 2)), F32), jnp.asarray(np.tile(sin64, (1, 2)), F32)


def _attn_inputs(x, mod, g_pre, w_attn, cos, sin, qg2, kg2):
    seq = x.shape[0]
    bs = 512

    def body(x_ref, mod_ref, g_ref, w_ref, cos_ref, sin_ref, qg_ref, kg_ref, h_ref, pa_ref, qt_ref, kd_ref, vd_ref):
        xv = x_ref[...]
        r = lax.rsqrt(jnp.mean(xv * xv, axis=-1, keepdims=True) + EPS)
        shift = mod_ref[:, 0:D_MODEL]
        scale = mod_ref[:, D_MODEL:2 * D_MODEL]
        hb = ((xv * r) * g_ref[...] * (1.0 + scale) + shift).astype(h_ref.dtype)
        h_ref[...] = hb
        p = _nt(hb, w_ref[...])
        pa_ref[...] = p
        m = _blockdiag64()
        cs, sn = cos_ref[...], sin_ref[...]
        lo = _lane((bs, 128)) < 64
        for pr in range(4):
            q = p[:, QA + 128 * pr:QA + 128 * (pr + 1)]
            r = lax.rsqrt(_seg64(q * q, m) * (1.0 / 64) + EPS)
            qt_ref[:, 128 * pr:128 * (pr + 1)] = (_rope(q * r * qg_ref[...], cs, sn) * (0.125 * LOG2E)).astype(qt_ref.dtype)
        k = p[:, KA:KA + 128]
        r = lax.rsqrt(_seg64(k * k, m) * (1.0 / 64) + EPS)
        kr = _rope(k * r * kg_ref[...], cs, sn)
        ksw = pltpu.roll(kr, 64, 1)
        kd_ref[0] = jnp.where(lo, kr, ksw).astype(kd_ref.dtype)
        kd_ref[1] = jnp.where(lo, ksw, kr).astype(kd_ref.dtype)
        v = p[:, VA:VA + 128]
        vsw = pltpu.roll(v, 64, 1)
        vd_ref[0] = jnp.where(lo, v, vsw).astype(vd_ref.dtype)
        vd_ref[1] = jnp.where(lo, vsw, v).astype(vd_ref.dtype)

    row = lambda w: pl.BlockSpec((bs, w), lambda i: (i, 0))
    fix = lambda a, b: pl.BlockSpec((a, b), lambda i: (0, 0))
    dup = pl.BlockSpec((2, bs, 128), lambda i: (0, i, 0))
    sds = jax.ShapeDtypeStruct
    return pl.pallas_call(
        body, grid=(seq // bs,), name="attn_inputs",
        in_specs=[row(D_MODEL), fix(1, 3 * D_MODEL), fix(1, D_MODEL), fix(ATTN_COLS, D_MODEL), row(128), row(128), fix(1, 128), fix(1, 128)],
        out_specs=[row(D_MODEL), row(ATTN_COLS), row(512), dup, dup],
        out_shape=[sds((seq, D_MODEL), MXU_DTYPE), sds((seq, ATTN_COLS), F32), sds((seq, 512), MXU_DTYPE),
                   sds((2, seq, 128), MXU_DTYPE), sds((2, seq, 128), MXU_DTYPE)],
        compiler_params=_cp("parallel"))(x, mod, g_pre, w_attn, cos, sin, qg2, kg2)


def _in_proj_dx(x, dp, win_t, dout, mod, g_pre, dep=None):
    seq = x.shape[0]
    bs = 512
    deps, dep_specs = _dep_args(dep, 1)

    def body(x_ref, dp_ref, w_ref, dout_ref, mod_ref, g_ref, *rest):
        gx_ref, dshift_ref, dscale_ref, dg_ref = rest[-4:]

        @pl.when(pl.program_id(0) == 0)
        def _():
            dshift_ref[...] = jnp.zeros_like(dshift_ref)
            dscale_ref[...] = jnp.zeros_like(dscale_ref)
            dg_ref[...] = jnp.zeros_like(dg_ref)

        xv = x_ref[...]
        dh = _nn(dp_ref[...], w_ref[...])
        g = g_ref[...]
        r = lax.rsqrt(jnp.mean(xv * xv, axis=-1, keepdims=True) + EPS)
        xn = xv * r
        scale = mod_ref[:, D_MODEL:2 * D_MODEL]
        dshift_ref[...] += jnp.sum(dh, axis=0, keepdims=True)
        dscale_ref[...] += jnp.sum(dh * (xn * g), axis=0, keepdims=True)
        da = dh * (1.0 + scale)
        dg_ref[...] += jnp.sum(da * xn, axis=0, keepdims=True)
        dxn = da * g
        dx = r * (dxn - xn * jnp.mean(dxn * xn, axis=-1, keepdims=True))
        gx_ref[...] = dout_ref[...] + dx

    row = pl.BlockSpec((bs, D_MODEL), lambda i: (i, 0))
    vec = pl.BlockSpec((1, D_MODEL), lambda i: (0, 0))
    return pl.pallas_call(
        body, grid=(seq // bs,), name="in_proj_dx",
        in_specs=[row, pl.BlockSpec((bs, IN_WIDTH), lambda i: (i, 0)), pl.BlockSpec((IN_WIDTH, D_MODEL), lambda i: (0, 0)), row,
                  pl.BlockSpec((1, 3 * D_MODEL), lambda i: (0, 0)), vec] + dep_specs,
        out_specs=[row, vec, vec, vec],
        out_shape=[jax.ShapeDtypeStruct((seq, D_MODEL), F32)] + [jax.ShapeDtypeStruct((1, D_MODEL), F32)] * 3,
        compiler_params=_cp("arbitrary"))(x, dp, win_t, dout, mod, g_pre, *deps)


def _dep_args(dep, grid_rank):
    if dep is None:
        return [], []
    return [dep], [pl.BlockSpec(dep.shape, lambda *_: (0,) * dep.ndim)]


def _matmul(a, b, *, ta, tb, tm, tn, out_dtype, name, dep=None):
    kdim = a.shape[0] if ta else a.shape[1]
    m = a.shape[1] if ta else a.shape[0]
    n = b.shape[0] if tb else b.shape[1]
    assert m % tm == 0 and n % tn == 0
    deps, dep_specs = _dep_args(dep, 2)

    def body(a_ref, b_ref, *rest):
        o_ref = rest[-1]
        dims = (((0 if ta else 1,), (1 if tb else 0,)), ((), ()))
        o_ref[...] = lax.dot_general(a_ref[...], b_ref[...], dims, preferred_element_type=F32).astype(o_ref.dtype)

    a_spec = pl.BlockSpec((kdim, tm), lambda j, i: (0, i)) if ta else pl.BlockSpec((tm, kdim), lambda j, i: (i, 0))
    b_spec = pl.BlockSpec((tn, kdim), lambda j, i: (j, 0)) if tb else pl.BlockSpec((kdim, tn), lambda j, i: (0, j))
    return pl.pallas_call(
        body, grid=(n // tn, m // tm), name=name, in_specs=[a_spec, b_spec] + dep_specs,
        out_specs=pl.BlockSpec((tm, tn), lambda j, i: (i, j)),
        out_shape=jax.ShapeDtypeStruct((m, n), out_dtype), compiler_params=_cp("parallel", "parallel"))(a, b, *deps)


DW_TILE = 256


def _in_proj_dw(dp, h, first, count, prev, name):
    seq = dp.shape[0]
    extra = [] if prev is None else [prev]

    def body(a_ref, b_ref, *rest):
        o_ref = rest[-1]
        o_ref[...] = _tn(a_ref[...], b_ref[...]).astype(o_ref.dtype)

    return pl.pallas_call(
        body, grid=(count,), name=name,
        in_specs=[pl.BlockSpec((seq, DW_TILE), lambda i: (0, first + i)), pl.BlockSpec((seq, D_MODEL), lambda i: (0, 0))]
        + [HBM_SPEC] * len(extra),
        out_specs=pl.BlockSpec((DW_TILE, D_MODEL), lambda i: (first + i, 0)),
        out_shape=jax.ShapeDtypeStruct((IN_WIDTH, D_MODEL), MXU_DTYPE),
        input_output_aliases={2: 0} if extra else {},
        compiler_params=_cp("parallel"))(dp, h, *extra)


def _in_proj_rest(h, win_t, cos, sin):
    seq = h.shape[0]
    tm = min(1024, seq)
    ncols = IN_WIDTH - ATTN_COLS
    tn = ncols // 2
    assert ZR_B <= tn and ATTN_COLS % 128 == 0 and tn % 128 == 0

    def body(h_ref, w_ref, cos_ref, sin_ref, p_ref, rq_ref, rk_ref, rv_ref):
        p = _nt(h_ref[...], w_ref[...])
        p_ref[...] = p

        @pl.when(pl.program_id(1) == 0)
        def _():
            cs, sn = cos_ref[...], sin_ref[...]
            for pr in range(2):
                sl = slice(128 * pr, 128 * (pr + 1))
                rq_ref[:, sl] = _rope(p[:, QR_B + 128 * pr:QR_B + 128 * (pr + 1)], cs, sn).astype(rq_ref.dtype)
                rk_ref[:, sl] = (_rope(p[:, KR_B + 128 * pr:KR_B + 128 * (pr + 1)], cs, sn) * 0.125).astype(rk_ref.dtype)
            rv_ref[...] = p[:, VR_B:VR_B + 512].astype(rv_ref.dtype)

    row = lambda w: pl.BlockSpec((tm, w), lambda i, j: (i, 0))
    sds = jax.ShapeDtypeStruct
    return pl.pallas_call(
        body, grid=(seq // tm, 2), name="in_proj_rest",
        in_specs=[row(D_MODEL), pl.BlockSpec((pl.Element(tn), pl.Element(D_MODEL)),
                                             lambda i, j: (pl.multiple_of(ATTN_COLS + j * tn, 128), 0)), row(128), row(128)],
        out_specs=[pl.BlockSpec((tm, tn), lambda i, j: (i, j)), row(256), row(256), row(512)],
        out_shape=[sds((seq, ncols), F32), sds((seq, 256), MXU_DTYPE), sds((seq, 256), MXU_DTYPE), sds((seq, 512), MXU_DTYPE)],
        compiler_params=_cp("parallel", "arbitrary"))(h, win_t, cos, sin)


def _halves(kd):
    lo = _lane(kd.shape) < 64
    z = jnp.zeros_like(kd)
    return jnp.concatenate([jnp.where(lo, kd, z), jnp.where(lo, z, kd)], axis=0)


def _attn_fwd(qt, kd, vd, dep=None):
    seq = qt.shape[0]
    bq, bk = min(1024, seq), min(1024, seq)
    nk = seq // bk
    deps, dep_specs = _dep_args(dep, 2)

    def body(q_ref, k_ref, v_ref, *rest):
        o_ref, lse_ref, m_scr, l_scr, acc = rest[-5:]
        j = pl.program_id(1)
        m_scr[...] = jnp.full_like(m_scr, -jnp.inf)
        l_scr[...] = jnp.zeros_like(l_scr)
        acc[...] = jnp.zeros_like(acc)
        lo = _lane((bq, 128)) < 64

        def kv_block(n, carry):
            rows = pl.ds(pl.multiple_of(n * bk, bk), bk)
            k2, v2 = _halves(k_ref[rows, :]), _halves(v_ref[rows, :])
            for pr in range(2):
                s2 = _nt(q_ref[:, 128 * pr:128 * (pr + 1)], k2)
                ps, alphas = [], []
                for e in range(2):
                    g = 2 * pr + e
                    s = s2[:, e * bk:(e + 1) * bk]
                    m_prev = m_scr[g]
                    m_next = jnp.maximum(m_prev, jnp.max(s, axis=1, keepdims=True))
                    alpha = jnp.exp2(m_prev - m_next)
                    pe = jnp.exp2(s - jnp.tile(m_next, (1, bk // 128)))
                    l_scr[g] = alpha * l_scr[g] + jnp.sum(pe, axis=1, keepdims=True)
                    m_scr[g] = m_next
                    ps.append(_mx(pe))
                    alphas.append(alpha)
                o2 = _nn(jnp.concatenate(ps, axis=1), v2)
                sl = slice(128 * pr, 128 * (pr + 1))
                acc[:, sl] = acc[:, sl] * jnp.where(lo, alphas[0], alphas[1]) + o2
            return carry

        lax.fori_loop(0, nk, kv_block, 0)
        for pr in range(2):
            sl = slice(128 * pr, 128 * (pr + 1))
            o_ref[:, sl] = acc[:, sl] / jnp.where(lo, l_scr[2 * pr], l_scr[2 * pr + 1])
        for g in range(4):
            lse = jnp.transpose(m_scr[g] + jnp.log2(l_scr[g]))
            lse_ref[pl.ds(4 * j + g, 1), :] = lse[0:1, :]

    return pl.pallas_call(
        body, grid=(seq // bq, 2), name="attn_fwd",
        in_specs=[pl.BlockSpec((bq, 256), lambda i, j: (i, j)), pl.BlockSpec((None, seq, 128), lambda i, j: (j, 0, 0)),
                  pl.BlockSpec((None, seq, 128), lambda i, j: (j, 0, 0))] + dep_specs,
        out_specs=[pl.BlockSpec((bq, 256), lambda i, j: (i, j)), pl.BlockSpec((8, bq), lambda i, j: (0, i))],
        out_shape=[jax.ShapeDtypeStruct((seq, 512), F32), jax.ShapeDtypeStruct((8, seq), F32)],
        scratch_shapes=[pltpu.VMEM((4, bq, 128), F32), pltpu.VMEM((4, bq, 128), F32), pltpu.VMEM((bq, 256), F32)],
        compiler_params=_cp("parallel", "arbitrary"))(qt, kd, vd, *deps)


def _attn_bwd(qt, kd, vd, do, lse, delta, dep=None):
    seq = qt.shape[0]
    bq, bk = min(1024, seq), min(1024, seq)
    nq, nk = seq // bq, seq // bk
    deps, dep_specs = _dep_args(dep, 3)

    def body(q_ref, do_ref, k_ref, v_ref, lse_ref, del_ref, *rest):
        dq_ref, dk_ref, dv_ref = rest[-3:]
        j, i = pl.program_id(0), pl.program_id(1)
        dq_ref[...] = jnp.zeros_like(dq_ref)

        @pl.when(i == 0)
        def _():
            dk_ref[...] = jnp.zeros_like(dk_ref)
            dv_ref[...] = jnp.zeros_like(dv_ref)

        lo = _lane((bk, 128)) < 64
        big = lambda r: jnp.concatenate([jnp.broadcast_to(r[0], (bk, bq)), jnp.broadcast_to(r[1], (bk, bq))], axis=0)

        def kv_block(n, carry):
            rows = pl.ds(pl.multiple_of(n * bk, bk), bk)
            k2, v2 = _halves(k_ref[rows, :]), _halves(v_ref[rows, :])
            for pr in range(2):
                sl = slice(128 * pr, 128 * (pr + 1))
                qp, dop = q_ref[:, sl], do_ref[:, sl]
                s_t = _nt(k2, qp)
                dp_t = _nt(v2, dop)
                ls = [lse_ref[pl.ds(4 * j + 2 * pr + e, 1), :] for e in range(2)]
                dl = [del_ref[pl.ds(4 * j + 2 * pr + e, 1), :] for e in range(2)]
                p_t = jnp.exp2(s_t - big(ls))
                ds_t = _mx(p_t * (dp_t - big(dl)))
                rv = _nn(_mx(p_t), dop)
                rk = _nn(ds_t, qp) * LN2
                dv_ref[rows, :] += jnp.where(lo, rv[:bk], rv[bk:])
                dk_ref[rows, :] += jnp.where(lo, rk[:bk], rk[bk:])
                dq_ref[:, sl] += _tn(ds_t, k2)
            return carry

        lax.fori_loop(0, nk, kv_block, 0)

    return pl.pallas_call(
        body, grid=(2, nq), name="attn_bwd",
        in_specs=[pl.BlockSpec((bq, 256), lambda j, i: (i, j)), pl.BlockSpec((bq, 256), lambda j, i: (i, j)),
                  pl.BlockSpec((None, seq, 128), lambda j, i: (j, 0, 0)), pl.BlockSpec((None, seq, 128), lambda j, i: (j, 0, 0)),
                  pl.BlockSpec((8, bq), lambda j, i: (0, i)), pl.BlockSpec((8, bq), lambda j, i: (0, i))] + dep_specs,
        out_specs=[pl.BlockSpec((bq, 256), lambda j, i: (i, j)), pl.BlockSpec((None, seq, 128), lambda j, i: (j, 0, 0)),
                   pl.BlockSpec((None, seq, 128), lambda j, i: (j, 0, 0))],
        out_shape=[jax.ShapeDtypeStruct((seq, 512), F32), jax.ShapeDtypeStruct((2, seq, 128), F32),
                   jax.ShapeDtypeStruct((2, seq, 128), F32)],
        compiler_params=_cp("arbitrary", "arbitrary"))(qt, do, kd, vd, lse, delta, *deps)


def _ret_tables(lg_f, lg_b, c):
    idx = jnp.arange(c, dtype=F32)
    diff = idx[:, None] - idx[None, :]
    a, b = lg_f[:, None, None], lg_b[:, None, None]
    low, up = (diff >= 0)[None], (diff < 0)[None]
    dmat = jnp.where(low, jnp.exp(a * jnp.maximum(diff, 0.0)[None]), jnp.exp(b * jnp.maximum(-diff, 0.0)[None]))
    dda = jnp.where(low, diff[None] * jnp.exp(a * jnp.maximum(diff, 0.0)[None]), 0.0)
    ddb = jnp.where(up, -diff[None] * jnp.exp(b * jnp.maximum(-diff, 0.0)[None]), 0.0)
    rep = lambda w: jnp.broadcast_to(w[:, :, None], (RET_HEADS, c, 128))
    wqf = rep(jnp.exp(lg_f[:, None] * (idx + 1.0)[None]))
    wqb = rep(jnp.exp(lg_b[:, None] * (c - idx)[None]))
    wkf = rep(jnp.exp(lg_f[:, None] * (c - 1.0 - idx)[None]))
    wkb = rep(jnp.exp(lg_b[:, None] * idx[None]))
    cdf, cdb = jnp.exp(lg_f * c), jnp.exp(lg_b * c)
    dec_fb = jnp.broadcast_to(jnp.concatenate([cdf, cdb])[:, None], (8, 128))
    dec_bf = jnp.broadcast_to(jnp.concatenate([cdb, cdf])[:, None], (8, 128))
    return dict(dmat=dmat, dda=dda, ddb=ddb, wqf=wqf, wqb=wqb, wkf=wkf, wkb=wkb, dec_fb=dec_fb, dec_bf=dec_bf)


def _ret_states(xk, yv, w_fwd, w_rev, dec, name):
    seq = xk.shape[0]
    c = RET_CHUNK
    nc = seq // c

    def body(xf_ref, yf_ref, xr_ref, yr_ref, wf_ref, wr_ref, dec_ref, sf_ref, sr_ref, st_f, st_r):
        @pl.when(pl.program_id(0) == 0)
        def _():
            st_f[...] = jnp.zeros_like(st_f)
            st_r[...] = jnp.zeros_like(st_r)

        lane = _lane((c, 128))
        for h in range(RET_HEADS):
            pr, e = h // 2, h % 2
            half = (lane < 64) if e == 0 else (lane >= 64)
            for x_ref, y_ref, w_ref, s_ref, st, drow in ((xf_ref, yf_ref, wf_ref, sf_ref, st_f, h),
                                                        (xr_ref, yr_ref, wr_ref, sr_ref, st_r, 4 + h)):
                s_ref[h] = st[h].astype(s_ref.dtype)
                xm = jnp.where(half, x_ref[:, 128 * pr:128 * (pr + 1)].astype(F32) * w_ref[h], 0.0)
                st[h] = st[h] * dec_ref[drow:drow + 1, :] + _tn(_mx(xm), _mx(y_ref[:, 128 * h:128 * (h + 1)]))

    xs = lambda f: pl.BlockSpec((c, 256), f)
    ys = lambda f: pl.BlockSpec((c, 512), f)
    fwd, rev = (lambda n: (n, 0)), (lambda n: (nc - 1 - n, 0))
    wsp = pl.BlockSpec((RET_HEADS, c, 128), lambda n: (0, 0, 0))
    return pl.pallas_call(
        body, grid=(nc,), name=name,
        in_specs=[xs(fwd), ys(fwd), xs(rev), ys(rev), wsp, wsp, pl.BlockSpec((8, 128), lambda n: (0, 0))],
        out_specs=[pl.BlockSpec((None, RET_HEADS, 128, 128), lambda n: (n, 0, 0, 0)),
                   pl.BlockSpec((None, RET_HEADS, 128, 128), lambda n: (nc - 1 - n, 0, 0, 0))],
        out_shape=[jax.ShapeDtypeStruct((nc, RET_HEADS, 128, 128), MXU_DTYPE)] * 2,
        scratch_shapes=[pltpu.VMEM((RET_HEADS, 128, 128), F32), pltpu.VMEM((RET_HEADS, 128, 128), F32)],
        compiler_params=_cp("arbitrary"))(xk, yv, xk, yv, w_fwd, w_rev, dec)


def _ret_fwd(rq, rk, rv, rf, rb, tb):
    seq = rq.shape[0]
    c = RET_CHUNK

    def body(q_ref, k_ref, v_ref, rf_ref, rb_ref, d_ref, wqf_ref, wqb_ref, o_ref):
        lane = _lane((c, 128))
        for h in range(RET_HEADS):
            pr, e = h // 2, h % 2
            half = (lane < 64) if e == 0 else (lane >= 64)
            sl = slice(128 * pr, 128 * (pr + 1))
            qp = q_ref[:, sl]
            km = jnp.where(half, k_ref[:, sl], jnp.zeros_like(k_ref[:, sl]))
            sd = _mx(_nt(qp, km) * d_ref[h])
            qf = qp.astype(F32)
            o = _nn(sd, v_ref[:, 128 * h:128 * (h + 1)])
            o = o + _nn(_mx(qf * wqf_ref[h]), rf_ref[h]) + _nn(_mx(qf * wqb_ref[h]), rb_ref[h])
            o_ref[:, 128 * h:128 * (h + 1)] = o

    st = pl.BlockSpec((None, RET_HEADS, 128, 128), lambda n: (n, 0, 0, 0))
    wsp = pl.BlockSpec((RET_HEADS, c, 128), lambda n: (0, 0, 0))
    return pl.pallas_call(
        body, grid=(seq // c,), name="ret_fwd",
        in_specs=[pl.BlockSpec((c, 256), lambda n: (n, 0)), pl.BlockSpec((c, 256), lambda n: (n, 0)),
                  pl.BlockSpec((c, 512), lambda n: (n, 0)), st, st, pl.BlockSpec((RET_HEADS, c, c), lambda n: (0, 0, 0)), wsp, wsp],
        out_specs=pl.BlockSpec((c, 512), lambda n: (n, 0)),
        out_shape=jax.ShapeDtypeStruct((seq, 512), F32), compiler_params=_cp("parallel"))(
            rq, rk, rv, rf, rb, tb["dmat"], tb["wqf"], tb["wqb"])


def _ret_bwd(rq, rk, rv, do, rf, rb, hf, hb, tb):
    seq = rq.shape[0]
    c = RET_CHUNK

    def body(q_ref, k_ref, v_ref, do_ref, rf_ref, rb_ref, hf_ref, hb_ref, d_ref, dda_ref, ddb_ref,
             wqf_ref, wqb_ref, wkf_ref, wkb_ref, cdec_ref, dq_ref, dk_ref, dv_ref, dlg_ref):
        @pl.when(pl.program_id(0) == 0)
        def _():
            dlg_ref[...] = jnp.zeros_like(dlg_ref)

        lane = _lane((c, 128))
        pos = lax.broadcasted_iota(jnp.int32, (c, 128), 0).astype(F32)
        for pr in range(2):
            sl = slice(128 * pr, 128 * (pr + 1))
            qp, kp = q_ref[:, sl], k_ref[:, sl]
            qf, kf = qp.astype(F32), kp.astype(F32)
            dq_acc = jnp.zeros((c, 128), F32)
            dk_acc = jnp.zeros((c, 128), F32)
            for e in range(2):
                h = 2 * pr + e
                half = (lane < 64) if e == 0 else (lane >= 64)
                hs = slice(128 * h, 128 * (h + 1))
                km = jnp.where(half, kp, jnp.zeros_like(kp))
                kmf = km.astype(F32)
                vh, doh = v_ref[:, hs], do_ref[:, hs]
                rfh, rbh, hfh, hbh = rf_ref[h], rb_ref[h], hf_ref[h], hb_ref[h]
                s = _nt(qp, km)
                dpm = _nt(doh, vh)
                ds_b = _mx(dpm * d_ref[h])
                sd_b = _mx(s * d_ref[h])
                dq_if = wqf_ref[h] * _nt(doh, rfh)
                dq_ib = wqb_ref[h] * _nt(doh, rbh)
                dq_acc = dq_acc + _nn(ds_b, km) + dq_if + dq_ib
                dk_sf = wkf_ref[h] * _nt(vh, hfh)
                dk_sb = wkb_ref[h] * _nt(vh, hbh)
                dk_acc = dk_acc + jnp.where(half, _tn(ds_b, qp), 0.0) + dk_sf + dk_sb
                dv = _tn(sd_b, doh) + _nn(_mx(kmf * wkf_ref[h]), hfh) + _nn(_mx(kmf * wkb_ref[h]), hbh)
                dv_ref[:, hs] = dv.astype(dv_ref.dtype)
                sdp = s * dpm
                hr_f = hfh.astype(F32) * rfh.astype(F32)
                hr_b = hbh.astype(F32) * rbh.astype(F32)
                da = (_rowsum128(sdp * dda_ref[h]) + _rowsum128((pos + 1.0) * qf * dq_if)
                      + _rowsum128((c - 1.0 - pos) * kf * dk_sf) + cdec_ref[h:h + 1, :] * _rowsum128(hr_f))
                db = (_rowsum128(sdp * ddb_ref[h]) + _rowsum128((c - pos) * qf * dq_ib)
                      + _rowsum128(pos * kf * dk_sb) + cdec_ref[4 + h:5 + h, :] * _rowsum128(hr_b))
                dlg_ref[h:h + 1, :] += da
                dlg_ref[4 + h:5 + h, :] += db
            dq_ref[:, sl] = dq_acc
            dk_ref[:, sl] = dk_acc

    st = pl.BlockSpec((None, RET_HEADS, 128, 128), lambda n: (n, 0, 0, 0))
    wsp = pl.BlockSpec((RET_HEADS, c, 128), lambda n: (0, 0, 0))
    dsp = pl.BlockSpec((RET_HEADS, c, c), lambda n: (0, 0, 0))
    x256 = pl.BlockSpec((c, 256), lambda n: (n, 0))
    x512 = pl.BlockSpec((c, 512), lambda n: (n, 0))
    cdec = tb["dec_fb"] * float(c)
    return pl.pallas_call(
        body, grid=(seq // c,), name="ret_bwd",
        in_specs=[x256, x256, x512, x512, st, st, st, st, dsp, dsp, dsp, wsp, wsp, wsp, wsp,
                  pl.BlockSpec((8, 128), lambda n: (0, 0))],
        out_specs=[x256, x256, x512, pl.BlockSpec((8, 128), lambda n: (0, 0))],
        out_shape=[jax.ShapeDtypeStruct((seq, 256), F32), jax.ShapeDtypeStruct((seq, 256), F32),
                   jax.ShapeDtypeStruct((seq, 512), MXU_DTYPE), jax.ShapeDtypeStruct((8, 128), F32)],
        compiler_params=_cp("arbitrary"))(
            rq, rk, rv, do, rf, rb, hf, hb, tb["dmat"], tb["dda"], tb["ddb"], tb["wqf"], tb["wqb"], tb["wkf"], tb["wkb"], cdec)


def _tail(x, tgt, o_att, o_ret, p, mod, g_post, gn_g, wpt, wout):
    seq = x.shape[0]
    bs = 256
    nb = seq // bs

    def body(x_ref, t_ref, oa_ref, or_ref, za_ref, zr_ref, gl_ref, mod_ref, gp_ref, gn_ref, wpt_ref, wout_ref,
             dout_ref, dza_ref, dzr_ref, dgl_ref, doa_ref, dor_ref, del_ref, gout_ref, gpt_ref,
             dgate_ref, dgpost_ref, dgn_ref, loss_ref, gout_acc, gpt_acc):
        i = pl.program_id(0)

        @pl.when(i == 0)
        def _():
            gout_acc[...] = jnp.zeros_like(gout_acc)
            gpt_acc[...] = jnp.zeros_like(gpt_acc)
            dgate_ref[...] = jnp.zeros_like(dgate_ref)
            dgpost_ref[...] = jnp.zeros_like(dgpost_ref)
            dgn_ref[...] = jnp.zeros_like(dgn_ref)
            loss_ref[...] = jnp.zeros_like(loss_ref)

        oa, za, zr = oa_ref[...], za_ref[...], zr_ref[...]
        sga, sgr = _sigmoid(za), _sigmoid(zr)
        sza, szr = za * sga, zr * sgr
        ya_b = _mx(oa * sza)
        ons, rstds = [], []
        for h in range(RET_HEADS):
            oh = or_ref[:, 128 * h:128 * (h + 1)]
            xc = oh - jnp.mean(oh, axis=-1, keepdims=True)
            rstd = lax.rsqrt(jnp.mean(xc * xc, axis=-1, keepdims=True) + EPS)
            ons.append(xc * rstd)
            rstds.append(rstd)
        on = jnp.concatenate(ons, axis=1)
        yn = on * gn_ref[...]
        yr_b = _mx(yn * szr)
        wpa_t, wpr_t = wpt_ref[:, 0:512], wpt_ref[:, 512:1024]
        a_att = _nt(ya_b, wpa_t)
        a_ret = _nt(yr_b, wpr_t)
        gts = _sigmoid(gl_ref[...])
        g_att, g_ret = gts[:, 0:D_MODEL], gts[:, D_MODEL:2 * D_MODEL]
        merged_b = _mx(g_att * a_att + g_ret * a_ret)
        u = _nn(merged_b, wout_ref[...])
        r = lax.rsqrt(jnp.mean(u * u, axis=-1, keepdims=True) + EPS)
        un = u * r
        gpost = gp_ref[...]
        y = un * gpost
        gate = mod_ref[:, 2 * D_MODEL:3 * D_MODEL]
        err = x_ref[...] + gate * y - t_ref[...]
        loss_ref[...] += (0.5 / D_MODEL) * _rowsum128(err * err)
        dout = err * (1.0 / D_MODEL)
        dout_ref[...] = dout
        dgate_ref[...] += jnp.sum(dout * y, axis=0, keepdims=True)
        dy = dout * gate
        dgpost_ref[...] += jnp.sum(dy * un, axis=0, keepdims=True)
        dun = dy * gpost
        du_b = _mx(r * (dun - un * jnp.mean(dun * un, axis=-1, keepdims=True)))
        dmerged = _nt(du_b, wout_ref[...])
        gout_acc[...] += _tn(merged_b, du_b)
        d_att, d_ret = dmerged * g_att, dmerged * g_ret
        dgl_ref[:, 0:D_MODEL] = (d_att * a_att * (1.0 - g_att)).astype(dgl_ref.dtype)
        dgl_ref[:, D_MODEL:2 * D_MODEL] = (d_ret * a_ret * (1.0 - g_ret)).astype(dgl_ref.dtype)
        d_att_b, d_ret_b = _mx(d_att), _mx(d_ret)
        dya = _nn(d_att_b, wpa_t)
        dyr = _nn(d_ret_b, wpr_t)
        gpt_acc[:, 0:512] += _tn(d_att_b, ya_b)
        gpt_acc[:, 512:1024] += _tn(d_ret_b, yr_b)
        dza_ref[...] = (dya * oa * (sga * (1.0 + za * (1.0 - sga)))).astype(dza_ref.dtype)
        doa = dya * sza
        doa_ref[...] = doa.astype(doa_ref.dtype)
        dt = jnp.transpose(doa * oa)
        del_ref[...] = jnp.sum(dt.reshape(8, HEAD_DIM, bs), axis=1)
        dzr_ref[...] = (dyr * yn * (sgr * (1.0 + zr * (1.0 - sgr)))).astype(dzr_ref.dtype)
        dyn = dyr * szr
        dgn_ref[...] += jnp.sum(dyn * on, axis=0, keepdims=True)
        don = dyn * gn_ref[...]
        for h in range(RET_HEADS):
            hs = slice(128 * h, 128 * (h + 1))
            dh, oh = don[:, hs], ons[h]
            doh = rstds[h] * (dh - jnp.mean(dh, axis=-1, keepdims=True) - oh * jnp.mean(dh * oh, axis=-1, keepdims=True))
            dor_ref[:, hs] = doh.astype(dor_ref.dtype)

        @pl.when(i == nb - 1)
        def _():
            gout_ref[...] = gout_acc[...].astype(gout_ref.dtype)
            gpt_ref[...] = gpt_acc[...].astype(gpt_ref.dtype)

    row = lambda w: pl.BlockSpec((bs, w), lambda i: (i, 0))
    el = lambda w, off: pl.BlockSpec((pl.Element(bs), pl.Element(w)), lambda i: (i * bs, off))
    vec = lambda w: pl.BlockSpec((1, w), lambda i: (0, 0))
    full = pl.BlockSpec((D_MODEL, D_MODEL), lambda i: (0, 0))
    sds = jax.ShapeDtypeStruct
    return pl.pallas_call(
        body, grid=(nb,), name="tail",
        in_specs=[row(D_MODEL), row(D_MODEL), row(512), row(512), el(512, ZA_B), el(512, ZR_B), el(2 * D_MODEL, GL_B),
                  vec(3 * D_MODEL), vec(D_MODEL), vec(512), full, full],
        out_specs=[row(D_MODEL), row(512), row(512), row(2 * D_MODEL), row(512), row(512),
                   pl.BlockSpec((8, bs), lambda i: (0, i)), full, full, vec(D_MODEL), vec(D_MODEL), vec(512), vec(128)],
        out_shape=[sds((seq, D_MODEL), F32), sds((seq, 512), MXU_DTYPE), sds((seq, 512), MXU_DTYPE),
                   sds((seq, 2 * D_MODEL), MXU_DTYPE), sds((seq, 512), MXU_DTYPE), sds((seq, 512), MXU_DTYPE),
                   sds((8, seq), F32), sds((D_MODEL, D_MODEL), MXU_DTYPE), sds((D_MODEL, D_MODEL), MXU_DTYPE),
                   sds((1, D_MODEL), F32), sds((1, D_MODEL), F32), sds((1, 512), F32), sds((1, 128), F32)],
        scratch_shapes=[pltpu.VMEM((D_MODEL, D_MODEL), F32), pltpu.VMEM((D_MODEL, D_MODEL), F32)],
        compiler_params=_cp("arbitrary"))(x, tgt, o_att, o_ret, p, p, p, mod, g_post, gn_g, wpt, wout)


def _assemble(p, cos, sin, qg2, kg2, dqt, dkp, dvp, dza, drq, drk, drv, dzr, dgl):
    seq = p.shape[0]
    bs = 512

    def body(p_ref, cos_ref, sin_ref, qg_ref, kg_ref, dqt_ref, dkp_ref, dvp_ref, dza_ref, drq_ref, drk_ref, drv_ref,
             dzr_ref, dgl_ref, dp_ref, dqg_ref, dkg_ref):
        @pl.when(pl.program_id(0) == 0)
        def _():
            dqg_ref[...] = jnp.zeros_like(dqg_ref)
            dkg_ref[...] = jnp.zeros_like(dkg_ref)

        m = _blockdiag64()
        cs, sn = cos_ref[...], sin_ref[...]
        lo = _lane((bs, 128)) < 64

        def norm_bwd(raw, dn, g, dg_ref):
            r = lax.rsqrt(_seg64(raw * raw, m) * (1.0 / 64) + EPS)
            xn = raw * r
            dg_ref[...] += jnp.sum(dn * xn, axis=0, keepdims=True)
            dxn = dn * g
            return r * (dxn - xn * (_seg64(dxn * xn, m) * (1.0 / 64)))

        for pr in range(4):
            sl = slice(128 * pr, 128 * (pr + 1))
            dn = _rope_t(dqt_ref[:, sl] * 0.125, cs, sn)
            dp_ref[:, QA + 128 * pr:QA + 128 * (pr + 1)] = norm_bwd(p_ref[:, sl], dn, qg_ref[...], dqg_ref).astype(dp_ref.dtype)
        fold = lambda a: a + pltpu.roll(a, 64, 1)
        dk = jnp.where(lo, fold(dkp_ref[0]), fold(dkp_ref[1]))
        dp_ref[:, KA:KA + 128] = norm_bwd(p_ref[:, KA:KA + 128], _rope_t(dk, cs, sn), kg_ref[...], dkg_ref).astype(dp_ref.dtype)
        dp_ref[:, VA:VA + 128] = jnp.where(lo, fold(dvp_ref[0]), fold(dvp_ref[1])).astype(dp_ref.dtype)
        dp_ref[:, ZA:ZA + 512] = dza_ref[...]
        for pr in range(2):
            sl = slice(128 * pr, 128 * (pr + 1))
            dp_ref[:, QR + 128 * pr:QR + 128 * (pr + 1)] = _rope_t(drq_ref[:, sl], cs, sn).astype(dp_ref.dtype)
            dp_ref[:, KR + 128 * pr:KR + 128 * (pr + 1)] = _rope_t(drk_ref[:, sl] * 0.125, cs, sn).astype(dp_ref.dtype)
        dp_ref[:, VR:VR + 512] = drv_ref[...]
        dp_ref[:, ZR:ZR + 512] = dzr_ref[...]
        dp_ref[:, GL:GL + 2 * D_MODEL] = dgl_ref[...]

    row = lambda w: pl.BlockSpec((bs, w), lambda i: (i, 0))
    gsp = pl.BlockSpec((1, 128), lambda i: (0, 0))
    dup = pl.BlockSpec((2, bs, 128), lambda i: (0, i, 0))
    return pl.pallas_call(
        body, grid=(seq // bs,), name="assemble_dp",
        in_specs=[row(768), row(128), row(128), gsp, gsp, row(512), dup, dup, row(512), row(256), row(256), row(512),
                  row(512), row(2 * D_MODEL)],
        out_specs=[row(IN_WIDTH), gsp, gsp],
        out_shape=[jax.ShapeDtypeStruct((seq, IN_WIDTH), MXU_DTYPE), jax.ShapeDtypeStruct((1, 128), F32),
                   jax.ShapeDtypeStruct((1, 128), F32)],
        compiler_params=_cp("arbitrary"))(p, cos, sin, qg2, kg2, dqt, dkp, dvp, dza, drq, drk, drv, dzr, dgl)


def _local_step(x, tgt, mod, g_pre, qn_g, kn_g, w_dec_f, w_dec_b, gn_g, g_post, w_attn, rest_start, rest_wait, send=None):
    send = send or (lambda name, arrays: (None, arrays))
    seq = x.shape[0]
    cos, sin = _rope_tables(seq)
    qg2, kg2 = jnp.tile(qn_g, (1, 2)), jnp.tile(kn_g, (1, 2))
    lg_f = jax.nn.log_sigmoid(w_dec_f[0])
    lg_b = jax.nn.log_sigmoid(w_dec_b[0])
    tb = _ret_tables(lg_f, lg_b, RET_CHUNK)

    h, p_a, qt, kd, vd = _attn_inputs(x, mod, g_pre, w_attn, cos, sin, qg2, kg2)
    o_att, lse = _attn_fwd(qt, kd, vd, dep=rest_start(qt))
    win_t, wpt, wout = rest_wait(o_att)
    p_b, rq, rk, rv = _in_proj_rest(h, win_t, cos, sin)
    rf, rb = _ret_states(rk, rv, tb["wkf"], tb["wkb"], tb["dec_fb"], "ret_states_fwd")
    o_ret = _ret_fwd(rq, rk, rv, rf, rb, tb)
    (dout, dza, dzr, dgl, do_att, do_ret, delta, g_out, g_pt, dgate, dgpost, dgn, loss_l) = _tail(
        x, tgt, o_att, o_ret, p_b, mod, g_post, gn_g, wpt, wout)
    dep, _ = send("early", (g_pt, g_out))
    dqt, dkp, dvp = _attn_bwd(qt, kd, vd, do_att, lse, delta, dep=dep)
    hb, hf = _ret_states(rq, do_ret, tb["wqb"], tb["wqf"], tb["dec_bf"], "ret_states_bwd")
    drq, drk, drv, dlg = _ret_bwd(rq, rk, rv, do_ret, rf, rb, hf, hb, tb)
    dp, dqg, dkg = _assemble(p_a, cos, sin, qg2, kg2, dqt, dkp, dvp, dza, drq, drk, drv, dzr, dgl)
    ntile = IN_WIDTH // DW_TILE
    split = (IN_WIDTH // 2) // DW_TILE
    g_in_t = _in_proj_dw(dp, h, split, ntile - split, None, "in_proj_dw_hi")
    _, (g_in_t,) = send("late_hi", (g_in_t,))
    g_in_t = _in_proj_dw(dp, h, 0, split, g_in_t, "in_proj_dw_lo")
    dep, (g_in_t,) = send("late_lo", (g_in_t,))
    grad_x, dshift, dscale, dgpre = _in_proj_dx(x, dp, win_t, dout, mod, g_pre, dep=dep)

    dlg = jnp.sum(dlg, axis=1)
    small = dict(
        dmod=jnp.concatenate([dshift, dscale, dgate], axis=1),
        g_pre=dgpre, g_post=dgpost, gn_g=dgn,
        qn_g=dqg[:, :64] + dqg[:, 64:], kn_g=dkg[:, :64] + dkg[:, 64:],
        w_dec_f=(dlg[0:4] * jax.nn.sigmoid(-w_dec_f[0]))[None], w_dec_b=(dlg[4:8] * jax.nn.sigmoid(-w_dec_b[0]))[None],
        loss=jnp.sum(loss_l, axis=1, keepdims=True))
    return grad_x, g_in_t, g_pt, g_out, small


N_DEV = 8
N_CHIP = 4
HBM_SPEC = pl.BlockSpec(memory_space=pl.ANY)
VMEM_SPEC = pl.BlockSpec(memory_space=pltpu.VMEM)
SHARD_ROWS = (IN_WIDTH // N_CHIP, D_MODEL // N_CHIP, D_MODEL // N_CHIP)


def _coords():
    return lax.axis_index("x"), lax.axis_index("y"), lax.axis_index("c")


def _peer(k):
    x, y, c = _coords()
    return (1 - x if k & 4 else x, 1 - y if k & 2 else y, 1 - c if k & 1 else c)


def _dev_index(p):
    return 4 * p[0] + 2 * p[1] + p[2]


def _rows(ref, start, size):
    return ref.at[pl.ds(pl.multiple_of(start, 16), size), :]


def _remote(src, dst, ssem, rsem, dev):
    return pltpu.make_async_remote_copy(src_ref=src, dst_ref=dst, send_sem=ssem, recv_sem=rsem, device_id=dev,
                                        device_id_type=MESH)


ATTN_CHUNK = 96


def _mod_and_attn_rows(c, w_ada_s, b4, win_s):
    ncol = w_ada_s.shape[1]
    half = ATTN_COLS // 2
    offs = list(range(0, half, ATTN_CHUNK))
    nq = len(offs)
    rows = lambda ref, cc, off: ref.at[pl.ds(pl.multiple_of(cc * half + off, 16), ATTN_CHUNK), :]

    def body(c_ref, w_ref, b_ref, src, cs_ref, mod_ref, out, modsh, modall, ssem, rsem, lsem, asem, bsem, fsem, gsem):
        x, y, cc = _coords()
        me = _dev_index((x, y, cc))
        chip = me // 2
        sib = (x, y, 1 - cc)
        cs_ref[me] = c_ref[...]
        sends = []
        for k in range(1, N_DEV):
            cp = _remote(c_ref, cs_ref.at[me], ssem.at[k - 1], rsem.at[k - 1], _peer(k))
            cp.start()
            sends.append(cp)
        own = pltpu.make_async_copy(src.at[pl.ds(0, ATTN_COLS), :], out, lsem.at[0])
        bulk = [_remote(rows(src, cc, off), rows(out, cc, off), asem.at[(k2 - 1) * nq + q], bsem.at[q], (k2 // 2, k2 % 2, cc))
                for k2 in range(1, N_CHIP) for q, off in enumerate(offs)]

        @pl.when(chip == 0)
        def _():
            own.start()
            for cp in bulk:
                cp.start()

        for k in range(1, N_DEV):
            slot = cs_ref.at[_dev_index(_peer(k))]
            _remote(slot, slot, ssem.at[k - 1], rsem.at[k - 1], _peer(k)).wait_recv()
        cs = jnp.concatenate([cs_ref[d] for d in range(N_DEV)], axis=0)
        ms = _nn(cs * _sigmoid(cs), w_ref[...], precision=HIGHEST) + b_ref[pl.ds(chip, 1), :]
        modsh[...] = ms
        modall[chip] = ms
        for k2 in range(1, N_CHIP):
            cp = _remote(modsh, modall.at[chip], ssem.at[6 + k2], rsem.at[6 + k2], _peer(2 * k2))
            cp.start()
            sends.append(cp)

        @pl.when(chip != 0)
        def _():
            passed = []
            for q, off in enumerate(offs):
                got = rows(out, cc, off)
                _remote(got, got, asem.at[q], bsem.at[q], (0, 0, cc)).wait_recv()
                cp = _remote(got, got, fsem.at[q], gsem.at[q], sib)
                cp.start()
                passed.append(cp)
            for q, off in enumerate(offs):
                got = rows(out, 1 - cc, off)
                _remote(got, got, fsem.at[q], gsem.at[q], sib).wait_recv()
            for cp in passed:
                cp.wait_send()

        for k2 in range(1, N_CHIP):
            slot = modall.at[_dev_index(_peer(2 * k2)) // 2]
            _remote(slot, slot, ssem.at[6 + k2], rsem.at[6 + k2], _peer(2 * k2)).wait_recv()
        for jj in range(N_CHIP):
            mod_ref[:, ncol * jj:ncol * (jj + 1)] = modall[jj, pl.ds(me, 1), :]
        for cp in sends:
            cp.wait_send()

        @pl.when(chip == 0)
        def _():
            for cp in bulk:
                cp.wait_send()
            own.wait()

    dma = pltpu.SemaphoreType.DMA
    return pl.pallas_call(
        body, name="mod_and_attn_rows", in_specs=[VMEM_SPEC] * 4, out_specs=[VMEM_SPEC, VMEM_SPEC, HBM_SPEC],
        out_shape=[jax.ShapeDtypeStruct((N_DEV, 1, D_MODEL), F32), jax.ShapeDtypeStruct((1, N_CHIP * ncol), F32),
                   jax.ShapeDtypeStruct((ATTN_COLS, win_s.shape[1]), win_s.dtype)],
        scratch_shapes=[pltpu.VMEM((N_DEV, ncol), F32), pltpu.VMEM((N_CHIP, N_DEV, ncol), F32), dma((10,)), dma((10,)),
                        dma((1,)), dma((3 * nq,)), dma((nq,)), dma((nq,)), dma((nq,))],
        compiler_params=_cp())(c, w_ada_s, b4, win_s)


GATHER_CHUNK = 32


def _chunks(kinds, chunk):
    out = []
    for t, kind in enumerate(kinds):
        half = SHARD_ROWS[kind] // 2
        step = chunk if half % chunk == 0 else half
        out += [(t, off, step) for off in range(0, half, step)]
    return out


SEM_SPEC = pl.BlockSpec(memory_space=pltpu.SEMAPHORE)
HBM_ONLY = pl.BlockSpec(memory_space=pltpu.HBM)
DATAFLOW = pltpu.SideEffectType.DATAFLOW_SIDE_EFFECTING


def _place_own(shards):
    nt = len(shards)

    def body(*refs):
        srcs, outs, lsem = refs[:nt], refs[nt:2 * nt], refs[2 * nt]
        x, y, _ = _coords()
        chip = 2 * x + y
        local = [pltpu.make_async_copy(srcs[t], _rows(outs[t], chip * SHARD_ROWS[t], SHARD_ROWS[t]), lsem.at[t]) for t in range(nt)]
        for cp in local:
            cp.start()
        for cp in local:
            cp.wait()

    return pl.pallas_call(
        body, name="place_own_shard", in_specs=[VMEM_SPEC] * nt, out_specs=[HBM_SPEC] * nt,
        out_shape=[jax.ShapeDtypeStruct((N_CHIP * SHARD_ROWS[t], s.shape[1]), s.dtype) for t, s in enumerate(shards)],
        scratch_shapes=[pltpu.SemaphoreType.DMA((nt,))], compiler_params=_cp())(*shards)


def _gather_rest_start(shards, zones, dep):
    n = len(shards)

    def body(*refs):
        srcs, lands = refs[:n], refs[n:2 * n]
        ssem, rsem = refs[2 * n + 1], refs[2 * n + 2]
        token = refs[-1]
        x, y, _ = _coords()
        chip = 2 * x + y
        for k2 in range(1, N_CHIP):
            for t in range(n):
                q = (k2 - 1) * n + t
                _remote(srcs[t], _rows(lands[t], chip * SHARD_ROWS[t], SHARD_ROWS[t]), ssem.at[q], rsem.at[q], _peer(2 * k2)).start()
        token[...] = jnp.zeros_like(token)

    hbm = lambda a: pltpu.with_memory_space_constraint(a, pltpu.HBM)
    dma = pltpu.SemaphoreType.DMA
    outs = pl.pallas_call(
        body, name="gather_rest", in_specs=[HBM_ONLY] * (2 * n) + [HBM_SPEC],
        out_specs=[SEM_SPEC, SEM_SPEC] + [HBM_ONLY] * (2 * n) + [VMEM_SPEC],
        out_shape=[dma((3 * n,)), dma((3 * n,))] + [pltpu.HBM(a.shape, a.dtype) for a in list(shards) + list(zones)]
        + [jax.ShapeDtypeStruct((8, 128), F32)],
        input_output_aliases={i: 2 + i for i in range(2 * n)},
        compiler_params=pltpu.CompilerParams(has_side_effects=DATAFLOW))(*[hbm(a) for a in list(shards) + list(zones)], dep)
    return outs[0], outs[1], outs[2:2 + n], outs[2 + n:2 + 2 * n], outs[-1]


def _gather_rest_wait(started, after):
    ssem, rsem, shards, zones, _ = started
    n = len(shards)

    def body(*refs):
        srcs, lands = refs[:n], refs[n:2 * n]
        ssem_ref, rsem_ref = refs[2 * n], refs[2 * n + 1]
        for k2 in range(1, N_CHIP):
            pchip = _dev_index(_peer(2 * k2)) // 2
            for t in range(n):
                q = (k2 - 1) * n + t
                cp = _remote(srcs[t], _rows(lands[t], pchip * SHARD_ROWS[t], SHARD_ROWS[t]), ssem_ref.at[q], rsem_ref.at[q], _peer(2 * k2))
                cp.wait_send()
                cp.wait_recv()

    outs = pl.pallas_call(
        body, name="gather_rest_wait", in_specs=[HBM_ONLY] * (2 * n) + [SEM_SPEC, SEM_SPEC, HBM_SPEC], out_specs=[HBM_ONLY] * (2 * n),
        out_shape=[pltpu.HBM(a.shape, a.dtype) for a in list(shards) + list(zones)],
        input_output_aliases={i: i for i in range(2 * n)},
        compiler_params=pltpu.CompilerParams(has_side_effects=DATAFLOW))(*shards, *zones, ssem, rsem, after)
    return outs[n:]


def _reduced_by(ref, t, dev):
    return _rows(ref, (dev // 2) * SHARD_ROWS[t] + (dev % 2) * (SHARD_ROWS[t] // 2), SHARD_ROWS[t] // 2)


ALL_DEVICES = (0, N_DEV)


def _among(dev, owners):
    return (dev >= owners[0]) & (dev < owners[1])


def _scatter_start(grads, kinds, name, owners=ALL_DEVICES):
    n = len(grads)

    def body(*refs):
        srcs, lands = refs[:n], refs[n:2 * n]
        ssem, rsem = refs[2 * n], refs[2 * n + 1]
        token = refs[-1]
        me = _dev_index(_coords())
        for k in range(1, N_DEV):
            peer = _dev_index(_peer(k))

            @pl.when(_among(peer, owners))
            def _():
                for i, t in enumerate(kinds):
                    q = (k - 1) * n + i
                    _remote(_reduced_by(srcs[i], t, peer), lands[i].at[me], ssem.at[q], rsem.at[q], _peer(k)).start()
        token[...] = jnp.zeros_like(token)

    zones = [lax.empty((N_DEV, SHARD_ROWS[t] // 2, g.shape[1]), g.dtype) for g, t in zip(grads, kinds)]
    hbm = lambda a: pltpu.with_memory_space_constraint(a, pltpu.HBM)
    dma = pltpu.SemaphoreType.DMA
    outs = pl.pallas_call(
        body, name=name, in_specs=[HBM_ONLY] * (2 * n), out_specs=[SEM_SPEC, SEM_SPEC] + [HBM_ONLY] * (2 * n) + [VMEM_SPEC],
        out_shape=[dma((7 * n,)), dma((7 * n,))] + [pltpu.HBM(a.shape, a.dtype) for a in list(grads) + zones]
        + [jax.ShapeDtypeStruct((8, 128), F32)],
        input_output_aliases={i: 2 + i for i in range(2 * n)},
        compiler_params=pltpu.CompilerParams(has_side_effects=DATAFLOW))(*[hbm(a) for a in list(grads) + zones])
    return outs[0], outs[1], outs[2:2 + n], outs[2 + n:2 + 2 * n], outs[-1]


def _scatter_wait(started, kinds, after, name, owners=ALL_DEVICES):
    ssem, rsem, grads, zones, _ = started
    n = len(grads)

    def body(*refs):
        srcs, lands = refs[:n], refs[n:2 * n]
        ssem_ref, rsem_ref = refs[2 * n], refs[2 * n + 1]
        me = _dev_index(_coords())
        for k in range(1, N_DEV):
            peer = _dev_index(_peer(k))
            cps = [_remote(_reduced_by(srcs[i], t, peer), lands[i].at[peer], ssem_ref.at[(k - 1) * n + i],
                           rsem_ref.at[(k - 1) * n + i], _peer(k)) for i, t in enumerate(kinds)]

            @pl.when(_among(peer, owners))
            def _():
                for cp in cps:
                    cp.wait_send()

            @pl.when(_among(me, owners))
            def _():
                for cp in cps:
                    cp.wait_recv()

    outs = pl.pallas_call(
        body, name=name, in_specs=[HBM_ONLY] * (2 * n) + [SEM_SPEC, SEM_SPEC, HBM_SPEC], out_specs=[HBM_ONLY] * (2 * n),
        out_shape=[pltpu.HBM(a.shape, a.dtype) for a in list(grads) + list(zones)],
        input_output_aliases={i: i for i in range(2 * n)},
        compiler_params=pltpu.CompilerParams(has_side_effects=DATAFLOW))(*grads, *zones, ssem, rsem, after)
    return outs[:n], outs[n:]


def _grad_finish(grads, zones, kinds, name, zone_owners=(ALL_DEVICES,)):
    nt = len(grads)
    ng = len(zone_owners)
    pieces = _chunks(kinds, GATHER_CHUNK)
    npc = len(pieces)
    shard = [SHARD_ROWS[k] for k in kinds]

    def body(*refs):
        srcs, lands, outs = refs[:nt], refs[nt:(1 + ng) * nt], refs[(1 + ng) * nt:(2 + ng) * nt]
        slots, sums = refs[(2 + ng) * nt:(3 + ng) * nt], refs[(3 + ng) * nt:(4 + ng) * nt]
        lsem, osem, xsem, ysem = refs[(4 + ng) * nt:]
        x, y, c = _coords()
        me = _dev_index((x, y, c))
        sib = (x, y, 1 - c)
        own = [pltpu.make_async_copy(_reduced_by(srcs[t], kinds[t], me), slots[t].at[me], lsem.at[t]) for t in range(nt)]
        for cp in own:
            cp.start()
        for g, owners in enumerate(zone_owners):
            @pl.when(_among(me, owners))
            def _():
                loads = []
                for k in range(1, N_DEV):
                    peer = _dev_index(_peer(k))
                    loads += [pltpu.make_async_copy(lands[g * nt + t].at[peer], slots[t].at[peer], lsem.at[k * nt + t])
                              for t in range(nt)]
                for cp in loads:
                    cp.start()
                for cp in loads:
                    cp.wait()
        for cp in own:
            cp.wait()
        sends = []
        place = lambda t, cc, off, n: _rows(outs[t], cc * (shard[t] // 2) + off, n)
        stores = []
        for q, (t, off, n) in enumerate(pieces):
            r = pl.ds(off, n)
            acc = slots[t][0, r, :].astype(F32)
            for d in range(1, N_DEV):
                acc = acc + slots[t][d, r, :].astype(F32)
            sums[t][r, :] = acc
            keep = pltpu.make_async_copy(sums[t].at[r, :], place(t, c, off, n), osem.at[q])
            give = _remote(sums[t].at[r, :], place(t, c, off, n), xsem.at[q], ysem.at[q], sib)
            keep.start()
            give.start()
            stores.append(keep)
            sends.append(give)
        for q, (t, off, n) in enumerate(pieces):
            got = place(t, 1 - c, off, n)
            _remote(got, got, xsem.at[q], ysem.at[q], sib).wait_recv()
        for cp in sends:
            cp.wait_send()
        for cp in stores:
            cp.wait()

    dma = pltpu.SemaphoreType.DMA
    return pl.pallas_call(
        body, name=name, in_specs=[HBM_SPEC] * ((1 + ng) * nt), out_specs=[HBM_SPEC] * nt,
        out_shape=[jax.ShapeDtypeStruct((shard[t], g.shape[1]), F32) for t, g in enumerate(grads)],
        scratch_shapes=([pltpu.VMEM((N_DEV, shard[t] // 2, g.shape[1]), g.dtype) for t, g in enumerate(grads)]
                        + [pltpu.VMEM((shard[t] // 2, g.shape[1]), F32) for t, g in enumerate(grads)]
                        + [dma((N_DEV * nt,)), dma((npc,)), dma((npc,)), dma((npc,))]),
        compiler_params=_cp())(*grads, *[z for group in zones for z in group])


def _adam_math(w, g, m, v):
    m = ADAM_B1 * m + (1.0 - ADAM_B1) * g
    v = ADAM_B2 * v + (1.0 - ADAM_B2) * (g * g)
    m_hat = m / (1.0 - ADAM_B1 ** ADAM_STEP)
    v_hat = v / (1.0 - ADAM_B2 ** ADAM_STEP)
    return -ADAM_LR * (m_hat / (jnp.sqrt(v_hat) + ADAM_EPS) + ADAM_WD * w), m, v


def _adamw(w, g, m, v, rb, name):
    rows, cols = w.shape

    def body(w_ref, g_ref, m_ref, v_ref, d_ref, nm_ref, nv_ref):
        d_ref[...], nm_ref[...], nv_ref[...] = _adam_math(w_ref[...], g_ref[...], m_ref[...], v_ref[...])

    spec = pl.BlockSpec((rb, cols), lambda i: (i, 0))
    return pl.pallas_call(body, grid=(rows // rb,), name=name, in_specs=[spec] * 4, out_specs=[spec] * 3,
                          out_shape=[jax.ShapeDtypeStruct(w.shape, F32)] * 3, compiler_params=_cp("parallel"))(w, g, m, v)


PACK = (("b_ada", 3 * D_MODEL), ("g_pre", D_MODEL), ("g_post", D_MODEL), ("gn_g", 512), ("qn_g", 64), ("kn_g", 64),
        ("w_dec_f", 4), ("w_dec_b", 4), ("loss", 1))
PACK_OFFSETS = {}
PACK_WIDTH = 0
for _name, _n in PACK:
    PACK_OFFSETS[_name] = PACK_WIDTH
    PACK_WIDTH += -(-_n // 128) * 128
SMALL_PARAMS = tuple(name for name, _ in PACK if name != "loss")


def _pack(parts):
    cols = []
    for name, n in PACK:
        pad = -(-n // 128) * 128 - n
        cols.append(parts[name].reshape(1, n).astype(F32))
        if pad:
            cols.append(jnp.zeros((1, pad), F32))
    return jnp.concatenate(cols, axis=1)


def _small_reduce(vec, cs, dep):
    ncol = 3 * D_MODEL // N_CHIP

    def body(vec_ref, cs_ref, dep_ref, tot_ref, gwa_ref, gat, ssem, rsem):
        me = _dev_index(_coords())
        chip = me // 2
        gat[me] = vec_ref[...]
        sends = []
        for k in range(1, N_DEV):
            cp = _remote(vec_ref, gat.at[me], ssem.at[k - 1], rsem.at[k - 1], _peer(k))
            cp.start()
            sends.append(cp)
        for k in range(1, N_DEV):
            slot = gat.at[_dev_index(_peer(k))]
            _remote(slot, slot, ssem.at[k - 1], rsem.at[k - 1], _peer(k)).wait_recv()
        rows = [gat[d] for d in range(N_DEV)]
        tot = rows[0]
        for d in range(1, N_DEV):
            tot = tot + rows[d]
        tot_ref[...] = tot
        cs = cs_ref[...]
        ca_t = jnp.transpose(jnp.concatenate([cs * _sigmoid(cs), jnp.zeros((128 - N_DEV, D_MODEL), F32)], axis=0))
        dm = jnp.concatenate(rows + [jnp.zeros((128 - N_DEV, PACK_WIDTH), F32)], axis=0)
        for jj in range(N_CHIP):
            @pl.when(chip == jj)
            def _():
                gwa_ref[...] = _nn(ca_t, dm[:, ncol * jj:ncol * (jj + 1)], precision=HIGHEST)
        for cp in sends:
            cp.wait_send()

    return pl.pallas_call(
        body, name="small_reduce", in_specs=[VMEM_SPEC, VMEM_SPEC, HBM_SPEC], out_specs=[VMEM_SPEC] * 2,
        out_shape=[jax.ShapeDtypeStruct((1, PACK_WIDTH), F32), jax.ShapeDtypeStruct((D_MODEL, ncol), F32)],
        scratch_shapes=[pltpu.VMEM((N_DEV, 1, PACK_WIDTH), F32), pltpu.SemaphoreType.DMA((7,)), pltpu.SemaphoreType.DMA((7,))],
        compiler_params=_cp())(vec, cs, dep)


def _small_adamw(tot, given):
    sizes = dict(PACK)
    np_ = len(SMALL_PARAMS)

    def body(*refs):
        tot_ref = refs[0]
        wmv = refs[1:1 + 3 * np_]
        outs = refs[1 + 3 * np_:1 + 7 * np_]
        loss_ref = refs[-1]
        for i, name in enumerate(SMALL_PARAMS):
            off, n = PACK_OFFSETS[name], sizes[name]
            g = tot_ref[:, off:off + n]
            w_ref, m_ref, v_ref = wmv[3 * i:3 * i + 3]
            outs[i][...] = g
            outs[np_ + i][...], outs[2 * np_ + i][...], outs[3 * np_ + i][...] = _adam_math(w_ref[...], g, m_ref[...], v_ref[...])
        loss_ref[...] = tot_ref[:, PACK_OFFSETS["loss"]:PACK_OFFSETS["loss"] + 1]

    flat = [a for name in SMALL_PARAMS for a in given[name]]
    shapes = [jax.ShapeDtypeStruct((1, sizes[name]), F32) for name in SMALL_PARAMS]
    res = pl.pallas_call(body, name="small_adamw", in_specs=[VMEM_SPEC] * (1 + 3 * np_), out_specs=[VMEM_SPEC] * (4 * np_ + 1),
                         out_shape=shapes * 4 + [jax.ShapeDtypeStruct((1, 1), F32)], compiler_params=_cp())(tot, *flat)
    return [dict(zip(SMALL_PARAMS, res[k * np_:(k + 1) * np_])) for k in range(4)], res[-1]


def kernel(x, c, w_ada, b_ada, g_pre, w_in, qn_g, kn_g, w_dec_f, w_dec_b, gn_g, w_pa, w_pr, w_out, g_post, loss_target, m_w_ada, m_b_ada, m_g_pre, m_w_in, m_qn_g, m_kn_g, m_w_dec_f, m_w_dec_b, m_gn_g, m_w_pa, m_w_pr, m_w_out, m_g_post, v_w_ada, v_b_ada, v_g_pre, v_w_in, v_qn_g, v_kn_g, v_w_dec_f, v_w_dec_b, v_gn_g, v_w_pa, v_w_pr, v_w_out, v_g_post):
    shards = (w_in[0].T.astype(MXU_DTYPE), jnp.concatenate([w_pa[0].T, w_pr[0].T], axis=1).astype(MXU_DTYPE),
              w_out[0].astype(MXU_DTYPE))
    cs, mod, w_attn = _mod_and_attn_rows(c, w_ada[0], b_ada.reshape(N_CHIP, 3 * D_MODEL // N_CHIP), shards[0])
    started = {}

    def rest_start(dep):
        started["rest"] = _gather_rest_start(shards, _place_own(shards), dep)
        return started["rest"][-1]

    def rest_wait(after):
        return _gather_rest_wait(started["rest"], after)

    kinds = {"early": (1, 2), "late_hi": (0,), "late_lo": (0,)}
    owners = {"early": ALL_DEVICES, "late_hi": (N_DEV // 2, N_DEV), "late_lo": (0, N_DEV // 2)}

    def send(name, arrays):
        started[name] = _scatter_start(arrays, kinds[name], "grad_scatter_" + name, owners[name])
        return started[name][-1], started[name][2]

    grad_x, _, _, _, small = _local_step(x[0], loss_target[0], mod, g_pre, qn_g, kn_g, w_dec_f, w_dec_b,
                                         gn_g, g_post, w_attn, rest_start, rest_wait, send=send)
    small = dict(small)
    small["b_ada"] = small.pop("dmod")
    given = dict(b_ada=(b_ada, m_b_ada, v_b_ada), g_pre=(g_pre, m_g_pre, v_g_pre), g_post=(g_post, m_g_post, v_g_post),
                 gn_g=(gn_g, m_gn_g, v_gn_g), qn_g=(qn_g, m_qn_g, v_qn_g), kn_g=(kn_g, m_kn_g, v_kn_g),
                 w_dec_f=(w_dec_f, m_w_dec_f, v_w_dec_f), w_dec_b=(w_dec_b, m_w_dec_b, v_w_dec_b))
    grads, deltas, new_m, new_v = {}, {}, {}, {}

    def update(name, w, g, m, v, back):
        d, nm, nv = _adamw(w, g, m, v, w.shape[0] // 4, "adamw_" + name)
        grads[name], deltas[name], new_m[name], new_v[name] = back(g), back(d), back(nm), back(nv)
        return d

    lead = lambda a: a[None]
    (g_pt, g_out), (z_pt, z_out) = _scatter_wait(started["early"], kinds["early"], grad_x, "grad_scatter_early_wait")
    r_pt, r_out = _grad_finish((g_pt, g_out), ((z_pt, z_out),), kinds["early"], "grad_finish_early")
    update("w_pa", w_pa[0], r_pt[:, :512].T, m_w_pa[0], v_w_pa[0], lead)
    update("w_pr", w_pr[0], r_pt[:, 512:].T, m_w_pr[0], v_w_pr[0], lead)
    last = update("w_out", w_out[0], r_out, m_w_out[0], v_w_out[0], lead)
    tot, g_w_ada = _small_reduce(_pack(small), cs.reshape(N_DEV, D_MODEL), last)
    small_parts, loss = _small_adamw(tot, given)
    for dst, part in zip((grads, deltas, new_m, new_v), small_parts):
        dst.update(part)
    last = update("w_ada", w_ada[0], g_w_ada, m_w_ada[0], v_w_ada[0], lead)

    hi, lo = started["late_hi"], started["late_lo"]
    g_in_t, z_hi = _scatter_wait((hi[0], hi[1], lo[2], hi[3], None), kinds["late_hi"], last, "grad_scatter_late_hi_wait", owners["late_hi"])
    g_in_t, z_lo = _scatter_wait((lo[0], lo[1], g_in_t, lo[3], None), kinds["late_lo"], last, "grad_scatter_late_lo_wait", owners["late_lo"])
    (r_in,) = _grad_finish(g_in_t, (z_hi, z_lo), kinds["late_lo"], "grad_finish_late", (owners["late_hi"], owners["late_lo"]))
    tr = lambda a: a[0].T
    update("w_in", tr(w_in), r_in, tr(m_w_in), tr(v_w_in), lambda a: a.T[None])
    order = ("w_ada", "b_ada", "g_pre", "w_in", "qn_g", "kn_g", "w_dec_f", "w_dec_b", "gn_g", "w_pa", "w_pr", "w_out", "g_post")
    return (loss[0, 0], grad_x[None], *[grads[n] for n in order], *[deltas[n] for n in order],
            *[new_m[n] for n in order], *[new_v[n] for n in order])
```

```python
import functools

import jax
import jax.numpy as jnp
import numpy as np
from jax import lax
from jax.experimental import pallas as pl
from jax.experimental.pallas import tpu as pltpu

F32 = jnp.float32
BF16 = jnp.bfloat16
MXU_DTYPE = jnp.bfloat16

D_MODEL = 1024
GRID_W = 64
HEAD_DIM = 64
ROPE_THETA = 10000.0
EPS = 1e-6
RET_HEADS = 4
RET_CHUNK = 256
IN_WIDTH = 4864
QA, KA, VA, ZA, QR, KR, VR, ZR, GL = 0, 512, 640, 768, 1280, 1536, 1792, 2304, 2816
ATTN_COLS = ZA
ZA_B, QR_B, KR_B, VR_B, ZR_B, GL_B = (c - ATTN_COLS for c in (ZA, QR, KR, VR, ZR, GL))

ADAM_LR, ADAM_B1, ADAM_B2, ADAM_EPS, ADAM_WD, ADAM_STEP = 0.001, 0.9, 0.999, 1e-08, 0.01, 10

VMEM_LIMIT = 56 * 1024 * 1024
MESH = pl.DeviceIdType.MESH
HIGHEST = lax.Precision.HIGHEST
LOG2E = 1.4426950408889634
LN2 = 0.6931471805599453


def _cp(*sem, **kw):
    if sem:
        kw["dimension_semantics"] = sem
    return pltpu.CompilerParams(vmem_limit_bytes=VMEM_LIMIT, **kw)


def _nn(a, b, **kw):
    return lax.dot_general(a, b, (((1,), (0,)), ((), ())), preferred_element_type=F32, **kw)


def _nt(a, b):
    return lax.dot_general(a, b, (((1,), (1,)), ((), ())), preferred_element_type=F32)


def _tn(a, b):
    return lax.dot_general(a, b, (((0,), (0,)), ((), ())), preferred_element_type=F32)


def _mx(x):
    return x.astype(MXU_DTYPE)


def _lane(shape):
    return lax.broadcasted_iota(jnp.int32, shape, 1)


def _sigmoid(z):
    return 1.0 / (1.0 + jnp.exp(-z))


def _rowsum128(x):
    s = jnp.sum(x, axis=0, keepdims=True)
    out = s[:, 0:128]
    for k in range(1, x.shape[1] // 128):
        out = out + s[:, 128 * k:128 * (k + 1)]
    return out


def _swap16(x):
    return jnp.where((_lane(x.shape) & 16) == 0, pltpu.roll(x, 112, 1), pltpu.roll(x, 16, 1))


def _rope(x, cos, sin):
    return x * cos + _swap16(x) * sin


def _rope_t(d, cos, sin):
    return d * cos + _swap16(d * sin)


def _blockdiag64():
    r = lax.broadcasted_iota(jnp.int32, (128, 128), 0) // 64
    c = lax.broadcasted_iota(jnp.int32, (128, 128), 1) // 64
    return (r == c).astype(BF16)


def _seg64(x, m):
    hi = x.astype(BF16)
    lo = (x - hi.astype(F32)).astype(BF16)
    return _nn(hi, m) + _nn(lo, m)


def _rope_tables(seq):
    t = np.arange(seq)
    row = (t // GRID_W).astype(np.float32)
    col = (t % GRID_W).astype(np.float32)
    half = HEAD_DIM // 2
    inv_freq = (np.float32(ROPE_THETA) ** (-np.arange(0, half, 2, dtype=np.float32) / np.float32(half))).astype(np.float32)
    ar = (row[:, None] * inv_freq[None, :]).astype(np.float32).astype(np.float64)
    ac = (col[:, None] * inv_freq[None, :]).astype(np.float32).astype(np.float64)
    cr, sr, cc, sc = np.cos(ar), np.sin(ar), np.cos(ac), np.sin(ac)
    cos64 = np.concatenate([cr, cr, cc, cc], axis=1)
    sin64 = np.concatenate([-sr, sr, -sc, sc], axis=1)
    return jnp.asarray(np.tile(cos64, (1, 2)), F32), jnp.asarray(np.tile(sin64, (1, 2)), F32)


def _attn_inputs(x, mod, g_pre, w_attn, cos, sin, qg2, kg2):
    seq = x.shape[0]
    bs = 512

    def body(x_ref, mod_ref, g_ref, w_ref, cos_ref, sin_ref, qg_ref, kg_ref, h_ref, pa_ref, qt_ref, kd_ref, vd_ref):
        xv = x_ref[...]
        r = lax.rsqrt(jnp.mean(xv * xv, axis=-1, keepdims=True) + EPS)
        shift = mod_ref[:, 0:D_MODEL]
        scale = mod_ref[:, D_MODEL:2 * D_MODEL]
        hb = ((xv * r) * g_ref[...] * (1.0 + scale) + shift).astype(h_ref.dtype)
        h_ref[...] = hb
        p = _nt(hb, w_ref[...])
        pa_ref[...] = p
        m = _blockdiag64()
        cs, sn = cos_ref[...], sin_ref[...]
        lo = _lane((bs, 128)) < 64
        for pr in range(4):
            q = p[:, QA + 128 * pr:QA + 128 * (pr + 1)]
            r = lax.rsqrt(_seg64(q * q, m) * (1.0 / 64) + EPS)
            qt_ref[:, 128 * pr:128 * (pr + 1)] = (_rope(q * r * qg_ref[...], cs, sn) * (0.125 * LOG2E)).astype(qt_ref.dtype)
        k = p[:, KA:KA + 128]
        r = lax.rsqrt(_seg64(k * k, m) * (1.0 / 64) + EPS)
        kr = _rope(k * r * kg_ref[...], cs, sn)
        ksw = pltpu.roll(kr, 64, 1)
        kd_ref[0] = jnp.where(lo, kr, ksw).astype(kd_ref.dtype)
        kd_ref[1] = jnp.where(lo, ksw, kr).astype(kd_ref.dtype)
        v = p[:, VA:VA + 128]
        vsw = pltpu.roll(v, 64, 1)
        vd_ref[0] = jnp.where(lo, v, vsw).astype(vd_ref.dtype)
        vd_ref[1] = jnp.where(lo, vsw, v).astype(vd_ref.dtype)

    row = lambda w: pl.BlockSpec((bs, w), lambda i: (i, 0))
    fix = lambda a, b: pl.BlockSpec((a, b), lambda i: (0, 0))
    dup = pl.BlockSpec((2, bs, 128), lambda i: (0, i, 0))
    sds = jax.ShapeDtypeStruct
    return pl.pallas_call(
        body, grid=(seq // bs,), name="attn_inputs",
        in_specs=[row(D_MODEL), fix(1, 3 * D_MODEL), fix(1, D_MODEL), fix(ATTN_COLS, D_MODEL), row(128), row(128), fix(1, 128), fix(1, 128)],
        out_specs=[row(D_MODEL), row(ATTN_COLS), row(512), dup, dup],
        out_shape=[sds((seq, D_MODEL), MXU_DTYPE), sds((seq, ATTN_COLS), F32), sds((seq, 512), MXU_DTYPE),
                   sds((2, seq, 128), MXU_DTYPE), sds((2, seq, 128), MXU_DTYPE)],
        compiler_params=_cp("parallel"))(x, mod, g_pre, w_attn, cos, sin, qg2, kg2)


def _in_proj_dx(x, dp, win_t, dout, mod, g_pre, dep=None):
    seq = x.shape[0]
    bs = 512
    deps, dep_specs = _dep_args(dep, 1)

    def body(x_ref, dp_ref, w_ref, dout_ref, mod_ref, g_ref, *rest):
        gx_ref, dshift_ref, dscale_ref, dg_ref = rest[-4:]

        @pl.when(pl.program_id(0) == 0)
        def _():
            dshift_ref[...] = jnp.zeros_like(dshift_ref)
            dscale_ref[...] = jnp.zeros_like(dscale_ref)
            dg_ref[...] = jnp.zeros_like(dg_ref)

        xv = x_ref[...]
        dh = _nn(dp_ref[...], w_ref[...])
        g = g_ref[...]
        r = lax.rsqrt(jnp.mean(xv * xv, axis=-1, keepdims=True) + EPS)
        xn = xv * r
        scale = mod_ref[:, D_MODEL:2 * D_MODEL]
        dshift_ref[...] += jnp.sum(dh, axis=0, keepdims=True)
        dscale_ref[...] += jnp.sum(dh * (xn * g), axis=0, keepdims=True)
        da = dh * (1.0 + scale)
        dg_ref[...] += jnp.sum(da * xn, axis=0, keepdims=True)
        dxn = da * g
        dx = r * (dxn - xn * jnp.mean(dxn * xn, axis=-1, keepdims=True))
        gx_ref[...] = dout_ref[...] + dx

    row = pl.BlockSpec((bs, D_MODEL), lambda i: (i, 0))
    vec = pl.BlockSpec((1, D_MODEL), lambda i: (0, 0))
    return pl.pallas_call(
        body, grid=(seq // bs,), name="in_proj_dx",
        in_specs=[row, pl.BlockSpec((bs, IN_WIDTH), lambda i: (i, 0)), pl.BlockSpec((IN_WIDTH, D_MODEL), lambda i: (0, 0)), row,
                  pl.BlockSpec((1, 3 * D_MODEL), lambda i: (0, 0)), vec] + dep_specs,
        out_specs=[row, vec, vec, vec],
        out_shape=[jax.ShapeDtypeStruct((seq, D_MODEL), F32)] + [jax.ShapeDtypeStruct((1, D_MODEL), F32)] * 3,
        compiler_params=_cp("arbitrary"))(x, dp, win_t, dout, mod, g_pre, *deps)


def _dep_args(dep, grid_rank):
    if dep is None:
        return [], []
    return [dep], [pl.BlockSpec(dep.shape, lambda *_: (0,) * dep.ndim)]


def _matmul(a, b, *, ta, tb, tm, tn, out_dtype, name, dep=None):
    kdim = a.shape[0] if ta else a.shape[1]
    m = a.shape[1] if ta else a.shape[0]
    n = b.shape[0] if tb else b.shape[1]
    assert m % tm == 0 and n % tn == 0
    deps, dep_specs = _dep_args(dep, 2)

    def body(a_ref, b_ref, *rest):
        o_ref = rest[-1]
        dims = (((0 if ta else 1,), (1 if tb else 0,)), ((), ()))
        o_ref[...] = lax.dot_general(a_ref[...], b_ref[...], dims, preferred_element_type=F32).astype(o_ref.dtype)

    a_spec = pl.BlockSpec((kdim, tm), lambda j, i: (0, i)) if ta else pl.BlockSpec((tm, kdim), lambda j, i: (i, 0))
    b_spec = pl.BlockSpec((tn, kdim), lambda j, i: (j, 0)) if tb else pl.BlockSpec((kdim, tn), lambda j, i: (0, j))
    return pl.pallas_call(
        body, grid=(n // tn, m // tm), name=name, in_specs=[a_spec, b_spec] + dep_specs,
        out_specs=pl.BlockSpec((tm, tn), lambda j, i: (i, j)),
        out_shape=jax.ShapeDtypeStruct((m, n), out_dtype), compiler_params=_cp("parallel", "parallel"))(a, b, *deps)


def _in_proj_rest(h, win_t, cos, sin):
    seq = h.shape[0]
    tm = min(1024, seq)
    ncols = IN_WIDTH - ATTN_COLS
    tn = ncols // 2
    assert ZR_B <= tn and ATTN_COLS % 128 == 0 and tn % 128 == 0

    def body(h_ref, w_ref, cos_ref, sin_ref, p_ref, rq_ref, rk_ref, rv_ref):
        p = _nt(h_ref[...], w_ref[...])
        p_ref[...] = p

        @pl.when(pl.program_id(1) == 0)
        def _():
            cs, sn = cos_ref[...], sin_ref[...]
            for pr in range(2):
                sl = slice(128 * pr, 128 * (pr + 1))
                rq_ref[:, sl] = _rope(p[:, QR_B + 128 * pr:QR_B + 128 * (pr + 1)], cs, sn).astype(rq_ref.dtype)
                rk_ref[:, sl] = (_rope(p[:, KR_B + 128 * pr:KR_B + 128 * (pr + 1)], cs, sn) * 0.125).astype(rk_ref.dtype)
            rv_ref[...] = p[:, VR_B:VR_B + 512].astype(rv_ref.dtype)

    row = lambda w: pl.BlockSpec((tm, w), lambda i, j: (i, 0))
    sds = jax.ShapeDtypeStruct
    return pl.pallas_call(
        body, grid=(seq // tm, 2), name="in_proj_rest",
        in_specs=[row(D_MODEL), pl.BlockSpec((pl.Element(tn), pl.Element(D_MODEL)),
                                             lambda i, j: (pl.multiple_of(ATTN_COLS + j * tn, 128), 0)), row(128), row(128)],
        out_specs=[pl.BlockSpec((tm, tn), lambda i, j: (i, j)), row(256), row(256), row(512)],
        out_shape=[sds((seq, ncols), F32), sds((seq, 256), MXU_DTYPE), sds((seq, 256), MXU_DTYPE), sds((seq, 512), MXU_DTYPE)],
        compiler_params=_cp("parallel", "arbitrary"))(h, win_t, cos, sin)


def _halves(kd):
    lo = _lane(kd.shape) < 64
    z = jnp.zeros_like(kd)
    return jnp.concatenate([jnp.where(lo, kd, z), jnp.where(lo, z, kd)], axis=0)


def _attn_fwd(qt, kd, vd, dep=None):
    seq = qt.shape[0]
    bq, bk = min(1024, seq), min(1024, seq)
    nk = seq // bk
    deps, dep_specs = _dep_args(dep, 2)

    def body(q_ref, k_ref, v_ref, *rest):
        o_ref, lse_ref, m_scr, l_scr, acc = rest[-5:]
        j = pl.program_id(1)
        m_scr[...] = jnp.full_like(m_scr, -jnp.inf)
        l_scr[...] = jnp.zeros_like(l_scr)
        acc[...] = jnp.zeros_like(acc)
        lo = _lane((bq, 128)) < 64

        def kv_block(n, carry):
            rows = pl.ds(pl.multiple_of(n * bk, bk), bk)
            k2, v2 = _halves(k_ref[rows, :]), _halves(v_ref[rows, :])
            for pr in range(2):
                s2 = _nt(q_ref[:, 128 * pr:128 * (pr + 1)], k2)
                ps, alphas = [], []
                for e in range(2):
                    g = 2 * pr + e
                    s = s2[:, e * bk:(e + 1) * bk]
                    m_prev = m_scr[g]
                    m_next = jnp.maximum(m_prev, jnp.max(s, axis=1, keepdims=True))
                    alpha = jnp.exp2(m_prev - m_next)
                    pe = jnp.exp2(s - jnp.tile(m_next, (1, bk // 128)))
                    l_scr[g] = alpha * l_scr[g] + jnp.sum(pe, axis=1, keepdims=True)
                    m_scr[g] = m_next
                    ps.append(_mx(pe))
                    alphas.append(alpha)
                o2 = _nn(jnp.concatenate(ps, axis=1), v2)
                sl = slice(128 * pr, 128 * (pr + 1))
                acc[:, sl] = acc[:, sl] * jnp.where(lo, alphas[0], alphas[1]) + o2
            return carry

        lax.fori_loop(0, nk, kv_block, 0)
        for pr in range(2):
            sl = slice(128 * pr, 128 * (pr + 1))
            o_ref[:, sl] = acc[:, sl] / jnp.where(lo, l_scr[2 * pr], l_scr[2 * pr + 1])
        for g in range(4):
            lse = jnp.transpose(m_scr[g] + jnp.log2(l_scr[g]))
            lse_ref[pl.ds(4 * j + g, 1), :] = lse[0:1, :]

    return pl.pallas_call(
        body, grid=(seq // bq, 2), name="attn_fwd",
        in_specs=[pl.BlockSpec((bq, 256), lambda i, j: (i, j)), pl.BlockSpec((None, seq, 128), lambda i, j: (j, 0, 0)),
                  pl.BlockSpec((None, seq, 128), lambda i, j: (j, 0, 0))] + dep_specs,
        out_specs=[pl.BlockSpec((bq, 256), lambda i, j: (i, j)), pl.BlockSpec((8, bq), lambda i, j: (0, i))],
        out_shape=[jax.ShapeDtypeStruct((seq, 512), F32), jax.ShapeDtypeStruct((8, seq), F32)],
        scratch_shapes=[pltpu.VMEM((4, bq, 128), F32), pltpu.VMEM((4, bq, 128), F32), pltpu.VMEM((bq, 256), F32)],
        compiler_params=_cp("parallel", "arbitrary"))(qt, kd, vd, *deps)


def _attn_bwd(qt, kd, vd, do, lse, delta, dep=None):
    seq = qt.shape[0]
    bq, bk = min(1024, seq), min(1024, seq)
    nq, nk = seq // bq, seq // bk
    deps, dep_specs = _dep_args(dep, 3)

    def body(q_ref, do_ref, k_ref, v_ref, lse_ref, del_ref, *rest):
        dq_ref, dk_ref, dv_ref = rest[-3:]
        j, i = pl.program_id(0), pl.program_id(1)
        dq_ref[...] = jnp.zeros_like(dq_ref)

        @pl.when(i == 0)
        def _():
            dk_ref[...] = jnp.zeros_like(dk_ref)
            dv_ref[...] = jnp.zeros_like(dv_ref)

        lo = _lane((bk, 128)) < 64
        big = lambda r: jnp.concatenate([jnp.broadcast_to(r[0], (bk, bq)), jnp.broadcast_to(r[1], (bk, bq))], axis=0)

        def kv_block(n, carry):
            rows = pl.ds(pl.multiple_of(n * bk, bk), bk)
            k2, v2 = _halves(k_ref[rows, :]), _halves(v_ref[rows, :])
            for pr in range(2):
                sl = slice(128 * pr, 128 * (pr + 1))
                qp, dop = q_ref[:, sl], do_ref[:, sl]
                s_t = _nt(k2, qp)
                dp_t = _nt(v2, dop)
                ls = [lse_ref[pl.ds(4 * j + 2 * pr + e, 1), :] for e in range(2)]
                dl = [del_ref[pl.ds(4 * j + 2 * pr + e, 1), :] for e in range(2)]
                p_t = jnp.exp2(s_t - big(ls))
                ds_t = _mx(p_t * (dp_t - big(dl)))
                rv = _nn(_mx(p_t), dop)
                rk = _nn(ds_t, qp) * LN2
                dv_ref[rows, :] += jnp.where(lo, rv[:bk], rv[bk:])
                dk_ref[rows, :] += jnp.where(lo, rk[:bk], rk[bk:])
                dq_ref[:, sl] += _tn(ds_t, k2)
            return carry

        lax.fori_loop(0, nk, kv_block, 0)

    return pl.pallas_call(
        body, grid=(2, nq), name="attn_bwd",
        in_specs=[pl.BlockSpec((bq, 256), lambda j, i: (i, j)), pl.BlockSpec((bq, 256), lambda j, i: (i, j)),
                  pl.BlockSpec((None, seq, 128), lambda j, i: (j, 0, 0)), pl.BlockSpec((None, seq, 128), lambda j, i: (j, 0, 0)),
                  pl.BlockSpec((8, bq), lambda j, i: (0, i)), pl.BlockSpec((8, bq), lambda j, i: (0, i))] + dep_specs,
        out_specs=[pl.BlockSpec((bq, 256), lambda j, i: (i, j)), pl.BlockSpec((None, seq, 128), lambda j, i: (j, 0, 0)),
                   pl.BlockSpec((None, seq, 128), lambda j, i: (j, 0, 0))],
        out_shape=[jax.ShapeDtypeStruct((seq, 512), F32), jax.ShapeDtypeStruct((2, seq, 128), F32),
                   jax.ShapeDtypeStruct((2, seq, 128), F32)],
        compiler_params=_cp("arbitrary", "arbitrary"))(qt, do, kd, vd, lse, delta, *deps)


def _ret_tables(lg_f, lg_b, c):
    idx = jnp.arange(c, dtype=F32)
    diff = idx[:, None] - idx[None, :]
    a, b = lg_f[:, None, None], lg_b[:, None, None]
    low, up = (diff >= 0)[None], (diff < 0)[None]
    dmat = jnp.where(low, jnp.exp(a * jnp.maximum(diff, 0.0)[None]), jnp.exp(b * jnp.maximum(-diff, 0.0)[None]))
    dda = jnp.where(low, diff[None] * jnp.exp(a * jnp.maximum(diff, 0.0)[None]), 0.0)
    ddb = jnp.where(up, -diff[None] * jnp.exp(b * jnp.maximum(-diff, 0.0)[None]), 0.0)
    rep = lambda w: jnp.broadcast_to(w[:, :, None], (RET_HEADS, c, 128))
    wqf = rep(jnp.exp(lg_f[:, None] * (idx + 1.0)[None]))
    wqb = rep(jnp.exp(lg_b[:, None] * (c - idx)[None]))
    wkf = rep(jnp.exp(lg_f[:, None] * (c - 1.0 - idx)[None]))
    wkb = rep(jnp.exp(lg_b[:, None] * idx[None]))
    cdf, cdb = jnp.exp(lg_f * c), jnp.exp(lg_b * c)
    dec_fb = jnp.broadcast_to(jnp.concatenate([cdf, cdb])[:, None], (8, 128))
    dec_bf = jnp.broadcast_to(jnp.concatenate([cdb, cdf])[:, None], (8, 128))
    return dict(dmat=dmat, dda=dda, ddb=ddb, wqf=wqf, wqb=wqb, wkf=wkf, wkb=wkb, dec_fb=dec_fb, dec_bf=dec_bf)


def _ret_states(xk, yv, w_fwd, w_rev, dec, name):
    seq = xk.shape[0]
    c = RET_CHUNK
    nc = seq // c

    def body(xf_ref, yf_ref, xr_ref, yr_ref, wf_ref, wr_ref, dec_ref, sf_ref, sr_ref, st_f, st_r):
        @pl.when(pl.program_id(0) == 0)
        def _():
            st_f[...] = jnp.zeros_like(st_f)
            st_r[...] = jnp.zeros_like(st_r)

        lane = _lane((c, 128))
        for h in range(RET_HEADS):
            pr, e = h // 2, h % 2
            half = (lane < 64) if e == 0 else (lane >= 64)
            for x_ref, y_ref, w_ref, s_ref, st, drow in ((xf_ref, yf_ref, wf_ref, sf_ref, st_f, h),
                                                        (xr_ref, yr_ref, wr_ref, sr_ref, st_r, 4 + h)):
                s_ref[h] = st[h].astype(s_ref.dtype)
                xm = jnp.where(half, x_ref[:, 128 * pr:128 * (pr + 1)].astype(F32) * w_ref[h], 0.0)
                st[h] = st[h] * dec_ref[drow:drow + 1, :] + _tn(_mx(xm), _mx(y_ref[:, 128 * h:128 * (h + 1)]))

    xs = lambda f: pl.BlockSpec((c, 256), f)
    ys = lambda f: pl.BlockSpec((c, 512), f)
    fwd, rev = (lambda n: (n, 0)), (lambda n: (nc - 1 - n, 0))
    wsp = pl.BlockSpec((RET_HEADS, c, 128), lambda n: (0, 0, 0))
    return pl.pallas_call(
        body, grid=(nc,), name=name,
        in_specs=[xs(fwd), ys(fwd), xs(rev), ys(rev), wsp, wsp, pl.BlockSpec((8, 128), lambda n: (0, 0))],
        out_specs=[pl.BlockSpec((None, RET_HEADS, 128, 128), lambda n: (n, 0, 0, 0)),
                   pl.BlockSpec((None, RET_HEADS, 128, 128), lambda n: (nc - 1 - n, 0, 0, 0))],
        out_shape=[jax.ShapeDtypeStruct((nc, RET_HEADS, 128, 128), MXU_DTYPE)] * 2,
        scratch_shapes=[pltpu.VMEM((RET_HEADS, 128, 128), F32), pltpu.VMEM((RET_HEADS, 128, 128), F32)],
        compiler_params=_cp("arbitrary"))(xk, yv, xk, yv, w_fwd, w_rev, dec)


def _ret_fwd(rq, rk, rv, rf, rb, tb):
    seq = rq.shape[0]
    c = RET_CHUNK

    def body(q_ref, k_ref, v_ref, rf_ref, rb_ref, d_ref, wqf_ref, wqb_ref, o_ref):
        lane = _lane((c, 128))
        for h in range(RET_HEADS):
            pr, e = h // 2, h % 2
            half = (lane < 64) if e == 0 else (lane >= 64)
            sl = slice(128 * pr, 128 * (pr + 1))
            qp = q_ref[:, sl]
            km = jnp.where(half, k_ref[:, sl], jnp.zeros_like(k_ref[:, sl]))
            sd = _mx(_nt(qp, km) * d_ref[h])
            qf = qp.astype(F32)
            o = _nn(sd, v_ref[:, 128 * h:128 * (h + 1)])
            o = o + _nn(_mx(qf * wqf_ref[h]), rf_ref[h]) + _nn(_mx(qf * wqb_ref[h]), rb_ref[h])
            o_ref[:, 128 * h:128 * (h + 1)] = o

    st = pl.BlockSpec((None, RET_HEADS, 128, 128), lambda n: (n, 0, 0, 0))
    wsp = pl.BlockSpec((RET_HEADS, c, 128), lambda n: (0, 0, 0))
    return pl.pallas_call(
        body, grid=(seq // c,), name="ret_fwd",
        in_specs=[pl.BlockSpec((c, 256), lambda n: (n, 0)), pl.BlockSpec((c, 256), lambda n: (n, 0)),
                  pl.BlockSpec((c, 512), lambda n: (n, 0)), st, st, pl.BlockSpec((RET_HEADS, c, c), lambda n: (0, 0, 0)), wsp, wsp],
        out_specs=pl.BlockSpec((c, 512), lambda n: (n, 0)),
        out_shape=jax.ShapeDtypeStruct((seq, 512), F32), compiler_params=_cp("parallel"))(
            rq, rk, rv, rf, rb, tb["dmat"], tb["wqf"], tb["wqb"])


def _ret_bwd(rq, rk, rv, do, rf, rb, hf, hb, tb):
    seq = rq.shape[0]
    c = RET_CHUNK

    def body(q_ref, k_ref, v_ref, do_ref, rf_ref, rb_ref, hf_ref, hb_ref, d_ref, dda_ref, ddb_ref,
             wqf_ref, wqb_ref, wkf_ref, wkb_ref, cdec_ref, dq_ref, dk_ref, dv_ref, dlg_ref):
        @pl.when(pl.program_id(0) == 0)
        def _():
            dlg_ref[...] = jnp.zeros_like(dlg_ref)

        lane = _lane((c, 128))
        pos = lax.broadcasted_iota(jnp.int32, (c, 128), 0).astype(F32)
        for pr in range(2):
            sl = slice(128 * pr, 128 * (pr + 1))
            qp, kp = q_ref[:, sl], k_ref[:, sl]
            qf, kf = qp.astype(F32), kp.astype(F32)
            dq_acc = jnp.zeros((c, 128), F32)
            dk_acc = jnp.zeros((c, 128), F32)
            for e in range(2):
                h = 2 * pr + e
                half = (lane < 64) if e == 0 else (lane >= 64)
                hs = slice(128 * h, 128 * (h + 1))
                km = jnp.where(half, kp, jnp.zeros_like(kp))
                kmf = km.astype(F32)
                vh, doh = v_ref[:, hs], do_ref[:, hs]
                rfh, rbh, hfh, hbh = rf_ref[h], rb_ref[h], hf_ref[h], hb_ref[h]
                s = _nt(qp, km)
                dpm = _nt(doh, vh)
                ds_b = _mx(dpm * d_ref[h])
                sd_b = _mx(s * d_ref[h])
                dq_if = wqf_ref[h] * _nt(doh, rfh)
                dq_ib = wqb_ref[h] * _nt(doh, rbh)
                dq_acc = dq_acc + _nn(ds_b, km) + dq_if + dq_ib
                dk_sf = wkf_ref[h] * _nt(vh, hfh)
                dk_sb = wkb_ref[h] * _nt(vh, hbh)
                dk_acc = dk_acc + jnp.where(half, _tn(ds_b, qp), 0.0) + dk_sf + dk_sb
                dv = _tn(sd_b, doh) + _nn(_mx(kmf * wkf_ref[h]), hfh) + _nn(_mx(kmf * wkb_ref[h]), hbh)
                dv_ref[:, hs] = dv.astype(dv_ref.dtype)
                sdp = s * dpm
                hr_f = hfh.astype(F32) * rfh.astype(F32)
                hr_b = hbh.astype(F32) * rbh.astype(F32)
                da = (_rowsum128(sdp * dda_ref[h]) + _rowsum128((pos + 1.0) * qf * dq_if)
                      + _rowsum128((c - 1.0 - pos) * kf * dk_sf) + cdec_ref[h:h + 1, :] * _rowsum128(hr_f))
                db = (_rowsum128(sdp * ddb_ref[h]) + _rowsum128((c - pos) * qf * dq_ib)
                      + _rowsum128(pos * kf * dk_sb) + cdec_ref[4 + h:5 + h, :] * _rowsum128(hr_b))
                dlg_ref[h:h + 1, :] += da
                dlg_ref[4 + h:5 + h, :] += db
            dq_ref[:, sl] = dq_acc
            dk_ref[:, sl] = dk_acc

    st = pl.BlockSpec((None, RET_HEADS, 128, 128), lambda n: (n, 0, 0, 0))
    wsp = pl.BlockSpec((RET_HEADS, c, 128), lambda n: (0, 0, 0))
    dsp = pl.BlockSpec((RET_HEADS, c, c), lambda n: (0, 0, 0))
    x256 = pl.BlockSpec((c, 256), lambda n: (n, 0))
    x512 = pl.BlockSpec((c, 512), lambda n: (n, 0))
    cdec = tb["dec_fb"] * float(c)
    return pl.pallas_call(
        body, grid=(seq // c,), name="ret_bwd",
        in_specs=[x256, x256, x512, x512, st, st, st, st, dsp, dsp, dsp, wsp, wsp, wsp, wsp,
                  pl.BlockSpec((8, 128), lambda n: (0, 0))],
        out_specs=[x256, x256, x512, pl.BlockSpec((8, 128), lambda n: (0, 0))],
        out_shape=[jax.ShapeDtypeStruct((seq, 256), F32), jax.ShapeDtypeStruct((seq, 256), F32),
                   jax.ShapeDtypeStruct((seq, 512), MXU_DTYPE), jax.ShapeDtypeStruct((8, 128), F32)],
        compiler_params=_cp("arbitrary"))(
            rq, rk, rv, do, rf, rb, hf, hb, tb["dmat"], tb["dda"], tb["ddb"], tb["wqf"], tb["wqb"], tb["wkf"], tb["wkb"], cdec)


def _tail(x, tgt, o_att, o_ret, p, mod, g_post, gn_g, wpt, wout):
    seq = x.shape[0]
    bs = 256
    nb = seq // bs

    def body(x_ref, t_ref, oa_ref, or_ref, za_ref, zr_ref, gl_ref, mod_ref, gp_ref, gn_ref, wpt_ref, wout_ref,
             dout_ref, dza_ref, dzr_ref, dgl_ref, doa_ref, dor_ref, del_ref, gout_ref, gpt_ref,
             dgate_ref, dgpost_ref, dgn_ref, loss_ref, gout_acc, gpt_acc):
        i = pl.program_id(0)

        @pl.when(i == 0)
        def _():
            gout_acc[...] = jnp.zeros_like(gout_acc)
            gpt_acc[...] = jnp.zeros_like(gpt_acc)
            dgate_ref[...] = jnp.zeros_like(dgate_ref)
            dgpost_ref[...] = jnp.zeros_like(dgpost_ref)
            dgn_ref[...] = jnp.zeros_like(dgn_ref)
            loss_ref[...] = jnp.zeros_like(loss_ref)

        oa, za, zr = oa_ref[...], za_ref[...], zr_ref[...]
        sga, sgr = _sigmoid(za), _sigmoid(zr)
        sza, szr = za * sga, zr * sgr
        ya_b = _mx(oa * sza)
        ons, rstds = [], []
        for h in range(RET_HEADS):
            oh = or_ref[:, 128 * h:128 * (h + 1)]
            xc = oh - jnp.mean(oh, axis=-1, keepdims=True)
            rstd = lax.rsqrt(jnp.mean(xc * xc, axis=-1, keepdims=True) + EPS)
            ons.append(xc * rstd)
            rstds.append(rstd)
        on = jnp.concatenate(ons, axis=1)
        yn = on * gn_ref[...]
        yr_b = _mx(yn * szr)
        wpa_t, wpr_t = wpt_ref[:, 0:512], wpt_ref[:, 512:1024]
        a_att = _nt(ya_b, wpa_t)
        a_ret = _nt(yr_b, wpr_t)
        gts = _sigmoid(gl_ref[...])
        g_att, g_ret = gts[:, 0:D_MODEL], gts[:, D_MODEL:2 * D_MODEL]
        merged_b = _mx(g_att * a_att + g_ret * a_ret)
        u = _nn(merged_b, wout_ref[...])
        r = lax.rsqrt(jnp.mean(u * u, axis=-1, keepdims=True) + EPS)
        un = u * r
        gpost = gp_ref[...]
        y = un * gpost
        gate = mod_ref[:, 2 * D_MODEL:3 * D_MODEL]
        err = x_ref[...] + gate * y - t_ref[...]
        loss_ref[...] += (0.5 / D_MODEL) * _rowsum128(err * err)
        dout = err * (1.0 / D_MODEL)
        dout_ref[...] = dout
        dgate_ref[...] += jnp.sum(dout * y, axis=0, keepdims=True)
        dy = dout * gate
        dgpost_ref[...] += jnp.sum(dy * un, axis=0, keepdims=True)
        dun = dy * gpost
        du_b = _mx(r * (dun - un * jnp.mean(dun * un, axis=-1, keepdims=True)))
        dmerged = _nt(du_b, wout_ref[...])
        gout_acc[...] += _tn(merged_b, du_b)
        d_att, d_ret = dmerged * g_att, dmerged * g_ret
        dgl_ref[:, 0:D_MODEL] = (d_att * a_att * (1.0 - g_att)).astype(dgl_ref.dtype)
        dgl_ref[:, D_MODEL:2 * D_MODEL] = (d_ret * a_ret * (1.0 - g_ret)).astype(dgl_ref.dtype)
        d_att_b, d_ret_b = _mx(d_att), _mx(d_ret)
        dya = _nn(d_att_b, wpa_t)
        dyr = _nn(d_ret_b, wpr_t)
        gpt_acc[:, 0:512] += _tn(d_att_b, ya_b)
        gpt_acc[:, 512:1024] += _tn(d_ret_b, yr_b)
        dza_ref[...] = (dya * oa * (sga * (1.0 + za * (1.0 - sga)))).astype(dza_ref.dtype)
        doa = dya * sza
        doa_ref[...] = doa.astype(doa_ref.dtype)
        dt = jnp.transpose(doa * oa)
        del_ref[...] = jnp.sum(dt.reshape(8, HEAD_DIM, bs), axis=1)
        dzr_ref[...] = (dyr * yn * (sgr * (1.0 + zr * (1.0 - sgr)))).astype(dzr_ref.dtype)
        dyn = dyr * szr
        dgn_ref[...] += jnp.sum(dyn * on, axis=0, keepdims=True)
        don = dyn * gn_ref[...]
        for h in range(RET_HEADS):
            hs = slice(128 * h, 128 * (h + 1))
            dh, oh = don[:, hs], ons[h]
            doh = rstds[h] * (dh - jnp.mean(dh, axis=-1, keepdims=True) - oh * jnp.mean(dh * oh, axis=-1, keepdims=True))
            dor_ref[:, hs] = doh.astype(dor_ref.dtype)

        @pl.when(i == nb - 1)
        def _():
            gout_ref[...] = gout_acc[...].astype(gout_ref.dtype)
            gpt_ref[...] = gpt_acc[...].astype(gpt_ref.dtype)

    row = lambda w: pl.BlockSpec((bs, w), lambda i: (i, 0))
    el = lambda w, off: pl.BlockSpec((pl.Element(bs), pl.Element(w)), lambda i: (i * bs, off))
    vec = lambda w: pl.BlockSpec((1, w), lambda i: (0, 0))
    full = pl.BlockSpec((D_MODEL, D_MODEL), lambda i: (0, 0))
    sds = jax.ShapeDtypeStruct
    return pl.pallas_call(
        body, grid=(nb,), name="tail",
        in_specs=[row(D_MODEL), row(D_MODEL), row(512), row(512), el(512, ZA_B), el(512, ZR_B), el(2 * D_MODEL, GL_B),
                  vec(3 * D_MODEL), vec(D_MODEL), vec(512), full, full],
        out_specs=[row(D_MODEL), row(512), row(512), row(2 * D_MODEL), row(512), row(512),
                   pl.BlockSpec((8, bs), lambda i: (0, i)), full, full, vec(D_MODEL), vec(D_MODEL), vec(512), vec(128)],
        out_shape=[sds((seq, D_MODEL), F32), sds((seq, 512), MXU_DTYPE), sds((seq, 512), MXU_DTYPE),
                   sds((seq, 2 * D_MODEL), MXU_DTYPE), sds((seq, 512), MXU_DTYPE), sds((seq, 512), MXU_DTYPE),
                   sds((8, seq), F32), sds((D_MODEL, D_MODEL), MXU_DTYPE), sds((D_MODEL, D_MODEL), MXU_DTYPE),
                   sds((1, D_MODEL), F32), sds((1, D_MODEL), F32), sds((1, 512), F32), sds((1, 128), F32)],
        scratch_shapes=[pltpu.VMEM((D_MODEL, D_MODEL), F32), pltpu.VMEM((D_MODEL, D_MODEL), F32)],
        compiler_params=_cp("arbitrary"))(x, tgt, o_att, o_ret, p, p, p, mod, g_post, gn_g, wpt, wout)


def _assemble(p, cos, sin, qg2, kg2, dqt, dkp, dvp, dza, drq, drk, drv, dzr, dgl):
    seq = p.shape[0]
    bs = 512

    def body(p_ref, cos_ref, sin_ref, qg_ref, kg_ref, dqt_ref, dkp_ref, dvp_ref, dza_ref, drq_ref, drk_ref, drv_ref,
             dzr_ref, dgl_ref, dp_ref, dqg_ref, dkg_ref):
        @pl.when(pl.program_id(0) == 0)
        def _():
            dqg_ref[...] = jnp.zeros_like(dqg_ref)
            dkg_ref[...] = jnp.zeros_like(dkg_ref)

        m = _blockdiag64()
        cs, sn = cos_ref[...], sin_ref[...]
        lo = _lane((bs, 128)) < 64

        def norm_bwd(raw, dn, g, dg_ref):
            r = lax.rsqrt(_seg64(raw * raw, m) * (1.0 / 64) + EPS)
            xn = raw * r
            dg_ref[...] += jnp.sum(dn * xn, axis=0, keepdims=True)
            dxn = dn * g
            return r * (dxn - xn * (_seg64(dxn * xn, m) * (1.0 / 64)))

        for pr in range(4):
            sl = slice(128 * pr, 128 * (pr + 1))
            dn = _rope_t(dqt_ref[:, sl] * 0.125, cs, sn)
            dp_ref[:, QA + 128 * pr:QA + 128 * (pr + 1)] = norm_bwd(p_ref[:, sl], dn, qg_ref[...], dqg_ref).astype(dp_ref.dtype)
        fold = lambda a: a + pltpu.roll(a, 64, 1)
        dk = jnp.where(lo, fold(dkp_ref[0]), fold(dkp_ref[1]))
        dp_ref[:, KA:KA + 128] = norm_bwd(p_ref[:, KA:KA + 128], _rope_t(dk, cs, sn), kg_ref[...], dkg_ref).astype(dp_ref.dtype)
        dp_ref[:, VA:VA + 128] = jnp.where(lo, fold(dvp_ref[0]), fold(dvp_ref[1])).astype(dp_ref.dtype)
        dp_ref[:, ZA:ZA + 512] = dza_ref[...]
        for pr in range(2):
            sl = slice(128 * pr, 128 * (pr + 1))
            dp_ref[:, QR + 128 * pr:QR + 128 * (pr + 1)] = _rope_t(drq_ref[:, sl], cs, sn).astype(dp_ref.dtype)
            dp_ref[:, KR + 128 * pr:KR + 128 * (pr + 1)] = _rope_t(drk_ref[:, sl] * 0.125, cs, sn).astype(dp_ref.dtype)
        dp_ref[:, VR:VR + 512] = drv_ref[...]
        dp_ref[:, ZR:ZR + 512] = dzr_ref[...]
        dp_ref[:, GL:GL + 2 * D_MODEL] = dgl_ref[...]

    row = lambda w: pl.BlockSpec((bs, w), lambda i: (i, 0))
    gsp = pl.BlockSpec((1, 128), lambda i: (0, 0))
    dup = pl.BlockSpec((2, bs, 128), lambda i: (0, i, 0))
    return pl.pallas_call(
        body, grid=(seq // bs,), name="assemble_dp",
        in_specs=[row(768), row(128), row(128), gsp, gsp, row(512), dup, dup, row(512), row(256), row(256), row(512),
                  row(512), row(2 * D_MODEL)],
        out_specs=[row(IN_WIDTH), gsp, gsp],
        out_shape=[jax.ShapeDtypeStruct((seq, IN_WIDTH), MXU_DTYPE), jax.ShapeDtypeStruct((1, 128), F32),
                   jax.ShapeDtypeStruct((1, 128), F32)],
        compiler_params=_cp("arbitrary"))(p, cos, sin, qg2, kg2, dqt, dkp, dvp, dza, drq, drk, drv, dzr, dgl)


def _local_step(x, tgt, mod, g_pre, qn_g, kn_g, w_dec_f, w_dec_b, gn_g, g_post, w_attn, rest_start, rest_wait, send=None):
    send = send or (lambda name, arrays: None)
    seq = x.shape[0]
    cos, sin = _rope_tables(seq)
    qg2, kg2 = jnp.tile(qn_g, (1, 2)), jnp.tile(kn_g, (1, 2))
    lg_f = jax.nn.log_sigmoid(w_dec_f[0])
    lg_b = jax.nn.log_sigmoid(w_dec_b[0])
    tb = _ret_tables(lg_f, lg_b, RET_CHUNK)

    h, p_a, qt, kd, vd = _attn_inputs(x, mod, g_pre, w_attn, cos, sin, qg2, kg2)
    o_att, lse = _attn_fwd(qt, kd, vd, dep=rest_start(qt))
    win_t, wpt, wout = rest_wait(o_att)
    p_b, rq, rk, rv = _in_proj_rest(h, win_t, cos, sin)
    rf, rb = _ret_states(rk, rv, tb["wkf"], tb["wkb"], tb["dec_fb"], "ret_states_fwd")
    o_ret = _ret_fwd(rq, rk, rv, rf, rb, tb)
    (dout, dza, dzr, dgl, do_att, do_ret, delta, g_out, g_pt, dgate, dgpost, dgn, loss_l) = _tail(
        x, tgt, o_att, o_ret, p_b, mod, g_post, gn_g, wpt, wout)
    dep = send("early", (g_pt, g_out))
    dqt, dkp, dvp = _attn_bwd(qt, kd, vd, do_att, lse, delta, dep=dep)
    hb, hf = _ret_states(rq, do_ret, tb["wqb"], tb["wqf"], tb["dec_bf"], "ret_states_bwd")
    drq, drk, drv, dlg = _ret_bwd(rq, rk, rv, do_ret, rf, rb, hf, hb, tb)
    dp, dqg, dkg = _assemble(p_a, cos, sin, qg2, kg2, dqt, dkp, dvp, dza, drq, drk, drv, dzr, dgl)
    g_in_t = _matmul(dp, h, ta=True, tb=False, tm=256, tn=D_MODEL, out_dtype=MXU_DTYPE, name="in_proj_dw")
    dep = send("late", (g_in_t,))
    grad_x, dshift, dscale, dgpre = _in_proj_dx(x, dp, win_t, dout, mod, g_pre, dep=dep)

    dlg = jnp.sum(dlg, axis=1)
    small = dict(
        dmod=jnp.concatenate([dshift, dscale, dgate], axis=1),
        g_pre=dgpre, g_post=dgpost, gn_g=dgn,
        qn_g=dqg[:, :64] + dqg[:, 64:], kn_g=dkg[:, :64] + dkg[:, 64:],
        w_dec_f=(dlg[0:4] * jax.nn.sigmoid(-w_dec_f[0]))[None], w_dec_b=(dlg[4:8] * jax.nn.sigmoid(-w_dec_b[0]))[None],
        loss=jnp.sum(loss_l, axis=1, keepdims=True))
    return grad_x, g_in_t, g_pt, g_out, small


N_DEV = 8
N_CHIP = 4
HBM_SPEC = pl.BlockSpec(memory_space=pl.ANY)
VMEM_SPEC = pl.BlockSpec(memory_space=pltpu.VMEM)
SHARD_ROWS = (IN_WIDTH // N_CHIP, D_MODEL // N_CHIP, D_MODEL // N_CHIP)


def _coords():
    return lax.axis_index("x"), lax.axis_index("y"), lax.axis_index("c")


def _peer(k):
    x, y, c = _coords()
    return (1 - x if k & 4 else x, 1 - y if k & 2 else y, 1 - c if k & 1 else c)


def _dev_index(p):
    return 4 * p[0] + 2 * p[1] + p[2]


def _rows(ref, start, size):
    return ref.at[pl.ds(pl.multiple_of(start, 16), size), :]


def _remote(src, dst, ssem, rsem, dev):
    return pltpu.make_async_remote_copy(src_ref=src, dst_ref=dst, send_sem=ssem, recv_sem=rsem, device_id=dev,
                                        device_id_type=MESH)


ATTN_CHUNK = 96


def _mod_and_attn_rows(c, w_ada_s, b4, win_s):
    ncol = w_ada_s.shape[1]
    half = ATTN_COLS // 2
    offs = list(range(0, half, ATTN_CHUNK))
    nq = len(offs)
    rows = lambda ref, cc, off: ref.at[pl.ds(pl.multiple_of(cc * half + off, 16), ATTN_CHUNK), :]

    def body(c_ref, w_ref, b_ref, src, cs_ref, mod_ref, out, modsh, modall, ssem, rsem, lsem, asem, bsem, fsem, gsem, hsem):
        x, y, cc = _coords()
        me = _dev_index((x, y, cc))
        chip = me // 2
        sib = (x, y, 1 - cc)
        cs_ref[me] = c_ref[...]
        sends = []
        for k in range(1, N_DEV):
            cp = _remote(c_ref, cs_ref.at[me], ssem.at[k - 1], rsem.at[k - 1], _peer(k))
            cp.start()
            sends.append(cp)
        own = pltpu.make_async_copy(src.at[pl.ds(0, ATTN_COLS), :], out, lsem.at[0])
        bulk = [_remote(rows(src, cc, off), rows(out, cc, off), asem.at[(k2 - 1) * nq + q], bsem.at[q], (k2 // 2, k2 % 2, cc))
                for k2 in (1, 2) for q, off in enumerate(offs)]
        relay_chip = lambda q: 1 + q % 2

        @pl.when(chip == 0)
        def _():
            own.start()
            for cp in bulk:
                cp.start()

        for k in range(1, N_DEV):
            slot = cs_ref.at[_dev_index(_peer(k))]
            _remote(slot, slot, ssem.at[k - 1], rsem.at[k - 1], _peer(k)).wait_recv()
        cs = jnp.concatenate([cs_ref[d] for d in range(N_DEV)], axis=0)
        ms = _nn(cs * _sigmoid(cs), w_ref[...], precision=HIGHEST) + b_ref[pl.ds(chip, 1), :]
        modsh[...] = ms
        modall[chip] = ms
        for k2 in range(1, N_CHIP):
            cp = _remote(modsh, modall.at[chip], ssem.at[6 + k2], rsem.at[6 + k2], _peer(2 * k2))
            cp.start()
            sends.append(cp)

        @pl.when(chip != 0)
        def _():
            passed = []
            relays = [_remote(rows(out, cc, off), rows(out, cc, off), hsem.at[q], bsem.at[q], (1, 1, cc)) for q, off in enumerate(offs)]
            for q, off in enumerate(offs):
                got = rows(out, cc, off)
                _remote(got, got, asem.at[q], bsem.at[q], (0, 0, cc)).wait_recv()
                cp = _remote(got, got, fsem.at[q], gsem.at[q], sib)
                cp.start()
                passed.append(cp)
                pl.when(chip == relay_chip(q))(relays[q].start)
            for q, off in enumerate(offs):
                got = rows(out, 1 - cc, off)
                _remote(got, got, fsem.at[q], gsem.at[q], sib).wait_recv()
            for cp in passed:
                cp.wait_send()
            for q in range(nq):
                pl.when(chip == relay_chip(q))(relays[q].wait_send)

        for k2 in range(1, N_CHIP):
            slot = modall.at[_dev_index(_peer(2 * k2)) // 2]
            _remote(slot, slot, ssem.at[6 + k2], rsem.at[6 + k2], _peer(2 * k2)).wait_recv()
        for jj in range(N_CHIP):
            mod_ref[:, ncol * jj:ncol * (jj + 1)] = modall[jj, pl.ds(me, 1), :]
        for cp in sends:
            cp.wait_send()

        @pl.when(chip == 0)
        def _():
            for cp in bulk:
                cp.wait_send()
            own.wait()

    dma = pltpu.SemaphoreType.DMA
    return pl.pallas_call(
        body, name="mod_and_attn_rows", in_specs=[VMEM_SPEC] * 4, out_specs=[VMEM_SPEC, VMEM_SPEC, HBM_SPEC],
        out_shape=[jax.ShapeDtypeStruct((N_DEV, 1, D_MODEL), F32), jax.ShapeDtypeStruct((1, N_CHIP * ncol), F32),
                   jax.ShapeDtypeStruct((ATTN_COLS, win_s.shape[1]), win_s.dtype)],
        scratch_shapes=[pltpu.VMEM((N_DEV, ncol), F32), pltpu.VMEM((N_CHIP, N_DEV, ncol), F32), dma((10,)), dma((10,)),
                        dma((1,)), dma((2 * nq,)), dma((nq,)), dma((nq,)), dma((nq,)), dma((nq,))],
        compiler_params=_cp())(c, w_ada_s, b4, win_s)


GATHER_CHUNK = 32


def _chunks(kinds, chunk):
    out = []
    for t, kind in enumerate(kinds):
        half = SHARD_ROWS[kind] // 2
        step = chunk if half % chunk == 0 else half
        out += [(t, off, step) for off in range(0, half, step)]
    return out


SEM_SPEC = pl.BlockSpec(memory_space=pltpu.SEMAPHORE)
HBM_ONLY = pl.BlockSpec(memory_space=pltpu.HBM)
DATAFLOW = pltpu.SideEffectType.DATAFLOW_SIDE_EFFECTING


def _place_own(shards):
    nt = len(shards)

    def body(*refs):
        srcs, outs, lsem = refs[:nt], refs[nt:2 * nt], refs[2 * nt]
        x, y, _ = _coords()
        chip = 2 * x + y
        local = [pltpu.make_async_copy(srcs[t], _rows(outs[t], chip * SHARD_ROWS[t], SHARD_ROWS[t]), lsem.at[t]) for t in range(nt)]
        for cp in local:
            cp.start()
        for cp in local:
            cp.wait()

    return pl.pallas_call(
        body, name="place_own_shard", in_specs=[VMEM_SPEC] * nt, out_specs=[HBM_SPEC] * nt,
        out_shape=[jax.ShapeDtypeStruct((N_CHIP * SHARD_ROWS[t], s.shape[1]), s.dtype) for t, s in enumerate(shards)],
        scratch_shapes=[pltpu.SemaphoreType.DMA((nt,))], compiler_params=_cp())(*shards)


def _gather_rest_start(shards, zones, dep):
    n = len(shards)

    def body(*refs):
        srcs, lands = refs[:n], refs[n:2 * n]
        ssem, rsem = refs[2 * n + 1], refs[2 * n + 2]
        token = refs[-1]
        x, y, _ = _coords()
        chip = 2 * x + y
        for k2 in range(1, N_CHIP):
            for t in range(n):
                q = (k2 - 1) * n + t
                _remote(srcs[t], _rows(lands[t], chip * SHARD_ROWS[t], SHARD_ROWS[t]), ssem.at[q], rsem.at[q], _peer(2 * k2)).start()
        token[...] = jnp.zeros_like(token)

    hbm = lambda a: pltpu.with_memory_space_constraint(a, pltpu.HBM)
    dma = pltpu.SemaphoreType.DMA
    outs = pl.pallas_call(
        body, name="gather_rest", in_specs=[HBM_ONLY] * (2 * n) + [HBM_SPEC],
        out_specs=[SEM_SPEC, SEM_SPEC] + [HBM_ONLY] * (2 * n) + [VMEM_SPEC],
        out_shape=[dma((3 * n,)), dma((3 * n,))] + [pltpu.HBM(a.shape, a.dtype) for a in list(shards) + list(zones)]
        + [jax.ShapeDtypeStruct((8, 128), F32)],
        input_output_aliases={i: 2 + i for i in range(2 * n)},
        compiler_params=pltpu.CompilerParams(has_side_effects=DATAFLOW))(*[hbm(a) for a in list(shards) + list(zones)], dep)
    return outs[0], outs[1], outs[2:2 + n], outs[2 + n:2 + 2 * n], outs[-1]


def _gather_rest_wait(started, after):
    ssem, rsem, shards, zones, _ = started
    n = len(shards)

    def body(*refs):
        srcs, lands = refs[:n], refs[n:2 * n]
        ssem_ref, rsem_ref = refs[2 * n], refs[2 * n + 1]
        for k2 in range(1, N_CHIP):
            pchip = _dev_index(_peer(2 * k2)) // 2
            for t in range(n):
                q = (k2 - 1) * n + t
                cp = _remote(srcs[t], _rows(lands[t], pchip * SHARD_ROWS[t], SHARD_ROWS[t]), ssem_ref.at[q], rsem_ref.at[q], _peer(2 * k2))
                cp.wait_send()
                cp.wait_recv()

    outs = pl.pallas_call(
        body, name="gather_rest_wait", in_specs=[HBM_ONLY] * (2 * n) + [SEM_SPEC, SEM_SPEC, HBM_SPEC], out_specs=[HBM_ONLY] * (2 * n),
        out_shape=[pltpu.HBM(a.shape, a.dtype) for a in list(shards) + list(zones)],
        input_output_aliases={i: i for i in range(2 * n)},
        compiler_params=pltpu.CompilerParams(has_side_effects=DATAFLOW))(*shards, *zones, ssem, rsem, after)
    return outs[n:]


def _reduced_by(ref, t, dev):
    return _rows(ref, (dev // 2) * SHARD_ROWS[t] + (dev % 2) * (SHARD_ROWS[t] // 2), SHARD_ROWS[t] // 2)


def _scatter_start(grads, kinds, name):
    n = len(grads)

    def body(*refs):
        srcs, lands = refs[:n], refs[n:2 * n]
        ssem, rsem = refs[2 * n], refs[2 * n + 1]
        token = refs[-1]
        me = _dev_index(_coords())
        for k in range(1, N_DEV):
            for i, t in enumerate(kinds):
                q = (k - 1) * n + i
                _remote(_reduced_by(srcs[i], t, _dev_index(_peer(k))), lands[i].at[me], ssem.at[q], rsem.at[q], _peer(k)).start()
        token[...] = jnp.zeros_like(token)

    zones = [lax.empty((N_DEV, SHARD_ROWS[t] // 2, g.shape[1]), g.dtype) for g, t in zip(grads, kinds)]
    hbm = lambda a: pltpu.with_memory_space_constraint(a, pltpu.HBM)
    dma = pltpu.SemaphoreType.DMA
    outs = pl.pallas_call(
        body, name=name, in_specs=[HBM_ONLY] * (2 * n), out_specs=[SEM_SPEC, SEM_SPEC] + [HBM_ONLY] * (2 * n) + [VMEM_SPEC],
        out_shape=[dma((7 * n,)), dma((7 * n,))] + [pltpu.HBM(a.shape, a.dtype) for a in list(grads) + zones]
        + [jax.ShapeDtypeStruct((8, 128), F32)],
        input_output_aliases={i: 2 + i for i in range(2 * n)},
        compiler_params=pltpu.CompilerParams(has_side_effects=DATAFLOW))(*[hbm(a) for a in list(grads) + zones])
    return outs[0], outs[1], outs[2:2 + n], outs[2 + n:2 + 2 * n], outs[-1]


def _scatter_wait(started, kinds, after, name):
    ssem, rsem, grads, zones, _ = started
    n = len(grads)

    def body(*refs):
        srcs, lands = refs[:n], refs[n:2 * n]
        ssem_ref, rsem_ref = refs[2 * n], refs[2 * n + 1]
        for k in range(1, N_DEV):
            peer = _dev_index(_peer(k))
            for i, t in enumerate(kinds):
                q = (k - 1) * n + i
                cp = _remote(_reduced_by(srcs[i], t, peer), lands[i].at[peer], ssem_ref.at[q], rsem_ref.at[q], _peer(k))
                cp.wait_send()
                cp.wait_recv()

    outs = pl.pallas_call(
        body, name=name, in_specs=[HBM_ONLY] * (2 * n) + [SEM_SPEC, SEM_SPEC, HBM_SPEC], out_specs=[HBM_ONLY] * (2 * n),
        out_shape=[pltpu.HBM(a.shape, a.dtype) for a in list(grads) + list(zones)],
        input_output_aliases={i: i for i in range(2 * n)},
        compiler_params=pltpu.CompilerParams(has_side_effects=DATAFLOW))(*grads, *zones, ssem, rsem, after)
    return outs[:n], outs[n:]


def _grad_finish(grads, zones, kinds, name):
    nt = len(grads)
    pieces = _chunks(kinds, GATHER_CHUNK)
    npc = len(pieces)
    shard = [SHARD_ROWS[k] for k in kinds]

    def body(*refs):
        srcs, lands, outs = refs[:nt], refs[nt:2 * nt], refs[2 * nt:3 * nt]
        slots, sums = refs[3 * nt:4 * nt], refs[4 * nt:5 * nt]
        lsem, osem, xsem, ysem = refs[5 * nt:]
        x, y, c = _coords()
        me = _dev_index((x, y, c))
        sib = (x, y, 1 - c)
        loads = [pltpu.make_async_copy(_reduced_by(srcs[t], kinds[t], me), slots[t].at[me], lsem.at[t]) for t in range(nt)]
        for k in range(1, N_DEV):
            peer = _dev_index(_peer(k))
            loads += [pltpu.make_async_copy(lands[t].at[peer], slots[t].at[peer], lsem.at[k * nt + t]) for t in range(nt)]
        for cp in loads:
            cp.start()
        for cp in loads:
            cp.wait()
        sends = []
        place = lambda t, cc, off, n: _rows(outs[t], cc * (shard[t] // 2) + off, n)
        stores = []
        for q, (t, off, n) in enumerate(pieces):
            r = pl.ds(off, n)
            acc = slots[t][0, r, :].astype(F32)
            for d in range(1, N_DEV):
                acc = acc + slots[t][d, r, :].astype(F32)
            sums[t][r, :] = acc
            keep = pltpu.make_async_copy(sums[t].at[r, :], place(t, c, off, n), osem.at[q])
            give = _remote(sums[t].at[r, :], place(t, c, off, n), xsem.at[q], ysem.at[q], sib)
            keep.start()
            give.start()
            stores.append(keep)
            sends.append(give)
        for q, (t, off, n) in enumerate(pieces):
            got = place(t, 1 - c, off, n)
            _remote(got, got, xsem.at[q], ysem.at[q], sib).wait_recv()
        for cp in sends:
            cp.wait_send()
        for cp in stores:
            cp.wait()

    dma = pltpu.SemaphoreType.DMA
    return pl.pallas_call(
        body, name=name, in_specs=[HBM_SPEC] * (2 * nt), out_specs=[HBM_SPEC] * nt,
        out_shape=[jax.ShapeDtypeStruct((shard[t], g.shape[1]), F32) for t, g in enumerate(grads)],
        scratch_shapes=([pltpu.VMEM((N_DEV, shard[t] // 2, g.shape[1]), g.dtype) for t, g in enumerate(grads)]
                        + [pltpu.VMEM((shard[t] // 2, g.shape[1]), F32) for t, g in enumerate(grads)]
                        + [dma((N_DEV * nt,)), dma((npc,)), dma((npc,)), dma((npc,))]),
        compiler_params=_cp())(*grads, *zones)


def _adam_math(w, g, m, v):
    m = ADAM_B1 * m + (1.0 - ADAM_B1) * g
    v = ADAM_B2 * v + (1.0 - ADAM_B2) * (g * g)
    m_hat = m / (1.0 - ADAM_B1 ** ADAM_STEP)
    v_hat = v / (1.0 - ADAM_B2 ** ADAM_STEP)
    return -ADAM_LR * (m_hat / (jnp.sqrt(v_hat) + ADAM_EPS) + ADAM_WD * w), m, v


def _adamw(w, g, m, v, rb, name):
    rows, cols = w.shape

    def body(w_ref, g_ref, m_ref, v_ref, d_ref, nm_ref, nv_ref):
        d_ref[...], nm_ref[...], nv_ref[...] = _adam_math(w_ref[...], g_ref[...], m_ref[...], v_ref[...])

    spec = pl.BlockSpec((rb, cols), lambda i: (i, 0))
    return pl.pallas_call(body, grid=(rows // rb,), name=name, in_specs=[spec] * 4, out_specs=[spec] * 3,
                          out_shape=[jax.ShapeDtypeStruct(w.shape, F32)] * 3, compiler_params=_cp("parallel"))(w, g, m, v)


PACK = (("b_ada", 3 * D_MODEL), ("g_pre", D_MODEL), ("g_post", D_MODEL), ("gn_g", 512), ("qn_g", 64), ("kn_g", 64),
        ("w_dec_f", 4), ("w_dec_b", 4), ("loss", 1))
PACK_OFFSETS = {}
PACK_WIDTH = 0
for _name, _n in PACK:
    PACK_OFFSETS[_name] = PACK_WIDTH
    PACK_WIDTH += -(-_n // 128) * 128
SMALL_PARAMS = tuple(name for name, _ in PACK if name != "loss")


def _pack(parts):
    cols = []
    for name, n in PACK:
        pad = -(-n // 128) * 128 - n
        cols.append(parts[name].reshape(1, n).astype(F32))
        if pad:
            cols.append(jnp.zeros((1, pad), F32))
    return jnp.concatenate(cols, axis=1)


def _small_reduce(vec, cs, deps):
    ncol = 3 * D_MODEL // N_CHIP

    def body(vec_ref, cs_ref, *rest):
        tot_ref, gwa_ref, gat, ssem, rsem = rest[-5:]
        me = _dev_index(_coords())
        chip = me // 2
        gat[me] = vec_ref[...]
        sends = []
        for k in range(1, N_DEV):
            cp = _remote(vec_ref, gat.at[me], ssem.at[k - 1], rsem.at[k - 1], _peer(k))
            cp.start()
            sends.append(cp)
        for k in range(1, N_DEV):
            slot = gat.at[_dev_index(_peer(k))]
            _remote(slot, slot, ssem.at[k - 1], rsem.at[k - 1], _peer(k)).wait_recv()
        rows = [gat[d] for d in range(N_DEV)]
        tot = rows[0]
        for d in range(1, N_DEV):
            tot = tot + rows[d]
        tot_ref[...] = tot
        cs = cs_ref[...]
        ca_t = jnp.transpose(jnp.concatenate([cs * _sigmoid(cs), jnp.zeros((128 - N_DEV, D_MODEL), F32)], axis=0))
        dm = jnp.concatenate(rows + [jnp.zeros((128 - N_DEV, PACK_WIDTH), F32)], axis=0)
        for jj in range(N_CHIP):
            @pl.when(chip == jj)
            def _():
                gwa_ref[...] = _nn(ca_t, dm[:, ncol * jj:ncol * (jj + 1)], precision=HIGHEST)
        for cp in sends:
            cp.wait_send()

    return pl.pallas_call(
        body, name="small_reduce", in_specs=[VMEM_SPEC, VMEM_SPEC] + [HBM_SPEC] * len(deps), out_specs=[VMEM_SPEC] * 2,
        out_shape=[jax.ShapeDtypeStruct((1, PACK_WIDTH), F32), jax.ShapeDtypeStruct((D_MODEL, ncol), F32)],
        scratch_shapes=[pltpu.VMEM((N_DEV, 1, PACK_WIDTH), F32), pltpu.SemaphoreType.DMA((7,)), pltpu.SemaphoreType.DMA((7,))],
        compiler_params=_cp())(vec, cs, *deps)


def _small_adamw(tot, given):
    sizes = dict(PACK)
    np_ = len(SMALL_PARAMS)

    def body(*refs):
        tot_ref = refs[0]
        wmv = refs[1:1 + 3 * np_]
        outs = refs[1 + 3 * np_:1 + 7 * np_]
        loss_ref = refs[-1]
        for i, name in enumerate(SMALL_PARAMS):
            off, n = PACK_OFFSETS[name], sizes[name]
            g = tot_ref[:, off:off + n]
            w_ref, m_ref, v_ref = wmv[3 * i:3 * i + 3]
            outs[i][...] = g
            outs[np_ + i][...], outs[2 * np_ + i][...], outs[3 * np_ + i][...] = _adam_math(w_ref[...], g, m_ref[...], v_ref[...])
        loss_ref[...] = tot_ref[:, PACK_OFFSETS["loss"]:PACK_OFFSETS["loss"] + 1]

    flat = [a for name in SMALL_PARAMS for a in given[name]]
    shapes = [jax.ShapeDtypeStruct((1, sizes[name]), F32) for name in SMALL_PARAMS]
    res = pl.pallas_call(body, name="small_adamw", in_specs=[VMEM_SPEC] * (1 + 3 * np_), out_specs=[VMEM_SPEC] * (4 * np_ + 1),
                         out_shape=shapes * 4 + [jax.ShapeDtypeStruct((1, 1), F32)], compiler_params=_cp())(tot, *flat)
    return [dict(zip(SMALL_PARAMS, res[k * np_:(k + 1) * np_])) for k in range(4)], res[-1]


def kernel(x, c, w_ada, b_ada, g_pre, w_in, qn_g, kn_g, w_dec_f, w_dec_b, gn_g, w_pa, w_pr, w_out, g_post, loss_target, m_w_ada, m_b_ada, m_g_pre, m_w_in, m_qn_g, m_kn_g, m_w_dec_f, m_w_dec_b, m_gn_g, m_w_pa, m_w_pr, m_w_out, m_g_post, v_w_ada, v_b_ada, v_g_pre, v_w_in, v_qn_g, v_kn_g, v_w_dec_f, v_w_dec_b, v_gn_g, v_w_pa, v_w_pr, v_w_out, v_g_post):
    shards = (w_in[0].T.astype(MXU_DTYPE), jnp.concatenate([w_pa[0].T, w_pr[0].T], axis=1).astype(MXU_DTYPE),
              w_out[0].astype(MXU_DTYPE))
    cs, mod, w_attn = _mod_and_attn_rows(c, w_ada[0], b_ada.reshape(N_CHIP, 3 * D_MODEL // N_CHIP), shards[0])
    started = {}

    def rest_start(dep):
        started["rest"] = _gather_rest_start(shards, _place_own(shards), dep)
        return started["rest"][-1]

    def rest_wait(after):
        return _gather_rest_wait(started["rest"], after)

    kinds = {"early": (1, 2), "late": (0,)}

    def send(name, arrays):
        started[name] = _scatter_start(arrays, kinds[name], "grad_scatter_" + name)
        return started[name][-1]

    grad_x, _, _, _, small = _local_step(x[0], loss_target[0], mod, g_pre, qn_g, kn_g, w_dec_f, w_dec_b,
                                         gn_g, g_post, w_attn, rest_start, rest_wait, send=send)
    small = dict(small)
    small["b_ada"] = small.pop("dmod")
    given = dict(b_ada=(b_ada, m_b_ada, v_b_ada), g_pre=(g_pre, m_g_pre, v_g_pre), g_post=(g_post, m_g_post, v_g_post),
                 gn_g=(gn_g, m_gn_g, v_gn_g), qn_g=(qn_g, m_qn_g, v_qn_g), kn_g=(kn_g, m_kn_g, v_kn_g),
                 w_dec_f=(w_dec_f, m_w_dec_f, v_w_dec_f), w_dec_b=(w_dec_b, m_w_dec_b, v_w_dec_b))
    grads, deltas, new_m, new_v = {}, {}, {}, {}

    def update(name, w, g, m, v, back):
        d, nm, nv = _adamw(w, g, m, v, w.shape[0] // 4, "adamw_" + name)
        grads[name], deltas[name], new_m[name], new_v[name] = back(g), back(d), back(nm), back(nv)
        return d

    lead = lambda a: a[None]
    (g_pt, g_out), (z_pt, z_out) = _scatter_wait(started["early"], kinds["early"], grad_x, "grad_scatter_early_wait")
    r_pt, r_out = _grad_finish((g_pt, g_out), (z_pt, z_out), kinds["early"], "grad_finish_early")
    early = (update("w_pa", w_pa[0], r_pt[:, :512].T, m_w_pa[0], v_w_pa[0], lead),
             update("w_pr", w_pr[0], r_pt[:, 512:].T, m_w_pr[0], v_w_pr[0], lead),
             update("w_out", w_out[0], r_out, m_w_out[0], v_w_out[0], lead))
    tot, g_w_ada = _small_reduce(_pack(small), cs.reshape(N_DEV, D_MODEL), early)
    small_parts, loss = _small_adamw(tot, given)
    for dst, part in zip((grads, deltas, new_m, new_v), small_parts):
        dst.update(part)
    last = update("w_ada", w_ada[0], g_w_ada, m_w_ada[0], v_w_ada[0], lead)

    (g_in_t,), (z_in,) = _scatter_wait(started["late"], kinds["late"], last, "grad_scatter_late_wait")
    (r_in,) = _grad_finish((g_in_t,), (z_in,), kinds["late"], "grad_finish_late")
    tr = lambda a: a[0].T
    update("w_in", tr(w_in), r_in, tr(m_w_in), tr(v_w_in), lambda a: a.T[None])
    order = ("w_ada", "b_ada", "g_pre", "w_in", "qn_g", "kn_g", "w_dec_f", "w_dec_b", "gn_g", "w_pa", "w_pr", "w_out", "g_post")
    return (loss[0, 0], grad_x[None], *[grads[n] for n in order], *[deltas[n] for n in order],
            *[new_m[n] for n in order], *[new_v[n] for n in order])
```

```python
import functools

import jax
import jax.numpy as jnp
import numpy as np
from jax import lax
from jax.experimental import pallas as pl
from jax.experimental.pallas import tpu as pltpu

F32 = jnp.float32
BF16 = jnp.bfloat16
MXU_DTYPE = jnp.bfloat16

D_MODEL = 1024
GRID_W = 64
HEAD_DIM = 64
ROPE_THETA = 10000.0
EPS = 1e-6
RET_HEADS = 4
RET_CHUNK = 256
RET_GROUP = 4
IN_WIDTH = 4864
QA, KA, VA, ZA, QR, KR, VR, ZR, GL = 0, 512, 640, 768, 1280, 1536, 1792, 2304, 2816
ATTN_COLS = ZA
ZA_B, QR_B, KR_B, VR_B, ZR_B, GL_B = (c - ATTN_COLS for c in (ZA, QR, KR, VR, ZR, GL))

ADAM_LR, ADAM_B1, ADAM_B2, ADAM_EPS, ADAM_WD, ADAM_STEP = 0.001, 0.9, 0.999, 1e-08, 0.01, 10

VMEM_LIMIT = 56 * 1024 * 1024
MESH = pl.DeviceIdType.MESH
HIGHEST = lax.Precision.HIGHEST
LOG2E = 1.4426950408889634
LN2 = 0.6931471805599453


def _cp(*sem, **kw):
    if sem:
        kw["dimension_semantics"] = sem
    return pltpu.CompilerParams(vmem_limit_bytes=VMEM_LIMIT, **kw)


def _nn(a, b, **kw):
    return lax.dot_general(a, b, (((1,), (0,)), ((), ())), preferred_element_type=F32, **kw)


def _nt(a, b):
    return lax.dot_general(a, b, (((1,), (1,)), ((), ())), preferred_element_type=F32)


def _tn(a, b):
    return lax.dot_general(a, b, (((0,), (0,)), ((), ())), preferred_element_type=F32)


def _mx(x):
    return x.astype(MXU_DTYPE)


def _lane(shape):
    return lax.broadcasted_iota(jnp.int32, shape, 1)


def _sigmoid(z):
    return 1.0 / (1.0 + jnp.exp(-z))


def _rowsum128(x):
    s = jnp.sum(x, axis=0, keepdims=True)
    out = s[:, 0:128]
    for k in range(1, x.shape[1] // 128):
        out = out + s[:, 128 * k:128 * (k + 1)]
    return out


def _swap16(x):
    return jnp.where((_lane(x.shape) & 16) == 0, pltpu.roll(x, 112, 1), pltpu.roll(x, 16, 1))


def _rope(x, cos, sin):
    return x * cos + _swap16(x) * sin


def _rope_t(d, cos, sin):
    return d * cos + _swap16(d * sin)


def _blockdiag64():
    r = lax.broadcasted_iota(jnp.int32, (128, 128), 0) // 64
    c = lax.broadcasted_iota(jnp.int32, (128, 128), 1) // 64
    return (r == c).astype(BF16)


def _seg64(x, m):
    hi = x.astype(BF16)
    lo = (x - hi.astype(F32)).astype(BF16)
    return _nn(hi, m) + _nn(lo, m)


def _rope_tables(seq):
    t = np.arange(seq)
    row = (t // GRID_W).astype(np.float32)
    col = (t % GRID_W).astype(np.float32)
    half = HEAD_DIM // 2
    inv_freq = (np.float32(ROPE_THETA) ** (-np.arange(0, half, 2, dtype=np.float32) / np.float32(half))).astype(np.float32)
    ar = (row[:, None] * inv_freq[None, :]).astype(np.float32).astype(np.float64)
    ac = (col[:, None] * inv_freq[None, :]).astype(np.float32).astype(np.float64)
    cr, sr, cc, sc = np.cos(ar), np.sin(ar), np.cos(ac), np.sin(ac)
    cos64 = np.concatenate([cr, cr, cc, cc], axis=1)
    sin64 = np.concatenate([-sr, sr, -sc, sc], axis=1)
    return jnp.asarray(np.tile(cos64, (1, 2)), F32), jnp.asarray(np.tile(sin64, (1, 2)), F32)


def _attn_inputs(x, mod, g_pre, w_attn, cos, sin, qg2, kg2):
    seq = x.shape[0]
    bs = 512

    def body(x_ref, mod_ref, g_ref, w_ref, cos_ref, sin_ref, qg_ref, kg_ref, h_ref, pa_ref, qt_ref, kd_ref, vd_ref):
        xv = x_ref[...]
        r = lax.rsqrt(jnp.mean(xv * xv, axis=-1, keepdims=True) + EPS)
        shift = mod_ref[:, 0:D_MODEL]
        scale = mod_ref[:, D_MODEL:2 * D_MODEL]
        hb = ((xv * r) * g_ref[...] * (1.0 + scale) + shift).astype(h_ref.dtype)
        h_ref[...] = hb
        p = _nt(hb, w_ref[...])
        pa_ref[...] = p
        m = _blockdiag64()
        cs, sn = cos_ref[...], sin_ref[...]
        lo = _lane((bs, 128)) < 64
        for pr in range(4):
            q = p[:, QA + 128 * pr:QA + 128 * (pr + 1)]
            r = lax.rsqrt(_seg64(q * q, m) * (1.0 / 64) + EPS)
            qt_ref[:, 128 * pr:128 * (pr + 1)] = (_rope(q * r * qg_ref[...], cs, sn) * (0.125 * LOG2E)).astype(qt_ref.dtype)
        k = p[:, KA:KA + 128]
        r = lax.rsqrt(_seg64(k * k, m) * (1.0 / 64) + EPS)
        kr = _rope(k * r * kg_ref[...], cs, sn)
        ksw = pltpu.roll(kr, 64, 1)
        kd_ref[0] = jnp.where(lo, kr, ksw).astype(kd_ref.dtype)
        kd_ref[1] = jnp.where(lo, ksw, kr).astype(kd_ref.dtype)
        v = p[:, VA:VA + 128]
        vsw = pltpu.roll(v, 64, 1)
        vd_ref[0] = jnp.where(lo, v, vsw).astype(vd_ref.dtype)
        vd_ref[1] = jnp.where(lo, vsw, v).astype(vd_ref.dtype)

    row = lambda w: pl.BlockSpec((bs, w), lambda i: (i, 0))
    fix = lambda a, b: pl.BlockSpec((a, b), lambda i: (0, 0))
    dup = pl.BlockSpec((2, bs, 128), lambda i: (0, i, 0))
    sds = jax.ShapeDtypeStruct
    return pl.pallas_call(
        body, grid=(seq // bs,), name="attn_inputs",
        in_specs=[row(D_MODEL), fix(1, 3 * D_MODEL), fix(1, D_MODEL), fix(ATTN_COLS, D_MODEL), row(128), row(128), fix(1, 128), fix(1, 128)],
        out_specs=[row(D_MODEL), row(ATTN_COLS), row(512), dup, dup],
        out_shape=[sds((seq, D_MODEL), MXU_DTYPE), sds((seq, ATTN_COLS), F32), sds((seq, 512), MXU_DTYPE),
                   sds((2, seq, 128), MXU_DTYPE), sds((2, seq, 128), MXU_DTYPE)],
        compiler_params=_cp("parallel"))(x, mod, g_pre, w_attn, cos, sin, qg2, kg2)


def _in_proj_dx(x, dp, win_t, dout, mod, g_pre, dep=None):
    seq = x.shape[0]
    bs = 512
    deps, dep_specs = _dep_args(dep, 1)

    def body(x_ref, dp_ref, w_ref, dout_ref, mod_ref, g_ref, *rest):
        gx_ref, dshift_ref, dscale_ref, dg_ref = rest[-4:]

        @pl.when(pl.program_id(0) == 0)
        def _():
            dshift_ref[...] = jnp.zeros_like(dshift_ref)
            dscale_ref[...] = jnp.zeros_like(dscale_ref)
            dg_ref[...] = jnp.zeros_like(dg_ref)

        xv = x_ref[...]
        dh = _nn(dp_ref[...], w_ref[...])
        g = g_ref[...]
        r = lax.rsqrt(jnp.mean(xv * xv, axis=-1, keepdims=True) + EPS)
        xn = xv * r
        scale = mod_ref[:, D_MODEL:2 * D_MODEL]
        dshift_ref[...] += jnp.sum(dh, axis=0, keepdims=True)
        dscale_ref[...] += jnp.sum(dh * (xn * g), axis=0, keepdims=True)
        da = dh * (1.0 + scale)
        dg_ref[...] += jnp.sum(da * xn, axis=0, keepdims=True)
        dxn = da * g
        dx = r * (dxn - xn * jnp.mean(dxn * xn, axis=-1, keepdims=True))
        gx_ref[...] = dout_ref[...] + dx

    row = pl.BlockSpec((bs, D_MODEL), lambda i: (i, 0))
    vec = pl.BlockSpec((1, D_MODEL), lambda i: (0, 0))
    return pl.pallas_call(
        body, grid=(seq // bs,), name="in_proj_dx",
        in_specs=[row, pl.BlockSpec((bs, IN_WIDTH), lambda i: (i, 0)), pl.BlockSpec((IN_WIDTH, D_MODEL), lambda i: (0, 0)), row,
                  pl.BlockSpec((1, 3 * D_MODEL), lambda i: (0, 0)), vec] + dep_specs,
        out_specs=[row, vec, vec, vec],
        out_shape=[jax.ShapeDtypeStruct((seq, D_MODEL), F32)] + [jax.ShapeDtypeStruct((1, D_MODEL), F32)] * 3,
        compiler_params=_cp("arbitrary"))(x, dp, win_t, dout, mod, g_pre, *deps)


def _dep_args(dep, grid_rank):
    if dep is None:
        return [], []
    return [dep], [pl.BlockSpec(dep.shape, lambda *_: (0,) * dep.ndim)]


def _matmul(a, b, *, ta, tb, tm, tn, out_dtype, name, dep=None):
    kdim = a.shape[0] if ta else a.shape[1]
    m = a.shape[1] if ta else a.shape[0]
    n = b.shape[0] if tb else b.shape[1]
    assert m % tm == 0 and n % tn == 0
    deps, dep_specs = _dep_args(dep, 2)

    def body(a_ref, b_ref, *rest):
        o_ref = rest[-1]
        dims = (((0 if ta else 1,), (1 if tb else 0,)), ((), ()))
        o_ref[...] = lax.dot_general(a_ref[...], b_ref[...], dims, preferred_element_type=F32).astype(o_ref.dtype)

    a_spec = pl.BlockSpec((kdim, tm), lambda j, i: (0, i)) if ta else pl.BlockSpec((tm, kdim), lambda j, i: (i, 0))
    b_spec = pl.BlockSpec((tn, kdim), lambda j, i: (j, 0)) if tb else pl.BlockSpec((kdim, tn), lambda j, i: (0, j))
    return pl.pallas_call(
        body, grid=(n // tn, m // tm), name=name, in_specs=[a_spec, b_spec] + dep_specs,
        out_specs=pl.BlockSpec((tm, tn), lambda j, i: (i, j)),
        out_shape=jax.ShapeDtypeStruct((m, n), out_dtype), compiler_params=_cp("parallel", "parallel"))(a, b, *deps)


def _in_proj_rest(h, win_t, cos, sin):
    seq = h.shape[0]
    tm = min(1024, seq)
    ncols = IN_WIDTH - ATTN_COLS
    tn = ncols // 2
    assert ZR_B <= tn and ATTN_COLS % 128 == 0 and tn % 128 == 0

    def body(h_ref, w_ref, cos_ref, sin_ref, p_ref, rq_ref, rk_ref, rv_ref):
        p = _nt(h_ref[...], w_ref[...])
        p_ref[...] = p

        @pl.when(pl.program_id(1) == 0)
        def _():
            cs, sn = cos_ref[...], sin_ref[...]
            for pr in range(2):
                sl = slice(128 * pr, 128 * (pr + 1))
                rq_ref[:, sl] = _rope(p[:, QR_B + 128 * pr:QR_B + 128 * (pr + 1)], cs, sn).astype(rq_ref.dtype)
                rk_ref[:, sl] = (_rope(p[:, KR_B + 128 * pr:KR_B + 128 * (pr + 1)], cs, sn) * 0.125).astype(rk_ref.dtype)
            rv_ref[...] = p[:, VR_B:VR_B + 512].astype(rv_ref.dtype)

    row = lambda w: pl.BlockSpec((tm, w), lambda i, j: (i, 0))
    sds = jax.ShapeDtypeStruct
    return pl.pallas_call(
        body, grid=(seq // tm, 2), name="in_proj_rest",
        in_specs=[row(D_MODEL), pl.BlockSpec((pl.Element(tn), pl.Element(D_MODEL)),
                                             lambda i, j: (pl.multiple_of(ATTN_COLS + j * tn, 128), 0)), row(128), row(128)],
        out_specs=[pl.BlockSpec((tm, tn), lambda i, j: (i, j)), row(256), row(256), row(512)],
        out_shape=[sds((seq, ncols), F32), sds((seq, 256), MXU_DTYPE), sds((seq, 256), MXU_DTYPE), sds((seq, 512), MXU_DTYPE)],
        compiler_params=_cp("parallel", "arbitrary"))(h, win_t, cos, sin)


def _halves(kd):
    lo = _lane(kd.shape) < 64
    z = jnp.zeros_like(kd)
    return jnp.concatenate([jnp.where(lo, kd, z), jnp.where(lo, z, kd)], axis=0)


def _attn_fwd(qt, kd, vd, dep=None):
    seq = qt.shape[0]
    bq, bk = min(1024, seq), min(1024, seq)
    nk = seq // bk
    deps, dep_specs = _dep_args(dep, 2)

    def body(q_ref, k_ref, v_ref, *rest):
        o_ref, lse_ref, m_scr, l_scr, acc = rest[-5:]
        j = pl.program_id(1)
        m_scr[...] = jnp.full_like(m_scr, -jnp.inf)
        l_scr[...] = jnp.zeros_like(l_scr)
        acc[...] = jnp.zeros_like(acc)
        lo = _lane((bq, 128)) < 64

        def kv_block(n, carry):
            rows = pl.ds(pl.multiple_of(n * bk, bk), bk)
            k2, v2 = _halves(k_ref[rows, :]), _halves(v_ref[rows, :])
            for pr in range(2):
                s2 = _nt(q_ref[:, 128 * pr:128 * (pr + 1)], k2)
                ps, alphas = [], []
                for e in range(2):
                    g = 2 * pr + e
                    s = s2[:, e * bk:(e + 1) * bk]
                    m_prev = m_scr[g]
                    m_next = jnp.maximum(m_prev, jnp.max(s, axis=1, keepdims=True))
                    alpha = jnp.exp2(m_prev - m_next)
                    pe = jnp.exp2(s - jnp.tile(m_next, (1, bk // 128)))
                    l_scr[g] = alpha * l_scr[g] + jnp.sum(pe, axis=1, keepdims=True)
                    m_scr[g] = m_next
                    ps.append(_mx(pe))
                    alphas.append(alpha)
                o2 = _nn(jnp.concatenate(ps, axis=1), v2)
                sl = slice(128 * pr, 128 * (pr + 1))
                acc[:, sl] = acc[:, sl] * jnp.where(lo, alphas[0], alphas[1]) + o2
            return carry

        lax.fori_loop(0, nk, kv_block, 0)
        for pr in range(2):
            sl = slice(128 * pr, 128 * (pr + 1))
            o_ref[:, sl] = acc[:, sl] / jnp.where(lo, l_scr[2 * pr], l_scr[2 * pr + 1])
        for g in range(4):
            lse = jnp.transpose(m_scr[g] + jnp.log2(l_scr[g]))
            lse_ref[pl.ds(4 * j + g, 1), :] = lse[0:1, :]

    return pl.pallas_call(
        body, grid=(seq // bq, 2), name="attn_fwd",
        in_specs=[pl.BlockSpec((bq, 256), lambda i, j: (i, j)), pl.BlockSpec((None, seq, 128), lambda i, j: (j, 0, 0)),
                  pl.BlockSpec((None, seq, 128), lambda i, j: (j, 0, 0))] + dep_specs,
        out_specs=[pl.BlockSpec((bq, 256), lambda i, j: (i, j)), pl.BlockSpec((8, bq), lambda i, j: (0, i))],
        out_shape=[jax.ShapeDtypeStruct((seq, 512), F32), jax.ShapeDtypeStruct((8, seq), F32)],
        scratch_shapes=[pltpu.VMEM((4, bq, 128), F32), pltpu.VMEM((4, bq, 128), F32), pltpu.VMEM((bq, 256), F32)],
        compiler_params=_cp("parallel", "arbitrary"))(qt, kd, vd, *deps)


def _attn_bwd(qt, kd, vd, do, lse, delta, dep=None):
    seq = qt.shape[0]
    bq, bk = min(1024, seq), min(1024, seq)
    nq, nk = seq // bq, seq // bk
    deps, dep_specs = _dep_args(dep, 3)

    def body(q_ref, do_ref, k_ref, v_ref, lse_ref, del_ref, *rest):
        dq_ref, dk_ref, dv_ref = rest[-3:]
        j, i = pl.program_id(0), pl.program_id(1)
        dq_ref[...] = jnp.zeros_like(dq_ref)

        @pl.when(i == 0)
        def _():
            dk_ref[...] = jnp.zeros_like(dk_ref)
            dv_ref[...] = jnp.zeros_like(dv_ref)

        lo = _lane((bk, 128)) < 64
        big = lambda r: jnp.concatenate([jnp.broadcast_to(r[0], (bk, bq)), jnp.broadcast_to(r[1], (bk, bq))], axis=0)

        def kv_block(n, carry):
            rows = pl.ds(pl.multiple_of(n * bk, bk), bk)
            k2, v2 = _halves(k_ref[rows, :]), _halves(v_ref[rows, :])
            for pr in range(2):
                sl = slice(128 * pr, 128 * (pr + 1))
                qp, dop = q_ref[:, sl], do_ref[:, sl]
                s_t = _nt(k2, qp)
                dp_t = _nt(v2, dop)
                ls = [lse_ref[pl.ds(4 * j + 2 * pr + e, 1), :] for e in range(2)]
                dl = [del_ref[pl.ds(4 * j + 2 * pr + e, 1), :] for e in range(2)]
                p_t = jnp.exp2(s_t - big(ls))
                ds_t = _mx(p_t * (dp_t - big(dl)))
                rv = _nn(_mx(p_t), dop)
                rk = _nn(ds_t, qp) * LN2
                dv_ref[rows, :] += jnp.where(lo, rv[:bk], rv[bk:])
                dk_ref[rows, :] += jnp.where(lo, rk[:bk], rk[bk:])
                dq_ref[:, sl] += _tn(ds_t, k2)
            return carry

        lax.fori_loop(0, nk, kv_block, 0)

    return pl.pallas_call(
        body, grid=(2, nq), name="attn_bwd",
        in_specs=[pl.BlockSpec((bq, 256), lambda j, i: (i, j)), pl.BlockSpec((bq, 256), lambda j, i: (i, j)),
                  pl.BlockSpec((None, seq, 128), lambda j, i: (j, 0, 0)), pl.BlockSpec((None, seq, 128), lambda j, i: (j, 0, 0)),
                  pl.BlockSpec((8, bq), lambda j, i: (0, i)), pl.BlockSpec((8, bq), lambda j, i: (0, i))] + dep_specs,
        out_specs=[pl.BlockSpec((bq, 256), lambda j, i: (i, j)), pl.BlockSpec((None, seq, 128), lambda j, i: (j, 0, 0)),
                   pl.BlockSpec((None, seq, 128), lambda j, i: (j, 0, 0))],
        out_shape=[jax.ShapeDtypeStruct((seq, 512), F32), jax.ShapeDtypeStruct((2, seq, 128), F32),
                   jax.ShapeDtypeStruct((2, seq, 128), F32)],
        compiler_params=_cp("arbitrary", "arbitrary"))(qt, do, kd, vd, lse, delta, *deps)


def _ret_tables(lg_f, lg_b, c):
    idx = jnp.arange(c, dtype=F32)
    diff = idx[:, None] - idx[None, :]
    a, b = lg_f[:, None, None], lg_b[:, None, None]
    low, up = (diff >= 0)[None], (diff < 0)[None]
    dmat = jnp.where(low, jnp.exp(a * jnp.maximum(diff, 0.0)[None]), jnp.exp(b * jnp.maximum(-diff, 0.0)[None]))
    dda = jnp.where(low, diff[None] * jnp.exp(a * jnp.maximum(diff, 0.0)[None]), 0.0)
    ddb = jnp.where(up, -diff[None] * jnp.exp(b * jnp.maximum(-diff, 0.0)[None]), 0.0)
    rep = lambda w: jnp.broadcast_to(w[:, :, None], (RET_HEADS, c, 128))
    wqf = rep(jnp.exp(lg_f[:, None] * (idx + 1.0)[None]))
    wqb = rep(jnp.exp(lg_b[:, None] * (c - idx)[None]))
    wkf = rep(jnp.exp(lg_f[:, None] * (c - 1.0 - idx)[None]))
    wkb = rep(jnp.exp(lg_b[:, None] * idx[None]))
    cdf, cdb = jnp.exp(lg_f * c), jnp.exp(lg_b * c)
    dec_fb = jnp.broadcast_to(jnp.concatenate([cdf, cdb])[:, None], (8, 128))
    dec_bf = jnp.broadcast_to(jnp.concatenate([cdb, cdf])[:, None], (8, 128))
    return dict(dmat=dmat, dda=dda, ddb=ddb, wqf=wqf, wqb=wqb, wkf=wkf, wkb=wkb, dec_fb=dec_fb, dec_bf=dec_bf)


def _ret_states(xk, yv, w_fwd, w_rev, dec, name):
    seq = xk.shape[0]
    c = RET_CHUNK
    nc = seq // c
    grp = min(RET_GROUP, nc)
    ns = nc // grp

    def body(xf_ref, yf_ref, xr_ref, yr_ref, wf_ref, wr_ref, dec_ref, sf_ref, sr_ref, st_f, st_r):
        @pl.when(pl.program_id(0) == 0)
        def _():
            st_f[...] = jnp.zeros_like(st_f)
            st_r[...] = jnp.zeros_like(st_r)

        lane = _lane((c, 128))

        def chunk(g, carry):
            for x_ref, y_ref, w_ref, s_ref, st, base, gg in ((xf_ref, yf_ref, wf_ref, sf_ref, st_f, 0, g),
                                                             (xr_ref, yr_ref, wr_ref, sr_ref, st_r, 4, grp - 1 - g)):
                rows = pl.ds(pl.multiple_of(gg * c, c), c)
                for h in range(RET_HEADS):
                    pr, e = h // 2, h % 2
                    half = (lane < 64) if e == 0 else (lane >= 64)
                    s_ref[gg, h] = st[h].astype(s_ref.dtype)
                    xm = jnp.where(half, x_ref[rows, 128 * pr:128 * (pr + 1)].astype(F32) * w_ref[h], 0.0)
                    st[h] = st[h] * dec_ref[base + h:base + h + 1, :] + _tn(_mx(xm), _mx(y_ref[rows, 128 * h:128 * (h + 1)]))
            return carry

        lax.fori_loop(0, grp, chunk, 0)

    xs = lambda f: pl.BlockSpec((grp * c, 256), f)
    ys = lambda f: pl.BlockSpec((grp * c, 512), f)
    fwd, rev = (lambda n: (n, 0)), (lambda n: (ns - 1 - n, 0))
    wsp = pl.BlockSpec((RET_HEADS, c, 128), lambda n: (0, 0, 0))
    return pl.pallas_call(
        body, grid=(ns,), name=name,
        in_specs=[xs(fwd), ys(fwd), xs(rev), ys(rev), wsp, wsp, pl.BlockSpec((8, 128), lambda n: (0, 0))],
        out_specs=[pl.BlockSpec((grp, RET_HEADS, 128, 128), lambda n: (n, 0, 0, 0)),
                   pl.BlockSpec((grp, RET_HEADS, 128, 128), lambda n: (ns - 1 - n, 0, 0, 0))],
        out_shape=[jax.ShapeDtypeStruct((nc, RET_HEADS, 128, 128), MXU_DTYPE)] * 2,
        scratch_shapes=[pltpu.VMEM((RET_HEADS, 128, 128), F32), pltpu.VMEM((RET_HEADS, 128, 128), F32)],
        compiler_params=_cp("arbitrary"))(xk, yv, xk, yv, w_fwd, w_rev, dec)


def _ret_fwd(rq, rk, rv, rf, rb, tb):
    seq = rq.shape[0]
    c = RET_CHUNK
    grp = min(RET_GROUP, seq // c)

    def body(q_ref, k_ref, v_ref, rf_ref, rb_ref, d_ref, wqf_ref, wqb_ref, o_ref):
        lane = _lane((c, 128))

        def chunk(g, carry):
            rows = pl.ds(pl.multiple_of(g * c, c), c)
            for h in range(RET_HEADS):
                pr, e = h // 2, h % 2
                half = (lane < 64) if e == 0 else (lane >= 64)
                sl = slice(128 * pr, 128 * (pr + 1))
                qp, kp = q_ref[rows, sl], k_ref[rows, sl]
                km = jnp.where(half, kp, jnp.zeros_like(kp))
                sd = _mx(_nt(qp, km) * d_ref[h])
                qf = qp.astype(F32)
                o = _nn(sd, v_ref[rows, 128 * h:128 * (h + 1)])
                o = o + _nn(_mx(qf * wqf_ref[h]), rf_ref[g, h]) + _nn(_mx(qf * wqb_ref[h]), rb_ref[g, h])
                o_ref[rows, 128 * h:128 * (h + 1)] = o
            return carry

        lax.fori_loop(0, grp, chunk, 0)

    st = pl.BlockSpec((grp, RET_HEADS, 128, 128), lambda n: (n, 0, 0, 0))
    wsp = pl.BlockSpec((RET_HEADS, c, 128), lambda n: (0, 0, 0))
    return pl.pallas_call(
        body, grid=(seq // (grp * c),), name="ret_fwd",
        in_specs=[pl.BlockSpec((grp * c, 256), lambda n: (n, 0)), pl.BlockSpec((grp * c, 256), lambda n: (n, 0)),
                  pl.BlockSpec((grp * c, 512), lambda n: (n, 0)), st, st, pl.BlockSpec((RET_HEADS, c, c), lambda n: (0, 0, 0)), wsp, wsp],
        out_specs=pl.BlockSpec((grp * c, 512), lambda n: (n, 0)),
        out_shape=jax.ShapeDtypeStruct((seq, 512), F32), compiler_params=_cp("parallel"))(
            rq, rk, rv, rf, rb, tb["dmat"], tb["wqf"], tb["wqb"])


def _ret_bwd(rq, rk, rv, do, rf, rb, hf, hb, tb):
    seq = rq.shape[0]
    c = RET_CHUNK
    grp = min(RET_GROUP, seq // c)

    def body(q_ref, k_ref, v_ref, do_ref, rf_ref, rb_ref, hf_ref, hb_ref, d_ref, dda_ref, ddb_ref,
             wqf_ref, wqb_ref, wkf_ref, wkb_ref, cdec_ref, dq_ref, dk_ref, dv_ref, dlg_ref):
        @pl.when(pl.program_id(0) == 0)
        def _():
            dlg_ref[...] = jnp.zeros_like(dlg_ref)

        lane = _lane((c, 128))
        pos = lax.broadcasted_iota(jnp.int32, (c, 128), 0).astype(F32)

        def chunk(g, carry):
            rows = pl.ds(pl.multiple_of(g * c, c), c)
            for pr in range(2):
                sl = slice(128 * pr, 128 * (pr + 1))
                qp, kp = q_ref[rows, sl], k_ref[rows, sl]
                qf, kf = qp.astype(F32), kp.astype(F32)
                dq_acc = jnp.zeros((c, 128), F32)
                dk_acc = jnp.zeros((c, 128), F32)
                for e in range(2):
                    h = 2 * pr + e
                    half = (lane < 64) if e == 0 else (lane >= 64)
                    hs = slice(128 * h, 128 * (h + 1))
                    km = jnp.where(half, kp, jnp.zeros_like(kp))
                    kmf = km.astype(F32)
                    vh, doh = v_ref[rows, hs], do_ref[rows, hs]
                    rfh, rbh, hfh, hbh = rf_ref[g, h], rb_ref[g, h], hf_ref[g, h], hb_ref[g, h]
                    s = _nt(qp, km)
                    dpm = _nt(doh, vh)
                    ds_b = _mx(dpm * d_ref[h])
                    sd_b = _mx(s * d_ref[h])
                    dq_if = wqf_ref[h] * _nt(doh, rfh)
                    dq_ib = wqb_ref[h] * _nt(doh, rbh)
                    dq_acc = dq_acc + _nn(ds_b, km) + dq_if + dq_ib
                    dk_sf = wkf_ref[h] * _nt(vh, hfh)
                    dk_sb = wkb_ref[h] * _nt(vh, hbh)
                    dk_acc = dk_acc + jnp.where(half, _tn(ds_b, qp), 0.0) + dk_sf + dk_sb
                    dv = _tn(sd_b, doh) + _nn(_mx(kmf * wkf_ref[h]), hfh) + _nn(_mx(kmf * wkb_ref[h]), hbh)
                    dv_ref[rows, hs] = dv.astype(dv_ref.dtype)
                    sdp = s * dpm
                    hr_f = hfh.astype(F32) * rfh.astype(F32)
                    hr_b = hbh.astype(F32) * rbh.astype(F32)
                    da = (_rowsum128(sdp * dda_ref[h]) + _rowsum128((pos + 1.0) * qf * dq_if)
                          + _rowsum128((c - 1.0 - pos) * kf * dk_sf) + cdec_ref[h:h + 1, :] * _rowsum128(hr_f))
                    db = (_rowsum128(sdp * ddb_ref[h]) + _rowsum128((c - pos) * qf * dq_ib)
                          + _rowsum128(pos * kf * dk_sb) + cdec_ref[4 + h:5 + h, :] * _rowsum128(hr_b))
                    dlg_ref[h:h + 1, :] += da
                    dlg_ref[4 + h:5 + h, :] += db
                dq_ref[rows, sl] = dq_acc
                dk_ref[rows, sl] = dk_acc
            return carry

        lax.fori_loop(0, grp, chunk, 0)

    st = pl.BlockSpec((grp, RET_HEADS, 128, 128), lambda n: (n, 0, 0, 0))
    wsp = pl.BlockSpec((RET_HEADS, c, 128), lambda n: (0, 0, 0))
    dsp = pl.BlockSpec((RET_HEADS, c, c), lambda n: (0, 0, 0))
    x256 = pl.BlockSpec((grp * c, 256), lambda n: (n, 0))
    x512 = pl.BlockSpec((grp * c, 512), lambda n: (n, 0))
    cdec = tb["dec_fb"] * float(c)
    return pl.pallas_call(
        body, grid=(seq // (grp * c),), name="ret_bwd",
        in_specs=[x256, x256, x512, x512, st, st, st, st, dsp, dsp, dsp, wsp, wsp, wsp, wsp,
                  pl.BlockSpec((8, 128), lambda n: (0, 0))],
        out_specs=[x256, x256, x512, pl.BlockSpec((8, 128), lambda n: (0, 0))],
        out_shape=[jax.ShapeDtypeStruct((seq, 256), F32), jax.ShapeDtypeStruct((seq, 256), F32),
                   jax.ShapeDtypeStruct((seq, 512), MXU_DTYPE), jax.ShapeDtypeStruct((8, 128), F32)],
        compiler_params=_cp("arbitrary"))(
            rq, rk, rv, do, rf, rb, hf, hb, tb["dmat"], tb["dda"], tb["ddb"], tb["wqf"], tb["wqb"], tb["wkf"], tb["wkb"], cdec)


def _tail(x, tgt, o_att, o_ret, p, mod, g_post, gn_g, wpt, wout):
    seq = x.shape[0]
    bs = 256
    nb = seq // bs

    def body(x_ref, t_ref, oa_ref, or_ref, za_ref, zr_ref, gl_ref, mod_ref, gp_ref, gn_ref, wpt_ref, wout_ref,
             dout_ref, dza_ref, dzr_ref, dgl_ref, doa_ref, dor_ref, del_ref, gout_ref, gpt_ref,
             dgate_ref, dgpost_ref, dgn_ref, loss_ref, gout_acc, gpt_acc):
        i = pl.program_id(0)

        @pl.when(i == 0)
        def _():
            gout_acc[...] = jnp.zeros_like(gout_acc)
            gpt_acc[...] = jnp.zeros_like(gpt_acc)
            dgate_ref[...] = jnp.zeros_like(dgate_ref)
            dgpost_ref[...] = jnp.zeros_like(dgpost_ref)
            dgn_ref[...] = jnp.zeros_like(dgn_ref)
            loss_ref[...] = jnp.zeros_like(loss_ref)

        oa, za, zr = oa_ref[...], za_ref[...], zr_ref[...]
        sga, sgr = _sigmoid(za), _sigmoid(zr)
        sza, szr = za * sga, zr * sgr
        ya_b = _mx(oa * sza)
        ons, rstds = [], []
        for h in range(RET_HEADS):
            oh = or_ref[:, 128 * h:128 * (h + 1)]
            xc = oh - jnp.mean(oh, axis=-1, keepdims=True)
            rstd = lax.rsqrt(jnp.mean(xc * xc, axis=-1, keepdims=True) + EPS)
            ons.append(xc * rstd)
            rstds.append(rstd)
        on = jnp.concatenate(ons, axis=1)
        yn = on * gn_ref[...]
        yr_b = _mx(yn * szr)
        wpa_t, wpr_t = wpt_ref[:, 0:512], wpt_ref[:, 512:1024]
        a_att = _nt(ya_b, wpa_t)
        a_ret = _nt(yr_b, wpr_t)
        gts = _sigmoid(gl_ref[...])
        g_att, g_ret = gts[:, 0:D_MODEL], gts[:, D_MODEL:2 * D_MODEL]
        merged_b = _mx(g_att * a_att + g_ret * a_ret)
        u = _nn(merged_b, wout_ref[...])
        r = lax.rsqrt(jnp.mean(u * u, axis=-1, keepdims=True) + EPS)
        un = u * r
        gpost = gp_ref[...]
        y = un * gpost
        gate = mod_ref[:, 2 * D_MODEL:3 * D_MODEL]
        err = x_ref[...] + gate * y - t_ref[...]
        loss_ref[...] += (0.5 / D_MODEL) * _rowsum128(err * err)
        dout = err * (1.0 / D_MODEL)
        dout_ref[...] = dout
        dgate_ref[...] += jnp.sum(dout * y, axis=0, keepdims=True)
        dy = dout * gate
        dgpost_ref[...] += jnp.sum(dy * un, axis=0, keepdims=True)
        dun = dy * gpost
        du_b = _mx(r * (dun - un * jnp.mean(dun * un, axis=-1, keepdims=True)))
        dmerged = _nt(du_b, wout_ref[...])
        gout_acc[...] += _tn(merged_b, du_b)
        d_att, d_ret = dmerged * g_att, dmerged * g_ret
        dgl_ref[:, 0:D_MODEL] = (d_att * a_att * (1.0 - g_att)).astype(dgl_ref.dtype)
        dgl_ref[:, D_MODEL:2 * D_MODEL] = (d_ret * a_ret * (1.0 - g_ret)).astype(dgl_ref.dtype)
        d_att_b, d_ret_b = _mx(d_att), _mx(d_ret)
        dya = _nn(d_att_b, wpa_t)
        dyr = _nn(d_ret_b, wpr_t)
        gpt_acc[:, 0:512] += _tn(d_att_b, ya_b)
        gpt_acc[:, 512:1024] += _tn(d_ret_b, yr_b)
        dza_ref[...] = (dya * oa * (sga * (1.0 + za * (1.0 - sga)))).astype(dza_ref.dtype)
        doa = dya * sza
        doa_ref[...] = doa.astype(doa_ref.dtype)
        dt = jnp.transpose(doa * oa)
        del_ref[...] = jnp.sum(dt.reshape(8, HEAD_DIM, bs), axis=1)
        dzr_ref[...] = (dyr * yn * (sgr * (1.0 + zr * (1.0 - sgr)))).astype(dzr_ref.dtype)
        dyn = dyr * szr
        dgn_ref[...] += jnp.sum(dyn * on, axis=0, keepdims=True)
        don = dyn * gn_ref[...]
        for h in range(RET_HEADS):
            hs = slice(128 * h, 128 * (h + 1))
            dh, oh = don[:, hs], ons[h]
            doh = rstds[h] * (dh - jnp.mean(dh, axis=-1, keepdims=True) - oh * jnp.mean(dh * oh, axis=-1, keepdims=True))
            dor_ref[:, hs] = doh.astype(dor_ref.dtype)

        @pl.when(i == nb - 1)
        def _():
            gout_ref[...] = gout_acc[...].astype(gout_ref.dtype)
            gpt_ref[...] = gpt_acc[...].astype(gpt_ref.dtype)

    row = lambda w: pl.BlockSpec((bs, w), lambda i: (i, 0))
    el = lambda w, off: pl.BlockSpec((pl.Element(bs), pl.Element(w)), lambda i: (i * bs, off))
    vec = lambda w: pl.BlockSpec((1, w), lambda i: (0, 0))
    full = pl.BlockSpec((D_MODEL, D_MODEL), lambda i: (0, 0))
    sds = jax.ShapeDtypeStruct
    return pl.pallas_call(
        body, grid=(nb,), name="tail",
        in_specs=[row(D_MODEL), row(D_MODEL), row(512), row(512), el(512, ZA_B), el(512, ZR_B), el(2 * D_MODEL, GL_B),
                  vec(3 * D_MODEL), vec(D_MODEL), vec(512), full, full],
        out_specs=[row(D_MODEL), row(512), row(512), row(2 * D_MODEL), row(512), row(512),
                   pl.BlockSpec((8, bs), lambda i: (0, i)), full, full, vec(D_MODEL), vec(D_MODEL), vec(512), vec(128)],
        out_shape=[sds((seq, D_MODEL), F32), sds((seq, 512), MXU_DTYPE), sds((seq, 512), MXU_DTYPE),
                   sds((seq, 2 * D_MODEL), MXU_DTYPE), sds((seq, 512), MXU_DTYPE), sds((seq, 512), MXU_DTYPE),
                   sds((8, seq), F32), sds((D_MODEL, D_MODEL), MXU_DTYPE), sds((D_MODEL, D_MODEL), MXU_DTYPE),
                   sds((1, D_MODEL), F32), sds((1, D_MODEL), F32), sds((1, 512), F32), sds((1, 128), F32)],
        scratch_shapes=[pltpu.VMEM((D_MODEL, D_MODEL), F32), pltpu.VMEM((D_MODEL, D_MODEL), F32)],
        compiler_params=_cp("arbitrary"))(x, tgt, o_att, o_ret, p, p, p, mod, g_post, gn_g, wpt, wout)


def _assemble(p, cos, sin, qg2, kg2, dqt, dkp, dvp, dza, drq, drk, drv, dzr, dgl):
    seq = p.shape[0]
    bs = 512

    def body(p_ref, cos_ref, sin_ref, qg_ref, kg_ref, dqt_ref, dkp_ref, dvp_ref, dza_ref, drq_ref, drk_ref, drv_ref,
             dzr_ref, dgl_ref, dp_ref, dqg_ref, dkg_ref):
        @pl.when(pl.program_id(0) == 0)
        def _():
            dqg_ref[...] = jnp.zeros_like(dqg_ref)
            dkg_ref[...] = jnp.zeros_like(dkg_ref)

        m = _blockdiag64()
        cs, sn = cos_ref[...], sin_ref[...]
        lo = _lane((bs, 128)) < 64

        def norm_bwd(raw, dn, g, dg_ref):
            r = lax.rsqrt(_seg64(raw * raw, m) * (1.0 / 64) + EPS)
            xn = raw * r
            dg_ref[...] += jnp.sum(dn * xn, axis=0, keepdims=True)
            dxn = dn * g
            return r * (dxn - xn * (_seg64(dxn * xn, m) * (1.0 / 64)))

        for pr in range(4):
            sl = slice(128 * pr, 128 * (pr + 1))
            dn = _rope_t(dqt_ref[:, sl] * 0.125, cs, sn)
            dp_ref[:, QA + 128 * pr:QA + 128 * (pr + 1)] = norm_bwd(p_ref[:, sl], dn, qg_ref[...], dqg_ref).astype(dp_ref.dtype)
        fold = lambda a: a + pltpu.roll(a, 64, 1)
        dk = jnp.where(lo, fold(dkp_ref[0]), fold(dkp_ref[1]))
        dp_ref[:, KA:KA + 128] = norm_bwd(p_ref[:, KA:KA + 128], _rope_t(dk, cs, sn), kg_ref[...], dkg_ref).astype(dp_ref.dtype)
        dp_ref[:, VA:VA + 128] = jnp.where(lo, fold(dvp_ref[0]), fold(dvp_ref[1])).astype(dp_ref.dtype)
        dp_ref[:, ZA:ZA + 512] = dza_ref[...]
        for pr in range(2):
            sl = slice(128 * pr, 128 * (pr + 1))
            dp_ref[:, QR + 128 * pr:QR + 128 * (pr + 1)] = _rope_t(drq_ref[:, sl], cs, sn).astype(dp_ref.dtype)
            dp_ref[:, KR + 128 * pr:KR + 128 * (pr + 1)] = _rope_t(drk_ref[:, sl] * 0.125, cs, sn).astype(dp_ref.dtype)
        dp_ref[:, VR:VR + 512] = drv_ref[...]
        dp_ref[:, ZR:ZR + 512] = dzr_ref[...]
        dp_ref[:, GL:GL + 2 * D_MODEL] = dgl_ref[...]

    row = lambda w: pl.BlockSpec((bs, w), lambda i: (i, 0))
    gsp = pl.BlockSpec((1, 128), lambda i: (0, 0))
    dup = pl.BlockSpec((2, bs, 128), lambda i: (0, i, 0))
    return pl.pallas_call(
        body, grid=(seq // bs,), name="assemble_dp",
        in_specs=[row(768), row(128), row(128), gsp, gsp, row(512), dup, dup, row(512), row(256), row(256), row(512),
                  row(512), row(2 * D_MODEL)],
        out_specs=[row(IN_WIDTH), gsp, gsp],
        out_shape=[jax.ShapeDtypeStruct((seq, IN_WIDTH), MXU_DTYPE), jax.ShapeDtypeStruct((1, 128), F32),
                   jax.ShapeDtypeStruct((1, 128), F32)],
        compiler_params=_cp("arbitrary"))(p, cos, sin, qg2, kg2, dqt, dkp, dvp, dza, drq, drk, drv, dzr, dgl)


def _local_step(x, tgt, mod, g_pre, qn_g, kn_g, w_dec_f, w_dec_b, gn_g, g_post, w_attn, rest_start, rest_wait, send=None):
    send = send or (lambda name, arrays: None)
    seq = x.shape[0]
    cos, sin = _rope_tables(seq)
    qg2, kg2 = jnp.tile(qn_g, (1, 2)), jnp.tile(kn_g, (1, 2))
    lg_f = jax.nn.log_sigmoid(w_dec_f[0])
    lg_b = jax.nn.log_sigmoid(w_dec_b[0])
    tb = _ret_tables(lg_f, lg_b, RET_CHUNK)

    h, p_a, qt, kd, vd = _attn_inputs(x, mod, g_pre, w_attn, cos, sin, qg2, kg2)
    o_att, lse = _attn_fwd(qt, kd, vd, dep=rest_start(qt))
    win_t, wpt, wout = rest_wait(o_att)
    p_b, rq, rk, rv = _in_proj_rest(h, win_t, cos, sin)
    rf, rb = _ret_states(rk, rv, tb["wkf"], tb["wkb"], tb["dec_fb"], "ret_states_fwd")
    o_ret = _ret_fwd(rq, rk, rv, rf, rb, tb)
    (dout, dza, dzr, dgl, do_att, do_ret, delta, g_out, g_pt, dgate, dgpost, dgn, loss_l) = _tail(
        x, tgt, o_att, o_ret, p_b, mod, g_post, gn_g, wpt, wout)
    dep = send("early", (g_pt, g_out))
    dqt, dkp, dvp = _attn_bwd(qt, kd, vd, do_att, lse, delta, dep=dep)
    hb, hf = _ret_states(rq, do_ret, tb["wqb"], tb["wqf"], tb["dec_bf"], "ret_states_bwd")
    drq, drk, drv, dlg = _ret_bwd(rq, rk, rv, do_ret, rf, rb, hf, hb, tb)
    dp, dqg, dkg = _assemble(p_a, cos, sin, qg2, kg2, dqt, dkp, dvp, dza, drq, drk, drv, dzr, dgl)
    g_in_t = _matmul(dp, h, ta=True, tb=False, tm=256, tn=D_MODEL, out_dtype=MXU_DTYPE, name="in_proj_dw")
    dep = send("late", (g_in_t,))
    grad_x, dshift, dscale, dgpre = _in_proj_dx(x, dp, win_t, dout, mod, g_pre, dep=dep)

    dlg = jnp.sum(dlg, axis=1)
    small = dict(
        dmod=jnp.concatenate([dshift, dscale, dgate], axis=1),
        g_pre=dgpre, g_post=dgpost, gn_g=dgn,
        qn_g=dqg[:, :64] + dqg[:, 64:], kn_g=dkg[:, :64] + dkg[:, 64:],
        w_dec_f=(dlg[0:4] * jax.nn.sigmoid(-w_dec_f[0]))[None], w_dec_b=(dlg[4:8] * jax.nn.sigmoid(-w_dec_b[0]))[None],
        loss=jnp.sum(loss_l, axis=1, keepdims=True))
    return grad_x, g_in_t, g_pt, g_out, small


N_DEV = 8
N_CHIP = 4
HBM_SPEC = pl.BlockSpec(memory_space=pl.ANY)
VMEM_SPEC = pl.BlockSpec(memory_space=pltpu.VMEM)
SHARD_ROWS = (IN_WIDTH // N_CHIP, D_MODEL // N_CHIP, D_MODEL // N_CHIP)


def _coords():
    return lax.axis_index("x"), lax.axis_index("y"), lax.axis_index("c")


def _peer(k):
    x, y, c = _coords()
    return (1 - x if k & 4 else x, 1 - y if k & 2 else y, 1 - c if k & 1 else c)


def _dev_index(p):
    return 4 * p[0] + 2 * p[1] + p[2]


def _rows(ref, start, size):
    return ref.at[pl.ds(pl.multiple_of(start, 16), size), :]


def _remote(src, dst, ssem, rsem, dev):
    return pltpu.make_async_remote_copy(src_ref=src, dst_ref=dst, send_sem=ssem, recv_sem=rsem, device_id=dev,
                                        device_id_type=MESH)


ATTN_CHUNK = 96


def _mod_and_attn_rows(c, w_ada_s, b4, win_s):
    ncol = w_ada_s.shape[1]
    half = ATTN_COLS // 2
    offs = list(range(0, half, ATTN_CHUNK))
    nq = len(offs)
    rows = lambda ref, cc, off: ref.at[pl.ds(pl.multiple_of(cc * half + off, 16), ATTN_CHUNK), :]

    def body(c_ref, w_ref, b_ref, src, cs_ref, mod_ref, out, modsh, modall, ssem, rsem, lsem, asem, bsem, fsem, gsem, hsem):
        x, y, cc = _coords()
        me = _dev_index((x, y, cc))
        chip = me // 2
        sib = (x, y, 1 - cc)
        cs_ref[me] = c_ref[...]
        sends = []
        for k in range(1, N_DEV):
            cp = _remote(c_ref, cs_ref.at[me], ssem.at[k - 1], rsem.at[k - 1], _peer(k))
            cp.start()
            sends.append(cp)
        own = pltpu.make_async_copy(src.at[pl.ds(0, ATTN_COLS), :], out, lsem.at[0])
        bulk = [_remote(rows(src, cc, off), rows(out, cc, off), asem.at[(k2 - 1) * nq + q], bsem.at[q], (k2 // 2, k2 % 2, cc))
                for k2 in (1, 2) for q, off in enumerate(offs)]
        relay_chip = lambda q: 1 + q % 2

        @pl.when(chip == 0)
        def _():
            own.start()
            for cp in bulk:
                cp.start()

        for k in range(1, N_DEV):
            slot = cs_ref.at[_dev_index(_peer(k))]
            _remote(slot, slot, ssem.at[k - 1], rsem.at[k - 1], _peer(k)).wait_recv()
        cs = jnp.concatenate([cs_ref[d] for d in range(N_DEV)], axis=0)
        ms = _nn(cs * _sigmoid(cs), w_ref[...], precision=HIGHEST) + b_ref[pl.ds(chip, 1), :]
        modsh[...] = ms
        modall[chip] = ms
        for k2 in range(1, N_CHIP):
            cp = _remote(modsh, modall.at[chip], ssem.at[6 + k2], rsem.at[6 + k2], _peer(2 * k2))
            cp.start()
            sends.append(cp)

        @pl.when(chip != 0)
        def _():
            passed = []
            relays = [_remote(rows(out, cc, off), rows(out, cc, off), hsem.at[q], bsem.at[q], (1, 1, cc)) for q, off in enumerate(offs)]
            for q, off in enumerate(offs):
                got = rows(out, cc, off)
                _remote(got, got, asem.at[q], bsem.at[q], (0, 0, cc)).wait_recv()
                cp = _remote(got, got, fsem.at[q], gsem.at[q], sib)
                cp.start()
                passed.append(cp)
                pl.when(chip == relay_chip(q))(relays[q].start)
            for q, off in enumerate(offs):
                got = rows(out, 1 - cc, off)
                _remote(got, got, fsem.at[q], gsem.at[q], sib).wait_recv()
            for cp in passed:
                cp.wait_send()
            for q in range(nq):
                pl.when(chip == relay_chip(q))(relays[q].wait_send)

        for k2 in range(1, N_CHIP):
            slot = modall.at[_dev_index(_peer(2 * k2)) // 2]
            _remote(slot, slot, ssem.at[6 + k2], rsem.at[6 + k2], _peer(2 * k2)).wait_recv()
        for jj in range(N_CHIP):
            mod_ref[:, ncol * jj:ncol * (jj + 1)] = modall[jj, pl.ds(me, 1), :]
        for cp in sends:
            cp.wait_send()

        @pl.when(chip == 0)
        def _():
            for cp in bulk:
                cp.wait_send()
            own.wait()

    dma = pltpu.SemaphoreType.DMA
    return pl.pallas_call(
        body, name="mod_and_attn_rows", in_specs=[VMEM_SPEC] * 4, out_specs=[VMEM_SPEC, VMEM_SPEC, HBM_SPEC],
        out_shape=[jax.ShapeDtypeStruct((N_DEV, 1, D_MODEL), F32), jax.ShapeDtypeStruct((1, N_CHIP * ncol), F32),
                   jax.ShapeDtypeStruct((ATTN_COLS, win_s.shape[1]), win_s.dtype)],
        scratch_shapes=[pltpu.VMEM((N_DEV, ncol), F32), pltpu.VMEM((N_CHIP, N_DEV, ncol), F32), dma((10,)), dma((10,)),
                        dma((1,)), dma((2 * nq,)), dma((nq,)), dma((nq,)), dma((nq,)), dma((nq,))],
        compiler_params=_cp())(c, w_ada_s, b4, win_s)


GATHER_CHUNK = 32


def _chunks(kinds, chunk):
    out = []
    for t, kind in enumerate(kinds):
        half = SHARD_ROWS[kind] // 2
        step = chunk if half % chunk == 0 else half
        out += [(t, off, step) for off in range(0, half, step)]
    return out


SEM_SPEC = pl.BlockSpec(memory_space=pltpu.SEMAPHORE)
HBM_ONLY = pl.BlockSpec(memory_space=pltpu.HBM)
DATAFLOW = pltpu.SideEffectType.DATAFLOW_SIDE_EFFECTING


def _place_own(shards):
    nt = len(shards)

    def body(*refs):
        srcs, outs, lsem = refs[:nt], refs[nt:2 * nt], refs[2 * nt]
        x, y, _ = _coords()
        chip = 2 * x + y
        local = [pltpu.make_async_copy(srcs[t], _rows(outs[t], chip * SHARD_ROWS[t], SHARD_ROWS[t]), lsem.at[t]) for t in range(nt)]
        for cp in local:
            cp.start()
        for cp in local:
            cp.wait()

    return pl.pallas_call(
        body, name="place_own_shard", in_specs=[VMEM_SPEC] * nt, out_specs=[HBM_SPEC] * nt,
        out_shape=[jax.ShapeDtypeStruct((N_CHIP * SHARD_ROWS[t], s.shape[1]), s.dtype) for t, s in enumerate(shards)],
        scratch_shapes=[pltpu.SemaphoreType.DMA((nt,))], compiler_params=_cp())(*shards)


def _gather_rest_start(shards, zones, dep):
    n = len(shards)

    def body(*refs):
        srcs, lands = refs[:n], refs[n:2 * n]
        ssem, rsem = refs[2 * n + 1], refs[2 * n + 2]
        token = refs[-1]
        x, y, _ = _coords()
        chip = 2 * x + y
        for k2 in range(1, N_CHIP):
            for t in range(n):
                q = (k2 - 1) * n + t
                _remote(srcs[t], _rows(lands[t], chip * SHARD_ROWS[t], SHARD_ROWS[t]), ssem.at[q], rsem.at[q], _peer(2 * k2)).start()
        token[...] = jnp.zeros_like(token)

    hbm = lambda a: pltpu.with_memory_space_constraint(a, pltpu.HBM)
    dma = pltpu.SemaphoreType.DMA
    outs = pl.pallas_call(
        body, name="gather_rest", in_specs=[HBM_ONLY] * (2 * n) + [HBM_SPEC],
        out_specs=[SEM_SPEC, SEM_SPEC] + [HBM_ONLY] * (2 * n) + [VMEM_SPEC],
        out_shape=[dma((3 * n,)), dma((3 * n,))] + [pltpu.HBM(a.shape, a.dtype) for a in list(shards) + list(zones)]
        + [jax.ShapeDtypeStruct((8, 128), F32)],
        input_output_aliases={i: 2 + i for i in range(2 * n)},
        compiler_params=pltpu.CompilerParams(has_side_effects=DATAFLOW))(*[hbm(a) for a in list(shards) + list(zones)], dep)
    return outs[0], outs[1], outs[2:2 + n], outs[2 + n:2 + 2 * n], outs[-1]


def _gather_rest_wait(started, after):
    ssem, rsem, shards, zones, _ = started
    n = len(shards)

    def body(*refs):
        srcs, lands = refs[:n], refs[n:2 * n]
        ssem_ref, rsem_ref = refs[2 * n], refs[2 * n + 1]
        for k2 in range(1, N_CHIP):
            pchip = _dev_index(_peer(2 * k2)) // 2
            for t in range(n):
                q = (k2 - 1) * n + t
                cp = _remote(srcs[t], _rows(lands[t], pchip * SHARD_ROWS[t], SHARD_ROWS[t]), ssem_ref.at[q], rsem_ref.at[q], _peer(2 * k2))
                cp.wait_send()
                cp.wait_recv()

    outs = pl.pallas_call(
        body, name="gather_rest_wait", in_specs=[HBM_ONLY] * (2 * n) + [SEM_SPEC, SEM_SPEC, HBM_SPEC], out_specs=[HBM_ONLY] * (2 * n),
        out_shape=[pltpu.HBM(a.shape, a.dtype) for a in list(shards) + list(zones)],
        input_output_aliases={i: i for i in range(2 * n)},
        compiler_params=pltpu.CompilerParams(has_side_effects=DATAFLOW))(*shards, *zones, ssem, rsem, after)
    return outs[n:]


def _reduced_by(ref, t, dev):
    return _rows(ref, (dev // 2) * SHARD_ROWS[t] + (dev % 2) * (SHARD_ROWS[t] // 2), SHARD_ROWS[t] // 2)


def _scatter_start(grads, kinds, name):
    n = len(grads)

    def body(*refs):
        srcs, lands = refs[:n], refs[n:2 * n]
        ssem, rsem = refs[2 * n], refs[2 * n + 1]
        token = refs[-1]
        me = _dev_index(_coords())
        for k in range(1, N_DEV):
            for i, t in enumerate(kinds):
                q = (k - 1) * n + i
                _remote(_reduced_by(srcs[i], t, _dev_index(_peer(k))), lands[i].at[me], ssem.at[q], rsem.at[q], _peer(k)).start()
        token[...] = jnp.zeros_like(token)

    zones = [lax.empty((N_DEV, SHARD_ROWS[t] // 2, g.shape[1]), g.dtype) for g, t in zip(grads, kinds)]
    hbm = lambda a: pltpu.with_memory_space_constraint(a, pltpu.HBM)
    dma = pltpu.SemaphoreType.DMA
    outs = pl.pallas_call(
        body, name=name, in_specs=[HBM_ONLY] * (2 * n), out_specs=[SEM_SPEC, SEM_SPEC] + [HBM_ONLY] * (2 * n) + [VMEM_SPEC],
        out_shape=[dma((7 * n,)), dma((7 * n,))] + [pltpu.HBM(a.shape, a.dtype) for a in list(grads) + zones]
        + [jax.ShapeDtypeStruct((8, 128), F32)],
        input_output_aliases={i: 2 + i for i in range(2 * n)},
        compiler_params=pltpu.CompilerParams(has_side_effects=DATAFLOW))(*[hbm(a) for a in list(grads) + zones])
    return outs[0], outs[1], outs[2:2 + n], outs[2 + n:2 + 2 * n], outs[-1]


def _scatter_wait(started, kinds, after, name):
    ssem, rsem, grads, zones, _ = started
    n = len(grads)

    def body(*refs):
        srcs, lands = refs[:n], refs[n:2 * n]
        ssem_ref, rsem_ref = refs[2 * n], refs[2 * n + 1]
        for k in range(1, N_DEV):
            peer = _dev_index(_peer(k))
            for i, t in enumerate(kinds):
                q = (k - 1) * n + i
                cp = _remote(_reduced_by(srcs[i], t, peer), lands[i].at[peer], ssem_ref.at[q], rsem_ref.at[q], _peer(k))
                cp.wait_send()
                cp.wait_recv()

    outs = pl.pallas_call(
        body, name=name, in_specs=[HBM_ONLY] * (2 * n) + [SEM_SPEC, SEM_SPEC, HBM_SPEC], out_specs=[HBM_ONLY] * (2 * n),
        out_shape=[pltpu.HBM(a.shape, a.dtype) for a in list(grads) + list(zones)],
        input_output_aliases={i: i for i in range(2 * n)},
        compiler_params=pltpu.CompilerParams(has_side_effects=DATAFLOW))(*grads, *zones, ssem, rsem, after)
    return outs[:n], outs[n:]


def _grad_finish(grads, zones, kinds, name):
    nt = len(grads)
    pieces = _chunks(kinds, GATHER_CHUNK)
    npc = len(pieces)
    shard = [SHARD_ROWS[k] for k in kinds]

    def body(*refs):
        srcs, lands, outs = refs[:nt], refs[nt:2 * nt], refs[2 * nt:3 * nt]
        slots, sums = refs[3 * nt:4 * nt], refs[4 * nt:5 * nt]
        lsem, osem, xsem, ysem = refs[5 * nt:]
        x, y, c = _coords()
        me = _dev_index((x, y, c))
        sib = (x, y, 1 - c)
        loads = [pltpu.make_async_copy(_reduced_by(srcs[t], kinds[t], me), slots[t].at[me], lsem.at[t]) for t in range(nt)]
        for k in range(1, N_DEV):
            peer = _dev_index(_peer(k))
            loads += [pltpu.make_async_copy(lands[t].at[peer], slots[t].at[peer], lsem.at[k * nt + t]) for t in range(nt)]
        for cp in loads:
            cp.start()
        for cp in loads:
            cp.wait()
        sends = []
        place = lambda t, cc, off, n: _rows(outs[t], cc * (shard[t] // 2) + off, n)
        stores = []
        for q, (t, off, n) in enumerate(pieces):
            r = pl.ds(off, n)
            acc = slots[t][0, r, :].astype(F32)
            for d in range(1, N_DEV):
                acc = acc + slots[t][d, r, :].astype(F32)
            sums[t][r, :] = acc
            keep = pltpu.make_async_copy(sums[t].at[r, :], place(t, c, off, n), osem.at[q])
            give = _remote(sums[t].at[r, :], place(t, c, off, n), xsem.at[q], ysem.at[q], sib)
            keep.start()
            give.start()
            stores.append(keep)
            sends.append(give)
        for q, (t, off, n) in enumerate(pieces):
            got = place(t, 1 - c, off, n)
            _remote(got, got, xsem.at[q], ysem.at[q], sib).wait_recv()
        for cp in sends:
            cp.wait_send()
        for cp in stores:
            cp.wait()

    dma = pltpu.SemaphoreType.DMA
    return pl.pallas_call(
        body, name=name, in_specs=[HBM_SPEC] * (2 * nt), out_specs=[HBM_SPEC] * nt,
        out_shape=[jax.ShapeDtypeStruct((shard[t], g.shape[1]), F32) for t, g in enumerate(grads)],
        scratch_shapes=([pltpu.VMEM((N_DEV, shard[t] // 2, g.shape[1]), g.dtype) for t, g in enumerate(grads)]
                        + [pltpu.VMEM((shard[t] // 2, g.shape[1]), F32) for t, g in enumerate(grads)]
                        + [dma((N_DEV * nt,)), dma((npc,)), dma((npc,)), dma((npc,))]),
        compiler_params=_cp())(*grads, *zones)


def _adam_math(w, g, m, v):
    m = ADAM_B1 * m + (1.0 - ADAM_B1) * g
    v = ADAM_B2 * v + (1.0 - ADAM_B2) * (g * g)
    m_hat = m / (1.0 - ADAM_B1 ** ADAM_STEP)
    v_hat = v / (1.0 - ADAM_B2 ** ADAM_STEP)
    return -ADAM_LR * (m_hat / (jnp.sqrt(v_hat) + ADAM_EPS) + ADAM_WD * w), m, v


def _adamw(w, g, m, v, rb, name):
    rows, cols = w.shape

    def body(w_ref, g_ref, m_ref, v_ref, d_ref, nm_ref, nv_ref):
        d_ref[...], nm_ref[...], nv_ref[...] = _adam_math(w_ref[...], g_ref[...], m_ref[...], v_ref[...])

    spec = pl.BlockSpec((rb, cols), lambda i: (i, 0))
    return pl.pallas_call(body, grid=(rows // rb,), name=name, in_specs=[spec] * 4, out_specs=[spec] * 3,
                          out_shape=[jax.ShapeDtypeStruct(w.shape, F32)] * 3, compiler_params=_cp("parallel"))(w, g, m, v)


PACK = (("b_ada", 3 * D_MODEL), ("g_pre", D_MODEL), ("g_post", D_MODEL), ("gn_g", 512), ("qn_g", 64), ("kn_g", 64),
        ("w_dec_f", 4), ("w_dec_b", 4), ("loss", 1))
PACK_OFFSETS = {}
PACK_WIDTH = 0
for _name, _n in PACK:
    PACK_OFFSETS[_name] = PACK_WIDTH
    PACK_WIDTH += -(-_n // 128) * 128
SMALL_PARAMS = tuple(name for name, _ in PACK if name != "loss")


def _pack(parts):
    cols = []
    for name, n in PACK:
        pad = -(-n // 128) * 128 - n
        cols.append(parts[name].reshape(1, n).astype(F32))
        if pad:
            cols.append(jnp.zeros((1, pad), F32))
    return jnp.concatenate(cols, axis=1)


def _small_reduce(vec, cs, deps):
    ncol = 3 * D_MODEL // N_CHIP

    def body(vec_ref, cs_ref, *rest):
        tot_ref, gwa_ref, gat, ssem, rsem = rest[-5:]
        me = _dev_index(_coords())
        chip = me // 2
        gat[me] = vec_ref[...]
        sends = []
        for k in range(1, N_DEV):
            cp = _remote(vec_ref, gat.at[me], ssem.at[k - 1], rsem.at[k - 1], _peer(k))
            cp.start()
            sends.append(cp)
        for k in range(1, N_DEV):
            slot = gat.at[_dev_index(_peer(k))]
            _remote(slot, slot, ssem.at[k - 1], rsem.at[k - 1], _peer(k)).wait_recv()
        rows = [gat[d] for d in range(N_DEV)]
        tot = rows[0]
        for d in range(1, N_DEV):
            tot = tot + rows[d]
        tot_ref[...] = tot
        cs = cs_ref[...]
        ca_t = jnp.transpose(jnp.concatenate([cs * _sigmoid(cs), jnp.zeros((128 - N_DEV, D_MODEL), F32)], axis=0))
        dm = jnp.concatenate(rows + [jnp.zeros((128 - N_DEV, PACK_WIDTH), F32)], axis=0)
        for jj in range(N_CHIP):
            @pl.when(chip == jj)
            def _():
                gwa_ref[...] = _nn(ca_t, dm[:, ncol * jj:ncol * (jj + 1)], precision=HIGHEST)
        for cp in sends:
            cp.wait_send()

    return pl.pallas_call(
        body, name="small_reduce", in_specs=[VMEM_SPEC, VMEM_SPEC] + [HBM_SPEC] * len(deps), out_specs=[VMEM_SPEC] * 2,
        out_shape=[jax.ShapeDtypeStruct((1, PACK_WIDTH), F32), jax.ShapeDtypeStruct((D_MODEL, ncol), F32)],
        scratch_shapes=[pltpu.VMEM((N_DEV, 1, PACK_WIDTH), F32), pltpu.SemaphoreType.DMA((7,)), pltpu.SemaphoreType.DMA((7,))],
        compiler_params=_cp())(vec, cs, *deps)


def _small_adamw(tot, given):
    sizes = dict(PACK)
    np_ = len(SMALL_PARAMS)

    def body(*refs):
        tot_ref = refs[0]
        wmv = refs[1:1 + 3 * np_]
        outs = refs[1 + 3 * np_:1 + 7 * np_]
        loss_ref = refs[-1]
        for i, name in enumerate(SMALL_PARAMS):
            off, n = PACK_OFFSETS[name], sizes[name]
            g = tot_ref[:, off:off + n]
            w_ref, m_ref, v_ref = wmv[3 * i:3 * i + 3]
            outs[i][...] = g
            outs[np_ + i][...], outs[2 * np_ + i][...], outs[3 * np_ + i][...] = _adam_math(w_ref[...], g, m_ref[...], v_ref[...])
        loss_ref[...] = tot_ref[:, PACK_OFFSETS["loss"]:PACK_OFFSETS["loss"] + 1]

    flat = [a for name in SMALL_PARAMS for a in given[name]]
    shapes = [jax.ShapeDtypeStruct((1, sizes[name]), F32) for name in SMALL_PARAMS]
    res = pl.pallas_call(body, name="small_adamw", in_specs=[VMEM_SPEC] * (1 + 3 * np_), out_specs=[VMEM_SPEC] * (4 * np_ + 1),
                         out_shape=shapes * 4 + [jax.ShapeDtypeStruct((1, 1), F32)], compiler_params=_cp())(tot, *flat)
    return [dict(zip(SMALL_PARAMS, res[k * np_:(k + 1) * np_])) for k in range(4)], res[-1]


def kernel(x, c, w_ada, b_ada, g_pre, w_in, qn_g, kn_g, w_dec_f, w_dec_b, gn_g, w_pa, w_pr, w_out, g_post, loss_target, m_w_ada, m_b_ada, m_g_pre, m_w_in, m_qn_g, m_kn_g, m_w_dec_f, m_w_dec_b, m_gn_g, m_w_pa, m_w_pr, m_w_out, m_g_post, v_w_ada, v_b_ada, v_g_pre, v_w_in, v_qn_g, v_kn_g, v_w_dec_f, v_w_dec_b, v_gn_g, v_w_pa, v_w_pr, v_w_out, v_g_post):
    shards = (w_in[0].T.astype(MXU_DTYPE), jnp.concatenate([w_pa[0].T, w_pr[0].T], axis=1).astype(MXU_DTYPE),
              w_out[0].astype(MXU_DTYPE))
    cs, mod, w_attn = _mod_and_attn_rows(c, w_ada[0], b_ada.reshape(N_CHIP, 3 * D_MODEL // N_CHIP), shards[0])
    started = {}

    def rest_start(dep):
        started["rest"] = _gather_rest_start(shards, _place_own(shards), dep)
        return started["rest"][-1]

    def rest_wait(after):
        return _gather_rest_wait(started["rest"], after)

    kinds = {"early": (1, 2), "late": (0,)}

    def send(name, arrays):
        started[name] = _scatter_start(arrays, kinds[name], "grad_scatter_" + name)
        return started[name][-1]

    grad_x, _, _, _, small = _local_step(x[0], loss_target[0], mod, g_pre, qn_g, kn_g, w_dec_f, w_dec_b,
                                         gn_g, g_post, w_attn, rest_start, rest_wait, send=send)
    small = dict(small)
    small["b_ada"] = small.pop("dmod")
    given = dict(b_ada=(b_ada, m_b_ada, v_b_ada), g_pre=(g_pre, m_g_pre, v_g_pre), g_post=(g_post, m_g_post, v_g_post),
                 gn_g=(gn_g, m_gn_g, v_gn_g), qn_g=(qn_g, m_qn_g, v_qn_g), kn_g=(kn_g, m_kn_g, v_kn_g),
                 w_dec_f=(w_dec_f, m_w_dec_f, v_w_dec_f), w_dec_b=(w_dec_b, m_w_dec_b, v_w_dec_b))
    grads, deltas, new_m, new_v = {}, {}, {}, {}

    def update(name, w, g, m, v, back):
        d, nm, nv = _adamw(w, g, m, v, w.shape[0] // 4, "adamw_" + name)
        grads[name], deltas[name], new_m[name], new_v[name] = back(g), back(d), back(nm), back(nv)
        return d

    lead = lambda a: a[None]
    (g_pt, g_out), (z_pt, z_out) = _scatter_wait(started["early"], kinds["early"], grad_x, "grad_scatter_early_wait")
    r_pt, r_out = _grad_finish((g_pt, g_out), (z_pt, z_out), kinds["early"], "grad_finish_early")
    early = (update("w_pa", w_pa[0], r_pt[:, :512].T, m_w_pa[0], v_w_pa[0], lead),
             update("w_pr", w_pr[0], r_pt[:, 512:].T, m_w_pr[0], v_w_pr[0], lead),
             update("w_out", w_out[0], r_out, m_w_out[0], v_w_out[0], lead))
    tot, g_w_ada = _small_reduce(_pack(small), cs.reshape(N_DEV, D_MODEL), early)
    small_parts, loss = _small_adamw(tot, given)
    for dst, part in zip((grads, deltas, new_m, new_v), small_parts):
        dst.update(part)
    last = update("w_ada", w_ada[0], g_w_ada, m_w_ada[0], v_w_ada[0], lead)

    (g_in_t,), (z_in,) = _scatter_wait(started["late"], kinds["late"], last, "grad_scatter_late_wait")
    (r_in,) = _grad_finish((g_in_t,), (z_in,), kinds["late"], "grad_finish_late")
    tr = lambda a: a[0].T
    update("w_in", tr(w_in), r_in, tr(m_w_in), tr(v_w_in), lambda a: a.T[None])
    order = ("w_ada", "b_ada", "g_pre", "w_in", "qn_g", "kn_g", "w_dec_f", "w_dec_b", "gn_g", "w_pa", "w_pr", "w_out", "g_post")
    return (loss[0, 0], grad_x[None], *[grads[n] for n in order], *[deltas[n] for n in order],
            *[new_m[n] for n in order], *[new_v[n] for n in order])
```

```python
import functools

import jax
import jax.numpy as jnp
import numpy as np
from jax import lax
from jax.experimental import pallas as pl
from jax.experimental.pallas import tpu as pltpu

F32 = jnp.float32
BF16 = jnp.bfloat16
MXU_DTYPE = jnp.bfloat16

D_MODEL = 1024
GRID_W = 64
HEAD_DIM = 64
ROPE_THETA = 10000.0
EPS = 1e-6
RET_HEADS = 4
RET_CHUNK = 256
RET_GROUP = 4
IN_WIDTH = 4864
QA, KA, VA, ZA, QR, KR, VR, ZR, GL = 0, 512, 640, 768, 1280, 1536, 1792, 2304, 2816
ATTN_COLS = ZA
ZA_B, QR_B, KR_B, VR_B, ZR_B, GL_B = (c - ATTN_COLS for c in (ZA, QR, KR, VR, ZR, GL))

ADAM_LR, ADAM_B1, ADAM_B2, ADAM_EPS, ADAM_WD, ADAM_STEP = 0.001, 0.9, 0.999, 1e-08, 0.01, 10

VMEM_LIMIT = 56 * 1024 * 1024
MESH = pl.DeviceIdType.MESH
HIGHEST = lax.Precision.HIGHEST
LOG2E = 1.4426950408889634
LN2 = 0.6931471805599453


def _cp(*sem, **kw):
    if sem:
        kw["dimension_semantics"] = sem
    return pltpu.CompilerParams(vmem_limit_bytes=VMEM_LIMIT, **kw)


def _nn(a, b, **kw):
    return lax.dot_general(a, b, (((1,), (0,)), ((), ())), preferred_element_type=F32, **kw)


def _nt(a, b):
    return lax.dot_general(a, b, (((1,), (1,)), ((), ())), preferred_element_type=F32)


def _tn(a, b):
    return lax.dot_general(a, b, (((0,), (0,)), ((), ())), preferred_element_type=F32)


def _mx(x):
    return x.astype(MXU_DTYPE)


def _lane(shape):
    return lax.broadcasted_iota(jnp.int32, shape, 1)


def _sigmoid(z):
    return 1.0 / (1.0 + jnp.exp(-z))


def _rowsum128(x):
    s = jnp.sum(x, axis=0, keepdims=True)
    out = s[:, 0:128]
    for k in range(1, x.shape[1] // 128):
        out = out + s[:, 128 * k:128 * (k + 1)]
    return out


def _swap16(x):
    return jnp.where((_lane(x.shape) & 16) == 0, pltpu.roll(x, 112, 1), pltpu.roll(x, 16, 1))


def _rope(x, cos, sin):
    return x * cos + _swap16(x) * sin


def _rope_t(d, cos, sin):
    return d * cos + _swap16(d * sin)


def _blockdiag64():
    r = lax.broadcasted_iota(jnp.int32, (128, 128), 0) // 64
    c = lax.broadcasted_iota(jnp.int32, (128, 128), 1) // 64
    return (r == c).astype(BF16)


def _seg64(x, m):
    hi = x.astype(BF16)
    lo = (x - hi.astype(F32)).astype(BF16)
    return _nn(hi, m) + _nn(lo, m)


def _rope_tables(seq):
    t = np.arange(seq)
    row = (t // GRID_W).astype(np.float32)
    col = (t % GRID_W).astype(np.float32)
    half = HEAD_DIM // 2
    inv_freq = (np.float32(ROPE_THETA) ** (-np.arange(0, half, 2, dtype=np.float32) / np.float32(half))).astype(np.float32)
    ar = (row[:, None] * inv_freq[None, :]).astype(np.float32).astype(np.float64)
    ac = (col[:, None] * inv_freq[None, :]).astype(np.float32).astype(np.float64)
    cr, sr, cc, sc = np.cos(ar), np.sin(ar), np.cos(ac), np.sin(ac)
    cos64 = np.concatenate([cr, cr, cc, cc], axis=1)
    sin64 = np.concatenate([-sr, sr, -sc, sc], axis=1)
    return jnp.asarray(np.tile(cos64, (1, 2)), F32), jnp.asarray(np.tile(sin64, (1, 2)), F32)


def _attn_inputs(x, mod, g_pre, w_attn, cos, sin, qg2, kg2):
    seq = x.shape[0]
    bs = 512

    def body(x_ref, mod_ref, g_ref, w_ref, cos_ref, sin_ref, qg_ref, kg_ref, h_ref, pa_ref, qt_ref, kd_ref, vd_ref):
        xv = x_ref[...]
        r = lax.rsqrt(jnp.mean(xv * xv, axis=-1, keepdims=True) + EPS)
        shift = mod_ref[:, 0:D_MODEL]
        scale = mod_ref[:, D_MODEL:2 * D_MODEL]
        hb = ((xv * r) * g_ref[...] * (1.0 + scale) + shift).astype(h_ref.dtype)
        h_ref[...] = hb
        p = _nt(hb, w_ref[...])
        pa_ref[...] = p
        m = _blockdiag64()
        cs, sn = cos_ref[...], sin_ref[...]
        lo = _lane((bs, 128)) < 64
        for pr in range(4):
            q = p[:, QA + 128 * pr:QA + 128 * (pr + 1)]
            r = lax.rsqrt(_seg64(q * q, m) * (1.0 / 64) + EPS)
            qt_ref[:, 128 * pr:128 * (pr + 1)] = (_rope(q * r * qg_ref[...], cs, sn) * (0.125 * LOG2E)).astype(qt_ref.dtype)
        k = p[:, KA:KA + 128]
        r = lax.rsqrt(_seg64(k * k, m) * (1.0 / 64) + EPS)
        kr = _rope(k * r * kg_ref[...], cs, sn)
        ksw = pltpu.roll(kr, 64, 1)
        kd_ref[0] = jnp.where(lo, kr, ksw).astype(kd_ref.dtype)
        kd_ref[1] = jnp.where(lo, ksw, kr).astype(kd_ref.dtype)
        v = p[:, VA:VA + 128]
        vsw = pltpu.roll(v, 64, 1)
        vd_ref[0] = jnp.where(lo, v, vsw).astype(vd_ref.dtype)
        vd_ref[1] = jnp.where(lo, vsw, v).astype(vd_ref.dtype)

    row = lambda w: pl.BlockSpec((bs, w), lambda i: (i, 0))
    fix = lambda a, b: pl.BlockSpec((a, b), lambda i: (0, 0))
    dup = pl.BlockSpec((2, bs, 128), lambda i: (0, i, 0))
    sds = jax.ShapeDtypeStruct
    return pl.pallas_call(
        body, grid=(seq // bs,), name="attn_inputs",
        in_specs=[row(D_MODEL), fix(1, 3 * D_MODEL), fix(1, D_MODEL), fix(ATTN_COLS, D_MODEL), row(128), row(128), fix(1, 128), fix(1, 128)],
        out_specs=[row(D_MODEL), row(ATTN_COLS), row(512), dup, dup],
        out_shape=[sds((seq, D_MODEL), MXU_DTYPE), sds((seq, ATTN_COLS), F32), sds((seq, 512), MXU_DTYPE),
                   sds((2, seq, 128), MXU_DTYPE), sds((2, seq, 128), MXU_DTYPE)],
        compiler_params=_cp("parallel"))(x, mod, g_pre, w_attn, cos, sin, qg2, kg2)


def _in_proj_dx(x, dp, win_t, dout, mod, g_pre, dep=None):
    seq = x.shape[0]
    bs = 512
    deps, dep_specs = _dep_args(dep, 1)

    def body(x_ref, dp_ref, w_ref, dout_ref, mod_ref, g_ref, *rest):
        gx_ref, dshift_ref, dscale_ref, dg_ref = rest[-4:]

        @pl.when(pl.program_id(0) == 0)
        def _():
            dshift_ref[...] = jnp.zeros_like(dshift_ref)
            dscale_ref[...] = jnp.zeros_like(dscale_ref)
            dg_ref[...] = jnp.zeros_like(dg_ref)

        xv = x_ref[...]
        dh = _nn(dp_ref[...], w_ref[...])
        g = g_ref[...]
        r = lax.rsqrt(jnp.mean(xv * xv, axis=-1, keepdims=True) + EPS)
        xn = xv * r
        scale = mod_ref[:, D_MODEL:2 * D_MODEL]
        dshift_ref[...] += jnp.sum(dh, axis=0, keepdims=True)
        dscale_ref[...] += jnp.sum(dh * (xn * g), axis=0, keepdims=True)
        da = dh * (1.0 + scale)
        dg_ref[...] += jnp.sum(da * xn, axis=0, keepdims=True)
        dxn = da * g
        dx = r * (dxn - xn * jnp.mean(dxn * xn, axis=-1, keepdims=True))
        gx_ref[...] = dout_ref[...] + dx

    row = pl.BlockSpec((bs, D_MODEL), lambda i: (i, 0))
    vec = pl.BlockSpec((1, D_MODEL), lambda i: (0, 0))
    return pl.pallas_call(
        body, grid=(seq // bs,), name="in_proj_dx",
        in_specs=[row, pl.BlockSpec((bs, IN_WIDTH), lambda i: (i, 0)), pl.BlockSpec((IN_WIDTH, D_MODEL), lambda i: (0, 0)), row,
                  pl.BlockSpec((1, 3 * D_MODEL), lambda i: (0, 0)), vec] + dep_specs,
        out_specs=[row, vec, vec, vec],
        out_shape=[jax.ShapeDtypeStruct((seq, D_MODEL), F32)] + [jax.ShapeDtypeStruct((1, D_MODEL), F32)] * 3,
        compiler_params=_cp("arbitrary"))(x, dp, win_t, dout, mod, g_pre, *deps)


def _dep_args(dep, grid_rank):
    if dep is None:
        return [], []
    return [dep], [pl.BlockSpec(dep.shape, lambda *_: (0,) * dep.ndim)]


def _matmul(a, b, *, ta, tb, tm, tn, out_dtype, name, dep=None):
    kdim = a.shape[0] if ta else a.shape[1]
    m = a.shape[1] if ta else a.shape[0]
    n = b.shape[0] if tb else b.shape[1]
    assert m % tm == 0 and n % tn == 0
    deps, dep_specs = _dep_args(dep, 2)

    def body(a_ref, b_ref, *rest):
        o_ref = rest[-1]
        dims = (((0 if ta else 1,), (1 if tb else 0,)), ((), ()))
        o_ref[...] = lax.dot_general(a_ref[...], b_ref[...], dims, preferred_element_type=F32).astype(o_ref.dtype)

    a_spec = pl.BlockSpec((kdim, tm), lambda j, i: (0, i)) if ta else pl.BlockSpec((tm, kdim), lambda j, i: (i, 0))
    b_spec = pl.BlockSpec((tn, kdim), lambda j, i: (j, 0)) if tb else pl.BlockSpec((kdim, tn), lambda j, i: (0, j))
    return pl.pallas_call(
        body, grid=(n // tn, m // tm), name=name, in_specs=[a_spec, b_spec] + dep_specs,
        out_specs=pl.BlockSpec((tm, tn), lambda j, i: (i, j)),
        out_shape=jax.ShapeDtypeStruct((m, n), out_dtype), compiler_params=_cp("parallel", "parallel"))(a, b, *deps)


def _in_proj_rest(h, win_t, cos, sin):
    seq = h.shape[0]
    tm = min(1024, seq)
    ncols = IN_WIDTH - ATTN_COLS
    tn = ncols // 2
    assert ZR_B <= tn and ATTN_COLS % 128 == 0 and tn % 128 == 0

    def body(h_ref, w_ref, cos_ref, sin_ref, p_ref, rq_ref, rk_ref, rv_ref):
        p = _nt(h_ref[...], w_ref[...])
        p_ref[...] = p

        @pl.when(pl.program_id(1) == 0)
        def _():
            cs, sn = cos_ref[...], sin_ref[...]
            for pr in range(2):
                sl = slice(128 * pr, 128 * (pr + 1))
                rq_ref[:, sl] = _rope(p[:, QR_B + 128 * pr:QR_B + 128 * (pr + 1)], cs, sn).astype(rq_ref.dtype)
                rk_ref[:, sl] = (_rope(p[:, KR_B + 128 * pr:KR_B + 128 * (pr + 1)], cs, sn) * 0.125).astype(rk_ref.dtype)
            rv_ref[...] = p[:, VR_B:VR_B + 512].astype(rv_ref.dtype)

    row = lambda w: pl.BlockSpec((tm, w), lambda i, j: (i, 0))
    sds = jax.ShapeDtypeStruct
    return pl.pallas_call(
        body, grid=(seq // tm, 2), name="in_proj_rest",
        in_specs=[row(D_MODEL), pl.BlockSpec((pl.Element(tn), pl.Element(D_MODEL)),
                                             lambda i, j: (pl.multiple_of(ATTN_COLS + j * tn, 128), 0)), row(128), row(128)],
        out_specs=[pl.BlockSpec((tm, tn), lambda i, j: (i, j)), row(256), row(256), row(512)],
        out_shape=[sds((seq, ncols), F32), sds((seq, 256), MXU_DTYPE), sds((seq, 256), MXU_DTYPE), sds((seq, 512), MXU_DTYPE)],
        compiler_params=_cp("parallel", "arbitrary"))(h, win_t, cos, sin)


def _halves(kd):
    lo = _lane(kd.shape) < 64
    z = jnp.zeros_like(kd)
    return jnp.concatenate([jnp.where(lo, kd, z), jnp.where(lo, z, kd)], axis=0)


def _attn_fwd(qt, kd, vd, dep=None):
    seq = qt.shape[0]
    bq, bk = min(1024, seq), min(1024, seq)
    nk = seq // bk
    deps, dep_specs = _dep_args(dep, 2)

    def body(q_ref, k_ref, v_ref, *rest):
        o_ref, lse_ref, m_scr, l_scr, acc = rest[-5:]
        j = pl.program_id(1)
        m_scr[...] = jnp.full_like(m_scr, -jnp.inf)
        l_scr[...] = jnp.zeros_like(l_scr)
        acc[...] = jnp.zeros_like(acc)
        lo = _lane((bq, 128)) < 64

        def kv_block(n, carry):
            rows = pl.ds(pl.multiple_of(n * bk, bk), bk)
            k2, v2 = _halves(k_ref[rows, :]), _halves(v_ref[rows, :])
            for pr in range(2):
                s2 = _nt(q_ref[:, 128 * pr:128 * (pr + 1)], k2)
                ps, alphas = [], []
                for e in range(2):
                    g = 2 * pr + e
                    s = s2[:, e * bk:(e + 1) * bk]
                    m_prev = m_scr[g]
                    m_next = jnp.maximum(m_prev, jnp.max(s, axis=1, keepdims=True))
                    alpha = jnp.exp2(m_prev - m_next)
                    pe = jnp.exp2(s - jnp.tile(m_next, (1, bk // 128)))
                    l_scr[g] = alpha * l_scr[g] + jnp.sum(pe, axis=1, keepdims=True)
                    m_scr[g] = m_next
                    ps.append(_mx(pe))
                    alphas.append(alpha)
                o2 = _nn(jnp.concatenate(ps, axis=1), v2)
                sl = slice(128 * pr, 128 * (pr + 1))
                acc[:, sl] = acc[:, sl] * jnp.where(lo, alphas[0], alphas[1]) + o2
            return carry

        lax.fori_loop(0, nk, kv_block, 0)
        for pr in range(2):
            sl = slice(128 * pr, 128 * (pr + 1))
            o_ref[:, sl] = acc[:, sl] / jnp.where(lo, l_scr[2 * pr], l_scr[2 * pr + 1])
        for g in range(4):
            lse = jnp.transpose(m_scr[g] + jnp.log2(l_scr[g]))
            lse_ref[pl.ds(4 * j + g, 1), :] = lse[0:1, :]

    return pl.pallas_call(
        body, grid=(seq // bq, 2), name="attn_fwd",
        in_specs=[pl.BlockSpec((bq, 256), lambda i, j: (i, j)), pl.BlockSpec((None, seq, 128), lambda i, j: (j, 0, 0)),
                  pl.BlockSpec((None, seq, 128), lambda i, j: (j, 0, 0))] + dep_specs,
        out_specs=[pl.BlockSpec((bq, 256), lambda i, j: (i, j)), pl.BlockSpec((8, bq), lambda i, j: (0, i))],
        out_shape=[jax.ShapeDtypeStruct((seq, 512), F32), jax.ShapeDtypeStruct((8, seq), F32)],
        scratch_shapes=[pltpu.VMEM((4, bq, 128), F32), pltpu.VMEM((4, bq, 128), F32), pltpu.VMEM((bq, 256), F32)],
        compiler_params=_cp("parallel", "arbitrary"))(qt, kd, vd, *deps)


def _attn_bwd(qt, kd, vd, do, lse, delta, dep=None):
    seq = qt.shape[0]
    bq, bk = min(1024, seq), min(1024, seq)
    nq, nk = seq // bq, seq // bk
    deps, dep_specs = _dep_args(dep, 3)

    def body(q_ref, do_ref, k_ref, v_ref, lse_ref, del_ref, *rest):
        dq_ref, dk_ref, dv_ref = rest[-3:]
        j, i = pl.program_id(0), pl.program_id(1)
        dq_ref[...] = jnp.zeros_like(dq_ref)

        @pl.when(i == 0)
        def _():
            dk_ref[...] = jnp.zeros_like(dk_ref)
            dv_ref[...] = jnp.zeros_like(dv_ref)

        lo = _lane((bk, 128)) < 64
        big = lambda r: jnp.concatenate([jnp.broadcast_to(r[0], (bk, bq)), jnp.broadcast_to(r[1], (bk, bq))], axis=0)

        def kv_block(n, carry):
            rows = pl.ds(pl.multiple_of(n * bk, bk), bk)
            k2, v2 = _halves(k_ref[rows, :]), _halves(v_ref[rows, :])
            for pr in range(2):
                sl = slice(128 * pr, 128 * (pr + 1))
                qp, dop = q_ref[:, sl], do_ref[:, sl]
                s_t = _nt(k2, qp)
                dp_t = _nt(v2, dop)
                ls = [lse_ref[pl.ds(4 * j + 2 * pr + e, 1), :] for e in range(2)]
                dl = [del_ref[pl.ds(4 * j + 2 * pr + e, 1), :] for e in range(2)]
                p_t = jnp.exp2(s_t - big(ls))
                ds_t = _mx(p_t * (dp_t - big(dl)))
                rv = _nn(_mx(p_t), dop)
                rk = _nn(ds_t, qp) * LN2
                dv_ref[rows, :] += jnp.where(lo, rv[:bk], rv[bk:])
                dk_ref[rows, :] += jnp.where(lo, rk[:bk], rk[bk:])
                dq_ref[:, sl] += _tn(ds_t, k2)
            return carry

        lax.fori_loop(0, nk, kv_block, 0)

    return pl.pallas_call(
        body, grid=(2, nq), name="attn_bwd",
        in_specs=[pl.BlockSpec((bq, 256), lambda j, i: (i, j)), pl.BlockSpec((bq, 256), lambda j, i: (i, j)),
                  pl.BlockSpec((None, seq, 128), lambda j, i: (j, 0, 0)), pl.BlockSpec((None, seq, 128), lambda j, i: (j, 0, 0)),
                  pl.BlockSpec((8, bq), lambda j, i: (0, i)), pl.BlockSpec((8, bq), lambda j, i: (0, i))] + dep_specs,
        out_specs=[pl.BlockSpec((bq, 256), lambda j, i: (i, j)), pl.BlockSpec((None, seq, 128), lambda j, i: (j, 0, 0)),
                   pl.BlockSpec((None, seq, 128), lambda j, i: (j, 0, 0))],
        out_shape=[jax.ShapeDtypeStruct((seq, 512), F32), jax.ShapeDtypeStruct((2, seq, 128), F32),
                   jax.ShapeDtypeStruct((2, seq, 128), F32)],
        compiler_params=_cp("arbitrary", "arbitrary"))(qt, do, kd, vd, lse, delta, *deps)


def _ret_tables(lg_f, lg_b, c):
    idx = jnp.arange(c, dtype=F32)
    diff = idx[:, None] - idx[None, :]
    a, b = lg_f[:, None, None], lg_b[:, None, None]
    low, up = (diff >= 0)[None], (diff < 0)[None]
    dmat = jnp.where(low, jnp.exp(a * jnp.maximum(diff, 0.0)[None]), jnp.exp(b * jnp.maximum(-diff, 0.0)[None]))
    dda = jnp.where(low, diff[None] * jnp.exp(a * jnp.maximum(diff, 0.0)[None]), 0.0)
    ddb = jnp.where(up, -diff[None] * jnp.exp(b * jnp.maximum(-diff, 0.0)[None]), 0.0)
    rep = lambda w: jnp.broadcast_to(w[:, :, None], (RET_HEADS, c, 128))
    wqf = rep(jnp.exp(lg_f[:, None] * (idx + 1.0)[None]))
    wqb = rep(jnp.exp(lg_b[:, None] * (c - idx)[None]))
    wkf = rep(jnp.exp(lg_f[:, None] * (c - 1.0 - idx)[None]))
    wkb = rep(jnp.exp(lg_b[:, None] * idx[None]))
    cdf, cdb = jnp.exp(lg_f * c), jnp.exp(lg_b * c)
    dec_fb = jnp.broadcast_to(jnp.concatenate([cdf, cdb])[:, None], (8, 128))
    dec_bf = jnp.broadcast_to(jnp.concatenate([cdb, cdf])[:, None], (8, 128))
    return dict(dmat=dmat, dda=dda, ddb=ddb, wqf=wqf, wqb=wqb, wkf=wkf, wkb=wkb, dec_fb=dec_fb, dec_bf=dec_bf)


def _ret_states(xk, yv, w_fwd, w_rev, dec, name):
    seq = xk.shape[0]
    c = RET_CHUNK
    nc = seq // c
    grp = min(2 * RET_GROUP, nc)
    ns = nc // grp

    def body(xf_ref, yf_ref, xr_ref, yr_ref, wf_ref, wr_ref, dec_ref, sf_ref, sr_ref, st_f, st_r):
        @pl.when(pl.program_id(0) == 0)
        def _():
            st_f[...] = jnp.zeros_like(st_f)
            st_r[...] = jnp.zeros_like(st_r)

        lane = _lane((c, 128))

        def chunk(g, carry):
            for x_ref, y_ref, w_ref, s_ref, st, base, gg in ((xf_ref, yf_ref, wf_ref, sf_ref, st_f, 0, g),
                                                             (xr_ref, yr_ref, wr_ref, sr_ref, st_r, 4, grp - 1 - g)):
                rows = pl.ds(pl.multiple_of(gg * c, c), c)
                for h in range(RET_HEADS):
                    pr, e = h // 2, h % 2
                    half = (lane < 64) if e == 0 else (lane >= 64)
                    s_ref[gg, h] = st[h].astype(s_ref.dtype)
                    xm = jnp.where(half, x_ref[rows, 128 * pr:128 * (pr + 1)].astype(F32) * w_ref[h], 0.0)
                    st[h] = st[h] * dec_ref[base + h:base + h + 1, :] + _tn(_mx(xm), _mx(y_ref[rows, 128 * h:128 * (h + 1)]))
            return carry

        lax.fori_loop(0, grp, chunk, 0)

    xs = lambda f: pl.BlockSpec((grp * c, 256), f)
    ys = lambda f: pl.BlockSpec((grp * c, 512), f)
    fwd, rev = (lambda n: (n, 0)), (lambda n: (ns - 1 - n, 0))
    wsp = pl.BlockSpec((RET_HEADS, c, 128), lambda n: (0, 0, 0))
    return pl.pallas_call(
        body, grid=(ns,), name=name,
        in_specs=[xs(fwd), ys(fwd), xs(rev), ys(rev), wsp, wsp, pl.BlockSpec((8, 128), lambda n: (0, 0))],
        out_specs=[pl.BlockSpec((grp, RET_HEADS, 128, 128), lambda n: (n, 0, 0, 0)),
                   pl.BlockSpec((grp, RET_HEADS, 128, 128), lambda n: (ns - 1 - n, 0, 0, 0))],
        out_shape=[jax.ShapeDtypeStruct((nc, RET_HEADS, 128, 128), MXU_DTYPE)] * 2,
        scratch_shapes=[pltpu.VMEM((RET_HEADS, 128, 128), F32), pltpu.VMEM((RET_HEADS, 128, 128), F32)],
        compiler_params=_cp("arbitrary"))(xk, yv, xk, yv, w_fwd, w_rev, dec)


def _ret_fwd(rq, rk, rv, rf, rb, tb):
    seq = rq.shape[0]
    c = RET_CHUNK
    grp = min(2 * RET_GROUP, seq // c)

    def body(q_ref, k_ref, v_ref, rf_ref, rb_ref, d_ref, wqf_ref, wqb_ref, o_ref):
        lane = _lane((c, 128))

        def chunk(g, carry):
            rows = pl.ds(pl.multiple_of(g * c, c), c)
            for h in range(RET_HEADS):
                pr, e = h // 2, h % 2
                half = (lane < 64) if e == 0 else (lane >= 64)
                sl = slice(128 * pr, 128 * (pr + 1))
                qp, kp = q_ref[rows, sl], k_ref[rows, sl]
                km = jnp.where(half, kp, jnp.zeros_like(kp))
                sd = _mx(_nt(qp, km) * d_ref[h])
                qf = qp.astype(F32)
                o = _nn(sd, v_ref[rows, 128 * h:128 * (h + 1)])
                o = o + _nn(_mx(qf * wqf_ref[h]), rf_ref[g, h]) + _nn(_mx(qf * wqb_ref[h]), rb_ref[g, h])
                o_ref[rows, 128 * h:128 * (h + 1)] = o
            return carry

        lax.fori_loop(0, grp, chunk, 0)

    st = pl.BlockSpec((grp, RET_HEADS, 128, 128), lambda n: (n, 0, 0, 0))
    wsp = pl.BlockSpec((RET_HEADS, c, 128), lambda n: (0, 0, 0))
    return pl.pallas_call(
        body, grid=(seq // (grp * c),), name="ret_fwd",
        in_specs=[pl.BlockSpec((grp * c, 256), lambda n: (n, 0)), pl.BlockSpec((grp * c, 256), lambda n: (n, 0)),
                  pl.BlockSpec((grp * c, 512), lambda n: (n, 0)), st, st, pl.BlockSpec((RET_HEADS, c, c), lambda n: (0, 0, 0)), wsp, wsp],
        out_specs=pl.BlockSpec((grp * c, 512), lambda n: (n, 0)),
        out_shape=jax.ShapeDtypeStruct((seq, 512), F32), compiler_params=_cp("parallel"))(
            rq, rk, rv, rf, rb, tb["dmat"], tb["wqf"], tb["wqb"])


def _ret_bwd(rq, rk, rv, do, rf, rb, hf, hb, tb):
    seq = rq.shape[0]
    c = RET_CHUNK
    grp = min(RET_GROUP, seq // c)

    def body(q_ref, k_ref, v_ref, do_ref, rf_ref, rb_ref, hf_ref, hb_ref, d_ref, dda_ref, ddb_ref,
             wqf_ref, wqb_ref, wkf_ref, wkb_ref, cdec_ref, dq_ref, dk_ref, dv_ref, dlg_ref):
        @pl.when(pl.program_id(0) == 0)
        def _():
            dlg_ref[...] = jnp.zeros_like(dlg_ref)

        lane = _lane((c, 128))
        pos = lax.broadcasted_iota(jnp.int32, (c, 128), 0).astype(F32)

        def chunk(g, carry):
            rows = pl.ds(pl.multiple_of(g * c, c), c)
            for pr in range(2):
                sl = slice(128 * pr, 128 * (pr + 1))
                qp, kp = q_ref[rows, sl], k_ref[rows, sl]
                qf, kf = qp.astype(F32), kp.astype(F32)
                dq_acc = jnp.zeros((c, 128), F32)
                dk_acc = jnp.zeros((c, 128), F32)
                for e in range(2):
                    h = 2 * pr + e
                    half = (lane < 64) if e == 0 else (lane >= 64)
                    hs = slice(128 * h, 128 * (h + 1))
                    km = jnp.where(half, kp, jnp.zeros_like(kp))
                    kmf = km.astype(F32)
                    vh, doh = v_ref[rows, hs], do_ref[rows, hs]
                    rfh, rbh, hfh, hbh = rf_ref[g, h], rb_ref[g, h], hf_ref[g, h], hb_ref[g, h]
                    s = _nt(qp, km)
                    dpm = _nt(doh, vh)
                    ds_b = _mx(dpm * d_ref[h])
                    sd_b = _mx(s * d_ref[h])
                    dq_if = wqf_ref[h] * _nt(doh, rfh)
                    dq_ib = wqb_ref[h] * _nt(doh, rbh)
                    dq_acc = dq_acc + _nn(ds_b, km) + dq_if + dq_ib
                    dk_sf = wkf_ref[h] * _nt(vh, hfh)
                    dk_sb = wkb_ref[h] * _nt(vh, hbh)
                    dk_acc = dk_acc + jnp.where(half, _tn(ds_b, qp), 0.0) + dk_sf + dk_sb
                    dv = _tn(sd_b, doh) + _nn(_mx(kmf * wkf_ref[h]), hfh) + _nn(_mx(kmf * wkb_ref[h]), hbh)
                    dv_ref[rows, hs] = dv.astype(dv_ref.dtype)
                    sdp = s * dpm
                    hr_f = hfh.astype(F32) * rfh.astype(F32)
                    hr_b = hbh.astype(F32) * rbh.astype(F32)
                    da = (_rowsum128(sdp * dda_ref[h]) + _rowsum128((pos + 1.0) * qf * dq_if)
                          + _rowsum128((c - 1.0 - pos) * kf * dk_sf) + cdec_ref[h:h + 1, :] * _rowsum128(hr_f))
                    db = (_rowsum128(sdp * ddb_ref[h]) + _rowsum128((c - pos) * qf * dq_ib)
                          + _rowsum128(pos * kf * dk_sb) + cdec_ref[4 + h:5 + h, :] * _rowsum128(hr_b))
                    dlg_ref[h:h + 1, :] += da
                    dlg_ref[4 + h:5 + h, :] += db
                dq_ref[rows, sl] = dq_acc
                dk_ref[rows, sl] = dk_acc
            return carry

        lax.fori_loop(0, grp, chunk, 0)

    st = pl.BlockSpec((grp, RET_HEADS, 128, 128), lambda n: (n, 0, 0, 0))
    wsp = pl.BlockSpec((RET_HEADS, c, 128), lambda n: (0, 0, 0))
    dsp = pl.BlockSpec((RET_HEADS, c, c), lambda n: (0, 0, 0))
    x256 = pl.BlockSpec((grp * c, 256), lambda n: (n, 0))
    x512 = pl.BlockSpec((grp * c, 512), lambda n: (n, 0))
    cdec = tb["dec_fb"] * float(c)
    return pl.pallas_call(
        body, grid=(seq // (grp * c),), name="ret_bwd",
        in_specs=[x256, x256, x512, x512, st, st, st, st, dsp, dsp, dsp, wsp, wsp, wsp, wsp,
                  pl.BlockSpec((8, 128), lambda n: (0, 0))],
        out_specs=[x256, x256, x512, pl.BlockSpec((8, 128), lambda n: (0, 0))],
        out_shape=[jax.ShapeDtypeStruct((seq, 256), F32), jax.ShapeDtypeStruct((seq, 256), F32),
                   jax.ShapeDtypeStruct((seq, 512), MXU_DTYPE), jax.ShapeDtypeStruct((8, 128), F32)],
        compiler_params=_cp("arbitrary"))(
            rq, rk, rv, do, rf, rb, hf, hb, tb["dmat"], tb["dda"], tb["ddb"], tb["wqf"], tb["wqb"], tb["wkf"], tb["wkb"], cdec)


def _tail(x, tgt, o_att, o_ret, p, mod, g_post, gn_g, wpt, wout):
    seq = x.shape[0]
    bs = 256
    nb = seq // bs

    def body(x_ref, t_ref, oa_ref, or_ref, za_ref, zr_ref, gl_ref, mod_ref, gp_ref, gn_ref, wpt_ref, wout_ref,
             dout_ref, dza_ref, dzr_ref, dgl_ref, doa_ref, dor_ref, del_ref, gout_ref, gpt_ref,
             dgate_ref, dgpost_ref, dgn_ref, loss_ref, gout_acc, gpt_acc):
        i = pl.program_id(0)

        @pl.when(i == 0)
        def _():
            gout_acc[...] = jnp.zeros_like(gout_acc)
            gpt_acc[...] = jnp.zeros_like(gpt_acc)
            dgate_ref[...] = jnp.zeros_like(dgate_ref)
            dgpost_ref[...] = jnp.zeros_like(dgpost_ref)
            dgn_ref[...] = jnp.zeros_like(dgn_ref)
            loss_ref[...] = jnp.zeros_like(loss_ref)

        oa, za, zr = oa_ref[...], za_ref[...], zr_ref[...]
        sga, sgr = _sigmoid(za), _sigmoid(zr)
        sza, szr = za * sga, zr * sgr
        ya_b = _mx(oa * sza)
        ons, rstds = [], []
        for h in range(RET_HEADS):
            oh = or_ref[:, 128 * h:128 * (h + 1)]
            xc = oh - jnp.mean(oh, axis=-1, keepdims=True)
            rstd = lax.rsqrt(jnp.mean(xc * xc, axis=-1, keepdims=True) + EPS)
            ons.append(xc * rstd)
            rstds.append(rstd)
        on = jnp.concatenate(ons, axis=1)
        yn = on * gn_ref[...]
        yr_b = _mx(yn * szr)
        wpa_t, wpr_t = wpt_ref[:, 0:512], wpt_ref[:, 512:1024]
        a_att = _nt(ya_b, wpa_t)
        a_ret = _nt(yr_b, wpr_t)
        gts = _sigmoid(gl_ref[...])
        g_att, g_ret = gts[:, 0:D_MODEL], gts[:, D_MODEL:2 * D_MODEL]
        merged_b = _mx(g_att * a_att + g_ret * a_ret)
        u = _nn(merged_b, wout_ref[...])
        r = lax.rsqrt(jnp.mean(u * u, axis=-1, keepdims=True) + EPS)
        un = u * r
        gpost = gp_ref[...]
        y = un * gpost
        gate = mod_ref[:, 2 * D_MODEL:3 * D_MODEL]
        err = x_ref[...] + gate * y - t_ref[...]
        loss_ref[...] += (0.5 / D_MODEL) * _rowsum128(err * err)
        dout = err * (1.0 / D_MODEL)
        dout_ref[...] = dout
        dgate_ref[...] += jnp.sum(dout * y, axis=0, keepdims=True)
        dy = dout * gate
        dgpost_ref[...] += jnp.sum(dy * un, axis=0, keepdims=True)
        dun = dy * gpost
        du_b = _mx(r * (dun - un * jnp.mean(dun * un, axis=-1, keepdims=True)))
        dmerged = _nt(du_b, wout_ref[...])
        gout_acc[...] += _tn(merged_b, du_b)
        d_att, d_ret = dmerged * g_att, dmerged * g_ret
        dgl_ref[:, 0:D_MODEL] = (d_att * a_att * (1.0 - g_att)).astype(dgl_ref.dtype)
        dgl_ref[:, D_MODEL:2 * D_MODEL] = (d_ret * a_ret * (1.0 - g_ret)).astype(dgl_ref.dtype)
        d_att_b, d_ret_b = _mx(d_att), _mx(d_ret)
        dya = _nn(d_att_b, wpa_t)
        dyr = _nn(d_ret_b, wpr_t)
        gpt_acc[:, 0:512] += _tn(d_att_b, ya_b)
        gpt_acc[:, 512:1024] += _tn(d_ret_b, yr_b)
        dza_ref[...] = (dya * oa * (sga * (1.0 + za * (1.0 - sga)))).astype(dza_ref.dtype)
        doa = dya * sza
        doa_ref[...] = doa.astype(doa_ref.dtype)
        dt = jnp.transpose(doa * oa)
        del_ref[...] = jnp.sum(dt.reshape(8, HEAD_DIM, bs), axis=1)
        dzr_ref[...] = (dyr * yn * (sgr * (1.0 + zr * (1.0 - sgr)))).astype(dzr_ref.dtype)
        dyn = dyr * szr
        dgn_ref[...] += jnp.sum(dyn * on, axis=0, keepdims=True)
        don = dyn * gn_ref[...]
        for h in range(RET_HEADS):
            hs = slice(128 * h, 128 * (h + 1))
            dh, oh = don[:, hs], ons[h]
            doh = rstds[h] * (dh - jnp.mean(dh, axis=-1, keepdims=True) - oh * jnp.mean(dh * oh, axis=-1, keepdims=True))
            dor_ref[:, hs] = doh.astype(dor_ref.dtype)

        @pl.when(i == nb - 1)
        def _():
            gout_ref[...] = gout_acc[...].astype(gout_ref.dtype)
            gpt_ref[...] = gpt_acc[...].astype(gpt_ref.dtype)

    row = lambda w: pl.BlockSpec((bs, w), lambda i: (i, 0))
    el = lambda w, off: pl.BlockSpec((pl.Element(bs), pl.Element(w)), lambda i: (i * bs, off))
    vec = lambda w: pl.BlockSpec((1, w), lambda i: (0, 0))
    full = pl.BlockSpec((D_MODEL, D_MODEL), lambda i: (0, 0))
    sds = jax.ShapeDtypeStruct
    return pl.pallas_call(
        body, grid=(nb,), name="tail",
        in_specs=[row(D_MODEL), row(D_MODEL), row(512), row(512), el(512, ZA_B), el(512, ZR_B), el(2 * D_MODEL, GL_B),
                  vec(3 * D_MODEL), vec(D_MODEL), vec(512), full, full],
        out_specs=[row(D_MODEL), row(512), row(512), row(2 * D_MODEL), row(512), row(512),
                   pl.BlockSpec((8, bs), lambda i: (0, i)), full, full, vec(D_MODEL), vec(D_MODEL), vec(512), vec(128)],
        out_shape=[sds((seq, D_MODEL), F32), sds((seq, 512), MXU_DTYPE), sds((seq, 512), MXU_DTYPE),
                   sds((seq, 2 * D_MODEL), MXU_DTYPE), sds((seq, 512), MXU_DTYPE), sds((seq, 512), MXU_DTYPE),
                   sds((8, seq), F32), sds((D_MODEL, D_MODEL), MXU_DTYPE), sds((D_MODEL, D_MODEL), MXU_DTYPE),
                   sds((1, D_MODEL), F32), sds((1, D_MODEL), F32), sds((1, 512), F32), sds((1, 128), F32)],
        scratch_shapes=[pltpu.VMEM((D_MODEL, D_MODEL), F32), pltpu.VMEM((D_MODEL, D_MODEL), F32)],
        compiler_params=_cp("arbitrary"))(x, tgt, o_att, o_ret, p, p, p, mod, g_post, gn_g, wpt, wout)


def _assemble(p, cos, sin, qg2, kg2, dqt, dkp, dvp, dza, drq, drk, drv, dzr, dgl):
    seq = p.shape[0]
    bs = 512

    def body(p_ref, cos_ref, sin_ref, qg_ref, kg_ref, dqt_ref, dkp_ref, dvp_ref, dza_ref, drq_ref, drk_ref, drv_ref,
             dzr_ref, dgl_ref, dp_ref, dqg_ref, dkg_ref):
        @pl.when(pl.program_id(0) == 0)
        def _():
            dqg_ref[...] = jnp.zeros_like(dqg_ref)
            dkg_ref[...] = jnp.zeros_like(dkg_ref)

        m = _blockdiag64()
        cs, sn = cos_ref[...], sin_ref[...]
        lo = _lane((bs, 128)) < 64

        def norm_bwd(raw, dn, g, dg_ref):
            r = lax.rsqrt(_seg64(raw * raw, m) * (1.0 / 64) + EPS)
            xn = raw * r
            dg_ref[...] += jnp.sum(dn * xn, axis=0, keepdims=True)
            dxn = dn * g
            return r * (dxn - xn * (_seg64(dxn * xn, m) * (1.0 / 64)))

        for pr in range(4):
            sl = slice(128 * pr, 128 * (pr + 1))
            dn = _rope_t(dqt_ref[:, sl] * 0.125, cs, sn)
            dp_ref[:, QA + 128 * pr:QA + 128 * (pr + 1)] = norm_bwd(p_ref[:, sl], dn, qg_ref[...], dqg_ref).astype(dp_ref.dtype)
        fold = lambda a: a + pltpu.roll(a, 64, 1)
        dk = jnp.where(lo, fold(dkp_ref[0]), fold(dkp_ref[1]))
        dp_ref[:, KA:KA + 128] = norm_bwd(p_ref[:, KA:KA + 128], _rope_t(dk, cs, sn), kg_ref[...], dkg_ref).astype(dp_ref.dtype)
        dp_ref[:, VA:VA + 128] = jnp.where(lo, fold(dvp_ref[0]), fold(dvp_ref[1])).astype(dp_ref.dtype)
        dp_ref[:, ZA:ZA + 512] = dza_ref[...]
        for pr in range(2):
            sl = slice(128 * pr, 128 * (pr + 1))
            dp_ref[:, QR + 128 * pr:QR + 128 * (pr + 1)] = _rope_t(drq_ref[:, sl], cs, sn).astype(dp_ref.dtype)
            dp_ref[:, KR + 128 * pr:KR + 128 * (pr + 1)] = _rope_t(drk_ref[:, sl] * 0.125, cs, sn).astype(dp_ref.dtype)
        dp_ref[:, VR:VR + 512] = drv_ref[...]
        dp_ref[:, ZR:ZR + 512] = dzr_ref[...]
        dp_ref[:, GL:GL + 2 * D_MODEL] = dgl_ref[...]

    row = lambda w: pl.BlockSpec((bs, w), lambda i: (i, 0))
    gsp = pl.BlockSpec((1, 128), lambda i: (0, 0))
    dup = pl.BlockSpec((2, bs, 128), lambda i: (0, i, 0))
    return pl.pallas_call(
        body, grid=(seq // bs,), name="assemble_dp",
        in_specs=[row(768), row(128), row(128), gsp, gsp, row(512), dup, dup, row(512), row(256), row(256), row(512),
                  row(512), row(2 * D_MODEL)],
        out_specs=[row(IN_WIDTH), gsp, gsp],
        out_shape=[jax.ShapeDtypeStruct((seq, IN_WIDTH), MXU_DTYPE), jax.ShapeDtypeStruct((1, 128), F32),
                   jax.ShapeDtypeStruct((1, 128), F32)],
        compiler_params=_cp("arbitrary"))(p, cos, sin, qg2, kg2, dqt, dkp, dvp, dza, drq, drk, drv, dzr, dgl)


def _local_step(x, tgt, mod, g_pre, qn_g, kn_g, w_dec_f, w_dec_b, gn_g, g_post, w_attn, rest_start, rest_wait, send=None):
    send = send or (lambda name, arrays: None)
    seq = x.shape[0]
    cos, sin = _rope_tables(seq)
    qg2, kg2 = jnp.tile(qn_g, (1, 2)), jnp.tile(kn_g, (1, 2))
    lg_f = jax.nn.log_sigmoid(w_dec_f[0])
    lg_b = jax.nn.log_sigmoid(w_dec_b[0])
    tb = _ret_tables(lg_f, lg_b, RET_CHUNK)

    h, p_a, qt, kd, vd = _attn_inputs(x, mod, g_pre, w_attn, cos, sin, qg2, kg2)
    o_att, lse = _attn_fwd(qt, kd, vd, dep=rest_start(qt))
    win_t, wpt, wout = rest_wait(o_att)
    p_b, rq, rk, rv = _in_proj_rest(h, win_t, cos, sin)
    rf, rb = _ret_states(rk, rv, tb["wkf"], tb["wkb"], tb["dec_fb"], "ret_states_fwd")
    o_ret = _ret_fwd(rq, rk, rv, rf, rb, tb)
    (dout, dza, dzr, dgl, do_att, do_ret, delta, g_out, g_pt, dgate, dgpost, dgn, loss_l) = _tail(
        x, tgt, o_att, o_ret, p_b, mod, g_post, gn_g, wpt, wout)
    dep = send("early", (g_pt, g_out))
    dqt, dkp, dvp = _attn_bwd(qt, kd, vd, do_att, lse, delta, dep=dep)
    hb, hf = _ret_states(rq, do_ret, tb["wqb"], tb["wqf"], tb["dec_bf"], "ret_states_bwd")
    drq, drk, drv, dlg = _ret_bwd(rq, rk, rv, do_ret, rf, rb, hf, hb, tb)
    dp, dqg, dkg = _assemble(p_a, cos, sin, qg2, kg2, dqt, dkp, dvp, dza, drq, drk, drv, dzr, dgl)
    g_in_t = _matmul(dp, h, ta=True, tb=False, tm=256, tn=D_MODEL, out_dtype=MXU_DTYPE, name="in_proj_dw")
    dep = send("late", (g_in_t,))
    grad_x, dshift, dscale, dgpre = _in_proj_dx(x, dp, win_t, dout, mod, g_pre, dep=dep)

    dlg = jnp.sum(dlg, axis=1)
    small = dict(
        dmod=jnp.concatenate([dshift, dscale, dgate], axis=1),
        g_pre=dgpre, g_post=dgpost, gn_g=dgn,
        qn_g=dqg[:, :64] + dqg[:, 64:], kn_g=dkg[:, :64] + dkg[:, 64:],
        w_dec_f=(dlg[0:4] * jax.nn.sigmoid(-w_dec_f[0]))[None], w_dec_b=(dlg[4:8] * jax.nn.sigmoid(-w_dec_b[0]))[None],
        loss=jnp.sum(loss_l, axis=1, keepdims=True))
    return grad_x, g_in_t, g_pt, g_out, small


N_DEV = 8
N_CHIP = 4
HBM_SPEC = pl.BlockSpec(memory_space=pl.ANY)
VMEM_SPEC = pl.BlockSpec(memory_space=pltpu.VMEM)
SHARD_ROWS = (IN_WIDTH // N_CHIP, D_MODEL // N_CHIP, D_MODEL // N_CHIP)


def _coords():
    return lax.axis_index("x"), lax.axis_index("y"), lax.axis_index("c")


def _peer(k):
    x, y, c = _coords()
    return (1 - x if k & 4 else x, 1 - y if k & 2 else y, 1 - c if k & 1 else c)


def _dev_index(p):
    return 4 * p[0] + 2 * p[1] + p[2]


def _rows(ref, start, size):
    return ref.at[pl.ds(pl.multiple_of(start, 16), size), :]


def _remote(src, dst, ssem, rsem, dev):
    return pltpu.make_async_remote_copy(src_ref=src, dst_ref=dst, send_sem=ssem, recv_sem=rsem, device_id=dev,
                                        device_id_type=MESH)


ATTN_CHUNK = 96


def _mod_and_attn_rows(c, w_ada_s, b4, win_s):
    ncol = w_ada_s.shape[1]
    half = ATTN_COLS // 2
    offs = list(range(0, half, ATTN_CHUNK))
    nq = len(offs)
    rows = lambda ref, cc, off: ref.at[pl.ds(pl.multiple_of(cc * half + off, 16), ATTN_CHUNK), :]

    def body(c_ref, w_ref, b_ref, src, cs_ref, mod_ref, out, modsh, modall, ssem, rsem, lsem, asem, bsem, fsem, gsem, hsem):
        x, y, cc = _coords()
        me = _dev_index((x, y, cc))
        chip = me // 2
        sib = (x, y, 1 - cc)
        cs_ref[me] = c_ref[...]
        sends = []
        for k in range(1, N_DEV):
            cp = _remote(c_ref, cs_ref.at[me], ssem.at[k - 1], rsem.at[k - 1], _peer(k))
            cp.start()
            sends.append(cp)
        own = pltpu.make_async_copy(src.at[pl.ds(0, ATTN_COLS), :], out, lsem.at[0])
        bulk = [_remote(rows(src, cc, off), rows(out, cc, off), asem.at[(k2 - 1) * nq + q], bsem.at[q], (k2 // 2, k2 % 2, cc))
                for k2 in (1, 2) for q, off in enumerate(offs)]
        relay_chip = lambda q: 1 + q % 2

        @pl.when(chip == 0)
        def _():
            own.start()
            for cp in bulk:
                cp.start()

        for k in range(1, N_DEV):
            slot = cs_ref.at[_dev_index(_peer(k))]
            _remote(slot, slot, ssem.at[k - 1], rsem.at[k - 1], _peer(k)).wait_recv()
        cs = jnp.concatenate([cs_ref[d] for d in range(N_DEV)], axis=0)
        ms = _nn(cs * _sigmoid(cs), w_ref[...], precision=HIGHEST) + b_ref[pl.ds(chip, 1), :]
        modsh[...] = ms
        modall[chip] = ms
        for k2 in range(1, N_CHIP):
            cp = _remote(modsh, modall.at[chip], ssem.at[6 + k2], rsem.at[6 + k2], _peer(2 * k2))
            cp.start()
            sends.append(cp)

        @pl.when(chip != 0)
        def _():
            passed = []
            relays = [_remote(rows(out, cc, off), rows(out, cc, off), hsem.at[q], bsem.at[q], (1, 1, cc)) for q, off in enumerate(offs)]
            for q, off in enumerate(offs):
                got = rows(out, cc, off)
                _remote(got, got, asem.at[q], bsem.at[q], (0, 0, cc)).wait_recv()
                cp = _remote(got, got, fsem.at[q], gsem.at[q], sib)
                cp.start()
                passed.append(cp)
                pl.when(chip == relay_chip(q))(relays[q].start)
            for q, off in enumerate(offs):
                got = rows(out, 1 - cc, off)
                _remote(got, got, fsem.at[q], gsem.at[q], sib).wait_recv()
            for cp in passed:
                cp.wait_send()
            for q in range(nq):
                pl.when(chip == relay_chip(q))(relays[q].wait_send)

        for k2 in range(1, N_CHIP):
            slot = modall.at[_dev_index(_peer(2 * k2)) // 2]
            _remote(slot, slot, ssem.at[6 + k2], rsem.at[6 + k2], _peer(2 * k2)).wait_recv()
        for jj in range(N_CHIP):
            mod_ref[:, ncol * jj:ncol * (jj + 1)] = modall[jj, pl.ds(me, 1), :]
        for cp in sends:
            cp.wait_send()

        @pl.when(chip == 0)
        def _():
            for cp in bulk:
                cp.wait_send()
            own.wait()

    dma = pltpu.SemaphoreType.DMA
    return pl.pallas_call(
        body, name="mod_and_attn_rows", in_specs=[VMEM_SPEC] * 4, out_specs=[VMEM_SPEC, VMEM_SPEC, HBM_SPEC],
        out_shape=[jax.ShapeDtypeStruct((N_DEV, 1, D_MODEL), F32), jax.ShapeDtypeStruct((1, N_CHIP * ncol), F32),
                   jax.ShapeDtypeStruct((ATTN_COLS, win_s.shape[1]), win_s.dtype)],
        scratch_shapes=[pltpu.VMEM((N_DEV, ncol), F32), pltpu.VMEM((N_CHIP, N_DEV, ncol), F32), dma((10,)), dma((10,)),
                        dma((1,)), dma((2 * nq,)), dma((nq,)), dma((nq,)), dma((nq,)), dma((nq,))],
        compiler_params=_cp())(c, w_ada_s, b4, win_s)


GATHER_CHUNK = 32


def _chunks(kinds, chunk):
    out = []
    for t, kind in enumerate(kinds):
        half = SHARD_ROWS[kind] // 2
        step = chunk if half % chunk == 0 else half
        out += [(t, off, step) for off in range(0, half, step)]
    return out


SEM_SPEC = pl.BlockSpec(memory_space=pltpu.SEMAPHORE)
HBM_ONLY = pl.BlockSpec(memory_space=pltpu.HBM)
DATAFLOW = pltpu.SideEffectType.DATAFLOW_SIDE_EFFECTING


def _place_own(shards):
    nt = len(shards)

    def body(*refs):
        srcs, outs, lsem = refs[:nt], refs[nt:2 * nt], refs[2 * nt]
        x, y, _ = _coords()
        chip = 2 * x + y
        local = [pltpu.make_async_copy(srcs[t], _rows(outs[t], chip * SHARD_ROWS[t], SHARD_ROWS[t]), lsem.at[t]) for t in range(nt)]
        for cp in local:
            cp.start()
        for cp in local:
            cp.wait()

    return pl.pallas_call(
        body, name="place_own_shard", in_specs=[VMEM_SPEC] * nt, out_specs=[HBM_SPEC] * nt,
        out_shape=[jax.ShapeDtypeStruct((N_CHIP * SHARD_ROWS[t], s.shape[1]), s.dtype) for t, s in enumerate(shards)],
        scratch_shapes=[pltpu.SemaphoreType.DMA((nt,))], compiler_params=_cp())(*shards)


def _gather_rest_start(shards, zones, dep):
    n = len(shards)

    def body(*refs):
        srcs, lands = refs[:n], refs[n:2 * n]
        ssem, rsem = refs[2 * n + 1], refs[2 * n + 2]
        token = refs[-1]
        x, y, _ = _coords()
        chip = 2 * x + y
        for k2 in range(1, N_CHIP):
            for t in range(n):
                q = (k2 - 1) * n + t
                _remote(srcs[t], _rows(lands[t], chip * SHARD_ROWS[t], SHARD_ROWS[t]), ssem.at[q], rsem.at[q], _peer(2 * k2)).start()
        token[...] = jnp.zeros_like(token)

    hbm = lambda a: pltpu.with_memory_space_constraint(a, pltpu.HBM)
    dma = pltpu.SemaphoreType.DMA
    outs = pl.pallas_call(
        body, name="gather_rest", in_specs=[HBM_ONLY] * (2 * n) + [HBM_SPEC],
        out_specs=[SEM_SPEC, SEM_SPEC] + [HBM_ONLY] * (2 * n) + [VMEM_SPEC],
        out_shape=[dma((3 * n,)), dma((3 * n,))] + [pltpu.HBM(a.shape, a.dtype) for a in list(shards) + list(zones)]
        + [jax.ShapeDtypeStruct((8, 128), F32)],
        input_output_aliases={i: 2 + i for i in range(2 * n)},
        compiler_params=pltpu.CompilerParams(has_side_effects=DATAFLOW))(*[hbm(a) for a in list(shards) + list(zones)], dep)
    return outs[0], outs[1], outs[2:2 + n], outs[2 + n:2 + 2 * n], outs[-1]


def _gather_rest_wait(started, after):
    ssem, rsem, shards, zones, _ = started
    n = len(shards)

    def body(*refs):
        srcs, lands = refs[:n], refs[n:2 * n]
        ssem_ref, rsem_ref = refs[2 * n], refs[2 * n + 1]
        for k2 in range(1, N_CHIP):
            pchip = _dev_index(_peer(2 * k2)) // 2
            for t in range(n):
                q = (k2 - 1) * n + t
                cp = _remote(srcs[t], _rows(lands[t], pchip * SHARD_ROWS[t], SHARD_ROWS[t]), ssem_ref.at[q], rsem_ref.at[q], _peer(2 * k2))
                cp.wait_send()
                cp.wait_recv()

    outs = pl.pallas_call(
        body, name="gather_rest_wait", in_specs=[HBM_ONLY] * (2 * n) + [SEM_SPEC, SEM_SPEC, HBM_SPEC], out_specs=[HBM_ONLY] * (2 * n),
        out_shape=[pltpu.HBM(a.shape, a.dtype) for a in list(shards) + list(zones)],
        input_output_aliases={i: i for i in range(2 * n)},
        compiler_params=pltpu.CompilerParams(has_side_effects=DATAFLOW))(*shards, *zones, ssem, rsem, after)
    return outs[n:]


def _reduced_by(ref, t, dev):
    return _rows(ref, (dev // 2) * SHARD_ROWS[t] + (dev % 2) * (SHARD_ROWS[t] // 2), SHARD_ROWS[t] // 2)


def _scatter_start(grads, kinds, name):
    n = len(grads)

    def body(*refs):
        srcs, lands = refs[:n], refs[n:2 * n]
        ssem, rsem = refs[2 * n], refs[2 * n + 1]
        token = refs[-1]
        me = _dev_index(_coords())
        for k in range(1, N_DEV):
            for i, t in enumerate(kinds):
                q = (k - 1) * n + i
                _remote(_reduced_by(srcs[i], t, _dev_index(_peer(k))), lands[i].at[me], ssem.at[q], rsem.at[q], _peer(k)).start()
        token[...] = jnp.zeros_like(token)

    zones = [lax.empty((N_DEV, SHARD_ROWS[t] // 2, g.shape[1]), g.dtype) for g, t in zip(grads, kinds)]
    hbm = lambda a: pltpu.with_memory_space_constraint(a, pltpu.HBM)
    dma = pltpu.SemaphoreType.DMA
    outs = pl.pallas_call(
        body, name=name, in_specs=[HBM_ONLY] * (2 * n), out_specs=[SEM_SPEC, SEM_SPEC] + [HBM_ONLY] * (2 * n) + [VMEM_SPEC],
        out_shape=[dma((7 * n,)), dma((7 * n,))] + [pltpu.HBM(a.shape, a.dtype) for a in list(grads) + zones]
        + [jax.ShapeDtypeStruct((8, 128), F32)],
        input_output_aliases={i: 2 + i for i in range(2 * n)},
        compiler_params=pltpu.CompilerParams(has_side_effects=DATAFLOW))(*[hbm(a) for a in list(grads) + zones])
    return outs[0], outs[1], outs[2:2 + n], outs[2 + n:2 + 2 * n], outs[-1]


def _scatter_wait(started, kinds, after, name):
    ssem, rsem, grads, zones, _ = started
    n = len(grads)

    def body(*refs):
        srcs, lands = refs[:n], refs[n:2 * n]
        ssem_ref, rsem_ref = refs[2 * n], refs[2 * n + 1]
        for k in range(1, N_DEV):
            peer = _dev_index(_peer(k))
            for i, t in enumerate(kinds):
                q = (k - 1) * n + i
                cp = _remote(_reduced_by(srcs[i], t, peer), lands[i].at[peer], ssem_ref.at[q], rsem_ref.at[q], _peer(k))
                cp.wait_send()
                cp.wait_recv()

    outs = pl.pallas_call(
        body, name=name, in_specs=[HBM_ONLY] * (2 * n) + [SEM_SPEC, SEM_SPEC, HBM_SPEC], out_specs=[HBM_ONLY] * (2 * n),
        out_shape=[pltpu.HBM(a.shape, a.dtype) for a in list(grads) + list(zones)],
        input_output_aliases={i: i for i in range(2 * n)},
        compiler_params=pltpu.CompilerParams(has_side_effects=DATAFLOW))(*grads, *zones, ssem, rsem, after)
    return outs[:n], outs[n:]


def _grad_finish(grads, zones, kinds, name):
    nt = len(grads)
    pieces = _chunks(kinds, GATHER_CHUNK)
    npc = len(pieces)
    shard = [SHARD_ROWS[k] for k in kinds]

    def body(*refs):
        srcs, lands, outs = refs[:nt], refs[nt:2 * nt], refs[2 * nt:3 * nt]
        slots, sums = refs[3 * nt:4 * nt], refs[4 * nt:5 * nt]
        lsem, osem, xsem, ysem = refs[5 * nt:]
        x, y, c = _coords()
        me = _dev_index((x, y, c))
        sib = (x, y, 1 - c)
        loads = [pltpu.make_async_copy(_reduced_by(srcs[t], kinds[t], me), slots[t].at[me], lsem.at[t]) for t in range(nt)]
        for k in range(1, N_DEV):
            peer = _dev_index(_peer(k))
            loads += [pltpu.make_async_copy(lands[t].at[peer], slots[t].at[peer], lsem.at[k * nt + t]) for t in range(nt)]
        for cp in loads:
            cp.start()
        for cp in loads:
            cp.wait()
        sends = []
        place = lambda t, cc, off, n: _rows(outs[t], cc * (shard[t] // 2) + off, n)
        stores = []
        for q, (t, off, n) in enumerate(pieces):
            r = pl.ds(off, n)
            acc = slots[t][0, r, :].astype(F32)
            for d in range(1, N_DEV):
                acc = acc + slots[t][d, r, :].astype(F32)
            sums[t][r, :] = acc
            keep = pltpu.make_async_copy(sums[t].at[r, :], place(t, c, off, n), osem.at[q])
            give = _remote(sums[t].at[r, :], place(t, c, off, n), xsem.at[q], ysem.at[q], sib)
            keep.start()
            give.start()
            stores.append(keep)
            sends.append(give)
        for q, (t, off, n) in enumerate(pieces):
            got = place(t, 1 - c, off, n)
            _remote(got, got, xsem.at[q], ysem.at[q], sib).wait_recv()
        for cp in sends:
            cp.wait_send()
        for cp in stores:
            cp.wait()

    dma = pltpu.SemaphoreType.DMA
    return pl.pallas_call(
        body, name=name, in_specs=[HBM_SPEC] * (2 * nt), out_specs=[HBM_SPEC] * nt,
        out_shape=[jax.ShapeDtypeStruct((shard[t], g.shape[1]), F32) for t, g in enumerate(grads)],
        scratch_shapes=([pltpu.VMEM((N_DEV, shard[t] // 2, g.shape[1]), g.dtype) for t, g in enumerate(grads)]
                        + [pltpu.VMEM((shard[t] // 2, g.shape[1]), F32) for t, g in enumerate(grads)]
                        + [dma((N_DEV * nt,)), dma((npc,)), dma((npc,)), dma((npc,))]),
        compiler_params=_cp())(*grads, *zones)


def _adam_math(w, g, m, v):
    m = ADAM_B1 * m + (1.0 - ADAM_B1) * g
    v = ADAM_B2 * v + (1.0 - ADAM_B2) * (g * g)
    m_hat = m / (1.0 - ADAM_B1 ** ADAM_STEP)
    v_hat = v / (1.0 - ADAM_B2 ** ADAM_STEP)
    return -ADAM_LR * (m_hat / (jnp.sqrt(v_hat) + ADAM_EPS) + ADAM_WD * w), m, v


def _adamw(w, g, m, v, rb, name):
    rows, cols = w.shape

    def body(w_ref, g_ref, m_ref, v_ref, d_ref, nm_ref, nv_ref):
        d_ref[...], nm_ref[...], nv_ref[...] = _adam_math(w_ref[...], g_ref[...], m_ref[...], v_ref[...])

    spec = pl.BlockSpec((rb, cols), lambda i: (i, 0))
    return pl.pallas_call(body, grid=(rows // rb,), name=name, in_specs=[spec] * 4, out_specs=[spec] * 3,
                          out_shape=[jax.ShapeDtypeStruct(w.shape, F32)] * 3, compiler_params=_cp("parallel"))(w, g, m, v)


PACK = (("b_ada", 3 * D_MODEL), ("g_pre", D_MODEL), ("g_post", D_MODEL), ("gn_g", 512), ("qn_g", 64), ("kn_g", 64),
        ("w_dec_f", 4), ("w_dec_b", 4), ("loss", 1))
PACK_OFFSETS = {}
PACK_WIDTH = 0
for _name, _n in PACK:
    PACK_OFFSETS[_name] = PACK_WIDTH
    PACK_WIDTH += -(-_n // 128) * 128
SMALL_PARAMS = tuple(name for name, _ in PACK if name != "loss")


def _pack(parts):
    cols = []
    for name, n in PACK:
        pad = -(-n // 128) * 128 - n
        cols.append(parts[name].reshape(1, n).astype(F32))
        if pad:
            cols.append(jnp.zeros((1, pad), F32))
    return jnp.concatenate(cols, axis=1)


def _small_reduce(vec, cs, deps):
    ncol = 3 * D_MODEL // N_CHIP

    def body(vec_ref, cs_ref, *rest):
        tot_ref, gwa_ref, gat, ssem, rsem = rest[-5:]
        me = _dev_index(_coords())
        chip = me // 2
        gat[me] = vec_ref[...]
        sends = []
        for k in range(1, N_DEV):
            cp = _remote(vec_ref, gat.at[me], ssem.at[k - 1], rsem.at[k - 1], _peer(k))
            cp.start()
            sends.append(cp)
        for k in range(1, N_DEV):
            slot = gat.at[_dev_index(_peer(k))]
            _remote(slot, slot, ssem.at[k - 1], rsem.at[k - 1], _peer(k)).wait_recv()
        rows = [gat[d] for d in range(N_DEV)]
        tot = rows[0]
        for d in range(1, N_DEV):
            tot = tot + rows[d]
        tot_ref[...] = tot
        cs = cs_ref[...]
        ca_t = jnp.transpose(jnp.concatenate([cs * _sigmoid(cs), jnp.zeros((128 - N_DEV, D_MODEL), F32)], axis=0))
        dm = jnp.concatenate(rows + [jnp.zeros((128 - N_DEV, PACK_WIDTH), F32)], axis=0)
        for jj in range(N_CHIP):
            @pl.when(chip == jj)
            def _():
                gwa_ref[...] = _nn(ca_t, dm[:, ncol * jj:ncol * (jj + 1)], precision=HIGHEST)
        for cp in sends:
            cp.wait_send()

    return pl.pallas_call(
        body, name="small_reduce", in_specs=[VMEM_SPEC, VMEM_SPEC] + [HBM_SPEC] * len(deps), out_specs=[VMEM_SPEC] * 2,
        out_shape=[jax.ShapeDtypeStruct((1, PACK_WIDTH), F32), jax.ShapeDtypeStruct((D_MODEL, ncol), F32)],
        scratch_shapes=[pltpu.VMEM((N_DEV, 1, PACK_WIDTH), F32), pltpu.SemaphoreType.DMA((7,)), pltpu.SemaphoreType.DMA((7,))],
        compiler_params=_cp())(vec, cs, *deps)


def _small_adamw(tot, given):
    sizes = dict(PACK)
    np_ = len(SMALL_PARAMS)

    def body(*refs):
        tot_ref = refs[0]
        wmv = refs[1:1 + 3 * np_]
        outs = refs[1 + 3 * np_:1 + 7 * np_]
        loss_ref = refs[-1]
        for i, name in enumerate(SMALL_PARAMS):
            off, n = PACK_OFFSETS[name], sizes[name]
            g = tot_ref[:, off:off + n]
            w_ref, m_ref, v_ref = wmv[3 * i:3 * i + 3]
            outs[i][...] = g
            outs[np_ + i][...], outs[2 * np_ + i][...], outs[3 * np_ + i][...] = _adam_math(w_ref[...], g, m_ref[...], v_ref[...])
        loss_ref[...] = tot_ref[:, PACK_OFFSETS["loss"]:PACK_OFFSETS["loss"] + 1]

    flat = [a for name in SMALL_PARAMS for a in given[name]]
    shapes = [jax.ShapeDtypeStruct((1, sizes[name]), F32) for name in SMALL_PARAMS]
    res = pl.pallas_call(body, name="small_adamw", in_specs=[VMEM_SPEC] * (1 + 3 * np_), out_specs=[VMEM_SPEC] * (4 * np_ + 1),
                         out_shape=shapes * 4 + [jax.ShapeDtypeStruct((1, 1), F32)], compiler_params=_cp())(tot, *flat)
    return [dict(zip(SMALL_PARAMS, res[k * np_:(k + 1) * np_])) for k in range(4)], res[-1]


def kernel(x, c, w_ada, b_ada, g_pre, w_in, qn_g, kn_g, w_dec_f, w_dec_b, gn_g, w_pa, w_pr, w_out, g_post, loss_target, m_w_ada, m_b_ada, m_g_pre, m_w_in, m_qn_g, m_kn_g, m_w_dec_f, m_w_dec_b, m_gn_g, m_w_pa, m_w_pr, m_w_out, m_g_post, v_w_ada, v_b_ada, v_g_pre, v_w_in, v_qn_g, v_kn_g, v_w_dec_f, v_w_dec_b, v_gn_g, v_w_pa, v_w_pr, v_w_out, v_g_post):
    shards = (w_in[0].T.astype(MXU_DTYPE), jnp.concatenate([w_pa[0].T, w_pr[0].T], axis=1).astype(MXU_DTYPE),
              w_out[0].astype(MXU_DTYPE))
    cs, mod, w_attn = _mod_and_attn_rows(c, w_ada[0], b_ada.reshape(N_CHIP, 3 * D_MODEL // N_CHIP), shards[0])
    started = {}

    def rest_start(dep):
        started["rest"] = _gather_rest_start(shards, _place_own(shards), dep)
        return started["rest"][-1]

    def rest_wait(after):
        return _gather_rest_wait(started["rest"], after)

    kinds = {"early": (1, 2), "late": (0,)}

    def send(name, arrays):
        started[name] = _scatter_start(arrays, kinds[name], "grad_scatter_" + name)
        return started[name][-1]

    grad_x, _, _, _, small = _local_step(x[0], loss_target[0], mod, g_pre, qn_g, kn_g, w_dec_f, w_dec_b,
                                         gn_g, g_post, w_attn, rest_start, rest_wait, send=send)
    small = dict(small)
    small["b_ada"] = small.pop("dmod")
    given = dict(b_ada=(b_ada, m_b_ada, v_b_ada), g_pre=(g_pre, m_g_pre, v_g_pre), g_post=(g_post, m_g_post, v_g_post),
                 gn_g=(gn_g, m_gn_g, v_gn_g), qn_g=(qn_g, m_qn_g, v_qn_g), kn_g=(kn_g, m_kn_g, v_kn_g),
                 w_dec_f=(w_dec_f, m_w_dec_f, v_w_dec_f), w_dec_b=(w_dec_b, m_w_dec_b, v_w_dec_b))
    grads, deltas, new_m, new_v = {}, {}, {}, {}

    def update(name, w, g, m, v, back):
        d, nm, nv = _adamw(w, g, m, v, w.shape[0] // 4, "adamw_" + name)
        grads[name], deltas[name], new_m[name], new_v[name] = back(g), back(d), back(nm), back(nv)
        return d

    lead = lambda a: a[None]
    (g_pt, g_out), (z_pt, z_out) = _scatter_wait(started["early"], kinds["early"], grad_x, "grad_scatter_early_wait")
    r_pt, r_out = _grad_finish((g_pt, g_out), (z_pt, z_out), kinds["early"], "grad_finish_early")
    early = (update("w_pa", w_pa[0], r_pt[:, :512].T, m_w_pa[0], v_w_pa[0], lead),
             update("w_pr", w_pr[0], r_pt[:, 512:].T, m_w_pr[0], v_w_pr[0], lead),
             update("w_out", w_out[0], r_out, m_w_out[0], v_w_out[0], lead))
    tot, g_w_ada = _small_reduce(_pack(small), cs.reshape(N_DEV, D_MODEL), early)
    small_parts, loss = _small_adamw(tot, given)
    for dst, part in zip((grads, deltas, new_m, new_v), small_parts):
        dst.update(part)
    last = update("w_ada", w_ada[0], g_w_ada, m_w_ada[0], v_w_ada[0], lead)

    (g_in_t,), (z_in,) = _scatter_wait(started["late"], kinds["late"], last, "grad_scatter_late_wait")
    (r_in,) = _grad_finish((g_in_t,), (z_in,), kinds["late"], "grad_finish_late")
    tr = lambda a: a[0].T
    update("w_in", tr(w_in), r_in, tr(m_w_in), tr(v_w_in), lambda a: a.T[None])
    order = ("w_ada", "b_ada", "g_pre", "w_in", "qn_g", "kn_g", "w_dec_f", "w_dec_b", "gn_g", "w_pa", "w_pr", "w_out", "g_post")
    return (loss[0, 0], grad_x[None], *[grads[n] for n in order], *[deltas[n] for n in order],
            *[new_m[n] for n in order], *[new_v[n] for n in order])
```

```python
import functools

import jax
import jax.numpy as jnp
import numpy as np
from jax import lax
from jax.experimental import pallas as pl
from jax.experimental.pallas import tpu as pltpu

F32 = jnp.float32
BF16 = jnp.bfloat16
MXU_DTYPE = jnp.bfloat16

D_MODEL = 1024
GRID_W = 64
HEAD_DIM = 64
ROPE_THETA = 10000.0
EPS = 1e-6
RET_HEADS = 4
RET_CHUNK = 256
RET_GROUP = 4
IN_WIDTH = 4864
QA, KA, VA, ZA, QR, KR, VR, ZR, GL = 0, 512, 640, 768, 1280, 1536, 1792, 2304, 2816
ATTN_COLS = ZA
ZA_B, QR_B, KR_B, VR_B, ZR_B, GL_B = (c - ATTN_COLS for c in (ZA, QR, KR, VR, ZR, GL))

ADAM_LR, ADAM_B1, ADAM_B2, ADAM_EPS, ADAM_WD, ADAM_STEP = 0.001, 0.9, 0.999, 1e-08, 0.01, 10

VMEM_LIMIT = 56 * 1024 * 1024
MESH = pl.DeviceIdType.MESH
HIGHEST = lax.Precision.HIGHEST
LOG2E = 1.4426950408889634
LN2 = 0.6931471805599453


def _cp(*sem, **kw):
    if sem:
        kw["dimension_semantics"] = sem
    return pltpu.CompilerParams(vmem_limit_bytes=VMEM_LIMIT, **kw)


def _nn(a, b, **kw):
    return lax.dot_general(a, b, (((1,), (0,)), ((), ())), preferred_element_type=F32, **kw)


def _nt(a, b):
    return lax.dot_general(a, b, (((1,), (1,)), ((), ())), preferred_element_type=F32)


def _tn(a, b):
    return lax.dot_general(a, b, (((0,), (0,)), ((), ())), preferred_element_type=F32)


def _mx(x):
    return x.astype(MXU_DTYPE)


def _lane(shape):
    return lax.broadcasted_iota(jnp.int32, shape, 1)


def _sigmoid(z):
    return 1.0 / (1.0 + jnp.exp(-z))


def _rowsum128(x):
    s = jnp.sum(x, axis=0, keepdims=True)
    out = s[:, 0:128]
    for k in range(1, x.shape[1] // 128):
        out = out + s[:, 128 * k:128 * (k + 1)]
    return out


def _swap16(x):
    return jnp.where((_lane(x.shape) & 16) == 0, pltpu.roll(x, 112, 1), pltpu.roll(x, 16, 1))


def _rope(x, cos, sin):
    return x * cos + _swap16(x) * sin


def _rope_t(d, cos, sin):
    return d * cos + _swap16(d * sin)


def _blockdiag64():
    r = lax.broadcasted_iota(jnp.int32, (128, 128), 0) // 64
    c = lax.broadcasted_iota(jnp.int32, (128, 128), 1) // 64
    return (r == c).astype(BF16)


def _seg64(x, m):
    hi = x.astype(BF16)
    lo = (x - hi.astype(F32)).astype(BF16)
    return _nn(hi, m) + _nn(lo, m)


def _rope_tables(seq):
    t = np.arange(seq)
    row = (t // GRID_W).astype(np.float32)
    col = (t % GRID_W).astype(np.float32)
    half = HEAD_DIM // 2
    inv_freq = (np.float32(ROPE_THETA) ** (-np.arange(0, half, 2, dtype=np.float32) / np.float32(half))).astype(np.float32)
    ar = (row[:, None] * inv_freq[None, :]).astype(np.float32).astype(np.float64)
    ac = (col[:, None] * inv_freq[None, :]).astype(np.float32).astype(np.float64)
    cr, sr, cc, sc = np.cos(ar), np.sin(ar), np.cos(ac), np.sin(ac)
    cos64 = np.concatenate([cr, cr, cc, cc], axis=1)
    sin64 = np.concatenate([-sr, sr, -sc, sc], axis=1)
    return jnp.asarray(np.tile(cos64, (1, 2)), F32), jnp.asarray(np.tile(sin64, (1, 2)), F32)


def _attn_inputs(x, mod, g_pre, w_attn, cos, sin, qg2, kg2):
    seq = x.shape[0]
    bs = min(1024, seq)

    def body(x_ref, mod_ref, g_ref, w_ref, cos_ref, sin_ref, qg_ref, kg_ref, h_ref, pa_ref, qt_ref, kd_ref, vd_ref):
        xv = x_ref[...]
        r = lax.rsqrt(jnp.mean(xv * xv, axis=-1, keepdims=True) + EPS)
        shift = mod_ref[:, 0:D_MODEL]
        scale = mod_ref[:, D_MODEL:2 * D_MODEL]
        hb = ((xv * r) * g_ref[...] * (1.0 + scale) + shift).astype(h_ref.dtype)
        h_ref[...] = hb
        p = _nt(hb, w_ref[...])
        pa_ref[...] = p
        m = _blockdiag64()
        cs, sn = cos_ref[...], sin_ref[...]
        lo = _lane((bs, 128)) < 64
        for pr in range(4):
            q = p[:, QA + 128 * pr:QA + 128 * (pr + 1)]
            r = lax.rsqrt(_seg64(q * q, m) * (1.0 / 64) + EPS)
            qt_ref[:, 128 * pr:128 * (pr + 1)] = (_rope(q * r * qg_ref[...], cs, sn) * (0.125 * LOG2E)).astype(qt_ref.dtype)
        k = p[:, KA:KA + 128]
        r = lax.rsqrt(_seg64(k * k, m) * (1.0 / 64) + EPS)
        kr = _rope(k * r * kg_ref[...], cs, sn)
        ksw = pltpu.roll(kr, 64, 1)
        kd_ref[0] = jnp.where(lo, kr, ksw).astype(kd_ref.dtype)
        kd_ref[1] = jnp.where(lo, ksw, kr).astype(kd_ref.dtype)
        v = p[:, VA:VA + 128]
        vsw = pltpu.roll(v, 64, 1)
        vd_ref[0] = jnp.where(lo, v, vsw).astype(vd_ref.dtype)
        vd_ref[1] = jnp.where(lo, vsw, v).astype(vd_ref.dtype)

    row = lambda w: pl.BlockSpec((bs, w), lambda i: (i, 0))
    fix = lambda a, b: pl.BlockSpec((a, b), lambda i: (0, 0))
    dup = pl.BlockSpec((2, bs, 128), lambda i: (0, i, 0))
    sds = jax.ShapeDtypeStruct
    return pl.pallas_call(
        body, grid=(seq // bs,), name="attn_inputs",
        in_specs=[row(D_MODEL), fix(1, 3 * D_MODEL), fix(1, D_MODEL), fix(ATTN_COLS, D_MODEL), row(128), row(128), fix(1, 128), fix(1, 128)],
        out_specs=[row(D_MODEL), row(ATTN_COLS), row(512), dup, dup],
        out_shape=[sds((seq, D_MODEL), MXU_DTYPE), sds((seq, ATTN_COLS), F32), sds((seq, 512), MXU_DTYPE),
                   sds((2, seq, 128), MXU_DTYPE), sds((2, seq, 128), MXU_DTYPE)],
        compiler_params=_cp("parallel"))(x, mod, g_pre, w_attn, cos, sin, qg2, kg2)


def _in_proj_dx(x, dp, win_t, dout, mod, g_pre, dep=None):
    seq = x.shape[0]
    bs = 512
    deps, dep_specs = _dep_args(dep, 1)

    def body(x_ref, dp_ref, w_ref, dout_ref, mod_ref, g_ref, *rest):
        gx_ref, dshift_ref, dscale_ref, dg_ref = rest[-4:]

        @pl.when(pl.program_id(0) == 0)
        def _():
            dshift_ref[...] = jnp.zeros_like(dshift_ref)
            dscale_ref[...] = jnp.zeros_like(dscale_ref)
            dg_ref[...] = jnp.zeros_like(dg_ref)

        xv = x_ref[...]
        dh = _nn(dp_ref[...], w_ref[...])
        g = g_ref[...]
        r = lax.rsqrt(jnp.mean(xv * xv, axis=-1, keepdims=True) + EPS)
        xn = xv * r
        scale = mod_ref[:, D_MODEL:2 * D_MODEL]
        dshift_ref[...] += jnp.sum(dh, axis=0, keepdims=True)
        dscale_ref[...] += jnp.sum(dh * (xn * g), axis=0, keepdims=True)
        da = dh * (1.0 + scale)
        dg_ref[...] += jnp.sum(da * xn, axis=0, keepdims=True)
        dxn = da * g
        dx = r * (dxn - xn * jnp.mean(dxn * xn, axis=-1, keepdims=True))
        gx_ref[...] = dout_ref[...] + dx

    row = pl.BlockSpec((bs, D_MODEL), lambda i: (i, 0))
    vec = pl.BlockSpec((1, D_MODEL), lambda i: (0, 0))
    return pl.pallas_call(
        body, grid=(seq // bs,), name="in_proj_dx",
        in_specs=[row, pl.BlockSpec((bs, IN_WIDTH), lambda i: (i, 0)), pl.BlockSpec((IN_WIDTH, D_MODEL), lambda i: (0, 0)), row,
                  pl.BlockSpec((1, 3 * D_MODEL), lambda i: (0, 0)), vec] + dep_specs,
        out_specs=[row, vec, vec, vec],
        out_shape=[jax.ShapeDtypeStruct((seq, D_MODEL), F32)] + [jax.ShapeDtypeStruct((1, D_MODEL), F32)] * 3,
        compiler_params=_cp("arbitrary"))(x, dp, win_t, dout, mod, g_pre, *deps)


def _dep_args(dep, grid_rank):
    if dep is None:
        return [], []
    return [dep], [pl.BlockSpec(dep.shape, lambda *_: (0,) * dep.ndim)]


def _matmul(a, b, *, ta, tb, tm, tn, out_dtype, name, dep=None):
    kdim = a.shape[0] if ta else a.shape[1]
    m = a.shape[1] if ta else a.shape[0]
    n = b.shape[0] if tb else b.shape[1]
    assert m % tm == 0 and n % tn == 0
    deps, dep_specs = _dep_args(dep, 2)

    def body(a_ref, b_ref, *rest):
        o_ref = rest[-1]
        dims = (((0 if ta else 1,), (1 if tb else 0,)), ((), ()))
        o_ref[...] = lax.dot_general(a_ref[...], b_ref[...], dims, preferred_element_type=F32).astype(o_ref.dtype)

    a_spec = pl.BlockSpec((kdim, tm), lambda j, i: (0, i)) if ta else pl.BlockSpec((tm, kdim), lambda j, i: (i, 0))
    b_spec = pl.BlockSpec((tn, kdim), lambda j, i: (j, 0)) if tb else pl.BlockSpec((kdim, tn), lambda j, i: (0, j))
    return pl.pallas_call(
        body, grid=(n // tn, m // tm), name=name, in_specs=[a_spec, b_spec] + dep_specs,
        out_specs=pl.BlockSpec((tm, tn), lambda j, i: (i, j)),
        out_shape=jax.ShapeDtypeStruct((m, n), out_dtype), compiler_params=_cp("parallel", "parallel"))(a, b, *deps)


def _in_proj_rest(h, win_t, cos, sin):
    seq = h.shape[0]
    tm = min(1024, seq)
    ncols = IN_WIDTH - ATTN_COLS
    tn = ncols // 2
    assert ZR_B <= tn and ATTN_COLS % 128 == 0 and tn % 128 == 0

    def body(h_ref, w_ref, cos_ref, sin_ref, p_ref, rq_ref, rk_ref, rv_ref):
        p = _nt(h_ref[...], w_ref[...])
        p_ref[...] = p

        @pl.when(pl.program_id(1) == 0)
        def _():
            cs, sn = cos_ref[...], sin_ref[...]
            for pr in range(2):
                sl = slice(128 * pr, 128 * (pr + 1))
                rq_ref[:, sl] = _rope(p[:, QR_B + 128 * pr:QR_B + 128 * (pr + 1)], cs, sn).astype(rq_ref.dtype)
                rk_ref[:, sl] = (_rope(p[:, KR_B + 128 * pr:KR_B + 128 * (pr + 1)], cs, sn) * 0.125).astype(rk_ref.dtype)
            rv_ref[...] = p[:, VR_B:VR_B + 512].astype(rv_ref.dtype)

    row = lambda w: pl.BlockSpec((tm, w), lambda i, j: (i, 0))
    sds = jax.ShapeDtypeStruct
    return pl.pallas_call(
        body, grid=(seq // tm, 2), name="in_proj_rest",
        in_specs=[row(D_MODEL), pl.BlockSpec((pl.Element(tn), pl.Element(D_MODEL)),
                                             lambda i, j: (pl.multiple_of(ATTN_COLS + j * tn, 128), 0)), row(128), row(128)],
        out_specs=[pl.BlockSpec((tm, tn), lambda i, j: (i, j)), row(256), row(256), row(512)],
        out_shape=[sds((seq, ncols), F32), sds((seq, 256), MXU_DTYPE), sds((seq, 256), MXU_DTYPE), sds((seq, 512), MXU_DTYPE)],
        compiler_params=_cp("parallel", "arbitrary"))(h, win_t, cos, sin)


def _halves(kd):
    lo = _lane(kd.shape) < 64
    z = jnp.zeros_like(kd)
    return jnp.concatenate([jnp.where(lo, kd, z), jnp.where(lo, z, kd)], axis=0)


def _attn_fwd(qt, kd, vd, dep=None):
    seq = qt.shape[0]
    bq, bk = min(1024, seq), min(1024, seq)
    nk = seq // bk
    deps, dep_specs = _dep_args(dep, 2)

    def body(q_ref, k_ref, v_ref, *rest):
        o_ref, lse_ref, m_scr, l_scr, acc = rest[-5:]
        j = pl.program_id(1)
        m_scr[...] = jnp.full_like(m_scr, -jnp.inf)
        l_scr[...] = jnp.zeros_like(l_scr)
        acc[...] = jnp.zeros_like(acc)
        lo = _lane((bq, 128)) < 64

        def kv_block(n, carry):
            rows = pl.ds(pl.multiple_of(n * bk, bk), bk)
            k2, v2 = _halves(k_ref[rows, :]), _halves(v_ref[rows, :])
            for pr in range(2):
                s2 = _nt(q_ref[:, 128 * pr:128 * (pr + 1)], k2)
                ps, alphas = [], []
                for e in range(2):
                    g = 2 * pr + e
                    s = s2[:, e * bk:(e + 1) * bk]
                    m_prev = m_scr[g]
                    m_next = jnp.maximum(m_prev, jnp.max(s, axis=1, keepdims=True))
                    alpha = jnp.exp2(m_prev - m_next)
                    pe = jnp.exp2(s - jnp.tile(m_next, (1, bk // 128)))
                    l_scr[g] = alpha * l_scr[g] + jnp.sum(pe, axis=1, keepdims=True)
                    m_scr[g] = m_next
                    ps.append(_mx(pe))
                    alphas.append(alpha)
                o2 = _nn(jnp.concatenate(ps, axis=1), v2)
                sl = slice(128 * pr, 128 * (pr + 1))
                acc[:, sl] = acc[:, sl] * jnp.where(lo, alphas[0], alphas[1]) + o2
            return carry

        lax.fori_loop(0, nk, kv_block, 0)
        for pr in range(2):
            sl = slice(128 * pr, 128 * (pr + 1))
            o_ref[:, sl] = acc[:, sl] / jnp.where(lo, l_scr[2 * pr], l_scr[2 * pr + 1])
        for g in range(4):
            lse = jnp.transpose(m_scr[g] + jnp.log2(l_scr[g]))
            lse_ref[pl.ds(4 * j + g, 1), :] = lse[0:1, :]

    return pl.pallas_call(
        body, grid=(seq // bq, 2), name="attn_fwd",
        in_specs=[pl.BlockSpec((bq, 256), lambda i, j: (i, j)), pl.BlockSpec((None, seq, 128), lambda i, j: (j, 0, 0)),
                  pl.BlockSpec((None, seq, 128), lambda i, j: (j, 0, 0))] + dep_specs,
        out_specs=[pl.BlockSpec((bq, 256), lambda i, j: (i, j)), pl.BlockSpec((8, bq), lambda i, j: (0, i))],
        out_shape=[jax.ShapeDtypeStruct((seq, 512), F32), jax.ShapeDtypeStruct((8, seq), F32)],
        scratch_shapes=[pltpu.VMEM((4, bq, 128), F32), pltpu.VMEM((4, bq, 128), F32), pltpu.VMEM((bq, 256), F32)],
        compiler_params=_cp("parallel", "arbitrary"))(qt, kd, vd, *deps)


def _attn_bwd(qt, kd, vd, do, lse, delta, dep=None):
    seq = qt.shape[0]
    bq, bk = min(1024, seq), min(1024, seq)
    nq, nk = seq // bq, seq // bk
    deps, dep_specs = _dep_args(dep, 3)

    def body(q_ref, do_ref, k_ref, v_ref, lse_ref, del_ref, *rest):
        dq_ref, dk_ref, dv_ref = rest[-3:]
        j, i = pl.program_id(0), pl.program_id(1)
        dq_ref[...] = jnp.zeros_like(dq_ref)

        @pl.when(i == 0)
        def _():
            dk_ref[...] = jnp.zeros_like(dk_ref)
            dv_ref[...] = jnp.zeros_like(dv_ref)

        lo = _lane((bk, 128)) < 64
        big = lambda r: jnp.concatenate([jnp.broadcast_to(r[0], (bk, bq)), jnp.broadcast_to(r[1], (bk, bq))], axis=0)

        def kv_block(n, carry):
            rows = pl.ds(pl.multiple_of(n * bk, bk), bk)
            k2, v2 = _halves(k_ref[rows, :]), _halves(v_ref[rows, :])
            for pr in range(2):
                sl = slice(128 * pr, 128 * (pr + 1))
                qp, dop = q_ref[:, sl], do_ref[:, sl]
                s_t = _nt(k2, qp)
                dp_t = _nt(v2, dop)
                ls = [lse_ref[pl.ds(4 * j + 2 * pr + e, 1), :] for e in range(2)]
                dl = [del_ref[pl.ds(4 * j + 2 * pr + e, 1), :] for e in range(2)]
                p_t = jnp.exp2(s_t - big(ls))
                ds_t = _mx(p_t * (dp_t - big(dl)))
                rv = _nn(_mx(p_t), dop)
                rk = _nn(ds_t, qp) * LN2
                dv_ref[rows, :] += jnp.where(lo, rv[:bk], rv[bk:])
                dk_ref[rows, :] += jnp.where(lo, rk[:bk], rk[bk:])
                dq_ref[:, sl] += _tn(ds_t, k2)
            return carry

        lax.fori_loop(0, nk, kv_block, 0)

    return pl.pallas_call(
        body, grid=(2, nq), name="attn_bwd",
        in_specs=[pl.BlockSpec((bq, 256), lambda j, i: (i, j)), pl.BlockSpec((bq, 256), lambda j, i: (i, j)),
                  pl.BlockSpec((None, seq, 128), lambda j, i: (j, 0, 0)), pl.BlockSpec((None, seq, 128), lambda j, i: (j, 0, 0)),
                  pl.BlockSpec((8, bq), lambda j, i: (0, i)), pl.BlockSpec((8, bq), lambda j, i: (0, i))] + dep_specs,
        out_specs=[pl.BlockSpec((bq, 256), lambda j, i: (i, j)), pl.BlockSpec((None, seq, 128), lambda j, i: (j, 0, 0)),
                   pl.BlockSpec((None, seq, 128), lambda j, i: (j, 0, 0))],
        out_shape=[jax.ShapeDtypeStruct((seq, 512), F32), jax.ShapeDtypeStruct((2, seq, 128), F32),
                   jax.ShapeDtypeStruct((2, seq, 128), F32)],
        compiler_params=_cp("arbitrary", "arbitrary"))(qt, do, kd, vd, lse, delta, *deps)


def _ret_tables(lg_f, lg_b, c):
    idx = jnp.arange(c, dtype=F32)
    diff = idx[:, None] - idx[None, :]
    a, b = lg_f[:, None, None], lg_b[:, None, None]
    low, up = (diff >= 0)[None], (diff < 0)[None]
    dmat = jnp.where(low, jnp.exp(a * jnp.maximum(diff, 0.0)[None]), jnp.exp(b * jnp.maximum(-diff, 0.0)[None]))
    dda = jnp.where(low, diff[None] * jnp.exp(a * jnp.maximum(diff, 0.0)[None]), 0.0)
    ddb = jnp.where(up, -diff[None] * jnp.exp(b * jnp.maximum(-diff, 0.0)[None]), 0.0)
    rep = lambda w: jnp.broadcast_to(w[:, :, None], (RET_HEADS, c, 128))
    wqf = rep(jnp.exp(lg_f[:, None] * (idx + 1.0)[None]))
    wqb = rep(jnp.exp(lg_b[:, None] * (c - idx)[None]))
    wkf = rep(jnp.exp(lg_f[:, None] * (c - 1.0 - idx)[None]))
    wkb = rep(jnp.exp(lg_b[:, None] * idx[None]))
    cdf, cdb = jnp.exp(lg_f * c), jnp.exp(lg_b * c)
    dec_fb = jnp.broadcast_to(jnp.concatenate([cdf, cdb])[:, None], (8, 128))
    dec_bf = jnp.broadcast_to(jnp.concatenate([cdb, cdf])[:, None], (8, 128))
    return dict(dmat=dmat, dda=dda, ddb=ddb, wqf=wqf, wqb=wqb, wkf=wkf, wkb=wkb, dec_fb=dec_fb, dec_bf=dec_bf)


def _ret_states(xk, yv, w_fwd, w_rev, dec, name):
    seq = xk.shape[0]
    c = RET_CHUNK
    nc = seq // c
    grp = min(RET_GROUP, nc)
    ns = nc // grp

    def body(xf_ref, yf_ref, xr_ref, yr_ref, wf_ref, wr_ref, dec_ref, sf_ref, sr_ref, st_f, st_r):
        @pl.when(pl.program_id(0) == 0)
        def _():
            st_f[...] = jnp.zeros_like(st_f)
            st_r[...] = jnp.zeros_like(st_r)

        lane = _lane((c, 128))

        def chunk(g, carry):
            for x_ref, y_ref, w_ref, s_ref, st, base, gg in ((xf_ref, yf_ref, wf_ref, sf_ref, st_f, 0, g),
                                                             (xr_ref, yr_ref, wr_ref, sr_ref, st_r, 4, grp - 1 - g)):
                rows = pl.ds(pl.multiple_of(gg * c, c), c)
                for h in range(RET_HEADS):
                    pr, e = h // 2, h % 2
                    half = (lane < 64) if e == 0 else (lane >= 64)
                    s_ref[gg, h] = st[h].astype(s_ref.dtype)
                    xm = jnp.where(half, x_ref[rows, 128 * pr:128 * (pr + 1)].astype(F32) * w_ref[h], 0.0)
                    st[h] = st[h] * dec_ref[base + h:base + h + 1, :] + _tn(_mx(xm), _mx(y_ref[rows, 128 * h:128 * (h + 1)]))
            return carry

        lax.fori_loop(0, grp, chunk, 0)

    xs = lambda f: pl.BlockSpec((grp * c, 256), f)
    ys = lambda f: pl.BlockSpec((grp * c, 512), f)
    fwd, rev = (lambda n: (n, 0)), (lambda n: (ns - 1 - n, 0))
    wsp = pl.BlockSpec((RET_HEADS, c, 128), lambda n: (0, 0, 0))
    return pl.pallas_call(
        body, grid=(ns,), name=name,
        in_specs=[xs(fwd), ys(fwd), xs(rev), ys(rev), wsp, wsp, pl.BlockSpec((8, 128), lambda n: (0, 0))],
        out_specs=[pl.BlockSpec((grp, RET_HEADS, 128, 128), lambda n: (n, 0, 0, 0)),
                   pl.BlockSpec((grp, RET_HEADS, 128, 128), lambda n: (ns - 1 - n, 0, 0, 0))],
        out_shape=[jax.ShapeDtypeStruct((nc, RET_HEADS, 128, 128), MXU_DTYPE)] * 2,
        scratch_shapes=[pltpu.VMEM((RET_HEADS, 128, 128), F32), pltpu.VMEM((RET_HEADS, 128, 128), F32)],
        compiler_params=_cp("arbitrary"))(xk, yv, xk, yv, w_fwd, w_rev, dec)


def _ret_fwd(rq, rk, rv, rf, rb, tb):
    seq = rq.shape[0]
    c = RET_CHUNK
    grp = min(RET_GROUP, seq // c)

    def body(q_ref, k_ref, v_ref, rf_ref, rb_ref, d_ref, wqf_ref, wqb_ref, o_ref):
        lane = _lane((c, 128))

        def chunk(g, carry):
            rows = pl.ds(pl.multiple_of(g * c, c), c)
            for h in range(RET_HEADS):
                pr, e = h // 2, h % 2
                half = (lane < 64) if e == 0 else (lane >= 64)
                sl = slice(128 * pr, 128 * (pr + 1))
                qp, kp = q_ref[rows, sl], k_ref[rows, sl]
                km = jnp.where(half, kp, jnp.zeros_like(kp))
                sd = _mx(_nt(qp, km) * d_ref[h])
                qf = qp.astype(F32)
                o = _nn(sd, v_ref[rows, 128 * h:128 * (h + 1)])
                o = o + _nn(_mx(qf * wqf_ref[h]), rf_ref[g, h]) + _nn(_mx(qf * wqb_ref[h]), rb_ref[g, h])
                o_ref[rows, 128 * h:128 * (h + 1)] = o
            return carry

        lax.fori_loop(0, grp, chunk, 0)

    st = pl.BlockSpec((grp, RET_HEADS, 128, 128), lambda n: (n, 0, 0, 0))
    wsp = pl.BlockSpec((RET_HEADS, c, 128), lambda n: (0, 0, 0))
    return pl.pallas_call(
        body, grid=(seq // (grp * c),), name="ret_fwd",
        in_specs=[pl.BlockSpec((grp * c, 256), lambda n: (n, 0)), pl.BlockSpec((grp * c, 256), lambda n: (n, 0)),
                  pl.BlockSpec((grp * c, 512), lambda n: (n, 0)), st, st, pl.BlockSpec((RET_HEADS, c, c), lambda n: (0, 0, 0)), wsp, wsp],
        out_specs=pl.BlockSpec((grp * c, 512), lambda n: (n, 0)),
        out_shape=jax.ShapeDtypeStruct((seq, 512), F32), compiler_params=_cp("parallel"))(
            rq, rk, rv, rf, rb, tb["dmat"], tb["wqf"], tb["wqb"])


def _ret_bwd(rq, rk, rv, do, rf, rb, hf, hb, tb):
    seq = rq.shape[0]
    c = RET_CHUNK
    grp = min(RET_GROUP, seq // c)

    def body(q_ref, k_ref, v_ref, do_ref, rf_ref, rb_ref, hf_ref, hb_ref, d_ref, dda_ref, ddb_ref,
             wqf_ref, wqb_ref, wkf_ref, wkb_ref, cdec_ref, dq_ref, dk_ref, dv_ref, dlg_ref):
        @pl.when(pl.program_id(0) == 0)
        def _():
            dlg_ref[...] = jnp.zeros_like(dlg_ref)

        lane = _lane((c, 128))
        pos = lax.broadcasted_iota(jnp.int32, (c, 128), 0).astype(F32)

        def chunk(g, carry):
            rows = pl.ds(pl.multiple_of(g * c, c), c)
            for pr in range(2):
                sl = slice(128 * pr, 128 * (pr + 1))
                qp, kp = q_ref[rows, sl], k_ref[rows, sl]
                qf, kf = qp.astype(F32), kp.astype(F32)
                dq_acc = jnp.zeros((c, 128), F32)
                dk_acc = jnp.zeros((c, 128), F32)
                for e in range(2):
                    h = 2 * pr + e
                    half = (lane < 64) if e == 0 else (lane >= 64)
                    hs = slice(128 * h, 128 * (h + 1))
                    km = jnp.where(half, kp, jnp.zeros_like(kp))
                    kmf = km.astype(F32)
                    vh, doh = v_ref[rows, hs], do_ref[rows, hs]
                    rfh, rbh, hfh, hbh = rf_ref[g, h], rb_ref[g, h], hf_ref[g, h], hb_ref[g, h]
                    s = _nt(qp, km)
                    dpm = _nt(doh, vh)
                    ds_b = _mx(dpm * d_ref[h])
                    sd_b = _mx(s * d_ref[h])
                    dq_if = wqf_ref[h] * _nt(doh, rfh)
                    dq_ib = wqb_ref[h] * _nt(doh, rbh)
                    dq_acc = dq_acc + _nn(ds_b, km) + dq_if + dq_ib
                    dk_sf = wkf_ref[h] * _nt(vh, hfh)
                    dk_sb = wkb_ref[h] * _nt(vh, hbh)
                    dk_acc = dk_acc + jnp.where(half, _tn(ds_b, qp), 0.0) + dk_sf + dk_sb
                    dv = _tn(sd_b, doh) + _nn(_mx(kmf * wkf_ref[h]), hfh) + _nn(_mx(kmf * wkb_ref[h]), hbh)
                    dv_ref[rows, hs] = dv.astype(dv_ref.dtype)
                    sdp = s * dpm
                    hr_f = hfh.astype(F32) * rfh.astype(F32)
                    hr_b = hbh.astype(F32) * rbh.astype(F32)
                    da = (_rowsum128(sdp * dda_ref[h]) + _rowsum128((pos + 1.0) * qf * dq_if)
                          + _rowsum128((c - 1.0 - pos) * kf * dk_sf) + cdec_ref[h:h + 1, :] * _rowsum128(hr_f))
                    db = (_rowsum128(sdp * ddb_ref[h]) + _rowsum128((c - pos) * qf * dq_ib)
                          + _rowsum128(pos * kf * dk_sb) + cdec_ref[4 + h:5 + h, :] * _rowsum128(hr_b))
                    dlg_ref[h:h + 1, :] += da
                    dlg_ref[4 + h:5 + h, :] += db
                dq_ref[rows, sl] = dq_acc
                dk_ref[rows, sl] = dk_acc
            return carry

        lax.fori_loop(0, grp, chunk, 0)

    st = pl.BlockSpec((grp, RET_HEADS, 128, 128), lambda n: (n, 0, 0, 0))
    wsp = pl.BlockSpec((RET_HEADS, c, 128), lambda n: (0, 0, 0))
    dsp = pl.BlockSpec((RET_HEADS, c, c), lambda n: (0, 0, 0))
    x256 = pl.BlockSpec((grp * c, 256), lambda n: (n, 0))
    x512 = pl.BlockSpec((grp * c, 512), lambda n: (n, 0))
    cdec = tb["dec_fb"] * float(c)
    return pl.pallas_call(
        body, grid=(seq // (grp * c),), name="ret_bwd",
        in_specs=[x256, x256, x512, x512, st, st, st, st, dsp, dsp, dsp, wsp, wsp, wsp, wsp,
                  pl.BlockSpec((8, 128), lambda n: (0, 0))],
        out_specs=[x256, x256, x512, pl.BlockSpec((8, 128), lambda n: (0, 0))],
        out_shape=[jax.ShapeDtypeStruct((seq, 256), F32), jax.ShapeDtypeStruct((seq, 256), F32),
                   jax.ShapeDtypeStruct((seq, 512), MXU_DTYPE), jax.ShapeDtypeStruct((8, 128), F32)],
        compiler_params=_cp("arbitrary"))(
            rq, rk, rv, do, rf, rb, hf, hb, tb["dmat"], tb["dda"], tb["ddb"], tb["wqf"], tb["wqb"], tb["wkf"], tb["wkb"], cdec)


def _tail(x, tgt, o_att, o_ret, p, mod, g_post, gn_g, wpt, wout):
    seq = x.shape[0]
    bs = 256
    nb = seq // bs

    def body(x_ref, t_ref, oa_ref, or_ref, za_ref, zr_ref, gl_ref, mod_ref, gp_ref, gn_ref, wpt_ref, wout_ref,
             dout_ref, dza_ref, dzr_ref, dgl_ref, doa_ref, dor_ref, del_ref, gout_ref, gpt_ref,
             dgate_ref, dgpost_ref, dgn_ref, loss_ref, gout_acc, gpt_acc):
        i = pl.program_id(0)

        @pl.when(i == 0)
        def _():
            gout_acc[...] = jnp.zeros_like(gout_acc)
            gpt_acc[...] = jnp.zeros_like(gpt_acc)
            dgate_ref[...] = jnp.zeros_like(dgate_ref)
            dgpost_ref[...] = jnp.zeros_like(dgpost_ref)
            dgn_ref[...] = jnp.zeros_like(dgn_ref)
            loss_ref[...] = jnp.zeros_like(loss_ref)

        oa, za, zr = oa_ref[...], za_ref[...], zr_ref[...]
        sga, sgr = _sigmoid(za), _sigmoid(zr)
        sza, szr = za * sga, zr * sgr
        ya_b = _mx(oa * sza)
        ons, rstds = [], []
        for h in range(RET_HEADS):
            oh = or_ref[:, 128 * h:128 * (h + 1)]
            xc = oh - jnp.mean(oh, axis=-1, keepdims=True)
            rstd = lax.rsqrt(jnp.mean(xc * xc, axis=-1, keepdims=True) + EPS)
            ons.append(xc * rstd)
            rstds.append(rstd)
        on = jnp.concatenate(ons, axis=1)
        yn = on * gn_ref[...]
        yr_b = _mx(yn * szr)
        wpa_t, wpr_t = wpt_ref[:, 0:512], wpt_ref[:, 512:1024]
        a_att = _nt(ya_b, wpa_t)
        a_ret = _nt(yr_b, wpr_t)
        gts = _sigmoid(gl_ref[...])
        g_att, g_ret = gts[:, 0:D_MODEL], gts[:, D_MODEL:2 * D_MODEL]
        merged_b = _mx(g_att * a_att + g_ret * a_ret)
        u = _nn(merged_b, wout_ref[...])
        r = lax.rsqrt(jnp.mean(u * u, axis=-1, keepdims=True) + EPS)
        un = u * r
        gpost = gp_ref[...]
        y = un * gpost
        gate = mod_ref[:, 2 * D_MODEL:3 * D_MODEL]
        err = x_ref[...] + gate * y - t_ref[...]
        loss_ref[...] += (0.5 / D_MODEL) * _rowsum128(err * err)
        dout = err * (1.0 / D_MODEL)
        dout_ref[...] = dout
        dgate_ref[...] += jnp.sum(dout * y, axis=0, keepdims=True)
        dy = dout * gate
        dgpost_ref[...] += jnp.sum(dy * un, axis=0, keepdims=True)
        dun = dy * gpost
        du_b = _mx(r * (dun - un * jnp.mean(dun * un, axis=-1, keepdims=True)))
        dmerged = _nt(du_b, wout_ref[...])
        gout_acc[...] += _tn(merged_b, du_b)
        d_att, d_ret = dmerged * g_att, dmerged * g_ret
        dgl_ref[:, 0:D_MODEL] = (d_att * a_att * (1.0 - g_att)).astype(dgl_ref.dtype)
        dgl_ref[:, D_MODEL:2 * D_MODEL] = (d_ret * a_ret * (1.0 - g_ret)).astype(dgl_ref.dtype)
        d_att_b, d_ret_b = _mx(d_att), _mx(d_ret)
        dya = _nn(d_att_b, wpa_t)
        dyr = _nn(d_ret_b, wpr_t)
        gpt_acc[:, 0:512] += _tn(d_att_b, ya_b)
        gpt_acc[:, 512:1024] += _tn(d_ret_b, yr_b)
        dza_ref[...] = (dya * oa * (sga * (1.0 + za * (1.0 - sga)))).astype(dza_ref.dtype)
        doa = dya * sza
        doa_ref[...] = doa.astype(doa_ref.dtype)
        dt = jnp.transpose(doa * oa)
        del_ref[...] = jnp.sum(dt.reshape(8, HEAD_DIM, bs), axis=1)
        dzr_ref[...] = (dyr * yn * (sgr * (1.0 + zr * (1.0 - sgr)))).astype(dzr_ref.dtype)
        dyn = dyr * szr
        dgn_ref[...] += jnp.sum(dyn * on, axis=0, keepdims=True)
        don = dyn * gn_ref[...]
        for h in range(RET_HEADS):
            hs = slice(128 * h, 128 * (h + 1))
            dh, oh = don[:, hs], ons[h]
            doh = rstds[h] * (dh - jnp.mean(dh, axis=-1, keepdims=True) - oh * jnp.mean(dh * oh, axis=-1, keepdims=True))
            dor_ref[:, hs] = doh.astype(dor_ref.dtype)

        @pl.when(i == nb - 1)
        def _():
            gout_ref[...] = gout_acc[...].astype(gout_ref.dtype)
            gpt_ref[...] = gpt_acc[...].astype(gpt_ref.dtype)

    row = lambda w: pl.BlockSpec((bs, w), lambda i: (i, 0))
    el = lambda w, off: pl.BlockSpec((pl.Element(bs), pl.Element(w)), lambda i: (i * bs, off))
    vec = lambda w: pl.BlockSpec((1, w), lambda i: (0, 0))
    full = pl.BlockSpec((D_MODEL, D_MODEL), lambda i: (0, 0))
    sds = jax.ShapeDtypeStruct
    return pl.pallas_call(
        body, grid=(nb,), name="tail",
        in_specs=[row(D_MODEL), row(D_MODEL), row(512), row(512), el(512, ZA_B), el(512, ZR_B), el(2 * D_MODEL, GL_B),
                  vec(3 * D_MODEL), vec(D_MODEL), vec(512), full, full],
        out_specs=[row(D_MODEL), row(512), row(512), row(2 * D_MODEL), row(512), row(512),
                   pl.BlockSpec((8, bs), lambda i: (0, i)), full, full, vec(D_MODEL), vec(D_MODEL), vec(512), vec(128)],
        out_shape=[sds((seq, D_MODEL), F32), sds((seq, 512), MXU_DTYPE), sds((seq, 512), MXU_DTYPE),
                   sds((seq, 2 * D_MODEL), MXU_DTYPE), sds((seq, 512), MXU_DTYPE), sds((seq, 512), MXU_DTYPE),
                   sds((8, seq), F32), sds((D_MODEL, D_MODEL), MXU_DTYPE), sds((D_MODEL, D_MODEL), MXU_DTYPE),
                   sds((1, D_MODEL), F32), sds((1, D_MODEL), F32), sds((1, 512), F32), sds((1, 128), F32)],
        scratch_shapes=[pltpu.VMEM((D_MODEL, D_MODEL), F32), pltpu.VMEM((D_MODEL, D_MODEL), F32)],
        compiler_params=_cp("arbitrary"))(x, tgt, o_att, o_ret, p, p, p, mod, g_post, gn_g, wpt, wout)


def _assemble(p, cos, sin, qg2, kg2, dqt, dkp, dvp, dza, drq, drk, drv, dzr, dgl):
    seq = p.shape[0]
    bs = 512

    def body(p_ref, cos_ref, sin_ref, qg_ref, kg_ref, dqt_ref, dkp_ref, dvp_ref, dza_ref, drq_ref, drk_ref, drv_ref,
             dzr_ref, dgl_ref, dp_ref, dqg_ref, dkg_ref):
        @pl.when(pl.program_id(0) == 0)
        def _():
            dqg_ref[...] = jnp.zeros_like(dqg_ref)
            dkg_ref[...] = jnp.zeros_like(dkg_ref)

        m = _blockdiag64()
        cs, sn = cos_ref[...], sin_ref[...]
        lo = _lane((bs, 128)) < 64

        def norm_bwd(raw, dn, g, dg_ref):
            r = lax.rsqrt(_seg64(raw * raw, m) * (1.0 / 64) + EPS)
            xn = raw * r
            dg_ref[...] += jnp.sum(dn * xn, axis=0, keepdims=True)
            dxn = dn * g
            return r * (dxn - xn * (_seg64(dxn * xn, m) * (1.0 / 64)))

        for pr in range(4):
            sl = slice(128 * pr, 128 * (pr + 1))
            dn = _rope_t(dqt_ref[:, sl] * 0.125, cs, sn)
            dp_ref[:, QA + 128 * pr:QA + 128 * (pr + 1)] = norm_bwd(p_ref[:, sl], dn, qg_ref[...], dqg_ref).astype(dp_ref.dtype)
        fold = lambda a: a + pltpu.roll(a, 64, 1)
        dk = jnp.where(lo, fold(dkp_ref[0]), fold(dkp_ref[1]))
        dp_ref[:, KA:KA + 128] = norm_bwd(p_ref[:, KA:KA + 128], _rope_t(dk, cs, sn), kg_ref[...], dkg_ref).astype(dp_ref.dtype)
        dp_ref[:, VA:VA + 128] = jnp.where(lo, fold(dvp_ref[0]), fold(dvp_ref[1])).astype(dp_ref.dtype)
        dp_ref[:, ZA:ZA + 512] = dza_ref[...]
        for pr in range(2):
            sl = slice(128 * pr, 128 * (pr + 1))
            dp_ref[:, QR + 128 * pr:QR + 128 * (pr + 1)] = _rope_t(drq_ref[:, sl], cs, sn).astype(dp_ref.dtype)
            dp_ref[:, KR + 128 * pr:KR + 128 * (pr + 1)] = _rope_t(drk_ref[:, sl] * 0.125, cs, sn).astype(dp_ref.dtype)
        dp_ref[:, VR:VR + 512] = drv_ref[...]
        dp_ref[:, ZR:ZR + 512] = dzr_ref[...]
        dp_ref[:, GL:GL + 2 * D_MODEL] = dgl_ref[...]

    row = lambda w: pl.BlockSpec((bs, w), lambda i: (i, 0))
    gsp = pl.BlockSpec((1, 128), lambda i: (0, 0))
    dup = pl.BlockSpec((2, bs, 128), lambda i: (0, i, 0))
    return pl.pallas_call(
        body, grid=(seq // bs,), name="assemble_dp",
        in_specs=[row(768), row(128), row(128), gsp, gsp, row(512), dup, dup, row(512), row(256), row(256), row(512),
                  row(512), row(2 * D_MODEL)],
        out_specs=[row(IN_WIDTH), gsp, gsp],
        out_shape=[jax.ShapeDtypeStruct((seq, IN_WIDTH), MXU_DTYPE), jax.ShapeDtypeStruct((1, 128), F32),
                   jax.ShapeDtypeStruct((1, 128), F32)],
        compiler_params=_cp("arbitrary"))(p, cos, sin, qg2, kg2, dqt, dkp, dvp, dza, drq, drk, drv, dzr, dgl)


def _local_step(x, tgt, mod, g_pre, qn_g, kn_g, w_dec_f, w_dec_b, gn_g, g_post, w_attn, rest_start, rest_wait, send=None):
    send = send or (lambda name, arrays: None)
    seq = x.shape[0]
    cos, sin = _rope_tables(seq)
    qg2, kg2 = jnp.tile(qn_g, (1, 2)), jnp.tile(kn_g, (1, 2))
    lg_f = jax.nn.log_sigmoid(w_dec_f[0])
    lg_b = jax.nn.log_sigmoid(w_dec_b[0])
    tb = _ret_tables(lg_f, lg_b, RET_CHUNK)

    h, p_a, qt, kd, vd = _attn_inputs(x, mod, g_pre, w_attn, cos, sin, qg2, kg2)
    o_att, lse = _attn_fwd(qt, kd, vd, dep=rest_start(qt))
    win_t, wpt, wout = rest_wait(o_att)
    p_b, rq, rk, rv = _in_proj_rest(h, win_t, cos, sin)
    rf, rb = _ret_states(rk, rv, tb["wkf"], tb["wkb"], tb["dec_fb"], "ret_states_fwd")
    o_ret = _ret_fwd(rq, rk, rv, rf, rb, tb)
    (dout, dza, dzr, dgl, do_att, do_ret, delta, g_out, g_pt, dgate, dgpost, dgn, loss_l) = _tail(
        x, tgt, o_att, o_ret, p_b, mod, g_post, gn_g, wpt, wout)
    dep = send("early", (g_pt, g_out))
    dqt, dkp, dvp = _attn_bwd(qt, kd, vd, do_att, lse, delta, dep=dep)
    hb, hf = _ret_states(rq, do_ret, tb["wqb"], tb["wqf"], tb["dec_bf"], "ret_states_bwd")
    drq, drk, drv, dlg = _ret_bwd(rq, rk, rv, do_ret, rf, rb, hf, hb, tb)
    dp, dqg, dkg = _assemble(p_a, cos, sin, qg2, kg2, dqt, dkp, dvp, dza, drq, drk, drv, dzr, dgl)
    g_in_t = _matmul(dp, h, ta=True, tb=False, tm=256, tn=D_MODEL, out_dtype=MXU_DTYPE, name="in_proj_dw")
    dep = send("late", (g_in_t,))
    grad_x, dshift, dscale, dgpre = _in_proj_dx(x, dp, win_t, dout, mod, g_pre, dep=dep)

    dlg = jnp.sum(dlg, axis=1)
    small = dict(
        dmod=jnp.concatenate([dshift, dscale, dgate], axis=1),
        g_pre=dgpre, g_post=dgpost, gn_g=dgn,
        qn_g=dqg[:, :64] + dqg[:, 64:], kn_g=dkg[:, :64] + dkg[:, 64:],
        w_dec_f=(dlg[0:4] * jax.nn.sigmoid(-w_dec_f[0]))[None], w_dec_b=(dlg[4:8] * jax.nn.sigmoid(-w_dec_b[0]))[None],
        loss=jnp.sum(loss_l, axis=1, keepdims=True))
    return grad_x, g_in_t, g_pt, g_out, small


N_DEV = 8
N_CHIP = 4
HBM_SPEC = pl.BlockSpec(memory_space=pl.ANY)
VMEM_SPEC = pl.BlockSpec(memory_space=pltpu.VMEM)
SHARD_ROWS = (IN_WIDTH // N_CHIP, D_MODEL // N_CHIP, D_MODEL // N_CHIP)


def _coords():
    return lax.axis_index("x"), lax.axis_index("y"), lax.axis_index("c")


def _peer(k):
    x, y, c = _coords()
    return (1 - x if k & 4 else x, 1 - y if k & 2 else y, 1 - c if k & 1 else c)


def _dev_index(p):
    return 4 * p[0] + 2 * p[1] + p[2]


def _rows(ref, start, size):
    return ref.at[pl.ds(pl.multiple_of(start, 16), size), :]


def _remote(src, dst, ssem, rsem, dev):
    return pltpu.make_async_remote_copy(src_ref=src, dst_ref=dst, send_sem=ssem, recv_sem=rsem, device_id=dev,
                                        device_id_type=MESH)


ATTN_CHUNK = 96


def _mod_and_attn_rows(c, w_ada_s, b4, win_s):
    ncol = w_ada_s.shape[1]
    half = ATTN_COLS // 2
    offs = list(range(0, half, ATTN_CHUNK))
    nq = len(offs)
    rows = lambda ref, cc, off: ref.at[pl.ds(pl.multiple_of(cc * half + off, 16), ATTN_CHUNK), :]

    def body(c_ref, w_ref, b_ref, src, cs_ref, mod_ref, out, modsh, modall, ssem, rsem, lsem, asem, bsem, fsem, gsem, hsem):
        x, y, cc = _coords()
        me = _dev_index((x, y, cc))
        chip = me // 2
        sib = (x, y, 1 - cc)
        cs_ref[me] = c_ref[...]
        sends = []
        for k in range(1, N_DEV):
            cp = _remote(c_ref, cs_ref.at[me], ssem.at[k - 1], rsem.at[k - 1], _peer(k))
            cp.start()
            sends.append(cp)
        own = pltpu.make_async_copy(src.at[pl.ds(0, ATTN_COLS), :], out, lsem.at[0])
        bulk = [_remote(rows(src, cc, off), rows(out, cc, off), asem.at[(k2 - 1) * nq + q], bsem.at[q], (k2 // 2, k2 % 2, cc))
                for k2 in (1, 2) for q, off in enumerate(offs)]
        relay_chip = lambda q: 1 + q % 2

        @pl.when(chip == 0)
        def _():
            own.start()
            for cp in bulk:
                cp.start()

        for k in range(1, N_DEV):
            slot = cs_ref.at[_dev_index(_peer(k))]
            _remote(slot, slot, ssem.at[k - 1], rsem.at[k - 1], _peer(k)).wait_recv()
        cs = jnp.concatenate([cs_ref[d] for d in range(N_DEV)], axis=0)
        ms = _nn(cs * _sigmoid(cs), w_ref[...], precision=HIGHEST) + b_ref[pl.ds(chip, 1), :]
        modsh[...] = ms
        modall[chip] = ms
        for k2 in range(1, N_CHIP):
            cp = _remote(modsh, modall.at[chip], ssem.at[6 + k2], rsem.at[6 + k2], _peer(2 * k2))
            cp.start()
            sends.append(cp)

        @pl.when(chip != 0)
        def _():
            passed = []
            relays = [_remote(rows(out, cc, off), rows(out, cc, off), hsem.at[q], bsem.at[q], (1, 1, cc)) for q, off in enumerate(offs)]
            for q, off in enumerate(offs):
                got = rows(out, cc, off)
                _remote(got, got, asem.at[q], bsem.at[q], (0, 0, cc)).wait_recv()
                cp = _remote(got, got, fsem.at[q], gsem.at[q], sib)
                cp.start()
                passed.append(cp)
                pl.when(chip == relay_chip(q))(relays[q].start)
            for q, off in enumerate(offs):
                got = rows(out, 1 - cc, off)
                _remote(got, got, fsem.at[q], gsem.at[q], sib).wait_recv()
            for cp in passed:
                cp.wait_send()
            for q in range(nq):
                pl.when(chip == relay_chip(q))(relays[q].wait_send)

        for k2 in range(1, N_CHIP):
            slot = modall.at[_dev_index(_peer(2 * k2)) // 2]
            _remote(slot, slot, ssem.at[6 + k2], rsem.at[6 + k2], _peer(2 * k2)).wait_recv()
        for jj in range(N_CHIP):
            mod_ref[:, ncol * jj:ncol * (jj + 1)] = modall[jj, pl.ds(me, 1), :]
        for cp in sends:
            cp.wait_send()

        @pl.when(chip == 0)
        def _():
            for cp in bulk:
                cp.wait_send()
            own.wait()

    dma = pltpu.SemaphoreType.DMA
    return pl.pallas_call(
        body, name="mod_and_attn_rows", in_specs=[VMEM_SPEC] * 4, out_specs=[VMEM_SPEC, VMEM_SPEC, HBM_SPEC],
        out_shape=[jax.ShapeDtypeStruct((N_DEV, 1, D_MODEL), F32), jax.ShapeDtypeStruct((1, N_CHIP * ncol), F32),
                   jax.ShapeDtypeStruct((ATTN_COLS, win_s.shape[1]), win_s.dtype)],
        scratch_shapes=[pltpu.VMEM((N_DEV, ncol), F32), pltpu.VMEM((N_CHIP, N_DEV, ncol), F32), dma((10,)), dma((10,)),
                        dma((1,)), dma((2 * nq,)), dma((nq,)), dma((nq,)), dma((nq,)), dma((nq,))],
        compiler_params=_cp())(c, w_ada_s, b4, win_s)


GATHER_CHUNK = 32


def _chunks(kinds, chunk):
    out = []
    for t, kind in enumerate(kinds):
        half = SHARD_ROWS[kind] // 2
        step = chunk if half % chunk == 0 else half
        out += [(t, off, step) for off in range(0, half, step)]
    return out


SEM_SPEC = pl.BlockSpec(memory_space=pltpu.SEMAPHORE)
HBM_ONLY = pl.BlockSpec(memory_space=pltpu.HBM)
DATAFLOW = pltpu.SideEffectType.DATAFLOW_SIDE_EFFECTING


def _place_own(shards):
    nt = len(shards)

    def body(*refs):
        srcs, outs, lsem = refs[:nt], refs[nt:2 * nt], refs[2 * nt]
        x, y, _ = _coords()
        chip = 2 * x + y
        local = [pltpu.make_async_copy(srcs[t], _rows(outs[t], chip * SHARD_ROWS[t], SHARD_ROWS[t]), lsem.at[t]) for t in range(nt)]
        for cp in local:
            cp.start()
        for cp in local:
            cp.wait()

    return pl.pallas_call(
        body, name="place_own_shard", in_specs=[VMEM_SPEC] * nt, out_specs=[HBM_SPEC] * nt,
        out_shape=[jax.ShapeDtypeStruct((N_CHIP * SHARD_ROWS[t], s.shape[1]), s.dtype) for t, s in enumerate(shards)],
        scratch_shapes=[pltpu.SemaphoreType.DMA((nt,))], compiler_params=_cp())(*shards)


def _gather_rest_start(shards, zones, dep):
    n = len(shards)

    def body(*refs):
        srcs, lands = refs[:n], refs[n:2 * n]
        ssem, rsem = refs[2 * n + 1], refs[2 * n + 2]
        token = refs[-1]
        x, y, _ = _coords()
        chip = 2 * x + y
        for k2 in range(1, N_CHIP):
            for t in range(n):
                q = (k2 - 1) * n + t
                _remote(srcs[t], _rows(lands[t], chip * SHARD_ROWS[t], SHARD_ROWS[t]), ssem.at[q], rsem.at[q], _peer(2 * k2)).start()
        token[...] = jnp.zeros_like(token)

    hbm = lambda a: pltpu.with_memory_space_constraint(a, pltpu.HBM)
    dma = pltpu.SemaphoreType.DMA
    outs = pl.pallas_call(
        body, name="gather_rest", in_specs=[HBM_ONLY] * (2 * n) + [HBM_SPEC],
        out_specs=[SEM_SPEC, SEM_SPEC] + [HBM_ONLY] * (2 * n) + [VMEM_SPEC],
        out_shape=[dma((3 * n,)), dma((3 * n,))] + [pltpu.HBM(a.shape, a.dtype) for a in list(shards) + list(zones)]
        + [jax.ShapeDtypeStruct((8, 128), F32)],
        input_output_aliases={i: 2 + i for i in range(2 * n)},
        compiler_params=pltpu.CompilerParams(has_side_effects=DATAFLOW))(*[hbm(a) for a in list(shards) + list(zones)], dep)
    return outs[0], outs[1], outs[2:2 + n], outs[2 + n:2 + 2 * n], outs[-1]


def _gather_rest_wait(started, after):
    ssem, rsem, shards, zones, _ = started
    n = len(shards)

    def body(*refs):
        srcs, lands = refs[:n], refs[n:2 * n]
        ssem_ref, rsem_ref = refs[2 * n], refs[2 * n + 1]
        for k2 in range(1, N_CHIP):
            pchip = _dev_index(_peer(2 * k2)) // 2
            for t in range(n):
                q = (k2 - 1) * n + t
                cp = _remote(srcs[t], _rows(lands[t], pchip * SHARD_ROWS[t], SHARD_ROWS[t]), ssem_ref.at[q], rsem_ref.at[q], _peer(2 * k2))
                cp.wait_send()
                cp.wait_recv()

    outs = pl.pallas_call(
        body, name="gather_rest_wait", in_specs=[HBM_ONLY] * (2 * n) + [SEM_SPEC, SEM_SPEC, HBM_SPEC], out_specs=[HBM_ONLY] * (2 * n),
        out_shape=[pltpu.HBM(a.shape, a.dtype) for a in list(shards) + list(zones)],
        input_output_aliases={i: i for i in range(2 * n)},
        compiler_params=pltpu.CompilerParams(has_side_effects=DATAFLOW))(*shards, *zones, ssem, rsem, after)
    return outs[n:]


def _reduced_by(ref, t, dev):
    return _rows(ref, (dev // 2) * SHARD_ROWS[t] + (dev % 2) * (SHARD_ROWS[t] // 2), SHARD_ROWS[t] // 2)


def _scatter_start(grads, kinds, name):
    n = len(grads)

    def body(*refs):
        srcs, lands = refs[:n], refs[n:2 * n]
        ssem, rsem = refs[2 * n], refs[2 * n + 1]
        token = refs[-1]
        me = _dev_index(_coords())
        for k in range(1, N_DEV):
            for i, t in enumerate(kinds):
                q = (k - 1) * n + i
                _remote(_reduced_by(srcs[i], t, _dev_index(_peer(k))), lands[i].at[me], ssem.at[q], rsem.at[q], _peer(k)).start()
        token[...] = jnp.zeros_like(token)

    zones = [lax.empty((N_DEV, SHARD_ROWS[t] // 2, g.shape[1]), g.dtype) for g, t in zip(grads, kinds)]
    hbm = lambda a: pltpu.with_memory_space_constraint(a, pltpu.HBM)
    dma = pltpu.SemaphoreType.DMA
    outs = pl.pallas_call(
        body, name=name, in_specs=[HBM_ONLY] * (2 * n), out_specs=[SEM_SPEC, SEM_SPEC] + [HBM_ONLY] * (2 * n) + [VMEM_SPEC],
        out_shape=[dma((7 * n,)), dma((7 * n,))] + [pltpu.HBM(a.shape, a.dtype) for a in list(grads) + zones]
        + [jax.ShapeDtypeStruct((8, 128), F32)],
        input_output_aliases={i: 2 + i for i in range(2 * n)},
        compiler_params=pltpu.CompilerParams(has_side_effects=DATAFLOW))(*[hbm(a) for a in list(grads) + zones])
    return outs[0], outs[1], outs[2:2 + n], outs[2 + n:2 + 2 * n], outs[-1]


def _scatter_wait(started, kinds, after, name):
    ssem, rsem, grads, zones, _ = started
    n = len(grads)

    def body(*refs):
        srcs, lands = refs[:n], refs[n:2 * n]
        ssem_ref, rsem_ref = refs[2 * n], refs[2 * n + 1]
        for k in range(1, N_DEV):
            peer = _dev_index(_peer(k))
            for i, t in enumerate(kinds):
                q = (k - 1) * n + i
                cp = _remote(_reduced_by(srcs[i], t, peer), lands[i].at[peer], ssem_ref.at[q], rsem_ref.at[q], _peer(k))
                cp.wait_send()
                cp.wait_recv()

    outs = pl.pallas_call(
        body, name=name, in_specs=[HBM_ONLY] * (2 * n) + [SEM_SPEC, SEM_SPEC, HBM_SPEC], out_specs=[HBM_ONLY] * (2 * n),
        out_shape=[pltpu.HBM(a.shape, a.dtype) for a in list(grads) + list(zones)],
        input_output_aliases={i: i for i in range(2 * n)},
        compiler_params=pltpu.CompilerParams(has_side_effects=DATAFLOW))(*grads, *zones, ssem, rsem, after)
    return outs[:n], outs[n:]


def _grad_finish(grads, zones, kinds, name):
    nt = len(grads)
    pieces = _chunks(kinds, GATHER_CHUNK)
    npc = len(pieces)
    shard = [SHARD_ROWS[k] for k in kinds]

    def body(*refs):
        srcs, lands, outs = refs[:nt], refs[nt:2 * nt], refs[2 * nt:3 * nt]
        slots, sums = refs[3 * nt:4 * nt], refs[4 * nt:5 * nt]
        lsem, osem, xsem, ysem = refs[5 * nt:]
        x, y, c = _coords()
        me = _dev_index((x, y, c))
        sib = (x, y, 1 - c)
        loads = [pltpu.make_async_copy(_reduced_by(srcs[t], kinds[t], me), slots[t].at[me], lsem.at[t]) for t in range(nt)]
        for k in range(1, N_DEV):
            peer = _dev_index(_peer(k))
            loads += [pltpu.make_async_copy(lands[t].at[peer], slots[t].at[peer], lsem.at[k * nt + t]) for t in range(nt)]
        for cp in loads:
            cp.start()
        for cp in loads:
            cp.wait()
        sends = []
        place = lambda t, cc, off, n: _rows(outs[t], cc * (shard[t] // 2) + off, n)
        stores = []
        for q, (t, off, n) in enumerate(pieces):
            r = pl.ds(off, n)
            acc = slots[t][0, r, :].astype(F32)
            for d in range(1, N_DEV):
                acc = acc + slots[t][d, r, :].astype(F32)
            sums[t][r, :] = acc
            keep = pltpu.make_async_copy(sums[t].at[r, :], place(t, c, off, n), osem.at[q])
            give = _remote(sums[t].at[r, :], place(t, c, off, n), xsem.at[q], ysem.at[q], sib)
            keep.start()
            give.start()
            stores.append(keep)
            sends.append(give)
        for q, (t, off, n) in enumerate(pieces):
            got = place(t, 1 - c, off, n)
            _remote(got, got, xsem.at[q], ysem.at[q], sib).wait_recv()
        for cp in sends:
            cp.wait_send()
        for cp in stores:
            cp.wait()

    dma = pltpu.SemaphoreType.DMA
    return pl.pallas_call(
        body, name=name, in_specs=[HBM_SPEC] * (2 * nt), out_specs=[HBM_SPEC] * nt,
        out_shape=[jax.ShapeDtypeStruct((shard[t], g.shape[1]), F32) for t, g in enumerate(grads)],
        scratch_shapes=([pltpu.VMEM((N_DEV, shard[t] // 2, g.shape[1]), g.dtype) for t, g in enumerate(grads)]
                        + [pltpu.VMEM((shard[t] // 2, g.shape[1]), F32) for t, g in enumerate(grads)]
                        + [dma((N_DEV * nt,)), dma((npc,)), dma((npc,)), dma((npc,))]),
        compiler_params=_cp())(*grads, *zones)


def _adam_math(w, g, m, v):
    m = ADAM_B1 * m + (1.0 - ADAM_B1) * g
    v = ADAM_B2 * v + (1.0 - ADAM_B2) * (g * g)
    m_hat = m / (1.0 - ADAM_B1 ** ADAM_STEP)
    v_hat = v / (1.0 - ADAM_B2 ** ADAM_STEP)
    return -ADAM_LR * (m_hat / (jnp.sqrt(v_hat) + ADAM_EPS) + ADAM_WD * w), m, v


def _adamw(w, g, m, v, rb, name):
    rows, cols = w.shape

    def body(w_ref, g_ref, m_ref, v_ref, d_ref, nm_ref, nv_ref):
        d_ref[...], nm_ref[...], nv_ref[...] = _adam_math(w_ref[...], g_ref[...], m_ref[...], v_ref[...])

    spec = pl.BlockSpec((rb, cols), lambda i: (i, 0))
    return pl.pallas_call(body, grid=(rows // rb,), name=name, in_specs=[spec] * 4, out_specs=[spec] * 3,
                          out_shape=[jax.ShapeDtypeStruct(w.shape, F32)] * 3, compiler_params=_cp("parallel"))(w, g, m, v)


PACK = (("b_ada", 3 * D_MODEL), ("g_pre", D_MODEL), ("g_post", D_MODEL), ("gn_g", 512), ("qn_g", 64), ("kn_g", 64),
        ("w_dec_f", 4), ("w_dec_b", 4), ("loss", 1))
PACK_OFFSETS = {}
PACK_WIDTH = 0
for _name, _n in PACK:
    PACK_OFFSETS[_name] = PACK_WIDTH
    PACK_WIDTH += -(-_n // 128) * 128
SMALL_PARAMS = tuple(name for name, _ in PACK if name != "loss")


def _pack(parts):
    cols = []
    for name, n in PACK:
        pad = -(-n // 128) * 128 - n
        cols.append(parts[name].reshape(1, n).astype(F32))
        if pad:
            cols.append(jnp.zeros((1, pad), F32))
    return jnp.concatenate(cols, axis=1)


def _small_reduce(vec, cs, deps):
    ncol = 3 * D_MODEL // N_CHIP

    def body(vec_ref, cs_ref, *rest):
        tot_ref, gwa_ref, gat, ssem, rsem = rest[-5:]
        me = _dev_index(_coords())
        chip = me // 2
        gat[me] = vec_ref[...]
        sends = []
        for k in range(1, N_DEV):
            cp = _remote(vec_ref, gat.at[me], ssem.at[k - 1], rsem.at[k - 1], _peer(k))
            cp.start()
            sends.append(cp)
        for k in range(1, N_DEV):
            slot = gat.at[_dev_index(_peer(k))]
            _remote(slot, slot, ssem.at[k - 1], rsem.at[k - 1], _peer(k)).wait_recv()
        rows = [gat[d] for d in range(N_DEV)]
        tot = rows[0]
        for d in range(1, N_DEV):
            tot = tot + rows[d]
        tot_ref[...] = tot
        cs = cs_ref[...]
        ca_t = jnp.transpose(jnp.concatenate([cs * _sigmoid(cs), jnp.zeros((128 - N_DEV, D_MODEL), F32)], axis=0))
        dm = jnp.concatenate(rows + [jnp.zeros((128 - N_DEV, PACK_WIDTH), F32)], axis=0)
        for jj in range(N_CHIP):
            @pl.when(chip == jj)
            def _():
                gwa_ref[...] = _nn(ca_t, dm[:, ncol * jj:ncol * (jj + 1)], precision=HIGHEST)
        for cp in sends:
            cp.wait_send()

    return pl.pallas_call(
        body, name="small_reduce", in_specs=[VMEM_SPEC, VMEM_SPEC] + [HBM_SPEC] * len(deps), out_specs=[VMEM_SPEC] * 2,
        out_shape=[jax.ShapeDtypeStruct((1, PACK_WIDTH), F32), jax.ShapeDtypeStruct((D_MODEL, ncol), F32)],
        scratch_shapes=[pltpu.VMEM((N_DEV, 1, PACK_WIDTH), F32), pltpu.SemaphoreType.DMA((7,)), pltpu.SemaphoreType.DMA((7,))],
        compiler_params=_cp())(vec, cs, *deps)


def _small_adamw(tot, given):
    sizes = dict(PACK)
    np_ = len(SMALL_PARAMS)

    def body(*refs):
        tot_ref = refs[0]
        wmv = refs[1:1 + 3 * np_]
        outs = refs[1 + 3 * np_:1 + 7 * np_]
        loss_ref = refs[-1]
        for i, name in enumerate(SMALL_PARAMS):
            off, n = PACK_OFFSETS[name], sizes[name]
            g = tot_ref[:, off:off + n]
            w_ref, m_ref, v_ref = wmv[3 * i:3 * i + 3]
            outs[i][...] = g
            outs[np_ + i][...], outs[2 * np_ + i][...], outs[3 * np_ + i][...] = _adam_math(w_ref[...], g, m_ref[...], v_ref[...])
        loss_ref[...] = tot_ref[:, PACK_OFFSETS["loss"]:PACK_OFFSETS["loss"] + 1]

    flat = [a for name in SMALL_PARAMS for a in given[name]]
    shapes = [jax.ShapeDtypeStruct((1, sizes[name]), F32) for name in SMALL_PARAMS]
    res = pl.pallas_call(body, name="small_adamw", in_specs=[VMEM_SPEC] * (1 + 3 * np_), out_specs=[VMEM_SPEC] * (4 * np_ + 1),
                         out_shape=shapes * 4 + [jax.ShapeDtypeStruct((1, 1), F32)], compiler_params=_cp())(tot, *flat)
    return [dict(zip(SMALL_PARAMS, res[k * np_:(k + 1) * np_])) for k in range(4)], res[-1]


def kernel(x, c, w_ada, b_ada, g_pre, w_in, qn_g, kn_g, w_dec_f, w_dec_b, gn_g, w_pa, w_pr, w_out, g_post, loss_target, m_w_ada, m_b_ada, m_g_pre, m_w_in, m_qn_g, m_kn_g, m_w_dec_f, m_w_dec_b, m_gn_g, m_w_pa, m_w_pr, m_w_out, m_g_post, v_w_ada, v_b_ada, v_g_pre, v_w_in, v_qn_g, v_kn_g, v_w_dec_f, v_w_dec_b, v_gn_g, v_w_pa, v_w_pr, v_w_out, v_g_post):
    shards = (w_in[0].T.astype(MXU_DTYPE), jnp.concatenate([w_pa[0].T, w_pr[0].T], axis=1).astype(MXU_DTYPE),
              w_out[0].astype(MXU_DTYPE))
    cs, mod, w_attn = _mod_and_attn_rows(c, w_ada[0], b_ada.reshape(N_CHIP, 3 * D_MODEL // N_CHIP), shards[0])
    started = {}

    def rest_start(dep):
        started["rest"] = _gather_rest_start(shards, _place_own(shards), dep)
        return started["rest"][-1]

    def rest_wait(after):
        return _gather_rest_wait(started["rest"], after)

    kinds = {"early": (1, 2), "late": (0,)}

    def send(name, arrays):
        started[name] = _scatter_start(arrays, kinds[name], "grad_scatter_" + name)
        return started[name][-1]

    grad_x, _, _, _, small = _local_step(x[0], loss_target[0], mod, g_pre, qn_g, kn_g, w_dec_f, w_dec_b,
                                         gn_g, g_post, w_attn, rest_start, rest_wait, send=send)
    small = dict(small)
    small["b_ada"] = small.pop("dmod")
    given = dict(b_ada=(b_ada, m_b_ada, v_b_ada), g_pre=(g_pre, m_g_pre, v_g_pre), g_post=(g_post, m_g_post, v_g_post),
                 gn_g=(gn_g, m_gn_g, v_gn_g), qn_g=(qn_g, m_qn_g, v_qn_g), kn_g=(kn_g, m_kn_g, v_kn_g),
                 w_dec_f=(w_dec_f, m_w_dec_f, v_w_dec_f), w_dec_b=(w_dec_b, m_w_dec_b, v_w_dec_b))
    grads, deltas, new_m, new_v = {}, {}, {}, {}

    def update(name, w, g, m, v, back):
        d, nm, nv = _adamw(w, g, m, v, w.shape[0] // 4, "adamw_" + name)
        grads[name], deltas[name], new_m[name], new_v[name] = back(g), back(d), back(nm), back(nv)
        return d

    lead = lambda a: a[None]
    (g_pt, g_out), (z_pt, z_out) = _scatter_wait(started["early"], kinds["early"], grad_x, "grad_scatter_early_wait")
    r_pt, r_out = _grad_finish((g_pt, g_out), (z_pt, z_out), kinds["early"], "grad_finish_early")
    early = (update("w_pa", w_pa[0], r_pt[:, :512].T, m_w_pa[0], v_w_pa[0], lead),
             update("w_pr", w_pr[0], r_pt[:, 512:].T, m_w_pr[0], v_w_pr[0], lead),
             update("w_out", w_out[0], r_out, m_w_out[0], v_w_out[0], lead))
    tot, g_w_ada = _small_reduce(_pack(small), cs.reshape(N_DEV, D_MODEL), early)
    small_parts, loss = _small_adamw(tot, given)
    for dst, part in zip((grads, deltas, new_m, new_v), small_parts):
        dst.update(part)
    last = update("w_ada", w_ada[0], g_w_ada, m_w_ada[0], v_w_ada[0], lead)

    (g_in_t,), (z_in,) = _scatter_wait(started["late"], kinds["late"], last, "grad_scatter_late_wait")
    (r_in,) = _grad_finish((g_in_t,), (z_in,), kinds["late"], "grad_finish_late")
    tr = lambda a: a[0].T
    update("w_in", tr(w_in), r_in, tr(m_w_in), tr(v_w_in), lambda a: a.T[None])
    order = ("w_ada", "b_ada", "g_pre", "w_in", "qn_g", "kn_g", "w_dec_f", "w_dec_b", "gn_g", "w_pa", "w_pr", "w_out", "g_post")
    return (loss[0, 0], grad_x[None], *[grads[n] for n in order], *[deltas[n] for n in order],
            *[new_m[n] for n in order], *[new_v[n] for n in order])
```

```python
import functools

import jax
import jax.numpy as jnp
import numpy as np
from jax import lax
from jax.experimental import pallas as pl
from jax.experimental.pallas import tpu as pltpu

F32 = jnp.float32
BF16 = jnp.bfloat16
MXU_DTYPE = jnp.bfloat16

D_MODEL = 1024
GRID_W = 64
HEAD_DIM = 64
ROPE_THETA = 10000.0
EPS = 1e-6
RET_HEADS = 4
RET_CHUNK = 256
RET_GROUP = 4
IN_WIDTH = 4864
QA, KA, VA, ZA, QR, KR, VR, ZR, GL = 0, 512, 640, 768, 1280, 1536, 1792, 2304, 2816
ATTN_COLS = ZA
ZA_B, QR_B, KR_B, VR_B, ZR_B, GL_B = (c - ATTN_COLS for c in (ZA, QR, KR, VR, ZR, GL))

ADAM_LR, ADAM_B1, ADAM_B2, ADAM_EPS, ADAM_WD, ADAM_STEP = 0.001, 0.9, 0.999, 1e-08, 0.01, 10

VMEM_LIMIT = 56 * 1024 * 1024
MESH = pl.DeviceIdType.MESH
HIGHEST = lax.Precision.HIGHEST
LOG2E = 1.4426950408889634
LN2 = 0.6931471805599453


def _cp(*sem, **kw):
    if sem:
        kw["dimension_semantics"] = sem
    return pltpu.CompilerParams(vmem_limit_bytes=VMEM_LIMIT, **kw)


def _nn(a, b, **kw):
    return lax.dot_general(a, b, (((1,), (0,)), ((), ())), preferred_element_type=F32, **kw)


def _nt(a, b):
    return lax.dot_general(a, b, (((1,), (1,)), ((), ())), preferred_element_type=F32)


def _tn(a, b):
    return lax.dot_general(a, b, (((0,), (0,)), ((), ())), preferred_element_type=F32)


def _mx(x):
    return x.astype(MXU_DTYPE)


def _lane(shape):
    return lax.broadcasted_iota(jnp.int32, shape, 1)


def _sigmoid(z):
    return 1.0 / (1.0 + jnp.exp(-z))


def _rowsum128(x):
    s = jnp.sum(x, axis=0, keepdims=True)
    out = s[:, 0:128]
    for k in range(1, x.shape[1] // 128):
        out = out + s[:, 128 * k:128 * (k + 1)]
    return out


def _swap16(x):
    return jnp.where((_lane(x.shape) & 16) == 0, pltpu.roll(x, 112, 1), pltpu.roll(x, 16, 1))


def _rope(x, cos, sin):
    return x * cos + _swap16(x) * sin


def _rope_t(d, cos, sin):
    return d * cos + _swap16(d * sin)


def _blockdiag64():
    r = lax.broadcasted_iota(jnp.int32, (128, 128), 0) // 64
    c = lax.broadcasted_iota(jnp.int32, (128, 128), 1) // 64
    return (r == c).astype(BF16)


def _seg64(x, m):
    hi = x.astype(BF16)
    lo = (x - hi.astype(F32)).astype(BF16)
    return _nn(hi, m) + _nn(lo, m)


def _rope_tables(seq):
    t = np.arange(seq)
    row = (t // GRID_W).astype(np.float32)
    col = (t % GRID_W).astype(np.float32)
    half = HEAD_DIM // 2
    inv_freq = (np.float32(ROPE_THETA) ** (-np.arange(0, half, 2, dtype=np.float32) / np.float32(half))).astype(np.float32)
    ar = (row[:, None] * inv_freq[None, :]).astype(np.float32).astype(np.float64)
    ac = (col[:, None] * inv_freq[None, :]).astype(np.float32).astype(np.float64)
    cr, sr, cc, sc = np.cos(ar), np.sin(ar), np.cos(ac), np.sin(ac)
    cos64 = np.concatenate([cr, cr, cc, cc], axis=1)
    sin64 = np.concatenate([-sr, sr, -sc, sc], axis=1)
    return jnp.asarray(np.tile(cos64, (1, 2)), F32), jnp.asarray(np.tile(sin64, (1, 2)), F32)


def _attn_inputs(x, mod, g_pre, w_attn, cos, sin, qg2, kg2):
    seq = x.shape[0]
    bs = 512

    def body(x_ref, mod_ref, g_ref, w_ref, cos_ref, sin_ref, qg_ref, kg_ref, h_ref, pa_ref, qt_ref, kd_ref, vd_ref):
        xv = x_ref[...]
        r = lax.rsqrt(jnp.mean(xv * xv, axis=-1, keepdims=True) + EPS)
        shift = mod_ref[:, 0:D_MODEL]
        scale = mod_ref[:, D_MODEL:2 * D_MODEL]
        hb = ((xv * r) * g_ref[...] * (1.0 + scale) + shift).astype(h_ref.dtype)
        h_ref[...] = hb
        p = _nt(hb, w_ref[...])
        pa_ref[...] = p
        m = _blockdiag64()
        cs, sn = cos_ref[...], sin_ref[...]
        lo = _lane((bs, 128)) < 64
        for pr in range(4):
            q = p[:, QA + 128 * pr:QA + 128 * (pr + 1)]
            r = lax.rsqrt(_seg64(q * q, m) * (1.0 / 64) + EPS)
            qt_ref[:, 128 * pr:128 * (pr + 1)] = (_rope(q * r * qg_ref[...], cs, sn) * (0.125 * LOG2E)).astype(qt_ref.dtype)
        k = p[:, KA:KA + 128]
        r = lax.rsqrt(_seg64(k * k, m) * (1.0 / 64) + EPS)
        kr = _rope(k * r * kg_ref[...], cs, sn)
        ksw = pltpu.roll(kr, 64, 1)
        kd_ref[0] = jnp.where(lo, kr, ksw).astype(kd_ref.dtype)
        kd_ref[1] = jnp.where(lo, ksw, kr).astype(kd_ref.dtype)
        v = p[:, VA:VA + 128]
        vsw = pltpu.roll(v, 64, 1)
        vd_ref[0] = jnp.where(lo, v, vsw).astype(vd_ref.dtype)
        vd_ref[1] = jnp.where(lo, vsw, v).astype(vd_ref.dtype)

    row = lambda w: pl.BlockSpec((bs, w), lambda i: (i, 0))
    fix = lambda a, b: pl.BlockSpec((a, b), lambda i: (0, 0))
    dup = pl.BlockSpec((2, bs, 128), lambda i: (0, i, 0))
    sds = jax.ShapeDtypeStruct
    return pl.pallas_call(
        body, grid=(seq // bs,), name="attn_inputs",
        in_specs=[row(D_MODEL), fix(1, 3 * D_MODEL), fix(1, D_MODEL), fix(ATTN_COLS, D_MODEL), row(128), row(128), fix(1, 128), fix(1, 128)],
        out_specs=[row(D_MODEL), row(ATTN_COLS), row(512), dup, dup],
        out_shape=[sds((seq, D_MODEL), MXU_DTYPE), sds((seq, ATTN_COLS), F32), sds((seq, 512), MXU_DTYPE),
                   sds((2, seq, 128), MXU_DTYPE), sds((2, seq, 128), MXU_DTYPE)],
        compiler_params=_cp("parallel"))(x, mod, g_pre, w_attn, cos, sin, qg2, kg2)


def _in_proj_dx(x, dp, win_t, dout, mod, g_pre, dep=None):
    seq = x.shape[0]
    bs = 512
    deps, dep_specs = _dep_args(dep, 1)

    def body(x_ref, dp_ref, w_ref, dout_ref, mod_ref, g_ref, *rest):
        gx_ref, dshift_ref, dscale_ref, dg_ref = rest[-4:]

        @pl.when(pl.program_id(0) == 0)
        def _():
            dshift_ref[...] = jnp.zeros_like(dshift_ref)
            dscale_ref[...] = jnp.zeros_like(dscale_ref)
            dg_ref[...] = jnp.zeros_like(dg_ref)

        xv = x_ref[...]
        dh = _nn(dp_ref[...], w_ref[...])
        g = g_ref[...]
        r = lax.rsqrt(jnp.mean(xv * xv, axis=-1, keepdims=True) + EPS)
        xn = xv * r
        scale = mod_ref[:, D_MODEL:2 * D_MODEL]
        dshift_ref[...] += jnp.sum(dh, axis=0, keepdims=True)
        dscale_ref[...] += jnp.sum(dh * (xn * g), axis=0, keepdims=True)
        da = dh * (1.0 + scale)
        dg_ref[...] += jnp.sum(da * xn, axis=0, keepdims=True)
        dxn = da * g
        dx = r * (dxn - xn * jnp.mean(dxn * xn, axis=-1, keepdims=True))
        gx_ref[...] = dout_ref[...] + dx

    row = pl.BlockSpec((bs, D_MODEL), lambda i: (i, 0))
    vec = pl.BlockSpec((1, D_MODEL), lambda i: (0, 0))
    return pl.pallas_call(
        body, grid=(seq // bs,), name="in_proj_dx",
        in_specs=[row, pl.BlockSpec((bs, IN_WIDTH), lambda i: (i, 0)), pl.BlockSpec((IN_WIDTH, D_MODEL), lambda i: (0, 0)), row,
                  pl.BlockSpec((1, 3 * D_MODEL), lambda i: (0, 0)), vec] + dep_specs,
        out_specs=[row, vec, vec, vec],
        out_shape=[jax.ShapeDtypeStruct((seq, D_MODEL), F32)] + [jax.ShapeDtypeStruct((1, D_MODEL), F32)] * 3,
        compiler_params=_cp("arbitrary"))(x, dp, win_t, dout, mod, g_pre, *deps)


def _dep_args(dep, grid_rank):
    if dep is None:
        return [], []
    return [dep], [pl.BlockSpec(dep.shape, lambda *_: (0,) * dep.ndim)]


def _matmul(a, b, *, ta, tb, tm, tn, out_dtype, name, dep=None):
    kdim = a.shape[0] if ta else a.shape[1]
    m = a.shape[1] if ta else a.shape[0]
    n = b.shape[0] if tb else b.shape[1]
    assert m % tm == 0 and n % tn == 0
    deps, dep_specs = _dep_args(dep, 2)

    def body(a_ref, b_ref, *rest):
        o_ref = rest[-1]
        dims = (((0 if ta else 1,), (1 if tb else 0,)), ((), ()))
        o_ref[...] = lax.dot_general(a_ref[...], b_ref[...], dims, preferred_element_type=F32).astype(o_ref.dtype)

    a_spec = pl.BlockSpec((kdim, tm), lambda j, i: (0, i)) if ta else pl.BlockSpec((tm, kdim), lambda j, i: (i, 0))
    b_spec = pl.BlockSpec((tn, kdim), lambda j, i: (j, 0)) if tb else pl.BlockSpec((kdim, tn), lambda j, i: (0, j))
    return pl.pallas_call(
        body, grid=(n // tn, m // tm), name=name, in_specs=[a_spec, b_spec] + dep_specs,
        out_specs=pl.BlockSpec((tm, tn), lambda j, i: (i, j)),
        out_shape=jax.ShapeDtypeStruct((m, n), out_dtype), compiler_params=_cp("parallel", "parallel"))(a, b, *deps)


def _in_proj_rest(h, win_t, cos, sin):
    seq = h.shape[0]
    tm = min(1024, seq)
    ncols = IN_WIDTH - ATTN_COLS
    tn = ncols // 2
    assert ZR_B <= tn and ATTN_COLS % 128 == 0 and tn % 128 == 0

    def body(h_ref, w_ref, cos_ref, sin_ref, p_ref, rq_ref, rk_ref, rv_ref):
        p = _nt(h_ref[...], w_ref[...])
        p_ref[...] = p

        @pl.when(pl.program_id(0) == 0)
        def _():
            cs, sn = cos_ref[...], sin_ref[...]
            for pr in range(2):
                sl = slice(128 * pr, 128 * (pr + 1))
                rq_ref[:, sl] = _rope(p[:, QR_B + 128 * pr:QR_B + 128 * (pr + 1)], cs, sn).astype(rq_ref.dtype)
                rk_ref[:, sl] = (_rope(p[:, KR_B + 128 * pr:KR_B + 128 * (pr + 1)], cs, sn) * 0.125).astype(rk_ref.dtype)
            rv_ref[...] = p[:, VR_B:VR_B + 512].astype(rv_ref.dtype)

    nrow = seq // tm
    row = lambda w: pl.BlockSpec((tm, w), lambda j, i: (i, 0))
    park = lambda w: pl.BlockSpec((tm, w), lambda j, i: (jnp.where(j == 0, i, nrow - 1), 0))
    sds = jax.ShapeDtypeStruct
    return pl.pallas_call(
        body, grid=(2, nrow), name="in_proj_rest",
        in_specs=[row(D_MODEL), pl.BlockSpec((pl.Element(tn), pl.Element(D_MODEL)),
                                             lambda j, i: (pl.multiple_of(ATTN_COLS + j * tn, 128), 0)), row(128), row(128)],
        out_specs=[pl.BlockSpec((tm, tn), lambda j, i: (i, j)), park(256), park(256), park(512)],
        out_shape=[sds((seq, ncols), F32), sds((seq, 256), MXU_DTYPE), sds((seq, 256), MXU_DTYPE), sds((seq, 512), MXU_DTYPE)],
        compiler_params=_cp("arbitrary", "arbitrary"))(h, win_t, cos, sin)


def _halves(kd):
    lo = _lane(kd.shape) < 64
    z = jnp.zeros_like(kd)
    return jnp.concatenate([jnp.where(lo, kd, z), jnp.where(lo, z, kd)], axis=0)


def _attn_fwd(qt, kd, vd, dep=None):
    seq = qt.shape[0]
    bq, bk = min(1024, seq), min(1024, seq)
    nk = seq // bk
    deps, dep_specs = _dep_args(dep, 2)

    def body(q_ref, k_ref, v_ref, *rest):
        o_ref, lse_ref, m_scr, l_scr, acc = rest[-5:]
        j = pl.program_id(1)
        m_scr[...] = jnp.full_like(m_scr, -jnp.inf)
        l_scr[...] = jnp.zeros_like(l_scr)
        acc[...] = jnp.zeros_like(acc)
        lo = _lane((bq, 128)) < 64

        def kv_block(n, carry):
            rows = pl.ds(pl.multiple_of(n * bk, bk), bk)
            k2, v2 = _halves(k_ref[rows, :]), _halves(v_ref[rows, :])
            for pr in range(2):
                s2 = _nt(q_ref[:, 128 * pr:128 * (pr + 1)], k2)
                ps, alphas = [], []
                for e in range(2):
                    g = 2 * pr + e
                    s = s2[:, e * bk:(e + 1) * bk]
                    m_prev = m_scr[g]
                    m_next = jnp.maximum(m_prev, jnp.max(s, axis=1, keepdims=True))
                    alpha = jnp.exp2(m_prev - m_next)
                    pe = jnp.exp2(s - jnp.tile(m_next, (1, bk // 128)))
                    l_scr[g] = alpha * l_scr[g] + jnp.sum(pe, axis=1, keepdims=True)
                    m_scr[g] = m_next
                    ps.append(_mx(pe))
                    alphas.append(alpha)
                o2 = _nn(jnp.concatenate(ps, axis=1), v2)
                sl = slice(128 * pr, 128 * (pr + 1))
                acc[:, sl] = acc[:, sl] * jnp.where(lo, alphas[0], alphas[1]) + o2
            return carry

        lax.fori_loop(0, nk, kv_block, 0)
        for pr in range(2):
            sl = slice(128 * pr, 128 * (pr + 1))
            o_ref[:, sl] = acc[:, sl] / jnp.where(lo, l_scr[2 * pr], l_scr[2 * pr + 1])
        for g in range(4):
            lse = jnp.transpose(m_scr[g] + jnp.log2(l_scr[g]))
            lse_ref[pl.ds(4 * j + g, 1), :] = lse[0:1, :]

    return pl.pallas_call(
        body, grid=(seq // bq, 2), name="attn_fwd",
        in_specs=[pl.BlockSpec((bq, 256), lambda i, j: (i, j)), pl.BlockSpec((None, seq, 128), lambda i, j: (j, 0, 0)),
                  pl.BlockSpec((None, seq, 128), lambda i, j: (j, 0, 0))] + dep_specs,
        out_specs=[pl.BlockSpec((bq, 256), lambda i, j: (i, j)), pl.BlockSpec((8, bq), lambda i, j: (0, i))],
        out_shape=[jax.ShapeDtypeStruct((seq, 512), F32), jax.ShapeDtypeStruct((8, seq), F32)],
        scratch_shapes=[pltpu.VMEM((4, bq, 128), F32), pltpu.VMEM((4, bq, 128), F32), pltpu.VMEM((bq, 256), F32)],
        compiler_params=_cp("parallel", "arbitrary"))(qt, kd, vd, *deps)


def _attn_bwd(qt, kd, vd, do, lse, delta, dep=None):
    seq = qt.shape[0]
    bq, bk = min(1024, seq), min(1024, seq)
    nq, nk = seq // bq, seq // bk
    deps, dep_specs = _dep_args(dep, 3)

    def body(q_ref, do_ref, k_ref, v_ref, lse_ref, del_ref, *rest):
        dq_ref, dk_ref, dv_ref = rest[-3:]
        j, i = pl.program_id(0), pl.program_id(1)
        dq_ref[...] = jnp.zeros_like(dq_ref)

        @pl.when(i == 0)
        def _():
            dk_ref[...] = jnp.zeros_like(dk_ref)
            dv_ref[...] = jnp.zeros_like(dv_ref)

        lo = _lane((bk, 128)) < 64
        big = lambda r: jnp.concatenate([jnp.broadcast_to(r[0], (bk, bq)), jnp.broadcast_to(r[1], (bk, bq))], axis=0)

        def kv_block(n, carry):
            rows = pl.ds(pl.multiple_of(n * bk, bk), bk)
            k2, v2 = _halves(k_ref[rows, :]), _halves(v_ref[rows, :])
            for pr in range(2):
                sl = slice(128 * pr, 128 * (pr + 1))
                qp, dop = q_ref[:, sl], do_ref[:, sl]
                s_t = _nt(k2, qp)
                dp_t = _nt(v2, dop)
                ls = [lse_ref[pl.ds(4 * j + 2 * pr + e, 1), :] for e in range(2)]
                dl = [del_ref[pl.ds(4 * j + 2 * pr + e, 1), :] for e in range(2)]
                p_t = jnp.exp2(s_t - big(ls))
                ds_t = _mx(p_t * (dp_t - big(dl)))
                rv = _nn(_mx(p_t), dop)
                rk = _nn(ds_t, qp) * LN2
                dv_ref[rows, :] += jnp.where(lo, rv[:bk], rv[bk:])
                dk_ref[rows, :] += jnp.where(lo, rk[:bk], rk[bk:])
                dq_ref[:, sl] += _tn(ds_t, k2)
            return carry

        lax.fori_loop(0, nk, kv_block, 0)

    return pl.pallas_call(
        body, grid=(2, nq), name="attn_bwd",
        in_specs=[pl.BlockSpec((bq, 256), lambda j, i: (i, j)), pl.BlockSpec((bq, 256), lambda j, i: (i, j)),
                  pl.BlockSpec((None, seq, 128), lambda j, i: (j, 0, 0)), pl.BlockSpec((None, seq, 128), lambda j, i: (j, 0, 0)),
                  pl.BlockSpec((8, bq), lambda j, i: (0, i)), pl.BlockSpec((8, bq), lambda j, i: (0, i))] + dep_specs,
        out_specs=[pl.BlockSpec((bq, 256), lambda j, i: (i, j)), pl.BlockSpec((None, seq, 128), lambda j, i: (j, 0, 0)),
                   pl.BlockSpec((None, seq, 128), lambda j, i: (j, 0, 0))],
        out_shape=[jax.ShapeDtypeStruct((seq, 512), F32), jax.ShapeDtypeStruct((2, seq, 128), F32),
                   jax.ShapeDtypeStruct((2, seq, 128), F32)],
        compiler_params=_cp("arbitrary", "arbitrary"))(qt, do, kd, vd, lse, delta, *deps)


def _ret_tables(lg_f, lg_b, c):
    idx = jnp.arange(c, dtype=F32)
    diff = idx[:, None] - idx[None, :]
    a, b = lg_f[:, None, None], lg_b[:, None, None]
    low, up = (diff >= 0)[None], (diff < 0)[None]
    dmat = jnp.where(low, jnp.exp(a * jnp.maximum(diff, 0.0)[None]), jnp.exp(b * jnp.maximum(-diff, 0.0)[None]))
    dda = jnp.where(low, diff[None] * jnp.exp(a * jnp.maximum(diff, 0.0)[None]), 0.0)
    ddb = jnp.where(up, -diff[None] * jnp.exp(b * jnp.maximum(-diff, 0.0)[None]), 0.0)
    rep = lambda w: jnp.broadcast_to(w[:, :, None], (RET_HEADS, c, 128))
    wqf = rep(jnp.exp(lg_f[:, None] * (idx + 1.0)[None]))
    wqb = rep(jnp.exp(lg_b[:, None] * (c - idx)[None]))
    wkf = rep(jnp.exp(lg_f[:, None] * (c - 1.0 - idx)[None]))
    wkb = rep(jnp.exp(lg_b[:, None] * idx[None]))
    cdf, cdb = jnp.exp(lg_f * c), jnp.exp(lg_b * c)
    dec_fb = jnp.broadcast_to(jnp.concatenate([cdf, cdb])[:, None], (8, 128))
    dec_bf = jnp.broadcast_to(jnp.concatenate([cdb, cdf])[:, None], (8, 128))
    return dict(dmat=dmat, dda=dda, ddb=ddb, wqf=wqf, wqb=wqb, wkf=wkf, wkb=wkb, dec_fb=dec_fb, dec_bf=dec_bf)


def _ret_states(xk, yv, w_fwd, w_rev, dec, name):
    seq = xk.shape[0]
    c = RET_CHUNK
    nc = seq // c
    grp = min(RET_GROUP, nc)
    ns = nc // grp

    def body(xf_ref, yf_ref, xr_ref, yr_ref, wf_ref, wr_ref, dec_ref, sf_ref, sr_ref, st_f, st_r):
        @pl.when(pl.program_id(0) == 0)
        def _():
            st_f[...] = jnp.zeros_like(st_f)
            st_r[...] = jnp.zeros_like(st_r)

        lane = _lane((c, 128))

        def chunk(g, carry):
            for x_ref, y_ref, w_ref, s_ref, st, base, gg in ((xf_ref, yf_ref, wf_ref, sf_ref, st_f, 0, g),
                                                             (xr_ref, yr_ref, wr_ref, sr_ref, st_r, 4, grp - 1 - g)):
                rows = pl.ds(pl.multiple_of(gg * c, c), c)
                for h in range(RET_HEADS):
                    pr, e = h // 2, h % 2
                    half = (lane < 64) if e == 0 else (lane >= 64)
                    s_ref[gg, h] = st[h].astype(s_ref.dtype)
                    xm = jnp.where(half, x_ref[rows, 128 * pr:128 * (pr + 1)].astype(F32) * w_ref[h], 0.0)
                    st[h] = st[h] * dec_ref[base + h:base + h + 1, :] + _tn(_mx(xm), _mx(y_ref[rows, 128 * h:128 * (h + 1)]))
            return carry

        lax.fori_loop(0, grp, chunk, 0)

    xs = lambda f: pl.BlockSpec((grp * c, 256), f)
    ys = lambda f: pl.BlockSpec((grp * c, 512), f)
    fwd, rev = (lambda n: (n, 0)), (lambda n: (ns - 1 - n, 0))
    wsp = pl.BlockSpec((RET_HEADS, c, 128), lambda n: (0, 0, 0))
    return pl.pallas_call(
        body, grid=(ns,), name=name,
        in_specs=[xs(fwd), ys(fwd), xs(rev), ys(rev), wsp, wsp, pl.BlockSpec((8, 128), lambda n: (0, 0))],
        out_specs=[pl.BlockSpec((grp, RET_HEADS, 128, 128), lambda n: (n, 0, 0, 0)),
                   pl.BlockSpec((grp, RET_HEADS, 128, 128), lambda n: (ns - 1 - n, 0, 0, 0))],
        out_shape=[jax.ShapeDtypeStruct((nc, RET_HEADS, 128, 128), MXU_DTYPE)] * 2,
        scratch_shapes=[pltpu.VMEM((RET_HEADS, 128, 128), F32), pltpu.VMEM((RET_HEADS, 128, 128), F32)],
        compiler_params=_cp("arbitrary"))(xk, yv, xk, yv, w_fwd, w_rev, dec)


def _ret_fwd(rq, rk, rv, rf, rb, tb):
    seq = rq.shape[0]
    c = RET_CHUNK
    grp = min(RET_GROUP, seq // c)

    def body(q_ref, k_ref, v_ref, rf_ref, rb_ref, d_ref, wqf_ref, wqb_ref, o_ref):
        lane = _lane((c, 128))

        def chunk(g, carry):
            rows = pl.ds(pl.multiple_of(g * c, c), c)
            for h in range(RET_HEADS):
                pr, e = h // 2, h % 2
                half = (lane < 64) if e == 0 else (lane >= 64)
                sl = slice(128 * pr, 128 * (pr + 1))
                qp, kp = q_ref[rows, sl], k_ref[rows, sl]
                km = jnp.where(half, kp, jnp.zeros_like(kp))
                sd = _mx(_nt(qp, km) * d_ref[h])
                qf = qp.astype(F32)
                o = _nn(sd, v_ref[rows, 128 * h:128 * (h + 1)])
                o = o + _nn(_mx(qf * wqf_ref[h]), rf_ref[g, h]) + _nn(_mx(qf * wqb_ref[h]), rb_ref[g, h])
                o_ref[rows, 128 * h:128 * (h + 1)] = o
            return carry

        lax.fori_loop(0, grp, chunk, 0)

    st = pl.BlockSpec((grp, RET_HEADS, 128, 128), lambda n: (n, 0, 0, 0))
    wsp = pl.BlockSpec((RET_HEADS, c, 128), lambda n: (0, 0, 0))
    return pl.pallas_call(
        body, grid=(seq // (grp * c),), name="ret_fwd",
        in_specs=[pl.BlockSpec((grp * c, 256), lambda n: (n, 0)), pl.BlockSpec((grp * c, 256), lambda n: (n, 0)),
                  pl.BlockSpec((grp * c, 512), lambda n: (n, 0)), st, st, pl.BlockSpec((RET_HEADS, c, c), lambda n: (0, 0, 0)), wsp, wsp],
        out_specs=pl.BlockSpec((grp * c, 512), lambda n: (n, 0)),
        out_shape=jax.ShapeDtypeStruct((seq, 512), F32), compiler_params=_cp("parallel"))(
            rq, rk, rv, rf, rb, tb["dmat"], tb["wqf"], tb["wqb"])


def _ret_bwd(rq, rk, rv, do, rf, rb, hf, hb, tb):
    seq = rq.shape[0]
    c = RET_CHUNK
    grp = min(RET_GROUP, seq // c)

    def body(q_ref, k_ref, v_ref, do_ref, rf_ref, rb_ref, hf_ref, hb_ref, d_ref, dda_ref, ddb_ref,
             wqf_ref, wqb_ref, wkf_ref, wkb_ref, cdec_ref, dq_ref, dk_ref, dv_ref, dlg_ref):
        @pl.when(pl.program_id(0) == 0)
        def _():
            dlg_ref[...] = jnp.zeros_like(dlg_ref)

        lane = _lane((c, 128))
        pos = lax.broadcasted_iota(jnp.int32, (c, 128), 0).astype(F32)

        def chunk(g, carry):
            rows = pl.ds(pl.multiple_of(g * c, c), c)
            for pr in range(2):
                sl = slice(128 * pr, 128 * (pr + 1))
                qp, kp = q_ref[rows, sl], k_ref[rows, sl]
                qf, kf = qp.astype(F32), kp.astype(F32)
                dq_acc = jnp.zeros((c, 128), F32)
                dk_acc = jnp.zeros((c, 128), F32)
                for e in range(2):
                    h = 2 * pr + e
                    half = (lane < 64) if e == 0 else (lane >= 64)
                    hs = slice(128 * h, 128 * (h + 1))
                    km = jnp.where(half, kp, jnp.zeros_like(kp))
                    kmf = km.astype(F32)
                    vh, doh = v_ref[rows, hs], do_ref[rows, hs]
                    rfh, rbh, hfh, hbh = rf_ref[g, h], rb_ref[g, h], hf_ref[g, h], hb_ref[g, h]
                    s = _nt(qp, km)
                    dpm = _nt(doh, vh)
                    ds_b = _mx(dpm * d_ref[h])
                    sd_b = _mx(s * d_ref[h])
                    dq_if = wqf_ref[h] * _nt(doh, rfh)
                    dq_ib = wqb_ref[h] * _nt(doh, rbh)
                    dq_acc = dq_acc + _nn(ds_b, km) + dq_if + dq_ib
                    dk_sf = wkf_ref[h] * _nt(vh, hfh)
                    dk_sb = wkb_ref[h] * _nt(vh, hbh)
                    dk_acc = dk_acc + jnp.where(half, _tn(ds_b, qp), 0.0) + dk_sf + dk_sb
                    dv = _tn(sd_b, doh) + _nn(_mx(kmf * wkf_ref[h]), hfh) + _nn(_mx(kmf * wkb_ref[h]), hbh)
                    dv_ref[rows, hs] = dv.astype(dv_ref.dtype)
                    sdp = s * dpm
                    hr_f = hfh.astype(F32) * rfh.astype(F32)
                    hr_b = hbh.astype(F32) * rbh.astype(F32)
                    da = (_rowsum128(sdp * dda_ref[h]) + _rowsum128((pos + 1.0) * qf * dq_if)
                          + _rowsum128((c - 1.0 - pos) * kf * dk_sf) + cdec_ref[h:h + 1, :] * _rowsum128(hr_f))
                    db = (_rowsum128(sdp * ddb_ref[h]) + _rowsum128((c - pos) * qf * dq_ib)
                          + _rowsum128(pos * kf * dk_sb) + cdec_ref[4 + h:5 + h, :] * _rowsum128(hr_b))
                    dlg_ref[h:h + 1, :] += da
                    dlg_ref[4 + h:5 + h, :] += db
                dq_ref[rows, sl] = dq_acc
                dk_ref[rows, sl] = dk_acc
            return carry

        lax.fori_loop(0, grp, chunk, 0)

    st = pl.BlockSpec((grp, RET_HEADS, 128, 128), lambda n: (n, 0, 0, 0))
    wsp = pl.BlockSpec((RET_HEADS, c, 128), lambda n: (0, 0, 0))
    dsp = pl.BlockSpec((RET_HEADS, c, c), lambda n: (0, 0, 0))
    x256 = pl.BlockSpec((grp * c, 256), lambda n: (n, 0))
    x512 = pl.BlockSpec((grp * c, 512), lambda n: (n, 0))
    cdec = tb["dec_fb"] * float(c)
    return pl.pallas_call(
        body, grid=(seq // (grp * c),), name="ret_bwd",
        in_specs=[x256, x256, x512, x512, st, st, st, st, dsp, dsp, dsp, wsp, wsp, wsp, wsp,
                  pl.BlockSpec((8, 128), lambda n: (0, 0))],
        out_specs=[x256, x256, x512, pl.BlockSpec((8, 128), lambda n: (0, 0))],
        out_shape=[jax.ShapeDtypeStruct((seq, 256), F32), jax.ShapeDtypeStruct((seq, 256), F32),
                   jax.ShapeDtypeStruct((seq, 512), MXU_DTYPE), jax.ShapeDtypeStruct((8, 128), F32)],
        compiler_params=_cp("arbitrary"))(
            rq, rk, rv, do, rf, rb, hf, hb, tb["dmat"], tb["dda"], tb["ddb"], tb["wqf"], tb["wqb"], tb["wkf"], tb["wkb"], cdec)


def _tail(x, tgt, o_att, o_ret, p, mod, g_post, gn_g, wpt, wout):
    seq = x.shape[0]
    bs = 256
    nb = seq // bs

    def body(x_ref, t_ref, oa_ref, or_ref, za_ref, zr_ref, gl_ref, mod_ref, gp_ref, gn_ref, wpt_ref, wout_ref,
             dout_ref, dza_ref, dzr_ref, dgl_ref, doa_ref, dor_ref, del_ref, gout_ref, gpt_ref,
             dgate_ref, dgpost_ref, dgn_ref, loss_ref, gout_acc, gpt_acc):
        i = pl.program_id(0)

        @pl.when(i == 0)
        def _():
            gout_acc[...] = jnp.zeros_like(gout_acc)
            gpt_acc[...] = jnp.zeros_like(gpt_acc)
            dgate_ref[...] = jnp.zeros_like(dgate_ref)
            dgpost_ref[...] = jnp.zeros_like(dgpost_ref)
            dgn_ref[...] = jnp.zeros_like(dgn_ref)
            loss_ref[...] = jnp.zeros_like(loss_ref)

        oa, za, zr = oa_ref[...], za_ref[...], zr_ref[...]
        sga, sgr = _sigmoid(za), _sigmoid(zr)
        sza, szr = za * sga, zr * sgr
        ya_b = _mx(oa * sza)
        ons, rstds = [], []
        for h in range(RET_HEADS):
            oh = or_ref[:, 128 * h:128 * (h + 1)]
            xc = oh - jnp.mean(oh, axis=-1, keepdims=True)
            rstd = lax.rsqrt(jnp.mean(xc * xc, axis=-1, keepdims=True) + EPS)
            ons.append(xc * rstd)
            rstds.append(rstd)
        on = jnp.concatenate(ons, axis=1)
        yn = on * gn_ref[...]
        yr_b = _mx(yn * szr)
        wpa_t, wpr_t = wpt_ref[:, 0:512], wpt_ref[:, 512:1024]
        a_att = _nt(ya_b, wpa_t)
        a_ret = _nt(yr_b, wpr_t)
        gts = _sigmoid(gl_ref[...])
        g_att, g_ret = gts[:, 0:D_MODEL], gts[:, D_MODEL:2 * D_MODEL]
        merged_b = _mx(g_att * a_att + g_ret * a_ret)
        u = _nn(merged_b, wout_ref[...])
        r = lax.rsqrt(jnp.mean(u * u, axis=-1, keepdims=True) + EPS)
        un = u * r
        gpost = gp_ref[...]
        y = un * gpost
        gate = mod_ref[:, 2 * D_MODEL:3 * D_MODEL]
        err = x_ref[...] + gate * y - t_ref[...]
        loss_ref[...] += (0.5 / D_MODEL) * _rowsum128(err * err)
        dout = err * (1.0 / D_MODEL)
        dout_ref[...] = dout
        dgate_ref[...] += jnp.sum(dout * y, axis=0, keepdims=True)
        dy = dout * gate
        dgpost_ref[...] += jnp.sum(dy * un, axis=0, keepdims=True)
        dun = dy * gpost
        du_b = _mx(r * (dun - un * jnp.mean(dun * un, axis=-1, keepdims=True)))
        dmerged = _nt(du_b, wout_ref[...])
        gout_acc[...] += _tn(merged_b, du_b)
        d_att, d_ret = dmerged * g_att, dmerged * g_ret
        dgl_ref[:, 0:D_MODEL] = (d_att * a_att * (1.0 - g_att)).astype(dgl_ref.dtype)
        dgl_ref[:, D_MODEL:2 * D_MODEL] = (d_ret * a_ret * (1.0 - g_ret)).astype(dgl_ref.dtype)
        d_att_b, d_ret_b = _mx(d_att), _mx(d_ret)
        dya = _nn(d_att_b, wpa_t)
        dyr = _nn(d_ret_b, wpr_t)
        gpt_acc[:, 0:512] += _tn(d_att_b, ya_b)
        gpt_acc[:, 512:1024] += _tn(d_ret_b, yr_b)
        dza_ref[...] = (dya * oa * (sga * (1.0 + za * (1.0 - sga)))).astype(dza_ref.dtype)
        doa = dya * sza
        doa_ref[...] = doa.astype(doa_ref.dtype)
        dt = jnp.transpose(doa * oa)
        del_ref[...] = jnp.sum(dt.reshape(8, HEAD_DIM, bs), axis=1)
        dzr_ref[...] = (dyr * yn * (sgr * (1.0 + zr * (1.0 - sgr)))).astype(dzr_ref.dtype)
        dyn = dyr * szr
        dgn_ref[...] += jnp.sum(dyn * on, axis=0, keepdims=True)
        don = dyn * gn_ref[...]
        for h in range(RET_HEADS):
            hs = slice(128 * h, 128 * (h + 1))
            dh, oh = don[:, hs], ons[h]
            doh = rstds[h] * (dh - jnp.mean(dh, axis=-1, keepdims=True) - oh * jnp.mean(dh * oh, axis=-1, keepdims=True))
            dor_ref[:, hs] = doh.astype(dor_ref.dtype)

        @pl.when(i == nb - 1)
        def _():
            gout_ref[...] = gout_acc[...].astype(gout_ref.dtype)
            gpt_ref[...] = gpt_acc[...].astype(gpt_ref.dtype)

    row = lambda w: pl.BlockSpec((bs, w), lambda i: (i, 0))
    el = lambda w, off: pl.BlockSpec((pl.Element(bs), pl.Element(w)), lambda i: (i * bs, off))
    vec = lambda w: pl.BlockSpec((1, w), lambda i: (0, 0))
    full = pl.BlockSpec((D_MODEL, D_MODEL), lambda i: (0, 0))
    sds = jax.ShapeDtypeStruct
    return pl.pallas_call(
        body, grid=(nb,), name="tail",
        in_specs=[row(D_MODEL), row(D_MODEL), row(512), row(512), el(512, ZA_B), el(512, ZR_B), el(2 * D_MODEL, GL_B),
                  vec(3 * D_MODEL), vec(D_MODEL), vec(512), full, full],
        out_specs=[row(D_MODEL), row(512), row(512), row(2 * D_MODEL), row(512), row(512),
                   pl.BlockSpec((8, bs), lambda i: (0, i)), full, full, vec(D_MODEL), vec(D_MODEL), vec(512), vec(128)],
        out_shape=[sds((seq, D_MODEL), F32), sds((seq, 512), MXU_DTYPE), sds((seq, 512), MXU_DTYPE),
                   sds((seq, 2 * D_MODEL), MXU_DTYPE), sds((seq, 512), MXU_DTYPE), sds((seq, 512), MXU_DTYPE),
                   sds((8, seq), F32), sds((D_MODEL, D_MODEL), MXU_DTYPE), sds((D_MODEL, D_MODEL), MXU_DTYPE),
                   sds((1, D_MODEL), F32), sds((1, D_MODEL), F32), sds((1, 512), F32), sds((1, 128), F32)],
        scratch_shapes=[pltpu.VMEM((D_MODEL, D_MODEL), F32), pltpu.VMEM((D_MODEL, D_MODEL), F32)],
        compiler_params=_cp("arbitrary"))(x, tgt, o_att, o_ret, p, p, p, mod, g_post, gn_g, wpt, wout)


def _assemble(p, cos, sin, qg2, kg2, dqt, dkp, dvp, dza, drq, drk, drv, dzr, dgl):
    seq = p.shape[0]
    bs = 512

    def body(p_ref, cos_ref, sin_ref, qg_ref, kg_ref, dqt_ref, dkp_ref, dvp_ref, dza_ref, drq_ref, drk_ref, drv_ref,
             dzr_ref, dgl_ref, dp_ref, dqg_ref, dkg_ref):
        @pl.when(pl.program_id(0) == 0)
        def _():
            dqg_ref[...] = jnp.zeros_like(dqg_ref)
            dkg_ref[...] = jnp.zeros_like(dkg_ref)

        m = _blockdiag64()
        cs, sn = cos_ref[...], sin_ref[...]
        lo = _lane((bs, 128)) < 64

        def norm_bwd(raw, dn, g, dg_ref):
            r = lax.rsqrt(_seg64(raw * raw, m) * (1.0 / 64) + EPS)
            xn = raw * r
            dg_ref[...] += jnp.sum(dn * xn, axis=0, keepdims=True)
            dxn = dn * g
            return r * (dxn - xn * (_seg64(dxn * xn, m) * (1.0 / 64)))

        for pr in range(4):
            sl = slice(128 * pr, 128 * (pr + 1))
            dn = _rope_t(dqt_ref[:, sl] * 0.125, cs, sn)
            dp_ref[:, QA + 128 * pr:QA + 128 * (pr + 1)] = norm_bwd(p_ref[:, sl], dn, qg_ref[...], dqg_ref).astype(dp_ref.dtype)
        fold = lambda a: a + pltpu.roll(a, 64, 1)
        dk = jnp.where(lo, fold(dkp_ref[0]), fold(dkp_ref[1]))
        dp_ref[:, KA:KA + 128] = norm_bwd(p_ref[:, KA:KA + 128], _rope_t(dk, cs, sn), kg_ref[...], dkg_ref).astype(dp_ref.dtype)
        dp_ref[:, VA:VA + 128] = jnp.where(lo, fold(dvp_ref[0]), fold(dvp_ref[1])).astype(dp_ref.dtype)
        dp_ref[:, ZA:ZA + 512] = dza_ref[...]
        for pr in range(2):
            sl = slice(128 * pr, 128 * (pr + 1))
            dp_ref[:, QR + 128 * pr:QR + 128 * (pr + 1)] = _rope_t(drq_ref[:, sl], cs, sn).astype(dp_ref.dtype)
            dp_ref[:, KR + 128 * pr:KR + 128 * (pr + 1)] = _rope_t(drk_ref[:, sl] * 0.125, cs, sn).astype(dp_ref.dtype)
        dp_ref[:, VR:VR + 512] = drv_ref[...]
        dp_ref[:, ZR:ZR + 512] = dzr_ref[...]
        dp_ref[:, GL:GL + 2 * D_MODEL] = dgl_ref[...]

    row = lambda w: pl.BlockSpec((bs, w), lambda i: (i, 0))
    gsp = pl.BlockSpec((1, 128), lambda i: (0, 0))
    dup = pl.BlockSpec((2, bs, 128), lambda i: (0, i, 0))
    return pl.pallas_call(
        body, grid=(seq // bs,), name="assemble_dp",
        in_specs=[row(768), row(128), row(128), gsp, gsp, row(512), dup, dup, row(512), row(256), row(256), row(512),
                  row(512), row(2 * D_MODEL)],
        out_specs=[row(IN_WIDTH), gsp, gsp],
        out_shape=[jax.ShapeDtypeStruct((seq, IN_WIDTH), MXU_DTYPE), jax.ShapeDtypeStruct((1, 128), F32),
                   jax.ShapeDtypeStruct((1, 128), F32)],
        compiler_params=_cp("arbitrary"))(p, cos, sin, qg2, kg2, dqt, dkp, dvp, dza, drq, drk, drv, dzr, dgl)


def _local_step(x, tgt, mod, g_pre, qn_g, kn_g, w_dec_f, w_dec_b, gn_g, g_post, w_attn, rest_start, rest_wait, send=None):
    send = send or (lambda name, arrays: None)
    seq = x.shape[0]
    cos, sin = _rope_tables(seq)
    qg2, kg2 = jnp.tile(qn_g, (1, 2)), jnp.tile(kn_g, (1, 2))
    lg_f = jax.nn.log_sigmoid(w_dec_f[0])
    lg_b = jax.nn.log_sigmoid(w_dec_b[0])
    tb = _ret_tables(lg_f, lg_b, RET_CHUNK)

    h, p_a, qt, kd, vd = _attn_inputs(x, mod, g_pre, w_attn, cos, sin, qg2, kg2)
    o_att, lse = _attn_fwd(qt, kd, vd, dep=rest_start(qt))
    win_t, wpt, wout = rest_wait(o_att)
    p_b, rq, rk, rv = _in_proj_rest(h, win_t, cos, sin)
    rf, rb = _ret_states(rk, rv, tb["wkf"], tb["wkb"], tb["dec_fb"], "ret_states_fwd")
    o_ret = _ret_fwd(rq, rk, rv, rf, rb, tb)
    (dout, dza, dzr, dgl, do_att, do_ret, delta, g_out, g_pt, dgate, dgpost, dgn, loss_l) = _tail(
        x, tgt, o_att, o_ret, p_b, mod, g_post, gn_g, wpt, wout)
    dep = send("early", (g_pt, g_out))
    dqt, dkp, dvp = _attn_bwd(qt, kd, vd, do_att, lse, delta, dep=dep)
    hb, hf = _ret_states(rq, do_ret, tb["wqb"], tb["wqf"], tb["dec_bf"], "ret_states_bwd")
    drq, drk, drv, dlg = _ret_bwd(rq, rk, rv, do_ret, rf, rb, hf, hb, tb)
    dp, dqg, dkg = _assemble(p_a, cos, sin, qg2, kg2, dqt, dkp, dvp, dza, drq, drk, drv, dzr, dgl)
    g_in_t = _matmul(dp, h, ta=True, tb=False, tm=256, tn=D_MODEL, out_dtype=MXU_DTYPE, name="in_proj_dw")
    dep = send("late", (g_in_t,))
    grad_x, dshift, dscale, dgpre = _in_proj_dx(x, dp, win_t, dout, mod, g_pre, dep=dep)

    dlg = jnp.sum(dlg, axis=1)
    small = dict(
        dmod=jnp.concatenate([dshift, dscale, dgate], axis=1),
        g_pre=dgpre, g_post=dgpost, gn_g=dgn,
        qn_g=dqg[:, :64] + dqg[:, 64:], kn_g=dkg[:, :64] + dkg[:, 64:],
        w_dec_f=(dlg[0:4] * jax.nn.sigmoid(-w_dec_f[0]))[None], w_dec_b=(dlg[4:8] * jax.nn.sigmoid(-w_dec_b[0]))[None],
        loss=jnp.sum(loss_l, axis=1, keepdims=True))
    return grad_x, g_in_t, g_pt, g_out, small


N_DEV = 8
N_CHIP = 4
HBM_SPEC = pl.BlockSpec(memory_space=pl.ANY)
VMEM_SPEC = pl.BlockSpec(memory_space=pltpu.VMEM)
SHARD_ROWS = (IN_WIDTH // N_CHIP, D_MODEL // N_CHIP, D_MODEL // N_CHIP)


def _coords():
    return lax.axis_index("x"), lax.axis_index("y"), lax.axis_index("c")


def _peer(k):
    x, y, c = _coords()
    return (1 - x if k & 4 else x, 1 - y if k & 2 else y, 1 - c if k & 1 else c)


def _dev_index(p):
    return 4 * p[0] + 2 * p[1] + p[2]


def _rows(ref, start, size):
    return ref.at[pl.ds(pl.multiple_of(start, 16), size), :]


def _remote(src, dst, ssem, rsem, dev):
    return pltpu.make_async_remote_copy(src_ref=src, dst_ref=dst, send_sem=ssem, recv_sem=rsem, device_id=dev,
                                        device_id_type=MESH)


ATTN_CHUNK = 96


def _mod_and_attn_rows(c, w_ada_s, b4, win_s):
    ncol = w_ada_s.shape[1]
    half = ATTN_COLS // 2
    offs = list(range(0, half, ATTN_CHUNK))
    nq = len(offs)
    rows = lambda ref, cc, off: ref.at[pl.ds(pl.multiple_of(cc * half + off, 16), ATTN_CHUNK), :]

    def body(c_ref, w_ref, b_ref, src, cs_ref, mod_ref, out, modsh, modall, ssem, rsem, lsem, asem, bsem, fsem, gsem, hsem):
        x, y, cc = _coords()
        me = _dev_index((x, y, cc))
        chip = me // 2
        sib = (x, y, 1 - cc)
        cs_ref[me] = c_ref[...]
        sends = []
        for k in range(1, N_DEV):
            cp = _remote(c_ref, cs_ref.at[me], ssem.at[k - 1], rsem.at[k - 1], _peer(k))
            cp.start()
            sends.append(cp)
        own = pltpu.make_async_copy(src.at[pl.ds(0, ATTN_COLS), :], out, lsem.at[0])
        bulk = [_remote(rows(src, cc, off), rows(out, cc, off), asem.at[(k2 - 1) * nq + q], bsem.at[q], (k2 // 2, k2 % 2, cc))
                for k2 in (1, 2) for q, off in enumerate(offs)]
        relay_chip = lambda q: 1 + q % 2

        @pl.when(chip == 0)
        def _():
            own.start()
            for cp in bulk:
                cp.start()

        for k in range(1, N_DEV):
            slot = cs_ref.at[_dev_index(_peer(k))]
            _remote(slot, slot, ssem.at[k - 1], rsem.at[k - 1], _peer(k)).wait_recv()
        cs = jnp.concatenate([cs_ref[d] for d in range(N_DEV)], axis=0)
        ms = _nn(cs * _sigmoid(cs), w_ref[...], precision=HIGHEST) + b_ref[pl.ds(chip, 1), :]
        modsh[...] = ms
        modall[chip] = ms
        for k2 in range(1, N_CHIP):
            cp = _remote(modsh, modall.at[chip], ssem.at[6 + k2], rsem.at[6 + k2], _peer(2 * k2))
            cp.start()
            sends.append(cp)

        @pl.when(chip != 0)
        def _():
            passed = []
            relays = [_remote(rows(out, cc, off), rows(out, cc, off), hsem.at[q], bsem.at[q], (1, 1, cc)) for q, off in enumerate(offs)]
            for q, off in enumerate(offs):
                got = rows(out, cc, off)
                _remote(got, got, asem.at[q], bsem.at[q], (0, 0, cc)).wait_recv()
                cp = _remote(got, got, fsem.at[q], gsem.at[q], sib)
                cp.start()
                passed.append(cp)
                pl.when(chip == relay_chip(q))(relays[q].start)
            for q, off in enumerate(offs):
                got = rows(out, 1 - cc, off)
                _remote(got, got, fsem.at[q], gsem.at[q], sib).wait_recv()
            for cp in passed:
                cp.wait_send()
            for q in range(nq):
                pl.when(chip == relay_chip(q))(relays[q].wait_send)

        for k2 in range(1, N_CHIP):
            slot = modall.at[_dev_index(_peer(2 * k2)) // 2]
            _remote(slot, slot, ssem.at[6 + k2], rsem.at[6 + k2], _peer(2 * k2)).wait_recv()
        for jj in range(N_CHIP):
            mod_ref[:, ncol * jj:ncol * (jj + 1)] = modall[jj, pl.ds(me, 1), :]
        for cp in sends:
            cp.wait_send()

        @pl.when(chip == 0)
        def _():
            for cp in bulk:
                cp.wait_send()
            own.wait()

    dma = pltpu.SemaphoreType.DMA
    return pl.pallas_call(
        body, name="mod_and_attn_rows", in_specs=[VMEM_SPEC] * 4, out_specs=[VMEM_SPEC, VMEM_SPEC, HBM_SPEC],
        out_shape=[jax.ShapeDtypeStruct((N_DEV, 1, D_MODEL), F32), jax.ShapeDtypeStruct((1, N_CHIP * ncol), F32),
                   jax.ShapeDtypeStruct((ATTN_COLS, win_s.shape[1]), win_s.dtype)],
        scratch_shapes=[pltpu.VMEM((N_DEV, ncol), F32), pltpu.VMEM((N_CHIP, N_DEV, ncol), F32), dma((10,)), dma((10,)),
                        dma((1,)), dma((2 * nq,)), dma((nq,)), dma((nq,)), dma((nq,)), dma((nq,))],
        compiler_params=_cp())(c, w_ada_s, b4, win_s)


GATHER_CHUNK = 32


def _chunks(kinds, chunk):
    out = []
    for t, kind in enumerate(kinds):
        half = SHARD_ROWS[kind] // 2
        step = chunk if half % chunk == 0 else half
        out += [(t, off, step) for off in range(0, half, step)]
    return out


SEM_SPEC = pl.BlockSpec(memory_space=pltpu.SEMAPHORE)
HBM_ONLY = pl.BlockSpec(memory_space=pltpu.HBM)
DATAFLOW = pltpu.SideEffectType.DATAFLOW_SIDE_EFFECTING


def _place_own(shards):
    nt = len(shards)

    def body(*refs):
        srcs, outs, lsem = refs[:nt], refs[nt:2 * nt], refs[2 * nt]
        x, y, _ = _coords()
        chip = 2 * x + y
        local = [pltpu.make_async_copy(srcs[t], _rows(outs[t], chip * SHARD_ROWS[t], SHARD_ROWS[t]), lsem.at[t]) for t in range(nt)]
        for cp in local:
            cp.start()
        for cp in local:
            cp.wait()

    return pl.pallas_call(
        body, name="place_own_shard", in_specs=[VMEM_SPEC] * nt, out_specs=[HBM_SPEC] * nt,
        out_shape=[jax.ShapeDtypeStruct((N_CHIP * SHARD_ROWS[t], s.shape[1]), s.dtype) for t, s in enumerate(shards)],
        scratch_shapes=[pltpu.SemaphoreType.DMA((nt,))], compiler_params=_cp())(*shards)


def _gather_rest_start(shards, zones, dep):
    n = len(shards)

    def body(*refs):
        srcs, lands = refs[:n], refs[n:2 * n]
        ssem, rsem = refs[2 * n + 1], refs[2 * n + 2]
        token = refs[-1]
        x, y, _ = _coords()
        chip = 2 * x + y
        for k2 in range(1, N_CHIP):
            for t in range(n):
                q = (k2 - 1) * n + t
                _remote(srcs[t], _rows(lands[t], chip * SHARD_ROWS[t], SHARD_ROWS[t]), ssem.at[q], rsem.at[q], _peer(2 * k2)).start()
        token[...] = jnp.zeros_like(token)

    hbm = lambda a: pltpu.with_memory_space_constraint(a, pltpu.HBM)
    dma = pltpu.SemaphoreType.DMA
    outs = pl.pallas_call(
        body, name="gather_rest", in_specs=[HBM_ONLY] * (2 * n) + [HBM_SPEC],
        out_specs=[SEM_SPEC, SEM_SPEC] + [HBM_ONLY] * (2 * n) + [VMEM_SPEC],
        out_shape=[dma((3 * n,)), dma((3 * n,))] + [pltpu.HBM(a.shape, a.dtype) for a in list(shards) + list(zones)]
        + [jax.ShapeDtypeStruct((8, 128), F32)],
        input_output_aliases={i: 2 + i for i in range(2 * n)},
        compiler_params=pltpu.CompilerParams(has_side_effects=DATAFLOW))(*[hbm(a) for a in list(shards) + list(zones)], dep)
    return outs[0], outs[1], outs[2:2 + n], outs[2 + n:2 + 2 * n], outs[-1]


def _gather_rest_wait(started, after):
    ssem, rsem, shards, zones, _ = started
    n = len(shards)

    def body(*refs):
        srcs, lands = refs[:n], refs[n:2 * n]
        ssem_ref, rsem_ref = refs[2 * n], refs[2 * n + 1]
        for k2 in range(1, N_CHIP):
            pchip = _dev_index(_peer(2 * k2)) // 2
            for t in range(n):
                q = (k2 - 1) * n + t
                cp = _remote(srcs[t], _rows(lands[t], pchip * SHARD_ROWS[t], SHARD_ROWS[t]), ssem_ref.at[q], rsem_ref.at[q], _peer(2 * k2))
                cp.wait_send()
                cp.wait_recv()

    outs = pl.pallas_call(
        body, name="gather_rest_wait", in_specs=[HBM_ONLY] * (2 * n) + [SEM_SPEC, SEM_SPEC, HBM_SPEC], out_specs=[HBM_ONLY] * (2 * n),
        out_shape=[pltpu.HBM(a.shape, a.dtype) for a in list(shards) + list(zones)],
        input_output_aliases={i: i for i in range(2 * n)},
        compiler_params=pltpu.CompilerParams(has_side_effects=DATAFLOW))(*shards, *zones, ssem, rsem, after)
    return outs[n:]


def _reduced_by(ref, t, dev):
    return _rows(ref, (dev // 2) * SHARD_ROWS[t] + (dev % 2) * (SHARD_ROWS[t] // 2), SHARD_ROWS[t] // 2)


def _scatter_start(grads, kinds, name):
    n = len(grads)

    def body(*refs):
        srcs, lands = refs[:n], refs[n:2 * n]
        ssem, rsem = refs[2 * n], refs[2 * n + 1]
        token = refs[-1]
        me = _dev_index(_coords())
        for k in range(1, N_DEV):
            for i, t in enumerate(kinds):
                q = (k - 1) * n + i
                _remote(_reduced_by(srcs[i], t, _dev_index(_peer(k))), lands[i].at[me], ssem.at[q], rsem.at[q], _peer(k)).start()
        token[...] = jnp.zeros_like(token)

    zones = [lax.empty((N_DEV, SHARD_ROWS[t] // 2, g.shape[1]), g.dtype) for g, t in zip(grads, kinds)]
    hbm = lambda a: pltpu.with_memory_space_constraint(a, pltpu.HBM)
    dma = pltpu.SemaphoreType.DMA
    outs = pl.pallas_call(
        body, name=name, in_specs=[HBM_ONLY] * (2 * n), out_specs=[SEM_SPEC, SEM_SPEC] + [HBM_ONLY] * (2 * n) + [VMEM_SPEC],
        out_shape=[dma((7 * n,)), dma((7 * n,))] + [pltpu.HBM(a.shape, a.dtype) for a in list(grads) + zones]
        + [jax.ShapeDtypeStruct((8, 128), F32)],
        input_output_aliases={i: 2 + i for i in range(2 * n)},
        compiler_params=pltpu.CompilerParams(has_side_effects=DATAFLOW))(*[hbm(a) for a in list(grads) + zones])
    return outs[0], outs[1], outs[2:2 + n], outs[2 + n:2 + 2 * n], outs[-1]


def _scatter_wait(started, kinds, after, name):
    ssem, rsem, grads, zones, _ = started
    n = len(grads)

    def body(*refs):
        srcs, lands = refs[:n], refs[n:2 * n]
        ssem_ref, rsem_ref = refs[2 * n], refs[2 * n + 1]
        for k in range(1, N_DEV):
            peer = _dev_index(_peer(k))
            for i, t in enumerate(kinds):
                q = (k - 1) * n + i
                cp = _remote(_reduced_by(srcs[i], t, peer), lands[i].at[peer], ssem_ref.at[q], rsem_ref.at[q], _peer(k))
                cp.wait_send()
                cp.wait_recv()

    outs = pl.pallas_call(
        body, name=name, in_specs=[HBM_ONLY] * (2 * n) + [SEM_SPEC, SEM_SPEC, HBM_SPEC], out_specs=[HBM_ONLY] * (2 * n),
        out_shape=[pltpu.HBM(a.shape, a.dtype) for a in list(grads) + list(zones)],
        input_output_aliases={i: i for i in range(2 * n)},
        compiler_params=pltpu.CompilerParams(has_side_effects=DATAFLOW))(*grads, *zones, ssem, rsem, after)
    return outs[:n], outs[n:]


def _grad_finish(grads, zones, kinds, name):
    nt = len(grads)
    pieces = _chunks(kinds, GATHER_CHUNK)
    npc = len(pieces)
    shard = [SHARD_ROWS[k] for k in kinds]

    def body(*refs):
        srcs, lands, outs = refs[:nt], refs[nt:2 * nt], refs[2 * nt:3 * nt]
        slots, sums = refs[3 * nt:4 * nt], refs[4 * nt:5 * nt]
        lsem, osem, xsem, ysem = refs[5 * nt:]
        x, y, c = _coords()
        me = _dev_index((x, y, c))
        sib = (x, y, 1 - c)
        loads = [pltpu.make_async_copy(_reduced_by(srcs[t], kinds[t], me), slots[t].at[me], lsem.at[t]) for t in range(nt)]
        for k in range(1, N_DEV):
            peer = _dev_index(_peer(k))
            loads += [pltpu.make_async_copy(lands[t].at[peer], slots[t].at[peer], lsem.at[k * nt + t]) for t in range(nt)]
        for cp in loads:
            cp.start()
        for cp in loads:
            cp.wait()
        sends = []
        place = lambda t, cc, off, n: _rows(outs[t], cc * (shard[t] // 2) + off, n)
        stores = []
        for q, (t, off, n) in enumerate(pieces):
            r = pl.ds(off, n)
            acc = slots[t][0, r, :].astype(F32)
            for d in range(1, N_DEV):
                acc = acc + slots[t][d, r, :].astype(F32)
            sums[t][r, :] = acc
            keep = pltpu.make_async_copy(sums[t].at[r, :], place(t, c, off, n), osem.at[q])
            give = _remote(sums[t].at[r, :], place(t, c, off, n), xsem.at[q], ysem.at[q], sib)
            keep.start()
            give.start()
            stores.append(keep)
            sends.append(give)
        for q, (t, off, n) in enumerate(pieces):
            got = place(t, 1 - c, off, n)
            _remote(got, got, xsem.at[q], ysem.at[q], sib).wait_recv()
        for cp in sends:
            cp.wait_send()
        for cp in stores:
            cp.wait()

    dma = pltpu.SemaphoreType.DMA
    return pl.pallas_call(
        body, name=name, in_specs=[HBM_SPEC] * (2 * nt), out_specs=[HBM_SPEC] * nt,
        out_shape=[jax.ShapeDtypeStruct((shard[t], g.shape[1]), F32) for t, g in enumerate(grads)],
        scratch_shapes=([pltpu.VMEM((N_DEV, shard[t] // 2, g.shape[1]), g.dtype) for t, g in enumerate(grads)]
                        + [pltpu.VMEM((shard[t] // 2, g.shape[1]), F32) for t, g in enumerate(grads)]
                        + [dma((N_DEV * nt,)), dma((npc,)), dma((npc,)), dma((npc,))]),
        compiler_params=_cp())(*grads, *zones)


def _adam_math(w, g, m, v):
    m = ADAM_B1 * m + (1.0 - ADAM_B1) * g
    v = ADAM_B2 * v + (1.0 - ADAM_B2) * (g * g)
    m_hat = m / (1.0 - ADAM_B1 ** ADAM_STEP)
    v_hat = v / (1.0 - ADAM_B2 ** ADAM_STEP)
    return -ADAM_LR * (m_hat / (jnp.sqrt(v_hat) + ADAM_EPS) + ADAM_WD * w), m, v


def _adamw(w, g, m, v, rb, name):
    rows, cols = w.shape

    def body(w_ref, g_ref, m_ref, v_ref, d_ref, nm_ref, nv_ref):
        d_ref[...], nm_ref[...], nv_ref[...] = _adam_math(w_ref[...], g_ref[...], m_ref[...], v_ref[...])

    spec = pl.BlockSpec((rb, cols), lambda i: (i, 0))
    return pl.pallas_call(body, grid=(rows // rb,), name=name, in_specs=[spec] * 4, out_specs=[spec] * 3,
                          out_shape=[jax.ShapeDtypeStruct(w.shape, F32)] * 3, compiler_params=_cp("parallel"))(w, g, m, v)


PACK = (("b_ada", 3 * D_MODEL), ("g_pre", D_MODEL), ("g_post", D_MODEL), ("gn_g", 512), ("qn_g", 64), ("kn_g", 64),
        ("w_dec_f", 4), ("w_dec_b", 4), ("loss", 1))
PACK_OFFSETS = {}
PACK_WIDTH = 0
for _name, _n in PACK:
    PACK_OFFSETS[_name] = PACK_WIDTH
    PACK_WIDTH += -(-_n // 128) * 128
SMALL_PARAMS = tuple(name for name, _ in PACK if name != "loss")


def _pack(parts):
    cols = []
    for name, n in PACK:
        pad = -(-n // 128) * 128 - n
        cols.append(parts[name].reshape(1, n).astype(F32))
        if pad:
            cols.append(jnp.zeros((1, pad), F32))
    return jnp.concatenate(cols, axis=1)


def _small_reduce(vec, cs, deps):
    ncol = 3 * D_MODEL // N_CHIP

    def body(vec_ref, cs_ref, *rest):
        tot_ref, gwa_ref, gat, ssem, rsem = rest[-5:]
        me = _dev_index(_coords())
        chip = me // 2
        gat[me] = vec_ref[...]
        sends = []
        for k in range(1, N_DEV):
            cp = _remote(vec_ref, gat.at[me], ssem.at[k - 1], rsem.at[k - 1], _peer(k))
            cp.start()
            sends.append(cp)
        for k in range(1, N_DEV):
            slot = gat.at[_dev_index(_peer(k))]
            _remote(slot, slot, ssem.at[k - 1], rsem.at[k - 1], _peer(k)).wait_recv()
        rows = [gat[d] for d in range(N_DEV)]
        tot = rows[0]
        for d in range(1, N_DEV):
            tot = tot + rows[d]
        tot_ref[...] = tot
        cs = cs_ref[...]
        ca_t = jnp.transpose(jnp.concatenate([cs * _sigmoid(cs), jnp.zeros((128 - N_DEV, D_MODEL), F32)], axis=0))
        dm = jnp.concatenate(rows + [jnp.zeros((128 - N_DEV, PACK_WIDTH), F32)], axis=0)
        for jj in range(N_CHIP):
            @pl.when(chip == jj)
            def _():
                gwa_ref[...] = _nn(ca_t, dm[:, ncol * jj:ncol * (jj + 1)], precision=HIGHEST)
        for cp in sends:
            cp.wait_send()

    return pl.pallas_call(
        body, name="small_reduce", in_specs=[VMEM_SPEC, VMEM_SPEC] + [HBM_SPEC] * len(deps), out_specs=[VMEM_SPEC] * 2,
        out_shape=[jax.ShapeDtypeStruct((1, PACK_WIDTH), F32), jax.ShapeDtypeStruct((D_MODEL, ncol), F32)],
        scratch_shapes=[pltpu.VMEM((N_DEV, 1, PACK_WIDTH), F32), pltpu.SemaphoreType.DMA((7,)), pltpu.SemaphoreType.DMA((7,))],
        compiler_params=_cp())(vec, cs, *deps)


def _small_adamw(tot, given):
    sizes = dict(PACK)
    np_ = len(SMALL_PARAMS)

    def body(*refs):
        tot_ref = refs[0]
        wmv = refs[1:1 + 3 * np_]
        outs = refs[1 + 3 * np_:1 + 7 * np_]
        loss_ref = refs[-1]
        for i, name in enumerate(SMALL_PARAMS):
            off, n = PACK_OFFSETS[name], sizes[name]
            g = tot_ref[:, off:off + n]
            w_ref, m_ref, v_ref = wmv[3 * i:3 * i + 3]
            outs[i][...] = g
            outs[np_ + i][...], outs[2 * np_ + i][...], outs[3 * np_ + i][...] = _adam_math(w_ref[...], g, m_ref[...], v_ref[...])
        loss_ref[...] = tot_ref[:, PACK_OFFSETS["loss"]:PACK_OFFSETS["loss"] + 1]

    flat = [a for name in SMALL_PARAMS for a in given[name]]
    shapes = [jax.ShapeDtypeStruct((1, sizes[name]), F32) for name in SMALL_PARAMS]
    res = pl.pallas_call(body, name="small_adamw", in_specs=[VMEM_SPEC] * (1 + 3 * np_), out_specs=[VMEM_SPEC] * (4 * np_ + 1),
                         out_shape=shapes * 4 + [jax.ShapeDtypeStruct((1, 1), F32)], compiler_params=_cp())(tot, *flat)
    return [dict(zip(SMALL_PARAMS, res[k * np_:(k + 1) * np_])) for k in range(4)], res[-1]


def kernel(x, c, w_ada, b_ada, g_pre, w_in, qn_g, kn_g, w_dec_f, w_dec_b, gn_g, w_pa, w_pr, w_out, g_post, loss_target, m_w_ada, m_b_ada, m_g_pre, m_w_in, m_qn_g, m_kn_g, m_w_dec_f, m_w_dec_b, m_gn_g, m_w_pa, m_w_pr, m_w_out, m_g_post, v_w_ada, v_b_ada, v_g_pre, v_w_in, v_qn_g, v_kn_g, v_w_dec_f, v_w_dec_b, v_gn_g, v_w_pa, v_w_pr, v_w_out, v_g_post):
    shards = (w_in[0].T.astype(MXU_DTYPE), jnp.concatenate([w_pa[0].T, w_pr[0].T], axis=1).astype(MXU_DTYPE),
              w_out[0].astype(MXU_DTYPE))
    cs, mod, w_attn = _mod_and_attn_rows(c, w_ada[0], b_ada.reshape(N_CHIP, 3 * D_MODEL // N_CHIP), shards[0])
    started = {}

    def rest_start(dep):
        started["rest"] = _gather_rest_start(shards, _place_own(shards), dep)
        return started["rest"][-1]

    def rest_wait(after):
        return _gather_rest_wait(started["rest"], after)

    kinds = {"early": (1, 2), "late": (0,)}

    def send(name, arrays):
        started[name] = _scatter_start(arrays, kinds[name], "grad_scatter_" + name)
        return started[name][-1]

    grad_x, _, _, _, small = _local_step(x[0], loss_target[0], mod, g_pre, qn_g, kn_g, w_dec_f, w_dec_b,
                                         gn_g, g_post, w_attn, rest_start, rest_wait, send=send)
    small = dict(small)
    small["b_ada"] = small.pop("dmod")
    given = dict(b_ada=(b_ada, m_b_ada, v_b_ada), g_pre=(g_pre, m_g_pre, v_g_pre), g_post=(g_post, m_g_post, v_g_post),
                 gn_g=(gn_g, m_gn_g, v_gn_g), qn_g=(qn_g, m_qn_g, v_qn_g), kn_g=(kn_g, m_kn_g, v_kn_g),
                 w_dec_f=(w_dec_f, m_w_dec_f, v_w_dec_f), w_dec_b=(w_dec_b, m_w_dec_b, v_w_dec_b))
    grads, deltas, new_m, new_v = {}, {}, {}, {}

    def update(name, w, g, m, v, back):
        d, nm, nv = _adamw(w, g, m, v, w.shape[0] // 4, "adamw_" + name)
        grads[name], deltas[name], new_m[name], new_v[name] = back(g), back(d), back(nm), back(nv)
        return d

    lead = lambda a: a[None]
    (g_pt, g_out), (z_pt, z_out) = _scatter_wait(started["early"], kinds["early"], grad_x, "grad_scatter_early_wait")
    r_pt, r_out = _grad_finish((g_pt, g_out), (z_pt, z_out), kinds["early"], "grad_finish_early")
    early = (update("w_pa", w_pa[0], r_pt[:, :512].T, m_w_pa[0], v_w_pa[0], lead),
             update("w_pr", w_pr[0], r_pt[:, 512:].T, m_w_pr[0], v_w_pr[0], lead),
             update("w_out", w_out[0], r_out, m_w_out[0], v_w_out[0], lead))
    tot, g_w_ada = _small_reduce(_pack(small), cs.reshape(N_DEV, D_MODEL), early)
    small_parts, loss = _small_adamw(tot, given)
    for dst, part in zip((grads, deltas, new_m, new_v), small_parts):
        dst.update(part)
    last = update("w_ada", w_ada[0], g_w_ada, m_w_ada[0], v_w_ada[0], lead)

    (g_in_t,), (z_in,) = _scatter_wait(started["late"], kinds["late"], last, "grad_scatter_late_wait")
    (r_in,) = _grad_finish((g_in_t,), (z_in,), kinds["late"], "grad_finish_late")
    tr = lambda a: a[0].T
    update("w_in", tr(w_in), r_in, tr(m_w_in), tr(v_w_in), lambda a: a.T[None])
    order = ("w_ada", "b_ada", "g_pre", "w_in", "qn_g", "kn_g", "w_dec_f", "w_dec_b", "gn_g", "w_pa", "w_pr", "w_out", "g_post")
    return (loss[0, 0], grad_x[None], *[grads[n] for n in order], *[deltas[n] for n in order],
            *[new_m[n] for n in order], *[new_v[n] for n in order])
```

```python
import jax
import jax.numpy as jnp
import numpy as np
from jax import lax
from jax.experimental import pallas as pl
from jax.experimental.pallas import tpu as pltpu

F32 = jnp.float32
BF16 = jnp.bfloat16
MXU_DTYPE = jnp.bfloat16

D_MODEL = 1024
GRID_W = 64
HEAD_DIM = 64
ROPE_THETA = 10000.0
EPS = 1e-6
RET_HEADS = 4
RET_CHUNK = 256
RET_GROUP = 4
IN_WIDTH = 4864
QA, KA, VA, ZA, QR, KR, VR, ZR, GL = 0, 512, 640, 768, 1280, 1536, 1792, 2304, 2816
ATTN_COLS = ZA
ZA_B, QR_B, KR_B, VR_B, ZR_B, GL_B = (c - ATTN_COLS for c in (ZA, QR, KR, VR, ZR, GL))

ADAM_LR, ADAM_B1, ADAM_B2, ADAM_EPS, ADAM_WD, ADAM_STEP = 0.001, 0.9, 0.999, 1e-08, 0.01, 10

VMEM_LIMIT = 56 * 1024 * 1024
MESH = pl.DeviceIdType.MESH
HIGHEST = lax.Precision.HIGHEST
LOG2E = 1.4426950408889634
LN2 = 0.6931471805599453


def _cp(*sem, **kw):
    if sem:
        kw["dimension_semantics"] = sem
    return pltpu.CompilerParams(vmem_limit_bytes=VMEM_LIMIT, **kw)


def _nn(a, b, **kw):
    return lax.dot_general(a, b, (((1,), (0,)), ((), ())), preferred_element_type=F32, **kw)


def _nt(a, b):
    return lax.dot_general(a, b, (((1,), (1,)), ((), ())), preferred_element_type=F32)


def _tn(a, b):
    return lax.dot_general(a, b, (((0,), (0,)), ((), ())), preferred_element_type=F32)


def _mx(x):
    return x.astype(MXU_DTYPE)


def _lane(shape):
    return lax.broadcasted_iota(jnp.int32, shape, 1)


def _sigmoid(z):
    return 1.0 / (1.0 + jnp.exp(-z))


def _rowsum128(x):
    s = jnp.sum(x, axis=0, keepdims=True)
    out = s[:, 0:128]
    for k in range(1, x.shape[1] // 128):
        out = out + s[:, 128 * k:128 * (k + 1)]
    return out


def _swap16(x):
    return jnp.where((_lane(x.shape) & 16) == 0, pltpu.roll(x, 112, 1), pltpu.roll(x, 16, 1))


def _rope(x, cos, sin):
    return x * cos + _swap16(x) * sin


def _rope_t(d, cos, sin):
    return d * cos + _swap16(d * sin)


def _blockdiag64():
    r = lax.broadcasted_iota(jnp.int32, (128, 128), 0) // 64
    c = lax.broadcasted_iota(jnp.int32, (128, 128), 1) // 64
    return (r == c).astype(BF16)


def _seg64(x, m):
    hi = x.astype(BF16)
    lo = (x - hi.astype(F32)).astype(BF16)
    return _nn(hi, m) + _nn(lo, m)


def _rope_tables(seq):
    t = np.arange(seq)
    row = (t // GRID_W).astype(np.float32)
    col = (t % GRID_W).astype(np.float32)
    half = HEAD_DIM // 2
    inv_freq = (np.float32(ROPE_THETA) ** (-np.arange(0, half, 2, dtype=np.float32) / np.float32(half))).astype(np.float32)
    ar = (row[:, None] * inv_freq[None, :]).astype(np.float32).astype(np.float64)
    ac = (col[:, None] * inv_freq[None, :]).astype(np.float32).astype(np.float64)
    cr, sr, cc, sc = np.cos(ar), np.sin(ar), np.cos(ac), np.sin(ac)
    cos64 = np.concatenate([cr, cr, cc, cc], axis=1)
    sin64 = np.concatenate([-sr, sr, -sc, sc], axis=1)
    return jnp.asarray(np.tile(cos64, (1, 2)), F32), jnp.asarray(np.tile(sin64, (1, 2)), F32)


def _attn_inputs(x, mod, g_pre, w_attn, cos, sin, qg2, kg2):
    seq = x.shape[0]
    bs = 512

    def body(x_ref, mod_ref, g_ref, w_ref, cos_ref, sin_ref, qg_ref, kg_ref, h_ref, pa_ref, qt_ref, kd_ref, vd_ref):
        xv = x_ref[...]
        r = lax.rsqrt(jnp.mean(xv * xv, axis=-1, keepdims=True) + EPS)
        shift = mod_ref[:, 0:D_MODEL]
        scale = mod_ref[:, D_MODEL:2 * D_MODEL]
        hb = ((xv * r) * g_ref[...] * (1.0 + scale) + shift).astype(h_ref.dtype)
        h_ref[...] = hb
        p = _nt(hb, w_ref[...])
        pa_ref[...] = p
        m = _blockdiag64()
        cs, sn = cos_ref[...], sin_ref[...]
        lo = _lane((bs, 128)) < 64
        for pr in range(4):
            q = p[:, QA + 128 * pr:QA + 128 * (pr + 1)]
            r = lax.rsqrt(_seg64(q * q, m) * (1.0 / 64) + EPS)
            qt_ref[:, 128 * pr:128 * (pr + 1)] = (_rope(q * r * qg_ref[...], cs, sn) * (0.125 * LOG2E)).astype(qt_ref.dtype)
        k = p[:, KA:KA + 128]
        r = lax.rsqrt(_seg64(k * k, m) * (1.0 / 64) + EPS)
        kr = _rope(k * r * kg_ref[...], cs, sn)
        ksw = pltpu.roll(kr, 64, 1)
        kd_ref[0] = jnp.where(lo, kr, ksw).astype(kd_ref.dtype)
        kd_ref[1] = jnp.where(lo, ksw, kr).astype(kd_ref.dtype)
        v = p[:, VA:VA + 128]
        vsw = pltpu.roll(v, 64, 1)
        vd_ref[0] = jnp.where(lo, v, vsw).astype(vd_ref.dtype)
        vd_ref[1] = jnp.where(lo, vsw, v).astype(vd_ref.dtype)

    row = lambda w: pl.BlockSpec((bs, w), lambda i: (i, 0))
    fix = lambda a, b: pl.BlockSpec((a, b), lambda i: (0, 0))
    dup = pl.BlockSpec((2, bs, 128), lambda i: (0, i, 0))
    sds = jax.ShapeDtypeStruct
    return pl.pallas_call(
        body, grid=(seq // bs,), name="attn_inputs",
        in_specs=[row(D_MODEL), fix(1, 3 * D_MODEL), fix(1, D_MODEL), fix(ATTN_COLS, D_MODEL), row(128), row(128), fix(1, 128), fix(1, 128)],
        out_specs=[row(D_MODEL), row(ATTN_COLS), row(512), dup, dup],
        out_shape=[sds((seq, D_MODEL), MXU_DTYPE), sds((seq, ATTN_COLS), F32), sds((seq, 512), MXU_DTYPE),
                   sds((2, seq, 128), MXU_DTYPE), sds((2, seq, 128), MXU_DTYPE)],
        compiler_params=_cp("parallel"))(x, mod, g_pre, w_attn, cos, sin, qg2, kg2)


def _in_proj_dx(x, dp, win_t, dout, mod, g_pre, dep=None):
    seq = x.shape[0]
    bs = 512
    deps, dep_specs = _dep_args(dep, 1)

    def body(x_ref, dp_ref, w_ref, dout_ref, mod_ref, g_ref, *rest):
        gx_ref, dshift_ref, dscale_ref, dg_ref = rest[-4:]

        @pl.when(pl.program_id(0) == 0)
        def _():
            dshift_ref[...] = jnp.zeros_like(dshift_ref)
            dscale_ref[...] = jnp.zeros_like(dscale_ref)
            dg_ref[...] = jnp.zeros_like(dg_ref)

        xv = x_ref[...]
        dh = _nn(dp_ref[...], w_ref[...])
        g = g_ref[...]
        r = lax.rsqrt(jnp.mean(xv * xv, axis=-1, keepdims=True) + EPS)
        xn = xv * r
        scale = mod_ref[:, D_MODEL:2 * D_MODEL]
        dshift_ref[...] += jnp.sum(dh, axis=0, keepdims=True)
        dscale_ref[...] += jnp.sum(dh * (xn * g), axis=0, keepdims=True)
        da = dh * (1.0 + scale)
        dg_ref[...] += jnp.sum(da * xn, axis=0, keepdims=True)
        dxn = da * g
        dx = r * (dxn - xn * jnp.mean(dxn * xn, axis=-1, keepdims=True))
        gx_ref[...] = dout_ref[...] + dx

    row = pl.BlockSpec((bs, D_MODEL), lambda i: (i, 0))
    vec = pl.BlockSpec((1, D_MODEL), lambda i: (0, 0))
    return pl.pallas_call(
        body, grid=(seq // bs,), name="in_proj_dx",
        in_specs=[row, pl.BlockSpec((bs, IN_WIDTH), lambda i: (i, 0)), pl.BlockSpec((IN_WIDTH, D_MODEL), lambda i: (0, 0)), row,
                  pl.BlockSpec((1, 3 * D_MODEL), lambda i: (0, 0)), vec] + dep_specs,
        out_specs=[row, vec, vec, vec],
        out_shape=[jax.ShapeDtypeStruct((seq, D_MODEL), F32)] + [jax.ShapeDtypeStruct((1, D_MODEL), F32)] * 3,
        compiler_params=_cp("arbitrary"))(x, dp, win_t, dout, mod, g_pre, *deps)


def _dep_args(dep, grid_rank):
    if dep is None:
        return [], []
    return [dep], [pl.BlockSpec(dep.shape, lambda *_: (0,) * dep.ndim)]


def _matmul(a, b, *, ta, tb, tm, tn, out_dtype, name, dep=None):
    kdim = a.shape[0] if ta else a.shape[1]
    m = a.shape[1] if ta else a.shape[0]
    n = b.shape[0] if tb else b.shape[1]
    assert m % tm == 0 and n % tn == 0
    deps, dep_specs = _dep_args(dep, 2)

    def body(a_ref, b_ref, *rest):
        o_ref = rest[-1]
        dims = (((0 if ta else 1,), (1 if tb else 0,)), ((), ()))
        o_ref[...] = lax.dot_general(a_ref[...], b_ref[...], dims, preferred_element_type=F32).astype(o_ref.dtype)

    a_spec = pl.BlockSpec((kdim, tm), lambda j, i: (0, i)) if ta else pl.BlockSpec((tm, kdim), lambda j, i: (i, 0))
    b_spec = pl.BlockSpec((tn, kdim), lambda j, i: (j, 0)) if tb else pl.BlockSpec((kdim, tn), lambda j, i: (0, j))
    return pl.pallas_call(
        body, grid=(n // tn, m // tm), name=name, in_specs=[a_spec, b_spec] + dep_specs,
        out_specs=pl.BlockSpec((tm, tn), lambda j, i: (i, j)),
        out_shape=jax.ShapeDtypeStruct((m, n), out_dtype), compiler_params=_cp("parallel", "parallel"))(a, b, *deps)


def _in_proj_rest(h, win_t, cos, sin):
    seq = h.shape[0]
    tm = min(1024, seq)
    ncols = IN_WIDTH - ATTN_COLS
    tn = ncols // 2
    assert ZR_B <= tn and ATTN_COLS % 128 == 0 and tn % 128 == 0

    def body(h_ref, w_ref, cos_ref, sin_ref, p_ref, rq_ref, rk_ref, rv_ref):
        p = _nt(h_ref[...], w_ref[...])
        p_ref[...] = p

        @pl.when(pl.program_id(0) == 0)
        def _():
            cs, sn = cos_ref[...], sin_ref[...]
            for pr in range(2):
                sl = slice(128 * pr, 128 * (pr + 1))
                rq_ref[:, sl] = _rope(p[:, QR_B + 128 * pr:QR_B + 128 * (pr + 1)], cs, sn).astype(rq_ref.dtype)
                rk_ref[:, sl] = (_rope(p[:, KR_B + 128 * pr:KR_B + 128 * (pr + 1)], cs, sn) * 0.125).astype(rk_ref.dtype)
            rv_ref[...] = p[:, VR_B:VR_B + 512].astype(rv_ref.dtype)

    nrow = seq // tm
    row = lambda w: pl.BlockSpec((tm, w), lambda j, i: (i, 0))
    park = lambda w: pl.BlockSpec((tm, w), lambda j, i: (jnp.where(j == 0, i, nrow - 1), 0))
    sds = jax.ShapeDtypeStruct
    return pl.pallas_call(
        body, grid=(2, nrow), name="in_proj_rest",
        in_specs=[row(D_MODEL), pl.BlockSpec((pl.Element(tn), pl.Element(D_MODEL)),
                                             lambda j, i: (pl.multiple_of(ATTN_COLS + j * tn, 128), 0)), row(128), row(128)],
        out_specs=[pl.BlockSpec((tm, tn), lambda j, i: (i, j)), park(256), park(256), park(512)],
        out_shape=[sds((seq, ncols), F32), sds((seq, 256), MXU_DTYPE), sds((seq, 256), MXU_DTYPE), sds((seq, 512), MXU_DTYPE)],
        compiler_params=_cp("arbitrary", "arbitrary"))(h, win_t, cos, sin)


def _halves(kd):
    lo = _lane(kd.shape) < 64
    z = jnp.zeros_like(kd)
    return jnp.concatenate([jnp.where(lo, kd, z), jnp.where(lo, z, kd)], axis=0)


def _attn_fwd(qt, kd, vd, dep=None):
    seq = qt.shape[0]
    bq, bk = min(1024, seq), min(1024, seq)
    nk = seq // bk
    deps, dep_specs = _dep_args(dep, 2)

    def body(q_ref, k_ref, v_ref, *rest):
        o_ref, lse_ref, m_scr, l_scr, acc = rest[-5:]
        j = pl.program_id(1)
        m_scr[...] = jnp.full_like(m_scr, -jnp.inf)
        l_scr[...] = jnp.zeros_like(l_scr)
        acc[...] = jnp.zeros_like(acc)
        nsub = 2 if bq % 256 == 0 else 1
        sub = bq // nsub
        lo = _lane((sub, 128)) < 64

        def kv_block(n, carry):
            rows = pl.ds(pl.multiple_of(n * bk, bk), bk)
            k2, v2 = _halves(k_ref[rows, :]), _halves(v_ref[rows, :])
            for pr in range(2):
                for hf in range(nsub):
                    qr = slice(hf * sub, (hf + 1) * sub)
                    s2 = _nt(q_ref[qr, 128 * pr:128 * (pr + 1)], k2)
                    ps, alphas = [], []
                    for e in range(2):
                        g = 2 * pr + e
                        s = s2[:, e * bk:(e + 1) * bk]
                        m_prev = m_scr[g, qr]
                        m_next = jnp.maximum(m_prev, jnp.max(s, axis=1, keepdims=True))
                        alpha = jnp.exp2(m_prev - m_next)
                        pe = jnp.exp2(s - jnp.tile(m_next, (1, bk // 128)))
                        l_scr[g, qr] = alpha * l_scr[g, qr] + jnp.sum(pe, axis=1, keepdims=True)
                        m_scr[g, qr] = m_next
                        ps.append(_mx(pe))
                        alphas.append(alpha)
                    o2 = _nn(jnp.concatenate(ps, axis=1), v2)
                    sl = slice(128 * pr, 128 * (pr + 1))
                    acc[qr, sl] = acc[qr, sl] * jnp.where(lo, alphas[0], alphas[1]) + o2
            return carry

        lax.fori_loop(0, nk, kv_block, 0)
        lo_all = _lane((bq, 128)) < 64
        for pr in range(2):
            sl = slice(128 * pr, 128 * (pr + 1))
            o_ref[:, sl] = acc[:, sl] / jnp.where(lo_all, l_scr[2 * pr], l_scr[2 * pr + 1])
        for g in range(4):
            lse = jnp.transpose(m_scr[g] + jnp.log2(l_scr[g]))
            lse_ref[pl.ds(4 * j + g, 1), :] = lse[0:1, :]

    return pl.pallas_call(
        body, grid=(seq // bq, 2), name="attn_fwd",
        in_specs=[pl.BlockSpec((bq, 256), lambda i, j: (i, j)), pl.BlockSpec((None, seq, 128), lambda i, j: (j, 0, 0)),
                  pl.BlockSpec((None, seq, 128), lambda i, j: (j, 0, 0))] + dep_specs,
        out_specs=[pl.BlockSpec((bq, 256), lambda i, j: (i, j)), pl.BlockSpec((8, bq), lambda i, j: (0, i))],
        out_shape=[jax.ShapeDtypeStruct((seq, 512), F32), jax.ShapeDtypeStruct((8, seq), F32)],
        scratch_shapes=[pltpu.VMEM((4, bq, 128), F32), pltpu.VMEM((4, bq, 128), F32), pltpu.VMEM((bq, 256), F32)],
        compiler_params=_cp("parallel", "arbitrary"))(qt, kd, vd, *deps)


def _attn_bwd(qt, kd, vd, do, lse, delta, dep=None):
    seq = qt.shape[0]
    bq, bk = min(1024, seq), min(1024, seq)
    nq, nk = seq // bq, seq // bk
    deps, dep_specs = _dep_args(dep, 3)

    def body(q_ref, do_ref, k_ref, v_ref, lse_ref, del_ref, *rest):
        dq_ref, dk_ref, dv_ref = rest[-3:]
        j, i = pl.program_id(0), pl.program_id(1)
        dq_ref[...] = jnp.zeros_like(dq_ref)

        @pl.when(i == 0)
        def _():
            dk_ref[...] = jnp.zeros_like(dk_ref)
            dv_ref[...] = jnp.zeros_like(dv_ref)

        lo = _lane((bk, 128)) < 64
        big = lambda r: jnp.concatenate([jnp.broadcast_to(r[0], (bk, bq)), jnp.broadcast_to(r[1], (bk, bq))], axis=0)

        def kv_block(n, carry):
            rows = pl.ds(pl.multiple_of(n * bk, bk), bk)
            k2, v2 = _halves(k_ref[rows, :]), _halves(v_ref[rows, :])
            for pr in range(2):
                sl = slice(128 * pr, 128 * (pr + 1))
                qp, dop = q_ref[:, sl], do_ref[:, sl]
                s_t = _nt(k2, qp)
                dp_t = _nt(v2, dop)
                ls = [lse_ref[pl.ds(4 * j + 2 * pr + e, 1), :] for e in range(2)]
                dl = [del_ref[pl.ds(4 * j + 2 * pr + e, 1), :] for e in range(2)]
                p_t = jnp.exp2(s_t - big(ls))
                ds_t = _mx(p_t * (dp_t - big(dl)))
                rv = _nn(_mx(p_t), dop)
                rk = _nn(ds_t, qp) * LN2
                dv_ref[rows, :] += jnp.where(lo, rv[:bk], rv[bk:])
                dk_ref[rows, :] += jnp.where(lo, rk[:bk], rk[bk:])
                dq_ref[:, sl] += _tn(ds_t, k2)
            return carry

        lax.fori_loop(0, nk, kv_block, 0)

    return pl.pallas_call(
        body, grid=(2, nq), name="attn_bwd",
        in_specs=[pl.BlockSpec((bq, 256), lambda j, i: (i, j)), pl.BlockSpec((bq, 256), lambda j, i: (i, j)),
                  pl.BlockSpec((None, seq, 128), lambda j, i: (j, 0, 0)), pl.BlockSpec((None, seq, 128), lambda j, i: (j, 0, 0)),
                  pl.BlockSpec((8, bq), lambda j, i: (0, i)), pl.BlockSpec((8, bq), lambda j, i: (0, i))] + dep_specs,
        out_specs=[pl.BlockSpec((bq, 256), lambda j, i: (i, j)), pl.BlockSpec((None, seq, 128), lambda j, i: (j, 0, 0)),
                   pl.BlockSpec((None, seq, 128), lambda j, i: (j, 0, 0))],
        out_shape=[jax.ShapeDtypeStruct((seq, 512), F32), jax.ShapeDtypeStruct((2, seq, 128), F32),
                   jax.ShapeDtypeStruct((2, seq, 128), F32)],
        compiler_params=_cp("arbitrary", "arbitrary"))(qt, do, kd, vd, lse, delta, *deps)


def _ret_tables(lg_f, lg_b, c):
    idx = jnp.arange(c, dtype=F32)
    diff = idx[:, None] - idx[None, :]
    a, b = lg_f[:, None, None], lg_b[:, None, None]
    low, up = (diff >= 0)[None], (diff < 0)[None]
    dmat = jnp.where(low, jnp.exp(a * jnp.maximum(diff, 0.0)[None]), jnp.exp(b * jnp.maximum(-diff, 0.0)[None]))
    dda = jnp.where(low, diff[None] * jnp.exp(a * jnp.maximum(diff, 0.0)[None]), 0.0)
    ddb = jnp.where(up, -diff[None] * jnp.exp(b * jnp.maximum(-diff, 0.0)[None]), 0.0)
    rep = lambda w: jnp.broadcast_to(w[:, :, None], (RET_HEADS, c, 128))
    wqf = rep(jnp.exp(lg_f[:, None] * (idx + 1.0)[None]))
    wqb = rep(jnp.exp(lg_b[:, None] * (c - idx)[None]))
    wkf = rep(jnp.exp(lg_f[:, None] * (c - 1.0 - idx)[None]))
    wkb = rep(jnp.exp(lg_b[:, None] * idx[None]))
    cdf, cdb = jnp.exp(lg_f * c), jnp.exp(lg_b * c)
    dec_fb = jnp.broadcast_to(jnp.concatenate([cdf, cdb])[:, None], (8, 128))
    dec_bf = jnp.broadcast_to(jnp.concatenate([cdb, cdf])[:, None], (8, 128))
    return dict(dmat=dmat, dda=dda, ddb=ddb, wqf=wqf, wqb=wqb, wkf=wkf, wkb=wkb, dec_fb=dec_fb, dec_bf=dec_bf)


def _ret_states(xk, yv, w_fwd, w_rev, dec, name):
    seq = xk.shape[0]
    c = RET_CHUNK
    nc = seq // c
    grp = min(RET_GROUP, nc)
    ns = nc // grp

    def body(xf_ref, yf_ref, xr_ref, yr_ref, wf_ref, wr_ref, dec_ref, sf_ref, sr_ref, st_f, st_r):
        @pl.when(pl.program_id(0) == 0)
        def _():
            st_f[...] = jnp.zeros_like(st_f)
            st_r[...] = jnp.zeros_like(st_r)

        lane = _lane((c, 128))

        def chunk(g, carry):
            for x_ref, y_ref, w_ref, s_ref, st, base, gg in ((xf_ref, yf_ref, wf_ref, sf_ref, st_f, 0, g),
                                                             (xr_ref, yr_ref, wr_ref, sr_ref, st_r, 4, grp - 1 - g)):
                rows = pl.ds(pl.multiple_of(gg * c, c), c)
                for h in range(RET_HEADS):
                    pr, e = h // 2, h % 2
                    half = (lane < 64) if e == 0 else (lane >= 64)
                    s_ref[gg, h] = st[h].astype(s_ref.dtype)
                    xm = jnp.where(half, x_ref[rows, 128 * pr:128 * (pr + 1)].astype(F32) * w_ref[h], 0.0)
                    st[h] = st[h] * dec_ref[base + h:base + h + 1, :] + _tn(_mx(xm), _mx(y_ref[rows, 128 * h:128 * (h + 1)]))
            return carry

        lax.fori_loop(0, grp, chunk, 0)

    xs = lambda f: pl.BlockSpec((grp * c, 256), f)
    ys = lambda f: pl.BlockSpec((grp * c, 512), f)
    fwd, rev = (lambda n: (n, 0)), (lambda n: (ns - 1 - n, 0))
    wsp = pl.BlockSpec((RET_HEADS, c, 128), lambda n: (0, 0, 0))
    return pl.pallas_call(
        body, grid=(ns,), name=name,
        in_specs=[xs(fwd), ys(fwd), xs(rev), ys(rev), wsp, wsp, pl.BlockSpec((8, 128), lambda n: (0, 0))],
        out_specs=[pl.BlockSpec((grp, RET_HEADS, 128, 128), lambda n: (n, 0, 0, 0)),
                   pl.BlockSpec((grp, RET_HEADS, 128, 128), lambda n: (ns - 1 - n, 0, 0, 0))],
        out_shape=[jax.ShapeDtypeStruct((nc, RET_HEADS, 128, 128), MXU_DTYPE)] * 2,
        scratch_shapes=[pltpu.VMEM((RET_HEADS, 128, 128), F32), pltpu.VMEM((RET_HEADS, 128, 128), F32)],
        compiler_params=_cp("arbitrary"))(xk, yv, xk, yv, w_fwd, w_rev, dec)


def _ret_fwd(rq, rk, rv, rf, rb, tb):
    seq = rq.shape[0]
    c = RET_CHUNK
    grp = min(RET_GROUP, seq // c)

    def body(q_ref, k_ref, v_ref, rf_ref, rb_ref, d_ref, wqf_ref, wqb_ref, o_ref):
        lane = _lane((c, 128))

        def chunk(g, carry):
            rows = pl.ds(pl.multiple_of(g * c, c), c)
            for h in range(RET_HEADS):
                pr, e = h // 2, h % 2
                half = (lane < 64) if e == 0 else (lane >= 64)
                sl = slice(128 * pr, 128 * (pr + 1))
                qp, kp = q_ref[rows, sl], k_ref[rows, sl]
                km = jnp.where(half, kp, jnp.zeros_like(kp))
                sd = _mx(_nt(qp, km) * d_ref[h])
                qf = qp.astype(F32)
                o = _nn(sd, v_ref[rows, 128 * h:128 * (h + 1)])
                o = o + _nn(_mx(qf * wqf_ref[h]), rf_ref[g, h]) + _nn(_mx(qf * wqb_ref[h]), rb_ref[g, h])
                o_ref[rows, 128 * h:128 * (h + 1)] = o
            return carry

        lax.fori_loop(0, grp, chunk, 0)

    st = pl.BlockSpec((grp, RET_HEADS, 128, 128), lambda n: (n, 0, 0, 0))
    wsp = pl.BlockSpec((RET_HEADS, c, 128), lambda n: (0, 0, 0))
    return pl.pallas_call(
        body, grid=(seq // (grp * c),), name="ret_fwd",
        in_specs=[pl.BlockSpec((grp * c, 256), lambda n: (n, 0)), pl.BlockSpec((grp * c, 256), lambda n: (n, 0)),
                  pl.BlockSpec((grp * c, 512), lambda n: (n, 0)), st, st, pl.BlockSpec((RET_HEADS, c, c), lambda n: (0, 0, 0)), wsp, wsp],
        out_specs=pl.BlockSpec((grp * c, 512), lambda n: (n, 0)),
        out_shape=jax.ShapeDtypeStruct((seq, 512), F32), compiler_params=_cp("parallel"))(
            rq, rk, rv, rf, rb, tb["dmat"], tb["wqf"], tb["wqb"])


def _ret_bwd(rq, rk, rv, do, rf, rb, hf, hb, tb):
    seq = rq.shape[0]
    c = RET_CHUNK
    grp = min(RET_GROUP, seq // c)

    def body(q_ref, k_ref, v_ref, do_ref, rf_ref, rb_ref, hf_ref, hb_ref, d_ref, dda_ref, ddb_ref,
             wqf_ref, wqb_ref, wkf_ref, wkb_ref, cdec_ref, dq_ref, dk_ref, dv_ref, dlg_ref):
        @pl.when(pl.program_id(0) == 0)
        def _():
            dlg_ref[...] = jnp.zeros_like(dlg_ref)

        lane = _lane((c, 128))
        pos = lax.broadcasted_iota(jnp.int32, (c, 128), 0).astype(F32)

        def chunk(g, carry):
            rows = pl.ds(pl.multiple_of(g * c, c), c)
            for pr in range(2):
                sl = slice(128 * pr, 128 * (pr + 1))
                qp, kp = q_ref[rows, sl], k_ref[rows, sl]
                qf, kf = qp.astype(F32), kp.astype(F32)
                dq_acc = jnp.zeros((c, 128), F32)
                dk_acc = jnp.zeros((c, 128), F32)
                for e in range(2):
                    h = 2 * pr + e
                    half = (lane < 64) if e == 0 else (lane >= 64)
                    hs = slice(128 * h, 128 * (h + 1))
                    km = jnp.where(half, kp, jnp.zeros_like(kp))
                    kmf = km.astype(F32)
                    vh, doh = v_ref[rows, hs], do_ref[rows, hs]
                    rfh, rbh, hfh, hbh = rf_ref[g, h], rb_ref[g, h], hf_ref[g, h], hb_ref[g, h]
                    s = _nt(qp, km)
                    dpm = _nt(doh, vh)
                    ds_b = _mx(dpm * d_ref[h])
                    sd_b = _mx(s * d_ref[h])
                    dq_if = wqf_ref[h] * _nt(doh, rfh)
                    dq_ib = wqb_ref[h] * _nt(doh, rbh)
                    dq_acc = dq_acc + _nn(ds_b, km) + dq_if + dq_ib
                    dk_sf = wkf_ref[h] * _nt(vh, hfh)
                    dk_sb = wkb_ref[h] * _nt(vh, hbh)
                    dk_acc = dk_acc + jnp.where(half, _tn(ds_b, qp), 0.0) + dk_sf + dk_sb
                    dv = _tn(sd_b, doh) + _nn(_mx(kmf * wkf_ref[h]), hfh) + _nn(_mx(kmf * wkb_ref[h]), hbh)
                    dv_ref[rows, hs] = dv.astype(dv_ref.dtype)
                    sdp = s * dpm
                    hr_f = hfh.astype(F32) * rfh.astype(F32)
                    hr_b = hbh.astype(F32) * rbh.astype(F32)
                    da = (_rowsum128(sdp * dda_ref[h]) + _rowsum128((pos + 1.0) * qf * dq_if)
                          + _rowsum128((c - 1.0 - pos) * kf * dk_sf) + cdec_ref[h:h + 1, :] * _rowsum128(hr_f))
                    db = (_rowsum128(sdp * ddb_ref[h]) + _rowsum128((c - pos) * qf * dq_ib)
                          + _rowsum128(pos * kf * dk_sb) + cdec_ref[4 + h:5 + h, :] * _rowsum128(hr_b))
                    dlg_ref[h:h + 1, :] += da
                    dlg_ref[4 + h:5 + h, :] += db
                dq_ref[rows, sl] = dq_acc
                dk_ref[rows, sl] = dk_acc
            return carry

        lax.fori_loop(0, grp, chunk, 0)

    st = pl.BlockSpec((grp, RET_HEADS, 128, 128), lambda n: (n, 0, 0, 0))
    wsp = pl.BlockSpec((RET_HEADS, c, 128), lambda n: (0, 0, 0))
    dsp = pl.BlockSpec((RET_HEADS, c, c), lambda n: (0, 0, 0))
    x256 = pl.BlockSpec((grp * c, 256), lambda n: (n, 0))
    x512 = pl.BlockSpec((grp * c, 512), lambda n: (n, 0))
    cdec = tb["dec_fb"] * float(c)
    return pl.pallas_call(
        body, grid=(seq // (grp * c),), name="ret_bwd",
        in_specs=[x256, x256, x512, x512, st, st, st, st, dsp, dsp, dsp, wsp, wsp, wsp, wsp,
                  pl.BlockSpec((8, 128), lambda n: (0, 0))],
        out_specs=[x256, x256, x512, pl.BlockSpec((8, 128), lambda n: (0, 0))],
        out_shape=[jax.ShapeDtypeStruct((seq, 256), F32), jax.ShapeDtypeStruct((seq, 256), F32),
                   jax.ShapeDtypeStruct((seq, 512), MXU_DTYPE), jax.ShapeDtypeStruct((8, 128), F32)],
        compiler_params=_cp("arbitrary"))(
            rq, rk, rv, do, rf, rb, hf, hb, tb["dmat"], tb["dda"], tb["ddb"], tb["wqf"], tb["wqb"], tb["wkf"], tb["wkb"], cdec)


def _tail(x, tgt, o_att, o_ret, p, mod, g_post, gn_g, wpt, wout):
    seq = x.shape[0]
    bs = 256
    nb = seq // bs

    def body(x_ref, t_ref, oa_ref, or_ref, za_ref, zr_ref, gl_ref, mod_ref, gp_ref, gn_ref, wpt_ref, wout_ref,
             dout_ref, dza_ref, dzr_ref, dgl_ref, doa_ref, dor_ref, del_ref, gout_ref, gpt_ref,
             dgate_ref, dgpost_ref, dgn_ref, loss_ref, gout_acc, gpt_acc):
        i = pl.program_id(0)

        @pl.when(i == 0)
        def _():
            gout_acc[...] = jnp.zeros_like(gout_acc)
            gpt_acc[...] = jnp.zeros_like(gpt_acc)
            dgate_ref[...] = jnp.zeros_like(dgate_ref)
            dgpost_ref[...] = jnp.zeros_like(dgpost_ref)
            dgn_ref[...] = jnp.zeros_like(dgn_ref)
            loss_ref[...] = jnp.zeros_like(loss_ref)

        oa, za, zr = oa_ref[...], za_ref[...], zr_ref[...]
        sga, sgr = _sigmoid(za), _sigmoid(zr)
        sza, szr = za * sga, zr * sgr
        ya_b = _mx(oa * sza)
        ons, rstds = [], []
        for h in range(RET_HEADS):
            oh = or_ref[:, 128 * h:128 * (h + 1)]
            xc = oh - jnp.mean(oh, axis=-1, keepdims=True)
            rstd = lax.rsqrt(jnp.mean(xc * xc, axis=-1, keepdims=True) + EPS)
            ons.append(xc * rstd)
            rstds.append(rstd)
        on = jnp.concatenate(ons, axis=1)
        yn = on * gn_ref[...]
        yr_b = _mx(yn * szr)
        wpa_t, wpr_t = wpt_ref[:, 0:512], wpt_ref[:, 512:1024]
        a_att = _nt(ya_b, wpa_t)
        a_ret = _nt(yr_b, wpr_t)
        gts = _sigmoid(gl_ref[...])
        g_att, g_ret = gts[:, 0:D_MODEL], gts[:, D_MODEL:2 * D_MODEL]
        merged_b = _mx(g_att * a_att + g_ret * a_ret)
        u = _nn(merged_b, wout_ref[...])
        r = lax.rsqrt(jnp.mean(u * u, axis=-1, keepdims=True) + EPS)
        un = u * r
        gpost = gp_ref[...]
        y = un * gpost
        gate = mod_ref[:, 2 * D_MODEL:3 * D_MODEL]
        err = x_ref[...] + gate * y - t_ref[...]
        loss_ref[...] += (0.5 / D_MODEL) * _rowsum128(err * err)
        dout = err * (1.0 / D_MODEL)
        dout_ref[...] = dout
        dgate_ref[...] += jnp.sum(dout * y, axis=0, keepdims=True)
        dy = dout * gate
        dgpost_ref[...] += jnp.sum(dy * un, axis=0, keepdims=True)
        dun = dy * gpost
        du_b = _mx(r * (dun - un * jnp.mean(dun * un, axis=-1, keepdims=True)))
        dmerged = _nt(du_b, wout_ref[...])
        gout_acc[...] += _tn(merged_b, du_b)
        d_att, d_ret = dmerged * g_att, dmerged * g_ret
        dgl_ref[:, 0:D_MODEL] = (d_att * a_att * (1.0 - g_att)).astype(dgl_ref.dtype)
        dgl_ref[:, D_MODEL:2 * D_MODEL] = (d_ret * a_ret * (1.0 - g_ret)).astype(dgl_ref.dtype)
        d_att_b, d_ret_b = _mx(d_att), _mx(d_ret)
        dya = _nn(d_att_b, wpa_t)
        dyr = _nn(d_ret_b, wpr_t)
        gpt_acc[:, 0:512] += _tn(d_att_b, ya_b)
        gpt_acc[:, 512:1024] += _tn(d_ret_b, yr_b)
        dza_ref[...] = (dya * oa * (sga * (1.0 + za * (1.0 - sga)))).astype(dza_ref.dtype)
        doa = dya * sza
        doa_ref[...] = doa.astype(doa_ref.dtype)
        dt = jnp.transpose(doa * oa)
        del_ref[...] = jnp.sum(dt.reshape(8, HEAD_DIM, bs), axis=1)
        dzr_ref[...] = (dyr * yn * (sgr * (1.0 + zr * (1.0 - sgr)))).astype(dzr_ref.dtype)
        dyn = dyr * szr
        dgn_ref[...] += jnp.sum(dyn * on, axis=0, keepdims=True)
        don = dyn * gn_ref[...]
        for h in range(RET_HEADS):
            hs = slice(128 * h, 128 * (h + 1))
            dh, oh = don[:, hs], ons[h]
            doh = rstds[h] * (dh - jnp.mean(dh, axis=-1, keepdims=True) - oh * jnp.mean(dh * oh, axis=-1, keepdims=True))
            dor_ref[:, hs] = doh.astype(dor_ref.dtype)

        @pl.when(i == nb - 1)
        def _():
            gout_ref[...] = gout_acc[...].astype(gout_ref.dtype)
            gpt_ref[...] = gpt_acc[...].astype(gpt_ref.dtype)

    row = lambda w: pl.BlockSpec((bs, w), lambda i: (i, 0))
    el = lambda w, off: pl.BlockSpec((pl.Element(bs), pl.Element(w)), lambda i: (i * bs, off))
    vec = lambda w: pl.BlockSpec((1, w), lambda i: (0, 0))
    full = pl.BlockSpec((D_MODEL, D_MODEL), lambda i: (0, 0))
    sds = jax.ShapeDtypeStruct
    return pl.pallas_call(
        body, grid=(nb,), name="tail",
        in_specs=[row(D_MODEL), row(D_MODEL), row(512), row(512), el(512, ZA_B), el(512, ZR_B), el(2 * D_MODEL, GL_B),
                  vec(3 * D_MODEL), vec(D_MODEL), vec(512), full, full],
        out_specs=[row(D_MODEL), row(512), row(512), row(2 * D_MODEL), row(512), row(512),
                   pl.BlockSpec((8, bs), lambda i: (0, i)), full, full, vec(D_MODEL), vec(D_MODEL), vec(512), vec(128)],
        out_shape=[sds((seq, D_MODEL), F32), sds((seq, 512), MXU_DTYPE), sds((seq, 512), MXU_DTYPE),
                   sds((seq, 2 * D_MODEL), MXU_DTYPE), sds((seq, 512), MXU_DTYPE), sds((seq, 512), MXU_DTYPE),
                   sds((8, seq), F32), sds((D_MODEL, D_MODEL), MXU_DTYPE), sds((D_MODEL, D_MODEL), MXU_DTYPE),
                   sds((1, D_MODEL), F32), sds((1, D_MODEL), F32), sds((1, 512), F32), sds((1, 128), F32)],
        scratch_shapes=[pltpu.VMEM((D_MODEL, D_MODEL), F32), pltpu.VMEM((D_MODEL, D_MODEL), F32)],
        compiler_params=_cp("arbitrary"))(x, tgt, o_att, o_ret, p, p, p, mod, g_post, gn_g, wpt, wout)


def _assemble(p, cos, sin, qg2, kg2, dqt, dkp, dvp, dza, drq, drk, drv, dzr, dgl):
    seq = p.shape[0]
    bs = 512

    def body(p_ref, cos_ref, sin_ref, qg_ref, kg_ref, dqt_ref, dkp_ref, dvp_ref, dza_ref, drq_ref, drk_ref, drv_ref,
             dzr_ref, dgl_ref, dp_ref, dqg_ref, dkg_ref):
        @pl.when(pl.program_id(0) == 0)
        def _():
            dqg_ref[...] = jnp.zeros_like(dqg_ref)
            dkg_ref[...] = jnp.zeros_like(dkg_ref)

        m = _blockdiag64()
        cs, sn = cos_ref[...], sin_ref[...]
        lo = _lane((bs, 128)) < 64

        def norm_bwd(raw, dn, g, dg_ref):
            r = lax.rsqrt(_seg64(raw * raw, m) * (1.0 / 64) + EPS)
            xn = raw * r
            dg_ref[...] += jnp.sum(dn * xn, axis=0, keepdims=True)
            dxn = dn * g
            return r * (dxn - xn * (_seg64(dxn * xn, m) * (1.0 / 64)))

        for pr in range(4):
            sl = slice(128 * pr, 128 * (pr + 1))
            dn = _rope_t(dqt_ref[:, sl] * 0.125, cs, sn)
            dp_ref[:, QA + 128 * pr:QA + 128 * (pr + 1)] = norm_bwd(p_ref[:, sl], dn, qg_ref[...], dqg_ref).astype(dp_ref.dtype)
        fold = lambda a: a + pltpu.roll(a, 64, 1)
        dk = jnp.where(lo, fold(dkp_ref[0]), fold(dkp_ref[1]))
        dp_ref[:, KA:KA + 128] = norm_bwd(p_ref[:, KA:KA + 128], _rope_t(dk, cs, sn), kg_ref[...], dkg_ref).astype(dp_ref.dtype)
        dp_ref[:, VA:VA + 128] = jnp.where(lo, fold(dvp_ref[0]), fold(dvp_ref[1])).astype(dp_ref.dtype)
        dp_ref[:, ZA:ZA + 512] = dza_ref[...]
        for pr in range(2):
            sl = slice(128 * pr, 128 * (pr + 1))
            dp_ref[:, QR + 128 * pr:QR + 128 * (pr + 1)] = _rope_t(drq_ref[:, sl], cs, sn).astype(dp_ref.dtype)
            dp_ref[:, KR + 128 * pr:KR + 128 * (pr + 1)] = _rope_t(drk_ref[:, sl] * 0.125, cs, sn).astype(dp_ref.dtype)
        dp_ref[:, VR:VR + 512] = drv_ref[...]
        dp_ref[:, ZR:ZR + 512] = dzr_ref[...]
        dp_ref[:, GL:GL + 2 * D_MODEL] = dgl_ref[...]

    row = lambda w: pl.BlockSpec((bs, w), lambda i: (i, 0))
    gsp = pl.BlockSpec((1, 128), lambda i: (0, 0))
    dup = pl.BlockSpec((2, bs, 128), lambda i: (0, i, 0))
    return pl.pallas_call(
        body, grid=(seq // bs,), name="assemble_dp",
        in_specs=[row(768), row(128), row(128), gsp, gsp, row(512), dup, dup, row(512), row(256), row(256), row(512),
                  row(512), row(2 * D_MODEL)],
        out_specs=[row(IN_WIDTH), gsp, gsp],
        out_shape=[jax.ShapeDtypeStruct((seq, IN_WIDTH), MXU_DTYPE), jax.ShapeDtypeStruct((1, 128), F32),
                   jax.ShapeDtypeStruct((1, 128), F32)],
        compiler_params=_cp("arbitrary"))(p, cos, sin, qg2, kg2, dqt, dkp, dvp, dza, drq, drk, drv, dzr, dgl)


def _local_step(x, tgt, mod, g_pre, qn_g, kn_g, w_dec_f, w_dec_b, gn_g, g_post, w_attn, rest_start, rest_wait, send=None):
    send = send or (lambda name, arrays: None)
    seq = x.shape[0]
    cos, sin = _rope_tables(seq)
    qg2, kg2 = jnp.tile(qn_g, (1, 2)), jnp.tile(kn_g, (1, 2))
    lg_f = jax.nn.log_sigmoid(w_dec_f[0])
    lg_b = jax.nn.log_sigmoid(w_dec_b[0])
    tb = _ret_tables(lg_f, lg_b, RET_CHUNK)

    h, p_a, qt, kd, vd = _attn_inputs(x, mod, g_pre, w_attn, cos, sin, qg2, kg2)
    o_att, lse = _attn_fwd(qt, kd, vd, dep=rest_start(qt))
    win_t, wpt, wout = rest_wait(o_att)
    p_b, rq, rk, rv = _in_proj_rest(h, win_t, cos, sin)
    rf, rb = _ret_states(rk, rv, tb["wkf"], tb["wkb"], tb["dec_fb"], "ret_states_fwd")
    o_ret = _ret_fwd(rq, rk, rv, rf, rb, tb)
    (dout, dza, dzr, dgl, do_att, do_ret, delta, g_out, g_pt, dgate, dgpost, dgn, loss_l) = _tail(
        x, tgt, o_att, o_ret, p_b, mod, g_post, gn_g, wpt, wout)
    dep = send("early", (g_pt, g_out))
    dqt, dkp, dvp = _attn_bwd(qt, kd, vd, do_att, lse, delta, dep=dep)
    hb, hf = _ret_states(rq, do_ret, tb["wqb"], tb["wqf"], tb["dec_bf"], "ret_states_bwd")
    drq, drk, drv, dlg = _ret_bwd(rq, rk, rv, do_ret, rf, rb, hf, hb, tb)
    dp, dqg, dkg = _assemble(p_a, cos, sin, qg2, kg2, dqt, dkp, dvp, dza, drq, drk, drv, dzr, dgl)
    g_in_t = _matmul(dp, h, ta=True, tb=False, tm=256, tn=D_MODEL, out_dtype=MXU_DTYPE, name="in_proj_dw")
    dep = send("late", (g_in_t,))
    grad_x, dshift, dscale, dgpre = _in_proj_dx(x, dp, win_t, dout, mod, g_pre, dep=dep)

    dlg = jnp.sum(dlg, axis=1)
    small = dict(
        dmod=jnp.concatenate([dshift, dscale, dgate], axis=1),
        g_pre=dgpre, g_post=dgpost, gn_g=dgn,
        qn_g=dqg[:, :64] + dqg[:, 64:], kn_g=dkg[:, :64] + dkg[:, 64:],
        w_dec_f=(dlg[0:4] * jax.nn.sigmoid(-w_dec_f[0]))[None], w_dec_b=(dlg[4:8] * jax.nn.sigmoid(-w_dec_b[0]))[None],
        loss=jnp.sum(loss_l, axis=1, keepdims=True))
    return grad_x, g_in_t, g_pt, g_out, small


N_DEV = 8
N_CHIP = 4
HBM_SPEC = pl.BlockSpec(memory_space=pl.ANY)
VMEM_SPEC = pl.BlockSpec(memory_space=pltpu.VMEM)
SHARD_ROWS = (IN_WIDTH // N_CHIP, D_MODEL // N_CHIP, D_MODEL // N_CHIP)


def _coords():
    return lax.axis_index("x"), lax.axis_index("y"), lax.axis_index("c")


def _peer(k):
    x, y, c = _coords()
    return (1 - x if k & 4 else x, 1 - y if k & 2 else y, 1 - c if k & 1 else c)


def _dev_index(p):
    return 4 * p[0] + 2 * p[1] + p[2]


def _rows(ref, start, size):
    return ref.at[pl.ds(pl.multiple_of(start, 16), size), :]


def _remote(src, dst, ssem, rsem, dev):
    return pltpu.make_async_remote_copy(src_ref=src, dst_ref=dst, send_sem=ssem, recv_sem=rsem, device_id=dev,
                                        device_id_type=MESH)


ATTN_CHUNK = 96


def _mod_and_attn_rows(c, w_ada_s, b4, win_s):
    ncol = w_ada_s.shape[1]
    half = ATTN_COLS // 2
    offs = list(range(0, half, ATTN_CHUNK))
    nq = len(offs)
    rows = lambda ref, cc, off: ref.at[pl.ds(pl.multiple_of(cc * half + off, 16), ATTN_CHUNK), :]

    def body(c_ref, w_ref, b_ref, src, cs_ref, mod_ref, out, modsh, modall, ssem, rsem, lsem, asem, bsem, fsem, gsem, hsem):
        x, y, cc = _coords()
        me = _dev_index((x, y, cc))
        chip = me // 2
        sib = (x, y, 1 - cc)
        cs_ref[me] = c_ref[...]
        sends = []
        for k in range(1, N_DEV):
            cp = _remote(c_ref, cs_ref.at[me], ssem.at[k - 1], rsem.at[k - 1], _peer(k))
            cp.start()
            sends.append(cp)
        own = pltpu.make_async_copy(src.at[pl.ds(0, ATTN_COLS), :], out, lsem.at[0])
        bulk = [_remote(rows(src, cc, off), rows(out, cc, off), asem.at[(k2 - 1) * nq + q], bsem.at[q], (k2 // 2, k2 % 2, cc))
                for k2 in (1, 2) for q, off in enumerate(offs)]
        relay_chip = lambda q: 1 + q % 2

        @pl.when(chip == 0)
        def _():
            own.start()
            for cp in bulk:
                cp.start()

        for k in range(1, N_DEV):
            slot = cs_ref.at[_dev_index(_peer(k))]
            _remote(slot, slot, ssem.at[k - 1], rsem.at[k - 1], _peer(k)).wait_recv()
        cs = jnp.concatenate([cs_ref[d] for d in range(N_DEV)], axis=0)
        ms = _nn(cs * _sigmoid(cs), w_ref[...], precision=HIGHEST) + b_ref[pl.ds(chip, 1), :]
        modsh[...] = ms
        modall[chip] = ms
        for k2 in range(1, N_CHIP):
            cp = _remote(modsh, modall.at[chip], ssem.at[6 + k2], rsem.at[6 + k2], _peer(2 * k2))
            cp.start()
            sends.append(cp)

        @pl.when(chip != 0)
        def _():
            passed = []
            relays = [_remote(rows(out, cc, off), rows(out, cc, off), hsem.at[q], bsem.at[q], (1, 1, cc)) for q, off in enumerate(offs)]
            for q, off in enumerate(offs):
                got = rows(out, cc, off)
                _remote(got, got, asem.at[q], bsem.at[q], (0, 0, cc)).wait_recv()
                cp = _remote(got, got, fsem.at[q], gsem.at[q], sib)
                cp.start()
                passed.append(cp)
                pl.when(chip == relay_chip(q))(relays[q].start)
            for q, off in enumerate(offs):
                got = rows(out, 1 - cc, off)
                _remote(got, got, fsem.at[q], gsem.at[q], sib).wait_recv()
            for cp in passed:
                cp.wait_send()
            for q in range(nq):
                pl.when(chip == relay_chip(q))(relays[q].wait_send)

        for k2 in range(1, N_CHIP):
            slot = modall.at[_dev_index(_peer(2 * k2)) // 2]
            _remote(slot, slot, ssem.at[6 + k2], rsem.at[6 + k2], _peer(2 * k2)).wait_recv()
        for jj in range(N_CHIP):
            mod_ref[:, ncol * jj:ncol * (jj + 1)] = modall[jj, pl.ds(me, 1), :]
        for cp in sends:
            cp.wait_send()

        @pl.when(chip == 0)
        def _():
            for cp in bulk:
                cp.wait_send()
            own.wait()

    dma = pltpu.SemaphoreType.DMA
    return pl.pallas_call(
        body, name="mod_and_attn_rows", in_specs=[VMEM_SPEC] * 4, out_specs=[VMEM_SPEC, VMEM_SPEC, HBM_SPEC],
        out_shape=[jax.ShapeDtypeStruct((N_DEV, 1, D_MODEL), F32), jax.ShapeDtypeStruct((1, N_CHIP * ncol), F32),
                   jax.ShapeDtypeStruct((ATTN_COLS, win_s.shape[1]), win_s.dtype)],
        scratch_shapes=[pltpu.VMEM((N_DEV, ncol), F32), pltpu.VMEM((N_CHIP, N_DEV, ncol), F32), dma((10,)), dma((10,)),
                        dma((1,)), dma((2 * nq,)), dma((nq,)), dma((nq,)), dma((nq,)), dma((nq,))],
        compiler_params=_cp())(c, w_ada_s, b4, win_s)


GATHER_CHUNK = 32


def _chunks(kinds, chunk):
    out = []
    for t, kind in enumerate(kinds):
        half = SHARD_ROWS[kind] // 2
        step = chunk if half % chunk == 0 else half
        out += [(t, off, step) for off in range(0, half, step)]
    return out


SEM_SPEC = pl.BlockSpec(memory_space=pltpu.SEMAPHORE)
HBM_ONLY = pl.BlockSpec(memory_space=pltpu.HBM)
DATAFLOW = pltpu.SideEffectType.DATAFLOW_SIDE_EFFECTING


def _place_own(shards):
    nt = len(shards)

    def body(*refs):
        srcs, outs, lsem = refs[:nt], refs[nt:2 * nt], refs[2 * nt]
        x, y, _ = _coords()
        chip = 2 * x + y
        local = [pltpu.make_async_copy(srcs[t], _rows(outs[t], chip * SHARD_ROWS[t], SHARD_ROWS[t]), lsem.at[t]) for t in range(nt)]
        for cp in local:
            cp.start()
        for cp in local:
            cp.wait()

    return pl.pallas_call(
        body, name="place_own_shard", in_specs=[VMEM_SPEC] * nt, out_specs=[HBM_SPEC] * nt,
        out_shape=[jax.ShapeDtypeStruct((N_CHIP * SHARD_ROWS[t], s.shape[1]), s.dtype) for t, s in enumerate(shards)],
        scratch_shapes=[pltpu.SemaphoreType.DMA((nt,))], compiler_params=_cp())(*shards)


def _gather_rest_start(shards, zones, dep):
    n = len(shards)

    def body(*refs):
        srcs, lands = refs[:n], refs[n:2 * n]
        ssem, rsem = refs[2 * n + 1], refs[2 * n + 2]
        token = refs[-1]
        x, y, _ = _coords()
        chip = 2 * x + y
        for k2 in range(1, N_CHIP):
            for t in range(n):
                q = (k2 - 1) * n + t
                _remote(srcs[t], _rows(lands[t], chip * SHARD_ROWS[t], SHARD_ROWS[t]), ssem.at[q], rsem.at[q], _peer(2 * k2)).start()
        token[...] = jnp.zeros_like(token)

    hbm = lambda a: pltpu.with_memory_space_constraint(a, pltpu.HBM)
    dma = pltpu.SemaphoreType.DMA
    outs = pl.pallas_call(
        body, name="gather_rest", in_specs=[HBM_ONLY] * (2 * n) + [HBM_SPEC],
        out_specs=[SEM_SPEC, SEM_SPEC] + [HBM_ONLY] * (2 * n) + [VMEM_SPEC],
        out_shape=[dma((3 * n,)), dma((3 * n,))] + [pltpu.HBM(a.shape, a.dtype) for a in list(shards) + list(zones)]
        + [jax.ShapeDtypeStruct((8, 128), F32)],
        input_output_aliases={i: 2 + i for i in range(2 * n)},
        compiler_params=pltpu.CompilerParams(has_side_effects=DATAFLOW))(*[hbm(a) for a in list(shards) + list(zones)], dep)
    return outs[0], outs[1], outs[2:2 + n], outs[2 + n:2 + 2 * n], outs[-1]


def _gather_rest_wait(started, after):
    ssem, rsem, shards, zones, _ = started
    n = len(shards)

    def body(*refs):
        srcs, lands = refs[:n], refs[n:2 * n]
        ssem_ref, rsem_ref = refs[2 * n], refs[2 * n + 1]
        for k2 in range(1, N_CHIP):
            pchip = _dev_index(_peer(2 * k2)) // 2
            for t in range(n):
                q = (k2 - 1) * n + t
                cp = _remote(srcs[t], _rows(lands[t], pchip * SHARD_ROWS[t], SHARD_ROWS[t]), ssem_ref.at[q], rsem_ref.at[q], _peer(2 * k2))
                cp.wait_send()
                cp.wait_recv()

    outs = pl.pallas_call(
        body, name="gather_rest_wait", in_specs=[HBM_ONLY] * (2 * n) + [SEM_SPEC, SEM_SPEC, HBM_SPEC], out_specs=[HBM_ONLY] * (2 * n),
        out_shape=[pltpu.HBM(a.shape, a.dtype) for a in list(shards) + list(zones)],
        input_output_aliases={i: i for i in range(2 * n)},
        compiler_params=pltpu.CompilerParams(has_side_effects=DATAFLOW))(*shards, *zones, ssem, rsem, after)
    return outs[n:]


def _reduced_by(ref, t, dev):
    return _rows(ref, (dev // 2) * SHARD_ROWS[t] + (dev % 2) * (SHARD_ROWS[t] // 2), SHARD_ROWS[t] // 2)


def _scatter_start(grads, kinds, name):
    n = len(grads)

    def body(*refs):
        srcs, lands = refs[:n], refs[n:2 * n]
        ssem, rsem = refs[2 * n], refs[2 * n + 1]
        token = refs[-1]
        me = _dev_index(_coords())
        for k in range(1, N_DEV):
            for i, t in enumerate(kinds):
                q = (k - 1) * n + i
                _remote(_reduced_by(srcs[i], t, _dev_index(_peer(k))), lands[i].at[me], ssem.at[q], rsem.at[q], _peer(k)).start()
        token[...] = jnp.zeros_like(token)

    zones = [lax.empty((N_DEV, SHARD_ROWS[t] // 2, g.shape[1]), g.dtype) for g, t in zip(grads, kinds)]
    hbm = lambda a: pltpu.with_memory_space_constraint(a, pltpu.HBM)
    dma = pltpu.SemaphoreType.DMA
    outs = pl.pallas_call(
        body, name=name, in_specs=[HBM_ONLY] * (2 * n), out_specs=[SEM_SPEC, SEM_SPEC] + [HBM_ONLY] * (2 * n) + [VMEM_SPEC],
        out_shape=[dma((7 * n,)), dma((7 * n,))] + [pltpu.HBM(a.shape, a.dtype) for a in list(grads) + zones]
        + [jax.ShapeDtypeStruct((8, 128), F32)],
        input_output_aliases={i: 2 + i for i in range(2 * n)},
        compiler_params=pltpu.CompilerParams(has_side_effects=DATAFLOW))(*[hbm(a) for a in list(grads) + zones])
    return outs[0], outs[1], outs[2:2 + n], outs[2 + n:2 + 2 * n], outs[-1]


def _scatter_wait(started, kinds, after, name):
    ssem, rsem, grads, zones, _ = started
    n = len(grads)

    def body(*refs):
        srcs, lands = refs[:n], refs[n:2 * n]
        ssem_ref, rsem_ref = refs[2 * n], refs[2 * n + 1]
        for k in range(1, N_DEV):
            peer = _dev_index(_peer(k))
            for i, t in enumerate(kinds):
                q = (k - 1) * n + i
                cp = _remote(_reduced_by(srcs[i], t, peer), lands[i].at[peer], ssem_ref.at[q], rsem_ref.at[q], _peer(k))
                cp.wait_send()
                cp.wait_recv()

    outs = pl.pallas_call(
        body, name=name, in_specs=[HBM_ONLY] * (2 * n) + [SEM_SPEC, SEM_SPEC, HBM_SPEC], out_specs=[HBM_ONLY] * (2 * n),
        out_shape=[pltpu.HBM(a.shape, a.dtype) for a in list(grads) + list(zones)],
        input_output_aliases={i: i for i in range(2 * n)},
        compiler_params=pltpu.CompilerParams(has_side_effects=DATAFLOW))(*grads, *zones, ssem, rsem, after)
    return outs[:n], outs[n:]


def _grad_finish(grads, zones, kinds, name):
    nt = len(grads)
    pieces = _chunks(kinds, GATHER_CHUNK)
    npc = len(pieces)
    shard = [SHARD_ROWS[k] for k in kinds]

    def body(*refs):
        srcs, lands, outs = refs[:nt], refs[nt:2 * nt], refs[2 * nt:3 * nt]
        slots, sums = refs[3 * nt:4 * nt], refs[4 * nt:5 * nt]
        lsem, osem, xsem, ysem = refs[5 * nt:]
        x, y, c = _coords()
        me = _dev_index((x, y, c))
        sib = (x, y, 1 - c)
        loads = [pltpu.make_async_copy(_reduced_by(srcs[t], kinds[t], me), slots[t].at[me], lsem.at[t]) for t in range(nt)]
        for k in range(1, N_DEV):
            peer = _dev_index(_peer(k))
            loads += [pltpu.make_async_copy(lands[t].at[peer], slots[t].at[peer], lsem.at[k * nt + t]) for t in range(nt)]
        for cp in loads:
            cp.start()
        for cp in loads:
            cp.wait()
        sends = []
        place = lambda t, cc, off, n: _rows(outs[t], cc * (shard[t] // 2) + off, n)
        stores = []
        for q, (t, off, n) in enumerate(pieces):
            r = pl.ds(off, n)
            acc = slots[t][0, r, :].astype(F32)
            for d in range(1, N_DEV):
                acc = acc + slots[t][d, r, :].astype(F32)
            sums[t][r, :] = acc
            keep = pltpu.make_async_copy(sums[t].at[r, :], place(t, c, off, n), osem.at[q])
            give = _remote(sums[t].at[r, :], place(t, c, off, n), xsem.at[q], ysem.at[q], sib)
            keep.start()
            give.start()
            stores.append(keep)
            sends.append(give)
        for q, (t, off, n) in enumerate(pieces):
            got = place(t, 1 - c, off, n)
            _remote(got, got, xsem.at[q], ysem.at[q], sib).wait_recv()
        for cp in sends:
            cp.wait_send()
        for cp in stores:
            cp.wait()

    dma = pltpu.SemaphoreType.DMA
    return pl.pallas_call(
        body, name=name, in_specs=[HBM_SPEC] * (2 * nt), out_specs=[HBM_SPEC] * nt,
        out_shape=[jax.ShapeDtypeStruct((shard[t], g.shape[1]), F32) for t, g in enumerate(grads)],
        scratch_shapes=([pltpu.VMEM((N_DEV, shard[t] // 2, g.shape[1]), g.dtype) for t, g in enumerate(grads)]
                        + [pltpu.VMEM((shard[t] // 2, g.shape[1]), F32) for t, g in enumerate(grads)]
                        + [dma((N_DEV * nt,)), dma((npc,)), dma((npc,)), dma((npc,))]),
        compiler_params=_cp())(*grads, *zones)


def _adam_math(w, g, m, v):
    m = ADAM_B1 * m + (1.0 - ADAM_B1) * g
    v = ADAM_B2 * v + (1.0 - ADAM_B2) * (g * g)
    m_hat = m / (1.0 - ADAM_B1 ** ADAM_STEP)
    v_hat = v / (1.0 - ADAM_B2 ** ADAM_STEP)
    return -ADAM_LR * (m_hat / (jnp.sqrt(v_hat) + ADAM_EPS) + ADAM_WD * w), m, v


def _adamw(w, g, m, v, rb, name):
    rows, cols = w.shape

    def body(w_ref, g_ref, m_ref, v_ref, d_ref, nm_ref, nv_ref):
        d_ref[...], nm_ref[...], nv_ref[...] = _adam_math(w_ref[...], g_ref[...], m_ref[...], v_ref[...])

    spec = pl.BlockSpec((rb, cols), lambda i: (i, 0))
    return pl.pallas_call(body, grid=(rows // rb,), name=name, in_specs=[spec] * 4, out_specs=[spec] * 3,
                          out_shape=[jax.ShapeDtypeStruct(w.shape, F32)] * 3, compiler_params=_cp("parallel"))(w, g, m, v)


PACK = (("b_ada", 3 * D_MODEL), ("g_pre", D_MODEL), ("g_post", D_MODEL), ("gn_g", 512), ("qn_g", 64), ("kn_g", 64),
        ("w_dec_f", 4), ("w_dec_b", 4), ("loss", 1))
PACK_OFFSETS = {}
PACK_WIDTH = 0
for _name, _n in PACK:
    PACK_OFFSETS[_name] = PACK_WIDTH
    PACK_WIDTH += -(-_n // 128) * 128
SMALL_PARAMS = tuple(name for name, _ in PACK if name != "loss")


def _pack(parts):
    cols = []
    for name, n in PACK:
        pad = -(-n // 128) * 128 - n
        cols.append(parts[name].reshape(1, n).astype(F32))
        if pad:
            cols.append(jnp.zeros((1, pad), F32))
    return jnp.concatenate(cols, axis=1)


def _small_reduce(vec, cs, deps):
    ncol = 3 * D_MODEL // N_CHIP

    def body(vec_ref, cs_ref, *rest):
        tot_ref, gwa_ref, gat, ssem, rsem = rest[-5:]
        me = _dev_index(_coords())
        chip = me // 2
        gat[me] = vec_ref[...]
        sends = []
        for k in range(1, N_DEV):
            cp = _remote(vec_ref, gat.at[me], ssem.at[k - 1], rsem.at[k - 1], _peer(k))
            cp.start()
            sends.append(cp)
        for k in range(1, N_DEV):
            slot = gat.at[_dev_index(_peer(k))]
            _remote(slot, slot, ssem.at[k - 1], rsem.at[k - 1], _peer(k)).wait_recv()
        rows = [gat[d] for d in range(N_DEV)]
        tot = rows[0]
        for d in range(1, N_DEV):
            tot = tot + rows[d]
        tot_ref[...] = tot
        cs = cs_ref[...]
        ca_t = jnp.transpose(jnp.concatenate([cs * _sigmoid(cs), jnp.zeros((128 - N_DEV, D_MODEL), F32)], axis=0))
        dm = jnp.concatenate(rows + [jnp.zeros((128 - N_DEV, PACK_WIDTH), F32)], axis=0)
        for jj in range(N_CHIP):
            @pl.when(chip == jj)
            def _():
                gwa_ref[...] = _nn(ca_t, dm[:, ncol * jj:ncol * (jj + 1)], precision=HIGHEST)
        for cp in sends:
            cp.wait_send()

    return pl.pallas_call(
        body, name="small_reduce", in_specs=[VMEM_SPEC, VMEM_SPEC] + [HBM_SPEC] * len(deps), out_specs=[VMEM_SPEC] * 2,
        out_shape=[jax.ShapeDtypeStruct((1, PACK_WIDTH), F32), jax.ShapeDtypeStruct((D_MODEL, ncol), F32)],
        scratch_shapes=[pltpu.VMEM((N_DEV, 1, PACK_WIDTH), F32), pltpu.SemaphoreType.DMA((7,)), pltpu.SemaphoreType.DMA((7,))],
        compiler_params=_cp())(vec, cs, *deps)


def _small_adamw(tot, given):
    sizes = dict(PACK)
    np_ = len(SMALL_PARAMS)

    def body(*refs):
        tot_ref = refs[0]
        wmv = refs[1:1 + 3 * np_]
        outs = refs[1 + 3 * np_:1 + 7 * np_]
        loss_ref = refs[-1]
        for i, name in enumerate(SMALL_PARAMS):
            off, n = PACK_OFFSETS[name], sizes[name]
            g = tot_ref[:, off:off + n]
            w_ref, m_ref, v_ref = wmv[3 * i:3 * i + 3]
            outs[i][...] = g
            outs[np_ + i][...], outs[2 * np_ + i][...], outs[3 * np_ + i][...] = _adam_math(w_ref[...], g, m_ref[...], v_ref[...])
        loss_ref[...] = tot_ref[:, PACK_OFFSETS["loss"]:PACK_OFFSETS["loss"] + 1]

    flat = [a for name in SMALL_PARAMS for a in given[name]]
    shapes = [jax.ShapeDtypeStruct((1, sizes[name]), F32) for name in SMALL_PARAMS]
    res = pl.pallas_call(body, name="small_adamw", in_specs=[VMEM_SPEC] * (1 + 3 * np_), out_specs=[VMEM_SPEC] * (4 * np_ + 1),
                         out_shape=shapes * 4 + [jax.ShapeDtypeStruct((1, 1), F32)], compiler_params=_cp())(tot, *flat)
    return [dict(zip(SMALL_PARAMS, res[k * np_:(k + 1) * np_])) for k in range(4)], res[-1]


def kernel(x, c, w_ada, b_ada, g_pre, w_in, qn_g, kn_g, w_dec_f, w_dec_b, gn_g, w_pa, w_pr, w_out, g_post, loss_target, m_w_ada, m_b_ada, m_g_pre, m_w_in, m_qn_g, m_kn_g, m_w_dec_f, m_w_dec_b, m_gn_g, m_w_pa, m_w_pr, m_w_out, m_g_post, v_w_ada, v_b_ada, v_g_pre, v_w_in, v_qn_g, v_kn_g, v_w_dec_f, v_w_dec_b, v_gn_g, v_w_pa, v_w_pr, v_w_out, v_g_post):
    shards = (w_in[0].T.astype(MXU_DTYPE), jnp.concatenate([w_pa[0].T, w_pr[0].T], axis=1).astype(MXU_DTYPE),
              w_out[0].astype(MXU_DTYPE))
    cs, mod, w_attn = _mod_and_attn_rows(c, w_ada[0], b_ada.reshape(N_CHIP, 3 * D_MODEL // N_CHIP), shards[0])
    started = {}

    def rest_start(dep):
        started["rest"] = _gather_rest_start(shards, _place_own(shards), dep)
        return started["rest"][-1]

    def rest_wait(after):
        return _gather_rest_wait(started["rest"], after)

    kinds = {"early": (1, 2), "late": (0,)}

    def send(name, arrays):
        started[name] = _scatter_start(arrays, kinds[name], "grad_scatter_" + name)
        return started[name][-1]

    grad_x, _, _, _, small = _local_step(x[0], loss_target[0], mod, g_pre, qn_g, kn_g, w_dec_f, w_dec_b,
                                         gn_g, g_post, w_attn, rest_start, rest_wait, send=send)
    small = dict(small)
    small["b_ada"] = small.pop("dmod")
    given = dict(b_ada=(b_ada, m_b_ada, v_b_ada), g_pre=(g_pre, m_g_pre, v_g_pre), g_post=(g_post, m_g_post, v_g_post),
                 gn_g=(gn_g, m_gn_g, v_gn_g), qn_g=(qn_g, m_qn_g, v_qn_g), kn_g=(kn_g, m_kn_g, v_kn_g),
                 w_dec_f=(w_dec_f, m_w_dec_f, v_w_dec_f), w_dec_b=(w_dec_b, m_w_dec_b, v_w_dec_b))
    grads, deltas, new_m, new_v = {}, {}, {}, {}

    def update(name, w, g, m, v, back):
        d, nm, nv = _adamw(w, g, m, v, w.shape[0] // 4, "adamw_" + name)
        grads[name], deltas[name], new_m[name], new_v[name] = back(g), back(d), back(nm), back(nv)
        return d

    lead = lambda a: a[None]
    (g_pt, g_out), (z_pt, z_out) = _scatter_wait(started["early"], kinds["early"], grad_x, "grad_scatter_early_wait")
    r_pt, r_out = _grad_finish((g_pt, g_out), (z_pt, z_out), kinds["early"], "grad_finish_early")
    early = (update("w_pa", w_pa[0], r_pt[:, :512].T, m_w_pa[0], v_w_pa[0], lead),
             update("w_pr", w_pr[0], r_pt[:, 512:].T, m_w_pr[0], v_w_pr[0], lead),
             update("w_out", w_out[0], r_out, m_w_out[0], v_w_out[0], lead))
    tot, g_w_ada = _small_reduce(_pack(small), cs.reshape(N_DEV, D_MODEL), early)
    small_parts, loss = _small_adamw(tot, given)
    for dst, part in zip((grads, deltas, new_m, new_v), small_parts):
        dst.update(part)
    last = update("w_ada", w_ada[0], g_w_ada, m_w_ada[0], v_w_ada[0], lead)

    (g_in_t,), (z_in,) = _scatter_wait(started["late"], kinds["late"], last, "grad_scatter_late_wait")
    (r_in,) = _grad_finish((g_in_t,), (z_in,), kinds["late"], "grad_finish_late")
    tr = lambda a: a[0].T
    update("w_in", tr(w_in), r_in, tr(m_w_in), tr(v_w_in), lambda a: a.T[None])
    order = ("w_ada", "b_ada", "g_pre", "w_in", "qn_g", "kn_g", "w_dec_f", "w_dec_b", "gn_g", "w_pa", "w_pr", "w_out", "g_post")
    return (loss[0, 0], grad_x[None], *[grads[n] for n in order], *[deltas[n] for n in order],
            *[new_m[n] for n in order], *[new_v[n] for n in order])
```

```python
import jax
import jax.numpy as jnp
import numpy as np
from jax import lax
from jax.experimental import pallas as pl
from jax.experimental.pallas import tpu as pltpu

F32 = jnp.float32
BF16 = jnp.bfloat16
MXU_DTYPE = jnp.bfloat16

D_MODEL = 1024
GRID_W = 64
HEAD_DIM = 64
ROPE_THETA = 10000.0
EPS = 1e-6
RET_HEADS = 4
RET_CHUNK = 256
RET_GROUP = 4
IN_WIDTH = 4864
QA, KA, VA, ZA, QR, KR, VR, ZR, GL = 0, 512, 640, 768, 1280, 1536, 1792, 2304, 2816
ATTN_COLS = ZA
ZA_B, QR_B, KR_B, VR_B, ZR_B, GL_B = (c - ATTN_COLS for c in (ZA, QR, KR, VR, ZR, GL))

ADAM_LR, ADAM_B1, ADAM_B2, ADAM_EPS, ADAM_WD, ADAM_STEP = 0.001, 0.9, 0.999, 1e-08, 0.01, 10

VMEM_LIMIT = 56 * 1024 * 1024
MESH = pl.DeviceIdType.MESH
HIGHEST = lax.Precision.HIGHEST
LOG2E = 1.4426950408889634
LN2 = 0.6931471805599453


def _cp(*sem, **kw):
    if sem:
        kw["dimension_semantics"] = sem
    return pltpu.CompilerParams(vmem_limit_bytes=VMEM_LIMIT, **kw)


def _nn(a, b, **kw):
    return lax.dot_general(a, b, (((1,), (0,)), ((), ())), preferred_element_type=F32, **kw)


def _nt(a, b):
    return lax.dot_general(a, b, (((1,), (1,)), ((), ())), preferred_element_type=F32)


def _tn(a, b):
    return lax.dot_general(a, b, (((0,), (0,)), ((), ())), preferred_element_type=F32)


def _mx(x):
    return x.astype(MXU_DTYPE)


def _lane(shape):
    return lax.broadcasted_iota(jnp.int32, shape, 1)


def _sigmoid(z):
    return 1.0 / (1.0 + jnp.exp(-z))


def _rowsum128(x):
    s = jnp.sum(x, axis=0, keepdims=True)
    out = s[:, 0:128]
    for k in range(1, x.shape[1] // 128):
        out = out + s[:, 128 * k:128 * (k + 1)]
    return out


def _swap16(x):
    return jnp.where((_lane(x.shape) & 16) == 0, pltpu.roll(x, 112, 1), pltpu.roll(x, 16, 1))


def _rope(x, cos, sin):
    return x * cos + _swap16(x) * sin


def _rope_t(d, cos, sin):
    return d * cos + _swap16(d * sin)


def _blockdiag64():
    r = lax.broadcasted_iota(jnp.int32, (128, 128), 0) // 64
    c = lax.broadcasted_iota(jnp.int32, (128, 128), 1) // 64
    return (r == c).astype(BF16)


def _seg64(x, m):
    hi = x.astype(BF16)
    lo = (x - hi.astype(F32)).astype(BF16)
    return _nn(hi, m) + _nn(lo, m)


def _rope_tables(seq):
    t = np.arange(seq)
    row = (t // GRID_W).astype(np.float32)
    col = (t % GRID_W).astype(np.float32)
    half = HEAD_DIM // 2
    inv_freq = (np.float32(ROPE_THETA) ** (-np.arange(0, half, 2, dtype=np.float32) / np.float32(half))).astype(np.float32)
    ar = (row[:, None] * inv_freq[None, :]).astype(np.float32).astype(np.float64)
    ac = (col[:, None] * inv_freq[None, :]).astype(np.float32).astype(np.float64)
    cr, sr, cc, sc = np.cos(ar), np.sin(ar), np.cos(ac), np.sin(ac)
    cos64 = np.concatenate([cr, cr, cc, cc], axis=1)
    sin64 = np.concatenate([-sr, sr, -sc, sc], axis=1)
    return jnp.asarray(np.tile(cos64, (1, 2)), F32), jnp.asarray(np.tile(sin64, (1, 2)), F32)


def _attn_inputs(x, mod, g_pre, w_attn, cos, sin, qg2, kg2):
    seq = x.shape[0]
    bs = 512

    def body(x_ref, mod_ref, g_ref, w_ref, cos_ref, sin_ref, qg_ref, kg_ref, h_ref, pa_ref, qt_ref, kd_ref, vd_ref):
        xv = x_ref[...]
        r = lax.rsqrt(jnp.mean(xv * xv, axis=-1, keepdims=True) + EPS)
        shift = mod_ref[:, 0:D_MODEL]
        scale = mod_ref[:, D_MODEL:2 * D_MODEL]
        hb = ((xv * r) * g_ref[...] * (1.0 + scale) + shift).astype(h_ref.dtype)
        h_ref[...] = hb
        p = _nt(hb, w_ref[...])
        pa_ref[...] = p
        m = _blockdiag64()
        cs, sn = cos_ref[...], sin_ref[...]
        lo = _lane((bs, 128)) < 64
        for pr in range(4):
            q = p[:, QA + 128 * pr:QA + 128 * (pr + 1)]
            r = lax.rsqrt(_seg64(q * q, m) * (1.0 / 64) + EPS)
            qt_ref[:, 128 * pr:128 * (pr + 1)] = (_rope(q * r * qg_ref[...], cs, sn) * (0.125 * LOG2E)).astype(qt_ref.dtype)
        k = p[:, KA:KA + 128]
        r = lax.rsqrt(_seg64(k * k, m) * (1.0 / 64) + EPS)
        kr = _rope(k * r * kg_ref[...], cs, sn)
        ksw = pltpu.roll(kr, 64, 1)
        kd_ref[0] = jnp.where(lo, kr, ksw).astype(kd_ref.dtype)
        kd_ref[1] = jnp.where(lo, ksw, kr).astype(kd_ref.dtype)
        v = p[:, VA:VA + 128]
        vsw = pltpu.roll(v, 64, 1)
        vd_ref[0] = jnp.where(lo, v, vsw).astype(vd_ref.dtype)
        vd_ref[1] = jnp.where(lo, vsw, v).astype(vd_ref.dtype)

    row = lambda w: pl.BlockSpec((bs, w), lambda i: (i, 0))
    fix = lambda a, b: pl.BlockSpec((a, b), lambda i: (0, 0))
    dup = pl.BlockSpec((2, bs, 128), lambda i: (0, i, 0))
    sds = jax.ShapeDtypeStruct
    return pl.pallas_call(
        body, grid=(seq // bs,), name="attn_inputs",
        in_specs=[row(D_MODEL), fix(1, 3 * D_MODEL), fix(1, D_MODEL), fix(ATTN_COLS, D_MODEL), row(128), row(128), fix(1, 128), fix(1, 128)],
        out_specs=[row(D_MODEL), row(ATTN_COLS), row(512), dup, dup],
        out_shape=[sds((seq, D_MODEL), MXU_DTYPE), sds((seq, ATTN_COLS), F32), sds((seq, 512), MXU_DTYPE),
                   sds((2, seq, 128), MXU_DTYPE), sds((2, seq, 128), MXU_DTYPE)],
        compiler_params=_cp("parallel"))(x, mod, g_pre, w_attn, cos, sin, qg2, kg2)


def _in_proj_dx(x, dp, win_t, dout, mod, g_pre, dep=None):
    seq = x.shape[0]
    bs = 512
    deps, dep_specs = _dep_args(dep, 1)

    def body(x_ref, dp_ref, w_ref, dout_ref, mod_ref, g_ref, *rest):
        gx_ref, dshift_ref, dscale_ref, dg_ref = rest[-4:]

        @pl.when(pl.program_id(0) == 0)
        def _():
            dshift_ref[...] = jnp.zeros_like(dshift_ref)
            dscale_ref[...] = jnp.zeros_like(dscale_ref)
            dg_ref[...] = jnp.zeros_like(dg_ref)

        xv = x_ref[...]
        dh = _nn(dp_ref[...], w_ref[...])
        g = g_ref[...]
        r = lax.rsqrt(jnp.mean(xv * xv, axis=-1, keepdims=True) + EPS)
        xn = xv * r
        scale = mod_ref[:, D_MODEL:2 * D_MODEL]
        dshift_ref[...] += jnp.sum(dh, axis=0, keepdims=True)
        dscale_ref[...] += jnp.sum(dh * (xn * g), axis=0, keepdims=True)
        da = dh * (1.0 + scale)
        dg_ref[...] += jnp.sum(da * xn, axis=0, keepdims=True)
        dxn = da * g
        dx = r * (dxn - xn * jnp.mean(dxn * xn, axis=-1, keepdims=True))
        gx_ref[...] = dout_ref[...] + dx

    row = pl.BlockSpec((bs, D_MODEL), lambda i: (i, 0))
    vec = pl.BlockSpec((1, D_MODEL), lambda i: (0, 0))
    return pl.pallas_call(
        body, grid=(seq // bs,), name="in_proj_dx",
        in_specs=[row, pl.BlockSpec((bs, IN_WIDTH), lambda i: (i, 0)), pl.BlockSpec((IN_WIDTH, D_MODEL), lambda i: (0, 0)), row,
                  pl.BlockSpec((1, 3 * D_MODEL), lambda i: (0, 0)), vec] + dep_specs,
        out_specs=[row, vec, vec, vec],
        out_shape=[jax.ShapeDtypeStruct((seq, D_MODEL), F32)] + [jax.ShapeDtypeStruct((1, D_MODEL), F32)] * 3,
        compiler_params=_cp("arbitrary"))(x, dp, win_t, dout, mod, g_pre, *deps)


def _dep_args(dep, grid_rank):
    if dep is None:
        return [], []
    return [dep], [pl.BlockSpec(dep.shape, lambda *_: (0,) * dep.ndim)]


def _matmul(a, b, *, ta, tb, tm, tn, out_dtype, name, dep=None):
    kdim = a.shape[0] if ta else a.shape[1]
    m = a.shape[1] if ta else a.shape[0]
    n = b.shape[0] if tb else b.shape[1]
    assert m % tm == 0 and n % tn == 0
    deps, dep_specs = _dep_args(dep, 2)

    def body(a_ref, b_ref, *rest):
        o_ref = rest[-1]
        dims = (((0 if ta else 1,), (1 if tb else 0,)), ((), ()))
        o_ref[...] = lax.dot_general(a_ref[...], b_ref[...], dims, preferred_element_type=F32).astype(o_ref.dtype)

    a_spec = pl.BlockSpec((kdim, tm), lambda j, i: (0, i)) if ta else pl.BlockSpec((tm, kdim), lambda j, i: (i, 0))
    b_spec = pl.BlockSpec((tn, kdim), lambda j, i: (j, 0)) if tb else pl.BlockSpec((kdim, tn), lambda j, i: (0, j))
    return pl.pallas_call(
        body, grid=(n // tn, m // tm), name=name, in_specs=[a_spec, b_spec] + dep_specs,
        out_specs=pl.BlockSpec((tm, tn), lambda j, i: (i, j)),
        out_shape=jax.ShapeDtypeStruct((m, n), out_dtype), compiler_params=_cp("parallel", "parallel"))(a, b, *deps)


def _in_proj_rest(h, win_t, cos, sin):
    seq = h.shape[0]
    tm = min(1024, seq)
    ncols = IN_WIDTH - ATTN_COLS
    tn = ncols // 2
    assert ZR_B <= tn and ATTN_COLS % 128 == 0 and tn % 128 == 0

    def body(h_ref, w_ref, cos_ref, sin_ref, p_ref, rq_ref, rk_ref, rv_ref):
        p = _nt(h_ref[...], w_ref[...])
        p_ref[...] = p

        @pl.when(pl.program_id(0) == 0)
        def _():
            cs, sn = cos_ref[...], sin_ref[...]
            for pr in range(2):
                sl = slice(128 * pr, 128 * (pr + 1))
                rq_ref[:, sl] = _rope(p[:, QR_B + 128 * pr:QR_B + 128 * (pr + 1)], cs, sn).astype(rq_ref.dtype)
                rk_ref[:, sl] = (_rope(p[:, KR_B + 128 * pr:KR_B + 128 * (pr + 1)], cs, sn) * 0.125).astype(rk_ref.dtype)
            rv_ref[...] = p[:, VR_B:VR_B + 512].astype(rv_ref.dtype)

    nrow = seq // tm
    row = lambda w: pl.BlockSpec((tm, w), lambda j, i: (i, 0))
    park = lambda w: pl.BlockSpec((tm, w), lambda j, i: (jnp.where(j == 0, i, nrow - 1), 0))
    sds = jax.ShapeDtypeStruct
    return pl.pallas_call(
        body, grid=(2, nrow), name="in_proj_rest",
        in_specs=[row(D_MODEL), pl.BlockSpec((pl.Element(tn), pl.Element(D_MODEL)),
                                             lambda j, i: (pl.multiple_of(ATTN_COLS + j * tn, 128), 0)), row(128), row(128)],
        out_specs=[pl.BlockSpec((tm, tn), lambda j, i: (i, j)), park(256), park(256), park(512)],
        out_shape=[sds((seq, ncols), F32), sds((seq, 256), MXU_DTYPE), sds((seq, 256), MXU_DTYPE), sds((seq, 512), MXU_DTYPE)],
        compiler_params=_cp("arbitrary", "arbitrary"))(h, win_t, cos, sin)


def _halves(kd):
    lo = _lane(kd.shape) < 64
    z = jnp.zeros_like(kd)
    return jnp.concatenate([jnp.where(lo, kd, z), jnp.where(lo, z, kd)], axis=0)


def _attn_fwd(qt, kd, vd, dep=None):
    seq = qt.shape[0]
    bq, bk = min(2048, seq), min(1024, seq)
    nk = seq // bk
    deps, dep_specs = _dep_args(dep, 2)

    def body(q_ref, k_ref, v_ref, *rest):
        o_ref, lse_ref, m_scr, l_scr, acc = rest[-5:]
        j = pl.program_id(1)
        m_scr[...] = jnp.full_like(m_scr, -jnp.inf)
        l_scr[...] = jnp.zeros_like(l_scr)
        acc[...] = jnp.zeros_like(acc)
        sub = min(512, bq)
        nsub = bq // sub
        lo = _lane((sub, 128)) < 64

        def kv_block(n, carry):
            rows = pl.ds(pl.multiple_of(n * bk, bk), bk)
            k2, v2 = _halves(k_ref[rows, :]), _halves(v_ref[rows, :])
            for pr in range(2):
                for hf in range(nsub):
                    qr = slice(hf * sub, (hf + 1) * sub)
                    s2 = _nt(q_ref[qr, 128 * pr:128 * (pr + 1)], k2)
                    ps, alphas = [], []
                    for e in range(2):
                        g = 2 * pr + e
                        s = s2[:, e * bk:(e + 1) * bk]
                        m_prev = m_scr[g, qr]
                        m_next = jnp.maximum(m_prev, jnp.max(s, axis=1, keepdims=True))
                        alpha = jnp.exp2(m_prev - m_next)
                        pe = jnp.exp2(s - jnp.tile(m_next, (1, bk // 128)))
                        l_scr[g, qr] = alpha * l_scr[g, qr] + jnp.sum(pe, axis=1, keepdims=True)
                        m_scr[g, qr] = m_next
                        ps.append(_mx(pe))
                        alphas.append(alpha)
                    o2 = _nn(jnp.concatenate(ps, axis=1), v2)
                    sl = slice(128 * pr, 128 * (pr + 1))
                    acc[qr, sl] = acc[qr, sl] * jnp.where(lo, alphas[0], alphas[1]) + o2
            return carry

        lax.fori_loop(0, nk, kv_block, 0)
        lo_all = _lane((bq, 128)) < 64
        for pr in range(2):
            sl = slice(128 * pr, 128 * (pr + 1))
            o_ref[:, sl] = acc[:, sl] / jnp.where(lo_all, l_scr[2 * pr], l_scr[2 * pr + 1])
        for g in range(4):
            lse = jnp.transpose(m_scr[g] + jnp.log2(l_scr[g]))
            lse_ref[pl.ds(4 * j + g, 1), :] = lse[0:1, :]

    return pl.pallas_call(
        body, grid=(seq // bq, 2), name="attn_fwd",
        in_specs=[pl.BlockSpec((bq, 256), lambda i, j: (i, j)), pl.BlockSpec((None, seq, 128), lambda i, j: (j, 0, 0)),
                  pl.BlockSpec((None, seq, 128), lambda i, j: (j, 0, 0))] + dep_specs,
        out_specs=[pl.BlockSpec((bq, 256), lambda i, j: (i, j)), pl.BlockSpec((8, bq), lambda i, j: (0, i))],
        out_shape=[jax.ShapeDtypeStruct((seq, 512), F32), jax.ShapeDtypeStruct((8, seq), F32)],
        scratch_shapes=[pltpu.VMEM((4, bq, 128), F32), pltpu.VMEM((4, bq, 128), F32), pltpu.VMEM((bq, 256), F32)],
        compiler_params=_cp("parallel", "arbitrary"))(qt, kd, vd, *deps)


def _attn_bwd(qt, kd, vd, do, lse, delta, dep=None):
    seq = qt.shape[0]
    bq, bk = min(1024, seq), min(1024, seq)
    nq, nk = seq // bq, seq // bk
    deps, dep_specs = _dep_args(dep, 3)

    def body(q_ref, do_ref, k_ref, v_ref, lse_ref, del_ref, *rest):
        dq_ref, dk_ref, dv_ref = rest[-3:]
        j, i = pl.program_id(0), pl.program_id(1)
        dq_ref[...] = jnp.zeros_like(dq_ref)

        @pl.when(i == 0)
        def _():
            dk_ref[...] = jnp.zeros_like(dk_ref)
            dv_ref[...] = jnp.zeros_like(dv_ref)

        lo = _lane((bk, 128)) < 64
        big = lambda r: jnp.concatenate([jnp.broadcast_to(r[0], (bk, bq)), jnp.broadcast_to(r[1], (bk, bq))], axis=0)

        def kv_block(n, carry):
            rows = pl.ds(pl.multiple_of(n * bk, bk), bk)
            k2, v2 = _halves(k_ref[rows, :]), _halves(v_ref[rows, :])
            for pr in range(2):
                sl = slice(128 * pr, 128 * (pr + 1))
                qp, dop = q_ref[:, sl], do_ref[:, sl]
                s_t = _nt(k2, qp)
                dp_t = _nt(v2, dop)
                ls = [lse_ref[pl.ds(4 * j + 2 * pr + e, 1), :] for e in range(2)]
                dl = [del_ref[pl.ds(4 * j + 2 * pr + e, 1), :] for e in range(2)]
                p_t = jnp.exp2(s_t - big(ls))
                ds_t = _mx(p_t * (dp_t - big(dl)))
                rv = _nn(_mx(p_t), dop)
                rk = _nn(ds_t, qp) * LN2
                dv_ref[rows, :] += jnp.where(lo, rv[:bk], rv[bk:])
                dk_ref[rows, :] += jnp.where(lo, rk[:bk], rk[bk:])
                dq_ref[:, sl] += _tn(ds_t, k2)
            return carry

        lax.fori_loop(0, nk, kv_block, 0)

    return pl.pallas_call(
        body, grid=(2, nq), name="attn_bwd",
        in_specs=[pl.BlockSpec((bq, 256), lambda j, i: (i, j)), pl.BlockSpec((bq, 256), lambda j, i: (i, j)),
                  pl.BlockSpec((None, seq, 128), lambda j, i: (j, 0, 0)), pl.BlockSpec((None, seq, 128), lambda j, i: (j, 0, 0)),
                  pl.BlockSpec((8, bq), lambda j, i: (0, i)), pl.BlockSpec((8, bq), lambda j, i: (0, i))] + dep_specs,
        out_specs=[pl.BlockSpec((bq, 256), lambda j, i: (i, j)), pl.BlockSpec((None, seq, 128), lambda j, i: (j, 0, 0)),
                   pl.BlockSpec((None, seq, 128), lambda j, i: (j, 0, 0))],
        out_shape=[jax.ShapeDtypeStruct((seq, 512), F32), jax.ShapeDtypeStruct((2, seq, 128), F32),
                   jax.ShapeDtypeStruct((2, seq, 128), F32)],
        compiler_params=_cp("arbitrary", "arbitrary"))(qt, do, kd, vd, lse, delta, *deps)


def _ret_tables(lg_f, lg_b, c):
    idx = jnp.arange(c, dtype=F32)
    diff = idx[:, None] - idx[None, :]
    a, b = lg_f[:, None, None], lg_b[:, None, None]
    low, up = (diff >= 0)[None], (diff < 0)[None]
    dmat = jnp.where(low, jnp.exp(a * jnp.maximum(diff, 0.0)[None]), jnp.exp(b * jnp.maximum(-diff, 0.0)[None]))
    dda = jnp.where(low, diff[None] * jnp.exp(a * jnp.maximum(diff, 0.0)[None]), 0.0)
    ddb = jnp.where(up, -diff[None] * jnp.exp(b * jnp.maximum(-diff, 0.0)[None]), 0.0)
    rep = lambda w: jnp.broadcast_to(w[:, :, None], (RET_HEADS, c, 128))
    wqf = rep(jnp.exp(lg_f[:, None] * (idx + 1.0)[None]))
    wqb = rep(jnp.exp(lg_b[:, None] * (c - idx)[None]))
    wkf = rep(jnp.exp(lg_f[:, None] * (c - 1.0 - idx)[None]))
    wkb = rep(jnp.exp(lg_b[:, None] * idx[None]))
    cdf, cdb = jnp.exp(lg_f * c), jnp.exp(lg_b * c)
    dec_fb = jnp.broadcast_to(jnp.concatenate([cdf, cdb])[:, None], (8, 128))
    dec_bf = jnp.broadcast_to(jnp.concatenate([cdb, cdf])[:, None], (8, 128))
    return dict(dmat=dmat, dda=dda, ddb=ddb, wqf=wqf, wqb=wqb, wkf=wkf, wkb=wkb, dec_fb=dec_fb, dec_bf=dec_bf)


def _ret_states(xk, yv, w_fwd, w_rev, dec, name):
    seq = xk.shape[0]
    c = RET_CHUNK
    nc = seq // c
    grp = min(RET_GROUP, nc)
    ns = nc // grp

    def body(xf_ref, yf_ref, xr_ref, yr_ref, wf_ref, wr_ref, dec_ref, sf_ref, sr_ref, st_f, st_r):
        @pl.when(pl.program_id(0) == 0)
        def _():
            st_f[...] = jnp.zeros_like(st_f)
            st_r[...] = jnp.zeros_like(st_r)

        lane = _lane((c, 128))

        def chunk(g, carry):
            for x_ref, y_ref, w_ref, s_ref, st, base, gg in ((xf_ref, yf_ref, wf_ref, sf_ref, st_f, 0, g),
                                                             (xr_ref, yr_ref, wr_ref, sr_ref, st_r, 4, grp - 1 - g)):
                rows = pl.ds(pl.multiple_of(gg * c, c), c)
                for h in range(RET_HEADS):
                    pr, e = h // 2, h % 2
                    half = (lane < 64) if e == 0 else (lane >= 64)
                    s_ref[gg, h] = st[h].astype(s_ref.dtype)
                    xm = jnp.where(half, x_ref[rows, 128 * pr:128 * (pr + 1)].astype(F32) * w_ref[h], 0.0)
                    st[h] = st[h] * dec_ref[base + h:base + h + 1, :] + _tn(_mx(xm), _mx(y_ref[rows, 128 * h:128 * (h + 1)]))
            return carry

        lax.fori_loop(0, grp, chunk, 0)

    xs = lambda f: pl.BlockSpec((grp * c, 256), f)
    ys = lambda f: pl.BlockSpec((grp * c, 512), f)
    fwd, rev = (lambda n: (n, 0)), (lambda n: (ns - 1 - n, 0))
    wsp = pl.BlockSpec((RET_HEADS, c, 128), lambda n: (0, 0, 0))
    return pl.pallas_call(
        body, grid=(ns,), name=name,
        in_specs=[xs(fwd), ys(fwd), xs(rev), ys(rev), wsp, wsp, pl.BlockSpec((8, 128), lambda n: (0, 0))],
        out_specs=[pl.BlockSpec((grp, RET_HEADS, 128, 128), lambda n: (n, 0, 0, 0)),
                   pl.BlockSpec((grp, RET_HEADS, 128, 128), lambda n: (ns - 1 - n, 0, 0, 0))],
        out_shape=[jax.ShapeDtypeStruct((nc, RET_HEADS, 128, 128), MXU_DTYPE)] * 2,
        scratch_shapes=[pltpu.VMEM((RET_HEADS, 128, 128), F32), pltpu.VMEM((RET_HEADS, 128, 128), F32)],
        compiler_params=_cp("arbitrary"))(xk, yv, xk, yv, w_fwd, w_rev, dec)


def _ret_fwd(rq, rk, rv, rf, rb, tb):
    seq = rq.shape[0]
    c = RET_CHUNK
    grp = min(RET_GROUP, seq // c)

    def body(q_ref, k_ref, v_ref, rf_ref, rb_ref, d_ref, wqf_ref, wqb_ref, o_ref):
        lane = _lane((c, 128))

        def chunk(g, carry):
            rows = pl.ds(pl.multiple_of(g * c, c), c)
            for h in range(RET_HEADS):
                pr, e = h // 2, h % 2
                half = (lane < 64) if e == 0 else (lane >= 64)
                sl = slice(128 * pr, 128 * (pr + 1))
                qp, kp = q_ref[rows, sl], k_ref[rows, sl]
                km = jnp.where(half, kp, jnp.zeros_like(kp))
                sd = _mx(_nt(qp, km) * d_ref[h])
                qf = qp.astype(F32)
                o = _nn(sd, v_ref[rows, 128 * h:128 * (h + 1)])
                o = o + _nn(_mx(qf * wqf_ref[h]), rf_ref[g, h]) + _nn(_mx(qf * wqb_ref[h]), rb_ref[g, h])
                o_ref[rows, 128 * h:128 * (h + 1)] = o
            return carry

        lax.fori_loop(0, grp, chunk, 0)

    st = pl.BlockSpec((grp, RET_HEADS, 128, 128), lambda n: (n, 0, 0, 0))
    wsp = pl.BlockSpec((RET_HEADS, c, 128), lambda n: (0, 0, 0))
    return pl.pallas_call(
        body, grid=(seq // (grp * c),), name="ret_fwd",
        in_specs=[pl.BlockSpec((grp * c, 256), lambda n: (n, 0)), pl.BlockSpec((grp * c, 256), lambda n: (n, 0)),
                  pl.BlockSpec((grp * c, 512), lambda n: (n, 0)), st, st, pl.BlockSpec((RET_HEADS, c, c), lambda n: (0, 0, 0)), wsp, wsp],
        out_specs=pl.BlockSpec((grp * c, 512), lambda n: (n, 0)),
        out_shape=jax.ShapeDtypeStruct((seq, 512), F32), compiler_params=_cp("parallel"))(
            rq, rk, rv, rf, rb, tb["dmat"], tb["wqf"], tb["wqb"])


def _ret_bwd(rq, rk, rv, do, rf, rb, hf, hb, tb):
    seq = rq.shape[0]
    c = RET_CHUNK
    grp = min(RET_GROUP, seq // c)

    def body(q_ref, k_ref, v_ref, do_ref, rf_ref, rb_ref, hf_ref, hb_ref, d_ref, dda_ref, ddb_ref,
             wqf_ref, wqb_ref, wkf_ref, wkb_ref, cdec_ref, dq_ref, dk_ref, dv_ref, dlg_ref):
        @pl.when(pl.program_id(0) == 0)
        def _():
            dlg_ref[...] = jnp.zeros_like(dlg_ref)

        lane = _lane((c, 128))
        pos = lax.broadcasted_iota(jnp.int32, (c, 128), 0).astype(F32)

        def chunk(g, carry):
            rows = pl.ds(pl.multiple_of(g * c, c), c)
            for pr in range(2):
                sl = slice(128 * pr, 128 * (pr + 1))
                qp, kp = q_ref[rows, sl], k_ref[rows, sl]
                qf, kf = qp.astype(F32), kp.astype(F32)
                dq_acc = jnp.zeros((c, 128), F32)
                dk_acc = jnp.zeros((c, 128), F32)
                for e in range(2):
                    h = 2 * pr + e
                    half = (lane < 64) if e == 0 else (lane >= 64)
                    hs = slice(128 * h, 128 * (h + 1))
                    km = jnp.where(half, kp, jnp.zeros_like(kp))
                    kmf = km.astype(F32)
                    vh, doh = v_ref[rows, hs], do_ref[rows, hs]
                    rfh, rbh, hfh, hbh = rf_ref[g, h], rb_ref[g, h], hf_ref[g, h], hb_ref[g, h]
                    s = _nt(qp, km)
                    dpm = _nt(doh, vh)
                    ds_b = _mx(dpm * d_ref[h])
                    sd_b = _mx(s * d_ref[h])
                    dq_if = wqf_ref[h] * _nt(doh, rfh)
                    dq_ib = wqb_ref[h] * _nt(doh, rbh)
                    dq_acc = dq_acc + _nn(ds_b, km) + dq_if + dq_ib
                    dk_sf = wkf_ref[h] * _nt(vh, hfh)
                    dk_sb = wkb_ref[h] * _nt(vh, hbh)
                    dk_acc = dk_acc + jnp.where(half, _tn(ds_b, qp), 0.0) + dk_sf + dk_sb
                    dv = _tn(sd_b, doh) + _nn(_mx(kmf * wkf_ref[h]), hfh) + _nn(_mx(kmf * wkb_ref[h]), hbh)
                    dv_ref[rows, hs] = dv.astype(dv_ref.dtype)
                    sdp = s * dpm
                    hr_f = hfh.astype(F32) * rfh.astype(F32)
                    hr_b = hbh.astype(F32) * rbh.astype(F32)
                    da = (_rowsum128(sdp * dda_ref[h]) + _rowsum128((pos + 1.0) * qf * dq_if)
                          + _rowsum128((c - 1.0 - pos) * kf * dk_sf) + cdec_ref[h:h + 1, :] * _rowsum128(hr_f))
                    db = (_rowsum128(sdp * ddb_ref[h]) + _rowsum128((c - pos) * qf * dq_ib)
                          + _rowsum128(pos * kf * dk_sb) + cdec_ref[4 + h:5 + h, :] * _rowsum128(hr_b))
                    dlg_ref[h:h + 1, :] += da
                    dlg_ref[4 + h:5 + h, :] += db
                dq_ref[rows, sl] = dq_acc
                dk_ref[rows, sl] = dk_acc
            return carry

        lax.fori_loop(0, grp, chunk, 0)

    st = pl.BlockSpec((grp, RET_HEADS, 128, 128), lambda n: (n, 0, 0, 0))
    wsp = pl.BlockSpec((RET_HEADS, c, 128), lambda n: (0, 0, 0))
    dsp = pl.BlockSpec((RET_HEADS, c, c), lambda n: (0, 0, 0))
    x256 = pl.BlockSpec((grp * c, 256), lambda n: (n, 0))
    x512 = pl.BlockSpec((grp * c, 512), lambda n: (n, 0))
    cdec = tb["dec_fb"] * float(c)
    return pl.pallas_call(
        body, grid=(seq // (grp * c),), name="ret_bwd",
        in_specs=[x256, x256, x512, x512, st, st, st, st, dsp, dsp, dsp, wsp, wsp, wsp, wsp,
                  pl.BlockSpec((8, 128), lambda n: (0, 0))],
        out_specs=[x256, x256, x512, pl.BlockSpec((8, 128), lambda n: (0, 0))],
        out_shape=[jax.ShapeDtypeStruct((seq, 256), F32), jax.ShapeDtypeStruct((seq, 256), F32),
                   jax.ShapeDtypeStruct((seq, 512), MXU_DTYPE), jax.ShapeDtypeStruct((8, 128), F32)],
        compiler_params=_cp("arbitrary"))(
            rq, rk, rv, do, rf, rb, hf, hb, tb["dmat"], tb["dda"], tb["ddb"], tb["wqf"], tb["wqb"], tb["wkf"], tb["wkb"], cdec)


def _tail(x, tgt, o_att, o_ret, p, mod, g_post, gn_g, wpt, wout):
    seq = x.shape[0]
    bs = 256
    nb = seq // bs

    def body(x_ref, t_ref, oa_ref, or_ref, za_ref, zr_ref, gl_ref, mod_ref, gp_ref, gn_ref, wpt_ref, wout_ref,
             dout_ref, dza_ref, dzr_ref, dgl_ref, doa_ref, dor_ref, del_ref, gout_ref, gpt_ref,
             dgate_ref, dgpost_ref, dgn_ref, loss_ref, gout_acc, gpt_acc):
        i = pl.program_id(0)

        @pl.when(i == 0)
        def _():
            gout_acc[...] = jnp.zeros_like(gout_acc)
            gpt_acc[...] = jnp.zeros_like(gpt_acc)
            dgate_ref[...] = jnp.zeros_like(dgate_ref)
            dgpost_ref[...] = jnp.zeros_like(dgpost_ref)
            dgn_ref[...] = jnp.zeros_like(dgn_ref)
            loss_ref[...] = jnp.zeros_like(loss_ref)

        oa, za, zr = oa_ref[...], za_ref[...], zr_ref[...]
        sga, sgr = _sigmoid(za), _sigmoid(zr)
        sza, szr = za * sga, zr * sgr
        ya_b = _mx(oa * sza)
        ons, rstds = [], []
        for h in range(RET_HEADS):
            oh = or_ref[:, 128 * h:128 * (h + 1)]
            xc = oh - jnp.mean(oh, axis=-1, keepdims=True)
            rstd = lax.rsqrt(jnp.mean(xc * xc, axis=-1, keepdims=True) + EPS)
            ons.append(xc * rstd)
            rstds.append(rstd)
        on = jnp.concatenate(ons, axis=1)
        yn = on * gn_ref[...]
        yr_b = _mx(yn * szr)
        wpa_t, wpr_t = wpt_ref[:, 0:512], wpt_ref[:, 512:1024]
        a_att = _nt(ya_b, wpa_t)
        a_ret = _nt(yr_b, wpr_t)
        gts = _sigmoid(gl_ref[...])
        g_att, g_ret = gts[:, 0:D_MODEL], gts[:, D_MODEL:2 * D_MODEL]
        merged_b = _mx(g_att * a_att + g_ret * a_ret)
        u = _nn(merged_b, wout_ref[...])
        r = lax.rsqrt(jnp.mean(u * u, axis=-1, keepdims=True) + EPS)
        un = u * r
        gpost = gp_ref[...]
        y = un * gpost
        gate = mod_ref[:, 2 * D_MODEL:3 * D_MODEL]
        err = x_ref[...] + gate * y - t_ref[...]
        loss_ref[...] += (0.5 / D_MODEL) * _rowsum128(err * err)
        dout = err * (1.0 / D_MODEL)
        dout_ref[...] = dout
        dgate_ref[...] += jnp.sum(dout * y, axis=0, keepdims=True)
        dy = dout * gate
        dgpost_ref[...] += jnp.sum(dy * un, axis=0, keepdims=True)
        dun = dy * gpost
        du_b = _mx(r * (dun - un * jnp.mean(dun * un, axis=-1, keepdims=True)))
        dmerged = _nt(du_b, wout_ref[...])
        gout_acc[...] += _tn(merged_b, du_b)
        d_att, d_ret = dmerged * g_att, dmerged * g_ret
        dgl_ref[:, 0:D_MODEL] = (d_att * a_att * (1.0 - g_att)).astype(dgl_ref.dtype)
        dgl_ref[:, D_MODEL:2 * D_MODEL] = (d_ret * a_ret * (1.0 - g_ret)).astype(dgl_ref.dtype)
        d_att_b, d_ret_b = _mx(d_att), _mx(d_ret)
        dya = _nn(d_att_b, wpa_t)
        dyr = _nn(d_ret_b, wpr_t)
        gpt_acc[:, 0:512] += _tn(d_att_b, ya_b)
        gpt_acc[:, 512:1024] += _tn(d_ret_b, yr_b)
        dza_ref[...] = (dya * oa * (sga * (1.0 + za * (1.0 - sga)))).astype(dza_ref.dtype)
        doa = dya * sza
        doa_ref[...] = doa.astype(doa_ref.dtype)
        dt = jnp.transpose(doa * oa)
        del_ref[...] = jnp.sum(dt.reshape(8, HEAD_DIM, bs), axis=1)
        dzr_ref[...] = (dyr * yn * (sgr * (1.0 + zr * (1.0 - sgr)))).astype(dzr_ref.dtype)
        dyn = dyr * szr
        dgn_ref[...] += jnp.sum(dyn * on, axis=0, keepdims=True)
        don = dyn * gn_ref[...]
        for h in range(RET_HEADS):
            hs = slice(128 * h, 128 * (h + 1))
            dh, oh = don[:, hs], ons[h]
            doh = rstds[h] * (dh - jnp.mean(dh, axis=-1, keepdims=True) - oh * jnp.mean(dh * oh, axis=-1, keepdims=True))
            dor_ref[:, hs] = doh.astype(dor_ref.dtype)

        @pl.when(i == nb - 1)
        def _():
            gout_ref[...] = gout_acc[...].astype(gout_ref.dtype)
            gpt_ref[...] = gpt_acc[...].astype(gpt_ref.dtype)

    row = lambda w: pl.BlockSpec((bs, w), lambda i: (i, 0))
    el = lambda w, off: pl.BlockSpec((pl.Element(bs), pl.Element(w)), lambda i: (i * bs, off))
    vec = lambda w: pl.BlockSpec((1, w), lambda i: (0, 0))
    full = pl.BlockSpec((D_MODEL, D_MODEL), lambda i: (0, 0))
    sds = jax.ShapeDtypeStruct
    return pl.pallas_call(
        body, grid=(nb,), name="tail",
        in_specs=[row(D_MODEL), row(D_MODEL), row(512), row(512), el(512, ZA_B), el(512, ZR_B), el(2 * D_MODEL, GL_B),
                  vec(3 * D_MODEL), vec(D_MODEL), vec(512), full, full],
        out_specs=[row(D_MODEL), row(512), row(512), row(2 * D_MODEL), row(512), row(512),
                   pl.BlockSpec((8, bs), lambda i: (0, i)), full, full, vec(D_MODEL), vec(D_MODEL), vec(512), vec(128)],
        out_shape=[sds((seq, D_MODEL), F32), sds((seq, 512), MXU_DTYPE), sds((seq, 512), MXU_DTYPE),
                   sds((seq, 2 * D_MODEL), MXU_DTYPE), sds((seq, 512), MXU_DTYPE), sds((seq, 512), MXU_DTYPE),
                   sds((8, seq), F32), sds((D_MODEL, D_MODEL), MXU_DTYPE), sds((D_MODEL, D_MODEL), MXU_DTYPE),
                   sds((1, D_MODEL), F32), sds((1, D_MODEL), F32), sds((1, 512), F32), sds((1, 128), F32)],
        scratch_shapes=[pltpu.VMEM((D_MODEL, D_MODEL), F32), pltpu.VMEM((D_MODEL, D_MODEL), F32)],
        compiler_params=_cp("arbitrary"))(x, tgt, o_att, o_ret, p, p, p, mod, g_post, gn_g, wpt, wout)


def _assemble(p, cos, sin, qg2, kg2, dqt, dkp, dvp, dza, drq, drk, drv, dzr, dgl):
    seq = p.shape[0]
    bs = 512

    def body(p_ref, cos_ref, sin_ref, qg_ref, kg_ref, dqt_ref, dkp_ref, dvp_ref, dza_ref, drq_ref, drk_ref, drv_ref,
             dzr_ref, dgl_ref, dp_ref, dqg_ref, dkg_ref):
        @pl.when(pl.program_id(0) == 0)
        def _():
            dqg_ref[...] = jnp.zeros_like(dqg_ref)
            dkg_ref[...] = jnp.zeros_like(dkg_ref)

        m = _blockdiag64()
        cs, sn = cos_ref[...], sin_ref[...]
        lo = _lane((bs, 128)) < 64

        def norm_bwd(raw, dn, g, dg_ref):
            r = lax.rsqrt(_seg64(raw * raw, m) * (1.0 / 64) + EPS)
            xn = raw * r
            dg_ref[...] += jnp.sum(dn * xn, axis=0, keepdims=True)
            dxn = dn * g
            return r * (dxn - xn * (_seg64(dxn * xn, m) * (1.0 / 64)))

        for pr in range(4):
            sl = slice(128 * pr, 128 * (pr + 1))
            dn = _rope_t(dqt_ref[:, sl] * 0.125, cs, sn)
            dp_ref[:, QA + 128 * pr:QA + 128 * (pr + 1)] = norm_bwd(p_ref[:, sl], dn, qg_ref[...], dqg_ref).astype(dp_ref.dtype)
        fold = lambda a: a + pltpu.roll(a, 64, 1)
        dk = jnp.where(lo, fold(dkp_ref[0]), fold(dkp_ref[1]))
        dp_ref[:, KA:KA + 128] = norm_bwd(p_ref[:, KA:KA + 128], _rope_t(dk, cs, sn), kg_ref[...], dkg_ref).astype(dp_ref.dtype)
        dp_ref[:, VA:VA + 128] = jnp.where(lo, fold(dvp_ref[0]), fold(dvp_ref[1])).astype(dp_ref.dtype)
        dp_ref[:, ZA:ZA + 512] = dza_ref[...]
        for pr in range(2):
            sl = slice(128 * pr, 128 * (pr + 1))
            dp_ref[:, QR + 128 * pr:QR + 128 * (pr + 1)] = _rope_t(drq_ref[:, sl], cs, sn).astype(dp_ref.dtype)
            dp_ref[:, KR + 128 * pr:KR + 128 * (pr + 1)] = _rope_t(drk_ref[:, sl] * 0.125, cs, sn).astype(dp_ref.dtype)
        dp_ref[:, VR:VR + 512] = drv_ref[...]
        dp_ref[:, ZR:ZR + 512] = dzr_ref[...]
        dp_ref[:, GL:GL + 2 * D_MODEL] = dgl_ref[...]

    row = lambda w: pl.BlockSpec((bs, w), lambda i: (i, 0))
    gsp = pl.BlockSpec((1, 128), lambda i: (0, 0))
    dup = pl.BlockSpec((2, bs, 128), lambda i: (0, i, 0))
    return pl.pallas_call(
        body, grid=(seq // bs,), name="assemble_dp",
        in_specs=[row(768), row(128), row(128), gsp, gsp, row(512), dup, dup, row(512), row(256), row(256), row(512),
                  row(512), row(2 * D_MODEL)],
        out_specs=[row(IN_WIDTH), gsp, gsp],
        out_shape=[jax.ShapeDtypeStruct((seq, IN_WIDTH), MXU_DTYPE), jax.ShapeDtypeStruct((1, 128), F32),
                   jax.ShapeDtypeStruct((1, 128), F32)],
        compiler_params=_cp("arbitrary"))(p, cos, sin, qg2, kg2, dqt, dkp, dvp, dza, drq, drk, drv, dzr, dgl)


def _local_step(x, tgt, mod, g_pre, qn_g, kn_g, w_dec_f, w_dec_b, gn_g, g_post, w_attn, rest_start, rest_wait, send=None):
    send = send or (lambda name, arrays: None)
    seq = x.shape[0]
    cos, sin = _rope_tables(seq)
    qg2, kg2 = jnp.tile(qn_g, (1, 2)), jnp.tile(kn_g, (1, 2))
    lg_f = jax.nn.log_sigmoid(w_dec_f[0])
    lg_b = jax.nn.log_sigmoid(w_dec_b[0])
    tb = _ret_tables(lg_f, lg_b, RET_CHUNK)

    h, p_a, qt, kd, vd = _attn_inputs(x, mod, g_pre, w_attn, cos, sin, qg2, kg2)
    o_att, lse = _attn_fwd(qt, kd, vd, dep=rest_start(qt))
    win_t, wpt, wout = rest_wait(o_att)
    p_b, rq, rk, rv = _in_proj_rest(h, win_t, cos, sin)
    rf, rb = _ret_states(rk, rv, tb["wkf"], tb["wkb"], tb["dec_fb"], "ret_states_fwd")
    o_ret = _ret_fwd(rq, rk, rv, rf, rb, tb)
    (dout, dza, dzr, dgl, do_att, do_ret, delta, g_out, g_pt, dgate, dgpost, dgn, loss_l) = _tail(
        x, tgt, o_att, o_ret, p_b, mod, g_post, gn_g, wpt, wout)
    dep = send("early", (g_pt, g_out))
    dqt, dkp, dvp = _attn_bwd(qt, kd, vd, do_att, lse, delta, dep=dep)
    hb, hf = _ret_states(rq, do_ret, tb["wqb"], tb["wqf"], tb["dec_bf"], "ret_states_bwd")
    drq, drk, drv, dlg = _ret_bwd(rq, rk, rv, do_ret, rf, rb, hf, hb, tb)
    dp, dqg, dkg = _assemble(p_a, cos, sin, qg2, kg2, dqt, dkp, dvp, dza, drq, drk, drv, dzr, dgl)
    g_in_t = _matmul(dp, h, ta=True, tb=False, tm=256, tn=D_MODEL, out_dtype=MXU_DTYPE, name="in_proj_dw")
    dep = send("late", (g_in_t,))
    grad_x, dshift, dscale, dgpre = _in_proj_dx(x, dp, win_t, dout, mod, g_pre, dep=dep)

    dlg = jnp.sum(dlg, axis=1)
    small = dict(
        dmod=jnp.concatenate([dshift, dscale, dgate], axis=1),
        g_pre=dgpre, g_post=dgpost, gn_g=dgn,
        qn_g=dqg[:, :64] + dqg[:, 64:], kn_g=dkg[:, :64] + dkg[:, 64:],
        w_dec_f=(dlg[0:4] * jax.nn.sigmoid(-w_dec_f[0]))[None], w_dec_b=(dlg[4:8] * jax.nn.sigmoid(-w_dec_b[0]))[None],
        loss=jnp.sum(loss_l, axis=1, keepdims=True))
    return grad_x, g_in_t, g_pt, g_out, small


N_DEV = 8
N_CHIP = 4
HBM_SPEC = pl.BlockSpec(memory_space=pl.ANY)
VMEM_SPEC = pl.BlockSpec(memory_space=pltpu.VMEM)
SHARD_ROWS = (IN_WIDTH // N_CHIP, D_MODEL // N_CHIP, D_MODEL // N_CHIP)


def _coords():
    return lax.axis_index("x"), lax.axis_index("y"), lax.axis_index("c")


def _peer(k):
    x, y, c = _coords()
    return (1 - x if k & 4 else x, 1 - y if k & 2 else y, 1 - c if k & 1 else c)


def _dev_index(p):
    return 4 * p[0] + 2 * p[1] + p[2]


def _rows(ref, start, size):
    return ref.at[pl.ds(pl.multiple_of(start, 16), size), :]


def _remote(src, dst, ssem, rsem, dev):
    return pltpu.make_async_remote_copy(src_ref=src, dst_ref=dst, send_sem=ssem, recv_sem=rsem, device_id=dev,
                                        device_id_type=MESH)


ATTN_CHUNK = 96


def _mod_and_attn_rows(c, w_ada_s, b4, win_s):
    ncol = w_ada_s.shape[1]
    half = ATTN_COLS // 2
    offs = list(range(0, half, ATTN_CHUNK))
    nq = len(offs)
    rows = lambda ref, cc, off: ref.at[pl.ds(pl.multiple_of(cc * half + off, 16), ATTN_CHUNK), :]

    def body(c_ref, w_ref, b_ref, src, cs_ref, mod_ref, out, modsh, modall, ssem, rsem, lsem, asem, bsem, fsem, gsem, hsem):
        x, y, cc = _coords()
        me = _dev_index((x, y, cc))
        chip = me // 2
        sib = (x, y, 1 - cc)
        cs_ref[me] = c_ref[...]
        sends = []
        for k in range(1, N_DEV):
            cp = _remote(c_ref, cs_ref.at[me], ssem.at[k - 1], rsem.at[k - 1], _peer(k))
            cp.start()
            sends.append(cp)
        own = pltpu.make_async_copy(src.at[pl.ds(0, ATTN_COLS), :], out, lsem.at[0])
        bulk = [_remote(rows(src, cc, off), rows(out, cc, off), asem.at[(k2 - 1) * nq + q], bsem.at[q], (k2 // 2, k2 % 2, cc))
                for k2 in (1, 2) for q, off in enumerate(offs)]
        relay_chip = lambda q: 1 + q % 2

        @pl.when(chip == 0)
        def _():
            own.start()
            for cp in bulk:
                cp.start()

        for k in range(1, N_DEV):
            slot = cs_ref.at[_dev_index(_peer(k))]
            _remote(slot, slot, ssem.at[k - 1], rsem.at[k - 1], _peer(k)).wait_recv()
        cs = jnp.concatenate([cs_ref[d] for d in range(N_DEV)], axis=0)
        ms = _nn(cs * _sigmoid(cs), w_ref[...], precision=HIGHEST) + b_ref[pl.ds(chip, 1), :]
        modsh[...] = ms
        modall[chip] = ms
        for k2 in range(1, N_CHIP):
            cp = _remote(modsh, modall.at[chip], ssem.at[6 + k2], rsem.at[6 + k2], _peer(2 * k2))
            cp.start()
            sends.append(cp)

        @pl.when(chip != 0)
        def _():
            passed = []
            relays = [_remote(rows(out, cc, off), rows(out, cc, off), hsem.at[q], bsem.at[q], (1, 1, cc)) for q, off in enumerate(offs)]
            for q, off in enumerate(offs):
                got = rows(out, cc, off)
                _remote(got, got, asem.at[q], bsem.at[q], (0, 0, cc)).wait_recv()
                cp = _remote(got, got, fsem.at[q], gsem.at[q], sib)
                cp.start()
                passed.append(cp)
                pl.when(chip == relay_chip(q))(relays[q].start)
            for q, off in enumerate(offs):
                got = rows(out, 1 - cc, off)
                _remote(got, got, fsem.at[q], gsem.at[q], sib).wait_recv()
            for cp in passed:
                cp.wait_send()
            for q in range(nq):
                pl.when(chip == relay_chip(q))(relays[q].wait_send)

        for k2 in range(1, N_CHIP):
            slot = modall.at[_dev_index(_peer(2 * k2)) // 2]
            _remote(slot, slot, ssem.at[6 + k2], rsem.at[6 + k2], _peer(2 * k2)).wait_recv()
        for jj in range(N_CHIP):
            mod_ref[:, ncol * jj:ncol * (jj + 1)] = modall[jj, pl.ds(me, 1), :]
        for cp in sends:
            cp.wait_send()

        @pl.when(chip == 0)
        def _():
            for cp in bulk:
                cp.wait_send()
            own.wait()

    dma = pltpu.SemaphoreType.DMA
    return pl.pallas_call(
        body, name="mod_and_attn_rows", in_specs=[VMEM_SPEC] * 4, out_specs=[VMEM_SPEC, VMEM_SPEC, HBM_SPEC],
        out_shape=[jax.ShapeDtypeStruct((N_DEV, 1, D_MODEL), F32), jax.ShapeDtypeStruct((1, N_CHIP * ncol), F32),
                   jax.ShapeDtypeStruct((ATTN_COLS, win_s.shape[1]), win_s.dtype)],
        scratch_shapes=[pltpu.VMEM((N_DEV, ncol), F32), pltpu.VMEM((N_CHIP, N_DEV, ncol), F32), dma((10,)), dma((10,)),
                        dma((1,)), dma((2 * nq,)), dma((nq,)), dma((nq,)), dma((nq,)), dma((nq,))],
        compiler_params=_cp())(c, w_ada_s, b4, win_s)


GATHER_CHUNK = 32


def _chunks(kinds, chunk):
    out = []
    for t, kind in enumerate(kinds):
        half = SHARD_ROWS[kind] // 2
        step = chunk if half % chunk == 0 else half
        out += [(t, off, step) for off in range(0, half, step)]
    return out


SEM_SPEC = pl.BlockSpec(memory_space=pltpu.SEMAPHORE)
HBM_ONLY = pl.BlockSpec(memory_space=pltpu.HBM)
DATAFLOW = pltpu.SideEffectType.DATAFLOW_SIDE_EFFECTING


def _place_own(shards):
    nt = len(shards)

    def body(*refs):
        srcs, outs, lsem = refs[:nt], refs[nt:2 * nt], refs[2 * nt]
        x, y, _ = _coords()
        chip = 2 * x + y
        local = [pltpu.make_async_copy(srcs[t], _rows(outs[t], chip * SHARD_ROWS[t], SHARD_ROWS[t]), lsem.at[t]) for t in range(nt)]
        for cp in local:
            cp.start()
        for cp in local:
            cp.wait()

    return pl.pallas_call(
        body, name="place_own_shard", in_specs=[VMEM_SPEC] * nt, out_specs=[HBM_SPEC] * nt,
        out_shape=[jax.ShapeDtypeStruct((N_CHIP * SHARD_ROWS[t], s.shape[1]), s.dtype) for t, s in enumerate(shards)],
        scratch_shapes=[pltpu.SemaphoreType.DMA((nt,))], compiler_params=_cp())(*shards)


def _gather_rest_start(shards, zones, dep):
    n = len(shards)

    def body(*refs):
        srcs, lands = refs[:n], refs[n:2 * n]
        ssem, rsem = refs[2 * n + 1], refs[2 * n + 2]
        token = refs[-1]
        x, y, _ = _coords()
        chip = 2 * x + y
        for k2 in range(1, N_CHIP):
            for t in range(n):
                q = (k2 - 1) * n + t
                _remote(srcs[t], _rows(lands[t], chip * SHARD_ROWS[t], SHARD_ROWS[t]), ssem.at[q], rsem.at[q], _peer(2 * k2)).start()
        token[...] = jnp.zeros_like(token)

    hbm = lambda a: pltpu.with_memory_space_constraint(a, pltpu.HBM)
    dma = pltpu.SemaphoreType.DMA
    outs = pl.pallas_call(
        body, name="gather_rest", in_specs=[HBM_ONLY] * (2 * n) + [HBM_SPEC],
        out_specs=[SEM_SPEC, SEM_SPEC] + [HBM_ONLY] * (2 * n) + [VMEM_SPEC],
        out_shape=[dma((3 * n,)), dma((3 * n,))] + [pltpu.HBM(a.shape, a.dtype) for a in list(shards) + list(zones)]
        + [jax.ShapeDtypeStruct((8, 128), F32)],
        input_output_aliases={i: 2 + i for i in range(2 * n)},
        compiler_params=pltpu.CompilerParams(has_side_effects=DATAFLOW))(*[hbm(a) for a in list(shards) + list(zones)], dep)
    return outs[0], outs[1], outs[2:2 + n], outs[2 + n:2 + 2 * n], outs[-1]


def _gather_rest_wait(started, after):
    ssem, rsem, shards, zones, _ = started
    n = len(shards)

    def body(*refs):
        srcs, lands = refs[:n], refs[n:2 * n]
        ssem_ref, rsem_ref = refs[2 * n], refs[2 * n + 1]
        for k2 in range(1, N_CHIP):
            pchip = _dev_index(_peer(2 * k2)) // 2
            for t in range(n):
                q = (k2 - 1) * n + t
                cp = _remote(srcs[t], _rows(lands[t], pchip * SHARD_ROWS[t], SHARD_ROWS[t]), ssem_ref.at[q], rsem_ref.at[q], _peer(2 * k2))
                cp.wait_send()
                cp.wait_recv()

    outs = pl.pallas_call(
        body, name="gather_rest_wait", in_specs=[HBM_ONLY] * (2 * n) + [SEM_SPEC, SEM_SPEC, HBM_SPEC], out_specs=[HBM_ONLY] * (2 * n),
        out_shape=[pltpu.HBM(a.shape, a.dtype) for a in list(shards) + list(zones)],
        input_output_aliases={i: i for i in range(2 * n)},
        compiler_params=pltpu.CompilerParams(has_side_effects=DATAFLOW))(*shards, *zones, ssem, rsem, after)
    return outs[n:]


def _reduced_by(ref, t, dev):
    return _rows(ref, (dev // 2) * SHARD_ROWS[t] + (dev % 2) * (SHARD_ROWS[t] // 2), SHARD_ROWS[t] // 2)


def _scatter_start(grads, kinds, name):
    n = len(grads)

    def body(*refs):
        srcs, lands = refs[:n], refs[n:2 * n]
        ssem, rsem = refs[2 * n], refs[2 * n + 1]
        token = refs[-1]
        me = _dev_index(_coords())
        for k in range(1, N_DEV):
            for i, t in enumerate(kinds):
                q = (k - 1) * n + i
                _remote(_reduced_by(srcs[i], t, _dev_index(_peer(k))), lands[i].at[me], ssem.at[q], rsem.at[q], _peer(k)).start()
        token[...] = jnp.zeros_like(token)

    zones = [lax.empty((N_DEV, SHARD_ROWS[t] // 2, g.shape[1]), g.dtype) for g, t in zip(grads, kinds)]
    hbm = lambda a: pltpu.with_memory_space_constraint(a, pltpu.HBM)
    dma = pltpu.SemaphoreType.DMA
    outs = pl.pallas_call(
        body, name=name, in_specs=[HBM_ONLY] * (2 * n), out_specs=[SEM_SPEC, SEM_SPEC] + [HBM_ONLY] * (2 * n) + [VMEM_SPEC],
        out_shape=[dma((7 * n,)), dma((7 * n,))] + [pltpu.HBM(a.shape, a.dtype) for a in list(grads) + zones]
        + [jax.ShapeDtypeStruct((8, 128), F32)],
        input_output_aliases={i: 2 + i for i in range(2 * n)},
        compiler_params=pltpu.CompilerParams(has_side_effects=DATAFLOW))(*[hbm(a) for a in list(grads) + zones])
    return outs[0], outs[1], outs[2:2 + n], outs[2 + n:2 + 2 * n], outs[-1]


def _scatter_wait(started, kinds, after, name):
    ssem, rsem, grads, zones, _ = started
    n = len(grads)

    def body(*refs):
        srcs, lands = refs[:n], refs[n:2 * n]
        ssem_ref, rsem_ref = refs[2 * n], refs[2 * n + 1]
        for k in range(1, N_DEV):
            peer = _dev_index(_peer(k))
            for i, t in enumerate(kinds):
                q = (k - 1) * n + i
                cp = _remote(_reduced_by(srcs[i], t, peer), lands[i].at[peer], ssem_ref.at[q], rsem_ref.at[q], _peer(k))
                cp.wait_send()
                cp.wait_recv()

    outs = pl.pallas_call(
        body, name=name, in_specs=[HBM_ONLY] * (2 * n) + [SEM_SPEC, SEM_SPEC, HBM_SPEC], out_specs=[HBM_ONLY] * (2 * n),
        out_shape=[pltpu.HBM(a.shape, a.dtype) for a in list(grads) + list(zones)],
        input_output_aliases={i: i for i in range(2 * n)},
        compiler_params=pltpu.CompilerParams(has_side_effects=DATAFLOW))(*grads, *zones, ssem, rsem, after)
    return outs[:n], outs[n:]


def _grad_finish(grads, zones, kinds, name):
    nt = len(grads)
    pieces = _chunks(kinds, GATHER_CHUNK)
    npc = len(pieces)
    shard = [SHARD_ROWS[k] for k in kinds]

    def body(*refs):
        srcs, lands, outs = refs[:nt], refs[nt:2 * nt], refs[2 * nt:3 * nt]
        slots, sums = refs[3 * nt:4 * nt], refs[4 * nt:5 * nt]
        lsem, osem, xsem, ysem = refs[5 * nt:]
        x, y, c = _coords()
        me = _dev_index((x, y, c))
        sib = (x, y, 1 - c)
        loads = [pltpu.make_async_copy(_reduced_by(srcs[t], kinds[t], me), slots[t].at[me], lsem.at[t]) for t in range(nt)]
        for k in range(1, N_DEV):
            peer = _dev_index(_peer(k))
            loads += [pltpu.make_async_copy(lands[t].at[peer], slots[t].at[peer], lsem.at[k * nt + t]) for t in range(nt)]
        for cp in loads:
            cp.start()
        for cp in loads:
            cp.wait()
        sends = []
        place = lambda t, cc, off, n: _rows(outs[t], cc * (shard[t] // 2) + off, n)
        stores = []
        for q, (t, off, n) in enumerate(pieces):
            r = pl.ds(off, n)
            acc = slots[t][0, r, :].astype(F32)
            for d in range(1, N_DEV):
                acc = acc + slots[t][d, r, :].astype(F32)
            sums[t][r, :] = acc
            keep = pltpu.make_async_copy(sums[t].at[r, :], place(t, c, off, n), osem.at[q])
            give = _remote(sums[t].at[r, :], place(t, c, off, n), xsem.at[q], ysem.at[q], sib)
            keep.start()
            give.start()
            stores.append(keep)
            sends.append(give)
        for q, (t, off, n) in enumerate(pieces):
            got = place(t, 1 - c, off, n)
            _remote(got, got, xsem.at[q], ysem.at[q], sib).wait_recv()
        for cp in sends:
            cp.wait_send()
        for cp in stores:
            cp.wait()

    dma = pltpu.SemaphoreType.DMA
    return pl.pallas_call(
        body, name=name, in_specs=[HBM_SPEC] * (2 * nt), out_specs=[HBM_SPEC] * nt,
        out_shape=[jax.ShapeDtypeStruct((shard[t], g.shape[1]), F32) for t, g in enumerate(grads)],
        scratch_shapes=([pltpu.VMEM((N_DEV, shard[t] // 2, g.shape[1]), g.dtype) for t, g in enumerate(grads)]
                        + [pltpu.VMEM((shard[t] // 2, g.shape[1]), F32) for t, g in enumerate(grads)]
                        + [dma((N_DEV * nt,)), dma((npc,)), dma((npc,)), dma((npc,))]),
        compiler_params=_cp())(*grads, *zones)


def _adam_math(w, g, m, v):
    m = ADAM_B1 * m + (1.0 - ADAM_B1) * g
    v = ADAM_B2 * v + (1.0 - ADAM_B2) * (g * g)
    m_hat = m / (1.0 - ADAM_B1 ** ADAM_STEP)
    v_hat = v / (1.0 - ADAM_B2 ** ADAM_STEP)
    return -ADAM_LR * (m_hat / (jnp.sqrt(v_hat) + ADAM_EPS) + ADAM_WD * w), m, v


def _adamw(w, g, m, v, rb, name):
    rows, cols = w.shape

    def body(w_ref, g_ref, m_ref, v_ref, d_ref, nm_ref, nv_ref):
        d_ref[...], nm_ref[...], nv_ref[...] = _adam_math(w_ref[...], g_ref[...], m_ref[...], v_ref[...])

    spec = pl.BlockSpec((rb, cols), lambda i: (i, 0))
    return pl.pallas_call(body, grid=(rows // rb,), name=name, in_specs=[spec] * 4, out_specs=[spec] * 3,
                          out_shape=[jax.ShapeDtypeStruct(w.shape, F32)] * 3, compiler_params=_cp("parallel"))(w, g, m, v)


PACK = (("b_ada", 3 * D_MODEL), ("g_pre", D_MODEL), ("g_post", D_MODEL), ("gn_g", 512), ("qn_g", 64), ("kn_g", 64),
        ("w_dec_f", 4), ("w_dec_b", 4), ("loss", 1))
PACK_OFFSETS = {}
PACK_WIDTH = 0
for _name, _n in PACK:
    PACK_OFFSETS[_name] = PACK_WIDTH
    PACK_WIDTH += -(-_n // 128) * 128
SMALL_PARAMS = tuple(name for name, _ in PACK if name != "loss")


def _pack(parts):
    cols = []
    for name, n in PACK:
        pad = -(-n // 128) * 128 - n
        cols.append(parts[name].reshape(1, n).astype(F32))
        if pad:
            cols.append(jnp.zeros((1, pad), F32))
    return jnp.concatenate(cols, axis=1)


def _small_reduce(vec, cs, deps):
    ncol = 3 * D_MODEL // N_CHIP

    def body(vec_ref, cs_ref, *rest):
        tot_ref, gwa_ref, gat, ssem, rsem = rest[-5:]
        me = _dev_index(_coords())
        chip = me // 2
        gat[me] = vec_ref[...]
        sends = []
        for k in range(1, N_DEV):
            cp = _remote(vec_ref, gat.at[me], ssem.at[k - 1], rsem.at[k - 1], _peer(k))
            cp.start()
            sends.append(cp)
        for k in range(1, N_DEV):
            slot = gat.at[_dev_index(_peer(k))]
            _remote(slot, slot, ssem.at[k - 1], rsem.at[k - 1], _peer(k)).wait_recv()
        rows = [gat[d] for d in range(N_DEV)]
        tot = rows[0]
        for d in range(1, N_DEV):
            tot = tot + rows[d]
        tot_ref[...] = tot
        cs = cs_ref[...]
        ca_t = jnp.transpose(jnp.concatenate([cs * _sigmoid(cs), jnp.zeros((128 - N_DEV, D_MODEL), F32)], axis=0))
        dm = jnp.concatenate(rows + [jnp.zeros((128 - N_DEV, PACK_WIDTH), F32)], axis=0)
        for jj in range(N_CHIP):
            @pl.when(chip == jj)
            def _():
                gwa_ref[...] = _nn(ca_t, dm[:, ncol * jj:ncol * (jj + 1)], precision=HIGHEST)
        for cp in sends:
            cp.wait_send()

    return pl.pallas_call(
        body, name="small_reduce", in_specs=[VMEM_SPEC, VMEM_SPEC] + [HBM_SPEC] * len(deps), out_specs=[VMEM_SPEC] * 2,
        out_shape=[jax.ShapeDtypeStruct((1, PACK_WIDTH), F32), jax.ShapeDtypeStruct((D_MODEL, ncol), F32)],
        scratch_shapes=[pltpu.VMEM((N_DEV, 1, PACK_WIDTH), F32), pltpu.SemaphoreType.DMA((7,)), pltpu.SemaphoreType.DMA((7,))],
        compiler_params=_cp())(vec, cs, *deps)


def _small_adamw(tot, given):
    sizes = dict(PACK)
    np_ = len(SMALL_PARAMS)

    def body(*refs):
        tot_ref = refs[0]
        wmv = refs[1:1 + 3 * np_]
        outs = refs[1 + 3 * np_:1 + 7 * np_]
        loss_ref = refs[-1]
        for i, name in enumerate(SMALL_PARAMS):
            off, n = PACK_OFFSETS[name], sizes[name]
            g = tot_ref[:, off:off + n]
            w_ref, m_ref, v_ref = wmv[3 * i:3 * i + 3]
            outs[i][...] = g
            outs[np_ + i][...], outs[2 * np_ + i][...], outs[3 * np_ + i][...] = _adam_math(w_ref[...], g, m_ref[...], v_ref[...])
        loss_ref[...] = tot_ref[:, PACK_OFFSETS["loss"]:PACK_OFFSETS["loss"] + 1]

    flat = [a for name in SMALL_PARAMS for a in given[name]]
    shapes = [jax.ShapeDtypeStruct((1, sizes[name]), F32) for name in SMALL_PARAMS]
    res = pl.pallas_call(body, name="small_adamw", in_specs=[VMEM_SPEC] * (1 + 3 * np_), out_specs=[VMEM_SPEC] * (4 * np_ + 1),
                         out_shape=shapes * 4 + [jax.ShapeDtypeStruct((1, 1), F32)], compiler_params=_cp())(tot, *flat)
    return [dict(zip(SMALL_PARAMS, res[k * np_:(k + 1) * np_])) for k in range(4)], res[-1]


def kernel(x, c, w_ada, b_ada, g_pre, w_in, qn_g, kn_g, w_dec_f, w_dec_b, gn_g, w_pa, w_pr, w_out, g_post, loss_target, m_w_ada, m_b_ada, m_g_pre, m_w_in, m_qn_g, m_kn_g, m_w_dec_f, m_w_dec_b, m_gn_g, m_w_pa, m_w_pr, m_w_out, m_g_post, v_w_ada, v_b_ada, v_g_pre, v_w_in, v_qn_g, v_kn_g, v_w_dec_f, v_w_dec_b, v_gn_g, v_w_pa, v_w_pr, v_w_out, v_g_post):
    shards = (w_in[0].T.astype(MXU_DTYPE), jnp.concatenate([w_pa[0].T, w_pr[0].T], axis=1).astype(MXU_DTYPE),
              w_out[0].astype(MXU_DTYPE))
    cs, mod, w_attn = _mod_and_attn_rows(c, w_ada[0], b_ada.reshape(N_CHIP, 3 * D_MODEL // N_CHIP), shards[0])
    started = {}

    def rest_start(dep):
        started["rest"] = _gather_rest_start(shards, _place_own(shards), dep)
        return started["rest"][-1]

    def rest_wait(after):
        return _gather_rest_wait(started["rest"], after)

    kinds = {"early": (1, 2), "late": (0,)}

    def send(name, arrays):
        started[name] = _scatter_start(arrays, kinds[name], "grad_scatter_" + name)
        return started[name][-1]

    grad_x, _, _, _, small = _local_step(x[0], loss_target[0], mod, g_pre, qn_g, kn_g, w_dec_f, w_dec_b,
                                         gn_g, g_post, w_attn, rest_start, rest_wait, send=send)
    small = dict(small)
    small["b_ada"] = small.pop("dmod")
    given = dict(b_ada=(b_ada, m_b_ada, v_b_ada), g_pre=(g_pre, m_g_pre, v_g_pre), g_post=(g_post, m_g_post, v_g_post),
                 gn_g=(gn_g, m_gn_g, v_gn_g), qn_g=(qn_g, m_qn_g, v_qn_g), kn_g=(kn_g, m_kn_g, v_kn_g),
                 w_dec_f=(w_dec_f, m_w_dec_f, v_w_dec_f), w_dec_b=(w_dec_b, m_w_dec_b, v_w_dec_b))
    grads, deltas, new_m, new_v = {}, {}, {}, {}

    def update(name, w, g, m, v, back):
        d, nm, nv = _adamw(w, g, m, v, w.shape[0] // 4, "adamw_" + name)
        grads[name], deltas[name], new_m[name], new_v[name] = back(g), back(d), back(nm), back(nv)
        return d

    lead = lambda a: a[None]
    (g_pt, g_out), (z_pt, z_out) = _scatter_wait(started["early"], kinds["early"], grad_x, "grad_scatter_early_wait")
    r_pt, r_out = _grad_finish((g_pt, g_out), (z_pt, z_out), kinds["early"], "grad_finish_early")
    early = (update("w_pa", w_pa[0], r_pt[:, :512].T, m_w_pa[0], v_w_pa[0], lead),
             update("w_pr", w_pr[0], r_pt[:, 512:].T, m_w_pr[0], v_w_pr[0], lead),
             update("w_out", w_out[0], r_out, m_w_out[0], v_w_out[0], lead))
    tot, g_w_ada = _small_reduce(_pack(small), cs.reshape(N_DEV, D_MODEL), early)
    small_parts, loss = _small_adamw(tot, given)
    for dst, part in zip((grads, deltas, new_m, new_v), small_parts):
        dst.update(part)
    last = update("w_ada", w_ada[0], g_w_ada, m_w_ada[0], v_w_ada[0], lead)

    (g_in_t,), (z_in,) = _scatter_wait(started["late"], kinds["late"], last, "grad_scatter_late_wait")
    (r_in,) = _grad_finish((g_in_t,), (z_in,), kinds["late"], "grad_finish_late")
    tr = lambda a: a[0].T
    update("w_in", tr(w_in), r_in, tr(m_w_in), tr(v_w_in), lambda a: a.T[None])
    order = ("w_ada", "b_ada", "g_pre", "w_in", "qn_g", "kn_g", "w_dec_f", "w_dec_b", "gn_g", "w_pa", "w_pr", "w_out", "g_post")
    return (loss[0, 0], grad_x[None], *[grads[n] for n in order], *[deltas[n] for n in order],
            *[new_m[n] for n in order], *[new_v[n] for n in order])
```

```python
import jax
import jax.numpy as jnp
import numpy as np
from jax import lax
from jax.experimental import pallas as pl
from jax.experimental.pallas import tpu as pltpu

F32 = jnp.float32
BF16 = jnp.bfloat16
MXU_DTYPE = jnp.bfloat16

D_MODEL = 1024
GRID_W = 64
HEAD_DIM = 64
ROPE_THETA = 10000.0
EPS = 1e-6
RET_HEADS = 4
RET_CHUNK = 256
RET_GROUP = 4
IN_WIDTH = 4864
QA, KA, VA, ZA, QR, KR, VR, ZR, GL = 0, 512, 640, 768, 1280, 1536, 1792, 2304, 2816
ATTN_COLS = ZA
ZA_B, QR_B, KR_B, VR_B, ZR_B, GL_B = (c - ATTN_COLS for c in (ZA, QR, KR, VR, ZR, GL))

ADAM_LR, ADAM_B1, ADAM_B2, ADAM_EPS, ADAM_WD, ADAM_STEP = 0.001, 0.9, 0.999, 1e-08, 0.01, 10

VMEM_LIMIT = 56 * 1024 * 1024
MESH = pl.DeviceIdType.MESH
HIGHEST = lax.Precision.HIGHEST
LOG2E = 1.4426950408889634
LN2 = 0.6931471805599453


def _cp(*sem, **kw):
    if sem:
        kw["dimension_semantics"] = sem
    return pltpu.CompilerParams(vmem_limit_bytes=VMEM_LIMIT, **kw)


def _nn(a, b, **kw):
    return lax.dot_general(a, b, (((1,), (0,)), ((), ())), preferred_element_type=F32, **kw)


def _nt(a, b):
    return lax.dot_general(a, b, (((1,), (1,)), ((), ())), preferred_element_type=F32)


def _tn(a, b):
    return lax.dot_general(a, b, (((0,), (0,)), ((), ())), preferred_element_type=F32)


def _mx(x):
    return x.astype(MXU_DTYPE)


def _lane(shape):
    return lax.broadcasted_iota(jnp.int32, shape, 1)


def _sigmoid(z):
    return 1.0 / (1.0 + jnp.exp(-z))


def _rowsum128(x):
    s = jnp.sum(x, axis=0, keepdims=True)
    out = s[:, 0:128]
    for k in range(1, x.shape[1] // 128):
        out = out + s[:, 128 * k:128 * (k + 1)]
    return out


def _swap16(x):
    return jnp.where((_lane(x.shape) & 16) == 0, pltpu.roll(x, 112, 1), pltpu.roll(x, 16, 1))


def _rope(x, cos, sin):
    return x * cos + _swap16(x) * sin


def _rope_t(d, cos, sin):
    return d * cos + _swap16(d * sin)


def _blockdiag64():
    r = lax.broadcasted_iota(jnp.int32, (128, 128), 0) // 64
    c = lax.broadcasted_iota(jnp.int32, (128, 128), 1) // 64
    return (r == c).astype(BF16)


def _seg64(x, m):
    hi = x.astype(BF16)
    lo = (x - hi.astype(F32)).astype(BF16)
    return _nn(hi, m) + _nn(lo, m)


def _rope_tables(seq):
    t = np.arange(seq)
    row = (t // GRID_W).astype(np.float32)
    col = (t % GRID_W).astype(np.float32)
    half = HEAD_DIM // 2
    inv_freq = (np.float32(ROPE_THETA) ** (-np.arange(0, half, 2, dtype=np.float32) / np.float32(half))).astype(np.float32)
    ar = (row[:, None] * inv_freq[None, :]).astype(np.float32).astype(np.float64)
    ac = (col[:, None] * inv_freq[None, :]).astype(np.float32).astype(np.float64)
    cr, sr, cc, sc = np.cos(ar), np.sin(ar), np.cos(ac), np.sin(ac)
    cos64 = np.concatenate([cr, cr, cc, cc], axis=1)
    sin64 = np.concatenate([-sr, sr, -sc, sc], axis=1)
    return jnp.asarray(np.tile(cos64, (1, 2)), F32), jnp.asarray(np.tile(sin64, (1, 2)), F32)


def _attn_inputs(x, mod, g_pre, w_attn, cos, sin, qg2, kg2):
    seq = x.shape[0]
    bs = 512

    def body(x_ref, mod_ref, g_ref, w_ref, cos_ref, sin_ref, qg_ref, kg_ref, h_ref, pa_ref, qt_ref, kd_ref, vd_ref):
        xv = x_ref[...]
        r = lax.rsqrt(jnp.mean(xv * xv, axis=-1, keepdims=True) + EPS)
        shift = mod_ref[:, 0:D_MODEL]
        scale = mod_ref[:, D_MODEL:2 * D_MODEL]
        hb = ((xv * r) * g_ref[...] * (1.0 + scale) + shift).astype(h_ref.dtype)
        h_ref[...] = hb
        p = _nt(hb, w_ref[...])
        pa_ref[...] = p
        m = _blockdiag64()
        cs, sn = cos_ref[...], sin_ref[...]
        lo = _lane((bs, 128)) < 64
        for pr in range(4):
            q = p[:, QA + 128 * pr:QA + 128 * (pr + 1)]
            r = lax.rsqrt(_seg64(q * q, m) * (1.0 / 64) + EPS)
            qt_ref[:, 128 * pr:128 * (pr + 1)] = (_rope(q * r * qg_ref[...], cs, sn) * (0.125 * LOG2E)).astype(qt_ref.dtype)
        k = p[:, KA:KA + 128]
        r = lax.rsqrt(_seg64(k * k, m) * (1.0 / 64) + EPS)
        kr = _rope(k * r * kg_ref[...], cs, sn)
        ksw = pltpu.roll(kr, 64, 1)
        kd_ref[0] = jnp.where(lo, kr, ksw).astype(kd_ref.dtype)
        kd_ref[1] = jnp.where(lo, ksw, kr).astype(kd_ref.dtype)
        v = p[:, VA:VA + 128]
        vsw = pltpu.roll(v, 64, 1)
        vd_ref[0] = jnp.where(lo, v, vsw).astype(vd_ref.dtype)
        vd_ref[1] = jnp.where(lo, vsw, v).astype(vd_ref.dtype)

    row = lambda w: pl.BlockSpec((bs, w), lambda i: (i, 0))
    fix = lambda a, b: pl.BlockSpec((a, b), lambda i: (0, 0))
    dup = pl.BlockSpec((2, bs, 128), lambda i: (0, i, 0))
    sds = jax.ShapeDtypeStruct
    return pl.pallas_call(
        body, grid=(seq // bs,), name="attn_inputs",
        in_specs=[row(D_MODEL), fix(1, 3 * D_MODEL), fix(1, D_MODEL), fix(ATTN_COLS, D_MODEL), row(128), row(128), fix(1, 128), fix(1, 128)],
        out_specs=[row(D_MODEL), row(ATTN_COLS), row(512), dup, dup],
        out_shape=[sds((seq, D_MODEL), MXU_DTYPE), sds((seq, ATTN_COLS), F32), sds((seq, 512), MXU_DTYPE),
                   sds((2, seq, 128), MXU_DTYPE), sds((2, seq, 128), MXU_DTYPE)],
        compiler_params=_cp("parallel"))(x, mod, g_pre, w_attn, cos, sin, qg2, kg2)


def _in_proj_dx(x, dp, win_t, dout, mod, g_pre, dep=None):
    seq = x.shape[0]
    bs = 512
    deps, dep_specs = _dep_args(dep, 1)

    def body(x_ref, dp_ref, w_ref, dout_ref, mod_ref, g_ref, *rest):
        gx_ref, dshift_ref, dscale_ref, dg_ref = rest[-4:]

        @pl.when(pl.program_id(0) == 0)
        def _():
            dshift_ref[...] = jnp.zeros_like(dshift_ref)
            dscale_ref[...] = jnp.zeros_like(dscale_ref)
            dg_ref[...] = jnp.zeros_like(dg_ref)

        xv = x_ref[...]
        dh = _nn(dp_ref[...], w_ref[...])
        g = g_ref[...]
        r = lax.rsqrt(jnp.mean(xv * xv, axis=-1, keepdims=True) + EPS)
        xn = xv * r
        scale = mod_ref[:, D_MODEL:2 * D_MODEL]
        dshift_ref[...] += jnp.sum(dh, axis=0, keepdims=True)
        dscale_ref[...] += jnp.sum(dh * (xn * g), axis=0, keepdims=True)
        da = dh * (1.0 + scale)
        dg_ref[...] += jnp.sum(da * xn, axis=0, keepdims=True)
        dxn = da * g
        dx = r * (dxn - xn * jnp.mean(dxn * xn, axis=-1, keepdims=True))
        gx_ref[...] = dout_ref[...] + dx

    row = pl.BlockSpec((bs, D_MODEL), lambda i: (i, 0))
    vec = pl.BlockSpec((1, D_MODEL), lambda i: (0, 0))
    return pl.pallas_call(
        body, grid=(seq // bs,), name="in_proj_dx",
        in_specs=[row, pl.BlockSpec((bs, IN_WIDTH), lambda i: (i, 0)), pl.BlockSpec((IN_WIDTH, D_MODEL), lambda i: (0, 0)), row,
                  pl.BlockSpec((1, 3 * D_MODEL), lambda i: (0, 0)), vec] + dep_specs,
        out_specs=[row, vec, vec, vec],
        out_shape=[jax.ShapeDtypeStruct((seq, D_MODEL), F32)] + [jax.ShapeDtypeStruct((1, D_MODEL), F32)] * 3,
        compiler_params=_cp("arbitrary"))(x, dp, win_t, dout, mod, g_pre, *deps)


def _dep_args(dep, grid_rank):
    if dep is None:
        return [], []
    return [dep], [pl.BlockSpec(dep.shape, lambda *_: (0,) * dep.ndim)]


def _matmul(a, b, *, ta, tb, tm, tn, out_dtype, name, dep=None):
    kdim = a.shape[0] if ta else a.shape[1]
    m = a.shape[1] if ta else a.shape[0]
    n = b.shape[0] if tb else b.shape[1]
    assert m % tm == 0 and n % tn == 0
    deps, dep_specs = _dep_args(dep, 2)

    def body(a_ref, b_ref, *rest):
        o_ref = rest[-1]
        dims = (((0 if ta else 1,), (1 if tb else 0,)), ((), ()))
        o_ref[...] = lax.dot_general(a_ref[...], b_ref[...], dims, preferred_element_type=F32).astype(o_ref.dtype)

    a_spec = pl.BlockSpec((kdim, tm), lambda j, i: (0, i)) if ta else pl.BlockSpec((tm, kdim), lambda j, i: (i, 0))
    b_spec = pl.BlockSpec((tn, kdim), lambda j, i: (j, 0)) if tb else pl.BlockSpec((kdim, tn), lambda j, i: (0, j))
    return pl.pallas_call(
        body, grid=(n // tn, m // tm), name=name, in_specs=[a_spec, b_spec] + dep_specs,
        out_specs=pl.BlockSpec((tm, tn), lambda j, i: (i, j)),
        out_shape=jax.ShapeDtypeStruct((m, n), out_dtype), compiler_params=_cp("parallel", "parallel"))(a, b, *deps)


def _in_proj_rest(h, win_t, cos, sin):
    seq = h.shape[0]
    tm = min(1024, seq)
    ncols = IN_WIDTH - ATTN_COLS
    tn = ncols // 2
    assert ZR_B <= tn and ATTN_COLS % 128 == 0 and tn % 128 == 0

    def body(h_ref, w_ref, cos_ref, sin_ref, p_ref, rq_ref, rk_ref, rv_ref):
        p = _nt(h_ref[...], w_ref[...])
        p_ref[...] = p

        @pl.when(pl.program_id(0) == 0)
        def _():
            cs, sn = cos_ref[...], sin_ref[...]
            for pr in range(2):
                sl = slice(128 * pr, 128 * (pr + 1))
                rq_ref[:, sl] = _rope(p[:, QR_B + 128 * pr:QR_B + 128 * (pr + 1)], cs, sn).astype(rq_ref.dtype)
                rk_ref[:, sl] = (_rope(p[:, KR_B + 128 * pr:KR_B + 128 * (pr + 1)], cs, sn) * 0.125).astype(rk_ref.dtype)
            rv_ref[...] = p[:, VR_B:VR_B + 512].astype(rv_ref.dtype)

    nrow = seq // tm
    row = lambda w: pl.BlockSpec((tm, w), lambda j, i: (i, 0))
    park = lambda w: pl.BlockSpec((tm, w), lambda j, i: (jnp.where(j == 0, i, nrow - 1), 0))
    sds = jax.ShapeDtypeStruct
    return pl.pallas_call(
        body, grid=(2, nrow), name="in_proj_rest",
        in_specs=[row(D_MODEL), pl.BlockSpec((pl.Element(tn), pl.Element(D_MODEL)),
                                             lambda j, i: (pl.multiple_of(ATTN_COLS + j * tn, 128), 0)), row(128), row(128)],
        out_specs=[pl.BlockSpec((tm, tn), lambda j, i: (i, j)), park(256), park(256), park(512)],
        out_shape=[sds((seq, ncols), F32), sds((seq, 256), MXU_DTYPE), sds((seq, 256), MXU_DTYPE), sds((seq, 512), MXU_DTYPE)],
        compiler_params=_cp("arbitrary", "arbitrary"))(h, win_t, cos, sin)


def _halves(kd):
    lo = _lane(kd.shape) < 64
    z = jnp.zeros_like(kd)
    return jnp.concatenate([jnp.where(lo, kd, z), jnp.where(lo, z, kd)], axis=0)


def _attn_fwd(qt, kd, vd, dep=None):
    seq = qt.shape[0]
    bq, bk = min(2048, seq), min(1024, seq)
    nk = seq // bk
    deps, dep_specs = _dep_args(dep, 2)

    def body(q_ref, k_ref, v_ref, *rest):
        o_ref, lse_ref, m_scr, l_scr, acc = rest[-5:]
        j = pl.program_id(1)
        m_scr[...] = jnp.full_like(m_scr, -jnp.inf)
        l_scr[...] = jnp.zeros_like(l_scr)
        acc[...] = jnp.zeros_like(acc)
        sub = min(512, bq)
        nsub = bq // sub
        lo = _lane((sub, 128)) < 64

        def kv_block(n, carry):
            rows = pl.ds(pl.multiple_of(n * bk, bk), bk)
            k2, v2 = _halves(k_ref[rows, :]), _halves(v_ref[rows, :])
            for pr in range(2):
                for hf in range(nsub):
                    qr = slice(hf * sub, (hf + 1) * sub)
                    s2 = _nt(q_ref[qr, 128 * pr:128 * (pr + 1)], k2)
                    ps, alphas = [], []
                    for e in range(2):
                        g = 2 * pr + e
                        s = s2[:, e * bk:(e + 1) * bk]
                        m_prev = m_scr[g, qr]
                        m_next = jnp.maximum(m_prev, jnp.max(s, axis=1, keepdims=True))
                        alpha = jnp.exp2(m_prev - m_next)
                        pe = jnp.exp2(s - jnp.tile(m_next, (1, bk // 128)))
                        l_scr[g, qr] = alpha * l_scr[g, qr] + jnp.sum(pe, axis=1, keepdims=True)
                        m_scr[g, qr] = m_next
                        ps.append(_mx(pe))
                        alphas.append(alpha)
                    o2 = _nn(jnp.concatenate(ps, axis=1), v2)
                    sl = slice(128 * pr, 128 * (pr + 1))
                    acc[qr, sl] = acc[qr, sl] * jnp.where(lo, alphas[0], alphas[1]) + o2
            return carry

        lax.fori_loop(0, nk, kv_block, 0)
        lo_all = _lane((bq, 128)) < 64
        for pr in range(2):
            sl = slice(128 * pr, 128 * (pr + 1))
            o_ref[:, sl] = acc[:, sl] / jnp.where(lo_all, l_scr[2 * pr], l_scr[2 * pr + 1])
        for g in range(4):
            lse = jnp.transpose(m_scr[g] + jnp.log2(l_scr[g]))
            lse_ref[pl.ds(4 * j + g, 1), :] = lse[0:1, :]

    return pl.pallas_call(
        body, grid=(seq // bq, 2), name="attn_fwd",
        in_specs=[pl.BlockSpec((bq, 256), lambda i, j: (i, j)), pl.BlockSpec((None, seq, 128), lambda i, j: (j, 0, 0)),
                  pl.BlockSpec((None, seq, 128), lambda i, j: (j, 0, 0))] + dep_specs,
        out_specs=[pl.BlockSpec((bq, 256), lambda i, j: (i, j)), pl.BlockSpec((8, bq), lambda i, j: (0, i))],
        out_shape=[jax.ShapeDtypeStruct((seq, 512), F32), jax.ShapeDtypeStruct((8, seq), F32)],
        scratch_shapes=[pltpu.VMEM((4, bq, 128), F32), pltpu.VMEM((4, bq, 128), F32), pltpu.VMEM((bq, 256), F32)],
        compiler_params=_cp("parallel", "arbitrary"))(qt, kd, vd, *deps)


def _attn_bwd(qt, kd, vd, do, lse, delta, dep=None):
    seq = qt.shape[0]
    bq, bk = min(1024, seq), min(1024, seq)
    nq, nk = seq // bq, seq // bk
    deps, dep_specs = _dep_args(dep, 3)

    def body(q_ref, do_ref, k_ref, v_ref, lse_ref, del_ref, *rest):
        dq_ref, dk_ref, dv_ref = rest[-3:]
        j, i = pl.program_id(0), pl.program_id(1)
        dq_ref[...] = jnp.zeros_like(dq_ref)

        @pl.when(i == 0)
        def _():
            dk_ref[...] = jnp.zeros_like(dk_ref)
            dv_ref[...] = jnp.zeros_like(dv_ref)

        lo = _lane((bk, 128)) < 64
        big = lambda r: jnp.concatenate([jnp.broadcast_to(r[0], (bk, bq)), jnp.broadcast_to(r[1], (bk, bq))], axis=0)

        def kv_block(n, carry):
            rows = pl.ds(pl.multiple_of(n * bk, bk), bk)
            k2, v2 = _halves(k_ref[rows, :]), _halves(v_ref[rows, :])
            for pr in range(2):
                sl = slice(128 * pr, 128 * (pr + 1))
                qp, dop = q_ref[:, sl], do_ref[:, sl]
                s_t = _nt(k2, qp)
                dp_t = _nt(v2, dop)
                ls = [lse_ref[pl.ds(4 * j + 2 * pr + e, 1), :] for e in range(2)]
                dl = [del_ref[pl.ds(4 * j + 2 * pr + e, 1), :] for e in range(2)]
                p_t = jnp.exp2(s_t - big(ls))
                ds_t = _mx(p_t * (dp_t - big(dl)))
                rv = _nn(_mx(p_t), dop)
                rk = _nn(ds_t, qp) * LN2
                dv_ref[rows, :] += jnp.where(lo, rv[:bk], rv[bk:])
                dk_ref[rows, :] += jnp.where(lo, rk[:bk], rk[bk:])
                dq_ref[:, sl] += _tn(ds_t, k2)
            return carry

        lax.fori_loop(0, nk, kv_block, 0)

    return pl.pallas_call(
        body, grid=(2, nq), name="attn_bwd",
        in_specs=[pl.BlockSpec((bq, 256), lambda j, i: (i, j)), pl.BlockSpec((bq, 256), lambda j, i: (i, j)),
                  pl.BlockSpec((None, seq, 128), lambda j, i: (j, 0, 0)), pl.BlockSpec((None, seq, 128), lambda j, i: (j, 0, 0)),
                  pl.BlockSpec((8, bq), lambda j, i: (0, i)), pl.BlockSpec((8, bq), lambda j, i: (0, i))] + dep_specs,
        out_specs=[pl.BlockSpec((bq, 256), lambda j, i: (i, j)), pl.BlockSpec((None, seq, 128), lambda j, i: (j, 0, 0)),
                   pl.BlockSpec((None, seq, 128), lambda j, i: (j, 0, 0))],
        out_shape=[jax.ShapeDtypeStruct((seq, 512), F32), jax.ShapeDtypeStruct((2, seq, 128), F32),
                   jax.ShapeDtypeStruct((2, seq, 128), F32)],
        compiler_params=_cp("arbitrary", "arbitrary"))(qt, do, kd, vd, lse, delta, *deps)


def _ret_tables(lg_f, lg_b, c):
    idx = jnp.arange(c, dtype=F32)
    diff = idx[:, None] - idx[None, :]
    a, b = lg_f[:, None, None], lg_b[:, None, None]
    low, up = (diff >= 0)[None], (diff < 0)[None]
    dmat = jnp.where(low, jnp.exp(a * jnp.maximum(diff, 0.0)[None]), jnp.exp(b * jnp.maximum(-diff, 0.0)[None]))
    dda = jnp.where(low, diff[None] * jnp.exp(a * jnp.maximum(diff, 0.0)[None]), 0.0)
    ddb = jnp.where(up, -diff[None] * jnp.exp(b * jnp.maximum(-diff, 0.0)[None]), 0.0)
    rep = lambda w: jnp.broadcast_to(w[:, :, None], (RET_HEADS, c, 128))
    wqf = rep(jnp.exp(lg_f[:, None] * (idx + 1.0)[None]))
    wqb = rep(jnp.exp(lg_b[:, None] * (c - idx)[None]))
    wkf = rep(jnp.exp(lg_f[:, None] * (c - 1.0 - idx)[None]))
    wkb = rep(jnp.exp(lg_b[:, None] * idx[None]))
    cdf, cdb = jnp.exp(lg_f * c), jnp.exp(lg_b * c)
    dec_fb = jnp.broadcast_to(jnp.concatenate([cdf, cdb])[:, None], (8, 128))
    dec_bf = jnp.broadcast_to(jnp.concatenate([cdb, cdf])[:, None], (8, 128))
    return dict(dmat=dmat, dda=dda, ddb=ddb, wqf=wqf, wqb=wqb, wkf=wkf, wkb=wkb, dec_fb=dec_fb, dec_bf=dec_bf)


def _ret_states(xk, yv, w_fwd, w_rev, dec, name):
    seq = xk.shape[0]
    c = RET_CHUNK
    nc = seq // c
    grp = min(RET_GROUP, nc)
    ns = nc // grp

    def body(xf_ref, yf_ref, xr_ref, yr_ref, wf_ref, wr_ref, dec_ref, sf_ref, sr_ref, st_f, st_r):
        @pl.when(pl.program_id(0) == 0)
        def _():
            st_f[...] = jnp.zeros_like(st_f)
            st_r[...] = jnp.zeros_like(st_r)

        lane = _lane((c, 128))

        def chunk(g, carry):
            for x_ref, y_ref, w_ref, s_ref, st, base, gg in ((xf_ref, yf_ref, wf_ref, sf_ref, st_f, 0, g),
                                                             (xr_ref, yr_ref, wr_ref, sr_ref, st_r, 4, grp - 1 - g)):
                rows = pl.ds(pl.multiple_of(gg * c, c), c)
                for h in range(RET_HEADS):
                    pr, e = h // 2, h % 2
                    half = (lane < 64) if e == 0 else (lane >= 64)
                    s_ref[gg, h] = st[h].astype(s_ref.dtype)
                    xm = jnp.where(half, x_ref[rows, 128 * pr:128 * (pr + 1)].astype(F32) * w_ref[h], 0.0)
                    st[h] = st[h] * dec_ref[base + h:base + h + 1, :] + _tn(_mx(xm), _mx(y_ref[rows, 128 * h:128 * (h + 1)]))
            return carry

        lax.fori_loop(0, grp, chunk, 0, unroll=2)

    xs = lambda f: pl.BlockSpec((grp * c, 256), f)
    ys = lambda f: pl.BlockSpec((grp * c, 512), f)
    fwd, rev = (lambda n: (n, 0)), (lambda n: (ns - 1 - n, 0))
    wsp = pl.BlockSpec((RET_HEADS, c, 128), lambda n: (0, 0, 0))
    return pl.pallas_call(
        body, grid=(ns,), name=name,
        in_specs=[xs(fwd), ys(fwd), xs(rev), ys(rev), wsp, wsp, pl.BlockSpec((8, 128), lambda n: (0, 0))],
        out_specs=[pl.BlockSpec((grp, RET_HEADS, 128, 128), lambda n: (n, 0, 0, 0)),
                   pl.BlockSpec((grp, RET_HEADS, 128, 128), lambda n: (ns - 1 - n, 0, 0, 0))],
        out_shape=[jax.ShapeDtypeStruct((nc, RET_HEADS, 128, 128), MXU_DTYPE)] * 2,
        scratch_shapes=[pltpu.VMEM((RET_HEADS, 128, 128), F32), pltpu.VMEM((RET_HEADS, 128, 128), F32)],
        compiler_params=_cp("arbitrary"))(xk, yv, xk, yv, w_fwd, w_rev, dec)


def _ret_fwd(rq, rk, rv, rf, rb, tb):
    seq = rq.shape[0]
    c = RET_CHUNK
    grp = min(RET_GROUP, seq // c)

    def body(q_ref, k_ref, v_ref, rf_ref, rb_ref, d_ref, wqf_ref, wqb_ref, o_ref):
        lane = _lane((c, 128))

        def chunk(g, carry):
            rows = pl.ds(pl.multiple_of(g * c, c), c)
            for h in range(RET_HEADS):
                pr, e = h // 2, h % 2
                half = (lane < 64) if e == 0 else (lane >= 64)
                sl = slice(128 * pr, 128 * (pr + 1))
                qp, kp = q_ref[rows, sl], k_ref[rows, sl]
                km = jnp.where(half, kp, jnp.zeros_like(kp))
                sd = _mx(_nt(qp, km) * d_ref[h])
                qf = qp.astype(F32)
                o = _nn(sd, v_ref[rows, 128 * h:128 * (h + 1)])
                o = o + _nn(_mx(qf * wqf_ref[h]), rf_ref[g, h]) + _nn(_mx(qf * wqb_ref[h]), rb_ref[g, h])
                o_ref[rows, 128 * h:128 * (h + 1)] = o
            return carry

        lax.fori_loop(0, grp, chunk, 0, unroll=2)

    st = pl.BlockSpec((grp, RET_HEADS, 128, 128), lambda n: (n, 0, 0, 0))
    wsp = pl.BlockSpec((RET_HEADS, c, 128), lambda n: (0, 0, 0))
    return pl.pallas_call(
        body, grid=(seq // (grp * c),), name="ret_fwd",
        in_specs=[pl.BlockSpec((grp * c, 256), lambda n: (n, 0)), pl.BlockSpec((grp * c, 256), lambda n: (n, 0)),
                  pl.BlockSpec((grp * c, 512), lambda n: (n, 0)), st, st, pl.BlockSpec((RET_HEADS, c, c), lambda n: (0, 0, 0)), wsp, wsp],
        out_specs=pl.BlockSpec((grp * c, 512), lambda n: (n, 0)),
        out_shape=jax.ShapeDtypeStruct((seq, 512), F32), compiler_params=_cp("parallel"))(
            rq, rk, rv, rf, rb, tb["dmat"], tb["wqf"], tb["wqb"])


def _ret_bwd(rq, rk, rv, do, rf, rb, hf, hb, tb):
    seq = rq.shape[0]
    c = RET_CHUNK
    grp = min(RET_GROUP, seq // c)

    def body(q_ref, k_ref, v_ref, do_ref, rf_ref, rb_ref, hf_ref, hb_ref, d_ref, dda_ref, ddb_ref,
             wqf_ref, wqb_ref, wkf_ref, wkb_ref, cdec_ref, dq_ref, dk_ref, dv_ref, dlg_ref):
        @pl.when(pl.program_id(0) == 0)
        def _():
            dlg_ref[...] = jnp.zeros_like(dlg_ref)

        lane = _lane((c, 128))
        pos = lax.broadcasted_iota(jnp.int32, (c, 128), 0).astype(F32)

        def chunk(g, carry):
            rows = pl.ds(pl.multiple_of(g * c, c), c)
            for pr in range(2):
                sl = slice(128 * pr, 128 * (pr + 1))
                qp, kp = q_ref[rows, sl], k_ref[rows, sl]
                qf, kf = qp.astype(F32), kp.astype(F32)
                dq_acc = jnp.zeros((c, 128), F32)
                dk_acc = jnp.zeros((c, 128), F32)
                for e in range(2):
                    h = 2 * pr + e
                    half = (lane < 64) if e == 0 else (lane >= 64)
                    hs = slice(128 * h, 128 * (h + 1))
                    km = jnp.where(half, kp, jnp.zeros_like(kp))
                    kmf = km.astype(F32)
                    vh, doh = v_ref[rows, hs], do_ref[rows, hs]
                    rfh, rbh, hfh, hbh = rf_ref[g, h], rb_ref[g, h], hf_ref[g, h], hb_ref[g, h]
                    s = _nt(qp, km)
                    dpm = _nt(doh, vh)
                    ds_b = _mx(dpm * d_ref[h])
                    sd_b = _mx(s * d_ref[h])
                    dq_if = wqf_ref[h] * _nt(doh, rfh)
                    dq_ib = wqb_ref[h] * _nt(doh, rbh)
                    dq_acc = dq_acc + _nn(ds_b, km) + dq_if + dq_ib
                    dk_sf = wkf_ref[h] * _nt(vh, hfh)
                    dk_sb = wkb_ref[h] * _nt(vh, hbh)
                    dk_acc = dk_acc + jnp.where(half, _tn(ds_b, qp), 0.0) + dk_sf + dk_sb
                    dv = _tn(sd_b, doh) + _nn(_mx(kmf * wkf_ref[h]), hfh) + _nn(_mx(kmf * wkb_ref[h]), hbh)
                    dv_ref[rows, hs] = dv.astype(dv_ref.dtype)
                    sdp = s * dpm
                    hr_f = hfh.astype(F32) * rfh.astype(F32)
                    hr_b = hbh.astype(F32) * rbh.astype(F32)
                    da = (_rowsum128(sdp * dda_ref[h]) + _rowsum128((pos + 1.0) * qf * dq_if)
                          + _rowsum128((c - 1.0 - pos) * kf * dk_sf) + cdec_ref[h:h + 1, :] * _rowsum128(hr_f))
                    db = (_rowsum128(sdp * ddb_ref[h]) + _rowsum128((c - pos) * qf * dq_ib)
                          + _rowsum128(pos * kf * dk_sb) + cdec_ref[4 + h:5 + h, :] * _rowsum128(hr_b))
                    dlg_ref[h:h + 1, :] += da
                    dlg_ref[4 + h:5 + h, :] += db
                dq_ref[rows, sl] = dq_acc
                dk_ref[rows, sl] = dk_acc
            return carry

        lax.fori_loop(0, grp, chunk, 0, unroll=2)

    st = pl.BlockSpec((grp, RET_HEADS, 128, 128), lambda n: (n, 0, 0, 0))
    wsp = pl.BlockSpec((RET_HEADS, c, 128), lambda n: (0, 0, 0))
    dsp = pl.BlockSpec((RET_HEADS, c, c), lambda n: (0, 0, 0))
    x256 = pl.BlockSpec((grp * c, 256), lambda n: (n, 0))
    x512 = pl.BlockSpec((grp * c, 512), lambda n: (n, 0))
    cdec = tb["dec_fb"] * float(c)
    return pl.pallas_call(
        body, grid=(seq // (grp * c),), name="ret_bwd",
        in_specs=[x256, x256, x512, x512, st, st, st, st, dsp, dsp, dsp, wsp, wsp, wsp, wsp,
                  pl.BlockSpec((8, 128), lambda n: (0, 0))],
        out_specs=[x256, x256, x512, pl.BlockSpec((8, 128), lambda n: (0, 0))],
        out_shape=[jax.ShapeDtypeStruct((seq, 256), F32), jax.ShapeDtypeStruct((seq, 256), F32),
                   jax.ShapeDtypeStruct((seq, 512), MXU_DTYPE), jax.ShapeDtypeStruct((8, 128), F32)],
        compiler_params=_cp("arbitrary"))(
            rq, rk, rv, do, rf, rb, hf, hb, tb["dmat"], tb["dda"], tb["ddb"], tb["wqf"], tb["wqb"], tb["wkf"], tb["wkb"], cdec)


def _tail(x, tgt, o_att, o_ret, p, mod, g_post, gn_g, wpt, wout):
    seq = x.shape[0]
    bs = 256
    nb = seq // bs

    def body(x_ref, t_ref, oa_ref, or_ref, za_ref, zr_ref, gl_ref, mod_ref, gp_ref, gn_ref, wpt_ref, wout_ref,
             dout_ref, dza_ref, dzr_ref, dgl_ref, doa_ref, dor_ref, del_ref, gout_ref, gpt_ref,
             dgate_ref, dgpost_ref, dgn_ref, loss_ref, gout_acc, gpt_acc):
        i = pl.program_id(0)

        @pl.when(i == 0)
        def _():
            gout_acc[...] = jnp.zeros_like(gout_acc)
            gpt_acc[...] = jnp.zeros_like(gpt_acc)
            dgate_ref[...] = jnp.zeros_like(dgate_ref)
            dgpost_ref[...] = jnp.zeros_like(dgpost_ref)
            dgn_ref[...] = jnp.zeros_like(dgn_ref)
            loss_ref[...] = jnp.zeros_like(loss_ref)

        oa, za, zr = oa_ref[...], za_ref[...], zr_ref[...]
        sga, sgr = _sigmoid(za), _sigmoid(zr)
        sza, szr = za * sga, zr * sgr
        ya_b = _mx(oa * sza)
        ons, rstds = [], []
        for h in range(RET_HEADS):
            oh = or_ref[:, 128 * h:128 * (h + 1)]
            xc = oh - jnp.mean(oh, axis=-1, keepdims=True)
            rstd = lax.rsqrt(jnp.mean(xc * xc, axis=-1, keepdims=True) + EPS)
            ons.append(xc * rstd)
            rstds.append(rstd)
        on = jnp.concatenate(ons, axis=1)
        yn = on * gn_ref[...]
        yr_b = _mx(yn * szr)
        wpa_t, wpr_t = wpt_ref[:, 0:512], wpt_ref[:, 512:1024]
        a_att = _nt(ya_b, wpa_t)
        a_ret = _nt(yr_b, wpr_t)
        gts = _sigmoid(gl_ref[...])
        g_att, g_ret = gts[:, 0:D_MODEL], gts[:, D_MODEL:2 * D_MODEL]
        merged_b = _mx(g_att * a_att + g_ret * a_ret)
        u = _nn(merged_b, wout_ref[...])
        r = lax.rsqrt(jnp.mean(u * u, axis=-1, keepdims=True) + EPS)
        un = u * r
        gpost = gp_ref[...]
        y = un * gpost
        gate = mod_ref[:, 2 * D_MODEL:3 * D_MODEL]
        err = x_ref[...] + gate * y - t_ref[...]
        loss_ref[...] += (0.5 / D_MODEL) * _rowsum128(err * err)
        dout = err * (1.0 / D_MODEL)
        dout_ref[...] = dout
        dgate_ref[...] += jnp.sum(dout * y, axis=0, keepdims=True)
        dy = dout * gate
        dgpost_ref[...] += jnp.sum(dy * un, axis=0, keepdims=True)
        dun = dy * gpost
        du_b = _mx(r * (dun - un * jnp.mean(dun * un, axis=-1, keepdims=True)))
        dmerged = _nt(du_b, wout_ref[...])
        gout_acc[...] += _tn(merged_b, du_b)
        d_att, d_ret = dmerged * g_att, dmerged * g_ret
        dgl_ref[:, 0:D_MODEL] = (d_att * a_att * (1.0 - g_att)).astype(dgl_ref.dtype)
        dgl_ref[:, D_MODEL:2 * D_MODEL] = (d_ret * a_ret * (1.0 - g_ret)).astype(dgl_ref.dtype)
        d_att_b, d_ret_b = _mx(d_att), _mx(d_ret)
        dya = _nn(d_att_b, wpa_t)
        dyr = _nn(d_ret_b, wpr_t)
        gpt_acc[:, 0:512] += _tn(d_att_b, ya_b)
        gpt_acc[:, 512:1024] += _tn(d_ret_b, yr_b)
        dza_ref[...] = (dya * oa * (sga * (1.0 + za * (1.0 - sga)))).astype(dza_ref.dtype)
        doa = dya * sza
        doa_ref[...] = doa.astype(doa_ref.dtype)
        dt = jnp.transpose(doa * oa)
        del_ref[...] = jnp.sum(dt.reshape(8, HEAD_DIM, bs), axis=1)
        dzr_ref[...] = (dyr * yn * (sgr * (1.0 + zr * (1.0 - sgr)))).astype(dzr_ref.dtype)
        dyn = dyr * szr
        dgn_ref[...] += jnp.sum(dyn * on, axis=0, keepdims=True)
        don = dyn * gn_ref[...]
        for h in range(RET_HEADS):
            hs = slice(128 * h, 128 * (h + 1))
            dh, oh = don[:, hs], ons[h]
            doh = rstds[h] * (dh - jnp.mean(dh, axis=-1, keepdims=True) - oh * jnp.mean(dh * oh, axis=-1, keepdims=True))
            dor_ref[:, hs] = doh.astype(dor_ref.dtype)

        @pl.when(i == nb - 1)
        def _():
            gout_ref[...] = gout_acc[...].astype(gout_ref.dtype)
            gpt_ref[...] = gpt_acc[...].astype(gpt_ref.dtype)

    row = lambda w: pl.BlockSpec((bs, w), lambda i: (i, 0))
    el = lambda w, off: pl.BlockSpec((pl.Element(bs), pl.Element(w)), lambda i: (i * bs, off))
    vec = lambda w: pl.BlockSpec((1, w), lambda i: (0, 0))
    full = pl.BlockSpec((D_MODEL, D_MODEL), lambda i: (0, 0))
    sds = jax.ShapeDtypeStruct
    return pl.pallas_call(
        body, grid=(nb,), name="tail",
        in_specs=[row(D_MODEL), row(D_MODEL), row(512), row(512), el(512, ZA_B), el(512, ZR_B), el(2 * D_MODEL, GL_B),
                  vec(3 * D_MODEL), vec(D_MODEL), vec(512), full, full],
        out_specs=[row(D_MODEL), row(512), row(512), row(2 * D_MODEL), row(512), row(512),
                   pl.BlockSpec((8, bs), lambda i: (0, i)), full, full, vec(D_MODEL), vec(D_MODEL), vec(512), vec(128)],
        out_shape=[sds((seq, D_MODEL), F32), sds((seq, 512), MXU_DTYPE), sds((seq, 512), MXU_DTYPE),
                   sds((seq, 2 * D_MODEL), MXU_DTYPE), sds((seq, 512), MXU_DTYPE), sds((seq, 512), MXU_DTYPE),
                   sds((8, seq), F32), sds((D_MODEL, D_MODEL), MXU_DTYPE), sds((D_MODEL, D_MODEL), MXU_DTYPE),
                   sds((1, D_MODEL), F32), sds((1, D_MODEL), F32), sds((1, 512), F32), sds((1, 128), F32)],
        scratch_shapes=[pltpu.VMEM((D_MODEL, D_MODEL), F32), pltpu.VMEM((D_MODEL, D_MODEL), F32)],
        compiler_params=_cp("arbitrary"))(x, tgt, o_att, o_ret, p, p, p, mod, g_post, gn_g, wpt, wout)


def _assemble(p, cos, sin, qg2, kg2, dqt, dkp, dvp, dza, drq, drk, drv, dzr, dgl):
    seq = p.shape[0]
    bs = 512

    def body(p_ref, cos_ref, sin_ref, qg_ref, kg_ref, dqt_ref, dkp_ref, dvp_ref, dza_ref, drq_ref, drk_ref, drv_ref,
             dzr_ref, dgl_ref, dp_ref, dqg_ref, dkg_ref):
        @pl.when(pl.program_id(0) == 0)
        def _():
            dqg_ref[...] = jnp.zeros_like(dqg_ref)
            dkg_ref[...] = jnp.zeros_like(dkg_ref)

        m = _blockdiag64()
        cs, sn = cos_ref[...], sin_ref[...]
        lo = _lane((bs, 128)) < 64

        def norm_bwd(raw, dn, g, dg_ref):
            r = lax.rsqrt(_seg64(raw * raw, m) * (1.0 / 64) + EPS)
            xn = raw * r
            dg_ref[...] += jnp.sum(dn * xn, axis=0, keepdims=True)
            dxn = dn * g
            return r * (dxn - xn * (_seg64(dxn * xn, m) * (1.0 / 64)))

        for pr in range(4):
            sl = slice(128 * pr, 128 * (pr + 1))
            dn = _rope_t(dqt_ref[:, sl] * 0.125, cs, sn)
            dp_ref[:, QA + 128 * pr:QA + 128 * (pr + 1)] = norm_bwd(p_ref[:, sl], dn, qg_ref[...], dqg_ref).astype(dp_ref.dtype)
        fold = lambda a: a + pltpu.roll(a, 64, 1)
        dk = jnp.where(lo, fold(dkp_ref[0]), fold(dkp_ref[1]))
        dp_ref[:, KA:KA + 128] = norm_bwd(p_ref[:, KA:KA + 128], _rope_t(dk, cs, sn), kg_ref[...], dkg_ref).astype(dp_ref.dtype)
        dp_ref[:, VA:VA + 128] = jnp.where(lo, fold(dvp_ref[0]), fold(dvp_ref[1])).astype(dp_ref.dtype)
        dp_ref[:, ZA:ZA + 512] = dza_ref[...]
        for pr in range(2):
            sl = slice(128 * pr, 128 * (pr + 1))
            dp_ref[:, QR + 128 * pr:QR + 128 * (pr + 1)] = _rope_t(drq_ref[:, sl], cs, sn).astype(dp_ref.dtype)
            dp_ref[:, KR + 128 * pr:KR + 128 * (pr + 1)] = _rope_t(drk_ref[:, sl] * 0.125, cs, sn).astype(dp_ref.dtype)
        dp_ref[:, VR:VR + 512] = drv_ref[...]
        dp_ref[:, ZR:ZR + 512] = dzr_ref[...]
        dp_ref[:, GL:GL + 2 * D_MODEL] = dgl_ref[...]

    row = lambda w: pl.BlockSpec((bs, w), lambda i: (i, 0))
    gsp = pl.BlockSpec((1, 128), lambda i: (0, 0))
    dup = pl.BlockSpec((2, bs, 128), lambda i: (0, i, 0))
    return pl.pallas_call(
        body, grid=(seq // bs,), name="assemble_dp",
        in_specs=[row(768), row(128), row(128), gsp, gsp, row(512), dup, dup, row(512), row(256), row(256), row(512),
                  row(512), row(2 * D_MODEL)],
        out_specs=[row(IN_WIDTH), gsp, gsp],
        out_shape=[jax.ShapeDtypeStruct((seq, IN_WIDTH), MXU_DTYPE), jax.ShapeDtypeStruct((1, 128), F32),
                   jax.ShapeDtypeStruct((1, 128), F32)],
        compiler_params=_cp("arbitrary"))(p, cos, sin, qg2, kg2, dqt, dkp, dvp, dza, drq, drk, drv, dzr, dgl)


def _local_step(x, tgt, mod, g_pre, qn_g, kn_g, w_dec_f, w_dec_b, gn_g, g_post, w_attn, rest_start, rest_wait, send=None):
    send = send or (lambda name, arrays: None)
    seq = x.shape[0]
    cos, sin = _rope_tables(seq)
    qg2, kg2 = jnp.tile(qn_g, (1, 2)), jnp.tile(kn_g, (1, 2))
    lg_f = jax.nn.log_sigmoid(w_dec_f[0])
    lg_b = jax.nn.log_sigmoid(w_dec_b[0])
    tb = _ret_tables(lg_f, lg_b, RET_CHUNK)

    h, p_a, qt, kd, vd = _attn_inputs(x, mod, g_pre, w_attn, cos, sin, qg2, kg2)
    o_att, lse = _attn_fwd(qt, kd, vd, dep=rest_start(qt))
    win_t, wpt, wout = rest_wait(o_att)
    p_b, rq, rk, rv = _in_proj_rest(h, win_t, cos, sin)
    rf, rb = _ret_states(rk, rv, tb["wkf"], tb["wkb"], tb["dec_fb"], "ret_states_fwd")
    o_ret = _ret_fwd(rq, rk, rv, rf, rb, tb)
    (dout, dza, dzr, dgl, do_att, do_ret, delta, g_out, g_pt, dgate, dgpost, dgn, loss_l) = _tail(
        x, tgt, o_att, o_ret, p_b, mod, g_post, gn_g, wpt, wout)
    dep = send("early", (g_pt, g_out))
    dqt, dkp, dvp = _attn_bwd(qt, kd, vd, do_att, lse, delta, dep=dep)
    hb, hf = _ret_states(rq, do_ret, tb["wqb"], tb["wqf"], tb["dec_bf"], "ret_states_bwd")
    drq, drk, drv, dlg = _ret_bwd(rq, rk, rv, do_ret, rf, rb, hf, hb, tb)
    dp, dqg, dkg = _assemble(p_a, cos, sin, qg2, kg2, dqt, dkp, dvp, dza, drq, drk, drv, dzr, dgl)
    g_in_t = _matmul(dp, h, ta=True, tb=False, tm=256, tn=D_MODEL, out_dtype=MXU_DTYPE, name="in_proj_dw")
    dep = send("late", (g_in_t,))
    grad_x, dshift, dscale, dgpre = _in_proj_dx(x, dp, win_t, dout, mod, g_pre, dep=dep)

    dlg = jnp.sum(dlg, axis=1)
    small = dict(
        dmod=jnp.concatenate([dshift, dscale, dgate], axis=1),
        g_pre=dgpre, g_post=dgpost, gn_g=dgn,
        qn_g=dqg[:, :64] + dqg[:, 64:], kn_g=dkg[:, :64] + dkg[:, 64:],
        w_dec_f=(dlg[0:4] * jax.nn.sigmoid(-w_dec_f[0]))[None], w_dec_b=(dlg[4:8] * jax.nn.sigmoid(-w_dec_b[0]))[None],
        loss=jnp.sum(loss_l, axis=1, keepdims=True))
    return grad_x, g_in_t, g_pt, g_out, small


N_DEV = 8
N_CHIP = 4
HBM_SPEC = pl.BlockSpec(memory_space=pl.ANY)
VMEM_SPEC = pl.BlockSpec(memory_space=pltpu.VMEM)
SHARD_ROWS = (IN_WIDTH // N_CHIP, D_MODEL // N_CHIP, D_MODEL // N_CHIP)


def _coords():
    return lax.axis_index("x"), lax.axis_index("y"), lax.axis_index("c")


def _peer(k):
    x, y, c = _coords()
    return (1 - x if k & 4 else x, 1 - y if k & 2 else y, 1 - c if k & 1 else c)


def _dev_index(p):
    return 4 * p[0] + 2 * p[1] + p[2]


def _rows(ref, start, size):
    return ref.at[pl.ds(pl.multiple_of(start, 16), size), :]


def _remote(src, dst, ssem, rsem, dev):
    return pltpu.make_async_remote_copy(src_ref=src, dst_ref=dst, send_sem=ssem, recv_sem=rsem, device_id=dev,
                                        device_id_type=MESH)


ATTN_CHUNK = 96


def _mod_and_attn_rows(c, w_ada_s, b4, win_s):
    ncol = w_ada_s.shape[1]
    half = ATTN_COLS // 2
    offs = list(range(0, half, ATTN_CHUNK))
    nq = len(offs)
    rows = lambda ref, cc, off: ref.at[pl.ds(pl.multiple_of(cc * half + off, 16), ATTN_CHUNK), :]

    def body(c_ref, w_ref, b_ref, src, cs_ref, mod_ref, out, modsh, modall, ssem, rsem, lsem, asem, bsem, fsem, gsem, hsem):
        x, y, cc = _coords()
        me = _dev_index((x, y, cc))
        chip = me // 2
        sib = (x, y, 1 - cc)
        cs_ref[me] = c_ref[...]
        sends = []
        for k in range(1, N_DEV):
            cp = _remote(c_ref, cs_ref.at[me], ssem.at[k - 1], rsem.at[k - 1], _peer(k))
            cp.start()
            sends.append(cp)
        own = pltpu.make_async_copy(src.at[pl.ds(0, ATTN_COLS), :], out, lsem.at[0])
        bulk = [_remote(rows(src, cc, off), rows(out, cc, off), asem.at[(k2 - 1) * nq + q], bsem.at[q], (k2 // 2, k2 % 2, cc))
                for k2 in (1, 2) for q, off in enumerate(offs)]
        relay_chip = lambda q: 1 + q % 2

        @pl.when(chip == 0)
        def _():
            own.start()
            for cp in bulk:
                cp.start()

        for k in range(1, N_DEV):
            slot = cs_ref.at[_dev_index(_peer(k))]
            _remote(slot, slot, ssem.at[k - 1], rsem.at[k - 1], _peer(k)).wait_recv()
        cs = jnp.concatenate([cs_ref[d] for d in range(N_DEV)], axis=0)
        ms = _nn(cs * _sigmoid(cs), w_ref[...], precision=HIGHEST) + b_ref[pl.ds(chip, 1), :]
        modsh[...] = ms
        modall[chip] = ms
        for k2 in range(1, N_CHIP):
            cp = _remote(modsh, modall.at[chip], ssem.at[6 + k2], rsem.at[6 + k2], _peer(2 * k2))
            cp.start()
            sends.append(cp)

        @pl.when(chip != 0)
        def _():
            passed = []
            relays = [_remote(rows(out, cc, off), rows(out, cc, off), hsem.at[q], bsem.at[q], (1, 1, cc)) for q, off in enumerate(offs)]
            for q, off in enumerate(offs):
                got = rows(out, cc, off)
                _remote(got, got, asem.at[q], bsem.at[q], (0, 0, cc)).wait_recv()
                cp = _remote(got, got, fsem.at[q], gsem.at[q], sib)
                cp.start()
                passed.append(cp)
                pl.when(chip == relay_chip(q))(relays[q].start)
            for q, off in enumerate(offs):
                got = rows(out, 1 - cc, off)
                _remote(got, got, fsem.at[q], gsem.at[q], sib).wait_recv()
            for cp in passed:
                cp.wait_send()
            for q in range(nq):
                pl.when(chip == relay_chip(q))(relays[q].wait_send)

        for k2 in range(1, N_CHIP):
            slot = modall.at[_dev_index(_peer(2 * k2)) // 2]
            _remote(slot, slot, ssem.at[6 + k2], rsem.at[6 + k2], _peer(2 * k2)).wait_recv()
        for jj in range(N_CHIP):
            mod_ref[:, ncol * jj:ncol * (jj + 1)] = modall[jj, pl.ds(me, 1), :]
        for cp in sends:
            cp.wait_send()

        @pl.when(chip == 0)
        def _():
            for cp in bulk:
                cp.wait_send()
            own.wait()

    dma = pltpu.SemaphoreType.DMA
    return pl.pallas_call(
        body, name="mod_and_attn_rows", in_specs=[VMEM_SPEC] * 4, out_specs=[VMEM_SPEC, VMEM_SPEC, HBM_SPEC],
        out_shape=[jax.ShapeDtypeStruct((N_DEV, 1, D_MODEL), F32), jax.ShapeDtypeStruct((1, N_CHIP * ncol), F32),
                   jax.ShapeDtypeStruct((ATTN_COLS, win_s.shape[1]), win_s.dtype)],
        scratch_shapes=[pltpu.VMEM((N_DEV, ncol), F32), pltpu.VMEM((N_CHIP, N_DEV, ncol), F32), dma((10,)), dma((10,)),
                        dma((1,)), dma((2 * nq,)), dma((nq,)), dma((nq,)), dma((nq,)), dma((nq,))],
        compiler_params=_cp())(c, w_ada_s, b4, win_s)


GATHER_CHUNK = 32


def _chunks(kinds, chunk):
    out = []
    for t, kind in enumerate(kinds):
        half = SHARD_ROWS[kind] // 2
        step = chunk if half % chunk == 0 else half
        out += [(t, off, step) for off in range(0, half, step)]
    return out


SEM_SPEC = pl.BlockSpec(memory_space=pltpu.SEMAPHORE)
HBM_ONLY = pl.BlockSpec(memory_space=pltpu.HBM)
DATAFLOW = pltpu.SideEffectType.DATAFLOW_SIDE_EFFECTING


def _place_own(shards):
    nt = len(shards)

    def body(*refs):
        srcs, outs, lsem = refs[:nt], refs[nt:2 * nt], refs[2 * nt]
        x, y, _ = _coords()
        chip = 2 * x + y
        local = [pltpu.make_async_copy(srcs[t], _rows(outs[t], chip * SHARD_ROWS[t], SHARD_ROWS[t]), lsem.at[t]) for t in range(nt)]
        for cp in local:
            cp.start()
        for cp in local:
            cp.wait()

    return pl.pallas_call(
        body, name="place_own_shard", in_specs=[VMEM_SPEC] * nt, out_specs=[HBM_SPEC] * nt,
        out_shape=[jax.ShapeDtypeStruct((N_CHIP * SHARD_ROWS[t], s.shape[1]), s.dtype) for t, s in enumerate(shards)],
        scratch_shapes=[pltpu.SemaphoreType.DMA((nt,))], compiler_params=_cp())(*shards)


def _gather_rest_start(shards, zones, dep):
    n = len(shards)

    def body(*refs):
        srcs, lands = refs[:n], refs[n:2 * n]
        ssem, rsem = refs[2 * n + 1], refs[2 * n + 2]
        token = refs[-1]
        x, y, _ = _coords()
        chip = 2 * x + y
        for k2 in range(1, N_CHIP):
            for t in range(n):
                q = (k2 - 1) * n + t
                _remote(srcs[t], _rows(lands[t], chip * SHARD_ROWS[t], SHARD_ROWS[t]), ssem.at[q], rsem.at[q], _peer(2 * k2)).start()
        token[...] = jnp.zeros_like(token)

    hbm = lambda a: pltpu.with_memory_space_constraint(a, pltpu.HBM)
    dma = pltpu.SemaphoreType.DMA
    outs = pl.pallas_call(
        body, name="gather_rest", in_specs=[HBM_ONLY] * (2 * n) + [HBM_SPEC],
        out_specs=[SEM_SPEC, SEM_SPEC] + [HBM_ONLY] * (2 * n) + [VMEM_SPEC],
        out_shape=[dma((3 * n,)), dma((3 * n,))] + [pltpu.HBM(a.shape, a.dtype) for a in list(shards) + list(zones)]
        + [jax.ShapeDtypeStruct((8, 128), F32)],
        input_output_aliases={i: 2 + i for i in range(2 * n)},
        compiler_params=pltpu.CompilerParams(has_side_effects=DATAFLOW))(*[hbm(a) for a in list(shards) + list(zones)], dep)
    return outs[0], outs[1], outs[2:2 + n], outs[2 + n:2 + 2 * n], outs[-1]


def _gather_rest_wait(started, after):
    ssem, rsem, shards, zones, _ = started
    n = len(shards)

    def body(*refs):
        srcs, lands = refs[:n], refs[n:2 * n]
        ssem_ref, rsem_ref = refs[2 * n], refs[2 * n + 1]
        for k2 in range(1, N_CHIP):
            pchip = _dev_index(_peer(2 * k2)) // 2
            for t in range(n):
                q = (k2 - 1) * n + t
                cp = _remote(srcs[t], _rows(lands[t], pchip * SHARD_ROWS[t], SHARD_ROWS[t]), ssem_ref.at[q], rsem_ref.at[q], _peer(2 * k2))
                cp.wait_send()
                cp.wait_recv()

    outs = pl.pallas_call(
        body, name="gather_rest_wait", in_specs=[HBM_ONLY] * (2 * n) + [SEM_SPEC, SEM_SPEC, HBM_SPEC], out_specs=[HBM_ONLY] * (2 * n),
        out_shape=[pltpu.HBM(a.shape, a.dtype) for a in list(shards) + list(zones)],
        input_output_aliases={i: i for i in range(2 * n)},
        compiler_params=pltpu.CompilerParams(has_side_effects=DATAFLOW))(*shards, *zones, ssem, rsem, after)
    return outs[n:]


def _reduced_by(ref, t, dev):
    return _rows(ref, (dev // 2) * SHARD_ROWS[t] + (dev % 2) * (SHARD_ROWS[t] // 2), SHARD_ROWS[t] // 2)


def _scatter_start(grads, kinds, name):
    n = len(grads)

    def body(*refs):
        srcs, lands = refs[:n], refs[n:2 * n]
        ssem, rsem = refs[2 * n], refs[2 * n + 1]
        token = refs[-1]
        me = _dev_index(_coords())
        for k in range(1, N_DEV):
            for i, t in enumerate(kinds):
                q = (k - 1) * n + i
                _remote(_reduced_by(srcs[i], t, _dev_index(_peer(k))), lands[i].at[me], ssem.at[q], rsem.at[q], _peer(k)).start()
        token[...] = jnp.zeros_like(token)

    zones = [lax.empty((N_DEV, SHARD_ROWS[t] // 2, g.shape[1]), g.dtype) for g, t in zip(grads, kinds)]
    hbm = lambda a: pltpu.with_memory_space_constraint(a, pltpu.HBM)
    dma = pltpu.SemaphoreType.DMA
    outs = pl.pallas_call(
        body, name=name, in_specs=[HBM_ONLY] * (2 * n), out_specs=[SEM_SPEC, SEM_SPEC] + [HBM_ONLY] * (2 * n) + [VMEM_SPEC],
        out_shape=[dma((7 * n,)), dma((7 * n,))] + [pltpu.HBM(a.shape, a.dtype) for a in list(grads) + zones]
        + [jax.ShapeDtypeStruct((8, 128), F32)],
        input_output_aliases={i: 2 + i for i in range(2 * n)},
        compiler_params=pltpu.CompilerParams(has_side_effects=DATAFLOW))(*[hbm(a) for a in list(grads) + zones])
    return outs[0], outs[1], outs[2:2 + n], outs[2 + n:2 + 2 * n], outs[-1]


def _scatter_wait(started, kinds, after, name):
    ssem, rsem, grads, zones, _ = started
    n = len(grads)

    def body(*refs):
        srcs, lands = refs[:n], refs[n:2 * n]
        ssem_ref, rsem_ref = refs[2 * n], refs[2 * n + 1]
        for k in range(1, N_DEV):
            peer = _dev_index(_peer(k))
            for i, t in enumerate(kinds):
                q = (k - 1) * n + i
                cp = _remote(_reduced_by(srcs[i], t, peer), lands[i].at[peer], ssem_ref.at[q], rsem_ref.at[q], _peer(k))
                cp.wait_send()
                cp.wait_recv()

    outs = pl.pallas_call(
        body, name=name, in_specs=[HBM_ONLY] * (2 * n) + [SEM_SPEC, SEM_SPEC, HBM_SPEC], out_specs=[HBM_ONLY] * (2 * n),
        out_shape=[pltpu.HBM(a.shape, a.dtype) for a in list(grads) + list(zones)],
        input_output_aliases={i: i for i in range(2 * n)},
        compiler_params=pltpu.CompilerParams(has_side_effects=DATAFLOW))(*grads, *zones, ssem, rsem, after)
    return outs[:n], outs[n:]


def _grad_finish(grads, zones, kinds, name):
    nt = len(grads)
    pieces = _chunks(kinds, GATHER_CHUNK)
    npc = len(pieces)
    shard = [SHARD_ROWS[k] for k in kinds]

    def body(*refs):
        srcs, lands, outs = refs[:nt], refs[nt:2 * nt], refs[2 * nt:3 * nt]
        slots, sums = refs[3 * nt:4 * nt], refs[4 * nt:5 * nt]
        lsem, osem, xsem, ysem = refs[5 * nt:]
        x, y, c = _coords()
        me = _dev_index((x, y, c))
        sib = (x, y, 1 - c)
        loads = [pltpu.make_async_copy(_reduced_by(srcs[t], kinds[t], me), slots[t].at[me], lsem.at[t]) for t in range(nt)]
        for k in range(1, N_DEV):
            peer = _dev_index(_peer(k))
            loads += [pltpu.make_async_copy(lands[t].at[peer], slots[t].at[peer], lsem.at[k * nt + t]) for t in range(nt)]
        for cp in loads:
            cp.start()
        for cp in loads:
            cp.wait()
        sends = []
        place = lambda t, cc, off, n: _rows(outs[t], cc * (shard[t] // 2) + off, n)
        stores = []
        for q, (t, off, n) in enumerate(pieces):
            r = pl.ds(off, n)
            acc = slots[t][0, r, :].astype(F32)
            for d in range(1, N_DEV):
                acc = acc + slots[t][d, r, :].astype(F32)
            sums[t][r, :] = acc
            keep = pltpu.make_async_copy(sums[t].at[r, :], place(t, c, off, n), osem.at[q])
            give = _remote(sums[t].at[r, :], place(t, c, off, n), xsem.at[q], ysem.at[q], sib)
            keep.start()
            give.start()
            stores.append(keep)
            sends.append(give)
        for q, (t, off, n) in enumerate(pieces):
            got = place(t, 1 - c, off, n)
            _remote(got, got, xsem.at[q], ysem.at[q], sib).wait_recv()
        for cp in sends:
            cp.wait_send()
        for cp in stores:
            cp.wait()

    dma = pltpu.SemaphoreType.DMA
    return pl.pallas_call(
        body, name=name, in_specs=[HBM_SPEC] * (2 * nt), out_specs=[HBM_SPEC] * nt,
        out_shape=[jax.ShapeDtypeStruct((shard[t], g.shape[1]), F32) for t, g in enumerate(grads)],
        scratch_shapes=([pltpu.VMEM((N_DEV, shard[t] // 2, g.shape[1]), g.dtype) for t, g in enumerate(grads)]
                        + [pltpu.VMEM((shard[t] // 2, g.shape[1]), F32) for t, g in enumerate(grads)]
                        + [dma((N_DEV * nt,)), dma((npc,)), dma((npc,)), dma((npc,))]),
        compiler_params=_cp())(*grads, *zones)


def _adam_math(w, g, m, v):
    m = ADAM_B1 * m + (1.0 - ADAM_B1) * g
    v = ADAM_B2 * v + (1.0 - ADAM_B2) * (g * g)
    m_hat = m / (1.0 - ADAM_B1 ** ADAM_STEP)
    v_hat = v / (1.0 - ADAM_B2 ** ADAM_STEP)
    return -ADAM_LR * (m_hat / (jnp.sqrt(v_hat) + ADAM_EPS) + ADAM_WD * w), m, v


def _adamw(w, g, m, v, rb, name):
    rows, cols = w.shape

    def body(w_ref, g_ref, m_ref, v_ref, d_ref, nm_ref, nv_ref):
        d_ref[...], nm_ref[...], nv_ref[...] = _adam_math(w_ref[...], g_ref[...], m_ref[...], v_ref[...])

    spec = pl.BlockSpec((rb, cols), lambda i: (i, 0))
    return pl.pallas_call(body, grid=(rows // rb,), name=name, in_specs=[spec] * 4, out_specs=[spec] * 3,
                          out_shape=[jax.ShapeDtypeStruct(w.shape, F32)] * 3, compiler_params=_cp("parallel"))(w, g, m, v)


PACK = (("b_ada", 3 * D_MODEL), ("g_pre", D_MODEL), ("g_post", D_MODEL), ("gn_g", 512), ("qn_g", 64), ("kn_g", 64),
        ("w_dec_f", 4), ("w_dec_b", 4), ("loss", 1))
PACK_OFFSETS = {}
PACK_WIDTH = 0
for _name, _n in PACK:
    PACK_OFFSETS[_name] = PACK_WIDTH
    PACK_WIDTH += -(-_n // 128) * 128
SMALL_PARAMS = tuple(name for name, _ in PACK if name != "loss")


def _pack(parts):
    cols = []
    for name, n in PACK:
        pad = -(-n // 128) * 128 - n
        cols.append(parts[name].reshape(1, n).astype(F32))
        if pad:
            cols.append(jnp.zeros((1, pad), F32))
    return jnp.concatenate(cols, axis=1)


def _small_reduce(vec, cs, deps):
    ncol = 3 * D_MODEL // N_CHIP

    def body(vec_ref, cs_ref, *rest):
        tot_ref, gwa_ref, gat, ssem, rsem = rest[-5:]
        me = _dev_index(_coords())
        chip = me // 2
        gat[me] = vec_ref[...]
        sends = []
        for k in range(1, N_DEV):
            cp = _remote(vec_ref, gat.at[me], ssem.at[k - 1], rsem.at[k - 1], _peer(k))
            cp.start()
            sends.append(cp)
        for k in range(1, N_DEV):
            slot = gat.at[_dev_index(_peer(k))]
            _remote(slot, slot, ssem.at[k - 1], rsem.at[k - 1], _peer(k)).wait_recv()
        rows = [gat[d] for d in range(N_DEV)]
        tot = rows[0]
        for d in range(1, N_DEV):
            tot = tot + rows[d]
        tot_ref[...] = tot
        cs = cs_ref[...]
        ca_t = jnp.transpose(jnp.concatenate([cs * _sigmoid(cs), jnp.zeros((128 - N_DEV, D_MODEL), F32)], axis=0))
        dm = jnp.concatenate(rows + [jnp.zeros((128 - N_DEV, PACK_WIDTH), F32)], axis=0)
        for jj in range(N_CHIP):
            @pl.when(chip == jj)
            def _():
                gwa_ref[...] = _nn(ca_t, dm[:, ncol * jj:ncol * (jj + 1)], precision=HIGHEST)
        for cp in sends:
            cp.wait_send()

    return pl.pallas_call(
        body, name="small_reduce", in_specs=[VMEM_SPEC, VMEM_SPEC] + [HBM_SPEC] * len(deps), out_specs=[VMEM_SPEC] * 2,
        out_shape=[jax.ShapeDtypeStruct((1, PACK_WIDTH), F32), jax.ShapeDtypeStruct((D_MODEL, ncol), F32)],
        scratch_shapes=[pltpu.VMEM((N_DEV, 1, PACK_WIDTH), F32), pltpu.SemaphoreType.DMA((7,)), pltpu.SemaphoreType.DMA((7,))],
        compiler_params=_cp())(vec, cs, *deps)


def _small_adamw(tot, given):
    sizes = dict(PACK)
    np_ = len(SMALL_PARAMS)

    def body(*refs):
        tot_ref = refs[0]
        wmv = refs[1:1 + 3 * np_]
        outs = refs[1 + 3 * np_:1 + 7 * np_]
        loss_ref = refs[-1]
        for i, name in enumerate(SMALL_PARAMS):
            off, n = PACK_OFFSETS[name], sizes[name]
            g = tot_ref[:, off:off + n]
            w_ref, m_ref, v_ref = wmv[3 * i:3 * i + 3]
            outs[i][...] = g
            outs[np_ + i][...], outs[2 * np_ + i][...], outs[3 * np_ + i][...] = _adam_math(w_ref[...], g, m_ref[...], v_ref[...])
        loss_ref[...] = tot_ref[:, PACK_OFFSETS["loss"]:PACK_OFFSETS["loss"] + 1]

    flat = [a for name in SMALL_PARAMS for a in given[name]]
    shapes = [jax.ShapeDtypeStruct((1, sizes[name]), F32) for name in SMALL_PARAMS]
    res = pl.pallas_call(body, name="small_adamw", in_specs=[VMEM_SPEC] * (1 + 3 * np_), out_specs=[VMEM_SPEC] * (4 * np_ + 1),
                         out_shape=shapes * 4 + [jax.ShapeDtypeStruct((1, 1), F32)], compiler_params=_cp())(tot, *flat)
    return [dict(zip(SMALL_PARAMS, res[k * np_:(k + 1) * np_])) for k in range(4)], res[-1]


def kernel(x, c, w_ada, b_ada, g_pre, w_in, qn_g, kn_g, w_dec_f, w_dec_b, gn_g, w_pa, w_pr, w_out, g_post, loss_target, m_w_ada, m_b_ada, m_g_pre, m_w_in, m_qn_g, m_kn_g, m_w_dec_f, m_w_dec_b, m_gn_g, m_w_pa, m_w_pr, m_w_out, m_g_post, v_w_ada, v_b_ada, v_g_pre, v_w_in, v_qn_g, v_kn_g, v_w_dec_f, v_w_dec_b, v_gn_g, v_w_pa, v_w_pr, v_w_out, v_g_post):
    shards = (w_in[0].T.astype(MXU_DTYPE), jnp.concatenate([w_pa[0].T, w_pr[0].T], axis=1).astype(MXU_DTYPE),
              w_out[0].astype(MXU_DTYPE))
    cs, mod, w_attn = _mod_and_attn_rows(c, w_ada[0], b_ada.reshape(N_CHIP, 3 * D_MODEL // N_CHIP), shards[0])
    started = {}

    def rest_start(dep):
        started["rest"] = _gather_rest_start(shards, _place_own(shards), dep)
        return started["rest"][-1]

    def rest_wait(after):
        return _gather_rest_wait(started["rest"], after)

    kinds = {"early": (1, 2), "late": (0,)}

    def send(name, arrays):
        started[name] = _scatter_start(arrays, kinds[name], "grad_scatter_" + name)
        return started[name][-1]

    grad_x, _, _, _, small = _local_step(x[0], loss_target[0], mod, g_pre, qn_g, kn_g, w_dec_f, w_dec_b,
                                         gn_g, g_post, w_attn, rest_start, rest_wait, send=send)
    small = dict(small)
    small["b_ada"] = small.pop("dmod")
    given = dict(b_ada=(b_ada, m_b_ada, v_b_ada), g_pre=(g_pre, m_g_pre, v_g_pre), g_post=(g_post, m_g_post, v_g_post),
                 gn_g=(gn_g, m_gn_g, v_gn_g), qn_g=(qn_g, m_qn_g, v_qn_g), kn_g=(kn_g, m_kn_g, v_kn_g),
                 w_dec_f=(w_dec_f, m_w_dec_f, v_w_dec_f), w_dec_b=(w_dec_b, m_w_dec_b, v_w_dec_b))
    grads, deltas, new_m, new_v = {}, {}, {}, {}

    def update(name, w, g, m, v, back):
        d, nm, nv = _adamw(w, g, m, v, w.shape[0] // 4, "adamw_" + name)
        grads[name], deltas[name], new_m[name], new_v[name] = back(g), back(d), back(nm), back(nv)
        return d

    lead = lambda a: a[None]
    (g_pt, g_out), (z_pt, z_out) = _scatter_wait(started["early"], kinds["early"], grad_x, "grad_scatter_early_wait")
    r_pt, r_out = _grad_finish((g_pt, g_out), (z_pt, z_out), kinds["early"], "grad_finish_early")
    early = (update("w_pa", w_pa[0], r_pt[:, :512].T, m_w_pa[0], v_w_pa[0], lead),
             update("w_pr", w_pr[0], r_pt[:, 512:].T, m_w_pr[0], v_w_pr[0], lead),
             update("w_out", w_out[0], r_out, m_w_out[0], v_w_out[0], lead))
    tot, g_w_ada = _small_reduce(_pack(small), cs.reshape(N_DEV, D_MODEL), early)
    small_parts, loss = _small_adamw(tot, given)
    for dst, part in zip((grads, deltas, new_m, new_v), small_parts):
        dst.update(part)
    last = update("w_ada", w_ada[0], g_w_ada, m_w_ada[0], v_w_ada[0], lead)

    (g_in_t,), (z_in,) = _scatter_wait(started["late"], kinds["late"], last, "grad_scatter_late_wait")
    (r_in,) = _grad_finish((g_in_t,), (z_in,), kinds["late"], "grad_finish_late")
    tr = lambda a: a[0].T
    update("w_in", tr(w_in), r_in, tr(m_w_in), tr(v_w_in), lambda a: a.T[None])
    order = ("w_ada", "b_ada", "g_pre", "w_in", "qn_g", "kn_g", "w_dec_f", "w_dec_b", "gn_g", "w_pa", "w_pr", "w_out", "g_post")
    return (loss[0, 0], grad_x[None], *[grads[n] for n in order], *[deltas[n] for n in order],
            *[new_m[n] for n in order], *[new_v[n] for n in order])
```

```python
import jax
import jax.numpy as jnp
import numpy as np
from jax import lax
from jax.experimental import pallas as pl
from jax.experimental.pallas import tpu as pltpu

F32 = jnp.float32
BF16 = jnp.bfloat16
MXU_DTYPE = jnp.bfloat16

D_MODEL = 1024
GRID_W = 64
HEAD_DIM = 64
ROPE_THETA = 10000.0
EPS = 1e-6
RET_HEADS = 4
RET_CHUNK = 256
RET_GROUP = 4
IN_WIDTH = 4864
QA, KA, VA, ZA, QR, KR, VR, ZR, GL = 0, 512, 640, 768, 1280, 1536, 1792, 2304, 2816
ATTN_COLS = ZA
ZA_B, QR_B, KR_B, VR_B, ZR_B, GL_B = (c - ATTN_COLS for c in (ZA, QR, KR, VR, ZR, GL))

ADAM_LR, ADAM_B1, ADAM_B2, ADAM_EPS, ADAM_WD, ADAM_STEP = 0.001, 0.9, 0.999, 1e-08, 0.01, 10

VMEM_LIMIT = 56 * 1024 * 1024
MESH = pl.DeviceIdType.MESH
HIGHEST = lax.Precision.HIGHEST
LOG2E = 1.4426950408889634
LN2 = 0.6931471805599453


def _cp(*sem, **kw):
    if sem:
        kw["dimension_semantics"] = sem
    return pltpu.CompilerParams(vmem_limit_bytes=VMEM_LIMIT, **kw)


def _nn(a, b, **kw):
    return lax.dot_general(a, b, (((1,), (0,)), ((), ())), preferred_element_type=F32, **kw)


def _nt(a, b):
    return lax.dot_general(a, b, (((1,), (1,)), ((), ())), preferred_element_type=F32)


def _tn(a, b):
    return lax.dot_general(a, b, (((0,), (0,)), ((), ())), preferred_element_type=F32)


def _mx(x):
    return x.astype(MXU_DTYPE)


def _lane(shape):
    return lax.broadcasted_iota(jnp.int32, shape, 1)


def _sigmoid(z):
    return 1.0 / (1.0 + jnp.exp(-z))


def _rowsum128(x):
    s = jnp.sum(x, axis=0, keepdims=True)
    out = s[:, 0:128]
    for k in range(1, x.shape[1] // 128):
        out = out + s[:, 128 * k:128 * (k + 1)]
    return out


def _swap16(x):
    return jnp.where((_lane(x.shape) & 16) == 0, pltpu.roll(x, 112, 1), pltpu.roll(x, 16, 1))


def _rope(x, cos, sin):
    return x * cos + _swap16(x) * sin


def _rope_t(d, cos, sin):
    return d * cos + _swap16(d * sin)


def _blockdiag64():
    r = lax.broadcasted_iota(jnp.int32, (128, 128), 0) // 64
    c = lax.broadcasted_iota(jnp.int32, (128, 128), 1) // 64
    return (r == c).astype(BF16)


def _seg64(x, m):
    hi = x.astype(BF16)
    lo = (x - hi.astype(F32)).astype(BF16)
    return _nn(hi, m) + _nn(lo, m)


def _rope_tables(seq):
    t = np.arange(seq)
    row = (t // GRID_W).astype(np.float32)
    col = (t % GRID_W).astype(np.float32)
    half = HEAD_DIM // 2
    inv_freq = (np.float32(ROPE_THETA) ** (-np.arange(0, half, 2, dtype=np.float32) / np.float32(half))).astype(np.float32)
    ar = (row[:, None] * inv_freq[None, :]).astype(np.float32).astype(np.float64)
    ac = (col[:, None] * inv_freq[None, :]).astype(np.float32).astype(np.float64)
    cr, sr, cc, sc = np.cos(ar), np.sin(ar), np.cos(ac), np.sin(ac)
    cos64 = np.concatenate([cr, cr, cc, cc], axis=1)
    sin64 = np.concatenate([-sr, sr, -sc, sc], axis=1)
    return jnp.asarray(np.tile(cos64, (1, 2)), F32), jnp.asarray(np.tile(sin64, (1, 2)), F32)


def _attn_inputs(x, mod, g_pre, w_attn, cos, sin, qg2, kg2):
    seq = x.shape[0]
    bs = 512

    def body(x_ref, mod_ref, g_ref, w_ref, cos_ref, sin_ref, qg_ref, kg_ref, h_ref, pa_ref, qt_ref, kd_ref, vd_ref):
        xv = x_ref[...]
        r = lax.rsqrt(jnp.mean(xv * xv, axis=-1, keepdims=True) + EPS)
        shift = mod_ref[:, 0:D_MODEL]
        scale = mod_ref[:, D_MODEL:2 * D_MODEL]
        hb = ((xv * r) * g_ref[...] * (1.0 + scale) + shift).astype(h_ref.dtype)
        h_ref[...] = hb
        p = _nt(hb, w_ref[...])
        pa_ref[...] = p
        m = _blockdiag64()
        cs, sn = cos_ref[...], sin_ref[...]
        lo = _lane((bs, 128)) < 64
        for pr in range(4):
            q = p[:, QA + 128 * pr:QA + 128 * (pr + 1)]
            r = lax.rsqrt(_seg64(q * q, m) * (1.0 / 64) + EPS)
            qt_ref[:, 128 * pr:128 * (pr + 1)] = (_rope(q * r * qg_ref[...], cs, sn) * (0.125 * LOG2E)).astype(qt_ref.dtype)
        k = p[:, KA:KA + 128]
        r = lax.rsqrt(_seg64(k * k, m) * (1.0 / 64) + EPS)
        kr = _rope(k * r * kg_ref[...], cs, sn)
        ksw = pltpu.roll(kr, 64, 1)
        kd_ref[0] = jnp.where(lo, kr, ksw).astype(kd_ref.dtype)
        kd_ref[1] = jnp.where(lo, ksw, kr).astype(kd_ref.dtype)
        v = p[:, VA:VA + 128]
        vsw = pltpu.roll(v, 64, 1)
        vd_ref[0] = jnp.where(lo, v, vsw).astype(vd_ref.dtype)
        vd_ref[1] = jnp.where(lo, vsw, v).astype(vd_ref.dtype)

    row = lambda w: pl.BlockSpec((bs, w), lambda i: (i, 0))
    fix = lambda a, b: pl.BlockSpec((a, b), lambda i: (0, 0))
    dup = pl.BlockSpec((2, bs, 128), lambda i: (0, i, 0))
    sds = jax.ShapeDtypeStruct
    return pl.pallas_call(
        body, grid=(seq // bs,), name="attn_inputs",
        in_specs=[row(D_MODEL), fix(1, 3 * D_MODEL), fix(1, D_MODEL), fix(ATTN_COLS, D_MODEL), row(128), row(128), fix(1, 128), fix(1, 128)],
        out_specs=[row(D_MODEL), row(ATTN_COLS), row(512), dup, dup],
        out_shape=[sds((seq, D_MODEL), MXU_DTYPE), sds((seq, ATTN_COLS), F32), sds((seq, 512), MXU_DTYPE),
                   sds((2, seq, 128), MXU_DTYPE), sds((2, seq, 128), MXU_DTYPE)],
        compiler_params=_cp("parallel"))(x, mod, g_pre, w_attn, cos, sin, qg2, kg2)


def _in_proj_dx(x, dp, win_t, dout, mod, g_pre, dep=None):
    seq = x.shape[0]
    bs = 512
    deps, dep_specs = _dep_args(dep, 1)

    def body(x_ref, dp_ref, w_ref, dout_ref, mod_ref, g_ref, *rest):
        gx_ref, dshift_ref, dscale_ref, dg_ref = rest[-4:]

        @pl.when(pl.program_id(0) == 0)
        def _():
            dshift_ref[...] = jnp.zeros_like(dshift_ref)
            dscale_ref[...] = jnp.zeros_like(dscale_ref)
            dg_ref[...] = jnp.zeros_like(dg_ref)

        xv = x_ref[...]
        dh = _nn(dp_ref[...], w_ref[...])
        g = g_ref[...]
        r = lax.rsqrt(jnp.mean(xv * xv, axis=-1, keepdims=True) + EPS)
        xn = xv * r
        scale = mod_ref[:, D_MODEL:2 * D_MODEL]
        dshift_ref[...] += jnp.sum(dh, axis=0, keepdims=True)
        dscale_ref[...] += jnp.sum(dh * (xn * g), axis=0, keepdims=True)
        da = dh * (1.0 + scale)
        dg_ref[...] += jnp.sum(da * xn, axis=0, keepdims=True)
        dxn = da * g
        dx = r * (dxn - xn * jnp.mean(dxn * xn, axis=-1, keepdims=True))
        gx_ref[...] = dout_ref[...] + dx

    row = pl.BlockSpec((bs, D_MODEL), lambda i: (i, 0))
    vec = pl.BlockSpec((1, D_MODEL), lambda i: (0, 0))
    return pl.pallas_call(
        body, grid=(seq // bs,), name="in_proj_dx",
        in_specs=[row, pl.BlockSpec((bs, IN_WIDTH), lambda i: (i, 0)), pl.BlockSpec((IN_WIDTH, D_MODEL), lambda i: (0, 0)), row,
                  pl.BlockSpec((1, 3 * D_MODEL), lambda i: (0, 0)), vec] + dep_specs,
        out_specs=[row, vec, vec, vec],
        out_shape=[jax.ShapeDtypeStruct((seq, D_MODEL), F32)] + [jax.ShapeDtypeStruct((1, D_MODEL), F32)] * 3,
        compiler_params=_cp("arbitrary"))(x, dp, win_t, dout, mod, g_pre, *deps)


def _dep_args(dep, grid_rank):
    if dep is None:
        return [], []
    return [dep], [pl.BlockSpec(dep.shape, lambda *_: (0,) * dep.ndim)]


def _matmul(a, b, *, ta, tb, tm, tn, out_dtype, name, dep=None):
    kdim = a.shape[0] if ta else a.shape[1]
    m = a.shape[1] if ta else a.shape[0]
    n = b.shape[0] if tb else b.shape[1]
    assert m % tm == 0 and n % tn == 0
    deps, dep_specs = _dep_args(dep, 2)

    def body(a_ref, b_ref, *rest):
        o_ref = rest[-1]
        dims = (((0 if ta else 1,), (1 if tb else 0,)), ((), ()))
        o_ref[...] = lax.dot_general(a_ref[...], b_ref[...], dims, preferred_element_type=F32).astype(o_ref.dtype)

    a_spec = pl.BlockSpec((kdim, tm), lambda j, i: (0, i)) if ta else pl.BlockSpec((tm, kdim), lambda j, i: (i, 0))
    b_spec = pl.BlockSpec((tn, kdim), lambda j, i: (j, 0)) if tb else pl.BlockSpec((kdim, tn), lambda j, i: (0, j))
    return pl.pallas_call(
        body, grid=(n // tn, m // tm), name=name, in_specs=[a_spec, b_spec] + dep_specs,
        out_specs=pl.BlockSpec((tm, tn), lambda j, i: (i, j)),
        out_shape=jax.ShapeDtypeStruct((m, n), out_dtype), compiler_params=_cp("parallel", "parallel"))(a, b, *deps)


def _in_proj_rest(h, win_t, cos, sin):
    seq = h.shape[0]
    tm = min(1024, seq)
    ncols = IN_WIDTH - ATTN_COLS
    tn = ncols // 2
    assert ZR_B <= tn and ATTN_COLS % 128 == 0 and tn % 128 == 0

    def body(h_ref, w_ref, cos_ref, sin_ref, p_ref, rq_ref, rk_ref, rv_ref):
        p = _nt(h_ref[...], w_ref[...])
        p_ref[...] = p

        @pl.when(pl.program_id(0) == 0)
        def _():
            cs, sn = cos_ref[...], sin_ref[...]
            for pr in range(2):
                sl = slice(128 * pr, 128 * (pr + 1))
                rq_ref[:, sl] = _rope(p[:, QR_B + 128 * pr:QR_B + 128 * (pr + 1)], cs, sn).astype(rq_ref.dtype)
                rk_ref[:, sl] = (_rope(p[:, KR_B + 128 * pr:KR_B + 128 * (pr + 1)], cs, sn) * 0.125).astype(rk_ref.dtype)
            rv_ref[...] = p[:, VR_B:VR_B + 512].astype(rv_ref.dtype)

    nrow = seq // tm
    row = lambda w: pl.BlockSpec((tm, w), lambda j, i: (i, 0))
    park = lambda w: pl.BlockSpec((tm, w), lambda j, i: (jnp.where(j == 0, i, nrow - 1), 0))
    sds = jax.ShapeDtypeStruct
    return pl.pallas_call(
        body, grid=(2, nrow), name="in_proj_rest",
        in_specs=[row(D_MODEL), pl.BlockSpec((pl.Element(tn), pl.Element(D_MODEL)),
                                             lambda j, i: (pl.multiple_of(ATTN_COLS + j * tn, 128), 0)), row(128), row(128)],
        out_specs=[pl.BlockSpec((tm, tn), lambda j, i: (i, j)), park(256), park(256), park(512)],
        out_shape=[sds((seq, ncols), F32), sds((seq, 256), MXU_DTYPE), sds((seq, 256), MXU_DTYPE), sds((seq, 512), MXU_DTYPE)],
        compiler_params=_cp("arbitrary", "arbitrary"))(h, win_t, cos, sin)


def _halves(kd):
    lo = _lane(kd.shape) < 64
    z = jnp.zeros_like(kd)
    return jnp.concatenate([jnp.where(lo, kd, z), jnp.where(lo, z, kd)], axis=0)


def _attn_fwd(qt, kd, vd, dep=None):
    seq = qt.shape[0]
    bq, bk = min(2048, seq), min(1024, seq)
    nk = seq // bk
    deps, dep_specs = _dep_args(dep, 2)

    def body(q_ref, k_ref, v_ref, *rest):
        o_ref, lse_ref, m_scr, l_scr, acc = rest[-5:]
        j = pl.program_id(1)
        m_scr[...] = jnp.full_like(m_scr, -jnp.inf)
        l_scr[...] = jnp.zeros_like(l_scr)
        acc[...] = jnp.zeros_like(acc)
        sub = min(512, bq)
        nsub = bq // sub
        lo = _lane((sub, 128)) < 64

        def kv_block(n, carry):
            rows = pl.ds(pl.multiple_of(n * bk, bk), bk)
            k2, v2 = _halves(k_ref[rows, :]), _halves(v_ref[rows, :])
            for pr in range(2):
                for hf in range(nsub):
                    qr = slice(hf * sub, (hf + 1) * sub)
                    s2 = _nt(q_ref[qr, 128 * pr:128 * (pr + 1)], k2)
                    ps, alphas = [], []
                    for e in range(2):
                        g = 2 * pr + e
                        s = s2[:, e * bk:(e + 1) * bk]
                        m_prev = m_scr[g, qr]
                        m_next = jnp.maximum(m_prev, jnp.max(s, axis=1, keepdims=True))
                        alpha = jnp.exp2(m_prev - m_next)
                        pe = jnp.exp2(s - jnp.tile(m_next, (1, bk // 128)))
                        l_scr[g, qr] = alpha * l_scr[g, qr] + jnp.sum(pe, axis=1, keepdims=True)
                        m_scr[g, qr] = m_next
                        ps.append(_mx(pe))
                        alphas.append(alpha)
                    o2 = _nn(jnp.concatenate(ps, axis=1), v2)
                    sl = slice(128 * pr, 128 * (pr + 1))
                    acc[qr, sl] = acc[qr, sl] * jnp.where(lo, alphas[0], alphas[1]) + o2
            return carry

        lax.fori_loop(0, nk, kv_block, 0)
        lo_all = _lane((bq, 128)) < 64
        for pr in range(2):
            sl = slice(128 * pr, 128 * (pr + 1))
            o_ref[:, sl] = acc[:, sl] / jnp.where(lo_all, l_scr[2 * pr], l_scr[2 * pr + 1])
        for g in range(4):
            lse = jnp.transpose(m_scr[g] + jnp.log2(l_scr[g]))
            lse_ref[pl.ds(4 * j + g, 1), :] = lse[0:1, :]

    return pl.pallas_call(
        body, grid=(seq // bq, 2), name="attn_fwd",
        in_specs=[pl.BlockSpec((bq, 256), lambda i, j: (i, j)), pl.BlockSpec((None, seq, 128), lambda i, j: (j, 0, 0)),
                  pl.BlockSpec((None, seq, 128), lambda i, j: (j, 0, 0))] + dep_specs,
        out_specs=[pl.BlockSpec((bq, 256), lambda i, j: (i, j)), pl.BlockSpec((8, bq), lambda i, j: (0, i))],
        out_shape=[jax.ShapeDtypeStruct((seq, 512), F32), jax.ShapeDtypeStruct((8, seq), F32)],
        scratch_shapes=[pltpu.VMEM((4, bq, 128), F32), pltpu.VMEM((4, bq, 128), F32), pltpu.VMEM((bq, 256), F32)],
        compiler_params=_cp("parallel", "arbitrary"))(qt, kd, vd, *deps)


def _attn_bwd(qt, kd, vd, do, lse, delta, dep=None):
    seq = qt.shape[0]
    bq, bk = min(1024, seq), min(1024, seq)
    nq, nk = seq // bq, seq // bk
    deps, dep_specs = _dep_args(dep, 3)

    def body(q_ref, do_ref, k_ref, v_ref, lse_ref, del_ref, *rest):
        dq_ref, dk_ref, dv_ref = rest[-3:]
        j, i = pl.program_id(0), pl.program_id(1)
        dq_ref[...] = jnp.zeros_like(dq_ref)

        @pl.when(i == 0)
        def _():
            dk_ref[...] = jnp.zeros_like(dk_ref)
            dv_ref[...] = jnp.zeros_like(dv_ref)

        lo = _lane((bk, 128)) < 64
        big = lambda r: jnp.concatenate([jnp.broadcast_to(r[0], (bk, bq)), jnp.broadcast_to(r[1], (bk, bq))], axis=0)

        def kv_block(n, carry):
            rows = pl.ds(pl.multiple_of(n * bk, bk), bk)
            k2, v2 = _halves(k_ref[rows, :]), _halves(v_ref[rows, :])
            for pr in range(2):
                sl = slice(128 * pr, 128 * (pr + 1))
                qp, dop = q_ref[:, sl], do_ref[:, sl]
                s_t = _nt(k2, qp)
                dp_t = _nt(v2, dop)
                ls = [lse_ref[pl.ds(4 * j + 2 * pr + e, 1), :] for e in range(2)]
                dl = [del_ref[pl.ds(4 * j + 2 * pr + e, 1), :] for e in range(2)]
                p_t = jnp.exp2(s_t - big(ls))
                ds_t = _mx(p_t * (dp_t - big(dl)))
                rv = _nn(_mx(p_t), dop)
                rk = _nn(ds_t, qp) * LN2
                dv_ref[rows, :] += jnp.where(lo, rv[:bk], rv[bk:])
                dk_ref[rows, :] += jnp.where(lo, rk[:bk], rk[bk:])
                dq_ref[:, sl] += _tn(ds_t, k2)
            return carry

        lax.fori_loop(0, nk, kv_block, 0)

    return pl.pallas_call(
        body, grid=(2, nq), name="attn_bwd",
        in_specs=[pl.BlockSpec((bq, 256), lambda j, i: (i, j)), pl.BlockSpec((bq, 256), lambda j, i: (i, j)),
                  pl.BlockSpec((None, seq, 128), lambda j, i: (j, 0, 0)), pl.BlockSpec((None, seq, 128), lambda j, i: (j, 0, 0)),
                  pl.BlockSpec((8, bq), lambda j, i: (0, i)), pl.BlockSpec((8, bq), lambda j, i: (0, i))] + dep_specs,
        out_specs=[pl.BlockSpec((bq, 256), lambda j, i: (i, j)), pl.BlockSpec((None, seq, 128), lambda j, i: (j, 0, 0)),
                   pl.BlockSpec((None, seq, 128), lambda j, i: (j, 0, 0))],
        out_shape=[jax.ShapeDtypeStruct((seq, 512), F32), jax.ShapeDtypeStruct((2, seq, 128), F32),
                   jax.ShapeDtypeStruct((2, seq, 128), F32)],
        compiler_params=_cp("arbitrary", "arbitrary"))(qt, do, kd, vd, lse, delta, *deps)


def _ret_tables(lg_f, lg_b, c):
    idx = jnp.arange(c, dtype=F32)
    diff = idx[:, None] - idx[None, :]
    a, b = lg_f[:, None, None], lg_b[:, None, None]
    low, up = (diff >= 0)[None], (diff < 0)[None]
    dmat = jnp.where(low, jnp.exp(a * jnp.maximum(diff, 0.0)[None]), jnp.exp(b * jnp.maximum(-diff, 0.0)[None]))
    dda = jnp.where(low, diff[None] * jnp.exp(a * jnp.maximum(diff, 0.0)[None]), 0.0)
    ddb = jnp.where(up, -diff[None] * jnp.exp(b * jnp.maximum(-diff, 0.0)[None]), 0.0)
    rep = lambda w: jnp.broadcast_to(w[:, :, None], (RET_HEADS, c, 128))
    wqf = rep(jnp.exp(lg_f[:, None] * (idx + 1.0)[None]))
    wqb = rep(jnp.exp(lg_b[:, None] * (c - idx)[None]))
    wkf = rep(jnp.exp(lg_f[:, None] * (c - 1.0 - idx)[None]))
    wkb = rep(jnp.exp(lg_b[:, None] * idx[None]))
    cdf, cdb = jnp.exp(lg_f * c), jnp.exp(lg_b * c)
    dec_fb = jnp.broadcast_to(jnp.concatenate([cdf, cdb])[:, None], (8, 128))
    dec_bf = jnp.broadcast_to(jnp.concatenate([cdb, cdf])[:, None], (8, 128))
    return dict(dmat=dmat, dda=dda, ddb=ddb, wqf=wqf, wqb=wqb, wkf=wkf, wkb=wkb, dec_fb=dec_fb, dec_bf=dec_bf)


def _ret_states(xk, yv, w_fwd, w_rev, dec, name):
    seq = xk.shape[0]
    c = RET_CHUNK
    nc = seq // c
    grp = min(RET_GROUP, nc)
    ns = nc // grp

    def body(xf_ref, yf_ref, xr_ref, yr_ref, wf_ref, wr_ref, dec_ref, sf_ref, sr_ref, st_f, st_r):
        @pl.when(pl.program_id(0) == 0)
        def _():
            st_f[...] = jnp.zeros_like(st_f)
            st_r[...] = jnp.zeros_like(st_r)

        lane = _lane((c, 128))

        def chunk(g, carry):
            for x_ref, y_ref, w_ref, s_ref, st, base, gg in ((xf_ref, yf_ref, wf_ref, sf_ref, st_f, 0, g),
                                                             (xr_ref, yr_ref, wr_ref, sr_ref, st_r, 4, grp - 1 - g)):
                rows = pl.ds(pl.multiple_of(gg * c, c), c)
                for h in range(RET_HEADS):
                    pr, e = h // 2, h % 2
                    half = (lane < 64) if e == 0 else (lane >= 64)
                    s_ref[gg, h] = st[h].astype(s_ref.dtype)
                    xm = jnp.where(half, x_ref[rows, 128 * pr:128 * (pr + 1)].astype(F32) * w_ref[h], 0.0)
                    st[h] = st[h] * dec_ref[base + h:base + h + 1, :] + _tn(_mx(xm), _mx(y_ref[rows, 128 * h:128 * (h + 1)]))
            return carry

        lax.fori_loop(0, grp, chunk, 0, unroll=2)

    xs = lambda f: pl.BlockSpec((grp * c, 256), f)
    ys = lambda f: pl.BlockSpec((grp * c, 512), f)
    fwd, rev = (lambda n: (n, 0)), (lambda n: (ns - 1 - n, 0))
    wsp = pl.BlockSpec((RET_HEADS, c, 128), lambda n: (0, 0, 0))
    return pl.pallas_call(
        body, grid=(ns,), name=name,
        in_specs=[xs(fwd), ys(fwd), xs(rev), ys(rev), wsp, wsp, pl.BlockSpec((8, 128), lambda n: (0, 0))],
        out_specs=[pl.BlockSpec((grp, RET_HEADS, 128, 128), lambda n: (n, 0, 0, 0)),
                   pl.BlockSpec((grp, RET_HEADS, 128, 128), lambda n: (ns - 1 - n, 0, 0, 0))],
        out_shape=[jax.ShapeDtypeStruct((nc, RET_HEADS, 128, 128), MXU_DTYPE)] * 2,
        scratch_shapes=[pltpu.VMEM((RET_HEADS, 128, 128), F32), pltpu.VMEM((RET_HEADS, 128, 128), F32)],
        compiler_params=_cp("arbitrary"))(xk, yv, xk, yv, w_fwd, w_rev, dec)


def _ret_fwd(rq, rk, rv, rf, rb, tb):
    seq = rq.shape[0]
    c = RET_CHUNK
    grp = min(RET_GROUP, seq // c)

    def body(q_ref, k_ref, v_ref, rf_ref, rb_ref, d_ref, wqf_ref, wqb_ref, o_ref):
        lane = _lane((c, 128))

        def chunk(g, carry):
            rows = pl.ds(pl.multiple_of(g * c, c), c)
            for h in range(RET_HEADS):
                pr, e = h // 2, h % 2
                half = (lane < 64) if e == 0 else (lane >= 64)
                sl = slice(128 * pr, 128 * (pr + 1))
                qp, kp = q_ref[rows, sl], k_ref[rows, sl]
                km = jnp.where(half, kp, jnp.zeros_like(kp))
                sd = _mx(_nt(qp, km) * d_ref[h])
                qf = qp.astype(F32)
                o = _nn(sd, v_ref[rows, 128 * h:128 * (h + 1)])
                o = o + _nn(_mx(qf * wqf_ref[h]), rf_ref[g, h]) + _nn(_mx(qf * wqb_ref[h]), rb_ref[g, h])
                o_ref[rows, 128 * h:128 * (h + 1)] = o
            return carry

        lax.fori_loop(0, grp, chunk, 0, unroll=2)

    st = pl.BlockSpec((grp, RET_HEADS, 128, 128), lambda n: (n, 0, 0, 0))
    wsp = pl.BlockSpec((RET_HEADS, c, 128), lambda n: (0, 0, 0))
    return pl.pallas_call(
        body, grid=(seq // (grp * c),), name="ret_fwd",
        in_specs=[pl.BlockSpec((grp * c, 256), lambda n: (n, 0)), pl.BlockSpec((grp * c, 256), lambda n: (n, 0)),
                  pl.BlockSpec((grp * c, 512), lambda n: (n, 0)), st, st, pl.BlockSpec((RET_HEADS, c, c), lambda n: (0, 0, 0)), wsp, wsp],
        out_specs=pl.BlockSpec((grp * c, 512), lambda n: (n, 0)),
        out_shape=jax.ShapeDtypeStruct((seq, 512), F32), compiler_params=_cp("parallel"))(
            rq, rk, rv, rf, rb, tb["dmat"], tb["wqf"], tb["wqb"])


def _ret_bwd(rq, rk, rv, do, rf, rb, hf, hb, tb):
    seq = rq.shape[0]
    c = RET_CHUNK
    grp = min(RET_GROUP, seq // c)

    def body(q_ref, k_ref, v_ref, do_ref, rf_ref, rb_ref, hf_ref, hb_ref, d_ref, dda_ref, ddb_ref,
             wqf_ref, wqb_ref, wkf_ref, wkb_ref, cdec_ref, dq_ref, dk_ref, dv_ref, dlg_ref):
        @pl.when(pl.program_id(0) == 0)
        def _():
            dlg_ref[...] = jnp.zeros_like(dlg_ref)

        lane = _lane((c, 128))
        pos = lax.broadcasted_iota(jnp.int32, (c, 128), 0).astype(F32)

        def chunk(g, carry):
            rows = pl.ds(pl.multiple_of(g * c, c), c)
            for pr in range(2):
                sl = slice(128 * pr, 128 * (pr + 1))
                qp, kp = q_ref[rows, sl], k_ref[rows, sl]
                qf, kf = qp.astype(F32), kp.astype(F32)
                dq_acc = jnp.zeros((c, 128), F32)
                dk_acc = jnp.zeros((c, 128), F32)
                for e in range(2):
                    h = 2 * pr + e
                    half = (lane < 64) if e == 0 else (lane >= 64)
                    hs = slice(128 * h, 128 * (h + 1))
                    km = jnp.where(half, kp, jnp.zeros_like(kp))
                    kmf = km.astype(F32)
                    vh, doh = v_ref[rows, hs], do_ref[rows, hs]
                    rfh, rbh, hfh, hbh = rf_ref[g, h], rb_ref[g, h], hf_ref[g, h], hb_ref[g, h]
                    s = _nt(qp, km)
                    dpm = _nt(doh, vh)
                    ds_b = _mx(dpm * d_ref[h])
                    sd_b = _mx(s * d_ref[h])
                    dq_if = wqf_ref[h] * _nt(doh, rfh)
                    dq_ib = wqb_ref[h] * _nt(doh, rbh)
                    dq_acc = dq_acc + _nn(ds_b, km) + dq_if + dq_ib
                    dk_sf = wkf_ref[h] * _nt(vh, hfh)
                    dk_sb = wkb_ref[h] * _nt(vh, hbh)
                    dk_acc = dk_acc + jnp.where(half, _tn(ds_b, qp), 0.0) + dk_sf + dk_sb
                    dv = _tn(sd_b, doh) + _nn(_mx(kmf * wkf_ref[h]), hfh) + _nn(_mx(kmf * wkb_ref[h]), hbh)
                    dv_ref[rows, hs] = dv.astype(dv_ref.dtype)
                    sdp = s * dpm
                    hr_f = hfh.astype(F32) * rfh.astype(F32)
                    hr_b = hbh.astype(F32) * rbh.astype(F32)
                    da = (_rowsum128(sdp * dda_ref[h]) + _rowsum128((pos + 1.0) * qf * dq_if)
                          + _rowsum128((c - 1.0 - pos) * kf * dk_sf) + cdec_ref[h:h + 1, :] * _rowsum128(hr_f))
                    db = (_rowsum128(sdp * ddb_ref[h]) + _rowsum128((c - pos) * qf * dq_ib)
                          + _rowsum128(pos * kf * dk_sb) + cdec_ref[4 + h:5 + h, :] * _rowsum128(hr_b))
                    dlg_ref[h:h + 1, :] += da
                    dlg_ref[4 + h:5 + h, :] += db
                dq_ref[rows, sl] = dq_acc
                dk_ref[rows, sl] = dk_acc
            return carry

        lax.fori_loop(0, grp, chunk, 0, unroll=2)

    st = pl.BlockSpec((grp, RET_HEADS, 128, 128), lambda n: (n, 0, 0, 0))
    wsp = pl.BlockSpec((RET_HEADS, c, 128), lambda n: (0, 0, 0))
    dsp = pl.BlockSpec((RET_HEADS, c, c), lambda n: (0, 0, 0))
    x256 = pl.BlockSpec((grp * c, 256), lambda n: (n, 0))
    x512 = pl.BlockSpec((grp * c, 512), lambda n: (n, 0))
    cdec = tb["dec_fb"] * float(c)
    return pl.pallas_call(
        body, grid=(seq // (grp * c),), name="ret_bwd",
        in_specs=[x256, x256, x512, x512, st, st, st, st, dsp, dsp, dsp, wsp, wsp, wsp, wsp,
                  pl.BlockSpec((8, 128), lambda n: (0, 0))],
        out_specs=[x256, x256, x512, pl.BlockSpec((8, 128), lambda n: (0, 0))],
        out_shape=[jax.ShapeDtypeStruct((seq, 256), F32), jax.ShapeDtypeStruct((seq, 256), F32),
                   jax.ShapeDtypeStruct((seq, 512), MXU_DTYPE), jax.ShapeDtypeStruct((8, 128), F32)],
        compiler_params=_cp("arbitrary"))(
            rq, rk, rv, do, rf, rb, hf, hb, tb["dmat"], tb["dda"], tb["ddb"], tb["wqf"], tb["wqb"], tb["wkf"], tb["wkb"], cdec)


def _tail(x, tgt, o_att, o_ret, p, mod, g_post, gn_g, wpt, wout):
    seq = x.shape[0]
    bs = 256
    nb = seq // bs

    def body(x_ref, t_ref, oa_ref, or_ref, za_ref, zr_ref, gl_ref, mod_ref, gp_ref, gn_ref, wpt_ref, wout_ref,
             dout_ref, dza_ref, dzr_ref, dgl_ref, doa_ref, dor_ref, del_ref, gout_ref, gpt_ref,
             dgate_ref, dgpost_ref, dgn_ref, loss_ref, gout_acc, gpt_acc):
        i = pl.program_id(0)

        @pl.when(i == 0)
        def _():
            gout_acc[...] = jnp.zeros_like(gout_acc)
            gpt_acc[...] = jnp.zeros_like(gpt_acc)
            dgate_ref[...] = jnp.zeros_like(dgate_ref)
            dgpost_ref[...] = jnp.zeros_like(dgpost_ref)
            dgn_ref[...] = jnp.zeros_like(dgn_ref)
            loss_ref[...] = jnp.zeros_like(loss_ref)

        oa, za, zr = oa_ref[...], za_ref[...], zr_ref[...]
        sga, sgr = _sigmoid(za), _sigmoid(zr)
        sza, szr = za * sga, zr * sgr
        ya_b = _mx(oa * sza)
        ons, rstds = [], []
        for h in range(RET_HEADS):
            oh = or_ref[:, 128 * h:128 * (h + 1)]
            xc = oh - jnp.mean(oh, axis=-1, keepdims=True)
            rstd = lax.rsqrt(jnp.mean(xc * xc, axis=-1, keepdims=True) + EPS)
            ons.append(xc * rstd)
            rstds.append(rstd)
        on = jnp.concatenate(ons, axis=1)
        yn = on * gn_ref[...]
        yr_b = _mx(yn * szr)
        wpa_t, wpr_t = wpt_ref[:, 0:512], wpt_ref[:, 512:1024]
        a_att = _nt(ya_b, wpa_t)
        a_ret = _nt(yr_b, wpr_t)
        gts = _sigmoid(gl_ref[...])
        g_att, g_ret = gts[:, 0:D_MODEL], gts[:, D_MODEL:2 * D_MODEL]
        merged_b = _mx(g_att * a_att + g_ret * a_ret)
        u = _nn(merged_b, wout_ref[...])
        r = lax.rsqrt(jnp.mean(u * u, axis=-1, keepdims=True) + EPS)
        un = u * r
        gpost = gp_ref[...]
        y = un * gpost
        gate = mod_ref[:, 2 * D_MODEL:3 * D_MODEL]
        err = x_ref[...] + gate * y - t_ref[...]
        loss_ref[...] += (0.5 / D_MODEL) * _rowsum128(err * err)
        dout = err * (1.0 / D_MODEL)
        dout_ref[...] = dout
        dgate_ref[...] += jnp.sum(dout * y, axis=0, keepdims=True)
        dy = dout * gate
        dgpost_ref[...] += jnp.sum(dy * un, axis=0, keepdims=True)
        dun = dy * gpost
        du_b = _mx(r * (dun - un * jnp.mean(dun * un, axis=-1, keepdims=True)))
        dmerged = _nt(du_b, wout_ref[...])
        gout_acc[...] += _tn(merged_b, du_b)
        d_att, d_ret = dmerged * g_att, dmerged * g_ret
        dgl_ref[:, 0:D_MODEL] = (d_att * a_att * (1.0 - g_att)).astype(dgl_ref.dtype)
        dgl_ref[:, D_MODEL:2 * D_MODEL] = (d_ret * a_ret * (1.0 - g_ret)).astype(dgl_ref.dtype)
        d_att_b, d_ret_b = _mx(d_att), _mx(d_ret)
        dya = _nn(d_att_b, wpa_t)
        dyr = _nn(d_ret_b, wpr_t)
        gpt_acc[:, 0:512] += _tn(d_att_b, ya_b)
        gpt_acc[:, 512:1024] += _tn(d_ret_b, yr_b)
        dza_ref[...] = (dya * oa * (sga * (1.0 + za * (1.0 - sga)))).astype(dza_ref.dtype)
        doa = dya * sza
        doa_ref[...] = doa.astype(doa_ref.dtype)
        dt = jnp.transpose(doa * oa)
        del_ref[...] = jnp.sum(dt.reshape(8, HEAD_DIM, bs), axis=1)
        dzr_ref[...] = (dyr * yn * (sgr * (1.0 + zr * (1.0 - sgr)))).astype(dzr_ref.dtype)
        dyn = dyr * szr
        dgn_ref[...] += jnp.sum(dyn * on, axis=0, keepdims=True)
        don = dyn * gn_ref[...]
        for h in range(RET_HEADS):
            hs = slice(128 * h, 128 * (h + 1))
            dh, oh = don[:, hs], ons[h]
            doh = rstds[h] * (dh - jnp.mean(dh, axis=-1, keepdims=True) - oh * jnp.mean(dh * oh, axis=-1, keepdims=True))
            dor_ref[:, hs] = doh.astype(dor_ref.dtype)

        @pl.when(i == nb - 1)
        def _():
            gout_ref[...] = gout_acc[...].astype(gout_ref.dtype)
            gpt_ref[...] = gpt_acc[...].astype(gpt_ref.dtype)

    row = lambda w: pl.BlockSpec((bs, w), lambda i: (i, 0))
    el = lambda w, off: pl.BlockSpec((pl.Element(bs), pl.Element(w)), lambda i: (i * bs, off))
    vec = lambda w: pl.BlockSpec((1, w), lambda i: (0, 0))
    full = pl.BlockSpec((D_MODEL, D_MODEL), lambda i: (0, 0))
    sds = jax.ShapeDtypeStruct
    return pl.pallas_call(
        body, grid=(nb,), name="tail",
        in_specs=[row(D_MODEL), row(D_MODEL), row(512), row(512), el(512, ZA_B), el(512, ZR_B), el(2 * D_MODEL, GL_B),
                  vec(3 * D_MODEL), vec(D_MODEL), vec(512), full, full],
        out_specs=[row(D_MODEL), row(512), row(512), row(2 * D_MODEL), row(512), row(512),
                   pl.BlockSpec((8, bs), lambda i: (0, i)), full, full, vec(D_MODEL), vec(D_MODEL), vec(512), vec(128)],
        out_shape=[sds((seq, D_MODEL), F32), sds((seq, 512), MXU_DTYPE), sds((seq, 512), MXU_DTYPE),
                   sds((seq, 2 * D_MODEL), MXU_DTYPE), sds((seq, 512), MXU_DTYPE), sds((seq, 512), MXU_DTYPE),
                   sds((8, seq), F32), sds((D_MODEL, D_MODEL), MXU_DTYPE), sds((D_MODEL, D_MODEL), MXU_DTYPE),
                   sds((1, D_MODEL), F32), sds((1, D_MODEL), F32), sds((1, 512), F32), sds((1, 128), F32)],
        scratch_shapes=[pltpu.VMEM((D_MODEL, D_MODEL), F32), pltpu.VMEM((D_MODEL, D_MODEL), F32)],
        compiler_params=_cp("arbitrary"))(x, tgt, o_att, o_ret, p, p, p, mod, g_post, gn_g, wpt, wout)


def _assemble(p, cos, sin, qg2, kg2, dqt, dkp, dvp, dza, drq, drk, drv, dzr, dgl):
    seq = p.shape[0]
    bs = 512

    def body(p_ref, cos_ref, sin_ref, qg_ref, kg_ref, dqt_ref, dkp_ref, dvp_ref, dza_ref, drq_ref, drk_ref, drv_ref,
             dzr_ref, dgl_ref, dp_ref, dqg_ref, dkg_ref):
        @pl.when(pl.program_id(0) == 0)
        def _():
            dqg_ref[...] = jnp.zeros_like(dqg_ref)
            dkg_ref[...] = jnp.zeros_like(dkg_ref)

        m = _blockdiag64()
        cs, sn = cos_ref[...], sin_ref[...]
        lo = _lane((bs, 128)) < 64

        def norm_bwd(raw, dn, g, dg_ref):
            r = lax.rsqrt(_seg64(raw * raw, m) * (1.0 / 64) + EPS)
            xn = raw * r
            dg_ref[...] += jnp.sum(dn * xn, axis=0, keepdims=True)
            dxn = dn * g
            return r * (dxn - xn * (_seg64(dxn * xn, m) * (1.0 / 64)))

        for pr in range(4):
            sl = slice(128 * pr, 128 * (pr + 1))
            dn = _rope_t(dqt_ref[:, sl] * 0.125, cs, sn)
            dp_ref[:, QA + 128 * pr:QA + 128 * (pr + 1)] = norm_bwd(p_ref[:, sl], dn, qg_ref[...], dqg_ref).astype(dp_ref.dtype)
        fold = lambda a: a + pltpu.roll(a, 64, 1)
        dk = jnp.where(lo, fold(dkp_ref[0]), fold(dkp_ref[1]))
        dp_ref[:, KA:KA + 128] = norm_bwd(p_ref[:, KA:KA + 128], _rope_t(dk, cs, sn), kg_ref[...], dkg_ref).astype(dp_ref.dtype)
        dp_ref[:, VA:VA + 128] = jnp.where(lo, fold(dvp_ref[0]), fold(dvp_ref[1])).astype(dp_ref.dtype)
        dp_ref[:, ZA:ZA + 512] = dza_ref[...]
        for pr in range(2):
            sl = slice(128 * pr, 128 * (pr + 1))
            dp_ref[:, QR + 128 * pr:QR + 128 * (pr + 1)] = _rope_t(drq_ref[:, sl], cs, sn).astype(dp_ref.dtype)
            dp_ref[:, KR + 128 * pr:KR + 128 * (pr + 1)] = _rope_t(drk_ref[:, sl] * 0.125, cs, sn).astype(dp_ref.dtype)
        dp_ref[:, VR:VR + 512] = drv_ref[...]
        dp_ref[:, ZR:ZR + 512] = dzr_ref[...]
        dp_ref[:, GL:GL + 2 * D_MODEL] = dgl_ref[...]

    row = lambda w: pl.BlockSpec((bs, w), lambda i: (i, 0))
    gsp = pl.BlockSpec((1, 128), lambda i: (0, 0))
    dup = pl.BlockSpec((2, bs, 128), lambda i: (0, i, 0))
    return pl.pallas_call(
        body, grid=(seq // bs,), name="assemble_dp",
        in_specs=[row(768), row(128), row(128), gsp, gsp, row(512), dup, dup, row(512), row(256), row(256), row(512),
                  row(512), row(2 * D_MODEL)],
        out_specs=[row(IN_WIDTH), gsp, gsp],
        out_shape=[jax.ShapeDtypeStruct((seq, IN_WIDTH), MXU_DTYPE), jax.ShapeDtypeStruct((1, 128), F32),
                   jax.ShapeDtypeStruct((1, 128), F32)],
        compiler_params=_cp("arbitrary"))(p, cos, sin, qg2, kg2, dqt, dkp, dvp, dza, drq, drk, drv, dzr, dgl)


def _local_step(x, tgt, mod, g_pre, qn_g, kn_g, w_dec_f, w_dec_b, gn_g, g_post, w_attn, rest_start, rest_wait, send=None):
    send = send or (lambda name, arrays: None)
    seq = x.shape[0]
    cos, sin = _rope_tables(seq)
    qg2, kg2 = jnp.tile(qn_g, (1, 2)), jnp.tile(kn_g, (1, 2))
    lg_f = jax.nn.log_sigmoid(w_dec_f[0])
    lg_b = jax.nn.log_sigmoid(w_dec_b[0])
    tb = _ret_tables(lg_f, lg_b, RET_CHUNK)

    h, p_a, qt, kd, vd = _attn_inputs(x, mod, g_pre, w_attn, cos, sin, qg2, kg2)
    o_att, lse = _attn_fwd(qt, kd, vd, dep=rest_start(qt))
    win_t, wpt, wout = rest_wait(o_att)
    p_b, rq, rk, rv = _in_proj_rest(h, win_t, cos, sin)
    rf, rb = _ret_states(rk, rv, tb["wkf"], tb["wkb"], tb["dec_fb"], "ret_states_fwd")
    o_ret = _ret_fwd(rq, rk, rv, rf, rb, tb)
    (dout, dza, dzr, dgl, do_att, do_ret, delta, g_out, g_pt, dgate, dgpost, dgn, loss_l) = _tail(
        x, tgt, o_att, o_ret, p_b, mod, g_post, gn_g, wpt, wout)
    dep = send("early", (g_pt, g_out))
    dqt, dkp, dvp = _attn_bwd(qt, kd, vd, do_att, lse, delta, dep=dep)
    hb, hf = _ret_states(rq, do_ret, tb["wqb"], tb["wqf"], tb["dec_bf"], "ret_states_bwd")
    drq, drk, drv, dlg = _ret_bwd(rq, rk, rv, do_ret, rf, rb, hf, hb, tb)
    dp, dqg, dkg = _assemble(p_a, cos, sin, qg2, kg2, dqt, dkp, dvp, dza, drq, drk, drv, dzr, dgl)
    g_in_t = _matmul(dp, h, ta=True, tb=False, tm=256, tn=D_MODEL, out_dtype=MXU_DTYPE, name="in_proj_dw")
    dep = send("late", (g_in_t,))
    grad_x, dshift, dscale, dgpre = _in_proj_dx(x, dp, win_t, dout, mod, g_pre, dep=dep)

    dlg = jnp.sum(dlg, axis=1)
    small = dict(
        dmod=jnp.concatenate([dshift, dscale, dgate], axis=1),
        g_pre=dgpre, g_post=dgpost, gn_g=dgn,
        qn_g=dqg[:, :64] + dqg[:, 64:], kn_g=dkg[:, :64] + dkg[:, 64:],
        w_dec_f=(dlg[0:4] * jax.nn.sigmoid(-w_dec_f[0]))[None], w_dec_b=(dlg[4:8] * jax.nn.sigmoid(-w_dec_b[0]))[None],
        loss=jnp.sum(loss_l, axis=1, keepdims=True))
    return grad_x, g_in_t, g_pt, g_out, small


N_DEV = 8
N_CHIP = 4
HBM_SPEC = pl.BlockSpec(memory_space=pl.ANY)
VMEM_SPEC = pl.BlockSpec(memory_space=pltpu.VMEM)
SHARD_ROWS = (IN_WIDTH // N_CHIP, D_MODEL // N_CHIP, D_MODEL // N_CHIP)


def _coords():
    return lax.axis_index("x"), lax.axis_index("y"), lax.axis_index("c")


def _peer(k):
    x, y, c = _coords()
    return (1 - x if k & 4 else x, 1 - y if k & 2 else y, 1 - c if k & 1 else c)


def _dev_index(p):
    return 4 * p[0] + 2 * p[1] + p[2]


def _rows(ref, start, size):
    return ref.at[pl.ds(pl.multiple_of(start, 16), size), :]


def _remote(src, dst, ssem, rsem, dev):
    return pltpu.make_async_remote_copy(src_ref=src, dst_ref=dst, send_sem=ssem, recv_sem=rsem, device_id=dev,
                                        device_id_type=MESH)


ATTN_CHUNK = 48


def _mod_and_attn_rows(c, w_ada_s, b4, win_s):
    ncol = w_ada_s.shape[1]
    half = ATTN_COLS // 2
    offs = list(range(0, half, ATTN_CHUNK))
    nq = len(offs)
    rows = lambda ref, cc, off: ref.at[pl.ds(pl.multiple_of(cc * half + off, 16), ATTN_CHUNK), :]

    def body(c_ref, w_ref, b_ref, src, cs_ref, mod_ref, out, modsh, modall, ssem, rsem, lsem, asem, bsem, fsem, gsem, hsem):
        x, y, cc = _coords()
        me = _dev_index((x, y, cc))
        chip = me // 2
        sib = (x, y, 1 - cc)
        cs_ref[me] = c_ref[...]
        sends = []
        for k in range(1, N_DEV):
            cp = _remote(c_ref, cs_ref.at[me], ssem.at[k - 1], rsem.at[k - 1], _peer(k))
            cp.start()
            sends.append(cp)
        own = pltpu.make_async_copy(src.at[pl.ds(0, ATTN_COLS), :], out, lsem.at[0])
        bulk = [_remote(rows(src, cc, off), rows(out, cc, off), asem.at[(k2 - 1) * nq + q], bsem.at[q], (k2 // 2, k2 % 2, cc))
                for k2 in (1, 2) for q, off in enumerate(offs)]
        relay_chip = lambda q: 1 + q % 2

        @pl.when(chip == 0)
        def _():
            own.start()
            for cp in bulk:
                cp.start()

        for k in range(1, N_DEV):
            slot = cs_ref.at[_dev_index(_peer(k))]
            _remote(slot, slot, ssem.at[k - 1], rsem.at[k - 1], _peer(k)).wait_recv()
        cs = jnp.concatenate([cs_ref[d] for d in range(N_DEV)], axis=0)
        ms = _nn(cs * _sigmoid(cs), w_ref[...], precision=HIGHEST) + b_ref[pl.ds(chip, 1), :]
        modsh[...] = ms
        modall[chip] = ms
        for k2 in range(1, N_CHIP):
            cp = _remote(modsh, modall.at[chip], ssem.at[6 + k2], rsem.at[6 + k2], _peer(2 * k2))
            cp.start()
            sends.append(cp)

        @pl.when(chip != 0)
        def _():
            passed = []
            relays = [_remote(rows(out, cc, off), rows(out, cc, off), hsem.at[q], bsem.at[q], (1, 1, cc)) for q, off in enumerate(offs)]
            for q, off in enumerate(offs):
                got = rows(out, cc, off)
                _remote(got, got, asem.at[q], bsem.at[q], (0, 0, cc)).wait_recv()
                cp = _remote(got, got, fsem.at[q], gsem.at[q], sib)
                cp.start()
                passed.append(cp)
                pl.when(chip == relay_chip(q))(relays[q].start)
            for q, off in enumerate(offs):
                got = rows(out, 1 - cc, off)
                _remote(got, got, fsem.at[q], gsem.at[q], sib).wait_recv()
            for cp in passed:
                cp.wait_send()
            for q in range(nq):
                pl.when(chip == relay_chip(q))(relays[q].wait_send)

        for k2 in range(1, N_CHIP):
            slot = modall.at[_dev_index(_peer(2 * k2)) // 2]
            _remote(slot, slot, ssem.at[6 + k2], rsem.at[6 + k2], _peer(2 * k2)).wait_recv()
        for jj in range(N_CHIP):
            mod_ref[:, ncol * jj:ncol * (jj + 1)] = modall[jj, pl.ds(me, 1), :]
        for cp in sends:
            cp.wait_send()

        @pl.when(chip == 0)
        def _():
            for cp in bulk:
                cp.wait_send()
            own.wait()

    dma = pltpu.SemaphoreType.DMA
    return pl.pallas_call(
        body, name="mod_and_attn_rows", in_specs=[VMEM_SPEC] * 4, out_specs=[VMEM_SPEC, VMEM_SPEC, HBM_SPEC],
        out_shape=[jax.ShapeDtypeStruct((N_DEV, 1, D_MODEL), F32), jax.ShapeDtypeStruct((1, N_CHIP * ncol), F32),
                   jax.ShapeDtypeStruct((ATTN_COLS, win_s.shape[1]), win_s.dtype)],
        scratch_shapes=[pltpu.VMEM((N_DEV, ncol), F32), pltpu.VMEM((N_CHIP, N_DEV, ncol), F32), dma((10,)), dma((10,)),
                        dma((1,)), dma((2 * nq,)), dma((nq,)), dma((nq,)), dma((nq,)), dma((nq,))],
        compiler_params=_cp())(c, w_ada_s, b4, win_s)


GATHER_CHUNK = 32


def _chunks(kinds, chunk):
    out = []
    for t, kind in enumerate(kinds):
        half = SHARD_ROWS[kind] // 2
        step = chunk if half % chunk == 0 else half
        out += [(t, off, step) for off in range(0, half, step)]
    return out


SEM_SPEC = pl.BlockSpec(memory_space=pltpu.SEMAPHORE)
HBM_ONLY = pl.BlockSpec(memory_space=pltpu.HBM)
DATAFLOW = pltpu.SideEffectType.DATAFLOW_SIDE_EFFECTING


def _place_own(shards):
    nt = len(shards)

    def body(*refs):
        srcs, outs, lsem = refs[:nt], refs[nt:2 * nt], refs[2 * nt]
        x, y, _ = _coords()
        chip = 2 * x + y
        local = [pltpu.make_async_copy(srcs[t], _rows(outs[t], chip * SHARD_ROWS[t], SHARD_ROWS[t]), lsem.at[t]) for t in range(nt)]
        for cp in local:
            cp.start()
        for cp in local:
            cp.wait()

    return pl.pallas_call(
        body, name="place_own_shard", in_specs=[VMEM_SPEC] * nt, out_specs=[HBM_SPEC] * nt,
        out_shape=[jax.ShapeDtypeStruct((N_CHIP * SHARD_ROWS[t], s.shape[1]), s.dtype) for t, s in enumerate(shards)],
        scratch_shapes=[pltpu.SemaphoreType.DMA((nt,))], compiler_params=_cp())(*shards)


def _gather_rest_start(shards, zones, dep):
    n = len(shards)

    def body(*refs):
        srcs, lands = refs[:n], refs[n:2 * n]
        ssem, rsem = refs[2 * n + 1], refs[2 * n + 2]
        token = refs[-1]
        x, y, _ = _coords()
        chip = 2 * x + y
        for k2 in range(1, N_CHIP):
            for t in range(n):
                q = (k2 - 1) * n + t
                _remote(srcs[t], _rows(lands[t], chip * SHARD_ROWS[t], SHARD_ROWS[t]), ssem.at[q], rsem.at[q], _peer(2 * k2)).start()
        token[...] = jnp.zeros_like(token)

    hbm = lambda a: pltpu.with_memory_space_constraint(a, pltpu.HBM)
    dma = pltpu.SemaphoreType.DMA
    outs = pl.pallas_call(
        body, name="gather_rest", in_specs=[HBM_ONLY] * (2 * n) + [HBM_SPEC],
        out_specs=[SEM_SPEC, SEM_SPEC] + [HBM_ONLY] * (2 * n) + [VMEM_SPEC],
        out_shape=[dma((3 * n,)), dma((3 * n,))] + [pltpu.HBM(a.shape, a.dtype) for a in list(shards) + list(zones)]
        + [jax.ShapeDtypeStruct((8, 128), F32)],
        input_output_aliases={i: 2 + i for i in range(2 * n)},
        compiler_params=pltpu.CompilerParams(has_side_effects=DATAFLOW))(*[hbm(a) for a in list(shards) + list(zones)], dep)
    return outs[0], outs[1], outs[2:2 + n], outs[2 + n:2 + 2 * n], outs[-1]


def _gather_rest_wait(started, after):
    ssem, rsem, shards, zones, _ = started
    n = len(shards)

    def body(*refs):
        srcs, lands = refs[:n], refs[n:2 * n]
        ssem_ref, rsem_ref = refs[2 * n], refs[2 * n + 1]
        for k2 in range(1, N_CHIP):
            pchip = _dev_index(_peer(2 * k2)) // 2
            for t in range(n):
                q = (k2 - 1) * n + t
                cp = _remote(srcs[t], _rows(lands[t], pchip * SHARD_ROWS[t], SHARD_ROWS[t]), ssem_ref.at[q], rsem_ref.at[q], _peer(2 * k2))
                cp.wait_send()
                cp.wait_recv()

    outs = pl.pallas_call(
        body, name="gather_rest_wait", in_specs=[HBM_ONLY] * (2 * n) + [SEM_SPEC, SEM_SPEC, HBM_SPEC], out_specs=[HBM_ONLY] * (2 * n),
        out_shape=[pltpu.HBM(a.shape, a.dtype) for a in list(shards) + list(zones)],
        input_output_aliases={i: i for i in range(2 * n)},
        compiler_params=pltpu.CompilerParams(has_side_effects=DATAFLOW))(*shards, *zones, ssem, rsem, after)
    return outs[n:]


def _reduced_by(ref, t, dev):
    return _rows(ref, (dev // 2) * SHARD_ROWS[t] + (dev % 2) * (SHARD_ROWS[t] // 2), SHARD_ROWS[t] // 2)


def _scatter_start(grads, kinds, name):
    n = len(grads)

    def body(*refs):
        srcs, lands = refs[:n], refs[n:2 * n]
        ssem, rsem = refs[2 * n], refs[2 * n + 1]
        token = refs[-1]
        me = _dev_index(_coords())
        for k in range(1, N_DEV):
            for i, t in enumerate(kinds):
                q = (k - 1) * n + i
                _remote(_reduced_by(srcs[i], t, _dev_index(_peer(k))), lands[i].at[me], ssem.at[q], rsem.at[q], _peer(k)).start()
        token[...] = jnp.zeros_like(token)

    zones = [lax.empty((N_DEV, SHARD_ROWS[t] // 2, g.shape[1]), g.dtype) for g, t in zip(grads, kinds)]
    hbm = lambda a: pltpu.with_memory_space_constraint(a, pltpu.HBM)
    dma = pltpu.SemaphoreType.DMA
    outs = pl.pallas_call(
        body, name=name, in_specs=[HBM_ONLY] * (2 * n), out_specs=[SEM_SPEC, SEM_SPEC] + [HBM_ONLY] * (2 * n) + [VMEM_SPEC],
        out_shape=[dma((7 * n,)), dma((7 * n,))] + [pltpu.HBM(a.shape, a.dtype) for a in list(grads) + zones]
        + [jax.ShapeDtypeStruct((8, 128), F32)],
        input_output_aliases={i: 2 + i for i in range(2 * n)},
        compiler_params=pltpu.CompilerParams(has_side_effects=DATAFLOW))(*[hbm(a) for a in list(grads) + zones])
    return outs[0], outs[1], outs[2:2 + n], outs[2 + n:2 + 2 * n], outs[-1]


def _scatter_wait(started, kinds, after, name):
    ssem, rsem, grads, zones, _ = started
    n = len(grads)

    def body(*refs):
        srcs, lands = refs[:n], refs[n:2 * n]
        ssem_ref, rsem_ref = refs[2 * n], refs[2 * n + 1]
        for k in range(1, N_DEV):
            peer = _dev_index(_peer(k))
            for i, t in enumerate(kinds):
                q = (k - 1) * n + i
                cp = _remote(_reduced_by(srcs[i], t, peer), lands[i].at[peer], ssem_ref.at[q], rsem_ref.at[q], _peer(k))
                cp.wait_send()
                cp.wait_recv()

    outs = pl.pallas_call(
        body, name=name, in_specs=[HBM_ONLY] * (2 * n) + [SEM_SPEC, SEM_SPEC, HBM_SPEC], out_specs=[HBM_ONLY] * (2 * n),
        out_shape=[pltpu.HBM(a.shape, a.dtype) for a in list(grads) + list(zones)],
        input_output_aliases={i: i for i in range(2 * n)},
        compiler_params=pltpu.CompilerParams(has_side_effects=DATAFLOW))(*grads, *zones, ssem, rsem, after)
    return outs[:n], outs[n:]


def _grad_finish(grads, zones, kinds, name):
    nt = len(grads)
    pieces = _chunks(kinds, GATHER_CHUNK)
    npc = len(pieces)
    shard = [SHARD_ROWS[k] for k in kinds]

    def body(*refs):
        srcs, lands, outs = refs[:nt], refs[nt:2 * nt], refs[2 * nt:3 * nt]
        slots, sums = refs[3 * nt:4 * nt], refs[4 * nt:5 * nt]
        lsem, osem, xsem, ysem = refs[5 * nt:]
        x, y, c = _coords()
        me = _dev_index((x, y, c))
        sib = (x, y, 1 - c)
        loads = [pltpu.make_async_copy(_reduced_by(srcs[t], kinds[t], me), slots[t].at[me], lsem.at[t]) for t in range(nt)]
        for k in range(1, N_DEV):
            peer = _dev_index(_peer(k))
            loads += [pltpu.make_async_copy(lands[t].at[peer], slots[t].at[peer], lsem.at[k * nt + t]) for t in range(nt)]
        for cp in loads:
            cp.start()
        for cp in loads:
            cp.wait()
        sends = []
        place = lambda t, cc, off, n: _rows(outs[t], cc * (shard[t] // 2) + off, n)
        stores = []
        for q, (t, off, n) in enumerate(pieces):
            r = pl.ds(off, n)
            acc = slots[t][0, r, :].astype(F32)
            for d in range(1, N_DEV):
                acc = acc + slots[t][d, r, :].astype(F32)
            sums[t][r, :] = acc
            keep = pltpu.make_async_copy(sums[t].at[r, :], place(t, c, off, n), osem.at[q])
            give = _remote(sums[t].at[r, :], place(t, c, off, n), xsem.at[q], ysem.at[q], sib)
            keep.start()
            give.start()
            stores.append(keep)
            sends.append(give)
        for q, (t, off, n) in enumerate(pieces):
            got = place(t, 1 - c, off, n)
            _remote(got, got, xsem.at[q], ysem.at[q], sib).wait_recv()
        for cp in sends:
            cp.wait_send()
        for cp in stores:
            cp.wait()

    dma = pltpu.SemaphoreType.DMA
    return pl.pallas_call(
        body, name=name, in_specs=[HBM_SPEC] * (2 * nt), out_specs=[HBM_SPEC] * nt,
        out_shape=[jax.ShapeDtypeStruct((shard[t], g.shape[1]), F32) for t, g in enumerate(grads)],
        scratch_shapes=([pltpu.VMEM((N_DEV, shard[t] // 2, g.shape[1]), g.dtype) for t, g in enumerate(grads)]
                        + [pltpu.VMEM((shard[t] // 2, g.shape[1]), F32) for t, g in enumerate(grads)]
                        + [dma((N_DEV * nt,)), dma((npc,)), dma((npc,)), dma((npc,))]),
        compiler_params=_cp())(*grads, *zones)


def _adam_math(w, g, m, v):
    m = ADAM_B1 * m + (1.0 - ADAM_B1) * g
    v = ADAM_B2 * v + (1.0 - ADAM_B2) * (g * g)
    m_hat = m / (1.0 - ADAM_B1 ** ADAM_STEP)
    v_hat = v / (1.0 - ADAM_B2 ** ADAM_STEP)
    return -ADAM_LR * (m_hat / (jnp.sqrt(v_hat) + ADAM_EPS) + ADAM_WD * w), m, v


def _adamw(w, g, m, v, rb, name):
    rows, cols = w.shape

    def body(w_ref, g_ref, m_ref, v_ref, d_ref, nm_ref, nv_ref):
        d_ref[...], nm_ref[...], nv_ref[...] = _adam_math(w_ref[...], g_ref[...], m_ref[...], v_ref[...])

    spec = pl.BlockSpec((rb, cols), lambda i: (i, 0))
    return pl.pallas_call(body, grid=(rows // rb,), name=name, in_specs=[spec] * 4, out_specs=[spec] * 3,
                          out_shape=[jax.ShapeDtypeStruct(w.shape, F32)] * 3, compiler_params=_cp("parallel"))(w, g, m, v)


PACK = (("b_ada", 3 * D_MODEL), ("g_pre", D_MODEL), ("g_post", D_MODEL), ("gn_g", 512), ("qn_g", 64), ("kn_g", 64),
        ("w_dec_f", 4), ("w_dec_b", 4), ("loss", 1))
PACK_OFFSETS = {}
PACK_WIDTH = 0
for _name, _n in PACK:
    PACK_OFFSETS[_name] = PACK_WIDTH
    PACK_WIDTH += -(-_n // 128) * 128
SMALL_PARAMS = tuple(name for name, _ in PACK if name != "loss")


def _pack(parts):
    cols = []
    for name, n in PACK:
        pad = -(-n // 128) * 128 - n
        cols.append(parts[name].reshape(1, n).astype(F32))
        if pad:
            cols.append(jnp.zeros((1, pad), F32))
    return jnp.concatenate(cols, axis=1)


def _small_reduce(vec, cs, deps):
    ncol = 3 * D_MODEL // N_CHIP

    def body(vec_ref, cs_ref, *rest):
        tot_ref, gwa_ref, gat, ssem, rsem = rest[-5:]
        me = _dev_index(_coords())
        chip = me // 2
        gat[me] = vec_ref[...]
        sends = []
        for k in range(1, N_DEV):
            cp = _remote(vec_ref, gat.at[me], ssem.at[k - 1], rsem.at[k - 1], _peer(k))
            cp.start()
            sends.append(cp)
        for k in range(1, N_DEV):
            slot = gat.at[_dev_index(_peer(k))]
            _remote(slot, slot, ssem.at[k - 1], rsem.at[k - 1], _peer(k)).wait_recv()
        rows = [gat[d] for d in range(N_DEV)]
        tot = rows[0]
        for d in range(1, N_DEV):
            tot = tot + rows[d]
        tot_ref[...] = tot
        cs = cs_ref[...]
        ca_t = jnp.transpose(jnp.concatenate([cs * _sigmoid(cs), jnp.zeros((128 - N_DEV, D_MODEL), F32)], axis=0))
        dm = jnp.concatenate(rows + [jnp.zeros((128 - N_DEV, PACK_WIDTH), F32)], axis=0)
        for jj in range(N_CHIP):
            @pl.when(chip == jj)
            def _():
                gwa_ref[...] = _nn(ca_t, dm[:, ncol * jj:ncol * (jj + 1)], precision=HIGHEST)
        for cp in sends:
            cp.wait_send()

    return pl.pallas_call(
        body, name="small_reduce", in_specs=[VMEM_SPEC, VMEM_SPEC] + [HBM_SPEC] * len(deps), out_specs=[VMEM_SPEC] * 2,
        out_shape=[jax.ShapeDtypeStruct((1, PACK_WIDTH), F32), jax.ShapeDtypeStruct((D_MODEL, ncol), F32)],
        scratch_shapes=[pltpu.VMEM((N_DEV, 1, PACK_WIDTH), F32), pltpu.SemaphoreType.DMA((7,)), pltpu.SemaphoreType.DMA((7,))],
        compiler_params=_cp())(vec, cs, *deps)


def _small_adamw(tot, given):
    sizes = dict(PACK)
    np_ = len(SMALL_PARAMS)

    def body(*refs):
        tot_ref = refs[0]
        wmv = refs[1:1 + 3 * np_]
        outs = refs[1 + 3 * np_:1 + 7 * np_]
        loss_ref = refs[-1]
        for i, name in enumerate(SMALL_PARAMS):
            off, n = PACK_OFFSETS[name], sizes[name]
            g = tot_ref[:, off:off + n]
            w_ref, m_ref, v_ref = wmv[3 * i:3 * i + 3]
            outs[i][...] = g
            outs[np_ + i][...], outs[2 * np_ + i][...], outs[3 * np_ + i][...] = _adam_math(w_ref[...], g, m_ref[...], v_ref[...])
        loss_ref[...] = tot_ref[:, PACK_OFFSETS["loss"]:PACK_OFFSETS["loss"] + 1]

    flat = [a for name in SMALL_PARAMS for a in given[name]]
    shapes = [jax.ShapeDtypeStruct((1, sizes[name]), F32) for name in SMALL_PARAMS]
    res = pl.pallas_call(body, name="small_adamw", in_specs=[VMEM_SPEC] * (1 + 3 * np_), out_specs=[VMEM_SPEC] * (4 * np_ + 1),
                         out_shape=shapes * 4 + [jax.ShapeDtypeStruct((1, 1), F32)], compiler_params=_cp())(tot, *flat)
    return [dict(zip(SMALL_PARAMS, res[k * np_:(k + 1) * np_])) for k in range(4)], res[-1]


def kernel(x, c, w_ada, b_ada, g_pre, w_in, qn_g, kn_g, w_dec_f, w_dec_b, gn_g, w_pa, w_pr, w_out, g_post, loss_target, m_w_ada, m_b_ada, m_g_pre, m_w_in, m_qn_g, m_kn_g, m_w_dec_f, m_w_dec_b, m_gn_g, m_w_pa, m_w_pr, m_w_out, m_g_post, v_w_ada, v_b_ada, v_g_pre, v_w_in, v_qn_g, v_kn_g, v_w_dec_f, v_w_dec_b, v_gn_g, v_w_pa, v_w_pr, v_w_out, v_g_post):
    shards = (w_in[0].T.astype(MXU_DTYPE), jnp.concatenate([w_pa[0].T, w_pr[0].T], axis=1).astype(MXU_DTYPE),
              w_out[0].astype(MXU_DTYPE))
    cs, mod, w_attn = _mod_and_attn_rows(c, w_ada[0], b_ada.reshape(N_CHIP, 3 * D_MODEL // N_CHIP), shards[0])
    started = {}

    def rest_start(dep):
        started["rest"] = _gather_rest_start(shards, _place_own(shards), dep)
        return started["rest"][-1]

    def rest_wait(after):
        return _gather_rest_wait(started["rest"], after)

    kinds = {"early": (1, 2), "late": (0,)}

    def send(name, arrays):
        started[name] = _scatter_start(arrays, kinds[name], "grad_scatter_" + name)
        return started[name][-1]

    grad_x, _, _, _, small = _local_step(x[0], loss_target[0], mod, g_pre, qn_g, kn_g, w_dec_f, w_dec_b,
                                         gn_g, g_post, w_attn, rest_start, rest_wait, send=send)
    small = dict(small)
    small["b_ada"] = small.pop("dmod")
    given = dict(b_ada=(b_ada, m_b_ada, v_b_ada), g_pre=(g_pre, m_g_pre, v_g_pre), g_post=(g_post, m_g_post, v_g_post),
                 gn_g=(gn_g, m_gn_g, v_gn_g), qn_g=(qn_g, m_qn_g, v_qn_g), kn_g=(kn_g, m_kn_g, v_kn_g),
                 w_dec_f=(w_dec_f, m_w_dec_f, v_w_dec_f), w_dec_b=(w_dec_b, m_w_dec_b, v_w_dec_b))
    grads, deltas, new_m, new_v = {}, {}, {}, {}

    def update(name, w, g, m, v, back):
        d, nm, nv = _adamw(w, g, m, v, w.shape[0] // 4, "adamw_" + name)
        grads[name], deltas[name], new_m[name], new_v[name] = back(g), back(d), back(nm), back(nv)
        return d

    lead = lambda a: a[None]
    (g_pt, g_out), (z_pt, z_out) = _scatter_wait(started["early"], kinds["early"], grad_x, "grad_scatter_early_wait")
    r_pt, r_out = _grad_finish((g_pt, g_out), (z_pt, z_out), kinds["early"], "grad_finish_early")
    early = (update("w_pa", w_pa[0], r_pt[:, :512].T, m_w_pa[0], v_w_pa[0], lead),
             update("w_pr", w_pr[0], r_pt[:, 512:].T, m_w_pr[0], v_w_pr[0], lead),
             update("w_out", w_out[0], r_out, m_w_out[0], v_w_out[0], lead))
    tot, g_w_ada = _small_reduce(_pack(small), cs.reshape(N_DEV, D_MODEL), early)
    small_parts, loss = _small_adamw(tot, given)
    for dst, part in zip((grads, deltas, new_m, new_v), small_parts):
        dst.update(part)
    last = update("w_ada", w_ada[0], g_w_ada, m_w_ada[0], v_w_ada[0], lead)

    (g_in_t,), (z_in,) = _scatter_wait(started["late"], kinds["late"], last, "grad_scatter_late_wait")
    (r_in,) = _grad_finish((g_in_t,), (z_in,), kinds["late"], "grad_finish_late")
    tr = lambda a: a[0].T
    update("w_in", tr(w_in), r_in, tr(m_w_in), tr(v_w_in), lambda a: a.T[None])
    order = ("w_ada", "b_ada", "g_pre", "w_in", "qn_g", "kn_g", "w_dec_f", "w_dec_b", "gn_g", "w_pa", "w_pr", "w_out", "g_post")
    return (loss[0, 0], grad_x[None], *[grads[n] for n in order], *[deltas[n] for n in order],
            *[new_m[n] for n in order], *[new_v[n] for n in order])
```

```python
import jax
import jax.numpy as jnp
import numpy as np
from jax import lax
from jax.experimental import pallas as pl
from jax.experimental.pallas import tpu as pltpu

F32 = jnp.float32
BF16 = jnp.bfloat16
MXU_DTYPE = jnp.bfloat16

D_MODEL = 1024
GRID_W = 64
HEAD_DIM = 64
ROPE_THETA = 10000.0
EPS = 1e-6
RET_HEADS = 4
RET_CHUNK = 256
RET_GROUP = 4
IN_WIDTH = 4864
QA, KA, VA, ZA, QR, KR, VR, ZR, GL = 0, 512, 640, 768, 1280, 1536, 1792, 2304, 2816
ATTN_COLS = ZA
ZA_B, QR_B, KR_B, VR_B, ZR_B, GL_B = (c - ATTN_COLS for c in (ZA, QR, KR, VR, ZR, GL))

ADAM_LR, ADAM_B1, ADAM_B2, ADAM_EPS, ADAM_WD, ADAM_STEP = 0.001, 0.9, 0.999, 1e-08, 0.01, 10

VMEM_LIMIT = 56 * 1024 * 1024
MESH = pl.DeviceIdType.MESH
HIGHEST = lax.Precision.HIGHEST
LOG2E = 1.4426950408889634
LN2 = 0.6931471805599453


def _cp(*sem, **kw):
    if sem:
        kw["dimension_semantics"] = sem
    return pltpu.CompilerParams(vmem_limit_bytes=VMEM_LIMIT, **kw)


def _nn(a, b, **kw):
    return lax.dot_general(a, b, (((1,), (0,)), ((), ())), preferred_element_type=F32, **kw)


def _nt(a, b):
    return lax.dot_general(a, b, (((1,), (1,)), ((), ())), preferred_element_type=F32)


def _tn(a, b):
    return lax.dot_general(a, b, (((0,), (0,)), ((), ())), preferred_element_type=F32)


def _mx(x):
    return x.astype(MXU_DTYPE)


def _lane(shape):
    return lax.broadcasted_iota(jnp.int32, shape, 1)


def _sigmoid(z):
    return 1.0 / (1.0 + jnp.exp(-z))


def _rowsum128(x):
    s = jnp.sum(x, axis=0, keepdims=True)
    out = s[:, 0:128]
    for k in range(1, x.shape[1] // 128):
        out = out + s[:, 128 * k:128 * (k + 1)]
    return out


def _swap16(x):
    return jnp.where((_lane(x.shape) & 16) == 0, pltpu.roll(x, 112, 1), pltpu.roll(x, 16, 1))


def _rope(x, cos, sin):
    return x * cos + _swap16(x) * sin


def _rope_t(d, cos, sin):
    return d * cos + _swap16(d * sin)


def _blockdiag64():
    r = lax.broadcasted_iota(jnp.int32, (128, 128), 0) // 64
    c = lax.broadcasted_iota(jnp.int32, (128, 128), 1) // 64
    return (r == c).astype(BF16)


def _seg64(x, m):
    hi = x.astype(BF16)
    lo = (x - hi.astype(F32)).astype(BF16)
    return _nn(hi, m) + _nn(lo, m)


def _rope_tables(seq):
    t = np.arange(seq)
    row = (t // GRID_W).astype(np.float32)
    col = (t % GRID_W).astype(np.float32)
    half = HEAD_DIM // 2
    inv_freq = (np.float32(ROPE_THETA) ** (-np.arange(0, half, 2, dtype=np.float32) / np.float32(half))).astype(np.float32)
    ar = (row[:, None] * inv_freq[None, :]).astype(np.float32).astype(np.float64)
    ac = (col[:, None] * inv_freq[None, :]).astype(np.float32).astype(np.float64)
    cr, sr, cc, sc = np.cos(ar), np.sin(ar), np.cos(ac), np.sin(ac)
    cos64 = np.concatenate([cr, cr, cc, cc], axis=1)
    sin64 = np.concatenate([-sr, sr, -sc, sc], axis=1)
    return jnp.asarray(np.tile(cos64, (1, 2)), F32), jnp.asarray(np.tile(sin64, (1, 2)), F32)


def _attn_inputs(x, mod, g_pre, w_attn, cos, sin, qg2, kg2):
    seq = x.shape[0]
    bs = 512

    def body(x_ref, mod_ref, g_ref, w_ref, cos_ref, sin_ref, qg_ref, kg_ref, h_ref, pa_ref, qt_ref, kd_ref, vd_ref):
        xv = x_ref[...]
        r = lax.rsqrt(jnp.mean(xv * xv, axis=-1, keepdims=True) + EPS)
        shift = mod_ref[:, 0:D_MODEL]
        scale = mod_ref[:, D_MODEL:2 * D_MODEL]
        hb = ((xv * r) * g_ref[...] * (1.0 + scale) + shift).astype(h_ref.dtype)
        h_ref[...] = hb
        p = _nt(hb, w_ref[...])
        pa_ref[...] = p
        m = _blockdiag64()
        cs, sn = cos_ref[...], sin_ref[...]
        lo = _lane((bs, 128)) < 64
        for pr in range(4):
            q = p[:, QA + 128 * pr:QA + 128 * (pr + 1)]
            r = lax.rsqrt(_seg64(q * q, m) * (1.0 / 64) + EPS)
            qt_ref[:, 128 * pr:128 * (pr + 1)] = (_rope(q * r * qg_ref[...], cs, sn) * (0.125 * LOG2E)).astype(qt_ref.dtype)
        k = p[:, KA:KA + 128]
        r = lax.rsqrt(_seg64(k * k, m) * (1.0 / 64) + EPS)
        kr = _rope(k * r * kg_ref[...], cs, sn)
        ksw = pltpu.roll(kr, 64, 1)
        kd_ref[0] = jnp.where(lo, kr, ksw).astype(kd_ref.dtype)
        kd_ref[1] = jnp.where(lo, ksw, kr).astype(kd_ref.dtype)
        v = p[:, VA:VA + 128]
        vsw = pltpu.roll(v, 64, 1)
        vd_ref[0] = jnp.where(lo, v, vsw).astype(vd_ref.dtype)
        vd_ref[1] = jnp.where(lo, vsw, v).astype(vd_ref.dtype)

    row = lambda w: pl.BlockSpec((bs, w), lambda i: (i, 0))
    fix = lambda a, b: pl.BlockSpec((a, b), lambda i: (0, 0))
    dup = pl.BlockSpec((2, bs, 128), lambda i: (0, i, 0))
    sds = jax.ShapeDtypeStruct
    return pl.pallas_call(
        body, grid=(seq // bs,), name="attn_inputs",
        in_specs=[row(D_MODEL), fix(1, 3 * D_MODEL), fix(1, D_MODEL), fix(ATTN_COLS, D_MODEL), row(128), row(128), fix(1, 128), fix(1, 128)],
        out_specs=[row(D_MODEL), row(ATTN_COLS), row(512), dup, dup],
        out_shape=[sds((seq, D_MODEL), MXU_DTYPE), sds((seq, ATTN_COLS), F32), sds((seq, 512), MXU_DTYPE),
                   sds((2, seq, 128), MXU_DTYPE), sds((2, seq, 128), MXU_DTYPE)],
        compiler_params=_cp("parallel"))(x, mod, g_pre, w_attn, cos, sin, qg2, kg2)


def _in_proj_dx(x, dp, win_t, dout, mod, g_pre, dep=None):
    seq = x.shape[0]
    bs = 512
    deps, dep_specs = _dep_args(dep, 1)

    def body(x_ref, dp_ref, w_ref, dout_ref, mod_ref, g_ref, *rest):
        gx_ref, dshift_ref, dscale_ref, dg_ref = rest[-4:]

        @pl.when(pl.program_id(0) == 0)
        def _():
            dshift_ref[...] = jnp.zeros_like(dshift_ref)
            dscale_ref[...] = jnp.zeros_like(dscale_ref)
            dg_ref[...] = jnp.zeros_like(dg_ref)

        xv = x_ref[...]
        dh = _nn(dp_ref[...], w_ref[...])
        g = g_ref[...]
        r = lax.rsqrt(jnp.mean(xv * xv, axis=-1, keepdims=True) + EPS)
        xn = xv * r
        scale = mod_ref[:, D_MODEL:2 * D_MODEL]
        dshift_ref[...] += jnp.sum(dh, axis=0, keepdims=True)
        dscale_ref[...] += jnp.sum(dh * (xn * g), axis=0, keepdims=True)
        da = dh * (1.0 + scale)
        dg_ref[...] += jnp.sum(da * xn, axis=0, keepdims=True)
        dxn = da * g
        dx = r * (dxn - xn * jnp.mean(dxn * xn, axis=-1, keepdims=True))
        gx_ref[...] = dout_ref[...] + dx

    row = pl.BlockSpec((bs, D_MODEL), lambda i: (i, 0))
    vec = pl.BlockSpec((1, D_MODEL), lambda i: (0, 0))
    return pl.pallas_call(
        body, grid=(seq // bs,), name="in_proj_dx",
        in_specs=[row, pl.BlockSpec((bs, IN_WIDTH), lambda i: (i, 0)), pl.BlockSpec((IN_WIDTH, D_MODEL), lambda i: (0, 0)), row,
                  pl.BlockSpec((1, 3 * D_MODEL), lambda i: (0, 0)), vec] + dep_specs,
        out_specs=[row, vec, vec, vec],
        out_shape=[jax.ShapeDtypeStruct((seq, D_MODEL), F32)] + [jax.ShapeDtypeStruct((1, D_MODEL), F32)] * 3,
        compiler_params=_cp("arbitrary"))(x, dp, win_t, dout, mod, g_pre, *deps)


def _dep_args(dep, grid_rank):
    if dep is None:
        return [], []
    return [dep], [pl.BlockSpec(dep.shape, lambda *_: (0,) * dep.ndim)]


def _matmul(a, b, *, ta, tb, tm, tn, out_dtype, name, dep=None):
    kdim = a.shape[0] if ta else a.shape[1]
    m = a.shape[1] if ta else a.shape[0]
    n = b.shape[0] if tb else b.shape[1]
    assert m % tm == 0 and n % tn == 0
    deps, dep_specs = _dep_args(dep, 2)

    def body(a_ref, b_ref, *rest):
        o_ref = rest[-1]
        dims = (((0 if ta else 1,), (1 if tb else 0,)), ((), ()))
        o_ref[...] = lax.dot_general(a_ref[...], b_ref[...], dims, preferred_element_type=F32).astype(o_ref.dtype)

    a_spec = pl.BlockSpec((kdim, tm), lambda j, i: (0, i)) if ta else pl.BlockSpec((tm, kdim), lambda j, i: (i, 0))
    b_spec = pl.BlockSpec((tn, kdim), lambda j, i: (j, 0)) if tb else pl.BlockSpec((kdim, tn), lambda j, i: (0, j))
    return pl.pallas_call(
        body, grid=(n // tn, m // tm), name=name, in_specs=[a_spec, b_spec] + dep_specs,
        out_specs=pl.BlockSpec((tm, tn), lambda j, i: (i, j)),
        out_shape=jax.ShapeDtypeStruct((m, n), out_dtype), compiler_params=_cp("parallel", "parallel"))(a, b, *deps)


def _in_proj_rest(h, win_t, cos, sin):
    seq = h.shape[0]
    tm = min(1024, seq)
    ncols = IN_WIDTH - ATTN_COLS
    tn = ncols // 2
    assert ZR_B <= tn and ATTN_COLS % 128 == 0 and tn % 128 == 0

    def body(h_ref, w_ref, cos_ref, sin_ref, p_ref, rq_ref, rk_ref, rv_ref):
        p = _nt(h_ref[...], w_ref[...])
        p_ref[...] = p

        @pl.when(pl.program_id(0) == 0)
        def _():
            cs, sn = cos_ref[...], sin_ref[...]
            for pr in range(2):
                sl = slice(128 * pr, 128 * (pr + 1))
                rq_ref[:, sl] = _rope(p[:, QR_B + 128 * pr:QR_B + 128 * (pr + 1)], cs, sn).astype(rq_ref.dtype)
                rk_ref[:, sl] = (_rope(p[:, KR_B + 128 * pr:KR_B + 128 * (pr + 1)], cs, sn) * 0.125).astype(rk_ref.dtype)
            rv_ref[...] = p[:, VR_B:VR_B + 512].astype(rv_ref.dtype)

    nrow = seq // tm
    row = lambda w: pl.BlockSpec((tm, w), lambda j, i: (i, 0))
    park = lambda w: pl.BlockSpec((tm, w), lambda j, i: (jnp.where(j == 0, i, nrow - 1), 0))
    sds = jax.ShapeDtypeStruct
    return pl.pallas_call(
        body, grid=(2, nrow), name="in_proj_rest",
        in_specs=[row(D_MODEL), pl.BlockSpec((pl.Element(tn), pl.Element(D_MODEL)),
                                             lambda j, i: (pl.multiple_of(ATTN_COLS + j * tn, 128), 0)), row(128), row(128)],
        out_specs=[pl.BlockSpec((tm, tn), lambda j, i: (i, j)), park(256), park(256), park(512)],
        out_shape=[sds((seq, ncols), F32), sds((seq, 256), MXU_DTYPE), sds((seq, 256), MXU_DTYPE), sds((seq, 512), MXU_DTYPE)],
        compiler_params=_cp("arbitrary", "arbitrary"))(h, win_t, cos, sin)


def _halves(kd):
    lo = _lane(kd.shape) < 64
    z = jnp.zeros_like(kd)
    return jnp.concatenate([jnp.where(lo, kd, z), jnp.where(lo, z, kd)], axis=0)


def _attn_fwd(qt, kd, vd, dep=None):
    seq = qt.shape[0]
    bq, bk = min(2048, seq), min(1024, seq)
    nk = seq // bk
    deps, dep_specs = _dep_args(dep, 2)

    def body(q_ref, k_ref, v_ref, *rest):
        o_ref, lse_ref, m_scr, l_scr, acc = rest[-5:]
        j = pl.program_id(1)
        m_scr[...] = jnp.full_like(m_scr, -jnp.inf)
        l_scr[...] = jnp.zeros_like(l_scr)
        acc[...] = jnp.zeros_like(acc)
        sub = min(512, bq)
        nsub = bq // sub
        lo = _lane((sub, 128)) < 64

        def kv_block(n, carry):
            rows = pl.ds(pl.multiple_of(n * bk, bk), bk)
            k2, v2 = _halves(k_ref[rows, :]), _halves(v_ref[rows, :])
            for pr in range(2):
                for hf in range(nsub):
                    qr = slice(hf * sub, (hf + 1) * sub)
                    s2 = _nt(q_ref[qr, 128 * pr:128 * (pr + 1)], k2)
                    ps, alphas = [], []
                    for e in range(2):
                        g = 2 * pr + e
                        s = s2[:, e * bk:(e + 1) * bk]
                        m_prev = m_scr[g, qr]
                        m_next = jnp.maximum(m_prev, jnp.max(s, axis=1, keepdims=True))
                        alpha = jnp.exp2(m_prev - m_next)
                        pe = jnp.exp2(s - jnp.tile(m_next, (1, bk // 128)))
                        l_scr[g, qr] = alpha * l_scr[g, qr] + jnp.sum(pe, axis=1, keepdims=True)
                        m_scr[g, qr] = m_next
                        ps.append(_mx(pe))
                        alphas.append(alpha)
                    o2 = _nn(jnp.concatenate(ps, axis=1), v2)
                    sl = slice(128 * pr, 128 * (pr + 1))
                    acc[qr, sl] = acc[qr, sl] * jnp.where(lo, alphas[0], alphas[1]) + o2
            return carry

        lax.fori_loop(0, nk, kv_block, 0)
        lo_all = _lane((bq, 128)) < 64
        for pr in range(2):
            sl = slice(128 * pr, 128 * (pr + 1))
            o_ref[:, sl] = acc[:, sl] / jnp.where(lo_all, l_scr[2 * pr], l_scr[2 * pr + 1])
        for g in range(4):
            lse = jnp.transpose(m_scr[g] + jnp.log2(l_scr[g]))
            lse_ref[pl.ds(4 * j + g, 1), :] = lse[0:1, :]

    return pl.pallas_call(
        body, grid=(seq // bq, 2), name="attn_fwd",
        in_specs=[pl.BlockSpec((bq, 256), lambda i, j: (i, j)), pl.BlockSpec((None, seq, 128), lambda i, j: (j, 0, 0)),
                  pl.BlockSpec((None, seq, 128), lambda i, j: (j, 0, 0))] + dep_specs,
        out_specs=[pl.BlockSpec((bq, 256), lambda i, j: (i, j)), pl.BlockSpec((8, bq), lambda i, j: (0, i))],
        out_shape=[jax.ShapeDtypeStruct((seq, 512), F32), jax.ShapeDtypeStruct((8, seq), F32)],
        scratch_shapes=[pltpu.VMEM((4, bq, 128), F32), pltpu.VMEM((4, bq, 128), F32), pltpu.VMEM((bq, 256), F32)],
        compiler_params=_cp("parallel", "arbitrary"))(qt, kd, vd, *deps)


def _attn_bwd(qt, kd, vd, do, lse, delta, dep=None):
    seq = qt.shape[0]
    bq, bk = min(1024, seq), min(1024, seq)
    nq, nk = seq // bq, seq // bk
    deps, dep_specs = _dep_args(dep, 3)

    def body(q_ref, do_ref, k_ref, v_ref, lse_ref, del_ref, *rest):
        dq_ref, dk_ref, dv_ref = rest[-3:]
        j, i = pl.program_id(0), pl.program_id(1)
        dq_ref[...] = jnp.zeros_like(dq_ref)

        @pl.when(i == 0)
        def _():
            dk_ref[...] = jnp.zeros_like(dk_ref)
            dv_ref[...] = jnp.zeros_like(dv_ref)

        lo = _lane((bk, 128)) < 64
        big = lambda r: jnp.concatenate([jnp.broadcast_to(r[0], (bk, bq)), jnp.broadcast_to(r[1], (bk, bq))], axis=0)

        def kv_block(n, carry):
            rows = pl.ds(pl.multiple_of(n * bk, bk), bk)
            k2, v2 = _halves(k_ref[rows, :]), _halves(v_ref[rows, :])
            for pr in range(2):
                sl = slice(128 * pr, 128 * (pr + 1))
                qp, dop = q_ref[:, sl], do_ref[:, sl]
                s_t = _nt(k2, qp)
                dp_t = _nt(v2, dop)
                ls = [lse_ref[pl.ds(4 * j + 2 * pr + e, 1), :] for e in range(2)]
                dl = [del_ref[pl.ds(4 * j + 2 * pr + e, 1), :] for e in range(2)]
                p_t = jnp.exp2(s_t - big(ls))
                ds_t = _mx(p_t * (dp_t - big(dl)))
                rv = _nn(_mx(p_t), dop)
                rk = _nn(ds_t, qp) * LN2
                dv_ref[rows, :] += jnp.where(lo, rv[:bk], rv[bk:])
                dk_ref[rows, :] += jnp.where(lo, rk[:bk], rk[bk:])
                dq_ref[:, sl] += _tn(ds_t, k2)
            return carry

        lax.fori_loop(0, nk, kv_block, 0)

    return pl.pallas_call(
        body, grid=(2, nq), name="attn_bwd",
        in_specs=[pl.BlockSpec((bq, 256), lambda j, i: (i, j)), pl.BlockSpec((bq, 256), lambda j, i: (i, j)),
                  pl.BlockSpec((None, seq, 128), lambda j, i: (j, 0, 0)), pl.BlockSpec((None, seq, 128), lambda j, i: (j, 0, 0)),
                  pl.BlockSpec((8, bq), lambda j, i: (0, i)), pl.BlockSpec((8, bq), lambda j, i: (0, i))] + dep_specs,
        out_specs=[pl.BlockSpec((bq, 256), lambda j, i: (i, j)), pl.BlockSpec((None, seq, 128), lambda j, i: (j, 0, 0)),
                   pl.BlockSpec((None, seq, 128), lambda j, i: (j, 0, 0))],
        out_shape=[jax.ShapeDtypeStruct((seq, 512), F32), jax.ShapeDtypeStruct((2, seq, 128), F32),
                   jax.ShapeDtypeStruct((2, seq, 128), F32)],
        compiler_params=_cp("arbitrary", "arbitrary"))(qt, do, kd, vd, lse, delta, *deps)


def _ret_tables(lg_f, lg_b, c):
    idx = jnp.arange(c, dtype=F32)
    diff = idx[:, None] - idx[None, :]
    a, b = lg_f[:, None, None], lg_b[:, None, None]
    low, up = (diff >= 0)[None], (diff < 0)[None]
    dmat = jnp.where(low, jnp.exp(a * jnp.maximum(diff, 0.0)[None]), jnp.exp(b * jnp.maximum(-diff, 0.0)[None]))
    dda = jnp.where(low, diff[None] * jnp.exp(a * jnp.maximum(diff, 0.0)[None]), 0.0)
    ddb = jnp.where(up, -diff[None] * jnp.exp(b * jnp.maximum(-diff, 0.0)[None]), 0.0)
    rep = lambda w: jnp.broadcast_to(w[:, :, None], (RET_HEADS, c, 128))
    wqf = rep(jnp.exp(lg_f[:, None] * (idx + 1.0)[None]))
    wqb = rep(jnp.exp(lg_b[:, None] * (c - idx)[None]))
    wkf = rep(jnp.exp(lg_f[:, None] * (c - 1.0 - idx)[None]))
    wkb = rep(jnp.exp(lg_b[:, None] * idx[None]))
    cdf, cdb = jnp.exp(lg_f * c), jnp.exp(lg_b * c)
    dec_fb = jnp.broadcast_to(jnp.concatenate([cdf, cdb])[:, None], (8, 128))
    dec_bf = jnp.broadcast_to(jnp.concatenate([cdb, cdf])[:, None], (8, 128))
    return dict(dmat=dmat, dda=dda, ddb=ddb, wqf=wqf, wqb=wqb, wkf=wkf, wkb=wkb, dec_fb=dec_fb, dec_bf=dec_bf)


def _ret_states(xk, yv, w_fwd, w_rev, dec, name):
    seq = xk.shape[0]
    c = RET_CHUNK
    nc = seq // c
    grp = min(RET_GROUP, nc)
    ns = nc // grp

    def body(xf_ref, yf_ref, xr_ref, yr_ref, wf_ref, wr_ref, dec_ref, sf_ref, sr_ref, st_f, st_r):
        @pl.when(pl.program_id(0) == 0)
        def _():
            st_f[...] = jnp.zeros_like(st_f)
            st_r[...] = jnp.zeros_like(st_r)

        lane = _lane((c, 128))

        def chunk(g, carry):
            for x_ref, y_ref, w_ref, s_ref, st, base, gg in ((xf_ref, yf_ref, wf_ref, sf_ref, st_f, 0, g),
                                                             (xr_ref, yr_ref, wr_ref, sr_ref, st_r, 4, grp - 1 - g)):
                rows = pl.ds(pl.multiple_of(gg * c, c), c)
                for h in range(RET_HEADS):
                    pr, e = h // 2, h % 2
                    half = (lane < 64) if e == 0 else (lane >= 64)
                    s_ref[gg, h] = st[h].astype(s_ref.dtype)
                    xm = jnp.where(half, x_ref[rows, 128 * pr:128 * (pr + 1)].astype(F32) * w_ref[h], 0.0)
                    st[h] = st[h] * dec_ref[base + h:base + h + 1, :] + _tn(_mx(xm), _mx(y_ref[rows, 128 * h:128 * (h + 1)]))
            return carry

        lax.fori_loop(0, grp, chunk, 0, unroll=True)

    xs = lambda f: pl.BlockSpec((grp * c, 256), f)
    ys = lambda f: pl.BlockSpec((grp * c, 512), f)
    fwd, rev = (lambda n: (n, 0)), (lambda n: (ns - 1 - n, 0))
    wsp = pl.BlockSpec((RET_HEADS, c, 128), lambda n: (0, 0, 0))
    return pl.pallas_call(
        body, grid=(ns,), name=name,
        in_specs=[xs(fwd), ys(fwd), xs(rev), ys(rev), wsp, wsp, pl.BlockSpec((8, 128), lambda n: (0, 0))],
        out_specs=[pl.BlockSpec((grp, RET_HEADS, 128, 128), lambda n: (n, 0, 0, 0)),
                   pl.BlockSpec((grp, RET_HEADS, 128, 128), lambda n: (ns - 1 - n, 0, 0, 0))],
        out_shape=[jax.ShapeDtypeStruct((nc, RET_HEADS, 128, 128), MXU_DTYPE)] * 2,
        scratch_shapes=[pltpu.VMEM((RET_HEADS, 128, 128), F32), pltpu.VMEM((RET_HEADS, 128, 128), F32)],
        compiler_params=_cp("arbitrary"))(xk, yv, xk, yv, w_fwd, w_rev, dec)


def _ret_fwd(rq, rk, rv, rf, rb, tb):
    seq = rq.shape[0]
    c = RET_CHUNK
    grp = min(RET_GROUP, seq // c)

    def body(q_ref, k_ref, v_ref, rf_ref, rb_ref, d_ref, wqf_ref, wqb_ref, o_ref):
        lane = _lane((c, 128))

        def chunk(g, carry):
            rows = pl.ds(pl.multiple_of(g * c, c), c)
            for h in range(RET_HEADS):
                pr, e = h // 2, h % 2
                half = (lane < 64) if e == 0 else (lane >= 64)
                sl = slice(128 * pr, 128 * (pr + 1))
                qp, kp = q_ref[rows, sl], k_ref[rows, sl]
                km = jnp.where(half, kp, jnp.zeros_like(kp))
                sd = _mx(_nt(qp, km) * d_ref[h])
                qf = qp.astype(F32)
                o = _nn(sd, v_ref[rows, 128 * h:128 * (h + 1)])
                o = o + _nn(_mx(qf * wqf_ref[h]), rf_ref[g, h]) + _nn(_mx(qf * wqb_ref[h]), rb_ref[g, h])
                o_ref[rows, 128 * h:128 * (h + 1)] = o
            return carry

        lax.fori_loop(0, grp, chunk, 0, unroll=2)

    st = pl.BlockSpec((grp, RET_HEADS, 128, 128), lambda n: (n, 0, 0, 0))
    wsp = pl.BlockSpec((RET_HEADS, c, 128), lambda n: (0, 0, 0))
    return pl.pallas_call(
        body, grid=(seq // (grp * c),), name="ret_fwd",
        in_specs=[pl.BlockSpec((grp * c, 256), lambda n: (n, 0)), pl.BlockSpec((grp * c, 256), lambda n: (n, 0)),
                  pl.BlockSpec((grp * c, 512), lambda n: (n, 0)), st, st, pl.BlockSpec((RET_HEADS, c, c), lambda n: (0, 0, 0)), wsp, wsp],
        out_specs=pl.BlockSpec((grp * c, 512), lambda n: (n, 0)),
        out_shape=jax.ShapeDtypeStruct((seq, 512), F32), compiler_params=_cp("parallel"))(
            rq, rk, rv, rf, rb, tb["dmat"], tb["wqf"], tb["wqb"])


def _ret_bwd(rq, rk, rv, do, rf, rb, hf, hb, tb):
    seq = rq.shape[0]
    c = RET_CHUNK
    grp = min(RET_GROUP, seq // c)

    def body(q_ref, k_ref, v_ref, do_ref, rf_ref, rb_ref, hf_ref, hb_ref, d_ref, dda_ref, ddb_ref,
             wqf_ref, wqb_ref, wkf_ref, wkb_ref, cdec_ref, dq_ref, dk_ref, dv_ref, dlg_ref):
        @pl.when(pl.program_id(0) == 0)
        def _():
            dlg_ref[...] = jnp.zeros_like(dlg_ref)

        lane = _lane((c, 128))
        pos = lax.broadcasted_iota(jnp.int32, (c, 128), 0).astype(F32)

        def chunk(g, carry):
            rows = pl.ds(pl.multiple_of(g * c, c), c)
            for pr in range(2):
                sl = slice(128 * pr, 128 * (pr + 1))
                qp, kp = q_ref[rows, sl], k_ref[rows, sl]
                qf, kf = qp.astype(F32), kp.astype(F32)
                dq_acc = jnp.zeros((c, 128), F32)
                dk_acc = jnp.zeros((c, 128), F32)
                for e in range(2):
                    h = 2 * pr + e
                    half = (lane < 64) if e == 0 else (lane >= 64)
                    hs = slice(128 * h, 128 * (h + 1))
                    km = jnp.where(half, kp, jnp.zeros_like(kp))
                    kmf = km.astype(F32)
                    vh, doh = v_ref[rows, hs], do_ref[rows, hs]
                    rfh, rbh, hfh, hbh = rf_ref[g, h], rb_ref[g, h], hf_ref[g, h], hb_ref[g, h]
                    s = _nt(qp, km)
                    dpm = _nt(doh, vh)
                    ds_b = _mx(dpm * d_ref[h])
                    sd_b = _mx(s * d_ref[h])
                    dq_if = wqf_ref[h] * _nt(doh, rfh)
                    dq_ib = wqb_ref[h] * _nt(doh, rbh)
                    dq_acc = dq_acc + _nn(ds_b, km) + dq_if + dq_ib
                    dk_sf = wkf_ref[h] * _nt(vh, hfh)
                    dk_sb = wkb_ref[h] * _nt(vh, hbh)
                    dk_acc = dk_acc + jnp.where(half, _tn(ds_b, qp), 0.0) + dk_sf + dk_sb
                    dv = _tn(sd_b, doh) + _nn(_mx(kmf * wkf_ref[h]), hfh) + _nn(_mx(kmf * wkb_ref[h]), hbh)
                    dv_ref[rows, hs] = dv.astype(dv_ref.dtype)
                    sdp = s * dpm
                    hr_f = hfh.astype(F32) * rfh.astype(F32)
                    hr_b = hbh.astype(F32) * rbh.astype(F32)
                    da = (_rowsum128(sdp * dda_ref[h]) + _rowsum128((pos + 1.0) * qf * dq_if)
                          + _rowsum128((c - 1.0 - pos) * kf * dk_sf) + cdec_ref[h:h + 1, :] * _rowsum128(hr_f))
                    db = (_rowsum128(sdp * ddb_ref[h]) + _rowsum128((c - pos) * qf * dq_ib)
                          + _rowsum128(pos * kf * dk_sb) + cdec_ref[4 + h:5 + h, :] * _rowsum128(hr_b))
                    dlg_ref[h:h + 1, :] += da
                    dlg_ref[4 + h:5 + h, :] += db
                dq_ref[rows, sl] = dq_acc
                dk_ref[rows, sl] = dk_acc
            return carry

        lax.fori_loop(0, grp, chunk, 0, unroll=2)

    st = pl.BlockSpec((grp, RET_HEADS, 128, 128), lambda n: (n, 0, 0, 0))
    wsp = pl.BlockSpec((RET_HEADS, c, 128), lambda n: (0, 0, 0))
    dsp = pl.BlockSpec((RET_HEADS, c, c), lambda n: (0, 0, 0))
    x256 = pl.BlockSpec((grp * c, 256), lambda n: (n, 0))
    x512 = pl.BlockSpec((grp * c, 512), lambda n: (n, 0))
    cdec = tb["dec_fb"] * float(c)
    return pl.pallas_call(
        body, grid=(seq // (grp * c),), name="ret_bwd",
        in_specs=[x256, x256, x512, x512, st, st, st, st, dsp, dsp, dsp, wsp, wsp, wsp, wsp,
                  pl.BlockSpec((8, 128), lambda n: (0, 0))],
        out_specs=[x256, x256, x512, pl.BlockSpec((8, 128), lambda n: (0, 0))],
        out_shape=[jax.ShapeDtypeStruct((seq, 256), F32), jax.ShapeDtypeStruct((seq, 256), F32),
                   jax.ShapeDtypeStruct((seq, 512), MXU_DTYPE), jax.ShapeDtypeStruct((8, 128), F32)],
        compiler_params=_cp("arbitrary"))(
            rq, rk, rv, do, rf, rb, hf, hb, tb["dmat"], tb["dda"], tb["ddb"], tb["wqf"], tb["wqb"], tb["wkf"], tb["wkb"], cdec)


def _tail(x, tgt, o_att, o_ret, p, mod, g_post, gn_g, wpt, wout):
    seq = x.shape[0]
    bs = 256
    nb = seq // bs

    def body(x_ref, t_ref, oa_ref, or_ref, za_ref, zr_ref, gl_ref, mod_ref, gp_ref, gn_ref, wpt_ref, wout_ref,
             dout_ref, dza_ref, dzr_ref, dgl_ref, doa_ref, dor_ref, del_ref, gout_ref, gpt_ref,
             dgate_ref, dgpost_ref, dgn_ref, loss_ref, gout_acc, gpt_acc):
        i = pl.program_id(0)

        @pl.when(i == 0)
        def _():
            gout_acc[...] = jnp.zeros_like(gout_acc)
            gpt_acc[...] = jnp.zeros_like(gpt_acc)
            dgate_ref[...] = jnp.zeros_like(dgate_ref)
            dgpost_ref[...] = jnp.zeros_like(dgpost_ref)
            dgn_ref[...] = jnp.zeros_like(dgn_ref)
            loss_ref[...] = jnp.zeros_like(loss_ref)

        oa, za, zr = oa_ref[...], za_ref[...], zr_ref[...]
        sga, sgr = _sigmoid(za), _sigmoid(zr)
        sza, szr = za * sga, zr * sgr
        ya_b = _mx(oa * sza)
        ons, rstds = [], []
        for h in range(RET_HEADS):
            oh = or_ref[:, 128 * h:128 * (h + 1)]
            xc = oh - jnp.mean(oh, axis=-1, keepdims=True)
            rstd = lax.rsqrt(jnp.mean(xc * xc, axis=-1, keepdims=True) + EPS)
            ons.append(xc * rstd)
            rstds.append(rstd)
        on = jnp.concatenate(ons, axis=1)
        yn = on * gn_ref[...]
        yr_b = _mx(yn * szr)
        wpa_t, wpr_t = wpt_ref[:, 0:512], wpt_ref[:, 512:1024]
        a_att = _nt(ya_b, wpa_t)
        a_ret = _nt(yr_b, wpr_t)
        gts = _sigmoid(gl_ref[...])
        g_att, g_ret = gts[:, 0:D_MODEL], gts[:, D_MODEL:2 * D_MODEL]
        merged_b = _mx(g_att * a_att + g_ret * a_ret)
        u = _nn(merged_b, wout_ref[...])
        r = lax.rsqrt(jnp.mean(u * u, axis=-1, keepdims=True) + EPS)
        un = u * r
        gpost = gp_ref[...]
        y = un * gpost
        gate = mod_ref[:, 2 * D_MODEL:3 * D_MODEL]
        err = x_ref[...] + gate * y - t_ref[...]
        loss_ref[...] += (0.5 / D_MODEL) * _rowsum128(err * err)
        dout = err * (1.0 / D_MODEL)
        dout_ref[...] = dout
        dgate_ref[...] += jnp.sum(dout * y, axis=0, keepdims=True)
        dy = dout * gate
        dgpost_ref[...] += jnp.sum(dy * un, axis=0, keepdims=True)
        dun = dy * gpost
        du_b = _mx(r * (dun - un * jnp.mean(dun * un, axis=-1, keepdims=True)))
        dmerged = _nt(du_b, wout_ref[...])
        gout_acc[...] += _tn(merged_b, du_b)
        d_att, d_ret = dmerged * g_att, dmerged * g_ret
        dgl_ref[:, 0:D_MODEL] = (d_att * a_att * (1.0 - g_att)).astype(dgl_ref.dtype)
        dgl_ref[:, D_MODEL:2 * D_MODEL] = (d_ret * a_ret * (1.0 - g_ret)).astype(dgl_ref.dtype)
        d_att_b, d_ret_b = _mx(d_att), _mx(d_ret)
        dya = _nn(d_att_b, wpa_t)
        dyr = _nn(d_ret_b, wpr_t)
        gpt_acc[:, 0:512] += _tn(d_att_b, ya_b)
        gpt_acc[:, 512:1024] += _tn(d_ret_b, yr_b)
        dza_ref[...] = (dya * oa * (sga * (1.0 + za * (1.0 - sga)))).astype(dza_ref.dtype)
        doa = dya * sza
        doa_ref[...] = doa.astype(doa_ref.dtype)
        dt = jnp.transpose(doa * oa)
        del_ref[...] = jnp.sum(dt.reshape(8, HEAD_DIM, bs), axis=1)
        dzr_ref[...] = (dyr * yn * (sgr * (1.0 + zr * (1.0 - sgr)))).astype(dzr_ref.dtype)
        dyn = dyr * szr
        dgn_ref[...] += jnp.sum(dyn * on, axis=0, keepdims=True)
        don = dyn * gn_ref[...]
        for h in range(RET_HEADS):
            hs = slice(128 * h, 128 * (h + 1))
            dh, oh = don[:, hs], ons[h]
            doh = rstds[h] * (dh - jnp.mean(dh, axis=-1, keepdims=True) - oh * jnp.mean(dh * oh, axis=-1, keepdims=True))
            dor_ref[:, hs] = doh.astype(dor_ref.dtype)

        @pl.when(i == nb - 1)
        def _():
            gout_ref[...] = gout_acc[...].astype(gout_ref.dtype)
            gpt_ref[...] = gpt_acc[...].astype(gpt_ref.dtype)

    row = lambda w: pl.BlockSpec((bs, w), lambda i: (i, 0))
    el = lambda w, off: pl.BlockSpec((pl.Element(bs), pl.Element(w)), lambda i: (i * bs, off))
    vec = lambda w: pl.BlockSpec((1, w), lambda i: (0, 0))
    full = pl.BlockSpec((D_MODEL, D_MODEL), lambda i: (0, 0))
    sds = jax.ShapeDtypeStruct
    return pl.pallas_call(
        body, grid=(nb,), name="tail",
        in_specs=[row(D_MODEL), row(D_MODEL), row(512), row(512), el(512, ZA_B), el(512, ZR_B), el(2 * D_MODEL, GL_B),
                  vec(3 * D_MODEL), vec(D_MODEL), vec(512), full, full],
        out_specs=[row(D_MODEL), row(512), row(512), row(2 * D_MODEL), row(512), row(512),
                   pl.BlockSpec((8, bs), lambda i: (0, i)), full, full, vec(D_MODEL), vec(D_MODEL), vec(512), vec(128)],
        out_shape=[sds((seq, D_MODEL), F32), sds((seq, 512), MXU_DTYPE), sds((seq, 512), MXU_DTYPE),
                   sds((seq, 2 * D_MODEL), MXU_DTYPE), sds((seq, 512), MXU_DTYPE), sds((seq, 512), MXU_DTYPE),
                   sds((8, seq), F32), sds((D_MODEL, D_MODEL), MXU_DTYPE), sds((D_MODEL, D_MODEL), MXU_DTYPE),
                   sds((1, D_MODEL), F32), sds((1, D_MODEL), F32), sds((1, 512), F32), sds((1, 128), F32)],
        scratch_shapes=[pltpu.VMEM((D_MODEL, D_MODEL), F32), pltpu.VMEM((D_MODEL, D_MODEL), F32)],
        compiler_params=_cp("arbitrary"))(x, tgt, o_att, o_ret, p, p, p, mod, g_post, gn_g, wpt, wout)


def _assemble(p, cos, sin, qg2, kg2, dqt, dkp, dvp, dza, drq, drk, drv, dzr, dgl):
    seq = p.shape[0]
    bs = 512

    def body(p_ref, cos_ref, sin_ref, qg_ref, kg_ref, dqt_ref, dkp_ref, dvp_ref, dza_ref, drq_ref, drk_ref, drv_ref,
             dzr_ref, dgl_ref, dp_ref, dqg_ref, dkg_ref):
        @pl.when(pl.program_id(0) == 0)
        def _():
            dqg_ref[...] = jnp.zeros_like(dqg_ref)
            dkg_ref[...] = jnp.zeros_like(dkg_ref)

        m = _blockdiag64()
        cs, sn = cos_ref[...], sin_ref[...]
        lo = _lane((bs, 128)) < 64

        def norm_bwd(raw, dn, g, dg_ref):
            r = lax.rsqrt(_seg64(raw * raw, m) * (1.0 / 64) + EPS)
            xn = raw * r
            dg_ref[...] += jnp.sum(dn * xn, axis=0, keepdims=True)
            dxn = dn * g
            return r * (dxn - xn * (_seg64(dxn * xn, m) * (1.0 / 64)))

        for pr in range(4):
            sl = slice(128 * pr, 128 * (pr + 1))
            dn = _rope_t(dqt_ref[:, sl] * 0.125, cs, sn)
            dp_ref[:, QA + 128 * pr:QA + 128 * (pr + 1)] = norm_bwd(p_ref[:, sl], dn, qg_ref[...], dqg_ref).astype(dp_ref.dtype)
        fold = lambda a: a + pltpu.roll(a, 64, 1)
        dk = jnp.where(lo, fold(dkp_ref[0]), fold(dkp_ref[1]))
        dp_ref[:, KA:KA + 128] = norm_bwd(p_ref[:, KA:KA + 128], _rope_t(dk, cs, sn), kg_ref[...], dkg_ref).astype(dp_ref.dtype)
        dp_ref[:, VA:VA + 128] = jnp.where(lo, fold(dvp_ref[0]), fold(dvp_ref[1])).astype(dp_ref.dtype)
        dp_ref[:, ZA:ZA + 512] = dza_ref[...]
        for pr in range(2):
            sl = slice(128 * pr, 128 * (pr + 1))
            dp_ref[:, QR + 128 * pr:QR + 128 * (pr + 1)] = _rope_t(drq_ref[:, sl], cs, sn).astype(dp_ref.dtype)
            dp_ref[:, KR + 128 * pr:KR + 128 * (pr + 1)] = _rope_t(drk_ref[:, sl] * 0.125, cs, sn).astype(dp_ref.dtype)
        dp_ref[:, VR:VR + 512] = drv_ref[...]
        dp_ref[:, ZR:ZR + 512] = dzr_ref[...]
        dp_ref[:, GL:GL + 2 * D_MODEL] = dgl_ref[...]

    row = lambda w: pl.BlockSpec((bs, w), lambda i: (i, 0))
    gsp = pl.BlockSpec((1, 128), lambda i: (0, 0))
    dup = pl.BlockSpec((2, bs, 128), lambda i: (0, i, 0))
    return pl.pallas_call(
        body, grid=(seq // bs,), name="assemble_dp",
        in_specs=[row(768), row(128), row(128), gsp, gsp, row(512), dup, dup, row(512), row(256), row(256), row(512),
                  row(512), row(2 * D_MODEL)],
        out_specs=[row(IN_WIDTH), gsp, gsp],
        out_shape=[jax.ShapeDtypeStruct((seq, IN_WIDTH), MXU_DTYPE), jax.ShapeDtypeStruct((1, 128), F32),
                   jax.ShapeDtypeStruct((1, 128), F32)],
        compiler_params=_cp("arbitrary"))(p, cos, sin, qg2, kg2, dqt, dkp, dvp, dza, drq, drk, drv, dzr, dgl)


def _local_step(x, tgt, mod, g_pre, qn_g, kn_g, w_dec_f, w_dec_b, gn_g, g_post, w_attn, rest_start, rest_wait, send=None):
    send = send or (lambda name, arrays: None)
    seq = x.shape[0]
    cos, sin = _rope_tables(seq)
    qg2, kg2 = jnp.tile(qn_g, (1, 2)), jnp.tile(kn_g, (1, 2))
    lg_f = jax.nn.log_sigmoid(w_dec_f[0])
    lg_b = jax.nn.log_sigmoid(w_dec_b[0])
    tb = _ret_tables(lg_f, lg_b, RET_CHUNK)

    h, p_a, qt, kd, vd = _attn_inputs(x, mod, g_pre, w_attn, cos, sin, qg2, kg2)
    o_att, lse = _attn_fwd(qt, kd, vd, dep=rest_start(qt))
    win_t, wpt, wout = rest_wait(o_att)
    p_b, rq, rk, rv = _in_proj_rest(h, win_t, cos, sin)
    rf, rb = _ret_states(rk, rv, tb["wkf"], tb["wkb"], tb["dec_fb"], "ret_states_fwd")
    o_ret = _ret_fwd(rq, rk, rv, rf, rb, tb)
    (dout, dza, dzr, dgl, do_att, do_ret, delta, g_out, g_pt, dgate, dgpost, dgn, loss_l) = _tail(
        x, tgt, o_att, o_ret, p_b, mod, g_post, gn_g, wpt, wout)
    dep = send("early", (g_pt, g_out))
    dqt, dkp, dvp = _attn_bwd(qt, kd, vd, do_att, lse, delta, dep=dep)
    hb, hf = _ret_states(rq, do_ret, tb["wqb"], tb["wqf"], tb["dec_bf"], "ret_states_bwd")
    drq, drk, drv, dlg = _ret_bwd(rq, rk, rv, do_ret, rf, rb, hf, hb, tb)
    dp, dqg, dkg = _assemble(p_a, cos, sin, qg2, kg2, dqt, dkp, dvp, dza, drq, drk, drv, dzr, dgl)
    g_in_t = _matmul(dp, h, ta=True, tb=False, tm=256, tn=D_MODEL, out_dtype=MXU_DTYPE, name="in_proj_dw")
    dep = send("late", (g_in_t,))
    grad_x, dshift, dscale, dgpre = _in_proj_dx(x, dp, win_t, dout, mod, g_pre, dep=dep)

    dlg = jnp.sum(dlg, axis=1)
    small = dict(
        dmod=jnp.concatenate([dshift, dscale, dgate], axis=1),
        g_pre=dgpre, g_post=dgpost, gn_g=dgn,
        qn_g=dqg[:, :64] + dqg[:, 64:], kn_g=dkg[:, :64] + dkg[:, 64:],
        w_dec_f=(dlg[0:4] * jax.nn.sigmoid(-w_dec_f[0]))[None], w_dec_b=(dlg[4:8] * jax.nn.sigmoid(-w_dec_b[0]))[None],
        loss=jnp.sum(loss_l, axis=1, keepdims=True))
    return grad_x, g_in_t, g_pt, g_out, small


N_DEV = 8
N_CHIP = 4
HBM_SPEC = pl.BlockSpec(memory_space=pl.ANY)
VMEM_SPEC = pl.BlockSpec(memory_space=pltpu.VMEM)
SHARD_ROWS = (IN_WIDTH // N_CHIP, D_MODEL // N_CHIP, D_MODEL // N_CHIP)


def _coords():
    return lax.axis_index("x"), lax.axis_index("y"), lax.axis_index("c")


def _peer(k):
    x, y, c = _coords()
    return (1 - x if k & 4 else x, 1 - y if k & 2 else y, 1 - c if k & 1 else c)


def _dev_index(p):
    return 4 * p[0] + 2 * p[1] + p[2]


def _rows(ref, start, size):
    return ref.at[pl.ds(pl.multiple_of(start, 16), size), :]


def _remote(src, dst, ssem, rsem, dev):
    return pltpu.make_async_remote_copy(src_ref=src, dst_ref=dst, send_sem=ssem, recv_sem=rsem, device_id=dev,
                                        device_id_type=MESH)


ATTN_CHUNK = 32


def _mod_and_attn_rows(c, w_ada_s, b4, win_s):
    ncol = w_ada_s.shape[1]
    half = ATTN_COLS // 2
    offs = list(range(0, half, ATTN_CHUNK))
    nq = len(offs)
    rows = lambda ref, cc, off: ref.at[pl.ds(pl.multiple_of(cc * half + off, 16), ATTN_CHUNK), :]

    def body(c_ref, w_ref, b_ref, src, cs_ref, mod_ref, out, modsh, modall, ssem, rsem, lsem, asem, bsem, fsem, gsem, hsem):
        x, y, cc = _coords()
        me = _dev_index((x, y, cc))
        chip = me // 2
        sib = (x, y, 1 - cc)
        cs_ref[me] = c_ref[...]
        sends = []
        for k in range(1, N_DEV):
            cp = _remote(c_ref, cs_ref.at[me], ssem.at[k - 1], rsem.at[k - 1], _peer(k))
            cp.start()
            sends.append(cp)
        own = pltpu.make_async_copy(src.at[pl.ds(0, ATTN_COLS), :], out, lsem.at[0])
        bulk = [_remote(rows(src, cc, off), rows(out, cc, off), asem.at[(k2 - 1) * nq + q], bsem.at[q], (k2 // 2, k2 % 2, cc))
                for k2 in (1, 2) for q, off in enumerate(offs)]
        relay_chip = lambda q: 1 + q % 2

        @pl.when(chip == 0)
        def _():
            own.start()
            for cp in bulk:
                cp.start()

        for k in range(1, N_DEV):
            slot = cs_ref.at[_dev_index(_peer(k))]
            _remote(slot, slot, ssem.at[k - 1], rsem.at[k - 1], _peer(k)).wait_recv()
        cs = jnp.concatenate([cs_ref[d] for d in range(N_DEV)], axis=0)
        ms = _nn(cs * _sigmoid(cs), w_ref[...], precision=HIGHEST) + b_ref[pl.ds(chip, 1), :]
        modsh[...] = ms
        modall[chip] = ms
        for k2 in range(1, N_CHIP):
            cp = _remote(modsh, modall.at[chip], ssem.at[6 + k2], rsem.at[6 + k2], _peer(2 * k2))
            cp.start()
            sends.append(cp)

        @pl.when(chip != 0)
        def _():
            passed = []
            relays = [_remote(rows(out, cc, off), rows(out, cc, off), hsem.at[q], bsem.at[q], (1, 1, cc)) for q, off in enumerate(offs)]
            for q, off in enumerate(offs):
                got = rows(out, cc, off)
                _remote(got, got, asem.at[q], bsem.at[q], (0, 0, cc)).wait_recv()
                cp = _remote(got, got, fsem.at[q], gsem.at[q], sib)
                cp.start()
                passed.append(cp)
                pl.when(chip == relay_chip(q))(relays[q].start)
            for q, off in enumerate(offs):
                got = rows(out, 1 - cc, off)
                _remote(got, got, fsem.at[q], gsem.at[q], sib).wait_recv()
            for cp in passed:
                cp.wait_send()
            for q in range(nq):
                pl.when(chip == relay_chip(q))(relays[q].wait_send)

        for k2 in range(1, N_CHIP):
            slot = modall.at[_dev_index(_peer(2 * k2)) // 2]
            _remote(slot, slot, ssem.at[6 + k2], rsem.at[6 + k2], _peer(2 * k2)).wait_recv()
        for jj in range(N_CHIP):
            mod_ref[:, ncol * jj:ncol * (jj + 1)] = modall[jj, pl.ds(me, 1), :]
        for cp in sends:
            cp.wait_send()

        @pl.when(chip == 0)
        def _():
            for cp in bulk:
                cp.wait_send()
            own.wait()

    dma = pltpu.SemaphoreType.DMA
    return pl.pallas_call(
        body, name="mod_and_attn_rows", in_specs=[VMEM_SPEC] * 4, out_specs=[VMEM_SPEC, VMEM_SPEC, HBM_SPEC],
        out_shape=[jax.ShapeDtypeStruct((N_DEV, 1, D_MODEL), F32), jax.ShapeDtypeStruct((1, N_CHIP * ncol), F32),
                   jax.ShapeDtypeStruct((ATTN_COLS, win_s.shape[1]), win_s.dtype)],
        scratch_shapes=[pltpu.VMEM((N_DEV, ncol), F32), pltpu.VMEM((N_CHIP, N_DEV, ncol), F32), dma((10,)), dma((10,)),
                        dma((1,)), dma((2 * nq,)), dma((nq,)), dma((nq,)), dma((nq,)), dma((nq,))],
        compiler_params=_cp())(c, w_ada_s, b4, win_s)


GATHER_CHUNK = 32


def _chunks(kinds, chunk):
    out = []
    for t, kind in enumerate(kinds):
        half = SHARD_ROWS[kind] // 2
        step = chunk if half % chunk == 0 else half
        out += [(t, off, step) for off in range(0, half, step)]
    return out


SEM_SPEC = pl.BlockSpec(memory_space=pltpu.SEMAPHORE)
HBM_ONLY = pl.BlockSpec(memory_space=pltpu.HBM)
DATAFLOW = pltpu.SideEffectType.DATAFLOW_SIDE_EFFECTING


def _place_own(shards):
    nt = len(shards)

    def body(*refs):
        srcs, outs, lsem = refs[:nt], refs[nt:2 * nt], refs[2 * nt]
        x, y, _ = _coords()
        chip = 2 * x + y
        local = [pltpu.make_async_copy(srcs[t], _rows(outs[t], chip * SHARD_ROWS[t], SHARD_ROWS[t]), lsem.at[t]) for t in range(nt)]
        for cp in local:
            cp.start()
        for cp in local:
            cp.wait()

    return pl.pallas_call(
        body, name="place_own_shard", in_specs=[VMEM_SPEC] * nt, out_specs=[HBM_SPEC] * nt,
        out_shape=[jax.ShapeDtypeStruct((N_CHIP * SHARD_ROWS[t], s.shape[1]), s.dtype) for t, s in enumerate(shards)],
        scratch_shapes=[pltpu.SemaphoreType.DMA((nt,))], compiler_params=_cp())(*shards)


def _gather_rest_start(shards, zones, dep):
    n = len(shards)

    def body(*refs):
        srcs, lands = refs[:n], refs[n:2 * n]
        ssem, rsem = refs[2 * n + 1], refs[2 * n + 2]
        token = refs[-1]
        x, y, _ = _coords()
        chip = 2 * x + y
        for k2 in range(1, N_CHIP):
            for t in range(n):
                q = (k2 - 1) * n + t
                _remote(srcs[t], _rows(lands[t], chip * SHARD_ROWS[t], SHARD_ROWS[t]), ssem.at[q], rsem.at[q], _peer(2 * k2)).start()
        token[...] = jnp.zeros_like(token)

    hbm = lambda a: pltpu.with_memory_space_constraint(a, pltpu.HBM)
    dma = pltpu.SemaphoreType.DMA
    outs = pl.pallas_call(
        body, name="gather_rest", in_specs=[HBM_ONLY] * (2 * n) + [HBM_SPEC],
        out_specs=[SEM_SPEC, SEM_SPEC] + [HBM_ONLY] * (2 * n) + [VMEM_SPEC],
        out_shape=[dma((3 * n,)), dma((3 * n,))] + [pltpu.HBM(a.shape, a.dtype) for a in list(shards) + list(zones)]
        + [jax.ShapeDtypeStruct((8, 128), F32)],
        input_output_aliases={i: 2 + i for i in range(2 * n)},
        compiler_params=pltpu.CompilerParams(has_side_effects=DATAFLOW))(*[hbm(a) for a in list(shards) + list(zones)], dep)
    return outs[0], outs[1], outs[2:2 + n], outs[2 + n:2 + 2 * n], outs[-1]


def _gather_rest_wait(started, after):
    ssem, rsem, shards, zones, _ = started
    n = len(shards)

    def body(*refs):
        srcs, lands = refs[:n], refs[n:2 * n]
        ssem_ref, rsem_ref = refs[2 * n], refs[2 * n + 1]
        for k2 in range(1, N_CHIP):
            pchip = _dev_index(_peer(2 * k2)) // 2
            for t in range(n):
                q = (k2 - 1) * n + t
                cp = _remote(srcs[t], _rows(lands[t], pchip * SHARD_ROWS[t], SHARD_ROWS[t]), ssem_ref.at[q], rsem_ref.at[q], _peer(2 * k2))
                cp.wait_send()
                cp.wait_recv()

    outs = pl.pallas_call(
        body, name="gather_rest_wait", in_specs=[HBM_ONLY] * (2 * n) + [SEM_SPEC, SEM_SPEC, HBM_SPEC], out_specs=[HBM_ONLY] * (2 * n),
        out_shape=[pltpu.HBM(a.shape, a.dtype) for a in list(shards) + list(zones)],
        input_output_aliases={i: i for i in range(2 * n)},
        compiler_params=pltpu.CompilerParams(has_side_effects=DATAFLOW))(*shards, *zones, ssem, rsem, after)
    return outs[n:]


def _reduced_by(ref, t, dev):
    return _rows(ref, (dev // 2) * SHARD_ROWS[t] + (dev % 2) * (SHARD_ROWS[t] // 2), SHARD_ROWS[t] // 2)


def _scatter_start(grads, kinds, name):
    n = len(grads)

    def body(*refs):
        srcs, lands = refs[:n], refs[n:2 * n]
        ssem, rsem = refs[2 * n], refs[2 * n + 1]
        token = refs[-1]
        me = _dev_index(_coords())
        for k in range(1, N_DEV):
            for i, t in enumerate(kinds):
                q = (k - 1) * n + i
                _remote(_reduced_by(srcs[i], t, _dev_index(_peer(k))), lands[i].at[me], ssem.at[q], rsem.at[q], _peer(k)).start()
        token[...] = jnp.zeros_like(token)

    zones = [lax.empty((N_DEV, SHARD_ROWS[t] // 2, g.shape[1]), g.dtype) for g, t in zip(grads, kinds)]
    hbm = lambda a: pltpu.with_memory_space_constraint(a, pltpu.HBM)
    dma = pltpu.SemaphoreType.DMA
    outs = pl.pallas_call(
        body, name=name, in_specs=[HBM_ONLY] * (2 * n), out_specs=[SEM_SPEC, SEM_SPEC] + [HBM_ONLY] * (2 * n) + [VMEM_SPEC],
        out_shape=[dma((7 * n,)), dma((7 * n,))] + [pltpu.HBM(a.shape, a.dtype) for a in list(grads) + zones]
        + [jax.ShapeDtypeStruct((8, 128), F32)],
        input_output_aliases={i: 2 + i for i in range(2 * n)},
        compiler_params=pltpu.CompilerParams(has_side_effects=DATAFLOW))(*[hbm(a) for a in list(grads) + zones])
    return outs[0], outs[1], outs[2:2 + n], outs[2 + n:2 + 2 * n], outs[-1]


def _scatter_wait(started, kinds, after, name):
    ssem, rsem, grads, zones, _ = started
    n = len(grads)

    def body(*refs):
        srcs, lands = refs[:n], refs[n:2 * n]
        ssem_ref, rsem_ref = refs[2 * n], refs[2 * n + 1]
        for k in range(1, N_DEV):
            peer = _dev_index(_peer(k))
            for i, t in enumerate(kinds):
                q = (k - 1) * n + i
                cp = _remote(_reduced_by(srcs[i], t, peer), lands[i].at[peer], ssem_ref.at[q], rsem_ref.at[q], _peer(k))
                cp.wait_send()
                cp.wait_recv()

    outs = pl.pallas_call(
        body, name=name, in_specs=[HBM_ONLY] * (2 * n) + [SEM_SPEC, SEM_SPEC, HBM_SPEC], out_specs=[HBM_ONLY] * (2 * n),
        out_shape=[pltpu.HBM(a.shape, a.dtype) for a in list(grads) + list(zones)],
        input_output_aliases={i: i for i in range(2 * n)},
        compiler_params=pltpu.CompilerParams(has_side_effects=DATAFLOW))(*grads, *zones, ssem, rsem, after)
    return outs[:n], outs[n:]


def _grad_finish(grads, zones, kinds, name):
    nt = len(grads)
    pieces = _chunks(kinds, GATHER_CHUNK)
    npc = len(pieces)
    shard = [SHARD_ROWS[k] for k in kinds]

    def body(*refs):
        srcs, lands, outs = refs[:nt], refs[nt:2 * nt], refs[2 * nt:3 * nt]
        slots, sums = refs[3 * nt:4 * nt], refs[4 * nt:5 * nt]
        lsem, osem, xsem, ysem = refs[5 * nt:]
        x, y, c = _coords()
        me = _dev_index((x, y, c))
        sib = (x, y, 1 - c)
        loads = [pltpu.make_async_copy(_reduced_by(srcs[t], kinds[t], me), slots[t].at[me], lsem.at[t]) for t in range(nt)]
        for k in range(1, N_DEV):
            peer = _dev_index(_peer(k))
            loads += [pltpu.make_async_copy(lands[t].at[peer], slots[t].at[peer], lsem.at[k * nt + t]) for t in range(nt)]
        for cp in loads:
            cp.start()
        for cp in loads:
            cp.wait()
        sends = []
        place = lambda t, cc, off, n: _rows(outs[t], cc * (shard[t] // 2) + off, n)
        stores = []
        for q, (t, off, n) in enumerate(pieces):
            r = pl.ds(off, n)
            acc = slots[t][0, r, :].astype(F32)
            for d in range(1, N_DEV):
                acc = acc + slots[t][d, r, :].astype(F32)
            sums[t][r, :] = acc
            keep = pltpu.make_async_copy(sums[t].at[r, :], place(t, c, off, n), osem.at[q])
            give = _remote(sums[t].at[r, :], place(t, c, off, n), xsem.at[q], ysem.at[q], sib)
            keep.start()
            give.start()
            stores.append(keep)
            sends.append(give)
        for q, (t, off, n) in enumerate(pieces):
            got = place(t, 1 - c, off, n)
            _remote(got, got, xsem.at[q], ysem.at[q], sib).wait_recv()
        for cp in sends:
            cp.wait_send()
        for cp in stores:
            cp.wait()

    dma = pltpu.SemaphoreType.DMA
    return pl.pallas_call(
        body, name=name, in_specs=[HBM_SPEC] * (2 * nt), out_specs=[HBM_SPEC] * nt,
        out_shape=[jax.ShapeDtypeStruct((shard[t], g.shape[1]), F32) for t, g in enumerate(grads)],
        scratch_shapes=([pltpu.VMEM((N_DEV, shard[t] // 2, g.shape[1]), g.dtype) for t, g in enumerate(grads)]
                        + [pltpu.VMEM((shard[t] // 2, g.shape[1]), F32) for t, g in enumerate(grads)]
                        + [dma((N_DEV * nt,)), dma((npc,)), dma((npc,)), dma((npc,))]),
        compiler_params=_cp())(*grads, *zones)


def _adam_math(w, g, m, v):
    m = ADAM_B1 * m + (1.0 - ADAM_B1) * g
    v = ADAM_B2 * v + (1.0 - ADAM_B2) * (g * g)
    m_hat = m / (1.0 - ADAM_B1 ** ADAM_STEP)
    v_hat = v / (1.0 - ADAM_B2 ** ADAM_STEP)
    return -ADAM_LR * (m_hat / (jnp.sqrt(v_hat) + ADAM_EPS) + ADAM_WD * w), m, v


def _adamw(w, g, m, v, rb, name):
    rows, cols = w.shape

    def body(w_ref, g_ref, m_ref, v_ref, d_ref, nm_ref, nv_ref):
        d_ref[...], nm_ref[...], nv_ref[...] = _adam_math(w_ref[...], g_ref[...], m_ref[...], v_ref[...])

    spec = pl.BlockSpec((rb, cols), lambda i: (i, 0))
    return pl.pallas_call(body, grid=(rows // rb,), name=name, in_specs=[spec] * 4, out_specs=[spec] * 3,
                          out_shape=[jax.ShapeDtypeStruct(w.shape, F32)] * 3, compiler_params=_cp("parallel"))(w, g, m, v)


PACK = (("b_ada", 3 * D_MODEL), ("g_pre", D_MODEL), ("g_post", D_MODEL), ("gn_g", 512), ("qn_g", 64), ("kn_g", 64),
        ("w_dec_f", 4), ("w_dec_b", 4), ("loss", 1))
PACK_OFFSETS = {}
PACK_WIDTH = 0
for _name, _n in PACK:
    PACK_OFFSETS[_name] = PACK_WIDTH
    PACK_WIDTH += -(-_n // 128) * 128
SMALL_PARAMS = tuple(name for name, _ in PACK if name != "loss")


def _pack(parts):
    cols = []
    for name, n in PACK:
        pad = -(-n // 128) * 128 - n
        cols.append(parts[name].reshape(1, n).astype(F32))
        if pad:
            cols.append(jnp.zeros((1, pad), F32))
    return jnp.concatenate(cols, axis=1)


def _small_reduce(vec, cs, deps):
    ncol = 3 * D_MODEL // N_CHIP

    def body(vec_ref, cs_ref, *rest):
        tot_ref, gwa_ref, gat, ssem, rsem = rest[-5:]
        me = _dev_index(_coords())
        chip = me // 2
        gat[me] = vec_ref[...]
        sends = []
        for k in range(1, N_DEV):
            cp = _remote(vec_ref, gat.at[me], ssem.at[k - 1], rsem.at[k - 1], _peer(k))
            cp.start()
            sends.append(cp)
        for k in range(1, N_DEV):
            slot = gat.at[_dev_index(_peer(k))]
            _remote(slot, slot, ssem.at[k - 1], rsem.at[k - 1], _peer(k)).wait_recv()
        rows = [gat[d] for d in range(N_DEV)]
        tot = rows[0]
        for d in range(1, N_DEV):
            tot = tot + rows[d]
        tot_ref[...] = tot
        cs = cs_ref[...]
        ca_t = jnp.transpose(jnp.concatenate([cs * _sigmoid(cs), jnp.zeros((128 - N_DEV, D_MODEL), F32)], axis=0))
        dm = jnp.concatenate(rows + [jnp.zeros((128 - N_DEV, PACK_WIDTH), F32)], axis=0)
        for jj in range(N_CHIP):
            @pl.when(chip == jj)
            def _():
                gwa_ref[...] = _nn(ca_t, dm[:, ncol * jj:ncol * (jj + 1)], precision=HIGHEST)
        for cp in sends:
            cp.wait_send()

    return pl.pallas_call(
        body, name="small_reduce", in_specs=[VMEM_SPEC, VMEM_SPEC] + [HBM_SPEC] * len(deps), out_specs=[VMEM_SPEC] * 2,
        out_shape=[jax.ShapeDtypeStruct((1, PACK_WIDTH), F32), jax.ShapeDtypeStruct((D_MODEL, ncol), F32)],
        scratch_shapes=[pltpu.VMEM((N_DEV, 1, PACK_WIDTH), F32), pltpu.SemaphoreType.DMA((7,)), pltpu.SemaphoreType.DMA((7,))],
        compiler_params=_cp())(vec, cs, *deps)


def _small_adamw(tot, given):
    sizes = dict(PACK)
    np_ = len(SMALL_PARAMS)

    def body(*refs):
        tot_ref = refs[0]
        wmv = refs[1:1 + 3 * np_]
        outs = refs[1 + 3 * np_:1 + 7 * np_]
        loss_ref = refs[-1]
        for i, name in enumerate(SMALL_PARAMS):
            off, n = PACK_OFFSETS[name], sizes[name]
            g = tot_ref[:, off:off + n]
            w_ref, m_ref, v_ref = wmv[3 * i:3 * i + 3]
            outs[i][...] = g
            outs[np_ + i][...], outs[2 * np_ + i][...], outs[3 * np_ + i][...] = _adam_math(w_ref[...], g, m_ref[...], v_ref[...])
        loss_ref[...] = tot_ref[:, PACK_OFFSETS["loss"]:PACK_OFFSETS["loss"] + 1]

    flat = [a for name in SMALL_PARAMS for a in given[name]]
    shapes = [jax.ShapeDtypeStruct((1, sizes[name]), F32) for name in SMALL_PARAMS]
    res = pl.pallas_call(body, name="small_adamw", in_specs=[VMEM_SPEC] * (1 + 3 * np_), out_specs=[VMEM_SPEC] * (4 * np_ + 1),
                         out_shape=shapes * 4 + [jax.ShapeDtypeStruct((1, 1), F32)], compiler_params=_cp())(tot, *flat)
    return [dict(zip(SMALL_PARAMS, res[k * np_:(k + 1) * np_])) for k in range(4)], res[-1]


def kernel(x, c, w_ada, b_ada, g_pre, w_in, qn_g, kn_g, w_dec_f, w_dec_b, gn_g, w_pa, w_pr, w_out, g_post, loss_target, m_w_ada, m_b_ada, m_g_pre, m_w_in, m_qn_g, m_kn_g, m_w_dec_f, m_w_dec_b, m_gn_g, m_w_pa, m_w_pr, m_w_out, m_g_post, v_w_ada, v_b_ada, v_g_pre, v_w_in, v_qn_g, v_kn_g, v_w_dec_f, v_w_dec_b, v_gn_g, v_w_pa, v_w_pr, v_w_out, v_g_post):
    shards = (w_in[0].T.astype(MXU_DTYPE), jnp.concatenate([w_pa[0].T, w_pr[0].T], axis=1).astype(MXU_DTYPE),
              w_out[0].astype(MXU_DTYPE))
    cs, mod, w_attn = _mod_and_attn_rows(c, w_ada[0], b_ada.reshape(N_CHIP, 3 * D_MODEL // N_CHIP), shards[0])
    started = {}

    def rest_start(dep):
        started["rest"] = _gather_rest_start(shards, _place_own(shards), dep)
        return started["rest"][-1]

    def rest_wait(after):
        return _gather_rest_wait(started["rest"], after)

    kinds = {"early": (1, 2), "late": (0,)}

    def send(name, arrays):
        started[name] = _scatter_start(arrays, kinds[name], "grad_scatter_" + name)
        return started[name][-1]

    grad_x, _, _, _, small = _local_step(x[0], loss_target[0], mod, g_pre, qn_g, kn_g, w_dec_f, w_dec_b,
                                         gn_g, g_post, w_attn, rest_start, rest_wait, send=send)
    small = dict(small)
    small["b_ada"] = small.pop("dmod")
    given = dict(b_ada=(b_ada, m_b_ada, v_b_ada), g_pre=(g_pre, m_g_pre, v_g_pre), g_post=(g_post, m_g_post, v_g_post),
                 gn_g=(gn_g, m_gn_g, v_gn_g), qn_g=(qn_g, m_qn_g, v_qn_g), kn_g=(kn_g, m_kn_g, v_kn_g),
                 w_dec_f=(w_dec_f, m_w_dec_f, v_w_dec_f), w_dec_b=(w_dec_b, m_w_dec_b, v_w_dec_b))
    grads, deltas, new_m, new_v = {}, {}, {}, {}

    def update(name, w, g, m, v, back):
        d, nm, nv = _adamw(w, g, m, v, w.shape[0] // 4, "adamw_" + name)
        grads[name], deltas[name], new_m[name], new_v[name] = back(g), back(d), back(nm), back(nv)
        return d

    lead = lambda a: a[None]
    (g_pt, g_out), (z_pt, z_out) = _scatter_wait(started["early"], kinds["early"], grad_x, "grad_scatter_early_wait")
    r_pt, r_out = _grad_finish((g_pt, g_out), (z_pt, z_out), kinds["early"], "grad_finish_early")
    early = (update("w_pa", w_pa[0], r_pt[:, :512].T, m_w_pa[0], v_w_pa[0], lead),
             update("w_pr", w_pr[0], r_pt[:, 512:].T, m_w_pr[0], v_w_pr[0], lead),
             update("w_out", w_out[0], r_out, m_w_out[0], v_w_out[0], lead))
    tot, g_w_ada = _small_reduce(_pack(small), cs.reshape(N_DEV, D_MODEL), early)
    small_parts, loss = _small_adamw(tot, given)
    for dst, part in zip((grads, deltas, new_m, new_v), small_parts):
        dst.update(part)
    last = update("w_ada", w_ada[0], g_w_ada, m_w_ada[0], v_w_ada[0], lead)

    (g_in_t,), (z_in,) = _scatter_wait(started["late"], kinds["late"], last, "grad_scatter_late_wait")
    (r_in,) = _grad_finish((g_in_t,), (z_in,), kinds["late"], "grad_finish_late")
    tr = lambda a: a[0].T
    update("w_in", tr(w_in), r_in, tr(m_w_in), tr(v_w_in), lambda a: a.T[None])
    order = ("w_ada", "b_ada", "g_pre", "w_in", "qn_g", "kn_g", "w_dec_f", "w_dec_b", "gn_g", "w_pa", "w_pr", "w_out", "g_post")
    return (loss[0, 0], grad_x[None], *[grads[n] for n in order], *[deltas[n] for n in order],
            *[new_m[n] for n in order], *[new_v[n] for n in order])
```

```python
import jax
import jax.numpy as jnp
import numpy as np
from jax import lax
from jax.experimental import pallas as pl
from jax.experimental.pallas import tpu as pltpu

F32 = jnp.float32
BF16 = jnp.bfloat16
MXU_DTYPE = jnp.bfloat16

D_MODEL = 1024
GRID_W = 64
HEAD_DIM = 64
ROPE_THETA = 10000.0
EPS = 1e-6
RET_HEADS = 4
RET_CHUNK = 256
RET_GROUP = 4
IN_WIDTH = 4864
QA, KA, VA, ZA, QR, KR, VR, ZR, GL = 0, 512, 640, 768, 1280, 1536, 1792, 2304, 2816
ATTN_COLS = ZA
ZA_B, QR_B, KR_B, VR_B, ZR_B, GL_B = (c - ATTN_COLS for c in (ZA, QR, KR, VR, ZR, GL))

ADAM_LR, ADAM_B1, ADAM_B2, ADAM_EPS, ADAM_WD, ADAM_STEP = 0.001, 0.9, 0.999, 1e-08, 0.01, 10

VMEM_LIMIT = 56 * 1024 * 1024
MESH = pl.DeviceIdType.MESH
HIGHEST = lax.Precision.HIGHEST
LOG2E = 1.4426950408889634
LN2 = 0.6931471805599453


def _cp(*sem, **kw):
    if sem:
        kw["dimension_semantics"] = sem
    return pltpu.CompilerParams(vmem_limit_bytes=VMEM_LIMIT, **kw)


def _nn(a, b, **kw):
    return lax.dot_general(a, b, (((1,), (0,)), ((), ())), preferred_element_type=F32, **kw)


def _nt(a, b):
    return lax.dot_general(a, b, (((1,), (1,)), ((), ())), preferred_element_type=F32)


def _tn(a, b):
    return lax.dot_general(a, b, (((0,), (0,)), ((), ())), preferred_element_type=F32)


def _mx(x):
    return x.astype(MXU_DTYPE)


def _lane(shape):
    return lax.broadcasted_iota(jnp.int32, shape, 1)


def _sigmoid(z):
    return 1.0 / (1.0 + jnp.exp(-z))


def _rowsum128(x):
    s = jnp.sum(x, axis=0, keepdims=True)
    out = s[:, 0:128]
    for k in range(1, x.shape[1] // 128):
        out = out + s[:, 128 * k:128 * (k + 1)]
    return out


def _swap16(x):
    return jnp.where((_lane(x.shape) & 16) == 0, pltpu.roll(x, 112, 1), pltpu.roll(x, 16, 1))


def _rope(x, cos, sin):
    return x * cos + _swap16(x) * sin


def _rope_t(d, cos, sin):
    return d * cos + _swap16(d * sin)


def _blockdiag64():
    r = lax.broadcasted_iota(jnp.int32, (128, 128), 0) // 64
    c = lax.broadcasted_iota(jnp.int32, (128, 128), 1) // 64
    return (r == c).astype(BF16)


def _seg64(x, m):
    hi = x.astype(BF16)
    lo = (x - hi.astype(F32)).astype(BF16)
    return _nn(hi, m) + _nn(lo, m)


def _rope_tables(seq):
    t = np.arange(seq)
    row = (t // GRID_W).astype(np.float32)
    col = (t % GRID_W).astype(np.float32)
    half = HEAD_DIM // 2
    inv_freq = (np.float32(ROPE_THETA) ** (-np.arange(0, half, 2, dtype=np.float32) / np.float32(half))).astype(np.float32)
    ar = (row[:, None] * inv_freq[None, :]).astype(np.float32).astype(np.float64)
    ac = (col[:, None] * inv_freq[None, :]).astype(np.float32).astype(np.float64)
    cr, sr, cc, sc = np.cos(ar), np.sin(ar), np.cos(ac), np.sin(ac)
    cos64 = np.concatenate([cr, cr, cc, cc], axis=1)
    sin64 = np.concatenate([-sr, sr, -sc, sc], axis=1)
    return jnp.asarray(np.tile(cos64, (1, 2)), F32), jnp.asarray(np.tile(sin64, (1, 2)), F32)


def _attn_inputs(x, mod, g_pre, w_attn, cos, sin, qg2, kg2):
    seq = x.shape[0]
    bs = 512

    def body(x_ref, mod_ref, g_ref, w_ref, cos_ref, sin_ref, qg_ref, kg_ref, h_ref, pa_ref, qt_ref, kd_ref, vd_ref):
        xv = x_ref[...]
        r = lax.rsqrt(jnp.mean(xv * xv, axis=-1, keepdims=True) + EPS)
        shift = mod_ref[:, 0:D_MODEL]
        scale = mod_ref[:, D_MODEL:2 * D_MODEL]
        hb = ((xv * r) * g_ref[...] * (1.0 + scale) + shift).astype(h_ref.dtype)
        h_ref[...] = hb
        p = _nt(hb, w_ref[...])
        pa_ref[...] = p
        m = _blockdiag64()
        cs, sn = cos_ref[...], sin_ref[...]
        lo = _lane((bs, 128)) < 64
        for pr in range(4):
            q = p[:, QA + 128 * pr:QA + 128 * (pr + 1)]
            r = lax.rsqrt(_seg64(q * q, m) * (1.0 / 64) + EPS)
            qt_ref[:, 128 * pr:128 * (pr + 1)] = (_rope(q * r * qg_ref[...], cs, sn) * (0.125 * LOG2E)).astype(qt_ref.dtype)
        k = p[:, KA:KA + 128]
        r = lax.rsqrt(_seg64(k * k, m) * (1.0 / 64) + EPS)
        kr = _rope(k * r * kg_ref[...], cs, sn)
        ksw = pltpu.roll(kr, 64, 1)
        kd_ref[0] = jnp.where(lo, kr, ksw).astype(kd_ref.dtype)
        kd_ref[1] = jnp.where(lo, ksw, kr).astype(kd_ref.dtype)
        v = p[:, VA:VA + 128]
        vsw = pltpu.roll(v, 64, 1)
        vd_ref[0] = jnp.where(lo, v, vsw).astype(vd_ref.dtype)
        vd_ref[1] = jnp.where(lo, vsw, v).astype(vd_ref.dtype)

    row = lambda w: pl.BlockSpec((bs, w), lambda i: (i, 0))
    fix = lambda a, b: pl.BlockSpec((a, b), lambda i: (0, 0))
    dup = pl.BlockSpec((2, bs, 128), lambda i: (0, i, 0))
    sds = jax.ShapeDtypeStruct
    return pl.pallas_call(
        body, grid=(seq // bs,), name="attn_inputs",
        in_specs=[row(D_MODEL), fix(1, 3 * D_MODEL), fix(1, D_MODEL), fix(ATTN_COLS, D_MODEL), row(128), row(128), fix(1, 128), fix(1, 128)],
        out_specs=[row(D_MODEL), row(ATTN_COLS), row(512), dup, dup],
        out_shape=[sds((seq, D_MODEL), MXU_DTYPE), sds((seq, ATTN_COLS), F32), sds((seq, 512), MXU_DTYPE),
                   sds((2, seq, 128), MXU_DTYPE), sds((2, seq, 128), MXU_DTYPE)],
        compiler_params=_cp("parallel"))(x, mod, g_pre, w_attn, cos, sin, qg2, kg2)


def _in_proj_dx(x, dp, win_t, dout, mod, g_pre, dep=None):
    seq = x.shape[0]
    bs = 512
    deps, dep_specs = _dep_args(dep, 1)

    def body(x_ref, dp_ref, w_ref, dout_ref, mod_ref, g_ref, *rest):
        gx_ref, dshift_ref, dscale_ref, dg_ref = rest[-4:]

        @pl.when(pl.program_id(0) == 0)
        def _():
            dshift_ref[...] = jnp.zeros_like(dshift_ref)
            dscale_ref[...] = jnp.zeros_like(dscale_ref)
            dg_ref[...] = jnp.zeros_like(dg_ref)

        xv = x_ref[...]
        dh = _nn(dp_ref[...], w_ref[...])
        g = g_ref[...]
        r = lax.rsqrt(jnp.mean(xv * xv, axis=-1, keepdims=True) + EPS)
        xn = xv * r
        scale = mod_ref[:, D_MODEL:2 * D_MODEL]
        dshift_ref[...] += jnp.sum(dh, axis=0, keepdims=True)
        dscale_ref[...] += jnp.sum(dh * (xn * g), axis=0, keepdims=True)
        da = dh * (1.0 + scale)
        dg_ref[...] += jnp.sum(da * xn, axis=0, keepdims=True)
        dxn = da * g
        dx = r * (dxn - xn * jnp.mean(dxn * xn, axis=-1, keepdims=True))
        gx_ref[...] = dout_ref[...] + dx

    row = pl.BlockSpec((bs, D_MODEL), lambda i: (i, 0))
    vec = pl.BlockSpec((1, D_MODEL), lambda i: (0, 0))
    return pl.pallas_call(
        body, grid=(seq // bs,), name="in_proj_dx",
        in_specs=[row, pl.BlockSpec((bs, IN_WIDTH), lambda i: (i, 0)), pl.BlockSpec((IN_WIDTH, D_MODEL), lambda i: (0, 0)), row,
                  pl.BlockSpec((1, 3 * D_MODEL), lambda i: (0, 0)), vec] + dep_specs,
        out_specs=[row, vec, vec, vec],
        out_shape=[jax.ShapeDtypeStruct((seq, D_MODEL), F32)] + [jax.ShapeDtypeStruct((1, D_MODEL), F32)] * 3,
        compiler_params=_cp("arbitrary"))(x, dp, win_t, dout, mod, g_pre, *deps)


def _dep_args(dep, grid_rank):
    if dep is None:
        return [], []
    return [dep], [pl.BlockSpec(dep.shape, lambda *_: (0,) * dep.ndim)]


def _matmul(a, b, *, ta, tb, tm, tn, out_dtype, name, dep=None):
    kdim = a.shape[0] if ta else a.shape[1]
    m = a.shape[1] if ta else a.shape[0]
    n = b.shape[0] if tb else b.shape[1]
    assert m % tm == 0 and n % tn == 0
    deps, dep_specs = _dep_args(dep, 2)

    def body(a_ref, b_ref, *rest):
        o_ref = rest[-1]
        dims = (((0 if ta else 1,), (1 if tb else 0,)), ((), ()))
        o_ref[...] = lax.dot_general(a_ref[...], b_ref[...], dims, preferred_element_type=F32).astype(o_ref.dtype)

    a_spec = pl.BlockSpec((kdim, tm), lambda j, i: (0, i)) if ta else pl.BlockSpec((tm, kdim), lambda j, i: (i, 0))
    b_spec = pl.BlockSpec((tn, kdim), lambda j, i: (j, 0)) if tb else pl.BlockSpec((kdim, tn), lambda j, i: (0, j))
    return pl.pallas_call(
        body, grid=(n // tn, m // tm), name=name, in_specs=[a_spec, b_spec] + dep_specs,
        out_specs=pl.BlockSpec((tm, tn), lambda j, i: (i, j)),
        out_shape=jax.ShapeDtypeStruct((m, n), out_dtype), compiler_params=_cp("parallel", "parallel"))(a, b, *deps)


def _in_proj_rest(h, win_t, cos, sin):
    seq = h.shape[0]
    tm = min(1024, seq)
    ncols = IN_WIDTH - ATTN_COLS
    tn = ncols // 2
    assert ZR_B <= tn and ATTN_COLS % 128 == 0 and tn % 128 == 0

    def body(h_ref, w_ref, cos_ref, sin_ref, p_ref, rq_ref, rk_ref, rv_ref):
        p = _nt(h_ref[...], w_ref[...])
        p_ref[...] = p

        @pl.when(pl.program_id(0) == 0)
        def _():
            cs, sn = cos_ref[...], sin_ref[...]
            for pr in range(2):
                sl = slice(128 * pr, 128 * (pr + 1))
                rq_ref[:, sl] = _rope(p[:, QR_B + 128 * pr:QR_B + 128 * (pr + 1)], cs, sn).astype(rq_ref.dtype)
                rk_ref[:, sl] = (_rope(p[:, KR_B + 128 * pr:KR_B + 128 * (pr + 1)], cs, sn) * 0.125).astype(rk_ref.dtype)
            rv_ref[...] = p[:, VR_B:VR_B + 512].astype(rv_ref.dtype)

    nrow = seq // tm
    row = lambda w: pl.BlockSpec((tm, w), lambda j, i: (i, 0))
    park = lambda w: pl.BlockSpec((tm, w), lambda j, i: (jnp.where(j == 0, i, nrow - 1), 0))
    sds = jax.ShapeDtypeStruct
    return pl.pallas_call(
        body, grid=(2, nrow), name="in_proj_rest",
        in_specs=[row(D_MODEL), pl.BlockSpec((pl.Element(tn), pl.Element(D_MODEL)),
                                             lambda j, i: (pl.multiple_of(ATTN_COLS + j * tn, 128), 0)), row(128), row(128)],
        out_specs=[pl.BlockSpec((tm, tn), lambda j, i: (i, j)), park(256), park(256), park(512)],
        out_shape=[sds((seq, ncols), F32), sds((seq, 256), MXU_DTYPE), sds((seq, 256), MXU_DTYPE), sds((seq, 512), MXU_DTYPE)],
        compiler_params=_cp("arbitrary", "arbitrary"))(h, win_t, cos, sin)


def _halves(kd):
    lo = _lane(kd.shape) < 64
    z = jnp.zeros_like(kd)
    return jnp.concatenate([jnp.where(lo, kd, z), jnp.where(lo, z, kd)], axis=0)


def _attn_fwd(qt, kd, vd, dep=None):
    seq = qt.shape[0]
    bq, bk = min(2048, seq), min(1024, seq)
    nk = seq // bk
    deps, dep_specs = _dep_args(dep, 2)

    def body(q_ref, k_ref, v_ref, *rest):
        o_ref, lse_ref, m_scr, l_scr, acc = rest[-5:]
        j = pl.program_id(1)
        m_scr[...] = jnp.full_like(m_scr, -jnp.inf)
        l_scr[...] = jnp.zeros_like(l_scr)
        acc[...] = jnp.zeros_like(acc)
        sub = min(512, bq)
        nsub = bq // sub
        lo = _lane((sub, 128)) < 64

        def kv_block(n, carry):
            rows = pl.ds(pl.multiple_of(n * bk, bk), bk)
            k2, v2 = _halves(k_ref[rows, :]), _halves(v_ref[rows, :])
            for pr in range(2):
                for hf in range(nsub):
                    qr = slice(hf * sub, (hf + 1) * sub)
                    s2 = _nt(q_ref[qr, 128 * pr:128 * (pr + 1)], k2)
                    ps, alphas = [], []
                    for e in range(2):
                        g = 2 * pr + e
                        s = s2[:, e * bk:(e + 1) * bk]
                        m_prev = m_scr[g, qr]
                        m_next = jnp.maximum(m_prev, jnp.max(s, axis=1, keepdims=True))
                        alpha = jnp.exp2(m_prev - m_next)
                        pe = jnp.exp2(s - jnp.tile(m_next, (1, bk // 128)))
                        l_scr[g, qr] = alpha * l_scr[g, qr] + jnp.sum(pe, axis=1, keepdims=True)
                        m_scr[g, qr] = m_next
                        ps.append(_mx(pe))
                        alphas.append(alpha)
                    o2 = _nn(jnp.concatenate(ps, axis=1), v2)
                    sl = slice(128 * pr, 128 * (pr + 1))
                    acc[qr, sl] = acc[qr, sl] * jnp.where(lo, alphas[0], alphas[1]) + o2
            return carry

        lax.fori_loop(0, nk, kv_block, 0)
        lo_all = _lane((bq, 128)) < 64
        for pr in range(2):
            sl = slice(128 * pr, 128 * (pr + 1))
            o_ref[:, sl] = acc[:, sl] / jnp.where(lo_all, l_scr[2 * pr], l_scr[2 * pr + 1])
        for g in range(4):
            lse = jnp.transpose(m_scr[g] + jnp.log2(l_scr[g]))
            lse_ref[pl.ds(4 * j + g, 1), :] = lse[0:1, :]

    return pl.pallas_call(
        body, grid=(seq // bq, 2), name="attn_fwd",
        in_specs=[pl.BlockSpec((bq, 256), lambda i, j: (i, j)), pl.BlockSpec((None, seq, 128), lambda i, j: (j, 0, 0)),
                  pl.BlockSpec((None, seq, 128), lambda i, j: (j, 0, 0))] + dep_specs,
        out_specs=[pl.BlockSpec((bq, 256), lambda i, j: (i, j)), pl.BlockSpec((8, bq), lambda i, j: (0, i))],
        out_shape=[jax.ShapeDtypeStruct((seq, 512), F32), jax.ShapeDtypeStruct((8, seq), F32)],
        scratch_shapes=[pltpu.VMEM((4, bq, 128), F32), pltpu.VMEM((4, bq, 128), F32), pltpu.VMEM((bq, 256), F32)],
        compiler_params=_cp("parallel", "arbitrary"))(qt, kd, vd, *deps)


def _attn_bwd(qt, kd, vd, do, lse, delta, dep=None):
    seq = qt.shape[0]
    bq, bk = min(1024, seq), min(1024, seq)
    nq, nk = seq // bq, seq // bk
    deps, dep_specs = _dep_args(dep, 3)

    def body(q_ref, do_ref, k_ref, v_ref, lse_ref, del_ref, *rest):
        dq_ref, dk_ref, dv_ref = rest[-3:]
        j, i = pl.program_id(0), pl.program_id(1)
        dq_ref[...] = jnp.zeros_like(dq_ref)

        @pl.when(i == 0)
        def _():
            dk_ref[...] = jnp.zeros_like(dk_ref)
            dv_ref[...] = jnp.zeros_like(dv_ref)

        lo = _lane((bk, 128)) < 64
        big = lambda r: jnp.concatenate([jnp.broadcast_to(r[0], (bk, bq)), jnp.broadcast_to(r[1], (bk, bq))], axis=0)

        def kv_block(n, carry):
            rows = pl.ds(pl.multiple_of(n * bk, bk), bk)
            k2, v2 = _halves(k_ref[rows, :]), _halves(v_ref[rows, :])
            for pr in range(2):
                sl = slice(128 * pr, 128 * (pr + 1))
                qp, dop = q_ref[:, sl], do_ref[:, sl]
                s_t = _nt(k2, qp)
                dp_t = _nt(v2, dop)
                ls = [lse_ref[pl.ds(4 * j + 2 * pr + e, 1), :] for e in range(2)]
                dl = [del_ref[pl.ds(4 * j + 2 * pr + e, 1), :] for e in range(2)]
                p_t = jnp.exp2(s_t - big(ls))
                ds_t = _mx(p_t * (dp_t - big(dl)))
                rv = _nn(_mx(p_t), dop)
                rk = _nn(ds_t, qp) * LN2
                dv_ref[rows, :] += jnp.where(lo, rv[:bk], rv[bk:])
                dk_ref[rows, :] += jnp.where(lo, rk[:bk], rk[bk:])
                dq_ref[:, sl] += _tn(ds_t, k2)
            return carry

        lax.fori_loop(0, nk, kv_block, 0)

    return pl.pallas_call(
        body, grid=(2, nq), name="attn_bwd",
        in_specs=[pl.BlockSpec((bq, 256), lambda j, i: (i, j)), pl.BlockSpec((bq, 256), lambda j, i: (i, j)),
                  pl.BlockSpec((None, seq, 128), lambda j, i: (j, 0, 0)), pl.BlockSpec((None, seq, 128), lambda j, i: (j, 0, 0)),
                  pl.BlockSpec((8, bq), lambda j, i: (0, i)), pl.BlockSpec((8, bq), lambda j, i: (0, i))] + dep_specs,
        out_specs=[pl.BlockSpec((bq, 256), lambda j, i: (i, j)), pl.BlockSpec((None, seq, 128), lambda j, i: (j, 0, 0)),
                   pl.BlockSpec((None, seq, 128), lambda j, i: (j, 0, 0))],
        out_shape=[jax.ShapeDtypeStruct((seq, 512), F32), jax.ShapeDtypeStruct((2, seq, 128), F32),
                   jax.ShapeDtypeStruct((2, seq, 128), F32)],
        compiler_params=_cp("arbitrary", "arbitrary"))(qt, do, kd, vd, lse, delta, *deps)


def _ret_tables(lg_f, lg_b, c):
    idx = jnp.arange(c, dtype=F32)
    diff = idx[:, None] - idx[None, :]
    a, b = lg_f[:, None, None], lg_b[:, None, None]
    low, up = (diff >= 0)[None], (diff < 0)[None]
    dmat = jnp.where(low, jnp.exp(a * jnp.maximum(diff, 0.0)[None]), jnp.exp(b * jnp.maximum(-diff, 0.0)[None]))
    dda = jnp.where(low, diff[None] * jnp.exp(a * jnp.maximum(diff, 0.0)[None]), 0.0)
    ddb = jnp.where(up, -diff[None] * jnp.exp(b * jnp.maximum(-diff, 0.0)[None]), 0.0)
    rep = lambda w: jnp.broadcast_to(w[:, :, None], (RET_HEADS, c, 128))
    wqf = rep(jnp.exp(lg_f[:, None] * (idx + 1.0)[None]))
    wqb = rep(jnp.exp(lg_b[:, None] * (c - idx)[None]))
    wkf = rep(jnp.exp(lg_f[:, None] * (c - 1.0 - idx)[None]))
    wkb = rep(jnp.exp(lg_b[:, None] * idx[None]))
    cdf, cdb = jnp.exp(lg_f * c), jnp.exp(lg_b * c)
    dec_fb = jnp.broadcast_to(jnp.concatenate([cdf, cdb])[:, None], (8, 128))
    dec_bf = jnp.broadcast_to(jnp.concatenate([cdb, cdf])[:, None], (8, 128))
    return dict(dmat=dmat, dda=dda, ddb=ddb, wqf=wqf, wqb=wqb, wkf=wkf, wkb=wkb, dec_fb=dec_fb, dec_bf=dec_bf)


def _ret_states(xk, yv, w_fwd, w_rev, dec, name):
    seq = xk.shape[0]
    c = RET_CHUNK
    nc = seq // c
    grp = min(RET_GROUP, nc)
    ns = nc // grp

    def body(xf_ref, yf_ref, xr_ref, yr_ref, wf_ref, wr_ref, dec_ref, sf_ref, sr_ref, st_f, st_r):
        @pl.when(pl.program_id(0) == 0)
        def _():
            st_f[...] = jnp.zeros_like(st_f)
            st_r[...] = jnp.zeros_like(st_r)

        lane = _lane((c, 128))

        def chunk(g, carry):
            for x_ref, y_ref, w_ref, s_ref, st, base, gg in ((xf_ref, yf_ref, wf_ref, sf_ref, st_f, 0, g),
                                                             (xr_ref, yr_ref, wr_ref, sr_ref, st_r, 4, grp - 1 - g)):
                rows = pl.ds(pl.multiple_of(gg * c, c), c)
                for h in range(RET_HEADS):
                    pr, e = h // 2, h % 2
                    half = (lane < 64) if e == 0 else (lane >= 64)
                    s_ref[gg, h] = st[h].astype(s_ref.dtype)
                    xm = jnp.where(half, x_ref[rows, 128 * pr:128 * (pr + 1)].astype(F32) * w_ref[h], 0.0)
                    st[h] = st[h] * dec_ref[base + h:base + h + 1, :] + _tn(_mx(xm), _mx(y_ref[rows, 128 * h:128 * (h + 1)]))
            return carry

        lax.fori_loop(0, grp, chunk, 0, unroll=True)

    xs = lambda f: pl.BlockSpec((grp * c, 256), f)
    ys = lambda f: pl.BlockSpec((grp * c, 512), f)
    fwd, rev = (lambda n: (n, 0)), (lambda n: (ns - 1 - n, 0))
    wsp = pl.BlockSpec((RET_HEADS, c, 128), lambda n: (0, 0, 0))
    return pl.pallas_call(
        body, grid=(ns,), name=name,
        in_specs=[xs(fwd), ys(fwd), xs(rev), ys(rev), wsp, wsp, pl.BlockSpec((8, 128), lambda n: (0, 0))],
        out_specs=[pl.BlockSpec((grp, RET_HEADS, 128, 128), lambda n: (n, 0, 0, 0)),
                   pl.BlockSpec((grp, RET_HEADS, 128, 128), lambda n: (ns - 1 - n, 0, 0, 0))],
        out_shape=[jax.ShapeDtypeStruct((nc, RET_HEADS, 128, 128), MXU_DTYPE)] * 2,
        scratch_shapes=[pltpu.VMEM((RET_HEADS, 128, 128), F32), pltpu.VMEM((RET_HEADS, 128, 128), F32)],
        compiler_params=_cp("arbitrary"))(xk, yv, xk, yv, w_fwd, w_rev, dec)


def _ret_fwd(rq, rk, rv, rf, rb, tb):
    seq = rq.shape[0]
    c = RET_CHUNK
    grp = min(RET_GROUP, seq // c)

    def body(q_ref, k_ref, v_ref, rf_ref, rb_ref, d_ref, wqf_ref, wqb_ref, o_ref):
        lane = _lane((c, 128))

        def chunk(g, carry):
            rows = pl.ds(pl.multiple_of(g * c, c), c)
            for h in range(RET_HEADS):
                pr, e = h // 2, h % 2
                half = (lane < 64) if e == 0 else (lane >= 64)
                sl = slice(128 * pr, 128 * (pr + 1))
                qp, kp = q_ref[rows, sl], k_ref[rows, sl]
                km = jnp.where(half, kp, jnp.zeros_like(kp))
                sd = _mx(_nt(qp, km) * d_ref[h])
                qf = qp.astype(F32)
                o = _nn(sd, v_ref[rows, 128 * h:128 * (h + 1)])
                o = o + _nn(_mx(qf * wqf_ref[h]), rf_ref[g, h]) + _nn(_mx(qf * wqb_ref[h]), rb_ref[g, h])
                o_ref[rows, 128 * h:128 * (h + 1)] = o
            return carry

        lax.fori_loop(0, grp, chunk, 0, unroll=True)

    st = pl.BlockSpec((grp, RET_HEADS, 128, 128), lambda n: (n, 0, 0, 0))
    wsp = pl.BlockSpec((RET_HEADS, c, 128), lambda n: (0, 0, 0))
    return pl.pallas_call(
        body, grid=(seq // (grp * c),), name="ret_fwd",
        in_specs=[pl.BlockSpec((grp * c, 256), lambda n: (n, 0)), pl.BlockSpec((grp * c, 256), lambda n: (n, 0)),
                  pl.BlockSpec((grp * c, 512), lambda n: (n, 0)), st, st, pl.BlockSpec((RET_HEADS, c, c), lambda n: (0, 0, 0)), wsp, wsp],
        out_specs=pl.BlockSpec((grp * c, 512), lambda n: (n, 0)),
        out_shape=jax.ShapeDtypeStruct((seq, 512), F32), compiler_params=_cp("parallel"))(
            rq, rk, rv, rf, rb, tb["dmat"], tb["wqf"], tb["wqb"])


def _ret_bwd(rq, rk, rv, do, rf, rb, hf, hb, tb):
    seq = rq.shape[0]
    c = RET_CHUNK
    grp = min(RET_GROUP, seq // c)

    def body(q_ref, k_ref, v_ref, do_ref, rf_ref, rb_ref, hf_ref, hb_ref, d_ref, dda_ref, ddb_ref,
             wqf_ref, wqb_ref, wkf_ref, wkb_ref, cdec_ref, dq_ref, dk_ref, dv_ref, dlg_ref):
        @pl.when(pl.program_id(0) == 0)
        def _():
            dlg_ref[...] = jnp.zeros_like(dlg_ref)

        lane = _lane((c, 128))
        pos = lax.broadcasted_iota(jnp.int32, (c, 128), 0).astype(F32)

        def chunk(g, carry):
            rows = pl.ds(pl.multiple_of(g * c, c), c)
            for pr in range(2):
                sl = slice(128 * pr, 128 * (pr + 1))
                qp, kp = q_ref[rows, sl], k_ref[rows, sl]
                qf, kf = qp.astype(F32), kp.astype(F32)
                dq_acc = jnp.zeros((c, 128), F32)
                dk_acc = jnp.zeros((c, 128), F32)
                for e in range(2):
                    h = 2 * pr + e
                    half = (lane < 64) if e == 0 else (lane >= 64)
                    hs = slice(128 * h, 128 * (h + 1))
                    km = jnp.where(half, kp, jnp.zeros_like(kp))
                    kmf = km.astype(F32)
                    vh, doh = v_ref[rows, hs], do_ref[rows, hs]
                    rfh, rbh, hfh, hbh = rf_ref[g, h], rb_ref[g, h], hf_ref[g, h], hb_ref[g, h]
                    s = _nt(qp, km)
                    dpm = _nt(doh, vh)
                    ds_b = _mx(dpm * d_ref[h])
                    sd_b = _mx(s * d_ref[h])
                    dq_if = wqf_ref[h] * _nt(doh, rfh)
                    dq_ib = wqb_ref[h] * _nt(doh, rbh)
                    dq_acc = dq_acc + _nn(ds_b, km) + dq_if + dq_ib
                    dk_sf = wkf_ref[h] * _nt(vh, hfh)
                    dk_sb = wkb_ref[h] * _nt(vh, hbh)
                    dk_acc = dk_acc + jnp.where(half, _tn(ds_b, qp), 0.0) + dk_sf + dk_sb
                    dv = _tn(sd_b, doh) + _nn(_mx(kmf * wkf_ref[h]), hfh) + _nn(_mx(kmf * wkb_ref[h]), hbh)
                    dv_ref[rows, hs] = dv.astype(dv_ref.dtype)
                    sdp = s * dpm
                    hr_f = hfh.astype(F32) * rfh.astype(F32)
                    hr_b = hbh.astype(F32) * rbh.astype(F32)
                    da = (_rowsum128(sdp * dda_ref[h]) + _rowsum128((pos + 1.0) * qf * dq_if)
                          + _rowsum128((c - 1.0 - pos) * kf * dk_sf) + cdec_ref[h:h + 1, :] * _rowsum128(hr_f))
                    db = (_rowsum128(sdp * ddb_ref[h]) + _rowsum128((c - pos) * qf * dq_ib)
                          + _rowsum128(pos * kf * dk_sb) + cdec_ref[4 + h:5 + h, :] * _rowsum128(hr_b))
                    dlg_ref[h:h + 1, :] += da
                    dlg_ref[4 + h:5 + h, :] += db
                dq_ref[rows, sl] = dq_acc
                dk_ref[rows, sl] = dk_acc
            return carry

        lax.fori_loop(0, grp, chunk, 0, unroll=2)

    st = pl.BlockSpec((grp, RET_HEADS, 128, 128), lambda n: (n, 0, 0, 0))
    wsp = pl.BlockSpec((RET_HEADS, c, 128), lambda n: (0, 0, 0))
    dsp = pl.BlockSpec((RET_HEADS, c, c), lambda n: (0, 0, 0))
    x256 = pl.BlockSpec((grp * c, 256), lambda n: (n, 0))
    x512 = pl.BlockSpec((grp * c, 512), lambda n: (n, 0))
    cdec = tb["dec_fb"] * float(c)
    return pl.pallas_call(
        body, grid=(seq // (grp * c),), name="ret_bwd",
        in_specs=[x256, x256, x512, x512, st, st, st, st, dsp, dsp, dsp, wsp, wsp, wsp, wsp,
                  pl.BlockSpec((8, 128), lambda n: (0, 0))],
        out_specs=[x256, x256, x512, pl.BlockSpec((8, 128), lambda n: (0, 0))],
        out_shape=[jax.ShapeDtypeStruct((seq, 256), F32), jax.ShapeDtypeStruct((seq, 256), F32),
                   jax.ShapeDtypeStruct((seq, 512), MXU_DTYPE), jax.ShapeDtypeStruct((8, 128), F32)],
        compiler_params=_cp("arbitrary"))(
            rq, rk, rv, do, rf, rb, hf, hb, tb["dmat"], tb["dda"], tb["ddb"], tb["wqf"], tb["wqb"], tb["wkf"], tb["wkb"], cdec)


def _tail(x, tgt, o_att, o_ret, p, mod, g_post, gn_g, wpt, wout):
    seq = x.shape[0]
    bs = 256
    nb = seq // bs

    def body(x_ref, t_ref, oa_ref, or_ref, za_ref, zr_ref, gl_ref, mod_ref, gp_ref, gn_ref, wpt_ref, wout_ref,
             dout_ref, dza_ref, dzr_ref, dgl_ref, doa_ref, dor_ref, del_ref, gout_ref, gpt_ref,
             dgate_ref, dgpost_ref, dgn_ref, loss_ref, gout_acc, gpt_acc):
        i = pl.program_id(0)

        @pl.when(i == 0)
        def _():
            gout_acc[...] = jnp.zeros_like(gout_acc)
            gpt_acc[...] = jnp.zeros_like(gpt_acc)
            dgate_ref[...] = jnp.zeros_like(dgate_ref)
            dgpost_ref[...] = jnp.zeros_like(dgpost_ref)
            dgn_ref[...] = jnp.zeros_like(dgn_ref)
            loss_ref[...] = jnp.zeros_like(loss_ref)

        oa, za, zr = oa_ref[...], za_ref[...], zr_ref[...]
        sga, sgr = _sigmoid(za), _sigmoid(zr)
        sza, szr = za * sga, zr * sgr
        ya_b = _mx(oa * sza)
        ons, rstds = [], []
        for h in range(RET_HEADS):
            oh = or_ref[:, 128 * h:128 * (h + 1)]
            xc = oh - jnp.mean(oh, axis=-1, keepdims=True)
            rstd = lax.rsqrt(jnp.mean(xc * xc, axis=-1, keepdims=True) + EPS)
            ons.append(xc * rstd)
            rstds.append(rstd)
        on = jnp.concatenate(ons, axis=1)
        yn = on * gn_ref[...]
        yr_b = _mx(yn * szr)
        wpa_t, wpr_t = wpt_ref[:, 0:512], wpt_ref[:, 512:1024]
        a_att = _nt(ya_b, wpa_t)
        a_ret = _nt(yr_b, wpr_t)
        gts = _sigmoid(gl_ref[...])
        g_att, g_ret = gts[:, 0:D_MODEL], gts[:, D_MODEL:2 * D_MODEL]
        merged_b = _mx(g_att * a_att + g_ret * a_ret)
        u = _nn(merged_b, wout_ref[...])
        r = lax.rsqrt(jnp.mean(u * u, axis=-1, keepdims=True) + EPS)
        un = u * r
        gpost = gp_ref[...]
        y = un * gpost
        gate = mod_ref[:, 2 * D_MODEL:3 * D_MODEL]
        err = x_ref[...] + gate * y - t_ref[...]
        loss_ref[...] += (0.5 / D_MODEL) * _rowsum128(err * err)
        dout = err * (1.0 / D_MODEL)
        dout_ref[...] = dout
        dgate_ref[...] += jnp.sum(dout * y, axis=0, keepdims=True)
        dy = dout * gate
        dgpost_ref[...] += jnp.sum(dy * un, axis=0, keepdims=True)
        dun = dy * gpost
        du_b = _mx(r * (dun - un * jnp.mean(dun * un, axis=-1, keepdims=True)))
        dmerged = _nt(du_b, wout_ref[...])
        gout_acc[...] += _tn(merged_b, du_b)
        d_att, d_ret = dmerged * g_att, dmerged * g_ret
        dgl_ref[:, 0:D_MODEL] = (d_att * a_att * (1.0 - g_att)).astype(dgl_ref.dtype)
        dgl_ref[:, D_MODEL:2 * D_MODEL] = (d_ret * a_ret * (1.0 - g_ret)).astype(dgl_ref.dtype)
        d_att_b, d_ret_b = _mx(d_att), _mx(d_ret)
        dya = _nn(d_att_b, wpa_t)
        dyr = _nn(d_ret_b, wpr_t)
        gpt_acc[:, 0:512] += _tn(d_att_b, ya_b)
        gpt_acc[:, 512:1024] += _tn(d_ret_b, yr_b)
        dza_ref[...] = (dya * oa * (sga * (1.0 + za * (1.0 - sga)))).astype(dza_ref.dtype)
        doa = dya * sza
        doa_ref[...] = doa.astype(doa_ref.dtype)
        dt = jnp.transpose(doa * oa)
        del_ref[...] = jnp.sum(dt.reshape(8, HEAD_DIM, bs), axis=1)
        dzr_ref[...] = (dyr * yn * (sgr * (1.0 + zr * (1.0 - sgr)))).astype(dzr_ref.dtype)
        dyn = dyr * szr
        dgn_ref[...] += jnp.sum(dyn * on, axis=0, keepdims=True)
        don = dyn * gn_ref[...]
        for h in range(RET_HEADS):
            hs = slice(128 * h, 128 * (h + 1))
            dh, oh = don[:, hs], ons[h]
            doh = rstds[h] * (dh - jnp.mean(dh, axis=-1, keepdims=True) - oh * jnp.mean(dh * oh, axis=-1, keepdims=True))
            dor_ref[:, hs] = doh.astype(dor_ref.dtype)

        @pl.when(i == nb - 1)
        def _():
            gout_ref[...] = gout_acc[...].astype(gout_ref.dtype)
            gpt_ref[...] = gpt_acc[...].astype(gpt_ref.dtype)

    row = lambda w: pl.BlockSpec((bs, w), lambda i: (i, 0))
    el = lambda w, off: pl.BlockSpec((pl.Element(bs), pl.Element(w)), lambda i: (i * bs, off))
    vec = lambda w: pl.BlockSpec((1, w), lambda i: (0, 0))
    full = pl.BlockSpec((D_MODEL, D_MODEL), lambda i: (0, 0))
    sds = jax.ShapeDtypeStruct
    return pl.pallas_call(
        body, grid=(nb,), name="tail",
        in_specs=[row(D_MODEL), row(D_MODEL), row(512), row(512), el(512, ZA_B), el(512, ZR_B), el(2 * D_MODEL, GL_B),
                  vec(3 * D_MODEL), vec(D_MODEL), vec(512), full, full],
        out_specs=[row(D_MODEL), row(512), row(512), row(2 * D_MODEL), row(512), row(512),
                   pl.BlockSpec((8, bs), lambda i: (0, i)), full, full, vec(D_MODEL), vec(D_MODEL), vec(512), vec(128)],
        out_shape=[sds((seq, D_MODEL), F32), sds((seq, 512), MXU_DTYPE), sds((seq, 512), MXU_DTYPE),
                   sds((seq, 2 * D_MODEL), MXU_DTYPE), sds((seq, 512), MXU_DTYPE), sds((seq, 512), MXU_DTYPE),
                   sds((8, seq), F32), sds((D_MODEL, D_MODEL), MXU_DTYPE), sds((D_MODEL, D_MODEL), MXU_DTYPE),
                   sds((1, D_MODEL), F32), sds((1, D_MODEL), F32), sds((1, 512), F32), sds((1, 128), F32)],
        scratch_shapes=[pltpu.VMEM((D_MODEL, D_MODEL), F32), pltpu.VMEM((D_MODEL, D_MODEL), F32)],
        compiler_params=_cp("arbitrary"))(x, tgt, o_att, o_ret, p, p, p, mod, g_post, gn_g, wpt, wout)


def _assemble(p, cos, sin, qg2, kg2, dqt, dkp, dvp, dza, drq, drk, drv, dzr, dgl):
    seq = p.shape[0]
    bs = 512

    def body(p_ref, cos_ref, sin_ref, qg_ref, kg_ref, dqt_ref, dkp_ref, dvp_ref, dza_ref, drq_ref, drk_ref, drv_ref,
             dzr_ref, dgl_ref, dp_ref, dqg_ref, dkg_ref):
        @pl.when(pl.program_id(0) == 0)
        def _():
            dqg_ref[...] = jnp.zeros_like(dqg_ref)
            dkg_ref[...] = jnp.zeros_like(dkg_ref)

        m = _blockdiag64()
        cs, sn = cos_ref[...], sin_ref[...]
        lo = _lane((bs, 128)) < 64

        def norm_bwd(raw, dn, g, dg_ref):
            r = lax.rsqrt(_seg64(raw * raw, m) * (1.0 / 64) + EPS)
            xn = raw * r
            dg_ref[...] += jnp.sum(dn * xn, axis=0, keepdims=True)
            dxn = dn * g
            return r * (dxn - xn * (_seg64(dxn * xn, m) * (1.0 / 64)))

        for pr in range(4):
            sl = slice(128 * pr, 128 * (pr + 1))
            dn = _rope_t(dqt_ref[:, sl] * 0.125, cs, sn)
            dp_ref[:, QA + 128 * pr:QA + 128 * (pr + 1)] = norm_bwd(p_ref[:, sl], dn, qg_ref[...], dqg_ref).astype(dp_ref.dtype)
        fold = lambda a: a + pltpu.roll(a, 64, 1)
        dk = jnp.where(lo, fold(dkp_ref[0]), fold(dkp_ref[1]))
        dp_ref[:, KA:KA + 128] = norm_bwd(p_ref[:, KA:KA + 128], _rope_t(dk, cs, sn), kg_ref[...], dkg_ref).astype(dp_ref.dtype)
        dp_ref[:, VA:VA + 128] = jnp.where(lo, fold(dvp_ref[0]), fold(dvp_ref[1])).astype(dp_ref.dtype)
        dp_ref[:, ZA:ZA + 512] = dza_ref[...]
        for pr in range(2):
            sl = slice(128 * pr, 128 * (pr + 1))
            dp_ref[:, QR + 128 * pr:QR + 128 * (pr + 1)] = _rope_t(drq_ref[:, sl], cs, sn).astype(dp_ref.dtype)
            dp_ref[:, KR + 128 * pr:KR + 128 * (pr + 1)] = _rope_t(drk_ref[:, sl] * 0.125, cs, sn).astype(dp_ref.dtype)
        dp_ref[:, VR:VR + 512] = drv_ref[...]
        dp_ref[:, ZR:ZR + 512] = dzr_ref[...]
        dp_ref[:, GL:GL + 2 * D_MODEL] = dgl_ref[...]

    row = lambda w: pl.BlockSpec((bs, w), lambda i: (i, 0))
    gsp = pl.BlockSpec((1, 128), lambda i: (0, 0))
    dup = pl.BlockSpec((2, bs, 128), lambda i: (0, i, 0))
    return pl.pallas_call(
        body, grid=(seq // bs,), name="assemble_dp",
        in_specs=[row(768), row(128), row(128), gsp, gsp, row(512), dup, dup, row(512), row(256), row(256), row(512),
                  row(512), row(2 * D_MODEL)],
        out_specs=[row(IN_WIDTH), gsp, gsp],
        out_shape=[jax.ShapeDtypeStruct((seq, IN_WIDTH), MXU_DTYPE), jax.ShapeDtypeStruct((1, 128), F32),
                   jax.ShapeDtypeStruct((1, 128), F32)],
        compiler_params=_cp("arbitrary"))(p, cos, sin, qg2, kg2, dqt, dkp, dvp, dza, drq, drk, drv, dzr, dgl)


def _local_step(x, tgt, mod, g_pre, qn_g, kn_g, w_dec_f, w_dec_b, gn_g, g_post, w_attn, rest_start, rest_wait, send=None):
    send = send or (lambda name, arrays: None)
    seq = x.shape[0]
    cos, sin = _rope_tables(seq)
    qg2, kg2 = jnp.tile(qn_g, (1, 2)), jnp.tile(kn_g, (1, 2))
    lg_f = jax.nn.log_sigmoid(w_dec_f[0])
    lg_b = jax.nn.log_sigmoid(w_dec_b[0])
    tb = _ret_tables(lg_f, lg_b, RET_CHUNK)

    h, p_a, qt, kd, vd = _attn_inputs(x, mod, g_pre, w_attn, cos, sin, qg2, kg2)
    o_att, lse = _attn_fwd(qt, kd, vd, dep=rest_start(qt))
    win_t, wpt, wout = rest_wait(o_att)
    p_b, rq, rk, rv = _in_proj_rest(h, win_t, cos, sin)
    rf, rb = _ret_states(rk, rv, tb["wkf"], tb["wkb"], tb["dec_fb"], "ret_states_fwd")
    o_ret = _ret_fwd(rq, rk, rv, rf, rb, tb)
    (dout, dza, dzr, dgl, do_att, do_ret, delta, g_out, g_pt, dgate, dgpost, dgn, loss_l) = _tail(
        x, tgt, o_att, o_ret, p_b, mod, g_post, gn_g, wpt, wout)
    dep = send("early", (g_pt, g_out))
    dqt, dkp, dvp = _attn_bwd(qt, kd, vd, do_att, lse, delta, dep=dep)
    hb, hf = _ret_states(rq, do_ret, tb["wqb"], tb["wqf"], tb["dec_bf"], "ret_states_bwd")
    drq, drk, drv, dlg = _ret_bwd(rq, rk, rv, do_ret, rf, rb, hf, hb, tb)
    dp, dqg, dkg = _assemble(p_a, cos, sin, qg2, kg2, dqt, dkp, dvp, dza, drq, drk, drv, dzr, dgl)
    g_in_t = _matmul(dp, h, ta=True, tb=False, tm=256, tn=D_MODEL, out_dtype=MXU_DTYPE, name="in_proj_dw")
    dep = send("late", (g_in_t,))
    grad_x, dshift, dscale, dgpre = _in_proj_dx(x, dp, win_t, dout, mod, g_pre, dep=dep)

    dlg = jnp.sum(dlg, axis=1)
    small = dict(
        dmod=jnp.concatenate([dshift, dscale, dgate], axis=1),
        g_pre=dgpre, g_post=dgpost, gn_g=dgn,
        qn_g=dqg[:, :64] + dqg[:, 64:], kn_g=dkg[:, :64] + dkg[:, 64:],
        w_dec_f=(dlg[0:4] * jax.nn.sigmoid(-w_dec_f[0]))[None], w_dec_b=(dlg[4:8] * jax.nn.sigmoid(-w_dec_b[0]))[None],
        loss=jnp.sum(loss_l, axis=1, keepdims=True))
    return grad_x, g_in_t, g_pt, g_out, small


N_DEV = 8
N_CHIP = 4
HBM_SPEC = pl.BlockSpec(memory_space=pl.ANY)
VMEM_SPEC = pl.BlockSpec(memory_space=pltpu.VMEM)
SHARD_ROWS = (IN_WIDTH // N_CHIP, D_MODEL // N_CHIP, D_MODEL // N_CHIP)


def _coords():
    return lax.axis_index("x"), lax.axis_index("y"), lax.axis_index("c")


def _peer(k):
    x, y, c = _coords()
    return (1 - x if k & 4 else x, 1 - y if k & 2 else y, 1 - c if k & 1 else c)


def _dev_index(p):
    return 4 * p[0] + 2 * p[1] + p[2]


def _rows(ref, start, size):
    return ref.at[pl.ds(pl.multiple_of(start, 16), size), :]


def _remote(src, dst, ssem, rsem, dev):
    return pltpu.make_async_remote_copy(src_ref=src, dst_ref=dst, send_sem=ssem, recv_sem=rsem, device_id=dev,
                                        device_id_type=MESH)


ATTN_CHUNK = 32


def _mod_and_attn_rows(c, w_ada_s, b4, win_s):
    ncol = w_ada_s.shape[1]
    half = ATTN_COLS // 2
    offs = list(range(0, half, ATTN_CHUNK))
    nq = len(offs)
    rows = lambda ref, cc, off: ref.at[pl.ds(pl.multiple_of(cc * half + off, 16), ATTN_CHUNK), :]

    def body(c_ref, w_ref, b_ref, src, cs_ref, mod_ref, out, modsh, modall, ssem, rsem, lsem, asem, bsem, fsem, gsem, hsem):
        x, y, cc = _coords()
        me = _dev_index((x, y, cc))
        chip = me // 2
        sib = (x, y, 1 - cc)
        cs_ref[me] = c_ref[...]
        sends = []
        for k in range(1, N_DEV):
            cp = _remote(c_ref, cs_ref.at[me], ssem.at[k - 1], rsem.at[k - 1], _peer(k))
            cp.start()
            sends.append(cp)
        own = pltpu.make_async_copy(src.at[pl.ds(0, ATTN_COLS), :], out, lsem.at[0])
        bulk = [_remote(rows(src, cc, off), rows(out, cc, off), asem.at[(k2 - 1) * nq + q], bsem.at[q], (k2 // 2, k2 % 2, cc))
                for k2 in (1, 2) for q, off in enumerate(offs)]
        relay_chip = lambda q: 1 + q % 2

        @pl.when(chip == 0)
        def _():
            own.start()
            for cp in bulk:
                cp.start()

        for k in range(1, N_DEV):
            slot = cs_ref.at[_dev_index(_peer(k))]
            _remote(slot, slot, ssem.at[k - 1], rsem.at[k - 1], _peer(k)).wait_recv()
        cs = jnp.concatenate([cs_ref[d] for d in range(N_DEV)], axis=0)
        ms = _nn(cs * _sigmoid(cs), w_ref[...], precision=HIGHEST) + b_ref[pl.ds(chip, 1), :]
        modsh[...] = ms
        modall[chip] = ms
        for k2 in range(1, N_CHIP):
            cp = _remote(modsh, modall.at[chip], ssem.at[6 + k2], rsem.at[6 + k2], _peer(2 * k2))
            cp.start()
            sends.append(cp)

        @pl.when(chip != 0)
        def _():
            passed = []
            relays = [_remote(rows(out, cc, off), rows(out, cc, off), hsem.at[q], bsem.at[q], (1, 1, cc)) for q, off in enumerate(offs)]
            for q, off in enumerate(offs):
                got = rows(out, cc, off)
                _remote(got, got, asem.at[q], bsem.at[q], (0, 0, cc)).wait_recv()
                cp = _remote(got, got, fsem.at[q], gsem.at[q], sib)
                cp.start()
                passed.append(cp)
                pl.when(chip == relay_chip(q))(relays[q].start)
            for q, off in enumerate(offs):
                got = rows(out, 1 - cc, off)
                _remote(got, got, fsem.at[q], gsem.at[q], sib).wait_recv()
            for cp in passed:
                cp.wait_send()
            for q in range(nq):
                pl.when(chip == relay_chip(q))(relays[q].wait_send)

        for k2 in range(1, N_CHIP):
            slot = modall.at[_dev_index(_peer(2 * k2)) // 2]
            _remote(slot, slot, ssem.at[6 + k2], rsem.at[6 + k2], _peer(2 * k2)).wait_recv()
        for jj in range(N_CHIP):
            mod_ref[:, ncol * jj:ncol * (jj + 1)] = modall[jj, pl.ds(me, 1), :]
        for cp in sends:
            cp.wait_send()

        @pl.when(chip == 0)
        def _():
            for cp in bulk:
                cp.wait_send()
            own.wait()

    dma = pltpu.SemaphoreType.DMA
    return pl.pallas_call(
        body, name="mod_and_attn_rows", in_specs=[VMEM_SPEC] * 4, out_specs=[VMEM_SPEC, VMEM_SPEC, HBM_SPEC],
        out_shape=[jax.ShapeDtypeStruct((N_DEV, 1, D_MODEL), F32), jax.ShapeDtypeStruct((1, N_CHIP * ncol), F32),
                   jax.ShapeDtypeStruct((ATTN_COLS, win_s.shape[1]), win_s.dtype)],
        scratch_shapes=[pltpu.VMEM((N_DEV, ncol), F32), pltpu.VMEM((N_CHIP, N_DEV, ncol), F32), dma((10,)), dma((10,)),
                        dma((1,)), dma((2 * nq,)), dma((nq,)), dma((nq,)), dma((nq,)), dma((nq,))],
        compiler_params=_cp())(c, w_ada_s, b4, win_s)


GATHER_CHUNK = 32


def _chunks(kinds, chunk):
    out = []
    for t, kind in enumerate(kinds):
        half = SHARD_ROWS[kind] // 2
        step = chunk if half % chunk == 0 else half
        out += [(t, off, step) for off in range(0, half, step)]
    return out


SEM_SPEC = pl.BlockSpec(memory_space=pltpu.SEMAPHORE)
HBM_ONLY = pl.BlockSpec(memory_space=pltpu.HBM)
DATAFLOW = pltpu.SideEffectType.DATAFLOW_SIDE_EFFECTING


def _place_own(shards):
    nt = len(shards)

    def body(*refs):
        srcs, outs, lsem = refs[:nt], refs[nt:2 * nt], refs[2 * nt]
        x, y, _ = _coords()
        chip = 2 * x + y
        local = [pltpu.make_async_copy(srcs[t], _rows(outs[t], chip * SHARD_ROWS[t], SHARD_ROWS[t]), lsem.at[t]) for t in range(nt)]
        for cp in local:
            cp.start()
        for cp in local:
            cp.wait()

    return pl.pallas_call(
        body, name="place_own_shard", in_specs=[VMEM_SPEC] * nt, out_specs=[HBM_SPEC] * nt,
        out_shape=[jax.ShapeDtypeStruct((N_CHIP * SHARD_ROWS[t], s.shape[1]), s.dtype) for t, s in enumerate(shards)],
        scratch_shapes=[pltpu.SemaphoreType.DMA((nt,))], compiler_params=_cp())(*shards)


def _gather_rest_start(shards, zones, dep):
    n = len(shards)

    def body(*refs):
        srcs, lands = refs[:n], refs[n:2 * n]
        ssem, rsem = refs[2 * n + 1], refs[2 * n + 2]
        token = refs[-1]
        x, y, _ = _coords()
        chip = 2 * x + y
        for k2 in range(1, N_CHIP):
            for t in range(n):
                q = (k2 - 1) * n + t
                _remote(srcs[t], _rows(lands[t], chip * SHARD_ROWS[t], SHARD_ROWS[t]), ssem.at[q], rsem.at[q], _peer(2 * k2)).start()
        token[...] = jnp.zeros_like(token)

    hbm = lambda a: pltpu.with_memory_space_constraint(a, pltpu.HBM)
    dma = pltpu.SemaphoreType.DMA
    outs = pl.pallas_call(
        body, name="gather_rest", in_specs=[HBM_ONLY] * (2 * n) + [HBM_SPEC],
        out_specs=[SEM_SPEC, SEM_SPEC] + [HBM_ONLY] * (2 * n) + [VMEM_SPEC],
        out_shape=[dma((3 * n,)), dma((3 * n,))] + [pltpu.HBM(a.shape, a.dtype) for a in list(shards) + list(zones)]
        + [jax.ShapeDtypeStruct((8, 128), F32)],
        input_output_aliases={i: 2 + i for i in range(2 * n)},
        compiler_params=pltpu.CompilerParams(has_side_effects=DATAFLOW))(*[hbm(a) for a in list(shards) + list(zones)], dep)
    return outs[0], outs[1], outs[2:2 + n], outs[2 + n:2 + 2 * n], outs[-1]


def _gather_rest_wait(started, after):
    ssem, rsem, shards, zones, _ = started
    n = len(shards)

    def body(*refs):
        srcs, lands = refs[:n], refs[n:2 * n]
        ssem_ref, rsem_ref = refs[2 * n], refs[2 * n + 1]
        for k2 in range(1, N_CHIP):
            pchip = _dev_index(_peer(2 * k2)) // 2
            for t in range(n):
                q = (k2 - 1) * n + t
                cp = _remote(srcs[t], _rows(lands[t], pchip * SHARD_ROWS[t], SHARD_ROWS[t]), ssem_ref.at[q], rsem_ref.at[q], _peer(2 * k2))
                cp.wait_send()
                cp.wait_recv()

    outs = pl.pallas_call(
        body, name="gather_rest_wait", in_specs=[HBM_ONLY] * (2 * n) + [SEM_SPEC, SEM_SPEC, HBM_SPEC], out_specs=[HBM_ONLY] * (2 * n),
        out_shape=[pltpu.HBM(a.shape, a.dtype) for a in list(shards) + list(zones)],
        input_output_aliases={i: i for i in range(2 * n)},
        compiler_params=pltpu.CompilerParams(has_side_effects=DATAFLOW))(*shards, *zones, ssem, rsem, after)
    return outs[n:]


def _reduced_by(ref, t, dev):
    return _rows(ref, (dev // 2) * SHARD_ROWS[t] + (dev % 2) * (SHARD_ROWS[t] // 2), SHARD_ROWS[t] // 2)


def _scatter_start(grads, kinds, name):
    n = len(grads)

    def body(*refs):
        srcs, lands = refs[:n], refs[n:2 * n]
        ssem, rsem = refs[2 * n], refs[2 * n + 1]
        token = refs[-1]
        me = _dev_index(_coords())
        for k in range(1, N_DEV):
            for i, t in enumerate(kinds):
                q = (k - 1) * n + i
                _remote(_reduced_by(srcs[i], t, _dev_index(_peer(k))), lands[i].at[me], ssem.at[q], rsem.at[q], _peer(k)).start()
        token[...] = jnp.zeros_like(token)

    zones = [lax.empty((N_DEV, SHARD_ROWS[t] // 2, g.shape[1]), g.dtype) for g, t in zip(grads, kinds)]
    hbm = lambda a: pltpu.with_memory_space_constraint(a, pltpu.HBM)
    dma = pltpu.SemaphoreType.DMA
    outs = pl.pallas_call(
        body, name=name, in_specs=[HBM_ONLY] * (2 * n), out_specs=[SEM_SPEC, SEM_SPEC] + [HBM_ONLY] * (2 * n) + [VMEM_SPEC],
        out_shape=[dma((7 * n,)), dma((7 * n,))] + [pltpu.HBM(a.shape, a.dtype) for a in list(grads) + zones]
        + [jax.ShapeDtypeStruct((8, 128), F32)],
        input_output_aliases={i: 2 + i for i in range(2 * n)},
        compiler_params=pltpu.CompilerParams(has_side_effects=DATAFLOW))(*[hbm(a) for a in list(grads) + zones])
    return outs[0], outs[1], outs[2:2 + n], outs[2 + n:2 + 2 * n], outs[-1]


def _scatter_wait(started, kinds, after, name):
    ssem, rsem, grads, zones, _ = started
    n = len(grads)

    def body(*refs):
        srcs, lands = refs[:n], refs[n:2 * n]
        ssem_ref, rsem_ref = refs[2 * n], refs[2 * n + 1]
        for k in range(1, N_DEV):
            peer = _dev_index(_peer(k))
            for i, t in enumerate(kinds):
                q = (k - 1) * n + i
                cp = _remote(_reduced_by(srcs[i], t, peer), lands[i].at[peer], ssem_ref.at[q], rsem_ref.at[q], _peer(k))
                cp.wait_send()
                cp.wait_recv()

    outs = pl.pallas_call(
        body, name=name, in_specs=[HBM_ONLY] * (2 * n) + [SEM_SPEC, SEM_SPEC, HBM_SPEC], out_specs=[HBM_ONLY] * (2 * n),
        out_shape=[pltpu.HBM(a.shape, a.dtype) for a in list(grads) + list(zones)],
        input_output_aliases={i: i for i in range(2 * n)},
        compiler_params=pltpu.CompilerParams(has_side_effects=DATAFLOW))(*grads, *zones, ssem, rsem, after)
    return outs[:n], outs[n:]


def _grad_finish(grads, zones, kinds, name):
    nt = len(grads)
    pieces = _chunks(kinds, GATHER_CHUNK)
    npc = len(pieces)
    shard = [SHARD_ROWS[k] for k in kinds]

    def body(*refs):
        srcs, lands, outs = refs[:nt], refs[nt:2 * nt], refs[2 * nt:3 * nt]
        slots, sums = refs[3 * nt:4 * nt], refs[4 * nt:5 * nt]
        lsem, osem, xsem, ysem = refs[5 * nt:]
        x, y, c = _coords()
        me = _dev_index((x, y, c))
        sib = (x, y, 1 - c)
        loads = [pltpu.make_async_copy(_reduced_by(srcs[t], kinds[t], me), slots[t].at[me], lsem.at[t]) for t in range(nt)]
        for k in range(1, N_DEV):
            peer = _dev_index(_peer(k))
            loads += [pltpu.make_async_copy(lands[t].at[peer], slots[t].at[peer], lsem.at[k * nt + t]) for t in range(nt)]
        for cp in loads:
            cp.start()
        for cp in loads:
            cp.wait()
        sends = []
        place = lambda t, cc, off, n: _rows(outs[t], cc * (shard[t] // 2) + off, n)
        stores = []
        for q, (t, off, n) in enumerate(pieces):
            r = pl.ds(off, n)
            acc = slots[t][0, r, :].astype(F32)
            for d in range(1, N_DEV):
                acc = acc + slots[t][d, r, :].astype(F32)
            sums[t][r, :] = acc
            keep = pltpu.make_async_copy(sums[t].at[r, :], place(t, c, off, n), osem.at[q])
            give = _remote(sums[t].at[r, :], place(t, c, off, n), xsem.at[q], ysem.at[q], sib)
            keep.start()
            give.start()
            stores.append(keep)
            sends.append(give)
        for q, (t, off, n) in enumerate(pieces):
            got = place(t, 1 - c, off, n)
            _remote(got, got, xsem.at[q], ysem.at[q], sib).wait_recv()
        for cp in sends:
            cp.wait_send()
        for cp in stores:
            cp.wait()

    dma = pltpu.SemaphoreType.DMA
    return pl.pallas_call(
        body, name=name, in_specs=[HBM_SPEC] * (2 * nt), out_specs=[HBM_SPEC] * nt,
        out_shape=[jax.ShapeDtypeStruct((shard[t], g.shape[1]), F32) for t, g in enumerate(grads)],
        scratch_shapes=([pltpu.VMEM((N_DEV, shard[t] // 2, g.shape[1]), g.dtype) for t, g in enumerate(grads)]
                        + [pltpu.VMEM((shard[t] // 2, g.shape[1]), F32) for t, g in enumerate(grads)]
                        + [dma((N_DEV * nt,)), dma((npc,)), dma((npc,)), dma((npc,))]),
        compiler_params=_cp())(*grads, *zones)


def _adam_math(w, g, m, v):
    m = ADAM_B1 * m + (1.0 - ADAM_B1) * g
    v = ADAM_B2 * v + (1.0 - ADAM_B2) * (g * g)
    m_hat = m / (1.0 - ADAM_B1 ** ADAM_STEP)
    v_hat = v / (1.0 - ADAM_B2 ** ADAM_STEP)
    return -ADAM_LR * (m_hat / (jnp.sqrt(v_hat) + ADAM_EPS) + ADAM_WD * w), m, v


def _adamw(w, g, m, v, rb, name):
    rows, cols = w.shape

    def body(w_ref, g_ref, m_ref, v_ref, d_ref, nm_ref, nv_ref):
        d_ref[...], nm_ref[...], nv_ref[...] = _adam_math(w_ref[...], g_ref[...], m_ref[...], v_ref[...])

    spec = pl.BlockSpec((rb, cols), lambda i: (i, 0))
    return pl.pallas_call(body, grid=(rows // rb,), name=name, in_specs=[spec] * 4, out_specs=[spec] * 3,
                          out_shape=[jax.ShapeDtypeStruct(w.shape, F32)] * 3, compiler_params=_cp("parallel"))(w, g, m, v)


PACK = (("b_ada", 3 * D_MODEL), ("g_pre", D_MODEL), ("g_post", D_MODEL), ("gn_g", 512), ("qn_g", 64), ("kn_g", 64),
        ("w_dec_f", 4), ("w_dec_b", 4), ("loss", 1))
PACK_OFFSETS = {}
PACK_WIDTH = 0
for _name, _n in PACK:
    PACK_OFFSETS[_name] = PACK_WIDTH
    PACK_WIDTH += -(-_n // 128) * 128
SMALL_PARAMS = tuple(name for name, _ in PACK if name != "loss")


def _pack(parts):
    cols = []
    for name, n in PACK:
        pad = -(-n // 128) * 128 - n
        cols.append(parts[name].reshape(1, n).astype(F32))
        if pad:
            cols.append(jnp.zeros((1, pad), F32))
    return jnp.concatenate(cols, axis=1)


def _small_reduce(vec, cs, deps):
    ncol = 3 * D_MODEL // N_CHIP

    def body(vec_ref, cs_ref, *rest):
        tot_ref, gwa_ref, gat, ssem, rsem = rest[-5:]
        me = _dev_index(_coords())
        chip = me // 2
        gat[me] = vec_ref[...]
        sends = []
        for k in range(1, N_DEV):
            cp = _remote(vec_ref, gat.at[me], ssem.at[k - 1], rsem.at[k - 1], _peer(k))
            cp.start()
            sends.append(cp)
        for k in range(1, N_DEV):
            slot = gat.at[_dev_index(_peer(k))]
            _remote(slot, slot, ssem.at[k - 1], rsem.at[k - 1], _peer(k)).wait_recv()
        rows = [gat[d] for d in range(N_DEV)]
        tot = rows[0]
        for d in range(1, N_DEV):
            tot = tot + rows[d]
        tot_ref[...] = tot
        cs = cs_ref[...]
        ca_t = jnp.transpose(jnp.concatenate([cs * _sigmoid(cs), jnp.zeros((128 - N_DEV, D_MODEL), F32)], axis=0))
        dm = jnp.concatenate(rows + [jnp.zeros((128 - N_DEV, PACK_WIDTH), F32)], axis=0)
        for jj in range(N_CHIP):
            @pl.when(chip == jj)
            def _():
                gwa_ref[...] = _nn(ca_t, dm[:, ncol * jj:ncol * (jj + 1)], precision=HIGHEST)
        for cp in sends:
            cp.wait_send()

    return pl.pallas_call(
        body, name="small_reduce", in_specs=[VMEM_SPEC, VMEM_SPEC] + [HBM_SPEC] * len(deps), out_specs=[VMEM_SPEC] * 2,
        out_shape=[jax.ShapeDtypeStruct((1, PACK_WIDTH), F32), jax.ShapeDtypeStruct((D_MODEL, ncol), F32)],
        scratch_shapes=[pltpu.VMEM((N_DEV, 1, PACK_WIDTH), F32), pltpu.SemaphoreType.DMA((7,)), pltpu.SemaphoreType.DMA((7,))],
        compiler_params=_cp())(vec, cs, *deps)


def _small_adamw(tot, given):
    sizes = dict(PACK)
    np_ = len(SMALL_PARAMS)

    def body(*refs):
        tot_ref = refs[0]
        wmv = refs[1:1 + 3 * np_]
        outs = refs[1 + 3 * np_:1 + 7 * np_]
        loss_ref = refs[-1]
        for i, name in enumerate(SMALL_PARAMS):
            off, n = PACK_OFFSETS[name], sizes[name]
            g = tot_ref[:, off:off + n]
            w_ref, m_ref, v_ref = wmv[3 * i:3 * i + 3]
            outs[i][...] = g
            outs[np_ + i][...], outs[2 * np_ + i][...], outs[3 * np_ + i][...] = _adam_math(w_ref[...], g, m_ref[...], v_ref[...])
        loss_ref[...] = tot_ref[:, PACK_OFFSETS["loss"]:PACK_OFFSETS["loss"] + 1]

    flat = [a for name in SMALL_PARAMS for a in given[name]]
    shapes = [jax.ShapeDtypeStruct((1, sizes[name]), F32) for name in SMALL_PARAMS]
    res = pl.pallas_call(body, name="small_adamw", in_specs=[VMEM_SPEC] * (1 + 3 * np_), out_specs=[VMEM_SPEC] * (4 * np_ + 1),
                         out_shape=shapes * 4 + [jax.ShapeDtypeStruct((1, 1), F32)], compiler_params=_cp())(tot, *flat)
    return [dict(zip(SMALL_PARAMS, res[k * np_:(k + 1) * np_])) for k in range(4)], res[-1]


def kernel(x, c, w_ada, b_ada, g_pre, w_in, qn_g, kn_g, w_dec_f, w_dec_b, gn_g, w_pa, w_pr, w_out, g_post, loss_target, m_w_ada, m_b_ada, m_g_pre, m_w_in, m_qn_g, m_kn_g, m_w_dec_f, m_w_dec_b, m_gn_g, m_w_pa, m_w_pr, m_w_out, m_g_post, v_w_ada, v_b_ada, v_g_pre, v_w_in, v_qn_g, v_kn_g, v_w_dec_f, v_w_dec_b, v_gn_g, v_w_pa, v_w_pr, v_w_out, v_g_post):
    shards = (w_in[0].T.astype(MXU_DTYPE), jnp.concatenate([w_pa[0].T, w_pr[0].T], axis=1).astype(MXU_DTYPE),
              w_out[0].astype(MXU_DTYPE))
    cs, mod, w_attn = _mod_and_attn_rows(c, w_ada[0], b_ada.reshape(N_CHIP, 3 * D_MODEL // N_CHIP), shards[0])
    started = {}

    def rest_start(dep):
        started["rest"] = _gather_rest_start(shards, _place_own(shards), dep)
        return started["rest"][-1]

    def rest_wait(after):
        return _gather_rest_wait(started["rest"], after)

    kinds = {"early": (1, 2), "late": (0,)}

    def send(name, arrays):
        started[name] = _scatter_start(arrays, kinds[name], "grad_scatter_" + name)
        return started[name][-1]

    grad_x, _, _, _, small = _local_step(x[0], loss_target[0], mod, g_pre, qn_g, kn_g, w_dec_f, w_dec_b,
                                         gn_g, g_post, w_attn, rest_start, rest_wait, send=send)
    small = dict(small)
    small["b_ada"] = small.pop("dmod")
    given = dict(b_ada=(b_ada, m_b_ada, v_b_ada), g_pre=(g_pre, m_g_pre, v_g_pre), g_post=(g_post, m_g_post, v_g_post),
                 gn_g=(gn_g, m_gn_g, v_gn_g), qn_g=(qn_g, m_qn_g, v_qn_g), kn_g=(kn_g, m_kn_g, v_kn_g),
                 w_dec_f=(w_dec_f, m_w_dec_f, v_w_dec_f), w_dec_b=(w_dec_b, m_w_dec_b, v_w_dec_b))
    grads, deltas, new_m, new_v = {}, {}, {}, {}

    def update(name, w, g, m, v, back):
        d, nm, nv = _adamw(w, g, m, v, w.shape[0] // 4, "adamw_" + name)
        grads[name], deltas[name], new_m[name], new_v[name] = back(g), back(d), back(nm), back(nv)
        return d

    lead = lambda a: a[None]
    (g_pt, g_out), (z_pt, z_out) = _scatter_wait(started["early"], kinds["early"], grad_x, "grad_scatter_early_wait")
    r_pt, r_out = _grad_finish((g_pt, g_out), (z_pt, z_out), kinds["early"], "grad_finish_early")
    early = (update("w_pa", w_pa[0], r_pt[:, :512].T, m_w_pa[0], v_w_pa[0], lead),
             update("w_pr", w_pr[0], r_pt[:, 512:].T, m_w_pr[0], v_w_pr[0], lead),
             update("w_out", w_out[0], r_out, m_w_out[0], v_w_out[0], lead))
    tot, g_w_ada = _small_reduce(_pack(small), cs.reshape(N_DEV, D_MODEL), early)
    small_parts, loss = _small_adamw(tot, given)
    for dst, part in zip((grads, deltas, new_m, new_v), small_parts):
        dst.update(part)
    last = update("w_ada", w_ada[0], g_w_ada, m_w_ada[0], v_w_ada[0], lead)

    (g_in_t,), (z_in,) = _scatter_wait(started["late"], kinds["late"], last, "grad_scatter_late_wait")
    (r_in,) = _grad_finish((g_in_t,), (z_in,), kinds["late"], "grad_finish_late")
    tr = lambda a: a[0].T
    update("w_in", tr(w_in), r_in, tr(m_w_in), tr(v_w_in), lambda a: a.T[None])
    order = ("w_ada", "b_ada", "g_pre", "w_in", "qn_g", "kn_g", "w_dec_f", "w_dec_b", "gn_g", "w_pa", "w_pr", "w_out", "g_post")
    return (loss[0, 0], grad_x[None], *[grads[n] for n in order], *[deltas[n] for n in order],
            *[new_m[n] for n in order], *[new_v[n] for n in order])
```

```python
import jax
import jax.numpy as jnp
import numpy as np
from jax import lax
from jax.experimental import pallas as pl
from jax.experimental.pallas import tpu as pltpu

F32 = jnp.float32
BF16 = jnp.bfloat16
MXU_DTYPE = jnp.bfloat16

D_MODEL = 1024
GRID_W = 64
HEAD_DIM = 64
ROPE_THETA = 10000.0
EPS = 1e-6
RET_HEADS = 4
RET_CHUNK = 256
RET_GROUP = 4
IN_WIDTH = 4864
QA, KA, VA, ZA, QR, KR, VR, ZR, GL = 0, 512, 640, 768, 1280, 1536, 1792, 2304, 2816
ATTN_COLS = ZA
ZA_B, QR_B, KR_B, VR_B, ZR_B, GL_B = (c - ATTN_COLS for c in (ZA, QR, KR, VR, ZR, GL))

ADAM_LR, ADAM_B1, ADAM_B2, ADAM_EPS, ADAM_WD, ADAM_STEP = 0.001, 0.9, 0.999, 1e-08, 0.01, 10

VMEM_LIMIT = 56 * 1024 * 1024
MESH = pl.DeviceIdType.MESH
HIGHEST = lax.Precision.HIGHEST
LOG2E = 1.4426950408889634
LN2 = 0.6931471805599453


def _cp(*sem, **kw):
    if sem:
        kw["dimension_semantics"] = sem
    return pltpu.CompilerParams(vmem_limit_bytes=VMEM_LIMIT, **kw)


def _nn(a, b, **kw):
    return lax.dot_general(a, b, (((1,), (0,)), ((), ())), preferred_element_type=F32, **kw)


def _nt(a, b):
    return lax.dot_general(a, b, (((1,), (1,)), ((), ())), preferred_element_type=F32)


def _tn(a, b):
    return lax.dot_general(a, b, (((0,), (0,)), ((), ())), preferred_element_type=F32)


def _mx(x):
    return x.astype(MXU_DTYPE)


def _lane(shape):
    return lax.broadcasted_iota(jnp.int32, shape, 1)


def _sigmoid(z):
    return 1.0 / (1.0 + jnp.exp(-z))


def _rowsum128(x):
    s = jnp.sum(x, axis=0, keepdims=True)
    out = s[:, 0:128]
    for k in range(1, x.shape[1] // 128):
        out = out + s[:, 128 * k:128 * (k + 1)]
    return out


def _swap16(x):
    return jnp.where((_lane(x.shape) & 16) == 0, pltpu.roll(x, 112, 1), pltpu.roll(x, 16, 1))


def _rope(x, cos, sin):
    return x * cos + _swap16(x) * sin


def _rope_t(d, cos, sin):
    return d * cos + _swap16(d * sin)


def _blockdiag64():
    r = lax.broadcasted_iota(jnp.int32, (128, 128), 0) // 64
    c = lax.broadcasted_iota(jnp.int32, (128, 128), 1) // 64
    return (r == c).astype(BF16)


def _seg64(x, m):
    hi = x.astype(BF16)
    lo = (x - hi.astype(F32)).astype(BF16)
    return _nn(hi, m) + _nn(lo, m)


def _rope_tables(seq):
    t = np.arange(seq)
    row = (t // GRID_W).astype(np.float32)
    col = (t % GRID_W).astype(np.float32)
    half = HEAD_DIM // 2
    inv_freq = (np.float32(ROPE_THETA) ** (-np.arange(0, half, 2, dtype=np.float32) / np.float32(half))).astype(np.float32)
    ar = (row[:, None] * inv_freq[None, :]).astype(np.float32).astype(np.float64)
    ac = (col[:, None] * inv_freq[None, :]).astype(np.float32).astype(np.float64)
    cr, sr, cc, sc = np.cos(ar), np.sin(ar), np.cos(ac), np.sin(ac)
    cos64 = np.concatenate([cr, cr, cc, cc], axis=1)
    sin64 = np.concatenate([-sr, sr, -sc, sc], axis=1)
    return jnp.asarray(np.tile(cos64, (1, 2)), F32), jnp.asarray(np.tile(sin64, (1, 2)), F32)


def _attn_inputs(x, mod, g_pre, w_attn, cos, sin, qg2, kg2, own=()):
    seq = x.shape[0]
    bs = 512
    nt = len(own)
    nstep = seq // bs

    def body(x_ref, mod_ref, g_ref, w_ref, cos_ref, sin_ref, qg_ref, kg_ref, *rest):
        srcs, (h_ref, pa_ref, qt_ref, kd_ref, vd_ref) = rest[:nt], rest[nt:nt + 5]
        lands, stage = rest[nt + 5:2 * nt + 5], rest[2 * nt + 5:3 * nt + 5]
        if nt:
            osem = rest[3 * nt + 5]
            step = pl.program_id(0)
            cx, cy, _ = _coords()
            chip = 2 * cx + cy
            loads = [pltpu.make_async_copy(srcs[t], stage[t], osem.at[t]) for t in range(nt)]
            stores = [pltpu.make_async_copy(stage[t], _rows(lands[t], chip * SHARD_ROWS[t], SHARD_ROWS[t]), osem.at[nt + t])
                      for t in range(nt)]

            @pl.when(step == 0)
            def _():
                for cp in loads:
                    cp.start()

            @pl.when(step == min(1, nstep - 1))
            def _():
                for cp in loads:
                    cp.wait()
                for cp in stores:
                    cp.start()

        xv = x_ref[...]
        r = lax.rsqrt(jnp.mean(xv * xv, axis=-1, keepdims=True) + EPS)
        shift = mod_ref[:, 0:D_MODEL]
        scale = mod_ref[:, D_MODEL:2 * D_MODEL]
        hb = ((xv * r) * g_ref[...] * (1.0 + scale) + shift).astype(h_ref.dtype)
        h_ref[...] = hb
        p = _nt(hb, w_ref[...])
        pa_ref[...] = p
        m = _blockdiag64()
        cs, sn = cos_ref[...], sin_ref[...]
        lo = _lane((bs, 128)) < 64
        for pr in range(4):
            q = p[:, QA + 128 * pr:QA + 128 * (pr + 1)]
            r = lax.rsqrt(_seg64(q * q, m) * (1.0 / 64) + EPS)
            qt_ref[:, 128 * pr:128 * (pr + 1)] = (_rope(q * r * qg_ref[...], cs, sn) * (0.125 * LOG2E)).astype(qt_ref.dtype)
        k = p[:, KA:KA + 128]
        r = lax.rsqrt(_seg64(k * k, m) * (1.0 / 64) + EPS)
        kr = _rope(k * r * kg_ref[...], cs, sn)
        ksw = pltpu.roll(kr, 64, 1)
        kd_ref[0] = jnp.where(lo, kr, ksw).astype(kd_ref.dtype)
        kd_ref[1] = jnp.where(lo, ksw, kr).astype(kd_ref.dtype)
        v = p[:, VA:VA + 128]
        vsw = pltpu.roll(v, 64, 1)
        vd_ref[0] = jnp.where(lo, v, vsw).astype(vd_ref.dtype)
        vd_ref[1] = jnp.where(lo, vsw, v).astype(vd_ref.dtype)

        if nt:
            @pl.when(step == nstep - 1)
            def _():
                for cp in stores:
                    cp.wait()

    row = lambda w: pl.BlockSpec((bs, w), lambda i: (i, 0))
    fix = lambda a, b: pl.BlockSpec((a, b), lambda i: (0, 0))
    dup = pl.BlockSpec((2, bs, 128), lambda i: (0, i, 0))
    sds = jax.ShapeDtypeStruct
    anywhere = pl.BlockSpec(memory_space=pl.ANY)
    return pl.pallas_call(
        body, grid=(nstep,), name="attn_inputs",
        in_specs=[row(D_MODEL), fix(1, 3 * D_MODEL), fix(1, D_MODEL), fix(ATTN_COLS, D_MODEL), row(128), row(128), fix(1, 128), fix(1, 128)]
        + [anywhere] * nt,
        out_specs=[row(D_MODEL), row(ATTN_COLS), row(512), dup, dup] + [anywhere] * nt,
        out_shape=[sds((seq, D_MODEL), MXU_DTYPE), sds((seq, ATTN_COLS), F32), sds((seq, 512), MXU_DTYPE),
                   sds((2, seq, 128), MXU_DTYPE), sds((2, seq, 128), MXU_DTYPE)]
        + [sds((N_CHIP * SHARD_ROWS[t], s.shape[1]), s.dtype) for t, s in enumerate(own)],
        scratch_shapes=[pltpu.VMEM(s.shape, s.dtype) for s in own] + ([pltpu.SemaphoreType.DMA((2 * nt,))] if nt else []),
        compiler_params=_cp("arbitrary"))(x, mod, g_pre, w_attn, cos, sin, qg2, kg2, *own)


def _in_proj_dx(x, dp, win_t, dout, mod, g_pre, dep=None):
    seq = x.shape[0]
    bs = 512
    deps, dep_specs = _dep_args(dep, 1)

    def body(x_ref, dp_ref, w_ref, dout_ref, mod_ref, g_ref, *rest):
        gx_ref, dshift_ref, dscale_ref, dg_ref = rest[-4:]

        @pl.when(pl.program_id(0) == 0)
        def _():
            dshift_ref[...] = jnp.zeros_like(dshift_ref)
            dscale_ref[...] = jnp.zeros_like(dscale_ref)
            dg_ref[...] = jnp.zeros_like(dg_ref)

        xv = x_ref[...]
        dh = _nn(dp_ref[...], w_ref[...])
        g = g_ref[...]
        r = lax.rsqrt(jnp.mean(xv * xv, axis=-1, keepdims=True) + EPS)
        xn = xv * r
        scale = mod_ref[:, D_MODEL:2 * D_MODEL]
        dshift_ref[...] += jnp.sum(dh, axis=0, keepdims=True)
        dscale_ref[...] += jnp.sum(dh * (xn * g), axis=0, keepdims=True)
        da = dh * (1.0 + scale)
        dg_ref[...] += jnp.sum(da * xn, axis=0, keepdims=True)
        dxn = da * g
        dx = r * (dxn - xn * jnp.mean(dxn * xn, axis=-1, keepdims=True))
        gx_ref[...] = dout_ref[...] + dx

    row = pl.BlockSpec((bs, D_MODEL), lambda i: (i, 0))
    vec = pl.BlockSpec((1, D_MODEL), lambda i: (0, 0))
    return pl.pallas_call(
        body, grid=(seq // bs,), name="in_proj_dx",
        in_specs=[row, pl.BlockSpec((bs, IN_WIDTH), lambda i: (i, 0)), pl.BlockSpec((IN_WIDTH, D_MODEL), lambda i: (0, 0)), row,
                  pl.BlockSpec((1, 3 * D_MODEL), lambda i: (0, 0)), vec] + dep_specs,
        out_specs=[row, vec, vec, vec],
        out_shape=[jax.ShapeDtypeStruct((seq, D_MODEL), F32)] + [jax.ShapeDtypeStruct((1, D_MODEL), F32)] * 3,
        compiler_params=_cp("arbitrary"))(x, dp, win_t, dout, mod, g_pre, *deps)


def _dep_args(dep, grid_rank):
    if dep is None:
        return [], []
    return [dep], [pl.BlockSpec(dep.shape, lambda *_: (0,) * dep.ndim)]


def _matmul(a, b, *, ta, tb, tm, tn, out_dtype, name, dep=None):
    kdim = a.shape[0] if ta else a.shape[1]
    m = a.shape[1] if ta else a.shape[0]
    n = b.shape[0] if tb else b.shape[1]
    assert m % tm == 0 and n % tn == 0
    deps, dep_specs = _dep_args(dep, 2)

    def body(a_ref, b_ref, *rest):
        o_ref = rest[-1]
        dims = (((0 if ta else 1,), (1 if tb else 0,)), ((), ()))
        o_ref[...] = lax.dot_general(a_ref[...], b_ref[...], dims, preferred_element_type=F32).astype(o_ref.dtype)

    a_spec = pl.BlockSpec((kdim, tm), lambda j, i: (0, i)) if ta else pl.BlockSpec((tm, kdim), lambda j, i: (i, 0))
    b_spec = pl.BlockSpec((tn, kdim), lambda j, i: (j, 0)) if tb else pl.BlockSpec((kdim, tn), lambda j, i: (0, j))
    return pl.pallas_call(
        body, grid=(n // tn, m // tm), name=name, in_specs=[a_spec, b_spec] + dep_specs,
        out_specs=pl.BlockSpec((tm, tn), lambda j, i: (i, j)),
        out_shape=jax.ShapeDtypeStruct((m, n), out_dtype), compiler_params=_cp("parallel", "parallel"))(a, b, *deps)


def _in_proj_rest(h, win_t, cos, sin):
    seq = h.shape[0]
    tm = min(1024, seq)
    ncols = IN_WIDTH - ATTN_COLS
    tn = ncols // 2
    assert ZR_B <= tn and ATTN_COLS % 128 == 0 and tn % 128 == 0

    def body(h_ref, w_ref, cos_ref, sin_ref, p_ref, rq_ref, rk_ref, rv_ref):
        p = _nt(h_ref[...], w_ref[...])
        p_ref[...] = p

        @pl.when(pl.program_id(0) == 0)
        def _():
            cs, sn = cos_ref[...], sin_ref[...]
            for pr in range(2):
                sl = slice(128 * pr, 128 * (pr + 1))
                rq_ref[:, sl] = _rope(p[:, QR_B + 128 * pr:QR_B + 128 * (pr + 1)], cs, sn).astype(rq_ref.dtype)
                rk_ref[:, sl] = (_rope(p[:, KR_B + 128 * pr:KR_B + 128 * (pr + 1)], cs, sn) * 0.125).astype(rk_ref.dtype)
            rv_ref[...] = p[:, VR_B:VR_B + 512].astype(rv_ref.dtype)

    nrow = seq // tm
    row = lambda w: pl.BlockSpec((tm, w), lambda j, i: (i, 0))
    park = lambda w: pl.BlockSpec((tm, w), lambda j, i: (jnp.where(j == 0, i, nrow - 1), 0))
    sds = jax.ShapeDtypeStruct
    return pl.pallas_call(
        body, grid=(2, nrow), name="in_proj_rest",
        in_specs=[row(D_MODEL), pl.BlockSpec((pl.Element(tn), pl.Element(D_MODEL)),
                                             lambda j, i: (pl.multiple_of(ATTN_COLS + j * tn, 128), 0)), row(128), row(128)],
        out_specs=[pl.BlockSpec((tm, tn), lambda j, i: (i, j)), park(256), park(256), park(512)],
        out_shape=[sds((seq, ncols), F32), sds((seq, 256), MXU_DTYPE), sds((seq, 256), MXU_DTYPE), sds((seq, 512), MXU_DTYPE)],
        compiler_params=_cp("arbitrary", "arbitrary"))(h, win_t, cos, sin)


def _halves(kd):
    lo = _lane(kd.shape) < 64
    z = jnp.zeros_like(kd)
    return jnp.concatenate([jnp.where(lo, kd, z), jnp.where(lo, z, kd)], axis=0)


def _attn_fwd(qt, kd, vd, dep=None):
    seq = qt.shape[0]
    bq, bk = min(2048, seq), min(1024, seq)
    nk = seq // bk
    deps, dep_specs = _dep_args(dep, 2)

    def body(q_ref, k_ref, v_ref, *rest):
        o_ref, lse_ref, m_scr, l_scr, acc = rest[-5:]
        j = pl.program_id(1)
        m_scr[...] = jnp.full_like(m_scr, -jnp.inf)
        l_scr[...] = jnp.zeros_like(l_scr)
        acc[...] = jnp.zeros_like(acc)
        sub = min(512, bq)
        nsub = bq // sub
        lo = _lane((sub, 128)) < 64

        def kv_block(n, carry):
            rows = pl.ds(pl.multiple_of(n * bk, bk), bk)
            k2, v2 = _halves(k_ref[rows, :]), _halves(v_ref[rows, :])
            for pr in range(2):
                for hf in range(nsub):
                    qr = slice(hf * sub, (hf + 1) * sub)
                    s2 = _nt(q_ref[qr, 128 * pr:128 * (pr + 1)], k2)
                    ps, alphas = [], []
                    for e in range(2):
                        g = 2 * pr + e
                        s = s2[:, e * bk:(e + 1) * bk]
                        m_prev = m_scr[g, qr]
                        m_next = jnp.maximum(m_prev, jnp.max(s, axis=1, keepdims=True))
                        alpha = jnp.exp2(m_prev - m_next)
                        pe = jnp.exp2(s - jnp.tile(m_next, (1, bk // 128)))
                        l_scr[g, qr] = alpha * l_scr[g, qr] + jnp.sum(pe, axis=1, keepdims=True)
                        m_scr[g, qr] = m_next
                        ps.append(_mx(pe))
                        alphas.append(alpha)
                    o2 = _nn(jnp.concatenate(ps, axis=1), v2)
                    sl = slice(128 * pr, 128 * (pr + 1))
                    acc[qr, sl] = acc[qr, sl] * jnp.where(lo, alphas[0], alphas[1]) + o2
            return carry

        lax.fori_loop(0, nk, kv_block, 0)
        lo_all = _lane((bq, 128)) < 64
        for pr in range(2):
            sl = slice(128 * pr, 128 * (pr + 1))
            o_ref[:, sl] = acc[:, sl] / jnp.where(lo_all, l_scr[2 * pr], l_scr[2 * pr + 1])
        for g in range(4):
            lse = jnp.transpose(m_scr[g] + jnp.log2(l_scr[g]))
            lse_ref[pl.ds(4 * j + g, 1), :] = lse[0:1, :]

    return pl.pallas_call(
        body, grid=(seq // bq, 2), name="attn_fwd",
        in_specs=[pl.BlockSpec((bq, 256), lambda i, j: (i, j)), pl.BlockSpec((None, seq, 128), lambda i, j: (j, 0, 0)),
                  pl.BlockSpec((None, seq, 128), lambda i, j: (j, 0, 0))] + dep_specs,
        out_specs=[pl.BlockSpec((bq, 256), lambda i, j: (i, j)), pl.BlockSpec((8, bq), lambda i, j: (0, i))],
        out_shape=[jax.ShapeDtypeStruct((seq, 512), F32), jax.ShapeDtypeStruct((8, seq), F32)],
        scratch_shapes=[pltpu.VMEM((4, bq, 128), F32), pltpu.VMEM((4, bq, 128), F32), pltpu.VMEM((bq, 256), F32)],
        compiler_params=_cp("parallel", "arbitrary"))(qt, kd, vd, *deps)


def _attn_bwd(qt, kd, vd, do, lse, delta, dep=None):
    seq = qt.shape[0]
    bq, bk = min(1024, seq), min(1024, seq)
    nq, nk = seq // bq, seq // bk
    deps, dep_specs = _dep_args(dep, 3)

    def body(q_ref, do_ref, k_ref, v_ref, lse_ref, del_ref, *rest):
        dq_ref, dk_ref, dv_ref = rest[-3:]
        j, i = pl.program_id(0), pl.program_id(1)
        dq_ref[...] = jnp.zeros_like(dq_ref)

        @pl.when(i == 0)
        def _():
            dk_ref[...] = jnp.zeros_like(dk_ref)
            dv_ref[...] = jnp.zeros_like(dv_ref)

        lo = _lane((bk, 128)) < 64
        big = lambda r: jnp.concatenate([jnp.broadcast_to(r[0], (bk, bq)), jnp.broadcast_to(r[1], (bk, bq))], axis=0)

        def kv_block(n, carry):
            rows = pl.ds(pl.multiple_of(n * bk, bk), bk)
            k2, v2 = _halves(k_ref[rows, :]), _halves(v_ref[rows, :])
            for pr in range(2):
                sl = slice(128 * pr, 128 * (pr + 1))
                qp, dop = q_ref[:, sl], do_ref[:, sl]
                s_t = _nt(k2, qp)
                dp_t = _nt(v2, dop)
                ls = [lse_ref[pl.ds(4 * j + 2 * pr + e, 1), :] for e in range(2)]
                dl = [del_ref[pl.ds(4 * j + 2 * pr + e, 1), :] for e in range(2)]
                p_t = jnp.exp2(s_t - big(ls))
                ds_t = _mx(p_t * (dp_t - big(dl)))
                rv = _nn(_mx(p_t), dop)
                rk = _nn(ds_t, qp) * LN2
                dv_ref[rows, :] += jnp.where(lo, rv[:bk], rv[bk:])
                dk_ref[rows, :] += jnp.where(lo, rk[:bk], rk[bk:])
                dq_ref[:, sl] += _tn(ds_t, k2)
            return carry

        lax.fori_loop(0, nk, kv_block, 0)

    return pl.pallas_call(
        body, grid=(2, nq), name="attn_bwd",
        in_specs=[pl.BlockSpec((bq, 256), lambda j, i: (i, j)), pl.BlockSpec((bq, 256), lambda j, i: (i, j)),
                  pl.BlockSpec((None, seq, 128), lambda j, i: (j, 0, 0)), pl.BlockSpec((None, seq, 128), lambda j, i: (j, 0, 0)),
                  pl.BlockSpec((8, bq), lambda j, i: (0, i)), pl.BlockSpec((8, bq), lambda j, i: (0, i))] + dep_specs,
        out_specs=[pl.BlockSpec((bq, 256), lambda j, i: (i, j)), pl.BlockSpec((None, seq, 128), lambda j, i: (j, 0, 0)),
                   pl.BlockSpec((None, seq, 128), lambda j, i: (j, 0, 0))],
        out_shape=[jax.ShapeDtypeStruct((seq, 512), F32), jax.ShapeDtypeStruct((2, seq, 128), F32),
                   jax.ShapeDtypeStruct((2, seq, 128), F32)],
        compiler_params=_cp("arbitrary", "arbitrary"))(qt, do, kd, vd, lse, delta, *deps)


def _ret_tables(lg_f, lg_b, c):
    idx = jnp.arange(c, dtype=F32)
    diff = idx[:, None] - idx[None, :]
    a, b = lg_f[:, None, None], lg_b[:, None, None]
    low, up = (diff >= 0)[None], (diff < 0)[None]
    dmat = jnp.where(low, jnp.exp(a * jnp.maximum(diff, 0.0)[None]), jnp.exp(b * jnp.maximum(-diff, 0.0)[None]))
    dda = jnp.where(low, diff[None] * jnp.exp(a * jnp.maximum(diff, 0.0)[None]), 0.0)
    ddb = jnp.where(up, -diff[None] * jnp.exp(b * jnp.maximum(-diff, 0.0)[None]), 0.0)
    rep = lambda w: jnp.broadcast_to(w[:, :, None], (RET_HEADS, c, 128))
    wqf = rep(jnp.exp(lg_f[:, None] * (idx + 1.0)[None]))
    wqb = rep(jnp.exp(lg_b[:, None] * (c - idx)[None]))
    wkf = rep(jnp.exp(lg_f[:, None] * (c - 1.0 - idx)[None]))
    wkb = rep(jnp.exp(lg_b[:, None] * idx[None]))
    cdf, cdb = jnp.exp(lg_f * c), jnp.exp(lg_b * c)
    dec_fb = jnp.broadcast_to(jnp.concatenate([cdf, cdb])[:, None], (8, 128))
    dec_bf = jnp.broadcast_to(jnp.concatenate([cdb, cdf])[:, None], (8, 128))
    return dict(dmat=dmat, dda=dda, ddb=ddb, wqf=wqf, wqb=wqb, wkf=wkf, wkb=wkb, dec_fb=dec_fb, dec_bf=dec_bf)


def _ret_states(xk, yv, w_fwd, w_rev, dec, name):
    seq = xk.shape[0]
    c = RET_CHUNK
    nc = seq // c
    grp = min(RET_GROUP, nc)
    ns = nc // grp

    def body(xf_ref, yf_ref, xr_ref, yr_ref, wf_ref, wr_ref, dec_ref, sf_ref, sr_ref, st_f, st_r):
        @pl.when(pl.program_id(0) == 0)
        def _():
            st_f[...] = jnp.zeros_like(st_f)
            st_r[...] = jnp.zeros_like(st_r)

        lane = _lane((c, 128))

        def chunk(g, carry):
            for x_ref, y_ref, w_ref, s_ref, st, base, gg in ((xf_ref, yf_ref, wf_ref, sf_ref, st_f, 0, g),
                                                             (xr_ref, yr_ref, wr_ref, sr_ref, st_r, 4, grp - 1 - g)):
                rows = pl.ds(pl.multiple_of(gg * c, c), c)
                for h in range(RET_HEADS):
                    pr, e = h // 2, h % 2
                    half = (lane < 64) if e == 0 else (lane >= 64)
                    s_ref[gg, h] = st[h].astype(s_ref.dtype)
                    xm = jnp.where(half, x_ref[rows, 128 * pr:128 * (pr + 1)].astype(F32) * w_ref[h], 0.0)
                    st[h] = st[h] * dec_ref[base + h:base + h + 1, :] + _tn(_mx(xm), _mx(y_ref[rows, 128 * h:128 * (h + 1)]))
            return carry

        lax.fori_loop(0, grp, chunk, 0, unroll=True)

    xs = lambda f: pl.BlockSpec((grp * c, 256), f)
    ys = lambda f: pl.BlockSpec((grp * c, 512), f)
    fwd, rev = (lambda n: (n, 0)), (lambda n: (ns - 1 - n, 0))
    wsp = pl.BlockSpec((RET_HEADS, c, 128), lambda n: (0, 0, 0))
    return pl.pallas_call(
        body, grid=(ns,), name=name,
        in_specs=[xs(fwd), ys(fwd), xs(rev), ys(rev), wsp, wsp, pl.BlockSpec((8, 128), lambda n: (0, 0))],
        out_specs=[pl.BlockSpec((grp, RET_HEADS, 128, 128), lambda n: (n, 0, 0, 0)),
                   pl.BlockSpec((grp, RET_HEADS, 128, 128), lambda n: (ns - 1 - n, 0, 0, 0))],
        out_shape=[jax.ShapeDtypeStruct((nc, RET_HEADS, 128, 128), MXU_DTYPE)] * 2,
        scratch_shapes=[pltpu.VMEM((RET_HEADS, 128, 128), F32), pltpu.VMEM((RET_HEADS, 128, 128), F32)],
        compiler_params=_cp("arbitrary"))(xk, yv, xk, yv, w_fwd, w_rev, dec)


def _ret_fwd(rq, rk, rv, rf, rb, tb):
    seq = rq.shape[0]
    c = RET_CHUNK
    grp = min(RET_GROUP, seq // c)

    def body(q_ref, k_ref, v_ref, rf_ref, rb_ref, d_ref, wqf_ref, wqb_ref, o_ref):
        lane = _lane((c, 128))

        def chunk(g, carry):
            rows = pl.ds(pl.multiple_of(g * c, c), c)
            for h in range(RET_HEADS):
                pr, e = h // 2, h % 2
                half = (lane < 64) if e == 0 else (lane >= 64)
                sl = slice(128 * pr, 128 * (pr + 1))
                qp, kp = q_ref[rows, sl], k_ref[rows, sl]
                km = jnp.where(half, kp, jnp.zeros_like(kp))
                sd = _mx(_nt(qp, km) * d_ref[h])
                qf = qp.astype(F32)
                o = _nn(sd, v_ref[rows, 128 * h:128 * (h + 1)])
                o = o + _nn(_mx(qf * wqf_ref[h]), rf_ref[g, h]) + _nn(_mx(qf * wqb_ref[h]), rb_ref[g, h])
                o_ref[rows, 128 * h:128 * (h + 1)] = o
            return carry

        lax.fori_loop(0, grp, chunk, 0, unroll=True)

    st = pl.BlockSpec((grp, RET_HEADS, 128, 128), lambda n: (n, 0, 0, 0))
    wsp = pl.BlockSpec((RET_HEADS, c, 128), lambda n: (0, 0, 0))
    return pl.pallas_call(
        body, grid=(seq // (grp * c),), name="ret_fwd",
        in_specs=[pl.BlockSpec((grp * c, 256), lambda n: (n, 0)), pl.BlockSpec((grp * c, 256), lambda n: (n, 0)),
                  pl.BlockSpec((grp * c, 512), lambda n: (n, 0)), st, st, pl.BlockSpec((RET_HEADS, c, c), lambda n: (0, 0, 0)), wsp, wsp],
        out_specs=pl.BlockSpec((grp * c, 512), lambda n: (n, 0)),
        out_shape=jax.ShapeDtypeStruct((seq, 512), F32), compiler_params=_cp("parallel"))(
            rq, rk, rv, rf, rb, tb["dmat"], tb["wqf"], tb["wqb"])


def _ret_bwd(rq, rk, rv, do, rf, rb, hf, hb, tb):
    seq = rq.shape[0]
    c = RET_CHUNK
    grp = min(RET_GROUP, seq // c)

    def body(q_ref, k_ref, v_ref, do_ref, rf_ref, rb_ref, hf_ref, hb_ref, d_ref, dda_ref, ddb_ref,
             wqf_ref, wqb_ref, wkf_ref, wkb_ref, cdec_ref, dq_ref, dk_ref, dv_ref, dlg_ref):
        @pl.when(pl.program_id(0) == 0)
        def _():
            dlg_ref[...] = jnp.zeros_like(dlg_ref)

        lane = _lane((c, 128))
        pos = lax.broadcasted_iota(jnp.int32, (c, 128), 0).astype(F32)

        def chunk(g, carry):
            rows = pl.ds(pl.multiple_of(g * c, c), c)
            for pr in range(2):
                sl = slice(128 * pr, 128 * (pr + 1))
                qp, kp = q_ref[rows, sl], k_ref[rows, sl]
                qf, kf = qp.astype(F32), kp.astype(F32)
                dq_acc = jnp.zeros((c, 128), F32)
                dk_acc = jnp.zeros((c, 128), F32)
                for e in range(2):
                    h = 2 * pr + e
                    half = (lane < 64) if e == 0 else (lane >= 64)
                    hs = slice(128 * h, 128 * (h + 1))
                    km = jnp.where(half, kp, jnp.zeros_like(kp))
                    kmf = km.astype(F32)
                    vh, doh = v_ref[rows, hs], do_ref[rows, hs]
                    rfh, rbh, hfh, hbh = rf_ref[g, h], rb_ref[g, h], hf_ref[g, h], hb_ref[g, h]
                    s = _nt(qp, km)
                    dpm = _nt(doh, vh)
                    ds_b = _mx(dpm * d_ref[h])
                    sd_b = _mx(s * d_ref[h])
                    dq_if = wqf_ref[h] * _nt(doh, rfh)
                    dq_ib = wqb_ref[h] * _nt(doh, rbh)
                    dq_acc = dq_acc + _nn(ds_b, km) + dq_if + dq_ib
                    dk_sf = wkf_ref[h] * _nt(vh, hfh)
                    dk_sb = wkb_ref[h] * _nt(vh, hbh)
                    dk_acc = dk_acc + jnp.where(half, _tn(ds_b, qp), 0.0) + dk_sf + dk_sb
                    dv = _tn(sd_b, doh) + _nn(_mx(kmf * wkf_ref[h]), hfh) + _nn(_mx(kmf * wkb_ref[h]), hbh)
                    dv_ref[rows, hs] = dv.astype(dv_ref.dtype)
                    sdp = s * dpm
                    hr_f = hfh.astype(F32) * rfh.astype(F32)
                    hr_b = hbh.astype(F32) * rbh.astype(F32)
                    da = (_rowsum128(sdp * dda_ref[h]) + _rowsum128((pos + 1.0) * qf * dq_if)
                          + _rowsum128((c - 1.0 - pos) * kf * dk_sf) + cdec_ref[h:h + 1, :] * _rowsum128(hr_f))
                    db = (_rowsum128(sdp * ddb_ref[h]) + _rowsum128((c - pos) * qf * dq_ib)
                          + _rowsum128(pos * kf * dk_sb) + cdec_ref[4 + h:5 + h, :] * _rowsum128(hr_b))
                    dlg_ref[h:h + 1, :] += da
                    dlg_ref[4 + h:5 + h, :] += db
                dq_ref[rows, sl] = dq_acc
                dk_ref[rows, sl] = dk_acc
            return carry

        lax.fori_loop(0, grp, chunk, 0, unroll=2)

    st = pl.BlockSpec((grp, RET_HEADS, 128, 128), lambda n: (n, 0, 0, 0))
    wsp = pl.BlockSpec((RET_HEADS, c, 128), lambda n: (0, 0, 0))
    dsp = pl.BlockSpec((RET_HEADS, c, c), lambda n: (0, 0, 0))
    x256 = pl.BlockSpec((grp * c, 256), lambda n: (n, 0))
    x512 = pl.BlockSpec((grp * c, 512), lambda n: (n, 0))
    cdec = tb["dec_fb"] * float(c)
    return pl.pallas_call(
        body, grid=(seq // (grp * c),), name="ret_bwd",
        in_specs=[x256, x256, x512, x512, st, st, st, st, dsp, dsp, dsp, wsp, wsp, wsp, wsp,
                  pl.BlockSpec((8, 128), lambda n: (0, 0))],
        out_specs=[x256, x256, x512, pl.BlockSpec((8, 128), lambda n: (0, 0))],
        out_shape=[jax.ShapeDtypeStruct((seq, 256), F32), jax.ShapeDtypeStruct((seq, 256), F32),
                   jax.ShapeDtypeStruct((seq, 512), MXU_DTYPE), jax.ShapeDtypeStruct((8, 128), F32)],
        compiler_params=_cp("arbitrary"))(
            rq, rk, rv, do, rf, rb, hf, hb, tb["dmat"], tb["dda"], tb["ddb"], tb["wqf"], tb["wqb"], tb["wkf"], tb["wkb"], cdec)


def _tail(x, tgt, o_att, o_ret, p, mod, g_post, gn_g, wpt, wout):
    seq = x.shape[0]
    bs = 256
    nb = seq // bs

    def body(x_ref, t_ref, oa_ref, or_ref, za_ref, zr_ref, gl_ref, mod_ref, gp_ref, gn_ref, wpt_ref, wout_ref,
             dout_ref, dza_ref, dzr_ref, dgl_ref, doa_ref, dor_ref, del_ref, gout_ref, gpt_ref,
             dgate_ref, dgpost_ref, dgn_ref, loss_ref, gout_acc, gpt_acc):
        i = pl.program_id(0)

        @pl.when(i == 0)
        def _():
            gout_acc[...] = jnp.zeros_like(gout_acc)
            gpt_acc[...] = jnp.zeros_like(gpt_acc)
            dgate_ref[...] = jnp.zeros_like(dgate_ref)
            dgpost_ref[...] = jnp.zeros_like(dgpost_ref)
            dgn_ref[...] = jnp.zeros_like(dgn_ref)
            loss_ref[...] = jnp.zeros_like(loss_ref)

        oa, za, zr = oa_ref[...], za_ref[...], zr_ref[...]
        sga, sgr = _sigmoid(za), _sigmoid(zr)
        sza, szr = za * sga, zr * sgr
        ya_b = _mx(oa * sza)
        ons, rstds = [], []
        for h in range(RET_HEADS):
            oh = or_ref[:, 128 * h:128 * (h + 1)]
            xc = oh - jnp.mean(oh, axis=-1, keepdims=True)
            rstd = lax.rsqrt(jnp.mean(xc * xc, axis=-1, keepdims=True) + EPS)
            ons.append(xc * rstd)
            rstds.append(rstd)
        on = jnp.concatenate(ons, axis=1)
        yn = on * gn_ref[...]
        yr_b = _mx(yn * szr)
        wpa_t, wpr_t = wpt_ref[:, 0:512], wpt_ref[:, 512:1024]
        a_att = _nt(ya_b, wpa_t)
        a_ret = _nt(yr_b, wpr_t)
        gts = _sigmoid(gl_ref[...])
        g_att, g_ret = gts[:, 0:D_MODEL], gts[:, D_MODEL:2 * D_MODEL]
        merged_b = _mx(g_att * a_att + g_ret * a_ret)
        u = _nn(merged_b, wout_ref[...])
        r = lax.rsqrt(jnp.mean(u * u, axis=-1, keepdims=True) + EPS)
        un = u * r
        gpost = gp_ref[...]
        y = un * gpost
        gate = mod_ref[:, 2 * D_MODEL:3 * D_MODEL]
        err = x_ref[...] + gate * y - t_ref[...]
        loss_ref[...] += (0.5 / D_MODEL) * _rowsum128(err * err)
        dout = err * (1.0 / D_MODEL)
        dout_ref[...] = dout
        dgate_ref[...] += jnp.sum(dout * y, axis=0, keepdims=True)
        dy = dout * gate
        dgpost_ref[...] += jnp.sum(dy * un, axis=0, keepdims=True)
        dun = dy * gpost
        du_b = _mx(r * (dun - un * jnp.mean(dun * un, axis=-1, keepdims=True)))
        dmerged = _nt(du_b, wout_ref[...])
        gout_acc[...] += _tn(merged_b, du_b)
        d_att, d_ret = dmerged * g_att, dmerged * g_ret
        dgl_ref[:, 0:D_MODEL] = (d_att * a_att * (1.0 - g_att)).astype(dgl_ref.dtype)
        dgl_ref[:, D_MODEL:2 * D_MODEL] = (d_ret * a_ret * (1.0 - g_ret)).astype(dgl_ref.dtype)
        d_att_b, d_ret_b = _mx(d_att), _mx(d_ret)
        dya = _nn(d_att_b, wpa_t)
        dyr = _nn(d_ret_b, wpr_t)
        gpt_acc[:, 0:512] += _tn(d_att_b, ya_b)
        gpt_acc[:, 512:1024] += _tn(d_ret_b, yr_b)
        dza_ref[...] = (dya * oa * (sga * (1.0 + za * (1.0 - sga)))).astype(dza_ref.dtype)
        doa = dya * sza
        doa_ref[...] = doa.astype(doa_ref.dtype)
        dt = jnp.transpose(doa * oa)
        del_ref[...] = jnp.sum(dt.reshape(8, HEAD_DIM, bs), axis=1)
        dzr_ref[...] = (dyr * yn * (sgr * (1.0 + zr * (1.0 - sgr)))).astype(dzr_ref.dtype)
        dyn = dyr * szr
        dgn_ref[...] += jnp.sum(dyn * on, axis=0, keepdims=True)
        don = dyn * gn_ref[...]
        for h in range(RET_HEADS):
            hs = slice(128 * h, 128 * (h + 1))
            dh, oh = don[:, hs], ons[h]
            doh = rstds[h] * (dh - jnp.mean(dh, axis=-1, keepdims=True) - oh * jnp.mean(dh * oh, axis=-1, keepdims=True))
            dor_ref[:, hs] = doh.astype(dor_ref.dtype)

        @pl.when(i == nb - 1)
        def _():
            gout_ref[...] = gout_acc[...].astype(gout_ref.dtype)
            gpt_ref[...] = gpt_acc[...].astype(gpt_ref.dtype)

    row = lambda w: pl.BlockSpec((bs, w), lambda i: (i, 0))
    el = lambda w, off: pl.BlockSpec((pl.Element(bs), pl.Element(w)), lambda i: (i * bs, off))
    vec = lambda w: pl.BlockSpec((1, w), lambda i: (0, 0))
    full = pl.BlockSpec((D_MODEL, D_MODEL), lambda i: (0, 0))
    sds = jax.ShapeDtypeStruct
    return pl.pallas_call(
        body, grid=(nb,), name="tail",
        in_specs=[row(D_MODEL), row(D_MODEL), row(512), row(512), el(512, ZA_B), el(512, ZR_B), el(2 * D_MODEL, GL_B),
                  vec(3 * D_MODEL), vec(D_MODEL), vec(512), full, full],
        out_specs=[row(D_MODEL), row(512), row(512), row(2 * D_MODEL), row(512), row(512),
                   pl.BlockSpec((8, bs), lambda i: (0, i)), full, full, vec(D_MODEL), vec(D_MODEL), vec(512), vec(128)],
        out_shape=[sds((seq, D_MODEL), F32), sds((seq, 512), MXU_DTYPE), sds((seq, 512), MXU_DTYPE),
                   sds((seq, 2 * D_MODEL), MXU_DTYPE), sds((seq, 512), MXU_DTYPE), sds((seq, 512), MXU_DTYPE),
                   sds((8, seq), F32), sds((D_MODEL, D_MODEL), MXU_DTYPE), sds((D_MODEL, D_MODEL), MXU_DTYPE),
                   sds((1, D_MODEL), F32), sds((1, D_MODEL), F32), sds((1, 512), F32), sds((1, 128), F32)],
        scratch_shapes=[pltpu.VMEM((D_MODEL, D_MODEL), F32), pltpu.VMEM((D_MODEL, D_MODEL), F32)],
        compiler_params=_cp("arbitrary"))(x, tgt, o_att, o_ret, p, p, p, mod, g_post, gn_g, wpt, wout)


def _assemble(p, cos, sin, qg2, kg2, dqt, dkp, dvp, dza, drq, drk, drv, dzr, dgl):
    seq = p.shape[0]
    bs = 512

    def body(p_ref, cos_ref, sin_ref, qg_ref, kg_ref, dqt_ref, dkp_ref, dvp_ref, dza_ref, drq_ref, drk_ref, drv_ref,
             dzr_ref, dgl_ref, dp_ref, dqg_ref, dkg_ref):
        @pl.when(pl.program_id(0) == 0)
        def _():
            dqg_ref[...] = jnp.zeros_like(dqg_ref)
            dkg_ref[...] = jnp.zeros_like(dkg_ref)

        m = _blockdiag64()
        cs, sn = cos_ref[...], sin_ref[...]
        lo = _lane((bs, 128)) < 64

        def norm_bwd(raw, dn, g, dg_ref):
            r = lax.rsqrt(_seg64(raw * raw, m) * (1.0 / 64) + EPS)
            xn = raw * r
            dg_ref[...] += jnp.sum(dn * xn, axis=0, keepdims=True)
            dxn = dn * g
            return r * (dxn - xn * (_seg64(dxn * xn, m) * (1.0 / 64)))

        for pr in range(4):
            sl = slice(128 * pr, 128 * (pr + 1))
            dn = _rope_t(dqt_ref[:, sl] * 0.125, cs, sn)
            dp_ref[:, QA + 128 * pr:QA + 128 * (pr + 1)] = norm_bwd(p_ref[:, sl], dn, qg_ref[...], dqg_ref).astype(dp_ref.dtype)
        fold = lambda a: a + pltpu.roll(a, 64, 1)
        dk = jnp.where(lo, fold(dkp_ref[0]), fold(dkp_ref[1]))
        dp_ref[:, KA:KA + 128] = norm_bwd(p_ref[:, KA:KA + 128], _rope_t(dk, cs, sn), kg_ref[...], dkg_ref).astype(dp_ref.dtype)
        dp_ref[:, VA:VA + 128] = jnp.where(lo, fold(dvp_ref[0]), fold(dvp_ref[1])).astype(dp_ref.dtype)
        dp_ref[:, ZA:ZA + 512] = dza_ref[...]
        for pr in range(2):
            sl = slice(128 * pr, 128 * (pr + 1))
            dp_ref[:, QR + 128 * pr:QR + 128 * (pr + 1)] = _rope_t(drq_ref[:, sl], cs, sn).astype(dp_ref.dtype)
            dp_ref[:, KR + 128 * pr:KR + 128 * (pr + 1)] = _rope_t(drk_ref[:, sl] * 0.125, cs, sn).astype(dp_ref.dtype)
        dp_ref[:, VR:VR + 512] = drv_ref[...]
        dp_ref[:, ZR:ZR + 512] = dzr_ref[...]
        dp_ref[:, GL:GL + 2 * D_MODEL] = dgl_ref[...]

    row = lambda w: pl.BlockSpec((bs, w), lambda i: (i, 0))
    gsp = pl.BlockSpec((1, 128), lambda i: (0, 0))
    dup = pl.BlockSpec((2, bs, 128), lambda i: (0, i, 0))
    return pl.pallas_call(
        body, grid=(seq // bs,), name="assemble_dp",
        in_specs=[row(768), row(128), row(128), gsp, gsp, row(512), dup, dup, row(512), row(256), row(256), row(512),
                  row(512), row(2 * D_MODEL)],
        out_specs=[row(IN_WIDTH), gsp, gsp],
        out_shape=[jax.ShapeDtypeStruct((seq, IN_WIDTH), MXU_DTYPE), jax.ShapeDtypeStruct((1, 128), F32),
                   jax.ShapeDtypeStruct((1, 128), F32)],
        compiler_params=_cp("arbitrary"))(p, cos, sin, qg2, kg2, dqt, dkp, dvp, dza, drq, drk, drv, dzr, dgl)


def _local_step(x, tgt, mod, g_pre, qn_g, kn_g, w_dec_f, w_dec_b, gn_g, g_post, w_attn, rest_start, rest_wait, send=None, own=()):
    send = send or (lambda name, arrays: None)
    seq = x.shape[0]
    cos, sin = _rope_tables(seq)
    qg2, kg2 = jnp.tile(qn_g, (1, 2)), jnp.tile(kn_g, (1, 2))
    lg_f = jax.nn.log_sigmoid(w_dec_f[0])
    lg_b = jax.nn.log_sigmoid(w_dec_b[0])
    tb = _ret_tables(lg_f, lg_b, RET_CHUNK)

    h, p_a, qt, kd, vd, *placed = _attn_inputs(x, mod, g_pre, w_attn, cos, sin, qg2, kg2, own=own)
    o_att, lse = _attn_fwd(qt, kd, vd, dep=rest_start(qt, *placed))
    win_t, wpt, wout = rest_wait(o_att)
    p_b, rq, rk, rv = _in_proj_rest(h, win_t, cos, sin)
    rf, rb = _ret_states(rk, rv, tb["wkf"], tb["wkb"], tb["dec_fb"], "ret_states_fwd")
    o_ret = _ret_fwd(rq, rk, rv, rf, rb, tb)
    (dout, dza, dzr, dgl, do_att, do_ret, delta, g_out, g_pt, dgate, dgpost, dgn, loss_l) = _tail(
        x, tgt, o_att, o_ret, p_b, mod, g_post, gn_g, wpt, wout)
    dep = send("early", (g_pt, g_out))
    dqt, dkp, dvp = _attn_bwd(qt, kd, vd, do_att, lse, delta, dep=dep)
    hb, hf = _ret_states(rq, do_ret, tb["wqb"], tb["wqf"], tb["dec_bf"], "ret_states_bwd")
    drq, drk, drv, dlg = _ret_bwd(rq, rk, rv, do_ret, rf, rb, hf, hb, tb)
    dp, dqg, dkg = _assemble(p_a, cos, sin, qg2, kg2, dqt, dkp, dvp, dza, drq, drk, drv, dzr, dgl)
    g_in_t = _matmul(dp, h, ta=True, tb=False, tm=256, tn=D_MODEL, out_dtype=MXU_DTYPE, name="in_proj_dw")
    dep = send("late", (g_in_t,))
    grad_x, dshift, dscale, dgpre = _in_proj_dx(x, dp, win_t, dout, mod, g_pre, dep=dep)

    dlg = jnp.sum(dlg, axis=1)
    small = dict(
        dmod=jnp.concatenate([dshift, dscale, dgate], axis=1),
        g_pre=dgpre, g_post=dgpost, gn_g=dgn,
        qn_g=dqg[:, :64] + dqg[:, 64:], kn_g=dkg[:, :64] + dkg[:, 64:],
        w_dec_f=(dlg[0:4] * jax.nn.sigmoid(-w_dec_f[0]))[None], w_dec_b=(dlg[4:8] * jax.nn.sigmoid(-w_dec_b[0]))[None],
        loss=jnp.sum(loss_l, axis=1, keepdims=True))
    return grad_x, g_in_t, g_pt, g_out, small


N_DEV = 8
N_CHIP = 4
HBM_SPEC = pl.BlockSpec(memory_space=pl.ANY)
VMEM_SPEC = pl.BlockSpec(memory_space=pltpu.VMEM)
SHARD_ROWS = (IN_WIDTH // N_CHIP, D_MODEL // N_CHIP, D_MODEL // N_CHIP)


def _coords():
    return lax.axis_index("x"), lax.axis_index("y"), lax.axis_index("c")


def _peer(k):
    x, y, c = _coords()
    return (1 - x if k & 4 else x, 1 - y if k & 2 else y, 1 - c if k & 1 else c)


def _dev_index(p):
    return 4 * p[0] + 2 * p[1] + p[2]


def _rows(ref, start, size):
    return ref.at[pl.ds(pl.multiple_of(start, 16), size), :]


def _remote(src, dst, ssem, rsem, dev):
    return pltpu.make_async_remote_copy(src_ref=src, dst_ref=dst, send_sem=ssem, recv_sem=rsem, device_id=dev,
                                        device_id_type=MESH)


ATTN_CHUNK = 32


def _mod_and_attn_rows(c, w_ada_s, b4, win_s):
    ncol = w_ada_s.shape[1]
    half = ATTN_COLS // 2
    offs = list(range(0, half, ATTN_CHUNK))
    nq = len(offs)
    rows = lambda ref, cc, off: ref.at[pl.ds(pl.multiple_of(cc * half + off, 16), ATTN_CHUNK), :]

    def body(c_ref, w_ref, b_ref, src, cs_ref, mod_ref, out, modsh, modall, ssem, rsem, lsem, asem, bsem, fsem, gsem, hsem):
        x, y, cc = _coords()
        me = _dev_index((x, y, cc))
        chip = me // 2
        sib = (x, y, 1 - cc)
        cs_ref[me] = c_ref[...]
        sends = []
        for k in range(1, N_DEV):
            cp = _remote(c_ref, cs_ref.at[me], ssem.at[k - 1], rsem.at[k - 1], _peer(k))
            cp.start()
            sends.append(cp)
        own = pltpu.make_async_copy(src.at[pl.ds(0, ATTN_COLS), :], out, lsem.at[0])
        bulk = [_remote(rows(src, cc, off), rows(out, cc, off), asem.at[(k2 - 1) * nq + q], bsem.at[q], (k2 // 2, k2 % 2, cc))
                for k2 in (1, 2) for q, off in enumerate(offs)]
        relay_chip = lambda q: 1 + q % 2

        @pl.when(chip == 0)
        def _():
            own.start()
            for cp in bulk:
                cp.start()

        for k in range(1, N_DEV):
            slot = cs_ref.at[_dev_index(_peer(k))]
            _remote(slot, slot, ssem.at[k - 1], rsem.at[k - 1], _peer(k)).wait_recv()
        cs = jnp.concatenate([cs_ref[d] for d in range(N_DEV)], axis=0)
        ms = _nn(cs * _sigmoid(cs), w_ref[...], precision=HIGHEST) + b_ref[pl.ds(chip, 1), :]
        modsh[...] = ms
        modall[chip] = ms
        for k2 in range(1, N_CHIP):
            cp = _remote(modsh, modall.at[chip], ssem.at[6 + k2], rsem.at[6 + k2], _peer(2 * k2))
            cp.start()
            sends.append(cp)

        @pl.when(chip != 0)
        def _():
            passed = []
            relays = [_remote(rows(out, cc, off), rows(out, cc, off), hsem.at[q], bsem.at[q], (1, 1, cc)) for q, off in enumerate(offs)]
            for q, off in enumerate(offs):
                got = rows(out, cc, off)
                _remote(got, got, asem.at[q], bsem.at[q], (0, 0, cc)).wait_recv()
                cp = _remote(got, got, fsem.at[q], gsem.at[q], sib)
                cp.start()
                passed.append(cp)
                pl.when(chip == relay_chip(q))(relays[q].start)
            for q, off in enumerate(offs):
                got = rows(out, 1 - cc, off)
                _remote(got, got, fsem.at[q], gsem.at[q], sib).wait_recv()
            for cp in passed:
                cp.wait_send()
            for q in range(nq):
                pl.when(chip == relay_chip(q))(relays[q].wait_send)

        for k2 in range(1, N_CHIP):
            slot = modall.at[_dev_index(_peer(2 * k2)) // 2]
            _remote(slot, slot, ssem.at[6 + k2], rsem.at[6 + k2], _peer(2 * k2)).wait_recv()
        for jj in range(N_CHIP):
            mod_ref[:, ncol * jj:ncol * (jj + 1)] = modall[jj, pl.ds(me, 1), :]
        for cp in sends:
            cp.wait_send()

        @pl.when(chip == 0)
        def _():
            for cp in bulk:
                cp.wait_send()
            own.wait()

    dma = pltpu.SemaphoreType.DMA
    return pl.pallas_call(
        body, name="mod_and_attn_rows", in_specs=[VMEM_SPEC] * 4, out_specs=[VMEM_SPEC, VMEM_SPEC, HBM_SPEC],
        out_shape=[jax.ShapeDtypeStruct((N_DEV, 1, D_MODEL), F32), jax.ShapeDtypeStruct((1, N_CHIP * ncol), F32),
                   jax.ShapeDtypeStruct((ATTN_COLS, win_s.shape[1]), win_s.dtype)],
        scratch_shapes=[pltpu.VMEM((N_DEV, ncol), F32), pltpu.VMEM((N_CHIP, N_DEV, ncol), F32), dma((10,)), dma((10,)),
                        dma((1,)), dma((2 * nq,)), dma((nq,)), dma((nq,)), dma((nq,)), dma((nq,))],
        compiler_params=_cp())(c, w_ada_s, b4, win_s)


GATHER_CHUNK = 32


def _chunks(kinds, chunk):
    out = []
    for t, kind in enumerate(kinds):
        half = SHARD_ROWS[kind] // 2
        step = chunk if half % chunk == 0 else half
        out += [(t, off, step) for off in range(0, half, step)]
    return out


SEM_SPEC = pl.BlockSpec(memory_space=pltpu.SEMAPHORE)
HBM_ONLY = pl.BlockSpec(memory_space=pltpu.HBM)
DATAFLOW = pltpu.SideEffectType.DATAFLOW_SIDE_EFFECTING


def _gather_rest_start(shards, zones, dep):
    n = len(shards)

    def body(*refs):
        srcs, lands = refs[:n], refs[n:2 * n]
        ssem, rsem = refs[2 * n + 1], refs[2 * n + 2]
        token = refs[-1]
        x, y, _ = _coords()
        chip = 2 * x + y
        for k2 in range(1, N_CHIP):
            for t in range(n):
                q = (k2 - 1) * n + t
                _remote(srcs[t], _rows(lands[t], chip * SHARD_ROWS[t], SHARD_ROWS[t]), ssem.at[q], rsem.at[q], _peer(2 * k2)).start()
        token[...] = jnp.zeros_like(token)

    hbm = lambda a: pltpu.with_memory_space_constraint(a, pltpu.HBM)
    dma = pltpu.SemaphoreType.DMA
    outs = pl.pallas_call(
        body, name="gather_rest", in_specs=[HBM_ONLY] * (2 * n) + [HBM_SPEC],
        out_specs=[SEM_SPEC, SEM_SPEC] + [HBM_ONLY] * (2 * n) + [VMEM_SPEC],
        out_shape=[dma((3 * n,)), dma((3 * n,))] + [pltpu.HBM(a.shape, a.dtype) for a in list(shards) + list(zones)]
        + [jax.ShapeDtypeStruct((8, 128), F32)],
        input_output_aliases={i: 2 + i for i in range(2 * n)},
        compiler_params=pltpu.CompilerParams(has_side_effects=DATAFLOW))(*[hbm(a) for a in list(shards) + list(zones)], dep)
    return outs[0], outs[1], outs[2:2 + n], outs[2 + n:2 + 2 * n], outs[-1]


def _gather_rest_wait(started, after):
    ssem, rsem, shards, zones, _ = started
    n = len(shards)

    def body(*refs):
        srcs, lands = refs[:n], refs[n:2 * n]
        ssem_ref, rsem_ref = refs[2 * n], refs[2 * n + 1]
        for k2 in range(1, N_CHIP):
            pchip = _dev_index(_peer(2 * k2)) // 2
            for t in range(n):
                q = (k2 - 1) * n + t
                cp = _remote(srcs[t], _rows(lands[t], pchip * SHARD_ROWS[t], SHARD_ROWS[t]), ssem_ref.at[q], rsem_ref.at[q], _peer(2 * k2))
                cp.wait_send()
                cp.wait_recv()

    outs = pl.pallas_call(
        body, name="gather_rest_wait", in_specs=[HBM_ONLY] * (2 * n) + [SEM_SPEC, SEM_SPEC, HBM_SPEC], out_specs=[HBM_ONLY] * (2 * n),
        out_shape=[pltpu.HBM(a.shape, a.dtype) for a in list(shards) + list(zones)],
        input_output_aliases={i: i for i in range(2 * n)},
        compiler_params=pltpu.CompilerParams(has_side_effects=DATAFLOW))(*shards, *zones, ssem, rsem, after)
    return outs[n:]


def _reduced_by(ref, t, dev):
    return _rows(ref, (dev // 2) * SHARD_ROWS[t] + (dev % 2) * (SHARD_ROWS[t] // 2), SHARD_ROWS[t] // 2)


def _scatter_start(grads, kinds, name):
    n = len(grads)

    def body(*refs):
        srcs, lands = refs[:n], refs[n:2 * n]
        ssem, rsem = refs[2 * n], refs[2 * n + 1]
        token = refs[-1]
        me = _dev_index(_coords())
        for k in range(1, N_DEV):
            for i, t in enumerate(kinds):
                q = (k - 1) * n + i
                _remote(_reduced_by(srcs[i], t, _dev_index(_peer(k))), lands[i].at[me], ssem.at[q], rsem.at[q], _peer(k)).start()
        token[...] = jnp.zeros_like(token)

    zones = [lax.empty((N_DEV, SHARD_ROWS[t] // 2, g.shape[1]), g.dtype) for g, t in zip(grads, kinds)]
    hbm = lambda a: pltpu.with_memory_space_constraint(a, pltpu.HBM)
    dma = pltpu.SemaphoreType.DMA
    outs = pl.pallas_call(
        body, name=name, in_specs=[HBM_ONLY] * (2 * n), out_specs=[SEM_SPEC, SEM_SPEC] + [HBM_ONLY] * (2 * n) + [VMEM_SPEC],
        out_shape=[dma((7 * n,)), dma((7 * n,))] + [pltpu.HBM(a.shape, a.dtype) for a in list(grads) + zones]
        + [jax.ShapeDtypeStruct((8, 128), F32)],
        input_output_aliases={i: 2 + i for i in range(2 * n)},
        compiler_params=pltpu.CompilerParams(has_side_effects=DATAFLOW))(*[hbm(a) for a in list(grads) + zones])
    return outs[0], outs[1], outs[2:2 + n], outs[2 + n:2 + 2 * n], outs[-1]


def _scatter_wait(started, kinds, after, name):
    ssem, rsem, grads, zones, _ = started
    n = len(grads)

    def body(*refs):
        srcs, lands = refs[:n], refs[n:2 * n]
        ssem_ref, rsem_ref = refs[2 * n], refs[2 * n + 1]
        for k in range(1, N_DEV):
            peer = _dev_index(_peer(k))
            for i, t in enumerate(kinds):
                q = (k - 1) * n + i
                cp = _remote(_reduced_by(srcs[i], t, peer), lands[i].at[peer], ssem_ref.at[q], rsem_ref.at[q], _peer(k))
                cp.wait_send()
                cp.wait_recv()

    outs = pl.pallas_call(
        body, name=name, in_specs=[HBM_ONLY] * (2 * n) + [SEM_SPEC, SEM_SPEC, HBM_SPEC], out_specs=[HBM_ONLY] * (2 * n),
        out_shape=[pltpu.HBM(a.shape, a.dtype) for a in list(grads) + list(zones)],
        input_output_aliases={i: i for i in range(2 * n)},
        compiler_params=pltpu.CompilerParams(has_side_effects=DATAFLOW))(*grads, *zones, ssem, rsem, after)
    return outs[:n], outs[n:]


def _grad_finish(grads, zones, kinds, name):
    nt = len(grads)
    pieces = _chunks(kinds, GATHER_CHUNK)
    npc = len(pieces)
    shard = [SHARD_ROWS[k] for k in kinds]

    def body(*refs):
        srcs, lands, outs = refs[:nt], refs[nt:2 * nt], refs[2 * nt:3 * nt]
        slots, sums = refs[3 * nt:4 * nt], refs[4 * nt:5 * nt]
        lsem, osem, xsem, ysem = refs[5 * nt:]
        x, y, c = _coords()
        me = _dev_index((x, y, c))
        sib = (x, y, 1 - c)
        loads = [pltpu.make_async_copy(_reduced_by(srcs[t], kinds[t], me), slots[t].at[me], lsem.at[t]) for t in range(nt)]
        for k in range(1, N_DEV):
            peer = _dev_index(_peer(k))
            loads += [pltpu.make_async_copy(lands[t].at[peer], slots[t].at[peer], lsem.at[k * nt + t]) for t in range(nt)]
        for cp in loads:
            cp.start()
        for cp in loads:
            cp.wait()
        sends = []
        place = lambda t, cc, off, n: _rows(outs[t], cc * (shard[t] // 2) + off, n)
        stores = []
        for q, (t, off, n) in enumerate(pieces):
            r = pl.ds(off, n)
            acc = slots[t][0, r, :].astype(F32)
            for d in range(1, N_DEV):
                acc = acc + slots[t][d, r, :].astype(F32)
            sums[t][r, :] = acc
            keep = pltpu.make_async_copy(sums[t].at[r, :], place(t, c, off, n), osem.at[q])
            give = _remote(sums[t].at[r, :], place(t, c, off, n), xsem.at[q], ysem.at[q], sib)
            keep.start()
            give.start()
            stores.append(keep)
            sends.append(give)
        for q, (t, off, n) in enumerate(pieces):
            got = place(t, 1 - c, off, n)
            _remote(got, got, xsem.at[q], ysem.at[q], sib).wait_recv()
        for cp in sends:
            cp.wait_send()
        for cp in stores:
            cp.wait()

    dma = pltpu.SemaphoreType.DMA
    return pl.pallas_call(
        body, name=name, in_specs=[HBM_SPEC] * (2 * nt), out_specs=[HBM_SPEC] * nt,
        out_shape=[jax.ShapeDtypeStruct((shard[t], g.shape[1]), F32) for t, g in enumerate(grads)],
        scratch_shapes=([pltpu.VMEM((N_DEV, shard[t] // 2, g.shape[1]), g.dtype) for t, g in enumerate(grads)]
                        + [pltpu.VMEM((shard[t] // 2, g.shape[1]), F32) for t, g in enumerate(grads)]
                        + [dma((N_DEV * nt,)), dma((npc,)), dma((npc,)), dma((npc,))]),
        compiler_params=_cp())(*grads, *zones)


def _adam_math(w, g, m, v):
    m = ADAM_B1 * m + (1.0 - ADAM_B1) * g
    v = ADAM_B2 * v + (1.0 - ADAM_B2) * (g * g)
    m_hat = m / (1.0 - ADAM_B1 ** ADAM_STEP)
    v_hat = v / (1.0 - ADAM_B2 ** ADAM_STEP)
    return -ADAM_LR * (m_hat / (jnp.sqrt(v_hat) + ADAM_EPS) + ADAM_WD * w), m, v


def _adamw(w, g, m, v, rb, name):
    rows, cols = w.shape

    def body(w_ref, g_ref, m_ref, v_ref, d_ref, nm_ref, nv_ref):
        d_ref[...], nm_ref[...], nv_ref[...] = _adam_math(w_ref[...], g_ref[...], m_ref[...], v_ref[...])

    spec = pl.BlockSpec((rb, cols), lambda i: (i, 0))
    return pl.pallas_call(body, grid=(rows // rb,), name=name, in_specs=[spec] * 4, out_specs=[spec] * 3,
                          out_shape=[jax.ShapeDtypeStruct(w.shape, F32)] * 3, compiler_params=_cp("parallel"))(w, g, m, v)


PACK = (("b_ada", 3 * D_MODEL), ("g_pre", D_MODEL), ("g_post", D_MODEL), ("gn_g", 512), ("qn_g", 64), ("kn_g", 64),
        ("w_dec_f", 4), ("w_dec_b", 4), ("loss", 1))
PACK_OFFSETS = {}
PACK_WIDTH = 0
for _name, _n in PACK:
    PACK_OFFSETS[_name] = PACK_WIDTH
    PACK_WIDTH += -(-_n // 128) * 128
SMALL_PARAMS = tuple(name for name, _ in PACK if name != "loss")


def _pack(parts):
    cols = []
    for name, n in PACK:
        pad = -(-n // 128) * 128 - n
        cols.append(parts[name].reshape(1, n).astype(F32))
        if pad:
            cols.append(jnp.zeros((1, pad), F32))
    return jnp.concatenate(cols, axis=1)


def _small_reduce(vec, cs, deps):
    ncol = 3 * D_MODEL // N_CHIP

    def body(vec_ref, cs_ref, *rest):
        tot_ref, gwa_ref, gat, ssem, rsem = rest[-5:]
        me = _dev_index(_coords())
        chip = me // 2
        gat[me] = vec_ref[...]
        sends = []
        for k in range(1, N_DEV):
            cp = _remote(vec_ref, gat.at[me], ssem.at[k - 1], rsem.at[k - 1], _peer(k))
            cp.start()
            sends.append(cp)
        for k in range(1, N_DEV):
            slot = gat.at[_dev_index(_peer(k))]
            _remote(slot, slot, ssem.at[k - 1], rsem.at[k - 1], _peer(k)).wait_recv()
        rows = [gat[d] for d in range(N_DEV)]
        tot = rows[0]
        for d in range(1, N_DEV):
            tot = tot + rows[d]
        tot_ref[...] = tot
        cs = cs_ref[...]
        ca_t = jnp.transpose(jnp.concatenate([cs * _sigmoid(cs), jnp.zeros((128 - N_DEV, D_MODEL), F32)], axis=0))
        dm = jnp.concatenate(rows + [jnp.zeros((128 - N_DEV, PACK_WIDTH), F32)], axis=0)
        for jj in range(N_CHIP):
            @pl.when(chip == jj)
            def _():
                gwa_ref[...] = _nn(ca_t, dm[:, ncol * jj:ncol * (jj + 1)], precision=HIGHEST)
        for cp in sends:
            cp.wait_send()

    return pl.pallas_call(
        body, name="small_reduce", in_specs=[VMEM_SPEC, VMEM_SPEC] + [HBM_SPEC] * len(deps), out_specs=[VMEM_SPEC] * 2,
        out_shape=[jax.ShapeDtypeStruct((1, PACK_WIDTH), F32), jax.ShapeDtypeStruct((D_MODEL, ncol), F32)],
        scratch_shapes=[pltpu.VMEM((N_DEV, 1, PACK_WIDTH), F32), pltpu.SemaphoreType.DMA((7,)), pltpu.SemaphoreType.DMA((7,))],
        compiler_params=_cp())(vec, cs, *deps)


def _small_adamw(tot, given):
    sizes = dict(PACK)
    np_ = len(SMALL_PARAMS)

    def body(*refs):
        tot_ref = refs[0]
        wmv = refs[1:1 + 3 * np_]
        outs = refs[1 + 3 * np_:1 + 7 * np_]
        loss_ref = refs[-1]
        for i, name in enumerate(SMALL_PARAMS):
            off, n = PACK_OFFSETS[name], sizes[name]
            g = tot_ref[:, off:off + n]
            w_ref, m_ref, v_ref = wmv[3 * i:3 * i + 3]
            outs[i][...] = g
            outs[np_ + i][...], outs[2 * np_ + i][...], outs[3 * np_ + i][...] = _adam_math(w_ref[...], g, m_ref[...], v_ref[...])
        loss_ref[...] = tot_ref[:, PACK_OFFSETS["loss"]:PACK_OFFSETS["loss"] + 1]

    flat = [a for name in SMALL_PARAMS for a in given[name]]
    shapes = [jax.ShapeDtypeStruct((1, sizes[name]), F32) for name in SMALL_PARAMS]
    res = pl.pallas_call(body, name="small_adamw", in_specs=[VMEM_SPEC] * (1 + 3 * np_), out_specs=[VMEM_SPEC] * (4 * np_ + 1),
                         out_shape=shapes * 4 + [jax.ShapeDtypeStruct((1, 1), F32)], compiler_params=_cp())(tot, *flat)
    return [dict(zip(SMALL_PARAMS, res[k * np_:(k + 1) * np_])) for k in range(4)], res[-1]


def kernel(x, c, w_ada, b_ada, g_pre, w_in, qn_g, kn_g, w_dec_f, w_dec_b, gn_g, w_pa, w_pr, w_out, g_post, loss_target, m_w_ada, m_b_ada, m_g_pre, m_w_in, m_qn_g, m_kn_g, m_w_dec_f, m_w_dec_b, m_gn_g, m_w_pa, m_w_pr, m_w_out, m_g_post, v_w_ada, v_b_ada, v_g_pre, v_w_in, v_qn_g, v_kn_g, v_w_dec_f, v_w_dec_b, v_gn_g, v_w_pa, v_w_pr, v_w_out, v_g_post):
    shards = (w_in[0].T.astype(MXU_DTYPE), jnp.concatenate([w_pa[0].T, w_pr[0].T], axis=1).astype(MXU_DTYPE),
              w_out[0].astype(MXU_DTYPE))
    cs, mod, w_attn = _mod_and_attn_rows(c, w_ada[0], b_ada.reshape(N_CHIP, 3 * D_MODEL // N_CHIP), shards[0])
    started = {}

    def rest_start(dep, *placed):
        started["rest"] = _gather_rest_start(shards, placed, dep)
        return started["rest"][-1]

    def rest_wait(after):
        return _gather_rest_wait(started["rest"], after)

    kinds = {"early": (1, 2), "late": (0,)}

    def send(name, arrays):
        started[name] = _scatter_start(arrays, kinds[name], "grad_scatter_" + name)
        return started[name][-1]

    grad_x, _, _, _, small = _local_step(x[0], loss_target[0], mod, g_pre, qn_g, kn_g, w_dec_f, w_dec_b,
                                         gn_g, g_post, w_attn, rest_start, rest_wait, send=send, own=shards)
    small = dict(small)
    small["b_ada"] = small.pop("dmod")
    given = dict(b_ada=(b_ada, m_b_ada, v_b_ada), g_pre=(g_pre, m_g_pre, v_g_pre), g_post=(g_post, m_g_post, v_g_post),
                 gn_g=(gn_g, m_gn_g, v_gn_g), qn_g=(qn_g, m_qn_g, v_qn_g), kn_g=(kn_g, m_kn_g, v_kn_g),
                 w_dec_f=(w_dec_f, m_w_dec_f, v_w_dec_f), w_dec_b=(w_dec_b, m_w_dec_b, v_w_dec_b))
    grads, deltas, new_m, new_v = {}, {}, {}, {}

    def update(name, w, g, m, v, back):
        d, nm, nv = _adamw(w, g, m, v, w.shape[0] // 4, "adamw_" + name)
        grads[name], deltas[name], new_m[name], new_v[name] = back(g), back(d), back(nm), back(nv)
        return d

    lead = lambda a: a[None]
    (g_pt, g_out), (z_pt, z_out) = _scatter_wait(started["early"], kinds["early"], grad_x, "grad_scatter_early_wait")
    r_pt, r_out = _grad_finish((g_pt, g_out), (z_pt, z_out), kinds["early"], "grad_finish_early")
    early = (update("w_pa", w_pa[0], r_pt[:, :512].T, m_w_pa[0], v_w_pa[0], lead),
             update("w_pr", w_pr[0], r_pt[:, 512:].T, m_w_pr[0], v_w_pr[0], lead),
             update("w_out", w_out[0], r_out, m_w_out[0], v_w_out[0], lead))
    tot, g_w_ada = _small_reduce(_pack(small), cs.reshape(N_DEV, D_MODEL), early)
    small_parts, loss = _small_adamw(tot, given)
    for dst, part in zip((grads, deltas, new_m, new_v), small_parts):
        dst.update(part)
    last = update("w_ada", w_ada[0], g_w_ada, m_w_ada[0], v_w_ada[0], lead)

    (g_in_t,), (z_in,) = _scatter_wait(started["late"], kinds["late"], last, "grad_scatter_late_wait")
    (r_in,) = _grad_finish((g_in_t,), (z_in,), kinds["late"], "grad_finish_late")
    tr = lambda a: a[0].T
    update("w_in", tr(w_in), r_in, tr(m_w_in), tr(v_w_in), lambda a: a.T[None])
    order = ("w_ada", "b_ada", "g_pre", "w_in", "qn_g", "kn_g", "w_dec_f", "w_dec_b", "gn_g", "w_pa", "w_pr", "w_out", "g_post")
    return (loss[0, 0], grad_x[None], *[grads[n] for n in order], *[deltas[n] for n in order],
            *[new_m[n] for n in order], *[new_v[n] for n in order])
```

```python
import jax
import jax.numpy as jnp
import numpy as np
from jax import lax
from jax.experimental import pallas as pl
from jax.experimental.pallas import tpu as pltpu

F32 = jnp.float32
BF16 = jnp.bfloat16
MXU_DTYPE = jnp.bfloat16

D_MODEL = 1024
GRID_W = 64
HEAD_DIM = 64
ROPE_THETA = 10000.0
EPS = 1e-6
RET_HEADS = 4
RET_CHUNK = 256
RET_GROUP = 4
IN_WIDTH = 4864
QA, KA, VA, ZA, QR, KR, VR, ZR, GL = 0, 512, 640, 768, 1280, 1536, 1792, 2304, 2816
ATTN_COLS = ZA
ZA_B, QR_B, KR_B, VR_B, ZR_B, GL_B = (c - ATTN_COLS for c in (ZA, QR, KR, VR, ZR, GL))

ADAM_LR, ADAM_B1, ADAM_B2, ADAM_EPS, ADAM_WD, ADAM_STEP = 0.001, 0.9, 0.999, 1e-08, 0.01, 10

VMEM_LIMIT = 56 * 1024 * 1024
MESH = pl.DeviceIdType.MESH
HIGHEST = lax.Precision.HIGHEST
LOG2E = 1.4426950408889634
LN2 = 0.6931471805599453


def _cp(*sem, **kw):
    if sem:
        kw["dimension_semantics"] = sem
    return pltpu.CompilerParams(vmem_limit_bytes=VMEM_LIMIT, **kw)


def _nn(a, b, **kw):
    return lax.dot_general(a, b, (((1,), (0,)), ((), ())), preferred_element_type=F32, **kw)


def _nt(a, b):
    return lax.dot_general(a, b, (((1,), (1,)), ((), ())), preferred_element_type=F32)


def _tn(a, b):
    return lax.dot_general(a, b, (((0,), (0,)), ((), ())), preferred_element_type=F32)


def _mx(x):
    return x.astype(MXU_DTYPE)


def _lane(shape):
    return lax.broadcasted_iota(jnp.int32, shape, 1)


def _sigmoid(z):
    return 1.0 / (1.0 + jnp.exp(-z))


def _rowsum128(x):
    s = jnp.sum(x, axis=0, keepdims=True)
    out = s[:, 0:128]
    for k in range(1, x.shape[1] // 128):
        out = out + s[:, 128 * k:128 * (k + 1)]
    return out


def _swap16(x):
    return jnp.where((_lane(x.shape) & 16) == 0, pltpu.roll(x, 112, 1), pltpu.roll(x, 16, 1))


def _rope(x, cos, sin):
    return x * cos + _swap16(x) * sin


def _rope_t(d, cos, sin):
    return d * cos + _swap16(d * sin)


def _blockdiag64():
    r = lax.broadcasted_iota(jnp.int32, (128, 128), 0) // 64
    c = lax.broadcasted_iota(jnp.int32, (128, 128), 1) // 64
    return (r == c).astype(BF16)


def _seg64(x, m):
    hi = x.astype(BF16)
    lo = (x - hi.astype(F32)).astype(BF16)
    return _nn(hi, m) + _nn(lo, m)


def _rope_tables(seq):
    t = np.arange(seq)
    row = (t // GRID_W).astype(np.float32)
    col = (t % GRID_W).astype(np.float32)
    half = HEAD_DIM // 2
    inv_freq = (np.float32(ROPE_THETA) ** (-np.arange(0, half, 2, dtype=np.float32) / np.float32(half))).astype(np.float32)
    ar = (row[:, None] * inv_freq[None, :]).astype(np.float32).astype(np.float64)
    ac = (col[:, None] * inv_freq[None, :]).astype(np.float32).astype(np.float64)
    cr, sr, cc, sc = np.cos(ar), np.sin(ar), np.cos(ac), np.sin(ac)
    cos64 = np.concatenate([cr, cr, cc, cc], axis=1)
    sin64 = np.concatenate([-sr, sr, -sc, sc], axis=1)
    return jnp.asarray(np.tile(cos64, (1, 2)), F32), jnp.asarray(np.tile(sin64, (1, 2)), F32)


def _attn_inputs(x, mod, g_pre, w_attn, cos, sin, qg2, kg2, own=()):
    seq = x.shape[0]
    bs = 512
    nt = len(own)
    nstep = seq // bs

    def body(x_ref, mod_ref, g_ref, w_ref, cos_ref, sin_ref, qg_ref, kg_ref, *rest):
        srcs, (h_ref, pa_ref, qt_ref, kd_ref, vd_ref) = rest[:nt], rest[nt:nt + 5]
        lands, stage = rest[nt + 5:2 * nt + 5], rest[2 * nt + 5:3 * nt + 5]
        if nt:
            osem = rest[3 * nt + 5]
            step = pl.program_id(0)
            cx, cy, _ = _coords()
            chip = 2 * cx + cy
            loads = [pltpu.make_async_copy(srcs[t], stage[t], osem.at[t]) for t in range(nt)]
            stores = [pltpu.make_async_copy(stage[t], _rows(lands[t], chip * SHARD_ROWS[t], SHARD_ROWS[t]), osem.at[nt + t])
                      for t in range(nt)]

            @pl.when(step == 0)
            def _():
                for cp in loads:
                    cp.start()

            @pl.when(step == min(1, nstep - 1))
            def _():
                for cp in loads:
                    cp.wait()
                for cp in stores:
                    cp.start()

        xv = x_ref[...]
        r = lax.rsqrt(jnp.mean(xv * xv, axis=-1, keepdims=True) + EPS)
        shift = mod_ref[:, 0:D_MODEL]
        scale = mod_ref[:, D_MODEL:2 * D_MODEL]
        hb = ((xv * r) * g_ref[...] * (1.0 + scale) + shift).astype(h_ref.dtype)
        h_ref[...] = hb
        p = _nt(hb, w_ref[...])
        pa_ref[...] = p
        m = _blockdiag64()
        cs, sn = cos_ref[...], sin_ref[...]
        lo = _lane((bs, 128)) < 64
        for pr in range(4):
            q = p[:, QA + 128 * pr:QA + 128 * (pr + 1)]
            r = lax.rsqrt(_seg64(q * q, m) * (1.0 / 64) + EPS)
            qt_ref[:, 128 * pr:128 * (pr + 1)] = (_rope(q * r * qg_ref[...], cs, sn) * (0.125 * LOG2E)).astype(qt_ref.dtype)
        k = p[:, KA:KA + 128]
        r = lax.rsqrt(_seg64(k * k, m) * (1.0 / 64) + EPS)
        kr = _rope(k * r * kg_ref[...], cs, sn)
        ksw = pltpu.roll(kr, 64, 1)
        kd_ref[0] = jnp.where(lo, kr, ksw).astype(kd_ref.dtype)
        kd_ref[1] = jnp.where(lo, ksw, kr).astype(kd_ref.dtype)
        v = p[:, VA:VA + 128]
        vsw = pltpu.roll(v, 64, 1)
        vd_ref[0] = jnp.where(lo, v, vsw).astype(vd_ref.dtype)
        vd_ref[1] = jnp.where(lo, vsw, v).astype(vd_ref.dtype)

        if nt:
            @pl.when(step == nstep - 1)
            def _():
                for cp in stores:
                    cp.wait()

    row = lambda w: pl.BlockSpec((bs, w), lambda i: (i, 0))
    fix = lambda a, b: pl.BlockSpec((a, b), lambda i: (0, 0))
    dup = pl.BlockSpec((2, bs, 128), lambda i: (0, i, 0))
    sds = jax.ShapeDtypeStruct
    anywhere = pl.BlockSpec(memory_space=pl.ANY)
    return pl.pallas_call(
        body, grid=(nstep,), name="attn_inputs",
        in_specs=[row(D_MODEL), fix(1, 3 * D_MODEL), fix(1, D_MODEL), fix(ATTN_COLS, D_MODEL), row(128), row(128), fix(1, 128), fix(1, 128)]
        + [anywhere] * nt,
        out_specs=[row(D_MODEL), row(ATTN_COLS), row(512), dup, dup] + [anywhere] * nt,
        out_shape=[sds((seq, D_MODEL), MXU_DTYPE), sds((seq, ATTN_COLS), F32), sds((seq, 512), MXU_DTYPE),
                   sds((2, seq, 128), MXU_DTYPE), sds((2, seq, 128), MXU_DTYPE)]
        + [sds((N_CHIP * SHARD_ROWS[t], s.shape[1]), s.dtype) for t, s in enumerate(own)],
        scratch_shapes=[pltpu.VMEM(s.shape, s.dtype) for s in own] + ([pltpu.SemaphoreType.DMA((2 * nt,))] if nt else []),
        compiler_params=_cp("arbitrary"))(x, mod, g_pre, w_attn, cos, sin, qg2, kg2, *own)


def _in_proj_dx(x, dp, win_t, dout, mod, g_pre, dep=None):
    seq = x.shape[0]
    bs = 512
    deps, dep_specs = _dep_args(dep, 1)

    def body(x_ref, dp_ref, w_ref, dout_ref, mod_ref, g_ref, *rest):
        gx_ref, dshift_ref, dscale_ref, dg_ref = rest[-4:]

        @pl.when(pl.program_id(0) == 0)
        def _():
            dshift_ref[...] = jnp.zeros_like(dshift_ref)
            dscale_ref[...] = jnp.zeros_like(dscale_ref)
            dg_ref[...] = jnp.zeros_like(dg_ref)

        xv = x_ref[...]
        dh = _nn(dp_ref[...], w_ref[...])
        g = g_ref[...]
        r = lax.rsqrt(jnp.mean(xv * xv, axis=-1, keepdims=True) + EPS)
        xn = xv * r
        scale = mod_ref[:, D_MODEL:2 * D_MODEL]
        dshift_ref[...] += jnp.sum(dh, axis=0, keepdims=True)
        dscale_ref[...] += jnp.sum(dh * (xn * g), axis=0, keepdims=True)
        da = dh * (1.0 + scale)
        dg_ref[...] += jnp.sum(da * xn, axis=0, keepdims=True)
        dxn = da * g
        dx = r * (dxn - xn * jnp.mean(dxn * xn, axis=-1, keepdims=True))
        gx_ref[...] = dout_ref[...] + dx

    row = pl.BlockSpec((bs, D_MODEL), lambda i: (i, 0))
    vec = pl.BlockSpec((1, D_MODEL), lambda i: (0, 0))
    return pl.pallas_call(
        body, grid=(seq // bs,), name="in_proj_dx",
        in_specs=[row, pl.BlockSpec((bs, IN_WIDTH), lambda i: (i, 0)), pl.BlockSpec((IN_WIDTH, D_MODEL), lambda i: (0, 0)), row,
                  pl.BlockSpec((1, 3 * D_MODEL), lambda i: (0, 0)), vec] + dep_specs,
        out_specs=[row, vec, vec, vec],
        out_shape=[jax.ShapeDtypeStruct((seq, D_MODEL), F32)] + [jax.ShapeDtypeStruct((1, D_MODEL), F32)] * 3,
        compiler_params=_cp("arbitrary"))(x, dp, win_t, dout, mod, g_pre, *deps)


def _dep_args(dep, grid_rank):
    if dep is None:
        return [], []
    return [dep], [pl.BlockSpec(dep.shape, lambda *_: (0,) * dep.ndim)]


def _matmul(a, b, *, ta, tb, tm, tn, out_dtype, name, dep=None):
    kdim = a.shape[0] if ta else a.shape[1]
    m = a.shape[1] if ta else a.shape[0]
    n = b.shape[0] if tb else b.shape[1]
    assert m % tm == 0 and n % tn == 0
    deps, dep_specs = _dep_args(dep, 2)

    def body(a_ref, b_ref, *rest):
        o_ref = rest[-1]
        dims = (((0 if ta else 1,), (1 if tb else 0,)), ((), ()))
        o_ref[...] = lax.dot_general(a_ref[...], b_ref[...], dims, preferred_element_type=F32).astype(o_ref.dtype)

    a_spec = pl.BlockSpec((kdim, tm), lambda j, i: (0, i)) if ta else pl.BlockSpec((tm, kdim), lambda j, i: (i, 0))
    b_spec = pl.BlockSpec((tn, kdim), lambda j, i: (j, 0)) if tb else pl.BlockSpec((kdim, tn), lambda j, i: (0, j))
    return pl.pallas_call(
        body, grid=(n // tn, m // tm), name=name, in_specs=[a_spec, b_spec] + dep_specs,
        out_specs=pl.BlockSpec((tm, tn), lambda j, i: (i, j)),
        out_shape=jax.ShapeDtypeStruct((m, n), out_dtype), compiler_params=_cp("parallel", "parallel"))(a, b, *deps)


def _in_proj_rest(h, win_t, cos, sin):
    seq = h.shape[0]
    tm = min(1024, seq)
    ncols = IN_WIDTH - ATTN_COLS
    tn = ncols // 2
    assert ZR_B <= tn and ATTN_COLS % 128 == 0 and tn % 128 == 0

    def body(h_ref, w_ref, cos_ref, sin_ref, p_ref, rq_ref, rk_ref, rv_ref):
        p = _nt(h_ref[...], w_ref[...])
        p_ref[...] = p

        @pl.when(pl.program_id(0) == 0)
        def _():
            cs, sn = cos_ref[...], sin_ref[...]
            for pr in range(2):
                sl = slice(128 * pr, 128 * (pr + 1))
                rq_ref[:, sl] = _rope(p[:, QR_B + 128 * pr:QR_B + 128 * (pr + 1)], cs, sn).astype(rq_ref.dtype)
                rk_ref[:, sl] = (_rope(p[:, KR_B + 128 * pr:KR_B + 128 * (pr + 1)], cs, sn) * 0.125).astype(rk_ref.dtype)
            rv_ref[...] = p[:, VR_B:VR_B + 512].astype(rv_ref.dtype)

    nrow = seq // tm
    row = lambda w: pl.BlockSpec((tm, w), lambda j, i: (i, 0))
    park = lambda w: pl.BlockSpec((tm, w), lambda j, i: (jnp.where(j == 0, i, nrow - 1), 0))
    sds = jax.ShapeDtypeStruct
    return pl.pallas_call(
        body, grid=(2, nrow), name="in_proj_rest",
        in_specs=[row(D_MODEL), pl.BlockSpec((pl.Element(tn), pl.Element(D_MODEL)),
                                             lambda j, i: (pl.multiple_of(ATTN_COLS + j * tn, 128), 0)), row(128), row(128)],
        out_specs=[pl.BlockSpec((tm, tn), lambda j, i: (i, j)), park(256), park(256), park(512)],
        out_shape=[sds((seq, ncols), F32), sds((seq, 256), MXU_DTYPE), sds((seq, 256), MXU_DTYPE), sds((seq, 512), MXU_DTYPE)],
        compiler_params=_cp("arbitrary", "arbitrary"))(h, win_t, cos, sin)


def _halves(kd):
    lo = _lane(kd.shape) < 64
    z = jnp.zeros_like(kd)
    return jnp.concatenate([jnp.where(lo, kd, z), jnp.where(lo, z, kd)], axis=0)


def _attn_fwd(qt, kd, vd, dep=None):
    seq = qt.shape[0]
    bq, bk = min(2048, seq), min(1024, seq)
    nk = seq // bk
    deps, dep_specs = _dep_args(dep, 2)

    def body(q_ref, k_ref, v_ref, *rest):
        o_ref, lse_ref, m_scr, l_scr, acc = rest[-5:]
        j = pl.program_id(1)
        m_scr[...] = jnp.full_like(m_scr, -jnp.inf)
        l_scr[...] = jnp.zeros_like(l_scr)
        acc[...] = jnp.zeros_like(acc)
        sub = min(512, bq)
        nsub = bq // sub
        lo = _lane((sub, 128)) < 64

        def kv_block(n, carry):
            rows = pl.ds(pl.multiple_of(n * bk, bk), bk)
            k2, v2 = _halves(k_ref[rows, :]), _halves(v_ref[rows, :])
            for pr in range(2):
                for hf in range(nsub):
                    qr = slice(hf * sub, (hf + 1) * sub)
                    s2 = _nt(q_ref[qr, 128 * pr:128 * (pr + 1)], k2)
                    ps, alphas = [], []
                    for e in range(2):
                        g = 2 * pr + e
                        s = s2[:, e * bk:(e + 1) * bk]
                        m_prev = m_scr[g, qr]
                        m_next = jnp.maximum(m_prev, jnp.max(s, axis=1, keepdims=True))
                        alpha = jnp.exp2(m_prev - m_next)
                        pe = jnp.exp2(s - jnp.tile(m_next, (1, bk // 128)))
                        l_scr[g, qr] = alpha * l_scr[g, qr] + jnp.sum(pe, axis=1, keepdims=True)
                        m_scr[g, qr] = m_next
                        ps.append(_mx(pe))
                        alphas.append(alpha)
                    o2 = _nn(jnp.concatenate(ps, axis=1), v2)
                    sl = slice(128 * pr, 128 * (pr + 1))
                    acc[qr, sl] = acc[qr, sl] * jnp.where(lo, alphas[0], alphas[1]) + o2
            return carry

        lax.fori_loop(0, nk, kv_block, 0)
        lo_all = _lane((bq, 128)) < 64
        for pr in range(2):
            sl = slice(128 * pr, 128 * (pr + 1))
            o_ref[:, sl] = acc[:, sl] / jnp.where(lo_all, l_scr[2 * pr], l_scr[2 * pr + 1])
        for g in range(4):
            lse = jnp.transpose(m_scr[g] + jnp.log2(l_scr[g]))
            lse_ref[pl.ds(4 * j + g, 1), :] = lse[0:1, :]

    return pl.pallas_call(
        body, grid=(seq // bq, 2), name="attn_fwd",
        in_specs=[pl.BlockSpec((bq, 256), lambda i, j: (i, j)), pl.BlockSpec((None, seq, 128), lambda i, j: (j, 0, 0)),
                  pl.BlockSpec((None, seq, 128), lambda i, j: (j, 0, 0))] + dep_specs,
        out_specs=[pl.BlockSpec((bq, 256), lambda i, j: (i, j)), pl.BlockSpec((8, bq), lambda i, j: (0, i))],
        out_shape=[jax.ShapeDtypeStruct((seq, 512), F32), jax.ShapeDtypeStruct((8, seq), F32)],
        scratch_shapes=[pltpu.VMEM((4, bq, 128), F32), pltpu.VMEM((4, bq, 128), F32), pltpu.VMEM((bq, 256), F32)],
        compiler_params=_cp("parallel", "arbitrary"))(qt, kd, vd, *deps)


def _attn_bwd(qt, kd, vd, do, lse, delta, dep=None):
    seq = qt.shape[0]
    bq, bk = min(1024, seq), min(1024, seq)
    nq, nk = seq // bq, seq // bk
    deps, dep_specs = _dep_args(dep, 3)

    def body(q_ref, do_ref, k_ref, v_ref, lse_ref, del_ref, *rest):
        dq_ref, dk_ref, dv_ref = rest[-3:]
        j, i = pl.program_id(0), pl.program_id(1)
        dq_ref[...] = jnp.zeros_like(dq_ref)

        @pl.when(i == 0)
        def _():
            dk_ref[...] = jnp.zeros_like(dk_ref)
            dv_ref[...] = jnp.zeros_like(dv_ref)

        lo = _lane((bk, 128)) < 64
        big = lambda r: jnp.concatenate([jnp.broadcast_to(r[0], (bk, bq)), jnp.broadcast_to(r[1], (bk, bq))], axis=0)

        def kv_block(n, carry):
            rows = pl.ds(pl.multiple_of(n * bk, bk), bk)
            k2, v2 = _halves(k_ref[rows, :]), _halves(v_ref[rows, :])
            for pr in range(2):
                sl = slice(128 * pr, 128 * (pr + 1))
                qp, dop = q_ref[:, sl], do_ref[:, sl]
                s_t = _nt(k2, qp)
                dp_t = _nt(v2, dop)
                ls = [lse_ref[pl.ds(4 * j + 2 * pr + e, 1), :] for e in range(2)]
                dl = [del_ref[pl.ds(4 * j + 2 * pr + e, 1), :] for e in range(2)]
                p_t = jnp.exp2(s_t - big(ls))
                ds_t = _mx(p_t * (dp_t - big(dl)))
                rv = _nn(_mx(p_t), dop)
                rk = _nn(ds_t, qp) * LN2
                dv_ref[rows, :] += jnp.where(lo, rv[:bk], rv[bk:])
                dk_ref[rows, :] += jnp.where(lo, rk[:bk], rk[bk:])
                dq_ref[:, sl] += _tn(ds_t, k2)
            return carry

        lax.fori_loop(0, nk, kv_block, 0)

    return pl.pallas_call(
        body, grid=(2, nq), name="attn_bwd",
        in_specs=[pl.BlockSpec((bq, 256), lambda j, i: (i, j)), pl.BlockSpec((bq, 256), lambda j, i: (i, j)),
                  pl.BlockSpec((None, seq, 128), lambda j, i: (j, 0, 0)), pl.BlockSpec((None, seq, 128), lambda j, i: (j, 0, 0)),
                  pl.BlockSpec((8, bq), lambda j, i: (0, i)), pl.BlockSpec((8, bq), lambda j, i: (0, i))] + dep_specs,
        out_specs=[pl.BlockSpec((bq, 256), lambda j, i: (i, j)), pl.BlockSpec((None, seq, 128), lambda j, i: (j, 0, 0)),
                   pl.BlockSpec((None, seq, 128), lambda j, i: (j, 0, 0))],
        out_shape=[jax.ShapeDtypeStruct((seq, 512), F32), jax.ShapeDtypeStruct((2, seq, 128), F32),
                   jax.ShapeDtypeStruct((2, seq, 128), F32)],
        compiler_params=_cp("arbitrary", "arbitrary"))(qt, do, kd, vd, lse, delta, *deps)


def _ret_tables(lg_f, lg_b, c):
    idx = jnp.arange(c, dtype=F32)
    diff = idx[:, None] - idx[None, :]
    a, b = lg_f[:, None, None], lg_b[:, None, None]
    low, up = (diff >= 0)[None], (diff < 0)[None]
    dmat = jnp.where(low, jnp.exp(a * jnp.maximum(diff, 0.0)[None]), jnp.exp(b * jnp.maximum(-diff, 0.0)[None]))
    dda = jnp.where(low, diff[None] * jnp.exp(a * jnp.maximum(diff, 0.0)[None]), 0.0)
    ddb = jnp.where(up, -diff[None] * jnp.exp(b * jnp.maximum(-diff, 0.0)[None]), 0.0)
    rep = lambda w: jnp.broadcast_to(w[:, :, None], (RET_HEADS, c, 128))
    wqf = rep(jnp.exp(lg_f[:, None] * (idx + 1.0)[None]))
    wqb = rep(jnp.exp(lg_b[:, None] * (c - idx)[None]))
    wkf = rep(jnp.exp(lg_f[:, None] * (c - 1.0 - idx)[None]))
    wkb = rep(jnp.exp(lg_b[:, None] * idx[None]))
    cdf, cdb = jnp.exp(lg_f * c), jnp.exp(lg_b * c)
    dec_fb = jnp.broadcast_to(jnp.concatenate([cdf, cdb])[:, None], (8, 128))
    dec_bf = jnp.broadcast_to(jnp.concatenate([cdb, cdf])[:, None], (8, 128))
    return dict(dmat=dmat, dda=dda, ddb=ddb, wqf=wqf, wqb=wqb, wkf=wkf, wkb=wkb, dec_fb=dec_fb, dec_bf=dec_bf)


def _ret_states(xk, yv, w_fwd, w_rev, dec, name):
    seq = xk.shape[0]
    c = RET_CHUNK
    nc = seq // c
    grp = min(RET_GROUP, nc)
    ns = nc // grp

    def body(xf_ref, yf_ref, xr_ref, yr_ref, wf_ref, wr_ref, dec_ref, sf_ref, sr_ref, st_f, st_r):
        @pl.when(pl.program_id(0) == 0)
        def _():
            st_f[...] = jnp.zeros_like(st_f)
            st_r[...] = jnp.zeros_like(st_r)

        lane = _lane((c, 128))

        def chunk(g, carry):
            for x_ref, y_ref, w_ref, s_ref, st, base, gg in ((xf_ref, yf_ref, wf_ref, sf_ref, st_f, 0, g),
                                                             (xr_ref, yr_ref, wr_ref, sr_ref, st_r, 4, grp - 1 - g)):
                rows = pl.ds(pl.multiple_of(gg * c, c), c)
                for h in range(RET_HEADS):
                    pr, e = h // 2, h % 2
                    half = (lane < 64) if e == 0 else (lane >= 64)
                    s_ref[gg, h] = st[h].astype(s_ref.dtype)
                    xm = jnp.where(half, x_ref[rows, 128 * pr:128 * (pr + 1)].astype(F32) * w_ref[h], 0.0)
                    st[h] = st[h] * dec_ref[base + h:base + h + 1, :] + _tn(_mx(xm), _mx(y_ref[rows, 128 * h:128 * (h + 1)]))
            return carry

        lax.fori_loop(0, grp, chunk, 0, unroll=True)

    xs = lambda f: pl.BlockSpec((grp * c, 256), f)
    ys = lambda f: pl.BlockSpec((grp * c, 512), f)
    fwd, rev = (lambda n: (n, 0)), (lambda n: (ns - 1 - n, 0))
    wsp = pl.BlockSpec((RET_HEADS, c, 128), lambda n: (0, 0, 0))
    return pl.pallas_call(
        body, grid=(ns,), name=name,
        in_specs=[xs(fwd), ys(fwd), xs(rev), ys(rev), wsp, wsp, pl.BlockSpec((8, 128), lambda n: (0, 0))],
        out_specs=[pl.BlockSpec((grp, RET_HEADS, 128, 128), lambda n: (n, 0, 0, 0)),
                   pl.BlockSpec((grp, RET_HEADS, 128, 128), lambda n: (ns - 1 - n, 0, 0, 0))],
        out_shape=[jax.ShapeDtypeStruct((nc, RET_HEADS, 128, 128), MXU_DTYPE)] * 2,
        scratch_shapes=[pltpu.VMEM((RET_HEADS, 128, 128), F32), pltpu.VMEM((RET_HEADS, 128, 128), F32)],
        compiler_params=_cp("arbitrary"))(xk, yv, xk, yv, w_fwd, w_rev, dec)


def _ret_fwd(rq, rk, rv, rf, rb, tb):
    seq = rq.shape[0]
    c = RET_CHUNK
    grp = min(RET_GROUP, seq // c)

    def body(q_ref, k_ref, v_ref, rf_ref, rb_ref, d_ref, wqf_ref, wqb_ref, o_ref):
        lane = _lane((c, 128))

        def chunk(g, carry):
            rows = pl.ds(pl.multiple_of(g * c, c), c)
            for h in range(RET_HEADS):
                pr, e = h // 2, h % 2
                half = (lane < 64) if e == 0 else (lane >= 64)
                sl = slice(128 * pr, 128 * (pr + 1))
                qp, kp = q_ref[rows, sl], k_ref[rows, sl]
                km = jnp.where(half, kp, jnp.zeros_like(kp))
                sd = _mx(_nt(qp, km) * d_ref[h])
                qf = qp.astype(F32)
                o = _nn(sd, v_ref[rows, 128 * h:128 * (h + 1)])
                o = o + _nn(_mx(qf * wqf_ref[h]), rf_ref[g, h]) + _nn(_mx(qf * wqb_ref[h]), rb_ref[g, h])
                o_ref[rows, 128 * h:128 * (h + 1)] = o
            return carry

        lax.fori_loop(0, grp, chunk, 0, unroll=True)

    st = pl.BlockSpec((grp, RET_HEADS, 128, 128), lambda n: (n, 0, 0, 0))
    wsp = pl.BlockSpec((RET_HEADS, c, 128), lambda n: (0, 0, 0))
    return pl.pallas_call(
        body, grid=(seq // (grp * c),), name="ret_fwd",
        in_specs=[pl.BlockSpec((grp * c, 256), lambda n: (n, 0)), pl.BlockSpec((grp * c, 256), lambda n: (n, 0)),
                  pl.BlockSpec((grp * c, 512), lambda n: (n, 0)), st, st, pl.BlockSpec((RET_HEADS, c, c), lambda n: (0, 0, 0)), wsp, wsp],
        out_specs=pl.BlockSpec((grp * c, 512), lambda n: (n, 0)),
        out_shape=jax.ShapeDtypeStruct((seq, 512), F32), compiler_params=_cp("parallel"))(
            rq, rk, rv, rf, rb, tb["dmat"], tb["wqf"], tb["wqb"])


def _ret_bwd(rq, rk, rv, do, rf, rb, hf, hb, tb):
    seq = rq.shape[0]
    c = RET_CHUNK
    grp = min(RET_GROUP, seq // c)

    def body(q_ref, k_ref, v_ref, do_ref, rf_ref, rb_ref, hf_ref, hb_ref, d_ref, dda_ref, ddb_ref,
             wqf_ref, wqb_ref, wkf_ref, wkb_ref, cdec_ref, dq_ref, dk_ref, dv_ref, dlg_ref):
        @pl.when(pl.program_id(0) == 0)
        def _():
            dlg_ref[...] = jnp.zeros_like(dlg_ref)

        lane = _lane((c, 128))
        pos = lax.broadcasted_iota(jnp.int32, (c, 128), 0).astype(F32)

        def chunk(g, carry):
            rows = pl.ds(pl.multiple_of(g * c, c), c)
            for pr in range(2):
                sl = slice(128 * pr, 128 * (pr + 1))
                qp, kp = q_ref[rows, sl], k_ref[rows, sl]
                qf, kf = qp.astype(F32), kp.astype(F32)
                dq_acc = jnp.zeros((c, 128), F32)
                dk_acc = jnp.zeros((c, 128), F32)
                for e in range(2):
                    h = 2 * pr + e
                    half = (lane < 64) if e == 0 else (lane >= 64)
                    hs = slice(128 * h, 128 * (h + 1))
                    km = jnp.where(half, kp, jnp.zeros_like(kp))
                    kmf = km.astype(F32)
                    vh, doh = v_ref[rows, hs], do_ref[rows, hs]
                    rfh, rbh, hfh, hbh = rf_ref[g, h], rb_ref[g, h], hf_ref[g, h], hb_ref[g, h]
                    s = _nt(qp, km)
                    dpm = _nt(doh, vh)
                    ds_b = _mx(dpm * d_ref[h])
                    sd_b = _mx(s * d_ref[h])
                    dq_if = wqf_ref[h] * _nt(doh, rfh)
                    dq_ib = wqb_ref[h] * _nt(doh, rbh)
                    dq_acc = dq_acc + _nn(ds_b, km) + dq_if + dq_ib
                    dk_sf = wkf_ref[h] * _nt(vh, hfh)
                    dk_sb = wkb_ref[h] * _nt(vh, hbh)
                    dk_acc = dk_acc + jnp.where(half, _tn(ds_b, qp), 0.0) + dk_sf + dk_sb
                    dv = _tn(sd_b, doh) + _nn(_mx(kmf * wkf_ref[h]), hfh) + _nn(_mx(kmf * wkb_ref[h]), hbh)
                    dv_ref[rows, hs] = dv.astype(dv_ref.dtype)
                    sdp = s * dpm
                    hr_f = hfh.astype(F32) * rfh.astype(F32)
                    hr_b = hbh.astype(F32) * rbh.astype(F32)
                    da = (_rowsum128(sdp * dda_ref[h]) + _rowsum128((pos + 1.0) * qf * dq_if)
                          + _rowsum128((c - 1.0 - pos) * kf * dk_sf) + cdec_ref[h:h + 1, :] * _rowsum128(hr_f))
                    db = (_rowsum128(sdp * ddb_ref[h]) + _rowsum128((c - pos) * qf * dq_ib)
                          + _rowsum128(pos * kf * dk_sb) + cdec_ref[4 + h:5 + h, :] * _rowsum128(hr_b))
                    dlg_ref[h:h + 1, :] += da
                    dlg_ref[4 + h:5 + h, :] += db
                dq_ref[rows, sl] = dq_acc
                dk_ref[rows, sl] = dk_acc
            return carry

        lax.fori_loop(0, grp, chunk, 0, unroll=2)

    st = pl.BlockSpec((grp, RET_HEADS, 128, 128), lambda n: (n, 0, 0, 0))
    wsp = pl.BlockSpec((RET_HEADS, c, 128), lambda n: (0, 0, 0))
    dsp = pl.BlockSpec((RET_HEADS, c, c), lambda n: (0, 0, 0))
    x256 = pl.BlockSpec((grp * c, 256), lambda n: (n, 0))
    x512 = pl.BlockSpec((grp * c, 512), lambda n: (n, 0))
    cdec = tb["dec_fb"] * float(c)
    return pl.pallas_call(
        body, grid=(seq // (grp * c),), name="ret_bwd",
        in_specs=[x256, x256, x512, x512, st, st, st, st, dsp, dsp, dsp, wsp, wsp, wsp, wsp,
                  pl.BlockSpec((8, 128), lambda n: (0, 0))],
        out_specs=[x256, x256, x512, pl.BlockSpec((8, 128), lambda n: (0, 0))],
        out_shape=[jax.ShapeDtypeStruct((seq, 256), F32), jax.ShapeDtypeStruct((seq, 256), F32),
                   jax.ShapeDtypeStruct((seq, 512), MXU_DTYPE), jax.ShapeDtypeStruct((8, 128), F32)],
        compiler_params=_cp("arbitrary"))(
            rq, rk, rv, do, rf, rb, hf, hb, tb["dmat"], tb["dda"], tb["ddb"], tb["wqf"], tb["wqb"], tb["wkf"], tb["wkb"], cdec)


def _tail(x, tgt, o_att, o_ret, p, mod, g_post, gn_g, wpt, wout):
    seq = x.shape[0]
    bs = 256
    nb = seq // bs

    def body(x_ref, t_ref, oa_ref, or_ref, za_ref, zr_ref, gl_ref, mod_ref, gp_ref, gn_ref, wpt_ref, wout_ref,
             dout_ref, dza_ref, dzr_ref, dgl_ref, doa_ref, dor_ref, del_ref, gout_ref, gpt_ref,
             dgate_ref, dgpost_ref, dgn_ref, loss_ref, gout_acc, gpt_acc):
        i = pl.program_id(0)

        @pl.when(i == 0)
        def _():
            gout_acc[...] = jnp.zeros_like(gout_acc)
            gpt_acc[...] = jnp.zeros_like(gpt_acc)
            dgate_ref[...] = jnp.zeros_like(dgate_ref)
            dgpost_ref[...] = jnp.zeros_like(dgpost_ref)
            dgn_ref[...] = jnp.zeros_like(dgn_ref)
            loss_ref[...] = jnp.zeros_like(loss_ref)

        oa, za, zr = oa_ref[...], za_ref[...], zr_ref[...]
        sga, sgr = _sigmoid(za), _sigmoid(zr)
        sza, szr = za * sga, zr * sgr
        ya_b = _mx(oa * sza)
        ons, rstds = [], []
        for h in range(RET_HEADS):
            oh = or_ref[:, 128 * h:128 * (h + 1)]
            xc = oh - jnp.mean(oh, axis=-1, keepdims=True)
            rstd = lax.rsqrt(jnp.mean(xc * xc, axis=-1, keepdims=True) + EPS)
            ons.append(xc * rstd)
            rstds.append(rstd)
        on = jnp.concatenate(ons, axis=1)
        yn = on * gn_ref[...]
        yr_b = _mx(yn * szr)
        wpa_t, wpr_t = wpt_ref[:, 0:512], wpt_ref[:, 512:1024]
        a_att = _nt(ya_b, wpa_t)
        a_ret = _nt(yr_b, wpr_t)
        gts = _sigmoid(gl_ref[...])
        g_att, g_ret = gts[:, 0:D_MODEL], gts[:, D_MODEL:2 * D_MODEL]
        merged_b = _mx(g_att * a_att + g_ret * a_ret)
        u = _nn(merged_b, wout_ref[...])
        r = lax.rsqrt(jnp.mean(u * u, axis=-1, keepdims=True) + EPS)
        un = u * r
        gpost = gp_ref[...]
        y = un * gpost
        gate = mod_ref[:, 2 * D_MODEL:3 * D_MODEL]
        err = x_ref[...] + gate * y - t_ref[...]
        loss_ref[...] += (0.5 / D_MODEL) * _rowsum128(err * err)
        dout = err * (1.0 / D_MODEL)
        dout_ref[...] = dout
        dgate_ref[...] += jnp.sum(dout * y, axis=0, keepdims=True)
        dy = dout * gate
        dgpost_ref[...] += jnp.sum(dy * un, axis=0, keepdims=True)
        dun = dy * gpost
        du_b = _mx(r * (dun - un * jnp.mean(dun * un, axis=-1, keepdims=True)))
        dmerged = _nt(du_b, wout_ref[...])
        gout_acc[...] += _tn(merged_b, du_b)
        d_att, d_ret = dmerged * g_att, dmerged * g_ret
        dgl_ref[:, 0:D_MODEL] = (d_att * a_att * (1.0 - g_att)).astype(dgl_ref.dtype)
        dgl_ref[:, D_MODEL:2 * D_MODEL] = (d_ret * a_ret * (1.0 - g_ret)).astype(dgl_ref.dtype)
        d_att_b, d_ret_b = _mx(d_att), _mx(d_ret)
        dya = _nn(d_att_b, wpa_t)
        dyr = _nn(d_ret_b, wpr_t)
        gpt_acc[:, 0:512] += _tn(d_att_b, ya_b)
        gpt_acc[:, 512:1024] += _tn(d_ret_b, yr_b)
        dza_ref[...] = (dya * oa * (sga * (1.0 + za * (1.0 - sga)))).astype(dza_ref.dtype)
        doa = dya * sza
        doa_ref[...] = doa.astype(doa_ref.dtype)
        dt = jnp.transpose(doa * oa)
        del_ref[...] = jnp.sum(dt.reshape(8, HEAD_DIM, bs), axis=1)
        dzr_ref[...] = (dyr * yn * (sgr * (1.0 + zr * (1.0 - sgr)))).astype(dzr_ref.dtype)
        dyn = dyr * szr
        dgn_ref[...] += jnp.sum(dyn * on, axis=0, keepdims=True)
        don = dyn * gn_ref[...]
        for h in range(RET_HEADS):
            hs = slice(128 * h, 128 * (h + 1))
            dh, oh = don[:, hs], ons[h]
            doh = rstds[h] * (dh - jnp.mean(dh, axis=-1, keepdims=True) - oh * jnp.mean(dh * oh, axis=-1, keepdims=True))
            dor_ref[:, hs] = doh.astype(dor_ref.dtype)

        @pl.when(i == nb - 1)
        def _():
            gout_ref[...] = gout_acc[...].astype(gout_ref.dtype)
            gpt_ref[...] = gpt_acc[...].astype(gpt_ref.dtype)

    row = lambda w: pl.BlockSpec((bs, w), lambda i: (i, 0))
    el = lambda w, off: pl.BlockSpec((pl.Element(bs), pl.Element(w)), lambda i: (i * bs, off))
    vec = lambda w: pl.BlockSpec((1, w), lambda i: (0, 0))
    full = pl.BlockSpec((D_MODEL, D_MODEL), lambda i: (0, 0))
    sds = jax.ShapeDtypeStruct
    return pl.pallas_call(
        body, grid=(nb,), name="tail",
        in_specs=[row(D_MODEL), row(D_MODEL), row(512), row(512), el(512, ZA_B), el(512, ZR_B), el(2 * D_MODEL, GL_B),
                  vec(3 * D_MODEL), vec(D_MODEL), vec(512), full, full],
        out_specs=[row(D_MODEL), row(512), row(512), row(2 * D_MODEL), row(512), row(512),
                   pl.BlockSpec((8, bs), lambda i: (0, i)), full, full, vec(D_MODEL), vec(D_MODEL), vec(512), vec(128)],
        out_shape=[sds((seq, D_MODEL), F32), sds((seq, 512), MXU_DTYPE), sds((seq, 512), MXU_DTYPE),
                   sds((seq, 2 * D_MODEL), MXU_DTYPE), sds((seq, 512), MXU_DTYPE), sds((seq, 512), MXU_DTYPE),
                   sds((8, seq), F32), sds((D_MODEL, D_MODEL), MXU_DTYPE), sds((D_MODEL, D_MODEL), MXU_DTYPE),
                   sds((1, D_MODEL), F32), sds((1, D_MODEL), F32), sds((1, 512), F32), sds((1, 128), F32)],
        scratch_shapes=[pltpu.VMEM((D_MODEL, D_MODEL), F32), pltpu.VMEM((D_MODEL, D_MODEL), F32)],
        compiler_params=_cp("arbitrary"))(x, tgt, o_att, o_ret, p, p, p, mod, g_post, gn_g, wpt, wout)


def _assemble(p, cos, sin, qg2, kg2, dqt, dkp, dvp, dza, drq, drk, drv, dzr, dgl):
    seq = p.shape[0]
    bs = 512

    def body(p_ref, cos_ref, sin_ref, qg_ref, kg_ref, dqt_ref, dkp_ref, dvp_ref, dza_ref, drq_ref, drk_ref, drv_ref,
             dzr_ref, dgl_ref, dp_ref, dqg_ref, dkg_ref):
        @pl.when(pl.program_id(0) == 0)
        def _():
            dqg_ref[...] = jnp.zeros_like(dqg_ref)
            dkg_ref[...] = jnp.zeros_like(dkg_ref)

        m = _blockdiag64()
        cs, sn = cos_ref[...], sin_ref[...]
        lo = _lane((bs, 128)) < 64

        def norm_bwd(raw, dn, g, dg_ref):
            r = lax.rsqrt(_seg64(raw * raw, m) * (1.0 / 64) + EPS)
            xn = raw * r
            dg_ref[...] += jnp.sum(dn * xn, axis=0, keepdims=True)
            dxn = dn * g
            return r * (dxn - xn * (_seg64(dxn * xn, m) * (1.0 / 64)))

        for pr in range(4):
            sl = slice(128 * pr, 128 * (pr + 1))
            dn = _rope_t(dqt_ref[:, sl] * 0.125, cs, sn)
            dp_ref[:, QA + 128 * pr:QA + 128 * (pr + 1)] = norm_bwd(p_ref[:, sl], dn, qg_ref[...], dqg_ref).astype(dp_ref.dtype)
        fold = lambda a: a + pltpu.roll(a, 64, 1)
        dk = jnp.where(lo, fold(dkp_ref[0]), fold(dkp_ref[1]))
        dp_ref[:, KA:KA + 128] = norm_bwd(p_ref[:, KA:KA + 128], _rope_t(dk, cs, sn), kg_ref[...], dkg_ref).astype(dp_ref.dtype)
        dp_ref[:, VA:VA + 128] = jnp.where(lo, fold(dvp_ref[0]), fold(dvp_ref[1])).astype(dp_ref.dtype)
        dp_ref[:, ZA:ZA + 512] = dza_ref[...]
        for pr in range(2):
            sl = slice(128 * pr, 128 * (pr + 1))
            dp_ref[:, QR + 128 * pr:QR + 128 * (pr + 1)] = _rope_t(drq_ref[:, sl], cs, sn).astype(dp_ref.dtype)
            dp_ref[:, KR + 128 * pr:KR + 128 * (pr + 1)] = _rope_t(drk_ref[:, sl] * 0.125, cs, sn).astype(dp_ref.dtype)
        dp_ref[:, VR:VR + 512] = drv_ref[...]
        dp_ref[:, ZR:ZR + 512] = dzr_ref[...]
        dp_ref[:, GL:GL + 2 * D_MODEL] = dgl_ref[...]

    row = lambda w: pl.BlockSpec((bs, w), lambda i: (i, 0))
    gsp = pl.BlockSpec((1, 128), lambda i: (0, 0))
    dup = pl.BlockSpec((2, bs, 128), lambda i: (0, i, 0))
    return pl.pallas_call(
        body, grid=(seq // bs,), name="assemble_dp",
        in_specs=[row(768), row(128), row(128), gsp, gsp, row(512), dup, dup, row(512), row(256), row(256), row(512),
                  row(512), row(2 * D_MODEL)],
        out_specs=[row(IN_WIDTH), gsp, gsp],
        out_shape=[jax.ShapeDtypeStruct((seq, IN_WIDTH), MXU_DTYPE), jax.ShapeDtypeStruct((1, 128), F32),
                   jax.ShapeDtypeStruct((1, 128), F32)],
        compiler_params=_cp("arbitrary"))(p, cos, sin, qg2, kg2, dqt, dkp, dvp, dza, drq, drk, drv, dzr, dgl)


def _local_step(x, tgt, mod, g_pre, qn_g, kn_g, w_dec_f, w_dec_b, gn_g, g_post, w_attn, rest_start, rest_wait, send=None, own=()):
    send = send or (lambda name, arrays: None)
    seq = x.shape[0]
    cos, sin = _rope_tables(seq)
    qg2, kg2 = jnp.tile(qn_g, (1, 2)), jnp.tile(kn_g, (1, 2))
    lg_f = jax.nn.log_sigmoid(w_dec_f[0])
    lg_b = jax.nn.log_sigmoid(w_dec_b[0])
    tb = _ret_tables(lg_f, lg_b, RET_CHUNK)

    h, p_a, qt, kd, vd, *placed = _attn_inputs(x, mod, g_pre, w_attn, cos, sin, qg2, kg2, own=own)
    o_att, lse = _attn_fwd(qt, kd, vd, dep=rest_start(qt, *placed))
    win_t, wpt, wout = rest_wait(o_att)
    p_b, rq, rk, rv = _in_proj_rest(h, win_t, cos, sin)
    rf, rb = _ret_states(rk, rv, tb["wkf"], tb["wkb"], tb["dec_fb"], "ret_states_fwd")
    o_ret = _ret_fwd(rq, rk, rv, rf, rb, tb)
    (dout, dza, dzr, dgl, do_att, do_ret, delta, g_out, g_pt, dgate, dgpost, dgn, loss_l) = _tail(
        x, tgt, o_att, o_ret, p_b, mod, g_post, gn_g, wpt, wout)
    dep = send("early", (g_pt, g_out))
    dqt, dkp, dvp = _attn_bwd(qt, kd, vd, do_att, lse, delta, dep=dep)
    hb, hf = _ret_states(rq, do_ret, tb["wqb"], tb["wqf"], tb["dec_bf"], "ret_states_bwd")
    drq, drk, drv, dlg = _ret_bwd(rq, rk, rv, do_ret, rf, rb, hf, hb, tb)
    dp, dqg, dkg = _assemble(p_a, cos, sin, qg2, kg2, dqt, dkp, dvp, dza, drq, drk, drv, dzr, dgl)
    g_in_t = _matmul(dp, h, ta=True, tb=False, tm=256, tn=D_MODEL, out_dtype=MXU_DTYPE, name="in_proj_dw")
    dep = send("late", (g_in_t,))
    grad_x, dshift, dscale, dgpre = _in_proj_dx(x, dp, win_t, dout, mod, g_pre, dep=dep)

    dlg = jnp.sum(dlg, axis=1)
    small = dict(
        dmod=jnp.concatenate([dshift, dscale, dgate], axis=1),
        g_pre=dgpre, g_post=dgpost, gn_g=dgn,
        qn_g=dqg[:, :64] + dqg[:, 64:], kn_g=dkg[:, :64] + dkg[:, 64:],
        w_dec_f=(dlg[0:4] * jax.nn.sigmoid(-w_dec_f[0]))[None], w_dec_b=(dlg[4:8] * jax.nn.sigmoid(-w_dec_b[0]))[None],
        loss=jnp.sum(loss_l, axis=1, keepdims=True))
    return grad_x, g_in_t, g_pt, g_out, small


N_DEV = 8
N_CHIP = 4
HBM_SPEC = pl.BlockSpec(memory_space=pl.ANY)
VMEM_SPEC = pl.BlockSpec(memory_space=pltpu.VMEM)
SHARD_ROWS = (IN_WIDTH // N_CHIP, D_MODEL // N_CHIP, D_MODEL // N_CHIP)


def _coords():
    return lax.axis_index("x"), lax.axis_index("y"), lax.axis_index("c")


def _peer(k):
    x, y, c = _coords()
    return (1 - x if k & 4 else x, 1 - y if k & 2 else y, 1 - c if k & 1 else c)


def _dev_index(p):
    return 4 * p[0] + 2 * p[1] + p[2]


def _rows(ref, start, size):
    return ref.at[pl.ds(pl.multiple_of(start, 16), size), :]


def _remote(src, dst, ssem, rsem, dev):
    return pltpu.make_async_remote_copy(src_ref=src, dst_ref=dst, send_sem=ssem, recv_sem=rsem, device_id=dev,
                                        device_id_type=MESH)


ATTN_CHUNK = 32


def _mod_and_attn_rows(c, w_ada_s, b4, win_s):
    ncol = w_ada_s.shape[1]
    half = ATTN_COLS // 2
    offs = list(range(0, half, ATTN_CHUNK))
    nq = len(offs)
    rows = lambda ref, cc, off: ref.at[pl.ds(pl.multiple_of(cc * half + off, 16), ATTN_CHUNK), :]

    def body(c_ref, w_ref, b_ref, src, cs_ref, mod_ref, out, modsh, modall, ssem, rsem, lsem, asem, bsem, fsem, gsem, hsem):
        x, y, cc = _coords()
        me = _dev_index((x, y, cc))
        chip = me // 2
        sib = (x, y, 1 - cc)
        cs_ref[me] = c_ref[...]
        sends = []
        for k in range(1, N_DEV):
            cp = _remote(c_ref, cs_ref.at[me], ssem.at[k - 1], rsem.at[k - 1], _peer(k))
            cp.start()
            sends.append(cp)
        own = pltpu.make_async_copy(src.at[pl.ds(0, ATTN_COLS), :], out, lsem.at[0])
        bulk = [_remote(rows(src, cc, off), rows(out, cc, off), asem.at[(k2 - 1) * nq + q], bsem.at[q], (k2 // 2, k2 % 2, cc))
                for k2 in (1, 2) for q, off in enumerate(offs)]
        relay_chip = lambda q: 1 + q % 2

        @pl.when(chip == 0)
        def _():
            own.start()
            for cp in bulk:
                cp.start()

        for k in range(1, N_DEV):
            slot = cs_ref.at[_dev_index(_peer(k))]
            _remote(slot, slot, ssem.at[k - 1], rsem.at[k - 1], _peer(k)).wait_recv()
        cs = jnp.concatenate([cs_ref[d] for d in range(N_DEV)], axis=0)
        ms = _nn(cs * _sigmoid(cs), w_ref[...], precision=HIGHEST) + b_ref[pl.ds(chip, 1), :]
        modsh[...] = ms
        modall[chip] = ms
        for k2 in range(1, N_CHIP):
            cp = _remote(modsh, modall.at[chip], ssem.at[6 + k2], rsem.at[6 + k2], _peer(2 * k2))
            cp.start()
            sends.append(cp)

        @pl.when(chip != 0)
        def _():
            passed = []
            relays = [_remote(rows(out, cc, off), rows(out, cc, off), hsem.at[q], bsem.at[q], (1, 1, cc)) for q, off in enumerate(offs)]
            for q, off in enumerate(offs):
                got = rows(out, cc, off)
                _remote(got, got, asem.at[q], bsem.at[q], (0, 0, cc)).wait_recv()
                cp = _remote(got, got, fsem.at[q], gsem.at[q], sib)
                cp.start()
                passed.append(cp)
                pl.when(chip == relay_chip(q))(relays[q].start)
            for q, off in enumerate(offs):
                got = rows(out, 1 - cc, off)
                _remote(got, got, fsem.at[q], gsem.at[q], sib).wait_recv()
            for cp in passed:
                cp.wait_send()
            for q in range(nq):
                pl.when(chip == relay_chip(q))(relays[q].wait_send)

        for k2 in range(1, N_CHIP):
            slot = modall.at[_dev_index(_peer(2 * k2)) // 2]
            _remote(slot, slot, ssem.at[6 + k2], rsem.at[6 + k2], _peer(2 * k2)).wait_recv()
        for jj in range(N_CHIP):
            mod_ref[:, ncol * jj:ncol * (jj + 1)] = modall[jj, pl.ds(me, 1), :]
        for cp in sends:
            cp.wait_send()

        @pl.when(chip == 0)
        def _():
            for cp in bulk:
                cp.wait_send()
            own.wait()

    dma = pltpu.SemaphoreType.DMA
    return pl.pallas_call(
        body, name="mod_and_attn_rows", in_specs=[VMEM_SPEC] * 4, out_specs=[VMEM_SPEC, VMEM_SPEC, HBM_SPEC],
        out_shape=[jax.ShapeDtypeStruct((N_DEV, 1, D_MODEL), F32), jax.ShapeDtypeStruct((1, N_CHIP * ncol), F32),
                   jax.ShapeDtypeStruct((ATTN_COLS, win_s.shape[1]), win_s.dtype)],
        scratch_shapes=[pltpu.VMEM((N_DEV, ncol), F32), pltpu.VMEM((N_CHIP, N_DEV, ncol), F32), dma((10,)), dma((10,)),
                        dma((1,)), dma((2 * nq,)), dma((nq,)), dma((nq,)), dma((nq,)), dma((nq,))],
        compiler_params=_cp())(c, w_ada_s, b4, win_s)


GATHER_CHUNK = 32


def _chunks(kinds, chunk):
    out = []
    for t, kind in enumerate(kinds):
        half = SHARD_ROWS[kind] // 2
        step = chunk if half % chunk == 0 else half
        out += [(t, off, step) for off in range(0, half, step)]
    return out


SEM_SPEC = pl.BlockSpec(memory_space=pltpu.SEMAPHORE)
HBM_ONLY = pl.BlockSpec(memory_space=pltpu.HBM)
DATAFLOW = pltpu.SideEffectType.DATAFLOW_SIDE_EFFECTING


def _gather_rest_start(shards, zones, dep):
    n = len(shards)

    def body(*refs):
        srcs, lands = refs[:n], refs[n:2 * n]
        ssem, rsem = refs[2 * n + 1], refs[2 * n + 2]
        token = refs[-1]
        x, y, _ = _coords()
        chip = 2 * x + y
        for k2 in range(1, N_CHIP):
            for t in range(n):
                q = (k2 - 1) * n + t
                _remote(srcs[t], _rows(lands[t], chip * SHARD_ROWS[t], SHARD_ROWS[t]), ssem.at[q], rsem.at[q], _peer(2 * k2)).start()
        token[...] = jnp.zeros_like(token)

    hbm = lambda a: pltpu.with_memory_space_constraint(a, pltpu.HBM)
    dma = pltpu.SemaphoreType.DMA
    outs = pl.pallas_call(
        body, name="gather_rest", in_specs=[HBM_ONLY] * (2 * n) + [HBM_SPEC],
        out_specs=[SEM_SPEC, SEM_SPEC] + [HBM_ONLY] * (2 * n) + [VMEM_SPEC],
        out_shape=[dma((3 * n,)), dma((3 * n,))] + [pltpu.HBM(a.shape, a.dtype) for a in list(shards) + list(zones)]
        + [jax.ShapeDtypeStruct((8, 128), F32)],
        input_output_aliases={i: 2 + i for i in range(2 * n)},
        compiler_params=pltpu.CompilerParams(has_side_effects=DATAFLOW))(*[hbm(a) for a in list(shards) + list(zones)], dep)
    return outs[0], outs[1], outs[2:2 + n], outs[2 + n:2 + 2 * n], outs[-1]


def _gather_rest_wait(started, after):
    ssem, rsem, shards, zones, _ = started
    n = len(shards)

    def body(*refs):
        srcs, lands = refs[:n], refs[n:2 * n]
        ssem_ref, rsem_ref = refs[2 * n], refs[2 * n + 1]
        for k2 in range(1, N_CHIP):
            pchip = _dev_index(_peer(2 * k2)) // 2
            for t in range(n):
                q = (k2 - 1) * n + t
                cp = _remote(srcs[t], _rows(lands[t], pchip * SHARD_ROWS[t], SHARD_ROWS[t]), ssem_ref.at[q], rsem_ref.at[q], _peer(2 * k2))
                cp.wait_send()
                cp.wait_recv()

    outs = pl.pallas_call(
        body, name="gather_rest_wait", in_specs=[HBM_ONLY] * (2 * n) + [SEM_SPEC, SEM_SPEC, HBM_SPEC], out_specs=[HBM_ONLY] * (2 * n),
        out_shape=[pltpu.HBM(a.shape, a.dtype) for a in list(shards) + list(zones)],
        input_output_aliases={i: i for i in range(2 * n)},
        compiler_params=pltpu.CompilerParams(has_side_effects=DATAFLOW))(*shards, *zones, ssem, rsem, after)
    return outs[n:]


def _reduced_by(ref, t, dev):
    return _rows(ref, (dev // 2) * SHARD_ROWS[t] + (dev % 2) * (SHARD_ROWS[t] // 2), SHARD_ROWS[t] // 2)


def _scatter_start(grads, kinds, name):
    n = len(grads)

    def body(*refs):
        srcs, lands = refs[:n], refs[n:2 * n]
        ssem, rsem = refs[2 * n], refs[2 * n + 1]
        token = refs[-1]
        me = _dev_index(_coords())
        for k in range(1, N_DEV):
            for i, t in enumerate(kinds):
                q = (k - 1) * n + i
                _remote(_reduced_by(srcs[i], t, _dev_index(_peer(k))), lands[i].at[me], ssem.at[q], rsem.at[q], _peer(k)).start()
        token[...] = jnp.zeros_like(token)

    zones = [lax.empty((N_DEV, SHARD_ROWS[t] // 2, g.shape[1]), g.dtype) for g, t in zip(grads, kinds)]
    hbm = lambda a: pltpu.with_memory_space_constraint(a, pltpu.HBM)
    dma = pltpu.SemaphoreType.DMA
    outs = pl.pallas_call(
        body, name=name, in_specs=[HBM_ONLY] * (2 * n), out_specs=[SEM_SPEC, SEM_SPEC] + [HBM_ONLY] * (2 * n) + [VMEM_SPEC],
        out_shape=[dma((7 * n,)), dma((7 * n,))] + [pltpu.HBM(a.shape, a.dtype) for a in list(grads) + zones]
        + [jax.ShapeDtypeStruct((8, 128), F32)],
        input_output_aliases={i: 2 + i for i in range(2 * n)},
        compiler_params=pltpu.CompilerParams(has_side_effects=DATAFLOW))(*[hbm(a) for a in list(grads) + zones])
    return outs[0], outs[1], outs[2:2 + n], outs[2 + n:2 + 2 * n], outs[-1]


def _scatter_wait(started, kinds, after, name):
    ssem, rsem, grads, zones, _ = started
    n = len(grads)

    def body(*refs):
        srcs, lands = refs[:n], refs[n:2 * n]
        ssem_ref, rsem_ref = refs[2 * n], refs[2 * n + 1]
        for k in range(1, N_DEV):
            peer = _dev_index(_peer(k))
            for i, t in enumerate(kinds):
                q = (k - 1) * n + i
                cp = _remote(_reduced_by(srcs[i], t, peer), lands[i].at[peer], ssem_ref.at[q], rsem_ref.at[q], _peer(k))
                cp.wait_send()
                cp.wait_recv()

    outs = pl.pallas_call(
        body, name=name, in_specs=[HBM_ONLY] * (2 * n) + [SEM_SPEC, SEM_SPEC, HBM_SPEC], out_specs=[HBM_ONLY] * (2 * n),
        out_shape=[pltpu.HBM(a.shape, a.dtype) for a in list(grads) + list(zones)],
        input_output_aliases={i: i for i in range(2 * n)},
        compiler_params=pltpu.CompilerParams(has_side_effects=DATAFLOW))(*grads, *zones, ssem, rsem, after)
    return outs[:n], outs[n:]


def _grad_finish(grads, zones, kinds, name):
    nt = len(grads)
    pieces = _chunks(kinds, GATHER_CHUNK)
    npc = len(pieces)
    shard = [SHARD_ROWS[k] for k in kinds]

    def body(*refs):
        srcs, lands, outs = refs[:nt], refs[nt:2 * nt], refs[2 * nt:3 * nt]
        slots, sums = refs[3 * nt:4 * nt], refs[4 * nt:5 * nt]
        lsem, osem, xsem, ysem = refs[5 * nt:]
        x, y, c = _coords()
        me = _dev_index((x, y, c))
        sib = (x, y, 1 - c)
        loads = [pltpu.make_async_copy(_reduced_by(srcs[t], kinds[t], me), slots[t].at[me], lsem.at[t]) for t in range(nt)]
        for k in range(1, N_DEV):
            peer = _dev_index(_peer(k))
            loads += [pltpu.make_async_copy(lands[t].at[peer], slots[t].at[peer], lsem.at[k * nt + t]) for t in range(nt)]
        for cp in loads:
            cp.start()
        for cp in loads:
            cp.wait()
        sends = []
        place = lambda t, cc, off, n: _rows(outs[t], cc * (shard[t] // 2) + off, n)
        stores = []
        for q, (t, off, n) in enumerate(pieces):
            r = pl.ds(off, n)
            acc = slots[t][0, r, :].astype(F32)
            for d in range(1, N_DEV):
                acc = acc + slots[t][d, r, :].astype(F32)
            sums[t][r, :] = acc
            keep = pltpu.make_async_copy(sums[t].at[r, :], place(t, c, off, n), osem.at[q])
            give = _remote(sums[t].at[r, :], place(t, c, off, n), xsem.at[q], ysem.at[q], sib)
            keep.start()
            give.start()
            stores.append(keep)
            sends.append(give)
        for q, (t, off, n) in enumerate(pieces):
            got = place(t, 1 - c, off, n)
            _remote(got, got, xsem.at[q], ysem.at[q], sib).wait_recv()
        for cp in sends:
            cp.wait_send()
        for cp in stores:
            cp.wait()

    dma = pltpu.SemaphoreType.DMA
    return pl.pallas_call(
        body, name=name, in_specs=[HBM_SPEC] * (2 * nt), out_specs=[HBM_SPEC] * nt,
        out_shape=[jax.ShapeDtypeStruct((shard[t], g.shape[1]), F32) for t, g in enumerate(grads)],
        scratch_shapes=([pltpu.VMEM((N_DEV, shard[t] // 2, g.shape[1]), g.dtype) for t, g in enumerate(grads)]
                        + [pltpu.VMEM((shard[t] // 2, g.shape[1]), F32) for t, g in enumerate(grads)]
                        + [dma((N_DEV * nt,)), dma((npc,)), dma((npc,)), dma((npc,))]),
        compiler_params=_cp())(*grads, *zones)


def _adam_math(w, g, m, v):
    m = ADAM_B1 * m + (1.0 - ADAM_B1) * g
    v = ADAM_B2 * v + (1.0 - ADAM_B2) * (g * g)
    m_hat = m / (1.0 - ADAM_B1 ** ADAM_STEP)
    v_hat = v / (1.0 - ADAM_B2 ** ADAM_STEP)
    return -ADAM_LR * (m_hat / (jnp.sqrt(v_hat) + ADAM_EPS) + ADAM_WD * w), m, v


def _adamw(items, name, nblk=4):
    n = len(items)

    def body(*refs):
        ins, outs = refs[:4 * n], refs[4 * n:]
        for t in range(n):
            w_ref, g_ref, m_ref, v_ref = ins[4 * t:4 * t + 4]
            d_ref, nm_ref, nv_ref = outs[3 * t:3 * t + 3]
            d_ref[...], nm_ref[...], nv_ref[...] = _adam_math(w_ref[...], g_ref[...], m_ref[...], v_ref[...])

    specs = [pl.BlockSpec((it[0].shape[0] // nblk, it[0].shape[1]), lambda i: (i, 0)) for it in items]
    outs = pl.pallas_call(
        body, grid=(nblk,), name=name, in_specs=[s for s in specs for _ in range(4)], out_specs=[s for s in specs for _ in range(3)],
        out_shape=[jax.ShapeDtypeStruct(it[0].shape, F32) for it in items for _ in range(3)],
        compiler_params=_cp("parallel"))(*[a for it in items for a in it])
    return [tuple(outs[3 * t:3 * t + 3]) for t in range(n)]


PACK = (("b_ada", 3 * D_MODEL), ("g_pre", D_MODEL), ("g_post", D_MODEL), ("gn_g", 512), ("qn_g", 64), ("kn_g", 64),
        ("w_dec_f", 4), ("w_dec_b", 4), ("loss", 1))
PACK_OFFSETS = {}
PACK_WIDTH = 0
for _name, _n in PACK:
    PACK_OFFSETS[_name] = PACK_WIDTH
    PACK_WIDTH += -(-_n // 128) * 128
SMALL_PARAMS = tuple(name for name, _ in PACK if name != "loss")


def _pack(parts):
    cols = []
    for name, n in PACK:
        pad = -(-n // 128) * 128 - n
        cols.append(parts[name].reshape(1, n).astype(F32))
        if pad:
            cols.append(jnp.zeros((1, pad), F32))
    return jnp.concatenate(cols, axis=1)


def _small_reduce(vec, cs, deps):
    ncol = 3 * D_MODEL // N_CHIP

    def body(vec_ref, cs_ref, *rest):
        tot_ref, gwa_ref, gat, ssem, rsem = rest[-5:]
        me = _dev_index(_coords())
        chip = me // 2
        gat[me] = vec_ref[...]
        sends = []
        for k in range(1, N_DEV):
            cp = _remote(vec_ref, gat.at[me], ssem.at[k - 1], rsem.at[k - 1], _peer(k))
            cp.start()
            sends.append(cp)
        for k in range(1, N_DEV):
            slot = gat.at[_dev_index(_peer(k))]
            _remote(slot, slot, ssem.at[k - 1], rsem.at[k - 1], _peer(k)).wait_recv()
        rows = [gat[d] for d in range(N_DEV)]
        tot = rows[0]
        for d in range(1, N_DEV):
            tot = tot + rows[d]
        tot_ref[...] = tot
        cs = cs_ref[...]
        ca_t = jnp.transpose(jnp.concatenate([cs * _sigmoid(cs), jnp.zeros((128 - N_DEV, D_MODEL), F32)], axis=0))
        dm = jnp.concatenate(rows + [jnp.zeros((128 - N_DEV, PACK_WIDTH), F32)], axis=0)
        for jj in range(N_CHIP):
            @pl.when(chip == jj)
            def _():
                gwa_ref[...] = _nn(ca_t, dm[:, ncol * jj:ncol * (jj + 1)], precision=HIGHEST)
        for cp in sends:
            cp.wait_send()

    return pl.pallas_call(
        body, name="small_reduce", in_specs=[VMEM_SPEC, VMEM_SPEC] + [HBM_SPEC] * len(deps), out_specs=[VMEM_SPEC] * 2,
        out_shape=[jax.ShapeDtypeStruct((1, PACK_WIDTH), F32), jax.ShapeDtypeStruct((D_MODEL, ncol), F32)],
        scratch_shapes=[pltpu.VMEM((N_DEV, 1, PACK_WIDTH), F32), pltpu.SemaphoreType.DMA((7,)), pltpu.SemaphoreType.DMA((7,))],
        compiler_params=_cp())(vec, cs, *deps)


def _small_adamw(tot, given):
    sizes = dict(PACK)
    np_ = len(SMALL_PARAMS)

    def body(*refs):
        tot_ref = refs[0]
        wmv = refs[1:1 + 3 * np_]
        outs = refs[1 + 3 * np_:1 + 7 * np_]
        loss_ref = refs[-1]
        for i, name in enumerate(SMALL_PARAMS):
            off, n = PACK_OFFSETS[name], sizes[name]
            g = tot_ref[:, off:off + n]
            w_ref, m_ref, v_ref = wmv[3 * i:3 * i + 3]
            outs[i][...] = g
            outs[np_ + i][...], outs[2 * np_ + i][...], outs[3 * np_ + i][...] = _adam_math(w_ref[...], g, m_ref[...], v_ref[...])
        loss_ref[...] = tot_ref[:, PACK_OFFSETS["loss"]:PACK_OFFSETS["loss"] + 1]

    flat = [a for name in SMALL_PARAMS for a in given[name]]
    shapes = [jax.ShapeDtypeStruct((1, sizes[name]), F32) for name in SMALL_PARAMS]
    res = pl.pallas_call(body, name="small_adamw", in_specs=[VMEM_SPEC] * (1 + 3 * np_), out_specs=[VMEM_SPEC] * (4 * np_ + 1),
                         out_shape=shapes * 4 + [jax.ShapeDtypeStruct((1, 1), F32)], compiler_params=_cp())(tot, *flat)
    return [dict(zip(SMALL_PARAMS, res[k * np_:(k + 1) * np_])) for k in range(4)], res[-1]


def kernel(x, c, w_ada, b_ada, g_pre, w_in, qn_g, kn_g, w_dec_f, w_dec_b, gn_g, w_pa, w_pr, w_out, g_post, loss_target, m_w_ada, m_b_ada, m_g_pre, m_w_in, m_qn_g, m_kn_g, m_w_dec_f, m_w_dec_b, m_gn_g, m_w_pa, m_w_pr, m_w_out, m_g_post, v_w_ada, v_b_ada, v_g_pre, v_w_in, v_qn_g, v_kn_g, v_w_dec_f, v_w_dec_b, v_gn_g, v_w_pa, v_w_pr, v_w_out, v_g_post):
    shards = (w_in[0].T.astype(MXU_DTYPE), jnp.concatenate([w_pa[0].T, w_pr[0].T], axis=1).astype(MXU_DTYPE),
              w_out[0].astype(MXU_DTYPE))
    cs, mod, w_attn = _mod_and_attn_rows(c, w_ada[0], b_ada.reshape(N_CHIP, 3 * D_MODEL // N_CHIP), shards[0])
    started = {}

    def rest_start(dep, *placed):
        started["rest"] = _gather_rest_start(shards, placed, dep)
        return started["rest"][-1]

    def rest_wait(after):
        return _gather_rest_wait(started["rest"], after)

    kinds = {"early": (1, 2), "late": (0,)}

    def send(name, arrays):
        started[name] = _scatter_start(arrays, kinds[name], "grad_scatter_" + name)
        return started[name][-1]

    grad_x, _, _, _, small = _local_step(x[0], loss_target[0], mod, g_pre, qn_g, kn_g, w_dec_f, w_dec_b,
                                         gn_g, g_post, w_attn, rest_start, rest_wait, send=send, own=shards)
    small = dict(small)
    small["b_ada"] = small.pop("dmod")
    given = dict(b_ada=(b_ada, m_b_ada, v_b_ada), g_pre=(g_pre, m_g_pre, v_g_pre), g_post=(g_post, m_g_post, v_g_post),
                 gn_g=(gn_g, m_gn_g, v_gn_g), qn_g=(qn_g, m_qn_g, v_qn_g), kn_g=(kn_g, m_kn_g, v_kn_g),
                 w_dec_f=(w_dec_f, m_w_dec_f, v_w_dec_f), w_dec_b=(w_dec_b, m_w_dec_b, v_w_dec_b))
    grads, deltas, new_m, new_v = {}, {}, {}, {}

    def update(call, back, **items):
        done = _adamw(list(items.values()), "adamw_" + call)
        for (name, (w, g, m, v)), (d, nm, nv) in zip(items.items(), done):
            grads[name], deltas[name], new_m[name], new_v[name] = back(g), back(d), back(nm), back(nv)
        return tuple(d for d, _, _ in done)

    lead = lambda a: a[None]
    (g_pt, g_out), (z_pt, z_out) = _scatter_wait(started["early"], kinds["early"], grad_x, "grad_scatter_early_wait")
    r_pt, r_out = _grad_finish((g_pt, g_out), (z_pt, z_out), kinds["early"], "grad_finish_early")
    early = update("early", lead, w_pa=(w_pa[0], r_pt[:, :512].T, m_w_pa[0], v_w_pa[0]),
                   w_pr=(w_pr[0], r_pt[:, 512:].T, m_w_pr[0], v_w_pr[0]), w_out=(w_out[0], r_out, m_w_out[0], v_w_out[0]))
    tot, g_w_ada = _small_reduce(_pack(small), cs.reshape(N_DEV, D_MODEL), early)
    small_parts, loss = _small_adamw(tot, given)
    for dst, part in zip((grads, deltas, new_m, new_v), small_parts):
        dst.update(part)
    (last,) = update("w_ada", lead, w_ada=(w_ada[0], g_w_ada, m_w_ada[0], v_w_ada[0]))

    (g_in_t,), (z_in,) = _scatter_wait(started["late"], kinds["late"], last, "grad_scatter_late_wait")
    (r_in,) = _grad_finish((g_in_t,), (z_in,), kinds["late"], "grad_finish_late")
    tr = lambda a: a[0].T
    update("w_in", lambda a: a.T[None], w_in=(tr(w_in), r_in, tr(m_w_in), tr(v_w_in)))
    order = ("w_ada", "b_ada", "g_pre", "w_in", "qn_g", "kn_g", "w_dec_f", "w_dec_b", "gn_g", "w_pa", "w_pr", "w_out", "g_post")
    return (loss[0, 0], grad_x[None], *[grads[n] for n in order], *[deltas[n] for n in order],
            *[new_m[n] for n in order], *[new_v[n] for n in order])
```

```python
import jax
import jax.numpy as jnp
import numpy as np
from jax import lax
from jax.experimental import pallas as pl
from jax.experimental.pallas import tpu as pltpu

F32 = jnp.float32
BF16 = jnp.bfloat16
MXU_DTYPE = jnp.bfloat16

D_MODEL = 1024
GRID_W = 64
HEAD_DIM = 64
ROPE_THETA = 10000.0
EPS = 1e-6
RET_HEADS = 4
RET_CHUNK = 256
RET_GROUP = 4
IN_WIDTH = 4864
QA, KA, VA, ZA, QR, KR, VR, ZR, GL = 0, 512, 640, 768, 1280, 1536, 1792, 2304, 2816
ATTN_COLS = ZA
ZA_B, QR_B, KR_B, VR_B, ZR_B, GL_B = (c - ATTN_COLS for c in (ZA, QR, KR, VR, ZR, GL))

ADAM_LR, ADAM_B1, ADAM_B2, ADAM_EPS, ADAM_WD, ADAM_STEP = 0.001, 0.9, 0.999, 1e-08, 0.01, 10

VMEM_LIMIT = 56 * 1024 * 1024
MESH = pl.DeviceIdType.MESH
HIGHEST = lax.Precision.HIGHEST
LOG2E = 1.4426950408889634
LN2 = 0.6931471805599453


def _cp(*sem, **kw):
    if sem:
        kw["dimension_semantics"] = sem
    return pltpu.CompilerParams(vmem_limit_bytes=VMEM_LIMIT, **kw)


def _nn(a, b, **kw):
    return lax.dot_general(a, b, (((1,), (0,)), ((), ())), preferred_element_type=F32, **kw)


def _nt(a, b):
    return lax.dot_general(a, b, (((1,), (1,)), ((), ())), preferred_element_type=F32)


def _tn(a, b):
    return lax.dot_general(a, b, (((0,), (0,)), ((), ())), preferred_element_type=F32)


def _mx(x):
    return x.astype(MXU_DTYPE)


def _lane(shape):
    return lax.broadcasted_iota(jnp.int32, shape, 1)


def _sigmoid(z):
    return 1.0 / (1.0 + jnp.exp(-z))


def _rowsum128(x):
    s = jnp.sum(x, axis=0, keepdims=True)
    out = s[:, 0:128]
    for k in range(1, x.shape[1] // 128):
        out = out + s[:, 128 * k:128 * (k + 1)]
    return out


def _swap16(x):
    return jnp.where((_lane(x.shape) & 16) == 0, pltpu.roll(x, 112, 1), pltpu.roll(x, 16, 1))


def _rope(x, cos, sin):
    return x * cos + _swap16(x) * sin


def _rope_t(d, cos, sin):
    return d * cos + _swap16(d * sin)


def _blockdiag64():
    r = lax.broadcasted_iota(jnp.int32, (128, 128), 0) // 64
    c = lax.broadcasted_iota(jnp.int32, (128, 128), 1) // 64
    return (r == c).astype(BF16)


def _seg64(x, m):
    hi = x.astype(BF16)
    lo = (x - hi.astype(F32)).astype(BF16)
    return _nn(hi, m) + _nn(lo, m)


def _rope_tables(seq):
    t = np.arange(seq)
    row = (t // GRID_W).astype(np.float32)
    col = (t % GRID_W).astype(np.float32)
    half = HEAD_DIM // 2
    inv_freq = (np.float32(ROPE_THETA) ** (-np.arange(0, half, 2, dtype=np.float32) / np.float32(half))).astype(np.float32)
    ar = (row[:, None] * inv_freq[None, :]).astype(np.float32).astype(np.float64)
    ac = (col[:, None] * inv_freq[None, :]).astype(np.float32).astype(np.float64)
    cr, sr, cc, sc = np.cos(ar), np.sin(ar), np.cos(ac), np.sin(ac)
    cos64 = np.concatenate([cr, cr, cc, cc], axis=1)
    sin64 = np.concatenate([-sr, sr, -sc, sc], axis=1)
    return jnp.asarray(np.tile(cos64, (1, 2)), F32), jnp.asarray(np.tile(sin64, (1, 2)), F32)


def _attn_inputs(x, mod, g_pre, w_attn, cos, sin, qg2, kg2, own=()):
    seq = x.shape[0]
    bs = 512
    nt = len(own)
    nstep = seq // bs

    def body(x_ref, mod_ref, g_ref, w_ref, cos_ref, sin_ref, qg_ref, kg_ref, *rest):
        srcs, (h_ref, pa_ref, qt_ref, kd_ref, vd_ref) = rest[:nt], rest[nt:nt + 5]
        lands, stage = rest[nt + 5:2 * nt + 5], rest[2 * nt + 5:3 * nt + 5]
        if nt:
            osem = rest[3 * nt + 5]
            step = pl.program_id(0)
            cx, cy, _ = _coords()
            chip = 2 * cx + cy
            loads = [pltpu.make_async_copy(srcs[t], stage[t], osem.at[t]) for t in range(nt)]
            stores = [pltpu.make_async_copy(stage[t], _rows(lands[t], chip * SHARD_ROWS[t], SHARD_ROWS[t]), osem.at[nt + t])
                      for t in range(nt)]

            @pl.when(step == 0)
            def _():
                for cp in loads:
                    cp.start()

            @pl.when(step == min(1, nstep - 1))
            def _():
                for cp in loads:
                    cp.wait()
                for cp in stores:
                    cp.start()

        xv = x_ref[...]
        r = lax.rsqrt(jnp.mean(xv * xv, axis=-1, keepdims=True) + EPS)
        shift = mod_ref[:, 0:D_MODEL]
        scale = mod_ref[:, D_MODEL:2 * D_MODEL]
        hb = ((xv * r) * g_ref[...] * (1.0 + scale) + shift).astype(h_ref.dtype)
        h_ref[...] = hb
        p = _nt(hb, w_ref[...])
        pa_ref[...] = p
        m = _blockdiag64()
        cs, sn = cos_ref[...], sin_ref[...]
        lo = _lane((bs, 128)) < 64
        for pr in range(4):
            q = p[:, QA + 128 * pr:QA + 128 * (pr + 1)]
            r = lax.rsqrt(_seg64(q * q, m) * (1.0 / 64) + EPS)
            qt_ref[:, 128 * pr:128 * (pr + 1)] = (_rope(q * r * qg_ref[...], cs, sn) * (0.125 * LOG2E)).astype(qt_ref.dtype)
        k = p[:, KA:KA + 128]
        r = lax.rsqrt(_seg64(k * k, m) * (1.0 / 64) + EPS)
        kr = _rope(k * r * kg_ref[...], cs, sn)
        ksw = pltpu.roll(kr, 64, 1)
        kd_ref[0] = jnp.where(lo, kr, ksw).astype(kd_ref.dtype)
        kd_ref[1] = jnp.where(lo, ksw, kr).astype(kd_ref.dtype)
        v = p[:, VA:VA + 128]
        vsw = pltpu.roll(v, 64, 1)
        vd_ref[0] = jnp.where(lo, v, vsw).astype(vd_ref.dtype)
        vd_ref[1] = jnp.where(lo, vsw, v).astype(vd_ref.dtype)

        if nt:
            @pl.when(step == nstep - 1)
            def _():
                for cp in stores:
                    cp.wait()

    row = lambda w: pl.BlockSpec((bs, w), lambda i: (i, 0))
    fix = lambda a, b: pl.BlockSpec((a, b), lambda i: (0, 0))
    dup = pl.BlockSpec((2, bs, 128), lambda i: (0, i, 0))
    sds = jax.ShapeDtypeStruct
    anywhere = pl.BlockSpec(memory_space=pl.ANY)
    return pl.pallas_call(
        body, grid=(nstep,), name="attn_inputs",
        in_specs=[row(D_MODEL), fix(1, 3 * D_MODEL), fix(1, D_MODEL), fix(ATTN_COLS, D_MODEL), row(128), row(128), fix(1, 128), fix(1, 128)]
        + [anywhere] * nt,
        out_specs=[row(D_MODEL), row(ATTN_COLS), row(512), dup, dup] + [anywhere] * nt,
        out_shape=[sds((seq, D_MODEL), MXU_DTYPE), sds((seq, ATTN_COLS), F32), sds((seq, 512), MXU_DTYPE),
                   sds((2, seq, 128), MXU_DTYPE), sds((2, seq, 128), MXU_DTYPE)]
        + [sds((N_CHIP * SHARD_ROWS[t], s.shape[1]), s.dtype) for t, s in enumerate(own)],
        scratch_shapes=[pltpu.VMEM(s.shape, s.dtype) for s in own] + ([pltpu.SemaphoreType.DMA((2 * nt,))] if nt else []),
        compiler_params=_cp("arbitrary"))(x, mod, g_pre, w_attn, cos, sin, qg2, kg2, *own)


def _in_proj_dx(x, dp, win_t, dout, mod, g_pre, dep=None):
    seq = x.shape[0]
    bs = 512
    deps, dep_specs = _dep_args(dep, 1)

    def body(x_ref, dp_ref, w_ref, dout_ref, mod_ref, g_ref, *rest):
        gx_ref, dshift_ref, dscale_ref, dg_ref = rest[-4:]

        @pl.when(pl.program_id(0) == 0)
        def _():
            dshift_ref[...] = jnp.zeros_like(dshift_ref)
            dscale_ref[...] = jnp.zeros_like(dscale_ref)
            dg_ref[...] = jnp.zeros_like(dg_ref)

        xv = x_ref[...]
        dh = _nn(dp_ref[...], w_ref[...])
        g = g_ref[...]
        r = lax.rsqrt(jnp.mean(xv * xv, axis=-1, keepdims=True) + EPS)
        xn = xv * r
        scale = mod_ref[:, D_MODEL:2 * D_MODEL]
        dshift_ref[...] += jnp.sum(dh, axis=0, keepdims=True)
        dscale_ref[...] += jnp.sum(dh * (xn * g), axis=0, keepdims=True)
        da = dh * (1.0 + scale)
        dg_ref[...] += jnp.sum(da * xn, axis=0, keepdims=True)
        dxn = da * g
        dx = r * (dxn - xn * jnp.mean(dxn * xn, axis=-1, keepdims=True))
        gx_ref[...] = dout_ref[...] + dx

    row = pl.BlockSpec((bs, D_MODEL), lambda i: (i, 0))
    vec = pl.BlockSpec((1, D_MODEL), lambda i: (0, 0))
    return pl.pallas_call(
        body, grid=(seq // bs,), name="in_proj_dx",
        in_specs=[row, pl.BlockSpec((bs, IN_WIDTH), lambda i: (i, 0)), pl.BlockSpec((IN_WIDTH, D_MODEL), lambda i: (0, 0)), row,
                  pl.BlockSpec((1, 3 * D_MODEL), lambda i: (0, 0)), vec] + dep_specs,
        out_specs=[row, vec, vec, vec],
        out_shape=[jax.ShapeDtypeStruct((seq, D_MODEL), F32)] + [jax.ShapeDtypeStruct((1, D_MODEL), F32)] * 3,
        compiler_params=_cp("arbitrary"))(x, dp, win_t, dout, mod, g_pre, *deps)


def _dep_args(dep, grid_rank):
    if dep is None:
        return [], []
    return [dep], [pl.BlockSpec(dep.shape, lambda *_: (0,) * dep.ndim)]


def _matmul(a, b, *, ta, tb, tm, tn, out_dtype, name, dep=None):
    kdim = a.shape[0] if ta else a.shape[1]
    m = a.shape[1] if ta else a.shape[0]
    n = b.shape[0] if tb else b.shape[1]
    assert m % tm == 0 and n % tn == 0
    deps, dep_specs = _dep_args(dep, 2)

    def body(a_ref, b_ref, *rest):
        o_ref = rest[-1]
        dims = (((0 if ta else 1,), (1 if tb else 0,)), ((), ()))
        o_ref[...] = lax.dot_general(a_ref[...], b_ref[...], dims, preferred_element_type=F32).astype(o_ref.dtype)

    a_spec = pl.BlockSpec((kdim, tm), lambda j, i: (0, i)) if ta else pl.BlockSpec((tm, kdim), lambda j, i: (i, 0))
    b_spec = pl.BlockSpec((tn, kdim), lambda j, i: (j, 0)) if tb else pl.BlockSpec((kdim, tn), lambda j, i: (0, j))
    return pl.pallas_call(
        body, grid=(n // tn, m // tm), name=name, in_specs=[a_spec, b_spec] + dep_specs,
        out_specs=pl.BlockSpec((tm, tn), lambda j, i: (i, j)),
        out_shape=jax.ShapeDtypeStruct((m, n), out_dtype), compiler_params=_cp("parallel", "parallel"))(a, b, *deps)


def _in_proj_rest(h, win_t, cos, sin):
    seq = h.shape[0]
    tm = min(1024, seq)
    ncols = IN_WIDTH - ATTN_COLS
    tn = ncols // 2
    assert ZR_B <= tn and ATTN_COLS % 128 == 0 and tn % 128 == 0

    def body(h_ref, w_ref, cos_ref, sin_ref, p_ref, rq_ref, rk_ref, rv_ref):
        p = _nt(h_ref[...], w_ref[...])
        p_ref[...] = p

        @pl.when(pl.program_id(0) == 0)
        def _():
            cs, sn = cos_ref[...], sin_ref[...]
            for pr in range(2):
                sl = slice(128 * pr, 128 * (pr + 1))
                rq_ref[:, sl] = _rope(p[:, QR_B + 128 * pr:QR_B + 128 * (pr + 1)], cs, sn).astype(rq_ref.dtype)
                rk_ref[:, sl] = (_rope(p[:, KR_B + 128 * pr:KR_B + 128 * (pr + 1)], cs, sn) * 0.125).astype(rk_ref.dtype)
            rv_ref[...] = p[:, VR_B:VR_B + 512].astype(rv_ref.dtype)

    nrow = seq // tm
    row = lambda w: pl.BlockSpec((tm, w), lambda j, i: (i, 0))
    park = lambda w: pl.BlockSpec((tm, w), lambda j, i: (jnp.where(j == 0, i, nrow - 1), 0))
    sds = jax.ShapeDtypeStruct
    return pl.pallas_call(
        body, grid=(2, nrow), name="in_proj_rest",
        in_specs=[row(D_MODEL), pl.BlockSpec((pl.Element(tn), pl.Element(D_MODEL)),
                                             lambda j, i: (pl.multiple_of(ATTN_COLS + j * tn, 128), 0)), row(128), row(128)],
        out_specs=[pl.BlockSpec((tm, tn), lambda j, i: (i, j)), park(256), park(256), park(512)],
        out_shape=[sds((seq, ncols), F32), sds((seq, 256), MXU_DTYPE), sds((seq, 256), MXU_DTYPE), sds((seq, 512), MXU_DTYPE)],
        compiler_params=_cp("arbitrary", "arbitrary"))(h, win_t, cos, sin)


def _halves(kd):
    lo = _lane(kd.shape) < 64
    z = jnp.zeros_like(kd)
    return jnp.concatenate([jnp.where(lo, kd, z), jnp.where(lo, z, kd)], axis=0)


def _attn_fwd(qt, kd, vd, dep=None):
    seq = qt.shape[0]
    bq, bk = min(2048, seq), min(1024, seq)
    nk = seq // bk
    deps, dep_specs = _dep_args(dep, 2)

    def body(q_ref, k_ref, v_ref, *rest):
        o_ref, lse_ref, m_scr, l_scr, acc = rest[-5:]
        j = pl.program_id(1)
        m_scr[...] = jnp.full_like(m_scr, -jnp.inf)
        l_scr[...] = jnp.zeros_like(l_scr)
        acc[...] = jnp.zeros_like(acc)
        sub = min(512, bq)
        nsub = bq // sub
        lo = _lane((sub, 128)) < 64

        def kv_block(n, carry):
            rows = pl.ds(pl.multiple_of(n * bk, bk), bk)
            k2, v2 = _halves(k_ref[rows, :]), _halves(v_ref[rows, :])
            for pr in range(2):
                for hf in range(nsub):
                    qr = slice(hf * sub, (hf + 1) * sub)
                    s2 = _nt(q_ref[qr, 128 * pr:128 * (pr + 1)], k2)
                    ps, alphas = [], []
                    for e in range(2):
                        g = 2 * pr + e
                        s = s2[:, e * bk:(e + 1) * bk]
                        m_prev = m_scr[g, qr]
                        m_next = jnp.maximum(m_prev, jnp.max(s, axis=1, keepdims=True))
                        alpha = jnp.exp2(m_prev - m_next)
                        pe = jnp.exp2(s - jnp.tile(m_next, (1, bk // 128)))
                        l_scr[g, qr] = alpha * l_scr[g, qr] + jnp.sum(pe, axis=1, keepdims=True)
                        m_scr[g, qr] = m_next
                        ps.append(_mx(pe))
                        alphas.append(alpha)
                    o2 = _nn(jnp.concatenate(ps, axis=1), v2)
                    sl = slice(128 * pr, 128 * (pr + 1))
                    acc[qr, sl] = acc[qr, sl] * jnp.where(lo, alphas[0], alphas[1]) + o2
            return carry

        lax.fori_loop(0, nk, kv_block, 0)
        lo_all = _lane((bq, 128)) < 64
        for pr in range(2):
            sl = slice(128 * pr, 128 * (pr + 1))
            o_ref[:, sl] = acc[:, sl] / jnp.where(lo_all, l_scr[2 * pr], l_scr[2 * pr + 1])
        for g in range(4):
            lse = jnp.transpose(m_scr[g] + jnp.log2(l_scr[g]))
            lse_ref[pl.ds(4 * j + g, 1), :] = lse[0:1, :]

    return pl.pallas_call(
        body, grid=(seq // bq, 2), name="attn_fwd",
        in_specs=[pl.BlockSpec((bq, 256), lambda i, j: (i, j)), pl.BlockSpec((None, seq, 128), lambda i, j: (j, 0, 0)),
                  pl.BlockSpec((None, seq, 128), lambda i, j: (j, 0, 0))] + dep_specs,
        out_specs=[pl.BlockSpec((bq, 256), lambda i, j: (i, j)), pl.BlockSpec((8, bq), lambda i, j: (0, i))],
        out_shape=[jax.ShapeDtypeStruct((seq, 512), F32), jax.ShapeDtypeStruct((8, seq), F32)],
        scratch_shapes=[pltpu.VMEM((4, bq, 128), F32), pltpu.VMEM((4, bq, 128), F32), pltpu.VMEM((bq, 256), F32)],
        compiler_params=_cp("parallel", "arbitrary"))(qt, kd, vd, *deps)


def _attn_bwd(qt, kd, vd, do, lse, delta, dep=None):
    seq = qt.shape[0]
    bq, bk = min(1024, seq), min(1024, seq)
    nq, nk = seq // bq, seq // bk
    deps, dep_specs = _dep_args(dep, 3)

    def body(q_ref, do_ref, k_ref, v_ref, lse_ref, del_ref, *rest):
        dq_ref, dk_ref, dv_ref = rest[-3:]
        j, i = pl.program_id(0), pl.program_id(1)
        dq_ref[...] = jnp.zeros_like(dq_ref)

        @pl.when(i == 0)
        def _():
            dk_ref[...] = jnp.zeros_like(dk_ref)
            dv_ref[...] = jnp.zeros_like(dv_ref)

        lo = _lane((bk, 128)) < 64
        big = lambda r: jnp.concatenate([jnp.broadcast_to(r[0], (bk, bq)), jnp.broadcast_to(r[1], (bk, bq))], axis=0)

        def kv_block(n, carry):
            rows = pl.ds(pl.multiple_of(n * bk, bk), bk)
            k2, v2 = _halves(k_ref[rows, :]), _halves(v_ref[rows, :])
            for pr in range(2):
                sl = slice(128 * pr, 128 * (pr + 1))
                qp, dop = q_ref[:, sl], do_ref[:, sl]
                s_t = _nt(k2, qp)
                dp_t = _nt(v2, dop)
                ls = [lse_ref[pl.ds(4 * j + 2 * pr + e, 1), :] for e in range(2)]
                dl = [del_ref[pl.ds(4 * j + 2 * pr + e, 1), :] for e in range(2)]
                p_t = jnp.exp2(s_t - big(ls))
                ds_t = _mx(p_t * (dp_t - big(dl)))
                rv = _nn(_mx(p_t), dop)
                rk = _nn(ds_t, qp) * LN2
                dv_ref[rows, :] += jnp.where(lo, rv[:bk], rv[bk:])
                dk_ref[rows, :] += jnp.where(lo, rk[:bk], rk[bk:])
                dq_ref[:, sl] += _tn(ds_t, k2)
            return carry

        lax.fori_loop(0, nk, kv_block, 0)

    return pl.pallas_call(
        body, grid=(2, nq), name="attn_bwd",
        in_specs=[pl.BlockSpec((bq, 256), lambda j, i: (i, j)), pl.BlockSpec((bq, 256), lambda j, i: (i, j)),
                  pl.BlockSpec((None, seq, 128), lambda j, i: (j, 0, 0)), pl.BlockSpec((None, seq, 128), lambda j, i: (j, 0, 0)),
                  pl.BlockSpec((8, bq), lambda j, i: (0, i)), pl.BlockSpec((8, bq), lambda j, i: (0, i))] + dep_specs,
        out_specs=[pl.BlockSpec((bq, 256), lambda j, i: (i, j)), pl.BlockSpec((None, seq, 128), lambda j, i: (j, 0, 0)),
                   pl.BlockSpec((None, seq, 128), lambda j, i: (j, 0, 0))],
        out_shape=[jax.ShapeDtypeStruct((seq, 512), F32), jax.ShapeDtypeStruct((2, seq, 128), F32),
                   jax.ShapeDtypeStruct((2, seq, 128), F32)],
        compiler_params=_cp("arbitrary", "arbitrary"))(qt, do, kd, vd, lse, delta, *deps)


def _ret_tables(lg_f, lg_b, c):
    idx = jnp.arange(c, dtype=F32)
    diff = idx[:, None] - idx[None, :]
    a, b = lg_f[:, None, None], lg_b[:, None, None]
    low, up = (diff >= 0)[None], (diff < 0)[None]
    dmat = jnp.where(low, jnp.exp(a * jnp.maximum(diff, 0.0)[None]), jnp.exp(b * jnp.maximum(-diff, 0.0)[None]))
    dda = jnp.where(low, diff[None] * jnp.exp(a * jnp.maximum(diff, 0.0)[None]), 0.0)
    ddb = jnp.where(up, -diff[None] * jnp.exp(b * jnp.maximum(-diff, 0.0)[None]), 0.0)
    rep = lambda w: jnp.broadcast_to(w[:, :, None], (RET_HEADS, c, 128))
    wqf = rep(jnp.exp(lg_f[:, None] * (idx + 1.0)[None]))
    wqb = rep(jnp.exp(lg_b[:, None] * (c - idx)[None]))
    wkf = rep(jnp.exp(lg_f[:, None] * (c - 1.0 - idx)[None]))
    wkb = rep(jnp.exp(lg_b[:, None] * idx[None]))
    cdf, cdb = jnp.exp(lg_f * c), jnp.exp(lg_b * c)
    dec_fb = jnp.broadcast_to(jnp.concatenate([cdf, cdb])[:, None], (8, 128))
    dec_bf = jnp.broadcast_to(jnp.concatenate([cdb, cdf])[:, None], (8, 128))
    return dict(dmat=dmat, dda=dda, ddb=ddb, wqf=wqf, wqb=wqb, wkf=wkf, wkb=wkb, dec_fb=dec_fb, dec_bf=dec_bf)


def _ret_states(xk, yv, w_fwd, w_rev, dec, name):
    seq = xk.shape[0]
    c = RET_CHUNK
    nc = seq // c
    grp = min(RET_GROUP, nc)
    ns = nc // grp

    def body(xf_ref, yf_ref, xr_ref, yr_ref, wf_ref, wr_ref, dec_ref, sf_ref, sr_ref, st_f, st_r):
        @pl.when(pl.program_id(0) == 0)
        def _():
            st_f[...] = jnp.zeros_like(st_f)
            st_r[...] = jnp.zeros_like(st_r)

        lane = _lane((c, 128))

        def chunk(g, carry):
            for x_ref, y_ref, w_ref, s_ref, st, base, gg in ((xf_ref, yf_ref, wf_ref, sf_ref, st_f, 0, g),
                                                             (xr_ref, yr_ref, wr_ref, sr_ref, st_r, 4, grp - 1 - g)):
                rows = pl.ds(pl.multiple_of(gg * c, c), c)
                for h in range(RET_HEADS):
                    pr, e = h // 2, h % 2
                    half = (lane < 64) if e == 0 else (lane >= 64)
                    s_ref[gg, h] = st[h].astype(s_ref.dtype)
                    xm = jnp.where(half, x_ref[rows, 128 * pr:128 * (pr + 1)].astype(F32) * w_ref[h], 0.0)
                    st[h] = st[h] * dec_ref[base + h:base + h + 1, :] + _tn(_mx(xm), _mx(y_ref[rows, 128 * h:128 * (h + 1)]))
            return carry

        lax.fori_loop(0, grp, chunk, 0, unroll=True)

    xs = lambda f: pl.BlockSpec((grp * c, 256), f)
    ys = lambda f: pl.BlockSpec((grp * c, 512), f)
    fwd, rev = (lambda n: (n, 0)), (lambda n: (ns - 1 - n, 0))
    wsp = pl.BlockSpec((RET_HEADS, c, 128), lambda n: (0, 0, 0))
    return pl.pallas_call(
        body, grid=(ns,), name=name,
        in_specs=[xs(fwd), ys(fwd), xs(rev), ys(rev), wsp, wsp, pl.BlockSpec((8, 128), lambda n: (0, 0))],
        out_specs=[pl.BlockSpec((grp, RET_HEADS, 128, 128), lambda n: (n, 0, 0, 0)),
                   pl.BlockSpec((grp, RET_HEADS, 128, 128), lambda n: (ns - 1 - n, 0, 0, 0))],
        out_shape=[jax.ShapeDtypeStruct((nc, RET_HEADS, 128, 128), MXU_DTYPE)] * 2,
        scratch_shapes=[pltpu.VMEM((RET_HEADS, 128, 128), F32), pltpu.VMEM((RET_HEADS, 128, 128), F32)],
        compiler_params=_cp("arbitrary"))(xk, yv, xk, yv, w_fwd, w_rev, dec)


def _ret_fwd(rq, rk, rv, rf, rb, tb):
    seq = rq.shape[0]
    c = RET_CHUNK
    grp = min(RET_GROUP, seq // c)

    def body(q_ref, k_ref, v_ref, rf_ref, rb_ref, d_ref, wqf_ref, wqb_ref, o_ref):
        lane = _lane((c, 128))

        def chunk(g, carry):
            rows = pl.ds(pl.multiple_of(g * c, c), c)
            for h in range(RET_HEADS):
                pr, e = h // 2, h % 2
                half = (lane < 64) if e == 0 else (lane >= 64)
                sl = slice(128 * pr, 128 * (pr + 1))
                qp, kp = q_ref[rows, sl], k_ref[rows, sl]
                km = jnp.where(half, kp, jnp.zeros_like(kp))
                sd = _mx(_nt(qp, km) * d_ref[h])
                qf = qp.astype(F32)
                o = _nn(sd, v_ref[rows, 128 * h:128 * (h + 1)])
                o = o + _nn(_mx(qf * wqf_ref[h]), rf_ref[g, h]) + _nn(_mx(qf * wqb_ref[h]), rb_ref[g, h])
                o_ref[rows, 128 * h:128 * (h + 1)] = o
            return carry

        lax.fori_loop(0, grp, chunk, 0, unroll=True)

    st = pl.BlockSpec((grp, RET_HEADS, 128, 128), lambda n: (n, 0, 0, 0))
    wsp = pl.BlockSpec((RET_HEADS, c, 128), lambda n: (0, 0, 0))
    return pl.pallas_call(
        body, grid=(seq // (grp * c),), name="ret_fwd",
        in_specs=[pl.BlockSpec((grp * c, 256), lambda n: (n, 0)), pl.BlockSpec((grp * c, 256), lambda n: (n, 0)),
                  pl.BlockSpec((grp * c, 512), lambda n: (n, 0)), st, st, pl.BlockSpec((RET_HEADS, c, c), lambda n: (0, 0, 0)), wsp, wsp],
        out_specs=pl.BlockSpec((grp * c, 512), lambda n: (n, 0)),
        out_shape=jax.ShapeDtypeStruct((seq, 512), F32), compiler_params=_cp("parallel"))(
            rq, rk, rv, rf, rb, tb["dmat"], tb["wqf"], tb["wqb"])


def _ret_bwd(rq, rk, rv, do, rf, rb, hf, hb, tb):
    seq = rq.shape[0]
    c = RET_CHUNK
    grp = min(RET_GROUP, seq // c)

    def body(q_ref, k_ref, v_ref, do_ref, rf_ref, rb_ref, hf_ref, hb_ref, d_ref, dda_ref, ddb_ref,
             wqf_ref, wqb_ref, wkf_ref, wkb_ref, cdec_ref, dq_ref, dk_ref, dv_ref, dlg_ref):
        @pl.when(pl.program_id(0) == 0)
        def _():
            dlg_ref[...] = jnp.zeros_like(dlg_ref)

        lane = _lane((c, 128))
        pos = lax.broadcasted_iota(jnp.int32, (c, 128), 0).astype(F32)

        def chunk(g, carry):
            rows = pl.ds(pl.multiple_of(g * c, c), c)
            for pr in range(2):
                sl = slice(128 * pr, 128 * (pr + 1))
                qp, kp = q_ref[rows, sl], k_ref[rows, sl]
                qf, kf = qp.astype(F32), kp.astype(F32)
                dq_acc = jnp.zeros((c, 128), F32)
                dk_acc = jnp.zeros((c, 128), F32)
                for e in range(2):
                    h = 2 * pr + e
                    half = (lane < 64) if e == 0 else (lane >= 64)
                    hs = slice(128 * h, 128 * (h + 1))
                    km = jnp.where(half, kp, jnp.zeros_like(kp))
                    kmf = km.astype(F32)
                    vh, doh = v_ref[rows, hs], do_ref[rows, hs]
                    rfh, rbh, hfh, hbh = rf_ref[g, h], rb_ref[g, h], hf_ref[g, h], hb_ref[g, h]
                    s = _nt(qp, km)
                    dpm = _nt(doh, vh)
                    ds_b = _mx(dpm * d_ref[h])
                    sd_b = _mx(s * d_ref[h])
                    dq_if = wqf_ref[h] * _nt(doh, rfh)
                    dq_ib = wqb_ref[h] * _nt(doh, rbh)
                    dq_acc = dq_acc + _nn(ds_b, km) + dq_if + dq_ib
                    dk_sf = wkf_ref[h] * _nt(vh, hfh)
                    dk_sb = wkb_ref[h] * _nt(vh, hbh)
                    dk_acc = dk_acc + jnp.where(half, _tn(ds_b, qp), 0.0) + dk_sf + dk_sb
                    dv = _tn(sd_b, doh) + _nn(_mx(kmf * wkf_ref[h]), hfh) + _nn(_mx(kmf * wkb_ref[h]), hbh)
                    dv_ref[rows, hs] = dv.astype(dv_ref.dtype)
                    sdp = s * dpm
                    hr_f = hfh.astype(F32) * rfh.astype(F32)
                    hr_b = hbh.astype(F32) * rbh.astype(F32)
                    da = (_rowsum128(sdp * dda_ref[h]) + _rowsum128((pos + 1.0) * qf * dq_if)
                          + _rowsum128((c - 1.0 - pos) * kf * dk_sf) + cdec_ref[h:h + 1, :] * _rowsum128(hr_f))
                    db = (_rowsum128(sdp * ddb_ref[h]) + _rowsum128((c - pos) * qf * dq_ib)
                          + _rowsum128(pos * kf * dk_sb) + cdec_ref[4 + h:5 + h, :] * _rowsum128(hr_b))
                    dlg_ref[h:h + 1, :] += da
                    dlg_ref[4 + h:5 + h, :] += db
                dq_ref[rows, sl] = dq_acc
                dk_ref[rows, sl] = dk_acc
            return carry

        lax.fori_loop(0, grp, chunk, 0, unroll=2)

    st = pl.BlockSpec((grp, RET_HEADS, 128, 128), lambda n: (n, 0, 0, 0))
    wsp = pl.BlockSpec((RET_HEADS, c, 128), lambda n: (0, 0, 0))
    dsp = pl.BlockSpec((RET_HEADS, c, c), lambda n: (0, 0, 0))
    x256 = pl.BlockSpec((grp * c, 256), lambda n: (n, 0))
    x512 = pl.BlockSpec((grp * c, 512), lambda n: (n, 0))
    cdec = tb["dec_fb"] * float(c)
    return pl.pallas_call(
        body, grid=(seq // (grp * c),), name="ret_bwd",
        in_specs=[x256, x256, x512, x512, st, st, st, st, dsp, dsp, dsp, wsp, wsp, wsp, wsp,
                  pl.BlockSpec((8, 128), lambda n: (0, 0))],
        out_specs=[x256, x256, x512, pl.BlockSpec((8, 128), lambda n: (0, 0))],
        out_shape=[jax.ShapeDtypeStruct((seq, 256), F32), jax.ShapeDtypeStruct((seq, 256), F32),
                   jax.ShapeDtypeStruct((seq, 512), MXU_DTYPE), jax.ShapeDtypeStruct((8, 128), F32)],
        compiler_params=_cp("arbitrary"))(
            rq, rk, rv, do, rf, rb, hf, hb, tb["dmat"], tb["dda"], tb["ddb"], tb["wqf"], tb["wqb"], tb["wkf"], tb["wkb"], cdec)


def _tail(x, tgt, o_att, o_ret, p, mod, g_post, gn_g, wpt, wout):
    seq = x.shape[0]
    bs = 256
    nb = seq // bs

    def body(x_ref, t_ref, oa_ref, or_ref, za_ref, zr_ref, gl_ref, mod_ref, gp_ref, gn_ref, wpt_ref, wout_ref,
             dout_ref, dza_ref, dzr_ref, dgl_ref, doa_ref, dor_ref, del_ref, gout_ref, gpt_ref,
             dgate_ref, dgpost_ref, dgn_ref, loss_ref, gout_acc, gpt_acc):
        i = pl.program_id(0)

        @pl.when(i == 0)
        def _():
            gout_acc[...] = jnp.zeros_like(gout_acc)
            gpt_acc[...] = jnp.zeros_like(gpt_acc)
            dgate_ref[...] = jnp.zeros_like(dgate_ref)
            dgpost_ref[...] = jnp.zeros_like(dgpost_ref)
            dgn_ref[...] = jnp.zeros_like(dgn_ref)
            loss_ref[...] = jnp.zeros_like(loss_ref)

        oa, za, zr = oa_ref[...], za_ref[...], zr_ref[...]
        sga, sgr = _sigmoid(za), _sigmoid(zr)
        sza, szr = za * sga, zr * sgr
        ya_b = _mx(oa * sza)
        ons, rstds = [], []
        for h in range(RET_HEADS):
            oh = or_ref[:, 128 * h:128 * (h + 1)]
            xc = oh - jnp.mean(oh, axis=-1, keepdims=True)
            rstd = lax.rsqrt(jnp.mean(xc * xc, axis=-1, keepdims=True) + EPS)
            ons.append(xc * rstd)
            rstds.append(rstd)
        on = jnp.concatenate(ons, axis=1)
        yn = on * gn_ref[...]
        yr_b = _mx(yn * szr)
        wpa_t, wpr_t = wpt_ref[:, 0:512], wpt_ref[:, 512:1024]
        a_att = _nt(ya_b, wpa_t)
        a_ret = _nt(yr_b, wpr_t)
        gts = _sigmoid(gl_ref[...])
        g_att, g_ret = gts[:, 0:D_MODEL], gts[:, D_MODEL:2 * D_MODEL]
        merged_b = _mx(g_att * a_att + g_ret * a_ret)
        u = _nn(merged_b, wout_ref[...])
        r = lax.rsqrt(jnp.mean(u * u, axis=-1, keepdims=True) + EPS)
        un = u * r
        gpost = gp_ref[...]
        y = un * gpost
        gate = mod_ref[:, 2 * D_MODEL:3 * D_MODEL]
        err = x_ref[...] + gate * y - t_ref[...]
        loss_ref[...] += (0.5 / D_MODEL) * _rowsum128(err * err)
        dout = err * (1.0 / D_MODEL)
        dout_ref[...] = dout
        dgate_ref[...] += jnp.sum(dout * y, axis=0, keepdims=True)
        dy = dout * gate
        dgpost_ref[...] += jnp.sum(dy * un, axis=0, keepdims=True)
        dun = dy * gpost
        du_b = _mx(r * (dun - un * jnp.mean(dun * un, axis=-1, keepdims=True)))
        dmerged = _nt(du_b, wout_ref[...])
        gout_acc[...] += _tn(merged_b, du_b)
        d_att, d_ret = dmerged * g_att, dmerged * g_ret
        dgl_ref[:, 0:D_MODEL] = (d_att * a_att * (1.0 - g_att)).astype(dgl_ref.dtype)
        dgl_ref[:, D_MODEL:2 * D_MODEL] = (d_ret * a_ret * (1.0 - g_ret)).astype(dgl_ref.dtype)
        d_att_b, d_ret_b = _mx(d_att), _mx(d_ret)
        dya = _nn(d_att_b, wpa_t)
        dyr = _nn(d_ret_b, wpr_t)
        gpt_acc[:, 0:512] += _tn(d_att_b, ya_b)
        gpt_acc[:, 512:1024] += _tn(d_ret_b, yr_b)
        dza_ref[...] = (dya * oa * (sga * (1.0 + za * (1.0 - sga)))).astype(dza_ref.dtype)
        doa = dya * sza
        doa_ref[...] = doa.astype(doa_ref.dtype)
        dt = jnp.transpose(doa * oa)
        del_ref[...] = jnp.sum(dt.reshape(8, HEAD_DIM, bs), axis=1)
        dzr_ref[...] = (dyr * yn * (sgr * (1.0 + zr * (1.0 - sgr)))).astype(dzr_ref.dtype)
        dyn = dyr * szr
        dgn_ref[...] += jnp.sum(dyn * on, axis=0, keepdims=True)
        don = dyn * gn_ref[...]
        for h in range(RET_HEADS):
            hs = slice(128 * h, 128 * (h + 1))
            dh, oh = don[:, hs], ons[h]
            doh = rstds[h] * (dh - jnp.mean(dh, axis=-1, keepdims=True) - oh * jnp.mean(dh * oh, axis=-1, keepdims=True))
            dor_ref[:, hs] = doh.astype(dor_ref.dtype)

        @pl.when(i == nb - 1)
        def _():
            gout_ref[...] = gout_acc[...].astype(gout_ref.dtype)
            gpt_ref[...] = gpt_acc[...].astype(gpt_ref.dtype)

    row = lambda w: pl.BlockSpec((bs, w), lambda i: (i, 0))
    el = lambda w, off: pl.BlockSpec((pl.Element(bs), pl.Element(w)), lambda i: (i * bs, off))
    vec = lambda w: pl.BlockSpec((1, w), lambda i: (0, 0))
    full = pl.BlockSpec((D_MODEL, D_MODEL), lambda i: (0, 0))
    sds = jax.ShapeDtypeStruct
    return pl.pallas_call(
        body, grid=(nb,), name="tail",
        in_specs=[row(D_MODEL), row(D_MODEL), row(512), row(512), el(512, ZA_B), el(512, ZR_B), el(2 * D_MODEL, GL_B),
                  vec(3 * D_MODEL), vec(D_MODEL), vec(512), full, full],
        out_specs=[row(D_MODEL), row(512), row(512), row(2 * D_MODEL), row(512), row(512),
                   pl.BlockSpec((8, bs), lambda i: (0, i)), full, full, vec(D_MODEL), vec(D_MODEL), vec(512), vec(128)],
        out_shape=[sds((seq, D_MODEL), F32), sds((seq, 512), MXU_DTYPE), sds((seq, 512), MXU_DTYPE),
                   sds((seq, 2 * D_MODEL), MXU_DTYPE), sds((seq, 512), MXU_DTYPE), sds((seq, 512), MXU_DTYPE),
                   sds((8, seq), F32), sds((D_MODEL, D_MODEL), MXU_DTYPE), sds((D_MODEL, D_MODEL), MXU_DTYPE),
                   sds((1, D_MODEL), F32), sds((1, D_MODEL), F32), sds((1, 512), F32), sds((1, 128), F32)],
        scratch_shapes=[pltpu.VMEM((D_MODEL, D_MODEL), F32), pltpu.VMEM((D_MODEL, D_MODEL), F32)],
        compiler_params=_cp("arbitrary"))(x, tgt, o_att, o_ret, p, p, p, mod, g_post, gn_g, wpt, wout)


def _assemble(p, cos, sin, qg2, kg2, dqt, dkp, dvp, dza, drq, drk, drv, dzr, dgl):
    seq = p.shape[0]
    bs = 512

    def body(p_ref, cos_ref, sin_ref, qg_ref, kg_ref, dqt_ref, dkp_ref, dvp_ref, dza_ref, drq_ref, drk_ref, drv_ref,
             dzr_ref, dgl_ref, dp_ref, dqg_ref, dkg_ref):
        @pl.when(pl.program_id(0) == 0)
        def _():
            dqg_ref[...] = jnp.zeros_like(dqg_ref)
            dkg_ref[...] = jnp.zeros_like(dkg_ref)

        m = _blockdiag64()
        cs, sn = cos_ref[...], sin_ref[...]
        lo = _lane((bs, 128)) < 64

        def norm_bwd(raw, dn, g, dg_ref):
            r = lax.rsqrt(_seg64(raw * raw, m) * (1.0 / 64) + EPS)
            xn = raw * r
            dg_ref[...] += jnp.sum(dn * xn, axis=0, keepdims=True)
            dxn = dn * g
            return r * (dxn - xn * (_seg64(dxn * xn, m) * (1.0 / 64)))

        for pr in range(4):
            sl = slice(128 * pr, 128 * (pr + 1))
            dn = _rope_t(dqt_ref[:, sl] * 0.125, cs, sn)
            dp_ref[:, QA + 128 * pr:QA + 128 * (pr + 1)] = norm_bwd(p_ref[:, sl], dn, qg_ref[...], dqg_ref).astype(dp_ref.dtype)
        fold = lambda a: a + pltpu.roll(a, 64, 1)
        dk = jnp.where(lo, fold(dkp_ref[0]), fold(dkp_ref[1]))
        dp_ref[:, KA:KA + 128] = norm_bwd(p_ref[:, KA:KA + 128], _rope_t(dk, cs, sn), kg_ref[...], dkg_ref).astype(dp_ref.dtype)
        dp_ref[:, VA:VA + 128] = jnp.where(lo, fold(dvp_ref[0]), fold(dvp_ref[1])).astype(dp_ref.dtype)
        dp_ref[:, ZA:ZA + 512] = dza_ref[...]
        for pr in range(2):
            sl = slice(128 * pr, 128 * (pr + 1))
            dp_ref[:, QR + 128 * pr:QR + 128 * (pr + 1)] = _rope_t(drq_ref[:, sl], cs, sn).astype(dp_ref.dtype)
            dp_ref[:, KR + 128 * pr:KR + 128 * (pr + 1)] = _rope_t(drk_ref[:, sl] * 0.125, cs, sn).astype(dp_ref.dtype)
        dp_ref[:, VR:VR + 512] = drv_ref[...]
        dp_ref[:, ZR:ZR + 512] = dzr_ref[...]
        dp_ref[:, GL:GL + 2 * D_MODEL] = dgl_ref[...]

    row = lambda w: pl.BlockSpec((bs, w), lambda i: (i, 0))
    gsp = pl.BlockSpec((1, 128), lambda i: (0, 0))
    dup = pl.BlockSpec((2, bs, 128), lambda i: (0, i, 0))
    return pl.pallas_call(
        body, grid=(seq // bs,), name="assemble_dp",
        in_specs=[row(768), row(128), row(128), gsp, gsp, row(512), dup, dup, row(512), row(256), row(256), row(512),
                  row(512), row(2 * D_MODEL)],
        out_specs=[row(IN_WIDTH), gsp, gsp],
        out_shape=[jax.ShapeDtypeStruct((seq, IN_WIDTH), MXU_DTYPE), jax.ShapeDtypeStruct((1, 128), F32),
                   jax.ShapeDtypeStruct((1, 128), F32)],
        compiler_params=_cp("arbitrary"))(p, cos, sin, qg2, kg2, dqt, dkp, dvp, dza, drq, drk, drv, dzr, dgl)


def _local_step(x, tgt, mod, g_pre, qn_g, kn_g, w_dec_f, w_dec_b, gn_g, g_post, w_attn, rest_start, rest_wait, send=None, own=()):
    send = send or (lambda name, arrays: None)
    seq = x.shape[0]
    cos, sin = _rope_tables(seq)
    qg2, kg2 = jnp.tile(qn_g, (1, 2)), jnp.tile(kn_g, (1, 2))
    lg_f = jax.nn.log_sigmoid(w_dec_f[0])
    lg_b = jax.nn.log_sigmoid(w_dec_b[0])
    tb = _ret_tables(lg_f, lg_b, RET_CHUNK)

    h, p_a, qt, kd, vd, *placed = _attn_inputs(x, mod, g_pre, w_attn, cos, sin, qg2, kg2, own=own)
    o_att, lse = _attn_fwd(qt, kd, vd, dep=rest_start(qt, *placed))
    win_t, wpt, wout = rest_wait(o_att)
    p_b, rq, rk, rv = _in_proj_rest(h, win_t, cos, sin)
    rf, rb = _ret_states(rk, rv, tb["wkf"], tb["wkb"], tb["dec_fb"], "ret_states_fwd")
    o_ret = _ret_fwd(rq, rk, rv, rf, rb, tb)
    (dout, dza, dzr, dgl, do_att, do_ret, delta, g_out, g_pt, dgate, dgpost, dgn, loss_l) = _tail(
        x, tgt, o_att, o_ret, p_b, mod, g_post, gn_g, wpt, wout)
    dep = send("early", (g_pt, g_out))
    dqt, dkp, dvp = _attn_bwd(qt, kd, vd, do_att, lse, delta, dep=dep)
    hb, hf = _ret_states(rq, do_ret, tb["wqb"], tb["wqf"], tb["dec_bf"], "ret_states_bwd")
    drq, drk, drv, dlg = _ret_bwd(rq, rk, rv, do_ret, rf, rb, hf, hb, tb)
    dp, dqg, dkg = _assemble(p_a, cos, sin, qg2, kg2, dqt, dkp, dvp, dza, drq, drk, drv, dzr, dgl)
    g_in_t = _matmul(dp, h, ta=True, tb=False, tm=256, tn=D_MODEL, out_dtype=MXU_DTYPE, name="in_proj_dw")
    dep = send("late", (g_in_t,))
    grad_x, dshift, dscale, dgpre = _in_proj_dx(x, dp, win_t, dout, mod, g_pre, dep=dep)

    dlg = jnp.sum(dlg, axis=1)
    small = dict(
        dmod=jnp.concatenate([dshift, dscale, dgate], axis=1),
        g_pre=dgpre, g_post=dgpost, gn_g=dgn,
        qn_g=dqg[:, :64] + dqg[:, 64:], kn_g=dkg[:, :64] + dkg[:, 64:],
        w_dec_f=(dlg[0:4] * jax.nn.sigmoid(-w_dec_f[0]))[None], w_dec_b=(dlg[4:8] * jax.nn.sigmoid(-w_dec_b[0]))[None],
        loss=jnp.sum(loss_l, axis=1, keepdims=True))
    return grad_x, g_in_t, g_pt, g_out, small


N_DEV = 8
N_CHIP = 4
HBM_SPEC = pl.BlockSpec(memory_space=pl.ANY)
VMEM_SPEC = pl.BlockSpec(memory_space=pltpu.VMEM)
SHARD_ROWS = (IN_WIDTH // N_CHIP, D_MODEL // N_CHIP, D_MODEL // N_CHIP)


def _coords():
    return lax.axis_index("x"), lax.axis_index("y"), lax.axis_index("c")


def _peer(k):
    x, y, c = _coords()
    return (1 - x if k & 4 else x, 1 - y if k & 2 else y, 1 - c if k & 1 else c)


def _dev_index(p):
    return 4 * p[0] + 2 * p[1] + p[2]


def _rows(ref, start, size):
    return ref.at[pl.ds(pl.multiple_of(start, 16), size), :]


def _remote(src, dst, ssem, rsem, dev):
    return pltpu.make_async_remote_copy(src_ref=src, dst_ref=dst, send_sem=ssem, recv_sem=rsem, device_id=dev,
                                        device_id_type=MESH)


ATTN_CHUNK = 32


def _mod_and_attn_rows(c, w_ada_s, b4, win_s):
    ncol = w_ada_s.shape[1]
    half = ATTN_COLS // 2
    offs = list(range(0, half, ATTN_CHUNK))
    nq = len(offs)
    rows = lambda ref, cc, off: ref.at[pl.ds(pl.multiple_of(cc * half + off, 16), ATTN_CHUNK), :]

    def body(c_ref, w_ref, b_ref, src, cs_ref, mod_ref, out, modsh, modall, ssem, rsem, lsem, asem, bsem, fsem, gsem, hsem):
        x, y, cc = _coords()
        me = _dev_index((x, y, cc))
        chip = me // 2
        sib = (x, y, 1 - cc)
        cs_ref[me] = c_ref[...]
        sends = []
        for k in range(1, N_DEV):
            cp = _remote(c_ref, cs_ref.at[me], ssem.at[k - 1], rsem.at[k - 1], _peer(k))
            cp.start()
            sends.append(cp)
        own = pltpu.make_async_copy(src.at[pl.ds(0, ATTN_COLS), :], out, lsem.at[0])
        bulk = [_remote(rows(src, cc, off), rows(out, cc, off), asem.at[(k2 - 1) * nq + q], bsem.at[q], (k2 // 2, k2 % 2, cc))
                for k2 in (1, 2) for q, off in enumerate(offs)]
        relay_chip = lambda q: 1 + q % 2

        @pl.when(chip == 0)
        def _():
            own.start()
            for cp in bulk:
                cp.start()

        for k in range(1, N_DEV):
            slot = cs_ref.at[_dev_index(_peer(k))]
            _remote(slot, slot, ssem.at[k - 1], rsem.at[k - 1], _peer(k)).wait_recv()
        cs = jnp.concatenate([cs_ref[d] for d in range(N_DEV)], axis=0)
        ms = _nn(cs * _sigmoid(cs), w_ref[...], precision=HIGHEST) + b_ref[pl.ds(chip, 1), :]
        modsh[...] = ms
        modall[chip] = ms
        for k2 in range(1, N_CHIP):
            cp = _remote(modsh, modall.at[chip], ssem.at[6 + k2], rsem.at[6 + k2], _peer(2 * k2))
            cp.start()
            sends.append(cp)

        @pl.when(chip != 0)
        def _():
            passed = []
            relays = [_remote(rows(out, cc, off), rows(out, cc, off), hsem.at[q], bsem.at[q], (1, 1, cc)) for q, off in enumerate(offs)]
            for q, off in enumerate(offs):
                got = rows(out, cc, off)
                _remote(got, got, asem.at[q], bsem.at[q], (0, 0, cc)).wait_recv()
                cp = _remote(got, got, fsem.at[q], gsem.at[q], sib)
                cp.start()
                passed.append(cp)
                pl.when(chip == relay_chip(q))(relays[q].start)
            for q, off in enumerate(offs):
                got = rows(out, 1 - cc, off)
                _remote(got, got, fsem.at[q], gsem.at[q], sib).wait_recv()
            for cp in passed:
                cp.wait_send()
            for q in range(nq):
                pl.when(chip == relay_chip(q))(relays[q].wait_send)

        for k2 in range(1, N_CHIP):
            slot = modall.at[_dev_index(_peer(2 * k2)) // 2]
            _remote(slot, slot, ssem.at[6 + k2], rsem.at[6 + k2], _peer(2 * k2)).wait_recv()
        for jj in range(N_CHIP):
            mod_ref[:, ncol * jj:ncol * (jj + 1)] = modall[jj, pl.ds(me, 1), :]
        for cp in sends:
            cp.wait_send()

        @pl.when(chip == 0)
        def _():
            for cp in bulk:
                cp.wait_send()
            own.wait()

    dma = pltpu.SemaphoreType.DMA
    return pl.pallas_call(
        body, name="mod_and_attn_rows", in_specs=[VMEM_SPEC] * 4, out_specs=[VMEM_SPEC, VMEM_SPEC, HBM_SPEC],
        out_shape=[jax.ShapeDtypeStruct((N_DEV, 1, D_MODEL), F32), jax.ShapeDtypeStruct((1, N_CHIP * ncol), F32),
                   jax.ShapeDtypeStruct((ATTN_COLS, win_s.shape[1]), win_s.dtype)],
        scratch_shapes=[pltpu.VMEM((N_DEV, ncol), F32), pltpu.VMEM((N_CHIP, N_DEV, ncol), F32), dma((10,)), dma((10,)),
                        dma((1,)), dma((2 * nq,)), dma((nq,)), dma((nq,)), dma((nq,)), dma((nq,))],
        compiler_params=_cp())(c, w_ada_s, b4, win_s)


GATHER_CHUNK = 32


def _chunks(kinds, chunk):
    out = []
    for t, kind in enumerate(kinds):
        half = SHARD_ROWS[kind] // 2
        step = chunk if half % chunk == 0 else half
        out += [(t, off, step) for off in range(0, half, step)]
    return out


SEM_SPEC = pl.BlockSpec(memory_space=pltpu.SEMAPHORE)
HBM_ONLY = pl.BlockSpec(memory_space=pltpu.HBM)
DATAFLOW = pltpu.SideEffectType.DATAFLOW_SIDE_EFFECTING


def _gather_rest_start(shards, zones, dep):
    n = len(shards)

    def body(*refs):
        srcs, lands = refs[:n], refs[n:2 * n]
        ssem, rsem = refs[2 * n + 1], refs[2 * n + 2]
        token = refs[-1]
        x, y, _ = _coords()
        chip = 2 * x + y
        for k2 in range(1, N_CHIP):
            for t in range(n):
                q = (k2 - 1) * n + t
                _remote(srcs[t], _rows(lands[t], chip * SHARD_ROWS[t], SHARD_ROWS[t]), ssem.at[q], rsem.at[q], _peer(2 * k2)).start()
        token[...] = jnp.zeros_like(token)

    hbm = lambda a: pltpu.with_memory_space_constraint(a, pltpu.HBM)
    dma = pltpu.SemaphoreType.DMA
    outs = pl.pallas_call(
        body, name="gather_rest", in_specs=[HBM_ONLY] * (2 * n) + [HBM_SPEC],
        out_specs=[SEM_SPEC, SEM_SPEC] + [HBM_ONLY] * (2 * n) + [VMEM_SPEC],
        out_shape=[dma((3 * n,)), dma((3 * n,))] + [pltpu.HBM(a.shape, a.dtype) for a in list(shards) + list(zones)]
        + [jax.ShapeDtypeStruct((8, 128), F32)],
        input_output_aliases={i: 2 + i for i in range(2 * n)},
        compiler_params=pltpu.CompilerParams(has_side_effects=DATAFLOW))(*[hbm(a) for a in list(shards) + list(zones)], dep)
    return outs[0], outs[1], outs[2:2 + n], outs[2 + n:2 + 2 * n], outs[-1]


def _gather_rest_wait(started, after):
    ssem, rsem, shards, zones, _ = started
    n = len(shards)

    def body(*refs):
        srcs, lands = refs[:n], refs[n:2 * n]
        ssem_ref, rsem_ref = refs[2 * n], refs[2 * n + 1]
        for k2 in range(1, N_CHIP):
            pchip = _dev_index(_peer(2 * k2)) // 2
            for t in range(n):
                q = (k2 - 1) * n + t
                cp = _remote(srcs[t], _rows(lands[t], pchip * SHARD_ROWS[t], SHARD_ROWS[t]), ssem_ref.at[q], rsem_ref.at[q], _peer(2 * k2))
                cp.wait_send()
                cp.wait_recv()

    outs = pl.pallas_call(
        body, name="gather_rest_wait", in_specs=[HBM_ONLY] * (2 * n) + [SEM_SPEC, SEM_SPEC, HBM_SPEC], out_specs=[HBM_ONLY] * (2 * n),
        out_shape=[pltpu.HBM(a.shape, a.dtype) for a in list(shards) + list(zones)],
        input_output_aliases={i: i for i in range(2 * n)},
        compiler_params=pltpu.CompilerParams(has_side_effects=DATAFLOW))(*shards, *zones, ssem, rsem, after)
    return outs[n:]


def _reduced_by(ref, t, dev):
    return _rows(ref, (dev // 2) * SHARD_ROWS[t] + (dev % 2) * (SHARD_ROWS[t] // 2), SHARD_ROWS[t] // 2)


def _scatter_start(grads, kinds, name):
    n = len(grads)

    def body(*refs):
        srcs, lands = refs[:n], refs[n:2 * n]
        ssem, rsem = refs[2 * n], refs[2 * n + 1]
        token = refs[-1]
        me = _dev_index(_coords())
        for k in range(1, N_DEV):
            for i, t in enumerate(kinds):
                q = (k - 1) * n + i
                _remote(_reduced_by(srcs[i], t, _dev_index(_peer(k))), lands[i].at[me], ssem.at[q], rsem.at[q], _peer(k)).start()
        token[...] = jnp.zeros_like(token)

    zones = [lax.empty((N_DEV, SHARD_ROWS[t] // 2, g.shape[1]), g.dtype) for g, t in zip(grads, kinds)]
    hbm = lambda a: pltpu.with_memory_space_constraint(a, pltpu.HBM)
    dma = pltpu.SemaphoreType.DMA
    outs = pl.pallas_call(
        body, name=name, in_specs=[HBM_ONLY] * (2 * n), out_specs=[SEM_SPEC, SEM_SPEC] + [HBM_ONLY] * (2 * n) + [VMEM_SPEC],
        out_shape=[dma((7 * n,)), dma((7 * n,))] + [pltpu.HBM(a.shape, a.dtype) for a in list(grads) + zones]
        + [jax.ShapeDtypeStruct((8, 128), F32)],
        input_output_aliases={i: 2 + i for i in range(2 * n)},
        compiler_params=pltpu.CompilerParams(has_side_effects=DATAFLOW))(*[hbm(a) for a in list(grads) + zones])
    return outs[0], outs[1], outs[2:2 + n], outs[2 + n:2 + 2 * n], outs[-1]


def _scatter_wait(started, kinds, after, name):
    ssem, rsem, grads, zones, _ = started
    n = len(grads)

    def body(*refs):
        srcs, lands = refs[:n], refs[n:2 * n]
        ssem_ref, rsem_ref = refs[2 * n], refs[2 * n + 1]
        for k in range(1, N_DEV):
            peer = _dev_index(_peer(k))
            for i, t in enumerate(kinds):
                q = (k - 1) * n + i
                cp = _remote(_reduced_by(srcs[i], t, peer), lands[i].at[peer], ssem_ref.at[q], rsem_ref.at[q], _peer(k))
                cp.wait_send()
                cp.wait_recv()

    outs = pl.pallas_call(
        body, name=name, in_specs=[HBM_ONLY] * (2 * n) + [SEM_SPEC, SEM_SPEC, HBM_SPEC], out_specs=[HBM_ONLY] * (2 * n),
        out_shape=[pltpu.HBM(a.shape, a.dtype) for a in list(grads) + list(zones)],
        input_output_aliases={i: i for i in range(2 * n)},
        compiler_params=pltpu.CompilerParams(has_side_effects=DATAFLOW))(*grads, *zones, ssem, rsem, after)
    return outs[:n], outs[n:]


def _grad_finish(grads, zones, kinds, name):
    nt = len(grads)
    pieces = _chunks(kinds, GATHER_CHUNK)
    npc = len(pieces)
    shard = [SHARD_ROWS[k] for k in kinds]
    count = [sum(1 for p in pieces if p[0] == t) for t in range(nt)]
    assert min(count) >= 2
    cut = [next(p[2] for p in pieces if p[0] == t) * ((count[t] + 1) // 2) for t in range(nt)]

    def body(*refs):
        srcs, lands, outs = refs[:nt], refs[nt:2 * nt], refs[2 * nt:3 * nt]
        slots, sums = refs[3 * nt:4 * nt], refs[4 * nt:5 * nt]
        lsem, osem, xsem, ysem = refs[5 * nt:]
        x, y, c = _coords()
        me = _dev_index((x, y, c))
        sib = (x, y, 1 - c)
        loads = []
        for s in range(2):
            part = []
            for t in range(nt):
                rows = pl.ds(cut[t] * s, cut[t] if s == 0 else shard[t] // 2 - cut[t])
                part.append(pltpu.make_async_copy(_reduced_by(srcs[t], kinds[t], me).at[rows, :], slots[t].at[me, rows, :],
                                                  lsem.at[s * N_DEV * nt + t]))
                for k in range(1, N_DEV):
                    peer = _dev_index(_peer(k))
                    part.append(pltpu.make_async_copy(lands[t].at[peer, rows, :], slots[t].at[peer, rows, :],
                                                      lsem.at[(s * N_DEV + k) * nt + t]))
            for cp in part:
                cp.start()
            loads.append(part)
        sends = []
        place = lambda t, cc, off, n: _rows(outs[t], cc * (shard[t] // 2) + off, n)
        stores = []
        for s in range(2):
            for cp in loads[s]:
                cp.wait()
            for q, (t, off, n) in enumerate(pieces):
                if (off >= cut[t]) != (s == 1):
                    continue
                r = pl.ds(off, n)
                acc = slots[t][0, r, :].astype(F32)
                for d in range(1, N_DEV):
                    acc = acc + slots[t][d, r, :].astype(F32)
                sums[t][r, :] = acc
                keep = pltpu.make_async_copy(sums[t].at[r, :], place(t, c, off, n), osem.at[q])
                give = _remote(sums[t].at[r, :], place(t, c, off, n), xsem.at[q], ysem.at[q], sib)
                keep.start()
                give.start()
                stores.append(keep)
                sends.append(give)
        for q, (t, off, n) in enumerate(pieces):
            got = place(t, 1 - c, off, n)
            _remote(got, got, xsem.at[q], ysem.at[q], sib).wait_recv()
        for cp in sends:
            cp.wait_send()
        for cp in stores:
            cp.wait()

    dma = pltpu.SemaphoreType.DMA
    return pl.pallas_call(
        body, name=name, in_specs=[HBM_SPEC] * (2 * nt), out_specs=[HBM_SPEC] * nt,
        out_shape=[jax.ShapeDtypeStruct((shard[t], g.shape[1]), F32) for t, g in enumerate(grads)],
        scratch_shapes=([pltpu.VMEM((N_DEV, shard[t] // 2, g.shape[1]), g.dtype) for t, g in enumerate(grads)]
                        + [pltpu.VMEM((shard[t] // 2, g.shape[1]), F32) for t, g in enumerate(grads)]
                        + [dma((2 * N_DEV * nt,)), dma((npc,)), dma((npc,)), dma((npc,))]),
        compiler_params=_cp())(*grads, *zones)


def _adam_math(w, g, m, v):
    m = ADAM_B1 * m + (1.0 - ADAM_B1) * g
    v = ADAM_B2 * v + (1.0 - ADAM_B2) * (g * g)
    m_hat = m / (1.0 - ADAM_B1 ** ADAM_STEP)
    v_hat = v / (1.0 - ADAM_B2 ** ADAM_STEP)
    return -ADAM_LR * (m_hat / (jnp.sqrt(v_hat) + ADAM_EPS) + ADAM_WD * w), m, v


def _adamw(items, name, nblk=4):
    n = len(items)

    def body(*refs):
        ins, outs = refs[:4 * n], refs[4 * n:]
        for t in range(n):
            w_ref, g_ref, m_ref, v_ref = ins[4 * t:4 * t + 4]
            d_ref, nm_ref, nv_ref = outs[3 * t:3 * t + 3]
            d_ref[...], nm_ref[...], nv_ref[...] = _adam_math(w_ref[...], g_ref[...], m_ref[...], v_ref[...])

    specs = [pl.BlockSpec((it[0].shape[0] // nblk, it[0].shape[1]), lambda i: (i, 0)) for it in items]
    outs = pl.pallas_call(
        body, grid=(nblk,), name=name, in_specs=[s for s in specs for _ in range(4)], out_specs=[s for s in specs for _ in range(3)],
        out_shape=[jax.ShapeDtypeStruct(it[0].shape, F32) for it in items for _ in range(3)],
        compiler_params=_cp("parallel"))(*[a for it in items for a in it])
    return [tuple(outs[3 * t:3 * t + 3]) for t in range(n)]


PACK = (("b_ada", 3 * D_MODEL), ("g_pre", D_MODEL), ("g_post", D_MODEL), ("gn_g", 512), ("qn_g", 64), ("kn_g", 64),
        ("w_dec_f", 4), ("w_dec_b", 4), ("loss", 1))
PACK_OFFSETS = {}
PACK_WIDTH = 0
for _name, _n in PACK:
    PACK_OFFSETS[_name] = PACK_WIDTH
    PACK_WIDTH += -(-_n // 128) * 128
SMALL_PARAMS = tuple(name for name, _ in PACK if name != "loss")


def _pack(parts):
    cols = []
    for name, n in PACK:
        pad = -(-n // 128) * 128 - n
        cols.append(parts[name].reshape(1, n).astype(F32))
        if pad:
            cols.append(jnp.zeros((1, pad), F32))
    return jnp.concatenate(cols, axis=1)


def _small_reduce(vec, cs, deps):
    ncol = 3 * D_MODEL // N_CHIP

    def body(vec_ref, cs_ref, *rest):
        tot_ref, gwa_ref, gat, ssem, rsem = rest[-5:]
        me = _dev_index(_coords())
        chip = me // 2
        gat[me] = vec_ref[...]
        sends = []
        for k in range(1, N_DEV):
            cp = _remote(vec_ref, gat.at[me], ssem.at[k - 1], rsem.at[k - 1], _peer(k))
            cp.start()
            sends.append(cp)
        for k in range(1, N_DEV):
            slot = gat.at[_dev_index(_peer(k))]
            _remote(slot, slot, ssem.at[k - 1], rsem.at[k - 1], _peer(k)).wait_recv()
        rows = [gat[d] for d in range(N_DEV)]
        tot = rows[0]
        for d in range(1, N_DEV):
            tot = tot + rows[d]
        tot_ref[...] = tot
        cs = cs_ref[...]
        ca_t = jnp.transpose(jnp.concatenate([cs * _sigmoid(cs), jnp.zeros((128 - N_DEV, D_MODEL), F32)], axis=0))
        dm = jnp.concatenate(rows + [jnp.zeros((128 - N_DEV, PACK_WIDTH), F32)], axis=0)
        for jj in range(N_CHIP):
            @pl.when(chip == jj)
            def _():
                gwa_ref[...] = _nn(ca_t, dm[:, ncol * jj:ncol * (jj + 1)], precision=HIGHEST)
        for cp in sends:
            cp.wait_send()

    return pl.pallas_call(
        body, name="small_reduce", in_specs=[VMEM_SPEC, VMEM_SPEC] + [HBM_SPEC] * len(deps), out_specs=[VMEM_SPEC] * 2,
        out_shape=[jax.ShapeDtypeStruct((1, PACK_WIDTH), F32), jax.ShapeDtypeStruct((D_MODEL, ncol), F32)],
        scratch_shapes=[pltpu.VMEM((N_DEV, 1, PACK_WIDTH), F32), pltpu.SemaphoreType.DMA((7,)), pltpu.SemaphoreType.DMA((7,))],
        compiler_params=_cp())(vec, cs, *deps)


def _small_adamw(tot, given):
    sizes = dict(PACK)
    np_ = len(SMALL_PARAMS)

    def body(*refs):
        tot_ref = refs[0]
        wmv = refs[1:1 + 3 * np_]
        outs = refs[1 + 3 * np_:1 + 7 * np_]
        loss_ref = refs[-1]
        for i, name in enumerate(SMALL_PARAMS):
            off, n = PACK_OFFSETS[name], sizes[name]
            g = tot_ref[:, off:off + n]
            w_ref, m_ref, v_ref = wmv[3 * i:3 * i + 3]
            outs[i][...] = g
            outs[np_ + i][...], outs[2 * np_ + i][...], outs[3 * np_ + i][...] = _adam_math(w_ref[...], g, m_ref[...], v_ref[...])
        loss_ref[...] = tot_ref[:, PACK_OFFSETS["loss"]:PACK_OFFSETS["loss"] + 1]

    flat = [a for name in SMALL_PARAMS for a in given[name]]
    shapes = [jax.ShapeDtypeStruct((1, sizes[name]), F32) for name in SMALL_PARAMS]
    res = pl.pallas_call(body, name="small_adamw", in_specs=[VMEM_SPEC] * (1 + 3 * np_), out_specs=[VMEM_SPEC] * (4 * np_ + 1),
                         out_shape=shapes * 4 + [jax.ShapeDtypeStruct((1, 1), F32)], compiler_params=_cp())(tot, *flat)
    return [dict(zip(SMALL_PARAMS, res[k * np_:(k + 1) * np_])) for k in range(4)], res[-1]


def kernel(x, c, w_ada, b_ada, g_pre, w_in, qn_g, kn_g, w_dec_f, w_dec_b, gn_g, w_pa, w_pr, w_out, g_post, loss_target, m_w_ada, m_b_ada, m_g_pre, m_w_in, m_qn_g, m_kn_g, m_w_dec_f, m_w_dec_b, m_gn_g, m_w_pa, m_w_pr, m_w_out, m_g_post, v_w_ada, v_b_ada, v_g_pre, v_w_in, v_qn_g, v_kn_g, v_w_dec_f, v_w_dec_b, v_gn_g, v_w_pa, v_w_pr, v_w_out, v_g_post):
    shards = (w_in[0].T.astype(MXU_DTYPE), jnp.concatenate([w_pa[0].T, w_pr[0].T], axis=1).astype(MXU_DTYPE),
              w_out[0].astype(MXU_DTYPE))
    cs, mod, w_attn = _mod_and_attn_rows(c, w_ada[0], b_ada.reshape(N_CHIP, 3 * D_MODEL // N_CHIP), shards[0])
    started = {}

    def rest_start(dep, *placed):
        started["rest"] = _gather_rest_start(shards, placed, dep)
        return started["rest"][-1]

    def rest_wait(after):
        return _gather_rest_wait(started["rest"], after)

    kinds = {"early": (1, 2), "late": (0,)}

    def send(name, arrays):
        started[name] = _scatter_start(arrays, kinds[name], "grad_scatter_" + name)
        return started[name][-1]

    grad_x, _, _, _, small = _local_step(x[0], loss_target[0], mod, g_pre, qn_g, kn_g, w_dec_f, w_dec_b,
                                         gn_g, g_post, w_attn, rest_start, rest_wait, send=send, own=shards)
    small = dict(small)
    small["b_ada"] = small.pop("dmod")
    given = dict(b_ada=(b_ada, m_b_ada, v_b_ada), g_pre=(g_pre, m_g_pre, v_g_pre), g_post=(g_post, m_g_post, v_g_post),
                 gn_g=(gn_g, m_gn_g, v_gn_g), qn_g=(qn_g, m_qn_g, v_qn_g), kn_g=(kn_g, m_kn_g, v_kn_g),
                 w_dec_f=(w_dec_f, m_w_dec_f, v_w_dec_f), w_dec_b=(w_dec_b, m_w_dec_b, v_w_dec_b))
    grads, deltas, new_m, new_v = {}, {}, {}, {}

    def update(call, back, **items):
        done = _adamw(list(items.values()), "adamw_" + call)
        for (name, (w, g, m, v)), (d, nm, nv) in zip(items.items(), done):
            grads[name], deltas[name], new_m[name], new_v[name] = back(g), back(d), back(nm), back(nv)
        return tuple(d for d, _, _ in done)

    lead = lambda a: a[None]
    (g_pt, g_out), (z_pt, z_out) = _scatter_wait(started["early"], kinds["early"], grad_x, "grad_scatter_early_wait")
    r_pt, r_out = _grad_finish((g_pt, g_out), (z_pt, z_out), kinds["early"], "grad_finish_early")
    early = update("early", lead, w_pa=(w_pa[0], r_pt[:, :512].T, m_w_pa[0], v_w_pa[0]),
                   w_pr=(w_pr[0], r_pt[:, 512:].T, m_w_pr[0], v_w_pr[0]), w_out=(w_out[0], r_out, m_w_out[0], v_w_out[0]))
    tot, g_w_ada = _small_reduce(_pack(small), cs.reshape(N_DEV, D_MODEL), early)
    small_parts, loss = _small_adamw(tot, given)
    for dst, part in zip((grads, deltas, new_m, new_v), small_parts):
        dst.update(part)
    (last,) = update("w_ada", lead, w_ada=(w_ada[0], g_w_ada, m_w_ada[0], v_w_ada[0]))

    (g_in_t,), (z_in,) = _scatter_wait(started["late"], kinds["late"], last, "grad_scatter_late_wait")
    (r_in,) = _grad_finish((g_in_t,), (z_in,), kinds["late"], "grad_finish_late")
    tr = lambda a: a[0].T
    update("w_in", lambda a: a.T[None], w_in=(tr(w_in), r_in, tr(m_w_in), tr(v_w_in)))
    order = ("w_ada", "b_ada", "g_pre", "w_in", "qn_g", "kn_g", "w_dec_f", "w_dec_b", "gn_g", "w_pa", "w_pr", "w_out", "g_post")
    return (loss[0, 0], grad_x[None], *[grads[n] for n in order], *[deltas[n] for n in order],
            *[new_m[n] for n in order], *[new_v[n] for n in order])
```

```python
import jax
import jax.numpy as jnp
import numpy as np
from jax import lax
from jax.experimental import pallas as pl
from jax.experimental.pallas import tpu as pltpu

F32 = jnp.float32
BF16 = jnp.bfloat16
MXU_DTYPE = jnp.bfloat16

D_MODEL = 1024
GRID_W = 64
HEAD_DIM = 64
ROPE_THETA = 10000.0
EPS = 1e-6
RET_HEADS = 4
RET_CHUNK = 256
RET_GROUP = 4
IN_WIDTH = 4864
QA, KA, VA, ZA, QR, KR, VR, ZR, GL = 0, 512, 640, 768, 1280, 1536, 1792, 2304, 2816
ATTN_COLS = ZA
ZA_B, QR_B, KR_B, VR_B, ZR_B, GL_B = (c - ATTN_COLS for c in (ZA, QR, KR, VR, ZR, GL))

ADAM_LR, ADAM_B1, ADAM_B2, ADAM_EPS, ADAM_WD, ADAM_STEP = 0.001, 0.9, 0.999, 1e-08, 0.01, 10

VMEM_LIMIT = 56 * 1024 * 1024
MESH = pl.DeviceIdType.MESH
HIGHEST = lax.Precision.HIGHEST
LOG2E = 1.4426950408889634
LN2 = 0.6931471805599453


def _cp(*sem, **kw):
    if sem:
        kw["dimension_semantics"] = sem
    return pltpu.CompilerParams(vmem_limit_bytes=VMEM_LIMIT, **kw)


def _nn(a, b, **kw):
    return lax.dot_general(a, b, (((1,), (0,)), ((), ())), preferred_element_type=F32, **kw)


def _nt(a, b):
    return lax.dot_general(a, b, (((1,), (1,)), ((), ())), preferred_element_type=F32)


def _tn(a, b):
    return lax.dot_general(a, b, (((0,), (0,)), ((), ())), preferred_element_type=F32)


def _mx(x):
    return x.astype(MXU_DTYPE)


def _lane(shape):
    return lax.broadcasted_iota(jnp.int32, shape, 1)


def _sigmoid(z):
    return 1.0 / (1.0 + jnp.exp(-z))


def _rowsum128(x):
    s = jnp.sum(x, axis=0, keepdims=True)
    out = s[:, 0:128]
    for k in range(1, x.shape[1] // 128):
        out = out + s[:, 128 * k:128 * (k + 1)]
    return out


def _swap16(x):
    return jnp.where((_lane(x.shape) & 16) == 0, pltpu.roll(x, 112, 1), pltpu.roll(x, 16, 1))


def _rope(x, cos, sin):
    return x * cos + _swap16(x) * sin


def _rope_t(d, cos, sin):
    return d * cos + _swap16(d * sin)


def _blockdiag64():
    r = lax.broadcasted_iota(jnp.int32, (128, 128), 0) // 64
    c = lax.broadcasted_iota(jnp.int32, (128, 128), 1) // 64
    return (r == c).astype(BF16)


def _seg64(x, m):
    hi = x.astype(BF16)
    lo = (x - hi.astype(F32)).astype(BF16)
    return _nn(hi, m) + _nn(lo, m)


def _rope_tables(seq):
    t = np.arange(seq)
    row = (t // GRID_W).astype(np.float32)
    col = (t % GRID_W).astype(np.float32)
    half = HEAD_DIM // 2
    inv_freq = (np.float32(ROPE_THETA) ** (-np.arange(0, half, 2, dtype=np.float32) / np.float32(half))).astype(np.float32)
    ar = (row[:, None] * inv_freq[None, :]).astype(np.float32).astype(np.float64)
    ac = (col[:, None] * inv_freq[None, :]).astype(np.float32).astype(np.float64)
    cr, sr, cc, sc = np.cos(ar), np.sin(ar), np.cos(ac), np.sin(ac)
    cos64 = np.concatenate([cr, cr, cc, cc], axis=1)
    sin64 = np.concatenate([-sr, sr, -sc, sc], axis=1)
    return jnp.asarray(np.tile(cos64, (1, 2)), F32), jnp.asarray(np.tile(sin64, (1, 2)), F32)


def _attn_inputs(x, mod, g_pre, w_attn, cos, sin, qg2, kg2, own=()):
    seq = x.shape[0]
    bs = 512
    nt = len(own)
    nstep = seq // bs

    def body(x_ref, mod_ref, g_ref, w_ref, cos_ref, sin_ref, qg_ref, kg_ref, *rest):
        srcs, (h_ref, pa_ref, qt_ref, kd_ref, vd_ref) = rest[:nt], rest[nt:nt + 5]
        lands, stage = rest[nt + 5:2 * nt + 5], rest[2 * nt + 5:3 * nt + 5]
        if nt:
            osem = rest[3 * nt + 5]
            step = pl.program_id(0)
            cx, cy, _ = _coords()
            chip = 2 * cx + cy
            loads = [pltpu.make_async_copy(srcs[t], stage[t], osem.at[t]) for t in range(nt)]
            stores = [pltpu.make_async_copy(stage[t], _rows(lands[t], chip * SHARD_ROWS[t], SHARD_ROWS[t]), osem.at[nt + t])
                      for t in range(nt)]

            @pl.when(step == 0)
            def _():
                for cp in loads:
                    cp.start()

            @pl.when(step == min(1, nstep - 1))
            def _():
                for cp in loads:
                    cp.wait()
                for cp in stores:
                    cp.start()

        xv = x_ref[...]
        r = lax.rsqrt(jnp.mean(xv * xv, axis=-1, keepdims=True) + EPS)
        shift = mod_ref[:, 0:D_MODEL]
        scale = mod_ref[:, D_MODEL:2 * D_MODEL]
        hb = ((xv * r) * g_ref[...] * (1.0 + scale) + shift).astype(h_ref.dtype)
        h_ref[...] = hb
        p = _nt(hb, w_ref[...])
        pa_ref[...] = p
        m = _blockdiag64()
        cs, sn = cos_ref[...], sin_ref[...]
        lo = _lane((bs, 128)) < 64
        for pr in range(4):
            q = p[:, QA + 128 * pr:QA + 128 * (pr + 1)]
            r = lax.rsqrt(_seg64(q * q, m) * (1.0 / 64) + EPS)
            qt_ref[:, 128 * pr:128 * (pr + 1)] = (_rope(q * r * qg_ref[...], cs, sn) * (0.125 * LOG2E)).astype(qt_ref.dtype)
        k = p[:, KA:KA + 128]
        r = lax.rsqrt(_seg64(k * k, m) * (1.0 / 64) + EPS)
        kr = _rope(k * r * kg_ref[...], cs, sn)
        ksw = pltpu.roll(kr, 64, 1)
        kd_ref[0] = jnp.where(lo, kr, ksw).astype(kd_ref.dtype)
        kd_ref[1] = jnp.where(lo, ksw, kr).astype(kd_ref.dtype)
        v = p[:, VA:VA + 128]
        vsw = pltpu.roll(v, 64, 1)
        vd_ref[0] = jnp.where(lo, v, vsw).astype(vd_ref.dtype)
        vd_ref[1] = jnp.where(lo, vsw, v).astype(vd_ref.dtype)

        if nt:
            @pl.when(step == nstep - 1)
            def _():
                for cp in stores:
                    cp.wait()

    row = lambda w: pl.BlockSpec((bs, w), lambda i: (i, 0))
    fix = lambda a, b: pl.BlockSpec((a, b), lambda i: (0, 0))
    dup = pl.BlockSpec((2, bs, 128), lambda i: (0, i, 0))
    sds = jax.ShapeDtypeStruct
    anywhere = pl.BlockSpec(memory_space=pl.ANY)
    return pl.pallas_call(
        body, grid=(nstep,), name="attn_inputs",
        in_specs=[row(D_MODEL), fix(1, 3 * D_MODEL), fix(1, D_MODEL), fix(ATTN_COLS, D_MODEL), row(128), row(128), fix(1, 128), fix(1, 128)]
        + [anywhere] * nt,
        out_specs=[row(D_MODEL), row(ATTN_COLS), row(512), dup, dup] + [anywhere] * nt,
        out_shape=[sds((seq, D_MODEL), MXU_DTYPE), sds((seq, ATTN_COLS), F32), sds((seq, 512), MXU_DTYPE),
                   sds((2, seq, 128), MXU_DTYPE), sds((2, seq, 128), MXU_DTYPE)]
        + [sds((N_CHIP * SHARD_ROWS[t], s.shape[1]), s.dtype) for t, s in enumerate(own)],
        scratch_shapes=[pltpu.VMEM(s.shape, s.dtype) for s in own] + ([pltpu.SemaphoreType.DMA((2 * nt,))] if nt else []),
        compiler_params=_cp("arbitrary"))(x, mod, g_pre, w_attn, cos, sin, qg2, kg2, *own)


def _in_proj_dx(x, dp, win_t, dout, mod, g_pre, dep=None):
    seq = x.shape[0]
    bs = 512
    deps, dep_specs = _dep_args(dep, 1)

    def body(x_ref, dp_ref, w_ref, dout_ref, mod_ref, g_ref, *rest):
        gx_ref, dshift_ref, dscale_ref, dg_ref = rest[-4:]

        @pl.when(pl.program_id(0) == 0)
        def _():
            dshift_ref[...] = jnp.zeros_like(dshift_ref)
            dscale_ref[...] = jnp.zeros_like(dscale_ref)
            dg_ref[...] = jnp.zeros_like(dg_ref)

        xv = x_ref[...]
        dh = _nn(dp_ref[...], w_ref[...])
        g = g_ref[...]
        r = lax.rsqrt(jnp.mean(xv * xv, axis=-1, keepdims=True) + EPS)
        xn = xv * r
        scale = mod_ref[:, D_MODEL:2 * D_MODEL]
        dshift_ref[...] += jnp.sum(dh, axis=0, keepdims=True)
        dscale_ref[...] += jnp.sum(dh * (xn * g), axis=0, keepdims=True)
        da = dh * (1.0 + scale)
        dg_ref[...] += jnp.sum(da * xn, axis=0, keepdims=True)
        dxn = da * g
        dx = r * (dxn - xn * jnp.mean(dxn * xn, axis=-1, keepdims=True))
        gx_ref[...] = dout_ref[...] + dx

    row = pl.BlockSpec((bs, D_MODEL), lambda i: (i, 0))
    vec = pl.BlockSpec((1, D_MODEL), lambda i: (0, 0))
    return pl.pallas_call(
        body, grid=(seq // bs,), name="in_proj_dx",
        in_specs=[row, pl.BlockSpec((bs, IN_WIDTH), lambda i: (i, 0)), pl.BlockSpec((IN_WIDTH, D_MODEL), lambda i: (0, 0)), row,
                  pl.BlockSpec((1, 3 * D_MODEL), lambda i: (0, 0)), vec] + dep_specs,
        out_specs=[row, vec, vec, vec],
        out_shape=[jax.ShapeDtypeStruct((seq, D_MODEL), F32)] + [jax.ShapeDtypeStruct((1, D_MODEL), F32)] * 3,
        compiler_params=_cp("arbitrary"))(x, dp, win_t, dout, mod, g_pre, *deps)


def _dep_args(dep, grid_rank):
    if dep is None:
        return [], []
    return [dep], [pl.BlockSpec(dep.shape, lambda *_: (0,) * dep.ndim)]


def _matmul(a, b, *, ta, tb, tm, tn, out_dtype, name, dep=None):
    kdim = a.shape[0] if ta else a.shape[1]
    m = a.shape[1] if ta else a.shape[0]
    n = b.shape[0] if tb else b.shape[1]
    assert m % tm == 0 and n % tn == 0
    deps, dep_specs = _dep_args(dep, 2)

    def body(a_ref, b_ref, *rest):
        o_ref = rest[-1]
        dims = (((0 if ta else 1,), (1 if tb else 0,)), ((), ()))
        o_ref[...] = lax.dot_general(a_ref[...], b_ref[...], dims, preferred_element_type=F32).astype(o_ref.dtype)

    a_spec = pl.BlockSpec((kdim, tm), lambda j, i: (0, i)) if ta else pl.BlockSpec((tm, kdim), lambda j, i: (i, 0))
    b_spec = pl.BlockSpec((tn, kdim), lambda j, i: (j, 0)) if tb else pl.BlockSpec((kdim, tn), lambda j, i: (0, j))
    return pl.pallas_call(
        body, grid=(n // tn, m // tm), name=name, in_specs=[a_spec, b_spec] + dep_specs,
        out_specs=pl.BlockSpec((tm, tn), lambda j, i: (i, j)),
        out_shape=jax.ShapeDtypeStruct((m, n), out_dtype), compiler_params=_cp("parallel", "parallel"))(a, b, *deps)


def _in_proj_rest(h, win_t, cos, sin):
    seq = h.shape[0]
    tm = min(1024, seq)
    ncols = IN_WIDTH - ATTN_COLS
    tn = ncols // 2
    assert ZR_B <= tn and ATTN_COLS % 128 == 0 and tn % 128 == 0

    def body(h_ref, w_ref, cos_ref, sin_ref, p_ref, rq_ref, rk_ref, rv_ref):
        p = _nt(h_ref[...], w_ref[...])
        p_ref[...] = p

        @pl.when(pl.program_id(0) == 0)
        def _():
            cs, sn = cos_ref[...], sin_ref[...]
            for pr in range(2):
                sl = slice(128 * pr, 128 * (pr + 1))
                rq_ref[:, sl] = _rope(p[:, QR_B + 128 * pr:QR_B + 128 * (pr + 1)], cs, sn).astype(rq_ref.dtype)
                rk_ref[:, sl] = (_rope(p[:, KR_B + 128 * pr:KR_B + 128 * (pr + 1)], cs, sn) * 0.125).astype(rk_ref.dtype)
            rv_ref[...] = p[:, VR_B:VR_B + 512].astype(rv_ref.dtype)

    nrow = seq // tm
    row = lambda w: pl.BlockSpec((tm, w), lambda j, i: (i, 0))
    park = lambda w: pl.BlockSpec((tm, w), lambda j, i: (jnp.where(j == 0, i, nrow - 1), 0))
    sds = jax.ShapeDtypeStruct
    return pl.pallas_call(
        body, grid=(2, nrow), name="in_proj_rest",
        in_specs=[row(D_MODEL), pl.BlockSpec((pl.Element(tn), pl.Element(D_MODEL)),
                                             lambda j, i: (pl.multiple_of(ATTN_COLS + j * tn, 128), 0)), row(128), row(128)],
        out_specs=[pl.BlockSpec((tm, tn), lambda j, i: (i, j)), park(256), park(256), park(512)],
        out_shape=[sds((seq, ncols), F32), sds((seq, 256), MXU_DTYPE), sds((seq, 256), MXU_DTYPE), sds((seq, 512), MXU_DTYPE)],
        compiler_params=_cp("arbitrary", "arbitrary"))(h, win_t, cos, sin)


def _halves(kd):
    lo = _lane(kd.shape) < 64
    z = jnp.zeros_like(kd)
    return jnp.concatenate([jnp.where(lo, kd, z), jnp.where(lo, z, kd)], axis=0)


def _attn_fwd(qt, kd, vd, dep=None):
    seq = qt.shape[0]
    bq, bk = min(2048, seq), min(1024, seq)
    nk = seq // bk
    deps, dep_specs = _dep_args(dep, 2)

    def body(q_ref, k_ref, v_ref, *rest):
        o_ref, lse_ref, m_scr, l_scr, acc = rest[-5:]
        j = pl.program_id(1)
        m_scr[...] = jnp.full_like(m_scr, -jnp.inf)
        l_scr[...] = jnp.zeros_like(l_scr)
        acc[...] = jnp.zeros_like(acc)
        sub = min(512, bq)
        nsub = bq // sub
        lo = _lane((sub, 128)) < 64

        def kv_block(n, carry):
            rows = pl.ds(pl.multiple_of(n * bk, bk), bk)
            k2, v2 = _halves(k_ref[rows, :]), _halves(v_ref[rows, :])
            for pr in range(2):
                for hf in range(nsub):
                    qr = slice(hf * sub, (hf + 1) * sub)
                    s2 = _nt(q_ref[qr, 128 * pr:128 * (pr + 1)], k2)
                    ps, alphas = [], []
                    for e in range(2):
                        g = 2 * pr + e
                        s = s2[:, e * bk:(e + 1) * bk]
                        m_prev = m_scr[g, qr]
                        m_next = jnp.maximum(m_prev, jnp.max(s, axis=1, keepdims=True))
                        alpha = jnp.exp2(m_prev - m_next)
                        pe = jnp.exp2(s - jnp.tile(m_next, (1, bk // 128)))
                        l_scr[g, qr] = alpha * l_scr[g, qr] + jnp.sum(pe, axis=1, keepdims=True)
                        m_scr[g, qr] = m_next
                        ps.append(_mx(pe))
                        alphas.append(alpha)
                    o2 = _nn(jnp.concatenate(ps, axis=1), v2)
                    sl = slice(128 * pr, 128 * (pr + 1))
                    acc[qr, sl] = acc[qr, sl] * jnp.where(lo, alphas[0], alphas[1]) + o2
            return carry

        lax.fori_loop(0, nk, kv_block, 0)
        lo_all = _lane((bq, 128)) < 64
        for pr in range(2):
            sl = slice(128 * pr, 128 * (pr + 1))
            o_ref[:, sl] = acc[:, sl] / jnp.where(lo_all, l_scr[2 * pr], l_scr[2 * pr + 1])
        for g in range(4):
            lse = jnp.transpose(m_scr[g] + jnp.log2(l_scr[g]))
            lse_ref[pl.ds(4 * j + g, 1), :] = lse[0:1, :]

    return pl.pallas_call(
        body, grid=(seq // bq, 2), name="attn_fwd",
        in_specs=[pl.BlockSpec((bq, 256), lambda i, j: (i, j)), pl.BlockSpec((None, seq, 128), lambda i, j: (j, 0, 0)),
                  pl.BlockSpec((None, seq, 128), lambda i, j: (j, 0, 0))] + dep_specs,
        out_specs=[pl.BlockSpec((bq, 256), lambda i, j: (i, j)), pl.BlockSpec((8, bq), lambda i, j: (0, i))],
        out_shape=[jax.ShapeDtypeStruct((seq, 512), F32), jax.ShapeDtypeStruct((8, seq), F32)],
        scratch_shapes=[pltpu.VMEM((4, bq, 128), F32), pltpu.VMEM((4, bq, 128), F32), pltpu.VMEM((bq, 256), F32)],
        compiler_params=_cp("parallel", "arbitrary"))(qt, kd, vd, *deps)


def _attn_bwd(qt, kd, vd, do, lse, delta, dep=None):
    seq = qt.shape[0]
    bq, bk = min(1024, seq), min(1024, seq)
    nq, nk = seq // bq, seq // bk
    deps, dep_specs = _dep_args(dep, 3)

    def body(q_ref, do_ref, k_ref, v_ref, lse_ref, del_ref, *rest):
        dq_ref, dk_ref, dv_ref = rest[-3:]
        j, i = pl.program_id(0), pl.program_id(1)
        dq_ref[...] = jnp.zeros_like(dq_ref)

        @pl.when(i == 0)
        def _():
            dk_ref[...] = jnp.zeros_like(dk_ref)
            dv_ref[...] = jnp.zeros_like(dv_ref)

        lo = _lane((bk, 128)) < 64
        big = lambda r: jnp.concatenate([jnp.broadcast_to(r[0], (bk, bq)), jnp.broadcast_to(r[1], (bk, bq))], axis=0)

        def kv_block(n, carry):
            rows = pl.ds(pl.multiple_of(n * bk, bk), bk)
            k2, v2 = _halves(k_ref[rows, :]), _halves(v_ref[rows, :])
            for pr in range(2):
                sl = slice(128 * pr, 128 * (pr + 1))
                qp, dop = q_ref[:, sl], do_ref[:, sl]
                s_t = _nt(k2, qp)
                dp_t = _nt(v2, dop)
                ls = [lse_ref[pl.ds(4 * j + 2 * pr + e, 1), :] for e in range(2)]
                dl = [del_ref[pl.ds(4 * j + 2 * pr + e, 1), :] for e in range(2)]
                p_t = jnp.exp2(s_t - big(ls))
                ds_t = _mx(p_t * (dp_t - big(dl)))
                rv = _nn(_mx(p_t), dop)
                rk = _nn(ds_t, qp) * LN2
                dv_ref[rows, :] += jnp.where(lo, rv[:bk], rv[bk:])
                dk_ref[rows, :] += jnp.where(lo, rk[:bk], rk[bk:])
                dq_ref[:, sl] += _tn(ds_t, k2)
            return carry

        lax.fori_loop(0, nk, kv_block, 0)

    return pl.pallas_call(
        body, grid=(2, nq), name="attn_bwd",
        in_specs=[pl.BlockSpec((bq, 256), lambda j, i: (i, j)), pl.BlockSpec((bq, 256), lambda j, i: (i, j)),
                  pl.BlockSpec((None, seq, 128), lambda j, i: (j, 0, 0)), pl.BlockSpec((None, seq, 128), lambda j, i: (j, 0, 0)),
                  pl.BlockSpec((8, bq), lambda j, i: (0, i)), pl.BlockSpec((8, bq), lambda j, i: (0, i))] + dep_specs,
        out_specs=[pl.BlockSpec((bq, 256), lambda j, i: (i, j)), pl.BlockSpec((None, seq, 128), lambda j, i: (j, 0, 0)),
                   pl.BlockSpec((None, seq, 128), lambda j, i: (j, 0, 0))],
        out_shape=[jax.ShapeDtypeStruct((seq, 512), F32), jax.ShapeDtypeStruct((2, seq, 128), F32),
                   jax.ShapeDtypeStruct((2, seq, 128), F32)],
        compiler_params=_cp("arbitrary", "arbitrary"))(qt, do, kd, vd, lse, delta, *deps)


def _ret_tables(lg_f, lg_b, c):
    idx = jnp.arange(c, dtype=F32)
    diff = idx[:, None] - idx[None, :]
    a, b = lg_f[:, None, None], lg_b[:, None, None]
    low, up = (diff >= 0)[None], (diff < 0)[None]
    dmat = jnp.where(low, jnp.exp(a * jnp.maximum(diff, 0.0)[None]), jnp.exp(b * jnp.maximum(-diff, 0.0)[None]))
    dda = jnp.where(low, diff[None] * jnp.exp(a * jnp.maximum(diff, 0.0)[None]), 0.0)
    ddb = jnp.where(up, -diff[None] * jnp.exp(b * jnp.maximum(-diff, 0.0)[None]), 0.0)
    rep = lambda w: jnp.broadcast_to(w[:, :, None], (RET_HEADS, c, 128))
    wqf = rep(jnp.exp(lg_f[:, None] * (idx + 1.0)[None]))
    wqb = rep(jnp.exp(lg_b[:, None] * (c - idx)[None]))
    wkf = rep(jnp.exp(lg_f[:, None] * (c - 1.0 - idx)[None]))
    wkb = rep(jnp.exp(lg_b[:, None] * idx[None]))
    cdf, cdb = jnp.exp(lg_f * c), jnp.exp(lg_b * c)
    dec_fb = jnp.broadcast_to(jnp.concatenate([cdf, cdb])[:, None], (8, 128))
    dec_bf = jnp.broadcast_to(jnp.concatenate([cdb, cdf])[:, None], (8, 128))
    return dict(dmat=dmat, dda=dda, ddb=ddb, wqf=wqf, wqb=wqb, wkf=wkf, wkb=wkb, dec_fb=dec_fb, dec_bf=dec_bf)


def _ret_states(xk, yv, w_fwd, w_rev, dec, name):
    seq = xk.shape[0]
    c = RET_CHUNK
    nc = seq // c
    grp = min(RET_GROUP, nc)
    ns = nc // grp

    def body(xf_ref, yf_ref, xr_ref, yr_ref, wf_ref, wr_ref, dec_ref, sf_ref, sr_ref, st_f, st_r):
        @pl.when(pl.program_id(0) == 0)
        def _():
            st_f[...] = jnp.zeros_like(st_f)
            st_r[...] = jnp.zeros_like(st_r)

        lane = _lane((c, 128))

        def chunk(g, carry):
            for x_ref, y_ref, w_ref, s_ref, st, base, gg in ((xf_ref, yf_ref, wf_ref, sf_ref, st_f, 0, g),
                                                             (xr_ref, yr_ref, wr_ref, sr_ref, st_r, 4, grp - 1 - g)):
                rows = pl.ds(pl.multiple_of(gg * c, c), c)
                for h in range(RET_HEADS):
                    pr, e = h // 2, h % 2
                    half = (lane < 64) if e == 0 else (lane >= 64)
                    s_ref[gg, h] = st[h].astype(s_ref.dtype)
                    xm = jnp.where(half, x_ref[rows, 128 * pr:128 * (pr + 1)].astype(F32) * w_ref[h], 0.0)
                    st[h] = st[h] * dec_ref[base + h:base + h + 1, :] + _tn(_mx(xm), _mx(y_ref[rows, 128 * h:128 * (h + 1)]))
            return carry

        lax.fori_loop(0, grp, chunk, 0, unroll=True)

    xs = lambda f: pl.BlockSpec((grp * c, 256), f)
    ys = lambda f: pl.BlockSpec((grp * c, 512), f)
    fwd, rev = (lambda n: (n, 0)), (lambda n: (ns - 1 - n, 0))
    wsp = pl.BlockSpec((RET_HEADS, c, 128), lambda n: (0, 0, 0))
    return pl.pallas_call(
        body, grid=(ns,), name=name,
        in_specs=[xs(fwd), ys(fwd), xs(rev), ys(rev), wsp, wsp, pl.BlockSpec((8, 128), lambda n: (0, 0))],
        out_specs=[pl.BlockSpec((grp, RET_HEADS, 128, 128), lambda n: (n, 0, 0, 0)),
                   pl.BlockSpec((grp, RET_HEADS, 128, 128), lambda n: (ns - 1 - n, 0, 0, 0))],
        out_shape=[jax.ShapeDtypeStruct((nc, RET_HEADS, 128, 128), MXU_DTYPE)] * 2,
        scratch_shapes=[pltpu.VMEM((RET_HEADS, 128, 128), F32), pltpu.VMEM((RET_HEADS, 128, 128), F32)],
        compiler_params=_cp("arbitrary"))(xk, yv, xk, yv, w_fwd, w_rev, dec)


def _ret_fwd(rq, rk, rv, rf, rb, tb):
    seq = rq.shape[0]
    c = RET_CHUNK
    grp = min(RET_GROUP, seq // c)

    def body(q_ref, k_ref, v_ref, rf_ref, rb_ref, d_ref, wqf_ref, wqb_ref, o_ref):
        lane = _lane((c, 128))

        def chunk(g, carry):
            rows = pl.ds(pl.multiple_of(g * c, c), c)
            for h in range(RET_HEADS):
                pr, e = h // 2, h % 2
                half = (lane < 64) if e == 0 else (lane >= 64)
                sl = slice(128 * pr, 128 * (pr + 1))
                qp, kp = q_ref[rows, sl], k_ref[rows, sl]
                km = jnp.where(half, kp, jnp.zeros_like(kp))
                sd = _mx(_nt(qp, km) * d_ref[h])
                qf = qp.astype(F32)
                o = _nn(sd, v_ref[rows, 128 * h:128 * (h + 1)])
                o = o + _nn(_mx(qf * wqf_ref[h]), rf_ref[g, h]) + _nn(_mx(qf * wqb_ref[h]), rb_ref[g, h])
                o_ref[rows, 128 * h:128 * (h + 1)] = o
            return carry

        lax.fori_loop(0, grp, chunk, 0, unroll=True)

    st = pl.BlockSpec((grp, RET_HEADS, 128, 128), lambda n: (n, 0, 0, 0))
    wsp = pl.BlockSpec((RET_HEADS, c, 128), lambda n: (0, 0, 0))
    return pl.pallas_call(
        body, grid=(seq // (grp * c),), name="ret_fwd",
        in_specs=[pl.BlockSpec((grp * c, 256), lambda n: (n, 0)), pl.BlockSpec((grp * c, 256), lambda n: (n, 0)),
                  pl.BlockSpec((grp * c, 512), lambda n: (n, 0)), st, st, pl.BlockSpec((RET_HEADS, c, c), lambda n: (0, 0, 0)), wsp, wsp],
        out_specs=pl.BlockSpec((grp * c, 512), lambda n: (n, 0)),
        out_shape=jax.ShapeDtypeStruct((seq, 512), F32), compiler_params=_cp("parallel"))(
            rq, rk, rv, rf, rb, tb["dmat"], tb["wqf"], tb["wqb"])


def _ret_bwd(rq, rk, rv, do, rf, rb, hf, hb, tb):
    seq = rq.shape[0]
    c = RET_CHUNK
    grp = min(RET_GROUP, seq // c)

    def body(q_ref, k_ref, v_ref, do_ref, rf_ref, rb_ref, hf_ref, hb_ref, d_ref, dda_ref, ddb_ref,
             wqf_ref, wqb_ref, wkf_ref, wkb_ref, cdec_ref, dq_ref, dk_ref, dv_ref, dlg_ref):
        @pl.when(pl.program_id(0) == 0)
        def _():
            dlg_ref[...] = jnp.zeros_like(dlg_ref)

        lane = _lane((c, 128))
        pos = lax.broadcasted_iota(jnp.int32, (c, 128), 0).astype(F32)

        def chunk(g, carry):
            rows = pl.ds(pl.multiple_of(g * c, c), c)
            for pr in range(2):
                sl = slice(128 * pr, 128 * (pr + 1))
                qp, kp = q_ref[rows, sl], k_ref[rows, sl]
                qf, kf = qp.astype(F32), kp.astype(F32)
                dq_acc = jnp.zeros((c, 128), F32)
                dk_acc = jnp.zeros((c, 128), F32)
                for e in range(2):
                    h = 2 * pr + e
                    half = (lane < 64) if e == 0 else (lane >= 64)
                    hs = slice(128 * h, 128 * (h + 1))
                    km = jnp.where(half, kp, jnp.zeros_like(kp))
                    kmf = km.astype(F32)
                    vh, doh = v_ref[rows, hs], do_ref[rows, hs]
                    rfh, rbh, hfh, hbh = rf_ref[g, h], rb_ref[g, h], hf_ref[g, h], hb_ref[g, h]
                    s = _nt(qp, km)
                    dpm = _nt(doh, vh)
                    ds_b = _mx(dpm * d_ref[h])
                    sd_b = _mx(s * d_ref[h])
                    dq_if = wqf_ref[h] * _nt(doh, rfh)
                    dq_ib = wqb_ref[h] * _nt(doh, rbh)
                    dq_acc = dq_acc + _nn(ds_b, km) + dq_if + dq_ib
                    dk_sf = wkf_ref[h] * _nt(vh, hfh)
                    dk_sb = wkb_ref[h] * _nt(vh, hbh)
                    dk_acc = dk_acc + jnp.where(half, _tn(ds_b, qp), 0.0) + dk_sf + dk_sb
                    dv = _tn(sd_b, doh) + _nn(_mx(kmf * wkf_ref[h]), hfh) + _nn(_mx(kmf * wkb_ref[h]), hbh)
                    dv_ref[rows, hs] = dv.astype(dv_ref.dtype)
                    sdp = s * dpm
                    hr_f = hfh.astype(F32) * rfh.astype(F32)
                    hr_b = hbh.astype(F32) * rbh.astype(F32)
                    da = (_rowsum128(sdp * dda_ref[h]) + _rowsum128((pos + 1.0) * qf * dq_if)
                          + _rowsum128((c - 1.0 - pos) * kf * dk_sf) + cdec_ref[h:h + 1, :] * _rowsum128(hr_f))
                    db = (_rowsum128(sdp * ddb_ref[h]) + _rowsum128((c - pos) * qf * dq_ib)
                          + _rowsum128(pos * kf * dk_sb) + cdec_ref[4 + h:5 + h, :] * _rowsum128(hr_b))
                    dlg_ref[h:h + 1, :] += da
                    dlg_ref[4 + h:5 + h, :] += db
                dq_ref[rows, sl] = dq_acc
                dk_ref[rows, sl] = dk_acc
            return carry

        lax.fori_loop(0, grp, chunk, 0, unroll=2)

    st = pl.BlockSpec((grp, RET_HEADS, 128, 128), lambda n: (n, 0, 0, 0))
    wsp = pl.BlockSpec((RET_HEADS, c, 128), lambda n: (0, 0, 0))
    dsp = pl.BlockSpec((RET_HEADS, c, c), lambda n: (0, 0, 0))
    x256 = pl.BlockSpec((grp * c, 256), lambda n: (n, 0))
    x512 = pl.BlockSpec((grp * c, 512), lambda n: (n, 0))
    cdec = tb["dec_fb"] * float(c)
    return pl.pallas_call(
        body, grid=(seq // (grp * c),), name="ret_bwd",
        in_specs=[x256, x256, x512, x512, st, st, st, st, dsp, dsp, dsp, wsp, wsp, wsp, wsp,
                  pl.BlockSpec((8, 128), lambda n: (0, 0))],
        out_specs=[x256, x256, x512, pl.BlockSpec((8, 128), lambda n: (0, 0))],
        out_shape=[jax.ShapeDtypeStruct((seq, 256), F32), jax.ShapeDtypeStruct((seq, 256), F32),
                   jax.ShapeDtypeStruct((seq, 512), MXU_DTYPE), jax.ShapeDtypeStruct((8, 128), F32)],
        compiler_params=_cp("arbitrary"))(
            rq, rk, rv, do, rf, rb, hf, hb, tb["dmat"], tb["dda"], tb["ddb"], tb["wqf"], tb["wqb"], tb["wkf"], tb["wkb"], cdec)


def _tail(x, tgt, o_att, o_ret, p, mod, g_post, gn_g, wpt, wout):
    seq = x.shape[0]
    bs = 256
    nb = seq // bs

    def body(x_ref, t_ref, oa_ref, or_ref, za_ref, zr_ref, gl_ref, mod_ref, gp_ref, gn_ref, wpt_ref, wout_ref,
             dout_ref, dza_ref, dzr_ref, dgl_ref, doa_ref, dor_ref, del_ref, gout_ref, gpt_ref,
             dgate_ref, dgpost_ref, dgn_ref, loss_ref, gout_acc, gpt_acc):
        i = pl.program_id(0)

        @pl.when(i == 0)
        def _():
            gout_acc[...] = jnp.zeros_like(gout_acc)
            gpt_acc[...] = jnp.zeros_like(gpt_acc)
            dgate_ref[...] = jnp.zeros_like(dgate_ref)
            dgpost_ref[...] = jnp.zeros_like(dgpost_ref)
            dgn_ref[...] = jnp.zeros_like(dgn_ref)
            loss_ref[...] = jnp.zeros_like(loss_ref)

        oa, za, zr = oa_ref[...], za_ref[...], zr_ref[...]
        sga, sgr = _sigmoid(za), _sigmoid(zr)
        sza, szr = za * sga, zr * sgr
        ya_b = _mx(oa * sza)
        ons, rstds = [], []
        for h in range(RET_HEADS):
            oh = or_ref[:, 128 * h:128 * (h + 1)]
            xc = oh - jnp.mean(oh, axis=-1, keepdims=True)
            rstd = lax.rsqrt(jnp.mean(xc * xc, axis=-1, keepdims=True) + EPS)
            ons.append(xc * rstd)
            rstds.append(rstd)
        on = jnp.concatenate(ons, axis=1)
        yn = on * gn_ref[...]
        yr_b = _mx(yn * szr)
        wpa_t, wpr_t = wpt_ref[:, 0:512], wpt_ref[:, 512:1024]
        a_att = _nt(ya_b, wpa_t)
        a_ret = _nt(yr_b, wpr_t)
        gts = _sigmoid(gl_ref[...])
        g_att, g_ret = gts[:, 0:D_MODEL], gts[:, D_MODEL:2 * D_MODEL]
        merged_b = _mx(g_att * a_att + g_ret * a_ret)
        u = _nn(merged_b, wout_ref[...])
        r = lax.rsqrt(jnp.mean(u * u, axis=-1, keepdims=True) + EPS)
        un = u * r
        gpost = gp_ref[...]
        y = un * gpost
        gate = mod_ref[:, 2 * D_MODEL:3 * D_MODEL]
        err = x_ref[...] + gate * y - t_ref[...]
        loss_ref[...] += (0.5 / D_MODEL) * _rowsum128(err * err)
        dout = err * (1.0 / D_MODEL)
        dout_ref[...] = dout
        dgate_ref[...] += jnp.sum(dout * y, axis=0, keepdims=True)
        dy = dout * gate
        dgpost_ref[...] += jnp.sum(dy * un, axis=0, keepdims=True)
        dun = dy * gpost
        du_b = _mx(r * (dun - un * jnp.mean(dun * un, axis=-1, keepdims=True)))
        dmerged = _nt(du_b, wout_ref[...])
        gout_acc[...] += _tn(merged_b, du_b)
        d_att, d_ret = dmerged * g_att, dmerged * g_ret
        dgl_ref[:, 0:D_MODEL] = (d_att * a_att * (1.0 - g_att)).astype(dgl_ref.dtype)
        dgl_ref[:, D_MODEL:2 * D_MODEL] = (d_ret * a_ret * (1.0 - g_ret)).astype(dgl_ref.dtype)
        d_att_b, d_ret_b = _mx(d_att), _mx(d_ret)
        dya = _nn(d_att_b, wpa_t)
        dyr = _nn(d_ret_b, wpr_t)
        gpt_acc[:, 0:512] += _tn(d_att_b, ya_b)
        gpt_acc[:, 512:1024] += _tn(d_ret_b, yr_b)
        dza_ref[...] = (dya * oa * (sga * (1.0 + za * (1.0 - sga)))).astype(dza_ref.dtype)
        doa = dya * sza
        doa_ref[...] = doa.astype(doa_ref.dtype)
        dt = jnp.transpose(doa * oa)
        del_ref[...] = jnp.sum(dt.reshape(8, HEAD_DIM, bs), axis=1)
        dzr_ref[...] = (dyr * yn * (sgr * (1.0 + zr * (1.0 - sgr)))).astype(dzr_ref.dtype)
        dyn = dyr * szr
        dgn_ref[...] += jnp.sum(dyn * on, axis=0, keepdims=True)
        don = dyn * gn_ref[...]
        for h in range(RET_HEADS):
            hs = slice(128 * h, 128 * (h + 1))
            dh, oh = don[:, hs], ons[h]
            doh = rstds[h] * (dh - jnp.mean(dh, axis=-1, keepdims=True) - oh * jnp.mean(dh * oh, axis=-1, keepdims=True))
            dor_ref[:, hs] = doh.astype(dor_ref.dtype)

        @pl.when(i == nb - 1)
        def _():
            gout_ref[...] = gout_acc[...].astype(gout_ref.dtype)
            gpt_ref[...] = gpt_acc[...].astype(gpt_ref.dtype)

    row = lambda w: pl.BlockSpec((bs, w), lambda i: (i, 0))
    el = lambda w, off: pl.BlockSpec((pl.Element(bs), pl.Element(w)), lambda i: (i * bs, off))
    vec = lambda w: pl.BlockSpec((1, w), lambda i: (0, 0))
    full = pl.BlockSpec((D_MODEL, D_MODEL), lambda i: (0, 0))
    sds = jax.ShapeDtypeStruct
    return pl.pallas_call(
        body, grid=(nb,), name="tail",
        in_specs=[row(D_MODEL), row(D_MODEL), row(512), row(512), el(512, ZA_B), el(512, ZR_B), el(2 * D_MODEL, GL_B),
                  vec(3 * D_MODEL), vec(D_MODEL), vec(512), full, full],
        out_specs=[row(D_MODEL), row(512), row(512), row(2 * D_MODEL), row(512), row(512),
                   pl.BlockSpec((8, bs), lambda i: (0, i)), full, full, vec(D_MODEL), vec(D_MODEL), vec(512), vec(128)],
        out_shape=[sds((seq, D_MODEL), F32), sds((seq, 512), MXU_DTYPE), sds((seq, 512), MXU_DTYPE),
                   sds((seq, 2 * D_MODEL), MXU_DTYPE), sds((seq, 512), MXU_DTYPE), sds((seq, 512), MXU_DTYPE),
                   sds((8, seq), F32), sds((D_MODEL, D_MODEL), MXU_DTYPE), sds((D_MODEL, D_MODEL), MXU_DTYPE),
                   sds((1, D_MODEL), F32), sds((1, D_MODEL), F32), sds((1, 512), F32), sds((1, 128), F32)],
        scratch_shapes=[pltpu.VMEM((D_MODEL, D_MODEL), F32), pltpu.VMEM((D_MODEL, D_MODEL), F32)],
        compiler_params=_cp("arbitrary"))(x, tgt, o_att, o_ret, p, p, p, mod, g_post, gn_g, wpt, wout)


def _assemble(p, cos, sin, qg2, kg2, dqt, dkp, dvp, dza, drq, drk, drv, dzr, dgl):
    seq = p.shape[0]
    bs = 512

    def body(p_ref, cos_ref, sin_ref, qg_ref, kg_ref, dqt_ref, dkp_ref, dvp_ref, dza_ref, drq_ref, drk_ref, drv_ref,
             dzr_ref, dgl_ref, dp_ref, dqg_ref, dkg_ref):
        @pl.when(pl.program_id(0) == 0)
        def _():
            dqg_ref[...] = jnp.zeros_like(dqg_ref)
            dkg_ref[...] = jnp.zeros_like(dkg_ref)

        m = _blockdiag64()
        cs, sn = cos_ref[...], sin_ref[...]
        lo = _lane((bs, 128)) < 64

        def norm_bwd(raw, dn, g, dg_ref):
            r = lax.rsqrt(_seg64(raw * raw, m) * (1.0 / 64) + EPS)
            xn = raw * r
            dg_ref[...] += jnp.sum(dn * xn, axis=0, keepdims=True)
            dxn = dn * g
            return r * (dxn - xn * (_seg64(dxn * xn, m) * (1.0 / 64)))

        for pr in range(4):
            sl = slice(128 * pr, 128 * (pr + 1))
            dn = _rope_t(dqt_ref[:, sl] * 0.125, cs, sn)
            dp_ref[:, QA + 128 * pr:QA + 128 * (pr + 1)] = norm_bwd(p_ref[:, sl], dn, qg_ref[...], dqg_ref).astype(dp_ref.dtype)
        fold = lambda a: a + pltpu.roll(a, 64, 1)
        dk = jnp.where(lo, fold(dkp_ref[0]), fold(dkp_ref[1]))
        dp_ref[:, KA:KA + 128] = norm_bwd(p_ref[:, KA:KA + 128], _rope_t(dk, cs, sn), kg_ref[...], dkg_ref).astype(dp_ref.dtype)
        dp_ref[:, VA:VA + 128] = jnp.where(lo, fold(dvp_ref[0]), fold(dvp_ref[1])).astype(dp_ref.dtype)
        dp_ref[:, ZA:ZA + 512] = dza_ref[...]
        for pr in range(2):
            sl = slice(128 * pr, 128 * (pr + 1))
            dp_ref[:, QR + 128 * pr:QR + 128 * (pr + 1)] = _rope_t(drq_ref[:, sl], cs, sn).astype(dp_ref.dtype)
            dp_ref[:, KR + 128 * pr:KR + 128 * (pr + 1)] = _rope_t(drk_ref[:, sl] * 0.125, cs, sn).astype(dp_ref.dtype)
        dp_ref[:, VR:VR + 512] = drv_ref[...]
        dp_ref[:, ZR:ZR + 512] = dzr_ref[...]
        dp_ref[:, GL:GL + 2 * D_MODEL] = dgl_ref[...]

    row = lambda w: pl.BlockSpec((bs, w), lambda i: (i, 0))
    gsp = pl.BlockSpec((1, 128), lambda i: (0, 0))
    dup = pl.BlockSpec((2, bs, 128), lambda i: (0, i, 0))
    return pl.pallas_call(
        body, grid=(seq // bs,), name="assemble_dp",
        in_specs=[row(768), row(128), row(128), gsp, gsp, row(512), dup, dup, row(512), row(256), row(256), row(512),
                  row(512), row(2 * D_MODEL)],
        out_specs=[row(IN_WIDTH), gsp, gsp],
        out_shape=[jax.ShapeDtypeStruct((seq, IN_WIDTH), MXU_DTYPE), jax.ShapeDtypeStruct((1, 128), F32),
                   jax.ShapeDtypeStruct((1, 128), F32)],
        compiler_params=_cp("arbitrary"))(p, cos, sin, qg2, kg2, dqt, dkp, dvp, dza, drq, drk, drv, dzr, dgl)


def _local_step(x, tgt, mod, g_pre, qn_g, kn_g, w_dec_f, w_dec_b, gn_g, g_post, w_attn, rest_start, rest_wait, send=None, own=()):
    send = send or (lambda name, arrays: None)
    seq = x.shape[0]
    cos, sin = _rope_tables(seq)
    qg2, kg2 = jnp.tile(qn_g, (1, 2)), jnp.tile(kn_g, (1, 2))
    lg_f = jax.nn.log_sigmoid(w_dec_f[0])
    lg_b = jax.nn.log_sigmoid(w_dec_b[0])
    tb = _ret_tables(lg_f, lg_b, RET_CHUNK)

    h, p_a, qt, kd, vd, *placed = _attn_inputs(x, mod, g_pre, w_attn, cos, sin, qg2, kg2, own=own)
    o_att, lse = _attn_fwd(qt, kd, vd, dep=rest_start(qt, *placed))
    win_t, wpt, wout = rest_wait(o_att)
    p_b, rq, rk, rv = _in_proj_rest(h, win_t, cos, sin)
    rf, rb = _ret_states(rk, rv, tb["wkf"], tb["wkb"], tb["dec_fb"], "ret_states_fwd")
    o_ret = _ret_fwd(rq, rk, rv, rf, rb, tb)
    (dout, dza, dzr, dgl, do_att, do_ret, delta, g_out, g_pt, dgate, dgpost, dgn, loss_l) = _tail(
        x, tgt, o_att, o_ret, p_b, mod, g_post, gn_g, wpt, wout)
    dep = send("early", (g_pt, g_out))
    dqt, dkp, dvp = _attn_bwd(qt, kd, vd, do_att, lse, delta, dep=dep)
    hb, hf = _ret_states(rq, do_ret, tb["wqb"], tb["wqf"], tb["dec_bf"], "ret_states_bwd")
    drq, drk, drv, dlg = _ret_bwd(rq, rk, rv, do_ret, rf, rb, hf, hb, tb)
    dp, dqg, dkg = _assemble(p_a, cos, sin, qg2, kg2, dqt, dkp, dvp, dza, drq, drk, drv, dzr, dgl)
    g_in_t = _matmul(dp, h, ta=True, tb=False, tm=256, tn=D_MODEL, out_dtype=MXU_DTYPE, name="in_proj_dw")
    dep = send("late", (g_in_t,))
    grad_x, dshift, dscale, dgpre = _in_proj_dx(x, dp, win_t, dout, mod, g_pre, dep=dep)

    dlg = jnp.sum(dlg, axis=1)
    small = dict(
        dmod=jnp.concatenate([dshift, dscale, dgate], axis=1),
        g_pre=dgpre, g_post=dgpost, gn_g=dgn,
        qn_g=dqg[:, :64] + dqg[:, 64:], kn_g=dkg[:, :64] + dkg[:, 64:],
        w_dec_f=(dlg[0:4] * jax.nn.sigmoid(-w_dec_f[0]))[None], w_dec_b=(dlg[4:8] * jax.nn.sigmoid(-w_dec_b[0]))[None],
        loss=jnp.sum(loss_l, axis=1, keepdims=True))
    return grad_x, g_in_t, g_pt, g_out, small


N_DEV = 8
N_CHIP = 4
HBM_SPEC = pl.BlockSpec(memory_space=pl.ANY)
VMEM_SPEC = pl.BlockSpec(memory_space=pltpu.VMEM)
SHARD_ROWS = (IN_WIDTH // N_CHIP, D_MODEL // N_CHIP, D_MODEL // N_CHIP)


def _coords():
    return lax.axis_index("x"), lax.axis_index("y"), lax.axis_index("c")


def _peer(k):
    x, y, c = _coords()
    return (1 - x if k & 4 else x, 1 - y if k & 2 else y, 1 - c if k & 1 else c)


def _dev_index(p):
    return 4 * p[0] + 2 * p[1] + p[2]


def _rows(ref, start, size):
    return ref.at[pl.ds(pl.multiple_of(start, 16), size), :]


def _remote(src, dst, ssem, rsem, dev):
    return pltpu.make_async_remote_copy(src_ref=src, dst_ref=dst, send_sem=ssem, recv_sem=rsem, device_id=dev,
                                        device_id_type=MESH)


ATTN_CHUNK = 32


def _mod_and_attn_rows(c, w_ada_s, b4, win_s):
    ncol = w_ada_s.shape[1]
    half = ATTN_COLS // 2
    offs = list(range(0, half, ATTN_CHUNK))
    nq = len(offs)
    rows = lambda ref, cc, off: ref.at[pl.ds(pl.multiple_of(cc * half + off, 16), ATTN_CHUNK), :]

    def body(c_ref, w_ref, b_ref, src, cs_ref, mod_ref, out, modsh, modall, ssem, rsem, lsem, asem, bsem, fsem, gsem, hsem):
        x, y, cc = _coords()
        me = _dev_index((x, y, cc))
        chip = me // 2
        sib = (x, y, 1 - cc)
        cs_ref[me] = c_ref[...]
        sends = []
        for k in range(1, N_DEV):
            cp = _remote(c_ref, cs_ref.at[me], ssem.at[k - 1], rsem.at[k - 1], _peer(k))
            cp.start()
            sends.append(cp)
        own = pltpu.make_async_copy(src.at[pl.ds(0, ATTN_COLS), :], out, lsem.at[0])
        bulk = [_remote(rows(src, cc, off), rows(out, cc, off), asem.at[(k2 - 1) * nq + q], bsem.at[q], (k2 // 2, k2 % 2, cc))
                for k2 in (1, 2) for q, off in enumerate(offs)]
        relay_chip = lambda q: 1 + q % 2

        @pl.when(chip == 0)
        def _():
            own.start()
            for cp in bulk:
                cp.start()

        for k in range(1, N_DEV):
            slot = cs_ref.at[_dev_index(_peer(k))]
            _remote(slot, slot, ssem.at[k - 1], rsem.at[k - 1], _peer(k)).wait_recv()
        cs = jnp.concatenate([cs_ref[d] for d in range(N_DEV)], axis=0)
        ms = _nn(cs * _sigmoid(cs), w_ref[...], precision=HIGHEST) + b_ref[pl.ds(chip, 1), :]
        modsh[...] = ms
        modall[chip] = ms
        for k2 in range(1, N_CHIP):
            cp = _remote(modsh, modall.at[chip], ssem.at[6 + k2], rsem.at[6 + k2], _peer(2 * k2))
            cp.start()
            sends.append(cp)

        @pl.when(chip != 0)
        def _():
            passed = []
            relays = [_remote(rows(out, cc, off), rows(out, cc, off), hsem.at[q], bsem.at[q], (1, 1, cc)) for q, off in enumerate(offs)]
            for q, off in enumerate(offs):
                got = rows(out, cc, off)
                _remote(got, got, asem.at[q], bsem.at[q], (0, 0, cc)).wait_recv()
                cp = _remote(got, got, fsem.at[q], gsem.at[q], sib)
                cp.start()
                passed.append(cp)
                pl.when(chip == relay_chip(q))(relays[q].start)
            for q, off in enumerate(offs):
                got = rows(out, 1 - cc, off)
                _remote(got, got, fsem.at[q], gsem.at[q], sib).wait_recv()
            for cp in passed:
                cp.wait_send()
            for q in range(nq):
                pl.when(chip == relay_chip(q))(relays[q].wait_send)

        for k2 in range(1, N_CHIP):
            slot = modall.at[_dev_index(_peer(2 * k2)) // 2]
            _remote(slot, slot, ssem.at[6 + k2], rsem.at[6 + k2], _peer(2 * k2)).wait_recv()
        for jj in range(N_CHIP):
            mod_ref[:, ncol * jj:ncol * (jj + 1)] = modall[jj, pl.ds(me, 1), :]
        for cp in sends:
            cp.wait_send()

        @pl.when(chip == 0)
        def _():
            for cp in bulk:
                cp.wait_send()
            own.wait()

    dma = pltpu.SemaphoreType.DMA
    return pl.pallas_call(
        body, name="mod_and_attn_rows", in_specs=[VMEM_SPEC] * 4, out_specs=[VMEM_SPEC, VMEM_SPEC, HBM_SPEC],
        out_shape=[jax.ShapeDtypeStruct((N_DEV, 1, D_MODEL), F32), jax.ShapeDtypeStruct((1, N_CHIP * ncol), F32),
                   jax.ShapeDtypeStruct((ATTN_COLS, win_s.shape[1]), win_s.dtype)],
        scratch_shapes=[pltpu.VMEM((N_DEV, ncol), F32), pltpu.VMEM((N_CHIP, N_DEV, ncol), F32), dma((10,)), dma((10,)),
                        dma((1,)), dma((2 * nq,)), dma((nq,)), dma((nq,)), dma((nq,)), dma((nq,))],
        compiler_params=_cp())(c, w_ada_s, b4, win_s)


GATHER_CHUNK = 32


def _chunks(kinds, chunk):
    out = []
    for t, kind in enumerate(kinds):
        half = SHARD_ROWS[kind] // 2
        step = chunk if half % chunk == 0 else half
        out += [(t, off, step) for off in range(0, half, step)]
    return out


SEM_SPEC = pl.BlockSpec(memory_space=pltpu.SEMAPHORE)
HBM_ONLY = pl.BlockSpec(memory_space=pltpu.HBM)
DATAFLOW = pltpu.SideEffectType.DATAFLOW_SIDE_EFFECTING


def _gather_rest_start(shards, zones, dep):
    n = len(shards)

    def body(*refs):
        srcs, lands = refs[:n], refs[n:2 * n]
        ssem, rsem = refs[2 * n + 1], refs[2 * n + 2]
        token = refs[-1]
        x, y, _ = _coords()
        chip = 2 * x + y
        for k2 in range(1, N_CHIP):
            for t in range(n):
                q = (k2 - 1) * n + t
                _remote(srcs[t], _rows(lands[t], chip * SHARD_ROWS[t], SHARD_ROWS[t]), ssem.at[q], rsem.at[q], _peer(2 * k2)).start()
        token[...] = jnp.zeros_like(token)

    hbm = lambda a: pltpu.with_memory_space_constraint(a, pltpu.HBM)
    dma = pltpu.SemaphoreType.DMA
    outs = pl.pallas_call(
        body, name="gather_rest", in_specs=[HBM_ONLY] * (2 * n) + [HBM_SPEC],
        out_specs=[SEM_SPEC, SEM_SPEC] + [HBM_ONLY] * (2 * n) + [VMEM_SPEC],
        out_shape=[dma((3 * n,)), dma((3 * n,))] + [pltpu.HBM(a.shape, a.dtype) for a in list(shards) + list(zones)]
        + [jax.ShapeDtypeStruct((8, 128), F32)],
        input_output_aliases={i: 2 + i for i in range(2 * n)},
        compiler_params=pltpu.CompilerParams(has_side_effects=DATAFLOW))(*[hbm(a) for a in list(shards) + list(zones)], dep)
    return outs[0], outs[1], outs[2:2 + n], outs[2 + n:2 + 2 * n], outs[-1]


def _gather_rest_wait(started, after):
    ssem, rsem, shards, zones, _ = started
    n = len(shards)

    def body(*refs):
        srcs, lands = refs[:n], refs[n:2 * n]
        ssem_ref, rsem_ref = refs[2 * n], refs[2 * n + 1]
        for k2 in range(1, N_CHIP):
            pchip = _dev_index(_peer(2 * k2)) // 2
            for t in range(n):
                q = (k2 - 1) * n + t
                cp = _remote(srcs[t], _rows(lands[t], pchip * SHARD_ROWS[t], SHARD_ROWS[t]), ssem_ref.at[q], rsem_ref.at[q], _peer(2 * k2))
                cp.wait_send()
                cp.wait_recv()

    outs = pl.pallas_call(
        body, name="gather_rest_wait", in_specs=[HBM_ONLY] * (2 * n) + [SEM_SPEC, SEM_SPEC, HBM_SPEC], out_specs=[HBM_ONLY] * (2 * n),
        out_shape=[pltpu.HBM(a.shape, a.dtype) for a in list(shards) + list(zones)],
        input_output_aliases={i: i for i in range(2 * n)},
        compiler_params=pltpu.CompilerParams(has_side_effects=DATAFLOW))(*shards, *zones, ssem, rsem, after)
    return outs[n:]


def _reduced_by(ref, t, dev):
    return _rows(ref, (dev // 2) * SHARD_ROWS[t] + (dev % 2) * (SHARD_ROWS[t] // 2), SHARD_ROWS[t] // 2)


def _scatter_start(grads, kinds, name):
    n = len(grads)

    def body(*refs):
        srcs, lands = refs[:n], refs[n:2 * n]
        ssem, rsem = refs[2 * n], refs[2 * n + 1]
        token = refs[-1]
        me = _dev_index(_coords())
        for k in range(1, N_DEV):
            for i, t in enumerate(kinds):
                q = (k - 1) * n + i
                _remote(_reduced_by(srcs[i], t, _dev_index(_peer(k))), lands[i].at[me], ssem.at[q], rsem.at[q], _peer(k)).start()
        token[...] = jnp.zeros_like(token)

    zones = [lax.empty((N_DEV, SHARD_ROWS[t] // 2, g.shape[1]), g.dtype) for g, t in zip(grads, kinds)]
    hbm = lambda a: pltpu.with_memory_space_constraint(a, pltpu.HBM)
    dma = pltpu.SemaphoreType.DMA
    outs = pl.pallas_call(
        body, name=name, in_specs=[HBM_ONLY] * (2 * n), out_specs=[SEM_SPEC, SEM_SPEC] + [HBM_ONLY] * (2 * n) + [VMEM_SPEC],
        out_shape=[dma((7 * n,)), dma((7 * n,))] + [pltpu.HBM(a.shape, a.dtype) for a in list(grads) + zones]
        + [jax.ShapeDtypeStruct((8, 128), F32)],
        input_output_aliases={i: 2 + i for i in range(2 * n)},
        compiler_params=pltpu.CompilerParams(has_side_effects=DATAFLOW))(*[hbm(a) for a in list(grads) + zones])
    return outs[0], outs[1], outs[2:2 + n], outs[2 + n:2 + 2 * n], outs[-1]


def _scatter_wait(started, kinds, after, name):
    ssem, rsem, grads, zones, _ = started
    n = len(grads)

    def body(*refs):
        srcs, lands = refs[:n], refs[n:2 * n]
        ssem_ref, rsem_ref = refs[2 * n], refs[2 * n + 1]
        for k in range(1, N_DEV):
            peer = _dev_index(_peer(k))
            for i, t in enumerate(kinds):
                q = (k - 1) * n + i
                cp = _remote(_reduced_by(srcs[i], t, peer), lands[i].at[peer], ssem_ref.at[q], rsem_ref.at[q], _peer(k))
                cp.wait_send()
                cp.wait_recv()

    outs = pl.pallas_call(
        body, name=name, in_specs=[HBM_ONLY] * (2 * n) + [SEM_SPEC, SEM_SPEC, HBM_SPEC], out_specs=[HBM_ONLY] * (2 * n),
        out_shape=[pltpu.HBM(a.shape, a.dtype) for a in list(grads) + list(zones)],
        input_output_aliases={i: i for i in range(2 * n)},
        compiler_params=pltpu.CompilerParams(has_side_effects=DATAFLOW))(*grads, *zones, ssem, rsem, after)
    return outs[:n], outs[n:]


GRAD_SEGMENTS = 4


def _grad_finish(grads, zones, kinds, name):
    nt = len(grads)
    pieces = _chunks(kinds, GATHER_CHUNK)
    npc = len(pieces)
    shard = [SHARD_ROWS[k] for k in kinds]
    nseg = GRAD_SEGMENTS
    count = [sum(1 for p in pieces if p[0] == t) for t in range(nt)]
    assert min(count) >= nseg
    cut = [[next(p[2] for p in pieces if p[0] == t) * -(-count[t] * s // nseg) for s in range(nseg + 1)] for t in range(nt)]

    def body(*refs):
        srcs, lands, outs = refs[:nt], refs[nt:2 * nt], refs[2 * nt:3 * nt]
        slots, sums = refs[3 * nt:4 * nt], refs[4 * nt:5 * nt]
        lsem, osem, xsem, ysem = refs[5 * nt:]
        x, y, c = _coords()
        me = _dev_index((x, y, c))
        sib = (x, y, 1 - c)
        loads = []
        for s in range(nseg):
            part = []
            for t in range(nt):
                rows = pl.ds(cut[t][s], cut[t][s + 1] - cut[t][s])
                part.append(pltpu.make_async_copy(_reduced_by(srcs[t], kinds[t], me).at[rows, :], slots[t].at[me, rows, :],
                                                  lsem.at[s * N_DEV * nt + t]))
                for k in range(1, N_DEV):
                    peer = _dev_index(_peer(k))
                    part.append(pltpu.make_async_copy(lands[t].at[peer, rows, :], slots[t].at[peer, rows, :],
                                                      lsem.at[(s * N_DEV + k) * nt + t]))
            for cp in part:
                cp.start()
            loads.append(part)
        sends = []
        place = lambda t, cc, off, n: _rows(outs[t], cc * (shard[t] // 2) + off, n)
        stores = []
        for s in range(nseg):
            for cp in loads[s]:
                cp.wait()
            for q, (t, off, n) in enumerate(pieces):
                if not cut[t][s] <= off < cut[t][s + 1]:
                    continue
                r = pl.ds(off, n)
                acc = slots[t][0, r, :].astype(F32)
                for d in range(1, N_DEV):
                    acc = acc + slots[t][d, r, :].astype(F32)
                sums[t][r, :] = acc
                keep = pltpu.make_async_copy(sums[t].at[r, :], place(t, c, off, n), osem.at[q])
                give = _remote(sums[t].at[r, :], place(t, c, off, n), xsem.at[q], ysem.at[q], sib)
                keep.start()
                give.start()
                stores.append(keep)
                sends.append(give)
        for q, (t, off, n) in enumerate(pieces):
            got = place(t, 1 - c, off, n)
            _remote(got, got, xsem.at[q], ysem.at[q], sib).wait_recv()
        for cp in sends:
            cp.wait_send()
        for cp in stores:
            cp.wait()

    dma = pltpu.SemaphoreType.DMA
    return pl.pallas_call(
        body, name=name, in_specs=[HBM_SPEC] * (2 * nt), out_specs=[HBM_SPEC] * nt,
        out_shape=[jax.ShapeDtypeStruct((shard[t], g.shape[1]), F32) for t, g in enumerate(grads)],
        scratch_shapes=([pltpu.VMEM((N_DEV, shard[t] // 2, g.shape[1]), g.dtype) for t, g in enumerate(grads)]
                        + [pltpu.VMEM((shard[t] // 2, g.shape[1]), F32) for t, g in enumerate(grads)]
                        + [dma((GRAD_SEGMENTS * N_DEV * nt,)), dma((npc,)), dma((npc,)), dma((npc,))]),
        compiler_params=_cp())(*grads, *zones)


def _adam_math(w, g, m, v):
    m = ADAM_B1 * m + (1.0 - ADAM_B1) * g
    v = ADAM_B2 * v + (1.0 - ADAM_B2) * (g * g)
    m_hat = m / (1.0 - ADAM_B1 ** ADAM_STEP)
    v_hat = v / (1.0 - ADAM_B2 ** ADAM_STEP)
    return -ADAM_LR * (m_hat / (jnp.sqrt(v_hat) + ADAM_EPS) + ADAM_WD * w), m, v


def _adamw(items, name, nblk=4):
    n = len(items)

    def body(*refs):
        ins, outs = refs[:4 * n], refs[4 * n:]
        for t in range(n):
            w_ref, g_ref, m_ref, v_ref = ins[4 * t:4 * t + 4]
            d_ref, nm_ref, nv_ref = outs[3 * t:3 * t + 3]
            d_ref[...], nm_ref[...], nv_ref[...] = _adam_math(w_ref[...], g_ref[...], m_ref[...], v_ref[...])

    specs = [pl.BlockSpec((it[0].shape[0] // nblk, it[0].shape[1]), lambda i: (i, 0)) for it in items]
    outs = pl.pallas_call(
        body, grid=(nblk,), name=name, in_specs=[s for s in specs for _ in range(4)], out_specs=[s for s in specs for _ in range(3)],
        out_shape=[jax.ShapeDtypeStruct(it[0].shape, F32) for it in items for _ in range(3)],
        compiler_params=_cp("parallel"))(*[a for it in items for a in it])
    return [tuple(outs[3 * t:3 * t + 3]) for t in range(n)]


PACK = (("b_ada", 3 * D_MODEL), ("g_pre", D_MODEL), ("g_post", D_MODEL), ("gn_g", 512), ("qn_g", 64), ("kn_g", 64),
        ("w_dec_f", 4), ("w_dec_b", 4), ("loss", 1))
PACK_OFFSETS = {}
PACK_WIDTH = 0
for _name, _n in PACK:
    PACK_OFFSETS[_name] = PACK_WIDTH
    PACK_WIDTH += -(-_n // 128) * 128
SMALL_PARAMS = tuple(name for name, _ in PACK if name != "loss")


def _pack(parts):
    cols = []
    for name, n in PACK:
        pad = -(-n // 128) * 128 - n
        cols.append(parts[name].reshape(1, n).astype(F32))
        if pad:
            cols.append(jnp.zeros((1, pad), F32))
    return jnp.concatenate(cols, axis=1)


def _small_reduce(vec, cs, deps):
    ncol = 3 * D_MODEL // N_CHIP

    def body(vec_ref, cs_ref, *rest):
        tot_ref, gwa_ref, gat, ssem, rsem = rest[-5:]
        me = _dev_index(_coords())
        chip = me // 2
        gat[me] = vec_ref[...]
        sends = []
        for k in range(1, N_DEV):
            cp = _remote(vec_ref, gat.at[me], ssem.at[k - 1], rsem.at[k - 1], _peer(k))
            cp.start()
            sends.append(cp)
        for k in range(1, N_DEV):
            slot = gat.at[_dev_index(_peer(k))]
            _remote(slot, slot, ssem.at[k - 1], rsem.at[k - 1], _peer(k)).wait_recv()
        rows = [gat[d] for d in range(N_DEV)]
        tot = rows[0]
        for d in range(1, N_DEV):
            tot = tot + rows[d]
        tot_ref[...] = tot
        cs = cs_ref[...]
        ca_t = jnp.transpose(jnp.concatenate([cs * _sigmoid(cs), jnp.zeros((128 - N_DEV, D_MODEL), F32)], axis=0))
        dm = jnp.concatenate(rows + [jnp.zeros((128 - N_DEV, PACK_WIDTH), F32)], axis=0)
        for jj in range(N_CHIP):
            @pl.when(chip == jj)
            def _():
                gwa_ref[...] = _nn(ca_t, dm[:, ncol * jj:ncol * (jj + 1)], precision=HIGHEST)
        for cp in sends:
            cp.wait_send()

    return pl.pallas_call(
        body, name="small_reduce", in_specs=[VMEM_SPEC, VMEM_SPEC] + [HBM_SPEC] * len(deps), out_specs=[VMEM_SPEC] * 2,
        out_shape=[jax.ShapeDtypeStruct((1, PACK_WIDTH), F32), jax.ShapeDtypeStruct((D_MODEL, ncol), F32)],
        scratch_shapes=[pltpu.VMEM((N_DEV, 1, PACK_WIDTH), F32), pltpu.SemaphoreType.DMA((7,)), pltpu.SemaphoreType.DMA((7,))],
        compiler_params=_cp())(vec, cs, *deps)


def _small_adamw(tot, given):
    sizes = dict(PACK)
    np_ = len(SMALL_PARAMS)

    def body(*refs):
        tot_ref = refs[0]
        wmv = refs[1:1 + 3 * np_]
        outs = refs[1 + 3 * np_:1 + 7 * np_]
        loss_ref = refs[-1]
        for i, name in enumerate(SMALL_PARAMS):
            off, n = PACK_OFFSETS[name], sizes[name]
            g = tot_ref[:, off:off + n]
            w_ref, m_ref, v_ref = wmv[3 * i:3 * i + 3]
            outs[i][...] = g
            outs[np_ + i][...], outs[2 * np_ + i][...], outs[3 * np_ + i][...] = _adam_math(w_ref[...], g, m_ref[...], v_ref[...])
        loss_ref[...] = tot_ref[:, PACK_OFFSETS["loss"]:PACK_OFFSETS["loss"] + 1]

    flat = [a for name in SMALL_PARAMS for a in given[name]]
    shapes = [jax.ShapeDtypeStruct((1, sizes[name]), F32) for name in SMALL_PARAMS]
    res = pl.pallas_call(body, name="small_adamw", in_specs=[VMEM_SPEC] * (1 + 3 * np_), out_specs=[VMEM_SPEC] * (4 * np_ + 1),
                         out_shape=shapes * 4 + [jax.ShapeDtypeStruct((1, 1), F32)], compiler_params=_cp())(tot, *flat)
    return [dict(zip(SMALL_PARAMS, res[k * np_:(k + 1) * np_])) for k in range(4)], res[-1]


def kernel(x, c, w_ada, b_ada, g_pre, w_in, qn_g, kn_g, w_dec_f, w_dec_b, gn_g, w_pa, w_pr, w_out, g_post, loss_target, m_w_ada, m_b_ada, m_g_pre, m_w_in, m_qn_g, m_kn_g, m_w_dec_f, m_w_dec_b, m_gn_g, m_w_pa, m_w_pr, m_w_out, m_g_post, v_w_ada, v_b_ada, v_g_pre, v_w_in, v_qn_g, v_kn_g, v_w_dec_f, v_w_dec_b, v_gn_g, v_w_pa, v_w_pr, v_w_out, v_g_post):
    shards = (w_in[0].T.astype(MXU_DTYPE), jnp.concatenate([w_pa[0].T, w_pr[0].T], axis=1).astype(MXU_DTYPE),
              w_out[0].astype(MXU_DTYPE))
    cs, mod, w_attn = _mod_and_attn_rows(c, w_ada[0], b_ada.reshape(N_CHIP, 3 * D_MODEL // N_CHIP), shards[0])
    started = {}

    def rest_start(dep, *placed):
        started["rest"] = _gather_rest_start(shards, placed, dep)
        return started["rest"][-1]

    def rest_wait(after):
        return _gather_rest_wait(started["rest"], after)

    kinds = {"early": (1, 2), "late": (0,)}

    def send(name, arrays):
        started[name] = _scatter_start(arrays, kinds[name], "grad_scatter_" + name)
        return started[name][-1]

    grad_x, _, _, _, small = _local_step(x[0], loss_target[0], mod, g_pre, qn_g, kn_g, w_dec_f, w_dec_b,
                                         gn_g, g_post, w_attn, rest_start, rest_wait, send=send, own=shards)
    small = dict(small)
    small["b_ada"] = small.pop("dmod")
    given = dict(b_ada=(b_ada, m_b_ada, v_b_ada), g_pre=(g_pre, m_g_pre, v_g_pre), g_post=(g_post, m_g_post, v_g_post),
                 gn_g=(gn_g, m_gn_g, v_gn_g), qn_g=(qn_g, m_qn_g, v_qn_g), kn_g=(kn_g, m_kn_g, v_kn_g),
                 w_dec_f=(w_dec_f, m_w_dec_f, v_w_dec_f), w_dec_b=(w_dec_b, m_w_dec_b, v_w_dec_b))
    grads, deltas, new_m, new_v = {}, {}, {}, {}

    def update(call, back, **items):
        done = _adamw(list(items.values()), "adamw_" + call)
        for (name, (w, g, m, v)), (d, nm, nv) in zip(items.items(), done):
            grads[name], deltas[name], new_m[name], new_v[name] = back(g), back(d), back(nm), back(nv)
        return tuple(d for d, _, _ in done)

    lead = lambda a: a[None]
    (g_pt, g_out), (z_pt, z_out) = _scatter_wait(started["early"], kinds["early"], grad_x, "grad_scatter_early_wait")
    r_pt, r_out = _grad_finish((g_pt, g_out), (z_pt, z_out), kinds["early"], "grad_finish_early")
    early = update("early", lead, w_pa=(w_pa[0], r_pt[:, :512].T, m_w_pa[0], v_w_pa[0]),
                   w_pr=(w_pr[0], r_pt[:, 512:].T, m_w_pr[0], v_w_pr[0]), w_out=(w_out[0], r_out, m_w_out[0], v_w_out[0]))
    tot, g_w_ada = _small_reduce(_pack(small), cs.reshape(N_DEV, D_MODEL), early)
    small_parts, loss = _small_adamw(tot, given)
    for dst, part in zip((grads, deltas, new_m, new_v), small_parts):
        dst.update(part)
    (last,) = update("w_ada", lead, w_ada=(w_ada[0], g_w_ada, m_w_ada[0], v_w_ada[0]))

    (g_in_t,), (z_in,) = _scatter_wait(started["late"], kinds["late"], last, "grad_scatter_late_wait")
    (r_in,) = _grad_finish((g_in_t,), (z_in,), kinds["late"], "grad_finish_late")
    tr = lambda a: a[0].T
    update("w_in", lambda a: a.T[None], w_in=(tr(w_in), r_in, tr(m_w_in), tr(v_w_in)))
    order = ("w_ada", "b_ada", "g_pre", "w_in", "qn_g", "kn_g", "w_dec_f", "w_dec_b", "gn_g", "w_pa", "w_pr", "w_out", "g_post")
    return (loss[0, 0], grad_x[None], *[grads[n] for n in order], *[deltas[n] for n in order],
            *[new_m[n] for n in order], *[new_v[n] for n in order])
```

```python
import jax
import jax.numpy as jnp
import numpy as np
from jax import lax
from jax.experimental import pallas as pl
from jax.experimental.pallas import tpu as pltpu

F32 = jnp.float32
BF16 = jnp.bfloat16
MXU_DTYPE = jnp.bfloat16

D_MODEL = 1024
GRID_W = 64
HEAD_DIM = 64
ROPE_THETA = 10000.0
EPS = 1e-6
RET_HEADS = 4
RET_CHUNK = 256
RET_GROUP = 4
IN_WIDTH = 4864
QA, KA, VA, ZA, QR, KR, VR, ZR, GL = 0, 512, 640, 768, 1280, 1536, 1792, 2304, 2816
ATTN_COLS = ZA
ZA_B, QR_B, KR_B, VR_B, ZR_B, GL_B = (c - ATTN_COLS for c in (ZA, QR, KR, VR, ZR, GL))

ADAM_LR, ADAM_B1, ADAM_B2, ADAM_EPS, ADAM_WD, ADAM_STEP = 0.001, 0.9, 0.999, 1e-08, 0.01, 10

VMEM_LIMIT = 56 * 1024 * 1024
MESH = pl.DeviceIdType.MESH
HIGHEST = lax.Precision.HIGHEST
LOG2E = 1.4426950408889634
LN2 = 0.6931471805599453


def _cp(*sem, **kw):
    if sem:
        kw["dimension_semantics"] = sem
    return pltpu.CompilerParams(vmem_limit_bytes=VMEM_LIMIT, **kw)


def _nn(a, b, **kw):
    return lax.dot_general(a, b, (((1,), (0,)), ((), ())), preferred_element_type=F32, **kw)


def _nt(a, b):
    return lax.dot_general(a, b, (((1,), (1,)), ((), ())), preferred_element_type=F32)


def _tn(a, b):
    return lax.dot_general(a, b, (((0,), (0,)), ((), ())), preferred_element_type=F32)


def _mx(x):
    return x.astype(MXU_DTYPE)


def _lane(shape):
    return lax.broadcasted_iota(jnp.int32, shape, 1)


def _sigmoid(z):
    return 1.0 / (1.0 + jnp.exp(-z))


def _rowsum128(x):
    s = jnp.sum(x, axis=0, keepdims=True)
    out = s[:, 0:128]
    for k in range(1, x.shape[1] // 128):
        out = out + s[:, 128 * k:128 * (k + 1)]
    return out


def _swap16(x):
    return jnp.where((_lane(x.shape) & 16) == 0, pltpu.roll(x, 112, 1), pltpu.roll(x, 16, 1))


def _rope(x, cos, sin):
    return x * cos + _swap16(x) * sin


def _rope_t(d, cos, sin):
    return d * cos + _swap16(d * sin)


def _blockdiag64():
    r = lax.broadcasted_iota(jnp.int32, (128, 128), 0) // 64
    c = lax.broadcasted_iota(jnp.int32, (128, 128), 1) // 64
    return (r == c).astype(BF16)


def _seg64(x, m):
    hi = x.astype(BF16)
    lo = (x - hi.astype(F32)).astype(BF16)
    return _nn(hi, m) + _nn(lo, m)


def _rope_tables(seq):
    t = np.arange(seq)
    row = (t // GRID_W).astype(np.float32)
    col = (t % GRID_W).astype(np.float32)
    half = HEAD_DIM // 2
    inv_freq = (np.float32(ROPE_THETA) ** (-np.arange(0, half, 2, dtype=np.float32) / np.float32(half))).astype(np.float32)
    ar = (row[:, None] * inv_freq[None, :]).astype(np.float32).astype(np.float64)
    ac = (col[:, None] * inv_freq[None, :]).astype(np.float32).astype(np.float64)
    cr, sr, cc, sc = np.cos(ar), np.sin(ar), np.cos(ac), np.sin(ac)
    cos64 = np.concatenate([cr, cr, cc, cc], axis=1)
    sin64 = np.concatenate([-sr, sr, -sc, sc], axis=1)
    return jnp.asarray(np.tile(cos64, (1, 2)), F32), jnp.asarray(np.tile(sin64, (1, 2)), F32)


def _attn_inputs(x, mod, g_pre, w_attn, cos, sin, qg2, kg2, own=()):
    seq = x.shape[0]
    bs = 512
    nt = len(own)
    nstep = seq // bs

    def body(x_ref, mod_ref, g_ref, w_ref, cos_ref, sin_ref, qg_ref, kg_ref, *rest):
        srcs, (h_ref, pa_ref, qt_ref, kd_ref, vd_ref) = rest[:nt], rest[nt:nt + 5]
        lands, stage = rest[nt + 5:2 * nt + 5], rest[2 * nt + 5:3 * nt + 5]
        if nt:
            osem = rest[3 * nt + 5]
            step = pl.program_id(0)
            cx, cy, _ = _coords()
            chip = 2 * cx + cy
            loads = [pltpu.make_async_copy(srcs[t], stage[t], osem.at[t]) for t in range(nt)]
            stores = [pltpu.make_async_copy(stage[t], _rows(lands[t], chip * SHARD_ROWS[t], SHARD_ROWS[t]), osem.at[nt + t])
                      for t in range(nt)]

            @pl.when(step == 0)
            def _():
                for cp in loads:
                    cp.start()

            @pl.when(step == min(1, nstep - 1))
            def _():
                for cp in loads:
                    cp.wait()
                for cp in stores:
                    cp.start()

        xv = x_ref[...]
        r = lax.rsqrt(jnp.mean(xv * xv, axis=-1, keepdims=True) + EPS)
        shift = mod_ref[:, 0:D_MODEL]
        scale = mod_ref[:, D_MODEL:2 * D_MODEL]
        hb = ((xv * r) * g_ref[...] * (1.0 + scale) + shift).astype(h_ref.dtype)
        h_ref[...] = hb
        p = _nt(hb, w_ref[...])
        pa_ref[...] = p
        m = _blockdiag64()
        cs, sn = cos_ref[...], sin_ref[...]
        lo = _lane((bs, 128)) < 64
        for pr in range(4):
            q = p[:, QA + 128 * pr:QA + 128 * (pr + 1)]
            r = lax.rsqrt(_seg64(q * q, m) * (1.0 / 64) + EPS)
            qt_ref[:, 128 * pr:128 * (pr + 1)] = (_rope(q * r * qg_ref[...], cs, sn) * (0.125 * LOG2E)).astype(qt_ref.dtype)
        k = p[:, KA:KA + 128]
        r = lax.rsqrt(_seg64(k * k, m) * (1.0 / 64) + EPS)
        kr = _rope(k * r * kg_ref[...], cs, sn)
        ksw = pltpu.roll(kr, 64, 1)
        kd_ref[0] = jnp.where(lo, kr, ksw).astype(kd_ref.dtype)
        kd_ref[1] = jnp.where(lo, ksw, kr).astype(kd_ref.dtype)
        v = p[:, VA:VA + 128]
        vsw = pltpu.roll(v, 64, 1)
        vd_ref[0] = jnp.where(lo, v, vsw).astype(vd_ref.dtype)
        vd_ref[1] = jnp.where(lo, vsw, v).astype(vd_ref.dtype)

        if nt:
            @pl.when(step == nstep - 1)
            def _():
                for cp in stores:
                    cp.wait()

    row = lambda w: pl.BlockSpec((bs, w), lambda i: (i, 0))
    fix = lambda a, b: pl.BlockSpec((a, b), lambda i: (0, 0))
    dup = pl.BlockSpec((2, bs, 128), lambda i: (0, i, 0))
    sds = jax.ShapeDtypeStruct
    anywhere = pl.BlockSpec(memory_space=pl.ANY)
    return pl.pallas_call(
        body, grid=(nstep,), name="attn_inputs",
        in_specs=[row(D_MODEL), fix(1, 3 * D_MODEL), fix(1, D_MODEL), fix(ATTN_COLS, D_MODEL), row(128), row(128), fix(1, 128), fix(1, 128)]
        + [anywhere] * nt,
        out_specs=[row(D_MODEL), row(ATTN_COLS), row(512), dup, dup] + [anywhere] * nt,
        out_shape=[sds((seq, D_MODEL), MXU_DTYPE), sds((seq, ATTN_COLS), F32), sds((seq, 512), MXU_DTYPE),
                   sds((2, seq, 128), MXU_DTYPE), sds((2, seq, 128), MXU_DTYPE)]
        + [sds((N_CHIP * SHARD_ROWS[t], s.shape[1]), s.dtype) for t, s in enumerate(own)],
        scratch_shapes=[pltpu.VMEM(s.shape, s.dtype) for s in own] + ([pltpu.SemaphoreType.DMA((2 * nt,))] if nt else []),
        compiler_params=_cp("arbitrary"))(x, mod, g_pre, w_attn, cos, sin, qg2, kg2, *own)


def _in_proj_dx(x, dp, win_t, dout, mod, g_pre, dep=None):
    seq = x.shape[0]
    bs = 512
    deps, dep_specs = _dep_args(dep, 1)

    def body(x_ref, dp_ref, w_ref, dout_ref, mod_ref, g_ref, *rest):
        gx_ref, dshift_ref, dscale_ref, dg_ref = rest[-4:]

        @pl.when(pl.program_id(0) == 0)
        def _():
            dshift_ref[...] = jnp.zeros_like(dshift_ref)
            dscale_ref[...] = jnp.zeros_like(dscale_ref)
            dg_ref[...] = jnp.zeros_like(dg_ref)

        xv = x_ref[...]
        dh = _nn(dp_ref[...], w_ref[...])
        g = g_ref[...]
        r = lax.rsqrt(jnp.mean(xv * xv, axis=-1, keepdims=True) + EPS)
        xn = xv * r
        scale = mod_ref[:, D_MODEL:2 * D_MODEL]
        dshift_ref[...] += jnp.sum(dh, axis=0, keepdims=True)
        dscale_ref[...] += jnp.sum(dh * (xn * g), axis=0, keepdims=True)
        da = dh * (1.0 + scale)
        dg_ref[...] += jnp.sum(da * xn, axis=0, keepdims=True)
        dxn = da * g
        dx = r * (dxn - xn * jnp.mean(dxn * xn, axis=-1, keepdims=True))
        gx_ref[...] = dout_ref[...] + dx

    row = pl.BlockSpec((bs, D_MODEL), lambda i: (i, 0))
    vec = pl.BlockSpec((1, D_MODEL), lambda i: (0, 0))
    return pl.pallas_call(
        body, grid=(seq // bs,), name="in_proj_dx",
        in_specs=[row, pl.BlockSpec((bs, IN_WIDTH), lambda i: (i, 0)), pl.BlockSpec((IN_WIDTH, D_MODEL), lambda i: (0, 0)), row,
                  pl.BlockSpec((1, 3 * D_MODEL), lambda i: (0, 0)), vec] + dep_specs,
        out_specs=[row, vec, vec, vec],
        out_shape=[jax.ShapeDtypeStruct((seq, D_MODEL), F32)] + [jax.ShapeDtypeStruct((1, D_MODEL), F32)] * 3,
        compiler_params=_cp("arbitrary"))(x, dp, win_t, dout, mod, g_pre, *deps)


def _dep_args(dep, grid_rank):
    if dep is None:
        return [], []
    return [dep], [pl.BlockSpec(dep.shape, lambda *_: (0,) * dep.ndim)]


def _matmul(a, b, *, ta, tb, tm, tn, out_dtype, name, dep=None):
    kdim = a.shape[0] if ta else a.shape[1]
    m = a.shape[1] if ta else a.shape[0]
    n = b.shape[0] if tb else b.shape[1]
    assert m % tm == 0 and n % tn == 0
    deps, dep_specs = _dep_args(dep, 2)

    def body(a_ref, b_ref, *rest):
        o_ref = rest[-1]
        dims = (((0 if ta else 1,), (1 if tb else 0,)), ((), ()))
        o_ref[...] = lax.dot_general(a_ref[...], b_ref[...], dims, preferred_element_type=F32).astype(o_ref.dtype)

    a_spec = pl.BlockSpec((kdim, tm), lambda j, i: (0, i)) if ta else pl.BlockSpec((tm, kdim), lambda j, i: (i, 0))
    b_spec = pl.BlockSpec((tn, kdim), lambda j, i: (j, 0)) if tb else pl.BlockSpec((kdim, tn), lambda j, i: (0, j))
    return pl.pallas_call(
        body, grid=(n // tn, m // tm), name=name, in_specs=[a_spec, b_spec] + dep_specs,
        out_specs=pl.BlockSpec((tm, tn), lambda j, i: (i, j)),
        out_shape=jax.ShapeDtypeStruct((m, n), out_dtype), compiler_params=_cp("parallel", "parallel"))(a, b, *deps)


def _in_proj_rest(h, win_t, cos, sin):
    seq = h.shape[0]
    tm = min(1024, seq)
    ncols = IN_WIDTH - ATTN_COLS
    tn = ncols // 2
    assert ZR_B <= tn and ATTN_COLS % 128 == 0 and tn % 128 == 0

    def body(h_ref, w_ref, cos_ref, sin_ref, p_ref, rq_ref, rk_ref, rv_ref):
        p = _nt(h_ref[...], w_ref[...])
        p_ref[...] = p

        @pl.when(pl.program_id(0) == 0)
        def _():
            cs, sn = cos_ref[...], sin_ref[...]
            for pr in range(2):
                sl = slice(128 * pr, 128 * (pr + 1))
                rq_ref[:, sl] = _rope(p[:, QR_B + 128 * pr:QR_B + 128 * (pr + 1)], cs, sn).astype(rq_ref.dtype)
                rk_ref[:, sl] = (_rope(p[:, KR_B + 128 * pr:KR_B + 128 * (pr + 1)], cs, sn) * 0.125).astype(rk_ref.dtype)
            rv_ref[...] = p[:, VR_B:VR_B + 512].astype(rv_ref.dtype)

    nrow = seq // tm
    row = lambda w: pl.BlockSpec((tm, w), lambda j, i: (i, 0))
    park = lambda w: pl.BlockSpec((tm, w), lambda j, i: (jnp.where(j == 0, i, nrow - 1), 0))
    sds = jax.ShapeDtypeStruct
    return pl.pallas_call(
        body, grid=(2, nrow), name="in_proj_rest",
        in_specs=[row(D_MODEL), pl.BlockSpec((pl.Element(tn), pl.Element(D_MODEL)),
                                             lambda j, i: (pl.multiple_of(ATTN_COLS + j * tn, 128), 0)), row(128), row(128)],
        out_specs=[pl.BlockSpec((tm, tn), lambda j, i: (i, j)), park(256), park(256), park(512)],
        out_shape=[sds((seq, ncols), F32), sds((seq, 256), MXU_DTYPE), sds((seq, 256), MXU_DTYPE), sds((seq, 512), MXU_DTYPE)],
        compiler_params=_cp("arbitrary", "arbitrary"))(h, win_t, cos, sin)


def _halves(kd):
    lo = _lane(kd.shape) < 64
    z = jnp.zeros_like(kd)
    return jnp.concatenate([jnp.where(lo, kd, z), jnp.where(lo, z, kd)], axis=0)


def _attn_fwd(qt, kd, vd, dep=None):
    seq = qt.shape[0]
    bq, bk = min(2048, seq), min(1024, seq)
    nk = seq // bk
    deps, dep_specs = _dep_args(dep, 2)

    def body(q_ref, k_ref, v_ref, *rest):
        o_ref, lse_ref, m_scr, l_scr, acc = rest[-5:]
        j = pl.program_id(1)
        m_scr[...] = jnp.full_like(m_scr, -jnp.inf)
        l_scr[...] = jnp.zeros_like(l_scr)
        acc[...] = jnp.zeros_like(acc)
        sub = min(512, bq)
        nsub = bq // sub
        lo = _lane((sub, 128)) < 64

        def kv_block(n, carry):
            rows = pl.ds(pl.multiple_of(n * bk, bk), bk)
            k2, v2 = _halves(k_ref[rows, :]), _halves(v_ref[rows, :])
            for pr in range(2):
                for hf in range(nsub):
                    qr = slice(hf * sub, (hf + 1) * sub)
                    s2 = _nt(q_ref[qr, 128 * pr:128 * (pr + 1)], k2)
                    ps, alphas = [], []
                    for e in range(2):
                        g = 2 * pr + e
                        s = s2[:, e * bk:(e + 1) * bk]
                        m_prev = m_scr[g, qr]
                        m_next = jnp.maximum(m_prev, jnp.max(s, axis=1, keepdims=True))
                        alpha = jnp.exp2(m_prev - m_next)
                        pe = jnp.exp2(s - jnp.tile(m_next, (1, bk // 128)))
                        l_scr[g, qr] = alpha * l_scr[g, qr] + jnp.sum(pe, axis=1, keepdims=True)
                        m_scr[g, qr] = m_next
                        ps.append(_mx(pe))
                        alphas.append(alpha)
                    o2 = _nn(jnp.concatenate(ps, axis=1), v2)
                    sl = slice(128 * pr, 128 * (pr + 1))
                    acc[qr, sl] = acc[qr, sl] * jnp.where(lo, alphas[0], alphas[1]) + o2
            return carry

        lax.fori_loop(0, nk, kv_block, 0)
        lo_all = _lane((bq, 128)) < 64
        for pr in range(2):
            sl = slice(128 * pr, 128 * (pr + 1))
            o_ref[:, sl] = acc[:, sl] / jnp.where(lo_all, l_scr[2 * pr], l_scr[2 * pr + 1])
        for g in range(4):
            lse = jnp.transpose(m_scr[g] + jnp.log2(l_scr[g]))
            lse_ref[pl.ds(4 * j + g, 1), :] = lse[0:1, :]

    return pl.pallas_call(
        body, grid=(seq // bq, 2), name="attn_fwd",
        in_specs=[pl.BlockSpec((bq, 256), lambda i, j: (i, j)), pl.BlockSpec((None, seq, 128), lambda i, j: (j, 0, 0)),
                  pl.BlockSpec((None, seq, 128), lambda i, j: (j, 0, 0))] + dep_specs,
        out_specs=[pl.BlockSpec((bq, 256), lambda i, j: (i, j)), pl.BlockSpec((8, bq), lambda i, j: (0, i))],
        out_shape=[jax.ShapeDtypeStruct((seq, 512), F32), jax.ShapeDtypeStruct((8, seq), F32)],
        scratch_shapes=[pltpu.VMEM((4, bq, 128), F32), pltpu.VMEM((4, bq, 128), F32), pltpu.VMEM((bq, 256), F32)],
        compiler_params=_cp("parallel", "arbitrary"))(qt, kd, vd, *deps)


def _attn_bwd(qt, kd, vd, do, lse, delta, dep=None):
    seq = qt.shape[0]
    bq, bk = min(1024, seq), min(1024, seq)
    nq, nk = seq // bq, seq // bk
    deps, dep_specs = _dep_args(dep, 3)

    def body(q_ref, do_ref, k_ref, v_ref, lse_ref, del_ref, *rest):
        dq_ref, dk_ref, dv_ref = rest[-3:]
        j, i = pl.program_id(0), pl.program_id(1)
        dq_ref[...] = jnp.zeros_like(dq_ref)

        @pl.when(i == 0)
        def _():
            dk_ref[...] = jnp.zeros_like(dk_ref)
            dv_ref[...] = jnp.zeros_like(dv_ref)

        lo = _lane((bk, 128)) < 64
        big = lambda r: jnp.concatenate([jnp.broadcast_to(r[0], (bk, bq)), jnp.broadcast_to(r[1], (bk, bq))], axis=0)

        def kv_block(n, carry):
            rows = pl.ds(pl.multiple_of(n * bk, bk), bk)
            k2, v2 = _halves(k_ref[rows, :]), _halves(v_ref[rows, :])
            for pr in range(2):
                sl = slice(128 * pr, 128 * (pr + 1))
                qp, dop = q_ref[:, sl], do_ref[:, sl]
                s_t = _nt(k2, qp)
                dp_t = _nt(v2, dop)
                ls = [lse_ref[pl.ds(4 * j + 2 * pr + e, 1), :] for e in range(2)]
                dl = [del_ref[pl.ds(4 * j + 2 * pr + e, 1), :] for e in range(2)]
                p_t = jnp.exp2(s_t - big(ls))
                ds_t = _mx(p_t * (dp_t - big(dl)))
                rv = _nn(_mx(p_t), dop)
                rk = _nn(ds_t, qp) * LN2
                dv_ref[rows, :] += jnp.where(lo, rv[:bk], rv[bk:])
                dk_ref[rows, :] += jnp.where(lo, rk[:bk], rk[bk:])
                dq_ref[:, sl] += _tn(ds_t, k2)
            return carry

        lax.fori_loop(0, nk, kv_block, 0)

    return pl.pallas_call(
        body, grid=(2, nq), name="attn_bwd",
        in_specs=[pl.BlockSpec((bq, 256), lambda j, i: (i, j)), pl.BlockSpec((bq, 256), lambda j, i: (i, j)),
                  pl.BlockSpec((None, seq, 128), lambda j, i: (j, 0, 0)), pl.BlockSpec((None, seq, 128), lambda j, i: (j, 0, 0)),
                  pl.BlockSpec((8, bq), lambda j, i: (0, i)), pl.BlockSpec((8, bq), lambda j, i: (0, i))] + dep_specs,
        out_specs=[pl.BlockSpec((bq, 256), lambda j, i: (i, j)), pl.BlockSpec((None, seq, 128), lambda j, i: (j, 0, 0)),
                   pl.BlockSpec((None, seq, 128), lambda j, i: (j, 0, 0))],
        out_shape=[jax.ShapeDtypeStruct((seq, 512), F32), jax.ShapeDtypeStruct((2, seq, 128), F32),
                   jax.ShapeDtypeStruct((2, seq, 128), F32)],
        compiler_params=_cp("arbitrary", "arbitrary"))(qt, do, kd, vd, lse, delta, *deps)


def _ret_tables(lg_f, lg_b, c):
    idx = jnp.arange(c, dtype=F32)
    diff = idx[:, None] - idx[None, :]
    a, b = lg_f[:, None, None], lg_b[:, None, None]
    low, up = (diff >= 0)[None], (diff < 0)[None]
    dmat = jnp.where(low, jnp.exp(a * jnp.maximum(diff, 0.0)[None]), jnp.exp(b * jnp.maximum(-diff, 0.0)[None]))
    dda = jnp.where(low, diff[None] * jnp.exp(a * jnp.maximum(diff, 0.0)[None]), 0.0)
    ddb = jnp.where(up, -diff[None] * jnp.exp(b * jnp.maximum(-diff, 0.0)[None]), 0.0)
    rep = lambda w: jnp.broadcast_to(w[:, :, None], (RET_HEADS, c, 128))
    wqf = rep(jnp.exp(lg_f[:, None] * (idx + 1.0)[None]))
    wqb = rep(jnp.exp(lg_b[:, None] * (c - idx)[None]))
    wkf = rep(jnp.exp(lg_f[:, None] * (c - 1.0 - idx)[None]))
    wkb = rep(jnp.exp(lg_b[:, None] * idx[None]))
    cdf, cdb = jnp.exp(lg_f * c), jnp.exp(lg_b * c)
    dec_fb = jnp.broadcast_to(jnp.concatenate([cdf, cdb])[:, None], (8, 128))
    dec_bf = jnp.broadcast_to(jnp.concatenate([cdb, cdf])[:, None], (8, 128))
    return dict(dmat=dmat, dda=dda, ddb=ddb, wqf=wqf, wqb=wqb, wkf=wkf, wkb=wkb, dec_fb=dec_fb, dec_bf=dec_bf)


def _ret_states(xk, yv, w_fwd, w_rev, dec, name):
    seq = xk.shape[0]
    c = RET_CHUNK
    nc = seq // c
    grp = min(RET_GROUP, nc)
    ns = nc // grp

    def body(xf_ref, yf_ref, xr_ref, yr_ref, wf_ref, wr_ref, dec_ref, sf_ref, sr_ref, st_f, st_r):
        @pl.when(pl.program_id(0) == 0)
        def _():
            st_f[...] = jnp.zeros_like(st_f)
            st_r[...] = jnp.zeros_like(st_r)

        lane = _lane((c, 128))

        def chunk(g, carry):
            for x_ref, y_ref, w_ref, s_ref, st, base, gg in ((xf_ref, yf_ref, wf_ref, sf_ref, st_f, 0, g),
                                                             (xr_ref, yr_ref, wr_ref, sr_ref, st_r, 4, grp - 1 - g)):
                rows = pl.ds(pl.multiple_of(gg * c, c), c)
                for h in range(RET_HEADS):
                    pr, e = h // 2, h % 2
                    half = (lane < 64) if e == 0 else (lane >= 64)
                    s_ref[gg, h] = st[h].astype(s_ref.dtype)
                    xm = jnp.where(half, x_ref[rows, 128 * pr:128 * (pr + 1)].astype(F32) * w_ref[h], 0.0)
                    st[h] = st[h] * dec_ref[base + h:base + h + 1, :] + _tn(_mx(xm), _mx(y_ref[rows, 128 * h:128 * (h + 1)]))
            return carry

        lax.fori_loop(0, grp, chunk, 0, unroll=True)

    xs = lambda f: pl.BlockSpec((grp * c, 256), f)
    ys = lambda f: pl.BlockSpec((grp * c, 512), f)
    fwd, rev = (lambda n: (n, 0)), (lambda n: (ns - 1 - n, 0))
    wsp = pl.BlockSpec((RET_HEADS, c, 128), lambda n: (0, 0, 0))
    return pl.pallas_call(
        body, grid=(ns,), name=name,
        in_specs=[xs(fwd), ys(fwd), xs(rev), ys(rev), wsp, wsp, pl.BlockSpec((8, 128), lambda n: (0, 0))],
        out_specs=[pl.BlockSpec((grp, RET_HEADS, 128, 128), lambda n: (n, 0, 0, 0)),
                   pl.BlockSpec((grp, RET_HEADS, 128, 128), lambda n: (ns - 1 - n, 0, 0, 0))],
        out_shape=[jax.ShapeDtypeStruct((nc, RET_HEADS, 128, 128), MXU_DTYPE)] * 2,
        scratch_shapes=[pltpu.VMEM((RET_HEADS, 128, 128), F32), pltpu.VMEM((RET_HEADS, 128, 128), F32)],
        compiler_params=_cp("arbitrary"))(xk, yv, xk, yv, w_fwd, w_rev, dec)


def _ret_fwd(rq, rk, rv, rf, rb, tb):
    seq = rq.shape[0]
    c = RET_CHUNK
    grp = min(RET_GROUP, seq // c)

    def body(q_ref, k_ref, v_ref, rf_ref, rb_ref, d_ref, wqf_ref, wqb_ref, o_ref):
        lane = _lane((c, 128))

        def chunk(g, carry):
            rows = pl.ds(pl.multiple_of(g * c, c), c)
            for h in range(RET_HEADS):
                pr, e = h // 2, h % 2
                half = (lane < 64) if e == 0 else (lane >= 64)
                sl = slice(128 * pr, 128 * (pr + 1))
                qp, kp = q_ref[rows, sl], k_ref[rows, sl]
                km = jnp.where(half, kp, jnp.zeros_like(kp))
                sd = _mx(_nt(qp, km) * d_ref[h])
                qf = qp.astype(F32)
                o = _nn(sd, v_ref[rows, 128 * h:128 * (h + 1)])
                o = o + _nn(_mx(qf * wqf_ref[h]), rf_ref[g, h]) + _nn(_mx(qf * wqb_ref[h]), rb_ref[g, h])
                o_ref[rows, 128 * h:128 * (h + 1)] = o
            return carry

        lax.fori_loop(0, grp, chunk, 0, unroll=True)

    st = pl.BlockSpec((grp, RET_HEADS, 128, 128), lambda n: (n, 0, 0, 0))
    wsp = pl.BlockSpec((RET_HEADS, c, 128), lambda n: (0, 0, 0))
    return pl.pallas_call(
        body, grid=(seq // (grp * c),), name="ret_fwd",
        in_specs=[pl.BlockSpec((grp * c, 256), lambda n: (n, 0)), pl.BlockSpec((grp * c, 256), lambda n: (n, 0)),
                  pl.BlockSpec((grp * c, 512), lambda n: (n, 0)), st, st, pl.BlockSpec((RET_HEADS, c, c), lambda n: (0, 0, 0)), wsp, wsp],
        out_specs=pl.BlockSpec((grp * c, 512), lambda n: (n, 0)),
        out_shape=jax.ShapeDtypeStruct((seq, 512), F32), compiler_params=_cp("parallel"))(
            rq, rk, rv, rf, rb, tb["dmat"], tb["wqf"], tb["wqb"])


def _ret_bwd(rq, rk, rv, do, rf, rb, hf, hb, tb):
    seq = rq.shape[0]
    c = RET_CHUNK
    grp = min(RET_GROUP, seq // c)

    def body(q_ref, k_ref, v_ref, do_ref, rf_ref, rb_ref, hf_ref, hb_ref, d_ref, dda_ref, ddb_ref,
             wqf_ref, wqb_ref, wkf_ref, wkb_ref, cdec_ref, dq_ref, dk_ref, dv_ref, dlg_ref):
        @pl.when(pl.program_id(0) == 0)
        def _():
            dlg_ref[...] = jnp.zeros_like(dlg_ref)

        lane = _lane((c, 128))
        pos = lax.broadcasted_iota(jnp.int32, (c, 128), 0).astype(F32)

        def chunk(g, carry):
            rows = pl.ds(pl.multiple_of(g * c, c), c)
            for pr in range(2):
                sl = slice(128 * pr, 128 * (pr + 1))
                qp, kp = q_ref[rows, sl], k_ref[rows, sl]
                qf, kf = qp.astype(F32), kp.astype(F32)
                dq_acc = jnp.zeros((c, 128), F32)
                dk_acc = jnp.zeros((c, 128), F32)
                for e in range(2):
                    h = 2 * pr + e
                    half = (lane < 64) if e == 0 else (lane >= 64)
                    hs = slice(128 * h, 128 * (h + 1))
                    km = jnp.where(half, kp, jnp.zeros_like(kp))
                    kmf = km.astype(F32)
                    vh, doh = v_ref[rows, hs], do_ref[rows, hs]
                    rfh, rbh, hfh, hbh = rf_ref[g, h], rb_ref[g, h], hf_ref[g, h], hb_ref[g, h]
                    s = _nt(qp, km)
                    dpm = _nt(doh, vh)
                    ds_b = _mx(dpm * d_ref[h])
                    sd_b = _mx(s * d_ref[h])
                    dq_if = wqf_ref[h] * _nt(doh, rfh)
                    dq_ib = wqb_ref[h] * _nt(doh, rbh)
                    dq_acc = dq_acc + _nn(ds_b, km) + dq_if + dq_ib
                    dk_sf = wkf_ref[h] * _nt(vh, hfh)
                    dk_sb = wkb_ref[h] * _nt(vh, hbh)
                    dk_acc = dk_acc + jnp.where(half, _tn(ds_b, qp), 0.0) + dk_sf + dk_sb
                    dv = _tn(sd_b, doh) + _nn(_mx(kmf * wkf_ref[h]), hfh) + _nn(_mx(kmf * wkb_ref[h]), hbh)
                    dv_ref[rows, hs] = dv.astype(dv_ref.dtype)
                    sdp = s * dpm
                    hr_f = hfh.astype(F32) * rfh.astype(F32)
                    hr_b = hbh.astype(F32) * rbh.astype(F32)
                    da = (_rowsum128(sdp * dda_ref[h]) + _rowsum128((pos + 1.0) * qf * dq_if)
                          + _rowsum128((c - 1.0 - pos) * kf * dk_sf) + cdec_ref[h:h + 1, :] * _rowsum128(hr_f))
                    db = (_rowsum128(sdp * ddb_ref[h]) + _rowsum128((c - pos) * qf * dq_ib)
                          + _rowsum128(pos * kf * dk_sb) + cdec_ref[4 + h:5 + h, :] * _rowsum128(hr_b))
                    dlg_ref[h:h + 1, :] += da
                    dlg_ref[4 + h:5 + h, :] += db
                dq_ref[rows, sl] = dq_acc
                dk_ref[rows, sl] = dk_acc
            return carry

        lax.fori_loop(0, grp, chunk, 0, unroll=2)

    st = pl.BlockSpec((grp, RET_HEADS, 128, 128), lambda n: (n, 0, 0, 0))
    wsp = pl.BlockSpec((RET_HEADS, c, 128), lambda n: (0, 0, 0))
    dsp = pl.BlockSpec((RET_HEADS, c, c), lambda n: (0, 0, 0))
    x256 = pl.BlockSpec((grp * c, 256), lambda n: (n, 0))
    x512 = pl.BlockSpec((grp * c, 512), lambda n: (n, 0))
    cdec = tb["dec_fb"] * float(c)
    return pl.pallas_call(
        body, grid=(seq // (grp * c),), name="ret_bwd",
        in_specs=[x256, x256, x512, x512, st, st, st, st, dsp, dsp, dsp, wsp, wsp, wsp, wsp,
                  pl.BlockSpec((8, 128), lambda n: (0, 0))],
        out_specs=[x256, x256, x512, pl.BlockSpec((8, 128), lambda n: (0, 0))],
        out_shape=[jax.ShapeDtypeStruct((seq, 256), F32), jax.ShapeDtypeStruct((seq, 256), F32),
                   jax.ShapeDtypeStruct((seq, 512), MXU_DTYPE), jax.ShapeDtypeStruct((8, 128), F32)],
        compiler_params=_cp("arbitrary"))(
            rq, rk, rv, do, rf, rb, hf, hb, tb["dmat"], tb["dda"], tb["ddb"], tb["wqf"], tb["wqb"], tb["wkf"], tb["wkb"], cdec)


def _tail(x, tgt, o_att, o_ret, p, mod, g_post, gn_g, wpt, wout):
    seq = x.shape[0]
    bs = 256
    nb = seq // bs

    def body(x_ref, t_ref, oa_ref, or_ref, za_ref, zr_ref, gl_ref, mod_ref, gp_ref, gn_ref, wpt_ref, wout_ref,
             dout_ref, dza_ref, dzr_ref, dgl_ref, doa_ref, dor_ref, del_ref, gout_ref, gpt_ref,
             dgate_ref, dgpost_ref, dgn_ref, loss_ref, gout_acc, gpt_acc):
        i = pl.program_id(0)

        @pl.when(i == 0)
        def _():
            gout_acc[...] = jnp.zeros_like(gout_acc)
            gpt_acc[...] = jnp.zeros_like(gpt_acc)
            dgate_ref[...] = jnp.zeros_like(dgate_ref)
            dgpost_ref[...] = jnp.zeros_like(dgpost_ref)
            dgn_ref[...] = jnp.zeros_like(dgn_ref)
            loss_ref[...] = jnp.zeros_like(loss_ref)

        oa, za, zr = oa_ref[...], za_ref[...], zr_ref[...]
        sga, sgr = _sigmoid(za), _sigmoid(zr)
        sza, szr = za * sga, zr * sgr
        ya_b = _mx(oa * sza)
        ons, rstds = [], []
        for h in range(RET_HEADS):
            oh = or_ref[:, 128 * h:128 * (h + 1)]
            xc = oh - jnp.mean(oh, axis=-1, keepdims=True)
            rstd = lax.rsqrt(jnp.mean(xc * xc, axis=-1, keepdims=True) + EPS)
            ons.append(xc * rstd)
            rstds.append(rstd)
        on = jnp.concatenate(ons, axis=1)
        yn = on * gn_ref[...]
        yr_b = _mx(yn * szr)
        wpa_t, wpr_t = wpt_ref[:, 0:512], wpt_ref[:, 512:1024]
        a_att = _nt(ya_b, wpa_t)
        a_ret = _nt(yr_b, wpr_t)
        gts = _sigmoid(gl_ref[...])
        g_att, g_ret = gts[:, 0:D_MODEL], gts[:, D_MODEL:2 * D_MODEL]
        merged_b = _mx(g_att * a_att + g_ret * a_ret)
        u = _nn(merged_b, wout_ref[...])
        r = lax.rsqrt(jnp.mean(u * u, axis=-1, keepdims=True) + EPS)
        un = u * r
        gpost = gp_ref[...]
        y = un * gpost
        gate = mod_ref[:, 2 * D_MODEL:3 * D_MODEL]
        err = x_ref[...] + gate * y - t_ref[...]
        loss_ref[...] += (0.5 / D_MODEL) * _rowsum128(err * err)
        dout = err * (1.0 / D_MODEL)
        dout_ref[...] = dout
        dgate_ref[...] += jnp.sum(dout * y, axis=0, keepdims=True)
        dy = dout * gate
        dgpost_ref[...] += jnp.sum(dy * un, axis=0, keepdims=True)
        dun = dy * gpost
        du_b = _mx(r * (dun - un * jnp.mean(dun * un, axis=-1, keepdims=True)))
        dmerged = _nt(du_b, wout_ref[...])
        gout_acc[...] += _tn(merged_b, du_b)
        d_att, d_ret = dmerged * g_att, dmerged * g_ret
        dgl_ref[:, 0:D_MODEL] = (d_att * a_att * (1.0 - g_att)).astype(dgl_ref.dtype)
        dgl_ref[:, D_MODEL:2 * D_MODEL] = (d_ret * a_ret * (1.0 - g_ret)).astype(dgl_ref.dtype)
        d_att_b, d_ret_b = _mx(d_att), _mx(d_ret)
        dya = _nn(d_att_b, wpa_t)
        dyr = _nn(d_ret_b, wpr_t)
        gpt_acc[:, 0:512] += _tn(d_att_b, ya_b)
        gpt_acc[:, 512:1024] += _tn(d_ret_b, yr_b)
        dza_ref[...] = (dya * oa * (sga * (1.0 + za * (1.0 - sga)))).astype(dza_ref.dtype)
        doa = dya * sza
        doa_ref[...] = doa.astype(doa_ref.dtype)
        dt = jnp.transpose(doa * oa)
        del_ref[...] = jnp.sum(dt.reshape(8, HEAD_DIM, bs), axis=1)
        dzr_ref[...] = (dyr * yn * (sgr * (1.0 + zr * (1.0 - sgr)))).astype(dzr_ref.dtype)
        dyn = dyr * szr
        dgn_ref[...] += jnp.sum(dyn * on, axis=0, keepdims=True)
        don = dyn * gn_ref[...]
        for h in range(RET_HEADS):
            hs = slice(128 * h, 128 * (h + 1))
            dh, oh = don[:, hs], ons[h]
            doh = rstds[h] * (dh - jnp.mean(dh, axis=-1, keepdims=True) - oh * jnp.mean(dh * oh, axis=-1, keepdims=True))
            dor_ref[:, hs] = doh.astype(dor_ref.dtype)

        @pl.when(i == nb - 1)
        def _():
            gout_ref[...] = gout_acc[...].astype(gout_ref.dtype)
            gpt_ref[...] = gpt_acc[...].astype(gpt_ref.dtype)

    row = lambda w: pl.BlockSpec((bs, w), lambda i: (i, 0))
    el = lambda w, off: pl.BlockSpec((pl.Element(bs), pl.Element(w)), lambda i: (i * bs, off))
    vec = lambda w: pl.BlockSpec((1, w), lambda i: (0, 0))
    full = pl.BlockSpec((D_MODEL, D_MODEL), lambda i: (0, 0))
    sds = jax.ShapeDtypeStruct
    return pl.pallas_call(
        body, grid=(nb,), name="tail",
        in_specs=[row(D_MODEL), row(D_MODEL), row(512), row(512), el(512, ZA_B), el(512, ZR_B), el(2 * D_MODEL, GL_B),
                  vec(3 * D_MODEL), vec(D_MODEL), vec(512), full, full],
        out_specs=[row(D_MODEL), row(512), row(512), row(2 * D_MODEL), row(512), row(512),
                   pl.BlockSpec((8, bs), lambda i: (0, i)), full, full, vec(D_MODEL), vec(D_MODEL), vec(512), vec(128)],
        out_shape=[sds((seq, D_MODEL), F32), sds((seq, 512), MXU_DTYPE), sds((seq, 512), MXU_DTYPE),
                   sds((seq, 2 * D_MODEL), MXU_DTYPE), sds((seq, 512), MXU_DTYPE), sds((seq, 512), MXU_DTYPE),
                   sds((8, seq), F32), sds((D_MODEL, D_MODEL), MXU_DTYPE), sds((D_MODEL, D_MODEL), MXU_DTYPE),
                   sds((1, D_MODEL), F32), sds((1, D_MODEL), F32), sds((1, 512), F32), sds((1, 128), F32)],
        scratch_shapes=[pltpu.VMEM((D_MODEL, D_MODEL), F32), pltpu.VMEM((D_MODEL, D_MODEL), F32)],
        compiler_params=_cp("arbitrary"))(x, tgt, o_att, o_ret, p, p, p, mod, g_post, gn_g, wpt, wout)


def _assemble(p, cos, sin, qg2, kg2, dqt, dkp, dvp, dza, drq, drk, drv, dzr, dgl):
    seq = p.shape[0]
    bs = 512

    def body(p_ref, cos_ref, sin_ref, qg_ref, kg_ref, dqt_ref, dkp_ref, dvp_ref, dza_ref, drq_ref, drk_ref, drv_ref,
             dzr_ref, dgl_ref, dp_ref, dqg_ref, dkg_ref):
        @pl.when(pl.program_id(0) == 0)
        def _():
            dqg_ref[...] = jnp.zeros_like(dqg_ref)
            dkg_ref[...] = jnp.zeros_like(dkg_ref)

        m = _blockdiag64()
        cs, sn = cos_ref[...], sin_ref[...]
        lo = _lane((bs, 128)) < 64

        def norm_bwd(raw, dn, g, dg_ref):
            r = lax.rsqrt(_seg64(raw * raw, m) * (1.0 / 64) + EPS)
            xn = raw * r
            dg_ref[...] += jnp.sum(dn * xn, axis=0, keepdims=True)
            dxn = dn * g
            return r * (dxn - xn * (_seg64(dxn * xn, m) * (1.0 / 64)))

        for pr in range(4):
            sl = slice(128 * pr, 128 * (pr + 1))
            dn = _rope_t(dqt_ref[:, sl] * 0.125, cs, sn)
            dp_ref[:, QA + 128 * pr:QA + 128 * (pr + 1)] = norm_bwd(p_ref[:, sl], dn, qg_ref[...], dqg_ref).astype(dp_ref.dtype)
        fold = lambda a: a + pltpu.roll(a, 64, 1)
        dk = jnp.where(lo, fold(dkp_ref[0]), fold(dkp_ref[1]))
        dp_ref[:, KA:KA + 128] = norm_bwd(p_ref[:, KA:KA + 128], _rope_t(dk, cs, sn), kg_ref[...], dkg_ref).astype(dp_ref.dtype)
        dp_ref[:, VA:VA + 128] = jnp.where(lo, fold(dvp_ref[0]), fold(dvp_ref[1])).astype(dp_ref.dtype)
        dp_ref[:, ZA:ZA + 512] = dza_ref[...]
        for pr in range(2):
            sl = slice(128 * pr, 128 * (pr + 1))
            dp_ref[:, QR + 128 * pr:QR + 128 * (pr + 1)] = _rope_t(drq_ref[:, sl], cs, sn).astype(dp_ref.dtype)
            dp_ref[:, KR + 128 * pr:KR + 128 * (pr + 1)] = _rope_t(drk_ref[:, sl] * 0.125, cs, sn).astype(dp_ref.dtype)
        dp_ref[:, VR:VR + 512] = drv_ref[...]
        dp_ref[:, ZR:ZR + 512] = dzr_ref[...]
        dp_ref[:, GL:GL + 2 * D_MODEL] = dgl_ref[...]

    row = lambda w: pl.BlockSpec((bs, w), lambda i: (i, 0))
    gsp = pl.BlockSpec((1, 128), lambda i: (0, 0))
    dup = pl.BlockSpec((2, bs, 128), lambda i: (0, i, 0))
    return pl.pallas_call(
        body, grid=(seq // bs,), name="assemble_dp",
        in_specs=[row(768), row(128), row(128), gsp, gsp, row(512), dup, dup, row(512), row(256), row(256), row(512),
                  row(512), row(2 * D_MODEL)],
        out_specs=[row(IN_WIDTH), gsp, gsp],
        out_shape=[jax.ShapeDtypeStruct((seq, IN_WIDTH), MXU_DTYPE), jax.ShapeDtypeStruct((1, 128), F32),
                   jax.ShapeDtypeStruct((1, 128), F32)],
        compiler_params=_cp("arbitrary"))(p, cos, sin, qg2, kg2, dqt, dkp, dvp, dza, drq, drk, drv, dzr, dgl)


def _local_step(x, tgt, mod, g_pre, qn_g, kn_g, w_dec_f, w_dec_b, gn_g, g_post, w_attn, rest_start, rest_wait, send=None, own=()):
    send = send or (lambda name, arrays: None)
    seq = x.shape[0]
    cos, sin = _rope_tables(seq)
    qg2, kg2 = jnp.tile(qn_g, (1, 2)), jnp.tile(kn_g, (1, 2))
    lg_f = jax.nn.log_sigmoid(w_dec_f[0])
    lg_b = jax.nn.log_sigmoid(w_dec_b[0])
    tb = _ret_tables(lg_f, lg_b, RET_CHUNK)

    h, p_a, qt, kd, vd, *placed = _attn_inputs(x, mod, g_pre, w_attn, cos, sin, qg2, kg2, own=own)
    o_att, lse = _attn_fwd(qt, kd, vd, dep=rest_start(qt, *placed))
    win_t, wpt, wout = rest_wait(o_att)
    p_b, rq, rk, rv = _in_proj_rest(h, win_t, cos, sin)
    rf, rb = _ret_states(rk, rv, tb["wkf"], tb["wkb"], tb["dec_fb"], "ret_states_fwd")
    o_ret = _ret_fwd(rq, rk, rv, rf, rb, tb)
    (dout, dza, dzr, dgl, do_att, do_ret, delta, g_out, g_pt, dgate, dgpost, dgn, loss_l) = _tail(
        x, tgt, o_att, o_ret, p_b, mod, g_post, gn_g, wpt, wout)
    dep = send("early", (g_pt, g_out))
    dqt, dkp, dvp = _attn_bwd(qt, kd, vd, do_att, lse, delta, dep=dep)
    hb, hf = _ret_states(rq, do_ret, tb["wqb"], tb["wqf"], tb["dec_bf"], "ret_states_bwd")
    drq, drk, drv, dlg = _ret_bwd(rq, rk, rv, do_ret, rf, rb, hf, hb, tb)
    dp, dqg, dkg = _assemble(p_a, cos, sin, qg2, kg2, dqt, dkp, dvp, dza, drq, drk, drv, dzr, dgl)
    g_in_t = _matmul(dp, h, ta=True, tb=False, tm=256, tn=D_MODEL, out_dtype=MXU_DTYPE, name="in_proj_dw")
    dep = send("late", (g_in_t,))
    grad_x, dshift, dscale, dgpre = _in_proj_dx(x, dp, win_t, dout, mod, g_pre, dep=dep)

    dlg = jnp.sum(dlg, axis=1)
    small = dict(
        dmod=jnp.concatenate([dshift, dscale, dgate], axis=1),
        g_pre=dgpre, g_post=dgpost, gn_g=dgn,
        qn_g=dqg[:, :64] + dqg[:, 64:], kn_g=dkg[:, :64] + dkg[:, 64:],
        w_dec_f=(dlg[0:4] * jax.nn.sigmoid(-w_dec_f[0]))[None], w_dec_b=(dlg[4:8] * jax.nn.sigmoid(-w_dec_b[0]))[None],
        loss=jnp.sum(loss_l, axis=1, keepdims=True))
    return grad_x, g_in_t, g_pt, g_out, small


N_DEV = 8
N_CHIP = 4
HBM_SPEC = pl.BlockSpec(memory_space=pl.ANY)
VMEM_SPEC = pl.BlockSpec(memory_space=pltpu.VMEM)
SHARD_ROWS = (IN_WIDTH // N_CHIP, D_MODEL // N_CHIP, D_MODEL // N_CHIP)


def _coords():
    return lax.axis_index("x"), lax.axis_index("y"), lax.axis_index("c")


def _peer(k):
    x, y, c = _coords()
    return (1 - x if k & 4 else x, 1 - y if k & 2 else y, 1 - c if k & 1 else c)


def _dev_index(p):
    return 4 * p[0] + 2 * p[1] + p[2]


def _rows(ref, start, size):
    return ref.at[pl.ds(pl.multiple_of(start, 16), size), :]


def _remote(src, dst, ssem, rsem, dev):
    return pltpu.make_async_remote_copy(src_ref=src, dst_ref=dst, send_sem=ssem, recv_sem=rsem, device_id=dev,
                                        device_id_type=MESH)


ATTN_CHUNK = 32


def _mod_and_attn_rows(c, w_ada_s, b4, win_s):
    ncol = w_ada_s.shape[1]
    half = ATTN_COLS // 2
    offs = list(range(0, half, ATTN_CHUNK))
    nq = len(offs)
    rows = lambda ref, cc, off: ref.at[pl.ds(pl.multiple_of(cc * half + off, 16), ATTN_CHUNK), :]

    def body(c_ref, w_ref, b_ref, src, cs_ref, mod_ref, out, modsh, modall, ssem, rsem, lsem, asem, bsem, fsem, gsem, hsem):
        x, y, cc = _coords()
        me = _dev_index((x, y, cc))
        chip = me // 2
        sib = (x, y, 1 - cc)
        cs_ref[me] = c_ref[...]
        sends = []
        for k in range(1, N_DEV):
            cp = _remote(c_ref, cs_ref.at[me], ssem.at[k - 1], rsem.at[k - 1], _peer(k))
            cp.start()
            sends.append(cp)
        own = pltpu.make_async_copy(src.at[pl.ds(0, ATTN_COLS), :], out, lsem.at[0])
        bulk = [_remote(rows(src, cc, off), rows(out, cc, off), asem.at[(k2 - 1) * nq + q], bsem.at[q], (k2 // 2, k2 % 2, cc))
                for k2 in (1, 2) for q, off in enumerate(offs)]
        relay_chip = lambda q: 1 + q % 2

        @pl.when(chip == 0)
        def _():
            own.start()
            for cp in bulk:
                cp.start()

        for k in range(1, N_DEV):
            slot = cs_ref.at[_dev_index(_peer(k))]
            _remote(slot, slot, ssem.at[k - 1], rsem.at[k - 1], _peer(k)).wait_recv()
        cs = jnp.concatenate([cs_ref[d] for d in range(N_DEV)], axis=0)
        ms = _nn(cs * _sigmoid(cs), w_ref[...], precision=HIGHEST) + b_ref[pl.ds(chip, 1), :]
        modsh[...] = ms
        modall[chip] = ms
        for k2 in range(1, N_CHIP):
            cp = _remote(modsh, modall.at[chip], ssem.at[6 + k2], rsem.at[6 + k2], _peer(2 * k2))
            cp.start()
            sends.append(cp)

        @pl.when(chip != 0)
        def _():
            passed = []
            relays = [_remote(rows(out, cc, off), rows(out, cc, off), hsem.at[q], bsem.at[q], (1, 1, cc)) for q, off in enumerate(offs)]
            for q, off in enumerate(offs):
                got = rows(out, cc, off)
                _remote(got, got, asem.at[q], bsem.at[q], (0, 0, cc)).wait_recv()
                cp = _remote(got, got, fsem.at[q], gsem.at[q], sib)
                cp.start()
                passed.append(cp)
                pl.when(chip == relay_chip(q))(relays[q].start)
            for q, off in enumerate(offs):
                got = rows(out, 1 - cc, off)
                _remote(got, got, fsem.at[q], gsem.at[q], sib).wait_recv()
            for cp in passed:
                cp.wait_send()
            for q in range(nq):
                pl.when(chip == relay_chip(q))(relays[q].wait_send)

        for k2 in range(1, N_CHIP):
            slot = modall.at[_dev_index(_peer(2 * k2)) // 2]
            _remote(slot, slot, ssem.at[6 + k2], rsem.at[6 + k2], _peer(2 * k2)).wait_recv()
        for jj in range(N_CHIP):
            mod_ref[:, ncol * jj:ncol * (jj + 1)] = modall[jj, pl.ds(me, 1), :]
        for cp in sends:
            cp.wait_send()

        @pl.when(chip == 0)
        def _():
            for cp in bulk:
                cp.wait_send()
            own.wait()

    dma = pltpu.SemaphoreType.DMA
    return pl.pallas_call(
        body, name="mod_and_attn_rows", in_specs=[VMEM_SPEC] * 4, out_specs=[VMEM_SPEC, VMEM_SPEC, HBM_SPEC],
        out_shape=[jax.ShapeDtypeStruct((N_DEV, 1, D_MODEL), F32), jax.ShapeDtypeStruct((1, N_CHIP * ncol), F32),
                   jax.ShapeDtypeStruct((ATTN_COLS, win_s.shape[1]), win_s.dtype)],
        scratch_shapes=[pltpu.VMEM((N_DEV, ncol), F32), pltpu.VMEM((N_CHIP, N_DEV, ncol), F32), dma((10,)), dma((10,)),
                        dma((1,)), dma((2 * nq,)), dma((nq,)), dma((nq,)), dma((nq,)), dma((nq,))],
        compiler_params=_cp())(c, w_ada_s, b4, win_s)


GATHER_CHUNK = 32


def _chunks(kinds, chunk):
    out = []
    for t, kind in enumerate(kinds):
        half = SHARD_ROWS[kind] // 2
        step = chunk if half % chunk == 0 else half
        out += [(t, off, step) for off in range(0, half, step)]
    return out


SEM_SPEC = pl.BlockSpec(memory_space=pltpu.SEMAPHORE)
HBM_ONLY = pl.BlockSpec(memory_space=pltpu.HBM)
DATAFLOW = pltpu.SideEffectType.DATAFLOW_SIDE_EFFECTING


def _gather_rest_start(shards, zones, dep):
    n = len(shards)

    def body(*refs):
        srcs, lands = refs[:n], refs[n:2 * n]
        ssem, rsem = refs[2 * n + 1], refs[2 * n + 2]
        token = refs[-1]
        x, y, _ = _coords()
        chip = 2 * x + y
        for k2 in range(1, N_CHIP):
            for t in range(n):
                q = (k2 - 1) * n + t
                _remote(srcs[t], _rows(lands[t], chip * SHARD_ROWS[t], SHARD_ROWS[t]), ssem.at[q], rsem.at[q], _peer(2 * k2)).start()
        token[...] = jnp.zeros_like(token)

    hbm = lambda a: pltpu.with_memory_space_constraint(a, pltpu.HBM)
    dma = pltpu.SemaphoreType.DMA
    outs = pl.pallas_call(
        body, name="gather_rest", in_specs=[HBM_ONLY] * (2 * n) + [HBM_SPEC],
        out_specs=[SEM_SPEC, SEM_SPEC] + [HBM_ONLY] * (2 * n) + [VMEM_SPEC],
        out_shape=[dma((3 * n,)), dma((3 * n,))] + [pltpu.HBM(a.shape, a.dtype) for a in list(shards) + list(zones)]
        + [jax.ShapeDtypeStruct((8, 128), F32)],
        input_output_aliases={i: 2 + i for i in range(2 * n)},
        compiler_params=pltpu.CompilerParams(has_side_effects=DATAFLOW))(*[hbm(a) for a in list(shards) + list(zones)], dep)
    return outs[0], outs[1], outs[2:2 + n], outs[2 + n:2 + 2 * n], outs[-1]


def _gather_rest_wait(started, after):
    ssem, rsem, shards, zones, _ = started
    n = len(shards)

    def body(*refs):
        srcs, lands = refs[:n], refs[n:2 * n]
        ssem_ref, rsem_ref = refs[2 * n], refs[2 * n + 1]
        for k2 in range(1, N_CHIP):
            pchip = _dev_index(_peer(2 * k2)) // 2
            for t in range(n):
                q = (k2 - 1) * n + t
                cp = _remote(srcs[t], _rows(lands[t], pchip * SHARD_ROWS[t], SHARD_ROWS[t]), ssem_ref.at[q], rsem_ref.at[q], _peer(2 * k2))
                cp.wait_send()
                cp.wait_recv()

    outs = pl.pallas_call(
        body, name="gather_rest_wait", in_specs=[HBM_ONLY] * (2 * n) + [SEM_SPEC, SEM_SPEC, HBM_SPEC], out_specs=[HBM_ONLY] * (2 * n),
        out_shape=[pltpu.HBM(a.shape, a.dtype) for a in list(shards) + list(zones)],
        input_output_aliases={i: i for i in range(2 * n)},
        compiler_params=pltpu.CompilerParams(has_side_effects=DATAFLOW))(*shards, *zones, ssem, rsem, after)
    return outs[n:]


def _reduced_by(ref, t, dev):
    return _rows(ref, (dev // 2) * SHARD_ROWS[t] + (dev % 2) * (SHARD_ROWS[t] // 2), SHARD_ROWS[t] // 2)


def _scatter_start(grads, kinds, name):
    n = len(grads)

    def body(*refs):
        srcs, lands = refs[:n], refs[n:2 * n]
        ssem, rsem = refs[2 * n], refs[2 * n + 1]
        token = refs[-1]
        me = _dev_index(_coords())
        for k in range(1, N_DEV):
            for i, t in enumerate(kinds):
                q = (k - 1) * n + i
                _remote(_reduced_by(srcs[i], t, _dev_index(_peer(k))), lands[i].at[me], ssem.at[q], rsem.at[q], _peer(k)).start()
        token[...] = jnp.zeros_like(token)

    zones = [lax.empty((N_DEV, SHARD_ROWS[t] // 2, g.shape[1]), g.dtype) for g, t in zip(grads, kinds)]
    hbm = lambda a: pltpu.with_memory_space_constraint(a, pltpu.HBM)
    dma = pltpu.SemaphoreType.DMA
    outs = pl.pallas_call(
        body, name=name, in_specs=[HBM_ONLY] * (2 * n), out_specs=[SEM_SPEC, SEM_SPEC] + [HBM_ONLY] * (2 * n) + [VMEM_SPEC],
        out_shape=[dma((7 * n,)), dma((7 * n,))] + [pltpu.HBM(a.shape, a.dtype) for a in list(grads) + zones]
        + [jax.ShapeDtypeStruct((8, 128), F32)],
        input_output_aliases={i: 2 + i for i in range(2 * n)},
        compiler_params=pltpu.CompilerParams(has_side_effects=DATAFLOW))(*[hbm(a) for a in list(grads) + zones])
    return outs[0], outs[1], outs[2:2 + n], outs[2 + n:2 + 2 * n], outs[-1]


def _scatter_wait(started, kinds, after, name):
    ssem, rsem, grads, zones, _ = started
    n = len(grads)

    def body(*refs):
        srcs, lands = refs[:n], refs[n:2 * n]
        ssem_ref, rsem_ref = refs[2 * n], refs[2 * n + 1]
        for k in range(1, N_DEV):
            peer = _dev_index(_peer(k))
            for i, t in enumerate(kinds):
                q = (k - 1) * n + i
                cp = _remote(_reduced_by(srcs[i], t, peer), lands[i].at[peer], ssem_ref.at[q], rsem_ref.at[q], _peer(k))
                cp.wait_send()
                cp.wait_recv()

    outs = pl.pallas_call(
        body, name=name, in_specs=[HBM_ONLY] * (2 * n) + [SEM_SPEC, SEM_SPEC, HBM_SPEC], out_specs=[HBM_ONLY] * (2 * n),
        out_shape=[pltpu.HBM(a.shape, a.dtype) for a in list(grads) + list(zones)],
        input_output_aliases={i: i for i in range(2 * n)},
        compiler_params=pltpu.CompilerParams(has_side_effects=DATAFLOW))(*grads, *zones, ssem, rsem, after)
    return outs[:n], outs[n:]


GRAD_SEGMENTS = 8


def _grad_finish(grads, zones, kinds, name):
    nt = len(grads)
    pieces = _chunks(kinds, GATHER_CHUNK)
    npc = len(pieces)
    shard = [SHARD_ROWS[k] for k in kinds]
    count = [sum(1 for p in pieces if p[0] == t) for t in range(nt)]
    nseg = min(GRAD_SEGMENTS, min(count))
    cut = [[next(p[2] for p in pieces if p[0] == t) * -(-count[t] * s // nseg) for s in range(nseg + 1)] for t in range(nt)]

    def body(*refs):
        srcs, lands, outs = refs[:nt], refs[nt:2 * nt], refs[2 * nt:3 * nt]
        slots, sums = refs[3 * nt:4 * nt], refs[4 * nt:5 * nt]
        lsem, osem, xsem, ysem = refs[5 * nt:]
        x, y, c = _coords()
        me = _dev_index((x, y, c))
        sib = (x, y, 1 - c)
        loads = []
        for s in range(nseg):
            part = []
            for t in range(nt):
                rows = pl.ds(cut[t][s], cut[t][s + 1] - cut[t][s])
                part.append(pltpu.make_async_copy(_reduced_by(srcs[t], kinds[t], me).at[rows, :], slots[t].at[me, rows, :],
                                                  lsem.at[s * N_DEV * nt + t]))
                for k in range(1, N_DEV):
                    peer = _dev_index(_peer(k))
                    part.append(pltpu.make_async_copy(lands[t].at[peer, rows, :], slots[t].at[peer, rows, :],
                                                      lsem.at[(s * N_DEV + k) * nt + t]))
            for cp in part:
                cp.start()
            loads.append(part)
        sends = []
        place = lambda t, cc, off, n: _rows(outs[t], cc * (shard[t] // 2) + off, n)
        stores = []
        for s in range(nseg):
            for cp in loads[s]:
                cp.wait()
            for q, (t, off, n) in enumerate(pieces):
                if not cut[t][s] <= off < cut[t][s + 1]:
                    continue
                r = pl.ds(off, n)
                acc = slots[t][0, r, :].astype(F32)
                for d in range(1, N_DEV):
                    acc = acc + slots[t][d, r, :].astype(F32)
                sums[t][r, :] = acc
                keep = pltpu.make_async_copy(sums[t].at[r, :], place(t, c, off, n), osem.at[q])
                give = _remote(sums[t].at[r, :], place(t, c, off, n), xsem.at[q], ysem.at[q], sib)
                keep.start()
                give.start()
                stores.append(keep)
                sends.append(give)
        for q, (t, off, n) in enumerate(pieces):
            got = place(t, 1 - c, off, n)
            _remote(got, got, xsem.at[q], ysem.at[q], sib).wait_recv()
        for cp in sends:
            cp.wait_send()
        for cp in stores:
            cp.wait()

    dma = pltpu.SemaphoreType.DMA
    return pl.pallas_call(
        body, name=name, in_specs=[HBM_SPEC] * (2 * nt), out_specs=[HBM_SPEC] * nt,
        out_shape=[jax.ShapeDtypeStruct((shard[t], g.shape[1]), F32) for t, g in enumerate(grads)],
        scratch_shapes=([pltpu.VMEM((N_DEV, shard[t] // 2, g.shape[1]), g.dtype) for t, g in enumerate(grads)]
                        + [pltpu.VMEM((shard[t] // 2, g.shape[1]), F32) for t, g in enumerate(grads)]
                        + [dma((GRAD_SEGMENTS * N_DEV * nt,)), dma((npc,)), dma((npc,)), dma((npc,))]),
        compiler_params=_cp())(*grads, *zones)


def _adam_math(w, g, m, v):
    m = ADAM_B1 * m + (1.0 - ADAM_B1) * g
    v = ADAM_B2 * v + (1.0 - ADAM_B2) * (g * g)
    m_hat = m / (1.0 - ADAM_B1 ** ADAM_STEP)
    v_hat = v / (1.0 - ADAM_B2 ** ADAM_STEP)
    return -ADAM_LR * (m_hat / (jnp.sqrt(v_hat) + ADAM_EPS) + ADAM_WD * w), m, v


def _adamw(items, name, nblk=4):
    n = len(items)

    def body(*refs):
        ins, outs = refs[:4 * n], refs[4 * n:]
        for t in range(n):
            w_ref, g_ref, m_ref, v_ref = ins[4 * t:4 * t + 4]
            d_ref, nm_ref, nv_ref = outs[3 * t:3 * t + 3]
            d_ref[...], nm_ref[...], nv_ref[...] = _adam_math(w_ref[...], g_ref[...], m_ref[...], v_ref[...])

    specs = [pl.BlockSpec((it[0].shape[0] // nblk, it[0].shape[1]), lambda i: (i, 0)) for it in items]
    outs = pl.pallas_call(
        body, grid=(nblk,), name=name, in_specs=[s for s in specs for _ in range(4)], out_specs=[s for s in specs for _ in range(3)],
        out_shape=[jax.ShapeDtypeStruct(it[0].shape, F32) for it in items for _ in range(3)],
        compiler_params=_cp("parallel"))(*[a for it in items for a in it])
    return [tuple(outs[3 * t:3 * t + 3]) for t in range(n)]


PACK = (("b_ada", 3 * D_MODEL), ("g_pre", D_MODEL), ("g_post", D_MODEL), ("gn_g", 512), ("qn_g", 64), ("kn_g", 64),
        ("w_dec_f", 4), ("w_dec_b", 4), ("loss", 1))
PACK_OFFSETS = {}
PACK_WIDTH = 0
for _name, _n in PACK:
    PACK_OFFSETS[_name] = PACK_WIDTH
    PACK_WIDTH += -(-_n // 128) * 128
SMALL_PARAMS = tuple(name for name, _ in PACK if name != "loss")


def _pack(parts):
    cols = []
    for name, n in PACK:
        pad = -(-n // 128) * 128 - n
        cols.append(parts[name].reshape(1, n).astype(F32))
        if pad:
            cols.append(jnp.zeros((1, pad), F32))
    return jnp.concatenate(cols, axis=1)


def _small_reduce(vec, cs, deps):
    ncol = 3 * D_MODEL // N_CHIP

    def body(vec_ref, cs_ref, *rest):
        tot_ref, gwa_ref, gat, ssem, rsem = rest[-5:]
        me = _dev_index(_coords())
        chip = me // 2
        gat[me] = vec_ref[...]
        sends = []
        for k in range(1, N_DEV):
            cp = _remote(vec_ref, gat.at[me], ssem.at[k - 1], rsem.at[k - 1], _peer(k))
            cp.start()
            sends.append(cp)
        for k in range(1, N_DEV):
            slot = gat.at[_dev_index(_peer(k))]
            _remote(slot, slot, ssem.at[k - 1], rsem.at[k - 1], _peer(k)).wait_recv()
        rows = [gat[d] for d in range(N_DEV)]
        tot = rows[0]
        for d in range(1, N_DEV):
            tot = tot + rows[d]
        tot_ref[...] = tot
        cs = cs_ref[...]
        ca_t = jnp.transpose(jnp.concatenate([cs * _sigmoid(cs), jnp.zeros((128 - N_DEV, D_MODEL), F32)], axis=0))
        dm = jnp.concatenate(rows + [jnp.zeros((128 - N_DEV, PACK_WIDTH), F32)], axis=0)
        for jj in range(N_CHIP):
            @pl.when(chip == jj)
            def _():
                gwa_ref[...] = _nn(ca_t, dm[:, ncol * jj:ncol * (jj + 1)], precision=HIGHEST)
        for cp in sends:
            cp.wait_send()

    return pl.pallas_call(
        body, name="small_reduce", in_specs=[VMEM_SPEC, VMEM_SPEC] + [HBM_SPEC] * len(deps), out_specs=[VMEM_SPEC] * 2,
        out_shape=[jax.ShapeDtypeStruct((1, PACK_WIDTH), F32), jax.ShapeDtypeStruct((D_MODEL, ncol), F32)],
        scratch_shapes=[pltpu.VMEM((N_DEV, 1, PACK_WIDTH), F32), pltpu.SemaphoreType.DMA((7,)), pltpu.SemaphoreType.DMA((7,))],
        compiler_params=_cp())(vec, cs, *deps)


def _small_adamw(tot, given):
    sizes = dict(PACK)
    np_ = len(SMALL_PARAMS)

    def body(*refs):
        tot_ref = refs[0]
        wmv = refs[1:1 + 3 * np_]
        outs = refs[1 + 3 * np_:1 + 7 * np_]
        loss_ref = refs[-1]
        for i, name in enumerate(SMALL_PARAMS):
            off, n = PACK_OFFSETS[name], sizes[name]
            g = tot_ref[:, off:off + n]
            w_ref, m_ref, v_ref = wmv[3 * i:3 * i + 3]
            outs[i][...] = g
            outs[np_ + i][...], outs[2 * np_ + i][...], outs[3 * np_ + i][...] = _adam_math(w_ref[...], g, m_ref[...], v_ref[...])
        loss_ref[...] = tot_ref[:, PACK_OFFSETS["loss"]:PACK_OFFSETS["loss"] + 1]

    flat = [a for name in SMALL_PARAMS for a in given[name]]
    shapes = [jax.ShapeDtypeStruct((1, sizes[name]), F32) for name in SMALL_PARAMS]
    res = pl.pallas_call(body, name="small_adamw", in_specs=[VMEM_SPEC] * (1 + 3 * np_), out_specs=[VMEM_SPEC] * (4 * np_ + 1),
                         out_shape=shapes * 4 + [jax.ShapeDtypeStruct((1, 1), F32)], compiler_params=_cp())(tot, *flat)
    return [dict(zip(SMALL_PARAMS, res[k * np_:(k + 1) * np_])) for k in range(4)], res[-1]


def kernel(x, c, w_ada, b_ada, g_pre, w_in, qn_g, kn_g, w_dec_f, w_dec_b, gn_g, w_pa, w_pr, w_out, g_post, loss_target, m_w_ada, m_b_ada, m_g_pre, m_w_in, m_qn_g, m_kn_g, m_w_dec_f, m_w_dec_b, m_gn_g, m_w_pa, m_w_pr, m_w_out, m_g_post, v_w_ada, v_b_ada, v_g_pre, v_w_in, v_qn_g, v_kn_g, v_w_dec_f, v_w_dec_b, v_gn_g, v_w_pa, v_w_pr, v_w_out, v_g_post):
    shards = (w_in[0].T.astype(MXU_DTYPE), jnp.concatenate([w_pa[0].T, w_pr[0].T], axis=1).astype(MXU_DTYPE),
              w_out[0].astype(MXU_DTYPE))
    cs, mod, w_attn = _mod_and_attn_rows(c, w_ada[0], b_ada.reshape(N_CHIP, 3 * D_MODEL // N_CHIP), shards[0])
    started = {}

    def rest_start(dep, *placed):
        started["rest"] = _gather_rest_start(shards, placed, dep)
        return started["rest"][-1]

    def rest_wait(after):
        return _gather_rest_wait(started["rest"], after)

    kinds = {"early": (1, 2), "late": (0,)}

    def send(name, arrays):
        started[name] = _scatter_start(arrays, kinds[name], "grad_scatter_" + name)
        return started[name][-1]

    grad_x, _, _, _, small = _local_step(x[0], loss_target[0], mod, g_pre, qn_g, kn_g, w_dec_f, w_dec_b,
                                         gn_g, g_post, w_attn, rest_start, rest_wait, send=send, own=shards)
    small = dict(small)
    small["b_ada"] = small.pop("dmod")
    given = dict(b_ada=(b_ada, m_b_ada, v_b_ada), g_pre=(g_pre, m_g_pre, v_g_pre), g_post=(g_post, m_g_post, v_g_post),
                 gn_g=(gn_g, m_gn_g, v_gn_g), qn_g=(qn_g, m_qn_g, v_qn_g), kn_g=(kn_g, m_kn_g, v_kn_g),
                 w_dec_f=(w_dec_f, m_w_dec_f, v_w_dec_f), w_dec_b=(w_dec_b, m_w_dec_b, v_w_dec_b))
    grads, deltas, new_m, new_v = {}, {}, {}, {}

    def update(call, back, **items):
        done = _adamw(list(items.values()), "adamw_" + call)
        for (name, (w, g, m, v)), (d, nm, nv) in zip(items.items(), done):
            grads[name], deltas[name], new_m[name], new_v[name] = back(g), back(d), back(nm), back(nv)
        return tuple(d for d, _, _ in done)

    lead = lambda a: a[None]
    (g_pt, g_out), (z_pt, z_out) = _scatter_wait(started["early"], kinds["early"], grad_x, "grad_scatter_early_wait")
    r_pt, r_out = _grad_finish((g_pt, g_out), (z_pt, z_out), kinds["early"], "grad_finish_early")
    early = update("early", lead, w_pa=(w_pa[0], r_pt[:, :512].T, m_w_pa[0], v_w_pa[0]),
                   w_pr=(w_pr[0], r_pt[:, 512:].T, m_w_pr[0], v_w_pr[0]), w_out=(w_out[0], r_out, m_w_out[0], v_w_out[0]))
    tot, g_w_ada = _small_reduce(_pack(small), cs.reshape(N_DEV, D_MODEL), early)
    small_parts, loss = _small_adamw(tot, given)
    for dst, part in zip((grads, deltas, new_m, new_v), small_parts):
        dst.update(part)
    (last,) = update("w_ada", lead, w_ada=(w_ada[0], g_w_ada, m_w_ada[0], v_w_ada[0]))

    (g_in_t,), (z_in,) = _scatter_wait(started["late"], kinds["late"], last, "grad_scatter_late_wait")
    (r_in,) = _grad_finish((g_in_t,), (z_in,), kinds["late"], "grad_finish_late")
    tr = lambda a: a[0].T
    update("w_in", lambda a: a.T[None], w_in=(tr(w_in), r_in, tr(m_w_in), tr(v_w_in)))
    order = ("w_ada", "b_ada", "g_pre", "w_in", "qn_g", "kn_g", "w_dec_f", "w_dec_b", "gn_g", "w_pa", "w_pr", "w_out", "g_post")
    return (loss[0, 0], grad_x[None], *[grads[n] for n in order], *[deltas[n] for n in order],
            *[new_m[n] for n in order], *[new_v[n] for n in order])
```

```python
import jax
import jax.numpy as jnp
import numpy as np
from jax import lax
from jax.experimental import pallas as pl
from jax.experimental.pallas import tpu as pltpu

F32 = jnp.float32
BF16 = jnp.bfloat16
MXU_DTYPE = jnp.bfloat16

D_MODEL = 1024
GRID_W = 64
HEAD_DIM = 64
ROPE_THETA = 10000.0
EPS = 1e-6
RET_HEADS = 4
RET_CHUNK = 256
RET_GROUP = 4
IN_WIDTH = 4864
QA, KA, VA, ZA, QR, KR, VR, ZR, GL = 0, 512, 640, 768, 1280, 1536, 1792, 2304, 2816
ATTN_COLS = ZA
ZA_B, QR_B, KR_B, VR_B, ZR_B, GL_B = (c - ATTN_COLS for c in (ZA, QR, KR, VR, ZR, GL))

ADAM_LR, ADAM_B1, ADAM_B2, ADAM_EPS, ADAM_WD, ADAM_STEP = 0.001, 0.9, 0.999, 1e-08, 0.01, 10

VMEM_LIMIT = 56 * 1024 * 1024
MESH = pl.DeviceIdType.MESH
HIGHEST = lax.Precision.HIGHEST
LOG2E = 1.4426950408889634
LN2 = 0.6931471805599453


def _cp(*sem, **kw):
    if sem:
        kw["dimension_semantics"] = sem
    return pltpu.CompilerParams(vmem_limit_bytes=VMEM_LIMIT, **kw)


def _nn(a, b, **kw):
    return lax.dot_general(a, b, (((1,), (0,)), ((), ())), preferred_element_type=F32, **kw)


def _nt(a, b):
    return lax.dot_general(a, b, (((1,), (1,)), ((), ())), preferred_element_type=F32)


def _tn(a, b):
    return lax.dot_general(a, b, (((0,), (0,)), ((), ())), preferred_element_type=F32)


def _mx(x):
    return x.astype(MXU_DTYPE)


def _lane(shape):
    return lax.broadcasted_iota(jnp.int32, shape, 1)


def _sigmoid(z):
    return 1.0 / (1.0 + jnp.exp(-z))


def _rowsum128(x):
    s = jnp.sum(x, axis=0, keepdims=True)
    out = s[:, 0:128]
    for k in range(1, x.shape[1] // 128):
        out = out + s[:, 128 * k:128 * (k + 1)]
    return out


def _swap16(x):
    return jnp.where((_lane(x.shape) & 16) == 0, pltpu.roll(x, 112, 1), pltpu.roll(x, 16, 1))


def _rope(x, cos, sin):
    return x * cos + _swap16(x) * sin


def _rope_t(d, cos, sin):
    return d * cos + _swap16(d * sin)


def _blockdiag64():
    r = lax.broadcasted_iota(jnp.int32, (128, 128), 0) // 64
    c = lax.broadcasted_iota(jnp.int32, (128, 128), 1) // 64
    return (r == c).astype(BF16)


def _seg64(x, m):
    hi = x.astype(BF16)
    lo = (x - hi.astype(F32)).astype(BF16)
    return _nn(hi, m) + _nn(lo, m)


def _rope_tables(seq):
    t = np.arange(seq)
    row = (t // GRID_W).astype(np.float32)
    col = (t % GRID_W).astype(np.float32)
    half = HEAD_DIM // 2
    inv_freq = (np.float32(ROPE_THETA) ** (-np.arange(0, half, 2, dtype=np.float32) / np.float32(half))).astype(np.float32)
    ar = (row[:, None] * inv_freq[None, :]).astype(np.float32).astype(np.float64)
    ac = (col[:, None] * inv_freq[None, :]).astype(np.float32).astype(np.float64)
    cr, sr, cc, sc = np.cos(ar), np.sin(ar), np.cos(ac), np.sin(ac)
    cos64 = np.concatenate([cr, cr, cc, cc], axis=1)
    sin64 = np.concatenate([-sr, sr, -sc, sc], axis=1)
    return jnp.asarray(np.tile(cos64, (1, 2)), F32), jnp.asarray(np.tile(sin64, (1, 2)), F32)


def _attn_inputs(x, mod, g_pre, w_attn, cos, sin, qg2, kg2, own=()):
    seq = x.shape[0]
    bs = 512
    nt = len(own)
    nstep = seq // bs

    def body(x_ref, mod_ref, g_ref, w_ref, cos_ref, sin_ref, qg_ref, kg_ref, *rest):
        srcs, (h_ref, pa_ref, qt_ref, kd_ref, vd_ref) = rest[:nt], rest[nt:nt + 5]
        lands, stage = rest[nt + 5:2 * nt + 5], rest[2 * nt + 5:3 * nt + 5]
        if nt:
            osem = rest[3 * nt + 5]
            step = pl.program_id(0)
            cx, cy, _ = _coords()
            chip = 2 * cx + cy
            loads = [pltpu.make_async_copy(srcs[t], stage[t], osem.at[t]) for t in range(nt)]
            stores = [pltpu.make_async_copy(stage[t], _rows(lands[t], chip * SHARD_ROWS[t], SHARD_ROWS[t]), osem.at[nt + t])
                      for t in range(nt)]

            @pl.when(step == 0)
            def _():
                for cp in loads:
                    cp.start()

            @pl.when(step == min(1, nstep - 1))
            def _():
                for cp in loads:
                    cp.wait()
                for cp in stores:
                    cp.start()

        xv = x_ref[...]
        r = lax.rsqrt(jnp.mean(xv * xv, axis=-1, keepdims=True) + EPS)
        shift = mod_ref[:, 0:D_MODEL]
        scale = mod_ref[:, D_MODEL:2 * D_MODEL]
        hb = ((xv * r) * g_ref[...] * (1.0 + scale) + shift).astype(h_ref.dtype)
        h_ref[...] = hb
        p = _nt(hb, w_ref[...])
        pa_ref[...] = p
        m = _blockdiag64()
        cs, sn = cos_ref[...], sin_ref[...]
        lo = _lane((bs, 128)) < 64
        for pr in range(4):
            q = p[:, QA + 128 * pr:QA + 128 * (pr + 1)]
            r = lax.rsqrt(_seg64(q * q, m) * (1.0 / 64) + EPS)
            qt_ref[:, 128 * pr:128 * (pr + 1)] = (_rope(q * r * qg_ref[...], cs, sn) * (0.125 * LOG2E)).astype(qt_ref.dtype)
        k = p[:, KA:KA + 128]
        r = lax.rsqrt(_seg64(k * k, m) * (1.0 / 64) + EPS)
        kr = _rope(k * r * kg_ref[...], cs, sn)
        ksw = pltpu.roll(kr, 64, 1)
        kd_ref[0] = jnp.where(lo, kr, ksw).astype(kd_ref.dtype)
        kd_ref[1] = jnp.where(lo, ksw, kr).astype(kd_ref.dtype)
        v = p[:, VA:VA + 128]
        vsw = pltpu.roll(v, 64, 1)
        vd_ref[0] = jnp.where(lo, v, vsw).astype(vd_ref.dtype)
        vd_ref[1] = jnp.where(lo, vsw, v).astype(vd_ref.dtype)

        if nt:
            @pl.when(step == nstep - 1)
            def _():
                for cp in stores:
                    cp.wait()

    row = lambda w: pl.BlockSpec((bs, w), lambda i: (i, 0))
    fix = lambda a, b: pl.BlockSpec((a, b), lambda i: (0, 0))
    dup = pl.BlockSpec((2, bs, 128), lambda i: (0, i, 0))
    sds = jax.ShapeDtypeStruct
    anywhere = pl.BlockSpec(memory_space=pl.ANY)
    return pl.pallas_call(
        body, grid=(nstep,), name="attn_inputs",
        in_specs=[row(D_MODEL), fix(1, 3 * D_MODEL), fix(1, D_MODEL), fix(ATTN_COLS, D_MODEL), row(128), row(128), fix(1, 128), fix(1, 128)]
        + [anywhere] * nt,
        out_specs=[row(D_MODEL), row(ATTN_COLS), row(512), dup, dup] + [anywhere] * nt,
        out_shape=[sds((seq, D_MODEL), MXU_DTYPE), sds((seq, ATTN_COLS), F32), sds((seq, 512), MXU_DTYPE),
                   sds((2, seq, 128), MXU_DTYPE), sds((2, seq, 128), MXU_DTYPE)]
        + [sds((N_CHIP * SHARD_ROWS[t], s.shape[1]), s.dtype) for t, s in enumerate(own)],
        scratch_shapes=[pltpu.VMEM(s.shape, s.dtype) for s in own] + ([pltpu.SemaphoreType.DMA((2 * nt,))] if nt else []),
        compiler_params=_cp("arbitrary"))(x, mod, g_pre, w_attn, cos, sin, qg2, kg2, *own)


def _in_proj_dx(x, dp, win_t, dout, mod, g_pre, dep=None):
    seq = x.shape[0]
    bs = 512
    deps, dep_specs = _dep_args(dep, 1)

    def body(x_ref, dp_ref, w_ref, dout_ref, mod_ref, g_ref, *rest):
        gx_ref, dshift_ref, dscale_ref, dg_ref = rest[-4:]

        @pl.when(pl.program_id(0) == 0)
        def _():
            dshift_ref[...] = jnp.zeros_like(dshift_ref)
            dscale_ref[...] = jnp.zeros_like(dscale_ref)
            dg_ref[...] = jnp.zeros_like(dg_ref)

        xv = x_ref[...]
        dh = _nn(dp_ref[...], w_ref[...])
        g = g_ref[...]
        r = lax.rsqrt(jnp.mean(xv * xv, axis=-1, keepdims=True) + EPS)
        xn = xv * r
        scale = mod_ref[:, D_MODEL:2 * D_MODEL]
        dshift_ref[...] += jnp.sum(dh, axis=0, keepdims=True)
        dscale_ref[...] += jnp.sum(dh * (xn * g), axis=0, keepdims=True)
        da = dh * (1.0 + scale)
        dg_ref[...] += jnp.sum(da * xn, axis=0, keepdims=True)
        dxn = da * g
        dx = r * (dxn - xn * jnp.mean(dxn * xn, axis=-1, keepdims=True))
        gx_ref[...] = dout_ref[...] + dx

    row = pl.BlockSpec((bs, D_MODEL), lambda i: (i, 0))
    vec = pl.BlockSpec((1, D_MODEL), lambda i: (0, 0))
    return pl.pallas_call(
        body, grid=(seq // bs,), name="in_proj_dx",
        in_specs=[row, pl.BlockSpec((bs, IN_WIDTH), lambda i: (i, 0)), pl.BlockSpec((IN_WIDTH, D_MODEL), lambda i: (0, 0)), row,
                  pl.BlockSpec((1, 3 * D_MODEL), lambda i: (0, 0)), vec] + dep_specs,
        out_specs=[row, vec, vec, vec],
        out_shape=[jax.ShapeDtypeStruct((seq, D_MODEL), F32)] + [jax.ShapeDtypeStruct((1, D_MODEL), F32)] * 3,
        compiler_params=_cp("arbitrary"))(x, dp, win_t, dout, mod, g_pre, *deps)


def _dep_args(dep, grid_rank):
    if dep is None:
        return [], []
    return [dep], [pl.BlockSpec(dep.shape, lambda *_: (0,) * dep.ndim)]


def _matmul(a, b, *, ta, tb, tm, tn, out_dtype, name, dep=None):
    kdim = a.shape[0] if ta else a.shape[1]
    m = a.shape[1] if ta else a.shape[0]
    n = b.shape[0] if tb else b.shape[1]
    assert m % tm == 0 and n % tn == 0
    deps, dep_specs = _dep_args(dep, 2)

    def body(a_ref, b_ref, *rest):
        o_ref = rest[-1]
        dims = (((0 if ta else 1,), (1 if tb else 0,)), ((), ()))
        o_ref[...] = lax.dot_general(a_ref[...], b_ref[...], dims, preferred_element_type=F32).astype(o_ref.dtype)

    a_spec = pl.BlockSpec((kdim, tm), lambda j, i: (0, i)) if ta else pl.BlockSpec((tm, kdim), lambda j, i: (i, 0))
    b_spec = pl.BlockSpec((tn, kdim), lambda j, i: (j, 0)) if tb else pl.BlockSpec((kdim, tn), lambda j, i: (0, j))
    return pl.pallas_call(
        body, grid=(n // tn, m // tm), name=name, in_specs=[a_spec, b_spec] + dep_specs,
        out_specs=pl.BlockSpec((tm, tn), lambda j, i: (i, j)),
        out_shape=jax.ShapeDtypeStruct((m, n), out_dtype), compiler_params=_cp("parallel", "parallel"))(a, b, *deps)


def _in_proj_rest(h, win_t, cos, sin):
    seq = h.shape[0]
    tm = min(1024, seq)
    ncols = IN_WIDTH - ATTN_COLS
    tn = ncols // 2
    assert ZR_B <= tn and ATTN_COLS % 128 == 0 and tn % 128 == 0

    def body(h_ref, w_ref, cos_ref, sin_ref, p_ref, rq_ref, rk_ref, rv_ref):
        p = _nt(h_ref[...], w_ref[...])
        p_ref[...] = p

        @pl.when(pl.program_id(0) == 0)
        def _():
            cs, sn = cos_ref[...], sin_ref[...]
            for pr in range(2):
                sl = slice(128 * pr, 128 * (pr + 1))
                rq_ref[:, sl] = _rope(p[:, QR_B + 128 * pr:QR_B + 128 * (pr + 1)], cs, sn).astype(rq_ref.dtype)
                rk_ref[:, sl] = (_rope(p[:, KR_B + 128 * pr:KR_B + 128 * (pr + 1)], cs, sn) * 0.125).astype(rk_ref.dtype)
            rv_ref[...] = p[:, VR_B:VR_B + 512].astype(rv_ref.dtype)

    nrow = seq // tm
    row = lambda w: pl.BlockSpec((tm, w), lambda j, i: (i, 0))
    park = lambda w: pl.BlockSpec((tm, w), lambda j, i: (jnp.where(j == 0, i, nrow - 1), 0))
    sds = jax.ShapeDtypeStruct
    return pl.pallas_call(
        body, grid=(2, nrow), name="in_proj_rest",
        in_specs=[row(D_MODEL), pl.BlockSpec((pl.Element(tn), pl.Element(D_MODEL)),
                                             lambda j, i: (pl.multiple_of(ATTN_COLS + j * tn, 128), 0)), row(128), row(128)],
        out_specs=[pl.BlockSpec((tm, tn), lambda j, i: (i, j)), park(256), park(256), park(512)],
        out_shape=[sds((seq, ncols), F32), sds((seq, 256), MXU_DTYPE), sds((seq, 256), MXU_DTYPE), sds((seq, 512), MXU_DTYPE)],
        compiler_params=_cp("arbitrary", "arbitrary"))(h, win_t, cos, sin)


def _halves(kd):
    lo = _lane(kd.shape) < 64
    z = jnp.zeros_like(kd)
    return jnp.concatenate([jnp.where(lo, kd, z), jnp.where(lo, z, kd)], axis=0)


def _attn_fwd(qt, kd, vd, dep=None):
    seq = qt.shape[0]
    bq, bk = min(2048, seq), min(1024, seq)
    nk = seq // bk
    deps, dep_specs = _dep_args(dep, 2)

    def body(q_ref, k_ref, v_ref, *rest):
        o_ref, lse_ref, m_scr, l_scr, acc = rest[-5:]
        j = pl.program_id(1)
        m_scr[...] = jnp.full_like(m_scr, -jnp.inf)
        l_scr[...] = jnp.zeros_like(l_scr)
        acc[...] = jnp.zeros_like(acc)
        sub = min(512, bq)
        nsub = bq // sub
        lo = _lane((sub, 128)) < 64

        def kv_block(n, carry):
            rows = pl.ds(pl.multiple_of(n * bk, bk), bk)
            k2, v2 = _halves(k_ref[rows, :]), _halves(v_ref[rows, :])
            for pr in range(2):
                for hf in range(nsub):
                    qr = slice(hf * sub, (hf + 1) * sub)
                    s2 = _nt(q_ref[qr, 128 * pr:128 * (pr + 1)], k2)
                    ps, alphas = [], []
                    for e in range(2):
                        g = 2 * pr + e
                        s = s2[:, e * bk:(e + 1) * bk]
                        m_prev = m_scr[g, qr]
                        m_next = jnp.maximum(m_prev, jnp.max(s, axis=1, keepdims=True))
                        alpha = jnp.exp2(m_prev - m_next)
                        pe = jnp.exp2(s - jnp.tile(m_next, (1, bk // 128)))
                        l_scr[g, qr] = alpha * l_scr[g, qr] + jnp.sum(pe, axis=1, keepdims=True)
                        m_scr[g, qr] = m_next
                        ps.append(_mx(pe))
                        alphas.append(alpha)
                    o2 = _nn(jnp.concatenate(ps, axis=1), v2)
                    sl = slice(128 * pr, 128 * (pr + 1))
                    acc[qr, sl] = acc[qr, sl] * jnp.where(lo, alphas[0], alphas[1]) + o2
            return carry

        lax.fori_loop(0, nk, kv_block, 0)
        lo_all = _lane((bq, 128)) < 64
        for pr in range(2):
            sl = slice(128 * pr, 128 * (pr + 1))
            o_ref[:, sl] = acc[:, sl] / jnp.where(lo_all, l_scr[2 * pr], l_scr[2 * pr + 1])
        for g in range(4):
            lse = jnp.transpose(m_scr[g] + jnp.log2(l_scr[g]))
            lse_ref[pl.ds(4 * j + g, 1), :] = lse[0:1, :]

    return pl.pallas_call(
        body, grid=(seq // bq, 2), name="attn_fwd",
        in_specs=[pl.BlockSpec((bq, 256), lambda i, j: (i, j)), pl.BlockSpec((None, seq, 128), lambda i, j: (j, 0, 0)),
                  pl.BlockSpec((None, seq, 128), lambda i, j: (j, 0, 0))] + dep_specs,
        out_specs=[pl.BlockSpec((bq, 256), lambda i, j: (i, j)), pl.BlockSpec((8, bq), lambda i, j: (0, i))],
        out_shape=[jax.ShapeDtypeStruct((seq, 512), F32), jax.ShapeDtypeStruct((8, seq), F32)],
        scratch_shapes=[pltpu.VMEM((4, bq, 128), F32), pltpu.VMEM((4, bq, 128), F32), pltpu.VMEM((bq, 256), F32)],
        compiler_params=_cp("parallel", "arbitrary"))(qt, kd, vd, *deps)


def _attn_bwd(qt, kd, vd, do, lse, delta, dep=None):
    seq = qt.shape[0]
    bq, bk = min(1024, seq), min(1024, seq)
    nq, nk = seq // bq, seq // bk
    deps, dep_specs = _dep_args(dep, 3)

    def body(q_ref, do_ref, k_ref, v_ref, lse_ref, del_ref, *rest):
        dq_ref, dk_ref, dv_ref = rest[-3:]
        j, i = pl.program_id(0), pl.program_id(1)
        dq_ref[...] = jnp.zeros_like(dq_ref)

        @pl.when(i == 0)
        def _():
            dk_ref[...] = jnp.zeros_like(dk_ref)
            dv_ref[...] = jnp.zeros_like(dv_ref)

        lo = _lane((bk, 128)) < 64
        big = lambda r: jnp.concatenate([jnp.broadcast_to(r[0], (bk, bq)), jnp.broadcast_to(r[1], (bk, bq))], axis=0)

        def kv_block(n, carry):
            rows = pl.ds(pl.multiple_of(n * bk, bk), bk)
            k2, v2 = _halves(k_ref[rows, :]), _halves(v_ref[rows, :])
            for pr in range(2):
                sl = slice(128 * pr, 128 * (pr + 1))
                qp, dop = q_ref[:, sl], do_ref[:, sl]
                s_t = _nt(k2, qp)
                dp_t = _nt(v2, dop)
                ls = [lse_ref[pl.ds(4 * j + 2 * pr + e, 1), :] for e in range(2)]
                dl = [del_ref[pl.ds(4 * j + 2 * pr + e, 1), :] for e in range(2)]
                p_t = jnp.exp2(s_t - big(ls))
                ds_t = _mx(p_t * (dp_t - big(dl)))
                rv = _nn(_mx(p_t), dop)
                rk = _nn(ds_t, qp) * LN2
                dv_ref[rows, :] += jnp.where(lo, rv[:bk], rv[bk:])
                dk_ref[rows, :] += jnp.where(lo, rk[:bk], rk[bk:])
                dq_ref[:, sl] += _tn(ds_t, k2)
            return carry

        lax.fori_loop(0, nk, kv_block, 0)

    return pl.pallas_call(
        body, grid=(2, nq), name="attn_bwd",
        in_specs=[pl.BlockSpec((bq, 256), lambda j, i: (i, j)), pl.BlockSpec((bq, 256), lambda j, i: (i, j)),
                  pl.BlockSpec((None, seq, 128), lambda j, i: (j, 0, 0)), pl.BlockSpec((None, seq, 128), lambda j, i: (j, 0, 0)),
                  pl.BlockSpec((8, bq), lambda j, i: (0, i)), pl.BlockSpec((8, bq), lambda j, i: (0, i))] + dep_specs,
        out_specs=[pl.BlockSpec((bq, 256), lambda j, i: (i, j)), pl.BlockSpec((None, seq, 128), lambda j, i: (j, 0, 0)),
                   pl.BlockSpec((None, seq, 128), lambda j, i: (j, 0, 0))],
        out_shape=[jax.ShapeDtypeStruct((seq, 512), F32), jax.ShapeDtypeStruct((2, seq, 128), F32),
                   jax.ShapeDtypeStruct((2, seq, 128), F32)],
        compiler_params=_cp("arbitrary", "arbitrary"))(qt, do, kd, vd, lse, delta, *deps)


def _ret_tables(lg_f, lg_b, c):
    idx = jnp.arange(c, dtype=F32)
    diff = idx[:, None] - idx[None, :]
    a, b = lg_f[:, None, None], lg_b[:, None, None]
    low, up = (diff >= 0)[None], (diff < 0)[None]
    dmat = jnp.where(low, jnp.exp(a * jnp.maximum(diff, 0.0)[None]), jnp.exp(b * jnp.maximum(-diff, 0.0)[None]))
    dda = jnp.where(low, diff[None] * jnp.exp(a * jnp.maximum(diff, 0.0)[None]), 0.0)
    ddb = jnp.where(up, -diff[None] * jnp.exp(b * jnp.maximum(-diff, 0.0)[None]), 0.0)
    rep = lambda w: jnp.broadcast_to(w[:, :, None], (RET_HEADS, c, 128))
    wqf = rep(jnp.exp(lg_f[:, None] * (idx + 1.0)[None]))
    wqb = rep(jnp.exp(lg_b[:, None] * (c - idx)[None]))
    wkf = rep(jnp.exp(lg_f[:, None] * (c - 1.0 - idx)[None]))
    wkb = rep(jnp.exp(lg_b[:, None] * idx[None]))
    cdf, cdb = jnp.exp(lg_f * c), jnp.exp(lg_b * c)
    dec_fb = jnp.broadcast_to(jnp.concatenate([cdf, cdb])[:, None], (8, 128))
    dec_bf = jnp.broadcast_to(jnp.concatenate([cdb, cdf])[:, None], (8, 128))
    return dict(dmat=dmat, dda=dda, ddb=ddb, wqf=wqf, wqb=wqb, wkf=wkf, wkb=wkb, dec_fb=dec_fb, dec_bf=dec_bf)


def _ret_states(xk, yv, w_fwd, w_rev, dec, name):
    seq = xk.shape[0]
    c = RET_CHUNK
    nc = seq // c
    grp = min(RET_GROUP, nc)
    ns = nc // grp

    def body(xf_ref, yf_ref, xr_ref, yr_ref, wf_ref, wr_ref, dec_ref, sf_ref, sr_ref, st_f, st_r):
        @pl.when(pl.program_id(0) == 0)
        def _():
            st_f[...] = jnp.zeros_like(st_f)
            st_r[...] = jnp.zeros_like(st_r)

        lane = _lane((c, 128))

        def chunk(g, carry):
            for x_ref, y_ref, w_ref, s_ref, st, base, gg in ((xf_ref, yf_ref, wf_ref, sf_ref, st_f, 0, g),
                                                             (xr_ref, yr_ref, wr_ref, sr_ref, st_r, 4, grp - 1 - g)):
                rows = pl.ds(pl.multiple_of(gg * c, c), c)
                for h in range(RET_HEADS):
                    pr, e = h // 2, h % 2
                    half = (lane < 64) if e == 0 else (lane >= 64)
                    s_ref[gg, h] = st[h].astype(s_ref.dtype)
                    xm = jnp.where(half, x_ref[rows, 128 * pr:128 * (pr + 1)].astype(F32) * w_ref[h], 0.0)
                    st[h] = st[h] * dec_ref[base + h:base + h + 1, :] + _tn(_mx(xm), _mx(y_ref[rows, 128 * h:128 * (h + 1)]))
            return carry

        lax.fori_loop(0, grp, chunk, 0, unroll=True)

    xs = lambda f: pl.BlockSpec((grp * c, 256), f)
    ys = lambda f: pl.BlockSpec((grp * c, 512), f)
    fwd, rev = (lambda n: (n, 0)), (lambda n: (ns - 1 - n, 0))
    wsp = pl.BlockSpec((RET_HEADS, c, 128), lambda n: (0, 0, 0))
    return pl.pallas_call(
        body, grid=(ns,), name=name,
        in_specs=[xs(fwd), ys(fwd), xs(rev), ys(rev), wsp, wsp, pl.BlockSpec((8, 128), lambda n: (0, 0))],
        out_specs=[pl.BlockSpec((grp, RET_HEADS, 128, 128), lambda n: (n, 0, 0, 0)),
                   pl.BlockSpec((grp, RET_HEADS, 128, 128), lambda n: (ns - 1 - n, 0, 0, 0))],
        out_shape=[jax.ShapeDtypeStruct((nc, RET_HEADS, 128, 128), MXU_DTYPE)] * 2,
        scratch_shapes=[pltpu.VMEM((RET_HEADS, 128, 128), F32), pltpu.VMEM((RET_HEADS, 128, 128), F32)],
        compiler_params=_cp("arbitrary"))(xk, yv, xk, yv, w_fwd, w_rev, dec)


def _ret_fwd(rq, rk, rv, rf, rb, tb):
    seq = rq.shape[0]
    c = RET_CHUNK
    grp = min(RET_GROUP, seq // c)

    def body(q_ref, k_ref, v_ref, rf_ref, rb_ref, d_ref, wqf_ref, wqb_ref, o_ref):
        lane = _lane((c, 128))

        def chunk(g, carry):
            rows = pl.ds(pl.multiple_of(g * c, c), c)
            for h in range(RET_HEADS):
                pr, e = h // 2, h % 2
                half = (lane < 64) if e == 0 else (lane >= 64)
                sl = slice(128 * pr, 128 * (pr + 1))
                qp, kp = q_ref[rows, sl], k_ref[rows, sl]
                km = jnp.where(half, kp, jnp.zeros_like(kp))
                sd = _mx(_nt(qp, km) * d_ref[h])
                qf = qp.astype(F32)
                o = _nn(sd, v_ref[rows, 128 * h:128 * (h + 1)])
                o = o + _nn(_mx(qf * wqf_ref[h]), rf_ref[g, h]) + _nn(_mx(qf * wqb_ref[h]), rb_ref[g, h])
                o_ref[rows, 128 * h:128 * (h + 1)] = o
            return carry

        lax.fori_loop(0, grp, chunk, 0, unroll=True)

    st = pl.BlockSpec((grp, RET_HEADS, 128, 128), lambda n: (n, 0, 0, 0))
    wsp = pl.BlockSpec((RET_HEADS, c, 128), lambda n: (0, 0, 0))
    return pl.pallas_call(
        body, grid=(seq // (grp * c),), name="ret_fwd",
        in_specs=[pl.BlockSpec((grp * c, 256), lambda n: (n, 0)), pl.BlockSpec((grp * c, 256), lambda n: (n, 0)),
                  pl.BlockSpec((grp * c, 512), lambda n: (n, 0)), st, st, pl.BlockSpec((RET_HEADS, c, c), lambda n: (0, 0, 0)), wsp, wsp],
        out_specs=pl.BlockSpec((grp * c, 512), lambda n: (n, 0)),
        out_shape=jax.ShapeDtypeStruct((seq, 512), F32), compiler_params=_cp("parallel"))(
            rq, rk, rv, rf, rb, tb["dmat"], tb["wqf"], tb["wqb"])


def _ret_bwd(rq, rk, rv, do, rf, rb, hf, hb, tb):
    seq = rq.shape[0]
    c = RET_CHUNK
    grp = min(RET_GROUP, seq // c)

    def body(q_ref, k_ref, v_ref, do_ref, rf_ref, rb_ref, hf_ref, hb_ref, d_ref, dda_ref, ddb_ref,
             wqf_ref, wqb_ref, wkf_ref, wkb_ref, cdec_ref, dq_ref, dk_ref, dv_ref, dlg_ref):
        @pl.when(pl.program_id(0) == 0)
        def _():
            dlg_ref[...] = jnp.zeros_like(dlg_ref)

        lane = _lane((c, 128))
        pos = lax.broadcasted_iota(jnp.int32, (c, 128), 0).astype(F32)

        def chunk(g, carry):
            rows = pl.ds(pl.multiple_of(g * c, c), c)
            for pr in range(2):
                sl = slice(128 * pr, 128 * (pr + 1))
                qp, kp = q_ref[rows, sl], k_ref[rows, sl]
                qf, kf = qp.astype(F32), kp.astype(F32)
                dq_acc = jnp.zeros((c, 128), F32)
                dk_acc = jnp.zeros((c, 128), F32)
                for e in range(2):
                    h = 2 * pr + e
                    half = (lane < 64) if e == 0 else (lane >= 64)
                    hs = slice(128 * h, 128 * (h + 1))
                    km = jnp.where(half, kp, jnp.zeros_like(kp))
                    kmf = km.astype(F32)
                    vh, doh = v_ref[rows, hs], do_ref[rows, hs]
                    rfh, rbh, hfh, hbh = rf_ref[g, h], rb_ref[g, h], hf_ref[g, h], hb_ref[g, h]
                    s = _nt(qp, km)
                    dpm = _nt(doh, vh)
                    ds_b = _mx(dpm * d_ref[h])
                    sd_b = _mx(s * d_ref[h])
                    dq_if = wqf_ref[h] * _nt(doh, rfh)
                    dq_ib = wqb_ref[h] * _nt(doh, rbh)
                    dq_acc = dq_acc + _nn(ds_b, km) + dq_if + dq_ib
                    dk_sf = wkf_ref[h] * _nt(vh, hfh)
                    dk_sb = wkb_ref[h] * _nt(vh, hbh)
                    dk_acc = dk_acc + jnp.where(half, _tn(ds_b, qp), 0.0) + dk_sf + dk_sb
                    dv = _tn(sd_b, doh) + _nn(_mx(kmf * wkf_ref[h]), hfh) + _nn(_mx(kmf * wkb_ref[h]), hbh)
                    dv_ref[rows, hs] = dv.astype(dv_ref.dtype)
                    sdp = s * dpm
                    hr_f = hfh.astype(F32) * rfh.astype(F32)
                    hr_b = hbh.astype(F32) * rbh.astype(F32)
                    da = (_rowsum128(sdp * dda_ref[h]) + _rowsum128((pos + 1.0) * qf * dq_if)
                          + _rowsum128((c - 1.0 - pos) * kf * dk_sf) + cdec_ref[h:h + 1, :] * _rowsum128(hr_f))
                    db = (_rowsum128(sdp * ddb_ref[h]) + _rowsum128((c - pos) * qf * dq_ib)
                          + _rowsum128(pos * kf * dk_sb) + cdec_ref[4 + h:5 + h, :] * _rowsum128(hr_b))
                    dlg_ref[h:h + 1, :] += da
                    dlg_ref[4 + h:5 + h, :] += db
                dq_ref[rows, sl] = dq_acc
                dk_ref[rows, sl] = dk_acc
            return carry

        lax.fori_loop(0, grp, chunk, 0, unroll=2)

    st = pl.BlockSpec((grp, RET_HEADS, 128, 128), lambda n: (n, 0, 0, 0))
    wsp = pl.BlockSpec((RET_HEADS, c, 128), lambda n: (0, 0, 0))
    dsp = pl.BlockSpec((RET_HEADS, c, c), lambda n: (0, 0, 0))
    x256 = pl.BlockSpec((grp * c, 256), lambda n: (n, 0))
    x512 = pl.BlockSpec((grp * c, 512), lambda n: (n, 0))
    cdec = tb["dec_fb"] * float(c)
    return pl.pallas_call(
        body, grid=(seq // (grp * c),), name="ret_bwd",
        in_specs=[x256, x256, x512, x512, st, st, st, st, dsp, dsp, dsp, wsp, wsp, wsp, wsp,
                  pl.BlockSpec((8, 128), lambda n: (0, 0))],
        out_specs=[x256, x256, x512, pl.BlockSpec((8, 128), lambda n: (0, 0))],
        out_shape=[jax.ShapeDtypeStruct((seq, 256), F32), jax.ShapeDtypeStruct((seq, 256), F32),
                   jax.ShapeDtypeStruct((seq, 512), MXU_DTYPE), jax.ShapeDtypeStruct((8, 128), F32)],
        compiler_params=_cp("arbitrary"))(
            rq, rk, rv, do, rf, rb, hf, hb, tb["dmat"], tb["dda"], tb["ddb"], tb["wqf"], tb["wqb"], tb["wkf"], tb["wkb"], cdec)


def _tail(x, tgt, o_att, o_ret, p, mod, g_post, gn_g, wpt, wout):
    seq = x.shape[0]
    bs = 256
    nb = seq // bs

    def body(x_ref, t_ref, oa_ref, or_ref, za_ref, zr_ref, gl_ref, mod_ref, gp_ref, gn_ref, wpt_ref, wout_ref,
             dout_ref, dza_ref, dzr_ref, dgl_ref, doa_ref, dor_ref, del_ref, gout_ref, gpt_ref,
             dgate_ref, dgpost_ref, dgn_ref, loss_ref, gout_acc, gpt_acc):
        i = pl.program_id(0)

        @pl.when(i == 0)
        def _():
            gout_acc[...] = jnp.zeros_like(gout_acc)
            gpt_acc[...] = jnp.zeros_like(gpt_acc)
            dgate_ref[...] = jnp.zeros_like(dgate_ref)
            dgpost_ref[...] = jnp.zeros_like(dgpost_ref)
            dgn_ref[...] = jnp.zeros_like(dgn_ref)
            loss_ref[...] = jnp.zeros_like(loss_ref)

        oa, za, zr = oa_ref[...], za_ref[...], zr_ref[...]
        sga, sgr = _sigmoid(za), _sigmoid(zr)
        sza, szr = za * sga, zr * sgr
        ya_b = _mx(oa * sza)
        ons, rstds = [], []
        for h in range(RET_HEADS):
            oh = or_ref[:, 128 * h:128 * (h + 1)]
            xc = oh - jnp.mean(oh, axis=-1, keepdims=True)
            rstd = lax.rsqrt(jnp.mean(xc * xc, axis=-1, keepdims=True) + EPS)
            ons.append(xc * rstd)
            rstds.append(rstd)
        on = jnp.concatenate(ons, axis=1)
        yn = on * gn_ref[...]
        yr_b = _mx(yn * szr)
        wpa_t, wpr_t = wpt_ref[:, 0:512], wpt_ref[:, 512:1024]
        a_att = _nt(ya_b, wpa_t)
        a_ret = _nt(yr_b, wpr_t)
        gts = _sigmoid(gl_ref[...])
        g_att, g_ret = gts[:, 0:D_MODEL], gts[:, D_MODEL:2 * D_MODEL]
        merged_b = _mx(g_att * a_att + g_ret * a_ret)
        u = _nn(merged_b, wout_ref[...])
        r = lax.rsqrt(jnp.mean(u * u, axis=-1, keepdims=True) + EPS)
        un = u * r
        gpost = gp_ref[...]
        y = un * gpost
        gate = mod_ref[:, 2 * D_MODEL:3 * D_MODEL]
        err = x_ref[...] + gate * y - t_ref[...]
        loss_ref[...] += (0.5 / D_MODEL) * _rowsum128(err * err)
        dout = err * (1.0 / D_MODEL)
        dout_ref[...] = dout
        dgate_ref[...] += jnp.sum(dout * y, axis=0, keepdims=True)
        dy = dout * gate
        dgpost_ref[...] += jnp.sum(dy * un, axis=0, keepdims=True)
        dun = dy * gpost
        du_b = _mx(r * (dun - un * jnp.mean(dun * un, axis=-1, keepdims=True)))
        dmerged = _nt(du_b, wout_ref[...])
        gout_acc[...] += _tn(merged_b, du_b)
        d_att, d_ret = dmerged * g_att, dmerged * g_ret
        dgl_ref[:, 0:D_MODEL] = (d_att * a_att * (1.0 - g_att)).astype(dgl_ref.dtype)
        dgl_ref[:, D_MODEL:2 * D_MODEL] = (d_ret * a_ret * (1.0 - g_ret)).astype(dgl_ref.dtype)
        d_att_b, d_ret_b = _mx(d_att), _mx(d_ret)
        dya = _nn(d_att_b, wpa_t)
        dyr = _nn(d_ret_b, wpr_t)
        gpt_acc[:, 0:512] += _tn(d_att_b, ya_b)
        gpt_acc[:, 512:1024] += _tn(d_ret_b, yr_b)
        dza_ref[...] = (dya * oa * (sga * (1.0 + za * (1.0 - sga)))).astype(dza_ref.dtype)
        doa = dya * sza
        doa_ref[...] = doa.astype(doa_ref.dtype)
        dt = jnp.transpose(doa * oa)
        del_ref[...] = jnp.sum(dt.reshape(8, HEAD_DIM, bs), axis=1)
        dzr_ref[...] = (dyr * yn * (sgr * (1.0 + zr * (1.0 - sgr)))).astype(dzr_ref.dtype)
        dyn = dyr * szr
        dgn_ref[...] += jnp.sum(dyn * on, axis=0, keepdims=True)
        don = dyn * gn_ref[...]
        for h in range(RET_HEADS):
            hs = slice(128 * h, 128 * (h + 1))
            dh, oh = don[:, hs], ons[h]
            doh = rstds[h] * (dh - jnp.mean(dh, axis=-1, keepdims=True) - oh * jnp.mean(dh * oh, axis=-1, keepdims=True))
            dor_ref[:, hs] = doh.astype(dor_ref.dtype)

        @pl.when(i == nb - 1)
        def _():
            gout_ref[...] = gout_acc[...].astype(gout_ref.dtype)
            gpt_ref[...] = gpt_acc[...].astype(gpt_ref.dtype)

    row = lambda w: pl.BlockSpec((bs, w), lambda i: (i, 0))
    el = lambda w, off: pl.BlockSpec((pl.Element(bs), pl.Element(w)), lambda i: (i * bs, off))
    vec = lambda w: pl.BlockSpec((1, w), lambda i: (0, 0))
    full = pl.BlockSpec((D_MODEL, D_MODEL), lambda i: (0, 0))
    sds = jax.ShapeDtypeStruct
    return pl.pallas_call(
        body, grid=(nb,), name="tail",
        in_specs=[row(D_MODEL), row(D_MODEL), row(512), row(512), el(512, ZA_B), el(512, ZR_B), el(2 * D_MODEL, GL_B),
                  vec(3 * D_MODEL), vec(D_MODEL), vec(512), full, full],
        out_specs=[row(D_MODEL), row(512), row(512), row(2 * D_MODEL), row(512), row(512),
                   pl.BlockSpec((8, bs), lambda i: (0, i)), full, full, vec(D_MODEL), vec(D_MODEL), vec(512), vec(128)],
        out_shape=[sds((seq, D_MODEL), F32), sds((seq, 512), MXU_DTYPE), sds((seq, 512), MXU_DTYPE),
                   sds((seq, 2 * D_MODEL), MXU_DTYPE), sds((seq, 512), MXU_DTYPE), sds((seq, 512), MXU_DTYPE),
                   sds((8, seq), F32), sds((D_MODEL, D_MODEL), MXU_DTYPE), sds((D_MODEL, D_MODEL), MXU_DTYPE),
                   sds((1, D_MODEL), F32), sds((1, D_MODEL), F32), sds((1, 512), F32), sds((1, 128), F32)],
        scratch_shapes=[pltpu.VMEM((D_MODEL, D_MODEL), F32), pltpu.VMEM((D_MODEL, D_MODEL), F32)],
        compiler_params=_cp("arbitrary"))(x, tgt, o_att, o_ret, p, p, p, mod, g_post, gn_g, wpt, wout)


def _assemble(p, cos, sin, qg2, kg2, dqt, dkp, dvp, dza, drq, drk, drv, dzr, dgl):
    seq = p.shape[0]
    bs = 512

    def body(p_ref, cos_ref, sin_ref, qg_ref, kg_ref, dqt_ref, dkp_ref, dvp_ref, dza_ref, drq_ref, drk_ref, drv_ref,
             dzr_ref, dgl_ref, dp_ref, dqg_ref, dkg_ref):
        @pl.when(pl.program_id(0) == 0)
        def _():
            dqg_ref[...] = jnp.zeros_like(dqg_ref)
            dkg_ref[...] = jnp.zeros_like(dkg_ref)

        m = _blockdiag64()
        cs, sn = cos_ref[...], sin_ref[...]
        lo = _lane((bs, 128)) < 64

        def norm_bwd(raw, dn, g, dg_ref):
            r = lax.rsqrt(_seg64(raw * raw, m) * (1.0 / 64) + EPS)
            xn = raw * r
            dg_ref[...] += jnp.sum(dn * xn, axis=0, keepdims=True)
            dxn = dn * g
            return r * (dxn - xn * (_seg64(dxn * xn, m) * (1.0 / 64)))

        for pr in range(4):
            sl = slice(128 * pr, 128 * (pr + 1))
            dn = _rope_t(dqt_ref[:, sl] * 0.125, cs, sn)
            dp_ref[:, QA + 128 * pr:QA + 128 * (pr + 1)] = norm_bwd(p_ref[:, sl], dn, qg_ref[...], dqg_ref).astype(dp_ref.dtype)
        fold = lambda a: a + pltpu.roll(a, 64, 1)
        dk = jnp.where(lo, fold(dkp_ref[0]), fold(dkp_ref[1]))
        dp_ref[:, KA:KA + 128] = norm_bwd(p_ref[:, KA:KA + 128], _rope_t(dk, cs, sn), kg_ref[...], dkg_ref).astype(dp_ref.dtype)
        dp_ref[:, VA:VA + 128] = jnp.where(lo, fold(dvp_ref[0]), fold(dvp_ref[1])).astype(dp_ref.dtype)
        dp_ref[:, ZA:ZA + 512] = dza_ref[...]
        for pr in range(2):
            sl = slice(128 * pr, 128 * (pr + 1))
            dp_ref[:, QR + 128 * pr:QR + 128 * (pr + 1)] = _rope_t(drq_ref[:, sl], cs, sn).astype(dp_ref.dtype)
            dp_ref[:, KR + 128 * pr:KR + 128 * (pr + 1)] = _rope_t(drk_ref[:, sl] * 0.125, cs, sn).astype(dp_ref.dtype)
        dp_ref[:, VR:VR + 512] = drv_ref[...]
        dp_ref[:, ZR:ZR + 512] = dzr_ref[...]
        dp_ref[:, GL:GL + 2 * D_MODEL] = dgl_ref[...]

    row = lambda w: pl.BlockSpec((bs, w), lambda i: (i, 0))
    gsp = pl.BlockSpec((1, 128), lambda i: (0, 0))
    dup = pl.BlockSpec((2, bs, 128), lambda i: (0, i, 0))
    return pl.pallas_call(
        body, grid=(seq // bs,), name="assemble_dp",
        in_specs=[row(768), row(128), row(128), gsp, gsp, row(512), dup, dup, row(512), row(256), row(256), row(512),
                  row(512), row(2 * D_MODEL)],
        out_specs=[row(IN_WIDTH), gsp, gsp],
        out_shape=[jax.ShapeDtypeStruct((seq, IN_WIDTH), MXU_DTYPE), jax.ShapeDtypeStruct((1, 128), F32),
                   jax.ShapeDtypeStruct((1, 128), F32)],
        compiler_params=_cp("arbitrary"))(p, cos, sin, qg2, kg2, dqt, dkp, dvp, dza, drq, drk, drv, dzr, dgl)


def _local_step(x, tgt, mod, g_pre, qn_g, kn_g, w_dec_f, w_dec_b, gn_g, g_post, w_attn, rest_start, rest_wait, send=None, own=()):
    send = send or (lambda name, arrays: None)
    seq = x.shape[0]
    cos, sin = _rope_tables(seq)
    qg2, kg2 = jnp.tile(qn_g, (1, 2)), jnp.tile(kn_g, (1, 2))
    lg_f = jax.nn.log_sigmoid(w_dec_f[0])
    lg_b = jax.nn.log_sigmoid(w_dec_b[0])
    tb = _ret_tables(lg_f, lg_b, RET_CHUNK)

    h, p_a, qt, kd, vd, *placed = _attn_inputs(x, mod, g_pre, w_attn, cos, sin, qg2, kg2, own=own)
    o_att, lse = _attn_fwd(qt, kd, vd, dep=rest_start(qt, *placed))
    win_t, wpt, wout = rest_wait(o_att)
    p_b, rq, rk, rv = _in_proj_rest(h, win_t, cos, sin)
    rf, rb = _ret_states(rk, rv, tb["wkf"], tb["wkb"], tb["dec_fb"], "ret_states_fwd")
    o_ret = _ret_fwd(rq, rk, rv, rf, rb, tb)
    (dout, dza, dzr, dgl, do_att, do_ret, delta, g_out, g_pt, dgate, dgpost, dgn, loss_l) = _tail(
        x, tgt, o_att, o_ret, p_b, mod, g_post, gn_g, wpt, wout)
    dep = send("early", (g_pt, g_out))
    dqt, dkp, dvp = _attn_bwd(qt, kd, vd, do_att, lse, delta, dep=dep)
    hb, hf = _ret_states(rq, do_ret, tb["wqb"], tb["wqf"], tb["dec_bf"], "ret_states_bwd")
    drq, drk, drv, dlg = _ret_bwd(rq, rk, rv, do_ret, rf, rb, hf, hb, tb)
    dp, dqg, dkg = _assemble(p_a, cos, sin, qg2, kg2, dqt, dkp, dvp, dza, drq, drk, drv, dzr, dgl)
    g_in_t = _matmul(dp, h, ta=True, tb=False, tm=256, tn=D_MODEL, out_dtype=MXU_DTYPE, name="in_proj_dw")
    dep = send("late", (g_in_t,))
    grad_x, dshift, dscale, dgpre = _in_proj_dx(x, dp, win_t, dout, mod, g_pre, dep=dep)

    dlg = jnp.sum(dlg, axis=1)
    small = dict(
        dmod=jnp.concatenate([dshift, dscale, dgate], axis=1),
        g_pre=dgpre, g_post=dgpost, gn_g=dgn,
        qn_g=dqg[:, :64] + dqg[:, 64:], kn_g=dkg[:, :64] + dkg[:, 64:],
        w_dec_f=(dlg[0:4] * jax.nn.sigmoid(-w_dec_f[0]))[None], w_dec_b=(dlg[4:8] * jax.nn.sigmoid(-w_dec_b[0]))[None],
        loss=jnp.sum(loss_l, axis=1, keepdims=True))
    return grad_x, g_in_t, g_pt, g_out, small


N_DEV = 8
N_CHIP = 4
HBM_SPEC = pl.BlockSpec(memory_space=pl.ANY)
VMEM_SPEC = pl.BlockSpec(memory_space=pltpu.VMEM)
SHARD_ROWS = (IN_WIDTH // N_CHIP, D_MODEL // N_CHIP, D_MODEL // N_CHIP)


def _coords():
    return lax.axis_index("x"), lax.axis_index("y"), lax.axis_index("c")


def _peer(k):
    x, y, c = _coords()
    return (1 - x if k & 4 else x, 1 - y if k & 2 else y, 1 - c if k & 1 else c)


def _dev_index(p):
    return 4 * p[0] + 2 * p[1] + p[2]


def _rows(ref, start, size):
    return ref.at[pl.ds(pl.multiple_of(start, 16), size), :]


def _remote(src, dst, ssem, rsem, dev):
    return pltpu.make_async_remote_copy(src_ref=src, dst_ref=dst, send_sem=ssem, recv_sem=rsem, device_id=dev,
                                        device_id_type=MESH)


ATTN_CHUNK = 32


def _mod_and_attn_rows(c, w_ada_s, b4, win_s):
    ncol = w_ada_s.shape[1]
    half = ATTN_COLS // 2
    offs = list(range(0, half, ATTN_CHUNK))
    nq = len(offs)
    rows = lambda ref, cc, off: ref.at[pl.ds(pl.multiple_of(cc * half + off, 16), ATTN_CHUNK), :]

    def body(c_ref, w_ref, b_ref, src, cs_ref, mod_ref, out, modsh, modall, ssem, rsem, lsem, asem, bsem, fsem, gsem, hsem):
        x, y, cc = _coords()
        me = _dev_index((x, y, cc))
        chip = me // 2
        sib = (x, y, 1 - cc)
        cs_ref[me] = c_ref[...]
        sends = []
        for k in range(1, N_DEV):
            cp = _remote(c_ref, cs_ref.at[me], ssem.at[k - 1], rsem.at[k - 1], _peer(k))
            cp.start()
            sends.append(cp)
        own = pltpu.make_async_copy(src.at[pl.ds(0, ATTN_COLS), :], out, lsem.at[0])
        bulk = [_remote(rows(src, cc, off), rows(out, cc, off), asem.at[(k2 - 1) * nq + q], bsem.at[q], (k2 // 2, k2 % 2, cc))
                for k2 in (1, 2) for q, off in enumerate(offs)]
        relay_chip = lambda q: 1 + q % 2

        @pl.when(chip == 0)
        def _():
            own.start()
            for cp in bulk:
                cp.start()

        for k in range(1, N_DEV):
            slot = cs_ref.at[_dev_index(_peer(k))]
            _remote(slot, slot, ssem.at[k - 1], rsem.at[k - 1], _peer(k)).wait_recv()
        cs = jnp.concatenate([cs_ref[d] for d in range(N_DEV)], axis=0)
        ms = _nn(cs * _sigmoid(cs), w_ref[...], precision=HIGHEST) + b_ref[pl.ds(chip, 1), :]
        modsh[...] = ms
        modall[chip] = ms
        for k2 in range(1, N_CHIP):
            cp = _remote(modsh, modall.at[chip], ssem.at[6 + k2], rsem.at[6 + k2], _peer(2 * k2))
            cp.start()
            sends.append(cp)

        @pl.when(chip != 0)
        def _():
            passed = []
            relays = [_remote(rows(out, cc, off), rows(out, cc, off), hsem.at[q], bsem.at[q], (1, 1, cc)) for q, off in enumerate(offs)]
            for q, off in enumerate(offs):
                got = rows(out, cc, off)
                _remote(got, got, asem.at[q], bsem.at[q], (0, 0, cc)).wait_recv()
                cp = _remote(got, got, fsem.at[q], gsem.at[q], sib)
                cp.start()
                passed.append(cp)
                pl.when(chip == relay_chip(q))(relays[q].start)
            for q, off in enumerate(offs):
                got = rows(out, 1 - cc, off)
                _remote(got, got, fsem.at[q], gsem.at[q], sib).wait_recv()
            for cp in passed:
                cp.wait_send()
            for q in range(nq):
                pl.when(chip == relay_chip(q))(relays[q].wait_send)

        for k2 in range(1, N_CHIP):
            slot = modall.at[_dev_index(_peer(2 * k2)) // 2]
            _remote(slot, slot, ssem.at[6 + k2], rsem.at[6 + k2], _peer(2 * k2)).wait_recv()
        for jj in range(N_CHIP):
            mod_ref[:, ncol * jj:ncol * (jj + 1)] = modall[jj, pl.ds(me, 1), :]
        for cp in sends:
            cp.wait_send()

        @pl.when(chip == 0)
        def _():
            for cp in bulk:
                cp.wait_send()
            own.wait()

    dma = pltpu.SemaphoreType.DMA
    return pl.pallas_call(
        body, name="mod_and_attn_rows", in_specs=[VMEM_SPEC] * 4, out_specs=[VMEM_SPEC, VMEM_SPEC, HBM_SPEC],
        out_shape=[jax.ShapeDtypeStruct((N_DEV, 1, D_MODEL), F32), jax.ShapeDtypeStruct((1, N_CHIP * ncol), F32),
                   jax.ShapeDtypeStruct((ATTN_COLS, win_s.shape[1]), win_s.dtype)],
        scratch_shapes=[pltpu.VMEM((N_DEV, ncol), F32), pltpu.VMEM((N_CHIP, N_DEV, ncol), F32), dma((10,)), dma((10,)),
                        dma((1,)), dma((2 * nq,)), dma((nq,)), dma((nq,)), dma((nq,)), dma((nq,))],
        compiler_params=_cp())(c, w_ada_s, b4, win_s)


GATHER_CHUNK = 32


def _chunks(kinds, chunk):
    out = []
    for t, kind in enumerate(kinds):
        half = SHARD_ROWS[kind] // 2
        step = chunk if half % chunk == 0 else half
        out += [(t, off, step) for off in range(0, half, step)]
    return out


SEM_SPEC = pl.BlockSpec(memory_space=pltpu.SEMAPHORE)
HBM_ONLY = pl.BlockSpec(memory_space=pltpu.HBM)
DATAFLOW = pltpu.SideEffectType.DATAFLOW_SIDE_EFFECTING


def _gather_rest_start(shards, zones, dep):
    n = len(shards)

    def body(*refs):
        srcs, lands = refs[:n], refs[n:2 * n]
        ssem, rsem = refs[2 * n + 1], refs[2 * n + 2]
        token = refs[-1]
        x, y, _ = _coords()
        chip = 2 * x + y
        for k2 in range(1, N_CHIP):
            for t in range(n):
                q = (k2 - 1) * n + t
                _remote(srcs[t], _rows(lands[t], chip * SHARD_ROWS[t], SHARD_ROWS[t]), ssem.at[q], rsem.at[q], _peer(2 * k2)).start()
        token[...] = jnp.zeros_like(token)

    hbm = lambda a: pltpu.with_memory_space_constraint(a, pltpu.HBM)
    dma = pltpu.SemaphoreType.DMA
    outs = pl.pallas_call(
        body, name="gather_rest", in_specs=[HBM_ONLY] * (2 * n) + [HBM_SPEC],
        out_specs=[SEM_SPEC, SEM_SPEC] + [HBM_ONLY] * (2 * n) + [VMEM_SPEC],
        out_shape=[dma((3 * n,)), dma((3 * n,))] + [pltpu.HBM(a.shape, a.dtype) for a in list(shards) + list(zones)]
        + [jax.ShapeDtypeStruct((8, 128), F32)],
        input_output_aliases={i: 2 + i for i in range(2 * n)},
        compiler_params=pltpu.CompilerParams(has_side_effects=DATAFLOW))(*[hbm(a) for a in list(shards) + list(zones)], dep)
    return outs[0], outs[1], outs[2:2 + n], outs[2 + n:2 + 2 * n], outs[-1]


def _gather_rest_wait(started, after):
    ssem, rsem, shards, zones, _ = started
    n = len(shards)

    def body(*refs):
        srcs, lands = refs[:n], refs[n:2 * n]
        ssem_ref, rsem_ref = refs[2 * n], refs[2 * n + 1]
        for k2 in range(1, N_CHIP):
            pchip = _dev_index(_peer(2 * k2)) // 2
            for t in range(n):
                q = (k2 - 1) * n + t
                cp = _remote(srcs[t], _rows(lands[t], pchip * SHARD_ROWS[t], SHARD_ROWS[t]), ssem_ref.at[q], rsem_ref.at[q], _peer(2 * k2))
                cp.wait_send()
                cp.wait_recv()

    outs = pl.pallas_call(
        body, name="gather_rest_wait", in_specs=[HBM_ONLY] * (2 * n) + [SEM_SPEC, SEM_SPEC, HBM_SPEC], out_specs=[HBM_ONLY] * (2 * n),
        out_shape=[pltpu.HBM(a.shape, a.dtype) for a in list(shards) + list(zones)],
        input_output_aliases={i: i for i in range(2 * n)},
        compiler_params=pltpu.CompilerParams(has_side_effects=DATAFLOW))(*shards, *zones, ssem, rsem, after)
    return outs[n:]


def _reduced_by(ref, t, dev):
    return _rows(ref, (dev // 2) * SHARD_ROWS[t] + (dev % 2) * (SHARD_ROWS[t] // 2), SHARD_ROWS[t] // 2)


def _scatter_start(grads, kinds, name):
    n = len(grads)

    def body(*refs):
        srcs, lands = refs[:n], refs[n:2 * n]
        ssem, rsem = refs[2 * n], refs[2 * n + 1]
        token = refs[-1]
        me = _dev_index(_coords())
        for k in range(1, N_DEV):
            for i, t in enumerate(kinds):
                q = (k - 1) * n + i
                _remote(_reduced_by(srcs[i], t, _dev_index(_peer(k))), lands[i].at[me], ssem.at[q], rsem.at[q], _peer(k)).start()
        token[...] = jnp.zeros_like(token)

    zones = [lax.empty((N_DEV, SHARD_ROWS[t] // 2, g.shape[1]), g.dtype) for g, t in zip(grads, kinds)]
    hbm = lambda a: pltpu.with_memory_space_constraint(a, pltpu.HBM)
    dma = pltpu.SemaphoreType.DMA
    outs = pl.pallas_call(
        body, name=name, in_specs=[HBM_ONLY] * (2 * n), out_specs=[SEM_SPEC, SEM_SPEC] + [HBM_ONLY] * (2 * n) + [VMEM_SPEC],
        out_shape=[dma((7 * n,)), dma((7 * n,))] + [pltpu.HBM(a.shape, a.dtype) for a in list(grads) + zones]
        + [jax.ShapeDtypeStruct((8, 128), F32)],
        input_output_aliases={i: 2 + i for i in range(2 * n)},
        compiler_params=pltpu.CompilerParams(has_side_effects=DATAFLOW))(*[hbm(a) for a in list(grads) + zones])
    return outs[0], outs[1], outs[2:2 + n], outs[2 + n:2 + 2 * n], outs[-1]


def _scatter_wait(started, kinds, after, name):
    ssem, rsem, grads, zones, _ = started
    n = len(grads)

    def body(*refs):
        srcs, lands = refs[:n], refs[n:2 * n]
        ssem_ref, rsem_ref = refs[2 * n], refs[2 * n + 1]
        for k in range(1, N_DEV):
            peer = _dev_index(_peer(k))
            for i, t in enumerate(kinds):
                q = (k - 1) * n + i
                cp = _remote(_reduced_by(srcs[i], t, peer), lands[i].at[peer], ssem_ref.at[q], rsem_ref.at[q], _peer(k))
                cp.wait_send()
                cp.wait_recv()

    outs = pl.pallas_call(
        body, name=name, in_specs=[HBM_ONLY] * (2 * n) + [SEM_SPEC, SEM_SPEC, HBM_SPEC], out_specs=[HBM_ONLY] * (2 * n),
        out_shape=[pltpu.HBM(a.shape, a.dtype) for a in list(grads) + list(zones)],
        input_output_aliases={i: i for i in range(2 * n)},
        compiler_params=pltpu.CompilerParams(has_side_effects=DATAFLOW))(*grads, *zones, ssem, rsem, after)
    return outs[:n], outs[n:]


GRAD_SEGMENTS = 19


def _grad_finish(grads, zones, kinds, name):
    nt = len(grads)
    pieces = _chunks(kinds, GATHER_CHUNK)
    npc = len(pieces)
    shard = [SHARD_ROWS[k] for k in kinds]
    count = [sum(1 for p in pieces if p[0] == t) for t in range(nt)]
    nseg = min(GRAD_SEGMENTS, min(count))
    cut = [[next(p[2] for p in pieces if p[0] == t) * -(-count[t] * s // nseg) for s in range(nseg + 1)] for t in range(nt)]

    def body(*refs):
        srcs, lands, outs = refs[:nt], refs[nt:2 * nt], refs[2 * nt:3 * nt]
        slots, sums = refs[3 * nt:4 * nt], refs[4 * nt:5 * nt]
        lsem, osem, xsem, ysem = refs[5 * nt:]
        x, y, c = _coords()
        me = _dev_index((x, y, c))
        sib = (x, y, 1 - c)
        loads = []
        for s in range(nseg):
            part = []
            for t in range(nt):
                rows = pl.ds(cut[t][s], cut[t][s + 1] - cut[t][s])
                part.append(pltpu.make_async_copy(_reduced_by(srcs[t], kinds[t], me).at[rows, :], slots[t].at[me, rows, :],
                                                  lsem.at[s * N_DEV * nt + t]))
                for k in range(1, N_DEV):
                    peer = _dev_index(_peer(k))
                    part.append(pltpu.make_async_copy(lands[t].at[peer, rows, :], slots[t].at[peer, rows, :],
                                                      lsem.at[(s * N_DEV + k) * nt + t]))
            for cp in part:
                cp.start()
            loads.append(part)
        sends = []
        place = lambda t, cc, off, n: _rows(outs[t], cc * (shard[t] // 2) + off, n)
        stores = []
        for s in range(nseg):
            for cp in loads[s]:
                cp.wait()
            for q, (t, off, n) in enumerate(pieces):
                if not cut[t][s] <= off < cut[t][s + 1]:
                    continue
                r = pl.ds(off, n)
                acc = slots[t][0, r, :].astype(F32)
                for d in range(1, N_DEV):
                    acc = acc + slots[t][d, r, :].astype(F32)
                sums[t][r, :] = acc
                keep = pltpu.make_async_copy(sums[t].at[r, :], place(t, c, off, n), osem.at[q])
                give = _remote(sums[t].at[r, :], place(t, c, off, n), xsem.at[q], ysem.at[q], sib)
                keep.start()
                give.start()
                stores.append(keep)
                sends.append(give)
        for q, (t, off, n) in enumerate(pieces):
            got = place(t, 1 - c, off, n)
            _remote(got, got, xsem.at[q], ysem.at[q], sib).wait_recv()
        for cp in sends:
            cp.wait_send()
        for cp in stores:
            cp.wait()

    dma = pltpu.SemaphoreType.DMA
    return pl.pallas_call(
        body, name=name, in_specs=[HBM_SPEC] * (2 * nt), out_specs=[HBM_SPEC] * nt,
        out_shape=[jax.ShapeDtypeStruct((shard[t], g.shape[1]), F32) for t, g in enumerate(grads)],
        scratch_shapes=([pltpu.VMEM((N_DEV, shard[t] // 2, g.shape[1]), g.dtype) for t, g in enumerate(grads)]
                        + [pltpu.VMEM((shard[t] // 2, g.shape[1]), F32) for t, g in enumerate(grads)]
                        + [dma((GRAD_SEGMENTS * N_DEV * nt,)), dma((npc,)), dma((npc,)), dma((npc,))]),
        compiler_params=_cp())(*grads, *zones)


def _adam_math(w, g, m, v):
    m = ADAM_B1 * m + (1.0 - ADAM_B1) * g
    v = ADAM_B2 * v + (1.0 - ADAM_B2) * (g * g)
    m_hat = m / (1.0 - ADAM_B1 ** ADAM_STEP)
    v_hat = v / (1.0 - ADAM_B2 ** ADAM_STEP)
    return -ADAM_LR * (m_hat / (jnp.sqrt(v_hat) + ADAM_EPS) + ADAM_WD * w), m, v


def _adamw(items, name, nblk=4):
    n = len(items)

    def body(*refs):
        ins, outs = refs[:4 * n], refs[4 * n:]
        for t in range(n):
            w_ref, g_ref, m_ref, v_ref = ins[4 * t:4 * t + 4]
            d_ref, nm_ref, nv_ref = outs[3 * t:3 * t + 3]
            d_ref[...], nm_ref[...], nv_ref[...] = _adam_math(w_ref[...], g_ref[...], m_ref[...], v_ref[...])

    specs = [pl.BlockSpec((it[0].shape[0] // nblk, it[0].shape[1]), lambda i: (i, 0)) for it in items]
    outs = pl.pallas_call(
        body, grid=(nblk,), name=name, in_specs=[s for s in specs for _ in range(4)], out_specs=[s for s in specs for _ in range(3)],
        out_shape=[jax.ShapeDtypeStruct(it[0].shape, F32) for it in items for _ in range(3)],
        compiler_params=_cp("parallel"))(*[a for it in items for a in it])
    return [tuple(outs[3 * t:3 * t + 3]) for t in range(n)]


PACK = (("b_ada", 3 * D_MODEL), ("g_pre", D_MODEL), ("g_post", D_MODEL), ("gn_g", 512), ("qn_g", 64), ("kn_g", 64),
        ("w_dec_f", 4), ("w_dec_b", 4), ("loss", 1))
PACK_OFFSETS = {}
PACK_WIDTH = 0
for _name, _n in PACK:
    PACK_OFFSETS[_name] = PACK_WIDTH
    PACK_WIDTH += -(-_n // 128) * 128
SMALL_PARAMS = tuple(name for name, _ in PACK if name != "loss")


def _pack(parts):
    cols = []
    for name, n in PACK:
        pad = -(-n // 128) * 128 - n
        cols.append(parts[name].reshape(1, n).astype(F32))
        if pad:
            cols.append(jnp.zeros((1, pad), F32))
    return jnp.concatenate(cols, axis=1)


def _small_reduce(vec, cs, deps):
    ncol = 3 * D_MODEL // N_CHIP

    def body(vec_ref, cs_ref, *rest):
        tot_ref, gwa_ref, gat, ssem, rsem = rest[-5:]
        me = _dev_index(_coords())
        chip = me // 2
        gat[me] = vec_ref[...]
        sends = []
        for k in range(1, N_DEV):
            cp = _remote(vec_ref, gat.at[me], ssem.at[k - 1], rsem.at[k - 1], _peer(k))
            cp.start()
            sends.append(cp)
        for k in range(1, N_DEV):
            slot = gat.at[_dev_index(_peer(k))]
            _remote(slot, slot, ssem.at[k - 1], rsem.at[k - 1], _peer(k)).wait_recv()
        rows = [gat[d] for d in range(N_DEV)]
        tot = rows[0]
        for d in range(1, N_DEV):
            tot = tot + rows[d]
        tot_ref[...] = tot
        cs = cs_ref[...]
        ca_t = jnp.transpose(jnp.concatenate([cs * _sigmoid(cs), jnp.zeros((128 - N_DEV, D_MODEL), F32)], axis=0))
        dm = jnp.concatenate(rows + [jnp.zeros((128 - N_DEV, PACK_WIDTH), F32)], axis=0)
        for jj in range(N_CHIP):
            @pl.when(chip == jj)
            def _():
                gwa_ref[...] = _nn(ca_t, dm[:, ncol * jj:ncol * (jj + 1)], precision=HIGHEST)
        for cp in sends:
            cp.wait_send()

    return pl.pallas_call(
        body, name="small_reduce", in_specs=[VMEM_SPEC, VMEM_SPEC] + [HBM_SPEC] * len(deps), out_specs=[VMEM_SPEC] * 2,
        out_shape=[jax.ShapeDtypeStruct((1, PACK_WIDTH), F32), jax.ShapeDtypeStruct((D_MODEL, ncol), F32)],
        scratch_shapes=[pltpu.VMEM((N_DEV, 1, PACK_WIDTH), F32), pltpu.SemaphoreType.DMA((7,)), pltpu.SemaphoreType.DMA((7,))],
        compiler_params=_cp())(vec, cs, *deps)


def _small_adamw(tot, given):
    sizes = dict(PACK)
    np_ = len(SMALL_PARAMS)

    def body(*refs):
        tot_ref = refs[0]
        wmv = refs[1:1 + 3 * np_]
        outs = refs[1 + 3 * np_:1 + 7 * np_]
        loss_ref = refs[-1]
        for i, name in enumerate(SMALL_PARAMS):
            off, n = PACK_OFFSETS[name], sizes[name]
            g = tot_ref[:, off:off + n]
            w_ref, m_ref, v_ref = wmv[3 * i:3 * i + 3]
            outs[i][...] = g
            outs[np_ + i][...], outs[2 * np_ + i][...], outs[3 * np_ + i][...] = _adam_math(w_ref[...], g, m_ref[...], v_ref[...])
        loss_ref[...] = tot_ref[:, PACK_OFFSETS["loss"]:PACK_OFFSETS["loss"] + 1]

    flat = [a for name in SMALL_PARAMS for a in given[name]]
    shapes = [jax.ShapeDtypeStruct((1, sizes[name]), F32) for name in SMALL_PARAMS]
    res = pl.pallas_call(body, name="small_adamw", in_specs=[VMEM_SPEC] * (1 + 3 * np_), out_specs=[VMEM_SPEC] * (4 * np_ + 1),
                         out_shape=shapes * 4 + [jax.ShapeDtypeStruct((1, 1), F32)], compiler_params=_cp())(tot, *flat)
    return [dict(zip(SMALL_PARAMS, res[k * np_:(k + 1) * np_])) for k in range(4)], res[-1]


def kernel(x, c, w_ada, b_ada, g_pre, w_in, qn_g, kn_g, w_dec_f, w_dec_b, gn_g, w_pa, w_pr, w_out, g_post, loss_target, m_w_ada, m_b_ada, m_g_pre, m_w_in, m_qn_g, m_kn_g, m_w_dec_f, m_w_dec_b, m_gn_g, m_w_pa, m_w_pr, m_w_out, m_g_post, v_w_ada, v_b_ada, v_g_pre, v_w_in, v_qn_g, v_kn_g, v_w_dec_f, v_w_dec_b, v_gn_g, v_w_pa, v_w_pr, v_w_out, v_g_post):
    shards = (w_in[0].T.astype(MXU_DTYPE), jnp.concatenate([w_pa[0].T, w_pr[0].T], axis=1).astype(MXU_DTYPE),
              w_out[0].astype(MXU_DTYPE))
    cs, mod, w_attn = _mod_and_attn_rows(c, w_ada[0], b_ada.reshape(N_CHIP, 3 * D_MODEL // N_CHIP), shards[0])
    started = {}

    def rest_start(dep, *placed):
        started["rest"] = _gather_rest_start(shards, placed, dep)
        return started["rest"][-1]

    def rest_wait(after):
        return _gather_rest_wait(started["rest"], after)

    kinds = {"early": (1, 2), "late": (0,)}

    def send(name, arrays):
        started[name] = _scatter_start(arrays, kinds[name], "grad_scatter_" + name)
        return started[name][-1]

    grad_x, _, _, _, small = _local_step(x[0], loss_target[0], mod, g_pre, qn_g, kn_g, w_dec_f, w_dec_b,
                                         gn_g, g_post, w_attn, rest_start, rest_wait, send=send, own=shards)
    small = dict(small)
    small["b_ada"] = small.pop("dmod")
    given = dict(b_ada=(b_ada, m_b_ada, v_b_ada), g_pre=(g_pre, m_g_pre, v_g_pre), g_post=(g_post, m_g_post, v_g_post),
                 gn_g=(gn_g, m_gn_g, v_gn_g), qn_g=(qn_g, m_qn_g, v_qn_g), kn_g=(kn_g, m_kn_g, v_kn_g),
                 w_dec_f=(w_dec_f, m_w_dec_f, v_w_dec_f), w_dec_b=(w_dec_b, m_w_dec_b, v_w_dec_b))
    grads, deltas, new_m, new_v = {}, {}, {}, {}

    def update(call, back, **items):
        done = _adamw(list(items.values()), "adamw_" + call)
        for (name, (w, g, m, v)), (d, nm, nv) in zip(items.items(), done):
            grads[name], deltas[name], new_m[name], new_v[name] = back(g), back(d), back(nm), back(nv)
        return tuple(d for d, _, _ in done)

    lead = lambda a: a[None]
    (g_pt, g_out), (z_pt, z_out) = _scatter_wait(started["early"], kinds["early"], grad_x, "grad_scatter_early_wait")
    r_pt, r_out = _grad_finish((g_pt, g_out), (z_pt, z_out), kinds["early"], "grad_finish_early")
    early = update("early", lead, w_pa=(w_pa[0], r_pt[:, :512].T, m_w_pa[0], v_w_pa[0]),
                   w_pr=(w_pr[0], r_pt[:, 512:].T, m_w_pr[0], v_w_pr[0]), w_out=(w_out[0], r_out, m_w_out[0], v_w_out[0]))
    tot, g_w_ada = _small_reduce(_pack(small), cs.reshape(N_DEV, D_MODEL), early)
    small_parts, loss = _small_adamw(tot, given)
    for dst, part in zip((grads, deltas, new_m, new_v), small_parts):
        dst.update(part)
    (last,) = update("w_ada", lead, w_ada=(w_ada[0], g_w_ada, m_w_ada[0], v_w_ada[0]))

    (g_in_t,), (z_in,) = _scatter_wait(started["late"], kinds["late"], last, "grad_scatter_late_wait")
    (r_in,) = _grad_finish((g_in_t,), (z_in,), kinds["late"], "grad_finish_late")
    tr = lambda a: a[0].T
    update("w_in", lambda a: a.T[None], w_in=(tr(w_in), r_in, tr(m_w_in), tr(v_w_in)))
    order = ("w_ada", "b_ada", "g_pre", "w_in", "qn_g", "kn_g", "w_dec_f", "w_dec_b", "gn_g", "w_pa", "w_pr", "w_out", "g_post")
    return (loss[0, 0], grad_x[None], *[grads[n] for n in order], *[deltas[n] for n in order],
            *[new_m[n] for n in order], *[new_v[n] for n in order])
```
